```python
import jax, jax.numpy as jnp
from jax import lax
import numpy as np

D_MODEL = 1024
BATCH = 8
SEQ = 8192
DEPTH = 1

D_PLE = 256
D_MIX = 2 * D_MODEL
GM_WIDTH = D_MIX // 2
GM_HEADS = 8
GM_HEAD_DIM = GM_WIDTH // GM_HEADS
GM_CHUNK = 128
SSM_WIDTH = D_MIX - GM_WIDTH
SSM_HEAD_DIM = 64
SSM_HEADS = SSM_WIDTH // SSM_HEAD_DIM
SSM_GROUPS = 2
SSM_STATE = 128
SSM_CONV = 4
SSM_CHUNK = 128
SSM_CONV_DIM = SSM_WIDTH + 2 * SSM_GROUPS * SSM_STATE
D_FF = 256 * ((8 * D_MODEL // 3 + 255) // 256)
EPS = 1e-6
IN_SPLITS = (GM_WIDTH, 2 * GM_WIDTH, 2 * GM_WIDTH + SSM_WIDTH, 2 * GM_WIDTH + SSM_WIDTH + SSM_CONV_DIM)
IN_PROJ_DIM = 2 * GM_WIDTH + SSM_WIDTH + SSM_CONV_DIM + SSM_HEADS

kernel_name = "hybrid_gmlp_ssd_macaron_block"


def rmsnorm(x, g):
    xf = x.astype(jnp.float32)
    y = xf * lax.rsqrt(jnp.mean(xf * xf, axis=-1, keepdims=True) + EPS)
    return (y * g.astype(jnp.float32)).astype(x.dtype)


def layernorm(x, g, b):
    xf = x.astype(jnp.float32)
    mu = jnp.mean(xf, axis=-1, keepdims=True)
    xc = xf - mu
    y = xc * lax.rsqrt(jnp.mean(xc * xc, axis=-1, keepdims=True) + EPS)
    return (y * g.astype(jnp.float32) + b.astype(jnp.float32)).astype(x.dtype)


def swiglu(x, w_gate, w_up, w_down):
    return (jax.nn.silu(x @ w_gate) * (x @ w_up)) @ w_down


def chunked_spatial_gating(u, v, ln_g, ln_b, w_s, b_s):
    bsz, L, _ = u.shape
    nc = L // GM_CHUNK
    v = layernorm(v, ln_g, ln_b).reshape(bsz, nc, GM_CHUNK, GM_HEADS, GM_HEAD_DIM)
    mask = jnp.tril(jnp.ones((GM_CHUNK, GM_CHUNK), dtype=bool))
    w = jnp.where(mask, w_s, jnp.zeros_like(w_s)).astype(v.dtype)
    mixed = jnp.einsum("hts,bcshd->bcthd", w, v) + b_s.T.astype(v.dtype)[None, None, :, :, None]
    return u * mixed.reshape(bsz, L, GM_WIDTH)


def causal_depthwise_conv(x, w, b):
    y = lax.conv_general_dilated(
        x, w[:, None, :].astype(x.dtype), window_strides=(1,), padding=[(SSM_CONV - 1, 0)],
        dimension_numbers=("NWC", "WIO", "NWC"), feature_group_count=x.shape[-1])
    return y + b.astype(x.dtype)


def segsum_exp(cs):
    T = cs.shape[-1]
    diff = cs[..., :, None] - cs[..., None, :]
    mask = jnp.tril(jnp.ones((T, T), dtype=bool))
    return jnp.exp(jnp.where(mask, diff, -jnp.inf))


def ssd_chunked(x, dt, a, bm, cm):
    bsz, L, H, P = x.shape
    nc = L // SSM_CHUNK
    k = H // SSM_GROUPS
    xdt = (x * dt[..., None]).reshape(bsz, nc, SSM_CHUNK, SSM_GROUPS, k, P)
    adt = (dt * a).reshape(bsz, nc, SSM_CHUNK, SSM_GROUPS, k).transpose(0, 3, 4, 1, 2)
    bm = bm.reshape(bsz, nc, SSM_CHUNK, SSM_GROUPS, SSM_STATE)
    cm = cm.reshape(bsz, nc, SSM_CHUNK, SSM_GROUPS, SSM_STATE)
    a_cs = jnp.cumsum(adt, axis=-1)
    decay = segsum_exp(a_cs)
    cb = jnp.einsum("bclgn,bcsgn->bgcls", cm, bm)
    y_diag = jnp.einsum("bgkcls,bcsgkp->bclgkp", cb[:, :, None] * decay, xdt)
    decay_states = jnp.exp(a_cs[..., -1:] - a_cs).transpose(0, 3, 4, 1, 2)
    states = jnp.einsum("bclgn,bclgkp->bcgkpn", bm, xdt * decay_states[..., None])
    chunk_tot = jnp.pad(a_cs[..., -1], ((0, 0), (0, 0), (0, 0), (1, 0)))
    decay_chunk = segsum_exp(jnp.cumsum(chunk_tot, axis=-1))
    states = jnp.concatenate([jnp.zeros_like(states[:, :1]), states], axis=1)
    new_states = jnp.einsum("bgkzc,bcgkpn->bzgkpn", decay_chunk, states)
    prev_states = new_states[:, :-1]
    out_decay = jnp.exp(a_cs).transpose(0, 3, 4, 1, 2)
    y_off = jnp.einsum("bclgn,bcgkpn->bclgkp", cm, prev_states) * out_decay[..., None]
    return (y_diag + y_off).reshape(bsz, L, H, P)


def mamba2_mixer(z, xbc, dt_raw, conv_w, conv_b, dt_bias, a_log, d_skip, norm_g):
    bsz, L, _ = z.shape
    f32 = jnp.float32
    xbc = jax.nn.silu(causal_depthwise_conv(xbc, conv_w, conv_b))
    xs, bm, cm = jnp.split(xbc, [SSM_WIDTH, SSM_WIDTH + SSM_GROUPS * SSM_STATE], axis=-1)
    xs = xs.reshape(bsz, L, SSM_HEADS, SSM_HEAD_DIM).astype(f32)
    bm = bm.reshape(bsz, L, SSM_GROUPS, SSM_STATE).astype(f32)
    cm = cm.reshape(bsz, L, SSM_GROUPS, SSM_STATE).astype(f32)
    dt = jax.nn.softplus(dt_raw.astype(f32) + dt_bias.astype(f32))
    a = -jnp.exp(a_log.astype(f32))
    y = ssd_chunked(xs, dt, a, bm, cm) + xs * d_skip.astype(f32)[:, None]
    y = y.reshape(bsz, L, SSM_WIDTH) * jax.nn.silu(z.astype(f32))
    y = y.reshape(bsz, L, SSM_GROUPS, SSM_WIDTH // SSM_GROUPS)
    y = y * lax.rsqrt(jnp.mean(y * y, axis=-1, keepdims=True) + EPS)
    return (y.reshape(bsz, L, SSM_WIDTH) * norm_g.astype(f32)).astype(z.dtype)


def _fwd_setup_inputs(seed: int = 0) -> dict:
    key = jax.random.key(seed)
    ks = iter(jax.random.split(key, 40))

    def nrm(shape, scale):
        return jax.random.normal(next(ks), shape, jnp.float32) * scale

    def gain(shape):
        return 1.0 + 0.1 * jax.random.normal(next(ks), shape, jnp.float32)

    L = DEPTH
    x = jax.random.normal(next(ks), (BATCH, SEQ, D_MODEL), jnp.float32)
    p = jax.random.normal(next(ks), (DEPTH, BATCH, SEQ, D_PLE), jnp.float32)
    dt0 = jnp.exp(jax.random.uniform(next(ks), (L, SSM_HEADS), jnp.float32)
                  * (np.log(0.1) - np.log(0.001)) + np.log(0.001))
    dt0 = jnp.maximum(dt0, 1e-4)
    dt_bias = dt0 + jnp.log(-jnp.expm1(-dt0))
    a_log = jnp.log(jax.random.uniform(next(ks), (L, SSM_HEADS), jnp.float32, 1.0, 16.0))
    return {
        "x": x,
        "p": p,
        "ffn1_norm": gain((L, D_MODEL)),
        "ffn1_w_gate": nrm((L, D_MODEL, D_FF), D_MODEL ** -0.5),
        "ffn1_w_up": nrm((L, D_MODEL, D_FF), D_MODEL ** -0.5),
        "ffn1_w_down": nrm((L, D_FF, D_MODEL), D_FF ** -0.5),
        "mix_norm": gain((L, D_MODEL)),
        "w_in": nrm((L, D_MODEL, IN_PROJ_DIM), D_MODEL ** -0.5),
        "gm_ln_g": gain((L, GM_WIDTH)),
        "gm_ln_b": nrm((L, GM_WIDTH), 0.02),
        "gm_w_s": nrm((L, GM_HEADS, GM_CHUNK, GM_CHUNK), 0.5 * GM_CHUNK ** -0.5),
        "gm_b_s": gain((L, GM_HEADS, GM_CHUNK)),
        "gm_out_norm": gain((L, GM_WIDTH)),
        "conv_w": nrm((L, SSM_CONV, SSM_CONV_DIM), SSM_CONV ** -0.5),
        "conv_b": nrm((L, SSM_CONV_DIM), 0.02),
        "dt_bias": dt_bias,
        "a_log": a_log,
        "d_skip": gain((L, SSM_HEADS)),
        "ssm_norm": gain((L, SSM_WIDTH)),
        "w_out": nrm((L, D_MIX, D_MODEL), D_MIX ** -0.5),
        "ffn2_norm": gain((L, D_MODEL)),
        "ffn2_w_gate": nrm((L, D_MODEL, D_FF), D_MODEL ** -0.5),
        "ffn2_w_up": nrm((L, D_MODEL, D_FF), D_MODEL ** -0.5),
        "ffn2_w_down": nrm((L, D_FF, D_MODEL), D_FF ** -0.5),
        "ple_norm": gain((L, D_MODEL)),
        "ple_w_gate": nrm((L, D_MODEL, D_MODEL), D_MODEL ** -0.5),
        "ple_b_gate": nrm((L, D_MODEL), 0.02),
        "ple_w_proj": nrm((L, D_PLE, D_MODEL), D_PLE ** -0.5),
        "final_norm": gain((D_MODEL,)),
    }


def _fwd_reference(x, p, ffn1_norm, ffn1_w_gate, ffn1_w_up, ffn1_w_down, mix_norm, w_in,
              gm_ln_g, gm_ln_b, gm_w_s, gm_b_s, gm_out_norm, conv_w, conv_b, dt_bias, a_log,
              d_skip, ssm_norm, w_out, ffn2_norm, ffn2_w_gate, ffn2_w_up, ffn2_w_down,
              ple_norm, ple_w_gate, ple_b_gate, ple_w_proj, final_norm):
    h = x
    for i in range(DEPTH):
        h = h + 0.5 * swiglu(rmsnorm(h, ffn1_norm[i]), ffn1_w_gate[i], ffn1_w_up[i], ffn1_w_down[i])
        n = rmsnorm(h, mix_norm[i])
        proj = n @ w_in[i]
        u, v, z, xbc, dt_raw = jnp.split(proj, IN_SPLITS, axis=-1)
        ya = chunked_spatial_gating(jax.nn.gelu(u, approximate=False), jax.nn.gelu(v, approximate=False),
                                    gm_ln_g[i], gm_ln_b[i], gm_w_s[i], gm_b_s[i])
        ya = rmsnorm(ya, gm_out_norm[i])
        yb = mamba2_mixer(z, xbc, dt_raw, conv_w[i], conv_b[i], dt_bias[i], a_log[i], d_skip[i], ssm_norm[i])
        h = h + jnp.concatenate([ya, yb], axis=-1) @ w_out[i]
        h = h + 0.5 * swiglu(rmsnorm(h, ffn2_norm[i]), ffn2_w_gate[i], ffn2_w_up[i], ffn2_w_down[i])
        gate = jax.nn.sigmoid(rmsnorm(h, ple_norm[i]) @ ple_w_gate[i] + ple_b_gate[i])
        h = h + gate * (p[i] @ ple_w_proj[i])
    return rmsnorm(h, final_norm)


import jax as _jax
import jax.numpy as _jnp

TWIN_FORMAT = 'train_step'
FWD_PARAMS = ['x', 'p', 'ffn1_norm', 'ffn1_w_gate', 'ffn1_w_up', 'ffn1_w_down', 'mix_norm', 'w_in', 'gm_ln_g', 'gm_ln_b', 'gm_w_s', 'gm_b_s', 'gm_out_norm', 'conv_w', 'conv_b', 'dt_bias', 'a_log', 'd_skip', 'ssm_norm', 'w_out', 'ffn2_norm', 'ffn2_w_gate', 'ffn2_w_up', 'ffn2_w_down', 'ple_norm', 'ple_w_gate', 'ple_b_gate', 'ple_w_proj', 'final_norm']
TWIN_WEIGHTS = ['ffn1_norm', 'ffn1_w_gate', 'ffn1_w_up', 'ffn1_w_down', 'mix_norm', 'w_in', 'gm_ln_g', 'gm_ln_b', 'gm_w_s', 'gm_b_s', 'gm_out_norm', 'conv_w', 'conv_b', 'dt_bias', 'a_log', 'd_skip', 'ssm_norm', 'w_out', 'ffn2_norm', 'ffn2_w_gate', 'ffn2_w_up', 'ffn2_w_down', 'ple_norm', 'ple_w_gate', 'ple_b_gate', 'ple_w_proj', 'final_norm']
TWIN_DIFF_INPUT = 'x'
TWIN_INPUTS = ['x', 'p', 'ffn1_norm', 'ffn1_w_gate', 'ffn1_w_up', 'ffn1_w_down', 'mix_norm', 'w_in', 'gm_ln_g', 'gm_ln_b', 'gm_w_s', 'gm_b_s', 'gm_out_norm', 'conv_w', 'conv_b', 'dt_bias', 'a_log', 'd_skip', 'ssm_norm', 'w_out', 'ffn2_norm', 'ffn2_w_gate', 'ffn2_w_up', 'ffn2_w_down', 'ple_norm', 'ple_w_gate', 'ple_b_gate', 'ple_w_proj', 'final_norm', 'loss_target', 'm_ffn1_norm', 'm_ffn1_w_gate', 'm_ffn1_w_up', 'm_ffn1_w_down', 'm_mix_norm', 'm_w_in', 'm_gm_ln_g', 'm_gm_ln_b', 'm_gm_w_s', 'm_gm_b_s', 'm_gm_out_norm', 'm_conv_w', 'm_conv_b', 'm_dt_bias', 'm_a_log', 'm_d_skip', 'm_ssm_norm', 'm_w_out', 'm_ffn2_norm', 'm_ffn2_w_gate', 'm_ffn2_w_up', 'm_ffn2_w_down', 'm_ple_norm', 'm_ple_w_gate', 'm_ple_b_gate', 'm_ple_w_proj', 'm_final_norm', 'v_ffn1_norm', 'v_ffn1_w_gate', 'v_ffn1_w_up', 'v_ffn1_w_down', 'v_mix_norm', 'v_w_in', 'v_gm_ln_g', 'v_gm_ln_b', 'v_gm_w_s', 'v_gm_b_s', 'v_gm_out_norm', 'v_conv_w', 'v_conv_b', 'v_dt_bias', 'v_a_log', 'v_d_skip', 'v_ssm_norm', 'v_w_out', 'v_ffn2_norm', 'v_ffn2_w_gate', 'v_ffn2_w_up', 'v_ffn2_w_down', 'v_ple_norm', 'v_ple_w_gate', 'v_ple_b_gate', 'v_ple_w_proj', 'v_final_norm']
TWIN_OUTPUTS = ['loss', 'grad_x', 'grad_ffn1_norm', 'grad_ffn1_w_gate', 'grad_ffn1_w_up', 'grad_ffn1_w_down', 'grad_mix_norm', 'grad_w_in', 'grad_gm_ln_g', 'grad_gm_ln_b', 'grad_gm_w_s', 'grad_gm_b_s', 'grad_gm_out_norm', 'grad_conv_w', 'grad_conv_b', 'grad_dt_bias', 'grad_a_log', 'grad_d_skip', 'grad_ssm_norm', 'grad_w_out', 'grad_ffn2_norm', 'grad_ffn2_w_gate', 'grad_ffn2_w_up', 'grad_ffn2_w_down', 'grad_ple_norm', 'grad_ple_w_gate', 'grad_ple_b_gate', 'grad_ple_w_proj', 'grad_final_norm', 'delta_ffn1_norm', 'delta_ffn1_w_gate', 'delta_ffn1_w_up', 'delta_ffn1_w_down', 'delta_mix_norm', 'delta_w_in', 'delta_gm_ln_g', 'delta_gm_ln_b', 'delta_gm_w_s', 'delta_gm_b_s', 'delta_gm_out_norm', 'delta_conv_w', 'delta_conv_b', 'delta_dt_bias', 'delta_a_log', 'delta_d_skip', 'delta_ssm_norm', 'delta_w_out', 'delta_ffn2_norm', 'delta_ffn2_w_gate', 'delta_ffn2_w_up', 'delta_ffn2_w_down', 'delta_ple_norm', 'delta_ple_w_gate', 'delta_ple_b_gate', 'delta_ple_w_proj', 'delta_final_norm', 'new_m_ffn1_norm', 'new_m_ffn1_w_gate', 'new_m_ffn1_w_up', 'new_m_ffn1_w_down', 'new_m_mix_norm', 'new_m_w_in', 'new_m_gm_ln_g', 'new_m_gm_ln_b', 'new_m_gm_w_s', 'new_m_gm_b_s', 'new_m_gm_out_norm', 'new_m_conv_w', 'new_m_conv_b', 'new_m_dt_bias', 'new_m_a_log', 'new_m_d_skip', 'new_m_ssm_norm', 'new_m_w_out', 'new_m_ffn2_norm', 'new_m_ffn2_w_gate', 'new_m_ffn2_w_up', 'new_m_ffn2_w_down', 'new_m_ple_norm', 'new_m_ple_w_gate', 'new_m_ple_b_gate', 'new_m_ple_w_proj', 'new_m_final_norm', 'new_v_ffn1_norm', 'new_v_ffn1_w_gate', 'new_v_ffn1_w_up', 'new_v_ffn1_w_down', 'new_v_mix_norm', 'new_v_w_in', 'new_v_gm_ln_g', 'new_v_gm_ln_b', 'new_v_gm_w_s', 'new_v_gm_b_s', 'new_v_gm_out_norm', 'new_v_conv_w', 'new_v_conv_b', 'new_v_dt_bias', 'new_v_a_log', 'new_v_d_skip', 'new_v_ssm_norm', 'new_v_w_out', 'new_v_ffn2_norm', 'new_v_ffn2_w_gate', 'new_v_ffn2_w_up', 'new_v_ffn2_w_down', 'new_v_ple_norm', 'new_v_ple_w_gate', 'new_v_ple_b_gate', 'new_v_ple_w_proj', 'new_v_final_norm']
TWIN_LEAF_KINDS = {'loss': 'loss', 'grad_x': 'grad_x', 'grad_ffn1_norm': 'grad_w', 'grad_ffn1_w_gate': 'grad_w', 'grad_ffn1_w_up': 'grad_w', 'grad_ffn1_w_down': 'grad_w', 'grad_mix_norm': 'grad_w', 'grad_w_in': 'grad_w', 'grad_gm_ln_g': 'grad_w', 'grad_gm_ln_b': 'grad_w', 'grad_gm_w_s': 'grad_w', 'grad_gm_b_s': 'grad_w', 'grad_gm_out_norm': 'grad_w', 'grad_conv_w': 'grad_w', 'grad_conv_b': 'grad_w', 'grad_dt_bias': 'grad_w', 'grad_a_log': 'grad_w', 'grad_d_skip': 'grad_w', 'grad_ssm_norm': 'grad_w', 'grad_w_out': 'grad_w', 'grad_ffn2_norm': 'grad_w', 'grad_ffn2_w_gate': 'grad_w', 'grad_ffn2_w_up': 'grad_w', 'grad_ffn2_w_down': 'grad_w', 'grad_ple_norm': 'grad_w', 'grad_ple_w_gate': 'grad_w', 'grad_ple_b_gate': 'grad_w', 'grad_ple_w_proj': 'grad_w', 'grad_final_norm': 'grad_w', 'delta_ffn1_norm': 'delta_w', 'delta_ffn1_w_gate': 'delta_w', 'delta_ffn1_w_up': 'delta_w', 'delta_ffn1_w_down': 'delta_w', 'delta_mix_norm': 'delta_w', 'delta_w_in': 'delta_w', 'delta_gm_ln_g': 'delta_w', 'delta_gm_ln_b': 'delta_w', 'delta_gm_w_s': 'delta_w', 'delta_gm_b_s': 'delta_w', 'delta_gm_out_norm': 'delta_w', 'delta_conv_w': 'delta_w', 'delta_conv_b': 'delta_w', 'delta_dt_bias': 'delta_w', 'delta_a_log': 'delta_w', 'delta_d_skip': 'delta_w', 'delta_ssm_norm': 'delta_w', 'delta_w_out': 'delta_w', 'delta_ffn2_norm': 'delta_w', 'delta_ffn2_w_gate': 'delta_w', 'delta_ffn2_w_up': 'delta_w', 'delta_ffn2_w_down': 'delta_w', 'delta_ple_norm': 'delta_w', 'delta_ple_w_gate': 'delta_w', 'delta_ple_b_gate': 'delta_w', 'delta_ple_w_proj': 'delta_w', 'delta_final_norm': 'delta_w', 'new_m_ffn1_norm': 'new_m', 'new_m_ffn1_w_gate': 'new_m', 'new_m_ffn1_w_up': 'new_m', 'new_m_ffn1_w_down': 'new_m', 'new_m_mix_norm': 'new_m', 'new_m_w_in': 'new_m', 'new_m_gm_ln_g': 'new_m', 'new_m_gm_ln_b': 'new_m', 'new_m_gm_w_s': 'new_m', 'new_m_gm_b_s': 'new_m', 'new_m_gm_out_norm': 'new_m', 'new_m_conv_w': 'new_m', 'new_m_conv_b': 'new_m', 'new_m_dt_bias': 'new_m', 'new_m_a_log': 'new_m', 'new_m_d_skip': 'new_m', 'new_m_ssm_norm': 'new_m', 'new_m_w_out': 'new_m', 'new_m_ffn2_norm': 'new_m', 'new_m_ffn2_w_gate': 'new_m', 'new_m_ffn2_w_up': 'new_m', 'new_m_ffn2_w_down': 'new_m', 'new_m_ple_norm': 'new_m', 'new_m_ple_w_gate': 'new_m', 'new_m_ple_b_gate': 'new_m', 'new_m_ple_w_proj': 'new_m', 'new_m_final_norm': 'new_m', 'new_v_ffn1_norm': 'new_v', 'new_v_ffn1_w_gate': 'new_v', 'new_v_ffn1_w_up': 'new_v', 'new_v_ffn1_w_down': 'new_v', 'new_v_mix_norm': 'new_v', 'new_v_w_in': 'new_v', 'new_v_gm_ln_g': 'new_v', 'new_v_gm_ln_b': 'new_v', 'new_v_gm_w_s': 'new_v', 'new_v_gm_b_s': 'new_v', 'new_v_gm_out_norm': 'new_v', 'new_v_conv_w': 'new_v', 'new_v_conv_b': 'new_v', 'new_v_dt_bias': 'new_v', 'new_v_a_log': 'new_v', 'new_v_d_skip': 'new_v', 'new_v_ssm_norm': 'new_v', 'new_v_w_out': 'new_v', 'new_v_ffn2_norm': 'new_v', 'new_v_ffn2_w_gate': 'new_v', 'new_v_ffn2_w_up': 'new_v', 'new_v_ffn2_w_down': 'new_v', 'new_v_ple_norm': 'new_v', 'new_v_ple_w_gate': 'new_v', 'new_v_ple_b_gate': 'new_v', 'new_v_ple_w_proj': 'new_v', 'new_v_final_norm': 'new_v'}


def _forward(args):
    return _fwd_reference(*[args[k] for k in FWD_PARAMS])


def _output_shape():
    out = _jax.eval_shape(lambda: _forward(_fwd_setup_inputs(0)))
    return out.shape, out.dtype

N_MICROBATCH = 1
ADAM_LR = 0.001
ADAM_B1 = 0.9
ADAM_B2 = 0.999
ADAM_EPS = 1e-08
ADAM_WD = 0.01
ADAM_STEP = 10
PER_EXAMPLE_BATCH_AXIS = {'x': 0, 'p': 1, 'loss_target': 0}
SHARED_INPUTS = []
_WEIGHT_DTYPES = {'ffn1_norm': _jnp.float32, 'ffn1_w_gate': _jnp.float32, 'ffn1_w_up': _jnp.float32, 'ffn1_w_down': _jnp.float32, 'mix_norm': _jnp.float32, 'w_in': _jnp.float32, 'gm_ln_g': _jnp.float32, 'gm_ln_b': _jnp.float32, 'gm_w_s': _jnp.float32, 'gm_b_s': _jnp.float32, 'gm_out_norm': _jnp.float32, 'conv_w': _jnp.float32, 'conv_b': _jnp.float32, 'dt_bias': _jnp.float32, 'a_log': _jnp.float32, 'd_skip': _jnp.float32, 'ssm_norm': _jnp.float32, 'w_out': _jnp.float32, 'ffn2_norm': _jnp.float32, 'ffn2_w_gate': _jnp.float32, 'ffn2_w_up': _jnp.float32, 'ffn2_w_down': _jnp.float32, 'ple_norm': _jnp.float32, 'ple_w_gate': _jnp.float32, 'ple_b_gate': _jnp.float32, 'ple_w_proj': _jnp.float32, 'final_norm': _jnp.float32}
MOMENT_SCALE = {'ffn1_norm': 1.180280e-01, 'ffn1_w_gate': 5.171534e-02, 'ffn1_w_up': 5.010704e-02, 'ffn1_w_down': 8.299430e-02, 'mix_norm': 2.327297e-01, 'w_in': 1.076270e-01, 'gm_ln_g': 4.501461e-02, 'gm_ln_b': 4.011162e-02, 'gm_w_s': 8.074279e-02, 'gm_b_s': 1.063634e-01, 'gm_out_norm': 2.855721e-01, 'conv_w': 1.250228e-01, 'conv_b': 3.459868e-01, 'dt_bias': 3.020585e-01, 'a_log': 8.128197e-01, 'd_skip': 7.422878e-01, 'ssm_norm': 1.749992e-01, 'w_out': 3.310176e-01, 'ffn2_norm': 8.343166e-02, 'ffn2_w_gate': 3.235917e-02, 'ffn2_w_up': 3.349133e-02, 'ffn2_w_down': 5.597672e-02, 'ple_norm': 6.656120e-02, 'ple_w_gate': 6.666174e-02, 'ple_b_gate': 4.619996e-01, 'ple_w_proj': 9.558574e-02, 'final_norm': 6.416343e+01}


def _to_microbatches(a, axis):
    t = _jnp.moveaxis(a, axis, 0)
    t = t.reshape((N_MICROBATCH, t.shape[0] // N_MICROBATCH) + t.shape[1:])
    return _jnp.moveaxis(t, 1, axis + 1)


def setup_inputs(seed: int = 0) -> dict:
    inp = _fwd_setup_inputs(seed)
    key = _jax.random.fold_in(_jax.random.key(seed), 7919)
    shape, _ = _output_shape()
    out = dict(inp)
    out["loss_target"] = _jax.random.normal(_jax.random.fold_in(key, 0), shape, _jnp.float32)
    for i, name in enumerate(TWIN_WEIGHTS):
        w = inp[name].astype(_jnp.float32)
        if MOMENT_SCALE is None:
            s = _jnp.sqrt(_jnp.mean(_jnp.square(w)) + 1e-30)
        else:
            s = MOMENT_SCALE[name]
        km, kv = _jax.random.split(_jax.random.fold_in(key, i + 1))
        out[name] = w
        out["m_" + name] = s * _jax.random.normal(km, w.shape, _jnp.float32)
        out["v_" + name] = (s * s) * _jax.random.uniform(kv, w.shape, _jnp.float32, 0.5, 1.5)
    if N_MICROBATCH > 1:
        for name, axis in PER_EXAMPLE_BATCH_AXIS.items():
            out[name] = _to_microbatches(out[name], axis)
    return {'x': out['x'], 'p': out['p'], 'ffn1_norm': out['ffn1_norm'], 'ffn1_w_gate': out['ffn1_w_gate'], 'ffn1_w_up': out['ffn1_w_up'], 'ffn1_w_down': out['ffn1_w_down'], 'mix_norm': out['mix_norm'], 'w_in': out['w_in'], 'gm_ln_g': out['gm_ln_g'], 'gm_ln_b': out['gm_ln_b'], 'gm_w_s': out['gm_w_s'], 'gm_b_s': out['gm_b_s'], 'gm_out_norm': out['gm_out_norm'], 'conv_w': out['conv_w'], 'conv_b': out['conv_b'], 'dt_bias': out['dt_bias'], 'a_log': out['a_log'], 'd_skip': out['d_skip'], 'ssm_norm': out['ssm_norm'], 'w_out': out['w_out'], 'ffn2_norm': out['ffn2_norm'], 'ffn2_w_gate': out['ffn2_w_gate'], 'ffn2_w_up': out['ffn2_w_up'], 'ffn2_w_down': out['ffn2_w_down'], 'ple_norm': out['ple_norm'], 'ple_w_gate': out['ple_w_gate'], 'ple_b_gate': out['ple_b_gate'], 'ple_w_proj': out['ple_w_proj'], 'final_norm': out['final_norm'], 'loss_target': out['loss_target'], 'm_ffn1_norm': out['m_ffn1_norm'], 'm_ffn1_w_gate': out['m_ffn1_w_gate'], 'm_ffn1_w_up': out['m_ffn1_w_up'], 'm_ffn1_w_down': out['m_ffn1_w_down'], 'm_mix_norm': out['m_mix_norm'], 'm_w_in': out['m_w_in'], 'm_gm_ln_g': out['m_gm_ln_g'], 'm_gm_ln_b': out['m_gm_ln_b'], 'm_gm_w_s': out['m_gm_w_s'], 'm_gm_b_s': out['m_gm_b_s'], 'm_gm_out_norm': out['m_gm_out_norm'], 'm_conv_w': out['m_conv_w'], 'm_conv_b': out['m_conv_b'], 'm_dt_bias': out['m_dt_bias'], 'm_a_log': out['m_a_log'], 'm_d_skip': out['m_d_skip'], 'm_ssm_norm': out['m_ssm_norm'], 'm_w_out': out['m_w_out'], 'm_ffn2_norm': out['m_ffn2_norm'], 'm_ffn2_w_gate': out['m_ffn2_w_gate'], 'm_ffn2_w_up': out['m_ffn2_w_up'], 'm_ffn2_w_down': out['m_ffn2_w_down'], 'm_ple_norm': out['m_ple_norm'], 'm_ple_w_gate': out['m_ple_w_gate'], 'm_ple_b_gate': out['m_ple_b_gate'], 'm_ple_w_proj': out['m_ple_w_proj'], 'm_final_norm': out['m_final_norm'], 'v_ffn1_norm': out['v_ffn1_norm'], 'v_ffn1_w_gate': out['v_ffn1_w_gate'], 'v_ffn1_w_up': out['v_ffn1_w_up'], 'v_ffn1_w_down': out['v_ffn1_w_down'], 'v_mix_norm': out['v_mix_norm'], 'v_w_in': out['v_w_in'], 'v_gm_ln_g': out['v_gm_ln_g'], 'v_gm_ln_b': out['v_gm_ln_b'], 'v_gm_w_s': out['v_gm_w_s'], 'v_gm_b_s': out['v_gm_b_s'], 'v_gm_out_norm': out['v_gm_out_norm'], 'v_conv_w': out['v_conv_w'], 'v_conv_b': out['v_conv_b'], 'v_dt_bias': out['v_dt_bias'], 'v_a_log': out['v_a_log'], 'v_d_skip': out['v_d_skip'], 'v_ssm_norm': out['v_ssm_norm'], 'v_w_out': out['v_w_out'], 'v_ffn2_norm': out['v_ffn2_norm'], 'v_ffn2_w_gate': out['v_ffn2_w_gate'], 'v_ffn2_w_up': out['v_ffn2_w_up'], 'v_ffn2_w_down': out['v_ffn2_w_down'], 'v_ple_norm': out['v_ple_norm'], 'v_ple_w_gate': out['v_ple_w_gate'], 'v_ple_b_gate': out['v_ple_b_gate'], 'v_ple_w_proj': out['v_ple_w_proj'], 'v_final_norm': out['v_final_norm']}


def _loss(weights, diff, rest, loss_target):
    with _jax.named_scope("forward"):
        args = {**rest, TWIN_DIFF_INPUT: diff, **{k: w.astype(_WEIGHT_DTYPES[k]) for k, w in weights.items()}}
        y = _forward(args)
    with _jax.named_scope("loss_head"):
        err = _jnp.square(y.astype(_jnp.float32) - loss_target)
        return 0.5 * _jnp.sum(_jnp.mean(err, axis=-1)) if err.ndim else 0.5 * err


def _adamw(w, g, m, v):
    m = ADAM_B1 * m + (1.0 - ADAM_B1) * g
    v = ADAM_B2 * v + (1.0 - ADAM_B2) * _jnp.square(g)
    m_hat = m / (1.0 - ADAM_B1 ** ADAM_STEP)
    v_hat = v / (1.0 - ADAM_B2 ** ADAM_STEP)
    delta = -ADAM_LR * (m_hat / (_jnp.sqrt(v_hat) + ADAM_EPS) + ADAM_WD * w)
    return delta, m, v


def reference(x, p, ffn1_norm, ffn1_w_gate, ffn1_w_up, ffn1_w_down, mix_norm, w_in, gm_ln_g, gm_ln_b, gm_w_s, gm_b_s, gm_out_norm, conv_w, conv_b, dt_bias, a_log, d_skip, ssm_norm, w_out, ffn2_norm, ffn2_w_gate, ffn2_w_up, ffn2_w_down, ple_norm, ple_w_gate, ple_b_gate, ple_w_proj, final_norm, loss_target, m_ffn1_norm, m_ffn1_w_gate, m_ffn1_w_up, m_ffn1_w_down, m_mix_norm, m_w_in, m_gm_ln_g, m_gm_ln_b, m_gm_w_s, m_gm_b_s, m_gm_out_norm, m_conv_w, m_conv_b, m_dt_bias, m_a_log, m_d_skip, m_ssm_norm, m_w_out, m_ffn2_norm, m_ffn2_w_gate, m_ffn2_w_up, m_ffn2_w_down, m_ple_norm, m_ple_w_gate, m_ple_b_gate, m_ple_w_proj, m_final_norm, v_ffn1_norm, v_ffn1_w_gate, v_ffn1_w_up, v_ffn1_w_down, v_mix_norm, v_w_in, v_gm_ln_g, v_gm_ln_b, v_gm_w_s, v_gm_b_s, v_gm_out_norm, v_conv_w, v_conv_b, v_dt_bias, v_a_log, v_d_skip, v_ssm_norm, v_w_out, v_ffn2_norm, v_ffn2_w_gate, v_ffn2_w_up, v_ffn2_w_down, v_ple_norm, v_ple_w_gate, v_ple_b_gate, v_ple_w_proj, v_final_norm):
    given = dict(x=x, p=p, ffn1_norm=ffn1_norm, ffn1_w_gate=ffn1_w_gate, ffn1_w_up=ffn1_w_up, ffn1_w_down=ffn1_w_down, mix_norm=mix_norm, w_in=w_in, gm_ln_g=gm_ln_g, gm_ln_b=gm_ln_b, gm_w_s=gm_w_s, gm_b_s=gm_b_s, gm_out_norm=gm_out_norm, conv_w=conv_w, conv_b=conv_b, dt_bias=dt_bias, a_log=a_log, d_skip=d_skip, ssm_norm=ssm_norm, w_out=w_out, ffn2_norm=ffn2_norm, ffn2_w_gate=ffn2_w_gate, ffn2_w_up=ffn2_w_up, ffn2_w_down=ffn2_w_down, ple_norm=ple_norm, ple_w_gate=ple_w_gate, ple_b_gate=ple_b_gate, ple_w_proj=ple_w_proj, final_norm=final_norm, loss_target=loss_target, m_ffn1_norm=m_ffn1_norm, m_ffn1_w_gate=m_ffn1_w_gate, m_ffn1_w_up=m_ffn1_w_up, m_ffn1_w_down=m_ffn1_w_down, m_mix_norm=m_mix_norm, m_w_in=m_w_in, m_gm_ln_g=m_gm_ln_g, m_gm_ln_b=m_gm_ln_b, m_gm_w_s=m_gm_w_s, m_gm_b_s=m_gm_b_s, m_gm_out_norm=m_gm_out_norm, m_conv_w=m_conv_w, m_conv_b=m_conv_b, m_dt_bias=m_dt_bias, m_a_log=m_a_log, m_d_skip=m_d_skip, m_ssm_norm=m_ssm_norm, m_w_out=m_w_out, m_ffn2_norm=m_ffn2_norm, m_ffn2_w_gate=m_ffn2_w_gate, m_ffn2_w_up=m_ffn2_w_up, m_ffn2_w_down=m_ffn2_w_down, m_ple_norm=m_ple_norm, m_ple_w_gate=m_ple_w_gate, m_ple_b_gate=m_ple_b_gate, m_ple_w_proj=m_ple_w_proj, m_final_norm=m_final_norm, v_ffn1_norm=v_ffn1_norm, v_ffn1_w_gate=v_ffn1_w_gate, v_ffn1_w_up=v_ffn1_w_up, v_ffn1_w_down=v_ffn1_w_down, v_mix_norm=v_mix_norm, v_w_in=v_w_in, v_gm_ln_g=v_gm_ln_g, v_gm_ln_b=v_gm_ln_b, v_gm_w_s=v_gm_w_s, v_gm_b_s=v_gm_b_s, v_gm_out_norm=v_gm_out_norm, v_conv_w=v_conv_w, v_conv_b=v_conv_b, v_dt_bias=v_dt_bias, v_a_log=v_a_log, v_d_skip=v_d_skip, v_ssm_norm=v_ssm_norm, v_w_out=v_w_out, v_ffn2_norm=v_ffn2_norm, v_ffn2_w_gate=v_ffn2_w_gate, v_ffn2_w_up=v_ffn2_w_up, v_ffn2_w_down=v_ffn2_w_down, v_ple_norm=v_ple_norm, v_ple_w_gate=v_ple_w_gate, v_ple_b_gate=v_ple_b_gate, v_ple_w_proj=v_ple_w_proj, v_final_norm=v_final_norm)
    weights = {n: given[n] for n in TWIN_WEIGHTS}
    shared = {n: given[n] for n in SHARED_INPUTS}
    per_example = {n: given[n] for n in ['x', 'p']}
    grad_fn = _jax.value_and_grad(_loss, argnums=(0, 1))

    def one_microbatch(ex, loss_target):
        ex = dict(ex)
        diff = ex.pop(TWIN_DIFF_INPUT)
        return grad_fn(weights, diff, {**shared, **ex}, loss_target)

    if N_MICROBATCH == 1:
        loss, (grad_w, grad_x) = one_microbatch(per_example, given["loss_target"])
    else:
        def body(carry, xs):
            loss_sum, grad_sum = carry
            l_k, (gw_k, gx_k) = one_microbatch(xs[0], xs[1])
            with _jax.named_scope("update"):
                return (loss_sum + l_k, _jax.tree.map(_jnp.add, grad_sum, gw_k)), gx_k

        init = (_jnp.zeros((), _jnp.float32), _jax.tree.map(_jnp.zeros_like, weights))
        (loss, grad_w), grad_x = _jax.lax.scan(body, init, (per_example, given["loss_target"]))
    with _jax.named_scope("update"):
        delta_w, new_m, new_v = {}, {}, {}
        for n in TWIN_WEIGHTS:
            delta_w[n], new_m[n], new_v[n] = _adamw(weights[n], grad_w[n], given["m_" + n], given["v_" + n])
    return (loss, grad_x, *[grad_w[n] for n in TWIN_WEIGHTS], *[delta_w[n] for n in TWIN_WEIGHTS],
            *[new_m[n] for n in TWIN_WEIGHTS], *[new_v[n] for n in TWIN_WEIGHTS])
```

```python
import functools

import jax
import jax.numpy as jnp
from jax import lax
from jax.experimental import pallas as pl
from jax.experimental.pallas import tpu as pltpu

F32 = jnp.float32
BF16 = jnp.bfloat16
HIGHEST = lax.Precision.HIGHEST
MESH = pl.DeviceIdType.MESH
AXES = ("x", "y", "c")
N_DEV = 8

D_MODEL = 1024
D_FF = 2816
D_PLE = 256
GM_WIDTH = 1024
GM_HEADS = 8
GM_HEAD_DIM = 128
CHUNK = 128
SSM_WIDTH = 1024
SSM_HEADS = 16
SSM_HEAD_DIM = 64
SSM_GROUPS = 2
SSM_STATE = 128
SSM_CONV = 4
CONV_DIM = SSM_WIDTH + 2 * SSM_GROUPS * SSM_STATE
IN_PROJ = 2 * GM_WIDTH + SSM_WIDTH + CONV_DIM + SSM_HEADS
LANES = 128
IN_PROJ_PAD = IN_PROJ - SSM_HEADS + LANES
UV_W = 2 * GM_WIDTH
ZXD_W = IN_PROJ_PAD - UV_W
HALO = 8
EPS = 1e-6

ADAM_LR = 0.001
ADAM_B1 = 0.9
ADAM_B2 = 0.999
ADAM_EPS = 1e-08
ADAM_WD = 0.01
ADAM_STEP = 10

VMEM_LIMIT = 56 * 1024 * 1024
PACK_COLS = 1024


def _rms(x, g):
    return x * lax.rsqrt(jnp.mean(x * x, axis=-1, keepdims=True) + EPS) * g


def _gelu(x):
    return 0.5 * x * (1.0 + lax.erf(x * (2.0 ** -0.5)))


def _silu(x):
    return x * jax.nn.sigmoid(x)


def _dot(a, b):
    return jnp.dot(a.astype(BF16), b.astype(BF16), preferred_element_type=F32)


def _dot_nt(a, b):
    return lax.dot_general(a.astype(BF16), b.astype(BF16), (((1,), (1,)), ((), ())), preferred_element_type=F32)


def _dot_tn(a, b):
    return lax.dot_general(a.astype(BF16), b.astype(BF16), (((0,), (0,)), ((), ())), preferred_element_type=F32)


def _hdot(a, b):
    return jnp.dot(a, b, precision=HIGHEST, preferred_element_type=F32)


def _hdot_tn(a, b):
    return lax.dot_general(a, b, (((0,), (0,)), ((), ())), precision=HIGHEST, preferred_element_type=F32)


def _tiled(body, name, n_steps, tiled_in, full_in, big_in, tiled_out, acc_out, scratch=(), reverse=False):
    n_t, n_f, n_b, n_to, n_a = len(tiled_in), len(full_in), len(big_in), len(tiled_out), len(acc_out)

    def row(i):
        return n_steps - 1 - i if reverse else i

    in_specs, args = [], []
    for arr, br, bc, cb in tiled_in:
        if callable(cb):
            in_specs.append(pl.BlockSpec((br, bc), cb))
        else:
            in_specs.append(pl.BlockSpec((br, bc), functools.partial(lambda i, cb: (row(i), cb), cb=cb)))
        args.append(arr)
    for arr in full_in:
        in_specs.append(pl.BlockSpec(arr.shape, functools.partial(lambda i, nd: (0,) * nd, nd=arr.ndim)))
        args.append(arr)
    for arr in big_in:
        in_specs.append(pl.BlockSpec(memory_space=pl.ANY))
        args.append(arr)
    out_specs, out_shape = [], []
    for rows, cols, dt, br in tiled_out:
        out_specs.append(pl.BlockSpec((br, cols), lambda i: (row(i), 0)))
        out_shape.append(jax.ShapeDtypeStruct((rows, cols), dt))
    for shp, dt in acc_out:
        out_specs.append(pl.BlockSpec(shp, functools.partial(lambda i, nd: (0,) * nd, nd=len(shp))))
        out_shape.append(jax.ShapeDtypeStruct(shp, dt))
    scratch_shapes = [pltpu.VMEM(a.shape, a.dtype) for a in big_in] + list(scratch)

    def kern(*refs):
        ins = refs[: n_t + n_f]
        big_hbm = refs[n_t + n_f : n_t + n_f + n_b]
        outs = refs[n_t + n_f + n_b : n_t + n_f + n_b + n_to + n_a]
        rest = refs[n_t + n_f + n_b + n_to + n_a :]
        big_vmem, scr = rest[:n_b], rest[n_b:]
        step = pl.program_id(0)

        @pl.when(step == 0)
        def _():
            for src, dst in zip(big_hbm, big_vmem):
                pltpu.sync_copy(src, dst)
            for acc in outs[n_to:]:
                acc[...] = jnp.zeros(acc.shape, acc.dtype)

        body(row(step), *ins, *big_vmem, *outs, *scr)

    res = pl.pallas_call(
        kern,
        out_shape=out_shape,
        grid=(n_steps,),
        in_specs=in_specs,
        out_specs=out_specs,
        scratch_shapes=scratch_shapes,
        name=name,
        compiler_params=pltpu.CompilerParams(dimension_semantics=("arbitrary",), vmem_limit_bytes=VMEM_LIMIT),
    )(*args)
    return res


FF_CHUNKS = ((0, 1536), (1536, D_FF))
FFN_TM = 256


def _ffn_fwd(h, g, wg, wu, wd, name):
    T = h.shape[0]

    def body(i, h_ref, g_ref, wg_ref, wu_ref, wd_ref, o_ref):
        x = h_ref[...]
        n = _rms(x, g_ref[...]).astype(BF16)
        f = jnp.zeros(x.shape, F32)
        for lo, hi in FF_CHUNKS:
            a = jnp.dot(n, wg_ref[:, lo:hi], preferred_element_type=F32)
            b = jnp.dot(n, wu_ref[:, lo:hi], preferred_element_type=F32)
            s = (_silu(a) * b).astype(BF16)
            f = f + jnp.dot(s, wd_ref[lo:hi, :], preferred_element_type=F32)
        o_ref[...] = x + 0.5 * f

    return _tiled(body, name, T // FFN_TM, [(h, FFN_TM, D_MODEL, 0)], [g], [wg, wu, wd],
                  [(T, D_MODEL, F32, FFN_TM)], [])[0]


def _ffn_dgrad(h, dout, g, wg, wu, wd, name):
    T = h.shape[0]

    def body(i, h_ref, do_ref, g_ref, wg_ref, wu_ref, wd_ref, dh_ref, n_ref, s_ref, da_ref, db_ref, dg_ref):
        x = h_ref[...]
        dout = do_ref[...]
        nf, rms_vjp = jax.vjp(_rms, x, g_ref[...])
        n = nf.astype(BF16)
        dfo = (0.5 * dout).astype(BF16)
        dn = jnp.zeros(x.shape, F32)
        for lo, hi in FF_CHUNKS:
            a = jnp.dot(n, wg_ref[:, lo:hi], preferred_element_type=F32)
            b = jnp.dot(n, wu_ref[:, lo:hi], preferred_element_type=F32)
            sg = jax.nn.sigmoid(a)
            sl = a * sg
            ds = _dot_nt(dfo, wd_ref[lo:hi, :])
            db = (ds * sl).astype(BF16)
            da = (ds * b * (sg * (1.0 + a * (1.0 - sg)))).astype(BF16)
            dn = dn + _dot_nt(da, wg_ref[:, lo:hi]) + _dot_nt(db, wu_ref[:, lo:hi])
            s_ref[:, lo:hi] = (sl * b).astype(BF16)
            da_ref[:, lo:hi] = da
            db_ref[:, lo:hi] = db
        dx, dg = rms_vjp(dn)
        dh_ref[...] = dout + dx
        n_ref[...] = n
        dg_ref[...] += dg

    return _tiled(body, name, T // FFN_TM, [(h, FFN_TM, D_MODEL, 0), (dout, FFN_TM, D_MODEL, 0)], [g], [wg, wu, wd],
                  [(T, D_MODEL, F32, FFN_TM), (T, D_MODEL, BF16, FFN_TM), (T, D_FF, BF16, FFN_TM),
                   (T, D_FF, BF16, FFN_TM), (T, D_FF, BF16, FFN_TM)], [((1, D_MODEL), F32)])


def _wgrad(a, b, bn, name, a_cols=None, b_cols=None, scale=None, bk=512):
    T = a.shape[0]
    a0, M = a_cols if a_cols else (0, a.shape[1])
    b0, N = b_cols if b_cols else (0, b.shape[1])
    bk = min(bk, T)
    assert M % LANES == 0 and a0 % M == 0 and N % bn == 0 and b0 % bn == 0 and T % bk == 0
    n_k = T // bk

    def kern(a_ref, b_ref, o_ref):
        @pl.when(pl.program_id(1) == 0)
        def _():
            o_ref[...] = jnp.zeros(o_ref.shape, F32)

        bv = b_ref[...]
        if scale is not None:
            bv = bv * scale
        o_ref[...] += _dot_tn(a_ref[...], bv)

    return pl.pallas_call(
        kern,
        out_shape=jax.ShapeDtypeStruct((M, N), F32),
        grid=(N // bn, n_k),
        in_specs=[pl.BlockSpec((bk, M), lambda j, k: (k, a0 // M)),
                  pl.BlockSpec((bk, bn), lambda j, k: (k, b0 // bn + j))],
        out_specs=pl.BlockSpec((M, bn), lambda j, k: (0, j)),
        name=name,
        compiler_params=pltpu.CompilerParams(dimension_semantics=("arbitrary", "arbitrary"), vmem_limit_bytes=VMEM_LIMIT),
    )(a, b)


PROJ_TM = 256


def _mix_in_fwd(h, g, w_in):
    T = h.shape[0]

    def body(i, h_ref, g_ref, w_ref, p_ref, n_ref):
        n = _rms(h_ref[...], g_ref[...]).astype(BF16)
        n_ref[...] = n
        p_ref[...] = jnp.dot(n, w_ref[...], preferred_element_type=F32)

    return _tiled(body, "mix_in_fwd", T // PROJ_TM, [(h, PROJ_TM, D_MODEL, 0)], [g], [w_in],
                  [(T, IN_PROJ_PAD, F32, PROJ_TM), (T, D_MODEL, BF16, PROJ_TM)], [])


def _mix_in_dgrad(h, dh_in, dp_uv, dp_zxd, g, w_in):
    T = h.shape[0]

    def body(i, h_ref, dh_ref, duv_ref, dzxd_ref, g_ref, w_ref, o_ref, dg_ref):
        dn = _dot_nt(duv_ref[...], w_ref[:, :UV_W]) + _dot_nt(dzxd_ref[...], w_ref[:, UV_W:])
        _, rms_vjp = jax.vjp(_rms, h_ref[...], g_ref[...])
        dx, dg = rms_vjp(dn)
        o_ref[...] = dh_ref[...] + dx
        dg_ref[...] += dg

    return _tiled(body, "mix_in_dgrad", T // PROJ_TM,
                  [(h, PROJ_TM, D_MODEL, 0), (dh_in, PROJ_TM, D_MODEL, 0), (dp_uv, PROJ_TM, UV_W, 0),
                   (dp_zxd, PROJ_TM, ZXD_W, 0)], [g], [w_in],
                  [(T, D_MODEL, F32, PROJ_TM)], [((1, D_MODEL), F32)])


def _out_proj_fwd(h, ya, yb, w_out):
    T = h.shape[0]

    def body(i, h_ref, ya_ref, yb_ref, w_ref, o_ref):
        o_ref[...] = (h_ref[...] + jnp.dot(ya_ref[...], w_ref[:GM_WIDTH, :], preferred_element_type=F32)
                      + jnp.dot(yb_ref[...], w_ref[GM_WIDTH:, :], preferred_element_type=F32))

    return _tiled(body, "out_proj_fwd", T // PROJ_TM,
                  [(h, PROJ_TM, D_MODEL, 0), (ya, PROJ_TM, GM_WIDTH, 0), (yb, PROJ_TM, SSM_WIDTH, 0)], [], [w_out],
                  [(T, D_MODEL, F32, PROJ_TM)], [])[0]


def _out_proj_dgrad(dh, w_out):
    T = dh.shape[0]

    def body(i, dh_ref, w_ref, dya_ref, dyb_ref):
        d = dh_ref[...].astype(BF16)
        dya_ref[...] = _dot_nt(d, w_ref[:GM_WIDTH, :])
        dyb_ref[...] = _dot_nt(d, w_ref[GM_WIDTH:, :])

    return _tiled(body, "out_proj_dgrad", T // PROJ_TM, [(dh, PROJ_TM, D_MODEL, 0)], [], [w_out],
                  [(T, GM_WIDTH, F32, PROJ_TM), (T, SSM_WIDTH, F32, PROJ_TM)], [])


def _gm_chunk(u, v, ln_g, ln_b, b_st, out_g, *w_heads):
    ug = _gelu(u)
    vg = _gelu(v)
    mu = jnp.mean(vg, axis=-1, keepdims=True)
    xc = vg - mu
    vn = xc * lax.rsqrt(jnp.mean(xc * xc, axis=-1, keepdims=True) + EPS) * ln_g + ln_b
    t_idx = lax.broadcasted_iota(jnp.int32, (CHUNK, CHUNK), 0)
    s_idx = lax.broadcasted_iota(jnp.int32, (CHUNK, CHUNK), 1)
    causal = t_idx >= s_idx
    mixed = []
    for hd in range(GM_HEADS):
        wm = jnp.where(causal, w_heads[hd], 0.0)
        cols = slice(hd * GM_HEAD_DIM, (hd + 1) * GM_HEAD_DIM)
        mixed.append(_dot(wm, vn[:, cols]) + b_st[:, hd:hd + 1])
    ya0 = ug * jnp.concatenate(mixed, axis=1)
    return _rms(ya0, out_g)


def _gm_fwd(proj, ln_g, ln_b, w_s, b_st, out_g):
    T = proj.shape[0]

    def body(i, u_ref, v_ref, lg_ref, lb_ref, w_ref, bs_ref, og_ref, ya_ref):
        w_heads = [w_ref[hd] for hd in range(GM_HEADS)]
        ya = _gm_chunk(u_ref[...], v_ref[...], lg_ref[...], lb_ref[...], bs_ref[...], og_ref[...], *w_heads)
        ya_ref[...] = ya.astype(BF16)

    return _tiled(body, "gmlp_fwd", T // CHUNK, [(proj, CHUNK, GM_WIDTH, 0), (proj, CHUNK, GM_WIDTH, 1)],
                  [ln_g, ln_b, w_s, b_st, out_g], [], [(T, GM_WIDTH, BF16, CHUNK)], [])[0]


def _gm_bwd(proj, dya, ln_g, ln_b, w_s, b_st, out_g):
    T = proj.shape[0]

    def body(i, u_ref, v_ref, dy_ref, lg_ref, lb_ref, w_ref, bs_ref, og_ref, duv_ref, dlg_ref, dlb_ref, dw_ref, dbs_ref,
             dog_ref):
        w_heads = [w_ref[hd] for hd in range(GM_HEADS)]
        _, vjp = jax.vjp(_gm_chunk, u_ref[...], v_ref[...], lg_ref[...], lb_ref[...], bs_ref[...], og_ref[...], *w_heads)
        grads = vjp(dy_ref[...])
        duv_ref[:, :GM_WIDTH] = grads[0].astype(BF16)
        duv_ref[:, GM_WIDTH:] = grads[1].astype(BF16)
        dlg_ref[...] += grads[2]
        dlb_ref[...] += grads[3]
        dbs_ref[...] += grads[4]
        dog_ref[...] += grads[5]
        for hd in range(GM_HEADS):
            dw_ref[hd] += grads[6 + hd]

    return _tiled(body, "gmlp_bwd", T // CHUNK,
                  [(proj, CHUNK, GM_WIDTH, 0), (proj, CHUNK, GM_WIDTH, 1), (dya, CHUNK, GM_WIDTH, 0)],
                  [ln_g, ln_b, w_s, b_st, out_g], [], [(T, UV_W, BF16, CHUNK)],
                  [((1, GM_WIDTH), F32), ((1, GM_WIDTH), F32), ((GM_HEADS, CHUNK, CHUNK), F32),
                   ((CHUNK, GM_HEADS), F32), ((1, GM_WIDTH), F32)])


def _ssd_chunk(xc, z, dtr, s_in, dt_bias, a_log, d_skip, norm_g):
    half = SSM_WIDTH // SSM_GROUPS
    l_idx = lax.broadcasted_iota(jnp.int32, (CHUNK, CHUNK), 0)
    s_idx = lax.broadcasted_iota(jnp.int32, (CHUNK, CHUNK), 1)
    causal = l_idx >= s_idx
    tril = causal.astype(F32)
    head_of_col = lax.broadcasted_iota(jnp.int32, (SSM_HEADS, SSM_WIDTH), 1) // SSM_HEAD_DIM
    expand = (head_of_col == lax.broadcasted_iota(jnp.int32, (SSM_HEADS, SSM_WIDTH), 0)).astype(F32)

    xcs = _silu(xc)
    xs = xcs[:, :SSM_WIDTH]
    dt = jax.nn.softplus(dtr + dt_bias)
    adt = dt * (-jnp.exp(a_log))
    acs = _hdot(tril, adt)
    acs_t = _hdot_tn(adt, 1.0 - tril + (l_idx == s_idx).astype(F32))
    tot = acs[CHUNK - 1:CHUNK, :]
    dt_w = _hdot(dt, expand)
    out_decay_w = _hdot(jnp.exp(acs), expand)
    state_decay_w = _hdot(jnp.exp(tot - acs), expand)
    chunk_decay_w = _hdot(jnp.exp(tot), expand)
    d_skip_w = _hdot(d_skip, expand)
    xdt = xs * dt_w
    xdt_decayed = xdt * state_decay_w

    y_diag, y_off, states = [], [], []
    for grp in range(SSM_GROUPS):
        b0 = SSM_WIDTH + grp * SSM_STATE
        c0 = SSM_WIDTH + SSM_GROUPS * SSM_STATE + grp * SSM_STATE
        bm = xcs[:, b0:b0 + SSM_STATE].astype(BF16)
        cm = xcs[:, c0:c0 + SSM_STATE].astype(BF16)
        cb = _dot_nt(cm, bm)
        for k in range(grp * SSM_HEADS // SSM_GROUPS, (grp + 1) * SSM_HEADS // SSM_GROUPS):
            decay = jnp.exp(jnp.where(causal, acs[:, k:k + 1] - acs_t[k:k + 1, :], -jnp.inf))
            y_diag.append(_dot(cb * decay, xdt[:, k * SSM_HEAD_DIM:(k + 1) * SSM_HEAD_DIM]))
        cols = slice(grp * half, (grp + 1) * half)
        states.append(_dot_tn(bm, xdt_decayed[:, cols]))
        y_off.append(_dot(cm, s_in[:, cols]))
    y = jnp.concatenate(y_diag, axis=1) + jnp.concatenate(y_off, axis=1) * out_decay_w + xs * d_skip_w
    s_out = s_in * chunk_decay_w + jnp.concatenate(states, axis=1)
    y = y * _silu(z)
    y3 = y.reshape(CHUNK, SSM_GROUPS, half)
    y3 = y3 * lax.rsqrt(jnp.mean(y3 * y3, axis=-1, keepdims=True) + EPS)
    return y3.reshape(CHUNK, SSM_WIDTH) * norm_g, s_out


def _conv_taps(ext_ref, w, b):
    y = b
    for k in range(SSM_CONV):
        y = y + w[k:k + 1, :] * ext_ref[pl.ds(HALO - (SSM_CONV - 1) + k, CHUNK), :]
    return y


def _ssd_fwd(proj, conv_w, conv_b, dt_bias, a_log, d_skip, norm_g):
    T = proj.shape[0]
    n_chunks = T // CHUNK

    def body(i, z_ref, x_ref, dt_ref, cw_ref, cb_ref, dtb_ref, al_ref, dsk_ref, ng_ref, yb_ref, sin_ref, ext_ref, st_ref):
        @pl.when(i == 0)
        def _():
            ext_ref[0:HALO, :] = jnp.zeros((HALO, CONV_DIM), F32)
            st_ref[...] = jnp.zeros(st_ref.shape, F32)

        ext_ref[HALO:, :] = x_ref[...]
        xc = _conv_taps(ext_ref, cw_ref[...], cb_ref[...])
        s_in = st_ref[...]
        yb, s_out = _ssd_chunk(xc, z_ref[...], dt_ref[:, 0:SSM_HEADS], s_in, dtb_ref[...], al_ref[...], dsk_ref[...],
                               ng_ref[...])
        yb_ref[...] = yb.astype(BF16)
        sin_ref[...] = s_in
        st_ref[...] = s_out
        ext_ref[0:HALO, :] = ext_ref[CHUNK:CHUNK + HALO, :]

    z_blk = 2 * GM_WIDTH // SSM_WIDTH
    x_blk = (2 * GM_WIDTH + SSM_WIDTH) // CONV_DIM
    dt_blk = (2 * GM_WIDTH + SSM_WIDTH + CONV_DIM) // LANES
    return _tiled(body, "ssd_fwd", n_chunks,
                  [(proj, CHUNK, SSM_WIDTH, z_blk), (proj, CHUNK, CONV_DIM, x_blk), (proj, CHUNK, LANES, dt_blk)],
                  [conv_w, conv_b, dt_bias, a_log, d_skip, norm_g], [],
                  [(T, SSM_WIDTH, BF16, CHUNK), (n_chunks * SSM_STATE, SSM_WIDTH, F32, SSM_STATE)], [],
                  scratch=[pltpu.VMEM((HALO + CHUNK, CONV_DIM), F32), pltpu.VMEM((SSM_STATE, SSM_WIDTH), F32)])


def _ssd_bwd(proj, dyb, s_all, conv_w, conv_b, dt_bias, a_log, d_skip, norm_g):
    T = proj.shape[0]
    n_chunks = T // CHUNK
    z_blk = 2 * GM_WIDTH // SSM_WIDTH
    x_blk = (2 * GM_WIDTH + SSM_WIDTH) // CONV_DIM
    dt_blk = (2 * GM_WIDTH + SSM_WIDTH + CONV_DIM) // LANES
    rows_per_halo = CHUNK // HALO

    def body(i, z_ref, x_ref, halo_ref, dt_ref, dy_ref, sin_ref, cw_ref, cb_ref, dtb_ref, al_ref, dsk_ref, ng_ref,
             dzxd_ref, dcw_ref, dcb_ref, ddtb_ref, dal_ref, ddsk_ref, dng_ref, ext_ref, dext_ref, dst_ref):
        @pl.when(i == n_chunks - 1)
        def _():
            dext_ref[CHUNK:, :] = jnp.zeros((HALO, CONV_DIM), F32)
            dst_ref[...] = jnp.zeros(dst_ref.shape, F32)

        halo = halo_ref[...]
        ext_ref[0:HALO, :] = jnp.where(i == 0, jnp.zeros_like(halo), halo)
        ext_ref[HALO:, :] = x_ref[...]
        cw = cw_ref[...]
        xc = _conv_taps(ext_ref, cw, cb_ref[...])
        _, vjp = jax.vjp(_ssd_chunk, xc, z_ref[...], dt_ref[:, 0:SSM_HEADS], sin_ref[...], dtb_ref[...], al_ref[...],
                         dsk_ref[...], ng_ref[...])
        dxc, dz, ddtr, ds_in, ddtb, dal, ddsk, dng = vjp((dy_ref[...], dst_ref[...]))
        dst_ref[...] = ds_in
        ddtb_ref[...] += ddtb
        dal_ref[...] += dal
        ddsk_ref[...] += ddsk
        dng_ref[...] += dng
        dext_ref[0:CHUNK, :] = dxc
        dx = jnp.zeros((CHUNK, CONV_DIM), F32)
        for k in range(SSM_CONV):
            dx = dx + cw[k:k + 1, :] * dext_ref[pl.ds(SSM_CONV - 1 - k, CHUNK), :]
            dcw_ref[k:k + 1, :] += jnp.sum(dxc * ext_ref[pl.ds(HALO - (SSM_CONV - 1) + k, CHUNK), :], axis=0, keepdims=True)
        dcb_ref[...] += jnp.sum(dxc, axis=0, keepdims=True)
        dext_ref[CHUNK:, :] = dext_ref[0:HALO, :]
        dzxd_ref[:, 0:SSM_WIDTH] = dz.astype(BF16)
        dzxd_ref[:, SSM_WIDTH:SSM_WIDTH + CONV_DIM] = dx.astype(BF16)
        dzxd_ref[:, SSM_WIDTH + CONV_DIM:] = jnp.concatenate(
            [ddtr, jnp.zeros((CHUNK, LANES - SSM_HEADS), F32)], axis=1).astype(BF16)

    def halo_index(step):
        c = n_chunks - 1 - step
        return (jnp.maximum(c * rows_per_halo - 1, 0), x_blk)

    return _tiled(body, "ssd_bwd", n_chunks,
                  [(proj, CHUNK, SSM_WIDTH, z_blk), (proj, CHUNK, CONV_DIM, x_blk), (proj, HALO, CONV_DIM, halo_index),
                   (proj, CHUNK, LANES, dt_blk), (dyb, CHUNK, SSM_WIDTH, 0), (s_all, SSM_STATE, SSM_WIDTH, 0)],
                  [conv_w, conv_b, dt_bias, a_log, d_skip, norm_g], [],
                  [(T, ZXD_W, BF16, CHUNK)],
                  [((SSM_CONV, CONV_DIM), F32), ((1, CONV_DIM), F32), ((1, SSM_HEADS), F32), ((1, SSM_HEADS), F32),
                   ((1, SSM_HEADS), F32), ((1, SSM_WIDTH), F32)],
                  scratch=[pltpu.VMEM((HALO + CHUNK, CONV_DIM), F32), pltpu.VMEM((CHUNK + HALO, CONV_DIM), F32),
                           pltpu.VMEM((SSM_STATE, SSM_WIDTH), F32)],
                  reverse=True)


TAIL_TM = 256


def _tail(h, p, target, ple_norm, w_gate, b_gate, w_proj, final_norm):
    T = h.shape[0]

    def head(x, pre, pp, b_g, f_norm, tgt):
        gate = jax.nn.sigmoid(pre + b_g)
        out = _rms(x + gate * pp, f_norm)
        err = out - tgt
        return 0.5 * jnp.sum(jnp.mean(err * err, axis=-1, keepdims=True), axis=0, keepdims=True)

    def body(i, h_ref, p_ref, t_ref, pn_ref, bg_ref, fn_ref, wg_ref, wp_ref, dh_ref, loss_ref, dwg_ref, dwp_ref, dpn_ref,
             dbg_ref, dfn_ref):
        x = h_ref[...]
        n4f, n_vjp = jax.vjp(_rms, x, pn_ref[...])
        n4 = n4f.astype(BF16)
        pre = jnp.dot(n4, wg_ref[...], preferred_element_type=F32)
        p16 = p_ref[...].astype(BF16)
        pp = jnp.dot(p16, wp_ref[...], preferred_element_type=F32)
        loss, h_vjp = jax.vjp(functools.partial(head, tgt=t_ref[...]), x, pre, pp, bg_ref[...], fn_ref[...])
        dx, dpre, dpp, dbg, dfn = h_vjp(jnp.ones((1, 1), F32))
        dpre16 = dpre.astype(BF16)
        dn4 = _dot_nt(dpre16, wg_ref[...])
        dx2, dpn = n_vjp(dn4)
        dh_ref[...] = dx + dx2
        loss_ref[...] += loss
        dwg_ref[...] += _dot_tn(n4, dpre16)
        dwp_ref[...] += _dot_tn(p16, dpp)
        dpn_ref[...] += dpn
        dbg_ref[...] += dbg
        dfn_ref[...] += dfn

    return _tiled(body, "tail", T // TAIL_TM,
                  [(h, TAIL_TM, D_MODEL, 0), (p, TAIL_TM, D_PLE, 0), (target, TAIL_TM, D_MODEL, 0)],
                  [ple_norm, b_gate, final_norm], [w_gate, w_proj],
                  [(T, D_MODEL, F32, TAIL_TM)],
                  [((1, 1), F32), ((D_MODEL, D_MODEL), F32), ((D_PLE, D_MODEL), F32), ((1, D_MODEL), F32),
                   ((1, D_MODEL), F32), ((1, D_MODEL), F32)])


def _all_gather(x, name):
    R, C = x.shape

    def body(x_ref, out_ref, send_sems, recv_sems, local_sem):
        mx, my, mc = lax.axis_index("x"), lax.axis_index("y"), lax.axis_index("c")
        me, sibling = (mx, my, mc), (mx, my, 1 - mc)
        chips = [(1 - mx, my), (mx, 1 - my), (1 - mx, 1 - my)]

        def rows(px, py, pc):
            return out_ref.at[4 * px + 2 * py + pc]

        def copy(k, block, to, src=None):
            return pltpu.make_async_remote_copy(
                src_ref=rows(*block) if src is None else src, dst_ref=rows(*block),
                send_sem=send_sems.at[k], recv_sem=recv_sems.at[k], device_id=to, device_id_type=MESH)

        mine = pltpu.make_async_copy(x_ref, rows(*me), local_sem)
        mine.start()
        first = [copy(0, me, sibling, src=x_ref)]
        first += [copy(1 + j, me, (*chip, mc), src=x_ref) for j, chip in enumerate(chips)]
        for cp in first:
            cp.start()
        passed = [copy(4 + j, (*chip, mc), sibling) for j, chip in enumerate(chips)]
        for j, chip in enumerate(chips):
            copy(1 + j, (*chip, mc), me).wait_recv()
            passed[j].start()
        copy(0, sibling, me).wait_recv()
        for j, chip in enumerate(chips):
            copy(4 + j, (*chip, 1 - mc), me).wait_recv()
        for cp in first + passed:
            cp.wait_send()
        mine.wait()

    return pl.pallas_call(
        body,
        out_shape=jax.ShapeDtypeStruct((N_DEV, R, C), x.dtype),
        in_specs=[pl.BlockSpec(memory_space=pl.ANY)],
        out_specs=pl.BlockSpec(memory_space=pl.ANY),
        scratch_shapes=[pltpu.SemaphoreType.DMA((7,)), pltpu.SemaphoreType.DMA((7,)), pltpu.SemaphoreType.DMA],
        name=name,
    )(x)


def _exchange(x, name):
    _, R, C = x.shape

    def body(x_ref, out_ref, send_sems, recv_sems, local_sem):
        mx, my, mc = lax.axis_index("x"), lax.axis_index("y"), lax.axis_index("c")
        me = 4 * mx + 2 * my + mc
        mine = pltpu.make_async_copy(x_ref.at[me], out_ref.at[me], local_sem)
        mine.start()
        copies = []
        for k in range(1, N_DEV):
            px = 1 - mx if k & 4 else mx
            py = 1 - my if k & 2 else my
            pc = 1 - mc if k & 1 else mc
            peer = 4 * px + 2 * py + pc
            copies.append(pltpu.make_async_remote_copy(
                src_ref=x_ref.at[peer], dst_ref=out_ref.at[me], send_sem=send_sems.at[k - 1],
                recv_sem=recv_sems.at[k - 1], device_id=(px, py, pc), device_id_type=MESH))
        for cp in copies:
            cp.start()
        for cp in copies:
            cp.wait_recv()
        for cp in copies:
            cp.wait_send()
        mine.wait()

    return pl.pallas_call(
        body,
        out_shape=jax.ShapeDtypeStruct(x.shape, x.dtype),
        in_specs=[pl.BlockSpec(memory_space=pl.ANY)],
        out_specs=pl.BlockSpec(memory_space=pl.ANY),
        scratch_shapes=[pltpu.SemaphoreType.DMA((7,)), pltpu.SemaphoreType.DMA((7,)), pltpu.SemaphoreType.DMA],
        name=name,
    )(x)


def _sum_adamw(parts, w, m, v, tr, name):
    _, R, C = parts.shape

    def kern(p_ref, w_ref, m_ref, v_ref, g_ref, d_ref, nm_ref, nv_ref):
        g = p_ref[0].astype(F32)
        for j in range(1, N_DEV):
            g = g + p_ref[j].astype(F32)
        m_new = ADAM_B1 * m_ref[...] + (1.0 - ADAM_B1) * g
        v_new = ADAM_B2 * v_ref[...] + (1.0 - ADAM_B2) * jnp.square(g)
        m_hat = m_new / (1.0 - ADAM_B1 ** ADAM_STEP)
        v_hat = v_new / (1.0 - ADAM_B2 ** ADAM_STEP)
        g_ref[...] = g
        d_ref[...] = -ADAM_LR * (m_hat / (jnp.sqrt(v_hat) + ADAM_EPS) + ADAM_WD * w_ref[...])
        nm_ref[...] = m_new
        nv_ref[...] = v_new

    row_spec = pl.BlockSpec((tr, C), lambda i: (i, 0))
    return pl.pallas_call(
        kern,
        out_shape=[jax.ShapeDtypeStruct((R, C), F32)] * 4,
        grid=(R // tr,),
        in_specs=[pl.BlockSpec((N_DEV, tr, C), lambda i: (0, i, 0)), row_spec, row_spec, row_spec],
        out_specs=[row_spec] * 4,
        name=name,
        compiler_params=pltpu.CompilerParams(dimension_semantics=("arbitrary",), vmem_limit_bytes=VMEM_LIMIT),
    )(parts, w, m, v)


BIG = (("ffn1_w_gate", (D_MODEL, D_FF), 1), ("ffn1_w_up", (D_MODEL, D_FF), 1), ("ffn1_w_down", (D_FF, D_MODEL), 0),
       ("w_in", (D_MODEL, IN_PROJ), 1), ("w_out", (2 * D_MODEL, D_MODEL), 0),
       ("ffn2_w_gate", (D_MODEL, D_FF), 1), ("ffn2_w_up", (D_MODEL, D_FF), 1), ("ffn2_w_down", (D_FF, D_MODEL), 0),
       ("ple_w_gate", (D_MODEL, D_MODEL), 0), ("ple_w_proj", (D_PLE, D_MODEL), 1), ("conv_w", (SSM_CONV, CONV_DIM), 1))
BIG_ROWS = 3200
BIG_TR = 320
SMALL = ("ffn1_norm", "mix_norm", "gm_ln_g", "gm_ln_b", "gm_w_s", "gm_b_s", "gm_out_norm", "conv_b", "dt_bias", "a_log",
         "d_skip", "ssm_norm", "ffn2_norm", "ple_norm", "ple_b_gate", "final_norm")
SMALL_ROWS = 144


def _shard_shape(shape, axis):
    return tuple(s // N_DEV if a == axis else s for a, s in enumerate(shape))


def _pack_rows(flat_parts, rows):
    flat = jnp.concatenate(flat_parts)
    return jnp.pad(flat, (0, rows * PACK_COLS - flat.shape[0])).reshape(rows, PACK_COLS)


GATHERED = BIG + (("conv_w_mid", (SSM_CONV, CONV_DIM), 1), ("conv_w_low", (SSM_CONV, CONV_DIM), 1))


def _pack_big_shards(shards, entries=BIG):
    return _pack_rows([shards[name].reshape(-1) for name, _, _ in entries], BIG_ROWS)


def _unpack_big_shards(packed):
    out, off = {}, 0
    flat = packed.reshape(-1)
    for name, shape, axis in BIG:
        shp = _shard_shape(shape, axis)
        n = shp[0] * shp[1]
        out[name] = flat[off:off + n].reshape(shp)
        off += n
    return out


def _unpack_gathered(gathered):
    out, off = {}, 0
    flat = gathered.reshape(N_DEV, -1)
    for name, shape, axis in GATHERED:
        shp = _shard_shape(shape, axis)
        n = shp[0] * shp[1]
        blocks = flat[:, off:off + n].reshape((N_DEV,) + shp)
        if axis == 0:
            out[name] = blocks.reshape(shape)
        else:
            out[name] = jnp.transpose(blocks, (1, 0, 2)).reshape(shape)
        off += n
    return out


def _pack_full_grads(grads):
    parts = []
    for name, shape, axis in BIG:
        g = grads[name].astype(BF16)
        shp = _shard_shape(shape, axis)
        if axis == 0:
            parts.append(g.reshape(N_DEV, -1))
        else:
            parts.append(jnp.transpose(g.reshape(shape[0], N_DEV, shp[1]), (1, 0, 2)).reshape(N_DEV, -1))
    flat = jnp.concatenate(parts, axis=1)
    flat = jnp.pad(flat, ((0, 0), (0, BIG_ROWS * PACK_COLS - flat.shape[1])))
    return flat.reshape(N_DEV, BIG_ROWS, PACK_COLS)


def _pack_small(vals):
    return _pack_rows([vals[name].reshape(-1).astype(F32) for name in SMALL], SMALL_ROWS)


def _unpack_small(packed, shapes):
    out, off = {}, 0
    flat = packed.reshape(-1)
    for name in SMALL:
        n = 1
        for s in shapes[name]:
            n *= s
        out[name] = flat[off:off + n].reshape(shapes[name])
        off += n
    return out


WEIGHTS = ("ffn1_norm", "ffn1_w_gate", "ffn1_w_up", "ffn1_w_down", "mix_norm", "w_in", "gm_ln_g", "gm_ln_b", "gm_w_s",
           "gm_b_s", "gm_out_norm", "conv_w", "conv_b", "dt_bias", "a_log", "d_skip", "ssm_norm", "w_out", "ffn2_norm",
           "ffn2_w_gate", "ffn2_w_up", "ffn2_w_down", "ple_norm", "ple_w_gate", "ple_b_gate", "ple_w_proj", "final_norm")


def _step(x, p, target, w, m, v):
    T = x.shape[0]
    big_names = [name for name, _, _ in BIG]
    local = lambda d: {name: d[name][0] for name in big_names}

    shards = {name: val.astype(BF16) for name, val in local(w).items()}
    conv_rest = w["conv_w"][0] - shards["conv_w"].astype(F32)
    shards["conv_w_mid"] = conv_rest.astype(BF16)
    shards["conv_w_low"] = (conv_rest - shards["conv_w_mid"].astype(F32)).astype(BF16)
    gathered = _all_gather(_pack_big_shards(shards, GATHERED), "gather_weights")
    full = _unpack_gathered(gathered)
    w_in = full["w_in"]
    w_in = jnp.concatenate([w_in, jnp.zeros((D_MODEL, IN_PROJ_PAD - IN_PROJ), BF16)], axis=1)
    conv_w = full["conv_w"].astype(F32) + full["conv_w_mid"].astype(F32) + full["conv_w_low"].astype(F32)

    row = lambda name: w[name].reshape(1, -1)
    gm_w_s = w["gm_w_s"][0]
    gm_b_st = jnp.transpose(w["gm_b_s"][0])

    h1 = _ffn_fwd(x, row("ffn1_norm"), full["ffn1_w_gate"], full["ffn1_w_up"], full["ffn1_w_down"], "ffn1_fwd")
    proj, n2 = _mix_in_fwd(h1, row("mix_norm"), w_in)
    ya = _gm_fwd(proj, row("gm_ln_g"), row("gm_ln_b"), gm_w_s, gm_b_st, row("gm_out_norm"))
    yb, s_all = _ssd_fwd(proj, conv_w, row("conv_b"), row("dt_bias"), row("a_log"), row("d_skip"), row("ssm_norm"))
    h2 = _out_proj_fwd(h1, ya, yb, full["w_out"])
    h3 = _ffn_fwd(h2, row("ffn2_norm"), full["ffn2_w_gate"], full["ffn2_w_up"], full["ffn2_w_down"], "ffn2_fwd")

    g = {}
    dh3, loss, g["ple_w_gate"], g["ple_w_proj"], g["ple_norm"], g["ple_b_gate"], g["final_norm"] = _tail(
        h3, p, target, row("ple_norm"), full["ple_w_gate"], row("ple_b_gate"), full["ple_w_proj"], row("final_norm"))

    dh2, n3, s3, da3, db3, g["ffn2_norm"] = _ffn_dgrad(
        h2, dh3, row("ffn2_norm"), full["ffn2_w_gate"], full["ffn2_w_up"], full["ffn2_w_down"], "ffn2_dgrad")
    g["ffn2_w_gate"] = _wgrad(n3, da3, 1408, "ffn2_wgrad_gate")
    g["ffn2_w_up"] = _wgrad(n3, db3, 1408, "ffn2_wgrad_up")
    g["ffn2_w_down"] = _wgrad(s3, dh3, 512, "ffn2_wgrad_down", scale=0.5)

    dya, dyb = _out_proj_dgrad(dh2, full["w_out"])
    g["w_out"] = jnp.concatenate([_wgrad(ya, dh2, 1024, "w_out_wgrad_a"), _wgrad(yb, dh2, 1024, "w_out_wgrad_b")], axis=0)

    dp_zxd, g["conv_w"], g["conv_b"], g["dt_bias"], g["a_log"], g["d_skip"], g["ssm_norm"] = _ssd_bwd(
        proj, dyb, s_all, conv_w, row("conv_b"), row("dt_bias"), row("a_log"), row("d_skip"), row("ssm_norm"))
    dp_uv, g["gm_ln_g"], g["gm_ln_b"], g["gm_w_s"], dbst, g["gm_out_norm"] = _gm_bwd(
        proj, dya, row("gm_ln_g"), row("gm_ln_b"), gm_w_s, gm_b_st, row("gm_out_norm"))
    g["gm_b_s"] = jnp.transpose(dbst)

    dh1, g["mix_norm"] = _mix_in_dgrad(h1, dh2, dp_uv, dp_zxd, row("mix_norm"), w_in)
    g["w_in"] = jnp.concatenate([_wgrad(n2, dp_uv, 1024, "w_in_wgrad_uv"), _wgrad(n2, dp_zxd, 896, "w_in_wgrad_zxd")],
                                axis=1)[:, :IN_PROJ]

    dx, n1, s1, da1, db1, g["ffn1_norm"] = _ffn_dgrad(
        x, dh1, row("ffn1_norm"), full["ffn1_w_gate"], full["ffn1_w_up"], full["ffn1_w_down"], "ffn1_dgrad")
    g["ffn1_w_gate"] = _wgrad(n1, da1, 1408, "ffn1_wgrad_gate")
    g["ffn1_w_up"] = _wgrad(n1, db1, 1408, "ffn1_wgrad_up")
    g["ffn1_w_down"] = _wgrad(s1, dh1, 512, "ffn1_wgrad_down", scale=0.5)

    parts = _exchange(_pack_full_grads(g), "scatter_grads")
    res_big = _sum_adamw(parts, _pack_big_shards(local(w)), _pack_big_shards(local(m)), _pack_big_shards(local(v)),
                         BIG_TR, "adamw_shards")
    res_big = [_unpack_big_shards(r) for r in res_big]

    small_shapes = {name: w[name].shape for name in SMALL}
    small_parts = _all_gather(_pack_small(g), "gather_small_grads")
    res_small = _sum_adamw(small_parts, _pack_small(w), _pack_small(m), _pack_small(v), SMALL_ROWS, "adamw_small")
    res_small = [_unpack_small(r, small_shapes) for r in res_small]

    outs = []
    for k in range(4):
        for name in WEIGHTS:
            if name in res_small[k]:
                outs.append(res_small[k][name])
            else:
                outs.append(res_big[k][name].reshape(w[name].shape))
    return loss[0, 0], dx, outs


def kernel(x, p, ffn1_norm, ffn1_w_gate, ffn1_w_up, ffn1_w_down, mix_norm, w_in, gm_ln_g, gm_ln_b, gm_w_s, gm_b_s, gm_out_norm, conv_w, conv_b, dt_bias, a_log, d_skip, ssm_norm, w_out, ffn2_norm, ffn2_w_gate, ffn2_w_up, ffn2_w_down, ple_norm, ple_w_gate, ple_b_gate, ple_w_proj, final_norm, loss_target, m_ffn1_norm, m_ffn1_w_gate, m_ffn1_w_up, m_ffn1_w_down, m_mix_norm, m_w_in, m_gm_ln_g, m_gm_ln_b, m_gm_w_s, m_gm_b_s, m_gm_out_norm, m_conv_w, m_conv_b, m_dt_bias, m_a_log, m_d_skip, m_ssm_norm, m_w_out, m_ffn2_norm, m_ffn2_w_gate, m_ffn2_w_up, m_ffn2_w_down, m_ple_norm, m_ple_w_gate, m_ple_b_gate, m_ple_w_proj, m_final_norm, v_ffn1_norm, v_ffn1_w_gate, v_ffn1_w_up, v_ffn1_w_down, v_mix_norm, v_w_in, v_gm_ln_g, v_gm_ln_b, v_gm_w_s, v_gm_b_s, v_gm_out_norm, v_conv_w, v_conv_b, v_dt_bias, v_a_log, v_d_skip, v_ssm_norm, v_w_out, v_ffn2_norm, v_ffn2_w_gate, v_ffn2_w_up, v_ffn2_w_down, v_ple_norm, v_ple_w_gate, v_ple_b_gate, v_ple_w_proj, v_final_norm):
    args = locals()
    w = {name: args[name] for name in WEIGHTS}
    m = {name: args["m_" + name] for name in WEIGHTS}
    v = {name: args["v_" + name] for name in WEIGHTS}
    loss, dx, outs = _step(x[0], p[0, 0], loss_target[0], w, m, v)
    loss = lax.psum(loss, AXES)
    return (loss, dx[None], *outs)
```

```python
import functools
from typing import NamedTuple

import jax
import jax.numpy as jnp
from jax import lax
from jax.experimental import pallas as pl
from jax.experimental.pallas import tpu as pltpu

F32 = jnp.float32
BF16 = jnp.bfloat16
HIGHEST = lax.Precision.HIGHEST
MESH = pl.DeviceIdType.MESH
AXES = ("x", "y", "c")
N_DEV = 8

D_MODEL = 1024
D_FF = 2816
D_PLE = 256
GM_WIDTH = 1024
GM_HEADS = 8
GM_HEAD_DIM = 128
CHUNK = 128
SSM_WIDTH = 1024
SSM_HEADS = 16
SSM_HEAD_DIM = 64
SSM_GROUPS = 2
SSM_STATE = 128
SSM_CONV = 4
CONV_DIM = SSM_WIDTH + 2 * SSM_GROUPS * SSM_STATE
IN_PROJ = 2 * GM_WIDTH + SSM_WIDTH + CONV_DIM + SSM_HEADS
LANES = 128
IN_PROJ_PAD = IN_PROJ - SSM_HEADS + LANES
UV_W = 2 * GM_WIDTH
ZXD_W = IN_PROJ_PAD - UV_W
HALO = 8
EPS = 1e-6

ADAM_LR = 0.001
ADAM_B1 = 0.9
ADAM_B2 = 0.999
ADAM_EPS = 1e-08
ADAM_WD = 0.01
ADAM_STEP = 10

VMEM_LIMIT = 56 * 1024 * 1024
PACK_COLS = 1024


def _rms(x, g):
    return x * lax.rsqrt(jnp.mean(x * x, axis=-1, keepdims=True) + EPS) * g


def _gelu(x):
    return 0.5 * x * (1.0 + lax.erf(x * (2.0 ** -0.5)))


def _silu(x):
    return x * jax.nn.sigmoid(x)


def _dot(a, b):
    return jnp.dot(a.astype(BF16), b.astype(BF16), preferred_element_type=F32)


def _dot_nt(a, b):
    return lax.dot_general(a.astype(BF16), b.astype(BF16), (((1,), (1,)), ((), ())), preferred_element_type=F32)


def _dot_tn(a, b):
    return lax.dot_general(a.astype(BF16), b.astype(BF16), (((0,), (0,)), ((), ())), preferred_element_type=F32)


def _hdot(a, b):
    return jnp.dot(a, b, precision=HIGHEST, preferred_element_type=F32)


def _hdot_tn(a, b):
    return lax.dot_general(a, b, (((0,), (0,)), ((), ())), precision=HIGHEST, preferred_element_type=F32)


class _Pieces(NamedTuple):
    gathered: jax.Array
    row_off: int
    rows: int


def _tiled(body, name, n_steps, tiled_in, full_in, big_in, tiled_out, acc_out, scratch=(), reverse=False):
    n_t, n_f, n_b, n_to, n_a = len(tiled_in), len(full_in), len(big_in), len(tiled_out), len(acc_out)

    def row(i):
        return n_steps - 1 - i if reverse else i

    in_specs, args = [], []
    for arr, br, bc, cb in tiled_in:
        if callable(cb):
            in_specs.append(pl.BlockSpec((br, bc), cb))
        else:
            in_specs.append(pl.BlockSpec((br, bc), functools.partial(lambda i, cb: (row(i), cb), cb=cb)))
        args.append(arr)
    for arr in full_in:
        in_specs.append(pl.BlockSpec(arr.shape, functools.partial(lambda i, nd: (0,) * nd, nd=arr.ndim)))
        args.append(arr)
    big_shapes, n_copies = [], 0
    for big in big_in:
        in_specs.append(pl.BlockSpec(memory_space=pl.ANY))
        if isinstance(big, _Pieces):
            args.append(big.gathered)
            big_shapes.append(((N_DEV * big.rows, PACK_COLS), big.gathered.dtype))
            n_copies += N_DEV
        else:
            args.append(big)
            big_shapes.append((big.shape, big.dtype))
            n_copies += 1
    out_specs, out_shape = [], []
    for rows, cols, dt, br in tiled_out:
        out_specs.append(pl.BlockSpec((br, cols), lambda i: (row(i), 0)))
        out_shape.append(jax.ShapeDtypeStruct((rows, cols), dt))
    for shp, dt in acc_out:
        out_specs.append(pl.BlockSpec(shp, functools.partial(lambda i, nd: (0,) * nd, nd=len(shp))))
        out_shape.append(jax.ShapeDtypeStruct(shp, dt))
    scratch_shapes = [pltpu.VMEM(shp, dt) for shp, dt in big_shapes] + list(scratch)
    if n_copies:
        scratch_shapes.append(pltpu.SemaphoreType.DMA((n_copies,)))

    def kern(*refs):
        ins = refs[: n_t + n_f]
        big_hbm = refs[n_t + n_f : n_t + n_f + n_b]
        outs = refs[n_t + n_f + n_b : n_t + n_f + n_b + n_to + n_a]
        rest = refs[n_t + n_f + n_b + n_to + n_a :]
        big_vmem, scr = rest[:n_b], rest[n_b:]
        if n_copies:
            scr, copy_sems = scr[:-1], scr[-1]
        step = pl.program_id(0)

        @pl.when(step == 0)
        def _():
            copies = []
            for big, src, dst in zip(big_in, big_hbm, big_vmem):
                if isinstance(big, _Pieces):
                    for j in range(N_DEV):
                        copies.append((src.at[j, pl.ds(big.row_off, big.rows), :], dst.at[pl.ds(j * big.rows, big.rows), :]))
                else:
                    copies.append((src, dst))
            copies = [pltpu.make_async_copy(a, b, copy_sems.at[k]) for k, (a, b) in enumerate(copies)]
            for cp in copies:
                cp.start()
            for cp in copies:
                cp.wait()
            for acc in outs[n_to:]:
                acc[...] = jnp.zeros(acc.shape, acc.dtype)

        body(row(step), *ins, *big_vmem, *outs, *scr)

    res = pl.pallas_call(
        kern,
        out_shape=out_shape,
        grid=(n_steps,),
        in_specs=in_specs,
        out_specs=out_specs,
        scratch_shapes=scratch_shapes,
        name=name,
        compiler_params=pltpu.CompilerParams(dimension_semantics=("arbitrary",), vmem_limit_bytes=VMEM_LIMIT),
    )(*args)
    return res


FF_CHUNKS = ((0, 1536), (1536, D_FF))
FFN_TM = 256


def _ffn_fwd(h, g, wg_t, wu_t, wd, name):
    T = h.shape[0]

    def body(i, h_ref, g_ref, wg_ref, wu_ref, wd_ref, o_ref):
        x = h_ref[...]
        n = _rms(x, g_ref[...]).astype(BF16)
        f = jnp.zeros(x.shape, F32)
        for lo, hi in FF_CHUNKS:
            a = _dot_nt(n, wg_ref[lo:hi, :])
            b = _dot_nt(n, wu_ref[lo:hi, :])
            s = (_silu(a) * b).astype(BF16)
            f = f + jnp.dot(s, wd_ref[lo:hi, :], preferred_element_type=F32)
        o_ref[...] = x + 0.5 * f

    return _tiled(body, name, T // FFN_TM, [(h, FFN_TM, D_MODEL, 0)], [g], [wg_t, wu_t, wd],
                  [(T, D_MODEL, F32, FFN_TM)], [])[0]


def _ffn_dgrad(h, dout, g, wg_t, wu_t, wd, name):
    T = h.shape[0]

    def body(i, h_ref, do_ref, g_ref, wg_ref, wu_ref, wd_ref, dh_ref, n_ref, s_ref, da_ref, db_ref, dg_ref):
        x = h_ref[...]
        dout = do_ref[...]
        nf, rms_vjp = jax.vjp(_rms, x, g_ref[...])
        n = nf.astype(BF16)
        dfo = (0.5 * dout).astype(BF16)
        dn = jnp.zeros(x.shape, F32)
        for lo, hi in FF_CHUNKS:
            a = _dot_nt(n, wg_ref[lo:hi, :])
            b = _dot_nt(n, wu_ref[lo:hi, :])
            sg = jax.nn.sigmoid(a)
            sl = a * sg
            ds = _dot_nt(dfo, wd_ref[lo:hi, :])
            db = (ds * sl).astype(BF16)
            da = (ds * b * (sg * (1.0 + a * (1.0 - sg)))).astype(BF16)
            dn = dn + _dot(da, wg_ref[lo:hi, :]) + _dot(db, wu_ref[lo:hi, :])
            s_ref[:, lo:hi] = (sl * b).astype(BF16)
            da_ref[:, lo:hi] = da
            db_ref[:, lo:hi] = db
        dx, dg = rms_vjp(dn)
        dh_ref[...] = dout + dx
        n_ref[...] = n
        dg_ref[...] += dg

    return _tiled(body, name, T // FFN_TM, [(h, FFN_TM, D_MODEL, 0), (dout, FFN_TM, D_MODEL, 0)], [g], [wg_t, wu_t, wd],
                  [(T, D_MODEL, F32, FFN_TM), (T, D_MODEL, BF16, FFN_TM), (T, D_FF, BF16, FFN_TM),
                   (T, D_FF, BF16, FFN_TM), (T, D_FF, BF16, FFN_TM)], [((1, D_MODEL), F32)])


def _wgrad(a, b, bn, name, scale=None, transpose_out=False, bk=512):
    T, M = a.shape
    N = b.shape[1]
    bk = min(bk, T)
    assert M % LANES == 0 and N % bn == 0 and T % bk == 0
    n_k = T // bk

    def kern(a_ref, b_ref, o_ref, acc_ref):
        @pl.when(pl.program_id(1) == 0)
        def _():
            acc_ref[...] = jnp.zeros(acc_ref.shape, F32)

        bv = b_ref[...]
        if scale is not None:
            bv = bv * scale
        acc_ref[...] += _dot_tn(a_ref[...], bv)

        @pl.when(pl.program_id(1) == n_k - 1)
        def _():
            acc = acc_ref[...]
            o_ref[...] = (acc.T if transpose_out else acc).astype(BF16)

    if transpose_out:
        out_shape, out_spec = (N, M), pl.BlockSpec((bn, M), lambda j, k: (j, 0))
    else:
        out_shape, out_spec = (M, N), pl.BlockSpec((M, bn), lambda j, k: (0, j))
    return pl.pallas_call(
        kern,
        out_shape=jax.ShapeDtypeStruct(out_shape, BF16),
        grid=(N // bn, n_k),
        in_specs=[pl.BlockSpec((bk, M), lambda j, k: (k, 0)), pl.BlockSpec((bk, bn), lambda j, k: (k, j))],
        out_specs=out_spec,
        scratch_shapes=[pltpu.VMEM((M, bn), F32)],
        name=name,
        compiler_params=pltpu.CompilerParams(dimension_semantics=("arbitrary", "arbitrary"), vmem_limit_bytes=VMEM_LIMIT),
    )(a, b)


PROJ_TM = 256


def _mix_in_fwd(h, g, w_in_t):
    T = h.shape[0]

    def body(i, h_ref, g_ref, w_ref, p_ref, n_ref):
        n = _rms(h_ref[...], g_ref[...]).astype(BF16)
        n_ref[...] = n
        p_ref[...] = _dot_nt(n, w_ref[...])

    return _tiled(body, "mix_in_fwd", T // PROJ_TM, [(h, PROJ_TM, D_MODEL, 0)], [g], [w_in_t],
                  [(T, IN_PROJ_PAD, F32, PROJ_TM), (T, D_MODEL, BF16, PROJ_TM)], [])


def _mix_in_dgrad(h, dh_in, dp_uv, dp_zxd, g, w_in_t):
    T = h.shape[0]

    def body(i, h_ref, dh_ref, duv_ref, dzxd_ref, g_ref, w_ref, o_ref, dg_ref):
        dn = _dot(duv_ref[...], w_ref[:UV_W, :]) + _dot(dzxd_ref[...], w_ref[UV_W:, :])
        _, rms_vjp = jax.vjp(_rms, h_ref[...], g_ref[...])
        dx, dg = rms_vjp(dn)
        o_ref[...] = dh_ref[...] + dx
        dg_ref[...] += dg

    return _tiled(body, "mix_in_dgrad", T // PROJ_TM,
                  [(h, PROJ_TM, D_MODEL, 0), (dh_in, PROJ_TM, D_MODEL, 0), (dp_uv, PROJ_TM, UV_W, 0),
                   (dp_zxd, PROJ_TM, ZXD_W, 0)], [g], [w_in_t],
                  [(T, D_MODEL, F32, PROJ_TM)], [((1, D_MODEL), F32)])


def _out_proj_fwd(h, ya, yb, w_out):
    T = h.shape[0]

    def body(i, h_ref, ya_ref, yb_ref, w_ref, o_ref):
        o_ref[...] = (h_ref[...] + jnp.dot(ya_ref[...], w_ref[:GM_WIDTH, :], preferred_element_type=F32)
                      + jnp.dot(yb_ref[...], w_ref[GM_WIDTH:, :], preferred_element_type=F32))

    return _tiled(body, "out_proj_fwd", T // PROJ_TM,
                  [(h, PROJ_TM, D_MODEL, 0), (ya, PROJ_TM, GM_WIDTH, 0), (yb, PROJ_TM, SSM_WIDTH, 0)], [], [w_out],
                  [(T, D_MODEL, F32, PROJ_TM)], [])[0]


def _out_proj_dgrad(dh, w_out):
    T = dh.shape[0]

    def body(i, dh_ref, w_ref, dya_ref, dyb_ref):
        d = dh_ref[...].astype(BF16)
        dya_ref[...] = _dot_nt(d, w_ref[:GM_WIDTH, :])
        dyb_ref[...] = _dot_nt(d, w_ref[GM_WIDTH:, :])

    return _tiled(body, "out_proj_dgrad", T // PROJ_TM, [(dh, PROJ_TM, D_MODEL, 0)], [], [w_out],
                  [(T, GM_WIDTH, F32, PROJ_TM), (T, SSM_WIDTH, F32, PROJ_TM)], [])


def _gm_chunk(u, v, ln_g, ln_b, b_st, out_g, *w_heads):
    ug = _gelu(u)
    vg = _gelu(v)
    mu = jnp.mean(vg, axis=-1, keepdims=True)
    xc = vg - mu
    vn = xc * lax.rsqrt(jnp.mean(xc * xc, axis=-1, keepdims=True) + EPS) * ln_g + ln_b
    t_idx = lax.broadcasted_iota(jnp.int32, (CHUNK, CHUNK), 0)
    s_idx = lax.broadcasted_iota(jnp.int32, (CHUNK, CHUNK), 1)
    causal = t_idx >= s_idx
    mixed = []
    for hd in range(GM_HEADS):
        wm = jnp.where(causal, w_heads[hd], 0.0)
        cols = slice(hd * GM_HEAD_DIM, (hd + 1) * GM_HEAD_DIM)
        mixed.append(_dot(wm, vn[:, cols]) + b_st[:, hd:hd + 1])
    ya0 = ug * jnp.concatenate(mixed, axis=1)
    return _rms(ya0, out_g)


def _gm_fwd(proj, ln_g, ln_b, w_s, b_st, out_g):
    T = proj.shape[0]

    def body(i, u_ref, v_ref, lg_ref, lb_ref, w_ref, bs_ref, og_ref, ya_ref):
        w_heads = [w_ref[hd] for hd in range(GM_HEADS)]
        ya = _gm_chunk(u_ref[...], v_ref[...], lg_ref[...], lb_ref[...], bs_ref[...], og_ref[...], *w_heads)
        ya_ref[...] = ya.astype(BF16)

    return _tiled(body, "gmlp_fwd", T // CHUNK, [(proj, CHUNK, GM_WIDTH, 0), (proj, CHUNK, GM_WIDTH, 1)],
                  [ln_g, ln_b, w_s, b_st, out_g], [], [(T, GM_WIDTH, BF16, CHUNK)], [])[0]


def _gm_bwd(proj, dya, ln_g, ln_b, w_s, b_st, out_g):
    T = proj.shape[0]

    def body(i, u_ref, v_ref, dy_ref, lg_ref, lb_ref, w_ref, bs_ref, og_ref, duv_ref, dlg_ref, dlb_ref, dw_ref, dbs_ref,
             dog_ref):
        w_heads = [w_ref[hd] for hd in range(GM_HEADS)]
        _, vjp = jax.vjp(_gm_chunk, u_ref[...], v_ref[...], lg_ref[...], lb_ref[...], bs_ref[...], og_ref[...], *w_heads)
        grads = vjp(dy_ref[...])
        duv_ref[:, :GM_WIDTH] = grads[0].astype(BF16)
        duv_ref[:, GM_WIDTH:] = grads[1].astype(BF16)
        dlg_ref[...] += grads[2]
        dlb_ref[...] += grads[3]
        dbs_ref[...] += grads[4]
        dog_ref[...] += grads[5]
        for hd in range(GM_HEADS):
            dw_ref[hd] += grads[6 + hd]

    return _tiled(body, "gmlp_bwd", T // CHUNK,
                  [(proj, CHUNK, GM_WIDTH, 0), (proj, CHUNK, GM_WIDTH, 1), (dya, CHUNK, GM_WIDTH, 0)],
                  [ln_g, ln_b, w_s, b_st, out_g], [], [(T, UV_W, BF16, CHUNK)],
                  [((1, GM_WIDTH), F32), ((1, GM_WIDTH), F32), ((GM_HEADS, CHUNK, CHUNK), F32),
                   ((CHUNK, GM_HEADS), F32), ((1, GM_WIDTH), F32)])


def _ssd_chunk(xc, z, dtr, s_in, dt_bias, a_log, d_skip, norm_g):
    half = SSM_WIDTH // SSM_GROUPS
    l_idx = lax.broadcasted_iota(jnp.int32, (CHUNK, CHUNK), 0)
    s_idx = lax.broadcasted_iota(jnp.int32, (CHUNK, CHUNK), 1)
    causal = l_idx >= s_idx
    tril = causal.astype(F32)
    head_of_col = lax.broadcasted_iota(jnp.int32, (SSM_HEADS, SSM_WIDTH), 1) // SSM_HEAD_DIM
    expand = (head_of_col == lax.broadcasted_iota(jnp.int32, (SSM_HEADS, SSM_WIDTH), 0)).astype(F32)

    xcs = _silu(xc)
    xs = xcs[:, :SSM_WIDTH]
    dt = jax.nn.softplus(dtr + dt_bias)
    adt = dt * (-jnp.exp(a_log))
    acs = _hdot(tril, adt)
    acs_t = _hdot_tn(adt, 1.0 - tril + (l_idx == s_idx).astype(F32))
    tot = acs[CHUNK - 1:CHUNK, :]
    dt_w = _hdot(dt, expand)
    out_decay_w = _hdot(jnp.exp(acs), expand)
    state_decay_w = _hdot(jnp.exp(tot - acs), expand)
    chunk_decay_w = _hdot(jnp.exp(tot), expand)
    d_skip_w = _hdot(d_skip, expand)
    xdt = xs * dt_w
    xdt_decayed = xdt * state_decay_w

    y_diag, y_off, states = [], [], []
    for grp in range(SSM_GROUPS):
        b0 = SSM_WIDTH + grp * SSM_STATE
        c0 = SSM_WIDTH + SSM_GROUPS * SSM_STATE + grp * SSM_STATE
        bm = xcs[:, b0:b0 + SSM_STATE].astype(BF16)
        cm = xcs[:, c0:c0 + SSM_STATE].astype(BF16)
        cb = _dot_nt(cm, bm)
        for k in range(grp * SSM_HEADS // SSM_GROUPS, (grp + 1) * SSM_HEADS // SSM_GROUPS):
            decay = jnp.exp(jnp.where(causal, acs[:, k:k + 1] - acs_t[k:k + 1, :], -jnp.inf))
            y_diag.append(_dot(cb * decay, xdt[:, k * SSM_HEAD_DIM:(k + 1) * SSM_HEAD_DIM]))
        cols = slice(grp * half, (grp + 1) * half)
        states.append(_dot_tn(bm, xdt_decayed[:, cols]))
        y_off.append(_dot(cm, s_in[:, cols]))
    y = jnp.concatenate(y_diag, axis=1) + jnp.concatenate(y_off, axis=1) * out_decay_w + xs * d_skip_w
    s_out = s_in * chunk_decay_w + jnp.concatenate(states, axis=1)
    y = y * _silu(z)
    y3 = y.reshape(CHUNK, SSM_GROUPS, half)
    y3 = y3 * lax.rsqrt(jnp.mean(y3 * y3, axis=-1, keepdims=True) + EPS)
    return y3.reshape(CHUNK, SSM_WIDTH) * norm_g, s_out


def _conv_taps(ext_ref, w, b):
    y = b
    for k in range(SSM_CONV):
        y = y + w[k:k + 1, :] * ext_ref[pl.ds(HALO - (SSM_CONV - 1) + k, CHUNK), :]
    return y


def _ssd_fwd(proj, conv_w, conv_b, dt_bias, a_log, d_skip, norm_g):
    T = proj.shape[0]
    n_chunks = T // CHUNK

    def body(i, z_ref, x_ref, dt_ref, cw_ref, cb_ref, dtb_ref, al_ref, dsk_ref, ng_ref, yb_ref, sin_ref, ext_ref, st_ref):
        @pl.when(i == 0)
        def _():
            ext_ref[0:HALO, :] = jnp.zeros((HALO, CONV_DIM), F32)
            st_ref[...] = jnp.zeros(st_ref.shape, F32)

        ext_ref[HALO:, :] = x_ref[...]
        xc = _conv_taps(ext_ref, cw_ref[...], cb_ref[...])
        s_in = st_ref[...]
        yb, s_out = _ssd_chunk(xc, z_ref[...], dt_ref[:, 0:SSM_HEADS], s_in, dtb_ref[...], al_ref[...], dsk_ref[...],
                               ng_ref[...])
        yb_ref[...] = yb.astype(BF16)
        sin_ref[...] = s_in
        st_ref[...] = s_out
        ext_ref[0:HALO, :] = ext_ref[CHUNK:CHUNK + HALO, :]

    z_blk = 2 * GM_WIDTH // SSM_WIDTH
    x_blk = (2 * GM_WIDTH + SSM_WIDTH) // CONV_DIM
    dt_blk = (2 * GM_WIDTH + SSM_WIDTH + CONV_DIM) // LANES
    return _tiled(body, "ssd_fwd", n_chunks,
                  [(proj, CHUNK, SSM_WIDTH, z_blk), (proj, CHUNK, CONV_DIM, x_blk), (proj, CHUNK, LANES, dt_blk)],
                  [conv_w, conv_b, dt_bias, a_log, d_skip, norm_g], [],
                  [(T, SSM_WIDTH, BF16, CHUNK), (n_chunks * SSM_STATE, SSM_WIDTH, F32, SSM_STATE)], [],
                  scratch=[pltpu.VMEM((HALO + CHUNK, CONV_DIM), F32), pltpu.VMEM((SSM_STATE, SSM_WIDTH), F32)])


def _ssd_bwd(proj, dyb, s_all, conv_w, conv_b, dt_bias, a_log, d_skip, norm_g):
    T = proj.shape[0]
    n_chunks = T // CHUNK
    z_blk = 2 * GM_WIDTH // SSM_WIDTH
    x_blk = (2 * GM_WIDTH + SSM_WIDTH) // CONV_DIM
    dt_blk = (2 * GM_WIDTH + SSM_WIDTH + CONV_DIM) // LANES
    rows_per_halo = CHUNK // HALO

    def body(i, z_ref, x_ref, halo_ref, dt_ref, dy_ref, sin_ref, cw_ref, cb_ref, dtb_ref, al_ref, dsk_ref, ng_ref,
             dzxd_ref, dcw_ref, dcb_ref, ddtb_ref, dal_ref, ddsk_ref, dng_ref, ext_ref, dext_ref, dst_ref):
        @pl.when(i == n_chunks - 1)
        def _():
            dext_ref[CHUNK:, :] = jnp.zeros((HALO, CONV_DIM), F32)
            dst_ref[...] = jnp.zeros(dst_ref.shape, F32)

        halo = halo_ref[...]
        ext_ref[0:HALO, :] = jnp.where(i == 0, jnp.zeros_like(halo), halo)
        ext_ref[HALO:, :] = x_ref[...]
        cw = cw_ref[...]
        xc = _conv_taps(ext_ref, cw, cb_ref[...])
        _, vjp = jax.vjp(_ssd_chunk, xc, z_ref[...], dt_ref[:, 0:SSM_HEADS], sin_ref[...], dtb_ref[...], al_ref[...],
                         dsk_ref[...], ng_ref[...])
        dxc, dz, ddtr, ds_in, ddtb, dal, ddsk, dng = vjp((dy_ref[...], dst_ref[...]))
        dst_ref[...] = ds_in
        ddtb_ref[...] += ddtb
        dal_ref[...] += dal
        ddsk_ref[...] += ddsk
        dng_ref[...] += dng
        dext_ref[0:CHUNK, :] = dxc
        dx = jnp.zeros((CHUNK, CONV_DIM), F32)
        for k in range(SSM_CONV):
            dx = dx + cw[k:k + 1, :] * dext_ref[pl.ds(SSM_CONV - 1 - k, CHUNK), :]
            dcw_ref[k:k + 1, :] += jnp.sum(dxc * ext_ref[pl.ds(HALO - (SSM_CONV - 1) + k, CHUNK), :], axis=0, keepdims=True)
        dcb_ref[...] += jnp.sum(dxc, axis=0, keepdims=True)
        dext_ref[CHUNK:, :] = dext_ref[0:HALO, :]
        dzxd_ref[:, 0:SSM_WIDTH] = dz.astype(BF16)
        dzxd_ref[:, SSM_WIDTH:SSM_WIDTH + CONV_DIM] = dx.astype(BF16)
        dzxd_ref[:, SSM_WIDTH + CONV_DIM:] = jnp.concatenate(
            [ddtr, jnp.zeros((CHUNK, LANES - SSM_HEADS), F32)], axis=1).astype(BF16)

    def halo_index(step):
        c = n_chunks - 1 - step
        return (jnp.maximum(c * rows_per_halo - 1, 0), x_blk)

    return _tiled(body, "ssd_bwd", n_chunks,
                  [(proj, CHUNK, SSM_WIDTH, z_blk), (proj, CHUNK, CONV_DIM, x_blk), (proj, HALO, CONV_DIM, halo_index),
                   (proj, CHUNK, LANES, dt_blk), (dyb, CHUNK, SSM_WIDTH, 0), (s_all, SSM_STATE, SSM_WIDTH, 0)],
                  [conv_w, conv_b, dt_bias, a_log, d_skip, norm_g], [],
                  [(T, ZXD_W, BF16, CHUNK)],
                  [((SSM_CONV, CONV_DIM), F32), ((1, CONV_DIM), F32), ((1, SSM_HEADS), F32), ((1, SSM_HEADS), F32),
                   ((1, SSM_HEADS), F32), ((1, SSM_WIDTH), F32)],
                  scratch=[pltpu.VMEM((HALO + CHUNK, CONV_DIM), F32), pltpu.VMEM((CHUNK + HALO, CONV_DIM), F32),
                           pltpu.VMEM((SSM_STATE, SSM_WIDTH), F32)],
                  reverse=True)


TAIL_TM = 256


def _tail(h, p, target, ple_norm, w_gate, b_gate, w_proj_t, final_norm):
    T = h.shape[0]

    def head(x, pre, pp, b_g, f_norm, tgt):
        gate = jax.nn.sigmoid(pre + b_g)
        out = _rms(x + gate * pp, f_norm)
        err = out - tgt
        return 0.5 * jnp.sum(jnp.mean(err * err, axis=-1, keepdims=True), axis=0, keepdims=True)

    def body(i, h_ref, p_ref, t_ref, pn_ref, bg_ref, fn_ref, wg_ref, wp_ref, dh_ref, loss_ref, dwg_ref, dwp_ref, dpn_ref,
             dbg_ref, dfn_ref):
        x = h_ref[...]
        n4f, n_vjp = jax.vjp(_rms, x, pn_ref[...])
        n4 = n4f.astype(BF16)
        pre = jnp.dot(n4, wg_ref[...], preferred_element_type=F32)
        p16 = p_ref[...].astype(BF16)
        pp = _dot_nt(p16, wp_ref[...])
        loss, h_vjp = jax.vjp(functools.partial(head, tgt=t_ref[...]), x, pre, pp, bg_ref[...], fn_ref[...])
        dx, dpre, dpp, dbg, dfn = h_vjp(jnp.ones((1, 1), F32))
        dpre16 = dpre.astype(BF16)
        dn4 = _dot_nt(dpre16, wg_ref[...])
        dx2, dpn = n_vjp(dn4)
        dh_ref[...] = dx + dx2
        loss_ref[...] += loss
        dwg_ref[...] += _dot_tn(n4, dpre16)
        dwp_ref[...] += _dot_tn(p16, dpp)
        dpn_ref[...] += dpn
        dbg_ref[...] += dbg
        dfn_ref[...] += dfn

    return _tiled(body, "tail", T // TAIL_TM,
                  [(h, TAIL_TM, D_MODEL, 0), (p, TAIL_TM, D_PLE, 0), (target, TAIL_TM, D_MODEL, 0)],
                  [ple_norm, b_gate, final_norm], [w_gate, w_proj_t],
                  [(T, D_MODEL, F32, TAIL_TM)],
                  [((1, 1), F32), ((D_MODEL, D_MODEL), F32), ((D_PLE, D_MODEL), F32), ((1, D_MODEL), F32),
                   ((1, D_MODEL), F32), ((1, D_MODEL), F32)])


def _all_gather(x, name):
    R, C = x.shape

    def body(x_ref, out_ref, send_sems, recv_sems, local_sem):
        mx, my, mc = lax.axis_index("x"), lax.axis_index("y"), lax.axis_index("c")
        me, sibling = (mx, my, mc), (mx, my, 1 - mc)
        chips = [(1 - mx, my), (mx, 1 - my), (1 - mx, 1 - my)]

        def rows(px, py, pc):
            return out_ref.at[4 * px + 2 * py + pc]

        def copy(k, block, to, src=None):
            return pltpu.make_async_remote_copy(
                src_ref=rows(*block) if src is None else src, dst_ref=rows(*block),
                send_sem=send_sems.at[k], recv_sem=recv_sems.at[k], device_id=to, device_id_type=MESH)

        mine = pltpu.make_async_copy(x_ref, rows(*me), local_sem)
        mine.start()
        first = [copy(0, me, sibling, src=x_ref)]
        first += [copy(1 + j, me, (*chip, mc), src=x_ref) for j, chip in enumerate(chips)]
        for cp in first:
            cp.start()
        passed = [copy(4 + j, (*chip, mc), sibling) for j, chip in enumerate(chips)]
        for j, chip in enumerate(chips):
            copy(1 + j, (*chip, mc), me).wait_recv()
            passed[j].start()
        copy(0, sibling, me).wait_recv()
        for j, chip in enumerate(chips):
            copy(4 + j, (*chip, 1 - mc), me).wait_recv()
        for cp in first + passed:
            cp.wait_send()
        mine.wait()

    return pl.pallas_call(
        body,
        out_shape=jax.ShapeDtypeStruct((N_DEV, R, C), x.dtype),
        in_specs=[pl.BlockSpec(memory_space=pl.ANY)],
        out_specs=pl.BlockSpec(memory_space=pl.ANY),
        scratch_shapes=[pltpu.SemaphoreType.DMA((7,)), pltpu.SemaphoreType.DMA((7,)), pltpu.SemaphoreType.DMA],
        name=name,
    )(x)


def _exchange(x, name):
    _, R, C = x.shape

    def body(x_ref, out_ref, send_sems, recv_sems, local_sem):
        mx, my, mc = lax.axis_index("x"), lax.axis_index("y"), lax.axis_index("c")
        me = 4 * mx + 2 * my + mc
        mine = pltpu.make_async_copy(x_ref.at[me], out_ref.at[me], local_sem)
        mine.start()
        copies = []
        for k in range(1, N_DEV):
            px = 1 - mx if k & 4 else mx
            py = 1 - my if k & 2 else my
            pc = 1 - mc if k & 1 else mc
            peer = 4 * px + 2 * py + pc
            copies.append(pltpu.make_async_remote_copy(
                src_ref=x_ref.at[peer], dst_ref=out_ref.at[me], send_sem=send_sems.at[k - 1],
                recv_sem=recv_sems.at[k - 1], device_id=(px, py, pc), device_id_type=MESH))
        for cp in copies:
            cp.start()
        for cp in copies:
            cp.wait_recv()
        for cp in copies:
            cp.wait_send()
        mine.wait()

    return pl.pallas_call(
        body,
        out_shape=jax.ShapeDtypeStruct(x.shape, x.dtype),
        in_specs=[pl.BlockSpec(memory_space=pl.ANY)],
        out_specs=pl.BlockSpec(memory_space=pl.ANY),
        scratch_shapes=[pltpu.SemaphoreType.DMA((7,)), pltpu.SemaphoreType.DMA((7,)), pltpu.SemaphoreType.DMA],
        name=name,
    )(x)


def _sum_adamw(parts, w, m, v, tr, name):
    _, R, C = parts.shape

    def kern(p_ref, w_ref, m_ref, v_ref, g_ref, d_ref, nm_ref, nv_ref):
        g = p_ref[0].astype(F32)
        for j in range(1, N_DEV):
            g = g + p_ref[j].astype(F32)
        m_new = ADAM_B1 * m_ref[...] + (1.0 - ADAM_B1) * g
        v_new = ADAM_B2 * v_ref[...] + (1.0 - ADAM_B2) * jnp.square(g)
        m_hat = m_new / (1.0 - ADAM_B1 ** ADAM_STEP)
        v_hat = v_new / (1.0 - ADAM_B2 ** ADAM_STEP)
        g_ref[...] = g
        d_ref[...] = -ADAM_LR * (m_hat / (jnp.sqrt(v_hat) + ADAM_EPS) + ADAM_WD * w_ref[...])
        nm_ref[...] = m_new
        nv_ref[...] = v_new

    row_spec = pl.BlockSpec((tr, C), lambda i: (i, 0))
    return pl.pallas_call(
        kern,
        out_shape=[jax.ShapeDtypeStruct((R, C), F32)] * 4,
        grid=(R // tr,),
        in_specs=[pl.BlockSpec((N_DEV, tr, C), lambda i: (0, i, 0)), row_spec, row_spec, row_spec],
        out_specs=[row_spec] * 4,
        name=name,
        compiler_params=pltpu.CompilerParams(dimension_semantics=("arbitrary",), vmem_limit_bytes=VMEM_LIMIT),
    )(parts, w, m, v)


FF_SHARD = D_FF // N_DEV
PACKED = (("ffn1_w_gate", (D_MODEL, FF_SHARD), True), ("ffn1_w_up", (D_MODEL, FF_SHARD), True),
          ("ffn1_w_down", (FF_SHARD, D_MODEL), False),
          ("ffn2_w_gate", (D_MODEL, FF_SHARD), True), ("ffn2_w_up", (D_MODEL, FF_SHARD), True),
          ("ffn2_w_down", (FF_SHARD, D_MODEL), False),
          ("w_out", (2 * D_MODEL // N_DEV, D_MODEL), False), ("ple_w_gate", (D_MODEL // N_DEV, D_MODEL), False),
          ("w_in", (D_MODEL, IN_PROJ // N_DEV), True), ("ple_w_proj", (D_PLE, D_MODEL // N_DEV), True),
          ("conv_w", (SSM_CONV, CONV_DIM // N_DEV), True))
GATHERED = PACKED + (("conv_w_mid", (SSM_CONV, CONV_DIM // N_DEV), True), ("conv_w_low", (SSM_CONV, CONV_DIM // N_DEV), True))
BIG_TR = 320
BIG_ROWS = 3200
SMALL = ("ffn1_norm", "mix_norm", "gm_ln_g", "gm_ln_b", "gm_w_s", "gm_b_s", "gm_out_norm", "conv_b", "dt_bias", "a_log",
         "d_skip", "ssm_norm", "ffn2_norm", "ple_norm", "ple_b_gate", "final_norm")
SMALL_ROWS = 144


def _piece_rows(shape):
    return -(-(shape[0] * shape[1]) // PACK_COLS)


def _row_offsets(entries):
    offsets, off = {}, 0
    for name, shape, _ in entries:
        offsets[name] = off
        off += _piece_rows(shape)
    assert off <= BIG_ROWS
    return offsets


ROW_OFF = _row_offsets(GATHERED)


def _pad_cols(flat, shape):
    pad = _piece_rows(shape) * PACK_COLS - flat.shape[-1]
    return flat if pad == 0 else jnp.pad(flat, [(0, 0)] * (flat.ndim - 1) + [(0, pad)])


def _pack_local(vals, entries=PACKED):
    parts = []
    for name, shape, transposed in entries:
        val = vals[name]
        parts.append(_pad_cols((val.T if transposed else val).reshape(-1), shape))
    flat = jnp.concatenate(parts)
    return jnp.pad(flat, (0, BIG_ROWS * PACK_COLS - flat.shape[0])).reshape(BIG_ROWS, PACK_COLS)


def _unpack_local(packed):
    out = {}
    for name, shape, transposed in PACKED:
        piece = packed[ROW_OFF[name]:ROW_OFF[name] + _piece_rows(shape)].reshape(-1)[:shape[0] * shape[1]]
        out[name] = piece.reshape(shape[::-1]).T if transposed else piece.reshape(shape)
    return out


def _pack_owner_major(grads):
    parts = [_pad_cols(grads[name].astype(BF16), shape) for name, shape, _ in PACKED]
    flat = jnp.concatenate(parts, axis=1)
    flat = jnp.pad(flat, ((0, 0), (0, BIG_ROWS * PACK_COLS - flat.shape[1])))
    return flat.reshape(N_DEV, BIG_ROWS, PACK_COLS)


def _gathered_piece(gathered, name):
    shape = dict((n, s) for n, s, _ in GATHERED)[name]
    rows = gathered[:, ROW_OFF[name]:ROW_OFF[name] + _piece_rows(shape), :]
    return rows.reshape(N_DEV, -1)[:, :shape[0] * shape[1]]


def _pack_small(vals):
    flat = jnp.concatenate([vals[name].reshape(-1).astype(F32) for name in SMALL])
    return jnp.pad(flat, (0, SMALL_ROWS * PACK_COLS - flat.shape[0])).reshape(SMALL_ROWS, PACK_COLS)


def _unpack_small(packed, shapes):
    out, off = {}, 0
    flat = packed.reshape(-1)
    for name in SMALL:
        n = 1
        for s in shapes[name]:
            n *= s
        out[name] = flat[off:off + n].reshape(shapes[name])
        off += n
    return out


WEIGHTS = ("ffn1_norm", "ffn1_w_gate", "ffn1_w_up", "ffn1_w_down", "mix_norm", "w_in", "gm_ln_g", "gm_ln_b", "gm_w_s",
           "gm_b_s", "gm_out_norm", "conv_w", "conv_b", "dt_bias", "a_log", "d_skip", "ssm_norm", "w_out", "ffn2_norm",
           "ffn2_w_gate", "ffn2_w_up", "ffn2_w_down", "ple_norm", "ple_w_gate", "ple_b_gate", "ple_w_proj", "final_norm")


def _step(x, p, target, w, m, v):
    local = lambda d: {name: d[name][0] for name, _, _ in PACKED}

    shards = {name: val.astype(BF16) for name, val in local(w).items()}
    conv_rest = w["conv_w"][0] - shards["conv_w"].astype(F32)
    shards["conv_w_mid"] = conv_rest.astype(BF16)
    shards["conv_w_low"] = (conv_rest - shards["conv_w_mid"].astype(F32)).astype(BF16)
    gathered = _all_gather(_pack_local(shards, GATHERED), "gather_weights")
    shapes = dict((n, s) for n, s, _ in GATHERED)
    piece = lambda name: _Pieces(gathered, ROW_OFF[name], _piece_rows(shapes[name]))
    w_in_t = _gathered_piece(gathered, "w_in").reshape(IN_PROJ, D_MODEL)
    w_in_t = jnp.concatenate([w_in_t, jnp.zeros((IN_PROJ_PAD - IN_PROJ, D_MODEL), BF16)], axis=0)
    w_proj_t = _gathered_piece(gathered, "ple_w_proj").reshape(D_MODEL, D_PLE)
    conv_w = sum(_gathered_piece(gathered, name).astype(F32) for name in ("conv_w", "conv_w_mid", "conv_w_low"))
    conv_w = conv_w.reshape(CONV_DIM, SSM_CONV).T

    row = lambda name: w[name].reshape(1, -1)
    gm_w_s = w["gm_w_s"][0]
    gm_b_st = jnp.transpose(w["gm_b_s"][0])
    ffn1 = (row("ffn1_norm"), piece("ffn1_w_gate"), piece("ffn1_w_up"), piece("ffn1_w_down"))
    ffn2 = (row("ffn2_norm"), piece("ffn2_w_gate"), piece("ffn2_w_up"), piece("ffn2_w_down"))
    gm = (row("gm_ln_g"), row("gm_ln_b"), gm_w_s, gm_b_st, row("gm_out_norm"))
    ssd = (conv_w, row("conv_b"), row("dt_bias"), row("a_log"), row("d_skip"), row("ssm_norm"))

    h1 = _ffn_fwd(x, *ffn1, "ffn1_fwd")
    proj, n2 = _mix_in_fwd(h1, row("mix_norm"), w_in_t)
    ya = _gm_fwd(proj, *gm)
    yb, s_all = _ssd_fwd(proj, *ssd)
    h2 = _out_proj_fwd(h1, ya, yb, piece("w_out"))
    h3 = _ffn_fwd(h2, *ffn2, "ffn2_fwd")

    g, gp = {}, {}
    dh3, loss, gp["ple_w_gate"], d_w_proj, g["ple_norm"], g["ple_b_gate"], g["final_norm"] = _tail(
        h3, p, target, row("ple_norm"), piece("ple_w_gate"), row("ple_b_gate"), w_proj_t, row("final_norm"))
    gp["ple_w_proj"] = d_w_proj.T

    dh2, n3, s3, da3, db3, g["ffn2_norm"] = _ffn_dgrad(h2, dh3, *ffn2, "ffn2_dgrad")
    gp["ffn2_w_gate"] = _wgrad(n3, da3, 1408, "ffn2_wgrad_gate", transpose_out=True)
    gp["ffn2_w_up"] = _wgrad(n3, db3, 1408, "ffn2_wgrad_up", transpose_out=True)
    gp["ffn2_w_down"] = _wgrad(s3, dh3, 512, "ffn2_wgrad_down", scale=0.5)

    dya, dyb = _out_proj_dgrad(dh2, piece("w_out"))
    gp["w_out"] = jnp.concatenate([_wgrad(ya, dh2, 1024, "w_out_wgrad_a"), _wgrad(yb, dh2, 1024, "w_out_wgrad_b")], axis=0)

    dp_zxd, d_conv_w, g["conv_b"], g["dt_bias"], g["a_log"], g["d_skip"], g["ssm_norm"] = _ssd_bwd(proj, dyb, s_all, *ssd)
    gp["conv_w"] = d_conv_w.T
    dp_uv, g["gm_ln_g"], g["gm_ln_b"], g["gm_w_s"], dbst, g["gm_out_norm"] = _gm_bwd(proj, dya, *gm)
    g["gm_b_s"] = jnp.transpose(dbst)

    dh1, g["mix_norm"] = _mix_in_dgrad(h1, dh2, dp_uv, dp_zxd, row("mix_norm"), w_in_t)
    gp["w_in"] = jnp.concatenate([_wgrad(n2, dp_uv, 1024, "w_in_wgrad_uv", transpose_out=True),
                                  _wgrad(n2, dp_zxd, 896, "w_in_wgrad_zxd", transpose_out=True)], axis=0)[:IN_PROJ]

    dx, n1, s1, da1, db1, g["ffn1_norm"] = _ffn_dgrad(x, dh1, *ffn1, "ffn1_dgrad")
    gp["ffn1_w_gate"] = _wgrad(n1, da1, 1408, "ffn1_wgrad_gate", transpose_out=True)
    gp["ffn1_w_up"] = _wgrad(n1, db1, 1408, "ffn1_wgrad_up", transpose_out=True)
    gp["ffn1_w_down"] = _wgrad(s1, dh1, 512, "ffn1_wgrad_down", scale=0.5)

    parts = _exchange(_pack_owner_major({name: val.reshape(N_DEV, -1) for name, val in gp.items()}), "scatter_grads")
    res_big = _sum_adamw(parts, _pack_local(local(w)), _pack_local(local(m)), _pack_local(local(v)), BIG_TR, "adamw_shards")
    res_big = [_unpack_local(r) for r in res_big]

    small_shapes = {name: w[name].shape for name in SMALL}
    small_parts = _all_gather(_pack_small(g), "gather_small_grads")
    res_small = _sum_adamw(small_parts, _pack_small(w), _pack_small(m), _pack_small(v), SMALL_ROWS, "adamw_small")
    res_small = [_unpack_small(r, small_shapes) for r in res_small]

    outs = []
    for k in range(4):
        for name in WEIGHTS:
            if name in res_small[k]:
                outs.append(res_small[k][name])
            else:
                outs.append(res_big[k][name].reshape(w[name].shape))
    return loss[0, 0], dx, outs


def kernel(x, p, ffn1_norm, ffn1_w_gate, ffn1_w_up, ffn1_w_down, mix_norm, w_in, gm_ln_g, gm_ln_b, gm_w_s, gm_b_s, gm_out_norm, conv_w, conv_b, dt_bias, a_log, d_skip, ssm_norm, w_out, ffn2_norm, ffn2_w_gate, ffn2_w_up, ffn2_w_down, ple_norm, ple_w_gate, ple_b_gate, ple_w_proj, final_norm, loss_target, m_ffn1_norm, m_ffn1_w_gate, m_ffn1_w_up, m_ffn1_w_down, m_mix_norm, m_w_in, m_gm_ln_g, m_gm_ln_b, m_gm_w_s, m_gm_b_s, m_gm_out_norm, m_conv_w, m_conv_b, m_dt_bias, m_a_log, m_d_skip, m_ssm_norm, m_w_out, m_ffn2_norm, m_ffn2_w_gate, m_ffn2_w_up, m_ffn2_w_down, m_ple_norm, m_ple_w_gate, m_ple_b_gate, m_ple_w_proj, m_final_norm, v_ffn1_norm, v_ffn1_w_gate, v_ffn1_w_up, v_ffn1_w_down, v_mix_norm, v_w_in, v_gm_ln_g, v_gm_ln_b, v_gm_w_s, v_gm_b_s, v_gm_out_norm, v_conv_w, v_conv_b, v_dt_bias, v_a_log, v_d_skip, v_ssm_norm, v_w_out, v_ffn2_norm, v_ffn2_w_gate, v_ffn2_w_up, v_ffn2_w_down, v_ple_norm, v_ple_w_gate, v_ple_b_gate, v_ple_w_proj, v_final_norm):
    args = locals()
    w = {name: args[name] for name in WEIGHTS}
    m = {name: args["m_" + name] for name in WEIGHTS}
    v = {name: args["v_" + name] for name in WEIGHTS}
    loss, dx, outs = _step(x[0], p[0, 0], loss_target[0], w, m, v)
    loss = lax.psum(loss, AXES)
    return (loss, dx[None], *outs)
```

```python
import functools
from typing import NamedTuple

import jax
import jax.numpy as jnp
from jax import lax
from jax.experimental import pallas as pl
from jax.experimental.pallas import tpu as pltpu

F32 = jnp.float32
BF16 = jnp.bfloat16
HIGHEST = lax.Precision.HIGHEST
MESH = pl.DeviceIdType.MESH
AXES = ("x", "y", "c")
N_DEV = 8

D_MODEL = 1024
D_FF = 2816
D_PLE = 256
GM_WIDTH = 1024
GM_HEADS = 8
GM_HEAD_DIM = 128
CHUNK = 128
SSM_WIDTH = 1024
SSM_HEADS = 16
SSM_HEAD_DIM = 64
SSM_GROUPS = 2
SSM_STATE = 128
SSM_CONV = 4
CONV_DIM = SSM_WIDTH + 2 * SSM_GROUPS * SSM_STATE
IN_PROJ = 2 * GM_WIDTH + SSM_WIDTH + CONV_DIM + SSM_HEADS
LANES = 128
IN_PROJ_PAD = IN_PROJ - SSM_HEADS + LANES
UV_W = 2 * GM_WIDTH
ZXD_W = IN_PROJ_PAD - UV_W
HALO = 8
EPS = 1e-6

ADAM_LR = 0.001
ADAM_B1 = 0.9
ADAM_B2 = 0.999
ADAM_EPS = 1e-08
ADAM_WD = 0.01
ADAM_STEP = 10

VMEM_LIMIT = 56 * 1024 * 1024
PACK_COLS = 1024


def _rms(x, g):
    return x * lax.rsqrt(jnp.mean(x * x, axis=-1, keepdims=True) + EPS) * g


def _gelu(x):
    return 0.5 * x * (1.0 + lax.erf(x * (2.0 ** -0.5)))


def _silu(x):
    return x * jax.nn.sigmoid(x)


def _dot(a, b):
    return jnp.dot(a.astype(BF16), b.astype(BF16), preferred_element_type=F32)


def _dot_nt(a, b):
    return lax.dot_general(a.astype(BF16), b.astype(BF16), (((1,), (1,)), ((), ())), preferred_element_type=F32)


def _dot_tn(a, b):
    return lax.dot_general(a.astype(BF16), b.astype(BF16), (((0,), (0,)), ((), ())), preferred_element_type=F32)


def _hdot(a, b):
    return jnp.dot(a, b, precision=HIGHEST, preferred_element_type=F32)


def _hdot_tn(a, b):
    return lax.dot_general(a, b, (((0,), (0,)), ((), ())), precision=HIGHEST, preferred_element_type=F32)


class _Pieces(NamedTuple):
    gathered: jax.Array
    row_off: int
    rows: int


class _Comm(NamedTuple):
    phases: object
    src: jax.Array
    dst: jax.ShapeDtypeStruct


def _tiled(body, name, n_steps, tiled_in, full_in, big_in, tiled_out, acc_out, scratch=(), reverse=False, comm=None):
    n_t, n_f, n_b, n_to, n_a = len(tiled_in), len(full_in), len(big_in), len(tiled_out), len(acc_out)
    n_c = 1 if comm else 0

    def row(i):
        return n_steps - 1 - i if reverse else i

    in_specs, args = [], []
    for arr, br, bc, cb in tiled_in:
        if callable(cb):
            in_specs.append(pl.BlockSpec((br, bc), cb))
        else:
            in_specs.append(pl.BlockSpec((br, bc), functools.partial(lambda i, cb: (row(i), cb), cb=cb)))
        args.append(arr)
    for arr in full_in:
        in_specs.append(pl.BlockSpec(arr.shape, functools.partial(lambda i, nd: (0,) * nd, nd=arr.ndim)))
        args.append(arr)
    big_shapes, n_copies = [], 0
    for big in big_in:
        in_specs.append(pl.BlockSpec(memory_space=pl.ANY))
        if isinstance(big, _Pieces):
            args.append(big.gathered)
            big_shapes.append(((N_DEV * big.rows, PACK_COLS), big.gathered.dtype))
            n_copies += N_DEV
        else:
            args.append(big)
            big_shapes.append((big.shape, big.dtype))
            n_copies += 1
    if comm:
        in_specs.append(pl.BlockSpec(memory_space=pl.ANY))
        args.append(comm.src)
    out_specs, out_shape = [], []
    for rows, cols, dt, br in tiled_out:
        out_specs.append(pl.BlockSpec((br, cols), lambda i: (row(i), 0)))
        out_shape.append(jax.ShapeDtypeStruct((rows, cols), dt))
    for shp, dt in acc_out:
        out_specs.append(pl.BlockSpec(shp, functools.partial(lambda i, nd: (0,) * nd, nd=len(shp))))
        out_shape.append(jax.ShapeDtypeStruct(shp, dt))
    if comm:
        out_specs.append(pl.BlockSpec(memory_space=pl.ANY))
        out_shape.append(comm.dst)
    scratch_shapes = [pltpu.VMEM(shp, dt) for shp, dt in big_shapes] + list(scratch)
    if n_copies:
        scratch_shapes.append(pltpu.SemaphoreType.DMA((n_copies,)))
    if comm:
        scratch_shapes += [pltpu.SemaphoreType.DMA((N_DEV - 1,)), pltpu.SemaphoreType.DMA((N_DEV - 1,)), pltpu.SemaphoreType.DMA]

    def kern(*refs):
        n_in = n_t + n_f + n_b + n_c
        ins = refs[: n_t + n_f]
        big_hbm = refs[n_t + n_f : n_t + n_f + n_b]
        outs = refs[n_in : n_in + n_to + n_a]
        rest = refs[n_in + n_to + n_a + n_c :]
        big_vmem, scr = rest[:n_b], rest[n_b:]
        if comm:
            scr, comm_sems = scr[:-3], scr[-3:]
            comm_start, comm_mid, comm_finish = comm.phases(refs[n_in - 1], refs[n_in + n_to + n_a], *comm_sems)
        if n_copies:
            scr, copy_sems = scr[:-1], scr[-1]
        step = pl.program_id(0)

        @pl.when(step == 0)
        def _():
            copies = []
            for big, src, dst in zip(big_in, big_hbm, big_vmem):
                if isinstance(big, _Pieces):
                    for j in range(N_DEV):
                        copies.append((src.at[j, pl.ds(big.row_off, big.rows), :], dst.at[pl.ds(j * big.rows, big.rows), :]))
                else:
                    copies.append((src, dst))
            copies = [pltpu.make_async_copy(a, b, copy_sems.at[k]) for k, (a, b) in enumerate(copies)]
            for cp in copies:
                cp.start()
            for cp in copies:
                cp.wait()
            for acc in outs[n_to:]:
                acc[...] = jnp.zeros(acc.shape, acc.dtype)
            if comm:
                comm_start()

        body(row(step), *ins, *big_vmem, *outs, *scr)
        if comm:
            pl.when(step == (n_steps - 1) // 2)(comm_mid)
            pl.when(step == n_steps - 1)(comm_finish)

    res = pl.pallas_call(
        kern,
        out_shape=out_shape,
        grid=(n_steps,),
        in_specs=in_specs,
        out_specs=out_specs,
        scratch_shapes=scratch_shapes,
        name=name,
        compiler_params=pltpu.CompilerParams(dimension_semantics=("arbitrary",), vmem_limit_bytes=VMEM_LIMIT),
    )(*args)
    return res


FF_CHUNKS = ((0, 1536), (1536, D_FF))
FFN_TM = 256


def _ffn_fwd(h, g, wg_t, wu_t, wd, name, comm=None):
    T = h.shape[0]

    def body(i, h_ref, g_ref, wg_ref, wu_ref, wd_ref, o_ref):
        x = h_ref[...]
        n = _rms(x, g_ref[...]).astype(BF16)
        f = jnp.zeros(x.shape, F32)
        for lo, hi in FF_CHUNKS:
            a = _dot_nt(n, wg_ref[lo:hi, :])
            b = _dot_nt(n, wu_ref[lo:hi, :])
            s = (_silu(a) * b).astype(BF16)
            f = f + jnp.dot(s, wd_ref[lo:hi, :], preferred_element_type=F32)
        o_ref[...] = x + 0.5 * f

    return _tiled(body, name, T // FFN_TM, [(h, FFN_TM, D_MODEL, 0)], [g], [wg_t, wu_t, wd],
                  [(T, D_MODEL, F32, FFN_TM)], [], comm=comm)


def _ffn_dgrad(h, dout, g, wg_t, wu_t, wd, name):
    T = h.shape[0]

    def body(i, h_ref, do_ref, g_ref, wg_ref, wu_ref, wd_ref, dh_ref, n_ref, s_ref, da_ref, db_ref, dg_ref):
        x = h_ref[...]
        dout = do_ref[...]
        nf, rms_vjp = jax.vjp(_rms, x, g_ref[...])
        n = nf.astype(BF16)
        dfo = (0.5 * dout).astype(BF16)
        dn = jnp.zeros(x.shape, F32)
        for lo, hi in FF_CHUNKS:
            a = _dot_nt(n, wg_ref[lo:hi, :])
            b = _dot_nt(n, wu_ref[lo:hi, :])
            sg = jax.nn.sigmoid(a)
            sl = a * sg
            ds = _dot_nt(dfo, wd_ref[lo:hi, :])
            db = (ds * sl).astype(BF16)
            da = (ds * b * (sg * (1.0 + a * (1.0 - sg)))).astype(BF16)
            dn = dn + _dot(da, wg_ref[lo:hi, :]) + _dot(db, wu_ref[lo:hi, :])
            s_ref[:, lo:hi] = (sl * b).astype(BF16)
            da_ref[:, lo:hi] = da
            db_ref[:, lo:hi] = db
        dx, dg = rms_vjp(dn)
        dh_ref[...] = dout + dx
        n_ref[...] = n
        dg_ref[...] += dg

    return _tiled(body, name, T // FFN_TM, [(h, FFN_TM, D_MODEL, 0), (dout, FFN_TM, D_MODEL, 0)], [g], [wg_t, wu_t, wd],
                  [(T, D_MODEL, F32, FFN_TM), (T, D_MODEL, BF16, FFN_TM), (T, D_FF, BF16, FFN_TM),
                   (T, D_FF, BF16, FFN_TM), (T, D_FF, BF16, FFN_TM)], [((1, D_MODEL), F32)])


def _wgrad(a, b, bn, name, scale=None, transpose_out=False, bk=512):
    T, M = a.shape
    N = b.shape[1]
    bk = min(bk, T)
    assert M % LANES == 0 and N % bn == 0 and T % bk == 0
    n_k = T // bk

    def kern(a_ref, b_ref, o_ref, acc_ref):
        @pl.when(pl.program_id(1) == 0)
        def _():
            acc_ref[...] = jnp.zeros(acc_ref.shape, F32)

        bv = b_ref[...]
        if scale is not None:
            bv = bv * scale
        acc_ref[...] += _dot_tn(a_ref[...], bv)

        @pl.when(pl.program_id(1) == n_k - 1)
        def _():
            acc = acc_ref[...]
            o_ref[...] = (acc.T if transpose_out else acc).astype(BF16)

    if transpose_out:
        out_shape, out_spec = (N, M), pl.BlockSpec((bn, M), lambda j, k: (j, 0))
    else:
        out_shape, out_spec = (M, N), pl.BlockSpec((M, bn), lambda j, k: (0, j))
    return pl.pallas_call(
        kern,
        out_shape=jax.ShapeDtypeStruct(out_shape, BF16),
        grid=(N // bn, n_k),
        in_specs=[pl.BlockSpec((bk, M), lambda j, k: (k, 0)), pl.BlockSpec((bk, bn), lambda j, k: (k, j))],
        out_specs=out_spec,
        scratch_shapes=[pltpu.VMEM((M, bn), F32)],
        name=name,
        compiler_params=pltpu.CompilerParams(dimension_semantics=("arbitrary", "arbitrary"), vmem_limit_bytes=VMEM_LIMIT),
    )(a, b)


PROJ_TM = 256


def _mix_in_fwd(h, g, w_in_t):
    T = h.shape[0]

    def body(i, h_ref, g_ref, w_ref, p_ref, n_ref):
        n = _rms(h_ref[...], g_ref[...]).astype(BF16)
        n_ref[...] = n
        p_ref[...] = _dot_nt(n, w_ref[...])

    return _tiled(body, "mix_in_fwd", T // PROJ_TM, [(h, PROJ_TM, D_MODEL, 0)], [g], [w_in_t],
                  [(T, IN_PROJ_PAD, F32, PROJ_TM), (T, D_MODEL, BF16, PROJ_TM)], [])


def _mix_in_dgrad(h, dh_in, dp_uv, dp_zxd, g, w_in_t):
    T = h.shape[0]

    def body(i, h_ref, dh_ref, duv_ref, dzxd_ref, g_ref, w_ref, o_ref, dg_ref):
        dn = _dot(duv_ref[...], w_ref[:UV_W, :]) + _dot(dzxd_ref[...], w_ref[UV_W:, :])
        _, rms_vjp = jax.vjp(_rms, h_ref[...], g_ref[...])
        dx, dg = rms_vjp(dn)
        o_ref[...] = dh_ref[...] + dx
        dg_ref[...] += dg

    return _tiled(body, "mix_in_dgrad", T // PROJ_TM,
                  [(h, PROJ_TM, D_MODEL, 0), (dh_in, PROJ_TM, D_MODEL, 0), (dp_uv, PROJ_TM, UV_W, 0),
                   (dp_zxd, PROJ_TM, ZXD_W, 0)], [g], [w_in_t],
                  [(T, D_MODEL, F32, PROJ_TM)], [((1, D_MODEL), F32)])


def _out_proj_fwd(h, ya, yb, w_out):
    T = h.shape[0]

    def body(i, h_ref, ya_ref, yb_ref, w_ref, o_ref):
        o_ref[...] = (h_ref[...] + jnp.dot(ya_ref[...], w_ref[:GM_WIDTH, :], preferred_element_type=F32)
                      + jnp.dot(yb_ref[...], w_ref[GM_WIDTH:, :], preferred_element_type=F32))

    return _tiled(body, "out_proj_fwd", T // PROJ_TM,
                  [(h, PROJ_TM, D_MODEL, 0), (ya, PROJ_TM, GM_WIDTH, 0), (yb, PROJ_TM, SSM_WIDTH, 0)], [], [w_out],
                  [(T, D_MODEL, F32, PROJ_TM)], [])[0]


def _out_proj_dgrad(dh, w_out):
    T = dh.shape[0]

    def body(i, dh_ref, w_ref, dya_ref, dyb_ref):
        d = dh_ref[...].astype(BF16)
        dya_ref[...] = _dot_nt(d, w_ref[:GM_WIDTH, :])
        dyb_ref[...] = _dot_nt(d, w_ref[GM_WIDTH:, :])

    return _tiled(body, "out_proj_dgrad", T // PROJ_TM, [(dh, PROJ_TM, D_MODEL, 0)], [], [w_out],
                  [(T, GM_WIDTH, F32, PROJ_TM), (T, SSM_WIDTH, F32, PROJ_TM)], [])


def _gm_chunk(u, v, ln_g, ln_b, b_st, out_g, *w_heads):
    ug = _gelu(u)
    vg = _gelu(v)
    mu = jnp.mean(vg, axis=-1, keepdims=True)
    xc = vg - mu
    vn = xc * lax.rsqrt(jnp.mean(xc * xc, axis=-1, keepdims=True) + EPS) * ln_g + ln_b
    t_idx = lax.broadcasted_iota(jnp.int32, (CHUNK, CHUNK), 0)
    s_idx = lax.broadcasted_iota(jnp.int32, (CHUNK, CHUNK), 1)
    causal = t_idx >= s_idx
    mixed = []
    for hd in range(GM_HEADS):
        wm = jnp.where(causal, w_heads[hd], 0.0)
        cols = slice(hd * GM_HEAD_DIM, (hd + 1) * GM_HEAD_DIM)
        mixed.append(_dot(wm, vn[:, cols]) + b_st[:, hd:hd + 1])
    ya0 = ug * jnp.concatenate(mixed, axis=1)
    return _rms(ya0, out_g)


def _gm_fwd(proj, ln_g, ln_b, w_s, b_st, out_g):
    T = proj.shape[0]

    def body(i, u_ref, v_ref, lg_ref, lb_ref, w_ref, bs_ref, og_ref, ya_ref):
        w_heads = [w_ref[hd] for hd in range(GM_HEADS)]
        ya = _gm_chunk(u_ref[...], v_ref[...], lg_ref[...], lb_ref[...], bs_ref[...], og_ref[...], *w_heads)
        ya_ref[...] = ya.astype(BF16)

    return _tiled(body, "gmlp_fwd", T // CHUNK, [(proj, CHUNK, GM_WIDTH, 0), (proj, CHUNK, GM_WIDTH, 1)],
                  [ln_g, ln_b, w_s, b_st, out_g], [], [(T, GM_WIDTH, BF16, CHUNK)], [])[0]


def _gm_bwd(proj, dya, ln_g, ln_b, w_s, b_st, out_g):
    T = proj.shape[0]

    def body(i, u_ref, v_ref, dy_ref, lg_ref, lb_ref, w_ref, bs_ref, og_ref, duv_ref, dlg_ref, dlb_ref, dw_ref, dbs_ref,
             dog_ref):
        w_heads = [w_ref[hd] for hd in range(GM_HEADS)]
        _, vjp = jax.vjp(_gm_chunk, u_ref[...], v_ref[...], lg_ref[...], lb_ref[...], bs_ref[...], og_ref[...], *w_heads)
        grads = vjp(dy_ref[...])
        duv_ref[:, :GM_WIDTH] = grads[0].astype(BF16)
        duv_ref[:, GM_WIDTH:] = grads[1].astype(BF16)
        dlg_ref[...] += grads[2]
        dlb_ref[...] += grads[3]
        dbs_ref[...] += grads[4]
        dog_ref[...] += grads[5]
        for hd in range(GM_HEADS):
            dw_ref[hd] += grads[6 + hd]

    return _tiled(body, "gmlp_bwd", T // CHUNK,
                  [(proj, CHUNK, GM_WIDTH, 0), (proj, CHUNK, GM_WIDTH, 1), (dya, CHUNK, GM_WIDTH, 0)],
                  [ln_g, ln_b, w_s, b_st, out_g], [], [(T, UV_W, BF16, CHUNK)],
                  [((1, GM_WIDTH), F32), ((1, GM_WIDTH), F32), ((GM_HEADS, CHUNK, CHUNK), F32),
                   ((CHUNK, GM_HEADS), F32), ((1, GM_WIDTH), F32)])


def _ssd_chunk(xc, z, dtr, s_in, dt_bias, a_log, d_skip, norm_g):
    half = SSM_WIDTH // SSM_GROUPS
    l_idx = lax.broadcasted_iota(jnp.int32, (CHUNK, CHUNK), 0)
    s_idx = lax.broadcasted_iota(jnp.int32, (CHUNK, CHUNK), 1)
    causal = l_idx >= s_idx
    tril = causal.astype(F32)
    head_of_col = lax.broadcasted_iota(jnp.int32, (SSM_HEADS, SSM_WIDTH), 1) // SSM_HEAD_DIM
    expand = (head_of_col == lax.broadcasted_iota(jnp.int32, (SSM_HEADS, SSM_WIDTH), 0)).astype(F32)

    xcs = _silu(xc)
    xs = xcs[:, :SSM_WIDTH]
    dt = jax.nn.softplus(dtr + dt_bias)
    adt = dt * (-jnp.exp(a_log))
    acs = _hdot(tril, adt)
    acs_t = _hdot_tn(adt, 1.0 - tril + (l_idx == s_idx).astype(F32))
    tot = acs[CHUNK - 1:CHUNK, :]
    dt_w = _hdot(dt, expand)
    out_decay_w = _hdot(jnp.exp(acs), expand)
    state_decay_w = _hdot(jnp.exp(tot - acs), expand)
    chunk_decay_w = _hdot(jnp.exp(tot), expand)
    d_skip_w = _hdot(d_skip, expand)
    xdt = xs * dt_w
    xdt_decayed = xdt * state_decay_w

    y_diag, y_off, states = [], [], []
    for grp in range(SSM_GROUPS):
        b0 = SSM_WIDTH + grp * SSM_STATE
        c0 = SSM_WIDTH + SSM_GROUPS * SSM_STATE + grp * SSM_STATE
        bm = xcs[:, b0:b0 + SSM_STATE].astype(BF16)
        cm = xcs[:, c0:c0 + SSM_STATE].astype(BF16)
        cb = _dot_nt(cm, bm)
        for k in range(grp * SSM_HEADS // SSM_GROUPS, (grp + 1) * SSM_HEADS // SSM_GROUPS):
            decay = jnp.exp(jnp.where(causal, acs[:, k:k + 1] - acs_t[k:k + 1, :], -jnp.inf))
            y_diag.append(_dot(cb * decay, xdt[:, k * SSM_HEAD_DIM:(k + 1) * SSM_HEAD_DIM]))
        cols = slice(grp * half, (grp + 1) * half)
        states.append(_dot_tn(bm, xdt_decayed[:, cols]))
        y_off.append(_dot(cm, s_in[:, cols]))
    y = jnp.concatenate(y_diag, axis=1) + jnp.concatenate(y_off, axis=1) * out_decay_w + xs * d_skip_w
    s_out = s_in * chunk_decay_w + jnp.concatenate(states, axis=1)
    y = y * _silu(z)
    y3 = y.reshape(CHUNK, SSM_GROUPS, half)
    y3 = y3 * lax.rsqrt(jnp.mean(y3 * y3, axis=-1, keepdims=True) + EPS)
    return y3.reshape(CHUNK, SSM_WIDTH) * norm_g, s_out


def _conv_taps(ext_ref, w, b):
    y = b
    for k in range(SSM_CONV):
        y = y + w[k:k + 1, :] * ext_ref[pl.ds(HALO - (SSM_CONV - 1) + k, CHUNK), :]
    return y


def _ssd_fwd(proj, conv_w, conv_b, dt_bias, a_log, d_skip, norm_g, comm=None):
    T = proj.shape[0]
    n_chunks = T // CHUNK

    def body(i, z_ref, x_ref, dt_ref, cw_ref, cb_ref, dtb_ref, al_ref, dsk_ref, ng_ref, yb_ref, sin_ref, ext_ref, st_ref):
        @pl.when(i == 0)
        def _():
            ext_ref[0:HALO, :] = jnp.zeros((HALO, CONV_DIM), F32)
            st_ref[...] = jnp.zeros(st_ref.shape, F32)

        ext_ref[HALO:, :] = x_ref[...]
        xc = _conv_taps(ext_ref, cw_ref[...], cb_ref[...])
        s_in = st_ref[...]
        yb, s_out = _ssd_chunk(xc, z_ref[...], dt_ref[:, 0:SSM_HEADS], s_in, dtb_ref[...], al_ref[...], dsk_ref[...],
                               ng_ref[...])
        yb_ref[...] = yb.astype(BF16)
        sin_ref[...] = s_in
        st_ref[...] = s_out
        ext_ref[0:HALO, :] = ext_ref[CHUNK:CHUNK + HALO, :]

    z_blk = 2 * GM_WIDTH // SSM_WIDTH
    x_blk = (2 * GM_WIDTH + SSM_WIDTH) // CONV_DIM
    dt_blk = (2 * GM_WIDTH + SSM_WIDTH + CONV_DIM) // LANES
    return _tiled(body, "ssd_fwd", n_chunks,
                  [(proj, CHUNK, SSM_WIDTH, z_blk), (proj, CHUNK, CONV_DIM, x_blk), (proj, CHUNK, LANES, dt_blk)],
                  [conv_w, conv_b, dt_bias, a_log, d_skip, norm_g], [],
                  [(T, SSM_WIDTH, BF16, CHUNK), (n_chunks * SSM_STATE, SSM_WIDTH, F32, SSM_STATE)], [],
                  scratch=[pltpu.VMEM((HALO + CHUNK, CONV_DIM), F32), pltpu.VMEM((SSM_STATE, SSM_WIDTH), F32)], comm=comm)


def _ssd_bwd(proj, dyb, s_all, conv_w, conv_b, dt_bias, a_log, d_skip, norm_g, comm=None):
    T = proj.shape[0]
    n_chunks = T // CHUNK
    z_blk = 2 * GM_WIDTH // SSM_WIDTH
    x_blk = (2 * GM_WIDTH + SSM_WIDTH) // CONV_DIM
    dt_blk = (2 * GM_WIDTH + SSM_WIDTH + CONV_DIM) // LANES
    rows_per_halo = CHUNK // HALO

    def body(i, z_ref, x_ref, halo_ref, dt_ref, dy_ref, sin_ref, cw_ref, cb_ref, dtb_ref, al_ref, dsk_ref, ng_ref,
             dzxd_ref, dcw_ref, dcb_ref, ddtb_ref, dal_ref, ddsk_ref, dng_ref, ext_ref, dext_ref, dst_ref):
        @pl.when(i == n_chunks - 1)
        def _():
            dext_ref[CHUNK:, :] = jnp.zeros((HALO, CONV_DIM), F32)
            dst_ref[...] = jnp.zeros(dst_ref.shape, F32)

        halo = halo_ref[...]
        ext_ref[0:HALO, :] = jnp.where(i == 0, jnp.zeros_like(halo), halo)
        ext_ref[HALO:, :] = x_ref[...]
        cw = cw_ref[...]
        xc = _conv_taps(ext_ref, cw, cb_ref[...])
        _, vjp = jax.vjp(_ssd_chunk, xc, z_ref[...], dt_ref[:, 0:SSM_HEADS], sin_ref[...], dtb_ref[...], al_ref[...],
                         dsk_ref[...], ng_ref[...])
        dxc, dz, ddtr, ds_in, ddtb, dal, ddsk, dng = vjp((dy_ref[...], dst_ref[...]))
        dst_ref[...] = ds_in
        ddtb_ref[...] += ddtb
        dal_ref[...] += dal
        ddsk_ref[...] += ddsk
        dng_ref[...] += dng
        dext_ref[0:CHUNK, :] = dxc
        dx = jnp.zeros((CHUNK, CONV_DIM), F32)
        for k in range(SSM_CONV):
            dx = dx + cw[k:k + 1, :] * dext_ref[pl.ds(SSM_CONV - 1 - k, CHUNK), :]
            dcw_ref[k:k + 1, :] += jnp.sum(dxc * ext_ref[pl.ds(HALO - (SSM_CONV - 1) + k, CHUNK), :], axis=0, keepdims=True)
        dcb_ref[...] += jnp.sum(dxc, axis=0, keepdims=True)
        dext_ref[CHUNK:, :] = dext_ref[0:HALO, :]
        dzxd_ref[:, 0:SSM_WIDTH] = dz.astype(BF16)
        dzxd_ref[:, SSM_WIDTH:SSM_WIDTH + CONV_DIM] = dx.astype(BF16)
        dzxd_ref[:, SSM_WIDTH + CONV_DIM:] = jnp.concatenate(
            [ddtr, jnp.zeros((CHUNK, LANES - SSM_HEADS), F32)], axis=1).astype(BF16)

    def halo_index(step):
        c = n_chunks - 1 - step
        return (jnp.maximum(c * rows_per_halo - 1, 0), x_blk)

    return _tiled(body, "ssd_bwd", n_chunks,
                  [(proj, CHUNK, SSM_WIDTH, z_blk), (proj, CHUNK, CONV_DIM, x_blk), (proj, HALO, CONV_DIM, halo_index),
                   (proj, CHUNK, LANES, dt_blk), (dyb, CHUNK, SSM_WIDTH, 0), (s_all, SSM_STATE, SSM_WIDTH, 0)],
                  [conv_w, conv_b, dt_bias, a_log, d_skip, norm_g], [],
                  [(T, ZXD_W, BF16, CHUNK)],
                  [((SSM_CONV, CONV_DIM), F32), ((1, CONV_DIM), F32), ((1, SSM_HEADS), F32), ((1, SSM_HEADS), F32),
                   ((1, SSM_HEADS), F32), ((1, SSM_WIDTH), F32)],
                  scratch=[pltpu.VMEM((HALO + CHUNK, CONV_DIM), F32), pltpu.VMEM((CHUNK + HALO, CONV_DIM), F32),
                           pltpu.VMEM((SSM_STATE, SSM_WIDTH), F32)],
                  reverse=True, comm=comm)


TAIL_TM = 256


def _tail(h, p, target, ple_norm, w_gate, b_gate, w_proj_t, final_norm):
    T = h.shape[0]

    def head(x, pre, pp, b_g, f_norm, tgt):
        gate = jax.nn.sigmoid(pre + b_g)
        out = _rms(x + gate * pp, f_norm)
        err = out - tgt
        return 0.5 * jnp.sum(jnp.mean(err * err, axis=-1, keepdims=True), axis=0, keepdims=True)

    def body(i, h_ref, p_ref, t_ref, pn_ref, bg_ref, fn_ref, wg_ref, wp_ref, dh_ref, loss_ref, dwg_ref, dwp_ref, dpn_ref,
             dbg_ref, dfn_ref):
        x = h_ref[...]
        n4f, n_vjp = jax.vjp(_rms, x, pn_ref[...])
        n4 = n4f.astype(BF16)
        pre = jnp.dot(n4, wg_ref[...], preferred_element_type=F32)
        p16 = p_ref[...].astype(BF16)
        pp = _dot_nt(p16, wp_ref[...])
        loss, h_vjp = jax.vjp(functools.partial(head, tgt=t_ref[...]), x, pre, pp, bg_ref[...], fn_ref[...])
        dx, dpre, dpp, dbg, dfn = h_vjp(jnp.ones((1, 1), F32))
        dpre16 = dpre.astype(BF16)
        dn4 = _dot_nt(dpre16, wg_ref[...])
        dx2, dpn = n_vjp(dn4)
        dh_ref[...] = dx + dx2
        loss_ref[...] += loss
        dwg_ref[...] += _dot_tn(n4, dpre16)
        dwp_ref[...] += _dot_tn(p16, dpp)
        dpn_ref[...] += dpn
        dbg_ref[...] += dbg
        dfn_ref[...] += dfn

    return _tiled(body, "tail", T // TAIL_TM,
                  [(h, TAIL_TM, D_MODEL, 0), (p, TAIL_TM, D_PLE, 0), (target, TAIL_TM, D_MODEL, 0)],
                  [ple_norm, b_gate, final_norm], [w_gate, w_proj_t],
                  [(T, D_MODEL, F32, TAIL_TM)],
                  [((1, 1), F32), ((D_MODEL, D_MODEL), F32), ((D_PLE, D_MODEL), F32), ((1, D_MODEL), F32),
                   ((1, D_MODEL), F32), ((1, D_MODEL), F32)])


def _gather_phases(x_ref, out_ref, send_sems, recv_sems, local_sem):
    mx, my, mc = lax.axis_index("x"), lax.axis_index("y"), lax.axis_index("c")
    me, sibling = (mx, my, mc), (mx, my, 1 - mc)
    chips = [(1 - mx, my), (mx, 1 - my), (1 - mx, 1 - my)]

    def rows(px, py, pc):
        return out_ref.at[4 * px + 2 * py + pc]

    def copy(k, block, to, src=None):
        return pltpu.make_async_remote_copy(
            src_ref=rows(*block) if src is None else src, dst_ref=rows(*block),
            send_sem=send_sems.at[k], recv_sem=recv_sems.at[k], device_id=to, device_id_type=MESH)

    mine = pltpu.make_async_copy(x_ref, rows(*me), local_sem)
    first = [copy(0, me, sibling, src=x_ref)] + [copy(1 + j, me, (*chip, mc), src=x_ref) for j, chip in enumerate(chips)]
    passed = [copy(4 + j, (*chip, mc), sibling) for j, chip in enumerate(chips)]

    def start():
        mine.start()
        for cp in first:
            cp.start()

    def mid():
        for j, chip in enumerate(chips):
            copy(1 + j, (*chip, mc), me).wait_recv()
            passed[j].start()

    def finish():
        copy(0, sibling, me).wait_recv()
        for j, chip in enumerate(chips):
            copy(4 + j, (*chip, 1 - mc), me).wait_recv()
        for cp in first + passed:
            cp.wait_send()
        mine.wait()

    return start, mid, finish


def _exchange_phases(x_ref, out_ref, send_sems, recv_sems, local_sem):
    mx, my, mc = lax.axis_index("x"), lax.axis_index("y"), lax.axis_index("c")
    me = 4 * mx + 2 * my + mc
    mine = pltpu.make_async_copy(x_ref.at[me], out_ref.at[me], local_sem)
    copies = []
    for k in range(1, N_DEV):
        px = 1 - mx if k & 4 else mx
        py = 1 - my if k & 2 else my
        pc = 1 - mc if k & 1 else mc
        copies.append(pltpu.make_async_remote_copy(
            src_ref=x_ref.at[4 * px + 2 * py + pc], dst_ref=out_ref.at[me], send_sem=send_sems.at[k - 1],
            recv_sem=recv_sems.at[k - 1], device_id=(px, py, pc), device_id_type=MESH))

    def start():
        mine.start()
        for cp in copies:
            cp.start()

    def finish():
        for cp in copies:
            cp.wait_recv()
        for cp in copies:
            cp.wait_send()
        mine.wait()

    return start, lambda: None, finish


def _gather_comm(x):
    return _Comm(_gather_phases, x, jax.ShapeDtypeStruct((N_DEV,) + x.shape, x.dtype))


def _exchange_comm(x):
    return _Comm(_exchange_phases, x, jax.ShapeDtypeStruct(x.shape, x.dtype))


def _comm_alone(comm, name):
    def body(x_ref, out_ref, send_sems, recv_sems, local_sem):
        for phase in comm.phases(x_ref, out_ref, send_sems, recv_sems, local_sem):
            phase()

    return pl.pallas_call(
        body,
        out_shape=comm.dst,
        in_specs=[pl.BlockSpec(memory_space=pl.ANY)],
        out_specs=pl.BlockSpec(memory_space=pl.ANY),
        scratch_shapes=[pltpu.SemaphoreType.DMA((N_DEV - 1,)), pltpu.SemaphoreType.DMA((N_DEV - 1,)), pltpu.SemaphoreType.DMA],
        name=name,
    )(comm.src)


def _sum_adamw(parts, w, m, v, tr, name):
    _, R, C = parts.shape

    def kern(p_ref, w_ref, m_ref, v_ref, g_ref, d_ref, nm_ref, nv_ref):
        g = p_ref[0].astype(F32)
        for j in range(1, N_DEV):
            g = g + p_ref[j].astype(F32)
        m_new = ADAM_B1 * m_ref[...] + (1.0 - ADAM_B1) * g
        v_new = ADAM_B2 * v_ref[...] + (1.0 - ADAM_B2) * jnp.square(g)
        m_hat = m_new / (1.0 - ADAM_B1 ** ADAM_STEP)
        v_hat = v_new / (1.0 - ADAM_B2 ** ADAM_STEP)
        g_ref[...] = g
        d_ref[...] = -ADAM_LR * (m_hat / (jnp.sqrt(v_hat) + ADAM_EPS) + ADAM_WD * w_ref[...])
        nm_ref[...] = m_new
        nv_ref[...] = v_new

    row_spec = pl.BlockSpec((tr, C), lambda i: (i, 0))
    return pl.pallas_call(
        kern,
        out_shape=[jax.ShapeDtypeStruct((R, C), F32)] * 4,
        grid=(R // tr,),
        in_specs=[pl.BlockSpec((N_DEV, tr, C), lambda i: (0, i, 0)), row_spec, row_spec, row_spec],
        out_specs=[row_spec] * 4,
        name=name,
        compiler_params=pltpu.CompilerParams(dimension_semantics=("arbitrary",), vmem_limit_bytes=VMEM_LIMIT),
    )(parts, w, m, v)


FF_SHARD = D_FF // N_DEV
CONV_SHARD = (SSM_CONV, CONV_DIM // N_DEV)
SHARDS = {"ffn1_w_gate": ((D_MODEL, FF_SHARD), True), "ffn1_w_up": ((D_MODEL, FF_SHARD), True),
          "ffn1_w_down": ((FF_SHARD, D_MODEL), False),
          "ffn2_w_gate": ((D_MODEL, FF_SHARD), True), "ffn2_w_up": ((D_MODEL, FF_SHARD), True),
          "ffn2_w_down": ((FF_SHARD, D_MODEL), False),
          "w_out": ((2 * D_MODEL // N_DEV, D_MODEL), False), "ple_w_gate": ((D_MODEL // N_DEV, D_MODEL), False),
          "w_in": ((D_MODEL, IN_PROJ // N_DEV), True), "ple_w_proj": ((D_PLE, D_MODEL // N_DEV), True),
          "conv_w": (CONV_SHARD, True),
          "conv_w_mid": (CONV_SHARD, True), "conv_w_low": (CONV_SHARD, True)}
BIG = tuple(name for name in SHARDS if not name.startswith("conv_w_"))
SMALL = ("ffn1_norm", "mix_norm", "gm_ln_g", "gm_ln_b", "gm_w_s", "gm_b_s", "gm_out_norm", "conv_b", "dt_bias", "a_log",
         "d_skip", "ssm_norm", "ffn2_norm", "ple_norm", "ple_b_gate", "final_norm")
SMALL_ROWS = 144


def _piece_rows(name):
    shape = SHARDS[name][0]
    return -(-(shape[0] * shape[1]) // PACK_COLS)


def _pad_cols(flat, name):
    pad = _piece_rows(name) * PACK_COLS - flat.shape[-1]
    return flat if pad == 0 else jnp.pad(flat, [(0, 0)] * (flat.ndim - 1) + [(0, pad)])


class _Pack:
    def __init__(self, names, tile_rows):
        self.names, self.tile_rows, self.offsets, off = names, tile_rows, {}, 0
        for name in names:
            self.offsets[name] = off
            off += _piece_rows(name)
        self.rows = -(-off // tile_rows) * tile_rows

    def pack_local(self, vals):
        parts = []
        for name in self.names:
            val = vals[name]
            parts.append(_pad_cols((val.T if SHARDS[name][1] else val).reshape(-1), name))
        flat = jnp.concatenate(parts)
        return jnp.pad(flat, (0, self.rows * PACK_COLS - flat.shape[0])).reshape(self.rows, PACK_COLS)

    def unpack_local(self, packed):
        out = {}
        for name in self.names:
            shape, transposed = SHARDS[name]
            piece = packed[self.offsets[name]:self.offsets[name] + _piece_rows(name)].reshape(-1)[:shape[0] * shape[1]]
            out[name] = piece.reshape(shape[::-1]).T if transposed else piece.reshape(shape)
        return out

    def pack_owner_major(self, grads):
        parts = [_pad_cols(grads[name].reshape(N_DEV, -1).astype(BF16), name) for name in self.names]
        flat = jnp.concatenate(parts, axis=1)
        flat = jnp.pad(flat, ((0, 0), (0, self.rows * PACK_COLS - flat.shape[1])))
        return flat.reshape(N_DEV, self.rows, PACK_COLS)

    def gathered_piece(self, gathered, name):
        shape = SHARDS[name][0]
        rows = gathered[:, self.offsets[name]:self.offsets[name] + _piece_rows(name), :]
        return rows.reshape(N_DEV, -1)[:, :shape[0] * shape[1]]

    def pieces(self, gathered, name):
        return _Pieces(gathered, self.offsets[name], _piece_rows(name))


BF16_ROWS = 16
GATHER_FFN1 = _Pack(("ffn1_w_gate", "ffn1_w_up", "ffn1_w_down"), BF16_ROWS)
GATHER_MIX = _Pack(("w_out", "ple_w_gate", "w_in", "ple_w_proj", "conv_w", "conv_w_mid", "conv_w_low"), BF16_ROWS)
GATHER_FFN2 = _Pack(("ffn2_w_gate", "ffn2_w_up", "ffn2_w_down"), BF16_ROWS)
SCATTER_LATE = _Pack(("ffn2_w_gate", "ffn2_w_up", "ffn2_w_down", "w_out", "ple_w_gate", "ple_w_proj"), 368)
SCATTER_EARLY = _Pack(("ffn1_w_gate", "ffn1_w_up", "ffn1_w_down", "w_in", "conv_w"), 208)


def _pack_small(vals):
    flat = jnp.concatenate([vals[name].reshape(-1).astype(F32) for name in SMALL])
    return jnp.pad(flat, (0, SMALL_ROWS * PACK_COLS - flat.shape[0])).reshape(SMALL_ROWS, PACK_COLS)


def _unpack_small(packed, shapes):
    out, off = {}, 0
    flat = packed.reshape(-1)
    for name in SMALL:
        n = 1
        for s in shapes[name]:
            n *= s
        out[name] = flat[off:off + n].reshape(shapes[name])
        off += n
    return out


WEIGHTS = ("ffn1_norm", "ffn1_w_gate", "ffn1_w_up", "ffn1_w_down", "mix_norm", "w_in", "gm_ln_g", "gm_ln_b", "gm_w_s",
           "gm_b_s", "gm_out_norm", "conv_w", "conv_b", "dt_bias", "a_log", "d_skip", "ssm_norm", "w_out", "ffn2_norm",
           "ffn2_w_gate", "ffn2_w_up", "ffn2_w_down", "ple_norm", "ple_w_gate", "ple_b_gate", "ple_w_proj", "final_norm")


def _step(x, p, target, w, m, v):
    local = lambda d: {name: d[name][0] for name in BIG}

    shards = {name: val.astype(BF16) for name, val in local(w).items()}
    conv_rest = w["conv_w"][0] - shards["conv_w"].astype(F32)
    shards["conv_w_mid"] = conv_rest.astype(BF16)
    shards["conv_w_low"] = (conv_rest - shards["conv_w_mid"].astype(F32)).astype(BF16)
    g_ffn1 = _comm_alone(_gather_comm(GATHER_FFN1.pack_local(shards)), "gather_ffn1")

    row = lambda name: w[name].reshape(1, -1)
    gm_w_s = w["gm_w_s"][0]
    gm_b_st = jnp.transpose(w["gm_b_s"][0])
    ffn1 = (row("ffn1_norm"),) + tuple(GATHER_FFN1.pieces(g_ffn1, name) for name in GATHER_FFN1.names)
    gm = (row("gm_ln_g"), row("gm_ln_b"), gm_w_s, gm_b_st, row("gm_out_norm"))

    h1, g_mix = _ffn_fwd(x, *ffn1, "ffn1_fwd", comm=_gather_comm(GATHER_MIX.pack_local(shards)))
    w_in_t = GATHER_MIX.gathered_piece(g_mix, "w_in").reshape(IN_PROJ, D_MODEL)
    w_in_t = jnp.concatenate([w_in_t, jnp.zeros((IN_PROJ_PAD - IN_PROJ, D_MODEL), BF16)], axis=0)
    w_proj_t = GATHER_MIX.gathered_piece(g_mix, "ple_w_proj").reshape(D_MODEL, D_PLE)
    conv_w = sum(GATHER_MIX.gathered_piece(g_mix, name).astype(F32) for name in ("conv_w", "conv_w_mid", "conv_w_low"))
    conv_w = conv_w.reshape(CONV_DIM, SSM_CONV).T
    ssd = (conv_w, row("conv_b"), row("dt_bias"), row("a_log"), row("d_skip"), row("ssm_norm"))
    w_out = GATHER_MIX.pieces(g_mix, "w_out")

    proj, n2 = _mix_in_fwd(h1, row("mix_norm"), w_in_t)
    ya = _gm_fwd(proj, *gm)
    yb, s_all, g_ffn2 = _ssd_fwd(proj, *ssd, comm=_gather_comm(GATHER_FFN2.pack_local(shards)))
    ffn2 = (row("ffn2_norm"),) + tuple(GATHER_FFN2.pieces(g_ffn2, name) for name in GATHER_FFN2.names)
    h2 = _out_proj_fwd(h1, ya, yb, w_out)
    h3 = _ffn_fwd(h2, *ffn2, "ffn2_fwd")[0]

    g, gp = {}, {}
    dh3, loss, gp["ple_w_gate"], d_w_proj, g["ple_norm"], g["ple_b_gate"], g["final_norm"] = _tail(
        h3, p, target, row("ple_norm"), GATHER_MIX.pieces(g_mix, "ple_w_gate"), row("ple_b_gate"), w_proj_t,
        row("final_norm"))
    gp["ple_w_proj"] = d_w_proj.T

    dh2, n3, s3, da3, db3, g["ffn2_norm"] = _ffn_dgrad(h2, dh3, *ffn2, "ffn2_dgrad")
    gp["ffn2_w_gate"] = _wgrad(n3, da3, 1408, "ffn2_wgrad_gate", transpose_out=True)
    gp["ffn2_w_up"] = _wgrad(n3, db3, 1408, "ffn2_wgrad_up", transpose_out=True)
    gp["ffn2_w_down"] = _wgrad(s3, dh3, 512, "ffn2_wgrad_down", scale=0.5)

    dya, dyb = _out_proj_dgrad(dh2, w_out)
    gp["w_out"] = jnp.concatenate([_wgrad(ya, dh2, 1024, "w_out_wgrad_a"), _wgrad(yb, dh2, 1024, "w_out_wgrad_b")], axis=0)

    dp_zxd, d_conv_w, g["conv_b"], g["dt_bias"], g["a_log"], g["d_skip"], g["ssm_norm"], parts_late = _ssd_bwd(
        proj, dyb, s_all, *ssd, comm=_exchange_comm(SCATTER_LATE.pack_owner_major(gp)))
    gp["conv_w"] = d_conv_w.T
    dp_uv, g["gm_ln_g"], g["gm_ln_b"], g["gm_w_s"], dbst, g["gm_out_norm"] = _gm_bwd(proj, dya, *gm)
    g["gm_b_s"] = jnp.transpose(dbst)

    dh1, g["mix_norm"] = _mix_in_dgrad(h1, dh2, dp_uv, dp_zxd, row("mix_norm"), w_in_t)
    gp["w_in"] = jnp.concatenate([_wgrad(n2, dp_uv, 1024, "w_in_wgrad_uv", transpose_out=True),
                                  _wgrad(n2, dp_zxd, 896, "w_in_wgrad_zxd", transpose_out=True)], axis=0)[:IN_PROJ]

    dx, n1, s1, da1, db1, g["ffn1_norm"] = _ffn_dgrad(x, dh1, *ffn1, "ffn1_dgrad")
    gp["ffn1_w_gate"] = _wgrad(n1, da1, 1408, "ffn1_wgrad_gate", transpose_out=True)
    gp["ffn1_w_up"] = _wgrad(n1, db1, 1408, "ffn1_wgrad_up", transpose_out=True)
    gp["ffn1_w_down"] = _wgrad(s1, dh1, 512, "ffn1_wgrad_down", scale=0.5)

    parts_early = _comm_alone(_exchange_comm(SCATTER_EARLY.pack_owner_major(gp)), "scatter_early")
    res_big = [{}, {}, {}, {}]
    for pack, parts, name in ((SCATTER_LATE, parts_late, "adamw_late"), (SCATTER_EARLY, parts_early, "adamw_early")):
        res = _sum_adamw(parts, pack.pack_local(local(w)), pack.pack_local(local(m)), pack.pack_local(local(v)),
                         pack.tile_rows, name)
        for k in range(4):
            res_big[k].update(pack.unpack_local(res[k]))

    small_shapes = {name: w[name].shape for name in SMALL}
    small_parts = _comm_alone(_gather_comm(_pack_small(g)), "gather_small_grads")
    res_small = _sum_adamw(small_parts, _pack_small(w), _pack_small(m), _pack_small(v), SMALL_ROWS, "adamw_small")
    res_small = [_unpack_small(r, small_shapes) for r in res_small]

    outs = []
    for k in range(4):
        for name in WEIGHTS:
            if name in res_small[k]:
                outs.append(res_small[k][name])
            else:
                outs.append(res_big[k][name].reshape(w[name].shape))
    return loss[0, 0], dx, outs


def kernel(x, p, ffn1_norm, ffn1_w_gate, ffn1_w_up, ffn1_w_down, mix_norm, w_in, gm_ln_g, gm_ln_b, gm_w_s, gm_b_s, gm_out_norm, conv_w, conv_b, dt_bias, a_log, d_skip, ssm_norm, w_out, ffn2_norm, ffn2_w_gate, ffn2_w_up, ffn2_w_down, ple_norm, ple_w_gate, ple_b_gate, ple_w_proj, final_norm, loss_target, m_ffn1_norm, m_ffn1_w_gate, m_ffn1_w_up, m_ffn1_w_down, m_mix_norm, m_w_in, m_gm_ln_g, m_gm_ln_b, m_gm_w_s, m_gm_b_s, m_gm_out_norm, m_conv_w, m_conv_b, m_dt_bias, m_a_log, m_d_skip, m_ssm_norm, m_w_out, m_ffn2_norm, m_ffn2_w_gate, m_ffn2_w_up, m_ffn2_w_down, m_ple_norm, m_ple_w_gate, m_ple_b_gate, m_ple_w_proj, m_final_norm, v_ffn1_norm, v_ffn1_w_gate, v_ffn1_w_up, v_ffn1_w_down, v_mix_norm, v_w_in, v_gm_ln_g, v_gm_ln_b, v_gm_w_s, v_gm_b_s, v_gm_out_norm, v_conv_w, v_conv_b, v_dt_bias, v_a_log, v_d_skip, v_ssm_norm, v_w_out, v_ffn2_norm, v_ffn2_w_gate, v_ffn2_w_up, v_ffn2_w_down, v_ple_norm, v_ple_w_gate, v_ple_b_gate, v_ple_w_proj, v_final_norm):
    args = locals()
    w = {name: args[name] for name in WEIGHTS}
    m = {name: args["m_" + name] for name in WEIGHTS}
    v = {name: args["v_" + name] for name in WEIGHTS}
    loss, dx, outs = _step(x[0], p[0, 0], loss_target[0], w, m, v)
    loss = lax.psum(loss, AXES)
    return (loss, dx[None], *outs)
```

```python
import functools
from typing import NamedTuple

import jax
import jax.numpy as jnp
from jax import lax
from jax.experimental import pallas as pl
from jax.experimental.pallas import tpu as pltpu

F32 = jnp.float32
BF16 = jnp.bfloat16
HIGHEST = lax.Precision.HIGHEST
MESH = pl.DeviceIdType.MESH
AXES = ("x", "y", "c")
N_DEV = 8

D_MODEL = 1024
D_FF = 2816
D_PLE = 256
GM_WIDTH = 1024
GM_HEADS = 8
GM_HEAD_DIM = 128
CHUNK = 128
SSM_WIDTH = 1024
SSM_HEADS = 16
SSM_HEAD_DIM = 64
SSM_GROUPS = 2
SSM_STATE = 128
SSM_CONV = 4
CONV_DIM = SSM_WIDTH + 2 * SSM_GROUPS * SSM_STATE
IN_PROJ = 2 * GM_WIDTH + SSM_WIDTH + CONV_DIM + SSM_HEADS
LANES = 128
BF16_ROWS = 16
IN_PROJ_PAD = IN_PROJ - SSM_HEADS + LANES
UV_W = 2 * GM_WIDTH
ZXD_W = IN_PROJ_PAD - UV_W
HALO = 8
EPS = 1e-6

ADAM_LR = 0.001
ADAM_B1 = 0.9
ADAM_B2 = 0.999
ADAM_EPS = 1e-08
ADAM_WD = 0.01
ADAM_STEP = 10

VMEM_LIMIT = 56 * 1024 * 1024
PACK_COLS = 1024


def _rms(x, g):
    return x * lax.rsqrt(jnp.mean(x * x, axis=-1, keepdims=True) + EPS) * g


def _gelu(x):
    return 0.5 * x * (1.0 + lax.erf(x * (2.0 ** -0.5)))


def _silu(x):
    return x * jax.nn.sigmoid(x)


def _dot(a, b):
    return jnp.dot(a.astype(BF16), b.astype(BF16), preferred_element_type=F32)


def _dot_nt(a, b):
    return lax.dot_general(a.astype(BF16), b.astype(BF16), (((1,), (1,)), ((), ())), preferred_element_type=F32)


def _dot_tn(a, b):
    return lax.dot_general(a.astype(BF16), b.astype(BF16), (((0,), (0,)), ((), ())), preferred_element_type=F32)


def _hdot(a, b):
    return jnp.dot(a, b, precision=HIGHEST, preferred_element_type=F32)


def _hdot_tn(a, b):
    return lax.dot_general(a, b, (((0,), (0,)), ((), ())), precision=HIGHEST, preferred_element_type=F32)


class _Pieces(NamedTuple):
    gathered: jax.Array
    row_off: int
    rows: int


class _Comm(NamedTuple):
    phases: object
    src: jax.Array
    dst: jax.ShapeDtypeStruct


def _tiled(body, name, n_steps, tiled_in, full_in, big_in, tiled_out, acc_out, scratch=(), reverse=False, comm=None):
    n_t, n_f, n_b, n_to, n_a = len(tiled_in), len(full_in), len(big_in), len(tiled_out), len(acc_out)
    n_c = 1 if comm else 0

    def row(i):
        return n_steps - 1 - i if reverse else i

    in_specs, args = [], []
    for arr, br, bc, cb in tiled_in:
        if callable(cb):
            in_specs.append(pl.BlockSpec((br, bc), cb))
        else:
            in_specs.append(pl.BlockSpec((br, bc), functools.partial(lambda i, cb: (row(i), cb), cb=cb)))
        args.append(arr)
    for arr in full_in:
        in_specs.append(pl.BlockSpec(arr.shape, functools.partial(lambda i, nd: (0,) * nd, nd=arr.ndim)))
        args.append(arr)
    big_shapes, n_copies = [], 0
    for big in big_in:
        in_specs.append(pl.BlockSpec(memory_space=pl.ANY))
        if isinstance(big, _Pieces):
            args.append(big.gathered)
            big_shapes.append(((N_DEV * big.rows, PACK_COLS), big.gathered.dtype))
            n_copies += N_DEV
        else:
            args.append(big)
            big_shapes.append((big.shape, big.dtype))
            n_copies += 1
    if comm:
        in_specs.append(pl.BlockSpec(memory_space=pl.ANY))
        args.append(comm.src)
    out_specs, out_shape = [], []
    for rows, cols, dt, br in tiled_out:
        out_specs.append(pl.BlockSpec((br, cols), lambda i: (row(i), 0)))
        out_shape.append(jax.ShapeDtypeStruct((rows, cols), dt))
    for shp, dt in acc_out:
        out_specs.append(pl.BlockSpec(shp, functools.partial(lambda i, nd: (0,) * nd, nd=len(shp))))
        out_shape.append(jax.ShapeDtypeStruct(shp, dt))
    if comm:
        out_specs.append(pl.BlockSpec(memory_space=pl.ANY))
        out_shape.append(comm.dst)
    scratch_shapes = [pltpu.VMEM(shp, dt) for shp, dt in big_shapes] + list(scratch)
    if n_copies:
        scratch_shapes.append(pltpu.SemaphoreType.DMA((n_copies,)))
    if comm:
        scratch_shapes += [pltpu.SemaphoreType.DMA((N_DEV - 1,)), pltpu.SemaphoreType.DMA((N_DEV - 1,)), pltpu.SemaphoreType.DMA]

    def kern(*refs):
        n_in = n_t + n_f + n_b + n_c
        ins = refs[: n_t + n_f]
        big_hbm = refs[n_t + n_f : n_t + n_f + n_b]
        outs = refs[n_in : n_in + n_to + n_a]
        rest = refs[n_in + n_to + n_a + n_c :]
        big_vmem, scr = rest[:n_b], rest[n_b:]
        if comm:
            scr, comm_sems = scr[:-3], scr[-3:]
            comm_start, comm_mid, comm_finish = comm.phases(refs[n_in - 1], refs[n_in + n_to + n_a], *comm_sems)
        if n_copies:
            scr, copy_sems = scr[:-1], scr[-1]
        step = pl.program_id(0)

        @pl.when(step == 0)
        def _():
            copies = []
            for big, src, dst in zip(big_in, big_hbm, big_vmem):
                if isinstance(big, _Pieces):
                    for j in range(N_DEV):
                        copies.append((src.at[j, pl.ds(big.row_off, big.rows), :], dst.at[pl.ds(j * big.rows, big.rows), :]))
                else:
                    copies.append((src, dst))
            copies = [pltpu.make_async_copy(a, b, copy_sems.at[k]) for k, (a, b) in enumerate(copies)]
            for cp in copies:
                cp.start()
            for cp in copies:
                cp.wait()
            for acc in outs[n_to:]:
                acc[...] = jnp.zeros(acc.shape, acc.dtype)
            if comm:
                comm_start()

        body(row(step), *ins, *big_vmem, *outs, *scr)
        if comm:
            pl.when(step == (n_steps - 1) // 2)(comm_mid)
            pl.when(step == n_steps - 1)(comm_finish)

    res = pl.pallas_call(
        kern,
        out_shape=out_shape,
        grid=(n_steps,),
        in_specs=in_specs,
        out_specs=out_specs,
        scratch_shapes=scratch_shapes,
        name=name,
        compiler_params=pltpu.CompilerParams(dimension_semantics=("arbitrary",), vmem_limit_bytes=VMEM_LIMIT),
    )(*args)
    return res


FF_CHUNKS = ((0, 1536), (1536, D_FF))
FFN_TM = 256


def _ffn_fwd(h, g, wg_t, wu_t, wd, name, comm=None):
    T = h.shape[0]

    def body(i, h_ref, g_ref, wg_ref, wu_ref, wd_ref, o_ref):
        x = h_ref[...]
        n = _rms(x, g_ref[...]).astype(BF16)
        f = jnp.zeros(x.shape, F32)
        for lo, hi in FF_CHUNKS:
            a = _dot_nt(n, wg_ref[lo:hi, :])
            b = _dot_nt(n, wu_ref[lo:hi, :])
            s = (_silu(a) * b).astype(BF16)
            f = f + jnp.dot(s, wd_ref[lo:hi, :], preferred_element_type=F32)
        o_ref[...] = x + 0.5 * f

    return _tiled(body, name, T // FFN_TM, [(h, FFN_TM, D_MODEL, 0)], [g], [wg_t, wu_t, wd],
                  [(T, D_MODEL, F32, FFN_TM)], [], comm=comm)


def _ffn_dgrad(h, dout, g, wg_t, wu_t, wd, name):
    T = h.shape[0]

    def body(i, h_ref, do_ref, g_ref, wg_ref, wu_ref, wd_ref, dh_ref, n_ref, s_ref, da_ref, db_ref, dg_ref):
        x = h_ref[...]
        dout = do_ref[...]
        nf, rms_vjp = jax.vjp(_rms, x, g_ref[...])
        n = nf.astype(BF16)
        dfo = (0.5 * dout).astype(BF16)
        dn = jnp.zeros(x.shape, F32)
        for lo, hi in FF_CHUNKS:
            a = _dot_nt(n, wg_ref[lo:hi, :])
            b = _dot_nt(n, wu_ref[lo:hi, :])
            sg = jax.nn.sigmoid(a)
            sl = a * sg
            ds = _dot_nt(dfo, wd_ref[lo:hi, :])
            db = (ds * sl).astype(BF16)
            da = (ds * b * (sg * (1.0 + a * (1.0 - sg)))).astype(BF16)
            dn = dn + _dot(da, wg_ref[lo:hi, :]) + _dot(db, wu_ref[lo:hi, :])
            s_ref[:, lo:hi] = (sl * b).astype(BF16)
            da_ref[:, lo:hi] = da
            db_ref[:, lo:hi] = db
        dx, dg = rms_vjp(dn)
        dh_ref[...] = dout + dx
        n_ref[...] = n
        dg_ref[...] += dg

    return _tiled(body, name, T // FFN_TM, [(h, FFN_TM, D_MODEL, 0), (dout, FFN_TM, D_MODEL, 0)], [g], [wg_t, wu_t, wd],
                  [(T, D_MODEL, F32, FFN_TM), (T, D_MODEL, BF16, FFN_TM), (T, D_FF, BF16, FFN_TM),
                   (T, D_FF, BF16, FFN_TM), (T, D_FF, BF16, FFN_TM)], [((1, D_MODEL), F32)])


def _wgrad(a, b, bn, name, scale=None, transpose_out=False, bk=512, comm=None):
    T, M = a.shape
    N = b.shape[1]
    bk = min(bk, T)
    assert M % LANES == 0 and N % bn == 0 and T % bk == 0
    n_j, n_k = N // bn, T // bk
    n_c = 1 if comm else 0

    def kern(*refs):
        a_ref, b_ref, o_ref, acc_ref = refs[0], refs[1], refs[2 + n_c], refs[3 + 2 * n_c]
        j, k = pl.program_id(0), pl.program_id(1)
        if comm:
            comm_start, _, comm_finish = comm.phases(refs[2], refs[4], *refs[6:])
            pl.when((j == 0) & (k == 0))(comm_start)

        @pl.when(k == 0)
        def _():
            acc_ref[...] = jnp.zeros(acc_ref.shape, F32)

        bv = b_ref[...]
        if scale is not None:
            bv = bv * scale
        acc_ref[...] += _dot_tn(a_ref[...], bv)

        @pl.when(k == n_k - 1)
        def _():
            acc = acc_ref[...]
            o_ref[...] = (acc.T if transpose_out else acc).astype(BF16)

        if comm:
            pl.when((j == n_j - 1) & (k == n_k - 1))(comm_finish)

    if transpose_out:
        out_shape, out_spec = (N, M), pl.BlockSpec((bn, M), lambda j, k: (j, 0))
    else:
        out_shape, out_spec = (M, N), pl.BlockSpec((M, bn), lambda j, k: (0, j))
    any_spec = pl.BlockSpec(memory_space=pl.ANY)
    comm_sems = [pltpu.SemaphoreType.DMA((N_DEV - 1,)), pltpu.SemaphoreType.DMA((N_DEV - 1,)), pltpu.SemaphoreType.DMA]
    res = pl.pallas_call(
        kern,
        out_shape=[jax.ShapeDtypeStruct(out_shape, BF16)] + ([comm.dst] if comm else []),
        grid=(n_j, n_k),
        in_specs=[pl.BlockSpec((bk, M), lambda j, k: (k, 0)), pl.BlockSpec((bk, bn), lambda j, k: (k, j))] + [any_spec] * n_c,
        out_specs=[out_spec] + [any_spec] * n_c,
        scratch_shapes=[pltpu.VMEM((M, bn), F32)] + (comm_sems if comm else []),
        name=name,
        compiler_params=pltpu.CompilerParams(dimension_semantics=("arbitrary", "arbitrary"), vmem_limit_bytes=VMEM_LIMIT),
    )(a, b, *([comm.src] if comm else []))
    return res if comm else res[0]


PROJ_TM = 256


def _mix_in_fwd(h, g, w_in_t):
    T = h.shape[0]

    def body(i, h_ref, g_ref, w_ref, p_ref, n_ref):
        n = _rms(h_ref[...], g_ref[...]).astype(BF16)
        n_ref[...] = n
        p_ref[...] = _dot_nt(n, w_ref[...])

    return _tiled(body, "mix_in_fwd", T // PROJ_TM, [(h, PROJ_TM, D_MODEL, 0)], [g], [w_in_t],
                  [(T, IN_PROJ_PAD, F32, PROJ_TM), (T, D_MODEL, BF16, PROJ_TM)], [])


def _mix_in_dgrad(h, dh_in, dp_uv, dp_zxd, g, w_in_t):
    T = h.shape[0]

    def body(i, h_ref, dh_ref, duv_ref, dzxd_ref, g_ref, w_ref, o_ref, dg_ref):
        dn = _dot(duv_ref[...], w_ref[:UV_W, :]) + _dot(dzxd_ref[...], w_ref[UV_W:, :])
        _, rms_vjp = jax.vjp(_rms, h_ref[...], g_ref[...])
        dx, dg = rms_vjp(dn)
        o_ref[...] = dh_ref[...] + dx
        dg_ref[...] += dg

    return _tiled(body, "mix_in_dgrad", T // PROJ_TM,
                  [(h, PROJ_TM, D_MODEL, 0), (dh_in, PROJ_TM, D_MODEL, 0), (dp_uv, PROJ_TM, UV_W, 0),
                   (dp_zxd, PROJ_TM, ZXD_W, 0)], [g], [w_in_t],
                  [(T, D_MODEL, F32, PROJ_TM)], [((1, D_MODEL), F32)])


def _out_proj_fwd(h, ya, yb, w_out):
    T = h.shape[0]

    def body(i, h_ref, ya_ref, yb_ref, w_ref, o_ref):
        o_ref[...] = (h_ref[...] + jnp.dot(ya_ref[...], w_ref[:GM_WIDTH, :], preferred_element_type=F32)
                      + jnp.dot(yb_ref[...], w_ref[GM_WIDTH:, :], preferred_element_type=F32))

    return _tiled(body, "out_proj_fwd", T // PROJ_TM,
                  [(h, PROJ_TM, D_MODEL, 0), (ya, PROJ_TM, GM_WIDTH, 0), (yb, PROJ_TM, SSM_WIDTH, 0)], [], [w_out],
                  [(T, D_MODEL, F32, PROJ_TM)], [])[0]


def _out_proj_dgrad(dh, w_out):
    T = dh.shape[0]

    def body(i, dh_ref, w_ref, dya_ref, dyb_ref):
        d = dh_ref[...].astype(BF16)
        dya_ref[...] = _dot_nt(d, w_ref[:GM_WIDTH, :])
        dyb_ref[...] = _dot_nt(d, w_ref[GM_WIDTH:, :])

    return _tiled(body, "out_proj_dgrad", T // PROJ_TM, [(dh, PROJ_TM, D_MODEL, 0)], [], [w_out],
                  [(T, GM_WIDTH, F32, PROJ_TM), (T, SSM_WIDTH, F32, PROJ_TM)], [])


def _gm_chunk(u, v, ln_g, ln_b, b_st, out_g, *w_heads):
    ug = _gelu(u)
    vg = _gelu(v)
    mu = jnp.mean(vg, axis=-1, keepdims=True)
    xc = vg - mu
    vn = xc * lax.rsqrt(jnp.mean(xc * xc, axis=-1, keepdims=True) + EPS) * ln_g + ln_b
    t_idx = lax.broadcasted_iota(jnp.int32, (CHUNK, CHUNK), 0)
    s_idx = lax.broadcasted_iota(jnp.int32, (CHUNK, CHUNK), 1)
    causal = t_idx >= s_idx
    mixed = []
    for hd in range(GM_HEADS):
        wm = jnp.where(causal, w_heads[hd], 0.0)
        cols = slice(hd * GM_HEAD_DIM, (hd + 1) * GM_HEAD_DIM)
        mixed.append(_dot(wm, vn[:, cols]) + b_st[:, hd:hd + 1])
    ya0 = ug * jnp.concatenate(mixed, axis=1)
    return _rms(ya0, out_g)


def _gm_fwd(proj, ln_g, ln_b, w_s, b_st, out_g):
    T = proj.shape[0]

    def body(i, u_ref, v_ref, lg_ref, lb_ref, w_ref, bs_ref, og_ref, ya_ref):
        w_heads = [w_ref[hd] for hd in range(GM_HEADS)]
        ya = _gm_chunk(u_ref[...], v_ref[...], lg_ref[...], lb_ref[...], bs_ref[...], og_ref[...], *w_heads)
        ya_ref[...] = ya.astype(BF16)

    return _tiled(body, "gmlp_fwd", T // CHUNK, [(proj, CHUNK, GM_WIDTH, 0), (proj, CHUNK, GM_WIDTH, 1)],
                  [ln_g, ln_b, w_s, b_st, out_g], [], [(T, GM_WIDTH, BF16, CHUNK)], [])[0]


def _gm_bwd(proj, dya, ln_g, ln_b, w_s, b_st, out_g):
    T = proj.shape[0]

    def body(i, u_ref, v_ref, dy_ref, lg_ref, lb_ref, w_ref, bs_ref, og_ref, duv_ref, dlg_ref, dlb_ref, dw_ref, dbs_ref,
             dog_ref):
        w_heads = [w_ref[hd] for hd in range(GM_HEADS)]
        _, vjp = jax.vjp(_gm_chunk, u_ref[...], v_ref[...], lg_ref[...], lb_ref[...], bs_ref[...], og_ref[...], *w_heads)
        grads = vjp(dy_ref[...])
        duv_ref[:, :GM_WIDTH] = grads[0].astype(BF16)
        duv_ref[:, GM_WIDTH:] = grads[1].astype(BF16)
        dlg_ref[...] += grads[2]
        dlb_ref[...] += grads[3]
        dbs_ref[...] += grads[4]
        dog_ref[...] += grads[5]
        for hd in range(GM_HEADS):
            dw_ref[hd] += grads[6 + hd]

    return _tiled(body, "gmlp_bwd", T // CHUNK,
                  [(proj, CHUNK, GM_WIDTH, 0), (proj, CHUNK, GM_WIDTH, 1), (dya, CHUNK, GM_WIDTH, 0)],
                  [ln_g, ln_b, w_s, b_st, out_g], [], [(T, UV_W, BF16, CHUNK)],
                  [((1, GM_WIDTH), F32), ((1, GM_WIDTH), F32), ((GM_HEADS, CHUNK, CHUNK), F32),
                   ((CHUNK, GM_HEADS), F32), ((1, GM_WIDTH), F32)])


def _ssd_chunk(xc, z, dtr, s_in, dt_bias, a_log, d_skip, norm_g):
    half = SSM_WIDTH // SSM_GROUPS
    l_idx = lax.broadcasted_iota(jnp.int32, (CHUNK, CHUNK), 0)
    s_idx = lax.broadcasted_iota(jnp.int32, (CHUNK, CHUNK), 1)
    causal = l_idx >= s_idx
    tril = causal.astype(F32)
    head_of_col = lax.broadcasted_iota(jnp.int32, (SSM_HEADS, SSM_WIDTH), 1) // SSM_HEAD_DIM
    expand = (head_of_col == lax.broadcasted_iota(jnp.int32, (SSM_HEADS, SSM_WIDTH), 0)).astype(F32)

    xcs = _silu(xc)
    xs = xcs[:, :SSM_WIDTH]
    dt = jax.nn.softplus(dtr + dt_bias)
    adt = dt * (-jnp.exp(a_log))
    acs = _hdot(tril, adt)
    acs_t = _hdot_tn(adt, 1.0 - tril + (l_idx == s_idx).astype(F32))
    tot = acs[CHUNK - 1:CHUNK, :]
    dt_w = _hdot(dt, expand)
    out_decay_w = _hdot(jnp.exp(acs), expand)
    state_decay_w = _hdot(jnp.exp(tot - acs), expand)
    chunk_decay_w = _hdot(jnp.exp(tot), expand)
    d_skip_w = _hdot(d_skip, expand)
    xdt = xs * dt_w
    xdt_decayed = xdt * state_decay_w

    y_diag, y_off, states = [], [], []
    for grp in range(SSM_GROUPS):
        b0 = SSM_WIDTH + grp * SSM_STATE
        c0 = SSM_WIDTH + SSM_GROUPS * SSM_STATE + grp * SSM_STATE
        bm = xcs[:, b0:b0 + SSM_STATE].astype(BF16)
        cm = xcs[:, c0:c0 + SSM_STATE].astype(BF16)
        cb = _dot_nt(cm, bm)
        for k in range(grp * SSM_HEADS // SSM_GROUPS, (grp + 1) * SSM_HEADS // SSM_GROUPS):
            decay = jnp.exp(jnp.where(causal, acs[:, k:k + 1] - acs_t[k:k + 1, :], -jnp.inf))
            y_diag.append(_dot(cb * decay, xdt[:, k * SSM_HEAD_DIM:(k + 1) * SSM_HEAD_DIM]))
        cols = slice(grp * half, (grp + 1) * half)
        states.append(_dot_tn(bm, xdt_decayed[:, cols]))
        y_off.append(_dot(cm, s_in[:, cols]))
    y = jnp.concatenate(y_diag, axis=1) + jnp.concatenate(y_off, axis=1) * out_decay_w + xs * d_skip_w
    s_out = s_in * chunk_decay_w + jnp.concatenate(states, axis=1)
    y = y * _silu(z)
    y3 = y.reshape(CHUNK, SSM_GROUPS, half)
    y3 = y3 * lax.rsqrt(jnp.mean(y3 * y3, axis=-1, keepdims=True) + EPS)
    return y3.reshape(CHUNK, SSM_WIDTH) * norm_g, s_out


def _conv_taps(ext_ref, w, b):
    y = b
    for k in range(SSM_CONV):
        y = y + w[k:k + 1, :] * ext_ref[pl.ds(HALO - (SSM_CONV - 1) + k, CHUNK), :]
    return y


def _ssd_fwd(proj, conv_w, conv_b, dt_bias, a_log, d_skip, norm_g, comm=None):
    T = proj.shape[0]
    n_chunks = T // CHUNK

    def body(i, z_ref, x_ref, dt_ref, cw_ref, cb_ref, dtb_ref, al_ref, dsk_ref, ng_ref, yb_ref, sin_ref, ext_ref, st_ref):
        @pl.when(i == 0)
        def _():
            ext_ref[0:HALO, :] = jnp.zeros((HALO, CONV_DIM), F32)
            st_ref[...] = jnp.zeros(st_ref.shape, F32)

        ext_ref[HALO:, :] = x_ref[...]
        xc = _conv_taps(ext_ref, cw_ref[...], cb_ref[...])
        s_in = st_ref[...]
        yb, s_out = _ssd_chunk(xc, z_ref[...], dt_ref[:, 0:SSM_HEADS], s_in, dtb_ref[...], al_ref[...], dsk_ref[...],
                               ng_ref[...])
        yb_ref[...] = yb.astype(BF16)
        sin_ref[...] = s_in
        st_ref[...] = s_out
        ext_ref[0:HALO, :] = ext_ref[CHUNK:CHUNK + HALO, :]

    z_blk = 2 * GM_WIDTH // SSM_WIDTH
    x_blk = (2 * GM_WIDTH + SSM_WIDTH) // CONV_DIM
    dt_blk = (2 * GM_WIDTH + SSM_WIDTH + CONV_DIM) // LANES
    return _tiled(body, "ssd_fwd", n_chunks,
                  [(proj, CHUNK, SSM_WIDTH, z_blk), (proj, CHUNK, CONV_DIM, x_blk), (proj, CHUNK, LANES, dt_blk)],
                  [conv_w, conv_b, dt_bias, a_log, d_skip, norm_g], [],
                  [(T, SSM_WIDTH, BF16, CHUNK), (n_chunks * SSM_STATE, SSM_WIDTH, F32, SSM_STATE)], [],
                  scratch=[pltpu.VMEM((HALO + CHUNK, CONV_DIM), F32), pltpu.VMEM((SSM_STATE, SSM_WIDTH), F32)], comm=comm)


def _ssd_bwd(proj, dyb, s_all, conv_w, conv_b, dt_bias, a_log, d_skip, norm_g, comm=None):
    T = proj.shape[0]
    n_chunks = T // CHUNK
    z_blk = 2 * GM_WIDTH // SSM_WIDTH
    x_blk = (2 * GM_WIDTH + SSM_WIDTH) // CONV_DIM
    dt_blk = (2 * GM_WIDTH + SSM_WIDTH + CONV_DIM) // LANES
    rows_per_halo = CHUNK // HALO

    def body(i, z_ref, x_ref, halo_ref, dt_ref, dy_ref, sin_ref, cw_ref, cb_ref, dtb_ref, al_ref, dsk_ref, ng_ref,
             dzxd_ref, dcw_ref, dcb_ref, ddtb_ref, dal_ref, ddsk_ref, dng_ref, ext_ref, dext_ref, dst_ref):
        @pl.when(i == n_chunks - 1)
        def _():
            dext_ref[CHUNK:, :] = jnp.zeros((HALO, CONV_DIM), F32)
            dst_ref[...] = jnp.zeros(dst_ref.shape, F32)

        halo = halo_ref[...]
        ext_ref[0:HALO, :] = jnp.where(i == 0, jnp.zeros_like(halo), halo)
        ext_ref[HALO:, :] = x_ref[...]
        cw = cw_ref[...]
        xc = _conv_taps(ext_ref, cw, cb_ref[...])
        _, vjp = jax.vjp(_ssd_chunk, xc, z_ref[...], dt_ref[:, 0:SSM_HEADS], sin_ref[...], dtb_ref[...], al_ref[...],
                         dsk_ref[...], ng_ref[...])
        dxc, dz, ddtr, ds_in, ddtb, dal, ddsk, dng = vjp((dy_ref[...], dst_ref[...]))
        dst_ref[...] = ds_in
        ddtb_ref[...] += ddtb
        dal_ref[...] += dal
        ddsk_ref[...] += ddsk
        dng_ref[...] += dng
        dext_ref[0:CHUNK, :] = dxc
        dx = jnp.zeros((CHUNK, CONV_DIM), F32)
        for k in range(SSM_CONV):
            dx = dx + cw[k:k + 1, :] * dext_ref[pl.ds(SSM_CONV - 1 - k, CHUNK), :]
            dcw_ref[k:k + 1, :] += jnp.sum(dxc * ext_ref[pl.ds(HALO - (SSM_CONV - 1) + k, CHUNK), :], axis=0, keepdims=True)
        dcb_ref[...] += jnp.sum(dxc, axis=0, keepdims=True)
        dext_ref[CHUNK:, :] = dext_ref[0:HALO, :]
        dzxd_ref[:, 0:SSM_WIDTH] = dz.astype(BF16)
        dzxd_ref[:, SSM_WIDTH:SSM_WIDTH + CONV_DIM] = dx.astype(BF16)
        dzxd_ref[:, SSM_WIDTH + CONV_DIM:] = jnp.concatenate(
            [ddtr, jnp.zeros((CHUNK, LANES - SSM_HEADS), F32)], axis=1).astype(BF16)

    def halo_index(step):
        c = n_chunks - 1 - step
        return (jnp.maximum(c * rows_per_halo - 1, 0), x_blk)

    return _tiled(body, "ssd_bwd", n_chunks,
                  [(proj, CHUNK, SSM_WIDTH, z_blk), (proj, CHUNK, CONV_DIM, x_blk), (proj, HALO, CONV_DIM, halo_index),
                   (proj, CHUNK, LANES, dt_blk), (dyb, CHUNK, SSM_WIDTH, 0), (s_all, SSM_STATE, SSM_WIDTH, 0)],
                  [conv_w, conv_b, dt_bias, a_log, d_skip, norm_g], [],
                  [(T, ZXD_W, BF16, CHUNK)],
                  [((SSM_CONV, CONV_DIM), F32), ((1, CONV_DIM), F32), ((1, SSM_HEADS), F32), ((1, SSM_HEADS), F32),
                   ((1, SSM_HEADS), F32), ((1, SSM_WIDTH), F32)],
                  scratch=[pltpu.VMEM((HALO + CHUNK, CONV_DIM), F32), pltpu.VMEM((CHUNK + HALO, CONV_DIM), F32),
                           pltpu.VMEM((SSM_STATE, SSM_WIDTH), F32)],
                  reverse=True, comm=comm)


TAIL_TM = 256


def _tail(h, p, target, ple_norm, w_gate, b_gate, w_proj_t, final_norm):
    T = h.shape[0]

    def head(x, pre, pp, b_g, f_norm, tgt):
        gate = jax.nn.sigmoid(pre + b_g)
        out = _rms(x + gate * pp, f_norm)
        err = out - tgt
        return 0.5 * jnp.sum(jnp.mean(err * err, axis=-1, keepdims=True), axis=0, keepdims=True)

    def body(i, h_ref, p_ref, t_ref, pn_ref, bg_ref, fn_ref, wg_ref, wp_ref, dh_ref, loss_ref, dwg_ref, dwp_ref, dpn_ref,
             dbg_ref, dfn_ref):
        x = h_ref[...]
        n4f, n_vjp = jax.vjp(_rms, x, pn_ref[...])
        n4 = n4f.astype(BF16)
        pre = jnp.dot(n4, wg_ref[...], preferred_element_type=F32)
        p16 = p_ref[...].astype(BF16)
        pp = _dot_nt(p16, wp_ref[...])
        loss, h_vjp = jax.vjp(functools.partial(head, tgt=t_ref[...]), x, pre, pp, bg_ref[...], fn_ref[...])
        dx, dpre, dpp, dbg, dfn = h_vjp(jnp.ones((1, 1), F32))
        dpre16 = dpre.astype(BF16)
        dn4 = _dot_nt(dpre16, wg_ref[...])
        dx2, dpn = n_vjp(dn4)
        dh_ref[...] = dx + dx2
        loss_ref[...] += loss
        dwg_ref[...] += _dot_tn(n4, dpre16)
        dwp_ref[...] += _dot_tn(p16, dpp)
        dpn_ref[...] += dpn
        dbg_ref[...] += dbg
        dfn_ref[...] += dfn

    return _tiled(body, "tail", T // TAIL_TM,
                  [(h, TAIL_TM, D_MODEL, 0), (p, TAIL_TM, D_PLE, 0), (target, TAIL_TM, D_MODEL, 0)],
                  [ple_norm, b_gate, final_norm], [w_gate, w_proj_t],
                  [(T, D_MODEL, F32, TAIL_TM)],
                  [((1, 1), F32), ((D_MODEL, D_MODEL), F32), ((D_PLE, D_MODEL), F32), ((1, D_MODEL), F32),
                   ((1, D_MODEL), F32), ((1, D_MODEL), F32)])


def _gather_phases(x_ref, out_ref, send_sems, recv_sems, local_sem):
    mx, my, mc = lax.axis_index("x"), lax.axis_index("y"), lax.axis_index("c")
    me, sibling = (mx, my, mc), (mx, my, 1 - mc)
    chips = [(1 - mx, my), (mx, 1 - my), (1 - mx, 1 - my)]

    def rows(px, py, pc):
        return out_ref.at[4 * px + 2 * py + pc]

    def copy(k, block, to, src=None):
        return pltpu.make_async_remote_copy(
            src_ref=rows(*block) if src is None else src, dst_ref=rows(*block),
            send_sem=send_sems.at[k], recv_sem=recv_sems.at[k], device_id=to, device_id_type=MESH)

    mine = pltpu.make_async_copy(x_ref, rows(*me), local_sem)
    first = [copy(0, me, sibling, src=x_ref)] + [copy(1 + j, me, (*chip, mc), src=x_ref) for j, chip in enumerate(chips)]
    passed = [copy(4 + j, (*chip, mc), sibling) for j, chip in enumerate(chips)]

    def start():
        mine.start()
        for cp in first:
            cp.start()

    def mid():
        for j, chip in enumerate(chips):
            copy(1 + j, (*chip, mc), me).wait_recv()
            passed[j].start()

    def finish():
        copy(0, sibling, me).wait_recv()
        for j, chip in enumerate(chips):
            copy(4 + j, (*chip, 1 - mc), me).wait_recv()
        for cp in first + passed:
            cp.wait_send()
        mine.wait()

    return start, mid, finish


def _exchange_phases(x_ref, out_ref, send_sems, recv_sems, local_sem):
    mx, my, mc = lax.axis_index("x"), lax.axis_index("y"), lax.axis_index("c")
    me = 4 * mx + 2 * my + mc
    mine = pltpu.make_async_copy(x_ref.at[me], out_ref.at[me], local_sem)
    copies = []
    for k in range(1, N_DEV):
        px = 1 - mx if k & 4 else mx
        py = 1 - my if k & 2 else my
        pc = 1 - mc if k & 1 else mc
        copies.append(pltpu.make_async_remote_copy(
            src_ref=x_ref.at[4 * px + 2 * py + pc], dst_ref=out_ref.at[me], send_sem=send_sems.at[k - 1],
            recv_sem=recv_sems.at[k - 1], device_id=(px, py, pc), device_id_type=MESH))

    def start():
        mine.start()
        for cp in copies:
            cp.start()

    def finish():
        for cp in copies:
            cp.wait_recv()
        for cp in copies:
            cp.wait_send()
        mine.wait()

    return start, lambda: None, finish


def _gather_comm(x):
    return _Comm(_gather_phases, x, jax.ShapeDtypeStruct((N_DEV,) + x.shape, x.dtype))


def _exchange_comm(x):
    return _Comm(_exchange_phases, x, jax.ShapeDtypeStruct(x.shape, x.dtype))


def _comm_alone(comm, name):
    def body(x_ref, out_ref, send_sems, recv_sems, local_sem):
        for phase in comm.phases(x_ref, out_ref, send_sems, recv_sems, local_sem):
            phase()

    return pl.pallas_call(
        body,
        out_shape=comm.dst,
        in_specs=[pl.BlockSpec(memory_space=pl.ANY)],
        out_specs=pl.BlockSpec(memory_space=pl.ANY),
        scratch_shapes=[pltpu.SemaphoreType.DMA((N_DEV - 1,)), pltpu.SemaphoreType.DMA((N_DEV - 1,)), pltpu.SemaphoreType.DMA],
        name=name,
    )(comm.src)


def _sum_parts(p_ref):
    g = p_ref[0].astype(F32)
    for j in range(1, N_DEV):
        g = g + p_ref[j].astype(F32)
    return g


def _adamw_store(g, w_ref, m_ref, v_ref, g_ref, d_ref, nm_ref, nv_ref):
    m_new = ADAM_B1 * m_ref[...] + (1.0 - ADAM_B1) * g
    v_new = ADAM_B2 * v_ref[...] + (1.0 - ADAM_B2) * jnp.square(g)
    m_hat = m_new / (1.0 - ADAM_B1 ** ADAM_STEP)
    v_hat = v_new / (1.0 - ADAM_B2 ** ADAM_STEP)
    g_ref[...] = g
    d_ref[...] = -ADAM_LR * (m_hat / (jnp.sqrt(v_hat) + ADAM_EPS) + ADAM_WD * w_ref[...])
    nm_ref[...] = m_new
    nv_ref[...] = v_new


def _adamw_shard(parts, off, transposed, w, m, v, name, n_tiles=1):
    r, c = w.shape
    tr = r // n_tiles
    if transposed:
        rows = -(-c // BF16_ROWS) * BF16_ROWS
        window = (N_DEV, rows, tr)
    else:
        assert c == PACK_COLS
        window = (N_DEV, tr, PACK_COLS)

    def kern(p_hbm, w_ref, m_ref, v_ref, g_ref, d_ref, nm_ref, nv_ref, buf, sem):
        i = pl.program_id(0)
        if transposed:
            src = p_hbm.at[:, pl.ds(off, rows), pl.ds(pl.multiple_of(i * tr, LANES), tr)]
        else:
            src = p_hbm.at[:, pl.ds(pl.multiple_of(off + i * tr, BF16_ROWS), tr), :]
        cp = pltpu.make_async_copy(src, buf, sem)
        cp.start()
        cp.wait()
        g = _sum_parts(buf)
        if transposed:
            eye = (lax.broadcasted_iota(jnp.int32, (rows, c), 0) == lax.broadcasted_iota(jnp.int32, (rows, c), 1)).astype(F32)
            g = _hdot_tn(g, eye)
        _adamw_store(g, w_ref, m_ref, v_ref, g_ref, d_ref, nm_ref, nv_ref)

    spec = pl.BlockSpec((tr, c), lambda i: (i, 0))
    return pl.pallas_call(
        kern,
        out_shape=[jax.ShapeDtypeStruct((r, c), F32)] * 4,
        grid=(n_tiles,),
        in_specs=[pl.BlockSpec(memory_space=pl.ANY), spec, spec, spec],
        out_specs=[spec] * 4,
        scratch_shapes=[pltpu.VMEM(window, parts.dtype), pltpu.SemaphoreType.DMA],
        name=name,
        compiler_params=pltpu.CompilerParams(dimension_semantics=("arbitrary",), vmem_limit_bytes=VMEM_LIMIT),
    )(parts, w, m, v)


def _sum_adamw(parts, w, m, v, tr, name):
    _, R, C = parts.shape

    def kern(p_ref, w_ref, m_ref, v_ref, g_ref, d_ref, nm_ref, nv_ref):
        _adamw_store(_sum_parts(p_ref), w_ref, m_ref, v_ref, g_ref, d_ref, nm_ref, nv_ref)

    row_spec = pl.BlockSpec((tr, C), lambda i: (i, 0))
    return pl.pallas_call(
        kern,
        out_shape=[jax.ShapeDtypeStruct((R, C), F32)] * 4,
        grid=(R // tr,),
        in_specs=[pl.BlockSpec((N_DEV, tr, C), lambda i: (0, i, 0)), row_spec, row_spec, row_spec],
        out_specs=[row_spec] * 4,
        name=name,
        compiler_params=pltpu.CompilerParams(dimension_semantics=("arbitrary",), vmem_limit_bytes=VMEM_LIMIT),
    )(parts, w, m, v)


FF_SHARD = D_FF // N_DEV
CONV_SHARD = (SSM_CONV, CONV_DIM // N_DEV)
SHARDS = {"ffn1_w_gate": ((D_MODEL, FF_SHARD), True), "ffn1_w_up": ((D_MODEL, FF_SHARD), True),
          "ffn1_w_down": ((FF_SHARD, D_MODEL), False),
          "ffn2_w_gate": ((D_MODEL, FF_SHARD), True), "ffn2_w_up": ((D_MODEL, FF_SHARD), True),
          "ffn2_w_down": ((FF_SHARD, D_MODEL), False),
          "w_out": ((2 * D_MODEL // N_DEV, D_MODEL), False), "ple_w_gate": ((D_MODEL // N_DEV, D_MODEL), False),
          "w_in": ((D_MODEL, IN_PROJ // N_DEV), True), "ple_w_proj": ((D_PLE, D_MODEL // N_DEV), True),
          "conv_w": (CONV_SHARD, True),
          "conv_w_mid": (CONV_SHARD, True), "conv_w_low": (CONV_SHARD, True)}
BIG = tuple(name for name in SHARDS if not name.startswith("conv_w_"))
SMALL = ("ffn1_norm", "mix_norm", "gm_ln_g", "gm_ln_b", "gm_w_s", "gm_b_s", "gm_out_norm", "conv_b", "dt_bias", "a_log",
         "d_skip", "ssm_norm", "ffn2_norm", "ple_norm", "ple_b_gate", "final_norm")
SMALL_ROWS = 144


def _piece_rows(name):
    shape = SHARDS[name][0]
    return -(-(shape[0] * shape[1]) // PACK_COLS)


def _pad_cols(flat, name):
    pad = _piece_rows(name) * PACK_COLS - flat.shape[-1]
    return flat if pad == 0 else jnp.pad(flat, [(0, 0)] * (flat.ndim - 1) + [(0, pad)])


class _Pack:
    def __init__(self, names, tile_rows):
        self.names, self.tile_rows, self.offsets, off = names, tile_rows, {}, 0
        for name in names:
            self.offsets[name] = off
            off += _piece_rows(name)
        self.rows = -(-off // tile_rows) * tile_rows

    def pack_local(self, vals):
        parts = []
        for name in self.names:
            val = vals[name]
            parts.append(_pad_cols((val.T if SHARDS[name][1] else val).reshape(-1), name))
        flat = jnp.concatenate(parts)
        return jnp.pad(flat, (0, self.rows * PACK_COLS - flat.shape[0])).reshape(self.rows, PACK_COLS)

    def unpack_local(self, packed):
        out = {}
        for name in self.names:
            shape, transposed = SHARDS[name]
            piece = packed[self.offsets[name]:self.offsets[name] + _piece_rows(name)].reshape(-1)[:shape[0] * shape[1]]
            out[name] = piece.reshape(shape[::-1]).T if transposed else piece.reshape(shape)
        return out

    def pack_owner_major(self, grads):
        parts = [_pad_cols(grads[name].reshape(N_DEV, -1).astype(BF16), name) for name in self.names]
        flat = jnp.concatenate(parts, axis=1)
        flat = jnp.pad(flat, ((0, 0), (0, self.rows * PACK_COLS - flat.shape[1])))
        return flat.reshape(N_DEV, self.rows, PACK_COLS)

    def gathered_piece(self, gathered, name):
        shape = SHARDS[name][0]
        rows = gathered[:, self.offsets[name]:self.offsets[name] + _piece_rows(name), :]
        return rows.reshape(N_DEV, -1)[:, :shape[0] * shape[1]]

    def pieces(self, gathered, name):
        return _Pieces(gathered, self.offsets[name], _piece_rows(name))


GATHER_FFN1 = _Pack(("ffn1_w_gate", "ffn1_w_up", "ffn1_w_down"), BF16_ROWS)
GATHER_MIX = _Pack(("w_out", "ple_w_gate", "w_in", "ple_w_proj", "conv_w", "conv_w_mid", "conv_w_low"), BF16_ROWS)
GATHER_FFN2 = _Pack(("ffn2_w_gate", "ffn2_w_up", "ffn2_w_down"), BF16_ROWS)
SCATTER_LATE = _Pack(("ffn2_w_gate", "ffn2_w_up", "ffn2_w_down", "w_out", "ple_w_gate", "ple_w_proj"), BF16_ROWS)
SCATTER_IN = _Pack(("w_in", "conv_w"), BF16_ROWS)
SCATTER_GATE = _Pack(("ffn1_w_gate",), BF16_ROWS)
SCATTER_UP = _Pack(("ffn1_w_up",), BF16_ROWS)
SCATTER_DOWN = _Pack(("ffn1_w_down",), BF16_ROWS)


def _pack_small(vals):
    flat = jnp.concatenate([vals[name].reshape(-1).astype(F32) for name in SMALL])
    return jnp.pad(flat, (0, SMALL_ROWS * PACK_COLS - flat.shape[0])).reshape(SMALL_ROWS, PACK_COLS)


def _unpack_small(packed, shapes):
    out, off = {}, 0
    flat = packed.reshape(-1)
    for name in SMALL:
        n = 1
        for s in shapes[name]:
            n *= s
        out[name] = flat[off:off + n].reshape(shapes[name])
        off += n
    return out


WEIGHTS = ("ffn1_norm", "ffn1_w_gate", "ffn1_w_up", "ffn1_w_down", "mix_norm", "w_in", "gm_ln_g", "gm_ln_b", "gm_w_s",
           "gm_b_s", "gm_out_norm", "conv_w", "conv_b", "dt_bias", "a_log", "d_skip", "ssm_norm", "w_out", "ffn2_norm",
           "ffn2_w_gate", "ffn2_w_up", "ffn2_w_down", "ple_norm", "ple_w_gate", "ple_b_gate", "ple_w_proj", "final_norm")


def _step(x, p, target, w, m, v):
    local = lambda d: {name: d[name][0] for name in BIG}

    shards = {name: val.astype(BF16) for name, val in local(w).items()}
    conv_rest = w["conv_w"][0] - shards["conv_w"].astype(F32)
    shards["conv_w_mid"] = conv_rest.astype(BF16)
    shards["conv_w_low"] = (conv_rest - shards["conv_w_mid"].astype(F32)).astype(BF16)
    g_ffn1 = _comm_alone(_gather_comm(GATHER_FFN1.pack_local(shards)), "gather_ffn1")

    row = lambda name: w[name].reshape(1, -1)
    gm_w_s = w["gm_w_s"][0]
    gm_b_st = jnp.transpose(w["gm_b_s"][0])
    ffn1 = (row("ffn1_norm"),) + tuple(GATHER_FFN1.pieces(g_ffn1, name) for name in GATHER_FFN1.names)
    gm = (row("gm_ln_g"), row("gm_ln_b"), gm_w_s, gm_b_st, row("gm_out_norm"))

    h1, g_mix = _ffn_fwd(x, *ffn1, "ffn1_fwd", comm=_gather_comm(GATHER_MIX.pack_local(shards)))
    w_in_t = GATHER_MIX.gathered_piece(g_mix, "w_in").reshape(IN_PROJ, D_MODEL)
    w_in_t = jnp.concatenate([w_in_t, jnp.zeros((IN_PROJ_PAD - IN_PROJ, D_MODEL), BF16)], axis=0)
    w_proj_t = GATHER_MIX.gathered_piece(g_mix, "ple_w_proj").reshape(D_MODEL, D_PLE)
    conv_w = sum(GATHER_MIX.gathered_piece(g_mix, name).astype(F32) for name in ("conv_w", "conv_w_mid", "conv_w_low"))
    conv_w = conv_w.reshape(CONV_DIM, SSM_CONV).T
    ssd = (conv_w, row("conv_b"), row("dt_bias"), row("a_log"), row("d_skip"), row("ssm_norm"))
    w_out = GATHER_MIX.pieces(g_mix, "w_out")

    proj, n2 = _mix_in_fwd(h1, row("mix_norm"), w_in_t)
    ya = _gm_fwd(proj, *gm)
    yb, s_all, g_ffn2 = _ssd_fwd(proj, *ssd, comm=_gather_comm(GATHER_FFN2.pack_local(shards)))
    ffn2 = (row("ffn2_norm"),) + tuple(GATHER_FFN2.pieces(g_ffn2, name) for name in GATHER_FFN2.names)
    h2 = _out_proj_fwd(h1, ya, yb, w_out)
    h3 = _ffn_fwd(h2, *ffn2, "ffn2_fwd")[0]

    g, gp = {}, {}
    dh3, loss, gp["ple_w_gate"], d_w_proj, g["ple_norm"], g["ple_b_gate"], g["final_norm"] = _tail(
        h3, p, target, row("ple_norm"), GATHER_MIX.pieces(g_mix, "ple_w_gate"), row("ple_b_gate"), w_proj_t,
        row("final_norm"))
    gp["ple_w_proj"] = d_w_proj.T

    dh2, n3, s3, da3, db3, g["ffn2_norm"] = _ffn_dgrad(h2, dh3, *ffn2, "ffn2_dgrad")
    gp["ffn2_w_gate"] = _wgrad(n3, da3, 1408, "ffn2_wgrad_gate", transpose_out=True)
    gp["ffn2_w_up"] = _wgrad(n3, db3, 1408, "ffn2_wgrad_up", transpose_out=True)
    gp["ffn2_w_down"] = _wgrad(s3, dh3, 512, "ffn2_wgrad_down", scale=0.5)

    dya, dyb = _out_proj_dgrad(dh2, w_out)
    gp["w_out"] = jnp.concatenate([_wgrad(ya, dh2, 1024, "w_out_wgrad_a"), _wgrad(yb, dh2, 1024, "w_out_wgrad_b")], axis=0)

    dp_zxd, d_conv_w, g["conv_b"], g["dt_bias"], g["a_log"], g["d_skip"], g["ssm_norm"], parts_late = _ssd_bwd(
        proj, dyb, s_all, *ssd, comm=_exchange_comm(SCATTER_LATE.pack_owner_major(gp)))
    gp["conv_w"] = d_conv_w.T
    dp_uv, g["gm_ln_g"], g["gm_ln_b"], g["gm_w_s"], dbst, g["gm_out_norm"] = _gm_bwd(proj, dya, *gm)
    g["gm_b_s"] = jnp.transpose(dbst)

    dh1, g["mix_norm"] = _mix_in_dgrad(h1, dh2, dp_uv, dp_zxd, row("mix_norm"), w_in_t)
    gp["w_in"] = jnp.concatenate([_wgrad(n2, dp_uv, 1024, "w_in_wgrad_uv", transpose_out=True),
                                  _wgrad(n2, dp_zxd, 896, "w_in_wgrad_zxd", transpose_out=True)], axis=0)[:IN_PROJ]

    parts = {}
    dx, n1, s1, da1, db1, g["ffn1_norm"] = _ffn_dgrad(x, dh1, *ffn1, "ffn1_dgrad")
    gp["ffn1_w_gate"], parts[SCATTER_IN] = _wgrad(n1, da1, 1408, "ffn1_wgrad_gate", transpose_out=True,
                                                  comm=_exchange_comm(SCATTER_IN.pack_owner_major(gp)))
    gp["ffn1_w_up"], parts[SCATTER_GATE] = _wgrad(n1, db1, 1408, "ffn1_wgrad_up", transpose_out=True,
                                                  comm=_exchange_comm(SCATTER_GATE.pack_owner_major(gp)))
    gp["ffn1_w_down"], parts[SCATTER_UP] = _wgrad(s1, dh1, 512, "ffn1_wgrad_down", scale=0.5,
                                                  comm=_exchange_comm(SCATTER_UP.pack_owner_major(gp)))
    parts[SCATTER_DOWN] = _comm_alone(_exchange_comm(SCATTER_DOWN.pack_owner_major(gp)), "scatter_ffn1_down")
    parts[SCATTER_LATE] = parts_late

    res_big = {}
    for pack, pack_parts in parts.items():
        for name in pack.names:
            shape, transposed = SHARDS[name]
            wmv = (w[name][0], m[name][0], v[name][0])
            if name in ("ple_w_proj", "conv_w"):
                nat = pack.gathered_piece(pack_parts, name).reshape((N_DEV,) + shape[::-1])
                res_big[name] = _sum_adamw(jnp.transpose(nat, (0, 2, 1)), *wmv, shape[0], "adamw_" + name)
            else:
                res_big[name] = _adamw_shard(pack_parts, pack.offsets[name], transposed, *wmv, "adamw_" + name,
                                             n_tiles=4 if name == "w_in" else 1)

    small_shapes = {name: w[name].shape for name in SMALL}
    small_parts = _comm_alone(_gather_comm(_pack_small(g)), "gather_small_grads")
    res_small = _sum_adamw(small_parts, _pack_small(w), _pack_small(m), _pack_small(v), SMALL_ROWS, "adamw_small")
    res_small = [_unpack_small(r, small_shapes) for r in res_small]

    outs = []
    for k in range(4):
        for name in WEIGHTS:
            if name in res_small[k]:
                outs.append(res_small[k][name])
            else:
                outs.append(res_big[name][k].reshape(w[name].shape))
    return loss[0, 0], dx, outs


def kernel(x, p, ffn1_norm, ffn1_w_gate, ffn1_w_up, ffn1_w_down, mix_norm, w_in, gm_ln_g, gm_ln_b, gm_w_s, gm_b_s, gm_out_norm, conv_w, conv_b, dt_bias, a_log, d_skip, ssm_norm, w_out, ffn2_norm, ffn2_w_gate, ffn2_w_up, ffn2_w_down, ple_norm, ple_w_gate, ple_b_gate, ple_w_proj, final_norm, loss_target, m_ffn1_norm, m_ffn1_w_gate, m_ffn1_w_up, m_ffn1_w_down, m_mix_norm, m_w_in, m_gm_ln_g, m_gm_ln_b, m_gm_w_s, m_gm_b_s, m_gm_out_norm, m_conv_w, m_conv_b, m_dt_bias, m_a_log, m_d_skip, m_ssm_norm, m_w_out, m_ffn2_norm, m_ffn2_w_gate, m_ffn2_w_up, m_ffn2_w_down, m_ple_norm, m_ple_w_gate, m_ple_b_gate, m_ple_w_proj, m_final_norm, v_ffn1_norm, v_ffn1_w_gate, v_ffn1_w_up, v_ffn1_w_down, v_mix_norm, v_w_in, v_gm_ln_g, v_gm_ln_b, v_gm_w_s, v_gm_b_s, v_gm_out_norm, v_conv_w, v_conv_b, v_dt_bias, v_a_log, v_d_skip, v_ssm_norm, v_w_out, v_ffn2_norm, v_ffn2_w_gate, v_ffn2_w_up, v_ffn2_w_down, v_ple_norm, v_ple_w_gate, v_ple_b_gate, v_ple_w_proj, v_final_norm):
    args = locals()
    w = {name: args[name] for name in WEIGHTS}
    m = {name: args["m_" + name] for name in WEIGHTS}
    v = {name: args["v_" + name] for name in WEIGHTS}
    loss, dx, outs = _step(x[0], p[0, 0], loss_target[0], w, m, v)
    loss = lax.psum(loss, AXES)
    return (loss, dx[None], *outs)
```

```python
import functools
from typing import NamedTuple

import jax
import jax.numpy as jnp
from jax import lax
from jax.experimental import pallas as pl
from jax.experimental.pallas import tpu as pltpu

F32 = jnp.float32
BF16 = jnp.bfloat16
HIGHEST = lax.Precision.HIGHEST
MESH = pl.DeviceIdType.MESH
AXES = ("x", "y", "c")
N_DEV = 8

D_MODEL = 1024
D_FF = 2816
D_PLE = 256
GM_WIDTH = 1024
GM_HEADS = 8
GM_HEAD_DIM = 128
CHUNK = 128
SSM_WIDTH = 1024
SSM_HEADS = 16
SSM_HEAD_DIM = 64
SSM_GROUPS = 2
SSM_STATE = 128
SSM_CONV = 4
CONV_DIM = SSM_WIDTH + 2 * SSM_GROUPS * SSM_STATE
IN_PROJ = 2 * GM_WIDTH + SSM_WIDTH + CONV_DIM + SSM_HEADS
LANES = 128
BF16_ROWS = 16
IN_PROJ_PAD = IN_PROJ - SSM_HEADS + LANES
UV_W = 2 * GM_WIDTH
ZXD_W = IN_PROJ_PAD - UV_W
HALO = 8
EPS = 1e-6

ADAM_LR = 0.001
ADAM_B1 = 0.9
ADAM_B2 = 0.999
ADAM_EPS = 1e-08
ADAM_WD = 0.01
ADAM_STEP = 10

VMEM_LIMIT = 56 * 1024 * 1024
PACK_COLS = 1024


def _rms(x, g):
    return x * lax.rsqrt(jnp.mean(x * x, axis=-1, keepdims=True) + EPS) * g


def _gelu(x):
    return 0.5 * x * (1.0 + lax.erf(x * (2.0 ** -0.5)))


def _silu(x):
    return x * jax.nn.sigmoid(x)


def _dot(a, b):
    return jnp.dot(a.astype(BF16), b.astype(BF16), preferred_element_type=F32)


def _dot_nt(a, b):
    return lax.dot_general(a.astype(BF16), b.astype(BF16), (((1,), (1,)), ((), ())), preferred_element_type=F32)


def _dot_tn(a, b):
    return lax.dot_general(a.astype(BF16), b.astype(BF16), (((0,), (0,)), ((), ())), preferred_element_type=F32)


def _hdot_tn(a, b):
    return lax.dot_general(a, b, (((0,), (0,)), ((), ())), precision=HIGHEST, preferred_element_type=F32)


def _split3(x):
    hi = x.astype(BF16)
    rest = x - hi.astype(F32)
    mid = rest.astype(BF16)
    return hi, mid, (rest - mid.astype(F32)).astype(BF16)


def _exact_dot(x, mask, dims, x_first=True):
    terms = [lax.dot_general(*((t, mask) if x_first else (mask, t)), (dims, ((), ())), preferred_element_type=F32)
             for t in _split3(x)]
    return (terms[0] + terms[1]) + terms[2]


def _mask_product(fwd_dims, fwd_x_first, bwd_dims, bwd_x_first):
    @jax.custom_vjp
    def product(x, mask):
        return _exact_dot(x, mask, fwd_dims, fwd_x_first)

    def fwd(x, mask):
        return product(x, mask), mask

    def bwd(mask, g):
        return _exact_dot(g, mask, bwd_dims, bwd_x_first), jnp.zeros_like(mask)

    product.defvjp(fwd, bwd)
    return product


_widen = _mask_product(((1,), (0,)), True, ((1,), (1,)), True)
_cumsum_rows = _mask_product(((1,), (0,)), False, ((0,), (0,)), False)
_cumsum_cols = _mask_product(((0,), (0,)), True, ((1,), (1,)), False)


class _Pieces(NamedTuple):
    gathered: jax.Array
    row_off: int
    rows: int


class _Comm(NamedTuple):
    phases: object
    src: jax.Array
    dst: jax.ShapeDtypeStruct


def _tiled(body, name, n_steps, tiled_in, full_in, big_in, tiled_out, acc_out, scratch=(), reverse=False, comm=None):
    n_t, n_f, n_b, n_to, n_a = len(tiled_in), len(full_in), len(big_in), len(tiled_out), len(acc_out)
    n_c = 1 if comm else 0

    def row(i):
        return n_steps - 1 - i if reverse else i

    in_specs, args = [], []
    for arr, br, bc, cb in tiled_in:
        if callable(cb):
            in_specs.append(pl.BlockSpec((br, bc), cb))
        else:
            in_specs.append(pl.BlockSpec((br, bc), functools.partial(lambda i, cb: (row(i), cb), cb=cb)))
        args.append(arr)
    for arr in full_in:
        in_specs.append(pl.BlockSpec(arr.shape, functools.partial(lambda i, nd: (0,) * nd, nd=arr.ndim)))
        args.append(arr)
    big_shapes, n_copies = [], 0
    for big in big_in:
        in_specs.append(pl.BlockSpec(memory_space=pl.ANY))
        if isinstance(big, _Pieces):
            args.append(big.gathered)
            big_shapes.append(((N_DEV * big.rows, PACK_COLS), big.gathered.dtype))
            n_copies += N_DEV
        else:
            args.append(big)
            big_shapes.append((big.shape, big.dtype))
            n_copies += 1
    if comm:
        in_specs.append(pl.BlockSpec(memory_space=pl.ANY))
        args.append(comm.src)
    out_specs, out_shape = [], []
    for rows, cols, dt, br in tiled_out:
        out_specs.append(pl.BlockSpec((br, cols), lambda i: (row(i), 0)))
        out_shape.append(jax.ShapeDtypeStruct((rows, cols), dt))
    for shp, dt in acc_out:
        out_specs.append(pl.BlockSpec(shp, functools.partial(lambda i, nd: (0,) * nd, nd=len(shp))))
        out_shape.append(jax.ShapeDtypeStruct(shp, dt))
    if comm:
        out_specs.append(pl.BlockSpec(memory_space=pl.ANY))
        out_shape.append(comm.dst)
    scratch_shapes = [pltpu.VMEM(shp, dt) for shp, dt in big_shapes] + list(scratch)
    if n_copies:
        scratch_shapes.append(pltpu.SemaphoreType.DMA((n_copies,)))
    if comm:
        scratch_shapes += [pltpu.SemaphoreType.DMA((N_DEV - 1,)), pltpu.SemaphoreType.DMA((N_DEV - 1,)), pltpu.SemaphoreType.DMA]

    def kern(*refs):
        n_in = n_t + n_f + n_b + n_c
        ins = refs[: n_t + n_f]
        big_hbm = refs[n_t + n_f : n_t + n_f + n_b]
        outs = refs[n_in : n_in + n_to + n_a]
        rest = refs[n_in + n_to + n_a + n_c :]
        big_vmem, scr = rest[:n_b], rest[n_b:]
        if comm:
            scr, comm_sems = scr[:-3], scr[-3:]
            comm_start, comm_mid, comm_finish = comm.phases(refs[n_in - 1], refs[n_in + n_to + n_a], *comm_sems)
        if n_copies:
            scr, copy_sems = scr[:-1], scr[-1]
        step = pl.program_id(0)

        @pl.when(step == 0)
        def _():
            copies = []
            for big, src, dst in zip(big_in, big_hbm, big_vmem):
                if isinstance(big, _Pieces):
                    for j in range(N_DEV):
                        copies.append((src.at[j, pl.ds(big.row_off, big.rows), :], dst.at[pl.ds(j * big.rows, big.rows), :]))
                else:
                    copies.append((src, dst))
            copies = [pltpu.make_async_copy(a, b, copy_sems.at[k]) for k, (a, b) in enumerate(copies)]
            for cp in copies:
                cp.start()
            for cp in copies:
                cp.wait()
            for acc in outs[n_to:]:
                acc[...] = jnp.zeros(acc.shape, acc.dtype)
            if comm:
                comm_start()

        body(row(step), *ins, *big_vmem, *outs, *scr)
        if comm:
            pl.when(step == (n_steps - 1) // 2)(comm_mid)
            pl.when(step == n_steps - 1)(comm_finish)

    res = pl.pallas_call(
        kern,
        out_shape=out_shape,
        grid=(n_steps,),
        in_specs=in_specs,
        out_specs=out_specs,
        scratch_shapes=scratch_shapes,
        name=name,
        compiler_params=pltpu.CompilerParams(dimension_semantics=("arbitrary",), vmem_limit_bytes=VMEM_LIMIT),
    )(*args)
    return res


FF_CHUNKS = ((0, 1536), (1536, D_FF))
FFN_TM = 256
FFN_FWD_TM = 512


def _ffn_fwd(h, g, wg_t, wu_t, wd, name, comm=None):
    T = h.shape[0]

    def body(i, h_ref, g_ref, wg_ref, wu_ref, wd_ref, o_ref):
        x = h_ref[...]
        n = _rms(x, g_ref[...]).astype(BF16)
        f = jnp.zeros(x.shape, F32)
        for lo, hi in FF_CHUNKS:
            a = _dot_nt(n, wg_ref[lo:hi, :])
            b = _dot_nt(n, wu_ref[lo:hi, :])
            s = (_silu(a) * b).astype(BF16)
            f = f + jnp.dot(s, wd_ref[lo:hi, :], preferred_element_type=F32)
        o_ref[...] = x + 0.5 * f

    return _tiled(body, name, T // FFN_FWD_TM, [(h, FFN_FWD_TM, D_MODEL, 0)], [g], [wg_t, wu_t, wd],
                  [(T, D_MODEL, F32, FFN_FWD_TM)], [], comm=comm)


def _ffn_dgrad(h, dout, g, wg_t, wu_t, wd, name):
    T = h.shape[0]

    def body(i, h_ref, do_ref, g_ref, wg_ref, wu_ref, wd_ref, dh_ref, n_ref, s_ref, da_ref, db_ref, dg_ref):
        x = h_ref[...]
        dout = do_ref[...]
        nf, rms_vjp = jax.vjp(_rms, x, g_ref[...])
        n = nf.astype(BF16)
        dfo = (0.5 * dout).astype(BF16)
        dn = jnp.zeros(x.shape, F32)
        for lo, hi in FF_CHUNKS:
            a = _dot_nt(n, wg_ref[lo:hi, :])
            b = _dot_nt(n, wu_ref[lo:hi, :])
            sg = jax.nn.sigmoid(a)
            sl = a * sg
            ds = _dot_nt(dfo, wd_ref[lo:hi, :])
            db = (ds * sl).astype(BF16)
            da = (ds * b * (sg * (1.0 + a * (1.0 - sg)))).astype(BF16)
            dn = dn + _dot(da, wg_ref[lo:hi, :]) + _dot(db, wu_ref[lo:hi, :])
            s_ref[:, lo:hi] = (sl * b).astype(BF16)
            da_ref[:, lo:hi] = da
            db_ref[:, lo:hi] = db
        dx, dg = rms_vjp(dn)
        dh_ref[...] = dout + dx
        n_ref[...] = n
        dg_ref[...] += dg

    return _tiled(body, name, T // FFN_TM, [(h, FFN_TM, D_MODEL, 0), (dout, FFN_TM, D_MODEL, 0)], [g], [wg_t, wu_t, wd],
                  [(T, D_MODEL, F32, FFN_TM), (T, D_MODEL, BF16, FFN_TM), (T, D_FF, BF16, FFN_TM),
                   (T, D_FF, BF16, FFN_TM), (T, D_FF, BF16, FFN_TM)], [((1, D_MODEL), F32)])


def _wgrad(a, b, bn, name, scale=None, transpose_out=False, bk=2048, comm=None):
    T, M = a.shape
    N = b.shape[1]
    bk = min(bk, T)
    assert M % LANES == 0 and N % bn == 0 and T % bk == 0
    n_j, n_k = N // bn, T // bk
    n_c = 1 if comm else 0

    def kern(*refs):
        a_ref, b_ref, o_ref, acc_ref = refs[0], refs[1], refs[2 + n_c], refs[3 + 2 * n_c]
        j, k = pl.program_id(0), pl.program_id(1)
        if comm:
            comm_start, _, comm_finish = comm.phases(refs[2], refs[4], *refs[6:])
            pl.when((j == 0) & (k == 0))(comm_start)

        @pl.when(k == 0)
        def _():
            acc_ref[...] = jnp.zeros(acc_ref.shape, F32)

        bv = b_ref[...]
        if scale is not None:
            bv = bv * scale
        acc_ref[...] += _dot_tn(a_ref[...], bv)

        @pl.when(k == n_k - 1)
        def _():
            acc = acc_ref[...]
            o_ref[...] = (acc.T if transpose_out else acc).astype(BF16)

        if comm:
            pl.when((j == n_j - 1) & (k == n_k - 1))(comm_finish)

    if transpose_out:
        out_shape, out_spec = (N, M), pl.BlockSpec((bn, M), lambda j, k: (j, 0))
    else:
        out_shape, out_spec = (M, N), pl.BlockSpec((M, bn), lambda j, k: (0, j))
    any_spec = pl.BlockSpec(memory_space=pl.ANY)
    comm_sems = [pltpu.SemaphoreType.DMA((N_DEV - 1,)), pltpu.SemaphoreType.DMA((N_DEV - 1,)), pltpu.SemaphoreType.DMA]
    res = pl.pallas_call(
        kern,
        out_shape=[jax.ShapeDtypeStruct(out_shape, BF16)] + ([comm.dst] if comm else []),
        grid=(n_j, n_k),
        in_specs=[pl.BlockSpec((bk, M), lambda j, k: (k, 0)), pl.BlockSpec((bk, bn), lambda j, k: (k, j))] + [any_spec] * n_c,
        out_specs=[out_spec] + [any_spec] * n_c,
        scratch_shapes=[pltpu.VMEM((M, bn), F32)] + (comm_sems if comm else []),
        name=name,
        compiler_params=pltpu.CompilerParams(dimension_semantics=("arbitrary", "arbitrary"), vmem_limit_bytes=VMEM_LIMIT),
    )(a, b, *([comm.src] if comm else []))
    return res if comm else res[0]


PROJ_TM = 256


def _mix_in_fwd(h, g, w_in_t):
    T = h.shape[0]

    def body(i, h_ref, g_ref, w_ref, p_ref, n_ref):
        n = _rms(h_ref[...], g_ref[...]).astype(BF16)
        n_ref[...] = n
        p_ref[...] = _dot_nt(n, w_ref[...])

    return _tiled(body, "mix_in_fwd", T // PROJ_TM, [(h, PROJ_TM, D_MODEL, 0)], [g], [w_in_t],
                  [(T, IN_PROJ_PAD, F32, PROJ_TM), (T, D_MODEL, BF16, PROJ_TM)], [])


def _mix_in_dgrad(h, dh_in, dp_uv, dp_zxd, g, w_in_t, comm=None):
    T = h.shape[0]

    def body(i, h_ref, dh_ref, duv_ref, dzxd_ref, g_ref, w_ref, o_ref, dg_ref):
        dn = _dot(duv_ref[...], w_ref[:UV_W, :]) + _dot(dzxd_ref[...], w_ref[UV_W:, :])
        _, rms_vjp = jax.vjp(_rms, h_ref[...], g_ref[...])
        dx, dg = rms_vjp(dn)
        o_ref[...] = dh_ref[...] + dx
        dg_ref[...] += dg

    return _tiled(body, "mix_in_dgrad", T // PROJ_TM,
                  [(h, PROJ_TM, D_MODEL, 0), (dh_in, PROJ_TM, D_MODEL, 0), (dp_uv, PROJ_TM, UV_W, 0),
                   (dp_zxd, PROJ_TM, ZXD_W, 0)], [g], [w_in_t],
                  [(T, D_MODEL, F32, PROJ_TM)], [((1, D_MODEL), F32)], comm=comm)


def _out_proj_fwd(h, ya, yb, w_out):
    T = h.shape[0]

    def body(i, h_ref, ya_ref, yb_ref, w_ref, o_ref):
        o_ref[...] = (h_ref[...] + jnp.dot(ya_ref[...], w_ref[:GM_WIDTH, :], preferred_element_type=F32)
                      + jnp.dot(yb_ref[...], w_ref[GM_WIDTH:, :], preferred_element_type=F32))

    return _tiled(body, "out_proj_fwd", T // PROJ_TM,
                  [(h, PROJ_TM, D_MODEL, 0), (ya, PROJ_TM, GM_WIDTH, 0), (yb, PROJ_TM, SSM_WIDTH, 0)], [], [w_out],
                  [(T, D_MODEL, F32, PROJ_TM)], [])[0]


def _out_proj_dgrad(dh, w_out):
    T = dh.shape[0]

    def body(i, dh_ref, w_ref, dya_ref, dyb_ref):
        d = dh_ref[...].astype(BF16)
        dya_ref[...] = _dot_nt(d, w_ref[:GM_WIDTH, :])
        dyb_ref[...] = _dot_nt(d, w_ref[GM_WIDTH:, :])

    return _tiled(body, "out_proj_dgrad", T // PROJ_TM, [(dh, PROJ_TM, D_MODEL, 0)], [], [w_out],
                  [(T, GM_WIDTH, F32, PROJ_TM), (T, SSM_WIDTH, F32, PROJ_TM)], [])


def _gm_chunk(u, v, ln_g, ln_b, b_st, out_g, *w_heads):
    ug = _gelu(u)
    vg = _gelu(v)
    mu = jnp.mean(vg, axis=-1, keepdims=True)
    xc = vg - mu
    vn = xc * lax.rsqrt(jnp.mean(xc * xc, axis=-1, keepdims=True) + EPS) * ln_g + ln_b
    t_idx = lax.broadcasted_iota(jnp.int32, (CHUNK, CHUNK), 0)
    s_idx = lax.broadcasted_iota(jnp.int32, (CHUNK, CHUNK), 1)
    causal = t_idx >= s_idx
    mixed = []
    for hd in range(GM_HEADS):
        wm = jnp.where(causal, w_heads[hd], 0.0)
        cols = slice(hd * GM_HEAD_DIM, (hd + 1) * GM_HEAD_DIM)
        mixed.append(_dot(wm, vn[:, cols]) + b_st[:, hd:hd + 1])
    ya0 = ug * jnp.concatenate(mixed, axis=1)
    return _rms(ya0, out_g)


def _gm_fwd(proj, ln_g, ln_b, w_s, b_st, out_g):
    T = proj.shape[0]

    def body(i, u_ref, v_ref, lg_ref, lb_ref, w_ref, bs_ref, og_ref, ya_ref):
        w_heads = [w_ref[hd] for hd in range(GM_HEADS)]
        ya = _gm_chunk(u_ref[...], v_ref[...], lg_ref[...], lb_ref[...], bs_ref[...], og_ref[...], *w_heads)
        ya_ref[...] = ya.astype(BF16)

    return _tiled(body, "gmlp_fwd", T // CHUNK, [(proj, CHUNK, GM_WIDTH, 0), (proj, CHUNK, GM_WIDTH, 1)],
                  [ln_g, ln_b, w_s, b_st, out_g], [], [(T, GM_WIDTH, BF16, CHUNK)], [])[0]


def _gm_bwd(proj, dya, ln_g, ln_b, w_s, b_st, out_g):
    T = proj.shape[0]

    def body(i, u_ref, v_ref, dy_ref, lg_ref, lb_ref, w_ref, bs_ref, og_ref, duv_ref, dlg_ref, dlb_ref, dw_ref, dbs_ref,
             dog_ref):
        w_heads = [w_ref[hd] for hd in range(GM_HEADS)]
        _, vjp = jax.vjp(_gm_chunk, u_ref[...], v_ref[...], lg_ref[...], lb_ref[...], bs_ref[...], og_ref[...], *w_heads)
        grads = vjp(dy_ref[...])
        duv_ref[:, :GM_WIDTH] = grads[0].astype(BF16)
        duv_ref[:, GM_WIDTH:] = grads[1].astype(BF16)
        dlg_ref[...] += grads[2]
        dlb_ref[...] += grads[3]
        dbs_ref[...] += grads[4]
        dog_ref[...] += grads[5]
        for hd in range(GM_HEADS):
            dw_ref[hd] += grads[6 + hd]

    return _tiled(body, "gmlp_bwd", T // CHUNK,
                  [(proj, CHUNK, GM_WIDTH, 0), (proj, CHUNK, GM_WIDTH, 1), (dya, CHUNK, GM_WIDTH, 0)],
                  [ln_g, ln_b, w_s, b_st, out_g], [], [(T, UV_W, BF16, CHUNK)],
                  [((1, GM_WIDTH), F32), ((1, GM_WIDTH), F32), ((GM_HEADS, CHUNK, CHUNK), F32),
                   ((CHUNK, GM_HEADS), F32), ((1, GM_WIDTH), F32)])


def _ssd_chunk(xc, z, dtr, s_in, dt_bias, a_log, d_skip, norm_g):
    half = SSM_WIDTH // SSM_GROUPS
    l_idx = lax.broadcasted_iota(jnp.int32, (CHUNK, CHUNK), 0)
    s_idx = lax.broadcasted_iota(jnp.int32, (CHUNK, CHUNK), 1)
    causal = l_idx >= s_idx
    head_of_col = lax.broadcasted_iota(jnp.int32, (SSM_HEADS, SSM_WIDTH), 1) // SSM_HEAD_DIM
    expand = (head_of_col == lax.broadcasted_iota(jnp.int32, (SSM_HEADS, SSM_WIDTH), 0)).astype(BF16)

    xcs = _silu(xc)
    xs = xcs[:, :SSM_WIDTH]
    dt = jax.nn.softplus(dtr + dt_bias)
    adt = dt * (-jnp.exp(a_log))
    acs = _cumsum_rows(adt, causal.astype(BF16))
    acs_t = _cumsum_cols(adt, (l_idx <= s_idx).astype(BF16))
    tot = acs[CHUNK - 1:CHUNK, :]
    dt_w = _widen(dt, expand)
    out_decay_w = _widen(jnp.exp(acs), expand)
    state_decay_w = _widen(jnp.exp(tot - acs), expand)
    chunk_decay_w = _widen(jnp.exp(tot), expand)
    d_skip_w = _widen(d_skip, expand)
    xdt = xs * dt_w
    xdt_decayed = xdt * state_decay_w

    y_diag, y_off, states = [], [], []
    for grp in range(SSM_GROUPS):
        b0 = SSM_WIDTH + grp * SSM_STATE
        c0 = SSM_WIDTH + SSM_GROUPS * SSM_STATE + grp * SSM_STATE
        bm = xcs[:, b0:b0 + SSM_STATE].astype(BF16)
        cm = xcs[:, c0:c0 + SSM_STATE].astype(BF16)
        cb = _dot_nt(cm, bm)
        for k in range(grp * SSM_HEADS // SSM_GROUPS, (grp + 1) * SSM_HEADS // SSM_GROUPS):
            decay = jnp.exp(jnp.where(causal, acs[:, k:k + 1] - acs_t[k:k + 1, :], -jnp.inf))
            y_diag.append(_dot(cb * decay, xdt[:, k * SSM_HEAD_DIM:(k + 1) * SSM_HEAD_DIM]))
        cols = slice(grp * half, (grp + 1) * half)
        states.append(_dot_tn(bm, xdt_decayed[:, cols]))
        y_off.append(_dot(cm, s_in[:, cols]))
    y = jnp.concatenate(y_diag, axis=1) + jnp.concatenate(y_off, axis=1) * out_decay_w + xs * d_skip_w
    s_out = s_in * chunk_decay_w + jnp.concatenate(states, axis=1)
    y = y * _silu(z)
    normed = []
    for grp in range(SSM_GROUPS):
        yg = y[:, grp * half:(grp + 1) * half]
        normed.append(yg * lax.rsqrt(jnp.mean(yg * yg, axis=-1, keepdims=True) + EPS))
    return jnp.concatenate(normed, axis=1) * norm_g, s_out


def _conv_taps(ext_ref, w, b):
    taps = [ext_ref[pl.ds(HALO - (SSM_CONV - 1) + k, CHUNK), :] for k in range(SSM_CONV)]
    y = b
    for k in range(SSM_CONV):
        y = y + w[k:k + 1, :] * taps[k]
    return y, taps


def _ssd_fwd(proj, conv_w, conv_b, dt_bias, a_log, d_skip, norm_g, comm=None):
    T = proj.shape[0]
    n_chunks = T // CHUNK

    def body(i, z_ref, x_ref, dt_ref, cw_ref, cb_ref, dtb_ref, al_ref, dsk_ref, ng_ref, yb_ref, sin_ref, ext_ref, st_ref):
        @pl.when(i == 0)
        def _():
            ext_ref[0:HALO, :] = jnp.zeros((HALO, CONV_DIM), F32)
            st_ref[...] = jnp.zeros(st_ref.shape, F32)

        ext_ref[HALO:, :] = x_ref[...]
        xc, _ = _conv_taps(ext_ref, cw_ref[...], cb_ref[...])
        s_in = st_ref[...]
        yb, s_out = _ssd_chunk(xc, z_ref[...], dt_ref[:, 0:SSM_HEADS], s_in, dtb_ref[...], al_ref[...], dsk_ref[...],
                               ng_ref[...])
        yb_ref[...] = yb.astype(BF16)
        sin_ref[...] = s_in
        st_ref[...] = s_out
        ext_ref[0:HALO, :] = ext_ref[CHUNK:CHUNK + HALO, :]

    z_blk = 2 * GM_WIDTH // SSM_WIDTH
    x_blk = (2 * GM_WIDTH + SSM_WIDTH) // CONV_DIM
    dt_blk = (2 * GM_WIDTH + SSM_WIDTH + CONV_DIM) // LANES
    return _tiled(body, "ssd_fwd", n_chunks,
                  [(proj, CHUNK, SSM_WIDTH, z_blk), (proj, CHUNK, CONV_DIM, x_blk), (proj, CHUNK, LANES, dt_blk)],
                  [conv_w, conv_b, dt_bias, a_log, d_skip, norm_g], [],
                  [(T, SSM_WIDTH, BF16, CHUNK), (n_chunks * SSM_STATE, SSM_WIDTH, F32, SSM_STATE)], [],
                  scratch=[pltpu.VMEM((HALO + CHUNK, CONV_DIM), F32), pltpu.VMEM((SSM_STATE, SSM_WIDTH), F32)], comm=comm)


def _ssd_bwd(proj, dyb, s_all, conv_w, conv_b, dt_bias, a_log, d_skip, norm_g, comm=None):
    T = proj.shape[0]
    n_chunks = T // CHUNK
    z_blk = 2 * GM_WIDTH // SSM_WIDTH
    x_blk = (2 * GM_WIDTH + SSM_WIDTH) // CONV_DIM
    dt_blk = (2 * GM_WIDTH + SSM_WIDTH + CONV_DIM) // LANES
    rows_per_halo = CHUNK // HALO

    def body(i, z_ref, x_ref, halo_ref, dt_ref, dy_ref, sin_ref, cw_ref, cb_ref, dtb_ref, al_ref, dsk_ref, ng_ref,
             dzxd_ref, dcw_ref, dcb_ref, ddtb_ref, dal_ref, ddsk_ref, dng_ref, ext_ref, dext_ref, dst_ref):
        @pl.when(i == n_chunks - 1)
        def _():
            dext_ref[CHUNK:, :] = jnp.zeros((HALO, CONV_DIM), F32)
            dst_ref[...] = jnp.zeros(dst_ref.shape, F32)

        halo = halo_ref[...]
        ext_ref[0:HALO, :] = jnp.where(i == 0, jnp.zeros_like(halo), halo)
        ext_ref[HALO:, :] = x_ref[...]
        cw = cw_ref[...]
        xc, taps = _conv_taps(ext_ref, cw, cb_ref[...])
        _, vjp = jax.vjp(_ssd_chunk, xc, z_ref[...], dt_ref[:, 0:SSM_HEADS], sin_ref[...], dtb_ref[...], al_ref[...],
                         dsk_ref[...], ng_ref[...])
        dxc, dz, ddtr, ds_in, ddtb, dal, ddsk, dng = vjp((dy_ref[...], dst_ref[...]))
        dst_ref[...] = ds_in
        ddtb_ref[...] += ddtb
        dal_ref[...] += dal
        ddsk_ref[...] += ddsk
        dng_ref[...] += dng
        dext_ref[0:CHUNK, :] = dxc
        dx = jnp.zeros((CHUNK, CONV_DIM), F32)
        for k in range(SSM_CONV):
            dx = dx + cw[k:k + 1, :] * dext_ref[pl.ds(SSM_CONV - 1 - k, CHUNK), :]
            dcw_ref[k:k + 1, :] += jnp.sum(dxc * taps[k], axis=0, keepdims=True)
        dcb_ref[...] += jnp.sum(dxc, axis=0, keepdims=True)
        dext_ref[CHUNK:, :] = dext_ref[0:HALO, :]
        dzxd_ref[:, 0:SSM_WIDTH] = dz.astype(BF16)
        dzxd_ref[:, SSM_WIDTH:SSM_WIDTH + CONV_DIM] = dx.astype(BF16)
        dzxd_ref[:, SSM_WIDTH + CONV_DIM:] = jnp.concatenate(
            [ddtr, jnp.zeros((CHUNK, LANES - SSM_HEADS), F32)], axis=1).astype(BF16)

    def halo_index(step):
        c = n_chunks - 1 - step
        return (jnp.maximum(c * rows_per_halo - 1, 0), x_blk)

    return _tiled(body, "ssd_bwd", n_chunks,
                  [(proj, CHUNK, SSM_WIDTH, z_blk), (proj, CHUNK, CONV_DIM, x_blk), (proj, HALO, CONV_DIM, halo_index),
                   (proj, CHUNK, LANES, dt_blk), (dyb, CHUNK, SSM_WIDTH, 0), (s_all, SSM_STATE, SSM_WIDTH, 0)],
                  [conv_w, conv_b, dt_bias, a_log, d_skip, norm_g], [],
                  [(T, ZXD_W, BF16, CHUNK)],
                  [((SSM_CONV, CONV_DIM), F32), ((1, CONV_DIM), F32), ((1, SSM_HEADS), F32), ((1, SSM_HEADS), F32),
                   ((1, SSM_HEADS), F32), ((1, SSM_WIDTH), F32)],
                  scratch=[pltpu.VMEM((HALO + CHUNK, CONV_DIM), F32), pltpu.VMEM((CHUNK + HALO, CONV_DIM), F32),
                           pltpu.VMEM((SSM_STATE, SSM_WIDTH), F32)],
                  reverse=True, comm=comm)


TAIL_TM = 512


def _tail(h, p, target, ple_norm, w_gate, b_gate, w_proj_t, final_norm):
    T = h.shape[0]

    def head(x, pre, pp, b_g, f_norm, tgt):
        gate = jax.nn.sigmoid(pre + b_g)
        out = _rms(x + gate * pp, f_norm)
        err = out - tgt
        return 0.5 * jnp.sum(jnp.mean(err * err, axis=-1, keepdims=True), axis=0, keepdims=True)

    def body(i, h_ref, p_ref, t_ref, pn_ref, bg_ref, fn_ref, wg_ref, wp_ref, dh_ref, loss_ref, dwg_ref, dwp_ref, dpn_ref,
             dbg_ref, dfn_ref):
        x = h_ref[...]
        n4f, n_vjp = jax.vjp(_rms, x, pn_ref[...])
        n4 = n4f.astype(BF16)
        pre = jnp.dot(n4, wg_ref[...], preferred_element_type=F32)
        p16 = p_ref[...].astype(BF16)
        pp = _dot_nt(p16, wp_ref[...])
        loss, h_vjp = jax.vjp(functools.partial(head, tgt=t_ref[...]), x, pre, pp, bg_ref[...], fn_ref[...])
        dx, dpre, dpp, dbg, dfn = h_vjp(jnp.ones((1, 1), F32))
        dpre16 = dpre.astype(BF16)
        dn4 = _dot_nt(dpre16, wg_ref[...])
        dx2, dpn = n_vjp(dn4)
        dh_ref[...] = dx + dx2
        loss_ref[...] += loss
        dwg_ref[...] += _dot_tn(n4, dpre16)
        dwp_ref[...] += _dot_tn(p16, dpp)
        dpn_ref[...] += dpn
        dbg_ref[...] += dbg
        dfn_ref[...] += dfn

    return _tiled(body, "tail", T // TAIL_TM,
                  [(h, TAIL_TM, D_MODEL, 0), (p, TAIL_TM, D_PLE, 0), (target, TAIL_TM, D_MODEL, 0)],
                  [ple_norm, b_gate, final_norm], [w_gate, w_proj_t],
                  [(T, D_MODEL, F32, TAIL_TM)],
                  [((1, 1), F32), ((D_MODEL, D_MODEL), F32), ((D_PLE, D_MODEL), F32), ((1, D_MODEL), F32),
                   ((1, D_MODEL), F32), ((1, D_MODEL), F32)])


def _gather_phases(x_ref, out_ref, send_sems, recv_sems, local_sem):
    mx, my, mc = lax.axis_index("x"), lax.axis_index("y"), lax.axis_index("c")
    me, sibling = (mx, my, mc), (mx, my, 1 - mc)
    chips = [(1 - mx, my), (mx, 1 - my), (1 - mx, 1 - my)]

    def rows(px, py, pc):
        return out_ref.at[4 * px + 2 * py + pc]

    def copy(k, block, to, src=None):
        return pltpu.make_async_remote_copy(
            src_ref=rows(*block) if src is None else src, dst_ref=rows(*block),
            send_sem=send_sems.at[k], recv_sem=recv_sems.at[k], device_id=to, device_id_type=MESH)

    mine = pltpu.make_async_copy(x_ref, rows(*me), local_sem)
    first = [copy(0, me, sibling, src=x_ref)] + [copy(1 + j, me, (*chip, mc), src=x_ref) for j, chip in enumerate(chips)]
    passed = [copy(4 + j, (*chip, mc), sibling) for j, chip in enumerate(chips)]

    def start():
        mine.start()
        for cp in first:
            cp.start()

    def mid():
        for j, chip in enumerate(chips):
            copy(1 + j, (*chip, mc), me).wait_recv()
            passed[j].start()

    def finish():
        copy(0, sibling, me).wait_recv()
        for j, chip in enumerate(chips):
            copy(4 + j, (*chip, 1 - mc), me).wait_recv()
        for cp in first + passed:
            cp.wait_send()
        mine.wait()

    return start, mid, finish


def _exchange_phases(x_ref, out_ref, send_sems, recv_sems, local_sem):
    mx, my, mc = lax.axis_index("x"), lax.axis_index("y"), lax.axis_index("c")
    me = 4 * mx + 2 * my + mc
    mine = pltpu.make_async_copy(x_ref.at[me], out_ref.at[me], local_sem)
    copies = []
    for k in range(1, N_DEV):
        px = 1 - mx if k & 4 else mx
        py = 1 - my if k & 2 else my
        pc = 1 - mc if k & 1 else mc
        copies.append(pltpu.make_async_remote_copy(
            src_ref=x_ref.at[4 * px + 2 * py + pc], dst_ref=out_ref.at[me], send_sem=send_sems.at[k - 1],
            recv_sem=recv_sems.at[k - 1], device_id=(px, py, pc), device_id_type=MESH))

    def start():
        mine.start()
        for cp in copies:
            cp.start()

    def finish():
        for cp in copies:
            cp.wait_recv()
        for cp in copies:
            cp.wait_send()
        mine.wait()

    return start, lambda: None, finish


def _gather_comm(x):
    return _Comm(_gather_phases, x, jax.ShapeDtypeStruct((N_DEV,) + x.shape, x.dtype))


def _exchange_comm(x):
    return _Comm(_exchange_phases, x, jax.ShapeDtypeStruct(x.shape, x.dtype))


def _comm_alone(comm, name):
    def body(x_ref, out_ref, send_sems, recv_sems, local_sem):
        for phase in comm.phases(x_ref, out_ref, send_sems, recv_sems, local_sem):
            phase()

    return pl.pallas_call(
        body,
        out_shape=comm.dst,
        in_specs=[pl.BlockSpec(memory_space=pl.ANY)],
        out_specs=pl.BlockSpec(memory_space=pl.ANY),
        scratch_shapes=[pltpu.SemaphoreType.DMA((N_DEV - 1,)), pltpu.SemaphoreType.DMA((N_DEV - 1,)), pltpu.SemaphoreType.DMA],
        name=name,
    )(comm.src)


def _sum_parts(p_ref):
    g = p_ref[0].astype(F32)
    for j in range(1, N_DEV):
        g = g + p_ref[j].astype(F32)
    return g


def _adamw_store(g, w_ref, m_ref, v_ref, g_ref, d_ref, nm_ref, nv_ref):
    m_new = ADAM_B1 * m_ref[...] + (1.0 - ADAM_B1) * g
    v_new = ADAM_B2 * v_ref[...] + (1.0 - ADAM_B2) * jnp.square(g)
    m_hat = m_new / (1.0 - ADAM_B1 ** ADAM_STEP)
    v_hat = v_new / (1.0 - ADAM_B2 ** ADAM_STEP)
    g_ref[...] = g
    d_ref[...] = -ADAM_LR * (m_hat / (jnp.sqrt(v_hat) + ADAM_EPS) + ADAM_WD * w_ref[...])
    nm_ref[...] = m_new
    nv_ref[...] = v_new


def _adamw_shard(parts, off, transposed, w, m, v, name, n_tiles=1):
    r, c = w.shape
    tr = r // n_tiles
    if transposed:
        rows = -(-c // BF16_ROWS) * BF16_ROWS
        window = (N_DEV, rows, tr)
    else:
        assert c == PACK_COLS
        window = (N_DEV, tr, PACK_COLS)

    def kern(p_hbm, w_ref, m_ref, v_ref, g_ref, d_ref, nm_ref, nv_ref, buf, sem):
        i = pl.program_id(0)
        if transposed:
            src = p_hbm.at[:, pl.ds(off, rows), pl.ds(pl.multiple_of(i * tr, LANES), tr)]
        else:
            src = p_hbm.at[:, pl.ds(pl.multiple_of(off + i * tr, BF16_ROWS), tr), :]
        cp = pltpu.make_async_copy(src, buf, sem)
        cp.start()
        cp.wait()
        g = _sum_parts(buf)
        if transposed:
            eye = (lax.broadcasted_iota(jnp.int32, (rows, c), 0) == lax.broadcasted_iota(jnp.int32, (rows, c), 1)).astype(F32)
            g = _hdot_tn(g, eye)
        _adamw_store(g, w_ref, m_ref, v_ref, g_ref, d_ref, nm_ref, nv_ref)

    spec = pl.BlockSpec((tr, c), lambda i: (i, 0))
    return pl.pallas_call(
        kern,
        out_shape=[jax.ShapeDtypeStruct((r, c), F32)] * 4,
        grid=(n_tiles,),
        in_specs=[pl.BlockSpec(memory_space=pl.ANY), spec, spec, spec],
        out_specs=[spec] * 4,
        scratch_shapes=[pltpu.VMEM(window, parts.dtype), pltpu.SemaphoreType.DMA],
        name=name,
        compiler_params=pltpu.CompilerParams(dimension_semantics=("arbitrary",), vmem_limit_bytes=VMEM_LIMIT),
    )(parts, w, m, v)


def _sum_adamw(parts, w, m, v, tr, name):
    _, R, C = parts.shape

    def kern(p_ref, w_ref, m_ref, v_ref, g_ref, d_ref, nm_ref, nv_ref):
        _adamw_store(_sum_parts(p_ref), w_ref, m_ref, v_ref, g_ref, d_ref, nm_ref, nv_ref)

    row_spec = pl.BlockSpec((tr, C), lambda i: (i, 0))
    return pl.pallas_call(
        kern,
        out_shape=[jax.ShapeDtypeStruct((R, C), F32)] * 4,
        grid=(R // tr,),
        in_specs=[pl.BlockSpec((N_DEV, tr, C), lambda i: (0, i, 0)), row_spec, row_spec, row_spec],
        out_specs=[row_spec] * 4,
        name=name,
        compiler_params=pltpu.CompilerParams(dimension_semantics=("arbitrary",), vmem_limit_bytes=VMEM_LIMIT),
    )(parts, w, m, v)


FF_SHARD = D_FF // N_DEV
CONV_SHARD = (SSM_CONV, CONV_DIM // N_DEV)
SHARDS = {"ffn1_w_gate": ((D_MODEL, FF_SHARD), True), "ffn1_w_up": ((D_MODEL, FF_SHARD), True),
          "ffn1_w_down": ((FF_SHARD, D_MODEL), False),
          "ffn2_w_gate": ((D_MODEL, FF_SHARD), True), "ffn2_w_up": ((D_MODEL, FF_SHARD), True),
          "ffn2_w_down": ((FF_SHARD, D_MODEL), False),
          "w_out": ((2 * D_MODEL // N_DEV, D_MODEL), False), "ple_w_gate": ((D_MODEL // N_DEV, D_MODEL), False),
          "w_in": ((D_MODEL, IN_PROJ // N_DEV), True), "ple_w_proj": ((D_PLE, D_MODEL // N_DEV), True),
          "conv_w": (CONV_SHARD, True),
          "conv_w_mid": (CONV_SHARD, True), "conv_w_low": (CONV_SHARD, True)}
BIG = tuple(name for name in SHARDS if not name.startswith("conv_w_"))
SMALL = ("ffn1_norm", "mix_norm", "gm_ln_g", "gm_ln_b", "gm_w_s", "gm_b_s", "gm_out_norm", "conv_b", "dt_bias", "a_log",
         "d_skip", "ssm_norm", "ffn2_norm", "ple_norm", "ple_b_gate", "final_norm")
SMALL_ROWS = 144


def _piece_rows(name):
    shape = SHARDS[name][0]
    return -(-(shape[0] * shape[1]) // PACK_COLS)


def _pad_cols(flat, name):
    pad = _piece_rows(name) * PACK_COLS - flat.shape[-1]
    return flat if pad == 0 else jnp.pad(flat, [(0, 0)] * (flat.ndim - 1) + [(0, pad)])


class _Pack:
    def __init__(self, names, tile_rows):
        self.names, self.tile_rows, self.offsets, off = names, tile_rows, {}, 0
        for name in names:
            self.offsets[name] = off
            off += _piece_rows(name)
        self.rows = -(-off // tile_rows) * tile_rows

    def pack_local(self, vals):
        parts = []
        for name in self.names:
            val = vals[name]
            parts.append(_pad_cols((val.T if SHARDS[name][1] else val).reshape(-1), name))
        flat = jnp.concatenate(parts)
        return jnp.pad(flat, (0, self.rows * PACK_COLS - flat.shape[0])).reshape(self.rows, PACK_COLS)

    def pack_owner_major(self, grads):
        parts = [_pad_cols(grads[name].reshape(N_DEV, -1).astype(BF16), name) for name in self.names]
        flat = jnp.concatenate(parts, axis=1)
        flat = jnp.pad(flat, ((0, 0), (0, self.rows * PACK_COLS - flat.shape[1])))
        return flat.reshape(N_DEV, self.rows, PACK_COLS)

    def gathered_piece(self, gathered, name):
        shape = SHARDS[name][0]
        rows = gathered[:, self.offsets[name]:self.offsets[name] + _piece_rows(name), :]
        return rows.reshape(N_DEV, -1)[:, :shape[0] * shape[1]]

    def pieces(self, gathered, name):
        return _Pieces(gathered, self.offsets[name], _piece_rows(name))


GATHER_FFN1 = _Pack(("ffn1_w_gate", "ffn1_w_up", "ffn1_w_down"), BF16_ROWS)
GATHER_MIX = _Pack(("w_out", "ple_w_gate", "w_in", "ple_w_proj", "conv_w", "conv_w_mid", "conv_w_low"), BF16_ROWS)
GATHER_FFN2 = _Pack(("ffn2_w_gate", "ffn2_w_up", "ffn2_w_down"), BF16_ROWS)
SCATTER_LATE = _Pack(("ffn2_w_gate", "ffn2_w_up", "ffn2_w_down", "w_out", "ple_w_gate", "ple_w_proj"), BF16_ROWS)
SCATTER_IN = _Pack(("w_in", "conv_w"), BF16_ROWS)
SCATTER_GATE = _Pack(("ffn1_w_gate",), BF16_ROWS)
SCATTER_UP = _Pack(("ffn1_w_up",), BF16_ROWS)
SCATTER_DOWN = _Pack(("ffn1_w_down",), BF16_ROWS)


def _pack_small(vals):
    flat = jnp.concatenate([vals[name].reshape(-1).astype(F32) for name in SMALL])
    return jnp.pad(flat, (0, SMALL_ROWS * PACK_COLS - flat.shape[0])).reshape(SMALL_ROWS, PACK_COLS)


def _unpack_small(packed, shapes):
    out, off = {}, 0
    flat = packed.reshape(-1)
    for name in SMALL:
        n = 1
        for s in shapes[name]:
            n *= s
        out[name] = flat[off:off + n].reshape(shapes[name])
        off += n
    return out


WEIGHTS = ("ffn1_norm", "ffn1_w_gate", "ffn1_w_up", "ffn1_w_down", "mix_norm", "w_in", "gm_ln_g", "gm_ln_b", "gm_w_s",
           "gm_b_s", "gm_out_norm", "conv_w", "conv_b", "dt_bias", "a_log", "d_skip", "ssm_norm", "w_out", "ffn2_norm",
           "ffn2_w_gate", "ffn2_w_up", "ffn2_w_down", "ple_norm", "ple_w_gate", "ple_b_gate", "ple_w_proj", "final_norm")


def _step(x, p, target, w, m, v):
    local = lambda d: {name: d[name][0] for name in BIG}

    shards = {name: val.astype(BF16) for name, val in local(w).items()}
    conv_high = lax.reduce_precision(w["conv_w"][0], 8, 7)
    conv_mid = lax.reduce_precision(w["conv_w"][0] - conv_high, 8, 7)
    shards["conv_w"] = conv_high.astype(BF16)
    shards["conv_w_mid"] = conv_mid.astype(BF16)
    shards["conv_w_low"] = (w["conv_w"][0] - conv_high - conv_mid).astype(BF16)
    g_ffn1 = _comm_alone(_gather_comm(GATHER_FFN1.pack_local(shards)), "gather_ffn1")

    row = lambda name: w[name].reshape(1, -1)
    gm_w_s = w["gm_w_s"][0]
    gm_b_st = jnp.transpose(w["gm_b_s"][0])
    ffn1 = (row("ffn1_norm"),) + tuple(GATHER_FFN1.pieces(g_ffn1, name) for name in GATHER_FFN1.names)
    gm = (row("gm_ln_g"), row("gm_ln_b"), gm_w_s, gm_b_st, row("gm_out_norm"))

    h1, g_mix = _ffn_fwd(x, *ffn1, "ffn1_fwd", comm=_gather_comm(GATHER_MIX.pack_local(shards)))
    w_in_t = GATHER_MIX.gathered_piece(g_mix, "w_in").reshape(IN_PROJ, D_MODEL)
    w_in_t = jnp.concatenate([w_in_t, jnp.zeros((IN_PROJ_PAD - IN_PROJ, D_MODEL), BF16)], axis=0)
    w_proj_t = GATHER_MIX.gathered_piece(g_mix, "ple_w_proj").reshape(D_MODEL, D_PLE)
    conv_w = sum(GATHER_MIX.gathered_piece(g_mix, name).astype(F32) for name in ("conv_w", "conv_w_mid", "conv_w_low"))
    conv_w = conv_w.reshape(CONV_DIM, SSM_CONV).T
    ssd = (conv_w, row("conv_b"), row("dt_bias"), row("a_log"), row("d_skip"), row("ssm_norm"))
    w_out = GATHER_MIX.pieces(g_mix, "w_out")

    proj, n2 = _mix_in_fwd(h1, row("mix_norm"), w_in_t)
    ya = _gm_fwd(proj, *gm)
    yb, s_all, g_ffn2 = _ssd_fwd(proj, *ssd, comm=_gather_comm(GATHER_FFN2.pack_local(shards)))
    ffn2 = (row("ffn2_norm"),) + tuple(GATHER_FFN2.pieces(g_ffn2, name) for name in GATHER_FFN2.names)
    h2 = _out_proj_fwd(h1, ya, yb, w_out)
    h3 = _ffn_fwd(h2, *ffn2, "ffn2_fwd")[0]

    g, gp = {}, {}
    dh3, loss, gp["ple_w_gate"], d_w_proj, g["ple_norm"], g["ple_b_gate"], g["final_norm"] = _tail(
        h3, p, target, row("ple_norm"), GATHER_MIX.pieces(g_mix, "ple_w_gate"), row("ple_b_gate"), w_proj_t,
        row("final_norm"))
    gp["ple_w_proj"] = d_w_proj.T

    dh2, n3, s3, da3, db3, g["ffn2_norm"] = _ffn_dgrad(h2, dh3, *ffn2, "ffn2_dgrad")
    gp["ffn2_w_gate"] = _wgrad(n3, da3, 1408, "ffn2_wgrad_gate", transpose_out=True)
    gp["ffn2_w_up"] = _wgrad(n3, db3, 1408, "ffn2_wgrad_up", transpose_out=True)
    gp["ffn2_w_down"] = _wgrad(s3, dh3, 512, "ffn2_wgrad_down", scale=0.5, bk=1024)

    dya, dyb = _out_proj_dgrad(dh2, w_out)
    gp["w_out"] = jnp.concatenate([_wgrad(ya, dh2, 1024, "w_out_wgrad_a"), _wgrad(yb, dh2, 1024, "w_out_wgrad_b")], axis=0)

    dp_zxd, d_conv_w, g["conv_b"], g["dt_bias"], g["a_log"], g["d_skip"], g["ssm_norm"], parts_late = _ssd_bwd(
        proj, dyb, s_all, *ssd, comm=_exchange_comm(SCATTER_LATE.pack_owner_major(gp)))
    gp["conv_w"] = d_conv_w.T
    dp_uv, g["gm_ln_g"], g["gm_ln_b"], g["gm_w_s"], dbst, g["gm_out_norm"] = _gm_bwd(proj, dya, *gm)
    g["gm_b_s"] = jnp.transpose(dbst)

    parts = {}
    gp["w_in"] = jnp.concatenate([_wgrad(n2, dp_uv, 1024, "w_in_wgrad_uv", transpose_out=True),
                                  _wgrad(n2, dp_zxd, 896, "w_in_wgrad_zxd", transpose_out=True)], axis=0)[:IN_PROJ]
    dh1, g["mix_norm"], parts[SCATTER_IN] = _mix_in_dgrad(h1, dh2, dp_uv, dp_zxd, row("mix_norm"), w_in_t,
                                                          comm=_exchange_comm(SCATTER_IN.pack_owner_major(gp)))

    dx, n1, s1, da1, db1, g["ffn1_norm"] = _ffn_dgrad(x, dh1, *ffn1, "ffn1_dgrad")
    gp["ffn1_w_gate"] = _wgrad(n1, da1, 1408, "ffn1_wgrad_gate", transpose_out=True)
    gp["ffn1_w_up"], parts[SCATTER_GATE] = _wgrad(n1, db1, 1408, "ffn1_wgrad_up", transpose_out=True,
                                                  comm=_exchange_comm(SCATTER_GATE.pack_owner_major(gp)))
    gp["ffn1_w_down"], parts[SCATTER_UP] = _wgrad(s1, dh1, 512, "ffn1_wgrad_down", scale=0.5, bk=1024,
                                                  comm=_exchange_comm(SCATTER_UP.pack_owner_major(gp)))
    parts[SCATTER_DOWN] = _comm_alone(_exchange_comm(SCATTER_DOWN.pack_owner_major(gp)), "scatter_ffn1_down")
    parts[SCATTER_LATE] = parts_late

    res_big = {}
    for pack, pack_parts in parts.items():
        for name in pack.names:
            shape, transposed = SHARDS[name]
            wmv = (w[name][0], m[name][0], v[name][0])
            if name in ("ple_w_proj", "conv_w"):
                nat = pack.gathered_piece(pack_parts, name).reshape((N_DEV,) + shape[::-1])
                res_big[name] = _sum_adamw(jnp.transpose(nat, (0, 2, 1)), *wmv, shape[0], "adamw_" + name)
            else:
                res_big[name] = _adamw_shard(pack_parts, pack.offsets[name], transposed, *wmv, "adamw_" + name,
                                             n_tiles=4 if name == "w_in" else 1)

    small_shapes = {name: w[name].shape for name in SMALL}
    small_parts = _comm_alone(_gather_comm(_pack_small(g)), "gather_small_grads")
    res_small = _sum_adamw(small_parts, _pack_small(w), _pack_small(m), _pack_small(v), SMALL_ROWS, "adamw_small")
    res_small = [_unpack_small(r, small_shapes) for r in res_small]

    outs = []
    for k in range(4):
        for name in WEIGHTS:
            if name in res_small[k]:
                outs.append(res_small[k][name])
            else:
                outs.append(res_big[name][k].reshape(w[name].shape))
    return loss[0, 0], dx, outs


def kernel(x, p, ffn1_norm, ffn1_w_gate, ffn1_w_up, ffn1_w_down, mix_norm, w_in, gm_ln_g, gm_ln_b, gm_w_s, gm_b_s, gm_out_norm, conv_w, conv_b, dt_bias, a_log, d_skip, ssm_norm, w_out, ffn2_norm, ffn2_w_gate, ffn2_w_up, ffn2_w_down, ple_norm, ple_w_gate, ple_b_gate, ple_w_proj, final_norm, loss_target, m_ffn1_norm, m_ffn1_w_gate, m_ffn1_w_up, m_ffn1_w_down, m_mix_norm, m_w_in, m_gm_ln_g, m_gm_ln_b, m_gm_w_s, m_gm_b_s, m_gm_out_norm, m_conv_w, m_conv_b, m_dt_bias, m_a_log, m_d_skip, m_ssm_norm, m_w_out, m_ffn2_norm, m_ffn2_w_gate, m_ffn2_w_up, m_ffn2_w_down, m_ple_norm, m_ple_w_gate, m_ple_b_gate, m_ple_w_proj, m_final_norm, v_ffn1_norm, v_ffn1_w_gate, v_ffn1_w_up, v_ffn1_w_down, v_mix_norm, v_w_in, v_gm_ln_g, v_gm_ln_b, v_gm_w_s, v_gm_b_s, v_gm_out_norm, v_conv_w, v_conv_b, v_dt_bias, v_a_log, v_d_skip, v_ssm_norm, v_w_out, v_ffn2_norm, v_ffn2_w_gate, v_ffn2_w_up, v_ffn2_w_down, v_ple_norm, v_ple_w_gate, v_ple_b_gate, v_ple_w_proj, v_final_norm):
    args = locals()
    w = {name: args[name] for name in WEIGHTS}
    m = {name: args["m_" + name] for name in WEIGHTS}
    v = {name: args["v_" + name] for name in WEIGHTS}
    loss, dx, outs = _step(x[0], p[0, 0], loss_target[0], w, m, v)
    loss = lax.psum(loss, AXES)
    return (loss, dx[None], *outs)
```

```python
import functools
from typing import NamedTuple

import jax
import jax.numpy as jnp
from jax import lax
from jax.experimental import pallas as pl
from jax.experimental.pallas import tpu as pltpu

F32 = jnp.float32
BF16 = jnp.bfloat16
HIGHEST = lax.Precision.HIGHEST
MESH = pl.DeviceIdType.MESH
AXES = ("x", "y", "c")
N_DEV = 8

D_MODEL = 1024
D_FF = 2816
D_PLE = 256
GM_WIDTH = 1024
GM_HEADS = 8
GM_HEAD_DIM = 128
CHUNK = 128
SSM_WIDTH = 1024
SSM_HEADS = 16
SSM_HEAD_DIM = 64
SSM_GROUPS = 2
SSM_STATE = 128
SSM_CONV = 4
CONV_DIM = SSM_WIDTH + 2 * SSM_GROUPS * SSM_STATE
IN_PROJ = 2 * GM_WIDTH + SSM_WIDTH + CONV_DIM + SSM_HEADS
LANES = 128
BF16_ROWS = 16
IN_PROJ_PAD = IN_PROJ - SSM_HEADS + LANES
UV_W = 2 * GM_WIDTH
ZXD_W = IN_PROJ_PAD - UV_W
HALO = 8
EPS = 1e-6

ADAM_LR = 0.001
ADAM_B1 = 0.9
ADAM_B2 = 0.999
ADAM_EPS = 1e-08
ADAM_WD = 0.01
ADAM_STEP = 10

VMEM_LIMIT = 56 * 1024 * 1024
PACK_COLS = 1024


def _rms(x, g):
    return x * lax.rsqrt(jnp.mean(x * x, axis=-1, keepdims=True) + EPS) * g


def _gelu(x):
    return 0.5 * x * (1.0 + lax.erf(x * (2.0 ** -0.5)))


def _silu(x):
    return x * jax.nn.sigmoid(x)


def _dot(a, b):
    return jnp.dot(a.astype(BF16), b.astype(BF16), preferred_element_type=F32)


def _dot_nt(a, b):
    return lax.dot_general(a.astype(BF16), b.astype(BF16), (((1,), (1,)), ((), ())), preferred_element_type=F32)


def _dot_tn(a, b):
    return lax.dot_general(a.astype(BF16), b.astype(BF16), (((0,), (0,)), ((), ())), preferred_element_type=F32)


def _hdot_tn(a, b):
    return lax.dot_general(a, b, (((0,), (0,)), ((), ())), precision=HIGHEST, preferred_element_type=F32)


def _split3(x):
    hi = x.astype(BF16)
    rest = x - hi.astype(F32)
    mid = rest.astype(BF16)
    return hi, mid, (rest - mid.astype(F32)).astype(BF16)


def _exact_dot(x, mask, dims, x_first=True):
    terms = [lax.dot_general(*((t, mask) if x_first else (mask, t)), (dims, ((), ())), preferred_element_type=F32)
             for t in _split3(x)]
    return (terms[0] + terms[1]) + terms[2]


def _mask_product(fwd_dims, fwd_x_first, bwd_dims, bwd_x_first):
    @jax.custom_vjp
    def product(x, mask):
        return _exact_dot(x, mask, fwd_dims, fwd_x_first)

    def fwd(x, mask):
        return product(x, mask), mask

    def bwd(mask, g):
        return _exact_dot(g, mask, bwd_dims, bwd_x_first), jnp.zeros_like(mask)

    product.defvjp(fwd, bwd)
    return product


_widen = _mask_product(((1,), (0,)), True, ((1,), (1,)), True)
_cumsum_rows = _mask_product(((1,), (0,)), False, ((0,), (0,)), False)
_cumsum_cols = _mask_product(((0,), (0,)), True, ((1,), (1,)), False)


class _Pieces(NamedTuple):
    gathered: jax.Array
    row_off: int
    rows: int


class _Comm(NamedTuple):
    phases: object
    src: jax.Array
    dst: jax.ShapeDtypeStruct


def _tiled(body, name, n_steps, tiled_in, full_in, big_in, tiled_out, acc_out, scratch=(), reverse=False, comm=None):
    n_t, n_f, n_b, n_to, n_a = len(tiled_in), len(full_in), len(big_in), len(tiled_out), len(acc_out)
    n_c = 1 if comm else 0

    def row(i):
        return n_steps - 1 - i if reverse else i

    in_specs, args = [], []
    for arr, br, bc, cb in tiled_in:
        if callable(cb):
            in_specs.append(pl.BlockSpec((br, bc), cb))
        else:
            in_specs.append(pl.BlockSpec((br, bc), functools.partial(lambda i, cb: (row(i), cb), cb=cb)))
        args.append(arr)
    for arr in full_in:
        in_specs.append(pl.BlockSpec(arr.shape, functools.partial(lambda i, nd: (0,) * nd, nd=arr.ndim)))
        args.append(arr)
    big_shapes, n_copies = [], 0
    for big in big_in:
        in_specs.append(pl.BlockSpec(memory_space=pl.ANY))
        if isinstance(big, _Pieces):
            args.append(big.gathered)
            big_shapes.append(((N_DEV * big.rows, PACK_COLS), big.gathered.dtype))
            n_copies += N_DEV
        else:
            args.append(big)
            big_shapes.append((big.shape, big.dtype))
            n_copies += 1
    if comm:
        in_specs.append(pl.BlockSpec(memory_space=pl.ANY))
        args.append(comm.src)
    out_specs, out_shape = [], []
    for rows, cols, dt, br in tiled_out:
        out_specs.append(pl.BlockSpec((br, cols), lambda i: (row(i), 0)))
        out_shape.append(jax.ShapeDtypeStruct((rows, cols), dt))
    for shp, dt in acc_out:
        out_specs.append(pl.BlockSpec(shp, functools.partial(lambda i, nd: (0,) * nd, nd=len(shp))))
        out_shape.append(jax.ShapeDtypeStruct(shp, dt))
    if comm:
        out_specs.append(pl.BlockSpec(memory_space=pl.ANY))
        out_shape.append(comm.dst)
    scratch_shapes = [pltpu.VMEM(shp, dt) for shp, dt in big_shapes] + list(scratch)
    if n_copies:
        scratch_shapes.append(pltpu.SemaphoreType.DMA((n_copies,)))
    if comm:
        scratch_shapes += [pltpu.SemaphoreType.DMA((N_DEV - 1,)), pltpu.SemaphoreType.DMA((N_DEV - 1,)), pltpu.SemaphoreType.DMA]

    def kern(*refs):
        n_in = n_t + n_f + n_b + n_c
        ins = refs[: n_t + n_f]
        big_hbm = refs[n_t + n_f : n_t + n_f + n_b]
        outs = refs[n_in : n_in + n_to + n_a]
        rest = refs[n_in + n_to + n_a + n_c :]
        big_vmem, scr = rest[:n_b], rest[n_b:]
        if comm:
            scr, comm_sems = scr[:-3], scr[-3:]
            comm_start, comm_mid, comm_finish = comm.phases(refs[n_in - 1], refs[n_in + n_to + n_a], *comm_sems)
        if n_copies:
            scr, copy_sems = scr[:-1], scr[-1]
        step = pl.program_id(0)

        @pl.when(step == 0)
        def _():
            copies = []
            for big, src, dst in zip(big_in, big_hbm, big_vmem):
                if isinstance(big, _Pieces):
                    for j in range(N_DEV):
                        copies.append((src.at[j, pl.ds(big.row_off, big.rows), :], dst.at[pl.ds(j * big.rows, big.rows), :]))
                else:
                    copies.append((src, dst))
            copies = [pltpu.make_async_copy(a, b, copy_sems.at[k]) for k, (a, b) in enumerate(copies)]
            for cp in copies:
                cp.start()
            for cp in copies:
                cp.wait()
            for acc in outs[n_to:]:
                acc[...] = jnp.zeros(acc.shape, acc.dtype)
            if comm:
                comm_start()

        body(row(step), *ins, *big_vmem, *outs, *scr)
        if comm:
            pl.when(step == (n_steps - 1) // 2)(comm_mid)
            pl.when(step == n_steps - 1)(comm_finish)

    res = pl.pallas_call(
        kern,
        out_shape=out_shape,
        grid=(n_steps,),
        in_specs=in_specs,
        out_specs=out_specs,
        scratch_shapes=scratch_shapes,
        name=name,
        compiler_params=pltpu.CompilerParams(dimension_semantics=("arbitrary",), vmem_limit_bytes=VMEM_LIMIT),
    )(*args)
    return res


FF_CHUNKS = ((0, 1536), (1536, D_FF))
FFN_TM = 256


def _ffn_fwd(h, g, wg_t, wu_t, wd, name, comm=None):
    T = h.shape[0]

    def body(i, h_ref, g_ref, wg_ref, wu_ref, wd_ref, o_ref, n_ref, a_ref, b_ref, s_ref):
        x = h_ref[...]
        n = _rms(x, g_ref[...]).astype(BF16)
        n_ref[...] = n
        f = jnp.zeros(x.shape, F32)
        for lo, hi in FF_CHUNKS:
            a = _dot_nt(n, wg_ref[lo:hi, :])
            b = _dot_nt(n, wu_ref[lo:hi, :])
            s = (_silu(a) * b).astype(BF16)
            a_ref[:, lo:hi] = a.astype(BF16)
            b_ref[:, lo:hi] = b.astype(BF16)
            s_ref[:, lo:hi] = s
            f = f + jnp.dot(s, wd_ref[lo:hi, :], preferred_element_type=F32)
        o_ref[...] = x + 0.5 * f

    return _tiled(body, name, T // FFN_TM, [(h, FFN_TM, D_MODEL, 0)], [g], [wg_t, wu_t, wd],
                  [(T, D_MODEL, F32, FFN_TM), (T, D_MODEL, BF16, FFN_TM), (T, D_FF, BF16, FFN_TM), (T, D_FF, BF16, FFN_TM),
                   (T, D_FF, BF16, FFN_TM)], [], comm=comm)


def _ffn_dgrad(h, dout, a16, b16, g, wg_t, wu_t, wd, name):
    T = h.shape[0]

    def body(i, h_ref, do_ref, a_ref, b_ref, g_ref, wg_ref, wu_ref, wd_ref, dh_ref, da_ref, db_ref, dg_ref):
        dout = do_ref[...]
        _, rms_vjp = jax.vjp(_rms, h_ref[...], g_ref[...])
        dfo = (0.5 * dout).astype(BF16)
        dn = jnp.zeros(dout.shape, F32)
        for lo, hi in FF_CHUNKS:
            a = a_ref[:, lo:hi].astype(F32)
            b = b_ref[:, lo:hi].astype(F32)
            sg = jax.nn.sigmoid(a)
            ds = _dot_nt(dfo, wd_ref[lo:hi, :])
            db = (ds * (a * sg)).astype(BF16)
            da = (ds * b * (sg * (1.0 + a * (1.0 - sg)))).astype(BF16)
            dn = dn + _dot(da, wg_ref[lo:hi, :]) + _dot(db, wu_ref[lo:hi, :])
            da_ref[:, lo:hi] = da
            db_ref[:, lo:hi] = db
        dx, dg = rms_vjp(dn)
        dh_ref[...] = dout + dx
        dg_ref[...] += dg

    return _tiled(body, name, T // FFN_TM,
                  [(h, FFN_TM, D_MODEL, 0), (dout, FFN_TM, D_MODEL, 0), (a16, FFN_TM, D_FF, 0), (b16, FFN_TM, D_FF, 0)],
                  [g], [wg_t, wu_t, wd],
                  [(T, D_MODEL, F32, FFN_TM), (T, D_FF, BF16, FFN_TM), (T, D_FF, BF16, FFN_TM)], [((1, D_MODEL), F32)])


def _wgrad(a, b, bn, name, scale=None, transpose_out=False, bk=2048, comm=None):
    T, M = a.shape
    N = b.shape[1]
    bk = min(bk, T)
    assert M % LANES == 0 and N % bn == 0 and T % bk == 0
    n_j, n_k = N // bn, T // bk
    n_c = 1 if comm else 0

    def kern(*refs):
        a_ref, b_ref, o_ref, acc_ref = refs[0], refs[1], refs[2 + n_c], refs[3 + 2 * n_c]
        j, k = pl.program_id(0), pl.program_id(1)
        if comm:
            comm_start, _, comm_finish = comm.phases(refs[2], refs[4], *refs[6:])
            pl.when((j == 0) & (k == 0))(comm_start)

        @pl.when(k == 0)
        def _():
            acc_ref[...] = jnp.zeros(acc_ref.shape, F32)

        bv = b_ref[...]
        if scale is not None:
            bv = bv * scale
        acc_ref[...] += _dot_tn(a_ref[...], bv)

        @pl.when(k == n_k - 1)
        def _():
            acc = acc_ref[...]
            o_ref[...] = (acc.T if transpose_out else acc).astype(BF16)

        if comm:
            pl.when((j == n_j - 1) & (k == n_k - 1))(comm_finish)

    if transpose_out:
        out_shape, out_spec = (N, M), pl.BlockSpec((bn, M), lambda j, k: (j, 0))
    else:
        out_shape, out_spec = (M, N), pl.BlockSpec((M, bn), lambda j, k: (0, j))
    any_spec = pl.BlockSpec(memory_space=pl.ANY)
    comm_sems = [pltpu.SemaphoreType.DMA((N_DEV - 1,)), pltpu.SemaphoreType.DMA((N_DEV - 1,)), pltpu.SemaphoreType.DMA]
    res = pl.pallas_call(
        kern,
        out_shape=[jax.ShapeDtypeStruct(out_shape, BF16)] + ([comm.dst] if comm else []),
        grid=(n_j, n_k),
        in_specs=[pl.BlockSpec((bk, M), lambda j, k: (k, 0)), pl.BlockSpec((bk, bn), lambda j, k: (k, j))] + [any_spec] * n_c,
        out_specs=[out_spec] + [any_spec] * n_c,
        scratch_shapes=[pltpu.VMEM((M, bn), F32)] + (comm_sems if comm else []),
        name=name,
        compiler_params=pltpu.CompilerParams(dimension_semantics=("arbitrary", "arbitrary"), vmem_limit_bytes=VMEM_LIMIT),
    )(a, b, *([comm.src] if comm else []))
    return res if comm else res[0]


PROJ_TM = 256


def _mix_in_fwd(h, g, w_in_t):
    T = h.shape[0]

    def body(i, h_ref, g_ref, w_ref, p_ref, n_ref):
        n = _rms(h_ref[...], g_ref[...]).astype(BF16)
        n_ref[...] = n
        p_ref[...] = _dot_nt(n, w_ref[...])

    return _tiled(body, "mix_in_fwd", T // PROJ_TM, [(h, PROJ_TM, D_MODEL, 0)], [g], [w_in_t],
                  [(T, IN_PROJ_PAD, F32, PROJ_TM), (T, D_MODEL, BF16, PROJ_TM)], [])


def _mix_in_dgrad(h, dh_in, dp_uv, dp_zxd, g, w_in_t, comm=None):
    T = h.shape[0]

    def body(i, h_ref, dh_ref, duv_ref, dzxd_ref, g_ref, w_ref, o_ref, dg_ref):
        dn = _dot(duv_ref[...], w_ref[:UV_W, :]) + _dot(dzxd_ref[...], w_ref[UV_W:, :])
        _, rms_vjp = jax.vjp(_rms, h_ref[...], g_ref[...])
        dx, dg = rms_vjp(dn)
        o_ref[...] = dh_ref[...] + dx
        dg_ref[...] += dg

    return _tiled(body, "mix_in_dgrad", T // PROJ_TM,
                  [(h, PROJ_TM, D_MODEL, 0), (dh_in, PROJ_TM, D_MODEL, 0), (dp_uv, PROJ_TM, UV_W, 0),
                   (dp_zxd, PROJ_TM, ZXD_W, 0)], [g], [w_in_t],
                  [(T, D_MODEL, F32, PROJ_TM)], [((1, D_MODEL), F32)], comm=comm)


def _out_proj_fwd(h, ya, yb, w_out):
    T = h.shape[0]

    def body(i, h_ref, ya_ref, yb_ref, w_ref, o_ref):
        o_ref[...] = (h_ref[...] + jnp.dot(ya_ref[...], w_ref[:GM_WIDTH, :], preferred_element_type=F32)
                      + jnp.dot(yb_ref[...], w_ref[GM_WIDTH:, :], preferred_element_type=F32))

    return _tiled(body, "out_proj_fwd", T // PROJ_TM,
                  [(h, PROJ_TM, D_MODEL, 0), (ya, PROJ_TM, GM_WIDTH, 0), (yb, PROJ_TM, SSM_WIDTH, 0)], [], [w_out],
                  [(T, D_MODEL, F32, PROJ_TM)], [])[0]


def _out_proj_dgrad(dh, w_out):
    T = dh.shape[0]

    def body(i, dh_ref, w_ref, dya_ref, dyb_ref):
        d = dh_ref[...].astype(BF16)
        dya_ref[...] = _dot_nt(d, w_ref[:GM_WIDTH, :])
        dyb_ref[...] = _dot_nt(d, w_ref[GM_WIDTH:, :])

    return _tiled(body, "out_proj_dgrad", T // PROJ_TM, [(dh, PROJ_TM, D_MODEL, 0)], [], [w_out],
                  [(T, GM_WIDTH, F32, PROJ_TM), (T, SSM_WIDTH, F32, PROJ_TM)], [])


def _gm_chunk(u, v, ln_g, ln_b, b_st, out_g, *w_heads):
    ug = _gelu(u)
    vg = _gelu(v)
    mu = jnp.mean(vg, axis=-1, keepdims=True)
    xc = vg - mu
    vn = xc * lax.rsqrt(jnp.mean(xc * xc, axis=-1, keepdims=True) + EPS) * ln_g + ln_b
    t_idx = lax.broadcasted_iota(jnp.int32, (CHUNK, CHUNK), 0)
    s_idx = lax.broadcasted_iota(jnp.int32, (CHUNK, CHUNK), 1)
    causal = t_idx >= s_idx
    mixed = []
    for hd in range(GM_HEADS):
        wm = jnp.where(causal, w_heads[hd], 0.0)
        cols = slice(hd * GM_HEAD_DIM, (hd + 1) * GM_HEAD_DIM)
        mixed.append(_dot(wm, vn[:, cols]) + b_st[:, hd:hd + 1])
    ya0 = ug * jnp.concatenate(mixed, axis=1)
    return _rms(ya0, out_g)


def _gm_fwd(proj, ln_g, ln_b, w_s, b_st, out_g):
    T = proj.shape[0]

    def body(i, u_ref, v_ref, lg_ref, lb_ref, w_ref, bs_ref, og_ref, ya_ref):
        w_heads = [w_ref[hd] for hd in range(GM_HEADS)]
        ya = _gm_chunk(u_ref[...], v_ref[...], lg_ref[...], lb_ref[...], bs_ref[...], og_ref[...], *w_heads)
        ya_ref[...] = ya.astype(BF16)

    return _tiled(body, "gmlp_fwd", T // CHUNK, [(proj, CHUNK, GM_WIDTH, 0), (proj, CHUNK, GM_WIDTH, 1)],
                  [ln_g, ln_b, w_s, b_st, out_g], [], [(T, GM_WIDTH, BF16, CHUNK)], [])[0]


def _gm_bwd(proj, dya, ln_g, ln_b, w_s, b_st, out_g):
    T = proj.shape[0]

    def body(i, u_ref, v_ref, dy_ref, lg_ref, lb_ref, w_ref, bs_ref, og_ref, duv_ref, dlg_ref, dlb_ref, dw_ref, dbs_ref,
             dog_ref):
        w_heads = [w_ref[hd] for hd in range(GM_HEADS)]
        _, vjp = jax.vjp(_gm_chunk, u_ref[...], v_ref[...], lg_ref[...], lb_ref[...], bs_ref[...], og_ref[...], *w_heads)
        grads = vjp(dy_ref[...])
        duv_ref[:, :GM_WIDTH] = grads[0].astype(BF16)
        duv_ref[:, GM_WIDTH:] = grads[1].astype(BF16)
        dlg_ref[...] += grads[2]
        dlb_ref[...] += grads[3]
        dbs_ref[...] += grads[4]
        dog_ref[...] += grads[5]
        for hd in range(GM_HEADS):
            dw_ref[hd] += grads[6 + hd]

    return _tiled(body, "gmlp_bwd", T // CHUNK,
                  [(proj, CHUNK, GM_WIDTH, 0), (proj, CHUNK, GM_WIDTH, 1), (dya, CHUNK, GM_WIDTH, 0)],
                  [ln_g, ln_b, w_s, b_st, out_g], [], [(T, UV_W, BF16, CHUNK)],
                  [((1, GM_WIDTH), F32), ((1, GM_WIDTH), F32), ((GM_HEADS, CHUNK, CHUNK), F32),
                   ((CHUNK, GM_HEADS), F32), ((1, GM_WIDTH), F32)])


def _ssd_chunk(xc, z, dtr, s_in, dt_bias, a_log, d_skip, norm_g):
    half = SSM_WIDTH // SSM_GROUPS
    l_idx = lax.broadcasted_iota(jnp.int32, (CHUNK, CHUNK), 0)
    s_idx = lax.broadcasted_iota(jnp.int32, (CHUNK, CHUNK), 1)
    causal = l_idx >= s_idx
    head_of_col = lax.broadcasted_iota(jnp.int32, (SSM_HEADS, SSM_WIDTH), 1) // SSM_HEAD_DIM
    expand = (head_of_col == lax.broadcasted_iota(jnp.int32, (SSM_HEADS, SSM_WIDTH), 0)).astype(BF16)

    xcs = _silu(xc)
    xs = xcs[:, :SSM_WIDTH]
    dt = jax.nn.softplus(dtr + dt_bias)
    adt = dt * (-jnp.exp(a_log))
    acs = _cumsum_rows(adt, causal.astype(BF16))
    acs_t = _cumsum_cols(adt, (l_idx <= s_idx).astype(BF16))
    tot = acs[CHUNK - 1:CHUNK, :]
    dt_w = _widen(dt, expand)
    out_decay_w = _widen(jnp.exp(acs), expand)
    state_decay_w = _widen(jnp.exp(tot - acs), expand)
    chunk_decay_w = _widen(jnp.exp(tot), expand)
    d_skip_w = _widen(d_skip, expand)
    xdt = xs * dt_w
    xdt_decayed = xdt * state_decay_w

    y_diag, y_off, states = [], [], []
    for grp in range(SSM_GROUPS):
        b0 = SSM_WIDTH + grp * SSM_STATE
        c0 = SSM_WIDTH + SSM_GROUPS * SSM_STATE + grp * SSM_STATE
        bm = xcs[:, b0:b0 + SSM_STATE].astype(BF16)
        cm = xcs[:, c0:c0 + SSM_STATE].astype(BF16)
        cb = _dot_nt(cm, bm)
        for k in range(grp * SSM_HEADS // SSM_GROUPS, (grp + 1) * SSM_HEADS // SSM_GROUPS):
            decay = jnp.exp(jnp.where(causal, acs[:, k:k + 1] - acs_t[k:k + 1, :], -jnp.inf))
            y_diag.append(_dot(cb * decay, xdt[:, k * SSM_HEAD_DIM:(k + 1) * SSM_HEAD_DIM]))
        cols = slice(grp * half, (grp + 1) * half)
        states.append(_dot_tn(bm, xdt_decayed[:, cols]))
        y_off.append(_dot(cm, s_in[:, cols]))
    y = jnp.concatenate(y_diag, axis=1) + jnp.concatenate(y_off, axis=1) * out_decay_w + xs * d_skip_w
    s_out = s_in * chunk_decay_w + jnp.concatenate(states, axis=1)
    y = y * _silu(z)
    normed = []
    for grp in range(SSM_GROUPS):
        yg = y[:, grp * half:(grp + 1) * half]
        normed.append(yg * lax.rsqrt(jnp.mean(yg * yg, axis=-1, keepdims=True) + EPS))
    return jnp.concatenate(normed, axis=1) * norm_g, s_out


def _conv_taps(ext_ref, w, b):
    taps = [ext_ref[pl.ds(HALO - (SSM_CONV - 1) + k, CHUNK), :] for k in range(SSM_CONV)]
    y = b
    for k in range(SSM_CONV):
        y = y + w[k:k + 1, :] * taps[k]
    return y, taps


def _ssd_fwd(proj, conv_w, conv_b, dt_bias, a_log, d_skip, norm_g, comm=None):
    T = proj.shape[0]
    n_chunks = T // CHUNK

    def body(i, z_ref, x_ref, dt_ref, cw_ref, cb_ref, dtb_ref, al_ref, dsk_ref, ng_ref, yb_ref, sin_ref, ext_ref, st_ref):
        @pl.when(i == 0)
        def _():
            ext_ref[0:HALO, :] = jnp.zeros((HALO, CONV_DIM), F32)
            st_ref[...] = jnp.zeros(st_ref.shape, F32)

        ext_ref[HALO:, :] = x_ref[...]
        xc, _ = _conv_taps(ext_ref, cw_ref[...], cb_ref[...])
        s_in = st_ref[...]
        yb, s_out = _ssd_chunk(xc, z_ref[...], dt_ref[:, 0:SSM_HEADS], s_in, dtb_ref[...], al_ref[...], dsk_ref[...],
                               ng_ref[...])
        yb_ref[...] = yb.astype(BF16)
        sin_ref[...] = s_in
        st_ref[...] = s_out
        ext_ref[0:HALO, :] = ext_ref[CHUNK:CHUNK + HALO, :]

    z_blk = 2 * GM_WIDTH // SSM_WIDTH
    x_blk = (2 * GM_WIDTH + SSM_WIDTH) // CONV_DIM
    dt_blk = (2 * GM_WIDTH + SSM_WIDTH + CONV_DIM) // LANES
    return _tiled(body, "ssd_fwd", n_chunks,
                  [(proj, CHUNK, SSM_WIDTH, z_blk), (proj, CHUNK, CONV_DIM, x_blk), (proj, CHUNK, LANES, dt_blk)],
                  [conv_w, conv_b, dt_bias, a_log, d_skip, norm_g], [],
                  [(T, SSM_WIDTH, BF16, CHUNK), (n_chunks * SSM_STATE, SSM_WIDTH, F32, SSM_STATE)], [],
                  scratch=[pltpu.VMEM((HALO + CHUNK, CONV_DIM), F32), pltpu.VMEM((SSM_STATE, SSM_WIDTH), F32)], comm=comm)


def _ssd_bwd(proj, dyb, s_all, conv_w, conv_b, dt_bias, a_log, d_skip, norm_g, comm=None):
    T = proj.shape[0]
    n_chunks = T // CHUNK
    z_blk = 2 * GM_WIDTH // SSM_WIDTH
    x_blk = (2 * GM_WIDTH + SSM_WIDTH) // CONV_DIM
    dt_blk = (2 * GM_WIDTH + SSM_WIDTH + CONV_DIM) // LANES
    rows_per_halo = CHUNK // HALO

    def body(i, z_ref, x_ref, halo_ref, dt_ref, dy_ref, sin_ref, cw_ref, cb_ref, dtb_ref, al_ref, dsk_ref, ng_ref,
             dzxd_ref, dcw_ref, dcb_ref, ddtb_ref, dal_ref, ddsk_ref, dng_ref, ext_ref, dext_ref, dst_ref):
        @pl.when(i == n_chunks - 1)
        def _():
            dext_ref[CHUNK:, :] = jnp.zeros((HALO, CONV_DIM), F32)
            dst_ref[...] = jnp.zeros(dst_ref.shape, F32)

        halo = halo_ref[...]
        ext_ref[0:HALO, :] = jnp.where(i == 0, jnp.zeros_like(halo), halo)
        ext_ref[HALO:, :] = x_ref[...]
        cw = cw_ref[...]
        xc, taps = _conv_taps(ext_ref, cw, cb_ref[...])
        _, vjp = jax.vjp(_ssd_chunk, xc, z_ref[...], dt_ref[:, 0:SSM_HEADS], sin_ref[...], dtb_ref[...], al_ref[...],
                         dsk_ref[...], ng_ref[...])
        dxc, dz, ddtr, ds_in, ddtb, dal, ddsk, dng = vjp((dy_ref[...], dst_ref[...]))
        dst_ref[...] = ds_in
        ddtb_ref[...] += ddtb
        dal_ref[...] += dal
        ddsk_ref[...] += ddsk
        dng_ref[...] += dng
        dext_ref[0:CHUNK, :] = dxc
        dx = jnp.zeros((CHUNK, CONV_DIM), F32)
        for k in range(SSM_CONV):
            dx = dx + cw[k:k + 1, :] * dext_ref[pl.ds(SSM_CONV - 1 - k, CHUNK), :]
            dcw_ref[k:k + 1, :] += jnp.sum(dxc * taps[k], axis=0, keepdims=True)
        dcb_ref[...] += jnp.sum(dxc, axis=0, keepdims=True)
        dext_ref[CHUNK:, :] = dext_ref[0:HALO, :]
        dzxd_ref[:, 0:SSM_WIDTH] = dz.astype(BF16)
        dzxd_ref[:, SSM_WIDTH:SSM_WIDTH + CONV_DIM] = dx.astype(BF16)
        dzxd_ref[:, SSM_WIDTH + CONV_DIM:] = jnp.concatenate(
            [ddtr, jnp.zeros((CHUNK, LANES - SSM_HEADS), F32)], axis=1).astype(BF16)

    def halo_index(step):
        c = n_chunks - 1 - step
        return (jnp.maximum(c * rows_per_halo - 1, 0), x_blk)

    return _tiled(body, "ssd_bwd", n_chunks,
                  [(proj, CHUNK, SSM_WIDTH, z_blk), (proj, CHUNK, CONV_DIM, x_blk), (proj, HALO, CONV_DIM, halo_index),
                   (proj, CHUNK, LANES, dt_blk), (dyb, CHUNK, SSM_WIDTH, 0), (s_all, SSM_STATE, SSM_WIDTH, 0)],
                  [conv_w, conv_b, dt_bias, a_log, d_skip, norm_g], [],
                  [(T, ZXD_W, BF16, CHUNK)],
                  [((SSM_CONV, CONV_DIM), F32), ((1, CONV_DIM), F32), ((1, SSM_HEADS), F32), ((1, SSM_HEADS), F32),
                   ((1, SSM_HEADS), F32), ((1, SSM_WIDTH), F32)],
                  scratch=[pltpu.VMEM((HALO + CHUNK, CONV_DIM), F32), pltpu.VMEM((CHUNK + HALO, CONV_DIM), F32),
                           pltpu.VMEM((SSM_STATE, SSM_WIDTH), F32)],
                  reverse=True, comm=comm)


TAIL_TM = 512


def _tail(h, p, target, ple_norm, w_gate, b_gate, w_proj_t, final_norm):
    T = h.shape[0]

    def head(x, pre, pp, b_g, f_norm, tgt):
        gate = jax.nn.sigmoid(pre + b_g)
        out = _rms(x + gate * pp, f_norm)
        err = out - tgt
        return 0.5 * jnp.sum(jnp.mean(err * err, axis=-1, keepdims=True), axis=0, keepdims=True)

    def body(i, h_ref, p_ref, t_ref, pn_ref, bg_ref, fn_ref, wg_ref, wp_ref, dh_ref, loss_ref, dwg_ref, dwp_ref, dpn_ref,
             dbg_ref, dfn_ref):
        x = h_ref[...]
        n4f, n_vjp = jax.vjp(_rms, x, pn_ref[...])
        n4 = n4f.astype(BF16)
        pre = jnp.dot(n4, wg_ref[...], preferred_element_type=F32)
        p16 = p_ref[...].astype(BF16)
        pp = _dot_nt(p16, wp_ref[...])
        loss, h_vjp = jax.vjp(functools.partial(head, tgt=t_ref[...]), x, pre, pp, bg_ref[...], fn_ref[...])
        dx, dpre, dpp, dbg, dfn = h_vjp(jnp.ones((1, 1), F32))
        dpre16 = dpre.astype(BF16)
        dn4 = _dot_nt(dpre16, wg_ref[...])
        dx2, dpn = n_vjp(dn4)
        dh_ref[...] = dx + dx2
        loss_ref[...] += loss
        dwg_ref[...] += _dot_tn(n4, dpre16)
        dwp_ref[...] += _dot_tn(p16, dpp)
        dpn_ref[...] += dpn
        dbg_ref[...] += dbg
        dfn_ref[...] += dfn

    return _tiled(body, "tail", T // TAIL_TM,
                  [(h, TAIL_TM, D_MODEL, 0), (p, TAIL_TM, D_PLE, 0), (target, TAIL_TM, D_MODEL, 0)],
                  [ple_norm, b_gate, final_norm], [w_gate, w_proj_t],
                  [(T, D_MODEL, F32, TAIL_TM)],
                  [((1, 1), F32), ((D_MODEL, D_MODEL), F32), ((D_PLE, D_MODEL), F32), ((1, D_MODEL), F32),
                   ((1, D_MODEL), F32), ((1, D_MODEL), F32)])


def _gather_phases(x_ref, out_ref, send_sems, recv_sems, local_sem):
    mx, my, mc = lax.axis_index("x"), lax.axis_index("y"), lax.axis_index("c")
    me, sibling = (mx, my, mc), (mx, my, 1 - mc)
    chips = [(1 - mx, my), (mx, 1 - my), (1 - mx, 1 - my)]

    def rows(px, py, pc):
        return out_ref.at[4 * px + 2 * py + pc]

    def copy(k, block, to, src=None):
        return pltpu.make_async_remote_copy(
            src_ref=rows(*block) if src is None else src, dst_ref=rows(*block),
            send_sem=send_sems.at[k], recv_sem=recv_sems.at[k], device_id=to, device_id_type=MESH)

    mine = pltpu.make_async_copy(x_ref, rows(*me), local_sem)
    first = [copy(0, me, sibling, src=x_ref)] + [copy(1 + j, me, (*chip, mc), src=x_ref) for j, chip in enumerate(chips)]
    passed = [copy(4 + j, (*chip, mc), sibling) for j, chip in enumerate(chips)]

    def start():
        mine.start()
        for cp in first:
            cp.start()

    def mid():
        for j, chip in enumerate(chips):
            copy(1 + j, (*chip, mc), me).wait_recv()
            passed[j].start()

    def finish():
        copy(0, sibling, me).wait_recv()
        for j, chip in enumerate(chips):
            copy(4 + j, (*chip, 1 - mc), me).wait_recv()
        for cp in first + passed:
            cp.wait_send()
        mine.wait()

    return start, mid, finish


def _exchange_phases(x_ref, out_ref, send_sems, recv_sems, local_sem):
    mx, my, mc = lax.axis_index("x"), lax.axis_index("y"), lax.axis_index("c")
    me = 4 * mx + 2 * my + mc
    mine = pltpu.make_async_copy(x_ref.at[me], out_ref.at[me], local_sem)
    copies = []
    for k in range(1, N_DEV):
        px = 1 - mx if k & 4 else mx
        py = 1 - my if k & 2 else my
        pc = 1 - mc if k & 1 else mc
        copies.append(pltpu.make_async_remote_copy(
            src_ref=x_ref.at[4 * px + 2 * py + pc], dst_ref=out_ref.at[me], send_sem=send_sems.at[k - 1],
            recv_sem=recv_sems.at[k - 1], device_id=(px, py, pc), device_id_type=MESH))

    def start():
        mine.start()
        for cp in copies:
            cp.start()

    def finish():
        for cp in copies:
            cp.wait_recv()
        for cp in copies:
            cp.wait_send()
        mine.wait()

    return start, lambda: None, finish


def _gather_comm(x):
    return _Comm(_gather_phases, x, jax.ShapeDtypeStruct((N_DEV,) + x.shape, x.dtype))


def _exchange_comm(x):
    return _Comm(_exchange_phases, x, jax.ShapeDtypeStruct(x.shape, x.dtype))


def _comm_alone(comms, name):
    n = len(comms)

    def body(*refs):
        phases = [comm.phases(refs[k], refs[n + k], *refs[2 * n + 3 * k:2 * n + 3 * k + 3]) for k, comm in enumerate(comms)]
        for step in range(3):
            for phase in phases:
                phase[step]()

    any_spec = pl.BlockSpec(memory_space=pl.ANY)
    return pl.pallas_call(
        body,
        out_shape=[comm.dst for comm in comms],
        in_specs=[any_spec] * n,
        out_specs=[any_spec] * n,
        scratch_shapes=[pltpu.SemaphoreType.DMA((N_DEV - 1,)), pltpu.SemaphoreType.DMA((N_DEV - 1,)), pltpu.SemaphoreType.DMA] * n,
        name=name,
    )(*[comm.src for comm in comms])


def _sum_parts(p_ref):
    g = p_ref[0].astype(F32)
    for j in range(1, N_DEV):
        g = g + p_ref[j].astype(F32)
    return g


def _adamw_store(g, w_ref, m_ref, v_ref, g_ref, d_ref, nm_ref, nv_ref):
    m_new = ADAM_B1 * m_ref[...] + (1.0 - ADAM_B1) * g
    v_new = ADAM_B2 * v_ref[...] + (1.0 - ADAM_B2) * jnp.square(g)
    m_hat = m_new / (1.0 - ADAM_B1 ** ADAM_STEP)
    v_hat = v_new / (1.0 - ADAM_B2 ** ADAM_STEP)
    g_ref[...] = g
    d_ref[...] = -ADAM_LR * (m_hat / (jnp.sqrt(v_hat) + ADAM_EPS) + ADAM_WD * w_ref[...])
    nm_ref[...] = m_new
    nv_ref[...] = v_new


def _adamw_shard(parts, off, transposed, w, m, v, name, n_tiles=1):
    _, r, c = w.shape
    tr = r // n_tiles
    if transposed:
        rows = -(-c // BF16_ROWS) * BF16_ROWS
        window = (N_DEV, rows, tr)
    else:
        assert c == PACK_COLS
        window = (N_DEV, tr, PACK_COLS)

    def kern(p_hbm, w_ref, m_ref, v_ref, g_ref, d_ref, nm_ref, nv_ref, buf, sem):
        i = pl.program_id(0)
        if transposed:
            src = p_hbm.at[:, pl.ds(off, rows), pl.ds(pl.multiple_of(i * tr, LANES), tr)]
        else:
            src = p_hbm.at[:, pl.ds(pl.multiple_of(off + i * tr, BF16_ROWS), tr), :]
        cp = pltpu.make_async_copy(src, buf, sem)
        cp.start()
        cp.wait()
        g = _sum_parts(buf)
        if transposed:
            eye = (lax.broadcasted_iota(jnp.int32, (rows, c), 0) == lax.broadcasted_iota(jnp.int32, (rows, c), 1)).astype(F32)
            g = _hdot_tn(g, eye)
        _adamw_store(g, w_ref, m_ref, v_ref, g_ref, d_ref, nm_ref, nv_ref)

    spec = pl.BlockSpec((None, tr, c), lambda i: (0, i, 0))
    return pl.pallas_call(
        kern,
        out_shape=[jax.ShapeDtypeStruct((1, r, c), F32)] * 4,
        grid=(n_tiles,),
        in_specs=[pl.BlockSpec(memory_space=pl.ANY), spec, spec, spec],
        out_specs=[spec] * 4,
        scratch_shapes=[pltpu.VMEM(window, parts.dtype), pltpu.SemaphoreType.DMA],
        name=name,
        compiler_params=pltpu.CompilerParams(dimension_semantics=("arbitrary",), vmem_limit_bytes=VMEM_LIMIT),
    )(parts, w, m, v)


def _sum_adamw(parts, w, m, v, tr, name):
    _, R, C = parts.shape

    def kern(p_ref, w_ref, m_ref, v_ref, g_ref, d_ref, nm_ref, nv_ref):
        _adamw_store(_sum_parts(p_ref), w_ref, m_ref, v_ref, g_ref, d_ref, nm_ref, nv_ref)

    row_spec = pl.BlockSpec((tr, C), lambda i: (i, 0))
    return pl.pallas_call(
        kern,
        out_shape=[jax.ShapeDtypeStruct((R, C), F32)] * 4,
        grid=(R // tr,),
        in_specs=[pl.BlockSpec((N_DEV, tr, C), lambda i: (0, i, 0)), row_spec, row_spec, row_spec],
        out_specs=[row_spec] * 4,
        name=name,
        compiler_params=pltpu.CompilerParams(dimension_semantics=("arbitrary",), vmem_limit_bytes=VMEM_LIMIT),
    )(parts, w, m, v)


FF_SHARD = D_FF // N_DEV
CONV_SHARD = (SSM_CONV, CONV_DIM // N_DEV)
SHARDS = {"ffn1_w_gate": ((D_MODEL, FF_SHARD), True), "ffn1_w_up": ((D_MODEL, FF_SHARD), True),
          "ffn1_w_down": ((FF_SHARD, D_MODEL), False),
          "ffn2_w_gate": ((D_MODEL, FF_SHARD), True), "ffn2_w_up": ((D_MODEL, FF_SHARD), True),
          "ffn2_w_down": ((FF_SHARD, D_MODEL), False),
          "w_out": ((2 * D_MODEL // N_DEV, D_MODEL), False), "ple_w_gate": ((D_MODEL // N_DEV, D_MODEL), False),
          "w_in": ((D_MODEL, IN_PROJ // N_DEV), True), "ple_w_proj": ((D_PLE, D_MODEL // N_DEV), True),
          "conv_w": (CONV_SHARD, True),
          "conv_w_mid": (CONV_SHARD, True), "conv_w_low": (CONV_SHARD, True)}
BIG = tuple(name for name in SHARDS if not name.startswith("conv_w_"))
SMALL = ("ffn1_norm", "mix_norm", "gm_ln_g", "gm_ln_b", "gm_w_s", "gm_b_s", "gm_out_norm", "conv_b", "dt_bias", "a_log",
         "d_skip", "ssm_norm", "ffn2_norm", "ple_norm", "ple_b_gate", "final_norm")
SMALL_ROWS = 144


def _piece_rows(name):
    shape = SHARDS[name][0]
    return -(-(shape[0] * shape[1]) // PACK_COLS)


def _pad_cols(flat, name):
    pad = _piece_rows(name) * PACK_COLS - flat.shape[-1]
    return flat if pad == 0 else jnp.pad(flat, [(0, 0)] * (flat.ndim - 1) + [(0, pad)])


class _Pack:
    def __init__(self, names, tile_rows):
        self.names, self.tile_rows, self.offsets, off = names, tile_rows, {}, 0
        for name in names:
            self.offsets[name] = off
            off += _piece_rows(name)
        self.rows = -(-off // tile_rows) * tile_rows

    def pack_local(self, vals):
        parts = []
        for name in self.names:
            val = vals[name]
            parts.append(_pad_cols((val.T if SHARDS[name][1] else val).reshape(-1), name))
        flat = jnp.concatenate(parts)
        return jnp.pad(flat, (0, self.rows * PACK_COLS - flat.shape[0])).reshape(self.rows, PACK_COLS)

    def pack_owner_major(self, grads):
        parts, rows = [], 0
        for name in self.names:
            grad, piece_rows = grads[name].astype(BF16), _piece_rows(name)
            if grad.shape != (N_DEV * piece_rows, PACK_COLS):
                grad = _pad_cols(grad.reshape(N_DEV, -1), name)
            parts.append(grad.reshape(N_DEV, piece_rows, PACK_COLS))
            rows += piece_rows
        if rows < self.rows:
            parts.append(jnp.zeros((N_DEV, self.rows - rows, PACK_COLS), BF16))
        return parts[0] if len(parts) == 1 else jnp.concatenate(parts, axis=1)

    def gathered_piece(self, gathered, name):
        shape = SHARDS[name][0]
        rows = gathered[:, self.offsets[name]:self.offsets[name] + _piece_rows(name), :]
        return rows.reshape(N_DEV, -1)[:, :shape[0] * shape[1]]

    def pieces(self, gathered, name):
        return _Pieces(gathered, self.offsets[name], _piece_rows(name))


GATHER_FFN1 = _Pack(("ffn1_w_gate", "ffn1_w_up", "ffn1_w_down"), BF16_ROWS)
GATHER_MIX = _Pack(("w_out", "ple_w_gate", "w_in", "ple_w_proj", "conv_w", "conv_w_mid", "conv_w_low"), BF16_ROWS)
GATHER_FFN2 = _Pack(("ffn2_w_gate", "ffn2_w_up", "ffn2_w_down"), BF16_ROWS)
SCATTER_LATE = _Pack(("ffn2_w_gate", "ffn2_w_up", "ffn2_w_down", "w_out", "ple_w_gate", "ple_w_proj"), BF16_ROWS)
SCATTER_IN = _Pack(("w_in", "conv_w"), BF16_ROWS)
SCATTER_GATE = _Pack(("ffn1_w_gate",), BF16_ROWS)
SCATTER_UP = _Pack(("ffn1_w_up",), BF16_ROWS)
SCATTER_DOWN = _Pack(("ffn1_w_down",), BF16_ROWS)


def _pack_small(vals):
    flat = jnp.concatenate([vals[name].reshape(-1).astype(F32) for name in SMALL])
    return jnp.pad(flat, (0, SMALL_ROWS * PACK_COLS - flat.shape[0])).reshape(SMALL_ROWS, PACK_COLS)


def _unpack_small(packed, shapes):
    out, off = {}, 0
    flat = packed.reshape(-1)
    for name in SMALL:
        n = 1
        for s in shapes[name]:
            n *= s
        out[name] = flat[off:off + n].reshape(shapes[name])
        off += n
    return out


WEIGHTS = ("ffn1_norm", "ffn1_w_gate", "ffn1_w_up", "ffn1_w_down", "mix_norm", "w_in", "gm_ln_g", "gm_ln_b", "gm_w_s",
           "gm_b_s", "gm_out_norm", "conv_w", "conv_b", "dt_bias", "a_log", "d_skip", "ssm_norm", "w_out", "ffn2_norm",
           "ffn2_w_gate", "ffn2_w_up", "ffn2_w_down", "ple_norm", "ple_w_gate", "ple_b_gate", "ple_w_proj", "final_norm")


def _step(x, p, target, w, m, v):
    local = lambda d: {name: d[name][0] for name in BIG}

    shards = {name: val.astype(BF16) for name, val in local(w).items()}
    conv_high = lax.reduce_precision(w["conv_w"][0], 8, 7)
    conv_mid = lax.reduce_precision(w["conv_w"][0] - conv_high, 8, 7)
    shards["conv_w"] = conv_high.astype(BF16)
    shards["conv_w_mid"] = conv_mid.astype(BF16)
    shards["conv_w_low"] = (w["conv_w"][0] - conv_high - conv_mid).astype(BF16)
    g_ffn1 = _comm_alone([_gather_comm(GATHER_FFN1.pack_local(shards))], "gather_ffn1")[0]

    row = lambda name: w[name].reshape(1, -1)
    gm_w_s = w["gm_w_s"][0]
    gm_b_st = jnp.transpose(w["gm_b_s"][0])
    ffn1 = (row("ffn1_norm"),) + tuple(GATHER_FFN1.pieces(g_ffn1, name) for name in GATHER_FFN1.names)
    gm = (row("gm_ln_g"), row("gm_ln_b"), gm_w_s, gm_b_st, row("gm_out_norm"))

    h1, n1, a1, b1, s1, g_mix = _ffn_fwd(x, *ffn1, "ffn1_fwd", comm=_gather_comm(GATHER_MIX.pack_local(shards)))
    w_in_t = GATHER_MIX.gathered_piece(g_mix, "w_in").reshape(IN_PROJ, D_MODEL)
    w_in_t = jnp.concatenate([w_in_t, jnp.zeros((IN_PROJ_PAD - IN_PROJ, D_MODEL), BF16)], axis=0)
    w_proj_t = GATHER_MIX.gathered_piece(g_mix, "ple_w_proj").reshape(D_MODEL, D_PLE)
    conv_w = sum(GATHER_MIX.gathered_piece(g_mix, name).astype(F32) for name in ("conv_w", "conv_w_mid", "conv_w_low"))
    conv_w = conv_w.reshape(CONV_DIM, SSM_CONV).T
    ssd = (conv_w, row("conv_b"), row("dt_bias"), row("a_log"), row("d_skip"), row("ssm_norm"))
    w_out = GATHER_MIX.pieces(g_mix, "w_out")

    proj, n2 = _mix_in_fwd(h1, row("mix_norm"), w_in_t)
    ya = _gm_fwd(proj, *gm)
    yb, s_all, g_ffn2 = _ssd_fwd(proj, *ssd, comm=_gather_comm(GATHER_FFN2.pack_local(shards)))
    ffn2 = (row("ffn2_norm"),) + tuple(GATHER_FFN2.pieces(g_ffn2, name) for name in GATHER_FFN2.names)
    h2 = _out_proj_fwd(h1, ya, yb, w_out)
    h3, n3, a3, b3, s3 = _ffn_fwd(h2, *ffn2, "ffn2_fwd")

    g, gp = {}, {}
    dh3, loss, gp["ple_w_gate"], d_w_proj, g["ple_norm"], g["ple_b_gate"], g["final_norm"] = _tail(
        h3, p, target, row("ple_norm"), GATHER_MIX.pieces(g_mix, "ple_w_gate"), row("ple_b_gate"), w_proj_t,
        row("final_norm"))
    gp["ple_w_proj"] = d_w_proj.T

    dh2, da3, db3, g["ffn2_norm"] = _ffn_dgrad(h2, dh3, a3, b3, *ffn2, "ffn2_dgrad")
    gp["ffn2_w_gate"] = _wgrad(n3, da3, 1408, "ffn2_wgrad_gate", transpose_out=True)
    gp["ffn2_w_up"] = _wgrad(n3, db3, 1408, "ffn2_wgrad_up", transpose_out=True)
    gp["ffn2_w_down"] = _wgrad(s3, dh3, 512, "ffn2_wgrad_down", scale=0.5, bk=1024)

    dya, dyb = _out_proj_dgrad(dh2, w_out)
    gp["w_out"] = jnp.concatenate([_wgrad(ya, dh2, 1024, "w_out_wgrad_a"), _wgrad(yb, dh2, 1024, "w_out_wgrad_b")], axis=0)

    dp_zxd, d_conv_w, g["conv_b"], g["dt_bias"], g["a_log"], g["d_skip"], g["ssm_norm"], parts_late = _ssd_bwd(
        proj, dyb, s_all, *ssd, comm=_exchange_comm(SCATTER_LATE.pack_owner_major(gp)))
    gp["conv_w"] = d_conv_w.T
    dp_uv, g["gm_ln_g"], g["gm_ln_b"], g["gm_w_s"], dbst, g["gm_out_norm"] = _gm_bwd(proj, dya, *gm)
    g["gm_b_s"] = jnp.transpose(dbst)

    parts = {}
    gp["w_in"] = jnp.concatenate([_wgrad(n2, dp_uv, 1024, "w_in_wgrad_uv", transpose_out=True),
                                  _wgrad(n2, dp_zxd, 896, "w_in_wgrad_zxd", transpose_out=True)], axis=0)[:IN_PROJ]
    dh1, g["mix_norm"], parts[SCATTER_IN] = _mix_in_dgrad(h1, dh2, dp_uv, dp_zxd, row("mix_norm"), w_in_t,
                                                          comm=_exchange_comm(SCATTER_IN.pack_owner_major(gp)))

    dx, da1, db1, g["ffn1_norm"] = _ffn_dgrad(x, dh1, a1, b1, *ffn1, "ffn1_dgrad")
    gp["ffn1_w_gate"] = _wgrad(n1, da1, 1408, "ffn1_wgrad_gate", transpose_out=True)
    gp["ffn1_w_up"], parts[SCATTER_GATE] = _wgrad(n1, db1, 1408, "ffn1_wgrad_up", transpose_out=True,
                                                  comm=_exchange_comm(SCATTER_GATE.pack_owner_major(gp)))
    gp["ffn1_w_down"], parts[SCATTER_UP] = _wgrad(s1, dh1, 512, "ffn1_wgrad_down", scale=0.5, bk=1024,
                                                  comm=_exchange_comm(SCATTER_UP.pack_owner_major(gp)))
    parts[SCATTER_DOWN], small_parts = _comm_alone(
        [_exchange_comm(SCATTER_DOWN.pack_owner_major(gp)), _gather_comm(_pack_small(g))], "scatter_ffn1_down_gather_small")
    parts[SCATTER_LATE] = parts_late

    res_big = {}
    for pack, pack_parts in parts.items():
        for name in pack.names:
            shape, transposed = SHARDS[name]
            if name in ("ple_w_proj", "conv_w"):
                nat = pack.gathered_piece(pack_parts, name).reshape((N_DEV,) + shape[::-1])
                res_big[name] = _sum_adamw(jnp.transpose(nat, (0, 2, 1)), w[name][0], m[name][0], v[name][0], shape[0],
                                           "adamw_" + name)
            else:
                res_big[name] = _adamw_shard(pack_parts, pack.offsets[name], transposed, w[name], m[name], v[name],
                                             "adamw_" + name, n_tiles=4 if name == "w_in" else 1)

    small_shapes = {name: w[name].shape for name in SMALL}
    res_small = _sum_adamw(small_parts, _pack_small(w), _pack_small(m), _pack_small(v), SMALL_ROWS, "adamw_small")
    res_small = [_unpack_small(r, small_shapes) for r in res_small]

    outs = []
    for k in range(4):
        for name in WEIGHTS:
            if name in res_small[k]:
                outs.append(res_small[k][name])
            else:
                outs.append(res_big[name][k].reshape(w[name].shape))
    return loss[0, 0], dx, outs


def kernel(x, p, ffn1_norm, ffn1_w_gate, ffn1_w_up, ffn1_w_down, mix_norm, w_in, gm_ln_g, gm_ln_b, gm_w_s, gm_b_s, gm_out_norm, conv_w, conv_b, dt_bias, a_log, d_skip, ssm_norm, w_out, ffn2_norm, ffn2_w_gate, ffn2_w_up, ffn2_w_down, ple_norm, ple_w_gate, ple_b_gate, ple_w_proj, final_norm, loss_target, m_ffn1_norm, m_ffn1_w_gate, m_ffn1_w_up, m_ffn1_w_down, m_mix_norm, m_w_in, m_gm_ln_g, m_gm_ln_b, m_gm_w_s, m_gm_b_s, m_gm_out_norm, m_conv_w, m_conv_b, m_dt_bias, m_a_log, m_d_skip, m_ssm_norm, m_w_out, m_ffn2_norm, m_ffn2_w_gate, m_ffn2_w_up, m_ffn2_w_down, m_ple_norm, m_ple_w_gate, m_ple_b_gate, m_ple_w_proj, m_final_norm, v_ffn1_norm, v_ffn1_w_gate, v_ffn1_w_up, v_ffn1_w_down, v_mix_norm, v_w_in, v_gm_ln_g, v_gm_ln_b, v_gm_w_s, v_gm_b_s, v_gm_out_norm, v_conv_w, v_conv_b, v_dt_bias, v_a_log, v_d_skip, v_ssm_norm, v_w_out, v_ffn2_norm, v_ffn2_w_gate, v_ffn2_w_up, v_ffn2_w_down, v_ple_norm, v_ple_w_gate, v_ple_b_gate, v_ple_w_proj, v_final_norm):
    args = locals()
    w = {name: args[name] for name in WEIGHTS}
    m = {name: args["m_" + name] for name in WEIGHTS}
    v = {name: args["v_" + name] for name in WEIGHTS}
    loss, dx, outs = _step(x[0], p[0, 0], loss_target[0], w, m, v)
    loss = lax.psum(loss, AXES)
    return (loss, dx[None], *outs)
```

```python
import functools
from typing import NamedTuple

import jax
import jax.numpy as jnp
from jax import lax
from jax.experimental import pallas as pl
from jax.experimental.pallas import tpu as pltpu

F32 = jnp.float32
BF16 = jnp.bfloat16
HIGHEST = lax.Precision.HIGHEST
MESH = pl.DeviceIdType.MESH
AXES = ("x", "y", "c")
N_DEV = 8

D_MODEL = 1024
D_FF = 2816
D_PLE = 256
GM_WIDTH = 1024
GM_HEADS = 8
GM_HEAD_DIM = 128
CHUNK = 128
SSM_WIDTH = 1024
SSM_HEADS = 16
SSM_HEAD_DIM = 64
SSM_GROUPS = 2
SSM_STATE = 128
SSM_CONV = 4
CONV_DIM = SSM_WIDTH + 2 * SSM_GROUPS * SSM_STATE
IN_PROJ = 2 * GM_WIDTH + SSM_WIDTH + CONV_DIM + SSM_HEADS
LANES = 128
BF16_ROWS = 16
IN_PROJ_PAD = IN_PROJ - SSM_HEADS + LANES
UV_W = 2 * GM_WIDTH
ZXD_W = IN_PROJ_PAD - UV_W
HALO = 8
EPS = 1e-6

ADAM_LR = 0.001
ADAM_B1 = 0.9
ADAM_B2 = 0.999
ADAM_EPS = 1e-08
ADAM_WD = 0.01
ADAM_STEP = 10

VMEM_LIMIT = 56 * 1024 * 1024
PACK_COLS = 1024


def _rms(x, g):
    return x * lax.rsqrt(jnp.mean(x * x, axis=-1, keepdims=True) + EPS) * g


def _gelu(x):
    return 0.5 * x * (1.0 + lax.erf(x * (2.0 ** -0.5)))


def _silu(x):
    return x * jax.nn.sigmoid(x)


def _dot(a, b):
    return jnp.dot(a.astype(BF16), b.astype(BF16), preferred_element_type=F32)


def _dot_nt(a, b):
    return lax.dot_general(a.astype(BF16), b.astype(BF16), (((1,), (1,)), ((), ())), preferred_element_type=F32)


def _dot_tn(a, b):
    return lax.dot_general(a.astype(BF16), b.astype(BF16), (((0,), (0,)), ((), ())), preferred_element_type=F32)


def _hdot_tn(a, b):
    return lax.dot_general(a, b, (((0,), (0,)), ((), ())), precision=HIGHEST, preferred_element_type=F32)


def _split3(x):
    hi = x.astype(BF16)
    rest = x - hi.astype(F32)
    mid = rest.astype(BF16)
    return hi, mid, (rest - mid.astype(F32)).astype(BF16)


def _exact_dot(x, mask, dims, x_first=True):
    terms = [lax.dot_general(*((t, mask) if x_first else (mask, t)), (dims, ((), ())), preferred_element_type=F32)
             for t in _split3(x)]
    return (terms[0] + terms[1]) + terms[2]


def _mask_product(fwd_dims, fwd_x_first, bwd_dims, bwd_x_first):
    @jax.custom_vjp
    def product(x, mask):
        return _exact_dot(x, mask, fwd_dims, fwd_x_first)

    def fwd(x, mask):
        return product(x, mask), mask

    def bwd(mask, g):
        return _exact_dot(g, mask, bwd_dims, bwd_x_first), jnp.zeros_like(mask)

    product.defvjp(fwd, bwd)
    return product


_widen = _mask_product(((1,), (0,)), True, ((1,), (1,)), True)
_cumsum_rows = _mask_product(((1,), (0,)), False, ((0,), (0,)), False)
_cumsum_cols = _mask_product(((0,), (0,)), True, ((1,), (1,)), False)


class _Pieces(NamedTuple):
    gathered: jax.Array
    row_off: int
    rows: int


class _Comm(NamedTuple):
    phases: object
    src: jax.Array
    dst: jax.ShapeDtypeStruct


def _tiled(body, name, n_steps, tiled_in, full_in, big_in, tiled_out, acc_out, scratch=(), reverse=False, comm=None):
    n_t, n_f, n_b, n_to, n_a = len(tiled_in), len(full_in), len(big_in), len(tiled_out), len(acc_out)
    n_c = 1 if comm else 0

    def row(i):
        return n_steps - 1 - i if reverse else i

    in_specs, args = [], []
    for arr, br, bc, cb in tiled_in:
        if callable(cb):
            in_specs.append(pl.BlockSpec((br, bc), cb))
        else:
            in_specs.append(pl.BlockSpec((br, bc), functools.partial(lambda i, cb: (row(i), cb), cb=cb)))
        args.append(arr)
    for arr in full_in:
        in_specs.append(pl.BlockSpec(arr.shape, functools.partial(lambda i, nd: (0,) * nd, nd=arr.ndim)))
        args.append(arr)
    big_shapes, n_copies = [], 0
    for big in big_in:
        in_specs.append(pl.BlockSpec(memory_space=pl.ANY))
        if isinstance(big, _Pieces):
            args.append(big.gathered)
            big_shapes.append(((N_DEV * big.rows, PACK_COLS), big.gathered.dtype))
            n_copies += N_DEV
        else:
            args.append(big)
            big_shapes.append((big.shape, big.dtype))
            n_copies += 1
    if comm:
        in_specs.append(pl.BlockSpec(memory_space=pl.ANY))
        args.append(comm.src)
    out_specs, out_shape = [], []
    for rows, cols, dt, br in tiled_out:
        out_specs.append(pl.BlockSpec((br, cols), lambda i: (row(i), 0)))
        out_shape.append(jax.ShapeDtypeStruct((rows, cols), dt))
    for shp, dt in acc_out:
        out_specs.append(pl.BlockSpec(shp, functools.partial(lambda i, nd: (0,) * nd, nd=len(shp))))
        out_shape.append(jax.ShapeDtypeStruct(shp, dt))
    if comm:
        out_specs.append(pl.BlockSpec(memory_space=pl.ANY))
        out_shape.append(comm.dst)
    scratch_shapes = [pltpu.VMEM(shp, dt) for shp, dt in big_shapes] + list(scratch)
    if n_copies:
        scratch_shapes.append(pltpu.SemaphoreType.DMA((n_copies,)))
    if comm:
        scratch_shapes += [pltpu.SemaphoreType.DMA((N_DEV - 1,)), pltpu.SemaphoreType.DMA((N_DEV - 1,)), pltpu.SemaphoreType.DMA]

    def kern(*refs):
        n_in = n_t + n_f + n_b + n_c
        ins = refs[: n_t + n_f]
        big_hbm = refs[n_t + n_f : n_t + n_f + n_b]
        outs = refs[n_in : n_in + n_to + n_a]
        rest = refs[n_in + n_to + n_a + n_c :]
        big_vmem, scr = rest[:n_b], rest[n_b:]
        if comm:
            scr, comm_sems = scr[:-3], scr[-3:]
            comm_start, comm_mid, comm_finish = comm.phases(refs[n_in - 1], refs[n_in + n_to + n_a], *comm_sems)
        if n_copies:
            scr, copy_sems = scr[:-1], scr[-1]
        step = pl.program_id(0)

        @pl.when(step == 0)
        def _():
            copies = []
            for big, src, dst in zip(big_in, big_hbm, big_vmem):
                if isinstance(big, _Pieces):
                    for j in range(N_DEV):
                        copies.append((src.at[j, pl.ds(big.row_off, big.rows), :], dst.at[pl.ds(j * big.rows, big.rows), :]))
                else:
                    copies.append((src, dst))
            copies = [pltpu.make_async_copy(a, b, copy_sems.at[k]) for k, (a, b) in enumerate(copies)]
            for cp in copies:
                cp.start()
            for cp in copies:
                cp.wait()
            for acc in outs[n_to:]:
                acc[...] = jnp.zeros(acc.shape, acc.dtype)
            if comm:
                comm_start()

        body(row(step), *ins, *big_vmem, *outs, *scr)
        if comm:
            pl.when(step == (n_steps - 1) // 2)(comm_mid)
            pl.when(step == n_steps - 1)(comm_finish)

    res = pl.pallas_call(
        kern,
        out_shape=out_shape,
        grid=(n_steps,),
        in_specs=in_specs,
        out_specs=out_specs,
        scratch_shapes=scratch_shapes,
        name=name,
        compiler_params=pltpu.CompilerParams(dimension_semantics=("arbitrary",), vmem_limit_bytes=VMEM_LIMIT),
    )(*args)
    return res


FF_CHUNKS = ((0, 1536), (1536, D_FF))
FFN_TM = 256


def _ffn_fwd(h, g, wg_t, wu_t, wd, name, comm=None):
    T = h.shape[0]

    def body(i, h_ref, g_ref, wg_ref, wu_ref, wd_ref, o_ref, n_ref, a_ref, b_ref, s_ref):
        x = h_ref[...]
        n = _rms(x, g_ref[...]).astype(BF16)
        n_ref[...] = n
        f = jnp.zeros(x.shape, F32)
        for lo, hi in FF_CHUNKS:
            a = _dot_nt(n, wg_ref[lo:hi, :])
            b = _dot_nt(n, wu_ref[lo:hi, :])
            s = (_silu(a) * b).astype(BF16)
            a_ref[:, lo:hi] = a.astype(BF16)
            b_ref[:, lo:hi] = b.astype(BF16)
            s_ref[:, lo:hi] = s
            f = f + jnp.dot(s, wd_ref[lo:hi, :], preferred_element_type=F32)
        o_ref[...] = x + 0.5 * f

    return _tiled(body, name, T // FFN_TM, [(h, FFN_TM, D_MODEL, 0)], [g], [wg_t, wu_t, wd],
                  [(T, D_MODEL, F32, FFN_TM), (T, D_MODEL, BF16, FFN_TM), (T, D_FF, BF16, FFN_TM), (T, D_FF, BF16, FFN_TM),
                   (T, D_FF, BF16, FFN_TM)], [], comm=comm)


def _ffn_dgrad(h, dout, a16, b16, g, wg_t, wu_t, wd, name):
    T = h.shape[0]

    def body(i, h_ref, do_ref, a_ref, b_ref, g_ref, wg_ref, wu_ref, wd_ref, dh_ref, da_ref, db_ref, dg_ref):
        dout = do_ref[...]
        _, rms_vjp = jax.vjp(_rms, h_ref[...], g_ref[...])
        dfo = (0.5 * dout).astype(BF16)
        dn = jnp.zeros(dout.shape, F32)
        for lo, hi in FF_CHUNKS:
            a = a_ref[:, lo:hi].astype(F32)
            b = b_ref[:, lo:hi].astype(F32)
            sg = jax.nn.sigmoid(a)
            ds = _dot_nt(dfo, wd_ref[lo:hi, :])
            db = (ds * (a * sg)).astype(BF16)
            da = (ds * b * (sg * (1.0 + a * (1.0 - sg)))).astype(BF16)
            dn = dn + _dot(da, wg_ref[lo:hi, :]) + _dot(db, wu_ref[lo:hi, :])
            da_ref[:, lo:hi] = da
            db_ref[:, lo:hi] = db
        dx, dg = rms_vjp(dn)
        dh_ref[...] = dout + dx
        dg_ref[...] += dg

    return _tiled(body, name, T // FFN_TM,
                  [(h, FFN_TM, D_MODEL, 0), (dout, FFN_TM, D_MODEL, 0), (a16, FFN_TM, D_FF, 0), (b16, FFN_TM, D_FF, 0)],
                  [g], [wg_t, wu_t, wd],
                  [(T, D_MODEL, F32, FFN_TM), (T, D_FF, BF16, FFN_TM), (T, D_FF, BF16, FFN_TM)], [((1, D_MODEL), F32)])


def _wgrad(a, b, bn, name, scale=None, transpose_out=False, bk=2048, comm=None):
    T, M = a.shape
    N = b.shape[1]
    bk = min(bk, T)
    assert M % LANES == 0 and N % bn == 0 and T % bk == 0
    n_j, n_k = N // bn, T // bk
    n_c = 1 if comm else 0

    def kern(*refs):
        a_ref, b_ref, o_ref, acc_ref = refs[0], refs[1], refs[2 + n_c], refs[3 + 2 * n_c]
        j, k = pl.program_id(0), pl.program_id(1)
        if comm:
            comm_start, _, comm_finish = comm.phases(refs[2], refs[4], *refs[6:])
            pl.when((j == 0) & (k == 0))(comm_start)

        @pl.when(k == 0)
        def _():
            acc_ref[...] = jnp.zeros(acc_ref.shape, F32)

        bv = b_ref[...]
        if scale is not None:
            bv = bv * scale
        acc_ref[...] += _dot_tn(a_ref[...], bv)

        @pl.when(k == n_k - 1)
        def _():
            acc = acc_ref[...]
            o_ref[...] = (acc.T if transpose_out else acc).astype(BF16)

        if comm:
            pl.when((j == n_j - 1) & (k == n_k - 1))(comm_finish)

    if transpose_out:
        out_shape, out_spec = (N, M), pl.BlockSpec((bn, M), lambda j, k: (j, 0))
    else:
        out_shape, out_spec = (M, N), pl.BlockSpec((M, bn), lambda j, k: (0, j))
    any_spec = pl.BlockSpec(memory_space=pl.ANY)
    comm_sems = [pltpu.SemaphoreType.DMA((N_DEV - 1,)), pltpu.SemaphoreType.DMA((N_DEV - 1,)), pltpu.SemaphoreType.DMA]
    res = pl.pallas_call(
        kern,
        out_shape=[jax.ShapeDtypeStruct(out_shape, BF16)] + ([comm.dst] if comm else []),
        grid=(n_j, n_k),
        in_specs=[pl.BlockSpec((bk, M), lambda j, k: (k, 0)), pl.BlockSpec((bk, bn), lambda j, k: (k, j))] + [any_spec] * n_c,
        out_specs=[out_spec] + [any_spec] * n_c,
        scratch_shapes=[pltpu.VMEM((M, bn), F32)] + (comm_sems if comm else []),
        name=name,
        compiler_params=pltpu.CompilerParams(dimension_semantics=("arbitrary", "arbitrary"), vmem_limit_bytes=VMEM_LIMIT),
    )(a, b, *([comm.src] if comm else []))
    return res if comm else res[0]


PROJ_TM = 256


def _mix_in_fwd(h, g, w_in_t):
    T = h.shape[0]

    def body(i, h_ref, g_ref, w_ref, p_ref, n_ref):
        n = _rms(h_ref[...], g_ref[...]).astype(BF16)
        n_ref[...] = n
        p_ref[...] = _dot_nt(n, w_ref[...])

    return _tiled(body, "mix_in_fwd", T // PROJ_TM, [(h, PROJ_TM, D_MODEL, 0)], [g], [w_in_t],
                  [(T, IN_PROJ_PAD, F32, PROJ_TM), (T, D_MODEL, BF16, PROJ_TM)], [])


def _mix_in_dgrad(h, dh_in, dp_uv, dp_zxd, g, w_in_t, comm=None):
    T = h.shape[0]

    def body(i, h_ref, dh_ref, duv_ref, dzxd_ref, g_ref, w_ref, o_ref, dg_ref):
        dn = _dot(duv_ref[...], w_ref[:UV_W, :]) + _dot(dzxd_ref[...], w_ref[UV_W:, :])
        _, rms_vjp = jax.vjp(_rms, h_ref[...], g_ref[...])
        dx, dg = rms_vjp(dn)
        o_ref[...] = dh_ref[...] + dx
        dg_ref[...] += dg

    return _tiled(body, "mix_in_dgrad", T // PROJ_TM,
                  [(h, PROJ_TM, D_MODEL, 0), (dh_in, PROJ_TM, D_MODEL, 0), (dp_uv, PROJ_TM, UV_W, 0),
                   (dp_zxd, PROJ_TM, ZXD_W, 0)], [g], [w_in_t],
                  [(T, D_MODEL, F32, PROJ_TM)], [((1, D_MODEL), F32)], comm=comm)


def _out_proj_fwd(h, ya, yb, w_out):
    T = h.shape[0]

    def body(i, h_ref, ya_ref, yb_ref, w_ref, o_ref):
        o_ref[...] = (h_ref[...] + jnp.dot(ya_ref[...], w_ref[:GM_WIDTH, :], preferred_element_type=F32)
                      + jnp.dot(yb_ref[...], w_ref[GM_WIDTH:, :], preferred_element_type=F32))

    return _tiled(body, "out_proj_fwd", T // PROJ_TM,
                  [(h, PROJ_TM, D_MODEL, 0), (ya, PROJ_TM, GM_WIDTH, 0), (yb, PROJ_TM, SSM_WIDTH, 0)], [], [w_out],
                  [(T, D_MODEL, F32, PROJ_TM)], [])[0]


def _out_proj_dgrad(dh, w_out):
    T = dh.shape[0]

    def body(i, dh_ref, w_ref, dya_ref, dyb_ref):
        d = dh_ref[...].astype(BF16)
        dya_ref[...] = _dot_nt(d, w_ref[:GM_WIDTH, :])
        dyb_ref[...] = _dot_nt(d, w_ref[GM_WIDTH:, :])

    return _tiled(body, "out_proj_dgrad", T // PROJ_TM, [(dh, PROJ_TM, D_MODEL, 0)], [], [w_out],
                  [(T, GM_WIDTH, F32, PROJ_TM), (T, SSM_WIDTH, F32, PROJ_TM)], [])


def _gm_chunk(u, v, ln_g, ln_b, b_st, out_g, *w_heads):
    ug = _gelu(u)
    vg = _gelu(v)
    mu = jnp.mean(vg, axis=-1, keepdims=True)
    xc = vg - mu
    vn = xc * lax.rsqrt(jnp.mean(xc * xc, axis=-1, keepdims=True) + EPS) * ln_g + ln_b
    t_idx = lax.broadcasted_iota(jnp.int32, (CHUNK, CHUNK), 0)
    s_idx = lax.broadcasted_iota(jnp.int32, (CHUNK, CHUNK), 1)
    causal = t_idx >= s_idx
    mixed = []
    for hd in range(GM_HEADS):
        wm = jnp.where(causal, w_heads[hd], 0.0)
        cols = slice(hd * GM_HEAD_DIM, (hd + 1) * GM_HEAD_DIM)
        mixed.append(_dot(wm, vn[:, cols]) + b_st[:, hd:hd + 1])
    ya0 = ug * jnp.concatenate(mixed, axis=1)
    return _rms(ya0, out_g)


def _gm_fwd(proj, ln_g, ln_b, w_s, b_st, out_g):
    T = proj.shape[0]

    def body(i, u_ref, v_ref, lg_ref, lb_ref, w_ref, bs_ref, og_ref, ya_ref):
        w_heads = [w_ref[hd] for hd in range(GM_HEADS)]
        ya = _gm_chunk(u_ref[...], v_ref[...], lg_ref[...], lb_ref[...], bs_ref[...], og_ref[...], *w_heads)
        ya_ref[...] = ya.astype(BF16)

    return _tiled(body, "gmlp_fwd", T // CHUNK, [(proj, CHUNK, GM_WIDTH, 0), (proj, CHUNK, GM_WIDTH, 1)],
                  [ln_g, ln_b, w_s, b_st, out_g], [], [(T, GM_WIDTH, BF16, CHUNK)], [])[0]


def _gm_bwd(proj, dya, ln_g, ln_b, w_s, b_st, out_g):
    T = proj.shape[0]

    def body(i, u_ref, v_ref, dy_ref, lg_ref, lb_ref, w_ref, bs_ref, og_ref, duv_ref, dlg_ref, dlb_ref, dw_ref, dbs_ref,
             dog_ref):
        w_heads = [w_ref[hd] for hd in range(GM_HEADS)]
        _, vjp = jax.vjp(_gm_chunk, u_ref[...], v_ref[...], lg_ref[...], lb_ref[...], bs_ref[...], og_ref[...], *w_heads)
        grads = vjp(dy_ref[...])
        duv_ref[:, :GM_WIDTH] = grads[0].astype(BF16)
        duv_ref[:, GM_WIDTH:] = grads[1].astype(BF16)
        dlg_ref[...] += grads[2]
        dlb_ref[...] += grads[3]
        dbs_ref[...] += grads[4]
        dog_ref[...] += grads[5]
        for hd in range(GM_HEADS):
            dw_ref[hd] += grads[6 + hd]

    return _tiled(body, "gmlp_bwd", T // CHUNK,
                  [(proj, CHUNK, GM_WIDTH, 0), (proj, CHUNK, GM_WIDTH, 1), (dya, CHUNK, GM_WIDTH, 0)],
                  [ln_g, ln_b, w_s, b_st, out_g], [], [(T, UV_W, BF16, CHUNK)],
                  [((1, GM_WIDTH), F32), ((1, GM_WIDTH), F32), ((GM_HEADS, CHUNK, CHUNK), F32),
                   ((CHUNK, GM_HEADS), F32), ((1, GM_WIDTH), F32)])


def _ssd_chunk(xc, z, dtr, s_in, dt_bias, a_log, d_skip, norm_g):
    half = SSM_WIDTH // SSM_GROUPS
    l_idx = lax.broadcasted_iota(jnp.int32, (CHUNK, CHUNK), 0)
    s_idx = lax.broadcasted_iota(jnp.int32, (CHUNK, CHUNK), 1)
    causal = l_idx >= s_idx
    head_of_col = lax.broadcasted_iota(jnp.int32, (SSM_HEADS, SSM_WIDTH), 1) // SSM_HEAD_DIM
    expand = (head_of_col == lax.broadcasted_iota(jnp.int32, (SSM_HEADS, SSM_WIDTH), 0)).astype(BF16)

    xcs = _silu(xc)
    xs = xcs[:, :SSM_WIDTH]
    dt = jax.nn.softplus(dtr + dt_bias)
    adt = dt * (-jnp.exp(a_log))
    acs = _cumsum_rows(adt, causal.astype(BF16))
    acs_t = _cumsum_cols(adt, (l_idx <= s_idx).astype(BF16))
    tot = acs[CHUNK - 1:CHUNK, :]
    dt_w = _widen(dt, expand)
    out_decay_w = _widen(jnp.exp(acs), expand)
    state_decay_w = _widen(jnp.exp(tot - acs), expand)
    chunk_decay_w = _widen(jnp.exp(tot), expand)
    d_skip_w = _widen(d_skip, expand)
    xdt = xs * dt_w
    xdt_decayed = xdt * state_decay_w

    y_diag, y_off, states = [], [], []
    for grp in range(SSM_GROUPS):
        b0 = SSM_WIDTH + grp * SSM_STATE
        c0 = SSM_WIDTH + SSM_GROUPS * SSM_STATE + grp * SSM_STATE
        bm = xcs[:, b0:b0 + SSM_STATE].astype(BF16)
        cm = xcs[:, c0:c0 + SSM_STATE].astype(BF16)
        cb = _dot_nt(cm, bm)
        for k in range(grp * SSM_HEADS // SSM_GROUPS, (grp + 1) * SSM_HEADS // SSM_GROUPS):
            decay = jnp.exp(jnp.where(causal, acs[:, k:k + 1] - acs_t[k:k + 1, :], -jnp.inf))
            y_diag.append(_dot(cb * decay, xdt[:, k * SSM_HEAD_DIM:(k + 1) * SSM_HEAD_DIM]))
        cols = slice(grp * half, (grp + 1) * half)
        states.append(_dot_tn(bm, xdt_decayed[:, cols]))
        y_off.append(_dot(cm, s_in[:, cols]))
    y = jnp.concatenate(y_diag, axis=1) + jnp.concatenate(y_off, axis=1) * out_decay_w + xs * d_skip_w
    s_out = s_in * chunk_decay_w + jnp.concatenate(states, axis=1)
    y = y * _silu(z)
    normed = []
    for grp in range(SSM_GROUPS):
        yg = y[:, grp * half:(grp + 1) * half]
        normed.append(yg * lax.rsqrt(jnp.mean(yg * yg, axis=-1, keepdims=True) + EPS))
    return jnp.concatenate(normed, axis=1) * norm_g, s_out


def _conv_taps(ext_ref, w, b):
    taps = [ext_ref[pl.ds(HALO - (SSM_CONV - 1) + k, CHUNK), :] for k in range(SSM_CONV)]
    y = b
    for k in range(SSM_CONV):
        y = y + w[k:k + 1, :] * taps[k]
    return y, taps


def _ssd_fwd(proj, conv_w, conv_b, dt_bias, a_log, d_skip, norm_g, comm=None):
    T = proj.shape[0]
    n_chunks = T // CHUNK

    def body(i, z_ref, x_ref, dt_ref, cw_ref, cb_ref, dtb_ref, al_ref, dsk_ref, ng_ref, yb_ref, sin_ref, ext_ref, st_ref):
        @pl.when(i == 0)
        def _():
            ext_ref[0:HALO, :] = jnp.zeros((HALO, CONV_DIM), F32)
            st_ref[...] = jnp.zeros(st_ref.shape, F32)

        ext_ref[HALO:, :] = x_ref[...]
        xc, _ = _conv_taps(ext_ref, cw_ref[...], cb_ref[...])
        s_in = st_ref[...]
        yb, s_out = _ssd_chunk(xc, z_ref[...], dt_ref[:, 0:SSM_HEADS], s_in, dtb_ref[...], al_ref[...], dsk_ref[...],
                               ng_ref[...])
        yb_ref[...] = yb.astype(BF16)
        sin_ref[...] = s_in
        st_ref[...] = s_out
        ext_ref[0:HALO, :] = ext_ref[CHUNK:CHUNK + HALO, :]

    z_blk = 2 * GM_WIDTH // SSM_WIDTH
    x_blk = (2 * GM_WIDTH + SSM_WIDTH) // CONV_DIM
    dt_blk = (2 * GM_WIDTH + SSM_WIDTH + CONV_DIM) // LANES
    return _tiled(body, "ssd_fwd", n_chunks,
                  [(proj, CHUNK, SSM_WIDTH, z_blk), (proj, CHUNK, CONV_DIM, x_blk), (proj, CHUNK, LANES, dt_blk)],
                  [conv_w, conv_b, dt_bias, a_log, d_skip, norm_g], [],
                  [(T, SSM_WIDTH, BF16, CHUNK), (n_chunks * SSM_STATE, SSM_WIDTH, F32, SSM_STATE)], [],
                  scratch=[pltpu.VMEM((HALO + CHUNK, CONV_DIM), F32), pltpu.VMEM((SSM_STATE, SSM_WIDTH), F32)], comm=comm)


def _ssd_bwd(proj, dyb, s_all, conv_w, conv_b, dt_bias, a_log, d_skip, norm_g, comm=None):
    T = proj.shape[0]
    n_chunks = T // CHUNK
    z_blk = 2 * GM_WIDTH // SSM_WIDTH
    x_blk = (2 * GM_WIDTH + SSM_WIDTH) // CONV_DIM
    dt_blk = (2 * GM_WIDTH + SSM_WIDTH + CONV_DIM) // LANES
    rows_per_halo = CHUNK // HALO

    def body(i, z_ref, x_ref, halo_ref, dt_ref, dy_ref, sin_ref, cw_ref, cb_ref, dtb_ref, al_ref, dsk_ref, ng_ref,
             dzxd_ref, dcw_ref, dcb_ref, ddtb_ref, dal_ref, ddsk_ref, dng_ref, ext_ref, dext_ref, dst_ref):
        @pl.when(i == n_chunks - 1)
        def _():
            dext_ref[CHUNK:, :] = jnp.zeros((HALO, CONV_DIM), F32)
            dst_ref[...] = jnp.zeros(dst_ref.shape, F32)

        halo = halo_ref[...]
        ext_ref[0:HALO, :] = jnp.where(i == 0, jnp.zeros_like(halo), halo)
        ext_ref[HALO:, :] = x_ref[...]
        cw = cw_ref[...]
        xc, taps = _conv_taps(ext_ref, cw, cb_ref[...])
        _, vjp = jax.vjp(_ssd_chunk, xc, z_ref[...], dt_ref[:, 0:SSM_HEADS], sin_ref[...], dtb_ref[...], al_ref[...],
                         dsk_ref[...], ng_ref[...])
        dxc, dz, ddtr, ds_in, ddtb, dal, ddsk, dng = vjp((dy_ref[...], dst_ref[...]))
        dst_ref[...] = ds_in
        ddtb_ref[...] += ddtb
        dal_ref[...] += dal
        ddsk_ref[...] += ddsk
        dng_ref[...] += dng
        dext_ref[0:CHUNK, :] = dxc
        dx = jnp.zeros((CHUNK, CONV_DIM), F32)
        for k in range(SSM_CONV):
            dx = dx + cw[k:k + 1, :] * dext_ref[pl.ds(SSM_CONV - 1 - k, CHUNK), :]
            dcw_ref[k:k + 1, :] += jnp.sum(dxc * taps[k], axis=0, keepdims=True)
        dcb_ref[...] += jnp.sum(dxc, axis=0, keepdims=True)
        dext_ref[CHUNK:, :] = dext_ref[0:HALO, :]
        dzxd_ref[:, 0:SSM_WIDTH] = dz.astype(BF16)
        dzxd_ref[:, SSM_WIDTH:SSM_WIDTH + CONV_DIM] = dx.astype(BF16)
        dzxd_ref[:, SSM_WIDTH + CONV_DIM:] = jnp.concatenate(
            [ddtr, jnp.zeros((CHUNK, LANES - SSM_HEADS), F32)], axis=1).astype(BF16)

    def halo_index(step):
        c = n_chunks - 1 - step
        return (jnp.maximum(c * rows_per_halo - 1, 0), x_blk)

    return _tiled(body, "ssd_bwd", n_chunks,
                  [(proj, CHUNK, SSM_WIDTH, z_blk), (proj, CHUNK, CONV_DIM, x_blk), (proj, HALO, CONV_DIM, halo_index),
                   (proj, CHUNK, LANES, dt_blk), (dyb, CHUNK, SSM_WIDTH, 0), (s_all, SSM_STATE, SSM_WIDTH, 0)],
                  [conv_w, conv_b, dt_bias, a_log, d_skip, norm_g], [],
                  [(T, ZXD_W, BF16, CHUNK)],
                  [((SSM_CONV, CONV_DIM), F32), ((1, CONV_DIM), F32), ((1, SSM_HEADS), F32), ((1, SSM_HEADS), F32),
                   ((1, SSM_HEADS), F32), ((1, SSM_WIDTH), F32)],
                  scratch=[pltpu.VMEM((HALO + CHUNK, CONV_DIM), F32), pltpu.VMEM((CHUNK + HALO, CONV_DIM), F32),
                           pltpu.VMEM((SSM_STATE, SSM_WIDTH), F32)],
                  reverse=True, comm=comm)


TAIL_TM = 512


def _tail(h, p, target, ple_norm, w_gate, b_gate, w_proj_t, final_norm):
    T = h.shape[0]

    def head(x, pre, pp, b_g, f_norm, tgt):
        gate = jax.nn.sigmoid(pre + b_g)
        out = _rms(x + gate * pp, f_norm)
        err = out - tgt
        return 0.5 * jnp.sum(jnp.mean(err * err, axis=-1, keepdims=True), axis=0, keepdims=True)

    def body(i, h_ref, p_ref, t_ref, pn_ref, bg_ref, fn_ref, wg_ref, wp_ref, dh_ref, loss_ref, dwg_ref, dwp_ref, dpn_ref,
             dbg_ref, dfn_ref):
        x = h_ref[...]
        n4f, n_vjp = jax.vjp(_rms, x, pn_ref[...])
        n4 = n4f.astype(BF16)
        pre = jnp.dot(n4, wg_ref[...], preferred_element_type=F32)
        p16 = p_ref[...].astype(BF16)
        pp = _dot_nt(p16, wp_ref[...])
        loss, h_vjp = jax.vjp(functools.partial(head, tgt=t_ref[...]), x, pre, pp, bg_ref[...], fn_ref[...])
        dx, dpre, dpp, dbg, dfn = h_vjp(jnp.ones((1, 1), F32))
        dpre16 = dpre.astype(BF16)
        dn4 = _dot_nt(dpre16, wg_ref[...])
        dx2, dpn = n_vjp(dn4)
        dh_ref[...] = dx + dx2
        loss_ref[...] += loss
        dwg_ref[...] += _dot_tn(n4, dpre16)
        dwp_ref[...] += _dot_tn(p16, dpp)
        dpn_ref[...] += dpn
        dbg_ref[...] += dbg
        dfn_ref[...] += dfn

    return _tiled(body, "tail", T // TAIL_TM,
                  [(h, TAIL_TM, D_MODEL, 0), (p, TAIL_TM, D_PLE, 0), (target, TAIL_TM, D_MODEL, 0)],
                  [ple_norm, b_gate, final_norm], [w_gate, w_proj_t],
                  [(T, D_MODEL, F32, TAIL_TM)],
                  [((1, 1), F32), ((D_MODEL, D_MODEL), F32), ((D_PLE, D_MODEL), F32), ((1, D_MODEL), F32),
                   ((1, D_MODEL), F32), ((1, D_MODEL), F32)])


def _gather_phases(x_ref, out_ref, send_sems, recv_sems, local_sem):
    mx, my, mc = lax.axis_index("x"), lax.axis_index("y"), lax.axis_index("c")
    me, sibling = (mx, my, mc), (mx, my, 1 - mc)
    chips = [(1 - mx, my), (mx, 1 - my), (1 - mx, 1 - my)]

    def rows(px, py, pc):
        return out_ref.at[4 * px + 2 * py + pc]

    def copy(k, block, to, src=None):
        return pltpu.make_async_remote_copy(
            src_ref=rows(*block) if src is None else src, dst_ref=rows(*block),
            send_sem=send_sems.at[k], recv_sem=recv_sems.at[k], device_id=to, device_id_type=MESH)

    mine = pltpu.make_async_copy(x_ref, rows(*me), local_sem)
    first = [copy(0, me, sibling, src=x_ref)] + [copy(1 + j, me, (*chip, mc), src=x_ref) for j, chip in enumerate(chips)]
    passed = [copy(4 + j, (*chip, mc), sibling) for j, chip in enumerate(chips)]

    def start():
        mine.start()
        for cp in first:
            cp.start()

    def mid():
        for j, chip in enumerate(chips):
            copy(1 + j, (*chip, mc), me).wait_recv()
            passed[j].start()

    def finish():
        copy(0, sibling, me).wait_recv()
        for j, chip in enumerate(chips):
            copy(4 + j, (*chip, 1 - mc), me).wait_recv()
        for cp in first + passed:
            cp.wait_send()
        mine.wait()

    return start, mid, finish


def _exchange_phases(x_ref, out_ref, send_sems, recv_sems, local_sem):
    mx, my, mc = lax.axis_index("x"), lax.axis_index("y"), lax.axis_index("c")
    me = 4 * mx + 2 * my + mc
    mine = pltpu.make_async_copy(x_ref.at[me], out_ref.at[me], local_sem)
    copies = []
    for k in range(1, N_DEV):
        px = 1 - mx if k & 4 else mx
        py = 1 - my if k & 2 else my
        pc = 1 - mc if k & 1 else mc
        copies.append(pltpu.make_async_remote_copy(
            src_ref=x_ref.at[4 * px + 2 * py + pc], dst_ref=out_ref.at[me], send_sem=send_sems.at[k - 1],
            recv_sem=recv_sems.at[k - 1], device_id=(px, py, pc), device_id_type=MESH))

    def start():
        mine.start()
        for cp in copies:
            cp.start()

    def finish():
        for cp in copies:
            cp.wait_recv()
        for cp in copies:
            cp.wait_send()
        mine.wait()

    return start, lambda: None, finish


def _gather_comm(x):
    return _Comm(_gather_phases, x, jax.ShapeDtypeStruct((N_DEV,) + x.shape, x.dtype))


def _exchange_comm(x):
    return _Comm(_exchange_phases, x, jax.ShapeDtypeStruct(x.shape, x.dtype))


def _comm_alone(comms, name):
    n = len(comms)

    def body(*refs):
        phases = [comm.phases(refs[k], refs[n + k], *refs[2 * n + 3 * k:2 * n + 3 * k + 3]) for k, comm in enumerate(comms)]
        for step in range(3):
            for phase in phases:
                phase[step]()

    any_spec = pl.BlockSpec(memory_space=pl.ANY)
    return pl.pallas_call(
        body,
        out_shape=[comm.dst for comm in comms],
        in_specs=[any_spec] * n,
        out_specs=[any_spec] * n,
        scratch_shapes=[pltpu.SemaphoreType.DMA((N_DEV - 1,)), pltpu.SemaphoreType.DMA((N_DEV - 1,)), pltpu.SemaphoreType.DMA] * n,
        name=name,
    )(*[comm.src for comm in comms])


def _sum_parts(p_ref):
    g = p_ref[0].astype(F32)
    for j in range(1, N_DEV):
        g = g + p_ref[j].astype(F32)
    return g


def _adamw_store(g, w_ref, m_ref, v_ref, g_ref, d_ref, nm_ref, nv_ref):
    m_new = ADAM_B1 * m_ref[...] + (1.0 - ADAM_B1) * g
    v_new = ADAM_B2 * v_ref[...] + (1.0 - ADAM_B2) * jnp.square(g)
    m_hat = m_new / (1.0 - ADAM_B1 ** ADAM_STEP)
    v_hat = v_new / (1.0 - ADAM_B2 ** ADAM_STEP)
    g_ref[...] = g
    d_ref[...] = -ADAM_LR * (m_hat / (jnp.sqrt(v_hat) + ADAM_EPS) + ADAM_WD * w_ref[...])
    nm_ref[...] = m_new
    nv_ref[...] = v_new


def _adamw_shard(parts, off, transposed, w, m, v, name, n_tiles=1):
    _, r, c = w.shape
    tr = r // n_tiles
    if transposed:
        rows = -(-c // BF16_ROWS) * BF16_ROWS
        window = (N_DEV, rows, tr)
    else:
        assert c == PACK_COLS
        window = (N_DEV, tr, PACK_COLS)

    def kern(p_hbm, w_ref, m_ref, v_ref, g_ref, d_ref, nm_ref, nv_ref, buf, sem):
        i = pl.program_id(0)
        if transposed:
            src = p_hbm.at[:, pl.ds(off, rows), pl.ds(pl.multiple_of(i * tr, LANES), tr)]
        else:
            src = p_hbm.at[:, pl.ds(pl.multiple_of(off + i * tr, BF16_ROWS), tr), :]
        cp = pltpu.make_async_copy(src, buf, sem)
        cp.start()
        cp.wait()
        g = _sum_parts(buf)
        if transposed:
            eye = (lax.broadcasted_iota(jnp.int32, (rows, c), 0) == lax.broadcasted_iota(jnp.int32, (rows, c), 1)).astype(F32)
            g = _hdot_tn(g, eye)
        _adamw_store(g, w_ref, m_ref, v_ref, g_ref, d_ref, nm_ref, nv_ref)

    spec = pl.BlockSpec((None, tr, c), lambda i: (0, i, 0))
    return pl.pallas_call(
        kern,
        out_shape=[jax.ShapeDtypeStruct((1, r, c), F32)] * 4,
        grid=(n_tiles,),
        in_specs=[pl.BlockSpec(memory_space=pl.ANY), spec, spec, spec],
        out_specs=[spec] * 4,
        scratch_shapes=[pltpu.VMEM(window, parts.dtype), pltpu.SemaphoreType.DMA],
        name=name,
        compiler_params=pltpu.CompilerParams(dimension_semantics=("arbitrary",), vmem_limit_bytes=VMEM_LIMIT),
    )(parts, w, m, v)


def _sum_adamw(parts, w, m, v, tr, name):
    _, R, C = parts.shape

    def kern(p_ref, w_ref, m_ref, v_ref, g_ref, d_ref, nm_ref, nv_ref):
        _adamw_store(_sum_parts(p_ref), w_ref, m_ref, v_ref, g_ref, d_ref, nm_ref, nv_ref)

    row_spec = pl.BlockSpec((tr, C), lambda i: (i, 0))
    return pl.pallas_call(
        kern,
        out_shape=[jax.ShapeDtypeStruct((R, C), F32)] * 4,
        grid=(R // tr,),
        in_specs=[pl.BlockSpec((N_DEV, tr, C), lambda i: (0, i, 0)), row_spec, row_spec, row_spec],
        out_specs=[row_spec] * 4,
        name=name,
        compiler_params=pltpu.CompilerParams(dimension_semantics=("arbitrary",), vmem_limit_bytes=VMEM_LIMIT),
    )(parts, w, m, v)


FF_SHARD = D_FF // N_DEV
CONV_SHARD = (SSM_CONV, CONV_DIM // N_DEV)
SHARDS = {"ffn1_w_gate": ((D_MODEL, FF_SHARD), True), "ffn1_w_up": ((D_MODEL, FF_SHARD), True),
          "ffn1_w_down": ((FF_SHARD, D_MODEL), False),
          "ffn2_w_gate": ((D_MODEL, FF_SHARD), True), "ffn2_w_up": ((D_MODEL, FF_SHARD), True),
          "ffn2_w_down": ((FF_SHARD, D_MODEL), False),
          "w_out": ((2 * D_MODEL // N_DEV, D_MODEL), False), "ple_w_gate": ((D_MODEL // N_DEV, D_MODEL), False),
          "w_in": ((D_MODEL, IN_PROJ // N_DEV), True), "ple_w_proj": ((D_PLE, D_MODEL // N_DEV), True),
          "conv_w": (CONV_SHARD, True),
          "conv_w_mid": (CONV_SHARD, True), "conv_w_low": (CONV_SHARD, True)}
BIG = tuple(name for name in SHARDS if not name.startswith("conv_w_"))
SMALL = ("ffn1_norm", "mix_norm", "gm_ln_g", "gm_ln_b", "gm_w_s", "gm_b_s", "gm_out_norm", "conv_b", "dt_bias", "a_log",
         "d_skip", "ssm_norm", "ffn2_norm", "ple_norm", "ple_b_gate", "final_norm")
SMALL_ROWS = 144


def _piece_rows(name):
    shape = SHARDS[name][0]
    return -(-(shape[0] * shape[1]) // PACK_COLS)


def _pad_cols(flat, name):
    pad = _piece_rows(name) * PACK_COLS - flat.shape[-1]
    return flat if pad == 0 else jnp.pad(flat, [(0, 0)] * (flat.ndim - 1) + [(0, pad)])


class _Pack:
    def __init__(self, names, tile_rows):
        self.names, self.tile_rows, self.offsets, off = names, tile_rows, {}, 0
        for name in names:
            self.offsets[name] = off
            off += _piece_rows(name)
        self.rows = -(-off // tile_rows) * tile_rows

    def pack_local(self, vals):
        parts = []
        for name in self.names:
            val = vals[name]
            parts.append(_pad_cols((val.T if SHARDS[name][1] else val).reshape(-1), name))
        flat = jnp.concatenate(parts)
        return jnp.pad(flat, (0, self.rows * PACK_COLS - flat.shape[0])).reshape(self.rows, PACK_COLS)

    def pack_owner_major(self, grads):
        parts, rows = [], 0
        for name in self.names:
            grad, piece_rows = grads[name].astype(BF16), _piece_rows(name)
            if grad.shape != (N_DEV * piece_rows, PACK_COLS):
                grad = _pad_cols(grad.reshape(N_DEV, -1), name)
            parts.append(grad.reshape(N_DEV, piece_rows, PACK_COLS))
            rows += piece_rows
        if rows < self.rows:
            parts.append(jnp.zeros((N_DEV, self.rows - rows, PACK_COLS), BF16))
        return parts[0] if len(parts) == 1 else jnp.concatenate(parts, axis=1)

    def gathered_piece(self, gathered, name):
        shape = SHARDS[name][0]
        rows = gathered[:, self.offsets[name]:self.offsets[name] + _piece_rows(name), :]
        return rows.reshape(N_DEV, -1)[:, :shape[0] * shape[1]]

    def pieces(self, gathered, name):
        return _Pieces(gathered, self.offsets[name], _piece_rows(name))


GATHER_FFN1 = _Pack(("ffn1_w_gate", "ffn1_w_up", "ffn1_w_down"), BF16_ROWS)
GATHER_MIX = _Pack(("w_out", "ple_w_gate", "w_in", "ple_w_proj", "conv_w", "conv_w_mid", "conv_w_low"), BF16_ROWS)
GATHER_FFN2 = _Pack(("ffn2_w_gate", "ffn2_w_up", "ffn2_w_down"), BF16_ROWS)
SCATTER_LATE = _Pack(("ffn2_w_gate", "ffn2_w_up", "ffn2_w_down", "w_out", "ple_w_gate", "ple_w_proj"), BF16_ROWS)
SCATTER_IN = _Pack(("w_in", "conv_w"), BF16_ROWS)
SCATTER_GATE = _Pack(("ffn1_w_gate",), BF16_ROWS)
SCATTER_UP = _Pack(("ffn1_w_up",), BF16_ROWS)
SCATTER_DOWN = _Pack(("ffn1_w_down",), BF16_ROWS)


def _pack_small(vals):
    flat = jnp.concatenate([vals[name].reshape(-1).astype(F32) for name in SMALL])
    return jnp.pad(flat, (0, SMALL_ROWS * PACK_COLS - flat.shape[0])).reshape(SMALL_ROWS, PACK_COLS)


def _unpack_small(packed, shapes):
    out, off = {}, 0
    flat = packed.reshape(-1)
    for name in SMALL:
        n = 1
        for s in shapes[name]:
            n *= s
        out[name] = flat[off:off + n].reshape(shapes[name])
        off += n
    return out


WEIGHTS = ("ffn1_norm", "ffn1_w_gate", "ffn1_w_up", "ffn1_w_down", "mix_norm", "w_in", "gm_ln_g", "gm_ln_b", "gm_w_s",
           "gm_b_s", "gm_out_norm", "conv_w", "conv_b", "dt_bias", "a_log", "d_skip", "ssm_norm", "w_out", "ffn2_norm",
           "ffn2_w_gate", "ffn2_w_up", "ffn2_w_down", "ple_norm", "ple_w_gate", "ple_b_gate", "ple_w_proj", "final_norm")


def _step(x, p, target, w, m, v):
    local = lambda d: {name: d[name][0] for name in BIG}

    shards = {name: val.astype(BF16) for name, val in local(w).items()}
    conv_high = lax.reduce_precision(w["conv_w"][0], 8, 7)
    conv_mid = lax.reduce_precision(w["conv_w"][0] - conv_high, 8, 7)
    shards["conv_w"] = conv_high.astype(BF16)
    shards["conv_w_mid"] = conv_mid.astype(BF16)
    shards["conv_w_low"] = (w["conv_w"][0] - conv_high - conv_mid).astype(BF16)
    g_ffn1 = _comm_alone([_gather_comm(GATHER_FFN1.pack_local(shards))], "gather_ffn1")[0]

    row = lambda name: w[name].reshape(1, -1)
    gm_w_s = w["gm_w_s"][0]
    gm_b_st = jnp.transpose(w["gm_b_s"][0])
    ffn1 = (row("ffn1_norm"),) + tuple(GATHER_FFN1.pieces(g_ffn1, name) for name in GATHER_FFN1.names)
    gm = (row("gm_ln_g"), row("gm_ln_b"), gm_w_s, gm_b_st, row("gm_out_norm"))

    h1, n1, a1, b1, s1, g_mix = _ffn_fwd(x, *ffn1, "ffn1_fwd", comm=_gather_comm(GATHER_MIX.pack_local(shards)))
    w_in_t = GATHER_MIX.gathered_piece(g_mix, "w_in").reshape(IN_PROJ, D_MODEL)
    w_in_t = jnp.concatenate([w_in_t, jnp.zeros((IN_PROJ_PAD - IN_PROJ, D_MODEL), BF16)], axis=0)
    w_proj_t = GATHER_MIX.gathered_piece(g_mix, "ple_w_proj").reshape(D_MODEL, D_PLE)
    conv_w = sum(GATHER_MIX.gathered_piece(g_mix, name).astype(F32) for name in ("conv_w", "conv_w_mid", "conv_w_low"))
    conv_w = conv_w.reshape(CONV_DIM, SSM_CONV).T
    ssd = (conv_w, row("conv_b"), row("dt_bias"), row("a_log"), row("d_skip"), row("ssm_norm"))
    w_out = GATHER_MIX.pieces(g_mix, "w_out")

    proj, n2 = _mix_in_fwd(h1, row("mix_norm"), w_in_t)
    ya = _gm_fwd(proj, *gm)
    yb, s_all, g_ffn2 = _ssd_fwd(proj, *ssd, comm=_gather_comm(GATHER_FFN2.pack_local(shards)))
    ffn2 = (row("ffn2_norm"),) + tuple(GATHER_FFN2.pieces(g_ffn2, name) for name in GATHER_FFN2.names)
    h2 = _out_proj_fwd(h1, ya, yb, w_out)
    h3, n3, a3, b3, s3 = _ffn_fwd(h2, *ffn2, "ffn2_fwd")

    g, gp = {}, {}
    dh3, loss, gp["ple_w_gate"], d_w_proj, g["ple_norm"], g["ple_b_gate"], g["final_norm"] = _tail(
        h3, p, target, row("ple_norm"), GATHER_MIX.pieces(g_mix, "ple_w_gate"), row("ple_b_gate"), w_proj_t,
        row("final_norm"))
    gp["ple_w_proj"] = d_w_proj.T

    dh2, da3, db3, g["ffn2_norm"] = _ffn_dgrad(h2, dh3, a3, b3, *ffn2, "ffn2_dgrad")
    gp["ffn2_w_gate"] = _wgrad(n3, da3, 1408, "ffn2_wgrad_gate", transpose_out=True)
    gp["ffn2_w_up"] = _wgrad(n3, db3, 1408, "ffn2_wgrad_up", transpose_out=True)
    gp["ffn2_w_down"] = _wgrad(s3, dh3, 512, "ffn2_wgrad_down", scale=0.5, bk=1024)

    dya, dyb = _out_proj_dgrad(dh2, w_out)
    gp["w_out"] = jnp.concatenate([_wgrad(ya, dh2, 1024, "w_out_wgrad_a"), _wgrad(yb, dh2, 1024, "w_out_wgrad_b")], axis=0)

    dp_zxd, d_conv_w, g["conv_b"], g["dt_bias"], g["a_log"], g["d_skip"], g["ssm_norm"], parts_late = _ssd_bwd(
        proj, dyb, s_all, *ssd, comm=_exchange_comm(SCATTER_LATE.pack_owner_major(gp)))
    gp["conv_w"] = d_conv_w.T
    dp_uv, g["gm_ln_g"], g["gm_ln_b"], g["gm_w_s"], dbst, g["gm_out_norm"] = _gm_bwd(proj, dya, *gm)
    g["gm_b_s"] = jnp.transpose(dbst)

    parts = {}
    gp["w_in"] = jnp.concatenate([_wgrad(n2, dp_uv, 1024, "w_in_wgrad_uv", transpose_out=True),
                                  _wgrad(n2, dp_zxd, 896, "w_in_wgrad_zxd", transpose_out=True)], axis=0)[:IN_PROJ]
    dh1, g["mix_norm"], parts[SCATTER_IN] = _mix_in_dgrad(h1, dh2, dp_uv, dp_zxd, row("mix_norm"), w_in_t,
                                                          comm=_exchange_comm(SCATTER_IN.pack_owner_major(gp)))

    dx, da1, db1, g["ffn1_norm"] = _ffn_dgrad(x, dh1, a1, b1, *ffn1, "ffn1_dgrad")
    gp["ffn1_w_gate"] = _wgrad(n1, da1, 1408, "ffn1_wgrad_gate", transpose_out=True)
    gp["ffn1_w_up"], parts[SCATTER_GATE] = _wgrad(n1, db1, 1408, "ffn1_wgrad_up", transpose_out=True,
                                                  comm=_exchange_comm(SCATTER_GATE.pack_owner_major(gp)))
    gp["ffn1_w_down"], parts[SCATTER_UP] = _wgrad(s1, dh1, 512, "ffn1_wgrad_down", scale=0.5, bk=1024,
                                                  comm=_exchange_comm(SCATTER_UP.pack_owner_major(gp)))
    parts[SCATTER_DOWN], small_parts = _comm_alone(
        [_exchange_comm(SCATTER_DOWN.pack_owner_major(gp)), _gather_comm(_pack_small(g))], "scatter_ffn1_down_gather_small")
    parts[SCATTER_LATE] = parts_late

    res_big = {}
    for pack, pack_parts in parts.items():
        for name in pack.names:
            shape, transposed = SHARDS[name]
            if name in ("ple_w_proj", "conv_w"):
                nat = pack.gathered_piece(pack_parts, name).reshape((N_DEV,) + shape[::-1])
                res_big[name] = _sum_adamw(jnp.transpose(nat, (0, 2, 1)), w[name][0], m[name][0], v[name][0], shape[0],
                                           "adamw_" + name)
            elif name == "w_in":
                res_big[name] = _adamw_shard(pack_parts, pack.offsets[name], True, w[name], m[name], v[name],
                                             "adamw_" + name, n_tiles=4)
            else:
                flip = (lambda a: jnp.transpose(a, (0, 2, 1))) if transposed else (lambda a: a)
                res = _adamw_shard(pack_parts, pack.offsets[name], False, flip(w[name]), flip(m[name]), flip(v[name]),
                                   "adamw_" + name, n_tiles=2)
                res_big[name] = [flip(r) for r in res]

    small_shapes = {name: w[name].shape for name in SMALL}
    res_small = _sum_adamw(small_parts, _pack_small(w), _pack_small(m), _pack_small(v), SMALL_ROWS, "adamw_small")
    res_small = [_unpack_small(r, small_shapes) for r in res_small]

    outs = []
    for k in range(4):
        for name in WEIGHTS:
            if name in res_small[k]:
                outs.append(res_small[k][name])
            else:
                outs.append(res_big[name][k].reshape(w[name].shape))
    return loss[0, 0], dx, outs


def kernel(x, p, ffn1_norm, ffn1_w_gate, ffn1_w_up, ffn1_w_down, mix_norm, w_in, gm_ln_g, gm_ln_b, gm_w_s, gm_b_s, gm_out_norm, conv_w, conv_b, dt_bias, a_log, d_skip, ssm_norm, w_out, ffn2_norm, ffn2_w_gate, ffn2_w_up, ffn2_w_down, ple_norm, ple_w_gate, ple_b_gate, ple_w_proj, final_norm, loss_target, m_ffn1_norm, m_ffn1_w_gate, m_ffn1_w_up, m_ffn1_w_down, m_mix_norm, m_w_in, m_gm_ln_g, m_gm_ln_b, m_gm_w_s, m_gm_b_s, m_gm_out_norm, m_conv_w, m_conv_b, m_dt_bias, m_a_log, m_d_skip, m_ssm_norm, m_w_out, m_ffn2_norm, m_ffn2_w_gate, m_ffn2_w_up, m_ffn2_w_down, m_ple_norm, m_ple_w_gate, m_ple_b_gate, m_ple_w_proj, m_final_norm, v_ffn1_norm, v_ffn1_w_gate, v_ffn1_w_up, v_ffn1_w_down, v_mix_norm, v_w_in, v_gm_ln_g, v_gm_ln_b, v_gm_w_s, v_gm_b_s, v_gm_out_norm, v_conv_w, v_conv_b, v_dt_bias, v_a_log, v_d_skip, v_ssm_norm, v_w_out, v_ffn2_norm, v_ffn2_w_gate, v_ffn2_w_up, v_ffn2_w_down, v_ple_norm, v_ple_w_gate, v_ple_b_gate, v_ple_w_proj, v_final_norm):
    args = locals()
    w = {name: args[name] for name in WEIGHTS}
    m = {name: args["m_" + name] for name in WEIGHTS}
    v = {name: args["v_" + name] for name in WEIGHTS}
    loss, dx, outs = _step(x[0], p[0, 0], loss_target[0], w, m, v)
    loss = lax.psum(loss, AXES)
    return (loss, dx[None], *outs)
```

```python
import functools
from typing import NamedTuple

import jax
import jax.numpy as jnp
from jax import lax
from jax.experimental import pallas as pl
from jax.experimental.pallas import tpu as pltpu

F32 = jnp.float32
BF16 = jnp.bfloat16
HIGHEST = lax.Precision.HIGHEST
MESH = pl.DeviceIdType.MESH
AXES = ("x", "y", "c")
N_DEV = 8

D_MODEL = 1024
D_FF = 2816
D_PLE = 256
GM_WIDTH = 1024
GM_HEADS = 8
GM_HEAD_DIM = 128
CHUNK = 128
SSM_WIDTH = 1024
SSM_HEADS = 16
SSM_HEAD_DIM = 64
SSM_GROUPS = 2
SSM_STATE = 128
SSM_CONV = 4
CONV_DIM = SSM_WIDTH + 2 * SSM_GROUPS * SSM_STATE
IN_PROJ = 2 * GM_WIDTH + SSM_WIDTH + CONV_DIM + SSM_HEADS
LANES = 128
BF16_ROWS = 16
IN_PROJ_PAD = IN_PROJ - SSM_HEADS + LANES
UV_W = 2 * GM_WIDTH
ZXD_W = IN_PROJ_PAD - UV_W
HALO = 8
EPS = 1e-6

ADAM_LR = 0.001
ADAM_B1 = 0.9
ADAM_B2 = 0.999
ADAM_EPS = 1e-08
ADAM_WD = 0.01
ADAM_STEP = 10

VMEM_LIMIT = 56 * 1024 * 1024
PACK_COLS = 1024


def _rms(x, g):
    return x * lax.rsqrt(jnp.mean(x * x, axis=-1, keepdims=True) + EPS) * g


def _gelu(x):
    return 0.5 * x * (1.0 + lax.erf(x * (2.0 ** -0.5)))


def _silu(x):
    return x * jax.nn.sigmoid(x)


def _dot(a, b):
    return jnp.dot(a.astype(BF16), b.astype(BF16), preferred_element_type=F32)


def _dot_nt(a, b):
    return lax.dot_general(a.astype(BF16), b.astype(BF16), (((1,), (1,)), ((), ())), preferred_element_type=F32)


def _dot_tn(a, b):
    return lax.dot_general(a.astype(BF16), b.astype(BF16), (((0,), (0,)), ((), ())), preferred_element_type=F32)


def _hdot_tn(a, b):
    return lax.dot_general(a, b, (((0,), (0,)), ((), ())), precision=HIGHEST, preferred_element_type=F32)


def _split3(x):
    hi = x.astype(BF16)
    rest = x - hi.astype(F32)
    mid = rest.astype(BF16)
    return hi, mid, (rest - mid.astype(F32)).astype(BF16)


def _exact_dot(x, mask, dims, x_first=True):
    terms = [lax.dot_general(*((t, mask) if x_first else (mask, t)), (dims, ((), ())), preferred_element_type=F32)
             for t in _split3(x)]
    return (terms[0] + terms[1]) + terms[2]


def _mask_product(fwd_dims, fwd_x_first, bwd_dims, bwd_x_first):
    @jax.custom_vjp
    def product(x, mask):
        return _exact_dot(x, mask, fwd_dims, fwd_x_first)

    def fwd(x, mask):
        return product(x, mask), mask

    def bwd(mask, g):
        return _exact_dot(g, mask, bwd_dims, bwd_x_first), jnp.zeros_like(mask)

    product.defvjp(fwd, bwd)
    return product


_widen = _mask_product(((1,), (0,)), True, ((1,), (1,)), True)
_cumsum_rows = _mask_product(((1,), (0,)), False, ((0,), (0,)), False)
_cumsum_cols = _mask_product(((0,), (0,)), True, ((1,), (1,)), False)


class _Pieces(NamedTuple):
    gathered: jax.Array
    row_off: int
    rows: int


class _Comm(NamedTuple):
    phases: object
    src: jax.Array
    dst: jax.ShapeDtypeStruct


def _tiled(body, name, n_steps, tiled_in, full_in, big_in, tiled_out, acc_out, scratch=(), reverse=False, comm=None):
    n_t, n_f, n_b, n_to, n_a = len(tiled_in), len(full_in), len(big_in), len(tiled_out), len(acc_out)
    n_c = 1 if comm else 0

    def row(i):
        return n_steps - 1 - i if reverse else i

    in_specs, args = [], []
    for arr, br, bc, cb in tiled_in:
        if callable(cb):
            in_specs.append(pl.BlockSpec((br, bc), cb))
        else:
            in_specs.append(pl.BlockSpec((br, bc), functools.partial(lambda i, cb: (row(i), cb), cb=cb)))
        args.append(arr)
    for arr in full_in:
        in_specs.append(pl.BlockSpec(arr.shape, functools.partial(lambda i, nd: (0,) * nd, nd=arr.ndim)))
        args.append(arr)
    big_shapes, n_copies = [], 0
    for big in big_in:
        in_specs.append(pl.BlockSpec(memory_space=pl.ANY))
        if isinstance(big, _Pieces):
            args.append(big.gathered)
            big_shapes.append(((N_DEV * big.rows, PACK_COLS), big.gathered.dtype))
            n_copies += N_DEV
        else:
            args.append(big)
            big_shapes.append((big.shape, big.dtype))
            n_copies += 1
    if comm:
        in_specs.append(pl.BlockSpec(memory_space=pl.ANY))
        args.append(comm.src)
    out_specs, out_shape = [], []
    for rows, cols, dt, br in tiled_out:
        out_specs.append(pl.BlockSpec((br, cols), lambda i: (row(i), 0)))
        out_shape.append(jax.ShapeDtypeStruct((rows, cols), dt))
    for shp, dt in acc_out:
        out_specs.append(pl.BlockSpec(shp, functools.partial(lambda i, nd: (0,) * nd, nd=len(shp))))
        out_shape.append(jax.ShapeDtypeStruct(shp, dt))
    if comm:
        out_specs.append(pl.BlockSpec(memory_space=pl.ANY))
        out_shape.append(comm.dst)
    scratch_shapes = [pltpu.VMEM(shp, dt) for shp, dt in big_shapes] + list(scratch)
    if n_copies:
        scratch_shapes.append(pltpu.SemaphoreType.DMA((n_copies,)))
    if comm:
        scratch_shapes += [pltpu.SemaphoreType.DMA((N_DEV - 1,)), pltpu.SemaphoreType.DMA((N_DEV - 1,)), pltpu.SemaphoreType.DMA]

    def kern(*refs):
        n_in = n_t + n_f + n_b + n_c
        ins = refs[: n_t + n_f]
        big_hbm = refs[n_t + n_f : n_t + n_f + n_b]
        outs = refs[n_in : n_in + n_to + n_a]
        rest = refs[n_in + n_to + n_a + n_c :]
        big_vmem, scr = rest[:n_b], rest[n_b:]
        if comm:
            scr, comm_sems = scr[:-3], scr[-3:]
            comm_start, comm_mid, comm_finish = comm.phases(refs[n_in - 1], refs[n_in + n_to + n_a], *comm_sems)
        if n_copies:
            scr, copy_sems = scr[:-1], scr[-1]
        step = pl.program_id(0)

        @pl.when(step == 0)
        def _():
            copies = []
            for big, src, dst in zip(big_in, big_hbm, big_vmem):
                if isinstance(big, _Pieces):
                    for j in range(N_DEV):
                        copies.append((src.at[j, pl.ds(big.row_off, big.rows), :], dst.at[pl.ds(j * big.rows, big.rows), :]))
                else:
                    copies.append((src, dst))
            copies = [pltpu.make_async_copy(a, b, copy_sems.at[k]) for k, (a, b) in enumerate(copies)]
            for cp in copies:
                cp.start()
            for cp in copies:
                cp.wait()
            for acc in outs[n_to:]:
                acc[...] = jnp.zeros(acc.shape, acc.dtype)
            if comm:
                comm_start()

        body(row(step), *ins, *big_vmem, *outs, *scr)
        if comm:
            pl.when(step == (n_steps - 1) // 2)(comm_mid)
            pl.when(step == n_steps - 1)(comm_finish)

    res = pl.pallas_call(
        kern,
        out_shape=out_shape,
        grid=(n_steps,),
        in_specs=in_specs,
        out_specs=out_specs,
        scratch_shapes=scratch_shapes,
        name=name,
        compiler_params=pltpu.CompilerParams(dimension_semantics=("arbitrary",), vmem_limit_bytes=VMEM_LIMIT),
    )(*args)
    return res


FF_CHUNKS = ((0, 1536), (1536, D_FF))
FFN_TM = 256


def _ffn_fwd(h, g, wg_t, wu_t, wd, name, comm=None, mixed=None):
    T = h.shape[0]
    n_mix = 2 if mixed else 0

    def body(i, h_ref, *refs):
        ya_ref, yb_ref = refs[:n_mix] if mixed else (None, None)
        g_ref, wg_ref, wu_ref, wd_ref = refs[n_mix:n_mix + 4]
        o_ref, n_ref, a_ref, b_ref, s_ref = refs[n_mix + 4 + n_mix // 2:n_mix + 9 + n_mix // 2]
        x = h_ref[...]
        if mixed:
            wo_ref, x_ref = refs[n_mix + 4], refs[-1]
            x = (x + jnp.dot(ya_ref[...], wo_ref[:GM_WIDTH, :], preferred_element_type=F32)
                 + jnp.dot(yb_ref[...], wo_ref[GM_WIDTH:, :], preferred_element_type=F32))
            x_ref[...] = x
        n = _rms(x, g_ref[...]).astype(BF16)
        n_ref[...] = n
        f = jnp.zeros(x.shape, F32)
        for lo, hi in FF_CHUNKS:
            a = _dot_nt(n, wg_ref[lo:hi, :])
            b = _dot_nt(n, wu_ref[lo:hi, :])
            s = (_silu(a) * b).astype(BF16)
            a_ref[:, lo:hi] = a.astype(BF16)
            b_ref[:, lo:hi] = b.astype(BF16)
            s_ref[:, lo:hi] = s
            f = f + jnp.dot(s, wd_ref[lo:hi, :], preferred_element_type=F32)
        o_ref[...] = x + 0.5 * f

    tiled_in, big_in = [(h, FFN_TM, D_MODEL, 0)], [wg_t, wu_t, wd]
    tiled_out = [(T, D_MODEL, F32, FFN_TM), (T, D_MODEL, BF16, FFN_TM), (T, D_FF, BF16, FFN_TM), (T, D_FF, BF16, FFN_TM),
                 (T, D_FF, BF16, FFN_TM)]
    if mixed:
        tiled_in += [(mixed[0], FFN_TM, GM_WIDTH, 0), (mixed[1], FFN_TM, SSM_WIDTH, 0)]
        big_in.append(mixed[2])
        tiled_out.append((T, D_MODEL, F32, FFN_TM))
    return _tiled(body, name, T // FFN_TM, tiled_in, [g], big_in, tiled_out, [], comm=comm)


def _ffn_dgrad(h, dout, a16, b16, g, wg_t, wu_t, wd, name):
    T = h.shape[0]

    def body(i, h_ref, do_ref, a_ref, b_ref, g_ref, wg_ref, wu_ref, wd_ref, dh_ref, da_ref, db_ref, dg_ref):
        dout = do_ref[...]
        _, rms_vjp = jax.vjp(_rms, h_ref[...], g_ref[...])
        dfo = (0.5 * dout).astype(BF16)
        dn = jnp.zeros(dout.shape, F32)
        for lo, hi in FF_CHUNKS:
            a = a_ref[:, lo:hi].astype(F32)
            b = b_ref[:, lo:hi].astype(F32)
            sg = jax.nn.sigmoid(a)
            ds = _dot_nt(dfo, wd_ref[lo:hi, :])
            db = (ds * (a * sg)).astype(BF16)
            da = (ds * b * (sg * (1.0 + a * (1.0 - sg)))).astype(BF16)
            dn = dn + _dot(da, wg_ref[lo:hi, :]) + _dot(db, wu_ref[lo:hi, :])
            da_ref[:, lo:hi] = da
            db_ref[:, lo:hi] = db
        dx, dg = rms_vjp(dn)
        dh_ref[...] = dout + dx
        dg_ref[...] += dg

    return _tiled(body, name, T // FFN_TM,
                  [(h, FFN_TM, D_MODEL, 0), (dout, FFN_TM, D_MODEL, 0), (a16, FFN_TM, D_FF, 0), (b16, FFN_TM, D_FF, 0)],
                  [g], [wg_t, wu_t, wd],
                  [(T, D_MODEL, F32, FFN_TM), (T, D_FF, BF16, FFN_TM), (T, D_FF, BF16, FFN_TM)], [((1, D_MODEL), F32)])


def _wgrad(a, b, bn, name, scale=None, transpose_out=False, bk=2048, comm=None):
    T, M = a.shape
    N = b.shape[1]
    bk = min(bk, T)
    assert M % LANES == 0 and N % bn == 0 and T % bk == 0
    n_j, n_k = N // bn, T // bk
    n_c = 1 if comm else 0

    def kern(*refs):
        a_ref, b_ref, o_ref, acc_ref = refs[0], refs[1], refs[2 + n_c], refs[3 + 2 * n_c]
        j, k = pl.program_id(0), pl.program_id(1)
        if comm:
            comm_start, comm_mid, comm_finish = comm.phases(refs[2], refs[4], *refs[6:])
            pl.when((j == 0) & (k == 0))(comm_start)

        @pl.when(k == 0)
        def _():
            acc_ref[...] = jnp.zeros(acc_ref.shape, F32)

        bv = b_ref[...]
        if scale is not None:
            bv = bv * scale
        acc_ref[...] += _dot_tn(a_ref[...], bv)

        @pl.when(k == n_k - 1)
        def _():
            acc = acc_ref[...]
            o_ref[...] = (acc.T if transpose_out else acc).astype(BF16)

        if comm:
            pl.when((j == (n_j - 1) // 2) & (k == n_k - 1))(comm_mid)
            pl.when((j == n_j - 1) & (k == n_k - 1))(comm_finish)

    if transpose_out:
        out_shape, out_spec = (N, M), pl.BlockSpec((bn, M), lambda j, k: (j, 0))
    else:
        out_shape, out_spec = (M, N), pl.BlockSpec((M, bn), lambda j, k: (0, j))
    any_spec = pl.BlockSpec(memory_space=pl.ANY)
    comm_sems = [pltpu.SemaphoreType.DMA((N_DEV - 1,)), pltpu.SemaphoreType.DMA((N_DEV - 1,)), pltpu.SemaphoreType.DMA]
    res = pl.pallas_call(
        kern,
        out_shape=[jax.ShapeDtypeStruct(out_shape, BF16)] + ([comm.dst] if comm else []),
        grid=(n_j, n_k),
        in_specs=[pl.BlockSpec((bk, M), lambda j, k: (k, 0)), pl.BlockSpec((bk, bn), lambda j, k: (k, j))] + [any_spec] * n_c,
        out_specs=[out_spec] + [any_spec] * n_c,
        scratch_shapes=[pltpu.VMEM((M, bn), F32)] + (comm_sems if comm else []),
        name=name,
        compiler_params=pltpu.CompilerParams(dimension_semantics=("arbitrary", "arbitrary"), vmem_limit_bytes=VMEM_LIMIT),
    )(a, b, *([comm.src] if comm else []))
    return res if comm else res[0]


PROJ_TM = 256


def _mix_in_fwd(h, g, w_in_t):
    T = h.shape[0]

    def body(i, h_ref, g_ref, w_ref, p_ref, n_ref):
        n = _rms(h_ref[...], g_ref[...]).astype(BF16)
        n_ref[...] = n
        p_ref[...] = _dot_nt(n, w_ref[...])

    return _tiled(body, "mix_in_fwd", T // PROJ_TM, [(h, PROJ_TM, D_MODEL, 0)], [g], [w_in_t],
                  [(T, IN_PROJ_PAD, F32, PROJ_TM), (T, D_MODEL, BF16, PROJ_TM)], [])


def _mix_in_dgrad(h, dh_in, dp_uv, dp_zxd, g, w_in_t, comm=None):
    T = h.shape[0]

    def body(i, h_ref, dh_ref, duv_ref, dzxd_ref, g_ref, w_ref, o_ref, dg_ref):
        dn = _dot(duv_ref[...], w_ref[:UV_W, :]) + _dot(dzxd_ref[...], w_ref[UV_W:, :])
        _, rms_vjp = jax.vjp(_rms, h_ref[...], g_ref[...])
        dx, dg = rms_vjp(dn)
        o_ref[...] = dh_ref[...] + dx
        dg_ref[...] += dg

    return _tiled(body, "mix_in_dgrad", T // PROJ_TM,
                  [(h, PROJ_TM, D_MODEL, 0), (dh_in, PROJ_TM, D_MODEL, 0), (dp_uv, PROJ_TM, UV_W, 0),
                   (dp_zxd, PROJ_TM, ZXD_W, 0)], [g], [w_in_t],
                  [(T, D_MODEL, F32, PROJ_TM)], [((1, D_MODEL), F32)], comm=comm)


def _out_proj_dgrad(dh, w_out):
    T = dh.shape[0]

    def body(i, dh_ref, w_ref, dya_ref, dyb_ref):
        d = dh_ref[...].astype(BF16)
        dya_ref[...] = _dot_nt(d, w_ref[:GM_WIDTH, :])
        dyb_ref[...] = _dot_nt(d, w_ref[GM_WIDTH:, :])

    return _tiled(body, "out_proj_dgrad", T // PROJ_TM, [(dh, PROJ_TM, D_MODEL, 0)], [], [w_out],
                  [(T, GM_WIDTH, F32, PROJ_TM), (T, SSM_WIDTH, F32, PROJ_TM)], [])


def _gm_chunk(u, v, ln_g, ln_b, b_st, out_g, *w_heads):
    ug = _gelu(u)
    vg = _gelu(v)
    mu = jnp.mean(vg, axis=-1, keepdims=True)
    xc = vg - mu
    vn = xc * lax.rsqrt(jnp.mean(xc * xc, axis=-1, keepdims=True) + EPS) * ln_g + ln_b
    t_idx = lax.broadcasted_iota(jnp.int32, (CHUNK, CHUNK), 0)
    s_idx = lax.broadcasted_iota(jnp.int32, (CHUNK, CHUNK), 1)
    causal = t_idx >= s_idx
    mixed = []
    for hd in range(GM_HEADS):
        wm = jnp.where(causal, w_heads[hd], 0.0)
        cols = slice(hd * GM_HEAD_DIM, (hd + 1) * GM_HEAD_DIM)
        mixed.append(_dot(wm, vn[:, cols]) + b_st[:, hd:hd + 1])
    ya0 = ug * jnp.concatenate(mixed, axis=1)
    return _rms(ya0, out_g)


GM_FWD_CHUNKS = 2


def _gm_fwd(proj, ln_g, ln_b, w_s, b_st, out_g):
    T = proj.shape[0]

    rows = GM_FWD_CHUNKS * CHUNK

    def body(i, u_ref, v_ref, lg_ref, lb_ref, w_ref, bs_ref, og_ref, ya_ref):
        w_heads = [w_ref[hd] for hd in range(GM_HEADS)]
        for c in range(GM_FWD_CHUNKS):
            tok = pl.ds(c * CHUNK, CHUNK)
            ya = _gm_chunk(u_ref[tok, :], v_ref[tok, :], lg_ref[...], lb_ref[...], bs_ref[...], og_ref[...], *w_heads)
            ya_ref[tok, :] = ya.astype(BF16)

    return _tiled(body, "gmlp_fwd", T // rows, [(proj, rows, GM_WIDTH, 0), (proj, rows, GM_WIDTH, 1)],
                  [ln_g, ln_b, w_s, b_st, out_g], [], [(T, GM_WIDTH, BF16, rows)], [])[0]


def _gm_bwd(proj, dya, ln_g, ln_b, w_s, b_st, out_g):
    T = proj.shape[0]

    def body(i, u_ref, v_ref, dy_ref, lg_ref, lb_ref, w_ref, bs_ref, og_ref, duv_ref, dlg_ref, dlb_ref, dw_ref, dbs_ref,
             dog_ref):
        w_heads = [w_ref[hd] for hd in range(GM_HEADS)]
        _, vjp = jax.vjp(_gm_chunk, u_ref[...], v_ref[...], lg_ref[...], lb_ref[...], bs_ref[...], og_ref[...], *w_heads)
        grads = vjp(dy_ref[...])
        duv_ref[:, :GM_WIDTH] = grads[0].astype(BF16)
        duv_ref[:, GM_WIDTH:] = grads[1].astype(BF16)
        dlg_ref[...] += grads[2]
        dlb_ref[...] += grads[3]
        dbs_ref[...] += grads[4]
        dog_ref[...] += grads[5]
        for hd in range(GM_HEADS):
            dw_ref[hd] += grads[6 + hd]

    return _tiled(body, "gmlp_bwd", T // CHUNK,
                  [(proj, CHUNK, GM_WIDTH, 0), (proj, CHUNK, GM_WIDTH, 1), (dya, CHUNK, GM_WIDTH, 0)],
                  [ln_g, ln_b, w_s, b_st, out_g], [], [(T, UV_W, BF16, CHUNK)],
                  [((1, GM_WIDTH), F32), ((1, GM_WIDTH), F32), ((GM_HEADS, CHUNK, CHUNK), F32),
                   ((CHUNK, GM_HEADS), F32), ((1, GM_WIDTH), F32)])


def _ssd_chunk(xc, z, dtr, s_in, dt_bias, a_log, d_skip, norm_g):
    half = SSM_WIDTH // SSM_GROUPS
    l_idx = lax.broadcasted_iota(jnp.int32, (CHUNK, CHUNK), 0)
    s_idx = lax.broadcasted_iota(jnp.int32, (CHUNK, CHUNK), 1)
    causal = l_idx >= s_idx
    head_of_col = lax.broadcasted_iota(jnp.int32, (SSM_HEADS, SSM_WIDTH), 1) // SSM_HEAD_DIM
    expand = (head_of_col == lax.broadcasted_iota(jnp.int32, (SSM_HEADS, SSM_WIDTH), 0)).astype(BF16)

    xcs = _silu(xc)
    xs = xcs[:, :SSM_WIDTH]
    dt = jax.nn.softplus(dtr + dt_bias)
    adt = dt * (-jnp.exp(a_log))
    acs = _cumsum_rows(adt, causal.astype(BF16))
    acs_t = _cumsum_cols(adt, (l_idx <= s_idx).astype(BF16))
    tot = acs[CHUNK - 1:CHUNK, :]
    dt_w = _widen(dt, expand)
    out_decay_w = _widen(jnp.exp(acs), expand)
    state_decay_w = _widen(jnp.exp(tot - acs), expand)
    chunk_decay_w = _widen(jnp.exp(tot), expand)
    d_skip_w = _widen(d_skip, expand)
    xdt = xs * dt_w
    xdt_decayed = xdt * state_decay_w

    y_diag, y_off, states = [], [], []
    for grp in range(SSM_GROUPS):
        b0 = SSM_WIDTH + grp * SSM_STATE
        c0 = SSM_WIDTH + SSM_GROUPS * SSM_STATE + grp * SSM_STATE
        bm = xcs[:, b0:b0 + SSM_STATE].astype(BF16)
        cm = xcs[:, c0:c0 + SSM_STATE].astype(BF16)
        cb = _dot_nt(cm, bm)
        for k in range(grp * SSM_HEADS // SSM_GROUPS, (grp + 1) * SSM_HEADS // SSM_GROUPS):
            decay = jnp.exp(jnp.where(causal, acs[:, k:k + 1] - acs_t[k:k + 1, :], -jnp.inf))
            y_diag.append(_dot(cb * decay, xdt[:, k * SSM_HEAD_DIM:(k + 1) * SSM_HEAD_DIM]))
        cols = slice(grp * half, (grp + 1) * half)
        states.append(_dot_tn(bm, xdt_decayed[:, cols]))
        y_off.append(_dot(cm, s_in[:, cols]))
    y = jnp.concatenate(y_diag, axis=1) + jnp.concatenate(y_off, axis=1) * out_decay_w + xs * d_skip_w
    s_out = s_in * chunk_decay_w + jnp.concatenate(states, axis=1)
    y = y * _silu(z)
    normed = []
    for grp in range(SSM_GROUPS):
        yg = y[:, grp * half:(grp + 1) * half]
        normed.append(yg * lax.rsqrt(jnp.mean(yg * yg, axis=-1, keepdims=True) + EPS))
    return jnp.concatenate(normed, axis=1) * norm_g, s_out


def _conv_taps(ext_ref, w, b):
    taps = [ext_ref[pl.ds(HALO - (SSM_CONV - 1) + k, CHUNK), :] for k in range(SSM_CONV)]
    y = b
    for k in range(SSM_CONV):
        y = y + w[k:k + 1, :] * taps[k]
    return y, taps


def _ssd_fwd(proj, conv_w, conv_b, dt_bias, a_log, d_skip, norm_g, comm=None):
    T = proj.shape[0]
    n_chunks = T // CHUNK

    def body(i, z_ref, x_ref, dt_ref, cw_ref, cb_ref, dtb_ref, al_ref, dsk_ref, ng_ref, yb_ref, sin_ref, ext_ref, st_ref):
        @pl.when(i == 0)
        def _():
            ext_ref[0:HALO, :] = jnp.zeros((HALO, CONV_DIM), F32)
            st_ref[...] = jnp.zeros(st_ref.shape, F32)

        ext_ref[HALO:, :] = x_ref[...]
        xc, _ = _conv_taps(ext_ref, cw_ref[...], cb_ref[...])
        s_in = st_ref[...]
        yb, s_out = _ssd_chunk(xc, z_ref[...], dt_ref[:, 0:SSM_HEADS], s_in, dtb_ref[...], al_ref[...], dsk_ref[...],
                               ng_ref[...])
        yb_ref[...] = yb.astype(BF16)
        sin_ref[...] = s_in
        st_ref[...] = s_out
        ext_ref[0:HALO, :] = ext_ref[CHUNK:CHUNK + HALO, :]

    z_blk = 2 * GM_WIDTH // SSM_WIDTH
    x_blk = (2 * GM_WIDTH + SSM_WIDTH) // CONV_DIM
    dt_blk = (2 * GM_WIDTH + SSM_WIDTH + CONV_DIM) // LANES
    return _tiled(body, "ssd_fwd", n_chunks,
                  [(proj, CHUNK, SSM_WIDTH, z_blk), (proj, CHUNK, CONV_DIM, x_blk), (proj, CHUNK, LANES, dt_blk)],
                  [conv_w, conv_b, dt_bias, a_log, d_skip, norm_g], [],
                  [(T, SSM_WIDTH, BF16, CHUNK), (n_chunks * SSM_STATE, SSM_WIDTH, F32, SSM_STATE)], [],
                  scratch=[pltpu.VMEM((HALO + CHUNK, CONV_DIM), F32), pltpu.VMEM((SSM_STATE, SSM_WIDTH), F32)], comm=comm)


def _ssd_bwd(proj, dyb, s_all, conv_w, conv_b, dt_bias, a_log, d_skip, norm_g, comm=None):
    T = proj.shape[0]
    n_chunks = T // CHUNK
    z_blk = 2 * GM_WIDTH // SSM_WIDTH
    x_blk = (2 * GM_WIDTH + SSM_WIDTH) // CONV_DIM
    dt_blk = (2 * GM_WIDTH + SSM_WIDTH + CONV_DIM) // LANES
    rows_per_halo = CHUNK // HALO

    def body(i, z_ref, x_ref, halo_ref, dt_ref, dy_ref, sin_ref, cw_ref, cb_ref, dtb_ref, al_ref, dsk_ref, ng_ref,
             dzxd_ref, dcw_ref, dcb_ref, ddtb_ref, dal_ref, ddsk_ref, dng_ref, ext_ref, dext_ref, dst_ref):
        @pl.when(i == n_chunks - 1)
        def _():
            dext_ref[CHUNK:, :] = jnp.zeros((HALO, CONV_DIM), F32)
            dst_ref[...] = jnp.zeros(dst_ref.shape, F32)

        halo = halo_ref[...]
        ext_ref[0:HALO, :] = jnp.where(i == 0, jnp.zeros_like(halo), halo)
        ext_ref[HALO:, :] = x_ref[...]
        cw = cw_ref[...]
        xc, taps = _conv_taps(ext_ref, cw, cb_ref[...])
        _, vjp = jax.vjp(_ssd_chunk, xc, z_ref[...], dt_ref[:, 0:SSM_HEADS], sin_ref[...], dtb_ref[...], al_ref[...],
                         dsk_ref[...], ng_ref[...])
        dxc, dz, ddtr, ds_in, ddtb, dal, ddsk, dng = vjp((dy_ref[...], dst_ref[...]))
        dst_ref[...] = ds_in
        ddtb_ref[...] += ddtb
        dal_ref[...] += dal
        ddsk_ref[...] += ddsk
        dng_ref[...] += dng
        dext_ref[0:CHUNK, :] = dxc
        dx = jnp.zeros((CHUNK, CONV_DIM), F32)
        for k in range(SSM_CONV):
            dx = dx + cw[k:k + 1, :] * dext_ref[pl.ds(SSM_CONV - 1 - k, CHUNK), :]
            dcw_ref[k:k + 1, :] += jnp.sum(dxc * taps[k], axis=0, keepdims=True)
        dcb_ref[...] += jnp.sum(dxc, axis=0, keepdims=True)
        dext_ref[CHUNK:, :] = dext_ref[0:HALO, :]
        dzxd_ref[:, 0:SSM_WIDTH] = dz.astype(BF16)
        dzxd_ref[:, SSM_WIDTH:SSM_WIDTH + CONV_DIM] = dx.astype(BF16)
        dzxd_ref[:, SSM_WIDTH + CONV_DIM:] = jnp.concatenate(
            [ddtr, jnp.zeros((CHUNK, LANES - SSM_HEADS), F32)], axis=1).astype(BF16)

    def halo_index(step):
        c = n_chunks - 1 - step
        return (jnp.maximum(c * rows_per_halo - 1, 0), x_blk)

    return _tiled(body, "ssd_bwd", n_chunks,
                  [(proj, CHUNK, SSM_WIDTH, z_blk), (proj, CHUNK, CONV_DIM, x_blk), (proj, HALO, CONV_DIM, halo_index),
                   (proj, CHUNK, LANES, dt_blk), (dyb, CHUNK, SSM_WIDTH, 0), (s_all, SSM_STATE, SSM_WIDTH, 0)],
                  [conv_w, conv_b, dt_bias, a_log, d_skip, norm_g], [],
                  [(T, ZXD_W, BF16, CHUNK)],
                  [((SSM_CONV, CONV_DIM), F32), ((1, CONV_DIM), F32), ((1, SSM_HEADS), F32), ((1, SSM_HEADS), F32),
                   ((1, SSM_HEADS), F32), ((1, SSM_WIDTH), F32)],
                  scratch=[pltpu.VMEM((HALO + CHUNK, CONV_DIM), F32), pltpu.VMEM((CHUNK + HALO, CONV_DIM), F32),
                           pltpu.VMEM((SSM_STATE, SSM_WIDTH), F32)],
                  reverse=True, comm=comm)


TAIL_TM = 512


def _tail(h, p, target, ple_norm, w_gate, b_gate, w_proj_t, final_norm):
    T = h.shape[0]

    def head(x, pre, pp, b_g, f_norm, tgt):
        gate = jax.nn.sigmoid(pre + b_g)
        out = _rms(x + gate * pp, f_norm)
        err = out - tgt
        return 0.5 * jnp.sum(jnp.mean(err * err, axis=-1, keepdims=True), axis=0, keepdims=True)

    def body(i, h_ref, p_ref, t_ref, pn_ref, bg_ref, fn_ref, wg_ref, wp_ref, dh_ref, loss_ref, dwg_ref, dwp_ref, dpn_ref,
             dbg_ref, dfn_ref):
        x = h_ref[...]
        n4f, n_vjp = jax.vjp(_rms, x, pn_ref[...])
        n4 = n4f.astype(BF16)
        pre = jnp.dot(n4, wg_ref[...], preferred_element_type=F32)
        p16 = p_ref[...].astype(BF16)
        pp = _dot_nt(p16, wp_ref[...])
        loss, h_vjp = jax.vjp(functools.partial(head, tgt=t_ref[...]), x, pre, pp, bg_ref[...], fn_ref[...])
        dx, dpre, dpp, dbg, dfn = h_vjp(jnp.ones((1, 1), F32))
        dpre16 = dpre.astype(BF16)
        dn4 = _dot_nt(dpre16, wg_ref[...])
        dx2, dpn = n_vjp(dn4)
        dh_ref[...] = dx + dx2
        loss_ref[...] += loss
        dwg_ref[...] += _dot_tn(n4, dpre16)
        dwp_ref[...] += _dot_tn(p16, dpp)
        dpn_ref[...] += dpn
        dbg_ref[...] += dbg
        dfn_ref[...] += dfn

    return _tiled(body, "tail", T // TAIL_TM,
                  [(h, TAIL_TM, D_MODEL, 0), (p, TAIL_TM, D_PLE, 0), (target, TAIL_TM, D_MODEL, 0)],
                  [ple_norm, b_gate, final_norm], [w_gate, w_proj_t],
                  [(T, D_MODEL, F32, TAIL_TM)],
                  [((1, 1), F32), ((D_MODEL, D_MODEL), F32), ((D_PLE, D_MODEL), F32), ((1, D_MODEL), F32),
                   ((1, D_MODEL), F32), ((1, D_MODEL), F32)])


def _gather_phases(x_ref, out_ref, send_sems, recv_sems, local_sem):
    mx, my, mc = lax.axis_index("x"), lax.axis_index("y"), lax.axis_index("c")
    me, sibling = (mx, my, mc), (mx, my, 1 - mc)
    chips = [(1 - mx, my), (mx, 1 - my), (1 - mx, 1 - my)]

    def rows(px, py, pc):
        return out_ref.at[4 * px + 2 * py + pc]

    def copy(k, block, to, src=None):
        return pltpu.make_async_remote_copy(
            src_ref=rows(*block) if src is None else src, dst_ref=rows(*block),
            send_sem=send_sems.at[k], recv_sem=recv_sems.at[k], device_id=to, device_id_type=MESH)

    mine = pltpu.make_async_copy(x_ref, rows(*me), local_sem)
    first = [copy(0, me, sibling, src=x_ref)] + [copy(1 + j, me, (*chip, mc), src=x_ref) for j, chip in enumerate(chips)]
    passed = [copy(4 + j, (*chip, mc), sibling) for j, chip in enumerate(chips)]

    def start():
        mine.start()
        for cp in first:
            cp.start()

    def mid():
        for j, chip in enumerate(chips):
            copy(1 + j, (*chip, mc), me).wait_recv()
            passed[j].start()

    def finish():
        copy(0, sibling, me).wait_recv()
        for j, chip in enumerate(chips):
            copy(4 + j, (*chip, 1 - mc), me).wait_recv()
        for cp in first + passed:
            cp.wait_send()
        mine.wait()

    return start, mid, finish


def _exchange_phases(x_ref, out_ref, send_sems, recv_sems, local_sem):
    mx, my, mc = lax.axis_index("x"), lax.axis_index("y"), lax.axis_index("c")
    me = 4 * mx + 2 * my + mc
    mine = pltpu.make_async_copy(x_ref.at[me], out_ref.at[me], local_sem)
    copies = []
    for k in range(1, N_DEV):
        px = 1 - mx if k & 4 else mx
        py = 1 - my if k & 2 else my
        pc = 1 - mc if k & 1 else mc
        copies.append(pltpu.make_async_remote_copy(
            src_ref=x_ref.at[4 * px + 2 * py + pc], dst_ref=out_ref.at[me], send_sem=send_sems.at[k - 1],
            recv_sem=recv_sems.at[k - 1], device_id=(px, py, pc), device_id_type=MESH))

    def start():
        mine.start()
        for cp in copies:
            cp.start()

    def finish():
        for cp in copies:
            cp.wait_recv()
        for cp in copies:
            cp.wait_send()
        mine.wait()

    return start, lambda: None, finish


def _gather_comm(x):
    return _Comm(_gather_phases, x, jax.ShapeDtypeStruct((N_DEV,) + x.shape, x.dtype))


def _exchange_comm(x):
    return _Comm(_exchange_phases, x, jax.ShapeDtypeStruct(x.shape, x.dtype))


def _comm_alone(comms, name):
    n = len(comms)

    def body(*refs):
        phases = [comm.phases(refs[k], refs[n + k], *refs[2 * n + 3 * k:2 * n + 3 * k + 3]) for k, comm in enumerate(comms)]
        for step in range(3):
            for phase in phases:
                phase[step]()

    any_spec = pl.BlockSpec(memory_space=pl.ANY)
    return pl.pallas_call(
        body,
        out_shape=[comm.dst for comm in comms],
        in_specs=[any_spec] * n,
        out_specs=[any_spec] * n,
        scratch_shapes=[pltpu.SemaphoreType.DMA((N_DEV - 1,)), pltpu.SemaphoreType.DMA((N_DEV - 1,)), pltpu.SemaphoreType.DMA] * n,
        name=name,
    )(*[comm.src for comm in comms])


def _sum_parts(p_ref):
    g = p_ref[0].astype(F32)
    for j in range(1, N_DEV):
        g = g + p_ref[j].astype(F32)
    return g


def _adamw_store(g, w_ref, m_ref, v_ref, g_ref, d_ref, nm_ref, nv_ref):
    m_new = ADAM_B1 * m_ref[...] + (1.0 - ADAM_B1) * g
    v_new = ADAM_B2 * v_ref[...] + (1.0 - ADAM_B2) * jnp.square(g)
    m_hat = m_new / (1.0 - ADAM_B1 ** ADAM_STEP)
    v_hat = v_new / (1.0 - ADAM_B2 ** ADAM_STEP)
    g_ref[...] = g
    d_ref[...] = -ADAM_LR * (m_hat / (jnp.sqrt(v_hat) + ADAM_EPS) + ADAM_WD * w_ref[...])
    nm_ref[...] = m_new
    nv_ref[...] = v_new


def _adamw_shard(parts, off, transposed, w, m, v, name, n_tiles=1):
    _, r, c = w.shape
    tr = r // n_tiles
    if transposed:
        rows = -(-c // BF16_ROWS) * BF16_ROWS
        window = (N_DEV, rows, tr)
    else:
        assert c == PACK_COLS
        window = (N_DEV, tr, PACK_COLS)

    def kern(p_hbm, w_ref, m_ref, v_ref, g_ref, d_ref, nm_ref, nv_ref, buf, sem):
        i = pl.program_id(0)
        if transposed:
            src = p_hbm.at[:, pl.ds(off, rows), pl.ds(pl.multiple_of(i * tr, LANES), tr)]
        else:
            src = p_hbm.at[:, pl.ds(pl.multiple_of(off + i * tr, BF16_ROWS), tr), :]
        cp = pltpu.make_async_copy(src, buf, sem)
        cp.start()
        cp.wait()
        g = _sum_parts(buf)
        if transposed:
            eye = (lax.broadcasted_iota(jnp.int32, (rows, c), 0) == lax.broadcasted_iota(jnp.int32, (rows, c), 1)).astype(F32)
            g = _hdot_tn(g, eye)
        _adamw_store(g, w_ref, m_ref, v_ref, g_ref, d_ref, nm_ref, nv_ref)

    spec = pl.BlockSpec((None, tr, c), lambda i: (0, i, 0))
    return pl.pallas_call(
        kern,
        out_shape=[jax.ShapeDtypeStruct((1, r, c), F32)] * 4,
        grid=(n_tiles,),
        in_specs=[pl.BlockSpec(memory_space=pl.ANY), spec, spec, spec],
        out_specs=[spec] * 4,
        scratch_shapes=[pltpu.VMEM(window, parts.dtype), pltpu.SemaphoreType.DMA],
        name=name,
        compiler_params=pltpu.CompilerParams(dimension_semantics=("arbitrary",), vmem_limit_bytes=VMEM_LIMIT),
    )(parts, w, m, v)


def _sum_adamw(parts, w, m, v, tr, name):
    _, R, C = parts.shape

    def kern(p_ref, w_ref, m_ref, v_ref, g_ref, d_ref, nm_ref, nv_ref):
        _adamw_store(_sum_parts(p_ref), w_ref, m_ref, v_ref, g_ref, d_ref, nm_ref, nv_ref)

    row_spec = pl.BlockSpec((tr, C), lambda i: (i, 0))
    return pl.pallas_call(
        kern,
        out_shape=[jax.ShapeDtypeStruct((R, C), F32)] * 4,
        grid=(R // tr,),
        in_specs=[pl.BlockSpec((N_DEV, tr, C), lambda i: (0, i, 0)), row_spec, row_spec, row_spec],
        out_specs=[row_spec] * 4,
        name=name,
        compiler_params=pltpu.CompilerParams(dimension_semantics=("arbitrary",), vmem_limit_bytes=VMEM_LIMIT),
    )(parts, w, m, v)


FF_SHARD = D_FF // N_DEV
CONV_SHARD = (SSM_CONV, CONV_DIM // N_DEV)
SHARDS = {"ffn1_w_gate": ((D_MODEL, FF_SHARD), True), "ffn1_w_up": ((D_MODEL, FF_SHARD), True),
          "ffn1_w_down": ((FF_SHARD, D_MODEL), False),
          "ffn2_w_gate": ((D_MODEL, FF_SHARD), True), "ffn2_w_up": ((D_MODEL, FF_SHARD), True),
          "ffn2_w_down": ((FF_SHARD, D_MODEL), False),
          "w_out": ((2 * D_MODEL // N_DEV, D_MODEL), False), "ple_w_gate": ((D_MODEL // N_DEV, D_MODEL), False),
          "w_in": ((D_MODEL, IN_PROJ // N_DEV), True), "ple_w_proj": ((D_PLE, D_MODEL // N_DEV), True),
          "conv_w": (CONV_SHARD, True),
          "conv_w_mid": (CONV_SHARD, True), "conv_w_low": (CONV_SHARD, True)}
BIG = tuple(name for name in SHARDS if not name.startswith("conv_w_"))
SMALL = ("ffn1_norm", "mix_norm", "gm_ln_g", "gm_ln_b", "gm_w_s", "gm_b_s", "gm_out_norm", "conv_b", "dt_bias", "a_log",
         "d_skip", "ssm_norm", "ffn2_norm", "ple_norm", "ple_b_gate", "final_norm")
SMALL_ROWS = 144


def _piece_rows(name):
    shape = SHARDS[name][0]
    return -(-(shape[0] * shape[1]) // PACK_COLS)


def _pad_cols(flat, name):
    pad = _piece_rows(name) * PACK_COLS - flat.shape[-1]
    return flat if pad == 0 else jnp.pad(flat, [(0, 0)] * (flat.ndim - 1) + [(0, pad)])


class _Pack:
    def __init__(self, names, tile_rows):
        self.names, self.tile_rows, self.offsets, off = names, tile_rows, {}, 0
        for name in names:
            self.offsets[name] = off
            off += _piece_rows(name)
        self.rows = -(-off // tile_rows) * tile_rows

    def pack_local(self, vals):
        parts = []
        for name in self.names:
            val = vals[name]
            parts.append(_pad_cols((val.T if SHARDS[name][1] else val).reshape(-1), name))
        flat = jnp.concatenate(parts)
        return jnp.pad(flat, (0, self.rows * PACK_COLS - flat.shape[0])).reshape(self.rows, PACK_COLS)

    def pack_owner_major(self, grads):
        parts, rows = [], 0
        for name in self.names:
            grad, piece_rows = grads[name].astype(BF16), _piece_rows(name)
            if grad.shape != (N_DEV * piece_rows, PACK_COLS):
                grad = _pad_cols(grad.reshape(N_DEV, -1), name)
            parts.append(grad.reshape(N_DEV, piece_rows, PACK_COLS))
            rows += piece_rows
        if rows < self.rows:
            parts.append(jnp.zeros((N_DEV, self.rows - rows, PACK_COLS), BF16))
        return parts[0] if len(parts) == 1 else jnp.concatenate(parts, axis=1)

    def gathered_piece(self, gathered, name):
        shape = SHARDS[name][0]
        rows = gathered[:, self.offsets[name]:self.offsets[name] + _piece_rows(name), :]
        return rows.reshape(N_DEV, -1)[:, :shape[0] * shape[1]]

    def pieces(self, gathered, name):
        return _Pieces(gathered, self.offsets[name], _piece_rows(name))


GATHER_FFN1 = _Pack(("ffn1_w_gate", "ffn1_w_up", "ffn1_w_down"), BF16_ROWS)
GATHER_MIX = _Pack(("w_out", "ple_w_gate", "w_in", "ple_w_proj", "conv_w", "conv_w_mid", "conv_w_low"), BF16_ROWS)
GATHER_FFN2 = _Pack(("ffn2_w_gate", "ffn2_w_up", "ffn2_w_down"), BF16_ROWS)
SCATTER_LATE = _Pack(("ffn2_w_gate", "ffn2_w_up", "ffn2_w_down", "w_out", "ple_w_gate", "ple_w_proj"), BF16_ROWS)
SCATTER_IN = _Pack(("w_in", "conv_w"), BF16_ROWS)
SCATTER_GATE = _Pack(("ffn1_w_gate",), BF16_ROWS)
SCATTER_UP = _Pack(("ffn1_w_up",), BF16_ROWS)
SCATTER_DOWN = _Pack(("ffn1_w_down",), BF16_ROWS)


def _pack_small(vals):
    flat = jnp.concatenate([vals[name].reshape(-1).astype(F32) for name in SMALL])
    return jnp.pad(flat, (0, SMALL_ROWS * PACK_COLS - flat.shape[0])).reshape(SMALL_ROWS, PACK_COLS)


def _unpack_small(packed, shapes):
    out, off = {}, 0
    flat = packed.reshape(-1)
    for name in SMALL:
        n = 1
        for s in shapes[name]:
            n *= s
        out[name] = flat[off:off + n].reshape(shapes[name])
        off += n
    return out


WEIGHTS = ("ffn1_norm", "ffn1_w_gate", "ffn1_w_up", "ffn1_w_down", "mix_norm", "w_in", "gm_ln_g", "gm_ln_b", "gm_w_s",
           "gm_b_s", "gm_out_norm", "conv_w", "conv_b", "dt_bias", "a_log", "d_skip", "ssm_norm", "w_out", "ffn2_norm",
           "ffn2_w_gate", "ffn2_w_up", "ffn2_w_down", "ple_norm", "ple_w_gate", "ple_b_gate", "ple_w_proj", "final_norm")


def _step(x, p, target, w, m, v):
    local = lambda d: {name: d[name][0] for name in BIG}

    shards = {name: val.astype(BF16) for name, val in local(w).items()}
    conv_high = lax.reduce_precision(w["conv_w"][0], 8, 7)
    conv_mid = lax.reduce_precision(w["conv_w"][0] - conv_high, 8, 7)
    shards["conv_w"] = conv_high.astype(BF16)
    shards["conv_w_mid"] = conv_mid.astype(BF16)
    shards["conv_w_low"] = (w["conv_w"][0] - conv_high - conv_mid).astype(BF16)
    g_ffn1 = _comm_alone([_gather_comm(GATHER_FFN1.pack_local(shards))], "gather_ffn1")[0]

    row = lambda name: w[name].reshape(1, -1)
    gm_w_s = w["gm_w_s"][0]
    gm_b_st = jnp.transpose(w["gm_b_s"][0])
    ffn1 = (row("ffn1_norm"),) + tuple(GATHER_FFN1.pieces(g_ffn1, name) for name in GATHER_FFN1.names)
    gm = (row("gm_ln_g"), row("gm_ln_b"), gm_w_s, gm_b_st, row("gm_out_norm"))

    h1, n1, a1, b1, s1, g_mix = _ffn_fwd(x, *ffn1, "ffn1_fwd", comm=_gather_comm(GATHER_MIX.pack_local(shards)))
    w_in_t = GATHER_MIX.gathered_piece(g_mix, "w_in").reshape(IN_PROJ, D_MODEL)
    w_in_t = jnp.concatenate([w_in_t, jnp.zeros((IN_PROJ_PAD - IN_PROJ, D_MODEL), BF16)], axis=0)
    w_proj_t = GATHER_MIX.gathered_piece(g_mix, "ple_w_proj").reshape(D_MODEL, D_PLE)
    conv_w = sum(GATHER_MIX.gathered_piece(g_mix, name).astype(F32) for name in ("conv_w", "conv_w_mid", "conv_w_low"))
    conv_w = conv_w.reshape(CONV_DIM, SSM_CONV).T
    ssd = (conv_w, row("conv_b"), row("dt_bias"), row("a_log"), row("d_skip"), row("ssm_norm"))
    w_out = GATHER_MIX.pieces(g_mix, "w_out")

    proj, n2 = _mix_in_fwd(h1, row("mix_norm"), w_in_t)
    ya = _gm_fwd(proj, *gm)
    yb, s_all, g_ffn2 = _ssd_fwd(proj, *ssd, comm=_gather_comm(GATHER_FFN2.pack_local(shards)))
    ffn2 = (row("ffn2_norm"),) + tuple(GATHER_FFN2.pieces(g_ffn2, name) for name in GATHER_FFN2.names)
    h3, n3, a3, b3, s3, h2 = _ffn_fwd(h1, *ffn2, "ffn2_fwd", mixed=(ya, yb, w_out))

    g, gp = {}, {}
    dh3, loss, gp["ple_w_gate"], d_w_proj, g["ple_norm"], g["ple_b_gate"], g["final_norm"] = _tail(
        h3, p, target, row("ple_norm"), GATHER_MIX.pieces(g_mix, "ple_w_gate"), row("ple_b_gate"), w_proj_t,
        row("final_norm"))
    gp["ple_w_proj"] = d_w_proj.T

    dh2, da3, db3, g["ffn2_norm"] = _ffn_dgrad(h2, dh3, a3, b3, *ffn2, "ffn2_dgrad")
    gp["ffn2_w_gate"] = _wgrad(n3, da3, 1408, "ffn2_wgrad_gate", transpose_out=True)
    gp["ffn2_w_up"] = _wgrad(n3, db3, 1408, "ffn2_wgrad_up", transpose_out=True)
    gp["ffn2_w_down"] = _wgrad(s3, dh3, 512, "ffn2_wgrad_down", scale=0.5, bk=1024)

    dya, dyb = _out_proj_dgrad(dh2, w_out)
    gp["w_out"] = jnp.concatenate([_wgrad(ya, dh2, 1024, "w_out_wgrad_a"), _wgrad(yb, dh2, 1024, "w_out_wgrad_b")], axis=0)

    dp_zxd, d_conv_w, g["conv_b"], g["dt_bias"], g["a_log"], g["d_skip"], g["ssm_norm"], parts_late = _ssd_bwd(
        proj, dyb, s_all, *ssd, comm=_exchange_comm(SCATTER_LATE.pack_owner_major(gp)))
    gp["conv_w"] = d_conv_w.T
    dp_uv, g["gm_ln_g"], g["gm_ln_b"], g["gm_w_s"], dbst, g["gm_out_norm"] = _gm_bwd(proj, dya, *gm)
    g["gm_b_s"] = jnp.transpose(dbst)

    parts = {}
    gp["w_in"] = jnp.concatenate([_wgrad(n2, dp_uv, 1024, "w_in_wgrad_uv", transpose_out=True),
                                  _wgrad(n2, dp_zxd, 896, "w_in_wgrad_zxd", transpose_out=True)], axis=0)[:IN_PROJ]
    dh1, g["mix_norm"], parts[SCATTER_IN] = _mix_in_dgrad(h1, dh2, dp_uv, dp_zxd, row("mix_norm"), w_in_t,
                                                          comm=_exchange_comm(SCATTER_IN.pack_owner_major(gp)))

    dx, da1, db1, g["ffn1_norm"] = _ffn_dgrad(x, dh1, a1, b1, *ffn1, "ffn1_dgrad")
    gp["ffn1_w_gate"], small_parts = _wgrad(n1, da1, 1408, "ffn1_wgrad_gate", transpose_out=True,
                                            comm=_gather_comm(_pack_small(g)))
    gp["ffn1_w_up"], parts[SCATTER_GATE] = _wgrad(n1, db1, 1408, "ffn1_wgrad_up", transpose_out=True,
                                                  comm=_exchange_comm(SCATTER_GATE.pack_owner_major(gp)))
    gp["ffn1_w_down"], parts[SCATTER_UP] = _wgrad(s1, dh1, 512, "ffn1_wgrad_down", scale=0.5, bk=1024,
                                                  comm=_exchange_comm(SCATTER_UP.pack_owner_major(gp)))
    parts[SCATTER_DOWN] = _comm_alone([_exchange_comm(SCATTER_DOWN.pack_owner_major(gp))], "scatter_ffn1_down")[0]
    parts[SCATTER_LATE] = parts_late

    res_big = {}
    for pack, pack_parts in parts.items():
        for name in pack.names:
            shape, transposed = SHARDS[name]
            if name in ("ple_w_proj", "conv_w"):
                nat = pack.gathered_piece(pack_parts, name).reshape((N_DEV,) + shape[::-1])
                res_big[name] = _sum_adamw(jnp.transpose(nat, (0, 2, 1)), w[name][0], m[name][0], v[name][0], shape[0],
                                           "adamw_" + name)
            elif name == "w_in":
                res_big[name] = _adamw_shard(pack_parts, pack.offsets[name], True, w[name], m[name], v[name],
                                             "adamw_" + name, n_tiles=4)
            else:
                flip = (lambda a: jnp.transpose(a, (0, 2, 1))) if transposed else (lambda a: a)
                res = _adamw_shard(pack_parts, pack.offsets[name], False, flip(w[name]), flip(m[name]), flip(v[name]),
                                   "adamw_" + name, n_tiles=2)
                res_big[name] = [flip(r) for r in res]

    small_shapes = {name: w[name].shape for name in SMALL}
    res_small = _sum_adamw(small_parts, _pack_small(w), _pack_small(m), _pack_small(v), SMALL_ROWS, "adamw_small")
    res_small = [_unpack_small(r, small_shapes) for r in res_small]

    outs = []
    for k in range(4):
        for name in WEIGHTS:
            if name in res_small[k]:
                outs.append(res_small[k][name])
            else:
                outs.append(res_big[name][k].reshape(w[name].shape))
    return loss[0, 0], dx, outs


def kernel(x, p, ffn1_norm, ffn1_w_gate, ffn1_w_up, ffn1_w_down, mix_norm, w_in, gm_ln_g, gm_ln_b, gm_w_s, gm_b_s, gm_out_norm, conv_w, conv_b, dt_bias, a_log, d_skip, ssm_norm, w_out, ffn2_norm, ffn2_w_gate, ffn2_w_up, ffn2_w_down, ple_norm, ple_w_gate, ple_b_gate, ple_w_proj, final_norm, loss_target, m_ffn1_norm, m_ffn1_w_gate, m_ffn1_w_up, m_ffn1_w_down, m_mix_norm, m_w_in, m_gm_ln_g, m_gm_ln_b, m_gm_w_s, m_gm_b_s, m_gm_out_norm, m_conv_w, m_conv_b, m_dt_bias, m_a_log, m_d_skip, m_ssm_norm, m_w_out, m_ffn2_norm, m_ffn2_w_gate, m_ffn2_w_up, m_ffn2_w_down, m_ple_norm, m_ple_w_gate, m_ple_b_gate, m_ple_w_proj, m_final_norm, v_ffn1_norm, v_ffn1_w_gate, v_ffn1_w_up, v_ffn1_w_down, v_mix_norm, v_w_in, v_gm_ln_g, v_gm_ln_b, v_gm_w_s, v_gm_b_s, v_gm_out_norm, v_conv_w, v_conv_b, v_dt_bias, v_a_log, v_d_skip, v_ssm_norm, v_w_out, v_ffn2_norm, v_ffn2_w_gate, v_ffn2_w_up, v_ffn2_w_down, v_ple_norm, v_ple_w_gate, v_ple_b_gate, v_ple_w_proj, v_final_norm):
    args = locals()
    w = {name: args[name] for name in WEIGHTS}
    m = {name: args["m_" + name] for name in WEIGHTS}
    v = {name: args["v_" + name] for name in WEIGHTS}
    loss, dx, outs = _step(x[0], p[0, 0], loss_target[0], w, m, v)
    loss = lax.psum(loss, AXES)
    return (loss, dx[None], *outs)
```

```python
import functools
from typing import NamedTuple

import jax
import jax.numpy as jnp
from jax import lax
from jax.experimental import pallas as pl
from jax.experimental.pallas import tpu as pltpu

F32 = jnp.float32
BF16 = jnp.bfloat16
HIGHEST = lax.Precision.HIGHEST
MESH = pl.DeviceIdType.MESH
AXES = ("x", "y", "c")
N_DEV = 8

D_MODEL = 1024
D_FF = 2816
D_PLE = 256
GM_WIDTH = 1024
GM_HEADS = 8
GM_HEAD_DIM = 128
CHUNK = 128
SSM_WIDTH = 1024
SSM_HEADS = 16
SSM_HEAD_DIM = 64
SSM_GROUPS = 2
SSM_STATE = 128
SSM_CONV = 4
CONV_DIM = SSM_WIDTH + 2 * SSM_GROUPS * SSM_STATE
IN_PROJ = 2 * GM_WIDTH + SSM_WIDTH + CONV_DIM + SSM_HEADS
LANES = 128
BF16_ROWS = 16
IN_PROJ_PAD = IN_PROJ - SSM_HEADS + LANES
UV_W = 2 * GM_WIDTH
ZXD_W = IN_PROJ_PAD - UV_W
HALO = 8
EPS = 1e-6

ADAM_LR = 0.001
ADAM_B1 = 0.9
ADAM_B2 = 0.999
ADAM_EPS = 1e-08
ADAM_WD = 0.01
ADAM_STEP = 10

VMEM_LIMIT = 56 * 1024 * 1024
PACK_COLS = 1024


def _rms(x, g):
    return x * lax.rsqrt(jnp.mean(x * x, axis=-1, keepdims=True) + EPS) * g


def _gelu(x):
    return 0.5 * x * (1.0 + lax.erf(x * (2.0 ** -0.5)))


def _silu(x):
    return x * jax.nn.sigmoid(x)


def _dot(a, b):
    return jnp.dot(a.astype(BF16), b.astype(BF16), preferred_element_type=F32)


def _dot_nt(a, b):
    return lax.dot_general(a.astype(BF16), b.astype(BF16), (((1,), (1,)), ((), ())), preferred_element_type=F32)


def _dot_tn(a, b):
    return lax.dot_general(a.astype(BF16), b.astype(BF16), (((0,), (0,)), ((), ())), preferred_element_type=F32)


def _hdot_tn(a, b):
    return lax.dot_general(a, b, (((0,), (0,)), ((), ())), precision=HIGHEST, preferred_element_type=F32)


def _split3(x):
    hi = x.astype(BF16)
    rest = x - hi.astype(F32)
    mid = rest.astype(BF16)
    return hi, mid, (rest - mid.astype(F32)).astype(BF16)


def _exact_dot(x, mask, dims, x_first=True):
    terms = [lax.dot_general(*((t, mask) if x_first else (mask, t)), (dims, ((), ())), preferred_element_type=F32)
             for t in _split3(x)]
    return (terms[0] + terms[1]) + terms[2]


def _mask_product(fwd_dims, fwd_x_first, bwd_dims, bwd_x_first):
    @jax.custom_vjp
    def product(x, mask):
        return _exact_dot(x, mask, fwd_dims, fwd_x_first)

    def fwd(x, mask):
        return product(x, mask), mask

    def bwd(mask, g):
        return _exact_dot(g, mask, bwd_dims, bwd_x_first), jnp.zeros_like(mask)

    product.defvjp(fwd, bwd)
    return product


_widen = _mask_product(((1,), (0,)), True, ((1,), (1,)), True)
_cumsum_rows = _mask_product(((1,), (0,)), False, ((0,), (0,)), False)
_cumsum_cols = _mask_product(((0,), (0,)), True, ((1,), (1,)), False)


class _Pieces(NamedTuple):
    gathered: jax.Array
    row_off: int
    rows: int


class _Comm(NamedTuple):
    phases: object
    src: jax.Array
    dst: jax.ShapeDtypeStruct


def _tiled(body, name, n_steps, tiled_in, full_in, big_in, tiled_out, acc_out, scratch=(), reverse=False, comm=None):
    n_t, n_f, n_b, n_to, n_a = len(tiled_in), len(full_in), len(big_in), len(tiled_out), len(acc_out)
    n_c = 1 if comm else 0

    def row(i):
        return n_steps - 1 - i if reverse else i

    in_specs, args = [], []
    for arr, br, bc, cb in tiled_in:
        if callable(cb):
            in_specs.append(pl.BlockSpec((br, bc), cb))
        else:
            in_specs.append(pl.BlockSpec((br, bc), functools.partial(lambda i, cb: (row(i), cb), cb=cb)))
        args.append(arr)
    for arr in full_in:
        in_specs.append(pl.BlockSpec(arr.shape, functools.partial(lambda i, nd: (0,) * nd, nd=arr.ndim)))
        args.append(arr)
    big_shapes, n_copies = [], 0
    for big in big_in:
        in_specs.append(pl.BlockSpec(memory_space=pl.ANY))
        if isinstance(big, _Pieces):
            args.append(big.gathered)
            big_shapes.append(((N_DEV * big.rows, PACK_COLS), big.gathered.dtype))
            n_copies += N_DEV
        else:
            args.append(big)
            big_shapes.append((big.shape, big.dtype))
            n_copies += 1
    if comm:
        in_specs.append(pl.BlockSpec(memory_space=pl.ANY))
        args.append(comm.src)
    out_specs, out_shape = [], []
    for rows, cols, dt, br in tiled_out:
        out_specs.append(pl.BlockSpec((br, cols), lambda i: (row(i), 0)))
        out_shape.append(jax.ShapeDtypeStruct((rows, cols), dt))
    for shp, dt in acc_out:
        out_specs.append(pl.BlockSpec(shp, functools.partial(lambda i, nd: (0,) * nd, nd=len(shp))))
        out_shape.append(jax.ShapeDtypeStruct(shp, dt))
    if comm:
        out_specs.append(pl.BlockSpec(memory_space=pl.ANY))
        out_shape.append(comm.dst)
    scratch_shapes = [pltpu.VMEM(shp, dt) for shp, dt in big_shapes] + list(scratch)
    if n_copies:
        scratch_shapes.append(pltpu.SemaphoreType.DMA((n_copies,)))
    if comm:
        scratch_shapes += [pltpu.SemaphoreType.DMA((N_DEV - 1,)), pltpu.SemaphoreType.DMA((N_DEV - 1,)), pltpu.SemaphoreType.DMA]

    def kern(*refs):
        n_in = n_t + n_f + n_b + n_c
        ins = refs[: n_t + n_f]
        big_hbm = refs[n_t + n_f : n_t + n_f + n_b]
        outs = refs[n_in : n_in + n_to + n_a]
        rest = refs[n_in + n_to + n_a + n_c :]
        big_vmem, scr = rest[:n_b], rest[n_b:]
        if comm:
            scr, comm_sems = scr[:-3], scr[-3:]
            comm_start, comm_mid, comm_finish = comm.phases(refs[n_in - 1], refs[n_in + n_to + n_a], *comm_sems)
        if n_copies:
            scr, copy_sems = scr[:-1], scr[-1]
        step = pl.program_id(0)

        @pl.when(step == 0)
        def _():
            copies = []
            for big, src, dst in zip(big_in, big_hbm, big_vmem):
                if isinstance(big, _Pieces):
                    for j in range(N_DEV):
                        copies.append((src.at[j, pl.ds(big.row_off, big.rows), :], dst.at[pl.ds(j * big.rows, big.rows), :]))
                else:
                    copies.append((src, dst))
            copies = [pltpu.make_async_copy(a, b, copy_sems.at[k]) for k, (a, b) in enumerate(copies)]
            for cp in copies:
                cp.start()
            for cp in copies:
                cp.wait()
            for acc in outs[n_to:]:
                acc[...] = jnp.zeros(acc.shape, acc.dtype)
            if comm:
                comm_start()

        body(row(step), *ins, *big_vmem, *outs, *scr)
        if comm:
            pl.when(step == (n_steps - 1) // 2)(comm_mid)
            pl.when(step == n_steps - 1)(comm_finish)

    res = pl.pallas_call(
        kern,
        out_shape=out_shape,
        grid=(n_steps,),
        in_specs=in_specs,
        out_specs=out_specs,
        scratch_shapes=scratch_shapes,
        name=name,
        compiler_params=pltpu.CompilerParams(dimension_semantics=("arbitrary",), vmem_limit_bytes=VMEM_LIMIT),
    )(*args)
    return res


FF_CHUNKS = ((0, 1536), (1536, D_FF))
FFN_TM = 256


def _ffn_fwd(h, g, wg_t, wu_t, wd, name, comm=None, mixed=None):
    T = h.shape[0]
    n_mix = 2 if mixed else 0

    def body(i, h_ref, *refs):
        ya_ref, yb_ref = refs[:n_mix] if mixed else (None, None)
        g_ref, wg_ref, wu_ref, wd_ref = refs[n_mix:n_mix + 4]
        o_ref, n_ref, a_ref, b_ref, s_ref = refs[n_mix + 4 + n_mix // 2:n_mix + 9 + n_mix // 2]
        x = h_ref[...]
        if mixed:
            wo_ref, x_ref = refs[n_mix + 4], refs[-1]
            x = (x + jnp.dot(ya_ref[...], wo_ref[:GM_WIDTH, :], preferred_element_type=F32)
                 + jnp.dot(yb_ref[...], wo_ref[GM_WIDTH:, :], preferred_element_type=F32))
            x_ref[...] = x
        n = _rms(x, g_ref[...]).astype(BF16)
        n_ref[...] = n
        f = jnp.zeros(x.shape, F32)
        for lo, hi in FF_CHUNKS:
            a = _dot_nt(n, wg_ref[lo:hi, :])
            b = _dot_nt(n, wu_ref[lo:hi, :])
            s = (_silu(a) * b).astype(BF16)
            a_ref[:, lo:hi] = a.astype(BF16)
            b_ref[:, lo:hi] = b.astype(BF16)
            s_ref[:, lo:hi] = s
            f = f + jnp.dot(s, wd_ref[lo:hi, :], preferred_element_type=F32)
        o_ref[...] = x + 0.5 * f

    tiled_in, big_in = [(h, FFN_TM, D_MODEL, 0)], [wg_t, wu_t, wd]
    tiled_out = [(T, D_MODEL, F32, FFN_TM), (T, D_MODEL, BF16, FFN_TM), (T, D_FF, BF16, FFN_TM), (T, D_FF, BF16, FFN_TM),
                 (T, D_FF, BF16, FFN_TM)]
    if mixed:
        tiled_in += [(mixed[0], FFN_TM, GM_WIDTH, 0), (mixed[1], FFN_TM, SSM_WIDTH, 0)]
        big_in.append(mixed[2])
        tiled_out.append((T, D_MODEL, F32, FFN_TM))
    return _tiled(body, name, T // FFN_TM, tiled_in, [g], big_in, tiled_out, [], comm=comm)


def _ffn_dgrad(h, dout, a16, b16, g, wg_t, wu_t, wd, name):
    T = h.shape[0]

    def body(i, h_ref, do_ref, a_ref, b_ref, g_ref, wg_ref, wu_ref, wd_ref, dh_ref, da_ref, db_ref, dg_ref):
        dout = do_ref[...]
        _, rms_vjp = jax.vjp(_rms, h_ref[...], g_ref[...])
        dfo = (0.5 * dout).astype(BF16)
        dn = jnp.zeros(dout.shape, F32)
        for lo, hi in FF_CHUNKS:
            a = a_ref[:, lo:hi].astype(F32)
            b = b_ref[:, lo:hi].astype(F32)
            sg = jax.nn.sigmoid(a)
            ds = _dot_nt(dfo, wd_ref[lo:hi, :])
            db = (ds * (a * sg)).astype(BF16)
            da = (ds * b * (sg * (1.0 + a * (1.0 - sg)))).astype(BF16)
            dn = dn + _dot(da, wg_ref[lo:hi, :]) + _dot(db, wu_ref[lo:hi, :])
            da_ref[:, lo:hi] = da
            db_ref[:, lo:hi] = db
        dx, dg = rms_vjp(dn)
        dh_ref[...] = dout + dx
        dg_ref[...] += dg

    return _tiled(body, name, T // FFN_TM,
                  [(h, FFN_TM, D_MODEL, 0), (dout, FFN_TM, D_MODEL, 0), (a16, FFN_TM, D_FF, 0), (b16, FFN_TM, D_FF, 0)],
                  [g], [wg_t, wu_t, wd],
                  [(T, D_MODEL, F32, FFN_TM), (T, D_FF, BF16, FFN_TM), (T, D_FF, BF16, FFN_TM)], [((1, D_MODEL), F32)])


def _wgrad(a, b, bn, name, scale=None, transpose_out=False, bk=2048, comm=None):
    T, M = a.shape
    N = b.shape[1]
    bk = min(bk, T)
    assert M % LANES == 0 and N % bn == 0 and T % bk == 0
    n_j, n_k = N // bn, T // bk
    n_c = 1 if comm else 0

    def kern(*refs):
        a_ref, b_ref, o_ref, acc_ref = refs[0], refs[1], refs[2 + n_c], refs[3 + 2 * n_c]
        j, k = pl.program_id(0), pl.program_id(1)
        if comm:
            comm_start, comm_mid, comm_finish = comm.phases(refs[2], refs[4], *refs[6:])
            pl.when((j == 0) & (k == 0))(comm_start)

        @pl.when(k == 0)
        def _():
            acc_ref[...] = jnp.zeros(acc_ref.shape, F32)

        bv = b_ref[...]
        if scale is not None:
            bv = bv * scale
        acc_ref[...] += _dot_tn(a_ref[...], bv)

        @pl.when(k == n_k - 1)
        def _():
            acc = acc_ref[...]
            o_ref[...] = (acc.T if transpose_out else acc).astype(BF16)

        if comm:
            pl.when((j == (n_j - 1) // 2) & (k == n_k - 1))(comm_mid)
            pl.when((j == n_j - 1) & (k == n_k - 1))(comm_finish)

    if transpose_out:
        out_shape, out_spec = (N, M), pl.BlockSpec((bn, M), lambda j, k: (j, 0))
    else:
        out_shape, out_spec = (M, N), pl.BlockSpec((M, bn), lambda j, k: (0, j))
    any_spec = pl.BlockSpec(memory_space=pl.ANY)
    comm_sems = [pltpu.SemaphoreType.DMA((N_DEV - 1,)), pltpu.SemaphoreType.DMA((N_DEV - 1,)), pltpu.SemaphoreType.DMA]
    res = pl.pallas_call(
        kern,
        out_shape=[jax.ShapeDtypeStruct(out_shape, BF16)] + ([comm.dst] if comm else []),
        grid=(n_j, n_k),
        in_specs=[pl.BlockSpec((bk, M), lambda j, k: (k, 0)), pl.BlockSpec((bk, bn), lambda j, k: (k, j))] + [any_spec] * n_c,
        out_specs=[out_spec] + [any_spec] * n_c,
        scratch_shapes=[pltpu.VMEM((M, bn), F32)] + (comm_sems if comm else []),
        name=name,
        compiler_params=pltpu.CompilerParams(dimension_semantics=("arbitrary", "arbitrary"), vmem_limit_bytes=VMEM_LIMIT),
    )(a, b, *([comm.src] if comm else []))
    return res if comm else res[0]


PROJ_TM = 256
Z_BLK = 2 * GM_WIDTH // SSM_WIDTH
XBC_BLK = (2 * GM_WIDTH + SSM_WIDTH) // CONV_DIM
DT_BLK = (2 * GM_WIDTH + SSM_WIDTH + CONV_DIM) // LANES


def _mix_in_fwd(h, g, w_in_t, conv_w, conv_b):
    T = h.shape[0]
    x0 = XBC_BLK * CONV_DIM

    def body(i, h_ref, g_ref, cw_ref, cb_ref, w_ref, p_ref, n_ref, xc_ref, ext_ref):
        @pl.when(i == 0)
        def _():
            ext_ref[0:HALO, :] = jnp.zeros((HALO, CONV_DIM), F32)

        n = _rms(h_ref[...], g_ref[...]).astype(BF16)
        n_ref[...] = n
        proj = _dot_nt(n, w_ref[...])
        p_ref[...] = proj
        ext_ref[HALO:, :] = proj[:, x0:x0 + CONV_DIM]
        xc_ref[...] = _conv_taps(ext_ref, cw_ref[...], cb_ref[...], PROJ_TM)
        ext_ref[0:HALO, :] = ext_ref[PROJ_TM:PROJ_TM + HALO, :]

    return _tiled(body, "mix_in_fwd", T // PROJ_TM, [(h, PROJ_TM, D_MODEL, 0)], [g, conv_w, conv_b], [w_in_t],
                  [(T, IN_PROJ_PAD, F32, PROJ_TM), (T, D_MODEL, BF16, PROJ_TM), (T, CONV_DIM, F32, PROJ_TM)], [],
                  scratch=[pltpu.VMEM((HALO + PROJ_TM, CONV_DIM), F32)])


def _mix_in_dgrad(h, dh_in, dp_uv, dp_zxd, g, w_in_t, comm=None):
    T = h.shape[0]

    def body(i, h_ref, dh_ref, duv_ref, dzxd_ref, g_ref, w_ref, o_ref, dg_ref):
        dn = _dot(duv_ref[...], w_ref[:UV_W, :]) + _dot(dzxd_ref[...], w_ref[UV_W:, :])
        _, rms_vjp = jax.vjp(_rms, h_ref[...], g_ref[...])
        dx, dg = rms_vjp(dn)
        o_ref[...] = dh_ref[...] + dx
        dg_ref[...] += dg

    return _tiled(body, "mix_in_dgrad", T // PROJ_TM,
                  [(h, PROJ_TM, D_MODEL, 0), (dh_in, PROJ_TM, D_MODEL, 0), (dp_uv, PROJ_TM, UV_W, 0),
                   (dp_zxd, PROJ_TM, ZXD_W, 0)], [g], [w_in_t],
                  [(T, D_MODEL, F32, PROJ_TM)], [((1, D_MODEL), F32)], comm=comm)


def _out_proj_dgrad(dh, w_out):
    T = dh.shape[0]

    def body(i, dh_ref, w_ref, dya_ref, dyb_ref):
        d = dh_ref[...].astype(BF16)
        dya_ref[...] = _dot_nt(d, w_ref[:GM_WIDTH, :])
        dyb_ref[...] = _dot_nt(d, w_ref[GM_WIDTH:, :])

    return _tiled(body, "out_proj_dgrad", T // PROJ_TM, [(dh, PROJ_TM, D_MODEL, 0)], [], [w_out],
                  [(T, GM_WIDTH, F32, PROJ_TM), (T, SSM_WIDTH, F32, PROJ_TM)], [])


def _gm_chunk(u, v, ln_g, ln_b, b_st, out_g, *w_heads):
    ug = _gelu(u)
    vg = _gelu(v)
    mu = jnp.mean(vg, axis=-1, keepdims=True)
    xc = vg - mu
    vn = xc * lax.rsqrt(jnp.mean(xc * xc, axis=-1, keepdims=True) + EPS) * ln_g + ln_b
    t_idx = lax.broadcasted_iota(jnp.int32, (CHUNK, CHUNK), 0)
    s_idx = lax.broadcasted_iota(jnp.int32, (CHUNK, CHUNK), 1)
    causal = t_idx >= s_idx
    mixed = []
    for hd in range(GM_HEADS):
        wm = jnp.where(causal, w_heads[hd], 0.0)
        cols = slice(hd * GM_HEAD_DIM, (hd + 1) * GM_HEAD_DIM)
        mixed.append(_dot(wm, vn[:, cols]) + b_st[:, hd:hd + 1])
    ya0 = ug * jnp.concatenate(mixed, axis=1)
    return _rms(ya0, out_g)


GM_FWD_CHUNKS = 2


def _gm_fwd(proj, ln_g, ln_b, w_s, b_st, out_g):
    T = proj.shape[0]

    rows = GM_FWD_CHUNKS * CHUNK

    def body(i, u_ref, v_ref, lg_ref, lb_ref, w_ref, bs_ref, og_ref, ya_ref):
        w_heads = [w_ref[hd] for hd in range(GM_HEADS)]
        for c in range(GM_FWD_CHUNKS):
            tok = pl.ds(c * CHUNK, CHUNK)
            ya = _gm_chunk(u_ref[tok, :], v_ref[tok, :], lg_ref[...], lb_ref[...], bs_ref[...], og_ref[...], *w_heads)
            ya_ref[tok, :] = ya.astype(BF16)

    return _tiled(body, "gmlp_fwd", T // rows, [(proj, rows, GM_WIDTH, 0), (proj, rows, GM_WIDTH, 1)],
                  [ln_g, ln_b, w_s, b_st, out_g], [], [(T, GM_WIDTH, BF16, rows)], [])[0]


def _gm_bwd(proj, dya, ln_g, ln_b, w_s, b_st, out_g):
    T = proj.shape[0]

    def body(i, u_ref, v_ref, dy_ref, lg_ref, lb_ref, w_ref, bs_ref, og_ref, duv_ref, dlg_ref, dlb_ref, dw_ref, dbs_ref,
             dog_ref):
        w_heads = [w_ref[hd] for hd in range(GM_HEADS)]
        _, vjp = jax.vjp(_gm_chunk, u_ref[...], v_ref[...], lg_ref[...], lb_ref[...], bs_ref[...], og_ref[...], *w_heads)
        grads = vjp(dy_ref[...])
        duv_ref[:, :GM_WIDTH] = grads[0].astype(BF16)
        duv_ref[:, GM_WIDTH:] = grads[1].astype(BF16)
        dlg_ref[...] += grads[2]
        dlb_ref[...] += grads[3]
        dbs_ref[...] += grads[4]
        dog_ref[...] += grads[5]
        for hd in range(GM_HEADS):
            dw_ref[hd] += grads[6 + hd]

    return _tiled(body, "gmlp_bwd", T // CHUNK,
                  [(proj, CHUNK, GM_WIDTH, 0), (proj, CHUNK, GM_WIDTH, 1), (dya, CHUNK, GM_WIDTH, 0)],
                  [ln_g, ln_b, w_s, b_st, out_g], [], [(T, UV_W, BF16, CHUNK)],
                  [((1, GM_WIDTH), F32), ((1, GM_WIDTH), F32), ((GM_HEADS, CHUNK, CHUNK), F32),
                   ((CHUNK, GM_HEADS), F32), ((1, GM_WIDTH), F32)])


def _ssd_chunk(xc, z, dtr, s_in, dt_bias, a_log, d_skip, norm_g):
    half = SSM_WIDTH // SSM_GROUPS
    l_idx = lax.broadcasted_iota(jnp.int32, (CHUNK, CHUNK), 0)
    s_idx = lax.broadcasted_iota(jnp.int32, (CHUNK, CHUNK), 1)
    causal = l_idx >= s_idx
    head_of_col = lax.broadcasted_iota(jnp.int32, (SSM_HEADS, SSM_WIDTH), 1) // SSM_HEAD_DIM
    expand = (head_of_col == lax.broadcasted_iota(jnp.int32, (SSM_HEADS, SSM_WIDTH), 0)).astype(BF16)

    xcs = _silu(xc)
    xs = xcs[:, :SSM_WIDTH]
    dt = jax.nn.softplus(dtr + dt_bias)
    adt = dt * (-jnp.exp(a_log))
    acs = _cumsum_rows(adt, causal.astype(BF16))
    acs_t = _cumsum_cols(adt, (l_idx <= s_idx).astype(BF16))
    tot = acs[CHUNK - 1:CHUNK, :]
    dt_w = _widen(dt, expand)
    out_decay_w = _widen(jnp.exp(acs), expand)
    state_decay_w = _widen(jnp.exp(tot - acs), expand)
    chunk_decay_w = _widen(jnp.exp(tot), expand)
    d_skip_w = _widen(d_skip, expand)
    xdt = xs * dt_w
    xdt_decayed = xdt * state_decay_w

    y_diag, y_off, states = [], [], []
    for grp in range(SSM_GROUPS):
        b0 = SSM_WIDTH + grp * SSM_STATE
        c0 = SSM_WIDTH + SSM_GROUPS * SSM_STATE + grp * SSM_STATE
        bm = xcs[:, b0:b0 + SSM_STATE].astype(BF16)
        cm = xcs[:, c0:c0 + SSM_STATE].astype(BF16)
        cb = _dot_nt(cm, bm)
        for k in range(grp * SSM_HEADS // SSM_GROUPS, (grp + 1) * SSM_HEADS // SSM_GROUPS):
            decay = jnp.exp(jnp.where(causal, acs[:, k:k + 1] - acs_t[k:k + 1, :], -jnp.inf))
            y_diag.append(_dot(cb * decay, xdt[:, k * SSM_HEAD_DIM:(k + 1) * SSM_HEAD_DIM]))
        cols = slice(grp * half, (grp + 1) * half)
        states.append(_dot_tn(bm, xdt_decayed[:, cols]))
        y_off.append(_dot(cm, s_in[:, cols]))
    y = jnp.concatenate(y_diag, axis=1) + jnp.concatenate(y_off, axis=1) * out_decay_w + xs * d_skip_w
    s_out = s_in * chunk_decay_w + jnp.concatenate(states, axis=1)
    y = y * _silu(z)
    normed = []
    for grp in range(SSM_GROUPS):
        yg = y[:, grp * half:(grp + 1) * half]
        normed.append(yg * lax.rsqrt(jnp.mean(yg * yg, axis=-1, keepdims=True) + EPS))
    return jnp.concatenate(normed, axis=1) * norm_g, s_out


def _conv_taps(ext_ref, w, b, rows):
    y = b
    for k in range(SSM_CONV):
        y = y + w[k:k + 1, :] * ext_ref[pl.ds(HALO - (SSM_CONV - 1) + k, rows), :]
    return y


def _ssd_fwd(proj, xc, dt_bias, a_log, d_skip, norm_g, comm=None):
    T = proj.shape[0]
    n_chunks = T // CHUNK

    def body(i, z_ref, xc_ref, dt_ref, dtb_ref, al_ref, dsk_ref, ng_ref, yb_ref, sin_ref, st_ref):
        @pl.when(i == 0)
        def _():
            st_ref[...] = jnp.zeros(st_ref.shape, F32)

        s_in = st_ref[...]
        yb, s_out = _ssd_chunk(xc_ref[...], z_ref[...], dt_ref[:, 0:SSM_HEADS], s_in, dtb_ref[...], al_ref[...],
                               dsk_ref[...], ng_ref[...])
        yb_ref[...] = yb.astype(BF16)
        sin_ref[...] = s_in
        st_ref[...] = s_out

    return _tiled(body, "ssd_fwd", n_chunks,
                  [(proj, CHUNK, SSM_WIDTH, Z_BLK), (xc, CHUNK, CONV_DIM, 0), (proj, CHUNK, LANES, DT_BLK)],
                  [dt_bias, a_log, d_skip, norm_g], [],
                  [(T, SSM_WIDTH, BF16, CHUNK), (n_chunks * SSM_STATE, SSM_WIDTH, F32, SSM_STATE)], [],
                  scratch=[pltpu.VMEM((SSM_STATE, SSM_WIDTH), F32)], comm=comm)


def _ssd_bwd(proj, xc, dyb, s_all, conv_w, dt_bias, a_log, d_skip, norm_g, comm=None):
    T = proj.shape[0]
    n_chunks = T // CHUNK

    def body(i, z_ref, x_ref, xc_ref, dt_ref, dy_ref, sin_ref, cw_ref, dtb_ref, al_ref, dsk_ref, ng_ref,
             dzxd_ref, dcw_ref, dcb_ref, ddtb_ref, dal_ref, ddsk_ref, dng_ref, dext_ref, dst_ref):
        @pl.when(i == n_chunks - 1)
        def _():
            dext_ref[CHUNK:, :] = jnp.zeros((HALO, CONV_DIM), F32)
            dst_ref[...] = jnp.zeros(dst_ref.shape, F32)

        _, vjp = jax.vjp(_ssd_chunk, xc_ref[...], z_ref[...], dt_ref[:, 0:SSM_HEADS], sin_ref[...], dtb_ref[...], al_ref[...],
                         dsk_ref[...], ng_ref[...])
        dxc, dz, ddtr, ds_in, ddtb, dal, ddsk, dng = vjp((dy_ref[...], dst_ref[...]))
        dst_ref[...] = ds_in
        ddtb_ref[...] += ddtb
        dal_ref[...] += dal
        ddsk_ref[...] += ddsk
        dng_ref[...] += dng
        dext_ref[0:CHUNK, :] = dxc
        cw = cw_ref[...]
        x = x_ref[...]
        dx = jnp.zeros((CHUNK, CONV_DIM), F32)
        for k in range(SSM_CONV):
            shifted = dext_ref[pl.ds(SSM_CONV - 1 - k, CHUNK), :]
            dx = dx + cw[k:k + 1, :] * shifted
            dcw_ref[k:k + 1, :] += jnp.sum(shifted * x, axis=0, keepdims=True)
        dcb_ref[...] += jnp.sum(dxc, axis=0, keepdims=True)
        dext_ref[CHUNK:, :] = dext_ref[0:HALO, :]
        dzxd_ref[:, 0:SSM_WIDTH] = dz.astype(BF16)
        dzxd_ref[:, SSM_WIDTH:SSM_WIDTH + CONV_DIM] = dx.astype(BF16)
        dzxd_ref[:, SSM_WIDTH + CONV_DIM:] = jnp.concatenate(
            [ddtr, jnp.zeros((CHUNK, LANES - SSM_HEADS), F32)], axis=1).astype(BF16)

    return _tiled(body, "ssd_bwd", n_chunks,
                  [(proj, CHUNK, SSM_WIDTH, Z_BLK), (proj, CHUNK, CONV_DIM, XBC_BLK), (xc, CHUNK, CONV_DIM, 0),
                   (proj, CHUNK, LANES, DT_BLK), (dyb, CHUNK, SSM_WIDTH, 0), (s_all, SSM_STATE, SSM_WIDTH, 0)],
                  [conv_w, dt_bias, a_log, d_skip, norm_g], [],
                  [(T, ZXD_W, BF16, CHUNK)],
                  [((SSM_CONV, CONV_DIM), F32), ((1, CONV_DIM), F32), ((1, SSM_HEADS), F32), ((1, SSM_HEADS), F32),
                   ((1, SSM_HEADS), F32), ((1, SSM_WIDTH), F32)],
                  scratch=[pltpu.VMEM((CHUNK + HALO, CONV_DIM), F32), pltpu.VMEM((SSM_STATE, SSM_WIDTH), F32)],
                  reverse=True, comm=comm)


TAIL_TM = 512


def _tail(h, p, target, ple_norm, w_gate, b_gate, w_proj_t, final_norm):
    T = h.shape[0]

    def head(x, pre, pp, b_g, f_norm, tgt):
        gate = jax.nn.sigmoid(pre + b_g)
        out = _rms(x + gate * pp, f_norm)
        err = out - tgt
        return 0.5 * jnp.sum(jnp.mean(err * err, axis=-1, keepdims=True), axis=0, keepdims=True)

    def body(i, h_ref, p_ref, t_ref, pn_ref, bg_ref, fn_ref, wg_ref, wp_ref, dh_ref, loss_ref, dwg_ref, dwp_ref, dpn_ref,
             dbg_ref, dfn_ref):
        x = h_ref[...]
        n4f, n_vjp = jax.vjp(_rms, x, pn_ref[...])
        n4 = n4f.astype(BF16)
        pre = jnp.dot(n4, wg_ref[...], preferred_element_type=F32)
        p16 = p_ref[...].astype(BF16)
        pp = _dot_nt(p16, wp_ref[...])
        loss, h_vjp = jax.vjp(functools.partial(head, tgt=t_ref[...]), x, pre, pp, bg_ref[...], fn_ref[...])
        dx, dpre, dpp, dbg, dfn = h_vjp(jnp.ones((1, 1), F32))
        dpre16 = dpre.astype(BF16)
        dn4 = _dot_nt(dpre16, wg_ref[...])
        dx2, dpn = n_vjp(dn4)
        dh_ref[...] = dx + dx2
        loss_ref[...] += loss
        dwg_ref[...] += _dot_tn(n4, dpre16)
        dwp_ref[...] += _dot_tn(p16, dpp)
        dpn_ref[...] += dpn
        dbg_ref[...] += dbg
        dfn_ref[...] += dfn

    return _tiled(body, "tail", T // TAIL_TM,
                  [(h, TAIL_TM, D_MODEL, 0), (p, TAIL_TM, D_PLE, 0), (target, TAIL_TM, D_MODEL, 0)],
                  [ple_norm, b_gate, final_norm], [w_gate, w_proj_t],
                  [(T, D_MODEL, F32, TAIL_TM)],
                  [((1, 1), F32), ((D_MODEL, D_MODEL), F32), ((D_PLE, D_MODEL), F32), ((1, D_MODEL), F32),
                   ((1, D_MODEL), F32), ((1, D_MODEL), F32)])


def _gather_phases(x_ref, out_ref, send_sems, recv_sems, local_sem):
    mx, my, mc = lax.axis_index("x"), lax.axis_index("y"), lax.axis_index("c")
    me, sibling = (mx, my, mc), (mx, my, 1 - mc)
    chips = [(1 - mx, my), (mx, 1 - my), (1 - mx, 1 - my)]

    def rows(px, py, pc):
        return out_ref.at[4 * px + 2 * py + pc]

    def copy(k, block, to, src=None):
        return pltpu.make_async_remote_copy(
            src_ref=rows(*block) if src is None else src, dst_ref=rows(*block),
            send_sem=send_sems.at[k], recv_sem=recv_sems.at[k], device_id=to, device_id_type=MESH)

    mine = pltpu.make_async_copy(x_ref, rows(*me), local_sem)
    first = [copy(0, me, sibling, src=x_ref)] + [copy(1 + j, me, (*chip, mc), src=x_ref) for j, chip in enumerate(chips)]
    passed = [copy(4 + j, (*chip, mc), sibling) for j, chip in enumerate(chips)]

    def start():
        mine.start()
        for cp in first:
            cp.start()

    def mid():
        for j, chip in enumerate(chips):
            copy(1 + j, (*chip, mc), me).wait_recv()
            passed[j].start()

    def finish():
        copy(0, sibling, me).wait_recv()
        for j, chip in enumerate(chips):
            copy(4 + j, (*chip, 1 - mc), me).wait_recv()
        for cp in first + passed:
            cp.wait_send()
        mine.wait()

    return start, mid, finish


def _exchange_phases(x_ref, out_ref, send_sems, recv_sems, local_sem):
    mx, my, mc = lax.axis_index("x"), lax.axis_index("y"), lax.axis_index("c")
    me = 4 * mx + 2 * my + mc
    mine = pltpu.make_async_copy(x_ref.at[me], out_ref.at[me], local_sem)
    copies = []
    for k in range(1, N_DEV):
        px = 1 - mx if k & 4 else mx
        py = 1 - my if k & 2 else my
        pc = 1 - mc if k & 1 else mc
        copies.append(pltpu.make_async_remote_copy(
            src_ref=x_ref.at[4 * px + 2 * py + pc], dst_ref=out_ref.at[me], send_sem=send_sems.at[k - 1],
            recv_sem=recv_sems.at[k - 1], device_id=(px, py, pc), device_id_type=MESH))

    def start():
        mine.start()
        for cp in copies:
            cp.start()

    def finish():
        for cp in copies:
            cp.wait_recv()
        for cp in copies:
            cp.wait_send()
        mine.wait()

    return start, lambda: None, finish


def _gather_comm(x):
    return _Comm(_gather_phases, x, jax.ShapeDtypeStruct((N_DEV,) + x.shape, x.dtype))


def _exchange_comm(x):
    return _Comm(_exchange_phases, x, jax.ShapeDtypeStruct(x.shape, x.dtype))


def _comm_alone(comms, name):
    n = len(comms)

    def body(*refs):
        phases = [comm.phases(refs[k], refs[n + k], *refs[2 * n + 3 * k:2 * n + 3 * k + 3]) for k, comm in enumerate(comms)]
        for step in range(3):
            for phase in phases:
                phase[step]()

    any_spec = pl.BlockSpec(memory_space=pl.ANY)
    return pl.pallas_call(
        body,
        out_shape=[comm.dst for comm in comms],
        in_specs=[any_spec] * n,
        out_specs=[any_spec] * n,
        scratch_shapes=[pltpu.SemaphoreType.DMA((N_DEV - 1,)), pltpu.SemaphoreType.DMA((N_DEV - 1,)), pltpu.SemaphoreType.DMA] * n,
        name=name,
    )(*[comm.src for comm in comms])


def _sum_parts(p_ref):
    g = p_ref[0].astype(F32)
    for j in range(1, N_DEV):
        g = g + p_ref[j].astype(F32)
    return g


def _adamw_store(g, w_ref, m_ref, v_ref, g_ref, d_ref, nm_ref, nv_ref):
    m_new = ADAM_B1 * m_ref[...] + (1.0 - ADAM_B1) * g
    v_new = ADAM_B2 * v_ref[...] + (1.0 - ADAM_B2) * jnp.square(g)
    m_hat = m_new / (1.0 - ADAM_B1 ** ADAM_STEP)
    v_hat = v_new / (1.0 - ADAM_B2 ** ADAM_STEP)
    g_ref[...] = g
    d_ref[...] = -ADAM_LR * (m_hat / (jnp.sqrt(v_hat) + ADAM_EPS) + ADAM_WD * w_ref[...])
    nm_ref[...] = m_new
    nv_ref[...] = v_new


def _adamw_shard(parts, off, transposed, w, m, v, name, n_tiles=1):
    _, r, c = w.shape
    tr = r // n_tiles
    if transposed:
        rows = -(-c // BF16_ROWS) * BF16_ROWS
        window = (N_DEV, rows, tr)
    else:
        assert c == PACK_COLS
        window = (N_DEV, tr, PACK_COLS)

    def kern(p_hbm, w_ref, m_ref, v_ref, g_ref, d_ref, nm_ref, nv_ref, buf, sem):
        i = pl.program_id(0)
        if transposed:
            src = p_hbm.at[:, pl.ds(off, rows), pl.ds(pl.multiple_of(i * tr, LANES), tr)]
        else:
            src = p_hbm.at[:, pl.ds(pl.multiple_of(off + i * tr, BF16_ROWS), tr), :]
        cp = pltpu.make_async_copy(src, buf, sem)
        cp.start()
        cp.wait()
        g = _sum_parts(buf)
        if transposed:
            eye = (lax.broadcasted_iota(jnp.int32, (rows, c), 0) == lax.broadcasted_iota(jnp.int32, (rows, c), 1)).astype(F32)
            g = _hdot_tn(g, eye)
        _adamw_store(g, w_ref, m_ref, v_ref, g_ref, d_ref, nm_ref, nv_ref)

    spec = pl.BlockSpec((None, tr, c), lambda i: (0, i, 0))
    return pl.pallas_call(
        kern,
        out_shape=[jax.ShapeDtypeStruct((1, r, c), F32)] * 4,
        grid=(n_tiles,),
        in_specs=[pl.BlockSpec(memory_space=pl.ANY), spec, spec, spec],
        out_specs=[spec] * 4,
        scratch_shapes=[pltpu.VMEM(window, parts.dtype), pltpu.SemaphoreType.DMA],
        name=name,
        compiler_params=pltpu.CompilerParams(dimension_semantics=("arbitrary",), vmem_limit_bytes=VMEM_LIMIT),
    )(parts, w, m, v)


def _sum_adamw(parts, w, m, v, tr, name):
    _, R, C = parts.shape

    def kern(p_ref, w_ref, m_ref, v_ref, g_ref, d_ref, nm_ref, nv_ref):
        _adamw_store(_sum_parts(p_ref), w_ref, m_ref, v_ref, g_ref, d_ref, nm_ref, nv_ref)

    row_spec = pl.BlockSpec((tr, C), lambda i: (i, 0))
    return pl.pallas_call(
        kern,
        out_shape=[jax.ShapeDtypeStruct((R, C), F32)] * 4,
        grid=(R // tr,),
        in_specs=[pl.BlockSpec((N_DEV, tr, C), lambda i: (0, i, 0)), row_spec, row_spec, row_spec],
        out_specs=[row_spec] * 4,
        name=name,
        compiler_params=pltpu.CompilerParams(dimension_semantics=("arbitrary",), vmem_limit_bytes=VMEM_LIMIT),
    )(parts, w, m, v)


FF_SHARD = D_FF // N_DEV
CONV_SHARD = (SSM_CONV, CONV_DIM // N_DEV)
SHARDS = {"ffn1_w_gate": ((D_MODEL, FF_SHARD), True), "ffn1_w_up": ((D_MODEL, FF_SHARD), True),
          "ffn1_w_down": ((FF_SHARD, D_MODEL), False),
          "ffn2_w_gate": ((D_MODEL, FF_SHARD), True), "ffn2_w_up": ((D_MODEL, FF_SHARD), True),
          "ffn2_w_down": ((FF_SHARD, D_MODEL), False),
          "w_out": ((2 * D_MODEL // N_DEV, D_MODEL), False), "ple_w_gate": ((D_MODEL // N_DEV, D_MODEL), False),
          "w_in": ((D_MODEL, IN_PROJ // N_DEV), True), "ple_w_proj": ((D_PLE, D_MODEL // N_DEV), True),
          "conv_w": (CONV_SHARD, True),
          "conv_w_mid": (CONV_SHARD, True), "conv_w_low": (CONV_SHARD, True)}
BIG = tuple(name for name in SHARDS if not name.startswith("conv_w_"))
SMALL = ("ffn1_norm", "mix_norm", "gm_ln_g", "gm_ln_b", "gm_w_s", "gm_b_s", "gm_out_norm", "conv_b", "dt_bias", "a_log",
         "d_skip", "ssm_norm", "ffn2_norm", "ple_norm", "ple_b_gate", "final_norm")
SMALL_ROWS = 144


def _piece_rows(name):
    shape = SHARDS[name][0]
    return -(-(shape[0] * shape[1]) // PACK_COLS)


def _pad_cols(flat, name):
    pad = _piece_rows(name) * PACK_COLS - flat.shape[-1]
    return flat if pad == 0 else jnp.pad(flat, [(0, 0)] * (flat.ndim - 1) + [(0, pad)])


class _Pack:
    def __init__(self, names, tile_rows):
        self.names, self.tile_rows, self.offsets, off = names, tile_rows, {}, 0
        for name in names:
            self.offsets[name] = off
            off += _piece_rows(name)
        self.rows = -(-off // tile_rows) * tile_rows

    def pack_local(self, vals):
        parts = []
        for name in self.names:
            val = vals[name]
            parts.append(_pad_cols((val.T if SHARDS[name][1] else val).reshape(-1), name))
        flat = jnp.concatenate(parts)
        return jnp.pad(flat, (0, self.rows * PACK_COLS - flat.shape[0])).reshape(self.rows, PACK_COLS)

    def pack_owner_major(self, grads):
        parts, rows = [], 0
        for name in self.names:
            grad, piece_rows = grads[name].astype(BF16), _piece_rows(name)
            if grad.shape != (N_DEV * piece_rows, PACK_COLS):
                grad = _pad_cols(grad.reshape(N_DEV, -1), name)
            parts.append(grad.reshape(N_DEV, piece_rows, PACK_COLS))
            rows += piece_rows
        if rows < self.rows:
            parts.append(jnp.zeros((N_DEV, self.rows - rows, PACK_COLS), BF16))
        return parts[0] if len(parts) == 1 else jnp.concatenate(parts, axis=1)

    def gathered_piece(self, gathered, name):
        shape = SHARDS[name][0]
        rows = gathered[:, self.offsets[name]:self.offsets[name] + _piece_rows(name), :]
        return rows.reshape(N_DEV, -1)[:, :shape[0] * shape[1]]

    def pieces(self, gathered, name):
        return _Pieces(gathered, self.offsets[name], _piece_rows(name))


GATHER_FFN1 = _Pack(("ffn1_w_gate", "ffn1_w_up", "ffn1_w_down"), BF16_ROWS)
GATHER_MIX = _Pack(("w_out", "ple_w_gate", "w_in", "ple_w_proj", "conv_w", "conv_w_mid", "conv_w_low"), BF16_ROWS)
GATHER_FFN2 = _Pack(("ffn2_w_gate", "ffn2_w_up", "ffn2_w_down"), BF16_ROWS)
SCATTER_LATE = _Pack(("ffn2_w_gate", "ffn2_w_up", "ffn2_w_down", "w_out", "ple_w_gate", "ple_w_proj"), BF16_ROWS)
SCATTER_IN = _Pack(("w_in", "conv_w"), BF16_ROWS)
SCATTER_GATE = _Pack(("ffn1_w_gate",), BF16_ROWS)
SCATTER_UP = _Pack(("ffn1_w_up",), BF16_ROWS)
SCATTER_DOWN = _Pack(("ffn1_w_down",), BF16_ROWS)


def _pack_small(vals):
    flat = jnp.concatenate([vals[name].reshape(-1).astype(F32) for name in SMALL])
    return jnp.pad(flat, (0, SMALL_ROWS * PACK_COLS - flat.shape[0])).reshape(SMALL_ROWS, PACK_COLS)


def _unpack_small(packed, shapes):
    out, off = {}, 0
    flat = packed.reshape(-1)
    for name in SMALL:
        n = 1
        for s in shapes[name]:
            n *= s
        out[name] = flat[off:off + n].reshape(shapes[name])
        off += n
    return out


WEIGHTS = ("ffn1_norm", "ffn1_w_gate", "ffn1_w_up", "ffn1_w_down", "mix_norm", "w_in", "gm_ln_g", "gm_ln_b", "gm_w_s",
           "gm_b_s", "gm_out_norm", "conv_w", "conv_b", "dt_bias", "a_log", "d_skip", "ssm_norm", "w_out", "ffn2_norm",
           "ffn2_w_gate", "ffn2_w_up", "ffn2_w_down", "ple_norm", "ple_w_gate", "ple_b_gate", "ple_w_proj", "final_norm")


def _step(x, p, target, w, m, v):
    local = lambda d: {name: d[name][0] for name in BIG}

    shards = {name: val.astype(BF16) for name, val in local(w).items()}
    conv_high = lax.reduce_precision(w["conv_w"][0], 8, 7)
    conv_mid = lax.reduce_precision(w["conv_w"][0] - conv_high, 8, 7)
    shards["conv_w"] = conv_high.astype(BF16)
    shards["conv_w_mid"] = conv_mid.astype(BF16)
    shards["conv_w_low"] = (w["conv_w"][0] - conv_high - conv_mid).astype(BF16)
    g_ffn1 = _comm_alone([_gather_comm(GATHER_FFN1.pack_local(shards))], "gather_ffn1")[0]

    row = lambda name: w[name].reshape(1, -1)
    gm_w_s = w["gm_w_s"][0]
    gm_b_st = jnp.transpose(w["gm_b_s"][0])
    ffn1 = (row("ffn1_norm"),) + tuple(GATHER_FFN1.pieces(g_ffn1, name) for name in GATHER_FFN1.names)
    gm = (row("gm_ln_g"), row("gm_ln_b"), gm_w_s, gm_b_st, row("gm_out_norm"))

    h1, n1, a1, b1, s1, g_mix = _ffn_fwd(x, *ffn1, "ffn1_fwd", comm=_gather_comm(GATHER_MIX.pack_local(shards)))
    w_in_t = GATHER_MIX.gathered_piece(g_mix, "w_in").reshape(IN_PROJ, D_MODEL)
    w_in_t = jnp.concatenate([w_in_t, jnp.zeros((IN_PROJ_PAD - IN_PROJ, D_MODEL), BF16)], axis=0)
    w_proj_t = GATHER_MIX.gathered_piece(g_mix, "ple_w_proj").reshape(D_MODEL, D_PLE)
    conv_w = sum(GATHER_MIX.gathered_piece(g_mix, name).astype(F32) for name in ("conv_w", "conv_w_mid", "conv_w_low"))
    conv_w = conv_w.reshape(CONV_DIM, SSM_CONV).T
    ssd = (row("dt_bias"), row("a_log"), row("d_skip"), row("ssm_norm"))
    w_out = GATHER_MIX.pieces(g_mix, "w_out")

    proj, n2, xc = _mix_in_fwd(h1, row("mix_norm"), w_in_t, conv_w, row("conv_b"))
    ya = _gm_fwd(proj, *gm)
    yb, s_all, g_ffn2 = _ssd_fwd(proj, xc, *ssd, comm=_gather_comm(GATHER_FFN2.pack_local(shards)))
    ffn2 = (row("ffn2_norm"),) + tuple(GATHER_FFN2.pieces(g_ffn2, name) for name in GATHER_FFN2.names)
    h3, n3, a3, b3, s3, h2 = _ffn_fwd(h1, *ffn2, "ffn2_fwd", mixed=(ya, yb, w_out))

    g, gp = {}, {}
    dh3, loss, gp["ple_w_gate"], d_w_proj, g["ple_norm"], g["ple_b_gate"], g["final_norm"] = _tail(
        h3, p, target, row("ple_norm"), GATHER_MIX.pieces(g_mix, "ple_w_gate"), row("ple_b_gate"), w_proj_t,
        row("final_norm"))
    gp["ple_w_proj"] = d_w_proj.T

    dh2, da3, db3, g["ffn2_norm"] = _ffn_dgrad(h2, dh3, a3, b3, *ffn2, "ffn2_dgrad")
    gp["ffn2_w_gate"] = _wgrad(n3, da3, 1408, "ffn2_wgrad_gate", transpose_out=True)
    gp["ffn2_w_up"] = _wgrad(n3, db3, 1408, "ffn2_wgrad_up", transpose_out=True)
    gp["ffn2_w_down"] = _wgrad(s3, dh3, 512, "ffn2_wgrad_down", scale=0.5, bk=1024)

    dya, dyb = _out_proj_dgrad(dh2, w_out)
    gp["w_out"] = jnp.concatenate([_wgrad(ya, dh2, 1024, "w_out_wgrad_a"), _wgrad(yb, dh2, 1024, "w_out_wgrad_b")], axis=0)

    dp_zxd, d_conv_w, g["conv_b"], g["dt_bias"], g["a_log"], g["d_skip"], g["ssm_norm"], parts_late = _ssd_bwd(
        proj, xc, dyb, s_all, conv_w, *ssd, comm=_exchange_comm(SCATTER_LATE.pack_owner_major(gp)))
    gp["conv_w"] = d_conv_w.T
    dp_uv, g["gm_ln_g"], g["gm_ln_b"], g["gm_w_s"], dbst, g["gm_out_norm"] = _gm_bwd(proj, dya, *gm)
    g["gm_b_s"] = jnp.transpose(dbst)

    parts = {}
    gp["w_in"] = jnp.concatenate([_wgrad(n2, dp_uv, 1024, "w_in_wgrad_uv", transpose_out=True),
                                  _wgrad(n2, dp_zxd, 896, "w_in_wgrad_zxd", transpose_out=True)], axis=0)[:IN_PROJ]
    dh1, g["mix_norm"], parts[SCATTER_IN] = _mix_in_dgrad(h1, dh2, dp_uv, dp_zxd, row("mix_norm"), w_in_t,
                                                          comm=_exchange_comm(SCATTER_IN.pack_owner_major(gp)))

    dx, da1, db1, g["ffn1_norm"] = _ffn_dgrad(x, dh1, a1, b1, *ffn1, "ffn1_dgrad")
    gp["ffn1_w_gate"], small_parts = _wgrad(n1, da1, 1408, "ffn1_wgrad_gate", transpose_out=True,
                                            comm=_gather_comm(_pack_small(g)))
    gp["ffn1_w_up"], parts[SCATTER_GATE] = _wgrad(n1, db1, 1408, "ffn1_wgrad_up", transpose_out=True,
                                                  comm=_exchange_comm(SCATTER_GATE.pack_owner_major(gp)))
    gp["ffn1_w_down"], parts[SCATTER_UP] = _wgrad(s1, dh1, 512, "ffn1_wgrad_down", scale=0.5, bk=1024,
                                                  comm=_exchange_comm(SCATTER_UP.pack_owner_major(gp)))
    parts[SCATTER_DOWN] = _comm_alone([_exchange_comm(SCATTER_DOWN.pack_owner_major(gp))], "scatter_ffn1_down")[0]
    parts[SCATTER_LATE] = parts_late

    res_big = {}
    for pack, pack_parts in parts.items():
        for name in pack.names:
            shape, transposed = SHARDS[name]
            if name in ("ple_w_proj", "conv_w"):
                nat = pack.gathered_piece(pack_parts, name).reshape((N_DEV,) + shape[::-1])
                res_big[name] = _sum_adamw(jnp.transpose(nat, (0, 2, 1)), w[name][0], m[name][0], v[name][0], shape[0],
                                           "adamw_" + name)
            elif name == "w_in":
                res_big[name] = _adamw_shard(pack_parts, pack.offsets[name], True, w[name], m[name], v[name],
                                             "adamw_" + name, n_tiles=4)
            else:
                flip = (lambda a: jnp.transpose(a, (0, 2, 1))) if transposed else (lambda a: a)
                res = _adamw_shard(pack_parts, pack.offsets[name], False, flip(w[name]), flip(m[name]), flip(v[name]),
                                   "adamw_" + name, n_tiles=2)
                res_big[name] = [flip(r) for r in res]

    small_shapes = {name: w[name].shape for name in SMALL}
    res_small = _sum_adamw(small_parts, _pack_small(w), _pack_small(m), _pack_small(v), SMALL_ROWS, "adamw_small")
    res_small = [_unpack_small(r, small_shapes) for r in res_small]

    outs = []
    for k in range(4):
        for name in WEIGHTS:
            if name in res_small[k]:
                outs.append(res_small[k][name])
            else:
                outs.append(res_big[name][k].reshape(w[name].shape))
    return loss[0, 0], dx, outs


def kernel(x, p, ffn1_norm, ffn1_w_gate, ffn1_w_up, ffn1_w_down, mix_norm, w_in, gm_ln_g, gm_ln_b, gm_w_s, gm_b_s, gm_out_norm, conv_w, conv_b, dt_bias, a_log, d_skip, ssm_norm, w_out, ffn2_norm, ffn2_w_gate, ffn2_w_up, ffn2_w_down, ple_norm, ple_w_gate, ple_b_gate, ple_w_proj, final_norm, loss_target, m_ffn1_norm, m_ffn1_w_gate, m_ffn1_w_up, m_ffn1_w_down, m_mix_norm, m_w_in, m_gm_ln_g, m_gm_ln_b, m_gm_w_s, m_gm_b_s, m_gm_out_norm, m_conv_w, m_conv_b, m_dt_bias, m_a_log, m_d_skip, m_ssm_norm, m_w_out, m_ffn2_norm, m_ffn2_w_gate, m_ffn2_w_up, m_ffn2_w_down, m_ple_norm, m_ple_w_gate, m_ple_b_gate, m_ple_w_proj, m_final_norm, v_ffn1_norm, v_ffn1_w_gate, v_ffn1_w_up, v_ffn1_w_down, v_mix_norm, v_w_in, v_gm_ln_g, v_gm_ln_b, v_gm_w_s, v_gm_b_s, v_gm_out_norm, v_conv_w, v_conv_b, v_dt_bias, v_a_log, v_d_skip, v_ssm_norm, v_w_out, v_ffn2_norm, v_ffn2_w_gate, v_ffn2_w_up, v_ffn2_w_down, v_ple_norm, v_ple_w_gate, v_ple_b_gate, v_ple_w_proj, v_final_norm):
    args = locals()
    w = {name: args[name] for name in WEIGHTS}
    m = {name: args["m_" + name] for name in WEIGHTS}
    v = {name: args["v_" + name] for name in WEIGHTS}
    loss, dx, outs = _step(x[0], p[0, 0], loss_target[0], w, m, v)
    loss = lax.psum(loss, AXES)
    return (loss, dx[None], *outs)
```

```python
import functools
from typing import NamedTuple

import jax
import jax.numpy as jnp
from jax import lax
from jax.experimental import pallas as pl
from jax.experimental.pallas import tpu as pltpu

F32 = jnp.float32
BF16 = jnp.bfloat16
HIGHEST = lax.Precision.HIGHEST
MESH = pl.DeviceIdType.MESH
AXES = ("x", "y", "c")
N_DEV = 8

D_MODEL = 1024
D_FF = 2816
D_PLE = 256
GM_WIDTH = 1024
GM_HEADS = 8
GM_HEAD_DIM = 128
CHUNK = 128
SSM_WIDTH = 1024
SSM_HEADS = 16
SSM_HEAD_DIM = 64
SSM_GROUPS = 2
SSM_STATE = 128
SSM_CONV = 4
CONV_DIM = SSM_WIDTH + 2 * SSM_GROUPS * SSM_STATE
IN_PROJ = 2 * GM_WIDTH + SSM_WIDTH + CONV_DIM + SSM_HEADS
LANES = 128
BF16_ROWS = 16
F32_ROWS = 8
IN_PROJ_PAD = IN_PROJ - SSM_HEADS + LANES
UV_W = 2 * GM_WIDTH
ZXD_W = IN_PROJ_PAD - UV_W
HALO = 8
EPS = 1e-6

ADAM_LR = 0.001
ADAM_B1 = 0.9
ADAM_B2 = 0.999
ADAM_EPS = 1e-08
ADAM_WD = 0.01
ADAM_STEP = 10

VMEM_LIMIT = 56 * 1024 * 1024
PACK_COLS = 1024


def _rms(x, g):
    return x * lax.rsqrt(jnp.mean(x * x, axis=-1, keepdims=True) + EPS) * g


def _gelu(x):
    return 0.5 * x * (1.0 + lax.erf(x * (2.0 ** -0.5)))


def _silu(x):
    return x * jax.nn.sigmoid(x)


def _dot(a, b):
    return jnp.dot(a.astype(BF16), b.astype(BF16), preferred_element_type=F32)


def _dot_nt(a, b):
    return lax.dot_general(a.astype(BF16), b.astype(BF16), (((1,), (1,)), ((), ())), preferred_element_type=F32)


def _dot_tn(a, b):
    return lax.dot_general(a.astype(BF16), b.astype(BF16), (((0,), (0,)), ((), ())), preferred_element_type=F32)


def _hdot_tn(a, b):
    return lax.dot_general(a, b, (((0,), (0,)), ((), ())), precision=HIGHEST, preferred_element_type=F32)


def _split3(x):
    hi = x.astype(BF16)
    rest = x - hi.astype(F32)
    mid = rest.astype(BF16)
    return hi, mid, (rest - mid.astype(F32)).astype(BF16)


def _exact_dot(x, mask, dims, x_first=True):
    terms = [lax.dot_general(*((t, mask) if x_first else (mask, t)), (dims, ((), ())), preferred_element_type=F32)
             for t in _split3(x)]
    return (terms[0] + terms[1]) + terms[2]


def _mask_product(fwd_dims, fwd_x_first, bwd_dims, bwd_x_first):
    @jax.custom_vjp
    def product(x, mask):
        return _exact_dot(x, mask, fwd_dims, fwd_x_first)

    def fwd(x, mask):
        return product(x, mask), mask

    def bwd(mask, g):
        return _exact_dot(g, mask, bwd_dims, bwd_x_first), jnp.zeros_like(mask)

    product.defvjp(fwd, bwd)
    return product


_widen = _mask_product(((1,), (0,)), True, ((1,), (1,)), True)
_cumsum_rows = _mask_product(((1,), (0,)), False, ((0,), (0,)), False)
_cumsum_cols = _mask_product(((0,), (0,)), True, ((1,), (1,)), False)


class _Pieces(NamedTuple):
    gathered: jax.Array
    row_off: int
    rows: int


class _Comm(NamedTuple):
    phases: object
    src: jax.Array
    dst: jax.ShapeDtypeStruct


def _tiled(body, name, n_steps, tiled_in, full_in, big_in, tiled_out, acc_out, scratch=(), reverse=False, comm=None):
    n_t, n_f, n_b, n_to, n_a = len(tiled_in), len(full_in), len(big_in), len(tiled_out), len(acc_out)
    n_c = 1 if comm else 0

    def row(i):
        return n_steps - 1 - i if reverse else i

    in_specs, args = [], []
    for arr, br, bc, cb in tiled_in:
        if callable(cb):
            in_specs.append(pl.BlockSpec((br, bc), cb))
        else:
            in_specs.append(pl.BlockSpec((br, bc), functools.partial(lambda i, cb: (row(i), cb), cb=cb)))
        args.append(arr)
    for arr in full_in:
        in_specs.append(pl.BlockSpec(arr.shape, functools.partial(lambda i, nd: (0,) * nd, nd=arr.ndim)))
        args.append(arr)
    big_shapes, n_copies = [], 0
    for big in big_in:
        in_specs.append(pl.BlockSpec(memory_space=pl.ANY))
        if isinstance(big, _Pieces):
            args.append(big.gathered)
            big_shapes.append(((N_DEV * big.rows, PACK_COLS), big.gathered.dtype))
            n_copies += N_DEV
        else:
            args.append(big)
            big_shapes.append((big.shape, big.dtype))
            n_copies += 1
    if comm:
        in_specs.append(pl.BlockSpec(memory_space=pl.ANY))
        args.append(comm.src)
    out_specs, out_shape = [], []
    for rows, cols, dt, br in tiled_out:
        out_specs.append(pl.BlockSpec((br, cols), lambda i: (row(i), 0)))
        out_shape.append(jax.ShapeDtypeStruct((rows, cols), dt))
    for shp, dt in acc_out:
        out_specs.append(pl.BlockSpec(shp, functools.partial(lambda i, nd: (0,) * nd, nd=len(shp))))
        out_shape.append(jax.ShapeDtypeStruct(shp, dt))
    if comm:
        out_specs.append(pl.BlockSpec(memory_space=pl.ANY))
        out_shape.append(comm.dst)
    scratch_shapes = [pltpu.VMEM(shp, dt) for shp, dt in big_shapes] + list(scratch)
    if n_copies:
        scratch_shapes.append(pltpu.SemaphoreType.DMA((n_copies,)))
    if comm:
        scratch_shapes += [pltpu.SemaphoreType.DMA((N_DEV - 1,)), pltpu.SemaphoreType.DMA((N_DEV - 1,)), pltpu.SemaphoreType.DMA]

    def kern(*refs):
        n_in = n_t + n_f + n_b + n_c
        ins = refs[: n_t + n_f]
        big_hbm = refs[n_t + n_f : n_t + n_f + n_b]
        outs = refs[n_in : n_in + n_to + n_a]
        rest = refs[n_in + n_to + n_a + n_c :]
        big_vmem, scr = rest[:n_b], rest[n_b:]
        if comm:
            scr, comm_sems = scr[:-3], scr[-3:]
            comm_start, comm_mid, comm_finish = comm.phases(refs[n_in - 1], refs[n_in + n_to + n_a], *comm_sems)
        if n_copies:
            scr, copy_sems = scr[:-1], scr[-1]
        step = pl.program_id(0)

        @pl.when(step == 0)
        def _():
            copies = []
            for big, src, dst in zip(big_in, big_hbm, big_vmem):
                if isinstance(big, _Pieces):
                    for j in range(N_DEV):
                        copies.append((src.at[j, pl.ds(big.row_off, big.rows), :], dst.at[pl.ds(j * big.rows, big.rows), :]))
                else:
                    copies.append((src, dst))
            copies = [pltpu.make_async_copy(a, b, copy_sems.at[k]) for k, (a, b) in enumerate(copies)]
            for cp in copies:
                cp.start()
            for cp in copies:
                cp.wait()
            for acc in outs[n_to:]:
                acc[...] = jnp.zeros(acc.shape, acc.dtype)
            if comm:
                comm_start()

        body(row(step), *ins, *big_vmem, *outs, *scr)
        if comm:
            pl.when(step == (n_steps - 1) // 2)(comm_mid)
            pl.when(step == n_steps - 1)(comm_finish)

    res = pl.pallas_call(
        kern,
        out_shape=out_shape,
        grid=(n_steps,),
        in_specs=in_specs,
        out_specs=out_specs,
        scratch_shapes=scratch_shapes,
        name=name,
        compiler_params=pltpu.CompilerParams(dimension_semantics=("arbitrary",), vmem_limit_bytes=VMEM_LIMIT),
    )(*args)
    return res


FF_CHUNKS = ((0, 1536), (1536, D_FF))
FFN_TM = 256


def _ffn_fwd(h, g, wg_t, wu_t, wd, name, comm=None, mixed=None):
    T = h.shape[0]
    n_mix = 2 if mixed else 0

    def body(i, h_ref, *refs):
        ya_ref, yb_ref = refs[:n_mix] if mixed else (None, None)
        g_ref, wg_ref, wu_ref, wd_ref = refs[n_mix:n_mix + 4]
        o_ref, n_ref, a_ref, b_ref, s_ref = refs[n_mix + 4 + n_mix // 2:n_mix + 9 + n_mix // 2]
        x = h_ref[...]
        if mixed:
            wo_ref, x_ref = refs[n_mix + 4], refs[-1]
            x = (x + jnp.dot(ya_ref[...], wo_ref[:GM_WIDTH, :], preferred_element_type=F32)
                 + jnp.dot(yb_ref[...], wo_ref[GM_WIDTH:, :], preferred_element_type=F32))
            x_ref[...] = x
        n = _rms(x, g_ref[...]).astype(BF16)
        n_ref[...] = n
        f = jnp.zeros(x.shape, F32)
        for lo, hi in FF_CHUNKS:
            a = _dot_nt(n, wg_ref[lo:hi, :])
            b = _dot_nt(n, wu_ref[lo:hi, :])
            s = (_silu(a) * b).astype(BF16)
            a_ref[:, lo:hi] = a.astype(BF16)
            b_ref[:, lo:hi] = b.astype(BF16)
            s_ref[:, lo:hi] = s
            f = f + jnp.dot(s, wd_ref[lo:hi, :], preferred_element_type=F32)
        o_ref[...] = x + 0.5 * f

    tiled_in, big_in = [(h, FFN_TM, D_MODEL, 0)], [wg_t, wu_t, wd]
    tiled_out = [(T, D_MODEL, F32, FFN_TM), (T, D_MODEL, BF16, FFN_TM), (T, D_FF, BF16, FFN_TM), (T, D_FF, BF16, FFN_TM),
                 (T, D_FF, BF16, FFN_TM)]
    if mixed:
        tiled_in += [(mixed[0], FFN_TM, GM_WIDTH, 0), (mixed[1], FFN_TM, SSM_WIDTH, 0)]
        big_in.append(mixed[2])
        tiled_out.append((T, D_MODEL, F32, FFN_TM))
    return _tiled(body, name, T // FFN_TM, tiled_in, [g], big_in, tiled_out, [], comm=comm)


def _ffn_dgrad(h, dout, a16, b16, g, wg_t, wu_t, wd, name):
    T = h.shape[0]

    def body(i, h_ref, do_ref, a_ref, b_ref, g_ref, wg_ref, wu_ref, wd_ref, dh_ref, da_ref, db_ref, dg_ref):
        dout = do_ref[...]
        _, rms_vjp = jax.vjp(_rms, h_ref[...], g_ref[...])
        dfo = (0.5 * dout).astype(BF16)
        dn = jnp.zeros(dout.shape, F32)
        for lo, hi in FF_CHUNKS:
            a = a_ref[:, lo:hi].astype(F32)
            b = b_ref[:, lo:hi].astype(F32)
            sg = jax.nn.sigmoid(a)
            ds = _dot_nt(dfo, wd_ref[lo:hi, :])
            db = (ds * (a * sg)).astype(BF16)
            da = (ds * b * (sg * (1.0 + a * (1.0 - sg)))).astype(BF16)
            dn = dn + _dot(da, wg_ref[lo:hi, :]) + _dot(db, wu_ref[lo:hi, :])
            da_ref[:, lo:hi] = da
            db_ref[:, lo:hi] = db
        dx, dg = rms_vjp(dn)
        dh_ref[...] = dout + dx
        dg_ref[...] += dg

    return _tiled(body, name, T // FFN_TM,
                  [(h, FFN_TM, D_MODEL, 0), (dout, FFN_TM, D_MODEL, 0), (a16, FFN_TM, D_FF, 0), (b16, FFN_TM, D_FF, 0)],
                  [g], [wg_t, wu_t, wd],
                  [(T, D_MODEL, F32, FFN_TM), (T, D_FF, BF16, FFN_TM), (T, D_FF, BF16, FFN_TM)], [((1, D_MODEL), F32)])


def _wgrad(a, b, bn, name, scale=None, transpose_out=False, bk=2048, comm=None):
    T, M = a.shape
    N = b.shape[1]
    bk = min(bk, T)
    assert M % LANES == 0 and N % bn == 0 and T % bk == 0
    n_j, n_k = N // bn, T // bk
    n_c = 1 if comm else 0

    def kern(*refs):
        a_ref, b_ref, o_ref, acc_ref = refs[0], refs[1], refs[2 + n_c], refs[3 + 2 * n_c]
        j, k = pl.program_id(0), pl.program_id(1)
        if comm:
            comm_start, comm_mid, comm_finish = comm.phases(refs[2], refs[4], *refs[6:])
            pl.when((j == 0) & (k == 0))(comm_start)

        @pl.when(k == 0)
        def _():
            acc_ref[...] = jnp.zeros(acc_ref.shape, F32)

        bv = b_ref[...]
        if scale is not None:
            bv = bv * scale
        acc_ref[...] += _dot_tn(a_ref[...], bv)

        @pl.when(k == n_k - 1)
        def _():
            acc = acc_ref[...]
            o_ref[...] = (acc.T if transpose_out else acc).astype(BF16)

        if comm:
            pl.when((j == (n_j - 1) // 2) & (k == n_k - 1))(comm_mid)
            pl.when((j == n_j - 1) & (k == n_k - 1))(comm_finish)

    if transpose_out:
        out_shape, out_spec = (N, M), pl.BlockSpec((bn, M), lambda j, k: (j, 0))
    else:
        out_shape, out_spec = (M, N), pl.BlockSpec((M, bn), lambda j, k: (0, j))
    any_spec = pl.BlockSpec(memory_space=pl.ANY)
    comm_sems = [pltpu.SemaphoreType.DMA((N_DEV - 1,)), pltpu.SemaphoreType.DMA((N_DEV - 1,)), pltpu.SemaphoreType.DMA]
    res = pl.pallas_call(
        kern,
        out_shape=[jax.ShapeDtypeStruct(out_shape, BF16)] + ([comm.dst] if comm else []),
        grid=(n_j, n_k),
        in_specs=[pl.BlockSpec((bk, M), lambda j, k: (k, 0)), pl.BlockSpec((bk, bn), lambda j, k: (k, j))] + [any_spec] * n_c,
        out_specs=[out_spec] + [any_spec] * n_c,
        scratch_shapes=[pltpu.VMEM((M, bn), F32)] + (comm_sems if comm else []),
        name=name,
        compiler_params=pltpu.CompilerParams(dimension_semantics=("arbitrary", "arbitrary"), vmem_limit_bytes=VMEM_LIMIT),
    )(a, b, *([comm.src] if comm else []))
    return res if comm else res[0]


PROJ_TM = 256
Z_BLK = 2 * GM_WIDTH // SSM_WIDTH
XBC_BLK = (2 * GM_WIDTH + SSM_WIDTH) // CONV_DIM
DT_BLK = (2 * GM_WIDTH + SSM_WIDTH + CONV_DIM) // LANES


def _mix_in_fwd(h, g, w_in_t, conv_w, conv_b):
    T = h.shape[0]
    x0 = XBC_BLK * CONV_DIM

    def body(i, h_ref, g_ref, cw_ref, cb_ref, w_ref, p_ref, n_ref, xc_ref, ext_ref):
        @pl.when(i == 0)
        def _():
            ext_ref[0:HALO, :] = jnp.zeros((HALO, CONV_DIM), F32)

        n = _rms(h_ref[...], g_ref[...]).astype(BF16)
        n_ref[...] = n
        xbc = _dot_nt(n, w_ref[x0:x0 + CONV_DIM, :])
        p_ref[:, x0:x0 + CONV_DIM] = xbc
        ext_ref[HALO:, :] = xbc
        xc_ref[...] = _conv_taps(ext_ref, cw_ref[...], cb_ref[...], PROJ_TM)
        ext_ref[0:HALO, :] = ext_ref[PROJ_TM:PROJ_TM + HALO, :]
        p_ref[:, :x0] = _dot_nt(n, w_ref[:x0, :])
        p_ref[:, x0 + CONV_DIM:] = _dot_nt(n, w_ref[x0 + CONV_DIM:, :])

    return _tiled(body, "mix_in_fwd", T // PROJ_TM, [(h, PROJ_TM, D_MODEL, 0)], [g, conv_w, conv_b], [w_in_t],
                  [(T, IN_PROJ_PAD, F32, PROJ_TM), (T, D_MODEL, BF16, PROJ_TM), (T, CONV_DIM, F32, PROJ_TM)], [],
                  scratch=[pltpu.VMEM((HALO + PROJ_TM, CONV_DIM), F32)])


def _mix_in_dgrad(h, dh_in, dp_uv, dp_zxd, g, w_in_t, comm=None):
    T = h.shape[0]

    def body(i, h_ref, dh_ref, duv_ref, dzxd_ref, g_ref, w_ref, o_ref, dg_ref):
        dn = _dot(duv_ref[...], w_ref[:UV_W, :]) + _dot(dzxd_ref[...], w_ref[UV_W:, :])
        _, rms_vjp = jax.vjp(_rms, h_ref[...], g_ref[...])
        dx, dg = rms_vjp(dn)
        o_ref[...] = dh_ref[...] + dx
        dg_ref[...] += dg

    return _tiled(body, "mix_in_dgrad", T // PROJ_TM,
                  [(h, PROJ_TM, D_MODEL, 0), (dh_in, PROJ_TM, D_MODEL, 0), (dp_uv, PROJ_TM, UV_W, 0),
                   (dp_zxd, PROJ_TM, ZXD_W, 0)], [g], [w_in_t],
                  [(T, D_MODEL, F32, PROJ_TM)], [((1, D_MODEL), F32)], comm=comm)


def _out_proj_dgrad(dh, w_out):
    T = dh.shape[0]

    def body(i, dh_ref, w_ref, dya_ref, dyb_ref):
        d = dh_ref[...].astype(BF16)
        dya_ref[...] = _dot_nt(d, w_ref[:GM_WIDTH, :])
        dyb_ref[...] = _dot_nt(d, w_ref[GM_WIDTH:, :])

    return _tiled(body, "out_proj_dgrad", T // PROJ_TM, [(dh, PROJ_TM, D_MODEL, 0)], [], [w_out],
                  [(T, GM_WIDTH, F32, PROJ_TM), (T, SSM_WIDTH, F32, PROJ_TM)], [])


def _gm_chunk(u, v, ln_g, ln_b, b_st, out_g, *w_heads):
    ug = _gelu(u)
    vg = _gelu(v)
    mu = jnp.mean(vg, axis=-1, keepdims=True)
    xc = vg - mu
    vn = xc * lax.rsqrt(jnp.mean(xc * xc, axis=-1, keepdims=True) + EPS) * ln_g + ln_b
    t_idx = lax.broadcasted_iota(jnp.int32, (CHUNK, CHUNK), 0)
    s_idx = lax.broadcasted_iota(jnp.int32, (CHUNK, CHUNK), 1)
    causal = t_idx >= s_idx
    mixed = []
    for hd in range(GM_HEADS):
        wm = jnp.where(causal, w_heads[hd], 0.0)
        cols = slice(hd * GM_HEAD_DIM, (hd + 1) * GM_HEAD_DIM)
        mixed.append(_dot(wm, vn[:, cols]) + b_st[:, hd:hd + 1])
    ya0 = ug * jnp.concatenate(mixed, axis=1)
    return _rms(ya0, out_g)


GM_FWD_CHUNKS = 2


def _gm_fwd(proj, ln_g, ln_b, w_s, b_st, out_g):
    T = proj.shape[0]

    rows = GM_FWD_CHUNKS * CHUNK

    def body(i, u_ref, v_ref, lg_ref, lb_ref, w_ref, bs_ref, og_ref, ya_ref):
        w_heads = [w_ref[hd] for hd in range(GM_HEADS)]
        for c in range(GM_FWD_CHUNKS):
            tok = pl.ds(c * CHUNK, CHUNK)
            ya = _gm_chunk(u_ref[tok, :], v_ref[tok, :], lg_ref[...], lb_ref[...], bs_ref[...], og_ref[...], *w_heads)
            ya_ref[tok, :] = ya.astype(BF16)

    return _tiled(body, "gmlp_fwd", T // rows, [(proj, rows, GM_WIDTH, 0), (proj, rows, GM_WIDTH, 1)],
                  [ln_g, ln_b, w_s, b_st, out_g], [], [(T, GM_WIDTH, BF16, rows)], [])[0]


def _gm_bwd(proj, dya, ln_g, ln_b, w_s, b_st, out_g):
    T = proj.shape[0]

    def body(i, u_ref, v_ref, dy_ref, lg_ref, lb_ref, w_ref, bs_ref, og_ref, duv_ref, dlg_ref, dlb_ref, dw_ref, dbs_ref,
             dog_ref):
        w_heads = [w_ref[hd] for hd in range(GM_HEADS)]
        _, vjp = jax.vjp(_gm_chunk, u_ref[...], v_ref[...], lg_ref[...], lb_ref[...], bs_ref[...], og_ref[...], *w_heads)
        grads = vjp(dy_ref[...])
        duv_ref[:, :GM_WIDTH] = grads[0].astype(BF16)
        duv_ref[:, GM_WIDTH:] = grads[1].astype(BF16)
        dlg_ref[...] += grads[2]
        dlb_ref[...] += grads[3]
        dbs_ref[...] += grads[4]
        dog_ref[...] += grads[5]
        for hd in range(GM_HEADS):
            dw_ref[hd] += grads[6 + hd]

    return _tiled(body, "gmlp_bwd", T // CHUNK,
                  [(proj, CHUNK, GM_WIDTH, 0), (proj, CHUNK, GM_WIDTH, 1), (dya, CHUNK, GM_WIDTH, 0)],
                  [ln_g, ln_b, w_s, b_st, out_g], [], [(T, UV_W, BF16, CHUNK)],
                  [((1, GM_WIDTH), F32), ((1, GM_WIDTH), F32), ((GM_HEADS, CHUNK, CHUNK), F32),
                   ((CHUNK, GM_HEADS), F32), ((1, GM_WIDTH), F32)])


def _ssd_chunk(xc, z, dtr, s_in, dt_bias, a_log, d_skip, norm_g):
    half = SSM_WIDTH // SSM_GROUPS
    l_idx = lax.broadcasted_iota(jnp.int32, (CHUNK, CHUNK), 0)
    s_idx = lax.broadcasted_iota(jnp.int32, (CHUNK, CHUNK), 1)
    causal = l_idx >= s_idx
    head_of_col = lax.broadcasted_iota(jnp.int32, (SSM_HEADS, SSM_WIDTH), 1) // SSM_HEAD_DIM
    expand = (head_of_col == lax.broadcasted_iota(jnp.int32, (SSM_HEADS, SSM_WIDTH), 0)).astype(BF16)

    xcs = _silu(xc)
    xs = xcs[:, :SSM_WIDTH]
    dt = jax.nn.softplus(dtr + dt_bias)
    adt = dt * (-jnp.exp(a_log))
    acs = _cumsum_rows(adt, causal.astype(BF16))
    acs_t = _cumsum_cols(adt, (l_idx <= s_idx).astype(BF16))
    tot = acs[CHUNK - 1:CHUNK, :]
    dt_w = _widen(dt, expand)
    out_decay_w = _widen(jnp.exp(acs), expand)
    state_decay_w = _widen(jnp.exp(tot - acs), expand)
    chunk_decay_w = _widen(jnp.exp(tot), expand)
    d_skip_w = _widen(d_skip, expand)
    xdt = xs * dt_w
    xdt_decayed = xdt * state_decay_w

    y_diag, y_off, states = [], [], []
    for grp in range(SSM_GROUPS):
        b0 = SSM_WIDTH + grp * SSM_STATE
        c0 = SSM_WIDTH + SSM_GROUPS * SSM_STATE + grp * SSM_STATE
        bm = xcs[:, b0:b0 + SSM_STATE].astype(BF16)
        cm = xcs[:, c0:c0 + SSM_STATE].astype(BF16)
        cb = _dot_nt(cm, bm)
        for k in range(grp * SSM_HEADS // SSM_GROUPS, (grp + 1) * SSM_HEADS // SSM_GROUPS):
            decay = jnp.exp(jnp.where(causal, acs[:, k:k + 1] - acs_t[k:k + 1, :], -jnp.inf))
            y_diag.append(_dot(cb * decay, xdt[:, k * SSM_HEAD_DIM:(k + 1) * SSM_HEAD_DIM]))
        cols = slice(grp * half, (grp + 1) * half)
        states.append(_dot_tn(bm, xdt_decayed[:, cols]))
        y_off.append(_dot(cm, s_in[:, cols]))
    y = jnp.concatenate(y_diag, axis=1) + jnp.concatenate(y_off, axis=1) * out_decay_w + xs * d_skip_w
    s_out = s_in * chunk_decay_w + jnp.concatenate(states, axis=1)
    y = y * _silu(z)
    normed = []
    for grp in range(SSM_GROUPS):
        yg = y[:, grp * half:(grp + 1) * half]
        normed.append(yg * lax.rsqrt(jnp.mean(yg * yg, axis=-1, keepdims=True) + EPS))
    return jnp.concatenate(normed, axis=1) * norm_g, s_out


def _sum_row_tiles(x):
    return x.reshape(x.shape[0] // F32_ROWS, F32_ROWS, x.shape[1]).sum(axis=0)


def _conv_taps(ext_ref, w, b, rows):
    y = b
    for k in range(SSM_CONV):
        y = y + w[k:k + 1, :] * ext_ref[pl.ds(HALO - (SSM_CONV - 1) + k, rows), :]
    return y


def _ssd_fwd(proj, xc, dt_bias, a_log, d_skip, norm_g, comm=None):
    T = proj.shape[0]
    n_chunks = T // CHUNK

    def body(i, z_ref, xc_ref, dt_ref, dtb_ref, al_ref, dsk_ref, ng_ref, yb_ref, sin_ref, st_ref):
        @pl.when(i == 0)
        def _():
            st_ref[...] = jnp.zeros(st_ref.shape, F32)

        s_in = st_ref[...]
        yb, s_out = _ssd_chunk(xc_ref[...], z_ref[...], dt_ref[:, 0:SSM_HEADS], s_in, dtb_ref[...], al_ref[...],
                               dsk_ref[...], ng_ref[...])
        yb_ref[...] = yb.astype(BF16)
        sin_ref[...] = s_in
        st_ref[...] = s_out

    return _tiled(body, "ssd_fwd", n_chunks,
                  [(proj, CHUNK, SSM_WIDTH, Z_BLK), (xc, CHUNK, CONV_DIM, 0), (proj, CHUNK, LANES, DT_BLK)],
                  [dt_bias, a_log, d_skip, norm_g], [],
                  [(T, SSM_WIDTH, BF16, CHUNK), (n_chunks * SSM_STATE, SSM_WIDTH, F32, SSM_STATE)], [],
                  scratch=[pltpu.VMEM((SSM_STATE, SSM_WIDTH), F32)], comm=comm)


def _ssd_bwd(proj, xc, dyb, s_all, conv_w, dt_bias, a_log, d_skip, norm_g, comm=None):
    T = proj.shape[0]
    n_chunks = T // CHUNK

    def body(i, z_ref, x_ref, xc_ref, dt_ref, dy_ref, sin_ref, cw_ref, dtb_ref, al_ref, dsk_ref, ng_ref,
             dzxd_ref, dcw_ref, dcb_ref, ddtb_ref, dal_ref, ddsk_ref, dng_ref, dext_ref, dst_ref, cw_acc, cb_acc):
        @pl.when(i == n_chunks - 1)
        def _():
            dext_ref[CHUNK:, :] = jnp.zeros((HALO, CONV_DIM), F32)
            dst_ref[...] = jnp.zeros(dst_ref.shape, F32)
            cw_acc[...] = jnp.zeros(cw_acc.shape, F32)
            cb_acc[...] = jnp.zeros(cb_acc.shape, F32)

        _, vjp = jax.vjp(_ssd_chunk, xc_ref[...], z_ref[...], dt_ref[:, 0:SSM_HEADS], sin_ref[...], dtb_ref[...], al_ref[...],
                         dsk_ref[...], ng_ref[...])
        dxc, dz, ddtr, ds_in, ddtb, dal, ddsk, dng = vjp((dy_ref[...], dst_ref[...]))
        dst_ref[...] = ds_in
        ddtb_ref[...] += ddtb
        dal_ref[...] += dal
        ddsk_ref[...] += ddsk
        dng_ref[...] += dng
        dext_ref[0:CHUNK, :] = dxc
        cw = cw_ref[...]
        x = x_ref[...]
        dx = jnp.zeros((CHUNK, CONV_DIM), F32)
        for k in range(SSM_CONV):
            shifted = dext_ref[pl.ds(SSM_CONV - 1 - k, CHUNK), :]
            dx = dx + cw[k:k + 1, :] * shifted
            cw_acc[k] += _sum_row_tiles(shifted * x)
        cb_acc[...] += _sum_row_tiles(dxc)

        @pl.when(i == 0)
        def _():
            dcw_ref[...] = jnp.sum(cw_acc[...], axis=1)
            dcb_ref[...] = jnp.sum(cb_acc[...], axis=0, keepdims=True)

        dext_ref[CHUNK:, :] = dext_ref[0:HALO, :]
        dzxd_ref[:, 0:SSM_WIDTH] = dz.astype(BF16)
        dzxd_ref[:, SSM_WIDTH:SSM_WIDTH + CONV_DIM] = dx.astype(BF16)
        dzxd_ref[:, SSM_WIDTH + CONV_DIM:] = jnp.concatenate(
            [ddtr, jnp.zeros((CHUNK, LANES - SSM_HEADS), F32)], axis=1).astype(BF16)

    return _tiled(body, "ssd_bwd", n_chunks,
                  [(proj, CHUNK, SSM_WIDTH, Z_BLK), (proj, CHUNK, CONV_DIM, XBC_BLK), (xc, CHUNK, CONV_DIM, 0),
                   (proj, CHUNK, LANES, DT_BLK), (dyb, CHUNK, SSM_WIDTH, 0), (s_all, SSM_STATE, SSM_WIDTH, 0)],
                  [conv_w, dt_bias, a_log, d_skip, norm_g], [],
                  [(T, ZXD_W, BF16, CHUNK)],
                  [((SSM_CONV, CONV_DIM), F32), ((1, CONV_DIM), F32), ((1, SSM_HEADS), F32), ((1, SSM_HEADS), F32),
                   ((1, SSM_HEADS), F32), ((1, SSM_WIDTH), F32)],
                  scratch=[pltpu.VMEM((CHUNK + HALO, CONV_DIM), F32), pltpu.VMEM((SSM_STATE, SSM_WIDTH), F32),
                           pltpu.VMEM((SSM_CONV, F32_ROWS, CONV_DIM), F32), pltpu.VMEM((F32_ROWS, CONV_DIM), F32)],
                  reverse=True, comm=comm)


TAIL_TM = 512


def _tail(h, p, target, ple_norm, w_gate, b_gate, w_proj_t, final_norm):
    T = h.shape[0]

    def head(x, pre, pp, b_g, f_norm, tgt):
        gate = jax.nn.sigmoid(pre + b_g)
        out = _rms(x + gate * pp, f_norm)
        err = out - tgt
        return 0.5 * jnp.sum(jnp.mean(err * err, axis=-1, keepdims=True), axis=0, keepdims=True)

    def body(i, h_ref, p_ref, t_ref, pn_ref, bg_ref, fn_ref, wg_ref, wp_ref, dh_ref, loss_ref, dwg_ref, dwp_ref, dpn_ref,
             dbg_ref, dfn_ref):
        x = h_ref[...]
        n4f, n_vjp = jax.vjp(_rms, x, pn_ref[...])
        n4 = n4f.astype(BF16)
        pre = jnp.dot(n4, wg_ref[...], preferred_element_type=F32)
        p16 = p_ref[...].astype(BF16)
        pp = _dot_nt(p16, wp_ref[...])
        loss, h_vjp = jax.vjp(functools.partial(head, tgt=t_ref[...]), x, pre, pp, bg_ref[...], fn_ref[...])
        dx, dpre, dpp, dbg, dfn = h_vjp(jnp.ones((1, 1), F32))
        dpre16 = dpre.astype(BF16)
        dn4 = _dot_nt(dpre16, wg_ref[...])
        dx2, dpn = n_vjp(dn4)
        dh_ref[...] = dx + dx2
        loss_ref[...] += loss
        dwg_ref[...] += _dot_tn(n4, dpre16)
        dwp_ref[...] += _dot_tn(p16, dpp)
        dpn_ref[...] += dpn
        dbg_ref[...] += dbg
        dfn_ref[...] += dfn

    return _tiled(body, "tail", T // TAIL_TM,
                  [(h, TAIL_TM, D_MODEL, 0), (p, TAIL_TM, D_PLE, 0), (target, TAIL_TM, D_MODEL, 0)],
                  [ple_norm, b_gate, final_norm], [w_gate, w_proj_t],
                  [(T, D_MODEL, F32, TAIL_TM)],
                  [((1, 1), F32), ((D_MODEL, D_MODEL), F32), ((D_PLE, D_MODEL), F32), ((1, D_MODEL), F32),
                   ((1, D_MODEL), F32), ((1, D_MODEL), F32)])


def _gather_phases(x_ref, out_ref, send_sems, recv_sems, local_sem):
    mx, my, mc = lax.axis_index("x"), lax.axis_index("y"), lax.axis_index("c")
    me, sibling = (mx, my, mc), (mx, my, 1 - mc)
    chips = [(1 - mx, my), (mx, 1 - my), (1 - mx, 1 - my)]

    def rows(px, py, pc):
        return out_ref.at[4 * px + 2 * py + pc]

    def copy(k, block, to, src=None):
        return pltpu.make_async_remote_copy(
            src_ref=rows(*block) if src is None else src, dst_ref=rows(*block),
            send_sem=send_sems.at[k], recv_sem=recv_sems.at[k], device_id=to, device_id_type=MESH)

    mine = pltpu.make_async_copy(x_ref, rows(*me), local_sem)
    first = [copy(0, me, sibling, src=x_ref)] + [copy(1 + j, me, (*chip, mc), src=x_ref) for j, chip in enumerate(chips)]
    passed = [copy(4 + j, (*chip, mc), sibling) for j, chip in enumerate(chips)]

    def start():
        mine.start()
        for cp in first:
            cp.start()

    def mid():
        for j, chip in enumerate(chips):
            copy(1 + j, (*chip, mc), me).wait_recv()
            passed[j].start()

    def finish():
        copy(0, sibling, me).wait_recv()
        for j, chip in enumerate(chips):
            copy(4 + j, (*chip, 1 - mc), me).wait_recv()
        for cp in first + passed:
            cp.wait_send()
        mine.wait()

    return start, mid, finish


def _exchange_phases(x_ref, out_ref, send_sems, recv_sems, local_sem):
    mx, my, mc = lax.axis_index("x"), lax.axis_index("y"), lax.axis_index("c")
    me = 4 * mx + 2 * my + mc
    mine = pltpu.make_async_copy(x_ref.at[me], out_ref.at[me], local_sem)
    copies = []
    for k in range(1, N_DEV):
        px = 1 - mx if k & 4 else mx
        py = 1 - my if k & 2 else my
        pc = 1 - mc if k & 1 else mc
        copies.append(pltpu.make_async_remote_copy(
            src_ref=x_ref.at[4 * px + 2 * py + pc], dst_ref=out_ref.at[me], send_sem=send_sems.at[k - 1],
            recv_sem=recv_sems.at[k - 1], device_id=(px, py, pc), device_id_type=MESH))

    def start():
        mine.start()
        for cp in copies:
            cp.start()

    def finish():
        for cp in copies:
            cp.wait_recv()
        for cp in copies:
            cp.wait_send()
        mine.wait()

    return start, lambda: None, finish


def _gather_comm(x):
    return _Comm(_gather_phases, x, jax.ShapeDtypeStruct((N_DEV,) + x.shape, x.dtype))


def _exchange_comm(x):
    return _Comm(_exchange_phases, x, jax.ShapeDtypeStruct(x.shape, x.dtype))


def _comm_alone(comms, name):
    n = len(comms)

    def body(*refs):
        phases = [comm.phases(refs[k], refs[n + k], *refs[2 * n + 3 * k:2 * n + 3 * k + 3]) for k, comm in enumerate(comms)]
        for step in range(3):
            for phase in phases:
                phase[step]()

    any_spec = pl.BlockSpec(memory_space=pl.ANY)
    return pl.pallas_call(
        body,
        out_shape=[comm.dst for comm in comms],
        in_specs=[any_spec] * n,
        out_specs=[any_spec] * n,
        scratch_shapes=[pltpu.SemaphoreType.DMA((N_DEV - 1,)), pltpu.SemaphoreType.DMA((N_DEV - 1,)), pltpu.SemaphoreType.DMA] * n,
        name=name,
    )(*[comm.src for comm in comms])


def _sum_parts(p_ref):
    g = p_ref[0].astype(F32)
    for j in range(1, N_DEV):
        g = g + p_ref[j].astype(F32)
    return g


def _adamw_store(g, w_ref, m_ref, v_ref, g_ref, d_ref, nm_ref, nv_ref):
    m_new = ADAM_B1 * m_ref[...] + (1.0 - ADAM_B1) * g
    v_new = ADAM_B2 * v_ref[...] + (1.0 - ADAM_B2) * jnp.square(g)
    m_hat = m_new / (1.0 - ADAM_B1 ** ADAM_STEP)
    v_hat = v_new / (1.0 - ADAM_B2 ** ADAM_STEP)
    g_ref[...] = g
    d_ref[...] = -ADAM_LR * (m_hat / (jnp.sqrt(v_hat) + ADAM_EPS) + ADAM_WD * w_ref[...])
    nm_ref[...] = m_new
    nv_ref[...] = v_new


def _adamw_shard(parts, off, transposed, w, m, v, name, n_tiles=1):
    _, r, c = w.shape
    tr = r // n_tiles
    if transposed:
        rows = -(-c // BF16_ROWS) * BF16_ROWS
        window = (N_DEV, rows, tr)
    else:
        assert c == PACK_COLS
        window = (N_DEV, tr, PACK_COLS)

    def kern(p_hbm, w_ref, m_ref, v_ref, g_ref, d_ref, nm_ref, nv_ref, buf, sem):
        i = pl.program_id(0)
        if transposed:
            src = p_hbm.at[:, pl.ds(off, rows), pl.ds(pl.multiple_of(i * tr, LANES), tr)]
        else:
            src = p_hbm.at[:, pl.ds(pl.multiple_of(off + i * tr, BF16_ROWS), tr), :]
        cp = pltpu.make_async_copy(src, buf, sem)
        cp.start()
        cp.wait()
        g = _sum_parts(buf)
        if transposed:
            eye = (lax.broadcasted_iota(jnp.int32, (rows, c), 0) == lax.broadcasted_iota(jnp.int32, (rows, c), 1)).astype(F32)
            g = _hdot_tn(g, eye)
        _adamw_store(g, w_ref, m_ref, v_ref, g_ref, d_ref, nm_ref, nv_ref)

    spec = pl.BlockSpec((None, tr, c), lambda i: (0, i, 0))
    return pl.pallas_call(
        kern,
        out_shape=[jax.ShapeDtypeStruct((1, r, c), F32)] * 4,
        grid=(n_tiles,),
        in_specs=[pl.BlockSpec(memory_space=pl.ANY), spec, spec, spec],
        out_specs=[spec] * 4,
        scratch_shapes=[pltpu.VMEM(window, parts.dtype), pltpu.SemaphoreType.DMA],
        name=name,
        compiler_params=pltpu.CompilerParams(dimension_semantics=("arbitrary",), vmem_limit_bytes=VMEM_LIMIT),
    )(parts, w, m, v)


def _sum_adamw(parts, w, m, v, tr, name):
    _, R, C = parts.shape

    def kern(p_ref, w_ref, m_ref, v_ref, g_ref, d_ref, nm_ref, nv_ref):
        _adamw_store(_sum_parts(p_ref), w_ref, m_ref, v_ref, g_ref, d_ref, nm_ref, nv_ref)

    row_spec = pl.BlockSpec((tr, C), lambda i: (i, 0))
    return pl.pallas_call(
        kern,
        out_shape=[jax.ShapeDtypeStruct((R, C), F32)] * 4,
        grid=(R // tr,),
        in_specs=[pl.BlockSpec((N_DEV, tr, C), lambda i: (0, i, 0)), row_spec, row_spec, row_spec],
        out_specs=[row_spec] * 4,
        name=name,
        compiler_params=pltpu.CompilerParams(dimension_semantics=("arbitrary",), vmem_limit_bytes=VMEM_LIMIT),
    )(parts, w, m, v)


FF_SHARD = D_FF // N_DEV
CONV_SHARD = (SSM_CONV, CONV_DIM // N_DEV)
SHARDS = {"ffn1_w_gate": ((D_MODEL, FF_SHARD), True), "ffn1_w_up": ((D_MODEL, FF_SHARD), True),
          "ffn1_w_down": ((FF_SHARD, D_MODEL), False),
          "ffn2_w_gate": ((D_MODEL, FF_SHARD), True), "ffn2_w_up": ((D_MODEL, FF_SHARD), True),
          "ffn2_w_down": ((FF_SHARD, D_MODEL), False),
          "w_out": ((2 * D_MODEL // N_DEV, D_MODEL), False), "ple_w_gate": ((D_MODEL // N_DEV, D_MODEL), False),
          "w_in": ((D_MODEL, IN_PROJ // N_DEV), True), "ple_w_proj": ((D_PLE, D_MODEL // N_DEV), True),
          "conv_w": (CONV_SHARD, True),
          "conv_w_mid": (CONV_SHARD, True), "conv_w_low": (CONV_SHARD, True)}
BIG = tuple(name for name in SHARDS if not name.startswith("conv_w_"))
SMALL = ("ffn1_norm", "mix_norm", "gm_ln_g", "gm_ln_b", "gm_w_s", "gm_b_s", "gm_out_norm", "conv_b", "dt_bias", "a_log",
         "d_skip", "ssm_norm", "ffn2_norm", "ple_norm", "ple_b_gate", "final_norm")
SMALL_ROWS = 144


def _piece_rows(name):
    shape = SHARDS[name][0]
    return -(-(shape[0] * shape[1]) // PACK_COLS)


def _pad_cols(flat, name):
    pad = _piece_rows(name) * PACK_COLS - flat.shape[-1]
    return flat if pad == 0 else jnp.pad(flat, [(0, 0)] * (flat.ndim - 1) + [(0, pad)])


class _Pack:
    def __init__(self, names, tile_rows):
        self.names, self.tile_rows, self.offsets, off = names, tile_rows, {}, 0
        for name in names:
            self.offsets[name] = off
            off += _piece_rows(name)
        self.rows = -(-off // tile_rows) * tile_rows

    def pack_local(self, vals):
        parts = []
        for name in self.names:
            val = vals[name]
            parts.append(_pad_cols((val.T if SHARDS[name][1] else val).reshape(-1), name))
        flat = jnp.concatenate(parts)
        return jnp.pad(flat, (0, self.rows * PACK_COLS - flat.shape[0])).reshape(self.rows, PACK_COLS)

    def pack_owner_major(self, grads):
        parts, rows = [], 0
        for name in self.names:
            grad, piece_rows = grads[name].astype(BF16), _piece_rows(name)
            if grad.shape != (N_DEV * piece_rows, PACK_COLS):
                grad = _pad_cols(grad.reshape(N_DEV, -1), name)
            parts.append(grad.reshape(N_DEV, piece_rows, PACK_COLS))
            rows += piece_rows
        if rows < self.rows:
            parts.append(jnp.zeros((N_DEV, self.rows - rows, PACK_COLS), BF16))
        return parts[0] if len(parts) == 1 else jnp.concatenate(parts, axis=1)

    def gathered_piece(self, gathered, name):
        shape = SHARDS[name][0]
        rows = gathered[:, self.offsets[name]:self.offsets[name] + _piece_rows(name), :]
        return rows.reshape(N_DEV, -1)[:, :shape[0] * shape[1]]

    def pieces(self, gathered, name):
        return _Pieces(gathered, self.offsets[name], _piece_rows(name))


GATHER_FFN1 = _Pack(("ffn1_w_gate", "ffn1_w_up", "ffn1_w_down"), BF16_ROWS)
GATHER_MIX = _Pack(("w_out", "ple_w_gate", "w_in", "ple_w_proj", "conv_w", "conv_w_mid", "conv_w_low"), BF16_ROWS)
GATHER_FFN2 = _Pack(("ffn2_w_gate", "ffn2_w_up", "ffn2_w_down"), BF16_ROWS)
SCATTER_LATE = _Pack(("ffn2_w_gate", "ffn2_w_up", "ffn2_w_down", "w_out", "ple_w_gate", "ple_w_proj"), BF16_ROWS)
SCATTER_IN = _Pack(("w_in", "conv_w"), BF16_ROWS)
SCATTER_GATE = _Pack(("ffn1_w_gate",), BF16_ROWS)
SCATTER_UP = _Pack(("ffn1_w_up",), BF16_ROWS)
SCATTER_DOWN = _Pack(("ffn1_w_down",), BF16_ROWS)


def _pack_small(vals):
    flat = jnp.concatenate([vals[name].reshape(-1).astype(F32) for name in SMALL])
    return jnp.pad(flat, (0, SMALL_ROWS * PACK_COLS - flat.shape[0])).reshape(SMALL_ROWS, PACK_COLS)


def _unpack_small(packed, shapes):
    out, off = {}, 0
    flat = packed.reshape(-1)
    for name in SMALL:
        n = 1
        for s in shapes[name]:
            n *= s
        out[name] = flat[off:off + n].reshape(shapes[name])
        off += n
    return out


WEIGHTS = ("ffn1_norm", "ffn1_w_gate", "ffn1_w_up", "ffn1_w_down", "mix_norm", "w_in", "gm_ln_g", "gm_ln_b", "gm_w_s",
           "gm_b_s", "gm_out_norm", "conv_w", "conv_b", "dt_bias", "a_log", "d_skip", "ssm_norm", "w_out", "ffn2_norm",
           "ffn2_w_gate", "ffn2_w_up", "ffn2_w_down", "ple_norm", "ple_w_gate", "ple_b_gate", "ple_w_proj", "final_norm")


def _step(x, p, target, w, m, v):
    local = lambda d: {name: d[name][0] for name in BIG}

    shards = {name: val.astype(BF16) for name, val in local(w).items()}
    conv_high = lax.reduce_precision(w["conv_w"][0], 8, 7)
    conv_mid = lax.reduce_precision(w["conv_w"][0] - conv_high, 8, 7)
    shards["conv_w"] = conv_high.astype(BF16)
    shards["conv_w_mid"] = conv_mid.astype(BF16)
    shards["conv_w_low"] = (w["conv_w"][0] - conv_high - conv_mid).astype(BF16)
    g_ffn1 = _comm_alone([_gather_comm(GATHER_FFN1.pack_local(shards))], "gather_ffn1")[0]

    row = lambda name: w[name].reshape(1, -1)
    gm_w_s = w["gm_w_s"][0]
    gm_b_st = jnp.transpose(w["gm_b_s"][0])
    ffn1 = (row("ffn1_norm"),) + tuple(GATHER_FFN1.pieces(g_ffn1, name) for name in GATHER_FFN1.names)
    gm = (row("gm_ln_g"), row("gm_ln_b"), gm_w_s, gm_b_st, row("gm_out_norm"))

    h1, n1, a1, b1, s1, g_mix = _ffn_fwd(x, *ffn1, "ffn1_fwd", comm=_gather_comm(GATHER_MIX.pack_local(shards)))
    w_in_t = GATHER_MIX.gathered_piece(g_mix, "w_in").reshape(IN_PROJ, D_MODEL)
    w_in_t = jnp.concatenate([w_in_t, jnp.zeros((IN_PROJ_PAD - IN_PROJ, D_MODEL), BF16)], axis=0)
    w_proj_t = GATHER_MIX.gathered_piece(g_mix, "ple_w_proj").reshape(D_MODEL, D_PLE)
    conv_w = sum(GATHER_MIX.gathered_piece(g_mix, name).astype(F32) for name in ("conv_w", "conv_w_mid", "conv_w_low"))
    conv_w = conv_w.reshape(CONV_DIM, SSM_CONV).T
    ssd = (row("dt_bias"), row("a_log"), row("d_skip"), row("ssm_norm"))
    w_out = GATHER_MIX.pieces(g_mix, "w_out")

    proj, n2, xc = _mix_in_fwd(h1, row("mix_norm"), w_in_t, conv_w, row("conv_b"))
    ya = _gm_fwd(proj, *gm)
    yb, s_all, g_ffn2 = _ssd_fwd(proj, xc, *ssd, comm=_gather_comm(GATHER_FFN2.pack_local(shards)))
    ffn2 = (row("ffn2_norm"),) + tuple(GATHER_FFN2.pieces(g_ffn2, name) for name in GATHER_FFN2.names)
    h3, n3, a3, b3, s3, h2 = _ffn_fwd(h1, *ffn2, "ffn2_fwd", mixed=(ya, yb, w_out))

    g, gp = {}, {}
    dh3, loss, gp["ple_w_gate"], d_w_proj, g["ple_norm"], g["ple_b_gate"], g["final_norm"] = _tail(
        h3, p, target, row("ple_norm"), GATHER_MIX.pieces(g_mix, "ple_w_gate"), row("ple_b_gate"), w_proj_t,
        row("final_norm"))
    gp["ple_w_proj"] = d_w_proj.T

    dh2, da3, db3, g["ffn2_norm"] = _ffn_dgrad(h2, dh3, a3, b3, *ffn2, "ffn2_dgrad")
    gp["ffn2_w_gate"] = _wgrad(n3, da3, 1408, "ffn2_wgrad_gate", transpose_out=True)
    gp["ffn2_w_up"] = _wgrad(n3, db3, 1408, "ffn2_wgrad_up", transpose_out=True)
    gp["ffn2_w_down"] = _wgrad(s3, dh3, 512, "ffn2_wgrad_down", scale=0.5, bk=1024)

    dya, dyb = _out_proj_dgrad(dh2, w_out)
    gp["w_out"] = jnp.concatenate([_wgrad(ya, dh2, 1024, "w_out_wgrad_a"), _wgrad(yb, dh2, 1024, "w_out_wgrad_b")], axis=0)

    dp_zxd, d_conv_w, g["conv_b"], g["dt_bias"], g["a_log"], g["d_skip"], g["ssm_norm"], parts_late = _ssd_bwd(
        proj, xc, dyb, s_all, conv_w, *ssd, comm=_exchange_comm(SCATTER_LATE.pack_owner_major(gp)))
    gp["conv_w"] = d_conv_w.T
    dp_uv, g["gm_ln_g"], g["gm_ln_b"], g["gm_w_s"], dbst, g["gm_out_norm"] = _gm_bwd(proj, dya, *gm)
    g["gm_b_s"] = jnp.transpose(dbst)

    parts = {}
    gp["w_in"] = jnp.concatenate([_wgrad(n2, dp_uv, 1024, "w_in_wgrad_uv", transpose_out=True),
                                  _wgrad(n2, dp_zxd, 896, "w_in_wgrad_zxd", transpose_out=True)], axis=0)[:IN_PROJ]
    dh1, g["mix_norm"], parts[SCATTER_IN] = _mix_in_dgrad(h1, dh2, dp_uv, dp_zxd, row("mix_norm"), w_in_t,
                                                          comm=_exchange_comm(SCATTER_IN.pack_owner_major(gp)))

    dx, da1, db1, g["ffn1_norm"] = _ffn_dgrad(x, dh1, a1, b1, *ffn1, "ffn1_dgrad")
    gp["ffn1_w_gate"], small_parts = _wgrad(n1, da1, 1408, "ffn1_wgrad_gate", transpose_out=True,
                                            comm=_gather_comm(_pack_small(g)))
    gp["ffn1_w_up"], parts[SCATTER_GATE] = _wgrad(n1, db1, 1408, "ffn1_wgrad_up", transpose_out=True,
                                                  comm=_exchange_comm(SCATTER_GATE.pack_owner_major(gp)))
    gp["ffn1_w_down"], parts[SCATTER_UP] = _wgrad(s1, dh1, 512, "ffn1_wgrad_down", scale=0.5, bk=1024,
                                                  comm=_exchange_comm(SCATTER_UP.pack_owner_major(gp)))
    parts[SCATTER_DOWN] = _comm_alone([_exchange_comm(SCATTER_DOWN.pack_owner_major(gp))], "scatter_ffn1_down")[0]
    parts[SCATTER_LATE] = parts_late

    res_big = {}
    for pack, pack_parts in parts.items():
        for name in pack.names:
            shape, transposed = SHARDS[name]
            if name in ("ple_w_proj", "conv_w"):
                nat = pack.gathered_piece(pack_parts, name).reshape((N_DEV,) + shape[::-1])
                res_big[name] = _sum_adamw(jnp.transpose(nat, (0, 2, 1)), w[name][0], m[name][0], v[name][0], shape[0],
                                           "adamw_" + name)
            elif name == "w_in":
                res_big[name] = _adamw_shard(pack_parts, pack.offsets[name], True, w[name], m[name], v[name],
                                             "adamw_" + name, n_tiles=4)
            else:
                flip = (lambda a: jnp.transpose(a, (0, 2, 1))) if transposed else (lambda a: a)
                res = _adamw_shard(pack_parts, pack.offsets[name], False, flip(w[name]), flip(m[name]), flip(v[name]),
                                   "adamw_" + name, n_tiles=2)
                res_big[name] = [flip(r) for r in res]

    small_shapes = {name: w[name].shape for name in SMALL}
    res_small = _sum_adamw(small_parts, _pack_small(w), _pack_small(m), _pack_small(v), SMALL_ROWS, "adamw_small")
    res_small = [_unpack_small(r, small_shapes) for r in res_small]

    outs = []
    for k in range(4):
        for name in WEIGHTS:
            if name in res_small[k]:
                outs.append(res_small[k][name])
            else:
                outs.append(res_big[name][k].reshape(w[name].shape))
    return loss[0, 0], dx, outs


def kernel(x, p, ffn1_norm, ffn1_w_gate, ffn1_w_up, ffn1_w_down, mix_norm, w_in, gm_ln_g, gm_ln_b, gm_w_s, gm_b_s, gm_out_norm, conv_w, conv_b, dt_bias, a_log, d_skip, ssm_norm, w_out, ffn2_norm, ffn2_w_gate, ffn2_w_up, ffn2_w_down, ple_norm, ple_w_gate, ple_b_gate, ple_w_proj, final_norm, loss_target, m_ffn1_norm, m_ffn1_w_gate, m_ffn1_w_up, m_ffn1_w_down, m_mix_norm, m_w_in, m_gm_ln_g, m_gm_ln_b, m_gm_w_s, m_gm_b_s, m_gm_out_norm, m_conv_w, m_conv_b, m_dt_bias, m_a_log, m_d_skip, m_ssm_norm, m_w_out, m_ffn2_norm, m_ffn2_w_gate, m_ffn2_w_up, m_ffn2_w_down, m_ple_norm, m_ple_w_gate, m_ple_b_gate, m_ple_w_proj, m_final_norm, v_ffn1_norm, v_ffn1_w_gate, v_ffn1_w_up, v_ffn1_w_down, v_mix_norm, v_w_in, v_gm_ln_g, v_gm_ln_b, v_gm_w_s, v_gm_b_s, v_gm_out_norm, v_conv_w, v_conv_b, v_dt_bias, v_a_log, v_d_skip, v_ssm_norm, v_w_out, v_ffn2_norm, v_ffn2_w_gate, v_ffn2_w_up, v_ffn2_w_down, v_ple_norm, v_ple_w_gate, v_ple_b_gate, v_ple_w_proj, v_final_norm):
    args = locals()
    w = {name: args[name] for name in WEIGHTS}
    m = {name: args["m_" + name] for name in WEIGHTS}
    v = {name: args["v_" + name] for name in WEIGHTS}
    loss, dx, outs = _step(x[0], p[0, 0], loss_target[0], w, m, v)
    loss = lax.psum(loss, AXES)
    return (loss, dx[None], *outs)
```

```python
import functools
from typing import NamedTuple

import jax
import jax.numpy as jnp
from jax import lax
from jax.experimental import pallas as pl
from jax.experimental.pallas import tpu as pltpu

F32 = jnp.float32
BF16 = jnp.bfloat16
HIGHEST = lax.Precision.HIGHEST
MESH = pl.DeviceIdType.MESH
AXES = ("x", "y", "c")
N_DEV = 8

D_MODEL = 1024
D_FF = 2816
D_PLE = 256
GM_WIDTH = 1024
GM_HEADS = 8
GM_HEAD_DIM = 128
CHUNK = 128
SSM_WIDTH = 1024
SSM_HEADS = 16
SSM_HEAD_DIM = 64
SSM_GROUPS = 2
SSM_STATE = 128
SSM_CONV = 4
CONV_DIM = SSM_WIDTH + 2 * SSM_GROUPS * SSM_STATE
IN_PROJ = 2 * GM_WIDTH + SSM_WIDTH + CONV_DIM + SSM_HEADS
LANES = 128
BF16_ROWS = 16
F32_ROWS = 8
IN_PROJ_PAD = IN_PROJ - SSM_HEADS + LANES
UV_W = 2 * GM_WIDTH
ZXD_W = IN_PROJ_PAD - UV_W
HALO = 8
EPS = 1e-6

ADAM_LR = 0.001
ADAM_B1 = 0.9
ADAM_B2 = 0.999
ADAM_EPS = 1e-08
ADAM_WD = 0.01
ADAM_STEP = 10

VMEM_LIMIT = 56 * 1024 * 1024
PACK_COLS = 1024


def _rms(x, g):
    return x * lax.rsqrt(jnp.mean(x * x, axis=-1, keepdims=True) + EPS) * g


def _gelu(x):
    return 0.5 * x * (1.0 + lax.erf(x * (2.0 ** -0.5)))


def _silu(x):
    return x * jax.nn.sigmoid(x)


def _dot(a, b):
    return jnp.dot(a.astype(BF16), b.astype(BF16), preferred_element_type=F32)


def _dot_nt(a, b):
    return lax.dot_general(a.astype(BF16), b.astype(BF16), (((1,), (1,)), ((), ())), preferred_element_type=F32)


def _dot_tn(a, b):
    return lax.dot_general(a.astype(BF16), b.astype(BF16), (((0,), (0,)), ((), ())), preferred_element_type=F32)


def _hdot_tn(a, b):
    return lax.dot_general(a, b, (((0,), (0,)), ((), ())), precision=HIGHEST, preferred_element_type=F32)


def _split3(x):
    hi = x.astype(BF16)
    rest = x - hi.astype(F32)
    mid = rest.astype(BF16)
    return hi, mid, (rest - mid.astype(F32)).astype(BF16)


def _exact_dot(x, mask, dims, x_first=True):
    terms = [lax.dot_general(*((t, mask) if x_first else (mask, t)), (dims, ((), ())), preferred_element_type=F32)
             for t in _split3(x)]
    return (terms[0] + terms[1]) + terms[2]


def _mask_product(fwd_dims, fwd_x_first, bwd_dims, bwd_x_first):
    @jax.custom_vjp
    def product(x, mask):
        return _exact_dot(x, mask, fwd_dims, fwd_x_first)

    def fwd(x, mask):
        return product(x, mask), mask

    def bwd(mask, g):
        return _exact_dot(g, mask, bwd_dims, bwd_x_first), jnp.zeros_like(mask)

    product.defvjp(fwd, bwd)
    return product


_widen = _mask_product(((1,), (0,)), True, ((1,), (1,)), True)
_cumsum_rows = _mask_product(((1,), (0,)), False, ((0,), (0,)), False)
_cumsum_cols = _mask_product(((0,), (0,)), True, ((1,), (1,)), False)


class _Pieces(NamedTuple):
    gathered: jax.Array
    row_off: int
    rows: int


class _Comm(NamedTuple):
    phases: object
    src: jax.Array
    dst: jax.ShapeDtypeStruct


def _tiled(body, name, n_steps, tiled_in, full_in, big_in, tiled_out, acc_out, scratch=(), reverse=False, comm=None):
    n_t, n_f, n_b, n_to, n_a = len(tiled_in), len(full_in), len(big_in), len(tiled_out), len(acc_out)
    n_c = 1 if comm else 0

    def row(i):
        return n_steps - 1 - i if reverse else i

    in_specs, args = [], []
    for arr, br, bc, cb in tiled_in:
        if callable(cb):
            in_specs.append(pl.BlockSpec((br, bc), cb))
        else:
            in_specs.append(pl.BlockSpec((br, bc), functools.partial(lambda i, cb: (row(i), cb), cb=cb)))
        args.append(arr)
    for arr in full_in:
        in_specs.append(pl.BlockSpec(arr.shape, functools.partial(lambda i, nd: (0,) * nd, nd=arr.ndim)))
        args.append(arr)
    big_shapes, n_copies = [], 0
    for big in big_in:
        in_specs.append(pl.BlockSpec(memory_space=pl.ANY))
        if isinstance(big, _Pieces):
            args.append(big.gathered)
            big_shapes.append(((N_DEV * big.rows, PACK_COLS), big.gathered.dtype))
            n_copies += N_DEV
        else:
            args.append(big)
            big_shapes.append((big.shape, big.dtype))
            n_copies += 1
    if comm:
        in_specs.append(pl.BlockSpec(memory_space=pl.ANY))
        args.append(comm.src)
    out_specs, out_shape = [], []
    for rows, cols, dt, br in tiled_out:
        out_specs.append(pl.BlockSpec((br, cols), lambda i: (row(i), 0)))
        out_shape.append(jax.ShapeDtypeStruct((rows, cols), dt))
    for shp, dt in acc_out:
        out_specs.append(pl.BlockSpec(shp, functools.partial(lambda i, nd: (0,) * nd, nd=len(shp))))
        out_shape.append(jax.ShapeDtypeStruct(shp, dt))
    if comm:
        out_specs.append(pl.BlockSpec(memory_space=pl.ANY))
        out_shape.append(comm.dst)
    scratch_shapes = [pltpu.VMEM(shp, dt) for shp, dt in big_shapes] + list(scratch)
    if n_copies:
        scratch_shapes.append(pltpu.SemaphoreType.DMA((n_copies,)))
    if comm:
        scratch_shapes += [pltpu.SemaphoreType.DMA((N_DEV - 1,)), pltpu.SemaphoreType.DMA((N_DEV - 1,)), pltpu.SemaphoreType.DMA]

    def kern(*refs):
        n_in = n_t + n_f + n_b + n_c
        ins = refs[: n_t + n_f]
        big_hbm = refs[n_t + n_f : n_t + n_f + n_b]
        outs = refs[n_in : n_in + n_to + n_a]
        rest = refs[n_in + n_to + n_a + n_c :]
        big_vmem, scr = rest[:n_b], rest[n_b:]
        if comm:
            scr, comm_sems = scr[:-3], scr[-3:]
            comm_start, comm_mid, comm_finish = comm.phases(refs[n_in - 1], refs[n_in + n_to + n_a], *comm_sems)
        if n_copies:
            scr, copy_sems = scr[:-1], scr[-1]
        step = pl.program_id(0)

        @pl.when(step == 0)
        def _():
            copies = []
            for big, src, dst in zip(big_in, big_hbm, big_vmem):
                if isinstance(big, _Pieces):
                    for j in range(N_DEV):
                        copies.append((src.at[j, pl.ds(big.row_off, big.rows), :], dst.at[pl.ds(j * big.rows, big.rows), :]))
                else:
                    copies.append((src, dst))
            copies = [pltpu.make_async_copy(a, b, copy_sems.at[k]) for k, (a, b) in enumerate(copies)]
            for cp in copies:
                cp.start()
            for cp in copies:
                cp.wait()
            for acc in outs[n_to:]:
                acc[...] = jnp.zeros(acc.shape, acc.dtype)
            if comm:
                comm_start()

        body(row(step), *ins, *big_vmem, *outs, *scr)
        if comm:
            pl.when(step == (n_steps - 1) // 2)(comm_mid)
            pl.when(step == n_steps - 1)(comm_finish)

    res = pl.pallas_call(
        kern,
        out_shape=out_shape,
        grid=(n_steps,),
        in_specs=in_specs,
        out_specs=out_specs,
        scratch_shapes=scratch_shapes,
        name=name,
        compiler_params=pltpu.CompilerParams(dimension_semantics=("arbitrary",), vmem_limit_bytes=VMEM_LIMIT),
    )(*args)
    return res


FF_CHUNKS = ((0, 1536), (1536, D_FF))
FFN_TM = 256


def _ffn_fwd(h, g, wg_t, wu_t, wd, name, comm=None, mixed=None):
    T = h.shape[0]
    n_mix = 2 if mixed else 0

    def body(i, h_ref, *refs):
        ya_ref, yb_ref = refs[:n_mix] if mixed else (None, None)
        g_ref, wg_ref, wu_ref, wd_ref = refs[n_mix:n_mix + 4]
        o_ref, n_ref, a_ref, b_ref, s_ref = refs[n_mix + 4 + n_mix // 2:n_mix + 9 + n_mix // 2]
        x = h_ref[...]
        if mixed:
            wo_ref, x_ref = refs[n_mix + 4], refs[-1]
            x = (x + jnp.dot(ya_ref[...], wo_ref[:GM_WIDTH, :], preferred_element_type=F32)
                 + jnp.dot(yb_ref[...], wo_ref[GM_WIDTH:, :], preferred_element_type=F32))
            x_ref[...] = x
        n = _rms(x, g_ref[...]).astype(BF16)
        n_ref[...] = n
        f = jnp.zeros(x.shape, F32)
        for lo, hi in FF_CHUNKS:
            a = _dot_nt(n, wg_ref[lo:hi, :])
            b = _dot_nt(n, wu_ref[lo:hi, :])
            s = (_silu(a) * b).astype(BF16)
            a_ref[:, lo:hi] = a.astype(BF16)
            b_ref[:, lo:hi] = b.astype(BF16)
            s_ref[:, lo:hi] = s
            f = f + jnp.dot(s, wd_ref[lo:hi, :], preferred_element_type=F32)
        o_ref[...] = x + 0.5 * f

    tiled_in, big_in = [(h, FFN_TM, D_MODEL, 0)], [wg_t, wu_t, wd]
    tiled_out = [(T, D_MODEL, F32, FFN_TM), (T, D_MODEL, BF16, FFN_TM), (T, D_FF, BF16, FFN_TM), (T, D_FF, BF16, FFN_TM),
                 (T, D_FF, BF16, FFN_TM)]
    if mixed:
        tiled_in += [(mixed[0], FFN_TM, GM_WIDTH, 0), (mixed[1], FFN_TM, SSM_WIDTH, 0)]
        big_in.append(mixed[2])
        tiled_out.append((T, D_MODEL, F32, FFN_TM))
    return _tiled(body, name, T // FFN_TM, tiled_in, [g], big_in, tiled_out, [], comm=comm)


def _ffn_dgrad(h, dout, a16, b16, g, wg_t, wu_t, wd, name):
    T = h.shape[0]

    def body(i, h_ref, do_ref, a_ref, b_ref, g_ref, wg_ref, wu_ref, wd_ref, dh_ref, da_ref, db_ref, dg_ref):
        dout = do_ref[...]
        _, rms_vjp = jax.vjp(_rms, h_ref[...], g_ref[...])
        dfo = (0.5 * dout).astype(BF16)
        dn = jnp.zeros(dout.shape, F32)
        for lo, hi in FF_CHUNKS:
            a = a_ref[:, lo:hi].astype(F32)
            b = b_ref[:, lo:hi].astype(F32)
            sg = jax.nn.sigmoid(a)
            ds = _dot_nt(dfo, wd_ref[lo:hi, :])
            db = (ds * (a * sg)).astype(BF16)
            da = (ds * b * (sg * (1.0 + a * (1.0 - sg)))).astype(BF16)
            dn = dn + _dot(da, wg_ref[lo:hi, :]) + _dot(db, wu_ref[lo:hi, :])
            da_ref[:, lo:hi] = da
            db_ref[:, lo:hi] = db
        dx, dg = rms_vjp(dn)
        dh_ref[...] = dout + dx
        dg_ref[...] += dg

    return _tiled(body, name, T // FFN_TM,
                  [(h, FFN_TM, D_MODEL, 0), (dout, FFN_TM, D_MODEL, 0), (a16, FFN_TM, D_FF, 0), (b16, FFN_TM, D_FF, 0)],
                  [g], [wg_t, wu_t, wd],
                  [(T, D_MODEL, F32, FFN_TM), (T, D_FF, BF16, FFN_TM), (T, D_FF, BF16, FFN_TM)], [((1, D_MODEL), F32)])


def _wgrad(a, b, bn, name, scale=None, transpose_out=False, bk=2048, comm=None):
    T, M = a.shape
    N = b.shape[1]
    bk = min(bk, T)
    assert M % LANES == 0 and N % bn == 0 and T % bk == 0
    n_j, n_k = N // bn, T // bk
    n_c = 1 if comm else 0

    def kern(*refs):
        a_ref, b_ref, o_ref, acc_ref = refs[0], refs[1], refs[2 + n_c], refs[3 + 2 * n_c]
        j, k = pl.program_id(0), pl.program_id(1)
        if comm:
            comm_start, comm_mid, comm_finish = comm.phases(refs[2], refs[4], *refs[6:])
            pl.when((j == 0) & (k == 0))(comm_start)

        @pl.when(k == 0)
        def _():
            acc_ref[...] = jnp.zeros(acc_ref.shape, F32)

        bv = b_ref[...]
        if scale is not None:
            bv = bv * scale
        acc_ref[...] += _dot_tn(a_ref[...], bv)

        @pl.when(k == n_k - 1)
        def _():
            acc = acc_ref[...]
            o_ref[...] = (acc.T if transpose_out else acc).astype(BF16)

        if comm:
            pl.when((j == (n_j - 1) // 2) & (k == n_k - 1))(comm_mid)
            pl.when((j == n_j - 1) & (k == n_k - 1))(comm_finish)

    if transpose_out:
        out_shape, out_spec = (N, M), pl.BlockSpec((bn, M), lambda j, k: (j, 0))
    else:
        out_shape, out_spec = (M, N), pl.BlockSpec((M, bn), lambda j, k: (0, j))
    any_spec = pl.BlockSpec(memory_space=pl.ANY)
    comm_sems = [pltpu.SemaphoreType.DMA((N_DEV - 1,)), pltpu.SemaphoreType.DMA((N_DEV - 1,)), pltpu.SemaphoreType.DMA]
    res = pl.pallas_call(
        kern,
        out_shape=[jax.ShapeDtypeStruct(out_shape, BF16)] + ([comm.dst] if comm else []),
        grid=(n_j, n_k),
        in_specs=[pl.BlockSpec((bk, M), lambda j, k: (k, 0)), pl.BlockSpec((bk, bn), lambda j, k: (k, j))] + [any_spec] * n_c,
        out_specs=[out_spec] + [any_spec] * n_c,
        scratch_shapes=[pltpu.VMEM((M, bn), F32)] + (comm_sems if comm else []),
        name=name,
        compiler_params=pltpu.CompilerParams(dimension_semantics=("arbitrary", "arbitrary"), vmem_limit_bytes=VMEM_LIMIT),
    )(a, b, *([comm.src] if comm else []))
    return res if comm else res[0]


PROJ_TM = 256
UVZ_W = 2 * GM_WIDTH + SSM_WIDTH
PROJ_KEPT = UVZ_W + LANES
Z_BLK = 2 * GM_WIDTH // SSM_WIDTH
DT_BLK = UVZ_W // LANES


def _mix_in_fwd(h, g, w_in_t, conv_w, conv_b):
    T = h.shape[0]

    def body(i, h_ref, g_ref, cw_ref, cb_ref, w_ref, p_ref, n_ref, x_ref, xc_ref, ext_ref):
        @pl.when(i == 0)
        def _():
            ext_ref[0:HALO, :] = jnp.zeros((HALO, CONV_DIM), F32)

        n = _rms(h_ref[...], g_ref[...]).astype(BF16)
        n_ref[...] = n
        proj = _dot_nt(n, w_ref[...])
        p_ref[:, :UVZ_W] = proj[:, :UVZ_W]
        p_ref[:, UVZ_W:] = proj[:, UVZ_W + CONV_DIM:]
        xbc = proj[:, UVZ_W:UVZ_W + CONV_DIM]
        x_ref[...] = xbc.astype(BF16)
        ext_ref[HALO:, :] = xbc
        xc_ref[...] = _conv_taps(ext_ref, cw_ref[...], cb_ref[...], PROJ_TM)
        ext_ref[0:HALO, :] = ext_ref[PROJ_TM:PROJ_TM + HALO, :]

    return _tiled(body, "mix_in_fwd", T // PROJ_TM, [(h, PROJ_TM, D_MODEL, 0)], [g, conv_w, conv_b], [w_in_t],
                  [(T, PROJ_KEPT, F32, PROJ_TM), (T, D_MODEL, BF16, PROJ_TM), (T, CONV_DIM, BF16, PROJ_TM),
                   (T, CONV_DIM, F32, PROJ_TM)], [],
                  scratch=[pltpu.VMEM((HALO + PROJ_TM, CONV_DIM), F32)])


def _mix_in_dgrad(h, dh_in, dp_uv, dp_zxd, g, w_in_t, comm=None):
    T = h.shape[0]

    def body(i, h_ref, dh_ref, duv_ref, dzxd_ref, g_ref, w_ref, o_ref, dg_ref):
        dn = _dot(duv_ref[...], w_ref[:UV_W, :]) + _dot(dzxd_ref[...], w_ref[UV_W:, :])
        _, rms_vjp = jax.vjp(_rms, h_ref[...], g_ref[...])
        dx, dg = rms_vjp(dn)
        o_ref[...] = dh_ref[...] + dx
        dg_ref[...] += dg

    return _tiled(body, "mix_in_dgrad", T // PROJ_TM,
                  [(h, PROJ_TM, D_MODEL, 0), (dh_in, PROJ_TM, D_MODEL, 0), (dp_uv, PROJ_TM, UV_W, 0),
                   (dp_zxd, PROJ_TM, ZXD_W, 0)], [g], [w_in_t],
                  [(T, D_MODEL, F32, PROJ_TM)], [((1, D_MODEL), F32)], comm=comm)


def _out_proj_dgrad(dh, w_out):
    T = dh.shape[0]

    def body(i, dh_ref, w_ref, dya_ref, dyb_ref):
        d = dh_ref[...].astype(BF16)
        dya_ref[...] = _dot_nt(d, w_ref[:GM_WIDTH, :])
        dyb_ref[...] = _dot_nt(d, w_ref[GM_WIDTH:, :])

    return _tiled(body, "out_proj_dgrad", T // PROJ_TM, [(dh, PROJ_TM, D_MODEL, 0)], [], [w_out],
                  [(T, GM_WIDTH, F32, PROJ_TM), (T, SSM_WIDTH, F32, PROJ_TM)], [])


def _gm_chunk(u, v, ln_g, ln_b, b_st, out_g, *w_heads):
    ug = _gelu(u)
    vg = _gelu(v)
    mu = jnp.mean(vg, axis=-1, keepdims=True)
    xc = vg - mu
    vn = xc * lax.rsqrt(jnp.mean(xc * xc, axis=-1, keepdims=True) + EPS) * ln_g + ln_b
    t_idx = lax.broadcasted_iota(jnp.int32, (CHUNK, CHUNK), 0)
    s_idx = lax.broadcasted_iota(jnp.int32, (CHUNK, CHUNK), 1)
    causal = t_idx >= s_idx
    mixed = []
    for hd in range(GM_HEADS):
        wm = jnp.where(causal, w_heads[hd], 0.0)
        cols = slice(hd * GM_HEAD_DIM, (hd + 1) * GM_HEAD_DIM)
        mixed.append(_dot(wm, vn[:, cols]) + b_st[:, hd:hd + 1])
    ya0 = ug * jnp.concatenate(mixed, axis=1)
    return _rms(ya0, out_g)


GM_FWD_CHUNKS = 2


def _gm_fwd(proj, ln_g, ln_b, w_s, b_st, out_g):
    T = proj.shape[0]

    rows = GM_FWD_CHUNKS * CHUNK

    def body(i, u_ref, v_ref, lg_ref, lb_ref, w_ref, bs_ref, og_ref, ya_ref):
        w_heads = [w_ref[hd] for hd in range(GM_HEADS)]
        for c in range(GM_FWD_CHUNKS):
            tok = pl.ds(c * CHUNK, CHUNK)
            ya = _gm_chunk(u_ref[tok, :], v_ref[tok, :], lg_ref[...], lb_ref[...], bs_ref[...], og_ref[...], *w_heads)
            ya_ref[tok, :] = ya.astype(BF16)

    return _tiled(body, "gmlp_fwd", T // rows, [(proj, rows, GM_WIDTH, 0), (proj, rows, GM_WIDTH, 1)],
                  [ln_g, ln_b, w_s, b_st, out_g], [], [(T, GM_WIDTH, BF16, rows)], [])[0]


def _gm_bwd(proj, dya, ln_g, ln_b, w_s, b_st, out_g):
    T = proj.shape[0]

    def body(i, u_ref, v_ref, dy_ref, lg_ref, lb_ref, w_ref, bs_ref, og_ref, duv_ref, dlg_ref, dlb_ref, dw_ref, dbs_ref,
             dog_ref):
        w_heads = [w_ref[hd] for hd in range(GM_HEADS)]
        _, vjp = jax.vjp(_gm_chunk, u_ref[...], v_ref[...], lg_ref[...], lb_ref[...], bs_ref[...], og_ref[...], *w_heads)
        grads = vjp(dy_ref[...])
        duv_ref[:, :GM_WIDTH] = grads[0].astype(BF16)
        duv_ref[:, GM_WIDTH:] = grads[1].astype(BF16)
        dlg_ref[...] += grads[2]
        dlb_ref[...] += grads[3]
        dbs_ref[...] += grads[4]
        dog_ref[...] += grads[5]
        for hd in range(GM_HEADS):
            dw_ref[hd] += grads[6 + hd]

    return _tiled(body, "gmlp_bwd", T // CHUNK,
                  [(proj, CHUNK, GM_WIDTH, 0), (proj, CHUNK, GM_WIDTH, 1), (dya, CHUNK, GM_WIDTH, 0)],
                  [ln_g, ln_b, w_s, b_st, out_g], [], [(T, UV_W, BF16, CHUNK)],
                  [((1, GM_WIDTH), F32), ((1, GM_WIDTH), F32), ((GM_HEADS, CHUNK, CHUNK), F32),
                   ((CHUNK, GM_HEADS), F32), ((1, GM_WIDTH), F32)])


def _ssd_chunk(xc, z, dtr, s_in, dt_bias, a_log, d_skip, norm_g):
    half = SSM_WIDTH // SSM_GROUPS
    l_idx = lax.broadcasted_iota(jnp.int32, (CHUNK, CHUNK), 0)
    s_idx = lax.broadcasted_iota(jnp.int32, (CHUNK, CHUNK), 1)
    causal = l_idx >= s_idx
    head_of_col = lax.broadcasted_iota(jnp.int32, (SSM_HEADS, SSM_WIDTH), 1) // SSM_HEAD_DIM
    expand = (head_of_col == lax.broadcasted_iota(jnp.int32, (SSM_HEADS, SSM_WIDTH), 0)).astype(BF16)

    xcs = _silu(xc)
    xs = xcs[:, :SSM_WIDTH]
    dt = jax.nn.softplus(dtr + dt_bias)
    adt = dt * (-jnp.exp(a_log))
    acs = _cumsum_rows(adt, causal.astype(BF16))
    acs_t = _cumsum_cols(adt, (l_idx <= s_idx).astype(BF16))
    tot = acs[CHUNK - 1:CHUNK, :]
    dt_w = _widen(dt, expand)
    out_decay_w = _widen(jnp.exp(acs), expand)
    state_decay_w = _widen(jnp.exp(tot - acs), expand)
    chunk_decay_w = _widen(jnp.exp(tot), expand)
    d_skip_w = _widen(d_skip, expand)
    xdt = xs * dt_w
    xdt_decayed = xdt * state_decay_w

    y_diag, y_off, states = [], [], []
    for grp in range(SSM_GROUPS):
        b0 = SSM_WIDTH + grp * SSM_STATE
        c0 = SSM_WIDTH + SSM_GROUPS * SSM_STATE + grp * SSM_STATE
        bm = xcs[:, b0:b0 + SSM_STATE].astype(BF16)
        cm = xcs[:, c0:c0 + SSM_STATE].astype(BF16)
        cb = _dot_nt(cm, bm)
        for k in range(grp * SSM_HEADS // SSM_GROUPS, (grp + 1) * SSM_HEADS // SSM_GROUPS):
            decay = jnp.exp(jnp.where(causal, acs[:, k:k + 1] - acs_t[k:k + 1, :], -jnp.inf))
            y_diag.append(_dot(cb * decay, xdt[:, k * SSM_HEAD_DIM:(k + 1) * SSM_HEAD_DIM]))
        cols = slice(grp * half, (grp + 1) * half)
        states.append(_dot_tn(bm, xdt_decayed[:, cols]))
        y_off.append(_dot(cm, s_in[:, cols]))
    y = jnp.concatenate(y_diag, axis=1) + jnp.concatenate(y_off, axis=1) * out_decay_w + xs * d_skip_w
    s_out = s_in * chunk_decay_w + jnp.concatenate(states, axis=1)
    y = y * _silu(z)
    normed = []
    for grp in range(SSM_GROUPS):
        yg = y[:, grp * half:(grp + 1) * half]
        normed.append(yg * lax.rsqrt(jnp.mean(yg * yg, axis=-1, keepdims=True) + EPS))
    return jnp.concatenate(normed, axis=1) * norm_g, s_out


def _sum_row_tiles(x):
    return x.reshape(x.shape[0] // F32_ROWS, F32_ROWS, x.shape[1]).sum(axis=0)


def _conv_taps(ext_ref, w, b, rows):
    y = b
    for k in range(SSM_CONV):
        y = y + w[k:k + 1, :] * ext_ref[pl.ds(HALO - (SSM_CONV - 1) + k, rows), :]
    return y


def _ssd_fwd(proj, xc, dt_bias, a_log, d_skip, norm_g, comm=None):
    T = proj.shape[0]
    n_chunks = T // CHUNK

    def body(i, z_ref, xc_ref, dt_ref, dtb_ref, al_ref, dsk_ref, ng_ref, yb_ref, sin_ref, st_ref):
        @pl.when(i == 0)
        def _():
            st_ref[...] = jnp.zeros(st_ref.shape, F32)

        s_in = st_ref[...]
        yb, s_out = _ssd_chunk(xc_ref[...], z_ref[...], dt_ref[:, 0:SSM_HEADS], s_in, dtb_ref[...], al_ref[...],
                               dsk_ref[...], ng_ref[...])
        yb_ref[...] = yb.astype(BF16)
        sin_ref[...] = s_in
        st_ref[...] = s_out

    return _tiled(body, "ssd_fwd", n_chunks,
                  [(proj, CHUNK, SSM_WIDTH, Z_BLK), (xc, CHUNK, CONV_DIM, 0), (proj, CHUNK, LANES, DT_BLK)],
                  [dt_bias, a_log, d_skip, norm_g], [],
                  [(T, SSM_WIDTH, BF16, CHUNK), (n_chunks * SSM_STATE, SSM_WIDTH, F32, SSM_STATE)], [],
                  scratch=[pltpu.VMEM((SSM_STATE, SSM_WIDTH), F32)], comm=comm)


def _ssd_bwd(proj, x16, xc, dyb, s_all, conv_w, dt_bias, a_log, d_skip, norm_g, comm=None):
    T = proj.shape[0]
    n_chunks = T // CHUNK

    def body(i, z_ref, x_ref, xc_ref, dt_ref, dy_ref, sin_ref, cw_ref, dtb_ref, al_ref, dsk_ref, ng_ref,
             dzxd_ref, dcw_ref, dcb_ref, ddtb_ref, dal_ref, ddsk_ref, dng_ref, dext_ref, dst_ref, cw_acc, cb_acc):
        @pl.when(i == n_chunks - 1)
        def _():
            dext_ref[CHUNK:, :] = jnp.zeros((HALO, CONV_DIM), F32)
            dst_ref[...] = jnp.zeros(dst_ref.shape, F32)
            cw_acc[...] = jnp.zeros(cw_acc.shape, F32)
            cb_acc[...] = jnp.zeros(cb_acc.shape, F32)

        _, vjp = jax.vjp(_ssd_chunk, xc_ref[...], z_ref[...], dt_ref[:, 0:SSM_HEADS], sin_ref[...], dtb_ref[...], al_ref[...],
                         dsk_ref[...], ng_ref[...])
        dxc, dz, ddtr, ds_in, ddtb, dal, ddsk, dng = vjp((dy_ref[...], dst_ref[...]))
        dst_ref[...] = ds_in
        ddtb_ref[...] += ddtb
        dal_ref[...] += dal
        ddsk_ref[...] += ddsk
        dng_ref[...] += dng
        dext_ref[0:CHUNK, :] = dxc
        cw = cw_ref[...]
        x = x_ref[...].astype(F32)
        dx = jnp.zeros((CHUNK, CONV_DIM), F32)
        for k in range(SSM_CONV):
            shifted = dext_ref[pl.ds(SSM_CONV - 1 - k, CHUNK), :]
            dx = dx + cw[k:k + 1, :] * shifted
            cw_acc[k] += _sum_row_tiles(shifted * x)
        cb_acc[...] += _sum_row_tiles(dxc)

        @pl.when(i == 0)
        def _():
            dcw_ref[...] = jnp.sum(cw_acc[...], axis=1)
            dcb_ref[...] = jnp.sum(cb_acc[...], axis=0, keepdims=True)

        dext_ref[CHUNK:, :] = dext_ref[0:HALO, :]
        dzxd_ref[:, 0:SSM_WIDTH] = dz.astype(BF16)
        dzxd_ref[:, SSM_WIDTH:SSM_WIDTH + CONV_DIM] = dx.astype(BF16)
        dzxd_ref[:, SSM_WIDTH + CONV_DIM:] = jnp.concatenate(
            [ddtr, jnp.zeros((CHUNK, LANES - SSM_HEADS), F32)], axis=1).astype(BF16)

    return _tiled(body, "ssd_bwd", n_chunks,
                  [(proj, CHUNK, SSM_WIDTH, Z_BLK), (x16, CHUNK, CONV_DIM, 0), (xc, CHUNK, CONV_DIM, 0),
                   (proj, CHUNK, LANES, DT_BLK), (dyb, CHUNK, SSM_WIDTH, 0), (s_all, SSM_STATE, SSM_WIDTH, 0)],
                  [conv_w, dt_bias, a_log, d_skip, norm_g], [],
                  [(T, ZXD_W, BF16, CHUNK)],
                  [((SSM_CONV, CONV_DIM), F32), ((1, CONV_DIM), F32), ((1, SSM_HEADS), F32), ((1, SSM_HEADS), F32),
                   ((1, SSM_HEADS), F32), ((1, SSM_WIDTH), F32)],
                  scratch=[pltpu.VMEM((CHUNK + HALO, CONV_DIM), F32), pltpu.VMEM((SSM_STATE, SSM_WIDTH), F32),
                           pltpu.VMEM((SSM_CONV, F32_ROWS, CONV_DIM), F32), pltpu.VMEM((F32_ROWS, CONV_DIM), F32)],
                  reverse=True, comm=comm)


TAIL_TM = 512


def _tail(h, p, target, ple_norm, w_gate, b_gate, w_proj_t, final_norm):
    T = h.shape[0]

    def head(x, pre, pp, b_g, f_norm, tgt):
        gate = jax.nn.sigmoid(pre + b_g)
        out = _rms(x + gate * pp, f_norm)
        err = out - tgt
        return 0.5 * jnp.sum(jnp.mean(err * err, axis=-1, keepdims=True), axis=0, keepdims=True)

    def body(i, h_ref, p_ref, t_ref, pn_ref, bg_ref, fn_ref, wg_ref, wp_ref, dh_ref, loss_ref, dwg_ref, dwp_ref, dpn_ref,
             dbg_ref, dfn_ref):
        x = h_ref[...]
        n4f, n_vjp = jax.vjp(_rms, x, pn_ref[...])
        n4 = n4f.astype(BF16)
        pre = jnp.dot(n4, wg_ref[...], preferred_element_type=F32)
        p16 = p_ref[...].astype(BF16)
        pp = _dot_nt(p16, wp_ref[...])
        loss, h_vjp = jax.vjp(functools.partial(head, tgt=t_ref[...]), x, pre, pp, bg_ref[...], fn_ref[...])
        dx, dpre, dpp, dbg, dfn = h_vjp(jnp.ones((1, 1), F32))
        dpre16 = dpre.astype(BF16)
        dn4 = _dot_nt(dpre16, wg_ref[...])
        dx2, dpn = n_vjp(dn4)
        dh_ref[...] = dx + dx2
        loss_ref[...] += loss
        dwg_ref[...] += _dot_tn(n4, dpre16)
        dwp_ref[...] += _dot_tn(p16, dpp)
        dpn_ref[...] += dpn
        dbg_ref[...] += dbg
        dfn_ref[...] += dfn

    return _tiled(body, "tail", T // TAIL_TM,
                  [(h, TAIL_TM, D_MODEL, 0), (p, TAIL_TM, D_PLE, 0), (target, TAIL_TM, D_MODEL, 0)],
                  [ple_norm, b_gate, final_norm], [w_gate, w_proj_t],
                  [(T, D_MODEL, F32, TAIL_TM)],
                  [((1, 1), F32), ((D_MODEL, D_MODEL), F32), ((D_PLE, D_MODEL), F32), ((1, D_MODEL), F32),
                   ((1, D_MODEL), F32), ((1, D_MODEL), F32)])


def _gather_phases(x_ref, out_ref, send_sems, recv_sems, local_sem):
    mx, my, mc = lax.axis_index("x"), lax.axis_index("y"), lax.axis_index("c")
    me, sibling = (mx, my, mc), (mx, my, 1 - mc)
    chips = [(1 - mx, my), (mx, 1 - my), (1 - mx, 1 - my)]

    def rows(px, py, pc):
        return out_ref.at[4 * px + 2 * py + pc]

    def copy(k, block, to, src=None):
        return pltpu.make_async_remote_copy(
            src_ref=rows(*block) if src is None else src, dst_ref=rows(*block),
            send_sem=send_sems.at[k], recv_sem=recv_sems.at[k], device_id=to, device_id_type=MESH)

    mine = pltpu.make_async_copy(x_ref, rows(*me), local_sem)
    first = [copy(0, me, sibling, src=x_ref)] + [copy(1 + j, me, (*chip, mc), src=x_ref) for j, chip in enumerate(chips)]
    passed = [copy(4 + j, (*chip, mc), sibling) for j, chip in enumerate(chips)]

    def start():
        mine.start()
        for cp in first:
            cp.start()

    def mid():
        for j, chip in enumerate(chips):
            copy(1 + j, (*chip, mc), me).wait_recv()
            passed[j].start()

    def finish():
        copy(0, sibling, me).wait_recv()
        for j, chip in enumerate(chips):
            copy(4 + j, (*chip, 1 - mc), me).wait_recv()
        for cp in first + passed:
            cp.wait_send()
        mine.wait()

    return start, mid, finish


def _exchange_phases(x_ref, out_ref, send_sems, recv_sems, local_sem):
    mx, my, mc = lax.axis_index("x"), lax.axis_index("y"), lax.axis_index("c")
    me = 4 * mx + 2 * my + mc
    mine = pltpu.make_async_copy(x_ref.at[me], out_ref.at[me], local_sem)
    copies = []
    for k in range(1, N_DEV):
        px = 1 - mx if k & 4 else mx
        py = 1 - my if k & 2 else my
        pc = 1 - mc if k & 1 else mc
        copies.append(pltpu.make_async_remote_copy(
            src_ref=x_ref.at[4 * px + 2 * py + pc], dst_ref=out_ref.at[me], send_sem=send_sems.at[k - 1],
            recv_sem=recv_sems.at[k - 1], device_id=(px, py, pc), device_id_type=MESH))

    def start():
        mine.start()
        for cp in copies:
            cp.start()

    def finish():
        for cp in copies:
            cp.wait_recv()
        for cp in copies:
            cp.wait_send()
        mine.wait()

    return start, lambda: None, finish


def _gather_comm(x):
    return _Comm(_gather_phases, x, jax.ShapeDtypeStruct((N_DEV,) + x.shape, x.dtype))


def _exchange_comm(x):
    return _Comm(_exchange_phases, x, jax.ShapeDtypeStruct(x.shape, x.dtype))


def _comm_alone(comms, name):
    n = len(comms)

    def body(*refs):
        phases = [comm.phases(refs[k], refs[n + k], *refs[2 * n + 3 * k:2 * n + 3 * k + 3]) for k, comm in enumerate(comms)]
        for step in range(3):
            for phase in phases:
                phase[step]()

    any_spec = pl.BlockSpec(memory_space=pl.ANY)
    return pl.pallas_call(
        body,
        out_shape=[comm.dst for comm in comms],
        in_specs=[any_spec] * n,
        out_specs=[any_spec] * n,
        scratch_shapes=[pltpu.SemaphoreType.DMA((N_DEV - 1,)), pltpu.SemaphoreType.DMA((N_DEV - 1,)), pltpu.SemaphoreType.DMA] * n,
        name=name,
    )(*[comm.src for comm in comms])


def _sum_parts(p_ref):
    g = p_ref[0].astype(F32)
    for j in range(1, N_DEV):
        g = g + p_ref[j].astype(F32)
    return g


def _adamw_store(g, w_ref, m_ref, v_ref, g_ref, d_ref, nm_ref, nv_ref):
    m_new = ADAM_B1 * m_ref[...] + (1.0 - ADAM_B1) * g
    v_new = ADAM_B2 * v_ref[...] + (1.0 - ADAM_B2) * jnp.square(g)
    m_hat = m_new / (1.0 - ADAM_B1 ** ADAM_STEP)
    v_hat = v_new / (1.0 - ADAM_B2 ** ADAM_STEP)
    g_ref[...] = g
    d_ref[...] = -ADAM_LR * (m_hat / (jnp.sqrt(v_hat) + ADAM_EPS) + ADAM_WD * w_ref[...])
    nm_ref[...] = m_new
    nv_ref[...] = v_new


def _adamw_shard(parts, off, transposed, w, m, v, name, n_tiles=1):
    _, r, c = w.shape
    tr = r // n_tiles
    if transposed:
        rows = -(-c // BF16_ROWS) * BF16_ROWS
        window = (N_DEV, rows, tr)
    else:
        assert c == PACK_COLS
        window = (N_DEV, tr, PACK_COLS)

    def kern(p_hbm, w_ref, m_ref, v_ref, g_ref, d_ref, nm_ref, nv_ref, buf, sem):
        i = pl.program_id(0)
        if transposed:
            src = p_hbm.at[:, pl.ds(off, rows), pl.ds(pl.multiple_of(i * tr, LANES), tr)]
        else:
            src = p_hbm.at[:, pl.ds(pl.multiple_of(off + i * tr, BF16_ROWS), tr), :]
        cp = pltpu.make_async_copy(src, buf, sem)
        cp.start()
        cp.wait()
        g = _sum_parts(buf)
        if transposed:
            eye = (lax.broadcasted_iota(jnp.int32, (rows, c), 0) == lax.broadcasted_iota(jnp.int32, (rows, c), 1)).astype(F32)
            g = _hdot_tn(g, eye)
        _adamw_store(g, w_ref, m_ref, v_ref, g_ref, d_ref, nm_ref, nv_ref)

    spec = pl.BlockSpec((None, tr, c), lambda i: (0, i, 0))
    return pl.pallas_call(
        kern,
        out_shape=[jax.ShapeDtypeStruct((1, r, c), F32)] * 4,
        grid=(n_tiles,),
        in_specs=[pl.BlockSpec(memory_space=pl.ANY), spec, spec, spec],
        out_specs=[spec] * 4,
        scratch_shapes=[pltpu.VMEM(window, parts.dtype), pltpu.SemaphoreType.DMA],
        name=name,
        compiler_params=pltpu.CompilerParams(dimension_semantics=("arbitrary",), vmem_limit_bytes=VMEM_LIMIT),
    )(parts, w, m, v)


def _sum_adamw(parts, w, m, v, tr, name):
    _, R, C = parts.shape

    def kern(p_ref, w_ref, m_ref, v_ref, g_ref, d_ref, nm_ref, nv_ref):
        _adamw_store(_sum_parts(p_ref), w_ref, m_ref, v_ref, g_ref, d_ref, nm_ref, nv_ref)

    row_spec = pl.BlockSpec((tr, C), lambda i: (i, 0))
    return pl.pallas_call(
        kern,
        out_shape=[jax.ShapeDtypeStruct((R, C), F32)] * 4,
        grid=(R // tr,),
        in_specs=[pl.BlockSpec((N_DEV, tr, C), lambda i: (0, i, 0)), row_spec, row_spec, row_spec],
        out_specs=[row_spec] * 4,
        name=name,
        compiler_params=pltpu.CompilerParams(dimension_semantics=("arbitrary",), vmem_limit_bytes=VMEM_LIMIT),
    )(parts, w, m, v)


FF_SHARD = D_FF // N_DEV
CONV_SHARD = (SSM_CONV, CONV_DIM // N_DEV)
SHARDS = {"ffn1_w_gate": ((D_MODEL, FF_SHARD), True), "ffn1_w_up": ((D_MODEL, FF_SHARD), True),
          "ffn1_w_down": ((FF_SHARD, D_MODEL), False),
          "ffn2_w_gate": ((D_MODEL, FF_SHARD), True), "ffn2_w_up": ((D_MODEL, FF_SHARD), True),
          "ffn2_w_down": ((FF_SHARD, D_MODEL), False),
          "w_out": ((2 * D_MODEL // N_DEV, D_MODEL), False), "ple_w_gate": ((D_MODEL // N_DEV, D_MODEL), False),
          "w_in": ((D_MODEL, IN_PROJ // N_DEV), True), "ple_w_proj": ((D_PLE, D_MODEL // N_DEV), True),
          "conv_w": (CONV_SHARD, True),
          "conv_w_mid": (CONV_SHARD, True), "conv_w_low": (CONV_SHARD, True)}
BIG = tuple(name for name in SHARDS if not name.startswith("conv_w_"))
SMALL = ("ffn1_norm", "mix_norm", "gm_ln_g", "gm_ln_b", "gm_w_s", "gm_b_s", "gm_out_norm", "conv_b", "dt_bias", "a_log",
         "d_skip", "ssm_norm", "ffn2_norm", "ple_norm", "ple_b_gate", "final_norm")
SMALL_ROWS = 144


def _piece_rows(name):
    shape = SHARDS[name][0]
    return -(-(shape[0] * shape[1]) // PACK_COLS)


def _pad_cols(flat, name):
    pad = _piece_rows(name) * PACK_COLS - flat.shape[-1]
    return flat if pad == 0 else jnp.pad(flat, [(0, 0)] * (flat.ndim - 1) + [(0, pad)])


class _Pack:
    def __init__(self, names, tile_rows):
        self.names, self.tile_rows, self.offsets, off = names, tile_rows, {}, 0
        for name in names:
            self.offsets[name] = off
            off += _piece_rows(name)
        self.rows = -(-off // tile_rows) * tile_rows

    def pack_local(self, vals):
        parts = []
        for name in self.names:
            val = vals[name]
            parts.append(_pad_cols((val.T if SHARDS[name][1] else val).reshape(-1), name))
        flat = jnp.concatenate(parts)
        return jnp.pad(flat, (0, self.rows * PACK_COLS - flat.shape[0])).reshape(self.rows, PACK_COLS)

    def pack_owner_major(self, grads):
        parts, rows = [], 0
        for name in self.names:
            grad, piece_rows = grads[name].astype(BF16), _piece_rows(name)
            if grad.shape != (N_DEV * piece_rows, PACK_COLS):
                grad = _pad_cols(grad.reshape(N_DEV, -1), name)
            parts.append(grad.reshape(N_DEV, piece_rows, PACK_COLS))
            rows += piece_rows
        if rows < self.rows:
            parts.append(jnp.zeros((N_DEV, self.rows - rows, PACK_COLS), BF16))
        return parts[0] if len(parts) == 1 else jnp.concatenate(parts, axis=1)

    def gathered_piece(self, gathered, name):
        shape = SHARDS[name][0]
        rows = gathered[:, self.offsets[name]:self.offsets[name] + _piece_rows(name), :]
        return rows.reshape(N_DEV, -1)[:, :shape[0] * shape[1]]

    def pieces(self, gathered, name):
        return _Pieces(gathered, self.offsets[name], _piece_rows(name))


GATHER_FFN1 = _Pack(("ffn1_w_gate", "ffn1_w_up", "ffn1_w_down"), BF16_ROWS)
GATHER_MIX = _Pack(("w_out", "ple_w_gate", "w_in", "ple_w_proj", "conv_w", "conv_w_mid", "conv_w_low"), BF16_ROWS)
GATHER_FFN2 = _Pack(("ffn2_w_gate", "ffn2_w_up", "ffn2_w_down"), BF16_ROWS)
SCATTER_LATE = _Pack(("ffn2_w_gate", "ffn2_w_up", "ffn2_w_down", "w_out", "ple_w_gate", "ple_w_proj"), BF16_ROWS)
SCATTER_IN = _Pack(("w_in", "conv_w"), BF16_ROWS)
SCATTER_GATE = _Pack(("ffn1_w_gate",), BF16_ROWS)
SCATTER_UP = _Pack(("ffn1_w_up",), BF16_ROWS)
SCATTER_DOWN = _Pack(("ffn1_w_down",), BF16_ROWS)


def _pack_small(vals):
    flat = jnp.concatenate([vals[name].reshape(-1).astype(F32) for name in SMALL])
    return jnp.pad(flat, (0, SMALL_ROWS * PACK_COLS - flat.shape[0])).reshape(SMALL_ROWS, PACK_COLS)


def _unpack_small(packed, shapes):
    out, off = {}, 0
    flat = packed.reshape(-1)
    for name in SMALL:
        n = 1
        for s in shapes[name]:
            n *= s
        out[name] = flat[off:off + n].reshape(shapes[name])
        off += n
    return out


WEIGHTS = ("ffn1_norm", "ffn1_w_gate", "ffn1_w_up", "ffn1_w_down", "mix_norm", "w_in", "gm_ln_g", "gm_ln_b", "gm_w_s",
           "gm_b_s", "gm_out_norm", "conv_w", "conv_b", "dt_bias", "a_log", "d_skip", "ssm_norm", "w_out", "ffn2_norm",
           "ffn2_w_gate", "ffn2_w_up", "ffn2_w_down", "ple_norm", "ple_w_gate", "ple_b_gate", "ple_w_proj", "final_norm")


def _step(x, p, target, w, m, v):
    local = lambda d: {name: d[name][0] for name in BIG}

    shards = {name: val.astype(BF16) for name, val in local(w).items()}
    conv_high = lax.reduce_precision(w["conv_w"][0], 8, 7)
    conv_mid = lax.reduce_precision(w["conv_w"][0] - conv_high, 8, 7)
    shards["conv_w"] = conv_high.astype(BF16)
    shards["conv_w_mid"] = conv_mid.astype(BF16)
    shards["conv_w_low"] = (w["conv_w"][0] - conv_high - conv_mid).astype(BF16)
    g_ffn1 = _comm_alone([_gather_comm(GATHER_FFN1.pack_local(shards))], "gather_ffn1")[0]

    row = lambda name: w[name].reshape(1, -1)
    gm_w_s = w["gm_w_s"][0]
    gm_b_st = jnp.transpose(w["gm_b_s"][0])
    ffn1 = (row("ffn1_norm"),) + tuple(GATHER_FFN1.pieces(g_ffn1, name) for name in GATHER_FFN1.names)
    gm = (row("gm_ln_g"), row("gm_ln_b"), gm_w_s, gm_b_st, row("gm_out_norm"))

    h1, n1, a1, b1, s1, g_mix = _ffn_fwd(x, *ffn1, "ffn1_fwd", comm=_gather_comm(GATHER_MIX.pack_local(shards)))
    w_in_t = GATHER_MIX.gathered_piece(g_mix, "w_in").reshape(IN_PROJ, D_MODEL)
    w_in_t = jnp.concatenate([w_in_t, jnp.zeros((IN_PROJ_PAD - IN_PROJ, D_MODEL), BF16)], axis=0)
    w_proj_t = GATHER_MIX.gathered_piece(g_mix, "ple_w_proj").reshape(D_MODEL, D_PLE)
    conv_w = sum(GATHER_MIX.gathered_piece(g_mix, name).astype(F32) for name in ("conv_w", "conv_w_mid", "conv_w_low"))
    conv_w = conv_w.reshape(CONV_DIM, SSM_CONV).T
    ssd = (row("dt_bias"), row("a_log"), row("d_skip"), row("ssm_norm"))
    w_out = GATHER_MIX.pieces(g_mix, "w_out")

    proj, n2, x16, xc = _mix_in_fwd(h1, row("mix_norm"), w_in_t, conv_w, row("conv_b"))
    ya = _gm_fwd(proj, *gm)
    yb, s_all, g_ffn2 = _ssd_fwd(proj, xc, *ssd, comm=_gather_comm(GATHER_FFN2.pack_local(shards)))
    ffn2 = (row("ffn2_norm"),) + tuple(GATHER_FFN2.pieces(g_ffn2, name) for name in GATHER_FFN2.names)
    h3, n3, a3, b3, s3, h2 = _ffn_fwd(h1, *ffn2, "ffn2_fwd", mixed=(ya, yb, w_out))

    g, gp = {}, {}
    dh3, loss, gp["ple_w_gate"], d_w_proj, g["ple_norm"], g["ple_b_gate"], g["final_norm"] = _tail(
        h3, p, target, row("ple_norm"), GATHER_MIX.pieces(g_mix, "ple_w_gate"), row("ple_b_gate"), w_proj_t,
        row("final_norm"))
    gp["ple_w_proj"] = d_w_proj.T

    dh2, da3, db3, g["ffn2_norm"] = _ffn_dgrad(h2, dh3, a3, b3, *ffn2, "ffn2_dgrad")
    gp["ffn2_w_gate"] = _wgrad(n3, da3, 1408, "ffn2_wgrad_gate", transpose_out=True)
    gp["ffn2_w_up"] = _wgrad(n3, db3, 1408, "ffn2_wgrad_up", transpose_out=True)
    gp["ffn2_w_down"] = _wgrad(s3, dh3, 512, "ffn2_wgrad_down", scale=0.5, bk=1024)

    dya, dyb = _out_proj_dgrad(dh2, w_out)
    gp["w_out"] = jnp.concatenate([_wgrad(ya, dh2, 1024, "w_out_wgrad_a"), _wgrad(yb, dh2, 1024, "w_out_wgrad_b")], axis=0)

    dp_zxd, d_conv_w, g["conv_b"], g["dt_bias"], g["a_log"], g["d_skip"], g["ssm_norm"], parts_late = _ssd_bwd(
        proj, x16, xc, dyb, s_all, conv_w, *ssd, comm=_exchange_comm(SCATTER_LATE.pack_owner_major(gp)))
    gp["conv_w"] = d_conv_w.T
    dp_uv, g["gm_ln_g"], g["gm_ln_b"], g["gm_w_s"], dbst, g["gm_out_norm"] = _gm_bwd(proj, dya, *gm)
    g["gm_b_s"] = jnp.transpose(dbst)

    parts = {}
    gp["w_in"] = jnp.concatenate([_wgrad(n2, dp_uv, 1024, "w_in_wgrad_uv", transpose_out=True),
                                  _wgrad(n2, dp_zxd, 896, "w_in_wgrad_zxd", transpose_out=True)], axis=0)[:IN_PROJ]
    dh1, g["mix_norm"], parts[SCATTER_IN] = _mix_in_dgrad(h1, dh2, dp_uv, dp_zxd, row("mix_norm"), w_in_t,
                                                          comm=_exchange_comm(SCATTER_IN.pack_owner_major(gp)))

    dx, da1, db1, g["ffn1_norm"] = _ffn_dgrad(x, dh1, a1, b1, *ffn1, "ffn1_dgrad")
    gp["ffn1_w_gate"], small_parts = _wgrad(n1, da1, 1408, "ffn1_wgrad_gate", transpose_out=True,
                                            comm=_gather_comm(_pack_small(g)))
    gp["ffn1_w_up"], parts[SCATTER_GATE] = _wgrad(n1, db1, 1408, "ffn1_wgrad_up", transpose_out=True,
                                                  comm=_exchange_comm(SCATTER_GATE.pack_owner_major(gp)))
    gp["ffn1_w_down"], parts[SCATTER_UP] = _wgrad(s1, dh1, 512, "ffn1_wgrad_down", scale=0.5, bk=1024,
                                                  comm=_exchange_comm(SCATTER_UP.pack_owner_major(gp)))
    parts[SCATTER_DOWN] = _comm_alone([_exchange_comm(SCATTER_DOWN.pack_owner_major(gp))], "scatter_ffn1_down")[0]
    parts[SCATTER_LATE] = parts_late

    res_big = {}
    for pack, pack_parts in parts.items():
        for name in pack.names:
            shape, transposed = SHARDS[name]
            if name in ("ple_w_proj", "conv_w"):
                nat = pack.gathered_piece(pack_parts, name).reshape((N_DEV,) + shape[::-1])
                res_big[name] = _sum_adamw(jnp.transpose(nat, (0, 2, 1)), w[name][0], m[name][0], v[name][0], shape[0],
                                           "adamw_" + name)
            elif name == "w_in":
                res_big[name] = _adamw_shard(pack_parts, pack.offsets[name], True, w[name], m[name], v[name],
                                             "adamw_" + name, n_tiles=4)
            else:
                flip = (lambda a: jnp.transpose(a, (0, 2, 1))) if transposed else (lambda a: a)
                res = _adamw_shard(pack_parts, pack.offsets[name], False, flip(w[name]), flip(m[name]), flip(v[name]),
                                   "adamw_" + name, n_tiles=2)
                res_big[name] = [flip(r) for r in res]

    small_shapes = {name: w[name].shape for name in SMALL}
    res_small = _sum_adamw(small_parts, _pack_small(w), _pack_small(m), _pack_small(v), SMALL_ROWS, "adamw_small")
    res_small = [_unpack_small(r, small_shapes) for r in res_small]

    outs = []
    for k in range(4):
        for name in WEIGHTS:
            if name in res_small[k]:
                outs.append(res_small[k][name])
            else:
                outs.append(res_big[name][k].reshape(w[name].shape))
    return loss[0, 0], dx, outs


def kernel(x, p, ffn1_norm, ffn1_w_gate, ffn1_w_up, ffn1_w_down, mix_norm, w_in, gm_ln_g, gm_ln_b, gm_w_s, gm_b_s, gm_out_norm, conv_w, conv_b, dt_bias, a_log, d_skip, ssm_norm, w_out, ffn2_norm, ffn2_w_gate, ffn2_w_up, ffn2_w_down, ple_norm, ple_w_gate, ple_b_gate, ple_w_proj, final_norm, loss_target, m_ffn1_norm, m_ffn1_w_gate, m_ffn1_w_up, m_ffn1_w_down, m_mix_norm, m_w_in, m_gm_ln_g, m_gm_ln_b, m_gm_w_s, m_gm_b_s, m_gm_out_norm, m_conv_w, m_conv_b, m_dt_bias, m_a_log, m_d_skip, m_ssm_norm, m_w_out, m_ffn2_norm, m_ffn2_w_gate, m_ffn2_w_up, m_ffn2_w_down, m_ple_norm, m_ple_w_gate, m_ple_b_gate, m_ple_w_proj, m_final_norm, v_ffn1_norm, v_ffn1_w_gate, v_ffn1_w_up, v_ffn1_w_down, v_mix_norm, v_w_in, v_gm_ln_g, v_gm_ln_b, v_gm_w_s, v_gm_b_s, v_gm_out_norm, v_conv_w, v_conv_b, v_dt_bias, v_a_log, v_d_skip, v_ssm_norm, v_w_out, v_ffn2_norm, v_ffn2_w_gate, v_ffn2_w_up, v_ffn2_w_down, v_ple_norm, v_ple_w_gate, v_ple_b_gate, v_ple_w_proj, v_final_norm):
    args = locals()
    w = {name: args[name] for name in WEIGHTS}
    m = {name: args["m_" + name] for name in WEIGHTS}
    v = {name: args["v_" + name] for name in WEIGHTS}
    loss, dx, outs = _step(x[0], p[0, 0], loss_target[0], w, m, v)
    loss = lax.psum(loss, AXES)
    return (loss, dx[None], *outs)
```

```python
import functools
from typing import NamedTuple

import jax
import jax.numpy as jnp
from jax import lax
from jax.experimental import pallas as pl
from jax.experimental.pallas import tpu as pltpu

F32 = jnp.float32
BF16 = jnp.bfloat16
HIGHEST = lax.Precision.HIGHEST
MESH = pl.DeviceIdType.MESH
AXES = ("x", "y", "c")
N_DEV = 8

D_MODEL = 1024
D_FF = 2816
D_PLE = 256
GM_WIDTH = 1024
GM_HEADS = 8
GM_HEAD_DIM = 128
CHUNK = 128
SSM_WIDTH = 1024
SSM_HEADS = 16
SSM_HEAD_DIM = 64
SSM_GROUPS = 2
SSM_STATE = 128
SSM_CONV = 4
CONV_DIM = SSM_WIDTH + 2 * SSM_GROUPS * SSM_STATE
IN_PROJ = 2 * GM_WIDTH + SSM_WIDTH + CONV_DIM + SSM_HEADS
LANES = 128
BF16_ROWS = 16
F32_ROWS = 8
IN_PROJ_PAD = IN_PROJ - SSM_HEADS + LANES
UV_W = 2 * GM_WIDTH
ZXD_W = IN_PROJ_PAD - UV_W
HALO = 8
EPS = 1e-6

ADAM_LR = 0.001
ADAM_B1 = 0.9
ADAM_B2 = 0.999
ADAM_EPS = 1e-08
ADAM_WD = 0.01
ADAM_STEP = 10

VMEM_LIMIT = 56 * 1024 * 1024
PACK_COLS = 1024


def _rms(x, g):
    return x * lax.rsqrt(jnp.mean(x * x, axis=-1, keepdims=True) + EPS) * g


def _gelu(x):
    return 0.5 * x * (1.0 + lax.erf(x * (2.0 ** -0.5)))


def _silu(x):
    return x * jax.nn.sigmoid(x)


def _dot(a, b):
    return jnp.dot(a.astype(BF16), b.astype(BF16), preferred_element_type=F32)


def _dot_nt(a, b):
    return lax.dot_general(a.astype(BF16), b.astype(BF16), (((1,), (1,)), ((), ())), preferred_element_type=F32)


def _dot_tn(a, b):
    return lax.dot_general(a.astype(BF16), b.astype(BF16), (((0,), (0,)), ((), ())), preferred_element_type=F32)


def _hdot_tn(a, b):
    return lax.dot_general(a, b, (((0,), (0,)), ((), ())), precision=HIGHEST, preferred_element_type=F32)


def _split3(x):
    hi = x.astype(BF16)
    rest = x - hi.astype(F32)
    mid = rest.astype(BF16)
    return hi, mid, (rest - mid.astype(F32)).astype(BF16)


def _exact_dot(x, mask, dims, x_first=True):
    terms = [lax.dot_general(*((t, mask) if x_first else (mask, t)), (dims, ((), ())), preferred_element_type=F32)
             for t in _split3(x)]
    return (terms[0] + terms[1]) + terms[2]


def _mask_product(fwd_dims, fwd_x_first, bwd_dims, bwd_x_first):
    @jax.custom_vjp
    def product(x, mask):
        return _exact_dot(x, mask, fwd_dims, fwd_x_first)

    def fwd(x, mask):
        return product(x, mask), mask

    def bwd(mask, g):
        return _exact_dot(g, mask, bwd_dims, bwd_x_first), jnp.zeros_like(mask)

    product.defvjp(fwd, bwd)
    return product


_widen = _mask_product(((1,), (0,)), True, ((1,), (1,)), True)
_cumsum_rows = _mask_product(((1,), (0,)), False, ((0,), (0,)), False)
_cumsum_cols = _mask_product(((0,), (0,)), True, ((1,), (1,)), False)


class _Pieces(NamedTuple):
    gathered: jax.Array
    row_off: int
    rows: int


class _Comm(NamedTuple):
    phases: object
    src: jax.Array
    dst: jax.ShapeDtypeStruct


def _tiled(body, name, n_steps, tiled_in, full_in, big_in, tiled_out, acc_out, scratch=(), reverse=False, comm=None):
    n_t, n_f, n_b, n_to, n_a = len(tiled_in), len(full_in), len(big_in), len(tiled_out), len(acc_out)
    n_c = 1 if comm else 0

    def row(i):
        return n_steps - 1 - i if reverse else i

    in_specs, args = [], []
    for arr, br, bc, cb in tiled_in:
        if callable(cb):
            in_specs.append(pl.BlockSpec((br, bc), cb))
        else:
            in_specs.append(pl.BlockSpec((br, bc), functools.partial(lambda i, cb: (row(i), cb), cb=cb)))
        args.append(arr)
    for arr in full_in:
        in_specs.append(pl.BlockSpec(arr.shape, functools.partial(lambda i, nd: (0,) * nd, nd=arr.ndim)))
        args.append(arr)
    big_shapes, n_copies = [], 0
    for big in big_in:
        in_specs.append(pl.BlockSpec(memory_space=pl.ANY))
        if isinstance(big, _Pieces):
            args.append(big.gathered)
            big_shapes.append(((N_DEV * big.rows, PACK_COLS), big.gathered.dtype))
            n_copies += N_DEV
        else:
            args.append(big)
            big_shapes.append((big.shape, big.dtype))
            n_copies += 1
    if comm:
        in_specs.append(pl.BlockSpec(memory_space=pl.ANY))
        args.append(comm.src)
    out_specs, out_shape = [], []
    for rows, cols, dt, br in tiled_out:
        out_specs.append(pl.BlockSpec((br, cols), lambda i: (row(i), 0)))
        out_shape.append(jax.ShapeDtypeStruct((rows, cols), dt))
    for shp, dt in acc_out:
        out_specs.append(pl.BlockSpec(shp, functools.partial(lambda i, nd: (0,) * nd, nd=len(shp))))
        out_shape.append(jax.ShapeDtypeStruct(shp, dt))
    if comm:
        out_specs.append(pl.BlockSpec(memory_space=pl.ANY))
        out_shape.append(comm.dst)
    scratch_shapes = [pltpu.VMEM(shp, dt) for shp, dt in big_shapes] + list(scratch)
    if n_copies:
        scratch_shapes.append(pltpu.SemaphoreType.DMA((n_copies,)))
    if comm:
        scratch_shapes += [pltpu.SemaphoreType.DMA((N_DEV - 1,)), pltpu.SemaphoreType.DMA((N_DEV - 1,)), pltpu.SemaphoreType.DMA]

    def kern(*refs):
        n_in = n_t + n_f + n_b + n_c
        ins = refs[: n_t + n_f]
        big_hbm = refs[n_t + n_f : n_t + n_f + n_b]
        outs = refs[n_in : n_in + n_to + n_a]
        rest = refs[n_in + n_to + n_a + n_c :]
        big_vmem, scr = rest[:n_b], rest[n_b:]
        if comm:
            scr, comm_sems = scr[:-3], scr[-3:]
            comm_start, comm_mid, comm_finish = comm.phases(refs[n_in - 1], refs[n_in + n_to + n_a], *comm_sems)
        if n_copies:
            scr, copy_sems = scr[:-1], scr[-1]
        step = pl.program_id(0)

        @pl.when(step == 0)
        def _():
            copies = []
            for big, src, dst in zip(big_in, big_hbm, big_vmem):
                if isinstance(big, _Pieces):
                    for j in range(N_DEV):
                        copies.append((src.at[j, pl.ds(big.row_off, big.rows), :], dst.at[pl.ds(j * big.rows, big.rows), :]))
                else:
                    copies.append((src, dst))
            copies = [pltpu.make_async_copy(a, b, copy_sems.at[k]) for k, (a, b) in enumerate(copies)]
            for cp in copies:
                cp.start()
            for cp in copies:
                cp.wait()
            for acc in outs[n_to:]:
                acc[...] = jnp.zeros(acc.shape, acc.dtype)
            if comm:
                comm_start()

        body(row(step), *ins, *big_vmem, *outs, *scr)
        if comm:
            pl.when(step == (n_steps - 1) // 2)(comm_mid)
            pl.when(step == n_steps - 1)(comm_finish)

    res = pl.pallas_call(
        kern,
        out_shape=out_shape,
        grid=(n_steps,),
        in_specs=in_specs,
        out_specs=out_specs,
        scratch_shapes=scratch_shapes,
        name=name,
        compiler_params=pltpu.CompilerParams(dimension_semantics=("arbitrary",), vmem_limit_bytes=VMEM_LIMIT),
    )(*args)
    return res


FF_CHUNKS = ((0, 1536), (1536, D_FF))
FFN_TM = 256


def _ffn_fwd(h, g, wg_t, wu_t, wd, name, comm=None, mixed=None):
    T = h.shape[0]
    n_mix = 2 if mixed else 0

    def body(i, h_ref, *refs):
        ya_ref, yb_ref = refs[:n_mix] if mixed else (None, None)
        g_ref, wg_ref, wu_ref, wd_ref = refs[n_mix:n_mix + 4]
        o_ref, n_ref, a_ref, b_ref, s_ref = refs[n_mix + 4 + n_mix // 2:n_mix + 9 + n_mix // 2]
        x = h_ref[...]
        if mixed:
            wo_ref, x_ref = refs[n_mix + 4], refs[-1]
            x = (x + jnp.dot(ya_ref[...], wo_ref[:GM_WIDTH, :], preferred_element_type=F32)
                 + jnp.dot(yb_ref[...], wo_ref[GM_WIDTH:, :], preferred_element_type=F32))
            x_ref[...] = x
        n = _rms(x, g_ref[...]).astype(BF16)
        n_ref[...] = n
        f = jnp.zeros(x.shape, F32)
        for lo, hi in FF_CHUNKS:
            a = _dot_nt(n, wg_ref[lo:hi, :])
            b = _dot_nt(n, wu_ref[lo:hi, :])
            s = (_silu(a) * b).astype(BF16)
            a_ref[:, lo:hi] = a.astype(BF16)
            b_ref[:, lo:hi] = b.astype(BF16)
            s_ref[:, lo:hi] = s
            f = f + jnp.dot(s, wd_ref[lo:hi, :], preferred_element_type=F32)
        o_ref[...] = x + 0.5 * f

    tiled_in, big_in = [(h, FFN_TM, D_MODEL, 0)], [wg_t, wu_t, wd]
    tiled_out = [(T, D_MODEL, F32, FFN_TM), (T, D_MODEL, BF16, FFN_TM), (T, D_FF, BF16, FFN_TM), (T, D_FF, BF16, FFN_TM),
                 (T, D_FF, BF16, FFN_TM)]
    if mixed:
        tiled_in += [(mixed[0], FFN_TM, GM_WIDTH, 0), (mixed[1], FFN_TM, SSM_WIDTH, 0)]
        big_in.append(mixed[2])
        tiled_out.append((T, D_MODEL, F32, FFN_TM))
    return _tiled(body, name, T // FFN_TM, tiled_in, [g], big_in, tiled_out, [], comm=comm)


def _ffn_dgrad(h, dout, a16, b16, g, wg_t, wu_t, wd, name):
    T = h.shape[0]

    def body(i, h_ref, do_ref, a_ref, b_ref, g_ref, wg_ref, wu_ref, wd_ref, dh_ref, da_ref, db_ref, dg_ref):
        dout = do_ref[...]
        _, rms_vjp = jax.vjp(_rms, h_ref[...], g_ref[...])
        dfo = (0.5 * dout).astype(BF16)
        dn = jnp.zeros(dout.shape, F32)
        for lo, hi in FF_CHUNKS:
            a = a_ref[:, lo:hi].astype(F32)
            b = b_ref[:, lo:hi].astype(F32)
            sg = jax.nn.sigmoid(a)
            ds = _dot_nt(dfo, wd_ref[lo:hi, :])
            db = (ds * (a * sg)).astype(BF16)
            da = (ds * b * (sg * (1.0 + a * (1.0 - sg)))).astype(BF16)
            dn = dn + _dot(da, wg_ref[lo:hi, :]) + _dot(db, wu_ref[lo:hi, :])
            da_ref[:, lo:hi] = da
            db_ref[:, lo:hi] = db
        dx, dg = rms_vjp(dn)
        dh_ref[...] = dout + dx
        dg_ref[...] += dg

    return _tiled(body, name, T // FFN_TM,
                  [(h, FFN_TM, D_MODEL, 0), (dout, FFN_TM, D_MODEL, 0), (a16, FFN_TM, D_FF, 0), (b16, FFN_TM, D_FF, 0)],
                  [g], [wg_t, wu_t, wd],
                  [(T, D_MODEL, F32, FFN_TM), (T, D_FF, BF16, FFN_TM), (T, D_FF, BF16, FFN_TM)], [((1, D_MODEL), F32)])


def _wgrad(a, b, bn, name, scale=None, transpose_out=False, bk=2048, comm=None):
    T, M = a.shape
    N = b.shape[1]
    bk = min(bk, T)
    assert M % LANES == 0 and N % bn == 0 and T % bk == 0
    n_j, n_k = N // bn, T // bk
    n_c = 1 if comm else 0

    def kern(*refs):
        a_ref, b_ref, o_ref, acc_ref = refs[0], refs[1], refs[2 + n_c], refs[3 + 2 * n_c]
        j, k = pl.program_id(0), pl.program_id(1)
        if comm:
            comm_start, comm_mid, comm_finish = comm.phases(refs[2], refs[4], *refs[6:])
            pl.when((j == 0) & (k == 0))(comm_start)

        @pl.when(k == 0)
        def _():
            acc_ref[...] = jnp.zeros(acc_ref.shape, F32)

        bv = b_ref[...]
        if scale is not None:
            bv = bv * scale
        acc_ref[...] += _dot_tn(a_ref[...], bv)

        @pl.when(k == n_k - 1)
        def _():
            acc = acc_ref[...]
            o_ref[...] = (acc.T if transpose_out else acc).astype(BF16)

        if comm:
            pl.when((j == (n_j - 1) // 2) & (k == n_k - 1))(comm_mid)
            pl.when((j == n_j - 1) & (k == n_k - 1))(comm_finish)

    if transpose_out:
        out_shape, out_spec = (N, M), pl.BlockSpec((bn, M), lambda j, k: (j, 0))
    else:
        out_shape, out_spec = (M, N), pl.BlockSpec((M, bn), lambda j, k: (0, j))
    any_spec = pl.BlockSpec(memory_space=pl.ANY)
    comm_sems = [pltpu.SemaphoreType.DMA((N_DEV - 1,)), pltpu.SemaphoreType.DMA((N_DEV - 1,)), pltpu.SemaphoreType.DMA]
    res = pl.pallas_call(
        kern,
        out_shape=[jax.ShapeDtypeStruct(out_shape, BF16)] + ([comm.dst] if comm else []),
        grid=(n_j, n_k),
        in_specs=[pl.BlockSpec((bk, M), lambda j, k: (k, 0)), pl.BlockSpec((bk, bn), lambda j, k: (k, j))] + [any_spec] * n_c,
        out_specs=[out_spec] + [any_spec] * n_c,
        scratch_shapes=[pltpu.VMEM((M, bn), F32)] + (comm_sems if comm else []),
        name=name,
        compiler_params=pltpu.CompilerParams(dimension_semantics=("arbitrary", "arbitrary"), vmem_limit_bytes=VMEM_LIMIT),
    )(a, b, *([comm.src] if comm else []))
    return res if comm else res[0]


PROJ_TM = 256
UVZ_W = 2 * GM_WIDTH + SSM_WIDTH
PROJ_KEPT = UVZ_W + LANES
Z_BLK = 2 * GM_WIDTH // SSM_WIDTH
DT_BLK = UVZ_W // LANES


def _mix_in_fwd(h, g, w_in_t, conv_w, conv_b):
    T = h.shape[0]

    def body(i, h_ref, g_ref, cw_ref, cb_ref, w_ref, p_ref, n_ref, x_ref, xc_ref, ext_ref):
        @pl.when(i == 0)
        def _():
            ext_ref[0:HALO, :] = jnp.zeros((HALO, CONV_DIM), F32)

        n = _rms(h_ref[...], g_ref[...]).astype(BF16)
        n_ref[...] = n
        proj = _dot_nt(n, w_ref[...])
        p_ref[:, :UVZ_W] = proj[:, :UVZ_W]
        p_ref[:, UVZ_W:] = proj[:, UVZ_W + CONV_DIM:]
        xbc = proj[:, UVZ_W:UVZ_W + CONV_DIM]
        x_ref[...] = xbc.astype(BF16)
        ext_ref[HALO:, :] = xbc
        xc_ref[...] = _conv_taps(ext_ref, cw_ref[...], cb_ref[...], PROJ_TM)
        ext_ref[0:HALO, :] = ext_ref[PROJ_TM:PROJ_TM + HALO, :]

    return _tiled(body, "mix_in_fwd", T // PROJ_TM, [(h, PROJ_TM, D_MODEL, 0)], [g, conv_w, conv_b], [w_in_t],
                  [(T, PROJ_KEPT, F32, PROJ_TM), (T, D_MODEL, BF16, PROJ_TM), (T, CONV_DIM, BF16, PROJ_TM),
                   (T, CONV_DIM, F32, PROJ_TM)], [],
                  scratch=[pltpu.VMEM((HALO + PROJ_TM, CONV_DIM), F32)])


def _mix_in_dgrad(h, dh_in, dp_uv, dp_zxd, g, w_in_t, comm=None):
    T = h.shape[0]

    def body(i, h_ref, dh_ref, duv_ref, dzxd_ref, g_ref, w_ref, o_ref, dg_ref):
        dn = _dot(duv_ref[...], w_ref[:UV_W, :]) + _dot(dzxd_ref[...], w_ref[UV_W:, :])
        _, rms_vjp = jax.vjp(_rms, h_ref[...], g_ref[...])
        dx, dg = rms_vjp(dn)
        o_ref[...] = dh_ref[...] + dx
        dg_ref[...] += dg

    return _tiled(body, "mix_in_dgrad", T // PROJ_TM,
                  [(h, PROJ_TM, D_MODEL, 0), (dh_in, PROJ_TM, D_MODEL, 0), (dp_uv, PROJ_TM, UV_W, 0),
                   (dp_zxd, PROJ_TM, ZXD_W, 0)], [g], [w_in_t],
                  [(T, D_MODEL, F32, PROJ_TM)], [((1, D_MODEL), F32)], comm=comm)


def _gm_chunk(u, v, ln_g, ln_b, b_st, out_g, *w_heads):
    ug = _gelu(u)
    vg = _gelu(v)
    mu = jnp.mean(vg, axis=-1, keepdims=True)
    xc = vg - mu
    vn = xc * lax.rsqrt(jnp.mean(xc * xc, axis=-1, keepdims=True) + EPS) * ln_g + ln_b
    t_idx = lax.broadcasted_iota(jnp.int32, (CHUNK, CHUNK), 0)
    s_idx = lax.broadcasted_iota(jnp.int32, (CHUNK, CHUNK), 1)
    causal = t_idx >= s_idx
    mixed = []
    for hd in range(GM_HEADS):
        wm = jnp.where(causal, w_heads[hd], 0.0)
        cols = slice(hd * GM_HEAD_DIM, (hd + 1) * GM_HEAD_DIM)
        mixed.append(_dot(wm, vn[:, cols]) + b_st[:, hd:hd + 1])
    ya0 = ug * jnp.concatenate(mixed, axis=1)
    return _rms(ya0, out_g)


GM_FWD_CHUNKS = 2


def _gm_fwd(proj, ln_g, ln_b, w_s, b_st, out_g):
    T = proj.shape[0]

    rows = GM_FWD_CHUNKS * CHUNK

    def body(i, u_ref, v_ref, lg_ref, lb_ref, w_ref, bs_ref, og_ref, ya_ref):
        w_heads = [w_ref[hd] for hd in range(GM_HEADS)]
        for c in range(GM_FWD_CHUNKS):
            tok = pl.ds(c * CHUNK, CHUNK)
            ya = _gm_chunk(u_ref[tok, :], v_ref[tok, :], lg_ref[...], lb_ref[...], bs_ref[...], og_ref[...], *w_heads)
            ya_ref[tok, :] = ya.astype(BF16)

    return _tiled(body, "gmlp_fwd", T // rows, [(proj, rows, GM_WIDTH, 0), (proj, rows, GM_WIDTH, 1)],
                  [ln_g, ln_b, w_s, b_st, out_g], [], [(T, GM_WIDTH, BF16, rows)], [])[0]


def _gm_bwd(proj, dh, w_out, ln_g, ln_b, w_s, b_st, out_g):
    T = proj.shape[0]

    def body(i, u_ref, v_ref, dh_ref, lg_ref, lb_ref, w_ref, bs_ref, og_ref, wo_ref, duv_ref, dlg_ref, dlb_ref, dw_ref,
             dbs_ref, dog_ref):
        w_heads = [w_ref[hd] for hd in range(GM_HEADS)]
        _, vjp = jax.vjp(_gm_chunk, u_ref[...], v_ref[...], lg_ref[...], lb_ref[...], bs_ref[...], og_ref[...], *w_heads)
        grads = vjp(_dot_nt(dh_ref[...], wo_ref[:GM_WIDTH, :]))
        duv_ref[:, :GM_WIDTH] = grads[0].astype(BF16)
        duv_ref[:, GM_WIDTH:] = grads[1].astype(BF16)
        dlg_ref[...] += grads[2]
        dlb_ref[...] += grads[3]
        dbs_ref[...] += grads[4]
        dog_ref[...] += grads[5]
        for hd in range(GM_HEADS):
            dw_ref[hd] += grads[6 + hd]

    return _tiled(body, "gmlp_bwd", T // CHUNK,
                  [(proj, CHUNK, GM_WIDTH, 0), (proj, CHUNK, GM_WIDTH, 1), (dh, CHUNK, D_MODEL, 0)],
                  [ln_g, ln_b, w_s, b_st, out_g], [w_out], [(T, UV_W, BF16, CHUNK)],
                  [((1, GM_WIDTH), F32), ((1, GM_WIDTH), F32), ((GM_HEADS, CHUNK, CHUNK), F32),
                   ((CHUNK, GM_HEADS), F32), ((1, GM_WIDTH), F32)])


def _ssd_chunk(xc, z, dtr, s_in, dt_bias, a_log, d_skip, norm_g):
    half = SSM_WIDTH // SSM_GROUPS
    l_idx = lax.broadcasted_iota(jnp.int32, (CHUNK, CHUNK), 0)
    s_idx = lax.broadcasted_iota(jnp.int32, (CHUNK, CHUNK), 1)
    causal = l_idx >= s_idx
    head_of_col = lax.broadcasted_iota(jnp.int32, (SSM_HEADS, SSM_WIDTH), 1) // SSM_HEAD_DIM
    expand = (head_of_col == lax.broadcasted_iota(jnp.int32, (SSM_HEADS, SSM_WIDTH), 0)).astype(BF16)

    xcs = _silu(xc)
    xs = xcs[:, :SSM_WIDTH]
    dt = jax.nn.softplus(dtr + dt_bias)
    adt = dt * (-jnp.exp(a_log))
    acs = _cumsum_rows(adt, causal.astype(BF16))
    acs_t = _cumsum_cols(adt, (l_idx <= s_idx).astype(BF16))
    tot = acs[CHUNK - 1:CHUNK, :]
    dt_w = _widen(dt, expand)
    out_decay_w = _widen(jnp.exp(acs), expand)
    state_decay_w = _widen(jnp.exp(tot - acs), expand)
    chunk_decay_w = _widen(jnp.exp(tot), expand)
    d_skip_w = _widen(d_skip, expand)
    xdt = xs * dt_w
    xdt_decayed = xdt * state_decay_w

    y_diag, y_off, states = [], [], []
    for grp in range(SSM_GROUPS):
        b0 = SSM_WIDTH + grp * SSM_STATE
        c0 = SSM_WIDTH + SSM_GROUPS * SSM_STATE + grp * SSM_STATE
        bm = xcs[:, b0:b0 + SSM_STATE].astype(BF16)
        cm = xcs[:, c0:c0 + SSM_STATE].astype(BF16)
        cb = _dot_nt(cm, bm)
        for k in range(grp * SSM_HEADS // SSM_GROUPS, (grp + 1) * SSM_HEADS // SSM_GROUPS):
            decay = jnp.exp(jnp.where(causal, acs[:, k:k + 1] - acs_t[k:k + 1, :], -jnp.inf))
            y_diag.append(_dot(cb * decay, xdt[:, k * SSM_HEAD_DIM:(k + 1) * SSM_HEAD_DIM]))
        cols = slice(grp * half, (grp + 1) * half)
        states.append(_dot_tn(bm, xdt_decayed[:, cols]))
        y_off.append(_dot(cm, s_in[:, cols]))
    y = jnp.concatenate(y_diag, axis=1) + jnp.concatenate(y_off, axis=1) * out_decay_w + xs * d_skip_w
    s_out = s_in * chunk_decay_w + jnp.concatenate(states, axis=1)
    y = y * _silu(z)
    normed = []
    for grp in range(SSM_GROUPS):
        yg = y[:, grp * half:(grp + 1) * half]
        normed.append(yg * lax.rsqrt(jnp.mean(yg * yg, axis=-1, keepdims=True) + EPS))
    return jnp.concatenate(normed, axis=1) * norm_g, s_out


def _sum_row_tiles(x):
    return x.reshape(x.shape[0] // F32_ROWS, F32_ROWS, x.shape[1]).sum(axis=0)


def _conv_taps(ext_ref, w, b, rows):
    y = b
    for k in range(SSM_CONV):
        y = y + w[k:k + 1, :] * ext_ref[pl.ds(HALO - (SSM_CONV - 1) + k, rows), :]
    return y


def _ssd_fwd(proj, xc, dt_bias, a_log, d_skip, norm_g, comm=None):
    T = proj.shape[0]
    n_chunks = T // CHUNK

    def body(i, z_ref, xc_ref, dt_ref, dtb_ref, al_ref, dsk_ref, ng_ref, yb_ref, sin_ref, st_ref):
        @pl.when(i == 0)
        def _():
            st_ref[...] = jnp.zeros(st_ref.shape, F32)

        s_in = st_ref[...]
        yb, s_out = _ssd_chunk(xc_ref[...], z_ref[...], dt_ref[:, 0:SSM_HEADS], s_in, dtb_ref[...], al_ref[...],
                               dsk_ref[...], ng_ref[...])
        yb_ref[...] = yb.astype(BF16)
        sin_ref[...] = s_in
        st_ref[...] = s_out

    return _tiled(body, "ssd_fwd", n_chunks,
                  [(proj, CHUNK, SSM_WIDTH, Z_BLK), (xc, CHUNK, CONV_DIM, 0), (proj, CHUNK, LANES, DT_BLK)],
                  [dt_bias, a_log, d_skip, norm_g], [],
                  [(T, SSM_WIDTH, BF16, CHUNK), (n_chunks * SSM_STATE, SSM_WIDTH, F32, SSM_STATE)], [],
                  scratch=[pltpu.VMEM((SSM_STATE, SSM_WIDTH), F32)], comm=comm)


def _ssd_bwd(proj, x16, xc, dh, w_out, s_all, conv_w, dt_bias, a_log, d_skip, norm_g, comm=None):
    T = proj.shape[0]
    n_chunks = T // CHUNK

    def body(i, z_ref, x_ref, xc_ref, dt_ref, dh_ref, sin_ref, cw_ref, dtb_ref, al_ref, dsk_ref, ng_ref, wo_ref,
             dzxd_ref, dcw_ref, dcb_ref, ddtb_ref, dal_ref, ddsk_ref, dng_ref, dext_ref, dst_ref, cw_acc, cb_acc):
        @pl.when(i == n_chunks - 1)
        def _():
            dext_ref[CHUNK:, :] = jnp.zeros((HALO, CONV_DIM), F32)
            dst_ref[...] = jnp.zeros(dst_ref.shape, F32)
            cw_acc[...] = jnp.zeros(cw_acc.shape, F32)
            cb_acc[...] = jnp.zeros(cb_acc.shape, F32)

        _, vjp = jax.vjp(_ssd_chunk, xc_ref[...], z_ref[...], dt_ref[:, 0:SSM_HEADS], sin_ref[...], dtb_ref[...], al_ref[...],
                         dsk_ref[...], ng_ref[...])
        dyb = _dot_nt(dh_ref[...], wo_ref[GM_WIDTH:, :])
        dxc, dz, ddtr, ds_in, ddtb, dal, ddsk, dng = vjp((dyb, dst_ref[...]))
        dst_ref[...] = ds_in
        ddtb_ref[...] += ddtb
        dal_ref[...] += dal
        ddsk_ref[...] += ddsk
        dng_ref[...] += dng
        dext_ref[0:CHUNK, :] = dxc
        cw = cw_ref[...]
        x = x_ref[...].astype(F32)
        dx = jnp.zeros((CHUNK, CONV_DIM), F32)
        for k in range(SSM_CONV):
            shifted = dext_ref[pl.ds(SSM_CONV - 1 - k, CHUNK), :]
            dx = dx + cw[k:k + 1, :] * shifted
            cw_acc[k] += _sum_row_tiles(shifted * x)
        cb_acc[...] += _sum_row_tiles(dxc)

        @pl.when(i == 0)
        def _():
            dcw_ref[...] = jnp.sum(cw_acc[...], axis=1)
            dcb_ref[...] = jnp.sum(cb_acc[...], axis=0, keepdims=True)

        dext_ref[CHUNK:, :] = dext_ref[0:HALO, :]
        dzxd_ref[:, 0:SSM_WIDTH] = dz.astype(BF16)
        dzxd_ref[:, SSM_WIDTH:SSM_WIDTH + CONV_DIM] = dx.astype(BF16)
        dzxd_ref[:, SSM_WIDTH + CONV_DIM:] = jnp.concatenate(
            [ddtr, jnp.zeros((CHUNK, LANES - SSM_HEADS), F32)], axis=1).astype(BF16)

    return _tiled(body, "ssd_bwd", n_chunks,
                  [(proj, CHUNK, SSM_WIDTH, Z_BLK), (x16, CHUNK, CONV_DIM, 0), (xc, CHUNK, CONV_DIM, 0),
                   (proj, CHUNK, LANES, DT_BLK), (dh, CHUNK, D_MODEL, 0), (s_all, SSM_STATE, SSM_WIDTH, 0)],
                  [conv_w, dt_bias, a_log, d_skip, norm_g], [w_out],
                  [(T, ZXD_W, BF16, CHUNK)],
                  [((SSM_CONV, CONV_DIM), F32), ((1, CONV_DIM), F32), ((1, SSM_HEADS), F32), ((1, SSM_HEADS), F32),
                   ((1, SSM_HEADS), F32), ((1, SSM_WIDTH), F32)],
                  scratch=[pltpu.VMEM((CHUNK + HALO, CONV_DIM), F32), pltpu.VMEM((SSM_STATE, SSM_WIDTH), F32),
                           pltpu.VMEM((SSM_CONV, F32_ROWS, CONV_DIM), F32), pltpu.VMEM((F32_ROWS, CONV_DIM), F32)],
                  reverse=True, comm=comm)


TAIL_TM = 512


def _tail(h, p, target, ple_norm, w_gate, b_gate, w_proj_t, final_norm):
    T = h.shape[0]

    def head(x, pre, pp, b_g, f_norm, tgt):
        gate = jax.nn.sigmoid(pre + b_g)
        out = _rms(x + gate * pp, f_norm)
        err = out - tgt
        return 0.5 * jnp.sum(jnp.mean(err * err, axis=-1, keepdims=True), axis=0, keepdims=True)

    def body(i, h_ref, p_ref, t_ref, pn_ref, bg_ref, fn_ref, wg_ref, wp_ref, dh_ref, loss_ref, dwg_ref, dwp_ref, dpn_ref,
             dbg_ref, dfn_ref):
        x = h_ref[...]
        n4f, n_vjp = jax.vjp(_rms, x, pn_ref[...])
        n4 = n4f.astype(BF16)
        pre = jnp.dot(n4, wg_ref[...], preferred_element_type=F32)
        p16 = p_ref[...].astype(BF16)
        pp = _dot_nt(p16, wp_ref[...])
        loss, h_vjp = jax.vjp(functools.partial(head, tgt=t_ref[...]), x, pre, pp, bg_ref[...], fn_ref[...])
        dx, dpre, dpp, dbg, dfn = h_vjp(jnp.ones((1, 1), F32))
        dpre16 = dpre.astype(BF16)
        dn4 = _dot_nt(dpre16, wg_ref[...])
        dx2, dpn = n_vjp(dn4)
        dh_ref[...] = dx + dx2
        loss_ref[...] += loss
        dwg_ref[...] += _dot_tn(n4, dpre16)
        dwp_ref[...] += _dot_tn(p16, dpp)
        dpn_ref[...] += dpn
        dbg_ref[...] += dbg
        dfn_ref[...] += dfn

    return _tiled(body, "tail", T // TAIL_TM,
                  [(h, TAIL_TM, D_MODEL, 0), (p, TAIL_TM, D_PLE, 0), (target, TAIL_TM, D_MODEL, 0)],
                  [ple_norm, b_gate, final_norm], [w_gate, w_proj_t],
                  [(T, D_MODEL, F32, TAIL_TM)],
                  [((1, 1), F32), ((D_MODEL, D_MODEL), F32), ((D_PLE, D_MODEL), F32), ((1, D_MODEL), F32),
                   ((1, D_MODEL), F32), ((1, D_MODEL), F32)])


def _gather_phases(x_ref, out_ref, send_sems, recv_sems, local_sem):
    mx, my, mc = lax.axis_index("x"), lax.axis_index("y"), lax.axis_index("c")
    me, sibling = (mx, my, mc), (mx, my, 1 - mc)
    chips = [(1 - mx, my), (mx, 1 - my), (1 - mx, 1 - my)]

    def rows(px, py, pc):
        return out_ref.at[4 * px + 2 * py + pc]

    def copy(k, block, to, src=None):
        return pltpu.make_async_remote_copy(
            src_ref=rows(*block) if src is None else src, dst_ref=rows(*block),
            send_sem=send_sems.at[k], recv_sem=recv_sems.at[k], device_id=to, device_id_type=MESH)

    mine = pltpu.make_async_copy(x_ref, rows(*me), local_sem)
    first = [copy(0, me, sibling, src=x_ref)] + [copy(1 + j, me, (*chip, mc), src=x_ref) for j, chip in enumerate(chips)]
    passed = [copy(4 + j, (*chip, mc), sibling) for j, chip in enumerate(chips)]

    def start():
        mine.start()
        for cp in first:
            cp.start()

    def mid():
        for j, chip in enumerate(chips):
            copy(1 + j, (*chip, mc), me).wait_recv()
            passed[j].start()

    def finish():
        copy(0, sibling, me).wait_recv()
        for j, chip in enumerate(chips):
            copy(4 + j, (*chip, 1 - mc), me).wait_recv()
        for cp in first + passed:
            cp.wait_send()
        mine.wait()

    return start, mid, finish


def _exchange_phases(x_ref, out_ref, send_sems, recv_sems, local_sem):
    mx, my, mc = lax.axis_index("x"), lax.axis_index("y"), lax.axis_index("c")
    me = 4 * mx + 2 * my + mc
    mine = pltpu.make_async_copy(x_ref.at[me], out_ref.at[me], local_sem)
    copies = []
    for k in range(1, N_DEV):
        px = 1 - mx if k & 4 else mx
        py = 1 - my if k & 2 else my
        pc = 1 - mc if k & 1 else mc
        copies.append(pltpu.make_async_remote_copy(
            src_ref=x_ref.at[4 * px + 2 * py + pc], dst_ref=out_ref.at[me], send_sem=send_sems.at[k - 1],
            recv_sem=recv_sems.at[k - 1], device_id=(px, py, pc), device_id_type=MESH))

    def start():
        mine.start()
        for cp in copies:
            cp.start()

    def finish():
        for cp in copies:
            cp.wait_recv()
        for cp in copies:
            cp.wait_send()
        mine.wait()

    return start, lambda: None, finish


def _gather_comm(x):
    return _Comm(_gather_phases, x, jax.ShapeDtypeStruct((N_DEV,) + x.shape, x.dtype))


def _exchange_comm(x):
    return _Comm(_exchange_phases, x, jax.ShapeDtypeStruct(x.shape, x.dtype))


def _comm_alone(comms, name):
    n = len(comms)

    def body(*refs):
        phases = [comm.phases(refs[k], refs[n + k], *refs[2 * n + 3 * k:2 * n + 3 * k + 3]) for k, comm in enumerate(comms)]
        for step in range(3):
            for phase in phases:
                phase[step]()

    any_spec = pl.BlockSpec(memory_space=pl.ANY)
    return pl.pallas_call(
        body,
        out_shape=[comm.dst for comm in comms],
        in_specs=[any_spec] * n,
        out_specs=[any_spec] * n,
        scratch_shapes=[pltpu.SemaphoreType.DMA((N_DEV - 1,)), pltpu.SemaphoreType.DMA((N_DEV - 1,)), pltpu.SemaphoreType.DMA] * n,
        name=name,
    )(*[comm.src for comm in comms])


def _sum_parts(p_ref):
    g = p_ref[0].astype(F32)
    for j in range(1, N_DEV):
        g = g + p_ref[j].astype(F32)
    return g


def _adamw_store(g, w_ref, m_ref, v_ref, g_ref, d_ref, nm_ref, nv_ref):
    m_new = ADAM_B1 * m_ref[...] + (1.0 - ADAM_B1) * g
    v_new = ADAM_B2 * v_ref[...] + (1.0 - ADAM_B2) * jnp.square(g)
    m_hat = m_new / (1.0 - ADAM_B1 ** ADAM_STEP)
    v_hat = v_new / (1.0 - ADAM_B2 ** ADAM_STEP)
    g_ref[...] = g
    d_ref[...] = -ADAM_LR * (m_hat / (jnp.sqrt(v_hat) + ADAM_EPS) + ADAM_WD * w_ref[...])
    nm_ref[...] = m_new
    nv_ref[...] = v_new


def _adamw_shard(parts, off, transposed, w, m, v, name, n_tiles=1):
    _, r, c = w.shape
    tr = r // n_tiles
    if transposed:
        rows = -(-c // BF16_ROWS) * BF16_ROWS
        window = (N_DEV, rows, tr)
    else:
        assert c == PACK_COLS
        window = (N_DEV, tr, PACK_COLS)

    def kern(p_hbm, w_ref, m_ref, v_ref, g_ref, d_ref, nm_ref, nv_ref, buf, sem):
        i = pl.program_id(0)
        if transposed:
            src = p_hbm.at[:, pl.ds(off, rows), pl.ds(pl.multiple_of(i * tr, LANES), tr)]
        else:
            src = p_hbm.at[:, pl.ds(pl.multiple_of(off + i * tr, BF16_ROWS), tr), :]
        cp = pltpu.make_async_copy(src, buf, sem)
        cp.start()
        cp.wait()
        g = _sum_parts(buf)
        if transposed:
            eye = (lax.broadcasted_iota(jnp.int32, (rows, c), 0) == lax.broadcasted_iota(jnp.int32, (rows, c), 1)).astype(F32)
            g = _hdot_tn(g, eye)
        _adamw_store(g, w_ref, m_ref, v_ref, g_ref, d_ref, nm_ref, nv_ref)

    spec = pl.BlockSpec((None, tr, c), lambda i: (0, i, 0))
    return pl.pallas_call(
        kern,
        out_shape=[jax.ShapeDtypeStruct((1, r, c), F32)] * 4,
        grid=(n_tiles,),
        in_specs=[pl.BlockSpec(memory_space=pl.ANY), spec, spec, spec],
        out_specs=[spec] * 4,
        scratch_shapes=[pltpu.VMEM(window, parts.dtype), pltpu.SemaphoreType.DMA],
        name=name,
        compiler_params=pltpu.CompilerParams(dimension_semantics=("arbitrary",), vmem_limit_bytes=VMEM_LIMIT),
    )(parts, w, m, v)


def _sum_adamw(parts, w, m, v, tr, name):
    _, R, C = parts.shape

    def kern(p_ref, w_ref, m_ref, v_ref, g_ref, d_ref, nm_ref, nv_ref):
        _adamw_store(_sum_parts(p_ref), w_ref, m_ref, v_ref, g_ref, d_ref, nm_ref, nv_ref)

    row_spec = pl.BlockSpec((tr, C), lambda i: (i, 0))
    return pl.pallas_call(
        kern,
        out_shape=[jax.ShapeDtypeStruct((R, C), F32)] * 4,
        grid=(R // tr,),
        in_specs=[pl.BlockSpec((N_DEV, tr, C), lambda i: (0, i, 0)), row_spec, row_spec, row_spec],
        out_specs=[row_spec] * 4,
        name=name,
        compiler_params=pltpu.CompilerParams(dimension_semantics=("arbitrary",), vmem_limit_bytes=VMEM_LIMIT),
    )(parts, w, m, v)


FF_SHARD = D_FF // N_DEV
CONV_SHARD = (SSM_CONV, CONV_DIM // N_DEV)
SHARDS = {"ffn1_w_gate": ((D_MODEL, FF_SHARD), True), "ffn1_w_up": ((D_MODEL, FF_SHARD), True),
          "ffn1_w_down": ((FF_SHARD, D_MODEL), False),
          "ffn2_w_gate": ((D_MODEL, FF_SHARD), True), "ffn2_w_up": ((D_MODEL, FF_SHARD), True),
          "ffn2_w_down": ((FF_SHARD, D_MODEL), False),
          "w_out": ((2 * D_MODEL // N_DEV, D_MODEL), False), "ple_w_gate": ((D_MODEL // N_DEV, D_MODEL), False),
          "w_in": ((D_MODEL, IN_PROJ // N_DEV), True), "ple_w_proj": ((D_PLE, D_MODEL // N_DEV), True),
          "conv_w": (CONV_SHARD, True),
          "conv_w_mid": (CONV_SHARD, True), "conv_w_low": (CONV_SHARD, True)}
BIG = tuple(name for name in SHARDS if not name.startswith("conv_w_"))
SMALL = ("ffn1_norm", "mix_norm", "gm_ln_g", "gm_ln_b", "gm_w_s", "gm_b_s", "gm_out_norm", "conv_b", "dt_bias", "a_log",
         "d_skip", "ssm_norm", "ffn2_norm", "ple_norm", "ple_b_gate", "final_norm")
SMALL_ROWS = 144


def _piece_rows(name):
    shape = SHARDS[name][0]
    return -(-(shape[0] * shape[1]) // PACK_COLS)


def _pad_cols(flat, name):
    pad = _piece_rows(name) * PACK_COLS - flat.shape[-1]
    return flat if pad == 0 else jnp.pad(flat, [(0, 0)] * (flat.ndim - 1) + [(0, pad)])


class _Pack:
    def __init__(self, names, tile_rows):
        self.names, self.tile_rows, self.offsets, off = names, tile_rows, {}, 0
        for name in names:
            self.offsets[name] = off
            off += _piece_rows(name)
        self.rows = -(-off // tile_rows) * tile_rows

    def pack_local(self, vals):
        parts = []
        for name in self.names:
            val = vals[name]
            parts.append(_pad_cols((val.T if SHARDS[name][1] else val).reshape(-1), name))
        flat = jnp.concatenate(parts)
        return jnp.pad(flat, (0, self.rows * PACK_COLS - flat.shape[0])).reshape(self.rows, PACK_COLS)

    def pack_owner_major(self, grads):
        parts, rows = [], 0
        for name in self.names:
            grad, piece_rows = grads[name].astype(BF16), _piece_rows(name)
            if grad.shape != (N_DEV * piece_rows, PACK_COLS):
                grad = _pad_cols(grad.reshape(N_DEV, -1), name)
            parts.append(grad.reshape(N_DEV, piece_rows, PACK_COLS))
            rows += piece_rows
        if rows < self.rows:
            parts.append(jnp.zeros((N_DEV, self.rows - rows, PACK_COLS), BF16))
        return parts[0] if len(parts) == 1 else jnp.concatenate(parts, axis=1)

    def gathered_piece(self, gathered, name):
        shape = SHARDS[name][0]
        rows = gathered[:, self.offsets[name]:self.offsets[name] + _piece_rows(name), :]
        return rows.reshape(N_DEV, -1)[:, :shape[0] * shape[1]]

    def pieces(self, gathered, name):
        return _Pieces(gathered, self.offsets[name], _piece_rows(name))


GATHER_FFN1 = _Pack(("ffn1_w_gate", "ffn1_w_up", "ffn1_w_down"), BF16_ROWS)
GATHER_MIX = _Pack(("w_out", "ple_w_gate", "w_in", "ple_w_proj", "conv_w", "conv_w_mid", "conv_w_low"), BF16_ROWS)
GATHER_FFN2 = _Pack(("ffn2_w_gate", "ffn2_w_up", "ffn2_w_down"), BF16_ROWS)
SCATTER_LATE = _Pack(("ffn2_w_gate", "ffn2_w_up", "ffn2_w_down", "w_out", "ple_w_gate", "ple_w_proj"), BF16_ROWS)
SCATTER_IN = _Pack(("w_in", "conv_w"), BF16_ROWS)
SCATTER_GATE = _Pack(("ffn1_w_gate",), BF16_ROWS)
SCATTER_UP = _Pack(("ffn1_w_up",), BF16_ROWS)
SCATTER_DOWN = _Pack(("ffn1_w_down",), BF16_ROWS)


def _pack_small(vals):
    flat = jnp.concatenate([vals[name].reshape(-1).astype(F32) for name in SMALL])
    return jnp.pad(flat, (0, SMALL_ROWS * PACK_COLS - flat.shape[0])).reshape(SMALL_ROWS, PACK_COLS)


def _unpack_small(packed, shapes):
    out, off = {}, 0
    flat = packed.reshape(-1)
    for name in SMALL:
        n = 1
        for s in shapes[name]:
            n *= s
        out[name] = flat[off:off + n].reshape(shapes[name])
        off += n
    return out


WEIGHTS = ("ffn1_norm", "ffn1_w_gate", "ffn1_w_up", "ffn1_w_down", "mix_norm", "w_in", "gm_ln_g", "gm_ln_b", "gm_w_s",
           "gm_b_s", "gm_out_norm", "conv_w", "conv_b", "dt_bias", "a_log", "d_skip", "ssm_norm", "w_out", "ffn2_norm",
           "ffn2_w_gate", "ffn2_w_up", "ffn2_w_down", "ple_norm", "ple_w_gate", "ple_b_gate", "ple_w_proj", "final_norm")


def _step(x, p, target, w, m, v):
    local = lambda d: {name: d[name][0] for name in BIG}

    shards = {name: val.astype(BF16) for name, val in local(w).items()}
    conv_high = lax.reduce_precision(w["conv_w"][0], 8, 7)
    conv_mid = lax.reduce_precision(w["conv_w"][0] - conv_high, 8, 7)
    shards["conv_w"] = conv_high.astype(BF16)
    shards["conv_w_mid"] = conv_mid.astype(BF16)
    shards["conv_w_low"] = (w["conv_w"][0] - conv_high - conv_mid).astype(BF16)
    g_ffn1 = _comm_alone([_gather_comm(GATHER_FFN1.pack_local(shards))], "gather_ffn1")[0]

    row = lambda name: w[name].reshape(1, -1)
    gm_w_s = w["gm_w_s"][0]
    gm_b_st = jnp.transpose(w["gm_b_s"][0])
    ffn1 = (row("ffn1_norm"),) + tuple(GATHER_FFN1.pieces(g_ffn1, name) for name in GATHER_FFN1.names)
    gm = (row("gm_ln_g"), row("gm_ln_b"), gm_w_s, gm_b_st, row("gm_out_norm"))

    h1, n1, a1, b1, s1, g_mix = _ffn_fwd(x, *ffn1, "ffn1_fwd", comm=_gather_comm(GATHER_MIX.pack_local(shards)))
    w_in_t = GATHER_MIX.gathered_piece(g_mix, "w_in").reshape(IN_PROJ, D_MODEL)
    w_in_t = jnp.concatenate([w_in_t, jnp.zeros((IN_PROJ_PAD - IN_PROJ, D_MODEL), BF16)], axis=0)
    w_proj_t = GATHER_MIX.gathered_piece(g_mix, "ple_w_proj").reshape(D_MODEL, D_PLE)
    conv_w = sum(GATHER_MIX.gathered_piece(g_mix, name).astype(F32) for name in ("conv_w", "conv_w_mid", "conv_w_low"))
    conv_w = conv_w.reshape(CONV_DIM, SSM_CONV).T
    ssd = (row("dt_bias"), row("a_log"), row("d_skip"), row("ssm_norm"))
    w_out = GATHER_MIX.pieces(g_mix, "w_out")

    proj, n2, x16, xc = _mix_in_fwd(h1, row("mix_norm"), w_in_t, conv_w, row("conv_b"))
    ya = _gm_fwd(proj, *gm)
    yb, s_all, g_ffn2 = _ssd_fwd(proj, xc, *ssd, comm=_gather_comm(GATHER_FFN2.pack_local(shards)))
    ffn2 = (row("ffn2_norm"),) + tuple(GATHER_FFN2.pieces(g_ffn2, name) for name in GATHER_FFN2.names)
    h3, n3, a3, b3, s3, h2 = _ffn_fwd(h1, *ffn2, "ffn2_fwd", mixed=(ya, yb, w_out))

    g, gp = {}, {}
    dh3, loss, gp["ple_w_gate"], d_w_proj, g["ple_norm"], g["ple_b_gate"], g["final_norm"] = _tail(
        h3, p, target, row("ple_norm"), GATHER_MIX.pieces(g_mix, "ple_w_gate"), row("ple_b_gate"), w_proj_t,
        row("final_norm"))
    gp["ple_w_proj"] = d_w_proj.T

    dh2, da3, db3, g["ffn2_norm"] = _ffn_dgrad(h2, dh3, a3, b3, *ffn2, "ffn2_dgrad")
    gp["ffn2_w_gate"] = _wgrad(n3, da3, 1408, "ffn2_wgrad_gate", transpose_out=True)
    gp["ffn2_w_up"] = _wgrad(n3, db3, 1408, "ffn2_wgrad_up", transpose_out=True)
    gp["ffn2_w_down"] = _wgrad(s3, dh3, 512, "ffn2_wgrad_down", scale=0.5, bk=1024)

    gp["w_out"] = jnp.concatenate([_wgrad(ya, dh2, 1024, "w_out_wgrad_a"), _wgrad(yb, dh2, 1024, "w_out_wgrad_b")], axis=0)

    dp_zxd, d_conv_w, g["conv_b"], g["dt_bias"], g["a_log"], g["d_skip"], g["ssm_norm"], parts_late = _ssd_bwd(
        proj, x16, xc, dh2, w_out, s_all, conv_w, *ssd, comm=_exchange_comm(SCATTER_LATE.pack_owner_major(gp)))
    gp["conv_w"] = d_conv_w.T
    dp_uv, g["gm_ln_g"], g["gm_ln_b"], g["gm_w_s"], dbst, g["gm_out_norm"] = _gm_bwd(proj, dh2, w_out, *gm)
    g["gm_b_s"] = jnp.transpose(dbst)

    parts = {}
    gp["w_in"] = jnp.concatenate([_wgrad(n2, dp_uv, 1024, "w_in_wgrad_uv", transpose_out=True),
                                  _wgrad(n2, dp_zxd, 896, "w_in_wgrad_zxd", transpose_out=True)], axis=0)[:IN_PROJ]
    dh1, g["mix_norm"], parts[SCATTER_IN] = _mix_in_dgrad(h1, dh2, dp_uv, dp_zxd, row("mix_norm"), w_in_t,
                                                          comm=_exchange_comm(SCATTER_IN.pack_owner_major(gp)))

    dx, da1, db1, g["ffn1_norm"] = _ffn_dgrad(x, dh1, a1, b1, *ffn1, "ffn1_dgrad")
    gp["ffn1_w_gate"], small_parts = _wgrad(n1, da1, 1408, "ffn1_wgrad_gate", transpose_out=True,
                                            comm=_gather_comm(_pack_small(g)))
    gp["ffn1_w_up"], parts[SCATTER_GATE] = _wgrad(n1, db1, 1408, "ffn1_wgrad_up", transpose_out=True,
                                                  comm=_exchange_comm(SCATTER_GATE.pack_owner_major(gp)))
    gp["ffn1_w_down"], parts[SCATTER_UP] = _wgrad(s1, dh1, 512, "ffn1_wgrad_down", scale=0.5, bk=1024,
                                                  comm=_exchange_comm(SCATTER_UP.pack_owner_major(gp)))
    parts[SCATTER_DOWN] = _comm_alone([_exchange_comm(SCATTER_DOWN.pack_owner_major(gp))], "scatter_ffn1_down")[0]
    parts[SCATTER_LATE] = parts_late

    res_big = {}
    for pack, pack_parts in parts.items():
        for name in pack.names:
            shape, transposed = SHARDS[name]
            if name in ("ple_w_proj", "conv_w"):
                nat = pack.gathered_piece(pack_parts, name).reshape((N_DEV,) + shape[::-1])
                res_big[name] = _sum_adamw(jnp.transpose(nat, (0, 2, 1)), w[name][0], m[name][0], v[name][0], shape[0],
                                           "adamw_" + name)
            elif name == "w_in":
                res_big[name] = _adamw_shard(pack_parts, pack.offsets[name], True, w[name], m[name], v[name],
                                             "adamw_" + name, n_tiles=4)
            else:
                flip = (lambda a: jnp.transpose(a, (0, 2, 1))) if transposed else (lambda a: a)
                res = _adamw_shard(pack_parts, pack.offsets[name], False, flip(w[name]), flip(m[name]), flip(v[name]),
                                   "adamw_" + name, n_tiles=2)
                res_big[name] = [flip(r) for r in res]

    small_shapes = {name: w[name].shape for name in SMALL}
    res_small = _sum_adamw(small_parts, _pack_small(w), _pack_small(m), _pack_small(v), SMALL_ROWS, "adamw_small")
    res_small = [_unpack_small(r, small_shapes) for r in res_small]

    outs = []
    for k in range(4):
        for name in WEIGHTS:
            if name in res_small[k]:
                outs.append(res_small[k][name])
            else:
                outs.append(res_big[name][k].reshape(w[name].shape))
    return loss[0, 0], dx, outs


def kernel(x, p, ffn1_norm, ffn1_w_gate, ffn1_w_up, ffn1_w_down, mix_norm, w_in, gm_ln_g, gm_ln_b, gm_w_s, gm_b_s, gm_out_norm, conv_w, conv_b, dt_bias, a_log, d_skip, ssm_norm, w_out, ffn2_norm, ffn2_w_gate, ffn2_w_up, ffn2_w_down, ple_norm, ple_w_gate, ple_b_gate, ple_w_proj, final_norm, loss_target, m_ffn1_norm, m_ffn1_w_gate, m_ffn1_w_up, m_ffn1_w_down, m_mix_norm, m_w_in, m_gm_ln_g, m_gm_ln_b, m_gm_w_s, m_gm_b_s, m_gm_out_norm, m_conv_w, m_conv_b, m_dt_bias, m_a_log, m_d_skip, m_ssm_norm, m_w_out, m_ffn2_norm, m_ffn2_w_gate, m_ffn2_w_up, m_ffn2_w_down, m_ple_norm, m_ple_w_gate, m_ple_b_gate, m_ple_w_proj, m_final_norm, v_ffn1_norm, v_ffn1_w_gate, v_ffn1_w_up, v_ffn1_w_down, v_mix_norm, v_w_in, v_gm_ln_g, v_gm_ln_b, v_gm_w_s, v_gm_b_s, v_gm_out_norm, v_conv_w, v_conv_b, v_dt_bias, v_a_log, v_d_skip, v_ssm_norm, v_w_out, v_ffn2_norm, v_ffn2_w_gate, v_ffn2_w_up, v_ffn2_w_down, v_ple_norm, v_ple_w_gate, v_ple_b_gate, v_ple_w_proj, v_final_norm):
    args = locals()
    w = {name: args[name] for name in WEIGHTS}
    m = {name: args["m_" + name] for name in WEIGHTS}
    v = {name: args["v_" + name] for name in WEIGHTS}
    loss, dx, outs = _step(x[0], p[0, 0], loss_target[0], w, m, v)
    loss = lax.psum(loss, AXES)
    return (loss, dx[None], *outs)
```

```python
import functools
from typing import NamedTuple

import jax
import jax.numpy as jnp
from jax import lax
from jax.experimental import pallas as pl
from jax.experimental.pallas import tpu as pltpu

F32 = jnp.float32
BF16 = jnp.bfloat16
HIGHEST = lax.Precision.HIGHEST
MESH = pl.DeviceIdType.MESH
AXES = ("x", "y", "c")
N_DEV = 8

D_MODEL = 1024
D_FF = 2816
D_PLE = 256
GM_WIDTH = 1024
GM_HEADS = 8
GM_HEAD_DIM = 128
CHUNK = 128
SSM_WIDTH = 1024
SSM_HEADS = 16
SSM_HEAD_DIM = 64
SSM_GROUPS = 2
SSM_STATE = 128
SSM_CONV = 4
CONV_DIM = SSM_WIDTH + 2 * SSM_GROUPS * SSM_STATE
IN_PROJ = 2 * GM_WIDTH + SSM_WIDTH + CONV_DIM + SSM_HEADS
LANES = 128
BF16_ROWS = 16
F32_ROWS = 8
IN_PROJ_PAD = IN_PROJ - SSM_HEADS + LANES
UV_W = 2 * GM_WIDTH
ZXD_W = IN_PROJ_PAD - UV_W
HALO = 8
EPS = 1e-6

ADAM_LR = 0.001
ADAM_B1 = 0.9
ADAM_B2 = 0.999
ADAM_EPS = 1e-08
ADAM_WD = 0.01
ADAM_STEP = 10

VMEM_LIMIT = 56 * 1024 * 1024
PACK_COLS = 1024


def _rms(x, g):
    return x * lax.rsqrt(jnp.mean(x * x, axis=-1, keepdims=True) + EPS) * g


def _gelu(x):
    return 0.5 * x * (1.0 + lax.erf(x * (2.0 ** -0.5)))


def _silu(x):
    return x * jax.nn.sigmoid(x)


def _dot(a, b):
    return jnp.dot(a.astype(BF16), b.astype(BF16), preferred_element_type=F32)


def _dot_nt(a, b):
    return lax.dot_general(a.astype(BF16), b.astype(BF16), (((1,), (1,)), ((), ())), preferred_element_type=F32)


def _dot_tn(a, b):
    return lax.dot_general(a.astype(BF16), b.astype(BF16), (((0,), (0,)), ((), ())), preferred_element_type=F32)


def _hdot_tn(a, b):
    return lax.dot_general(a, b, (((0,), (0,)), ((), ())), precision=HIGHEST, preferred_element_type=F32)


def _split3(x):
    hi = x.astype(BF16)
    rest = x - hi.astype(F32)
    mid = rest.astype(BF16)
    return hi, mid, (rest - mid.astype(F32)).astype(BF16)


def _exact_dot(x, mask, dims, x_first=True, n_terms=3):
    terms = [lax.dot_general(*((t, mask) if x_first else (mask, t)), (dims, ((), ())), preferred_element_type=F32)
             for t in _split3(x)[:n_terms]]
    total = terms[0]
    for term in terms[1:]:
        total = total + term
    return total


def _mask_product(fwd_dims, fwd_x_first, bwd_dims, bwd_x_first, bwd_terms=3):
    @jax.custom_vjp
    def product(x, mask):
        return _exact_dot(x, mask, fwd_dims, fwd_x_first)

    def fwd(x, mask):
        return product(x, mask), mask

    def bwd(mask, g):
        return _exact_dot(g, mask, bwd_dims, bwd_x_first, bwd_terms), jnp.zeros_like(mask)

    product.defvjp(fwd, bwd)
    return product


_widen = _mask_product(((1,), (0,)), True, ((1,), (1,)), True, bwd_terms=2)
_cumsum_rows = _mask_product(((1,), (0,)), False, ((0,), (0,)), False)
_cumsum_cols = _mask_product(((0,), (0,)), True, ((1,), (1,)), False)


class _Pieces(NamedTuple):
    gathered: jax.Array
    row_off: int
    rows: int


class _Comm(NamedTuple):
    phases: object
    src: jax.Array
    dst: jax.ShapeDtypeStruct


def _tiled(body, name, n_steps, tiled_in, full_in, big_in, tiled_out, acc_out, scratch=(), reverse=False, comm=None):
    n_t, n_f, n_b, n_to, n_a = len(tiled_in), len(full_in), len(big_in), len(tiled_out), len(acc_out)
    n_c = 1 if comm else 0

    def row(i):
        return n_steps - 1 - i if reverse else i

    in_specs, args = [], []
    for arr, br, bc, cb in tiled_in:
        if callable(cb):
            in_specs.append(pl.BlockSpec((br, bc), cb))
        else:
            in_specs.append(pl.BlockSpec((br, bc), functools.partial(lambda i, cb: (row(i), cb), cb=cb)))
        args.append(arr)
    for arr in full_in:
        in_specs.append(pl.BlockSpec(arr.shape, functools.partial(lambda i, nd: (0,) * nd, nd=arr.ndim)))
        args.append(arr)
    big_shapes, n_copies = [], 0
    for big in big_in:
        in_specs.append(pl.BlockSpec(memory_space=pl.ANY))
        if isinstance(big, _Pieces):
            args.append(big.gathered)
            big_shapes.append(((N_DEV * big.rows, PACK_COLS), big.gathered.dtype))
            n_copies += N_DEV
        else:
            args.append(big)
            big_shapes.append((big.shape, big.dtype))
            n_copies += 1
    if comm:
        in_specs.append(pl.BlockSpec(memory_space=pl.ANY))
        args.append(comm.src)
    out_specs, out_shape = [], []
    for rows, cols, dt, br in tiled_out:
        out_specs.append(pl.BlockSpec((br, cols), lambda i: (row(i), 0)))
        out_shape.append(jax.ShapeDtypeStruct((rows, cols), dt))
    for shp, dt in acc_out:
        out_specs.append(pl.BlockSpec(shp, functools.partial(lambda i, nd: (0,) * nd, nd=len(shp))))
        out_shape.append(jax.ShapeDtypeStruct(shp, dt))
    if comm:
        out_specs.append(pl.BlockSpec(memory_space=pl.ANY))
        out_shape.append(comm.dst)
    scratch_shapes = [pltpu.VMEM(shp, dt) for shp, dt in big_shapes] + list(scratch)
    if n_copies:
        scratch_shapes.append(pltpu.SemaphoreType.DMA((n_copies,)))
    if comm:
        scratch_shapes += [pltpu.SemaphoreType.DMA((N_DEV - 1,)), pltpu.SemaphoreType.DMA((N_DEV - 1,)), pltpu.SemaphoreType.DMA]

    def kern(*refs):
        n_in = n_t + n_f + n_b + n_c
        ins = refs[: n_t + n_f]
        big_hbm = refs[n_t + n_f : n_t + n_f + n_b]
        outs = refs[n_in : n_in + n_to + n_a]
        rest = refs[n_in + n_to + n_a + n_c :]
        big_vmem, scr = rest[:n_b], rest[n_b:]
        if comm:
            scr, comm_sems = scr[:-3], scr[-3:]
            comm_start, comm_mid, comm_finish = comm.phases(refs[n_in - 1], refs[n_in + n_to + n_a], *comm_sems)
        if n_copies:
            scr, copy_sems = scr[:-1], scr[-1]
        step = pl.program_id(0)

        @pl.when(step == 0)
        def _():
            copies = []
            for big, src, dst in zip(big_in, big_hbm, big_vmem):
                if isinstance(big, _Pieces):
                    for j in range(N_DEV):
                        copies.append((src.at[j, pl.ds(big.row_off, big.rows), :], dst.at[pl.ds(j * big.rows, big.rows), :]))
                else:
                    copies.append((src, dst))
            copies = [pltpu.make_async_copy(a, b, copy_sems.at[k]) for k, (a, b) in enumerate(copies)]
            for cp in copies:
                cp.start()
            for cp in copies:
                cp.wait()
            for acc in outs[n_to:]:
                acc[...] = jnp.zeros(acc.shape, acc.dtype)
            if comm:
                comm_start()

        body(row(step), *ins, *big_vmem, *outs, *scr)
        if comm:
            pl.when(step == (n_steps - 1) // 2)(comm_mid)
            pl.when(step == n_steps - 1)(comm_finish)

    res = pl.pallas_call(
        kern,
        out_shape=out_shape,
        grid=(n_steps,),
        in_specs=in_specs,
        out_specs=out_specs,
        scratch_shapes=scratch_shapes,
        name=name,
        compiler_params=pltpu.CompilerParams(dimension_semantics=("arbitrary",), vmem_limit_bytes=VMEM_LIMIT),
    )(*args)
    return res


FF_CHUNKS = ((0, 1536), (1536, D_FF))
FFN_TM = 256


def _ffn_fwd(h, g, wg_t, wu_t, wd, name, comm=None, mixed=None):
    T = h.shape[0]
    n_mix = 2 if mixed else 0

    def body(i, h_ref, *refs):
        ya_ref, yb_ref = refs[:n_mix] if mixed else (None, None)
        g_ref, wg_ref, wu_ref, wd_ref = refs[n_mix:n_mix + 4]
        o_ref, n_ref, a_ref, b_ref, s_ref = refs[n_mix + 4 + n_mix // 2:n_mix + 9 + n_mix // 2]
        x = h_ref[...]
        if mixed:
            wo_ref, x_ref = refs[n_mix + 4], refs[-1]
            x = (x + jnp.dot(ya_ref[...], wo_ref[:GM_WIDTH, :], preferred_element_type=F32)
                 + jnp.dot(yb_ref[...], wo_ref[GM_WIDTH:, :], preferred_element_type=F32))
            x_ref[...] = x
        n = _rms(x, g_ref[...]).astype(BF16)
        n_ref[...] = n
        f = jnp.zeros(x.shape, F32)
        for lo, hi in FF_CHUNKS:
            a = _dot_nt(n, wg_ref[lo:hi, :])
            b = _dot_nt(n, wu_ref[lo:hi, :])
            s = (_silu(a) * b).astype(BF16)
            a_ref[:, lo:hi] = a.astype(BF16)
            b_ref[:, lo:hi] = b.astype(BF16)
            s_ref[:, lo:hi] = s
            f = f + jnp.dot(s, wd_ref[lo:hi, :], preferred_element_type=F32)
        o_ref[...] = x + 0.5 * f

    tiled_in, big_in = [(h, FFN_TM, D_MODEL, 0)], [wg_t, wu_t, wd]
    tiled_out = [(T, D_MODEL, F32, FFN_TM), (T, D_MODEL, BF16, FFN_TM), (T, D_FF, BF16, FFN_TM), (T, D_FF, BF16, FFN_TM),
                 (T, D_FF, BF16, FFN_TM)]
    if mixed:
        tiled_in += [(mixed[0], FFN_TM, GM_WIDTH, 0), (mixed[1], FFN_TM, SSM_WIDTH, 0)]
        big_in.append(mixed[2])
        tiled_out.append((T, D_MODEL, F32, FFN_TM))
    return _tiled(body, name, T // FFN_TM, tiled_in, [g], big_in, tiled_out, [], comm=comm)


def _ffn_dgrad(h, dout, a16, b16, g, wg_t, wu_t, wd, name):
    T = h.shape[0]

    def body(i, h_ref, do_ref, a_ref, b_ref, g_ref, wg_ref, wu_ref, wd_ref, dh_ref, da_ref, db_ref, dg_ref):
        dout = do_ref[...]
        _, rms_vjp = jax.vjp(_rms, h_ref[...], g_ref[...])
        dfo = (0.5 * dout).astype(BF16)
        dn = jnp.zeros(dout.shape, F32)
        for lo, hi in FF_CHUNKS:
            a = a_ref[:, lo:hi].astype(F32)
            b = b_ref[:, lo:hi].astype(F32)
            sg = jax.nn.sigmoid(a)
            ds = _dot_nt(dfo, wd_ref[lo:hi, :])
            db = (ds * (a * sg)).astype(BF16)
            da = (ds * b * (sg * (1.0 + a * (1.0 - sg)))).astype(BF16)
            dn = dn + _dot(da, wg_ref[lo:hi, :]) + _dot(db, wu_ref[lo:hi, :])
            da_ref[:, lo:hi] = da
            db_ref[:, lo:hi] = db
        dx, dg = rms_vjp(dn)
        dh_ref[...] = dout + dx
        dg_ref[...] += dg

    return _tiled(body, name, T // FFN_TM,
                  [(h, FFN_TM, D_MODEL, 0), (dout, FFN_TM, D_MODEL, 0), (a16, FFN_TM, D_FF, 0), (b16, FFN_TM, D_FF, 0)],
                  [g], [wg_t, wu_t, wd],
                  [(T, D_MODEL, F32, FFN_TM), (T, D_FF, BF16, FFN_TM), (T, D_FF, BF16, FFN_TM)], [((1, D_MODEL), F32)])


def _wgrad(a, b, bn, name, scale=None, transpose_out=False, bk=2048, comm=None):
    T, M = a.shape
    N = b.shape[1]
    bk = min(bk, T)
    assert M % LANES == 0 and N % bn == 0 and T % bk == 0
    n_j, n_k = N // bn, T // bk
    n_c = 1 if comm else 0

    def kern(*refs):
        a_ref, b_ref, o_ref, acc_ref = refs[0], refs[1], refs[2 + n_c], refs[3 + 2 * n_c]
        j, k = pl.program_id(0), pl.program_id(1)
        if comm:
            comm_start, comm_mid, comm_finish = comm.phases(refs[2], refs[4], *refs[6:])
            pl.when((j == 0) & (k == 0))(comm_start)

        @pl.when(k == 0)
        def _():
            acc_ref[...] = jnp.zeros(acc_ref.shape, F32)

        bv = b_ref[...]
        if scale is not None:
            bv = bv * scale
        acc_ref[...] += _dot_tn(a_ref[...], bv)

        @pl.when(k == n_k - 1)
        def _():
            acc = acc_ref[...]
            o_ref[...] = (acc.T if transpose_out else acc).astype(BF16)

        if comm:
            pl.when((j == (n_j - 1) // 2) & (k == n_k - 1))(comm_mid)
            pl.when((j == n_j - 1) & (k == n_k - 1))(comm_finish)

    if transpose_out:
        out_shape, out_spec = (N, M), pl.BlockSpec((bn, M), lambda j, k: (j, 0))
    else:
        out_shape, out_spec = (M, N), pl.BlockSpec((M, bn), lambda j, k: (0, j))
    any_spec = pl.BlockSpec(memory_space=pl.ANY)
    comm_sems = [pltpu.SemaphoreType.DMA((N_DEV - 1,)), pltpu.SemaphoreType.DMA((N_DEV - 1,)), pltpu.SemaphoreType.DMA]
    res = pl.pallas_call(
        kern,
        out_shape=[jax.ShapeDtypeStruct(out_shape, BF16)] + ([comm.dst] if comm else []),
        grid=(n_j, n_k),
        in_specs=[pl.BlockSpec((bk, M), lambda j, k: (k, 0)), pl.BlockSpec((bk, bn), lambda j, k: (k, j))] + [any_spec] * n_c,
        out_specs=[out_spec] + [any_spec] * n_c,
        scratch_shapes=[pltpu.VMEM((M, bn), F32)] + (comm_sems if comm else []),
        name=name,
        compiler_params=pltpu.CompilerParams(dimension_semantics=("arbitrary", "arbitrary"), vmem_limit_bytes=VMEM_LIMIT),
    )(a, b, *([comm.src] if comm else []))
    return res if comm else res[0]


PROJ_TM = 512
UVZ_W = 2 * GM_WIDTH + SSM_WIDTH
PROJ_KEPT = UVZ_W + LANES
Z_BLK = 2 * GM_WIDTH // SSM_WIDTH
DT_BLK = UVZ_W // LANES


def _mix_in_fwd(h, g, w_in_t, conv_w, conv_b):
    T = h.shape[0]

    def body(i, h_ref, g_ref, cw_ref, cb_ref, w_ref, p_ref, n_ref, x_ref, xc_ref, ext_ref):
        @pl.when(i == 0)
        def _():
            ext_ref[0:HALO, :] = jnp.zeros((HALO, CONV_DIM), F32)

        n = _rms(h_ref[...], g_ref[...]).astype(BF16)
        n_ref[...] = n
        proj = _dot_nt(n, w_ref[...])
        p_ref[:, :UVZ_W] = proj[:, :UVZ_W]
        p_ref[:, UVZ_W:] = proj[:, UVZ_W + CONV_DIM:]
        xbc = proj[:, UVZ_W:UVZ_W + CONV_DIM]
        x_ref[...] = xbc.astype(BF16)
        ext_ref[HALO:, :] = xbc
        xc_ref[...] = _conv_taps(ext_ref, cw_ref[...], cb_ref[...], PROJ_TM)
        ext_ref[0:HALO, :] = ext_ref[PROJ_TM:PROJ_TM + HALO, :]

    return _tiled(body, "mix_in_fwd", T // PROJ_TM, [(h, PROJ_TM, D_MODEL, 0)], [g, conv_w, conv_b], [w_in_t],
                  [(T, PROJ_KEPT, F32, PROJ_TM), (T, D_MODEL, BF16, PROJ_TM), (T, CONV_DIM, BF16, PROJ_TM),
                   (T, CONV_DIM, F32, PROJ_TM)], [],
                  scratch=[pltpu.VMEM((HALO + PROJ_TM, CONV_DIM), F32)])


def _mix_in_dgrad(h, dh_in, dp_uv, dp_zxd, g, w_in_t, comm=None):
    T = h.shape[0]

    def body(i, h_ref, dh_ref, duv_ref, dzxd_ref, g_ref, w_ref, o_ref, dg_ref):
        dn = _dot(duv_ref[...], w_ref[:UV_W, :]) + _dot(dzxd_ref[...], w_ref[UV_W:, :])
        _, rms_vjp = jax.vjp(_rms, h_ref[...], g_ref[...])
        dx, dg = rms_vjp(dn)
        o_ref[...] = dh_ref[...] + dx
        dg_ref[...] += dg

    return _tiled(body, "mix_in_dgrad", T // PROJ_TM,
                  [(h, PROJ_TM, D_MODEL, 0), (dh_in, PROJ_TM, D_MODEL, 0), (dp_uv, PROJ_TM, UV_W, 0),
                   (dp_zxd, PROJ_TM, ZXD_W, 0)], [g], [w_in_t],
                  [(T, D_MODEL, F32, PROJ_TM)], [((1, D_MODEL), F32)], comm=comm)


def _out_proj_dgrad(dh, w_out):
    T = dh.shape[0]

    def body(i, dh_ref, w_ref, dya_ref, dyb_ref):
        d = dh_ref[...].astype(BF16)
        dya_ref[...] = _dot_nt(d, w_ref[:GM_WIDTH, :])
        dyb_ref[...] = _dot_nt(d, w_ref[GM_WIDTH:, :])

    return _tiled(body, "out_proj_dgrad", T // PROJ_TM, [(dh, PROJ_TM, D_MODEL, 0)], [], [w_out],
                  [(T, GM_WIDTH, F32, PROJ_TM), (T, SSM_WIDTH, F32, PROJ_TM)], [])


def _gm_chunk(u, v, ln_g, ln_b, b_st, out_g, *w_heads):
    ug = _gelu(u)
    vg = _gelu(v)
    mu = jnp.mean(vg, axis=-1, keepdims=True)
    xc = vg - mu
    vn = xc * lax.rsqrt(jnp.mean(xc * xc, axis=-1, keepdims=True) + EPS) * ln_g + ln_b
    t_idx = lax.broadcasted_iota(jnp.int32, (CHUNK, CHUNK), 0)
    s_idx = lax.broadcasted_iota(jnp.int32, (CHUNK, CHUNK), 1)
    causal = t_idx >= s_idx
    mixed = []
    for hd in range(GM_HEADS):
        wm = jnp.where(causal, w_heads[hd], 0.0)
        cols = slice(hd * GM_HEAD_DIM, (hd + 1) * GM_HEAD_DIM)
        mixed.append(_dot(wm, vn[:, cols]) + b_st[:, hd:hd + 1])
    ya0 = ug * jnp.concatenate(mixed, axis=1)
    return _rms(ya0, out_g)


GM_FWD_CHUNKS = 2


def _gm_fwd(proj, ln_g, ln_b, w_s, b_st, out_g):
    T = proj.shape[0]

    rows = GM_FWD_CHUNKS * CHUNK

    def body(i, u_ref, v_ref, lg_ref, lb_ref, w_ref, bs_ref, og_ref, ya_ref):
        w_heads = [w_ref[hd] for hd in range(GM_HEADS)]
        for c in range(GM_FWD_CHUNKS):
            tok = pl.ds(c * CHUNK, CHUNK)
            ya = _gm_chunk(u_ref[tok, :], v_ref[tok, :], lg_ref[...], lb_ref[...], bs_ref[...], og_ref[...], *w_heads)
            ya_ref[tok, :] = ya.astype(BF16)

    return _tiled(body, "gmlp_fwd", T // rows, [(proj, rows, GM_WIDTH, 0), (proj, rows, GM_WIDTH, 1)],
                  [ln_g, ln_b, w_s, b_st, out_g], [], [(T, GM_WIDTH, BF16, rows)], [])[0]


def _gm_bwd(proj, dya, ln_g, ln_b, w_s, b_st, out_g):
    T = proj.shape[0]

    def body(i, u_ref, v_ref, dy_ref, lg_ref, lb_ref, w_ref, bs_ref, og_ref, duv_ref, dlg_ref, dlb_ref, dw_ref, dbs_ref,
             dog_ref):
        w_heads = [w_ref[hd] for hd in range(GM_HEADS)]
        _, vjp = jax.vjp(_gm_chunk, u_ref[...], v_ref[...], lg_ref[...], lb_ref[...], bs_ref[...], og_ref[...], *w_heads)
        grads = vjp(dy_ref[...])
        duv_ref[:, :GM_WIDTH] = grads[0].astype(BF16)
        duv_ref[:, GM_WIDTH:] = grads[1].astype(BF16)
        dlg_ref[...] += grads[2]
        dlb_ref[...] += grads[3]
        dbs_ref[...] += grads[4]
        dog_ref[...] += grads[5]
        for hd in range(GM_HEADS):
            dw_ref[hd] += grads[6 + hd]

    return _tiled(body, "gmlp_bwd", T // CHUNK,
                  [(proj, CHUNK, GM_WIDTH, 0), (proj, CHUNK, GM_WIDTH, 1), (dya, CHUNK, GM_WIDTH, 0)],
                  [ln_g, ln_b, w_s, b_st, out_g], [], [(T, UV_W, BF16, CHUNK)],
                  [((1, GM_WIDTH), F32), ((1, GM_WIDTH), F32), ((GM_HEADS, CHUNK, CHUNK), F32),
                   ((CHUNK, GM_HEADS), F32), ((1, GM_WIDTH), F32)])


def _ssd_chunk(xc, z, dtr, s_in, dt_bias, a_log, d_skip, norm_g):
    half = SSM_WIDTH // SSM_GROUPS
    l_idx = lax.broadcasted_iota(jnp.int32, (CHUNK, CHUNK), 0)
    s_idx = lax.broadcasted_iota(jnp.int32, (CHUNK, CHUNK), 1)
    causal = l_idx >= s_idx
    head_of_col = lax.broadcasted_iota(jnp.int32, (SSM_HEADS, SSM_WIDTH), 1) // SSM_HEAD_DIM
    expand = (head_of_col == lax.broadcasted_iota(jnp.int32, (SSM_HEADS, SSM_WIDTH), 0)).astype(BF16)

    xcs = _silu(xc)
    xs = xcs[:, :SSM_WIDTH]
    dt = jax.nn.softplus(dtr + dt_bias)
    adt = dt * (-jnp.exp(a_log))
    acs = _cumsum_rows(adt, causal.astype(BF16))
    acs_t = _cumsum_cols(adt, (l_idx <= s_idx).astype(BF16))
    tot = acs[CHUNK - 1:CHUNK, :]
    dt_w = _widen(dt, expand)
    out_decay_w = _widen(jnp.exp(acs), expand)
    state_decay_w = _widen(jnp.exp(tot - acs), expand)
    chunk_decay_w = _widen(jnp.exp(tot), expand)
    d_skip_w = _widen(d_skip, expand)
    xdt = xs * dt_w
    xdt_decayed = xdt * state_decay_w

    y_diag, y_off, states = [], [], []
    for grp in range(SSM_GROUPS):
        b0 = SSM_WIDTH + grp * SSM_STATE
        c0 = SSM_WIDTH + SSM_GROUPS * SSM_STATE + grp * SSM_STATE
        bm = xcs[:, b0:b0 + SSM_STATE].astype(BF16)
        cm = xcs[:, c0:c0 + SSM_STATE].astype(BF16)
        cb = _dot_nt(cm, bm)
        for k in range(grp * SSM_HEADS // SSM_GROUPS, (grp + 1) * SSM_HEADS // SSM_GROUPS):
            decay = jnp.exp(jnp.where(causal, acs[:, k:k + 1] - acs_t[k:k + 1, :], -jnp.inf))
            y_diag.append(_dot(cb * decay, xdt[:, k * SSM_HEAD_DIM:(k + 1) * SSM_HEAD_DIM]))
        cols = slice(grp * half, (grp + 1) * half)
        states.append(_dot_tn(bm, xdt_decayed[:, cols]))
        y_off.append(_dot(cm, s_in[:, cols]))
    y = jnp.concatenate(y_diag, axis=1) + jnp.concatenate(y_off, axis=1) * out_decay_w + xs * d_skip_w
    s_out = s_in * chunk_decay_w + jnp.concatenate(states, axis=1)
    y = y * _silu(z)
    normed = []
    for grp in range(SSM_GROUPS):
        yg = y[:, grp * half:(grp + 1) * half]
        normed.append(yg * lax.rsqrt(jnp.mean(yg * yg, axis=-1, keepdims=True) + EPS))
    return jnp.concatenate(normed, axis=1) * norm_g, s_out


def _sum_row_tiles(x):
    return x.reshape(x.shape[0] // F32_ROWS, F32_ROWS, x.shape[1]).sum(axis=0)


def _conv_taps(ext_ref, w, b, rows):
    y = b
    for k in range(SSM_CONV):
        y = y + w[k:k + 1, :] * ext_ref[pl.ds(HALO - (SSM_CONV - 1) + k, rows), :]
    return y


def _ssd_fwd(proj, xc, dt_bias, a_log, d_skip, norm_g, comm=None):
    T = proj.shape[0]
    n_chunks = T // CHUNK

    def body(i, z_ref, xc_ref, dt_ref, dtb_ref, al_ref, dsk_ref, ng_ref, yb_ref, sin_ref, st_ref):
        @pl.when(i == 0)
        def _():
            st_ref[...] = jnp.zeros(st_ref.shape, F32)

        s_in = st_ref[...]
        yb, s_out = _ssd_chunk(xc_ref[...], z_ref[...], dt_ref[:, 0:SSM_HEADS], s_in, dtb_ref[...], al_ref[...],
                               dsk_ref[...], ng_ref[...])
        yb_ref[...] = yb.astype(BF16)
        sin_ref[...] = s_in
        st_ref[...] = s_out

    return _tiled(body, "ssd_fwd", n_chunks,
                  [(proj, CHUNK, SSM_WIDTH, Z_BLK), (xc, CHUNK, CONV_DIM, 0), (proj, CHUNK, LANES, DT_BLK)],
                  [dt_bias, a_log, d_skip, norm_g], [],
                  [(T, SSM_WIDTH, BF16, CHUNK), (n_chunks * SSM_STATE, SSM_WIDTH, F32, SSM_STATE)], [],
                  scratch=[pltpu.VMEM((SSM_STATE, SSM_WIDTH), F32)], comm=comm)


def _ssd_bwd(proj, x16, xc, dyb, s_all, conv_w, dt_bias, a_log, d_skip, norm_g, comm=None):
    T = proj.shape[0]
    n_chunks = T // CHUNK

    def body(i, z_ref, x_ref, xc_ref, dt_ref, dy_ref, sin_ref, cw_ref, dtb_ref, al_ref, dsk_ref, ng_ref,
             dzxd_ref, dcw_ref, dcb_ref, ddtb_ref, dal_ref, ddsk_ref, dng_ref, dext_ref, dst_ref, cw_acc, cb_acc):
        @pl.when(i == n_chunks - 1)
        def _():
            dext_ref[CHUNK:, :] = jnp.zeros((HALO, CONV_DIM), F32)
            dst_ref[...] = jnp.zeros(dst_ref.shape, F32)
            cw_acc[...] = jnp.zeros(cw_acc.shape, F32)
            cb_acc[...] = jnp.zeros(cb_acc.shape, F32)

        _, vjp = jax.vjp(_ssd_chunk, xc_ref[...], z_ref[...], dt_ref[:, 0:SSM_HEADS], sin_ref[...], dtb_ref[...], al_ref[...],
                         dsk_ref[...], ng_ref[...])
        dxc, dz, ddtr, ds_in, ddtb, dal, ddsk, dng = vjp((dy_ref[...], dst_ref[...]))
        dst_ref[...] = ds_in
        ddtb_ref[...] += ddtb
        dal_ref[...] += dal
        ddsk_ref[...] += ddsk
        dng_ref[...] += dng
        dext_ref[0:CHUNK, :] = dxc
        cw = cw_ref[...]
        x = x_ref[...].astype(F32)
        dx = jnp.zeros((CHUNK, CONV_DIM), F32)
        for k in range(SSM_CONV):
            shifted = dext_ref[pl.ds(SSM_CONV - 1 - k, CHUNK), :]
            dx = dx + cw[k:k + 1, :] * shifted
            cw_acc[k] += _sum_row_tiles(shifted * x)
        cb_acc[...] += _sum_row_tiles(dxc)

        @pl.when(i == 0)
        def _():
            dcw_ref[...] = jnp.sum(cw_acc[...], axis=1)
            dcb_ref[...] = jnp.sum(cb_acc[...], axis=0, keepdims=True)

        dext_ref[CHUNK:, :] = dext_ref[0:HALO, :]
        dzxd_ref[:, 0:SSM_WIDTH] = dz.astype(BF16)
        dzxd_ref[:, SSM_WIDTH:SSM_WIDTH + CONV_DIM] = dx.astype(BF16)
        dzxd_ref[:, SSM_WIDTH + CONV_DIM:] = jnp.concatenate(
            [ddtr, jnp.zeros((CHUNK, LANES - SSM_HEADS), F32)], axis=1).astype(BF16)

    return _tiled(body, "ssd_bwd", n_chunks,
                  [(proj, CHUNK, SSM_WIDTH, Z_BLK), (x16, CHUNK, CONV_DIM, 0), (xc, CHUNK, CONV_DIM, 0),
                   (proj, CHUNK, LANES, DT_BLK), (dyb, CHUNK, SSM_WIDTH, 0), (s_all, SSM_STATE, SSM_WIDTH, 0)],
                  [conv_w, dt_bias, a_log, d_skip, norm_g], [],
                  [(T, ZXD_W, BF16, CHUNK)],
                  [((SSM_CONV, CONV_DIM), F32), ((1, CONV_DIM), F32), ((1, SSM_HEADS), F32), ((1, SSM_HEADS), F32),
                   ((1, SSM_HEADS), F32), ((1, SSM_WIDTH), F32)],
                  scratch=[pltpu.VMEM((CHUNK + HALO, CONV_DIM), F32), pltpu.VMEM((SSM_STATE, SSM_WIDTH), F32),
                           pltpu.VMEM((SSM_CONV, F32_ROWS, CONV_DIM), F32), pltpu.VMEM((F32_ROWS, CONV_DIM), F32)],
                  reverse=True, comm=comm)


TAIL_TM = 512


def _tail(h, p, target, ple_norm, w_gate, b_gate, w_proj_t, final_norm):
    T = h.shape[0]

    def head(x, pre, pp, b_g, f_norm, tgt):
        gate = jax.nn.sigmoid(pre + b_g)
        out = _rms(x + gate * pp, f_norm)
        err = out - tgt
        return 0.5 * jnp.sum(jnp.mean(err * err, axis=-1, keepdims=True), axis=0, keepdims=True)

    def body(i, h_ref, p_ref, t_ref, pn_ref, bg_ref, fn_ref, wg_ref, wp_ref, dh_ref, loss_ref, dwg_ref, dwp_ref, dpn_ref,
             dbg_ref, dfn_ref):
        x = h_ref[...]
        n4f, n_vjp = jax.vjp(_rms, x, pn_ref[...])
        n4 = n4f.astype(BF16)
        pre = jnp.dot(n4, wg_ref[...], preferred_element_type=F32)
        p16 = p_ref[...].astype(BF16)
        pp = _dot_nt(p16, wp_ref[...])
        loss, h_vjp = jax.vjp(functools.partial(head, tgt=t_ref[...]), x, pre, pp, bg_ref[...], fn_ref[...])
        dx, dpre, dpp, dbg, dfn = h_vjp(jnp.ones((1, 1), F32))
        dpre16 = dpre.astype(BF16)
        dn4 = _dot_nt(dpre16, wg_ref[...])
        dx2, dpn = n_vjp(dn4)
        dh_ref[...] = dx + dx2
        loss_ref[...] += loss
        dwg_ref[...] += _dot_tn(n4, dpre16)
        dwp_ref[...] += _dot_tn(p16, dpp)
        dpn_ref[...] += dpn
        dbg_ref[...] += dbg
        dfn_ref[...] += dfn

    return _tiled(body, "tail", T // TAIL_TM,
                  [(h, TAIL_TM, D_MODEL, 0), (p, TAIL_TM, D_PLE, 0), (target, TAIL_TM, D_MODEL, 0)],
                  [ple_norm, b_gate, final_norm], [w_gate, w_proj_t],
                  [(T, D_MODEL, F32, TAIL_TM)],
                  [((1, 1), F32), ((D_MODEL, D_MODEL), F32), ((D_PLE, D_MODEL), F32), ((1, D_MODEL), F32),
                   ((1, D_MODEL), F32), ((1, D_MODEL), F32)])


def _gather_phases(x_ref, out_ref, send_sems, recv_sems, local_sem):
    mx, my, mc = lax.axis_index("x"), lax.axis_index("y"), lax.axis_index("c")
    me, sibling = (mx, my, mc), (mx, my, 1 - mc)
    chips = [(1 - mx, my), (mx, 1 - my), (1 - mx, 1 - my)]

    def rows(px, py, pc):
        return out_ref.at[4 * px + 2 * py + pc]

    def copy(k, block, to, src=None):
        return pltpu.make_async_remote_copy(
            src_ref=rows(*block) if src is None else src, dst_ref=rows(*block),
            send_sem=send_sems.at[k], recv_sem=recv_sems.at[k], device_id=to, device_id_type=MESH)

    mine = pltpu.make_async_copy(x_ref, rows(*me), local_sem)
    first = [copy(0, me, sibling, src=x_ref)] + [copy(1 + j, me, (*chip, mc), src=x_ref) for j, chip in enumerate(chips)]
    passed = [copy(4 + j, (*chip, mc), sibling) for j, chip in enumerate(chips)]

    def start():
        mine.start()
        for cp in first:
            cp.start()

    def mid():
        for j, chip in enumerate(chips):
            copy(1 + j, (*chip, mc), me).wait_recv()
            passed[j].start()

    def finish():
        copy(0, sibling, me).wait_recv()
        for j, chip in enumerate(chips):
            copy(4 + j, (*chip, 1 - mc), me).wait_recv()
        for cp in first + passed:
            cp.wait_send()
        mine.wait()

    return start, mid, finish


def _exchange_phases(x_ref, out_ref, send_sems, recv_sems, local_sem):
    mx, my, mc = lax.axis_index("x"), lax.axis_index("y"), lax.axis_index("c")
    me = 4 * mx + 2 * my + mc
    mine = pltpu.make_async_copy(x_ref.at[me], out_ref.at[me], local_sem)
    copies = []
    for k in range(1, N_DEV):
        px = 1 - mx if k & 4 else mx
        py = 1 - my if k & 2 else my
        pc = 1 - mc if k & 1 else mc
        copies.append(pltpu.make_async_remote_copy(
            src_ref=x_ref.at[4 * px + 2 * py + pc], dst_ref=out_ref.at[me], send_sem=send_sems.at[k - 1],
            recv_sem=recv_sems.at[k - 1], device_id=(px, py, pc), device_id_type=MESH))

    def start():
        mine.start()
        for cp in copies:
            cp.start()

    def finish():
        for cp in copies:
            cp.wait_recv()
        for cp in copies:
            cp.wait_send()
        mine.wait()

    return start, lambda: None, finish


def _gather_comm(x):
    return _Comm(_gather_phases, x, jax.ShapeDtypeStruct((N_DEV,) + x.shape, x.dtype))


def _exchange_comm(x):
    return _Comm(_exchange_phases, x, jax.ShapeDtypeStruct(x.shape, x.dtype))


def _comm_alone(comms, name):
    n = len(comms)

    def body(*refs):
        phases = [comm.phases(refs[k], refs[n + k], *refs[2 * n + 3 * k:2 * n + 3 * k + 3]) for k, comm in enumerate(comms)]
        for step in range(3):
            for phase in phases:
                phase[step]()

    any_spec = pl.BlockSpec(memory_space=pl.ANY)
    return pl.pallas_call(
        body,
        out_shape=[comm.dst for comm in comms],
        in_specs=[any_spec] * n,
        out_specs=[any_spec] * n,
        scratch_shapes=[pltpu.SemaphoreType.DMA((N_DEV - 1,)), pltpu.SemaphoreType.DMA((N_DEV - 1,)), pltpu.SemaphoreType.DMA] * n,
        name=name,
    )(*[comm.src for comm in comms])


def _sum_parts(p_ref):
    g = p_ref[0].astype(F32)
    for j in range(1, N_DEV):
        g = g + p_ref[j].astype(F32)
    return g


def _adamw_store(g, w_ref, m_ref, v_ref, g_ref, d_ref, nm_ref, nv_ref):
    m_new = ADAM_B1 * m_ref[...] + (1.0 - ADAM_B1) * g
    v_new = ADAM_B2 * v_ref[...] + (1.0 - ADAM_B2) * jnp.square(g)
    m_hat = m_new / (1.0 - ADAM_B1 ** ADAM_STEP)
    v_hat = v_new / (1.0 - ADAM_B2 ** ADAM_STEP)
    g_ref[...] = g
    d_ref[...] = -ADAM_LR * (m_hat / (jnp.sqrt(v_hat) + ADAM_EPS) + ADAM_WD * w_ref[...])
    nm_ref[...] = m_new
    nv_ref[...] = v_new


def _adamw_shard(parts, off, transposed, w, m, v, name, n_tiles=1):
    _, r, c = w.shape
    tr = r // n_tiles
    if transposed:
        rows = -(-c // BF16_ROWS) * BF16_ROWS
        window = (N_DEV, rows, tr)
    else:
        assert c == PACK_COLS
        window = (N_DEV, tr, PACK_COLS)

    def kern(p_hbm, w_ref, m_ref, v_ref, g_ref, d_ref, nm_ref, nv_ref, buf, sem):
        i = pl.program_id(0)
        if transposed:
            src = p_hbm.at[:, pl.ds(off, rows), pl.ds(pl.multiple_of(i * tr, LANES), tr)]
        else:
            src = p_hbm.at[:, pl.ds(pl.multiple_of(off + i * tr, BF16_ROWS), tr), :]
        cp = pltpu.make_async_copy(src, buf, sem)
        cp.start()
        cp.wait()
        g = _sum_parts(buf)
        if transposed:
            eye = (lax.broadcasted_iota(jnp.int32, (rows, c), 0) == lax.broadcasted_iota(jnp.int32, (rows, c), 1)).astype(F32)
            g = _hdot_tn(g, eye)
        _adamw_store(g, w_ref, m_ref, v_ref, g_ref, d_ref, nm_ref, nv_ref)

    spec = pl.BlockSpec((None, tr, c), lambda i: (0, i, 0))
    return pl.pallas_call(
        kern,
        out_shape=[jax.ShapeDtypeStruct((1, r, c), F32)] * 4,
        grid=(n_tiles,),
        in_specs=[pl.BlockSpec(memory_space=pl.ANY), spec, spec, spec],
        out_specs=[spec] * 4,
        scratch_shapes=[pltpu.VMEM(window, parts.dtype), pltpu.SemaphoreType.DMA],
        name=name,
        compiler_params=pltpu.CompilerParams(dimension_semantics=("arbitrary",), vmem_limit_bytes=VMEM_LIMIT),
    )(parts, w, m, v)


def _sum_adamw(parts, w, m, v, tr, name):
    _, R, C = parts.shape

    def kern(p_ref, w_ref, m_ref, v_ref, g_ref, d_ref, nm_ref, nv_ref):
        _adamw_store(_sum_parts(p_ref), w_ref, m_ref, v_ref, g_ref, d_ref, nm_ref, nv_ref)

    row_spec = pl.BlockSpec((tr, C), lambda i: (i, 0))
    return pl.pallas_call(
        kern,
        out_shape=[jax.ShapeDtypeStruct((R, C), F32)] * 4,
        grid=(R // tr,),
        in_specs=[pl.BlockSpec((N_DEV, tr, C), lambda i: (0, i, 0)), row_spec, row_spec, row_spec],
        out_specs=[row_spec] * 4,
        name=name,
        compiler_params=pltpu.CompilerParams(dimension_semantics=("arbitrary",), vmem_limit_bytes=VMEM_LIMIT),
    )(parts, w, m, v)


FF_SHARD = D_FF // N_DEV
CONV_SHARD = (SSM_CONV, CONV_DIM // N_DEV)
SHARDS = {"ffn1_w_gate": ((D_MODEL, FF_SHARD), True), "ffn1_w_up": ((D_MODEL, FF_SHARD), True),
          "ffn1_w_down": ((FF_SHARD, D_MODEL), False),
          "ffn2_w_gate": ((D_MODEL, FF_SHARD), True), "ffn2_w_up": ((D_MODEL, FF_SHARD), True),
          "ffn2_w_down": ((FF_SHARD, D_MODEL), False),
          "w_out": ((2 * D_MODEL // N_DEV, D_MODEL), False), "ple_w_gate": ((D_MODEL // N_DEV, D_MODEL), False),
          "w_in": ((D_MODEL, IN_PROJ // N_DEV), True), "ple_w_proj": ((D_PLE, D_MODEL // N_DEV), True),
          "conv_w": (CONV_SHARD, True),
          "conv_w_mid": (CONV_SHARD, True), "conv_w_low": (CONV_SHARD, True)}
BIG = tuple(name for name in SHARDS if not name.startswith("conv_w_"))
SMALL = ("ffn1_norm", "mix_norm", "gm_ln_g", "gm_ln_b", "gm_w_s", "gm_b_s", "gm_out_norm", "conv_b", "dt_bias", "a_log",
         "d_skip", "ssm_norm", "ffn2_norm", "ple_norm", "ple_b_gate", "final_norm")
SMALL_ROWS = 144


def _piece_rows(name):
    shape = SHARDS[name][0]
    return -(-(shape[0] * shape[1]) // PACK_COLS)


def _pad_cols(flat, name):
    pad = _piece_rows(name) * PACK_COLS - flat.shape[-1]
    return flat if pad == 0 else jnp.pad(flat, [(0, 0)] * (flat.ndim - 1) + [(0, pad)])


class _Pack:
    def __init__(self, names, tile_rows):
        self.names, self.tile_rows, self.offsets, off = names, tile_rows, {}, 0
        for name in names:
            self.offsets[name] = off
            off += _piece_rows(name)
        self.rows = -(-off // tile_rows) * tile_rows

    def pack_local(self, vals):
        parts = []
        for name in self.names:
            val = vals[name]
            parts.append(_pad_cols((val.T if SHARDS[name][1] else val).reshape(-1), name))
        flat = jnp.concatenate(parts)
        return jnp.pad(flat, (0, self.rows * PACK_COLS - flat.shape[0])).reshape(self.rows, PACK_COLS)

    def pack_owner_major(self, grads):
        parts, rows = [], 0
        for name in self.names:
            grad, piece_rows = grads[name].astype(BF16), _piece_rows(name)
            if grad.shape != (N_DEV * piece_rows, PACK_COLS):
                grad = _pad_cols(grad.reshape(N_DEV, -1), name)
            parts.append(grad.reshape(N_DEV, piece_rows, PACK_COLS))
            rows += piece_rows
        if rows < self.rows:
            parts.append(jnp.zeros((N_DEV, self.rows - rows, PACK_COLS), BF16))
        return parts[0] if len(parts) == 1 else jnp.concatenate(parts, axis=1)

    def gathered_piece(self, gathered, name):
        shape = SHARDS[name][0]
        rows = gathered[:, self.offsets[name]:self.offsets[name] + _piece_rows(name), :]
        return rows.reshape(N_DEV, -1)[:, :shape[0] * shape[1]]

    def pieces(self, gathered, name):
        return _Pieces(gathered, self.offsets[name], _piece_rows(name))


GATHER_FFN1 = _Pack(("ffn1_w_gate", "ffn1_w_up", "ffn1_w_down"), BF16_ROWS)
GATHER_MIX = _Pack(("w_out", "ple_w_gate", "w_in", "ple_w_proj", "conv_w", "conv_w_mid", "conv_w_low"), BF16_ROWS)
GATHER_FFN2 = _Pack(("ffn2_w_gate", "ffn2_w_up", "ffn2_w_down"), BF16_ROWS)
SCATTER_LATE = _Pack(("ffn2_w_gate", "ffn2_w_up", "ffn2_w_down", "w_out", "ple_w_gate", "ple_w_proj"), BF16_ROWS)
SCATTER_IN = _Pack(("w_in", "conv_w"), BF16_ROWS)
SCATTER_GATE = _Pack(("ffn1_w_gate",), BF16_ROWS)
SCATTER_UP = _Pack(("ffn1_w_up",), BF16_ROWS)
SCATTER_DOWN = _Pack(("ffn1_w_down",), BF16_ROWS)


def _pack_small(vals):
    flat = jnp.concatenate([vals[name].reshape(-1).astype(F32) for name in SMALL])
    return jnp.pad(flat, (0, SMALL_ROWS * PACK_COLS - flat.shape[0])).reshape(SMALL_ROWS, PACK_COLS)


def _unpack_small(packed, shapes):
    out, off = {}, 0
    flat = packed.reshape(-1)
    for name in SMALL:
        n = 1
        for s in shapes[name]:
            n *= s
        out[name] = flat[off:off + n].reshape(shapes[name])
        off += n
    return out


WEIGHTS = ("ffn1_norm", "ffn1_w_gate", "ffn1_w_up", "ffn1_w_down", "mix_norm", "w_in", "gm_ln_g", "gm_ln_b", "gm_w_s",
           "gm_b_s", "gm_out_norm", "conv_w", "conv_b", "dt_bias", "a_log", "d_skip", "ssm_norm", "w_out", "ffn2_norm",
           "ffn2_w_gate", "ffn2_w_up", "ffn2_w_down", "ple_norm", "ple_w_gate", "ple_b_gate", "ple_w_proj", "final_norm")


def _step(x, p, target, w, m, v):
    local = lambda d: {name: d[name][0] for name in BIG}

    shards = {name: val.astype(BF16) for name, val in local(w).items()}
    conv_high = lax.reduce_precision(w["conv_w"][0], 8, 7)
    conv_mid = lax.reduce_precision(w["conv_w"][0] - conv_high, 8, 7)
    shards["conv_w"] = conv_high.astype(BF16)
    shards["conv_w_mid"] = conv_mid.astype(BF16)
    shards["conv_w_low"] = (w["conv_w"][0] - conv_high - conv_mid).astype(BF16)
    g_ffn1 = _comm_alone([_gather_comm(GATHER_FFN1.pack_local(shards))], "gather_ffn1")[0]

    row = lambda name: w[name].reshape(1, -1)
    gm_w_s = w["gm_w_s"][0]
    gm_b_st = jnp.transpose(w["gm_b_s"][0])
    ffn1 = (row("ffn1_norm"),) + tuple(GATHER_FFN1.pieces(g_ffn1, name) for name in GATHER_FFN1.names)
    gm = (row("gm_ln_g"), row("gm_ln_b"), gm_w_s, gm_b_st, row("gm_out_norm"))

    h1, n1, a1, b1, s1, g_mix = _ffn_fwd(x, *ffn1, "ffn1_fwd", comm=_gather_comm(GATHER_MIX.pack_local(shards)))
    w_in_t = GATHER_MIX.gathered_piece(g_mix, "w_in").reshape(IN_PROJ, D_MODEL)
    w_in_t = jnp.concatenate([w_in_t, jnp.zeros((IN_PROJ_PAD - IN_PROJ, D_MODEL), BF16)], axis=0)
    w_proj_t = GATHER_MIX.gathered_piece(g_mix, "ple_w_proj").reshape(D_MODEL, D_PLE)
    conv_w = sum(GATHER_MIX.gathered_piece(g_mix, name).astype(F32) for name in ("conv_w", "conv_w_mid", "conv_w_low"))
    conv_w = conv_w.reshape(CONV_DIM, SSM_CONV).T
    ssd = (row("dt_bias"), row("a_log"), row("d_skip"), row("ssm_norm"))
    w_out = GATHER_MIX.pieces(g_mix, "w_out")

    proj, n2, x16, xc = _mix_in_fwd(h1, row("mix_norm"), w_in_t, conv_w, row("conv_b"))
    ya = _gm_fwd(proj, *gm)
    yb, s_all, g_ffn2 = _ssd_fwd(proj, xc, *ssd, comm=_gather_comm(GATHER_FFN2.pack_local(shards)))
    ffn2 = (row("ffn2_norm"),) + tuple(GATHER_FFN2.pieces(g_ffn2, name) for name in GATHER_FFN2.names)
    h3, n3, a3, b3, s3, h2 = _ffn_fwd(h1, *ffn2, "ffn2_fwd", mixed=(ya, yb, w_out))

    g, gp = {}, {}
    dh3, loss, gp["ple_w_gate"], d_w_proj, g["ple_norm"], g["ple_b_gate"], g["final_norm"] = _tail(
        h3, p, target, row("ple_norm"), GATHER_MIX.pieces(g_mix, "ple_w_gate"), row("ple_b_gate"), w_proj_t,
        row("final_norm"))
    gp["ple_w_proj"] = d_w_proj.T

    dh2, da3, db3, g["ffn2_norm"] = _ffn_dgrad(h2, dh3, a3, b3, *ffn2, "ffn2_dgrad")
    gp["ffn2_w_gate"] = _wgrad(n3, da3, 1408, "ffn2_wgrad_gate", transpose_out=True)
    gp["ffn2_w_up"] = _wgrad(n3, db3, 1408, "ffn2_wgrad_up", transpose_out=True)
    gp["ffn2_w_down"] = _wgrad(s3, dh3, 512, "ffn2_wgrad_down", scale=0.5, bk=1024)

    dya, dyb = _out_proj_dgrad(dh2, w_out)
    gp["w_out"] = jnp.concatenate([_wgrad(ya, dh2, 1024, "w_out_wgrad_a"), _wgrad(yb, dh2, 1024, "w_out_wgrad_b")], axis=0)

    dp_zxd, d_conv_w, g["conv_b"], g["dt_bias"], g["a_log"], g["d_skip"], g["ssm_norm"], parts_late = _ssd_bwd(
        proj, x16, xc, dyb, s_all, conv_w, *ssd, comm=_exchange_comm(SCATTER_LATE.pack_owner_major(gp)))
    gp["conv_w"] = d_conv_w.T
    dp_uv, g["gm_ln_g"], g["gm_ln_b"], g["gm_w_s"], dbst, g["gm_out_norm"] = _gm_bwd(proj, dya, *gm)
    g["gm_b_s"] = jnp.transpose(dbst)

    parts = {}
    gp["w_in"] = jnp.concatenate([_wgrad(n2, dp_uv, 1024, "w_in_wgrad_uv", transpose_out=True),
                                  _wgrad(n2, dp_zxd, 896, "w_in_wgrad_zxd", transpose_out=True)], axis=0)[:IN_PROJ]
    dh1, g["mix_norm"], parts[SCATTER_IN] = _mix_in_dgrad(h1, dh2, dp_uv, dp_zxd, row("mix_norm"), w_in_t,
                                                          comm=_exchange_comm(SCATTER_IN.pack_owner_major(gp)))

    dx, da1, db1, g["ffn1_norm"] = _ffn_dgrad(x, dh1, a1, b1, *ffn1, "ffn1_dgrad")
    gp["ffn1_w_gate"], small_parts = _wgrad(n1, da1, 1408, "ffn1_wgrad_gate", transpose_out=True,
                                            comm=_gather_comm(_pack_small(g)))
    gp["ffn1_w_up"], parts[SCATTER_GATE] = _wgrad(n1, db1, 1408, "ffn1_wgrad_up", transpose_out=True,
                                                  comm=_exchange_comm(SCATTER_GATE.pack_owner_major(gp)))
    gp["ffn1_w_down"], parts[SCATTER_UP] = _wgrad(s1, dh1, 512, "ffn1_wgrad_down", scale=0.5, bk=1024,
                                                  comm=_exchange_comm(SCATTER_UP.pack_owner_major(gp)))
    parts[SCATTER_DOWN] = _comm_alone([_exchange_comm(SCATTER_DOWN.pack_owner_major(gp))], "scatter_ffn1_down")[0]
    parts[SCATTER_LATE] = parts_late

    res_big = {}
    for pack, pack_parts in parts.items():
        for name in pack.names:
            shape, transposed = SHARDS[name]
            if name in ("ple_w_proj", "conv_w"):
                nat = pack.gathered_piece(pack_parts, name).reshape((N_DEV,) + shape[::-1])
                res_big[name] = _sum_adamw(jnp.transpose(nat, (0, 2, 1)), w[name][0], m[name][0], v[name][0], shape[0],
                                           "adamw_" + name)
            elif name == "w_in":
                res_big[name] = _adamw_shard(pack_parts, pack.offsets[name], True, w[name], m[name], v[name],
                                             "adamw_" + name, n_tiles=4)
            else:
                flip = (lambda a: jnp.transpose(a, (0, 2, 1))) if transposed else (lambda a: a)
                res = _adamw_shard(pack_parts, pack.offsets[name], False, flip(w[name]), flip(m[name]), flip(v[name]),
                                   "adamw_" + name, n_tiles=2)
                res_big[name] = [flip(r) for r in res]

    small_shapes = {name: w[name].shape for name in SMALL}
    res_small = _sum_adamw(small_parts, _pack_small(w), _pack_small(m), _pack_small(v), SMALL_ROWS, "adamw_small")
    res_small = [_unpack_small(r, small_shapes) for r in res_small]

    outs = []
    for k in range(4):
        for name in WEIGHTS:
            if name in res_small[k]:
                outs.append(res_small[k][name])
            else:
                outs.append(res_big[name][k].reshape(w[name].shape))
    return loss[0, 0], dx, outs


def kernel(x, p, ffn1_norm, ffn1_w_gate, ffn1_w_up, ffn1_w_down, mix_norm, w_in, gm_ln_g, gm_ln_b, gm_w_s, gm_b_s, gm_out_norm, conv_w, conv_b, dt_bias, a_log, d_skip, ssm_norm, w_out, ffn2_norm, ffn2_w_gate, ffn2_w_up, ffn2_w_down, ple_norm, ple_w_gate, ple_b_gate, ple_w_proj, final_norm, loss_target, m_ffn1_norm, m_ffn1_w_gate, m_ffn1_w_up, m_ffn1_w_down, m_mix_norm, m_w_in, m_gm_ln_g, m_gm_ln_b, m_gm_w_s, m_gm_b_s, m_gm_out_norm, m_conv_w, m_conv_b, m_dt_bias, m_a_log, m_d_skip, m_ssm_norm, m_w_out, m_ffn2_norm, m_ffn2_w_gate, m_ffn2_w_up, m_ffn2_w_down, m_ple_norm, m_ple_w_gate, m_ple_b_gate, m_ple_w_proj, m_final_norm, v_ffn1_norm, v_ffn1_w_gate, v_ffn1_w_up, v_ffn1_w_down, v_mix_norm, v_w_in, v_gm_ln_g, v_gm_ln_b, v_gm_w_s, v_gm_b_s, v_gm_out_norm, v_conv_w, v_conv_b, v_dt_bias, v_a_log, v_d_skip, v_ssm_norm, v_w_out, v_ffn2_norm, v_ffn2_w_gate, v_ffn2_w_up, v_ffn2_w_down, v_ple_norm, v_ple_w_gate, v_ple_b_gate, v_ple_w_proj, v_final_norm):
    args = locals()
    w = {name: args[name] for name in WEIGHTS}
    m = {name: args["m_" + name] for name in WEIGHTS}
    v = {name: args["v_" + name] for name in WEIGHTS}
    loss, dx, outs = _step(x[0], p[0, 0], loss_target[0], w, m, v)
    loss = lax.psum(loss, AXES)
    return (loss, dx[None], *outs)
```

```python
import functools
from typing import NamedTuple

import jax
import jax.numpy as jnp
from jax import lax
from jax.experimental import pallas as pl
from jax.experimental.pallas import tpu as pltpu

F32 = jnp.float32
BF16 = jnp.bfloat16
HIGHEST = lax.Precision.HIGHEST
MESH = pl.DeviceIdType.MESH
AXES = ("x", "y", "c")
N_DEV = 8

D_MODEL = 1024
D_FF = 2816
D_PLE = 256
GM_WIDTH = 1024
GM_HEADS = 8
GM_HEAD_DIM = 128
CHUNK = 128
SSM_WIDTH = 1024
SSM_HEADS = 16
SSM_HEAD_DIM = 64
SSM_GROUPS = 2
SSM_STATE = 128
SSM_CONV = 4
CONV_DIM = SSM_WIDTH + 2 * SSM_GROUPS * SSM_STATE
IN_PROJ = 2 * GM_WIDTH + SSM_WIDTH + CONV_DIM + SSM_HEADS
LANES = 128
BF16_ROWS = 16
F32_ROWS = 8
IN_PROJ_PAD = IN_PROJ - SSM_HEADS + LANES
UV_W = 2 * GM_WIDTH
ZXD_W = IN_PROJ_PAD - UV_W
HALO = 8
EPS = 1e-6

ADAM_LR = 0.001
ADAM_B1 = 0.9
ADAM_B2 = 0.999
ADAM_EPS = 1e-08
ADAM_WD = 0.01
ADAM_STEP = 10

VMEM_LIMIT = 56 * 1024 * 1024
PACK_COLS = 1024


def _rms(x, g):
    return x * lax.rsqrt(jnp.mean(x * x, axis=-1, keepdims=True) + EPS) * g


def _gelu(x):
    return 0.5 * x * (1.0 + lax.erf(x * (2.0 ** -0.5)))


def _silu(x):
    return x * jax.nn.sigmoid(x)


def _dot(a, b):
    return jnp.dot(a.astype(BF16), b.astype(BF16), preferred_element_type=F32)


def _dot_nt(a, b):
    return lax.dot_general(a.astype(BF16), b.astype(BF16), (((1,), (1,)), ((), ())), preferred_element_type=F32)


def _dot_tn(a, b):
    return lax.dot_general(a.astype(BF16), b.astype(BF16), (((0,), (0,)), ((), ())), preferred_element_type=F32)


def _hdot_tn(a, b):
    return lax.dot_general(a, b, (((0,), (0,)), ((), ())), precision=HIGHEST, preferred_element_type=F32)


def _split3(x):
    hi = x.astype(BF16)
    rest = x - hi.astype(F32)
    mid = rest.astype(BF16)
    return hi, mid, (rest - mid.astype(F32)).astype(BF16)


def _exact_dot(x, mask, dims, x_first=True, n_terms=3):
    terms = [lax.dot_general(*((t, mask) if x_first else (mask, t)), (dims, ((), ())), preferred_element_type=F32)
             for t in _split3(x)[:n_terms]]
    total = terms[0]
    for term in terms[1:]:
        total = total + term
    return total


def _mask_product(fwd_dims, fwd_x_first, bwd_dims, bwd_x_first, bwd_terms=3):
    @jax.custom_vjp
    def product(x, mask):
        return _exact_dot(x, mask, fwd_dims, fwd_x_first)

    def fwd(x, mask):
        return product(x, mask), mask

    def bwd(mask, g):
        return _exact_dot(g, mask, bwd_dims, bwd_x_first, bwd_terms), jnp.zeros_like(mask)

    product.defvjp(fwd, bwd)
    return product


_widen = _mask_product(((1,), (0,)), True, ((1,), (1,)), True, bwd_terms=2)
_cumsum_rows = _mask_product(((1,), (0,)), False, ((0,), (0,)), False)
_cumsum_cols = _mask_product(((0,), (0,)), True, ((1,), (1,)), False)


class _Pieces(NamedTuple):
    gathered: jax.Array
    row_off: int
    rows: int


class _Comm(NamedTuple):
    phases: object
    src: jax.Array
    dst: jax.ShapeDtypeStruct


def _tiled(body, name, n_steps, tiled_in, full_in, big_in, tiled_out, acc_out, scratch=(), reverse=False, comm=None):
    n_t, n_f, n_b, n_to, n_a = len(tiled_in), len(full_in), len(big_in), len(tiled_out), len(acc_out)
    n_c = 1 if comm else 0

    def row(i):
        return n_steps - 1 - i if reverse else i

    in_specs, args = [], []
    for arr, br, bc, cb in tiled_in:
        if callable(cb):
            in_specs.append(pl.BlockSpec((br, bc), cb))
        else:
            in_specs.append(pl.BlockSpec((br, bc), functools.partial(lambda i, cb: (row(i), cb), cb=cb)))
        args.append(arr)
    for arr in full_in:
        in_specs.append(pl.BlockSpec(arr.shape, functools.partial(lambda i, nd: (0,) * nd, nd=arr.ndim)))
        args.append(arr)
    big_shapes, n_copies = [], 0
    for big in big_in:
        in_specs.append(pl.BlockSpec(memory_space=pl.ANY))
        if isinstance(big, _Pieces):
            args.append(big.gathered)
            big_shapes.append(((N_DEV * big.rows, PACK_COLS), big.gathered.dtype))
            n_copies += N_DEV
        else:
            args.append(big)
            big_shapes.append((big.shape, big.dtype))
            n_copies += 1
    if comm:
        in_specs.append(pl.BlockSpec(memory_space=pl.ANY))
        args.append(comm.src)
    out_specs, out_shape = [], []
    for rows, cols, dt, br in tiled_out:
        out_specs.append(pl.BlockSpec((br, cols), lambda i: (row(i), 0)))
        out_shape.append(jax.ShapeDtypeStruct((rows, cols), dt))
    for shp, dt in acc_out:
        out_specs.append(pl.BlockSpec(shp, functools.partial(lambda i, nd: (0,) * nd, nd=len(shp))))
        out_shape.append(jax.ShapeDtypeStruct(shp, dt))
    if comm:
        out_specs.append(pl.BlockSpec(memory_space=pl.ANY))
        out_shape.append(comm.dst)
    scratch_shapes = [pltpu.VMEM(shp, dt) for shp, dt in big_shapes] + list(scratch)
    if n_copies:
        scratch_shapes.append(pltpu.SemaphoreType.DMA((n_copies,)))
    if comm:
        scratch_shapes += [pltpu.SemaphoreType.DMA((N_DEV - 1,)), pltpu.SemaphoreType.DMA((N_DEV - 1,)), pltpu.SemaphoreType.DMA]

    def kern(*refs):
        n_in = n_t + n_f + n_b + n_c
        ins = refs[: n_t + n_f]
        big_hbm = refs[n_t + n_f : n_t + n_f + n_b]
        outs = refs[n_in : n_in + n_to + n_a]
        rest = refs[n_in + n_to + n_a + n_c :]
        big_vmem, scr = rest[:n_b], rest[n_b:]
        if comm:
            scr, comm_sems = scr[:-3], scr[-3:]
            comm_start, comm_mid, comm_finish = comm.phases(refs[n_in - 1], refs[n_in + n_to + n_a], *comm_sems)
        if n_copies:
            scr, copy_sems = scr[:-1], scr[-1]
        step = pl.program_id(0)

        @pl.when(step == 0)
        def _():
            copies = []
            for big, src, dst in zip(big_in, big_hbm, big_vmem):
                if isinstance(big, _Pieces):
                    for j in range(N_DEV):
                        copies.append((src.at[j, pl.ds(big.row_off, big.rows), :], dst.at[pl.ds(j * big.rows, big.rows), :]))
                else:
                    copies.append((src, dst))
            copies = [pltpu.make_async_copy(a, b, copy_sems.at[k]) for k, (a, b) in enumerate(copies)]
            for cp in copies:
                cp.start()
            for cp in copies:
                cp.wait()
            for acc in outs[n_to:]:
                acc[...] = jnp.zeros(acc.shape, acc.dtype)
            if comm:
                comm_start()

        body(row(step), *ins, *big_vmem, *outs, *scr)
        if comm:
            pl.when(step == (n_steps - 1) // 2)(comm_mid)
            pl.when(step == n_steps - 1)(comm_finish)

    res = pl.pallas_call(
        kern,
        out_shape=out_shape,
        grid=(n_steps,),
        in_specs=in_specs,
        out_specs=out_specs,
        scratch_shapes=scratch_shapes,
        name=name,
        compiler_params=pltpu.CompilerParams(dimension_semantics=("arbitrary",), vmem_limit_bytes=VMEM_LIMIT),
    )(*args)
    return res


FF_CHUNKS = ((0, 1536), (1536, D_FF))
FF_CHUNKS_3 = ((0, 1024), (1024, 2048), (2048, D_FF))
FFN_TM = 256


def _ffn_fwd(h, g, wg_t, wu_t, wd, name, comm=None, mixed=None, chunks=FF_CHUNKS):
    T = h.shape[0]
    n_mix = 2 if mixed else 0

    def body(i, h_ref, *refs):
        ya_ref, yb_ref = refs[:n_mix] if mixed else (None, None)
        g_ref, wg_ref, wu_ref, wd_ref = refs[n_mix:n_mix + 4]
        o_ref, n_ref, a_ref, b_ref, s_ref = refs[n_mix + 4 + n_mix // 2:n_mix + 9 + n_mix // 2]
        x = h_ref[...]
        if mixed:
            wo_ref, x_ref = refs[n_mix + 4], refs[-1]
            x = (x + jnp.dot(ya_ref[...], wo_ref[:GM_WIDTH, :], preferred_element_type=F32)
                 + jnp.dot(yb_ref[...], wo_ref[GM_WIDTH:, :], preferred_element_type=F32))
            x_ref[...] = x
        n = _rms(x, g_ref[...]).astype(BF16)
        n_ref[...] = n
        f = jnp.zeros(x.shape, F32)
        for lo, hi in chunks:
            a = _dot_nt(n, wg_ref[lo:hi, :])
            b = _dot_nt(n, wu_ref[lo:hi, :])
            s = (_silu(a) * b).astype(BF16)
            a_ref[:, lo:hi] = a.astype(BF16)
            b_ref[:, lo:hi] = b.astype(BF16)
            s_ref[:, lo:hi] = s
            f = f + jnp.dot(s, wd_ref[lo:hi, :], preferred_element_type=F32)
        o_ref[...] = x + 0.5 * f

    tiled_in, big_in = [(h, FFN_TM, D_MODEL, 0)], [wg_t, wu_t, wd]
    tiled_out = [(T, D_MODEL, F32, FFN_TM), (T, D_MODEL, BF16, FFN_TM), (T, D_FF, BF16, FFN_TM), (T, D_FF, BF16, FFN_TM),
                 (T, D_FF, BF16, FFN_TM)]
    if mixed:
        tiled_in += [(mixed[0], FFN_TM, GM_WIDTH, 0), (mixed[1], FFN_TM, SSM_WIDTH, 0)]
        big_in.append(mixed[2])
        tiled_out.append((T, D_MODEL, F32, FFN_TM))
    return _tiled(body, name, T // FFN_TM, tiled_in, [g], big_in, tiled_out, [], comm=comm)


def _ffn_dgrad(h, dout, a16, b16, g, wg_t, wu_t, wd, name, chunks=FF_CHUNKS):
    T = h.shape[0]

    def body(i, h_ref, do_ref, a_ref, b_ref, g_ref, wg_ref, wu_ref, wd_ref, dh_ref, da_ref, db_ref, dg_ref):
        dout = do_ref[...]
        _, rms_vjp = jax.vjp(_rms, h_ref[...], g_ref[...])
        dfo = (0.5 * dout).astype(BF16)
        dn = jnp.zeros(dout.shape, F32)
        for lo, hi in chunks:
            a = a_ref[:, lo:hi].astype(F32)
            b = b_ref[:, lo:hi].astype(F32)
            sg = jax.nn.sigmoid(a)
            ds = _dot_nt(dfo, wd_ref[lo:hi, :])
            db = (ds * (a * sg)).astype(BF16)
            da = (ds * b * (sg * (1.0 + a * (1.0 - sg)))).astype(BF16)
            dn = dn + _dot(da, wg_ref[lo:hi, :]) + _dot(db, wu_ref[lo:hi, :])
            da_ref[:, lo:hi] = da
            db_ref[:, lo:hi] = db
        dx, dg = rms_vjp(dn)
        dh_ref[...] = dout + dx
        dg_ref[...] += dg

    return _tiled(body, name, T // FFN_TM,
                  [(h, FFN_TM, D_MODEL, 0), (dout, FFN_TM, D_MODEL, 0), (a16, FFN_TM, D_FF, 0), (b16, FFN_TM, D_FF, 0)],
                  [g], [wg_t, wu_t, wd],
                  [(T, D_MODEL, F32, FFN_TM), (T, D_FF, BF16, FFN_TM), (T, D_FF, BF16, FFN_TM)], [((1, D_MODEL), F32)])


def _wgrad(a, b, bn, name, scale=None, transpose_out=False, bk=2048, comm=None):
    T, M = a.shape
    N = b.shape[1]
    bk = min(bk, T)
    assert M % LANES == 0 and N % bn == 0 and T % bk == 0
    n_j, n_k = N // bn, T // bk
    n_c = 1 if comm else 0

    def kern(*refs):
        a_ref, b_ref, o_ref, acc_ref = refs[0], refs[1], refs[2 + n_c], refs[3 + 2 * n_c]
        j, k = pl.program_id(0), pl.program_id(1)
        if comm:
            comm_start, comm_mid, comm_finish = comm.phases(refs[2], refs[4], *refs[6:])
            pl.when((j == 0) & (k == 0))(comm_start)

        @pl.when(k == 0)
        def _():
            acc_ref[...] = jnp.zeros(acc_ref.shape, F32)

        bv = b_ref[...]
        if scale is not None:
            bv = bv * scale
        acc_ref[...] += _dot_tn(a_ref[...], bv)

        @pl.when(k == n_k - 1)
        def _():
            acc = acc_ref[...]
            o_ref[...] = (acc.T if transpose_out else acc).astype(BF16)

        if comm:
            pl.when((j == (n_j - 1) // 2) & (k == n_k - 1))(comm_mid)
            pl.when((j == n_j - 1) & (k == n_k - 1))(comm_finish)

    if transpose_out:
        out_shape, out_spec = (N, M), pl.BlockSpec((bn, M), lambda j, k: (j, 0))
    else:
        out_shape, out_spec = (M, N), pl.BlockSpec((M, bn), lambda j, k: (0, j))
    any_spec = pl.BlockSpec(memory_space=pl.ANY)
    comm_sems = [pltpu.SemaphoreType.DMA((N_DEV - 1,)), pltpu.SemaphoreType.DMA((N_DEV - 1,)), pltpu.SemaphoreType.DMA]
    res = pl.pallas_call(
        kern,
        out_shape=[jax.ShapeDtypeStruct(out_shape, BF16)] + ([comm.dst] if comm else []),
        grid=(n_j, n_k),
        in_specs=[pl.BlockSpec((bk, M), lambda j, k: (k, 0)), pl.BlockSpec((bk, bn), lambda j, k: (k, j))] + [any_spec] * n_c,
        out_specs=[out_spec] + [any_spec] * n_c,
        scratch_shapes=[pltpu.VMEM((M, bn), F32)] + (comm_sems if comm else []),
        name=name,
        compiler_params=pltpu.CompilerParams(dimension_semantics=("arbitrary", "arbitrary"), vmem_limit_bytes=VMEM_LIMIT),
    )(a, b, *([comm.src] if comm else []))
    return res if comm else res[0]


PROJ_TM = 512
UVZ_W = 2 * GM_WIDTH + SSM_WIDTH
PROJ_KEPT = UVZ_W + LANES
Z_BLK = 2 * GM_WIDTH // SSM_WIDTH
DT_BLK = UVZ_W // LANES


def _mix_in_fwd(h, g, w_in_t, conv_w, conv_b):
    T = h.shape[0]

    def body(i, h_ref, g_ref, cw_ref, cb_ref, w_ref, p_ref, n_ref, x_ref, xc_ref, ext_ref):
        @pl.when(i == 0)
        def _():
            ext_ref[0:HALO, :] = jnp.zeros((HALO, CONV_DIM), F32)

        n = _rms(h_ref[...], g_ref[...]).astype(BF16)
        n_ref[...] = n
        proj = _dot_nt(n, w_ref[...])
        p_ref[:, :UVZ_W] = proj[:, :UVZ_W]
        p_ref[:, UVZ_W:] = proj[:, UVZ_W + CONV_DIM:]
        xbc = proj[:, UVZ_W:UVZ_W + CONV_DIM]
        x_ref[...] = xbc.astype(BF16)
        ext_ref[HALO:, :] = xbc
        xc_ref[...] = _conv_taps(ext_ref, cw_ref[...], cb_ref[...], PROJ_TM)
        ext_ref[0:HALO, :] = ext_ref[PROJ_TM:PROJ_TM + HALO, :]

    return _tiled(body, "mix_in_fwd", T // PROJ_TM, [(h, PROJ_TM, D_MODEL, 0)], [g, conv_w, conv_b], [w_in_t],
                  [(T, PROJ_KEPT, F32, PROJ_TM), (T, D_MODEL, BF16, PROJ_TM), (T, CONV_DIM, BF16, PROJ_TM),
                   (T, CONV_DIM, F32, PROJ_TM)], [],
                  scratch=[pltpu.VMEM((HALO + PROJ_TM, CONV_DIM), F32)])


def _mix_in_dgrad(h, dh_in, dp_uv, dp_zxd, g, w_in_t, comm=None):
    T = h.shape[0]

    def body(i, h_ref, dh_ref, duv_ref, dzxd_ref, g_ref, w_ref, o_ref, dg_ref):
        dn = _dot(duv_ref[...], w_ref[:UV_W, :]) + _dot(dzxd_ref[...], w_ref[UV_W:, :])
        _, rms_vjp = jax.vjp(_rms, h_ref[...], g_ref[...])
        dx, dg = rms_vjp(dn)
        o_ref[...] = dh_ref[...] + dx
        dg_ref[...] += dg

    return _tiled(body, "mix_in_dgrad", T // PROJ_TM,
                  [(h, PROJ_TM, D_MODEL, 0), (dh_in, PROJ_TM, D_MODEL, 0), (dp_uv, PROJ_TM, UV_W, 0),
                   (dp_zxd, PROJ_TM, ZXD_W, 0)], [g], [w_in_t],
                  [(T, D_MODEL, F32, PROJ_TM)], [((1, D_MODEL), F32)], comm=comm)


def _out_proj_dgrad(dh, w_out):
    T = dh.shape[0]

    def body(i, dh_ref, w_ref, dya_ref, dyb_ref):
        d = dh_ref[...].astype(BF16)
        dya_ref[...] = _dot_nt(d, w_ref[:GM_WIDTH, :])
        dyb_ref[...] = _dot_nt(d, w_ref[GM_WIDTH:, :])

    return _tiled(body, "out_proj_dgrad", T // PROJ_TM, [(dh, PROJ_TM, D_MODEL, 0)], [], [w_out],
                  [(T, GM_WIDTH, F32, PROJ_TM), (T, SSM_WIDTH, F32, PROJ_TM)], [])


def _gm_chunk(u, v, ln_g, ln_b, b_st, out_g, *w_heads):
    ug = _gelu(u)
    vg = _gelu(v)
    mu = jnp.mean(vg, axis=-1, keepdims=True)
    xc = vg - mu
    vn = xc * lax.rsqrt(jnp.mean(xc * xc, axis=-1, keepdims=True) + EPS) * ln_g + ln_b
    t_idx = lax.broadcasted_iota(jnp.int32, (CHUNK, CHUNK), 0)
    s_idx = lax.broadcasted_iota(jnp.int32, (CHUNK, CHUNK), 1)
    causal = t_idx >= s_idx
    mixed = []
    for hd in range(GM_HEADS):
        wm = jnp.where(causal, w_heads[hd], 0.0)
        cols = slice(hd * GM_HEAD_DIM, (hd + 1) * GM_HEAD_DIM)
        mixed.append(_dot(wm, vn[:, cols]) + b_st[:, hd:hd + 1])
    ya0 = ug * jnp.concatenate(mixed, axis=1)
    return _rms(ya0, out_g)


GM_FWD_CHUNKS = 2


def _gm_fwd(proj, ln_g, ln_b, w_s, b_st, out_g):
    T = proj.shape[0]

    rows = GM_FWD_CHUNKS * CHUNK

    def body(i, u_ref, v_ref, lg_ref, lb_ref, w_ref, bs_ref, og_ref, ya_ref):
        w_heads = [w_ref[hd] for hd in range(GM_HEADS)]
        for c in range(GM_FWD_CHUNKS):
            tok = pl.ds(c * CHUNK, CHUNK)
            ya = _gm_chunk(u_ref[tok, :], v_ref[tok, :], lg_ref[...], lb_ref[...], bs_ref[...], og_ref[...], *w_heads)
            ya_ref[tok, :] = ya.astype(BF16)

    return _tiled(body, "gmlp_fwd", T // rows, [(proj, rows, GM_WIDTH, 0), (proj, rows, GM_WIDTH, 1)],
                  [ln_g, ln_b, w_s, b_st, out_g], [], [(T, GM_WIDTH, BF16, rows)], [])[0]


def _gm_bwd(proj, dya, ln_g, ln_b, w_s, b_st, out_g):
    T = proj.shape[0]

    def body(i, u_ref, v_ref, dy_ref, lg_ref, lb_ref, w_ref, bs_ref, og_ref, duv_ref, dlg_ref, dlb_ref, dw_ref, dbs_ref,
             dog_ref):
        w_heads = [w_ref[hd] for hd in range(GM_HEADS)]
        _, vjp = jax.vjp(_gm_chunk, u_ref[...], v_ref[...], lg_ref[...], lb_ref[...], bs_ref[...], og_ref[...], *w_heads)
        grads = vjp(dy_ref[...])
        duv_ref[:, :GM_WIDTH] = grads[0].astype(BF16)
        duv_ref[:, GM_WIDTH:] = grads[1].astype(BF16)
        dlg_ref[...] += grads[2]
        dlb_ref[...] += grads[3]
        dbs_ref[...] += grads[4]
        dog_ref[...] += grads[5]
        for hd in range(GM_HEADS):
            dw_ref[hd] += grads[6 + hd]

    return _tiled(body, "gmlp_bwd", T // CHUNK,
                  [(proj, CHUNK, GM_WIDTH, 0), (proj, CHUNK, GM_WIDTH, 1), (dya, CHUNK, GM_WIDTH, 0)],
                  [ln_g, ln_b, w_s, b_st, out_g], [], [(T, UV_W, BF16, CHUNK)],
                  [((1, GM_WIDTH), F32), ((1, GM_WIDTH), F32), ((GM_HEADS, CHUNK, CHUNK), F32),
                   ((CHUNK, GM_HEADS), F32), ((1, GM_WIDTH), F32)])


def _ssd_chunk(xc, z, dtr, s_in, dt_bias, a_log, d_skip, norm_g):
    half = SSM_WIDTH // SSM_GROUPS
    l_idx = lax.broadcasted_iota(jnp.int32, (CHUNK, CHUNK), 0)
    s_idx = lax.broadcasted_iota(jnp.int32, (CHUNK, CHUNK), 1)
    causal = l_idx >= s_idx
    head_of_col = lax.broadcasted_iota(jnp.int32, (SSM_HEADS, SSM_WIDTH), 1) // SSM_HEAD_DIM
    expand = (head_of_col == lax.broadcasted_iota(jnp.int32, (SSM_HEADS, SSM_WIDTH), 0)).astype(BF16)

    xcs = _silu(xc)
    xs = xcs[:, :SSM_WIDTH]
    dt = jax.nn.softplus(dtr + dt_bias)
    adt = dt * (-jnp.exp(a_log))
    acs = _cumsum_rows(adt, causal.astype(BF16))
    acs_t = _cumsum_cols(adt, (l_idx <= s_idx).astype(BF16))
    tot = acs[CHUNK - 1:CHUNK, :]
    dt_w = _widen(dt, expand)
    out_decay_w = _widen(jnp.exp(acs), expand)
    state_decay_w = _widen(jnp.exp(tot - acs), expand)
    chunk_decay_w = _widen(jnp.exp(tot), expand)
    d_skip_w = _widen(d_skip, expand)
    xdt = xs * dt_w
    xdt_decayed = xdt * state_decay_w

    y_diag, y_off, states = [], [], []
    for grp in range(SSM_GROUPS):
        b0 = SSM_WIDTH + grp * SSM_STATE
        c0 = SSM_WIDTH + SSM_GROUPS * SSM_STATE + grp * SSM_STATE
        bm = xcs[:, b0:b0 + SSM_STATE].astype(BF16)
        cm = xcs[:, c0:c0 + SSM_STATE].astype(BF16)
        cb = _dot_nt(cm, bm)
        for k in range(grp * SSM_HEADS // SSM_GROUPS, (grp + 1) * SSM_HEADS // SSM_GROUPS):
            decay = jnp.exp(jnp.where(causal, acs[:, k:k + 1] - acs_t[k:k + 1, :], -jnp.inf))
            y_diag.append(_dot(cb * decay, xdt[:, k * SSM_HEAD_DIM:(k + 1) * SSM_HEAD_DIM]))
        cols = slice(grp * half, (grp + 1) * half)
        states.append(_dot_tn(bm, xdt_decayed[:, cols]))
        y_off.append(_dot(cm, s_in[:, cols]))
    y = jnp.concatenate(y_diag, axis=1) + jnp.concatenate(y_off, axis=1) * out_decay_w + xs * d_skip_w
    s_out = s_in * chunk_decay_w + jnp.concatenate(states, axis=1)
    y = y * _silu(z)
    normed = []
    for grp in range(SSM_GROUPS):
        yg = y[:, grp * half:(grp + 1) * half]
        normed.append(yg * lax.rsqrt(jnp.mean(yg * yg, axis=-1, keepdims=True) + EPS))
    return jnp.concatenate(normed, axis=1) * norm_g, s_out


def _sum_row_tiles(x):
    return x.reshape(x.shape[0] // F32_ROWS, F32_ROWS, x.shape[1]).sum(axis=0)


def _conv_taps(ext_ref, w, b, rows):
    y = b
    for k in range(SSM_CONV):
        y = y + w[k:k + 1, :] * ext_ref[pl.ds(HALO - (SSM_CONV - 1) + k, rows), :]
    return y


def _ssd_fwd(proj, xc, dt_bias, a_log, d_skip, norm_g, comm=None):
    T = proj.shape[0]
    n_chunks = T // CHUNK

    def body(i, z_ref, xc_ref, dt_ref, dtb_ref, al_ref, dsk_ref, ng_ref, yb_ref, sin_ref, st_ref):
        @pl.when(i == 0)
        def _():
            st_ref[...] = jnp.zeros(st_ref.shape, F32)

        s_in = st_ref[...]
        yb, s_out = _ssd_chunk(xc_ref[...], z_ref[...], dt_ref[:, 0:SSM_HEADS], s_in, dtb_ref[...], al_ref[...],
                               dsk_ref[...], ng_ref[...])
        yb_ref[...] = yb.astype(BF16)
        sin_ref[...] = s_in
        st_ref[...] = s_out

    return _tiled(body, "ssd_fwd", n_chunks,
                  [(proj, CHUNK, SSM_WIDTH, Z_BLK), (xc, CHUNK, CONV_DIM, 0), (proj, CHUNK, LANES, DT_BLK)],
                  [dt_bias, a_log, d_skip, norm_g], [],
                  [(T, SSM_WIDTH, BF16, CHUNK), (n_chunks * SSM_STATE, SSM_WIDTH, F32, SSM_STATE)], [],
                  scratch=[pltpu.VMEM((SSM_STATE, SSM_WIDTH), F32)], comm=comm)


def _ssd_bwd(proj, x16, xc, dyb, s_all, conv_w, dt_bias, a_log, d_skip, norm_g, comm=None):
    T = proj.shape[0]
    n_chunks = T // CHUNK

    def body(i, z_ref, x_ref, xc_ref, dt_ref, dy_ref, sin_ref, cw_ref, dtb_ref, al_ref, dsk_ref, ng_ref,
             dzxd_ref, dcw_ref, dcb_ref, ddtb_ref, dal_ref, ddsk_ref, dng_ref, dext_ref, dst_ref, cw_acc, cb_acc):
        @pl.when(i == n_chunks - 1)
        def _():
            dext_ref[CHUNK:, :] = jnp.zeros((HALO, CONV_DIM), F32)
            dst_ref[...] = jnp.zeros(dst_ref.shape, F32)
            cw_acc[...] = jnp.zeros(cw_acc.shape, F32)
            cb_acc[...] = jnp.zeros(cb_acc.shape, F32)

        _, vjp = jax.vjp(_ssd_chunk, xc_ref[...], z_ref[...], dt_ref[:, 0:SSM_HEADS], sin_ref[...], dtb_ref[...], al_ref[...],
                         dsk_ref[...], ng_ref[...])
        dxc, dz, ddtr, ds_in, ddtb, dal, ddsk, dng = vjp((dy_ref[...], dst_ref[...]))
        dst_ref[...] = ds_in
        ddtb_ref[...] += ddtb
        dal_ref[...] += dal
        ddsk_ref[...] += ddsk
        dng_ref[...] += dng
        dext_ref[0:CHUNK, :] = dxc
        cw = cw_ref[...]
        x = x_ref[...].astype(F32)
        dx = jnp.zeros((CHUNK, CONV_DIM), F32)
        for k in range(SSM_CONV):
            shifted = dext_ref[pl.ds(SSM_CONV - 1 - k, CHUNK), :]
            dx = dx + cw[k:k + 1, :] * shifted
            cw_acc[k] += _sum_row_tiles(shifted * x)
        cb_acc[...] += _sum_row_tiles(dxc)

        @pl.when(i == 0)
        def _():
            dcw_ref[...] = jnp.sum(cw_acc[...], axis=1)
            dcb_ref[...] = jnp.sum(cb_acc[...], axis=0, keepdims=True)

        dext_ref[CHUNK:, :] = dext_ref[0:HALO, :]
        dzxd_ref[:, 0:SSM_WIDTH] = dz.astype(BF16)
        dzxd_ref[:, SSM_WIDTH:SSM_WIDTH + CONV_DIM] = dx.astype(BF16)
        dzxd_ref[:, SSM_WIDTH + CONV_DIM:] = jnp.concatenate(
            [ddtr, jnp.zeros((CHUNK, LANES - SSM_HEADS), F32)], axis=1).astype(BF16)

    return _tiled(body, "ssd_bwd", n_chunks,
                  [(proj, CHUNK, SSM_WIDTH, Z_BLK), (x16, CHUNK, CONV_DIM, 0), (xc, CHUNK, CONV_DIM, 0),
                   (proj, CHUNK, LANES, DT_BLK), (dyb, CHUNK, SSM_WIDTH, 0), (s_all, SSM_STATE, SSM_WIDTH, 0)],
                  [conv_w, dt_bias, a_log, d_skip, norm_g], [],
                  [(T, ZXD_W, BF16, CHUNK)],
                  [((SSM_CONV, CONV_DIM), F32), ((1, CONV_DIM), F32), ((1, SSM_HEADS), F32), ((1, SSM_HEADS), F32),
                   ((1, SSM_HEADS), F32), ((1, SSM_WIDTH), F32)],
                  scratch=[pltpu.VMEM((CHUNK + HALO, CONV_DIM), F32), pltpu.VMEM((SSM_STATE, SSM_WIDTH), F32),
                           pltpu.VMEM((SSM_CONV, F32_ROWS, CONV_DIM), F32), pltpu.VMEM((F32_ROWS, CONV_DIM), F32)],
                  reverse=True, comm=comm)


TAIL_TM = 512


def _tail(h, p, target, ple_norm, w_gate, b_gate, w_proj_t, final_norm):
    T = h.shape[0]

    def head(x, pre, pp, b_g, f_norm, tgt):
        gate = jax.nn.sigmoid(pre + b_g)
        out = _rms(x + gate * pp, f_norm)
        err = out - tgt
        return 0.5 * jnp.sum(jnp.mean(err * err, axis=-1, keepdims=True), axis=0, keepdims=True)

    def body(i, h_ref, p_ref, t_ref, pn_ref, bg_ref, fn_ref, wg_ref, wp_ref, dh_ref, loss_ref, dwg_ref, dwp_ref, dpn_ref,
             dbg_ref, dfn_ref):
        x = h_ref[...]
        n4f, n_vjp = jax.vjp(_rms, x, pn_ref[...])
        n4 = n4f.astype(BF16)
        pre = jnp.dot(n4, wg_ref[...], preferred_element_type=F32)
        p16 = p_ref[...].astype(BF16)
        pp = _dot_nt(p16, wp_ref[...])
        loss, h_vjp = jax.vjp(functools.partial(head, tgt=t_ref[...]), x, pre, pp, bg_ref[...], fn_ref[...])
        dx, dpre, dpp, dbg, dfn = h_vjp(jnp.ones((1, 1), F32))
        dpre16 = dpre.astype(BF16)
        dn4 = _dot_nt(dpre16, wg_ref[...])
        dx2, dpn = n_vjp(dn4)
        dh_ref[...] = dx + dx2
        loss_ref[...] += loss
        dwg_ref[...] += _dot_tn(n4, dpre16)
        dwp_ref[...] += _dot_tn(p16, dpp)
        dpn_ref[...] += dpn
        dbg_ref[...] += dbg
        dfn_ref[...] += dfn

    return _tiled(body, "tail", T // TAIL_TM,
                  [(h, TAIL_TM, D_MODEL, 0), (p, TAIL_TM, D_PLE, 0), (target, TAIL_TM, D_MODEL, 0)],
                  [ple_norm, b_gate, final_norm], [w_gate, w_proj_t],
                  [(T, D_MODEL, F32, TAIL_TM)],
                  [((1, 1), F32), ((D_MODEL, D_MODEL), F32), ((D_PLE, D_MODEL), F32), ((1, D_MODEL), F32),
                   ((1, D_MODEL), F32), ((1, D_MODEL), F32)])


def _gather_phases(x_ref, out_ref, send_sems, recv_sems, local_sem):
    mx, my, mc = lax.axis_index("x"), lax.axis_index("y"), lax.axis_index("c")
    me, sibling = (mx, my, mc), (mx, my, 1 - mc)
    chips = [(1 - mx, my), (mx, 1 - my), (1 - mx, 1 - my)]

    def rows(px, py, pc):
        return out_ref.at[4 * px + 2 * py + pc]

    def copy(k, block, to, src=None):
        return pltpu.make_async_remote_copy(
            src_ref=rows(*block) if src is None else src, dst_ref=rows(*block),
            send_sem=send_sems.at[k], recv_sem=recv_sems.at[k], device_id=to, device_id_type=MESH)

    mine = pltpu.make_async_copy(x_ref, rows(*me), local_sem)
    first = [copy(0, me, sibling, src=x_ref)] + [copy(1 + j, me, (*chip, mc), src=x_ref) for j, chip in enumerate(chips)]
    passed = [copy(4 + j, (*chip, mc), sibling) for j, chip in enumerate(chips)]

    def start():
        mine.start()
        for cp in first:
            cp.start()

    def mid():
        for j, chip in enumerate(chips):
            copy(1 + j, (*chip, mc), me).wait_recv()
            passed[j].start()

    def finish():
        copy(0, sibling, me).wait_recv()
        for j, chip in enumerate(chips):
            copy(4 + j, (*chip, 1 - mc), me).wait_recv()
        for cp in first + passed:
            cp.wait_send()
        mine.wait()

    return start, mid, finish


def _exchange_phases(x_ref, out_ref, send_sems, recv_sems, local_sem):
    mx, my, mc = lax.axis_index("x"), lax.axis_index("y"), lax.axis_index("c")
    me = 4 * mx + 2 * my + mc
    mine = pltpu.make_async_copy(x_ref.at[me], out_ref.at[me], local_sem)
    copies = []
    for k in range(1, N_DEV):
        px = 1 - mx if k & 4 else mx
        py = 1 - my if k & 2 else my
        pc = 1 - mc if k & 1 else mc
        copies.append(pltpu.make_async_remote_copy(
            src_ref=x_ref.at[4 * px + 2 * py + pc], dst_ref=out_ref.at[me], send_sem=send_sems.at[k - 1],
            recv_sem=recv_sems.at[k - 1], device_id=(px, py, pc), device_id_type=MESH))

    def start():
        mine.start()
        for cp in copies:
            cp.start()

    def finish():
        for cp in copies:
            cp.wait_recv()
        for cp in copies:
            cp.wait_send()
        mine.wait()

    return start, lambda: None, finish


def _gather_comm(x):
    return _Comm(_gather_phases, x, jax.ShapeDtypeStruct((N_DEV,) + x.shape, x.dtype))


def _exchange_comm(x):
    return _Comm(_exchange_phases, x, jax.ShapeDtypeStruct(x.shape, x.dtype))


def _comm_alone(comms, name):
    n = len(comms)

    def body(*refs):
        phases = [comm.phases(refs[k], refs[n + k], *refs[2 * n + 3 * k:2 * n + 3 * k + 3]) for k, comm in enumerate(comms)]
        for step in range(3):
            for phase in phases:
                phase[step]()

    any_spec = pl.BlockSpec(memory_space=pl.ANY)
    return pl.pallas_call(
        body,
        out_shape=[comm.dst for comm in comms],
        in_specs=[any_spec] * n,
        out_specs=[any_spec] * n,
        scratch_shapes=[pltpu.SemaphoreType.DMA((N_DEV - 1,)), pltpu.SemaphoreType.DMA((N_DEV - 1,)), pltpu.SemaphoreType.DMA] * n,
        name=name,
    )(*[comm.src for comm in comms])


def _sum_parts(p_ref):
    g = p_ref[0].astype(F32)
    for j in range(1, N_DEV):
        g = g + p_ref[j].astype(F32)
    return g


def _adamw_store(g, w_ref, m_ref, v_ref, g_ref, d_ref, nm_ref, nv_ref):
    m_new = ADAM_B1 * m_ref[...] + (1.0 - ADAM_B1) * g
    v_new = ADAM_B2 * v_ref[...] + (1.0 - ADAM_B2) * jnp.square(g)
    m_hat = m_new / (1.0 - ADAM_B1 ** ADAM_STEP)
    v_hat = v_new / (1.0 - ADAM_B2 ** ADAM_STEP)
    g_ref[...] = g
    d_ref[...] = -ADAM_LR * (m_hat / (jnp.sqrt(v_hat) + ADAM_EPS) + ADAM_WD * w_ref[...])
    nm_ref[...] = m_new
    nv_ref[...] = v_new


def _adamw_shard(parts, off, transposed, w, m, v, name, n_tiles=1):
    _, r, c = w.shape
    tr = r // n_tiles
    if transposed:
        rows = -(-c // BF16_ROWS) * BF16_ROWS
        window = (N_DEV, rows, tr)
    else:
        assert c == PACK_COLS
        window = (N_DEV, tr, PACK_COLS)

    def kern(p_hbm, w_ref, m_ref, v_ref, g_ref, d_ref, nm_ref, nv_ref, buf, sem):
        i = pl.program_id(0)
        if transposed:
            src = p_hbm.at[:, pl.ds(off, rows), pl.ds(pl.multiple_of(i * tr, LANES), tr)]
        else:
            src = p_hbm.at[:, pl.ds(pl.multiple_of(off + i * tr, BF16_ROWS), tr), :]
        cp = pltpu.make_async_copy(src, buf, sem)
        cp.start()
        cp.wait()
        g = _sum_parts(buf)
        if transposed:
            eye = (lax.broadcasted_iota(jnp.int32, (rows, c), 0) == lax.broadcasted_iota(jnp.int32, (rows, c), 1)).astype(F32)
            g = _hdot_tn(g, eye)
        _adamw_store(g, w_ref, m_ref, v_ref, g_ref, d_ref, nm_ref, nv_ref)

    spec = pl.BlockSpec((None, tr, c), lambda i: (0, i, 0))
    return pl.pallas_call(
        kern,
        out_shape=[jax.ShapeDtypeStruct((1, r, c), F32)] * 4,
        grid=(n_tiles,),
        in_specs=[pl.BlockSpec(memory_space=pl.ANY), spec, spec, spec],
        out_specs=[spec] * 4,
        scratch_shapes=[pltpu.VMEM(window, parts.dtype), pltpu.SemaphoreType.DMA],
        name=name,
        compiler_params=pltpu.CompilerParams(dimension_semantics=("arbitrary",), vmem_limit_bytes=VMEM_LIMIT),
    )(parts, w, m, v)


def _sum_adamw(parts, w, m, v, tr, name):
    _, R, C = parts.shape

    def kern(p_ref, w_ref, m_ref, v_ref, g_ref, d_ref, nm_ref, nv_ref):
        _adamw_store(_sum_parts(p_ref), w_ref, m_ref, v_ref, g_ref, d_ref, nm_ref, nv_ref)

    row_spec = pl.BlockSpec((tr, C), lambda i: (i, 0))
    return pl.pallas_call(
        kern,
        out_shape=[jax.ShapeDtypeStruct((R, C), F32)] * 4,
        grid=(R // tr,),
        in_specs=[pl.BlockSpec((N_DEV, tr, C), lambda i: (0, i, 0)), row_spec, row_spec, row_spec],
        out_specs=[row_spec] * 4,
        name=name,
        compiler_params=pltpu.CompilerParams(dimension_semantics=("arbitrary",), vmem_limit_bytes=VMEM_LIMIT),
    )(parts, w, m, v)


FF_SHARD = D_FF // N_DEV
CONV_SHARD = (SSM_CONV, CONV_DIM // N_DEV)
SHARDS = {"ffn1_w_gate": ((D_MODEL, FF_SHARD), True), "ffn1_w_up": ((D_MODEL, FF_SHARD), True),
          "ffn1_w_down": ((FF_SHARD, D_MODEL), False),
          "ffn2_w_gate": ((D_MODEL, FF_SHARD), True), "ffn2_w_up": ((D_MODEL, FF_SHARD), True),
          "ffn2_w_down": ((FF_SHARD, D_MODEL), False),
          "w_out": ((2 * D_MODEL // N_DEV, D_MODEL), False), "ple_w_gate": ((D_MODEL // N_DEV, D_MODEL), False),
          "w_in": ((D_MODEL, IN_PROJ // N_DEV), True), "ple_w_proj": ((D_PLE, D_MODEL // N_DEV), True),
          "conv_w": (CONV_SHARD, True),
          "conv_w_mid": (CONV_SHARD, True), "conv_w_low": (CONV_SHARD, True)}
BIG = tuple(name for name in SHARDS if not name.startswith("conv_w_"))
SMALL = ("ffn1_norm", "mix_norm", "gm_ln_g", "gm_ln_b", "gm_w_s", "gm_b_s", "gm_out_norm", "conv_b", "dt_bias", "a_log",
         "d_skip", "ssm_norm", "ffn2_norm", "ple_norm", "ple_b_gate", "final_norm")
SMALL_ROWS = 144


def _piece_rows(name):
    shape = SHARDS[name][0]
    return -(-(shape[0] * shape[1]) // PACK_COLS)


def _pad_cols(flat, name):
    pad = _piece_rows(name) * PACK_COLS - flat.shape[-1]
    return flat if pad == 0 else jnp.pad(flat, [(0, 0)] * (flat.ndim - 1) + [(0, pad)])


class _Pack:
    def __init__(self, names, tile_rows):
        self.names, self.tile_rows, self.offsets, off = names, tile_rows, {}, 0
        for name in names:
            self.offsets[name] = off
            off += _piece_rows(name)
        self.rows = -(-off // tile_rows) * tile_rows

    def pack_local(self, vals):
        parts = []
        for name in self.names:
            val = vals[name]
            parts.append(_pad_cols((val.T if SHARDS[name][1] else val).reshape(-1), name))
        flat = jnp.concatenate(parts)
        return jnp.pad(flat, (0, self.rows * PACK_COLS - flat.shape[0])).reshape(self.rows, PACK_COLS)

    def pack_owner_major(self, grads):
        parts, rows = [], 0
        for name in self.names:
            grad, piece_rows = grads[name].astype(BF16), _piece_rows(name)
            if grad.shape != (N_DEV * piece_rows, PACK_COLS):
                grad = _pad_cols(grad.reshape(N_DEV, -1), name)
            parts.append(grad.reshape(N_DEV, piece_rows, PACK_COLS))
            rows += piece_rows
        if rows < self.rows:
            parts.append(jnp.zeros((N_DEV, self.rows - rows, PACK_COLS), BF16))
        return parts[0] if len(parts) == 1 else jnp.concatenate(parts, axis=1)

    def gathered_piece(self, gathered, name):
        shape = SHARDS[name][0]
        rows = gathered[:, self.offsets[name]:self.offsets[name] + _piece_rows(name), :]
        return rows.reshape(N_DEV, -1)[:, :shape[0] * shape[1]]

    def pieces(self, gathered, name):
        return _Pieces(gathered, self.offsets[name], _piece_rows(name))


GATHER_FFN1 = _Pack(("ffn1_w_gate", "ffn1_w_up", "ffn1_w_down"), BF16_ROWS)
GATHER_MIX = _Pack(("w_out", "ple_w_gate", "w_in", "ple_w_proj", "conv_w", "conv_w_mid", "conv_w_low"), BF16_ROWS)
GATHER_FFN2 = _Pack(("ffn2_w_gate", "ffn2_w_up", "ffn2_w_down"), BF16_ROWS)
SCATTER_LATE = _Pack(("ffn2_w_gate", "ffn2_w_up", "ffn2_w_down", "w_out", "ple_w_gate", "ple_w_proj"), BF16_ROWS)
SCATTER_IN = _Pack(("w_in", "conv_w"), BF16_ROWS)
SCATTER_GATE = _Pack(("ffn1_w_gate",), BF16_ROWS)
SCATTER_UP = _Pack(("ffn1_w_up",), BF16_ROWS)
SCATTER_DOWN = _Pack(("ffn1_w_down",), BF16_ROWS)


def _pack_small(vals):
    flat = jnp.concatenate([vals[name].reshape(-1).astype(F32) for name in SMALL])
    return jnp.pad(flat, (0, SMALL_ROWS * PACK_COLS - flat.shape[0])).reshape(SMALL_ROWS, PACK_COLS)


def _unpack_small(packed, shapes):
    out, off = {}, 0
    flat = packed.reshape(-1)
    for name in SMALL:
        n = 1
        for s in shapes[name]:
            n *= s
        out[name] = flat[off:off + n].reshape(shapes[name])
        off += n
    return out


WEIGHTS = ("ffn1_norm", "ffn1_w_gate", "ffn1_w_up", "ffn1_w_down", "mix_norm", "w_in", "gm_ln_g", "gm_ln_b", "gm_w_s",
           "gm_b_s", "gm_out_norm", "conv_w", "conv_b", "dt_bias", "a_log", "d_skip", "ssm_norm", "w_out", "ffn2_norm",
           "ffn2_w_gate", "ffn2_w_up", "ffn2_w_down", "ple_norm", "ple_w_gate", "ple_b_gate", "ple_w_proj", "final_norm")


def _step(x, p, target, w, m, v):
    local = lambda d: {name: d[name][0] for name in BIG}

    shards = {name: val.astype(BF16) for name, val in local(w).items()}
    conv_high = lax.reduce_precision(w["conv_w"][0], 8, 7)
    conv_mid = lax.reduce_precision(w["conv_w"][0] - conv_high, 8, 7)
    shards["conv_w"] = conv_high.astype(BF16)
    shards["conv_w_mid"] = conv_mid.astype(BF16)
    shards["conv_w_low"] = (w["conv_w"][0] - conv_high - conv_mid).astype(BF16)
    g_ffn1 = _comm_alone([_gather_comm(GATHER_FFN1.pack_local(shards))], "gather_ffn1")[0]

    row = lambda name: w[name].reshape(1, -1)
    gm_w_s = w["gm_w_s"][0]
    gm_b_st = jnp.transpose(w["gm_b_s"][0])
    ffn1 = (row("ffn1_norm"),) + tuple(GATHER_FFN1.pieces(g_ffn1, name) for name in GATHER_FFN1.names)
    gm = (row("gm_ln_g"), row("gm_ln_b"), gm_w_s, gm_b_st, row("gm_out_norm"))

    h1, n1, a1, b1, s1, g_mix = _ffn_fwd(x, *ffn1, "ffn1_fwd", comm=_gather_comm(GATHER_MIX.pack_local(shards)),
                                         chunks=FF_CHUNKS_3)
    w_in_t = GATHER_MIX.gathered_piece(g_mix, "w_in").reshape(IN_PROJ, D_MODEL)
    w_in_t = jnp.concatenate([w_in_t, jnp.zeros((IN_PROJ_PAD - IN_PROJ, D_MODEL), BF16)], axis=0)
    w_proj_t = GATHER_MIX.gathered_piece(g_mix, "ple_w_proj").reshape(D_MODEL, D_PLE)
    conv_w = sum(GATHER_MIX.gathered_piece(g_mix, name).astype(F32) for name in ("conv_w", "conv_w_mid", "conv_w_low"))
    conv_w = conv_w.reshape(CONV_DIM, SSM_CONV).T
    ssd = (row("dt_bias"), row("a_log"), row("d_skip"), row("ssm_norm"))
    w_out = GATHER_MIX.pieces(g_mix, "w_out")

    proj, n2, x16, xc = _mix_in_fwd(h1, row("mix_norm"), w_in_t, conv_w, row("conv_b"))
    ya = _gm_fwd(proj, *gm)
    yb, s_all, g_ffn2 = _ssd_fwd(proj, xc, *ssd, comm=_gather_comm(GATHER_FFN2.pack_local(shards)))
    ffn2 = (row("ffn2_norm"),) + tuple(GATHER_FFN2.pieces(g_ffn2, name) for name in GATHER_FFN2.names)
    h3, n3, a3, b3, s3, h2 = _ffn_fwd(h1, *ffn2, "ffn2_fwd", mixed=(ya, yb, w_out))

    g, gp = {}, {}
    dh3, loss, gp["ple_w_gate"], d_w_proj, g["ple_norm"], g["ple_b_gate"], g["final_norm"] = _tail(
        h3, p, target, row("ple_norm"), GATHER_MIX.pieces(g_mix, "ple_w_gate"), row("ple_b_gate"), w_proj_t,
        row("final_norm"))
    gp["ple_w_proj"] = d_w_proj.T

    dh2, da3, db3, g["ffn2_norm"] = _ffn_dgrad(h2, dh3, a3, b3, *ffn2, "ffn2_dgrad")
    gp["ffn2_w_gate"] = _wgrad(n3, da3, 1408, "ffn2_wgrad_gate", transpose_out=True, bk=1024)
    gp["ffn2_w_up"] = _wgrad(n3, db3, 1408, "ffn2_wgrad_up", transpose_out=True)
    gp["ffn2_w_down"] = _wgrad(s3, dh3, 512, "ffn2_wgrad_down", scale=0.5, bk=1024)

    dya, dyb = _out_proj_dgrad(dh2, w_out)
    gp["w_out"] = jnp.concatenate([_wgrad(ya, dh2, 1024, "w_out_wgrad_a"), _wgrad(yb, dh2, 1024, "w_out_wgrad_b")], axis=0)

    dp_zxd, d_conv_w, g["conv_b"], g["dt_bias"], g["a_log"], g["d_skip"], g["ssm_norm"], parts_late = _ssd_bwd(
        proj, x16, xc, dyb, s_all, conv_w, *ssd, comm=_exchange_comm(SCATTER_LATE.pack_owner_major(gp)))
    gp["conv_w"] = d_conv_w.T
    dp_uv, g["gm_ln_g"], g["gm_ln_b"], g["gm_w_s"], dbst, g["gm_out_norm"] = _gm_bwd(proj, dya, *gm)
    g["gm_b_s"] = jnp.transpose(dbst)

    parts = {}
    gp["w_in"] = jnp.concatenate([_wgrad(n2, dp_uv, 1024, "w_in_wgrad_uv", transpose_out=True),
                                  _wgrad(n2, dp_zxd, 896, "w_in_wgrad_zxd", transpose_out=True)], axis=0)[:IN_PROJ]
    dh1, g["mix_norm"], parts[SCATTER_IN] = _mix_in_dgrad(h1, dh2, dp_uv, dp_zxd, row("mix_norm"), w_in_t,
                                                          comm=_exchange_comm(SCATTER_IN.pack_owner_major(gp)))

    dx, da1, db1, g["ffn1_norm"] = _ffn_dgrad(x, dh1, a1, b1, *ffn1, "ffn1_dgrad", chunks=FF_CHUNKS_3)
    gp["ffn1_w_gate"], small_parts = _wgrad(n1, da1, 1408, "ffn1_wgrad_gate", transpose_out=True,
                                            comm=_gather_comm(_pack_small(g)))
    gp["ffn1_w_up"], parts[SCATTER_GATE] = _wgrad(n1, db1, 1408, "ffn1_wgrad_up", transpose_out=True,
                                                  comm=_exchange_comm(SCATTER_GATE.pack_owner_major(gp)))
    gp["ffn1_w_down"], parts[SCATTER_UP] = _wgrad(s1, dh1, 512, "ffn1_wgrad_down", scale=0.5, bk=1024,
                                                  comm=_exchange_comm(SCATTER_UP.pack_owner_major(gp)))
    parts[SCATTER_DOWN] = _comm_alone([_exchange_comm(SCATTER_DOWN.pack_owner_major(gp))], "scatter_ffn1_down")[0]
    parts[SCATTER_LATE] = parts_late

    res_big = {}
    for pack, pack_parts in parts.items():
        for name in pack.names:
            shape, transposed = SHARDS[name]
            if name in ("ple_w_proj", "conv_w"):
                nat = pack.gathered_piece(pack_parts, name).reshape((N_DEV,) + shape[::-1])
                res_big[name] = _sum_adamw(jnp.transpose(nat, (0, 2, 1)), w[name][0], m[name][0], v[name][0], shape[0],
                                           "adamw_" + name)
            elif name == "w_in":
                res_big[name] = _adamw_shard(pack_parts, pack.offsets[name], True, w[name], m[name], v[name],
                                             "adamw_" + name, n_tiles=4)
            else:
                flip = (lambda a: jnp.transpose(a, (0, 2, 1))) if transposed else (lambda a: a)
                res = _adamw_shard(pack_parts, pack.offsets[name], False, flip(w[name]), flip(m[name]), flip(v[name]),
                                   "adamw_" + name, n_tiles=2)
                res_big[name] = [flip(r) for r in res]

    small_shapes = {name: w[name].shape for name in SMALL}
    res_small = _sum_adamw(small_parts, _pack_small(w), _pack_small(m), _pack_small(v), SMALL_ROWS, "adamw_small")
    res_small = [_unpack_small(r, small_shapes) for r in res_small]

    outs = []
    for k in range(4):
        for name in WEIGHTS:
            if name in res_small[k]:
                outs.append(res_small[k][name])
            else:
                outs.append(res_big[name][k].reshape(w[name].shape))
    return loss[0, 0], dx, outs


def kernel(x, p, ffn1_norm, ffn1_w_gate, ffn1_w_up, ffn1_w_down, mix_norm, w_in, gm_ln_g, gm_ln_b, gm_w_s, gm_b_s, gm_out_norm, conv_w, conv_b, dt_bias, a_log, d_skip, ssm_norm, w_out, ffn2_norm, ffn2_w_gate, ffn2_w_up, ffn2_w_down, ple_norm, ple_w_gate, ple_b_gate, ple_w_proj, final_norm, loss_target, m_ffn1_norm, m_ffn1_w_gate, m_ffn1_w_up, m_ffn1_w_down, m_mix_norm, m_w_in, m_gm_ln_g, m_gm_ln_b, m_gm_w_s, m_gm_b_s, m_gm_out_norm, m_conv_w, m_conv_b, m_dt_bias, m_a_log, m_d_skip, m_ssm_norm, m_w_out, m_ffn2_norm, m_ffn2_w_gate, m_ffn2_w_up, m_ffn2_w_down, m_ple_norm, m_ple_w_gate, m_ple_b_gate, m_ple_w_proj, m_final_norm, v_ffn1_norm, v_ffn1_w_gate, v_ffn1_w_up, v_ffn1_w_down, v_mix_norm, v_w_in, v_gm_ln_g, v_gm_ln_b, v_gm_w_s, v_gm_b_s, v_gm_out_norm, v_conv_w, v_conv_b, v_dt_bias, v_a_log, v_d_skip, v_ssm_norm, v_w_out, v_ffn2_norm, v_ffn2_w_gate, v_ffn2_w_up, v_ffn2_w_down, v_ple_norm, v_ple_w_gate, v_ple_b_gate, v_ple_w_proj, v_final_norm):
    args = locals()
    w = {name: args[name] for name in WEIGHTS}
    m = {name: args["m_" + name] for name in WEIGHTS}
    v = {name: args["v_" + name] for name in WEIGHTS}
    loss, dx, outs = _step(x[0], p[0, 0], loss_target[0], w, m, v)
    loss = lax.psum(loss, AXES)
    return (loss, dx[None], *outs)
```

```python
import functools
from typing import NamedTuple

import jax
import jax.numpy as jnp
from jax import lax
from jax.experimental import pallas as pl
from jax.experimental.pallas import tpu as pltpu

F32 = jnp.float32
BF16 = jnp.bfloat16
HIGHEST = lax.Precision.HIGHEST
MESH = pl.DeviceIdType.MESH
AXES = ("x", "y", "c")
N_DEV = 8

D_MODEL = 1024
D_FF = 2816
D_PLE = 256
GM_WIDTH = 1024
GM_HEADS = 8
GM_HEAD_DIM = 128
CHUNK = 128
SSM_WIDTH = 1024
SSM_HEADS = 16
SSM_HEAD_DIM = 64
SSM_GROUPS = 2
SSM_STATE = 128
SSM_CONV = 4
CONV_DIM = SSM_WIDTH + 2 * SSM_GROUPS * SSM_STATE
IN_PROJ = 2 * GM_WIDTH + SSM_WIDTH + CONV_DIM + SSM_HEADS
LANES = 128
BF16_ROWS = 16
F32_ROWS = 8
IN_PROJ_PAD = IN_PROJ - SSM_HEADS + LANES
UV_W = 2 * GM_WIDTH
ZXD_W = IN_PROJ_PAD - UV_W
HALO = 8
EPS = 1e-6

ADAM_LR = 0.001
ADAM_B1 = 0.9
ADAM_B2 = 0.999
ADAM_EPS = 1e-08
ADAM_WD = 0.01
ADAM_STEP = 10

VMEM_LIMIT = 56 * 1024 * 1024
PACK_COLS = 1024


def _rms(x, g):
    return x * lax.rsqrt(jnp.mean(x * x, axis=-1, keepdims=True) + EPS) * g


def _gelu(x):
    return 0.5 * x * (1.0 + lax.erf(x * (2.0 ** -0.5)))


def _silu(x):
    return x * jax.nn.sigmoid(x)


def _dot(a, b):
    return jnp.dot(a.astype(BF16), b.astype(BF16), preferred_element_type=F32)


def _dot_nt(a, b):
    return lax.dot_general(a.astype(BF16), b.astype(BF16), (((1,), (1,)), ((), ())), preferred_element_type=F32)


def _dot_tn(a, b):
    return lax.dot_general(a.astype(BF16), b.astype(BF16), (((0,), (0,)), ((), ())), preferred_element_type=F32)


def _hdot_tn(a, b):
    return lax.dot_general(a, b, (((0,), (0,)), ((), ())), precision=HIGHEST, preferred_element_type=F32)


def _split3(x):
    hi = x.astype(BF16)
    rest = x - hi.astype(F32)
    mid = rest.astype(BF16)
    return hi, mid, (rest - mid.astype(F32)).astype(BF16)


def _exact_dot(x, mask, dims, x_first=True, n_terms=3):
    terms = [lax.dot_general(*((t, mask) if x_first else (mask, t)), (dims, ((), ())), preferred_element_type=F32)
             for t in _split3(x)[:n_terms]]
    total = terms[0]
    for term in terms[1:]:
        total = total + term
    return total


def _mask_product(fwd_dims, fwd_x_first, bwd_dims, bwd_x_first, bwd_terms=3):
    @jax.custom_vjp
    def product(x, mask):
        return _exact_dot(x, mask, fwd_dims, fwd_x_first)

    def fwd(x, mask):
        return product(x, mask), mask

    def bwd(mask, g):
        return _exact_dot(g, mask, bwd_dims, bwd_x_first, bwd_terms), jnp.zeros_like(mask)

    product.defvjp(fwd, bwd)
    return product


_widen = _mask_product(((1,), (0,)), True, ((1,), (1,)), True, bwd_terms=2)
_cumsum_rows = _mask_product(((1,), (0,)), False, ((0,), (0,)), False)
_cumsum_cols = _mask_product(((0,), (0,)), True, ((1,), (1,)), False)


class _Pieces(NamedTuple):
    gathered: jax.Array
    row_off: int
    rows: int


class _Comm(NamedTuple):
    phases: object
    src: jax.Array
    dst: jax.ShapeDtypeStruct


def _tiled(body, name, n_steps, tiled_in, full_in, big_in, tiled_out, acc_out, scratch=(), reverse=False, comm=None):
    n_t, n_f, n_b, n_to, n_a = len(tiled_in), len(full_in), len(big_in), len(tiled_out), len(acc_out)
    n_c = 1 if comm else 0

    def row(i):
        return n_steps - 1 - i if reverse else i

    in_specs, args = [], []
    for arr, br, bc, cb in tiled_in:
        if callable(cb):
            in_specs.append(pl.BlockSpec((br, bc), cb))
        else:
            in_specs.append(pl.BlockSpec((br, bc), functools.partial(lambda i, cb: (row(i), cb), cb=cb)))
        args.append(arr)
    for arr in full_in:
        in_specs.append(pl.BlockSpec(arr.shape, functools.partial(lambda i, nd: (0,) * nd, nd=arr.ndim)))
        args.append(arr)
    big_shapes, n_copies = [], 0
    for big in big_in:
        in_specs.append(pl.BlockSpec(memory_space=pl.ANY))
        if isinstance(big, _Pieces):
            args.append(big.gathered)
            big_shapes.append(((N_DEV * big.rows, PACK_COLS), big.gathered.dtype))
            n_copies += N_DEV
        else:
            args.append(big)
            big_shapes.append((big.shape, big.dtype))
            n_copies += 1
    if comm:
        in_specs.append(pl.BlockSpec(memory_space=pl.ANY))
        args.append(comm.src)
    out_specs, out_shape = [], []
    for rows, cols, dt, br in tiled_out:
        out_specs.append(pl.BlockSpec((br, cols), lambda i: (row(i), 0)))
        out_shape.append(jax.ShapeDtypeStruct((rows, cols), dt))
    for shp, dt in acc_out:
        out_specs.append(pl.BlockSpec(shp, functools.partial(lambda i, nd: (0,) * nd, nd=len(shp))))
        out_shape.append(jax.ShapeDtypeStruct(shp, dt))
    if comm:
        out_specs.append(pl.BlockSpec(memory_space=pl.ANY))
        out_shape.append(comm.dst)
    scratch_shapes = [pltpu.VMEM(shp, dt) for shp, dt in big_shapes] + list(scratch)
    if n_copies:
        scratch_shapes.append(pltpu.SemaphoreType.DMA((n_copies,)))
    if comm:
        scratch_shapes += [pltpu.SemaphoreType.DMA((N_DEV - 1,)), pltpu.SemaphoreType.DMA((N_DEV - 1,)), pltpu.SemaphoreType.DMA]

    def kern(*refs):
        n_in = n_t + n_f + n_b + n_c
        ins = refs[: n_t + n_f]
        big_hbm = refs[n_t + n_f : n_t + n_f + n_b]
        outs = refs[n_in : n_in + n_to + n_a]
        rest = refs[n_in + n_to + n_a + n_c :]
        big_vmem, scr = rest[:n_b], rest[n_b:]
        if comm:
            scr, comm_sems = scr[:-3], scr[-3:]
            comm_start, comm_mid, comm_finish = comm.phases(refs[n_in - 1], refs[n_in + n_to + n_a], *comm_sems)
        if n_copies:
            scr, copy_sems = scr[:-1], scr[-1]
        step = pl.program_id(0)

        @pl.when(step == 0)
        def _():
            copies = []
            for big, src, dst in zip(big_in, big_hbm, big_vmem):
                if isinstance(big, _Pieces):
                    for j in range(N_DEV):
                        copies.append((src.at[j, pl.ds(big.row_off, big.rows), :], dst.at[pl.ds(j * big.rows, big.rows), :]))
                else:
                    copies.append((src, dst))
            copies = [pltpu.make_async_copy(a, b, copy_sems.at[k]) for k, (a, b) in enumerate(copies)]
            for cp in copies:
                cp.start()
            for cp in copies:
                cp.wait()
            for acc in outs[n_to:]:
                acc[...] = jnp.zeros(acc.shape, acc.dtype)
            if comm:
                comm_start()

        body(row(step), *ins, *big_vmem, *outs, *scr)
        if comm:
            pl.when(step == (n_steps - 1) // 2)(comm_mid)
            pl.when(step == n_steps - 1)(comm_finish)

    res = pl.pallas_call(
        kern,
        out_shape=out_shape,
        grid=(n_steps,),
        in_specs=in_specs,
        out_specs=out_specs,
        scratch_shapes=scratch_shapes,
        name=name,
        compiler_params=pltpu.CompilerParams(dimension_semantics=("arbitrary",), vmem_limit_bytes=VMEM_LIMIT),
    )(*args)
    return res


FWD_CHUNKS = ((0, 1536), (1536, D_FF))
DGRAD_CHUNKS = ((0, 1024), (1024, 2048), (2048, D_FF))
FFN_TM = 256


def _ffn_fwd(h, g, wg_t, wu_t, wd, name, comm=None, mixed=None):
    T = h.shape[0]

    def ffn(x, g_ref, wg_ref, wu_ref, wd_ref, o_ref, n_ref, a_ref, b_ref, s_ref):
        n = _rms(x, g_ref[...]).astype(BF16)
        n_ref[...] = n
        f = jnp.zeros(x.shape, F32)
        for lo, hi in FWD_CHUNKS:
            a = _dot_nt(n, wg_ref[lo:hi, :])
            b = _dot_nt(n, wu_ref[lo:hi, :])
            s = (_silu(a) * b).astype(BF16)
            a_ref[:, lo:hi] = a.astype(BF16)
            b_ref[:, lo:hi] = b.astype(BF16)
            s_ref[:, lo:hi] = s
            f = f + jnp.dot(s, wd_ref[lo:hi, :], preferred_element_type=F32)
        o_ref[...] = x + 0.5 * f

    def body_plain(i, h_ref, *refs):
        ffn(h_ref[...], *refs)

    def body_mixed(i, h_ref, ya_ref, yb_ref, g_ref, wg_ref, wu_ref, wd_ref, wo_ref, o_ref, n_ref, a_ref, b_ref, s_ref, x_ref):
        x = (h_ref[...] + jnp.dot(ya_ref[...], wo_ref[:GM_WIDTH, :], preferred_element_type=F32)
             + jnp.dot(yb_ref[...], wo_ref[GM_WIDTH:, :], preferred_element_type=F32))
        x_ref[...] = x
        ffn(x, g_ref, wg_ref, wu_ref, wd_ref, o_ref, n_ref, a_ref, b_ref, s_ref)

    body = body_mixed if mixed else body_plain
    tiled_in, big_in = [(h, FFN_TM, D_MODEL, 0)], [wg_t, wu_t, wd]
    tiled_out = [(T, D_MODEL, F32, FFN_TM), (T, D_MODEL, BF16, FFN_TM), (T, D_FF, BF16, FFN_TM), (T, D_FF, BF16, FFN_TM),
                 (T, D_FF, BF16, FFN_TM)]
    if mixed:
        tiled_in += [(mixed[0], FFN_TM, GM_WIDTH, 0), (mixed[1], FFN_TM, SSM_WIDTH, 0)]
        big_in.append(mixed[2])
        tiled_out.append((T, D_MODEL, F32, FFN_TM))
    return _tiled(body, name, T // FFN_TM, tiled_in, [g], big_in, tiled_out, [], comm=comm)


def _ffn_dgrad(h, dout, a16, b16, g, wg_t, wu_t, wd, name):
    T = h.shape[0]

    def body(i, h_ref, do_ref, a_ref, b_ref, g_ref, wg_ref, wu_ref, wd_ref, dh_ref, da_ref, db_ref, dg_ref):
        dout = do_ref[...]
        _, rms_vjp = jax.vjp(_rms, h_ref[...], g_ref[...])
        dfo = (0.5 * dout).astype(BF16)
        dn = jnp.zeros(dout.shape, F32)
        for lo, hi in DGRAD_CHUNKS:
            a = a_ref[:, lo:hi].astype(F32)
            b = b_ref[:, lo:hi].astype(F32)
            sg = jax.nn.sigmoid(a)
            ds = _dot_nt(dfo, wd_ref[lo:hi, :])
            db = (ds * (a * sg)).astype(BF16)
            da = (ds * b * (sg * (1.0 + a * (1.0 - sg)))).astype(BF16)
            dn = dn + _dot(da, wg_ref[lo:hi, :]) + _dot(db, wu_ref[lo:hi, :])
            da_ref[:, lo:hi] = da
            db_ref[:, lo:hi] = db
        dx, dg = rms_vjp(dn)
        dh_ref[...] = dout + dx
        dg_ref[...] += dg

    return _tiled(body, name, T // FFN_TM,
                  [(h, FFN_TM, D_MODEL, 0), (dout, FFN_TM, D_MODEL, 0), (a16, FFN_TM, D_FF, 0), (b16, FFN_TM, D_FF, 0)],
                  [g], [wg_t, wu_t, wd],
                  [(T, D_MODEL, F32, FFN_TM), (T, D_FF, BF16, FFN_TM), (T, D_FF, BF16, FFN_TM)], [((1, D_MODEL), F32)])


FF_BN = D_FF // 2
DOWN_BN, DOWN_BK = 512, 1024
SQUARE_BN = 1024
ZXD_BN = ZXD_W // 3


def _wgrad(a, b, bn, name, scale=None, transpose_out=False, bk=2048, comm=None):
    T, M = a.shape
    N = b.shape[1]
    bk = min(bk, T)
    assert M % LANES == 0 and N % bn == 0 and T % bk == 0
    n_j, n_k = N // bn, T // bk
    n_c = 1 if comm else 0

    def kern(*refs):
        a_ref, b_ref, o_ref, acc_ref = refs[0], refs[1], refs[2 + n_c], refs[3 + 2 * n_c]
        j, k = pl.program_id(0), pl.program_id(1)
        if comm:
            comm_start, comm_mid, comm_finish = comm.phases(refs[2], refs[4], *refs[6:])
            pl.when((j == 0) & (k == 0))(comm_start)

        @pl.when(k == 0)
        def _():
            acc_ref[...] = jnp.zeros(acc_ref.shape, F32)

        bv = b_ref[...]
        if scale is not None:
            bv = bv * scale
        acc_ref[...] += _dot_tn(a_ref[...], bv)

        @pl.when(k == n_k - 1)
        def _():
            acc = acc_ref[...]
            o_ref[...] = (acc.T if transpose_out else acc).astype(BF16)

        if comm:
            pl.when((j == (n_j - 1) // 2) & (k == n_k - 1))(comm_mid)
            pl.when((j == n_j - 1) & (k == n_k - 1))(comm_finish)

    if transpose_out:
        out_shape, out_spec = (N, M), pl.BlockSpec((bn, M), lambda j, k: (j, 0))
    else:
        out_shape, out_spec = (M, N), pl.BlockSpec((M, bn), lambda j, k: (0, j))
    any_spec = pl.BlockSpec(memory_space=pl.ANY)
    comm_sems = [pltpu.SemaphoreType.DMA((N_DEV - 1,)), pltpu.SemaphoreType.DMA((N_DEV - 1,)), pltpu.SemaphoreType.DMA]
    res = pl.pallas_call(
        kern,
        out_shape=[jax.ShapeDtypeStruct(out_shape, BF16)] + ([comm.dst] if comm else []),
        grid=(n_j, n_k),
        in_specs=[pl.BlockSpec((bk, M), lambda j, k: (k, 0)), pl.BlockSpec((bk, bn), lambda j, k: (k, j))] + [any_spec] * n_c,
        out_specs=[out_spec] + [any_spec] * n_c,
        scratch_shapes=[pltpu.VMEM((M, bn), F32)] + (comm_sems if comm else []),
        name=name,
        compiler_params=pltpu.CompilerParams(dimension_semantics=("arbitrary", "arbitrary"), vmem_limit_bytes=VMEM_LIMIT),
    )(a, b, *([comm.src] if comm else []))
    return res if comm else res[0]


PROJ_TM = 512
PROJ_DGRAD_TM = 256
UVZ_W = 2 * GM_WIDTH + SSM_WIDTH
PROJ_KEPT = UVZ_W + LANES
Z_BLK = 2 * GM_WIDTH // SSM_WIDTH
DT_BLK = UVZ_W // LANES


def _mix_in_fwd(h, g, w_in_t, conv_w, conv_b):
    T = h.shape[0]

    def body(i, h_ref, g_ref, cw_ref, cb_ref, w_ref, p_ref, n_ref, x_ref, xc_ref, ext_ref):
        @pl.when(i == 0)
        def _():
            ext_ref[0:HALO, :] = jnp.zeros((HALO, CONV_DIM), F32)

        n = _rms(h_ref[...], g_ref[...]).astype(BF16)
        n_ref[...] = n
        proj = _dot_nt(n, w_ref[...])
        p_ref[:, :UVZ_W] = proj[:, :UVZ_W]
        p_ref[:, UVZ_W:] = proj[:, UVZ_W + CONV_DIM:]
        xbc = proj[:, UVZ_W:UVZ_W + CONV_DIM]
        x_ref[...] = xbc.astype(BF16)
        ext_ref[HALO:, :] = xbc
        xc_ref[...] = _conv_taps(ext_ref, cw_ref[...], cb_ref[...], PROJ_TM)
        ext_ref[0:HALO, :] = ext_ref[PROJ_TM:PROJ_TM + HALO, :]

    return _tiled(body, "mix_in_fwd", T // PROJ_TM, [(h, PROJ_TM, D_MODEL, 0)], [g, conv_w, conv_b], [w_in_t],
                  [(T, PROJ_KEPT, F32, PROJ_TM), (T, D_MODEL, BF16, PROJ_TM), (T, CONV_DIM, BF16, PROJ_TM),
                   (T, CONV_DIM, F32, PROJ_TM)], [],
                  scratch=[pltpu.VMEM((HALO + PROJ_TM, CONV_DIM), F32)])


def _mix_in_dgrad(h, dh_in, dp_uv, dp_zxd, g, w_in_t, comm=None):
    T = h.shape[0]

    def body(i, h_ref, dh_ref, duv_ref, dzxd_ref, g_ref, w_ref, o_ref, dg_ref):
        dn = _dot(duv_ref[...], w_ref[:UV_W, :]) + _dot(dzxd_ref[...], w_ref[UV_W:, :])
        _, rms_vjp = jax.vjp(_rms, h_ref[...], g_ref[...])
        dx, dg = rms_vjp(dn)
        o_ref[...] = dh_ref[...] + dx
        dg_ref[...] += dg

    return _tiled(body, "mix_in_dgrad", T // PROJ_DGRAD_TM,
                  [(h, PROJ_DGRAD_TM, D_MODEL, 0), (dh_in, PROJ_DGRAD_TM, D_MODEL, 0), (dp_uv, PROJ_DGRAD_TM, UV_W, 0),
                   (dp_zxd, PROJ_DGRAD_TM, ZXD_W, 0)], [g], [w_in_t],
                  [(T, D_MODEL, F32, PROJ_DGRAD_TM)], [((1, D_MODEL), F32)], comm=comm)


def _out_proj_dgrad(dh, w_out):
    T = dh.shape[0]

    def body(i, dh_ref, w_ref, dya_ref, dyb_ref):
        d = dh_ref[...].astype(BF16)
        dya_ref[...] = _dot_nt(d, w_ref[:GM_WIDTH, :])
        dyb_ref[...] = _dot_nt(d, w_ref[GM_WIDTH:, :])

    return _tiled(body, "out_proj_dgrad", T // PROJ_TM, [(dh, PROJ_TM, D_MODEL, 0)], [], [w_out],
                  [(T, GM_WIDTH, F32, PROJ_TM), (T, SSM_WIDTH, F32, PROJ_TM)], [])


def _gm_chunk(u, v, ln_g, ln_b, b_st, out_g, *w_heads):
    ug = _gelu(u)
    vg = _gelu(v)
    mu = jnp.mean(vg, axis=-1, keepdims=True)
    xc = vg - mu
    vn = xc * lax.rsqrt(jnp.mean(xc * xc, axis=-1, keepdims=True) + EPS) * ln_g + ln_b
    t_idx = lax.broadcasted_iota(jnp.int32, (CHUNK, CHUNK), 0)
    s_idx = lax.broadcasted_iota(jnp.int32, (CHUNK, CHUNK), 1)
    causal = t_idx >= s_idx
    mixed = []
    for hd in range(GM_HEADS):
        wm = jnp.where(causal, w_heads[hd], 0.0)
        cols = slice(hd * GM_HEAD_DIM, (hd + 1) * GM_HEAD_DIM)
        mixed.append(_dot(wm, vn[:, cols]) + b_st[:, hd:hd + 1])
    ya0 = ug * jnp.concatenate(mixed, axis=1)
    return _rms(ya0, out_g)


GM_FWD_CHUNKS = 2


def _gm_fwd(proj, ln_g, ln_b, w_s, b_st, out_g):
    T = proj.shape[0]

    rows = GM_FWD_CHUNKS * CHUNK

    def body(i, u_ref, v_ref, lg_ref, lb_ref, w_ref, bs_ref, og_ref, ya_ref):
        w_heads = [w_ref[hd] for hd in range(GM_HEADS)]
        for c in range(GM_FWD_CHUNKS):
            tok = pl.ds(c * CHUNK, CHUNK)
            ya = _gm_chunk(u_ref[tok, :], v_ref[tok, :], lg_ref[...], lb_ref[...], bs_ref[...], og_ref[...], *w_heads)
            ya_ref[tok, :] = ya.astype(BF16)

    return _tiled(body, "gmlp_fwd", T // rows, [(proj, rows, GM_WIDTH, 0), (proj, rows, GM_WIDTH, 1)],
                  [ln_g, ln_b, w_s, b_st, out_g], [], [(T, GM_WIDTH, BF16, rows)], [])[0]


def _gm_bwd(proj, dya, ln_g, ln_b, w_s, b_st, out_g):
    T = proj.shape[0]

    def body(i, u_ref, v_ref, dy_ref, lg_ref, lb_ref, w_ref, bs_ref, og_ref, duv_ref, dlg_ref, dlb_ref, dw_ref, dbs_ref,
             dog_ref):
        w_heads = [w_ref[hd] for hd in range(GM_HEADS)]
        _, vjp = jax.vjp(_gm_chunk, u_ref[...], v_ref[...], lg_ref[...], lb_ref[...], bs_ref[...], og_ref[...], *w_heads)
        grads = vjp(dy_ref[...])
        duv_ref[:, :GM_WIDTH] = grads[0].astype(BF16)
        duv_ref[:, GM_WIDTH:] = grads[1].astype(BF16)
        dlg_ref[...] += grads[2]
        dlb_ref[...] += grads[3]
        dbs_ref[...] += grads[4]
        dog_ref[...] += grads[5]
        for hd in range(GM_HEADS):
            dw_ref[hd] += grads[6 + hd]

    return _tiled(body, "gmlp_bwd", T // CHUNK,
                  [(proj, CHUNK, GM_WIDTH, 0), (proj, CHUNK, GM_WIDTH, 1), (dya, CHUNK, GM_WIDTH, 0)],
                  [ln_g, ln_b, w_s, b_st, out_g], [], [(T, UV_W, BF16, CHUNK)],
                  [((1, GM_WIDTH), F32), ((1, GM_WIDTH), F32), ((GM_HEADS, CHUNK, CHUNK), F32),
                   ((CHUNK, GM_HEADS), F32), ((1, GM_WIDTH), F32)])


def _ssd_chunk(xc, z, dtr, s_in, dt_bias, a_log, d_skip, norm_g):
    half = SSM_WIDTH // SSM_GROUPS
    l_idx = lax.broadcasted_iota(jnp.int32, (CHUNK, CHUNK), 0)
    s_idx = lax.broadcasted_iota(jnp.int32, (CHUNK, CHUNK), 1)
    causal = l_idx >= s_idx
    head_of_col = lax.broadcasted_iota(jnp.int32, (SSM_HEADS, SSM_WIDTH), 1) // SSM_HEAD_DIM
    expand = (head_of_col == lax.broadcasted_iota(jnp.int32, (SSM_HEADS, SSM_WIDTH), 0)).astype(BF16)

    xcs = _silu(xc)
    xs = xcs[:, :SSM_WIDTH]
    dt = jax.nn.softplus(dtr + dt_bias)
    adt = dt * (-jnp.exp(a_log))
    acs = _cumsum_rows(adt, causal.astype(BF16))
    acs_t = _cumsum_cols(adt, (l_idx <= s_idx).astype(BF16))
    tot = acs[CHUNK - 1:CHUNK, :]
    dt_w = _widen(dt, expand)
    out_decay_w = _widen(jnp.exp(acs), expand)
    state_decay_w = _widen(jnp.exp(tot - acs), expand)
    chunk_decay_w = _widen(jnp.exp(tot), expand)
    d_skip_w = _widen(d_skip, expand)
    xdt = xs * dt_w
    xdt_decayed = xdt * state_decay_w

    y_diag, y_off, states = [], [], []
    for grp in range(SSM_GROUPS):
        b0 = SSM_WIDTH + grp * SSM_STATE
        c0 = SSM_WIDTH + SSM_GROUPS * SSM_STATE + grp * SSM_STATE
        bm = xcs[:, b0:b0 + SSM_STATE].astype(BF16)
        cm = xcs[:, c0:c0 + SSM_STATE].astype(BF16)
        cb = _dot_nt(cm, bm)
        for k in range(grp * SSM_HEADS // SSM_GROUPS, (grp + 1) * SSM_HEADS // SSM_GROUPS):
            decay = jnp.exp(jnp.where(causal, acs[:, k:k + 1] - acs_t[k:k + 1, :], -jnp.inf))
            y_diag.append(_dot(cb * decay, xdt[:, k * SSM_HEAD_DIM:(k + 1) * SSM_HEAD_DIM]))
        cols = slice(grp * half, (grp + 1) * half)
        states.append(_dot_tn(bm, xdt_decayed[:, cols]))
        y_off.append(_dot(cm, s_in[:, cols]))
    y = jnp.concatenate(y_diag, axis=1) + jnp.concatenate(y_off, axis=1) * out_decay_w + xs * d_skip_w
    s_out = s_in * chunk_decay_w + jnp.concatenate(states, axis=1)
    y = y * _silu(z)
    normed = []
    for grp in range(SSM_GROUPS):
        yg = y[:, grp * half:(grp + 1) * half]
        normed.append(yg * lax.rsqrt(jnp.mean(yg * yg, axis=-1, keepdims=True) + EPS))
    return jnp.concatenate(normed, axis=1) * norm_g, s_out


def _sum_row_tiles(x):
    return x.reshape(x.shape[0] // F32_ROWS, F32_ROWS, x.shape[1]).sum(axis=0)


def _conv_taps(ext_ref, w, b, rows):
    y = b
    for k in range(SSM_CONV):
        y = y + w[k:k + 1, :] * ext_ref[pl.ds(HALO - (SSM_CONV - 1) + k, rows), :]
    return y


def _ssd_fwd(proj, xc, dt_bias, a_log, d_skip, norm_g, comm=None):
    T = proj.shape[0]
    n_chunks = T // CHUNK

    def body(i, z_ref, xc_ref, dt_ref, dtb_ref, al_ref, dsk_ref, ng_ref, yb_ref, sin_ref, st_ref):
        @pl.when(i == 0)
        def _():
            st_ref[...] = jnp.zeros(st_ref.shape, F32)

        s_in = st_ref[...]
        yb, s_out = _ssd_chunk(xc_ref[...], z_ref[...], dt_ref[:, 0:SSM_HEADS], s_in, dtb_ref[...], al_ref[...],
                               dsk_ref[...], ng_ref[...])
        yb_ref[...] = yb.astype(BF16)
        sin_ref[...] = s_in
        st_ref[...] = s_out

    return _tiled(body, "ssd_fwd", n_chunks,
                  [(proj, CHUNK, SSM_WIDTH, Z_BLK), (xc, CHUNK, CONV_DIM, 0), (proj, CHUNK, LANES, DT_BLK)],
                  [dt_bias, a_log, d_skip, norm_g], [],
                  [(T, SSM_WIDTH, BF16, CHUNK), (n_chunks * SSM_STATE, SSM_WIDTH, F32, SSM_STATE)], [],
                  scratch=[pltpu.VMEM((SSM_STATE, SSM_WIDTH), F32)], comm=comm)


def _ssd_bwd(proj, x16, xc, dyb, s_all, conv_w, dt_bias, a_log, d_skip, norm_g, comm=None):
    T = proj.shape[0]
    n_chunks = T // CHUNK

    def body(i, z_ref, x_ref, xc_ref, dt_ref, dy_ref, sin_ref, cw_ref, dtb_ref, al_ref, dsk_ref, ng_ref,
             dzxd_ref, dcw_ref, dcb_ref, ddtb_ref, dal_ref, ddsk_ref, dng_ref, dext_ref, dst_ref, cw_acc, cb_acc):
        @pl.when(i == n_chunks - 1)
        def _():
            dext_ref[CHUNK:, :] = jnp.zeros((HALO, CONV_DIM), F32)
            dst_ref[...] = jnp.zeros(dst_ref.shape, F32)
            cw_acc[...] = jnp.zeros(cw_acc.shape, F32)
            cb_acc[...] = jnp.zeros(cb_acc.shape, F32)

        _, vjp = jax.vjp(_ssd_chunk, xc_ref[...], z_ref[...], dt_ref[:, 0:SSM_HEADS], sin_ref[...], dtb_ref[...], al_ref[...],
                         dsk_ref[...], ng_ref[...])
        dxc, dz, ddtr, ds_in, ddtb, dal, ddsk, dng = vjp((dy_ref[...], dst_ref[...]))
        dst_ref[...] = ds_in
        ddtb_ref[...] += ddtb
        dal_ref[...] += dal
        ddsk_ref[...] += ddsk
        dng_ref[...] += dng
        dext_ref[0:CHUNK, :] = dxc
        cw = cw_ref[...]
        x = x_ref[...].astype(F32)
        dx = jnp.zeros((CHUNK, CONV_DIM), F32)
        for k in range(SSM_CONV):
            shifted = dext_ref[pl.ds(SSM_CONV - 1 - k, CHUNK), :]
            dx = dx + cw[k:k + 1, :] * shifted
            cw_acc[k] += _sum_row_tiles(shifted * x)
        cb_acc[...] += _sum_row_tiles(dxc)

        @pl.when(i == 0)
        def _():
            dcw_ref[...] = jnp.sum(cw_acc[...], axis=1)
            dcb_ref[...] = jnp.sum(cb_acc[...], axis=0, keepdims=True)

        dext_ref[CHUNK:, :] = dext_ref[0:HALO, :]
        dzxd_ref[:, 0:SSM_WIDTH] = dz.astype(BF16)
        dzxd_ref[:, SSM_WIDTH:SSM_WIDTH + CONV_DIM] = dx.astype(BF16)
        dzxd_ref[:, SSM_WIDTH + CONV_DIM:] = jnp.concatenate(
            [ddtr, jnp.zeros((CHUNK, LANES - SSM_HEADS), F32)], axis=1).astype(BF16)

    return _tiled(body, "ssd_bwd", n_chunks,
                  [(proj, CHUNK, SSM_WIDTH, Z_BLK), (x16, CHUNK, CONV_DIM, 0), (xc, CHUNK, CONV_DIM, 0),
                   (proj, CHUNK, LANES, DT_BLK), (dyb, CHUNK, SSM_WIDTH, 0), (s_all, SSM_STATE, SSM_WIDTH, 0)],
                  [conv_w, dt_bias, a_log, d_skip, norm_g], [],
                  [(T, ZXD_W, BF16, CHUNK)],
                  [((SSM_CONV, CONV_DIM), F32), ((1, CONV_DIM), F32), ((1, SSM_HEADS), F32), ((1, SSM_HEADS), F32),
                   ((1, SSM_HEADS), F32), ((1, SSM_WIDTH), F32)],
                  scratch=[pltpu.VMEM((CHUNK + HALO, CONV_DIM), F32), pltpu.VMEM((SSM_STATE, SSM_WIDTH), F32),
                           pltpu.VMEM((SSM_CONV, F32_ROWS, CONV_DIM), F32), pltpu.VMEM((F32_ROWS, CONV_DIM), F32)],
                  reverse=True, comm=comm)


TAIL_TM = 512


def _tail(h, p, target, ple_norm, w_gate, b_gate, w_proj_t, final_norm):
    T = h.shape[0]

    def head(x, pre, pp, b_g, f_norm, tgt):
        gate = jax.nn.sigmoid(pre + b_g)
        out = _rms(x + gate * pp, f_norm)
        err = out - tgt
        return 0.5 * jnp.sum(jnp.mean(err * err, axis=-1, keepdims=True), axis=0, keepdims=True)

    def body(i, h_ref, p_ref, t_ref, pn_ref, bg_ref, fn_ref, wg_ref, wp_ref, dh_ref, loss_ref, dwg_ref, dwp_ref, dpn_ref,
             dbg_ref, dfn_ref):
        x = h_ref[...]
        n4f, n_vjp = jax.vjp(_rms, x, pn_ref[...])
        n4 = n4f.astype(BF16)
        pre = jnp.dot(n4, wg_ref[...], preferred_element_type=F32)
        p16 = p_ref[...].astype(BF16)
        pp = _dot_nt(p16, wp_ref[...])
        loss, h_vjp = jax.vjp(functools.partial(head, tgt=t_ref[...]), x, pre, pp, bg_ref[...], fn_ref[...])
        dx, dpre, dpp, dbg, dfn = h_vjp(jnp.ones((1, 1), F32))
        dpre16 = dpre.astype(BF16)
        dn4 = _dot_nt(dpre16, wg_ref[...])
        dx2, dpn = n_vjp(dn4)
        dh_ref[...] = dx + dx2
        loss_ref[...] += loss
        dwg_ref[...] += _dot_tn(n4, dpre16)
        dwp_ref[...] += _dot_tn(p16, dpp)
        dpn_ref[...] += dpn
        dbg_ref[...] += dbg
        dfn_ref[...] += dfn

    return _tiled(body, "tail", T // TAIL_TM,
                  [(h, TAIL_TM, D_MODEL, 0), (p, TAIL_TM, D_PLE, 0), (target, TAIL_TM, D_MODEL, 0)],
                  [ple_norm, b_gate, final_norm], [w_gate, w_proj_t],
                  [(T, D_MODEL, F32, TAIL_TM)],
                  [((1, 1), F32), ((D_MODEL, D_MODEL), F32), ((D_PLE, D_MODEL), F32), ((1, D_MODEL), F32),
                   ((1, D_MODEL), F32), ((1, D_MODEL), F32)])


def _gather_phases(x_ref, out_ref, send_sems, recv_sems, local_sem):
    mx, my, mc = lax.axis_index("x"), lax.axis_index("y"), lax.axis_index("c")
    me, sibling = (mx, my, mc), (mx, my, 1 - mc)
    chips = [(1 - mx, my), (mx, 1 - my), (1 - mx, 1 - my)]

    def rows(px, py, pc):
        return out_ref.at[4 * px + 2 * py + pc]

    def copy(k, block, to, src=None):
        return pltpu.make_async_remote_copy(
            src_ref=rows(*block) if src is None else src, dst_ref=rows(*block),
            send_sem=send_sems.at[k], recv_sem=recv_sems.at[k], device_id=to, device_id_type=MESH)

    mine = pltpu.make_async_copy(x_ref, rows(*me), local_sem)
    first = [copy(0, me, sibling, src=x_ref)] + [copy(1 + j, me, (*chip, mc), src=x_ref) for j, chip in enumerate(chips)]
    passed = [copy(4 + j, (*chip, mc), sibling) for j, chip in enumerate(chips)]

    def start():
        mine.start()
        for cp in first:
            cp.start()

    def mid():
        for j, chip in enumerate(chips):
            copy(1 + j, (*chip, mc), me).wait_recv()
            passed[j].start()

    def finish():
        copy(0, sibling, me).wait_recv()
        for j, chip in enumerate(chips):
            copy(4 + j, (*chip, 1 - mc), me).wait_recv()
        for cp in first + passed:
            cp.wait_send()
        mine.wait()

    return start, mid, finish


def _exchange_phases(x_ref, out_ref, send_sems, recv_sems, local_sem):
    mx, my, mc = lax.axis_index("x"), lax.axis_index("y"), lax.axis_index("c")
    me = 4 * mx + 2 * my + mc
    mine = pltpu.make_async_copy(x_ref.at[me], out_ref.at[me], local_sem)
    copies = []
    for k in range(1, N_DEV):
        px = 1 - mx if k & 4 else mx
        py = 1 - my if k & 2 else my
        pc = 1 - mc if k & 1 else mc
        copies.append(pltpu.make_async_remote_copy(
            src_ref=x_ref.at[4 * px + 2 * py + pc], dst_ref=out_ref.at[me], send_sem=send_sems.at[k - 1],
            recv_sem=recv_sems.at[k - 1], device_id=(px, py, pc), device_id_type=MESH))

    def start():
        mine.start()
        for cp in copies:
            cp.start()

    def finish():
        for cp in copies:
            cp.wait_recv()
        for cp in copies:
            cp.wait_send()
        mine.wait()

    return start, lambda: None, finish


def _gather_comm(x):
    return _Comm(_gather_phases, x, jax.ShapeDtypeStruct((N_DEV,) + x.shape, x.dtype))


def _exchange_comm(x):
    return _Comm(_exchange_phases, x, jax.ShapeDtypeStruct(x.shape, x.dtype))


def _comm_alone(comms, name):
    n = len(comms)

    def body(*refs):
        phases = [comm.phases(refs[k], refs[n + k], *refs[2 * n + 3 * k:2 * n + 3 * k + 3]) for k, comm in enumerate(comms)]
        for step in range(3):
            for phase in phases:
                phase[step]()

    any_spec = pl.BlockSpec(memory_space=pl.ANY)
    return pl.pallas_call(
        body,
        out_shape=[comm.dst for comm in comms],
        in_specs=[any_spec] * n,
        out_specs=[any_spec] * n,
        scratch_shapes=[pltpu.SemaphoreType.DMA((N_DEV - 1,)), pltpu.SemaphoreType.DMA((N_DEV - 1,)), pltpu.SemaphoreType.DMA] * n,
        name=name,
    )(*[comm.src for comm in comms])


def _sum_parts(p_ref):
    g = p_ref[0].astype(F32)
    for j in range(1, N_DEV):
        g = g + p_ref[j].astype(F32)
    return g


def _adamw_store(g, w_ref, m_ref, v_ref, g_ref, d_ref, nm_ref, nv_ref):
    m_new = ADAM_B1 * m_ref[...] + (1.0 - ADAM_B1) * g
    v_new = ADAM_B2 * v_ref[...] + (1.0 - ADAM_B2) * jnp.square(g)
    m_hat = m_new / (1.0 - ADAM_B1 ** ADAM_STEP)
    v_hat = v_new / (1.0 - ADAM_B2 ** ADAM_STEP)
    g_ref[...] = g
    d_ref[...] = -ADAM_LR * (m_hat / (jnp.sqrt(v_hat) + ADAM_EPS) + ADAM_WD * w_ref[...])
    nm_ref[...] = m_new
    nv_ref[...] = v_new


def _adamw_shard(parts, off, transposed, w, m, v, name, n_tiles=1):
    _, r, c = w.shape
    tr = r // n_tiles
    if transposed:
        rows = -(-c // BF16_ROWS) * BF16_ROWS
        window = (N_DEV, rows, tr)
    else:
        assert c == PACK_COLS
        window = (N_DEV, tr, PACK_COLS)

    def kern(p_hbm, w_ref, m_ref, v_ref, g_ref, d_ref, nm_ref, nv_ref, buf, sem):
        i = pl.program_id(0)
        if transposed:
            src = p_hbm.at[:, pl.ds(off, rows), pl.ds(pl.multiple_of(i * tr, LANES), tr)]
        else:
            src = p_hbm.at[:, pl.ds(pl.multiple_of(off + i * tr, BF16_ROWS), tr), :]
        cp = pltpu.make_async_copy(src, buf, sem)
        cp.start()
        cp.wait()
        g = _sum_parts(buf)
        if transposed:
            eye = (lax.broadcasted_iota(jnp.int32, (rows, c), 0) == lax.broadcasted_iota(jnp.int32, (rows, c), 1)).astype(F32)
            g = _hdot_tn(g, eye)
        _adamw_store(g, w_ref, m_ref, v_ref, g_ref, d_ref, nm_ref, nv_ref)

    spec = pl.BlockSpec((None, tr, c), lambda i: (0, i, 0))
    return pl.pallas_call(
        kern,
        out_shape=[jax.ShapeDtypeStruct((1, r, c), F32)] * 4,
        grid=(n_tiles,),
        in_specs=[pl.BlockSpec(memory_space=pl.ANY), spec, spec, spec],
        out_specs=[spec] * 4,
        scratch_shapes=[pltpu.VMEM(window, parts.dtype), pltpu.SemaphoreType.DMA],
        name=name,
        compiler_params=pltpu.CompilerParams(dimension_semantics=("arbitrary",), vmem_limit_bytes=VMEM_LIMIT),
    )(parts, w, m, v)


def _sum_adamw(parts, w, m, v, tr, name):
    _, R, C = parts.shape

    def kern(p_ref, w_ref, m_ref, v_ref, g_ref, d_ref, nm_ref, nv_ref):
        _adamw_store(_sum_parts(p_ref), w_ref, m_ref, v_ref, g_ref, d_ref, nm_ref, nv_ref)

    row_spec = pl.BlockSpec((tr, C), lambda i: (i, 0))
    return pl.pallas_call(
        kern,
        out_shape=[jax.ShapeDtypeStruct((R, C), F32)] * 4,
        grid=(R // tr,),
        in_specs=[pl.BlockSpec((N_DEV, tr, C), lambda i: (0, i, 0)), row_spec, row_spec, row_spec],
        out_specs=[row_spec] * 4,
        name=name,
        compiler_params=pltpu.CompilerParams(dimension_semantics=("arbitrary",), vmem_limit_bytes=VMEM_LIMIT),
    )(parts, w, m, v)


FF_SHARD = D_FF // N_DEV
CONV_SHARD = (SSM_CONV, CONV_DIM // N_DEV)
SHARDS = {"ffn1_w_gate": ((D_MODEL, FF_SHARD), True), "ffn1_w_up": ((D_MODEL, FF_SHARD), True),
          "ffn1_w_down": ((FF_SHARD, D_MODEL), False),
          "ffn2_w_gate": ((D_MODEL, FF_SHARD), True), "ffn2_w_up": ((D_MODEL, FF_SHARD), True),
          "ffn2_w_down": ((FF_SHARD, D_MODEL), False),
          "w_out": ((2 * D_MODEL // N_DEV, D_MODEL), False), "ple_w_gate": ((D_MODEL // N_DEV, D_MODEL), False),
          "w_in": ((D_MODEL, IN_PROJ // N_DEV), True), "ple_w_proj": ((D_PLE, D_MODEL // N_DEV), True),
          "conv_w": (CONV_SHARD, True),
          "conv_w_mid": (CONV_SHARD, True), "conv_w_low": (CONV_SHARD, True)}
BIG = tuple(name for name in SHARDS if not name.startswith("conv_w_"))
SMALL = ("ffn1_norm", "mix_norm", "gm_ln_g", "gm_ln_b", "gm_w_s", "gm_b_s", "gm_out_norm", "conv_b", "dt_bias", "a_log",
         "d_skip", "ssm_norm", "ffn2_norm", "ple_norm", "ple_b_gate", "final_norm")
SMALL_ROWS = 144


def _piece_rows(name):
    shape = SHARDS[name][0]
    return -(-(shape[0] * shape[1]) // PACK_COLS)


def _pad_cols(flat, name):
    pad = _piece_rows(name) * PACK_COLS - flat.shape[-1]
    return flat if pad == 0 else jnp.pad(flat, [(0, 0)] * (flat.ndim - 1) + [(0, pad)])


class _Pack:
    def __init__(self, names, tile_rows):
        self.names, self.tile_rows, self.offsets, off = names, tile_rows, {}, 0
        for name in names:
            self.offsets[name] = off
            off += _piece_rows(name)
        self.rows = -(-off // tile_rows) * tile_rows

    def pack_local(self, vals):
        parts = []
        for name in self.names:
            val = vals[name]
            parts.append(_pad_cols((val.T if SHARDS[name][1] else val).reshape(-1), name))
        flat = jnp.concatenate(parts)
        return jnp.pad(flat, (0, self.rows * PACK_COLS - flat.shape[0])).reshape(self.rows, PACK_COLS)

    def pack_owner_major(self, grads):
        parts, rows = [], 0
        for name in self.names:
            grad, piece_rows = grads[name].astype(BF16), _piece_rows(name)
            if grad.shape != (N_DEV * piece_rows, PACK_COLS):
                grad = _pad_cols(grad.reshape(N_DEV, -1), name)
            parts.append(grad.reshape(N_DEV, piece_rows, PACK_COLS))
            rows += piece_rows
        if rows < self.rows:
            parts.append(jnp.zeros((N_DEV, self.rows - rows, PACK_COLS), BF16))
        return parts[0] if len(parts) == 1 else jnp.concatenate(parts, axis=1)

    def gathered_piece(self, gathered, name):
        shape = SHARDS[name][0]
        rows = gathered[:, self.offsets[name]:self.offsets[name] + _piece_rows(name), :]
        return rows.reshape(N_DEV, -1)[:, :shape[0] * shape[1]]

    def pieces(self, gathered, name):
        return _Pieces(gathered, self.offsets[name], _piece_rows(name))


GATHER_FFN1 = _Pack(("ffn1_w_gate", "ffn1_w_up", "ffn1_w_down"), BF16_ROWS)
GATHER_MIX = _Pack(("w_out", "ple_w_gate", "w_in", "ple_w_proj", "conv_w", "conv_w_mid", "conv_w_low"), BF16_ROWS)
GATHER_FFN2 = _Pack(("ffn2_w_gate", "ffn2_w_up", "ffn2_w_down"), BF16_ROWS)
SCATTER_LATE = _Pack(("ffn2_w_gate", "ffn2_w_up", "ffn2_w_down", "w_out", "ple_w_gate", "ple_w_proj"), BF16_ROWS)
SCATTER_IN = _Pack(("w_in", "conv_w"), BF16_ROWS)
SCATTER_GATE = _Pack(("ffn1_w_gate",), BF16_ROWS)
SCATTER_UP = _Pack(("ffn1_w_up",), BF16_ROWS)
SCATTER_DOWN = _Pack(("ffn1_w_down",), BF16_ROWS)


def _pack_small(vals):
    flat = jnp.concatenate([vals[name].reshape(-1).astype(F32) for name in SMALL])
    return jnp.pad(flat, (0, SMALL_ROWS * PACK_COLS - flat.shape[0])).reshape(SMALL_ROWS, PACK_COLS)


def _unpack_small(packed, shapes):
    out, off = {}, 0
    flat = packed.reshape(-1)
    for name in SMALL:
        n = 1
        for s in shapes[name]:
            n *= s
        out[name] = flat[off:off + n].reshape(shapes[name])
        off += n
    return out


WEIGHTS = ("ffn1_norm", "ffn1_w_gate", "ffn1_w_up", "ffn1_w_down", "mix_norm", "w_in", "gm_ln_g", "gm_ln_b", "gm_w_s",
           "gm_b_s", "gm_out_norm", "conv_w", "conv_b", "dt_bias", "a_log", "d_skip", "ssm_norm", "w_out", "ffn2_norm",
           "ffn2_w_gate", "ffn2_w_up", "ffn2_w_down", "ple_norm", "ple_w_gate", "ple_b_gate", "ple_w_proj", "final_norm")


def _step(x, p, target, w, m, v):
    local = lambda d: {name: d[name][0] for name in BIG}

    shards = {name: val.astype(BF16) for name, val in local(w).items()}
    conv_high = lax.reduce_precision(w["conv_w"][0], 8, 7)
    conv_mid = lax.reduce_precision(w["conv_w"][0] - conv_high, 8, 7)
    shards["conv_w"] = conv_high.astype(BF16)
    shards["conv_w_mid"] = conv_mid.astype(BF16)
    shards["conv_w_low"] = (w["conv_w"][0] - conv_high - conv_mid).astype(BF16)
    g_ffn1 = _comm_alone([_gather_comm(GATHER_FFN1.pack_local(shards))], "gather_ffn1")[0]

    row = lambda name: w[name].reshape(1, -1)
    gm_w_s = w["gm_w_s"][0]
    gm_b_st = jnp.transpose(w["gm_b_s"][0])
    ffn1 = (row("ffn1_norm"),) + tuple(GATHER_FFN1.pieces(g_ffn1, name) for name in GATHER_FFN1.names)
    gm = (row("gm_ln_g"), row("gm_ln_b"), gm_w_s, gm_b_st, row("gm_out_norm"))

    h1, n1, a1, b1, s1, g_mix = _ffn_fwd(x, *ffn1, "ffn1_fwd", comm=_gather_comm(GATHER_MIX.pack_local(shards)))
    w_in_t = GATHER_MIX.gathered_piece(g_mix, "w_in").reshape(IN_PROJ, D_MODEL)
    w_in_t = jnp.concatenate([w_in_t, jnp.zeros((IN_PROJ_PAD - IN_PROJ, D_MODEL), BF16)], axis=0)
    w_proj_t = GATHER_MIX.gathered_piece(g_mix, "ple_w_proj").reshape(D_MODEL, D_PLE)
    conv_w = sum(GATHER_MIX.gathered_piece(g_mix, name).astype(F32) for name in ("conv_w", "conv_w_mid", "conv_w_low"))
    conv_w = conv_w.reshape(CONV_DIM, SSM_CONV).T
    ssd = (row("dt_bias"), row("a_log"), row("d_skip"), row("ssm_norm"))
    w_out = GATHER_MIX.pieces(g_mix, "w_out")

    proj, n2, x16, xc = _mix_in_fwd(h1, row("mix_norm"), w_in_t, conv_w, row("conv_b"))
    ya = _gm_fwd(proj, *gm)
    yb, s_all, g_ffn2 = _ssd_fwd(proj, xc, *ssd, comm=_gather_comm(GATHER_FFN2.pack_local(shards)))
    ffn2 = (row("ffn2_norm"),) + tuple(GATHER_FFN2.pieces(g_ffn2, name) for name in GATHER_FFN2.names)
    h3, n3, a3, b3, s3, h2 = _ffn_fwd(h1, *ffn2, "ffn2_fwd", mixed=(ya, yb, w_out))

    g, gp = {}, {}
    dh3, loss, gp["ple_w_gate"], d_w_proj, g["ple_norm"], g["ple_b_gate"], g["final_norm"] = _tail(
        h3, p, target, row("ple_norm"), GATHER_MIX.pieces(g_mix, "ple_w_gate"), row("ple_b_gate"), w_proj_t,
        row("final_norm"))
    gp["ple_w_proj"] = d_w_proj.T

    dh2, da3, db3, g["ffn2_norm"] = _ffn_dgrad(h2, dh3, a3, b3, *ffn2, "ffn2_dgrad")
    gp["ffn2_w_gate"] = _wgrad(n3, da3, FF_BN, "ffn2_wgrad_gate", transpose_out=True)
    gp["ffn2_w_up"] = _wgrad(n3, db3, FF_BN, "ffn2_wgrad_up", transpose_out=True)
    gp["ffn2_w_down"] = _wgrad(s3, dh3, DOWN_BN, "ffn2_wgrad_down", scale=0.5, bk=DOWN_BK)

    dya, dyb = _out_proj_dgrad(dh2, w_out)
    gp["w_out"] = jnp.concatenate([_wgrad(ya, dh2, SQUARE_BN, "w_out_wgrad_a"), _wgrad(yb, dh2, SQUARE_BN, "w_out_wgrad_b")], axis=0)

    dp_zxd, d_conv_w, g["conv_b"], g["dt_bias"], g["a_log"], g["d_skip"], g["ssm_norm"], parts_late = _ssd_bwd(
        proj, x16, xc, dyb, s_all, conv_w, *ssd, comm=_exchange_comm(SCATTER_LATE.pack_owner_major(gp)))
    gp["conv_w"] = d_conv_w.T
    dp_uv, g["gm_ln_g"], g["gm_ln_b"], g["gm_w_s"], dbst, g["gm_out_norm"] = _gm_bwd(proj, dya, *gm)
    g["gm_b_s"] = jnp.transpose(dbst)

    parts = {}
    gp["w_in"] = jnp.concatenate([_wgrad(n2, dp_uv, SQUARE_BN, "w_in_wgrad_uv", transpose_out=True),
                                  _wgrad(n2, dp_zxd, ZXD_BN, "w_in_wgrad_zxd", transpose_out=True)], axis=0)[:IN_PROJ]
    dh1, g["mix_norm"], parts[SCATTER_IN] = _mix_in_dgrad(h1, dh2, dp_uv, dp_zxd, row("mix_norm"), w_in_t,
                                                          comm=_exchange_comm(SCATTER_IN.pack_owner_major(gp)))

    dx, da1, db1, g["ffn1_norm"] = _ffn_dgrad(x, dh1, a1, b1, *ffn1, "ffn1_dgrad")
    gp["ffn1_w_gate"], small_parts = _wgrad(n1, da1, FF_BN, "ffn1_wgrad_gate", transpose_out=True,
                                            comm=_gather_comm(_pack_small(g)))
    gp["ffn1_w_up"], parts[SCATTER_GATE] = _wgrad(n1, db1, FF_BN, "ffn1_wgrad_up", transpose_out=True,
                                                  comm=_exchange_comm(SCATTER_GATE.pack_owner_major(gp)))
    gp["ffn1_w_down"], parts[SCATTER_UP] = _wgrad(s1, dh1, DOWN_BN, "ffn1_wgrad_down", scale=0.5, bk=DOWN_BK,
                                                  comm=_exchange_comm(SCATTER_UP.pack_owner_major(gp)))
    parts[SCATTER_DOWN] = _comm_alone([_exchange_comm(SCATTER_DOWN.pack_owner_major(gp))], "scatter_ffn1_down")[0]
    parts[SCATTER_LATE] = parts_late

    res_big = {}
    for pack, pack_parts in parts.items():
        for name in pack.names:
            shape, transposed = SHARDS[name]
            if name in ("ple_w_proj", "conv_w"):
                nat = pack.gathered_piece(pack_parts, name).reshape((N_DEV,) + shape[::-1])
                res_big[name] = _sum_adamw(jnp.transpose(nat, (0, 2, 1)), w[name][0], m[name][0], v[name][0], shape[0],
                                           "adamw_" + name)
            elif name == "w_in":
                res_big[name] = _adamw_shard(pack_parts, pack.offsets[name], True, w[name], m[name], v[name],
                                             "adamw_" + name, n_tiles=4)
            else:
                flip = (lambda a: jnp.transpose(a, (0, 2, 1))) if transposed else (lambda a: a)
                res = _adamw_shard(pack_parts, pack.offsets[name], False, flip(w[name]), flip(m[name]), flip(v[name]),
                                   "adamw_" + name, n_tiles=2)
                res_big[name] = [flip(r) for r in res]

    small_shapes = {name: w[name].shape for name in SMALL}
    res_small = _sum_adamw(small_parts, _pack_small(w), _pack_small(m), _pack_small(v), SMALL_ROWS, "adamw_small")
    res_small = [_unpack_small(r, small_shapes) for r in res_small]

    outs = []
    for k in range(4):
        for name in WEIGHTS:
            if name in res_small[k]:
                outs.append(res_small[k][name])
            else:
                outs.append(res_big[name][k].reshape(w[name].shape))
    return loss[0, 0], dx, outs


def kernel(x, p, ffn1_norm, ffn1_w_gate, ffn1_w_up, ffn1_w_down, mix_norm, w_in, gm_ln_g, gm_ln_b, gm_w_s, gm_b_s, gm_out_norm, conv_w, conv_b, dt_bias, a_log, d_skip, ssm_norm, w_out, ffn2_norm, ffn2_w_gate, ffn2_w_up, ffn2_w_down, ple_norm, ple_w_gate, ple_b_gate, ple_w_proj, final_norm, loss_target, m_ffn1_norm, m_ffn1_w_gate, m_ffn1_w_up, m_ffn1_w_down, m_mix_norm, m_w_in, m_gm_ln_g, m_gm_ln_b, m_gm_w_s, m_gm_b_s, m_gm_out_norm, m_conv_w, m_conv_b, m_dt_bias, m_a_log, m_d_skip, m_ssm_norm, m_w_out, m_ffn2_norm, m_ffn2_w_gate, m_ffn2_w_up, m_ffn2_w_down, m_ple_norm, m_ple_w_gate, m_ple_b_gate, m_ple_w_proj, m_final_norm, v_ffn1_norm, v_ffn1_w_gate, v_ffn1_w_up, v_ffn1_w_down, v_mix_norm, v_w_in, v_gm_ln_g, v_gm_ln_b, v_gm_w_s, v_gm_b_s, v_gm_out_norm, v_conv_w, v_conv_b, v_dt_bias, v_a_log, v_d_skip, v_ssm_norm, v_w_out, v_ffn2_norm, v_ffn2_w_gate, v_ffn2_w_up, v_ffn2_w_down, v_ple_norm, v_ple_w_gate, v_ple_b_gate, v_ple_w_proj, v_final_norm):
    args = locals()
    w = {name: args[name] for name in WEIGHTS}
    m = {name: args["m_" + name] for name in WEIGHTS}
    v = {name: args["v_" + name] for name in WEIGHTS}
    loss, dx, outs = _step(x[0], p[0, 0], loss_target[0], w, m, v)
    loss = lax.psum(loss, AXES)
    return (loss, dx[None], *outs)
```

```python
import functools
from typing import NamedTuple

import jax
import jax.numpy as jnp
from jax import lax
from jax.experimental import pallas as pl
from jax.experimental.pallas import tpu as pltpu

F32 = jnp.float32
BF16 = jnp.bfloat16
MESH = pl.DeviceIdType.MESH
AXES = ("x", "y", "c")
N_DEV = 8

D_MODEL = 1024
D_FF = 2816
D_PLE = 256
GM_WIDTH = 1024
GM_HEADS = 8
GM_HEAD_DIM = 128
CHUNK = 128
SSM_WIDTH = 1024
SSM_HEADS = 16
SSM_HEAD_DIM = 64
SSM_GROUPS = 2
SSM_STATE = 128
SSM_CONV = 4
CONV_DIM = SSM_WIDTH + 2 * SSM_GROUPS * SSM_STATE
IN_PROJ = 2 * GM_WIDTH + SSM_WIDTH + CONV_DIM + SSM_HEADS
LANES = 128
BF16_ROWS = 16
F32_ROWS = 8
IN_PROJ_PAD = IN_PROJ - SSM_HEADS + LANES
UV_W = 2 * GM_WIDTH
ZXD_W = IN_PROJ_PAD - UV_W
HALO = 8
EPS = 1e-6

ADAM_LR = 0.001
ADAM_B1 = 0.9
ADAM_B2 = 0.999
ADAM_EPS = 1e-08
ADAM_WD = 0.01
ADAM_STEP = 10

VMEM_LIMIT = 56 * 1024 * 1024
PACK_COLS = 1024


def _rms(x, g):
    return x * lax.rsqrt(jnp.mean(x * x, axis=-1, keepdims=True) + EPS) * g


def _gelu(x):
    return 0.5 * x * (1.0 + lax.erf(x * (2.0 ** -0.5)))


def _silu(x):
    return x * jax.nn.sigmoid(x)


def _dot(a, b):
    return jnp.dot(a.astype(BF16), b.astype(BF16), preferred_element_type=F32)


def _dot_nt(a, b):
    return lax.dot_general(a.astype(BF16), b.astype(BF16), (((1,), (1,)), ((), ())), preferred_element_type=F32)


def _dot_tn(a, b):
    return lax.dot_general(a.astype(BF16), b.astype(BF16), (((0,), (0,)), ((), ())), preferred_element_type=F32)


def _split3(x):
    hi = x.astype(BF16)
    rest = x - hi.astype(F32)
    mid = rest.astype(BF16)
    return hi, mid, (rest - mid.astype(F32)).astype(BF16)


def _exact_dot(x, mask, dims, x_first=True, n_terms=3):
    terms = [lax.dot_general(*((t, mask) if x_first else (mask, t)), (dims, ((), ())), preferred_element_type=F32)
             for t in _split3(x)[:n_terms]]
    total = terms[0]
    for term in terms[1:]:
        total = total + term
    return total


def _mask_product(fwd_dims, fwd_x_first, bwd_dims, bwd_x_first, bwd_terms=3):
    @jax.custom_vjp
    def product(x, mask):
        return _exact_dot(x, mask, fwd_dims, fwd_x_first)

    def fwd(x, mask):
        return product(x, mask), mask

    def bwd(mask, g):
        return _exact_dot(g, mask, bwd_dims, bwd_x_first, bwd_terms), jnp.zeros_like(mask)

    product.defvjp(fwd, bwd)
    return product


_widen = _mask_product(((1,), (0,)), True, ((1,), (1,)), True, bwd_terms=2)
_cumsum_rows = _mask_product(((1,), (0,)), False, ((0,), (0,)), False)
_cumsum_cols = _mask_product(((0,), (0,)), True, ((1,), (1,)), False)


class _Pieces(NamedTuple):
    gathered: jax.Array
    row_off: int
    rows: int


class _Comm(NamedTuple):
    phases: object
    src: jax.Array
    dst: jax.ShapeDtypeStruct


def _tiled(body, name, n_steps, tiled_in, full_in, big_in, tiled_out, acc_out, scratch=(), reverse=False, comm=None):
    n_t, n_f, n_b, n_to, n_a = len(tiled_in), len(full_in), len(big_in), len(tiled_out), len(acc_out)
    n_c = 1 if comm else 0

    def row(i):
        return n_steps - 1 - i if reverse else i

    in_specs, args = [], []
    for arr, br, bc, cb in tiled_in:
        if callable(cb):
            in_specs.append(pl.BlockSpec((br, bc), cb))
        else:
            in_specs.append(pl.BlockSpec((br, bc), functools.partial(lambda i, cb: (row(i), cb), cb=cb)))
        args.append(arr)
    for arr in full_in:
        in_specs.append(pl.BlockSpec(arr.shape, functools.partial(lambda i, nd: (0,) * nd, nd=arr.ndim)))
        args.append(arr)
    big_shapes, n_copies = [], 0
    for big in big_in:
        in_specs.append(pl.BlockSpec(memory_space=pl.ANY))
        if isinstance(big, _Pieces):
            args.append(big.gathered)
            big_shapes.append(((N_DEV * big.rows, PACK_COLS), big.gathered.dtype))
            n_copies += N_DEV
        else:
            args.append(big)
            big_shapes.append((big.shape, big.dtype))
            n_copies += 1
    if comm:
        in_specs.append(pl.BlockSpec(memory_space=pl.ANY))
        args.append(comm.src)
    out_specs, out_shape = [], []
    for rows, cols, dt, br in tiled_out:
        out_specs.append(pl.BlockSpec((br, cols), lambda i: (row(i), 0)))
        out_shape.append(jax.ShapeDtypeStruct((rows, cols), dt))
    for shp, dt in acc_out:
        out_specs.append(pl.BlockSpec(shp, functools.partial(lambda i, nd: (0,) * nd, nd=len(shp))))
        out_shape.append(jax.ShapeDtypeStruct(shp, dt))
    if comm:
        out_specs.append(pl.BlockSpec(memory_space=pl.ANY))
        out_shape.append(comm.dst)
    scratch_shapes = [pltpu.VMEM(shp, dt) for shp, dt in big_shapes] + list(scratch)
    if n_copies:
        scratch_shapes.append(pltpu.SemaphoreType.DMA((n_copies,)))
    if comm:
        scratch_shapes += [pltpu.SemaphoreType.DMA((N_DEV - 1,)), pltpu.SemaphoreType.DMA((N_DEV - 1,)), pltpu.SemaphoreType.DMA]

    def kern(*refs):
        n_in = n_t + n_f + n_b + n_c
        ins = refs[: n_t + n_f]
        big_hbm = refs[n_t + n_f : n_t + n_f + n_b]
        outs = refs[n_in : n_in + n_to + n_a]
        rest = refs[n_in + n_to + n_a + n_c :]
        big_vmem, scr = rest[:n_b], rest[n_b:]
        if comm:
            scr, comm_sems = scr[:-3], scr[-3:]
            comm_start, comm_mid, comm_finish = comm.phases(refs[n_in - 1], refs[n_in + n_to + n_a], *comm_sems)
        if n_copies:
            scr, copy_sems = scr[:-1], scr[-1]
        step = pl.program_id(0)

        @pl.when(step == 0)
        def _():
            copies = []
            for big, src, dst in zip(big_in, big_hbm, big_vmem):
                if isinstance(big, _Pieces):
                    for j in range(N_DEV):
                        copies.append((src.at[j, pl.ds(big.row_off, big.rows), :], dst.at[pl.ds(j * big.rows, big.rows), :]))
                else:
                    copies.append((src, dst))
            copies = [pltpu.make_async_copy(a, b, copy_sems.at[k]) for k, (a, b) in enumerate(copies)]
            for cp in copies:
                cp.start()
            for cp in copies:
                cp.wait()
            for acc in outs[n_to:]:
                acc[...] = jnp.zeros(acc.shape, acc.dtype)
            if comm:
                comm_start()

        body(row(step), *ins, *big_vmem, *outs, *scr)
        if comm:
            pl.when(step == (n_steps - 1) // 2)(comm_mid)
            pl.when(step == n_steps - 1)(comm_finish)

    res = pl.pallas_call(
        kern,
        out_shape=out_shape,
        grid=(n_steps,),
        in_specs=in_specs,
        out_specs=out_specs,
        scratch_shapes=scratch_shapes,
        name=name,
        compiler_params=pltpu.CompilerParams(dimension_semantics=("arbitrary",), vmem_limit_bytes=VMEM_LIMIT),
    )(*args)
    return res


FWD_CHUNKS = ((0, 1536), (1536, D_FF))
DGRAD_CHUNKS = ((0, 1024), (1024, 2048), (2048, D_FF))
FFN_TM = 256


def _ffn_fwd(h, g, wg_t, wu_t, wd, name, comm=None, mixed=None):
    T = h.shape[0]

    def ffn(x, g_ref, wg_ref, wu_ref, wd_ref, o_ref, n_ref, a_ref, b_ref, s_ref):
        n = _rms(x, g_ref[...]).astype(BF16)
        n_ref[...] = n
        f = jnp.zeros(x.shape, F32)
        for lo, hi in FWD_CHUNKS:
            a = _dot_nt(n, wg_ref[lo:hi, :])
            b = _dot_nt(n, wu_ref[lo:hi, :])
            s = (_silu(a) * b).astype(BF16)
            a_ref[:, lo:hi] = a.astype(BF16)
            b_ref[:, lo:hi] = b.astype(BF16)
            s_ref[:, lo:hi] = s
            f = f + jnp.dot(s, wd_ref[lo:hi, :], preferred_element_type=F32)
        o_ref[...] = x + 0.5 * f

    def body_plain(i, h_ref, *refs):
        ffn(h_ref[...], *refs)

    def body_mixed(i, h_ref, ya_ref, yb_ref, g_ref, wg_ref, wu_ref, wd_ref, wo_ref, o_ref, n_ref, a_ref, b_ref, s_ref, x_ref):
        x = (h_ref[...] + jnp.dot(ya_ref[...], wo_ref[:GM_WIDTH, :], preferred_element_type=F32)
             + jnp.dot(yb_ref[...], wo_ref[GM_WIDTH:, :], preferred_element_type=F32))
        x_ref[...] = x
        ffn(x, g_ref, wg_ref, wu_ref, wd_ref, o_ref, n_ref, a_ref, b_ref, s_ref)

    body = body_mixed if mixed else body_plain
    tiled_in, big_in = [(h, FFN_TM, D_MODEL, 0)], [wg_t, wu_t, wd]
    tiled_out = [(T, D_MODEL, F32, FFN_TM), (T, D_MODEL, BF16, FFN_TM), (T, D_FF, BF16, FFN_TM), (T, D_FF, BF16, FFN_TM),
                 (T, D_FF, BF16, FFN_TM)]
    if mixed:
        tiled_in += [(mixed[0], FFN_TM, GM_WIDTH, 0), (mixed[1], FFN_TM, SSM_WIDTH, 0)]
        big_in.append(mixed[2])
        tiled_out.append((T, D_MODEL, F32, FFN_TM))
    return _tiled(body, name, T // FFN_TM, tiled_in, [g], big_in, tiled_out, [], comm=comm)


def _ffn_dgrad(h, dout, a16, b16, g, wg_t, wu_t, wd, name):
    T = h.shape[0]

    def body(i, h_ref, do_ref, a_ref, b_ref, g_ref, wg_ref, wu_ref, wd_ref, dh_ref, da_ref, db_ref, dg_ref):
        dout = do_ref[...]
        _, rms_vjp = jax.vjp(_rms, h_ref[...], g_ref[...])
        dfo = (0.5 * dout).astype(BF16)
        dn = jnp.zeros(dout.shape, F32)
        for lo, hi in DGRAD_CHUNKS:
            a = a_ref[:, lo:hi].astype(F32)
            b = b_ref[:, lo:hi].astype(F32)
            sg = jax.nn.sigmoid(a)
            ds = _dot_nt(dfo, wd_ref[lo:hi, :])
            db = (ds * (a * sg)).astype(BF16)
            da = (ds * b * (sg * (1.0 + a * (1.0 - sg)))).astype(BF16)
            dn = dn + _dot(da, wg_ref[lo:hi, :]) + _dot(db, wu_ref[lo:hi, :])
            da_ref[:, lo:hi] = da
            db_ref[:, lo:hi] = db
        dx, dg = rms_vjp(dn)
        dh_ref[...] = dout + dx
        dg_ref[...] += dg

    return _tiled(body, name, T // FFN_TM,
                  [(h, FFN_TM, D_MODEL, 0), (dout, FFN_TM, D_MODEL, 0), (a16, FFN_TM, D_FF, 0), (b16, FFN_TM, D_FF, 0)],
                  [g], [wg_t, wu_t, wd],
                  [(T, D_MODEL, F32, FFN_TM), (T, D_FF, BF16, FFN_TM), (T, D_FF, BF16, FFN_TM)], [((1, D_MODEL), F32)])


FF_BN = D_FF // 2
DOWN_BN, DOWN_BK = 512, 1024
SQUARE_BN = 1024
ZXD_BN = ZXD_W // 3


def _wgrad(a, b, bn, name, scale=None, transpose_out=False, bk=2048, comm=None):
    T, M = a.shape
    N = b.shape[1]
    bk = min(bk, T)
    assert M % LANES == 0 and N % bn == 0 and T % bk == 0
    n_j, n_k = N // bn, T // bk
    n_c = 1 if comm else 0

    def kern(*refs):
        a_ref, b_ref, o_ref, acc_ref = refs[0], refs[1], refs[2 + n_c], refs[3 + 2 * n_c]
        j, k = pl.program_id(0), pl.program_id(1)
        if comm:
            comm_start, comm_mid, comm_finish = comm.phases(refs[2], refs[4], *refs[6:])
            pl.when((j == 0) & (k == 0))(comm_start)

        @pl.when(k == 0)
        def _():
            acc_ref[...] = jnp.zeros(acc_ref.shape, F32)

        bv = b_ref[...]
        if scale is not None:
            bv = bv * scale
        acc_ref[...] += _dot_tn(a_ref[...], bv)

        @pl.when(k == n_k - 1)
        def _():
            acc = acc_ref[...]
            o_ref[...] = (acc.T if transpose_out else acc).astype(BF16)

        if comm:
            pl.when((j == (n_j - 1) // 2) & (k == n_k - 1))(comm_mid)
            pl.when((j == n_j - 1) & (k == n_k - 1))(comm_finish)

    if transpose_out:
        out_shape, out_spec = (N, M), pl.BlockSpec((bn, M), lambda j, k: (j, 0))
    else:
        out_shape, out_spec = (M, N), pl.BlockSpec((M, bn), lambda j, k: (0, j))
    any_spec = pl.BlockSpec(memory_space=pl.ANY)
    comm_sems = [pltpu.SemaphoreType.DMA((N_DEV - 1,)), pltpu.SemaphoreType.DMA((N_DEV - 1,)), pltpu.SemaphoreType.DMA]
    res = pl.pallas_call(
        kern,
        out_shape=[jax.ShapeDtypeStruct(out_shape, BF16)] + ([comm.dst] if comm else []),
        grid=(n_j, n_k),
        in_specs=[pl.BlockSpec((bk, M), lambda j, k: (k, 0)), pl.BlockSpec((bk, bn), lambda j, k: (k, j))] + [any_spec] * n_c,
        out_specs=[out_spec] + [any_spec] * n_c,
        scratch_shapes=[pltpu.VMEM((M, bn), F32)] + (comm_sems if comm else []),
        name=name,
        compiler_params=pltpu.CompilerParams(dimension_semantics=("arbitrary", "arbitrary"), vmem_limit_bytes=VMEM_LIMIT),
    )(a, b, *([comm.src] if comm else []))
    return res if comm else res[0]


PROJ_TM = 512
PROJ_DGRAD_TM = 256
UVZ_W = 2 * GM_WIDTH + SSM_WIDTH
PROJ_KEPT = UVZ_W + LANES
Z_BLK = 2 * GM_WIDTH // SSM_WIDTH
DT_BLK = UVZ_W // LANES


def _mix_in_fwd(h, g, w_in_t, conv_w, conv_b):
    T = h.shape[0]

    def body(i, h_ref, g_ref, cw_ref, cb_ref, w_ref, p_ref, n_ref, x_ref, xc_ref, ext_ref):
        @pl.when(i == 0)
        def _():
            ext_ref[0:HALO, :] = jnp.zeros((HALO, CONV_DIM), F32)

        n = _rms(h_ref[...], g_ref[...]).astype(BF16)
        n_ref[...] = n
        proj = _dot_nt(n, w_ref[...])
        p_ref[:, :UVZ_W] = proj[:, :UVZ_W]
        p_ref[:, UVZ_W:] = proj[:, UVZ_W + CONV_DIM:]
        xbc = proj[:, UVZ_W:UVZ_W + CONV_DIM]
        x_ref[...] = xbc.astype(BF16)
        ext_ref[HALO:, :] = xbc
        xc_ref[...] = _conv_taps(ext_ref, cw_ref[...], cb_ref[...], PROJ_TM)
        ext_ref[0:HALO, :] = ext_ref[PROJ_TM:PROJ_TM + HALO, :]

    return _tiled(body, "mix_in_fwd", T // PROJ_TM, [(h, PROJ_TM, D_MODEL, 0)], [g, conv_w, conv_b], [w_in_t],
                  [(T, PROJ_KEPT, F32, PROJ_TM), (T, D_MODEL, BF16, PROJ_TM), (T, CONV_DIM, BF16, PROJ_TM),
                   (T, CONV_DIM, F32, PROJ_TM)], [],
                  scratch=[pltpu.VMEM((HALO + PROJ_TM, CONV_DIM), F32)])


def _mix_in_dgrad(h, dh_in, dp_uv, dp_zxd, g, w_in_t, comm=None):
    T = h.shape[0]

    def body(i, h_ref, dh_ref, duv_ref, dzxd_ref, g_ref, w_ref, o_ref, dg_ref):
        dn = _dot(duv_ref[...], w_ref[:UV_W, :]) + _dot(dzxd_ref[...], w_ref[UV_W:, :])
        _, rms_vjp = jax.vjp(_rms, h_ref[...], g_ref[...])
        dx, dg = rms_vjp(dn)
        o_ref[...] = dh_ref[...] + dx
        dg_ref[...] += dg

    return _tiled(body, "mix_in_dgrad", T // PROJ_DGRAD_TM,
                  [(h, PROJ_DGRAD_TM, D_MODEL, 0), (dh_in, PROJ_DGRAD_TM, D_MODEL, 0), (dp_uv, PROJ_DGRAD_TM, UV_W, 0),
                   (dp_zxd, PROJ_DGRAD_TM, ZXD_W, 0)], [g], [w_in_t],
                  [(T, D_MODEL, F32, PROJ_DGRAD_TM)], [((1, D_MODEL), F32)], comm=comm)


def _out_proj_dgrad(dh, w_out):
    T = dh.shape[0]

    def body(i, dh_ref, w_ref, dya_ref, dyb_ref):
        d = dh_ref[...].astype(BF16)
        dya_ref[...] = _dot_nt(d, w_ref[:GM_WIDTH, :])
        dyb_ref[...] = _dot_nt(d, w_ref[GM_WIDTH:, :])

    return _tiled(body, "out_proj_dgrad", T // PROJ_TM, [(dh, PROJ_TM, D_MODEL, 0)], [], [w_out],
                  [(T, GM_WIDTH, F32, PROJ_TM), (T, SSM_WIDTH, F32, PROJ_TM)], [])


def _gm_chunk(u, v, ln_g, ln_b, b_st, out_g, *w_heads):
    ug = _gelu(u)
    vg = _gelu(v)
    mu = jnp.mean(vg, axis=-1, keepdims=True)
    xc = vg - mu
    vn = xc * lax.rsqrt(jnp.mean(xc * xc, axis=-1, keepdims=True) + EPS) * ln_g + ln_b
    t_idx = lax.broadcasted_iota(jnp.int32, (CHUNK, CHUNK), 0)
    s_idx = lax.broadcasted_iota(jnp.int32, (CHUNK, CHUNK), 1)
    causal = t_idx >= s_idx
    mixed = []
    for hd in range(GM_HEADS):
        wm = jnp.where(causal, w_heads[hd], 0.0)
        cols = slice(hd * GM_HEAD_DIM, (hd + 1) * GM_HEAD_DIM)
        mixed.append(_dot(wm, vn[:, cols]) + b_st[:, hd:hd + 1])
    ya0 = ug * jnp.concatenate(mixed, axis=1)
    return _rms(ya0, out_g)


GM_FWD_CHUNKS = 2


def _gm_fwd(proj, ln_g, ln_b, w_s, b_st, out_g):
    T = proj.shape[0]

    rows = GM_FWD_CHUNKS * CHUNK

    def body(i, u_ref, v_ref, lg_ref, lb_ref, w_ref, bs_ref, og_ref, ya_ref):
        w_heads = [w_ref[hd] for hd in range(GM_HEADS)]
        for c in range(GM_FWD_CHUNKS):
            tok = pl.ds(c * CHUNK, CHUNK)
            ya = _gm_chunk(u_ref[tok, :], v_ref[tok, :], lg_ref[...], lb_ref[...], bs_ref[...], og_ref[...], *w_heads)
            ya_ref[tok, :] = ya.astype(BF16)

    return _tiled(body, "gmlp_fwd", T // rows, [(proj, rows, GM_WIDTH, 0), (proj, rows, GM_WIDTH, 1)],
                  [ln_g, ln_b, w_s, b_st, out_g], [], [(T, GM_WIDTH, BF16, rows)], [])[0]


def _gm_bwd(proj, dya, ln_g, ln_b, w_s, b_st, out_g):
    T = proj.shape[0]

    def body(i, u_ref, v_ref, dy_ref, lg_ref, lb_ref, w_ref, bs_ref, og_ref, duv_ref, dlg_ref, dlb_ref, dw_ref, dbs_ref,
             dog_ref):
        w_heads = [w_ref[hd] for hd in range(GM_HEADS)]
        _, vjp = jax.vjp(_gm_chunk, u_ref[...], v_ref[...], lg_ref[...], lb_ref[...], bs_ref[...], og_ref[...], *w_heads)
        grads = vjp(dy_ref[...])
        duv_ref[:, :GM_WIDTH] = grads[0].astype(BF16)
        duv_ref[:, GM_WIDTH:] = grads[1].astype(BF16)
        dlg_ref[...] += grads[2]
        dlb_ref[...] += grads[3]
        dbs_ref[...] += grads[4]
        dog_ref[...] += grads[5]
        for hd in range(GM_HEADS):
            dw_ref[hd] += grads[6 + hd]

    return _tiled(body, "gmlp_bwd", T // CHUNK,
                  [(proj, CHUNK, GM_WIDTH, 0), (proj, CHUNK, GM_WIDTH, 1), (dya, CHUNK, GM_WIDTH, 0)],
                  [ln_g, ln_b, w_s, b_st, out_g], [], [(T, UV_W, BF16, CHUNK)],
                  [((1, GM_WIDTH), F32), ((1, GM_WIDTH), F32), ((GM_HEADS, CHUNK, CHUNK), F32),
                   ((CHUNK, GM_HEADS), F32), ((1, GM_WIDTH), F32)])


def _ssd_chunk(xc, z, dtr, s_in, dt_bias, a_log, d_skip, norm_g):
    half = SSM_WIDTH // SSM_GROUPS
    l_idx = lax.broadcasted_iota(jnp.int32, (CHUNK, CHUNK), 0)
    s_idx = lax.broadcasted_iota(jnp.int32, (CHUNK, CHUNK), 1)
    causal = l_idx >= s_idx
    head_of_col = lax.broadcasted_iota(jnp.int32, (SSM_HEADS, SSM_WIDTH), 1) // SSM_HEAD_DIM
    expand = (head_of_col == lax.broadcasted_iota(jnp.int32, (SSM_HEADS, SSM_WIDTH), 0)).astype(BF16)

    xcs = _silu(xc)
    xs = xcs[:, :SSM_WIDTH]
    dt = jax.nn.softplus(dtr + dt_bias)
    adt = dt * (-jnp.exp(a_log))
    acs = _cumsum_rows(adt, causal.astype(BF16))
    acs_t = _cumsum_cols(adt, (l_idx <= s_idx).astype(BF16))
    tot = acs[CHUNK - 1:CHUNK, :]
    dt_w = _widen(dt, expand)
    out_decay_w = _widen(jnp.exp(acs), expand)
    state_decay_w = _widen(jnp.exp(tot - acs), expand)
    chunk_decay_w = _widen(jnp.exp(tot), expand)
    d_skip_w = _widen(d_skip, expand)
    xdt = xs * dt_w
    xdt_decayed = xdt * state_decay_w

    y_diag, y_off, states = [], [], []
    for grp in range(SSM_GROUPS):
        b0 = SSM_WIDTH + grp * SSM_STATE
        c0 = SSM_WIDTH + SSM_GROUPS * SSM_STATE + grp * SSM_STATE
        bm = xcs[:, b0:b0 + SSM_STATE].astype(BF16)
        cm = xcs[:, c0:c0 + SSM_STATE].astype(BF16)
        cb = _dot_nt(cm, bm)
        for k in range(grp * SSM_HEADS // SSM_GROUPS, (grp + 1) * SSM_HEADS // SSM_GROUPS):
            decay = jnp.exp(jnp.where(causal, acs[:, k:k + 1] - acs_t[k:k + 1, :], -jnp.inf))
            y_diag.append(_dot(cb * decay, xdt[:, k * SSM_HEAD_DIM:(k + 1) * SSM_HEAD_DIM]))
        cols = slice(grp * half, (grp + 1) * half)
        states.append(_dot_tn(bm, xdt_decayed[:, cols]))
        y_off.append(_dot(cm, s_in[:, cols]))
    y = jnp.concatenate(y_diag, axis=1) + jnp.concatenate(y_off, axis=1) * out_decay_w + xs * d_skip_w
    s_out = s_in * chunk_decay_w + jnp.concatenate(states, axis=1)
    y = y * _silu(z)
    normed = []
    for grp in range(SSM_GROUPS):
        yg = y[:, grp * half:(grp + 1) * half]
        normed.append(yg * lax.rsqrt(jnp.mean(yg * yg, axis=-1, keepdims=True) + EPS))
    return jnp.concatenate(normed, axis=1) * norm_g, s_out


def _sum_row_tiles(x):
    return x.reshape(x.shape[0] // F32_ROWS, F32_ROWS, x.shape[1]).sum(axis=0)


def _conv_taps(ext_ref, w, b, rows):
    y = b
    for k in range(SSM_CONV):
        y = y + w[k:k + 1, :] * ext_ref[pl.ds(HALO - (SSM_CONV - 1) + k, rows), :]
    return y


def _ssd_fwd(proj, xc, dt_bias, a_log, d_skip, norm_g, comm=None):
    T = proj.shape[0]
    n_chunks = T // CHUNK

    def body(i, z_ref, xc_ref, dt_ref, dtb_ref, al_ref, dsk_ref, ng_ref, yb_ref, sin_ref, st_ref):
        @pl.when(i == 0)
        def _():
            st_ref[...] = jnp.zeros(st_ref.shape, F32)

        s_in = st_ref[...]
        yb, s_out = _ssd_chunk(xc_ref[...], z_ref[...], dt_ref[:, 0:SSM_HEADS], s_in, dtb_ref[...], al_ref[...],
                               dsk_ref[...], ng_ref[...])
        yb_ref[...] = yb.astype(BF16)
        sin_ref[...] = s_in
        st_ref[...] = s_out

    return _tiled(body, "ssd_fwd", n_chunks,
                  [(proj, CHUNK, SSM_WIDTH, Z_BLK), (xc, CHUNK, CONV_DIM, 0), (proj, CHUNK, LANES, DT_BLK)],
                  [dt_bias, a_log, d_skip, norm_g], [],
                  [(T, SSM_WIDTH, BF16, CHUNK), (n_chunks * SSM_STATE, SSM_WIDTH, F32, SSM_STATE)], [],
                  scratch=[pltpu.VMEM((SSM_STATE, SSM_WIDTH), F32)], comm=comm)


def _ssd_bwd(proj, x16, xc, dyb, s_all, conv_w, dt_bias, a_log, d_skip, norm_g, comm=None):
    T = proj.shape[0]
    n_chunks = T // CHUNK

    def body(i, z_ref, x_ref, xc_ref, dt_ref, dy_ref, sin_ref, cw_ref, dtb_ref, al_ref, dsk_ref, ng_ref,
             dzxd_ref, dcw_ref, dcb_ref, ddtb_ref, dal_ref, ddsk_ref, dng_ref, dext_ref, dst_ref, cw_acc, cb_acc):
        @pl.when(i == n_chunks - 1)
        def _():
            dext_ref[CHUNK:, :] = jnp.zeros((HALO, CONV_DIM), F32)
            dst_ref[...] = jnp.zeros(dst_ref.shape, F32)
            cw_acc[...] = jnp.zeros(cw_acc.shape, F32)
            cb_acc[...] = jnp.zeros(cb_acc.shape, F32)

        _, vjp = jax.vjp(_ssd_chunk, xc_ref[...], z_ref[...], dt_ref[:, 0:SSM_HEADS], sin_ref[...], dtb_ref[...], al_ref[...],
                         dsk_ref[...], ng_ref[...])
        dxc, dz, ddtr, ds_in, ddtb, dal, ddsk, dng = vjp((dy_ref[...], dst_ref[...]))
        dst_ref[...] = ds_in
        ddtb_ref[...] += ddtb
        dal_ref[...] += dal
        ddsk_ref[...] += ddsk
        dng_ref[...] += dng
        dext_ref[0:CHUNK, :] = dxc
        cw = cw_ref[...]
        x = x_ref[...].astype(F32)
        dx = jnp.zeros((CHUNK, CONV_DIM), F32)
        for k in range(SSM_CONV):
            shifted = dext_ref[pl.ds(SSM_CONV - 1 - k, CHUNK), :]
            dx = dx + cw[k:k + 1, :] * shifted
            cw_acc[k] += _sum_row_tiles(shifted * x)
        cb_acc[...] += _sum_row_tiles(dxc)

        @pl.when(i == 0)
        def _():
            dcw_ref[...] = jnp.sum(cw_acc[...], axis=1)
            dcb_ref[...] = jnp.sum(cb_acc[...], axis=0, keepdims=True)

        dext_ref[CHUNK:, :] = dext_ref[0:HALO, :]
        dzxd_ref[:, 0:SSM_WIDTH] = dz.astype(BF16)
        dzxd_ref[:, SSM_WIDTH:SSM_WIDTH + CONV_DIM] = dx.astype(BF16)
        dzxd_ref[:, SSM_WIDTH + CONV_DIM:] = jnp.concatenate(
            [ddtr, jnp.zeros((CHUNK, LANES - SSM_HEADS), F32)], axis=1).astype(BF16)

    return _tiled(body, "ssd_bwd", n_chunks,
                  [(proj, CHUNK, SSM_WIDTH, Z_BLK), (x16, CHUNK, CONV_DIM, 0), (xc, CHUNK, CONV_DIM, 0),
                   (proj, CHUNK, LANES, DT_BLK), (dyb, CHUNK, SSM_WIDTH, 0), (s_all, SSM_STATE, SSM_WIDTH, 0)],
                  [conv_w, dt_bias, a_log, d_skip, norm_g], [],
                  [(T, ZXD_W, BF16, CHUNK)],
                  [((SSM_CONV, CONV_DIM), F32), ((1, CONV_DIM), F32), ((1, SSM_HEADS), F32), ((1, SSM_HEADS), F32),
                   ((1, SSM_HEADS), F32), ((1, SSM_WIDTH), F32)],
                  scratch=[pltpu.VMEM((CHUNK + HALO, CONV_DIM), F32), pltpu.VMEM((SSM_STATE, SSM_WIDTH), F32),
                           pltpu.VMEM((SSM_CONV, F32_ROWS, CONV_DIM), F32), pltpu.VMEM((F32_ROWS, CONV_DIM), F32)],
                  reverse=True, comm=comm)


TAIL_TM = 512


def _tail(h, p, target, ple_norm, w_gate, b_gate, w_proj_t, final_norm):
    T = h.shape[0]

    def head(x, pre, pp, b_g, f_norm, tgt):
        gate = jax.nn.sigmoid(pre + b_g)
        out = _rms(x + gate * pp, f_norm)
        err = out - tgt
        return 0.5 * jnp.sum(jnp.mean(err * err, axis=-1, keepdims=True), axis=0, keepdims=True)

    def body(i, h_ref, p_ref, t_ref, pn_ref, bg_ref, fn_ref, wg_ref, wp_ref, dh_ref, loss_ref, dwg_ref, dwp_ref, dpn_ref,
             dbg_ref, dfn_ref):
        x = h_ref[...]
        n4f, n_vjp = jax.vjp(_rms, x, pn_ref[...])
        n4 = n4f.astype(BF16)
        pre = jnp.dot(n4, wg_ref[...], preferred_element_type=F32)
        p16 = p_ref[...].astype(BF16)
        pp = _dot_nt(p16, wp_ref[...])
        loss, h_vjp = jax.vjp(functools.partial(head, tgt=t_ref[...]), x, pre, pp, bg_ref[...], fn_ref[...])
        dx, dpre, dpp, dbg, dfn = h_vjp(jnp.ones((1, 1), F32))
        dpre16 = dpre.astype(BF16)
        dn4 = _dot_nt(dpre16, wg_ref[...])
        dx2, dpn = n_vjp(dn4)
        dh_ref[...] = dx + dx2
        loss_ref[...] += loss
        dwg_ref[...] += _dot_tn(n4, dpre16)
        dwp_ref[...] += _dot_tn(p16, dpp)
        dpn_ref[...] += dpn
        dbg_ref[...] += dbg
        dfn_ref[...] += dfn

    return _tiled(body, "tail", T // TAIL_TM,
                  [(h, TAIL_TM, D_MODEL, 0), (p, TAIL_TM, D_PLE, 0), (target, TAIL_TM, D_MODEL, 0)],
                  [ple_norm, b_gate, final_norm], [w_gate, w_proj_t],
                  [(T, D_MODEL, F32, TAIL_TM)],
                  [((1, 1), F32), ((D_MODEL, D_MODEL), F32), ((D_PLE, D_MODEL), F32), ((1, D_MODEL), F32),
                   ((1, D_MODEL), F32), ((1, D_MODEL), F32)])


def _gather_phases(x_ref, out_ref, send_sems, recv_sems, local_sem):
    mx, my, mc = lax.axis_index("x"), lax.axis_index("y"), lax.axis_index("c")
    me, sibling = (mx, my, mc), (mx, my, 1 - mc)
    chips = [(1 - mx, my), (mx, 1 - my), (1 - mx, 1 - my)]

    def rows(px, py, pc):
        return out_ref.at[4 * px + 2 * py + pc]

    def copy(k, block, to, src=None):
        return pltpu.make_async_remote_copy(
            src_ref=rows(*block) if src is None else src, dst_ref=rows(*block),
            send_sem=send_sems.at[k], recv_sem=recv_sems.at[k], device_id=to, device_id_type=MESH)

    mine = pltpu.make_async_copy(x_ref, rows(*me), local_sem)
    first = [copy(0, me, sibling, src=x_ref)] + [copy(1 + j, me, (*chip, mc), src=x_ref) for j, chip in enumerate(chips)]
    passed = [copy(4 + j, (*chip, mc), sibling) for j, chip in enumerate(chips)]

    def start():
        mine.start()
        for cp in first:
            cp.start()

    def mid():
        for j, chip in enumerate(chips):
            copy(1 + j, (*chip, mc), me).wait_recv()
            passed[j].start()

    def finish():
        copy(0, sibling, me).wait_recv()
        for j, chip in enumerate(chips):
            copy(4 + j, (*chip, 1 - mc), me).wait_recv()
        for cp in first + passed:
            cp.wait_send()
        mine.wait()

    return start, mid, finish


def _exchange_phases(x_ref, out_ref, send_sems, recv_sems, local_sem):
    mx, my, mc = lax.axis_index("x"), lax.axis_index("y"), lax.axis_index("c")
    me = 4 * mx + 2 * my + mc
    mine = pltpu.make_async_copy(x_ref.at[me], out_ref.at[me], local_sem)
    copies = []
    for k in range(1, N_DEV):
        px = 1 - mx if k & 4 else mx
        py = 1 - my if k & 2 else my
        pc = 1 - mc if k & 1 else mc
        copies.append(pltpu.make_async_remote_copy(
            src_ref=x_ref.at[4 * px + 2 * py + pc], dst_ref=out_ref.at[me], send_sem=send_sems.at[k - 1],
            recv_sem=recv_sems.at[k - 1], device_id=(px, py, pc), device_id_type=MESH))

    def start():
        mine.start()
        for cp in copies:
            cp.start()

    def finish():
        for cp in copies:
            cp.wait_recv()
        for cp in copies:
            cp.wait_send()
        mine.wait()

    return start, lambda: None, finish


def _gather_comm(x):
    return _Comm(_gather_phases, x, jax.ShapeDtypeStruct((N_DEV,) + x.shape, x.dtype))


def _exchange_comm(x):
    return _Comm(_exchange_phases, x, jax.ShapeDtypeStruct(x.shape, x.dtype))


def _comm_alone(comms, name):
    n = len(comms)

    def body(*refs):
        phases = [comm.phases(refs[k], refs[n + k], *refs[2 * n + 3 * k:2 * n + 3 * k + 3]) for k, comm in enumerate(comms)]
        for step in range(3):
            for phase in phases:
                phase[step]()

    any_spec = pl.BlockSpec(memory_space=pl.ANY)
    return pl.pallas_call(
        body,
        out_shape=[comm.dst for comm in comms],
        in_specs=[any_spec] * n,
        out_specs=[any_spec] * n,
        scratch_shapes=[pltpu.SemaphoreType.DMA((N_DEV - 1,)), pltpu.SemaphoreType.DMA((N_DEV - 1,)), pltpu.SemaphoreType.DMA] * n,
        name=name,
    )(*[comm.src for comm in comms])


def _sum_parts(p_ref):
    g = p_ref[0].astype(F32)
    for j in range(1, N_DEV):
        g = g + p_ref[j].astype(F32)
    return g


def _adamw_store(g, w_ref, m_ref, v_ref, g_ref, d_ref, nm_ref, nv_ref):
    m_new = ADAM_B1 * m_ref[...] + (1.0 - ADAM_B1) * g
    v_new = ADAM_B2 * v_ref[...] + (1.0 - ADAM_B2) * jnp.square(g)
    m_hat = m_new / (1.0 - ADAM_B1 ** ADAM_STEP)
    v_hat = v_new / (1.0 - ADAM_B2 ** ADAM_STEP)
    g_ref[...] = g
    d_ref[...] = -ADAM_LR * (m_hat / (jnp.sqrt(v_hat) + ADAM_EPS) + ADAM_WD * w_ref[...])
    nm_ref[...] = m_new
    nv_ref[...] = v_new


def _adamw_shard(parts, off, w, m, v, name, n_tiles):
    _, rows, c = w.shape
    assert c == PACK_COLS
    by_rows = rows % BF16_ROWS == 0
    if by_rows:
        tr = rows // n_tiles
        window = (N_DEV, tr, PACK_COLS)
        spec = pl.BlockSpec((None, tr, PACK_COLS), lambda i: (0, i, 0))
    else:
        padded, tc = -(-rows // BF16_ROWS) * BF16_ROWS, PACK_COLS // n_tiles
        window = (N_DEV, padded, tc)
        spec = pl.BlockSpec((None, rows, tc), lambda i: (0, 0, i))

    def kern(p_hbm, w_ref, m_ref, v_ref, g_ref, d_ref, nm_ref, nv_ref, buf, sem):
        i = pl.program_id(0)
        if by_rows:
            src = p_hbm.at[:, pl.ds(pl.multiple_of(off + i * tr, BF16_ROWS), tr), :]
        else:
            src = p_hbm.at[:, pl.ds(off, padded), pl.ds(pl.multiple_of(i * tc, LANES), tc)]
        cp = pltpu.make_async_copy(src, buf, sem)
        cp.start()
        cp.wait()
        g = _sum_parts(buf)
        if not by_rows:
            keep = lax.broadcasted_iota(jnp.int32, (rows, padded), 0) == lax.broadcasted_iota(jnp.int32, (rows, padded), 1)
            g = _exact_dot(g, keep.astype(BF16), ((1,), (0,)), x_first=False)
        _adamw_store(g, w_ref, m_ref, v_ref, g_ref, d_ref, nm_ref, nv_ref)

    return pl.pallas_call(
        kern,
        out_shape=[jax.ShapeDtypeStruct(w.shape, F32)] * 4,
        grid=(n_tiles,),
        in_specs=[pl.BlockSpec(memory_space=pl.ANY), spec, spec, spec],
        out_specs=[spec] * 4,
        scratch_shapes=[pltpu.VMEM(window, parts.dtype), pltpu.SemaphoreType.DMA],
        name=name,
        compiler_params=pltpu.CompilerParams(dimension_semantics=("arbitrary",), vmem_limit_bytes=VMEM_LIMIT),
    )(parts, w, m, v)


def _sum_adamw(parts, w, m, v, tr, name):
    _, R, C = parts.shape

    def kern(p_ref, w_ref, m_ref, v_ref, g_ref, d_ref, nm_ref, nv_ref):
        _adamw_store(_sum_parts(p_ref), w_ref, m_ref, v_ref, g_ref, d_ref, nm_ref, nv_ref)

    row_spec = pl.BlockSpec((tr, C), lambda i: (i, 0))
    return pl.pallas_call(
        kern,
        out_shape=[jax.ShapeDtypeStruct((R, C), F32)] * 4,
        grid=(R // tr,),
        in_specs=[pl.BlockSpec((N_DEV, tr, C), lambda i: (0, i, 0)), row_spec, row_spec, row_spec],
        out_specs=[row_spec] * 4,
        name=name,
        compiler_params=pltpu.CompilerParams(dimension_semantics=("arbitrary",), vmem_limit_bytes=VMEM_LIMIT),
    )(parts, w, m, v)


FF_SHARD = D_FF // N_DEV
CONV_SHARD = (SSM_CONV, CONV_DIM // N_DEV)
SHARDS = {"ffn1_w_gate": ((D_MODEL, FF_SHARD), True), "ffn1_w_up": ((D_MODEL, FF_SHARD), True),
          "ffn1_w_down": ((FF_SHARD, D_MODEL), False),
          "ffn2_w_gate": ((D_MODEL, FF_SHARD), True), "ffn2_w_up": ((D_MODEL, FF_SHARD), True),
          "ffn2_w_down": ((FF_SHARD, D_MODEL), False),
          "w_out": ((2 * D_MODEL // N_DEV, D_MODEL), False), "ple_w_gate": ((D_MODEL // N_DEV, D_MODEL), False),
          "w_in": ((D_MODEL, IN_PROJ // N_DEV), True), "ple_w_proj": ((D_PLE, D_MODEL // N_DEV), True),
          "conv_w": (CONV_SHARD, True),
          "conv_w_mid": (CONV_SHARD, True), "conv_w_low": (CONV_SHARD, True)}
BIG = tuple(name for name in SHARDS if not name.startswith("conv_w_"))
SMALL = ("ffn1_norm", "mix_norm", "gm_ln_g", "gm_ln_b", "gm_w_s", "gm_b_s", "gm_out_norm", "conv_b", "dt_bias", "a_log",
         "d_skip", "ssm_norm", "ffn2_norm", "ple_norm", "ple_b_gate", "final_norm")
SMALL_ROWS = 144


def _piece_rows(name):
    shape = SHARDS[name][0]
    return -(-(shape[0] * shape[1]) // PACK_COLS)


def _pad_cols(flat, name):
    pad = _piece_rows(name) * PACK_COLS - flat.shape[-1]
    return flat if pad == 0 else jnp.pad(flat, [(0, 0)] * (flat.ndim - 1) + [(0, pad)])


class _Pack:
    def __init__(self, names, tile_rows):
        self.names, self.tile_rows, self.offsets, off = names, tile_rows, {}, 0
        for name in names:
            self.offsets[name] = off
            off += _piece_rows(name)
        self.rows = -(-off // tile_rows) * tile_rows

    def pack_local(self, vals):
        parts = []
        for name in self.names:
            val = vals[name]
            parts.append(_pad_cols((val.T if SHARDS[name][1] else val).reshape(-1), name))
        flat = jnp.concatenate(parts)
        return jnp.pad(flat, (0, self.rows * PACK_COLS - flat.shape[0])).reshape(self.rows, PACK_COLS)

    def pack_owner_major(self, grads):
        parts, rows = [], 0
        for name in self.names:
            grad, piece_rows = grads[name].astype(BF16), _piece_rows(name)
            if grad.shape != (N_DEV * piece_rows, PACK_COLS):
                grad = _pad_cols(grad.reshape(N_DEV, -1), name)
            parts.append(grad.reshape(N_DEV, piece_rows, PACK_COLS))
            rows += piece_rows
        if rows < self.rows:
            parts.append(jnp.zeros((N_DEV, self.rows - rows, PACK_COLS), BF16))
        return parts[0] if len(parts) == 1 else jnp.concatenate(parts, axis=1)

    def gathered_piece(self, gathered, name):
        shape = SHARDS[name][0]
        rows = gathered[:, self.offsets[name]:self.offsets[name] + _piece_rows(name), :]
        return rows.reshape(N_DEV, -1)[:, :shape[0] * shape[1]]

    def pieces(self, gathered, name):
        return _Pieces(gathered, self.offsets[name], _piece_rows(name))


GATHER_FFN1 = _Pack(("ffn1_w_gate", "ffn1_w_up", "ffn1_w_down"), BF16_ROWS)
GATHER_MIX = _Pack(("w_out", "ple_w_gate", "w_in", "ple_w_proj", "conv_w", "conv_w_mid", "conv_w_low"), BF16_ROWS)
GATHER_FFN2 = _Pack(("ffn2_w_gate", "ffn2_w_up", "ffn2_w_down"), BF16_ROWS)
SCATTER_LATE = _Pack(("ffn2_w_gate", "ffn2_w_up", "ffn2_w_down", "w_out", "ple_w_gate", "ple_w_proj"), BF16_ROWS)
SCATTER_IN = _Pack(("w_in", "conv_w"), BF16_ROWS)
SCATTER_GATE = _Pack(("ffn1_w_gate",), BF16_ROWS)
SCATTER_UP = _Pack(("ffn1_w_up",), BF16_ROWS)
SCATTER_DOWN = _Pack(("ffn1_w_down",), BF16_ROWS)


def _pack_small(vals):
    flat = jnp.concatenate([vals[name].reshape(-1).astype(F32) for name in SMALL])
    return jnp.pad(flat, (0, SMALL_ROWS * PACK_COLS - flat.shape[0])).reshape(SMALL_ROWS, PACK_COLS)


def _unpack_small(packed, shapes):
    out, off = {}, 0
    flat = packed.reshape(-1)
    for name in SMALL:
        n = 1
        for s in shapes[name]:
            n *= s
        out[name] = flat[off:off + n].reshape(shapes[name])
        off += n
    return out


WEIGHTS = ("ffn1_norm", "ffn1_w_gate", "ffn1_w_up", "ffn1_w_down", "mix_norm", "w_in", "gm_ln_g", "gm_ln_b", "gm_w_s",
           "gm_b_s", "gm_out_norm", "conv_w", "conv_b", "dt_bias", "a_log", "d_skip", "ssm_norm", "w_out", "ffn2_norm",
           "ffn2_w_gate", "ffn2_w_up", "ffn2_w_down", "ple_norm", "ple_w_gate", "ple_b_gate", "ple_w_proj", "final_norm")


def _step(x, p, target, w, m, v):
    local = lambda d: {name: d[name][0] for name in BIG}

    shards = {name: val.astype(BF16) for name, val in local(w).items()}
    conv_high = lax.reduce_precision(w["conv_w"][0], 8, 7)
    conv_mid = lax.reduce_precision(w["conv_w"][0] - conv_high, 8, 7)
    shards["conv_w"] = conv_high.astype(BF16)
    shards["conv_w_mid"] = conv_mid.astype(BF16)
    shards["conv_w_low"] = (w["conv_w"][0] - conv_high - conv_mid).astype(BF16)
    g_ffn1 = _comm_alone([_gather_comm(GATHER_FFN1.pack_local(shards))], "gather_ffn1")[0]

    row = lambda name: w[name].reshape(1, -1)
    gm_w_s = w["gm_w_s"][0]
    gm_b_st = jnp.transpose(w["gm_b_s"][0])
    ffn1 = (row("ffn1_norm"),) + tuple(GATHER_FFN1.pieces(g_ffn1, name) for name in GATHER_FFN1.names)
    gm = (row("gm_ln_g"), row("gm_ln_b"), gm_w_s, gm_b_st, row("gm_out_norm"))

    h1, n1, a1, b1, s1, g_mix = _ffn_fwd(x, *ffn1, "ffn1_fwd", comm=_gather_comm(GATHER_MIX.pack_local(shards)))
    w_in_t = GATHER_MIX.gathered_piece(g_mix, "w_in").reshape(IN_PROJ, D_MODEL)
    w_in_t = jnp.concatenate([w_in_t, jnp.zeros((IN_PROJ_PAD - IN_PROJ, D_MODEL), BF16)], axis=0)
    w_proj_t = GATHER_MIX.gathered_piece(g_mix, "ple_w_proj").reshape(D_MODEL, D_PLE)
    conv_w = sum(GATHER_MIX.gathered_piece(g_mix, name).astype(F32) for name in ("conv_w", "conv_w_mid", "conv_w_low"))
    conv_w = conv_w.reshape(CONV_DIM, SSM_CONV).T
    ssd = (row("dt_bias"), row("a_log"), row("d_skip"), row("ssm_norm"))
    w_out = GATHER_MIX.pieces(g_mix, "w_out")

    proj, n2, x16, xc = _mix_in_fwd(h1, row("mix_norm"), w_in_t, conv_w, row("conv_b"))
    ya = _gm_fwd(proj, *gm)
    yb, s_all, g_ffn2 = _ssd_fwd(proj, xc, *ssd, comm=_gather_comm(GATHER_FFN2.pack_local(shards)))
    ffn2 = (row("ffn2_norm"),) + tuple(GATHER_FFN2.pieces(g_ffn2, name) for name in GATHER_FFN2.names)
    h3, n3, a3, b3, s3, h2 = _ffn_fwd(h1, *ffn2, "ffn2_fwd", mixed=(ya, yb, w_out))

    g, gp = {}, {}
    dh3, loss, gp["ple_w_gate"], d_w_proj, g["ple_norm"], g["ple_b_gate"], g["final_norm"] = _tail(
        h3, p, target, row("ple_norm"), GATHER_MIX.pieces(g_mix, "ple_w_gate"), row("ple_b_gate"), w_proj_t,
        row("final_norm"))
    gp["ple_w_proj"] = d_w_proj.T

    dh2, da3, db3, g["ffn2_norm"] = _ffn_dgrad(h2, dh3, a3, b3, *ffn2, "ffn2_dgrad")
    gp["ffn2_w_gate"] = _wgrad(n3, da3, FF_BN, "ffn2_wgrad_gate", transpose_out=True)
    gp["ffn2_w_up"] = _wgrad(n3, db3, FF_BN, "ffn2_wgrad_up", transpose_out=True)
    gp["ffn2_w_down"] = _wgrad(s3, dh3, DOWN_BN, "ffn2_wgrad_down", scale=0.5, bk=DOWN_BK)

    dya, dyb = _out_proj_dgrad(dh2, w_out)
    gp["w_out"] = jnp.concatenate([_wgrad(ya, dh2, SQUARE_BN, "w_out_wgrad_a"), _wgrad(yb, dh2, SQUARE_BN, "w_out_wgrad_b")], axis=0)

    dp_zxd, d_conv_w, g["conv_b"], g["dt_bias"], g["a_log"], g["d_skip"], g["ssm_norm"], parts_late = _ssd_bwd(
        proj, x16, xc, dyb, s_all, conv_w, *ssd, comm=_exchange_comm(SCATTER_LATE.pack_owner_major(gp)))
    gp["conv_w"] = d_conv_w.T
    dp_uv, g["gm_ln_g"], g["gm_ln_b"], g["gm_w_s"], dbst, g["gm_out_norm"] = _gm_bwd(proj, dya, *gm)
    g["gm_b_s"] = jnp.transpose(dbst)

    parts = {}
    gp["w_in"] = jnp.concatenate([_wgrad(n2, dp_uv, SQUARE_BN, "w_in_wgrad_uv", transpose_out=True),
                                  _wgrad(n2, dp_zxd, ZXD_BN, "w_in_wgrad_zxd", transpose_out=True)], axis=0)[:IN_PROJ]
    dh1, g["mix_norm"], parts[SCATTER_IN] = _mix_in_dgrad(h1, dh2, dp_uv, dp_zxd, row("mix_norm"), w_in_t,
                                                          comm=_exchange_comm(SCATTER_IN.pack_owner_major(gp)))

    dx, da1, db1, g["ffn1_norm"] = _ffn_dgrad(x, dh1, a1, b1, *ffn1, "ffn1_dgrad")
    gp["ffn1_w_gate"], small_parts = _wgrad(n1, da1, FF_BN, "ffn1_wgrad_gate", transpose_out=True,
                                            comm=_gather_comm(_pack_small(g)))
    gp["ffn1_w_up"], parts[SCATTER_GATE] = _wgrad(n1, db1, FF_BN, "ffn1_wgrad_up", transpose_out=True,
                                                  comm=_exchange_comm(SCATTER_GATE.pack_owner_major(gp)))
    gp["ffn1_w_down"], parts[SCATTER_UP] = _wgrad(s1, dh1, DOWN_BN, "ffn1_wgrad_down", scale=0.5, bk=DOWN_BK,
                                                  comm=_exchange_comm(SCATTER_UP.pack_owner_major(gp)))
    parts[SCATTER_DOWN] = _comm_alone([_exchange_comm(SCATTER_DOWN.pack_owner_major(gp))], "scatter_ffn1_down")[0]
    parts[SCATTER_LATE] = parts_late

    res_big = {}
    for pack, pack_parts in parts.items():
        for name in pack.names:
            shape, transposed = SHARDS[name]
            if name in ("ple_w_proj", "conv_w"):
                nat = pack.gathered_piece(pack_parts, name).reshape((N_DEV,) + shape[::-1])
                res_big[name] = _sum_adamw(jnp.transpose(nat, (0, 2, 1)), w[name][0], m[name][0], v[name][0], shape[0],
                                           "adamw_" + name)
            else:
                flip = (lambda a: jnp.transpose(a, (0, 2, 1))) if transposed else (lambda a: a)
                res = _adamw_shard(pack_parts, pack.offsets[name], flip(w[name]), flip(m[name]), flip(v[name]),
                                   "adamw_" + name, n_tiles=4 if name == "w_in" else 2)
                res_big[name] = [flip(r) for r in res]

    small_shapes = {name: w[name].shape for name in SMALL}
    res_small = _sum_adamw(small_parts, _pack_small(w), _pack_small(m), _pack_small(v), SMALL_ROWS, "adamw_small")
    res_small = [_unpack_small(r, small_shapes) for r in res_small]

    outs = []
    for k in range(4):
        for name in WEIGHTS:
            if name in res_small[k]:
                outs.append(res_small[k][name])
            else:
                outs.append(res_big[name][k].reshape(w[name].shape))
    return loss[0, 0], dx, outs


def kernel(x, p, ffn1_norm, ffn1_w_gate, ffn1_w_up, ffn1_w_down, mix_norm, w_in, gm_ln_g, gm_ln_b, gm_w_s, gm_b_s, gm_out_norm, conv_w, conv_b, dt_bias, a_log, d_skip, ssm_norm, w_out, ffn2_norm, ffn2_w_gate, ffn2_w_up, ffn2_w_down, ple_norm, ple_w_gate, ple_b_gate, ple_w_proj, final_norm, loss_target, m_ffn1_norm, m_ffn1_w_gate, m_ffn1_w_up, m_ffn1_w_down, m_mix_norm, m_w_in, m_gm_ln_g, m_gm_ln_b, m_gm_w_s, m_gm_b_s, m_gm_out_norm, m_conv_w, m_conv_b, m_dt_bias, m_a_log, m_d_skip, m_ssm_norm, m_w_out, m_ffn2_norm, m_ffn2_w_gate, m_ffn2_w_up, m_ffn2_w_down, m_ple_norm, m_ple_w_gate, m_ple_b_gate, m_ple_w_proj, m_final_norm, v_ffn1_norm, v_ffn1_w_gate, v_ffn1_w_up, v_ffn1_w_down, v_mix_norm, v_w_in, v_gm_ln_g, v_gm_ln_b, v_gm_w_s, v_gm_b_s, v_gm_out_norm, v_conv_w, v_conv_b, v_dt_bias, v_a_log, v_d_skip, v_ssm_norm, v_w_out, v_ffn2_norm, v_ffn2_w_gate, v_ffn2_w_up, v_ffn2_w_down, v_ple_norm, v_ple_w_gate, v_ple_b_gate, v_ple_w_proj, v_final_norm):
    args = locals()
    w = {name: args[name] for name in WEIGHTS}
    m = {name: args["m_" + name] for name in WEIGHTS}
    v = {name: args["v_" + name] for name in WEIGHTS}
    loss, dx, outs = _step(x[0], p[0, 0], loss_target[0], w, m, v)
    loss = lax.psum(loss, AXES)
    return (loss, dx[None], *outs)
```

```python
import functools
from typing import NamedTuple

import jax
import jax.numpy as jnp
from jax import lax
from jax.experimental import pallas as pl
from jax.experimental.pallas import tpu as pltpu

F32 = jnp.float32
BF16 = jnp.bfloat16
MESH = pl.DeviceIdType.MESH
N_DEV = 8

D_MODEL = 1024
D_FF = 2816
D_PLE = 256
GM_WIDTH = 1024
GM_HEADS = 8
GM_HEAD_DIM = 128
CHUNK = 128
SSM_WIDTH = 1024
SSM_HEADS = 16
SSM_HEAD_DIM = 64
SSM_GROUPS = 2
SSM_STATE = 128
SSM_CONV = 4
CONV_DIM = SSM_WIDTH + 2 * SSM_GROUPS * SSM_STATE
IN_PROJ = 2 * GM_WIDTH + SSM_WIDTH + CONV_DIM + SSM_HEADS
LANES = 128
BF16_ROWS = 16
F32_ROWS = 8
IN_PROJ_PAD = IN_PROJ - SSM_HEADS + LANES
UV_W = 2 * GM_WIDTH
ZXD_W = IN_PROJ_PAD - UV_W
HALO = 8
EPS = 1e-6

ADAM_LR = 0.001
ADAM_B1 = 0.9
ADAM_B2 = 0.999
ADAM_EPS = 1e-08
ADAM_WD = 0.01
ADAM_STEP = 10

VMEM_LIMIT = 56 * 1024 * 1024
PACK_COLS = 1024


def _rms(x, g):
    return x * lax.rsqrt(jnp.mean(x * x, axis=-1, keepdims=True) + EPS) * g


def _gelu(x):
    return 0.5 * x * (1.0 + lax.erf(x * (2.0 ** -0.5)))


def _silu(x):
    return x * jax.nn.sigmoid(x)


def _dot(a, b):
    return jnp.dot(a.astype(BF16), b.astype(BF16), preferred_element_type=F32)


def _dot_nt(a, b):
    return lax.dot_general(a.astype(BF16), b.astype(BF16), (((1,), (1,)), ((), ())), preferred_element_type=F32)


def _dot_tn(a, b):
    return lax.dot_general(a.astype(BF16), b.astype(BF16), (((0,), (0,)), ((), ())), preferred_element_type=F32)


def _split3(x):
    hi = x.astype(BF16)
    rest = x - hi.astype(F32)
    mid = rest.astype(BF16)
    return hi, mid, (rest - mid.astype(F32)).astype(BF16)


def _exact_dot(x, mask, dims, x_first=True, n_terms=3):
    terms = [lax.dot_general(*((t, mask) if x_first else (mask, t)), (dims, ((), ())), preferred_element_type=F32)
             for t in _split3(x)[:n_terms]]
    total = terms[0]
    for term in terms[1:]:
        total = total + term
    return total


def _mask_product(fwd_dims, fwd_x_first, bwd_dims, bwd_x_first, bwd_terms=3):
    @jax.custom_vjp
    def product(x, mask):
        return _exact_dot(x, mask, fwd_dims, fwd_x_first)

    def fwd(x, mask):
        return product(x, mask), mask

    def bwd(mask, g):
        return _exact_dot(g, mask, bwd_dims, bwd_x_first, bwd_terms), jnp.zeros_like(mask)

    product.defvjp(fwd, bwd)
    return product


_widen = _mask_product(((1,), (0,)), True, ((1,), (1,)), True, bwd_terms=2)
_cumsum_rows = _mask_product(((1,), (0,)), False, ((0,), (0,)), False)
_cumsum_cols = _mask_product(((0,), (0,)), True, ((1,), (1,)), False)


class _Pieces(NamedTuple):
    gathered: jax.Array
    row_off: int
    rows: int


class _Comm(NamedTuple):
    phases: object
    src: jax.Array
    dst: jax.ShapeDtypeStruct


def _tiled(body, name, n_steps, tiled_in, full_in, big_in, tiled_out, acc_out, scratch=(), reverse=False, comm=None):
    n_t, n_f, n_b, n_to, n_a = len(tiled_in), len(full_in), len(big_in), len(tiled_out), len(acc_out)
    n_c = 1 if comm else 0

    def row(i):
        return n_steps - 1 - i if reverse else i

    in_specs, args = [], []
    for arr, br, bc, cb in tiled_in:
        if callable(cb):
            in_specs.append(pl.BlockSpec((br, bc), cb))
        else:
            in_specs.append(pl.BlockSpec((br, bc), functools.partial(lambda i, cb: (row(i), cb), cb=cb)))
        args.append(arr)
    for arr in full_in:
        in_specs.append(pl.BlockSpec(arr.shape, functools.partial(lambda i, nd: (0,) * nd, nd=arr.ndim)))
        args.append(arr)
    big_shapes, n_copies = [], 0
    for big in big_in:
        in_specs.append(pl.BlockSpec(memory_space=pl.ANY))
        if isinstance(big, _Pieces):
            args.append(big.gathered)
            big_shapes.append(((N_DEV * big.rows, PACK_COLS), big.gathered.dtype))
            n_copies += N_DEV
        else:
            args.append(big)
            big_shapes.append((big.shape, big.dtype))
            n_copies += 1
    if comm:
        in_specs.append(pl.BlockSpec(memory_space=pl.ANY))
        args.append(comm.src)
    out_specs, out_shape = [], []
    for rows, cols, dt, br in tiled_out:
        out_specs.append(pl.BlockSpec((br, cols), lambda i: (row(i), 0)))
        out_shape.append(jax.ShapeDtypeStruct((rows, cols), dt))
    for shp, dt in acc_out:
        out_specs.append(pl.BlockSpec(shp, functools.partial(lambda i, nd: (0,) * nd, nd=len(shp))))
        out_shape.append(jax.ShapeDtypeStruct(shp, dt))
    if comm:
        out_specs.append(pl.BlockSpec(memory_space=pl.ANY))
        out_shape.append(comm.dst)
    scratch_shapes = [pltpu.VMEM(shp, dt) for shp, dt in big_shapes] + list(scratch)
    if n_copies:
        scratch_shapes.append(pltpu.SemaphoreType.DMA((n_copies,)))
    if comm:
        scratch_shapes += [pltpu.SemaphoreType.DMA((N_DEV - 1,)), pltpu.SemaphoreType.DMA((N_DEV - 1,)), pltpu.SemaphoreType.DMA]

    def kern(*refs):
        n_in = n_t + n_f + n_b + n_c
        ins = refs[: n_t + n_f]
        big_hbm = refs[n_t + n_f : n_t + n_f + n_b]
        outs = refs[n_in : n_in + n_to + n_a]
        rest = refs[n_in + n_to + n_a + n_c :]
        big_vmem, scr = rest[:n_b], rest[n_b:]
        if comm:
            scr, comm_sems = scr[:-3], scr[-3:]
            comm_start, comm_mid, comm_finish = comm.phases(refs[n_in - 1], refs[n_in + n_to + n_a], *comm_sems)
        if n_copies:
            scr, copy_sems = scr[:-1], scr[-1]
        step = pl.program_id(0)

        @pl.when(step == 0)
        def _():
            copies = []
            for big, src, dst in zip(big_in, big_hbm, big_vmem):
                if isinstance(big, _Pieces):
                    for j in range(N_DEV):
                        copies.append((src.at[j, pl.ds(big.row_off, big.rows), :], dst.at[pl.ds(j * big.rows, big.rows), :]))
                else:
                    copies.append((src, dst))
            copies = [pltpu.make_async_copy(a, b, copy_sems.at[k]) for k, (a, b) in enumerate(copies)]
            for cp in copies:
                cp.start()
            for cp in copies:
                cp.wait()
            for acc in outs[n_to:]:
                acc[...] = jnp.zeros(acc.shape, acc.dtype)
            if comm:
                comm_start()

        body(row(step), *ins, *big_vmem, *outs, *scr)
        if comm:
            pl.when(step == (n_steps - 1) // 2)(comm_mid)
            pl.when(step == n_steps - 1)(comm_finish)

    res = pl.pallas_call(
        kern,
        out_shape=out_shape,
        grid=(n_steps,),
        in_specs=in_specs,
        out_specs=out_specs,
        scratch_shapes=scratch_shapes,
        name=name,
        compiler_params=pltpu.CompilerParams(dimension_semantics=("arbitrary",), vmem_limit_bytes=VMEM_LIMIT),
    )(*args)
    return res


FWD_CHUNKS = ((0, 1536), (1536, D_FF))
DGRAD_CHUNKS = ((0, 1024), (1024, 2048), (2048, D_FF))
FFN_TM = 256


def _ffn_fwd(h, g, wg_t, wu_t, wd, name, comm=None, mixed=None):
    T = h.shape[0]

    def ffn(x, g_ref, wg_ref, wu_ref, wd_ref, o_ref, n_ref, a_ref, b_ref, s_ref):
        n = _rms(x, g_ref[...]).astype(BF16)
        n_ref[...] = n
        f = jnp.zeros(x.shape, F32)
        for lo, hi in FWD_CHUNKS:
            a = _dot_nt(n, wg_ref[lo:hi, :])
            b = _dot_nt(n, wu_ref[lo:hi, :])
            s = (_silu(a) * b).astype(BF16)
            a_ref[:, lo:hi] = a.astype(BF16)
            b_ref[:, lo:hi] = b.astype(BF16)
            s_ref[:, lo:hi] = s
            f = f + jnp.dot(s, wd_ref[lo:hi, :], preferred_element_type=F32)
        o_ref[...] = x + 0.5 * f

    def body_plain(i, h_ref, *refs):
        ffn(h_ref[...], *refs)

    def body_mixed(i, h_ref, ya_ref, yb_ref, g_ref, wg_ref, wu_ref, wd_ref, wo_ref, o_ref, n_ref, a_ref, b_ref, s_ref, x_ref):
        x = (h_ref[...] + jnp.dot(ya_ref[...], wo_ref[:GM_WIDTH, :], preferred_element_type=F32)
             + jnp.dot(yb_ref[...], wo_ref[GM_WIDTH:, :], preferred_element_type=F32))
        x_ref[...] = x
        ffn(x, g_ref, wg_ref, wu_ref, wd_ref, o_ref, n_ref, a_ref, b_ref, s_ref)

    body = body_mixed if mixed else body_plain
    tiled_in, big_in = [(h, FFN_TM, D_MODEL, 0)], [wg_t, wu_t, wd]
    tiled_out = [(T, D_MODEL, F32, FFN_TM), (T, D_MODEL, BF16, FFN_TM), (T, D_FF, BF16, FFN_TM), (T, D_FF, BF16, FFN_TM),
                 (T, D_FF, BF16, FFN_TM)]
    if mixed:
        tiled_in += [(mixed[0], FFN_TM, GM_WIDTH, 0), (mixed[1], FFN_TM, SSM_WIDTH, 0)]
        big_in.append(mixed[2])
        tiled_out.append((T, D_MODEL, F32, FFN_TM))
    return _tiled(body, name, T // FFN_TM, tiled_in, [g], big_in, tiled_out, [], comm=comm)


def _ffn_dgrad(h, dout, a16, b16, g, wg_t, wu_t, wd, name):
    T = h.shape[0]

    def body(i, h_ref, do_ref, a_ref, b_ref, g_ref, wg_ref, wu_ref, wd_ref, dh_ref, da_ref, db_ref, dg_ref):
        dout = do_ref[...]
        _, rms_vjp = jax.vjp(_rms, h_ref[...], g_ref[...])
        dfo = (0.5 * dout).astype(BF16)
        dn = jnp.zeros(dout.shape, F32)
        for lo, hi in DGRAD_CHUNKS:
            a = a_ref[:, lo:hi].astype(F32)
            b = b_ref[:, lo:hi].astype(F32)
            sg = jax.nn.sigmoid(a)
            ds = _dot_nt(dfo, wd_ref[lo:hi, :])
            db = (ds * (a * sg)).astype(BF16)
            da = (ds * b * (sg * (1.0 + a * (1.0 - sg)))).astype(BF16)
            dn = dn + _dot(da, wg_ref[lo:hi, :]) + _dot(db, wu_ref[lo:hi, :])
            da_ref[:, lo:hi] = da
            db_ref[:, lo:hi] = db
        dx, dg = rms_vjp(dn)
        dh_ref[...] = dout + dx
        dg_ref[...] += dg

    return _tiled(body, name, T // FFN_TM,
                  [(h, FFN_TM, D_MODEL, 0), (dout, FFN_TM, D_MODEL, 0), (a16, FFN_TM, D_FF, 0), (b16, FFN_TM, D_FF, 0)],
                  [g], [wg_t, wu_t, wd],
                  [(T, D_MODEL, F32, FFN_TM), (T, D_FF, BF16, FFN_TM), (T, D_FF, BF16, FFN_TM)], [((1, D_MODEL), F32)])


FF_BN = D_FF // 2
DOWN_BN, DOWN_BK = 512, 1024
SQUARE_BN = 1024
ZXD_BN = ZXD_W // 3


def _wgrad(a, b, bn, name, scale=None, transpose_out=False, bk=2048, comm=None):
    T, M = a.shape
    N = b.shape[1]
    bk = min(bk, T)
    assert M % LANES == 0 and N % bn == 0 and T % bk == 0
    n_j, n_k = N // bn, T // bk
    n_c = 1 if comm else 0

    def kern(*refs):
        a_ref, b_ref, o_ref, acc_ref = refs[0], refs[1], refs[2 + n_c], refs[3 + 2 * n_c]
        j, k = pl.program_id(0), pl.program_id(1)
        if comm:
            comm_start, comm_mid, comm_finish = comm.phases(refs[2], refs[4], *refs[6:])
            pl.when((j == 0) & (k == 0))(comm_start)

        @pl.when(k == 0)
        def _():
            acc_ref[...] = jnp.zeros(acc_ref.shape, F32)

        bv = b_ref[...]
        if scale is not None:
            bv = bv * scale
        acc_ref[...] += _dot_tn(a_ref[...], bv)

        @pl.when(k == n_k - 1)
        def _():
            acc = acc_ref[...]
            o_ref[...] = (acc.T if transpose_out else acc).astype(BF16)

        if comm:
            pl.when((j == (n_j - 1) // 2) & (k == n_k - 1))(comm_mid)
            pl.when((j == n_j - 1) & (k == n_k - 1))(comm_finish)

    if transpose_out:
        out_shape, out_spec = (N, M), pl.BlockSpec((bn, M), lambda j, k: (j, 0))
    else:
        out_shape, out_spec = (M, N), pl.BlockSpec((M, bn), lambda j, k: (0, j))
    any_spec = pl.BlockSpec(memory_space=pl.ANY)
    comm_sems = [pltpu.SemaphoreType.DMA((N_DEV - 1,)), pltpu.SemaphoreType.DMA((N_DEV - 1,)), pltpu.SemaphoreType.DMA]
    res = pl.pallas_call(
        kern,
        out_shape=[jax.ShapeDtypeStruct(out_shape, BF16)] + ([comm.dst] if comm else []),
        grid=(n_j, n_k),
        in_specs=[pl.BlockSpec((bk, M), lambda j, k: (k, 0)), pl.BlockSpec((bk, bn), lambda j, k: (k, j))] + [any_spec] * n_c,
        out_specs=[out_spec] + [any_spec] * n_c,
        scratch_shapes=[pltpu.VMEM((M, bn), F32)] + (comm_sems if comm else []),
        name=name,
        compiler_params=pltpu.CompilerParams(dimension_semantics=("arbitrary", "arbitrary"), vmem_limit_bytes=VMEM_LIMIT),
    )(a, b, *([comm.src] if comm else []))
    return res if comm else res[0]


PROJ_TM = 512
PROJ_DGRAD_TM = 256
UVZ_W = 2 * GM_WIDTH + SSM_WIDTH
PROJ_KEPT = UVZ_W + LANES
Z_BLK = 2 * GM_WIDTH // SSM_WIDTH
DT_BLK = UVZ_W // LANES


def _mix_in_fwd(h, g, w_in_t, conv_w, conv_b):
    T = h.shape[0]

    def body(i, h_ref, g_ref, cw_ref, cb_ref, w_ref, p_ref, n_ref, x_ref, xc_ref, ext_ref):
        @pl.when(i == 0)
        def _():
            ext_ref[0:HALO, :] = jnp.zeros((HALO, CONV_DIM), F32)

        n = _rms(h_ref[...], g_ref[...]).astype(BF16)
        n_ref[...] = n
        proj = _dot_nt(n, w_ref[...])
        p_ref[:, :UVZ_W] = proj[:, :UVZ_W]
        p_ref[:, UVZ_W:] = proj[:, UVZ_W + CONV_DIM:]
        xbc = proj[:, UVZ_W:UVZ_W + CONV_DIM]
        x_ref[...] = xbc.astype(BF16)
        ext_ref[HALO:, :] = xbc
        xc_ref[...] = _conv_taps(ext_ref, cw_ref[...], cb_ref[...], PROJ_TM)
        ext_ref[0:HALO, :] = ext_ref[PROJ_TM:PROJ_TM + HALO, :]

    return _tiled(body, "mix_in_fwd", T // PROJ_TM, [(h, PROJ_TM, D_MODEL, 0)], [g, conv_w, conv_b], [w_in_t],
                  [(T, PROJ_KEPT, F32, PROJ_TM), (T, D_MODEL, BF16, PROJ_TM), (T, CONV_DIM, BF16, PROJ_TM),
                   (T, CONV_DIM, F32, PROJ_TM)], [],
                  scratch=[pltpu.VMEM((HALO + PROJ_TM, CONV_DIM), F32)])


def _mix_in_dgrad(h, dh_in, dp_uv, dp_zxd, g, w_in_t, comm=None):
    T = h.shape[0]

    def body(i, h_ref, dh_ref, duv_ref, dzxd_ref, g_ref, w_ref, o_ref, dg_ref):
        dn = _dot(duv_ref[...], w_ref[:UV_W, :]) + _dot(dzxd_ref[...], w_ref[UV_W:, :])
        _, rms_vjp = jax.vjp(_rms, h_ref[...], g_ref[...])
        dx, dg = rms_vjp(dn)
        o_ref[...] = dh_ref[...] + dx
        dg_ref[...] += dg

    return _tiled(body, "mix_in_dgrad", T // PROJ_DGRAD_TM,
                  [(h, PROJ_DGRAD_TM, D_MODEL, 0), (dh_in, PROJ_DGRAD_TM, D_MODEL, 0), (dp_uv, PROJ_DGRAD_TM, UV_W, 0),
                   (dp_zxd, PROJ_DGRAD_TM, ZXD_W, 0)], [g], [w_in_t],
                  [(T, D_MODEL, F32, PROJ_DGRAD_TM)], [((1, D_MODEL), F32)], comm=comm)


def _out_proj_dgrad(dh, w_out):
    T = dh.shape[0]

    def body(i, dh_ref, w_ref, dya_ref, dyb_ref):
        d = dh_ref[...].astype(BF16)
        dya_ref[...] = _dot_nt(d, w_ref[:GM_WIDTH, :])
        dyb_ref[...] = _dot_nt(d, w_ref[GM_WIDTH:, :])

    return _tiled(body, "out_proj_dgrad", T // PROJ_TM, [(dh, PROJ_TM, D_MODEL, 0)], [], [w_out],
                  [(T, GM_WIDTH, F32, PROJ_TM), (T, SSM_WIDTH, F32, PROJ_TM)], [])


def _gm_chunk(u, v, ln_g, ln_b, b_st, out_g, *w_heads):
    ug = _gelu(u)
    vg = _gelu(v)
    mu = jnp.mean(vg, axis=-1, keepdims=True)
    xc = vg - mu
    vn = xc * lax.rsqrt(jnp.mean(xc * xc, axis=-1, keepdims=True) + EPS) * ln_g + ln_b
    t_idx = lax.broadcasted_iota(jnp.int32, (CHUNK, CHUNK), 0)
    s_idx = lax.broadcasted_iota(jnp.int32, (CHUNK, CHUNK), 1)
    causal = t_idx >= s_idx
    mixed = []
    for hd in range(GM_HEADS):
        wm = jnp.where(causal, w_heads[hd], 0.0)
        cols = slice(hd * GM_HEAD_DIM, (hd + 1) * GM_HEAD_DIM)
        mixed.append(_dot(wm, vn[:, cols]) + b_st[:, hd:hd + 1])
    ya0 = ug * jnp.concatenate(mixed, axis=1)
    return _rms(ya0, out_g)


GM_FWD_CHUNKS = 2


def _gm_fwd(proj, ln_g, ln_b, w_s, b_st, out_g):
    T = proj.shape[0]

    rows = GM_FWD_CHUNKS * CHUNK

    def body(i, u_ref, v_ref, lg_ref, lb_ref, w_ref, bs_ref, og_ref, ya_ref):
        w_heads = [w_ref[hd] for hd in range(GM_HEADS)]
        for c in range(GM_FWD_CHUNKS):
            tok = pl.ds(c * CHUNK, CHUNK)
            ya = _gm_chunk(u_ref[tok, :], v_ref[tok, :], lg_ref[...], lb_ref[...], bs_ref[...], og_ref[...], *w_heads)
            ya_ref[tok, :] = ya.astype(BF16)

    return _tiled(body, "gmlp_fwd", T // rows, [(proj, rows, GM_WIDTH, 0), (proj, rows, GM_WIDTH, 1)],
                  [ln_g, ln_b, w_s, b_st, out_g], [], [(T, GM_WIDTH, BF16, rows)], [])[0]


def _gm_bwd(proj, dya, ln_g, ln_b, w_s, b_st, out_g):
    T = proj.shape[0]

    def body(i, u_ref, v_ref, dy_ref, lg_ref, lb_ref, w_ref, bs_ref, og_ref, duv_ref, dlg_ref, dlb_ref, dw_ref, dbs_ref,
             dog_ref):
        w_heads = [w_ref[hd] for hd in range(GM_HEADS)]
        _, vjp = jax.vjp(_gm_chunk, u_ref[...], v_ref[...], lg_ref[...], lb_ref[...], bs_ref[...], og_ref[...], *w_heads)
        grads = vjp(dy_ref[...])
        duv_ref[:, :GM_WIDTH] = grads[0].astype(BF16)
        duv_ref[:, GM_WIDTH:] = grads[1].astype(BF16)
        dlg_ref[...] += grads[2]
        dlb_ref[...] += grads[3]
        dbs_ref[...] += grads[4]
        dog_ref[...] += grads[5]
        for hd in range(GM_HEADS):
            dw_ref[hd] += grads[6 + hd]

    return _tiled(body, "gmlp_bwd", T // CHUNK,
                  [(proj, CHUNK, GM_WIDTH, 0), (proj, CHUNK, GM_WIDTH, 1), (dya, CHUNK, GM_WIDTH, 0)],
                  [ln_g, ln_b, w_s, b_st, out_g], [], [(T, UV_W, BF16, CHUNK)],
                  [((1, GM_WIDTH), F32), ((1, GM_WIDTH), F32), ((GM_HEADS, CHUNK, CHUNK), F32),
                   ((CHUNK, GM_HEADS), F32), ((1, GM_WIDTH), F32)])


def _ssd_chunk(xc, z, dtr, s_in, dt_bias, a_log, d_skip, norm_g):
    half = SSM_WIDTH // SSM_GROUPS
    l_idx = lax.broadcasted_iota(jnp.int32, (CHUNK, CHUNK), 0)
    s_idx = lax.broadcasted_iota(jnp.int32, (CHUNK, CHUNK), 1)
    causal = l_idx >= s_idx
    head_of_col = lax.broadcasted_iota(jnp.int32, (SSM_HEADS, SSM_WIDTH), 1) // SSM_HEAD_DIM
    expand = (head_of_col == lax.broadcasted_iota(jnp.int32, (SSM_HEADS, SSM_WIDTH), 0)).astype(BF16)

    xcs = _silu(xc)
    xs = xcs[:, :SSM_WIDTH]
    dt = jax.nn.softplus(dtr + dt_bias)
    adt = dt * (-jnp.exp(a_log))
    acs = _cumsum_rows(adt, causal.astype(BF16))
    acs_t = _cumsum_cols(adt, (l_idx <= s_idx).astype(BF16))
    tot = acs[CHUNK - 1:CHUNK, :]
    dt_w = _widen(dt, expand)
    out_decay_w = _widen(jnp.exp(acs), expand)
    state_decay_w = _widen(jnp.exp(tot - acs), expand)
    chunk_decay_w = _widen(jnp.exp(tot), expand)
    d_skip_w = _widen(d_skip, expand)
    xdt = xs * dt_w
    xdt_decayed = xdt * state_decay_w

    y_diag, y_off, states = [], [], []
    for grp in range(SSM_GROUPS):
        b0 = SSM_WIDTH + grp * SSM_STATE
        c0 = SSM_WIDTH + SSM_GROUPS * SSM_STATE + grp * SSM_STATE
        bm = xcs[:, b0:b0 + SSM_STATE].astype(BF16)
        cm = xcs[:, c0:c0 + SSM_STATE].astype(BF16)
        cb = _dot_nt(cm, bm)
        for k in range(grp * SSM_HEADS // SSM_GROUPS, (grp + 1) * SSM_HEADS // SSM_GROUPS):
            decay = jnp.exp(jnp.where(causal, acs[:, k:k + 1] - acs_t[k:k + 1, :], -jnp.inf))
            y_diag.append(_dot(cb * decay, xdt[:, k * SSM_HEAD_DIM:(k + 1) * SSM_HEAD_DIM]))
        cols = slice(grp * half, (grp + 1) * half)
        states.append(_dot_tn(bm, xdt_decayed[:, cols]))
        y_off.append(_dot(cm, s_in[:, cols]))
    y = jnp.concatenate(y_diag, axis=1) + jnp.concatenate(y_off, axis=1) * out_decay_w + xs * d_skip_w
    s_out = s_in * chunk_decay_w + jnp.concatenate(states, axis=1)
    y = y * _silu(z)
    normed = []
    for grp in range(SSM_GROUPS):
        yg = y[:, grp * half:(grp + 1) * half]
        normed.append(yg * lax.rsqrt(jnp.mean(yg * yg, axis=-1, keepdims=True) + EPS))
    return jnp.concatenate(normed, axis=1) * norm_g, s_out


def _sum_row_tiles(x):
    return x.reshape(x.shape[0] // F32_ROWS, F32_ROWS, x.shape[1]).sum(axis=0)


def _conv_taps(ext_ref, w, b, rows):
    y = b
    for k in range(SSM_CONV):
        y = y + w[k:k + 1, :] * ext_ref[pl.ds(HALO - (SSM_CONV - 1) + k, rows), :]
    return y


def _ssd_fwd(proj, xc, dt_bias, a_log, d_skip, norm_g, comm=None):
    T = proj.shape[0]
    n_chunks = T // CHUNK

    def body(i, z_ref, xc_ref, dt_ref, dtb_ref, al_ref, dsk_ref, ng_ref, yb_ref, sin_ref, st_ref):
        @pl.when(i == 0)
        def _():
            st_ref[...] = jnp.zeros(st_ref.shape, F32)

        s_in = st_ref[...]
        yb, s_out = _ssd_chunk(xc_ref[...], z_ref[...], dt_ref[:, 0:SSM_HEADS], s_in, dtb_ref[...], al_ref[...],
                               dsk_ref[...], ng_ref[...])
        yb_ref[...] = yb.astype(BF16)
        sin_ref[...] = s_in
        st_ref[...] = s_out

    return _tiled(body, "ssd_fwd", n_chunks,
                  [(proj, CHUNK, SSM_WIDTH, Z_BLK), (xc, CHUNK, CONV_DIM, 0), (proj, CHUNK, LANES, DT_BLK)],
                  [dt_bias, a_log, d_skip, norm_g], [],
                  [(T, SSM_WIDTH, BF16, CHUNK), (n_chunks * SSM_STATE, SSM_WIDTH, F32, SSM_STATE)], [],
                  scratch=[pltpu.VMEM((SSM_STATE, SSM_WIDTH), F32)], comm=comm)


def _ssd_bwd(proj, x16, xc, dyb, s_all, conv_w, dt_bias, a_log, d_skip, norm_g, comm=None):
    T = proj.shape[0]
    n_chunks = T // CHUNK

    def body(i, z_ref, x_ref, xc_ref, dt_ref, dy_ref, sin_ref, cw_ref, dtb_ref, al_ref, dsk_ref, ng_ref,
             dzxd_ref, dcw_ref, dcb_ref, ddtb_ref, dal_ref, ddsk_ref, dng_ref, dext_ref, dst_ref, cw_acc, cb_acc):
        @pl.when(i == n_chunks - 1)
        def _():
            dext_ref[CHUNK:, :] = jnp.zeros((HALO, CONV_DIM), F32)
            dst_ref[...] = jnp.zeros(dst_ref.shape, F32)
            cw_acc[...] = jnp.zeros(cw_acc.shape, F32)
            cb_acc[...] = jnp.zeros(cb_acc.shape, F32)

        _, vjp = jax.vjp(_ssd_chunk, xc_ref[...], z_ref[...], dt_ref[:, 0:SSM_HEADS], sin_ref[...], dtb_ref[...], al_ref[...],
                         dsk_ref[...], ng_ref[...])
        dxc, dz, ddtr, ds_in, ddtb, dal, ddsk, dng = vjp((dy_ref[...], dst_ref[...]))
        dst_ref[...] = ds_in
        ddtb_ref[...] += ddtb
        dal_ref[...] += dal
        ddsk_ref[...] += ddsk
        dng_ref[...] += dng
        dext_ref[0:CHUNK, :] = dxc
        cw = cw_ref[...]
        x = x_ref[...].astype(F32)
        dx = jnp.zeros((CHUNK, CONV_DIM), F32)
        for k in range(SSM_CONV):
            shifted = dext_ref[pl.ds(SSM_CONV - 1 - k, CHUNK), :]
            dx = dx + cw[k:k + 1, :] * shifted
            cw_acc[k] += _sum_row_tiles(shifted * x)
        cb_acc[...] += _sum_row_tiles(dxc)

        @pl.when(i == 0)
        def _():
            dcw_ref[...] = jnp.sum(cw_acc[...], axis=1)
            dcb_ref[...] = jnp.sum(cb_acc[...], axis=0, keepdims=True)

        dext_ref[CHUNK:, :] = dext_ref[0:HALO, :]
        dzxd_ref[:, 0:SSM_WIDTH] = dz.astype(BF16)
        dzxd_ref[:, SSM_WIDTH:SSM_WIDTH + CONV_DIM] = dx.astype(BF16)
        dzxd_ref[:, SSM_WIDTH + CONV_DIM:] = jnp.concatenate(
            [ddtr, jnp.zeros((CHUNK, LANES - SSM_HEADS), F32)], axis=1).astype(BF16)

    return _tiled(body, "ssd_bwd", n_chunks,
                  [(proj, CHUNK, SSM_WIDTH, Z_BLK), (x16, CHUNK, CONV_DIM, 0), (xc, CHUNK, CONV_DIM, 0),
                   (proj, CHUNK, LANES, DT_BLK), (dyb, CHUNK, SSM_WIDTH, 0), (s_all, SSM_STATE, SSM_WIDTH, 0)],
                  [conv_w, dt_bias, a_log, d_skip, norm_g], [],
                  [(T, ZXD_W, BF16, CHUNK)],
                  [((SSM_CONV, CONV_DIM), F32), ((1, CONV_DIM), F32), ((1, SSM_HEADS), F32), ((1, SSM_HEADS), F32),
                   ((1, SSM_HEADS), F32), ((1, SSM_WIDTH), F32)],
                  scratch=[pltpu.VMEM((CHUNK + HALO, CONV_DIM), F32), pltpu.VMEM((SSM_STATE, SSM_WIDTH), F32),
                           pltpu.VMEM((SSM_CONV, F32_ROWS, CONV_DIM), F32), pltpu.VMEM((F32_ROWS, CONV_DIM), F32)],
                  reverse=True, comm=comm)


TAIL_TM = 512


def _tail(h, p, target, ple_norm, w_gate, b_gate, w_proj_t, final_norm):
    T = h.shape[0]

    def head(x, pre, pp, b_g, f_norm, tgt):
        gate = jax.nn.sigmoid(pre + b_g)
        out = _rms(x + gate * pp, f_norm)
        err = out - tgt
        return 0.5 * jnp.sum(jnp.mean(err * err, axis=-1, keepdims=True), axis=0, keepdims=True)

    def body(i, h_ref, p_ref, t_ref, pn_ref, bg_ref, fn_ref, wg_ref, wp_ref, dh_ref, loss_ref, dwg_ref, dwp_ref, dpn_ref,
             dbg_ref, dfn_ref):
        x = h_ref[...]
        n4f, n_vjp = jax.vjp(_rms, x, pn_ref[...])
        n4 = n4f.astype(BF16)
        pre = jnp.dot(n4, wg_ref[...], preferred_element_type=F32)
        p16 = p_ref[...].astype(BF16)
        pp = _dot_nt(p16, wp_ref[...])
        loss, h_vjp = jax.vjp(functools.partial(head, tgt=t_ref[...]), x, pre, pp, bg_ref[...], fn_ref[...])
        dx, dpre, dpp, dbg, dfn = h_vjp(jnp.ones((1, 1), F32))
        dpre16 = dpre.astype(BF16)
        dn4 = _dot_nt(dpre16, wg_ref[...])
        dx2, dpn = n_vjp(dn4)
        dh_ref[...] = dx + dx2
        loss_ref[...] += loss
        dwg_ref[...] += _dot_tn(n4, dpre16)
        dwp_ref[...] += _dot_tn(p16, dpp)
        dpn_ref[...] += dpn
        dbg_ref[...] += dbg
        dfn_ref[...] += dfn

    return _tiled(body, "tail", T // TAIL_TM,
                  [(h, TAIL_TM, D_MODEL, 0), (p, TAIL_TM, D_PLE, 0), (target, TAIL_TM, D_MODEL, 0)],
                  [ple_norm, b_gate, final_norm], [w_gate, w_proj_t],
                  [(T, D_MODEL, F32, TAIL_TM)],
                  [((1, 1), F32), ((D_MODEL, D_MODEL), F32), ((D_PLE, D_MODEL), F32), ((1, D_MODEL), F32),
                   ((1, D_MODEL), F32), ((1, D_MODEL), F32)])


def _gather_phases(x_ref, out_ref, send_sems, recv_sems, local_sem):
    mx, my, mc = lax.axis_index("x"), lax.axis_index("y"), lax.axis_index("c")
    me, sibling = (mx, my, mc), (mx, my, 1 - mc)
    chips = [(1 - mx, my), (mx, 1 - my), (1 - mx, 1 - my)]

    def rows(px, py, pc):
        return out_ref.at[4 * px + 2 * py + pc]

    def copy(k, block, to, src=None):
        return pltpu.make_async_remote_copy(
            src_ref=rows(*block) if src is None else src, dst_ref=rows(*block),
            send_sem=send_sems.at[k], recv_sem=recv_sems.at[k], device_id=to, device_id_type=MESH)

    mine = pltpu.make_async_copy(x_ref, rows(*me), local_sem)
    first = [copy(0, me, sibling, src=x_ref)] + [copy(1 + j, me, (*chip, mc), src=x_ref) for j, chip in enumerate(chips)]
    passed = [copy(4 + j, (*chip, mc), sibling) for j, chip in enumerate(chips)]

    def start():
        mine.start()
        for cp in first:
            cp.start()

    def mid():
        for j, chip in enumerate(chips):
            copy(1 + j, (*chip, mc), me).wait_recv()
            passed[j].start()

    def finish():
        copy(0, sibling, me).wait_recv()
        for j, chip in enumerate(chips):
            copy(4 + j, (*chip, 1 - mc), me).wait_recv()
        for cp in first + passed:
            cp.wait_send()
        mine.wait()

    return start, mid, finish


def _exchange_phases(x_ref, out_ref, send_sems, recv_sems, local_sem):
    mx, my, mc = lax.axis_index("x"), lax.axis_index("y"), lax.axis_index("c")
    me = 4 * mx + 2 * my + mc
    mine = pltpu.make_async_copy(x_ref.at[me], out_ref.at[me], local_sem)
    copies = []
    for k in range(1, N_DEV):
        px = 1 - mx if k & 4 else mx
        py = 1 - my if k & 2 else my
        pc = 1 - mc if k & 1 else mc
        copies.append(pltpu.make_async_remote_copy(
            src_ref=x_ref.at[4 * px + 2 * py + pc], dst_ref=out_ref.at[me], send_sem=send_sems.at[k - 1],
            recv_sem=recv_sems.at[k - 1], device_id=(px, py, pc), device_id_type=MESH))

    def start():
        mine.start()
        for cp in copies:
            cp.start()

    def finish():
        for cp in copies:
            cp.wait_recv()
        for cp in copies:
            cp.wait_send()
        mine.wait()

    return start, lambda: None, finish


def _gather_comm(x):
    return _Comm(_gather_phases, x, jax.ShapeDtypeStruct((N_DEV,) + x.shape, x.dtype))


def _exchange_comm(x):
    return _Comm(_exchange_phases, x, jax.ShapeDtypeStruct(x.shape, x.dtype))


def _comm_alone(comms, name):
    n = len(comms)

    def body(*refs):
        phases = [comm.phases(refs[k], refs[n + k], *refs[2 * n + 3 * k:2 * n + 3 * k + 3]) for k, comm in enumerate(comms)]
        for step in range(3):
            for phase in phases:
                phase[step]()

    any_spec = pl.BlockSpec(memory_space=pl.ANY)
    return pl.pallas_call(
        body,
        out_shape=[comm.dst for comm in comms],
        in_specs=[any_spec] * n,
        out_specs=[any_spec] * n,
        scratch_shapes=[pltpu.SemaphoreType.DMA((N_DEV - 1,)), pltpu.SemaphoreType.DMA((N_DEV - 1,)), pltpu.SemaphoreType.DMA] * n,
        name=name,
    )(*[comm.src for comm in comms])


def _sum_parts(p_ref):
    g = p_ref[0].astype(F32)
    for j in range(1, N_DEV):
        g = g + p_ref[j].astype(F32)
    return g


def _adamw_store(g, w_ref, m_ref, v_ref, g_ref, d_ref, nm_ref, nv_ref):
    m_new = ADAM_B1 * m_ref[...] + (1.0 - ADAM_B1) * g
    v_new = ADAM_B2 * v_ref[...] + (1.0 - ADAM_B2) * jnp.square(g)
    m_hat = m_new / (1.0 - ADAM_B1 ** ADAM_STEP)
    v_hat = v_new / (1.0 - ADAM_B2 ** ADAM_STEP)
    g_ref[...] = g
    d_ref[...] = -ADAM_LR * (m_hat / (jnp.sqrt(v_hat) + ADAM_EPS) + ADAM_WD * w_ref[...])
    nm_ref[...] = m_new
    nv_ref[...] = v_new


def _adamw_shard(parts, off, w, m, v, name, n_tiles):
    _, rows, c = w.shape
    assert c == PACK_COLS
    by_rows = rows % BF16_ROWS == 0
    if by_rows:
        tr = rows // n_tiles
        window = (N_DEV, tr, PACK_COLS)
        spec = pl.BlockSpec((None, tr, PACK_COLS), lambda i: (0, i, 0))
    else:
        padded, tc = -(-rows // BF16_ROWS) * BF16_ROWS, PACK_COLS // n_tiles
        window = (N_DEV, padded, tc)
        spec = pl.BlockSpec((None, rows, tc), lambda i: (0, 0, i))

    def kern(p_hbm, w_ref, m_ref, v_ref, g_ref, d_ref, nm_ref, nv_ref, buf, sem):
        i = pl.program_id(0)
        if by_rows:
            src = p_hbm.at[:, pl.ds(pl.multiple_of(off + i * tr, BF16_ROWS), tr), :]
        else:
            src = p_hbm.at[:, pl.ds(off, padded), pl.ds(pl.multiple_of(i * tc, LANES), tc)]
        cp = pltpu.make_async_copy(src, buf, sem)
        cp.start()
        cp.wait()
        g = _sum_parts(buf)
        if not by_rows:
            keep = lax.broadcasted_iota(jnp.int32, (rows, padded), 0) == lax.broadcasted_iota(jnp.int32, (rows, padded), 1)
            g = _exact_dot(g, keep.astype(BF16), ((1,), (0,)), x_first=False)
        _adamw_store(g, w_ref, m_ref, v_ref, g_ref, d_ref, nm_ref, nv_ref)

    return pl.pallas_call(
        kern,
        out_shape=[jax.ShapeDtypeStruct(w.shape, F32)] * 4,
        grid=(n_tiles,),
        in_specs=[pl.BlockSpec(memory_space=pl.ANY), spec, spec, spec],
        out_specs=[spec] * 4,
        scratch_shapes=[pltpu.VMEM(window, parts.dtype), pltpu.SemaphoreType.DMA],
        name=name,
        compiler_params=pltpu.CompilerParams(dimension_semantics=("arbitrary",), vmem_limit_bytes=VMEM_LIMIT),
    )(parts, w, m, v)


def _sum_adamw(parts, w, m, v, tr, name):
    _, R, C = parts.shape

    def kern(p_ref, w_ref, m_ref, v_ref, g_ref, d_ref, nm_ref, nv_ref):
        _adamw_store(_sum_parts(p_ref), w_ref, m_ref, v_ref, g_ref, d_ref, nm_ref, nv_ref)

    row_spec = pl.BlockSpec((tr, C), lambda i: (i, 0))
    return pl.pallas_call(
        kern,
        out_shape=[jax.ShapeDtypeStruct((R, C), F32)] * 4,
        grid=(R // tr,),
        in_specs=[pl.BlockSpec((N_DEV, tr, C), lambda i: (0, i, 0)), row_spec, row_spec, row_spec],
        out_specs=[row_spec] * 4,
        name=name,
        compiler_params=pltpu.CompilerParams(dimension_semantics=("arbitrary",), vmem_limit_bytes=VMEM_LIMIT),
    )(parts, w, m, v)


FF_SHARD = D_FF // N_DEV
CONV_SHARD = (SSM_CONV, CONV_DIM // N_DEV)
SHARDS = {"ffn1_w_gate": ((D_MODEL, FF_SHARD), True), "ffn1_w_up": ((D_MODEL, FF_SHARD), True),
          "ffn1_w_down": ((FF_SHARD, D_MODEL), False),
          "ffn2_w_gate": ((D_MODEL, FF_SHARD), True), "ffn2_w_up": ((D_MODEL, FF_SHARD), True),
          "ffn2_w_down": ((FF_SHARD, D_MODEL), False),
          "w_out": ((2 * D_MODEL // N_DEV, D_MODEL), False), "ple_w_gate": ((D_MODEL // N_DEV, D_MODEL), False),
          "w_in": ((D_MODEL, IN_PROJ // N_DEV), True), "ple_w_proj": ((D_PLE, D_MODEL // N_DEV), True),
          "conv_w": (CONV_SHARD, True),
          "conv_w_mid": (CONV_SHARD, True), "conv_w_low": (CONV_SHARD, True)}
BIG = tuple(name for name in SHARDS if not name.startswith("conv_w_"))
SMALL = ("ffn1_norm", "mix_norm", "gm_ln_g", "gm_ln_b", "gm_w_s", "gm_b_s", "gm_out_norm", "conv_b", "dt_bias", "a_log",
         "d_skip", "ssm_norm", "ffn2_norm", "ple_norm", "ple_b_gate", "final_norm")
SMALL_ROWS = 144


def _piece_rows(name):
    shape = SHARDS[name][0]
    return -(-(shape[0] * shape[1]) // PACK_COLS)


def _pad_cols(flat, name):
    pad = _piece_rows(name) * PACK_COLS - flat.shape[-1]
    return flat if pad == 0 else jnp.pad(flat, [(0, 0)] * (flat.ndim - 1) + [(0, pad)])


class _Pack:
    def __init__(self, names, tile_rows):
        self.names, self.tile_rows, self.offsets, off = names, tile_rows, {}, 0
        for name in names:
            self.offsets[name] = off
            off += _piece_rows(name)
        self.rows = -(-off // tile_rows) * tile_rows

    def pack_local(self, vals):
        parts = []
        for name in self.names:
            val = vals[name]
            parts.append(_pad_cols((val.T if SHARDS[name][1] else val).reshape(-1), name))
        flat = jnp.concatenate(parts)
        return jnp.pad(flat, (0, self.rows * PACK_COLS - flat.shape[0])).reshape(self.rows, PACK_COLS)

    def pack_owner_major(self, grads):
        parts, rows = [], 0
        for name in self.names:
            grad, piece_rows = grads[name].astype(BF16), _piece_rows(name)
            if grad.shape != (N_DEV * piece_rows, PACK_COLS):
                grad = _pad_cols(grad.reshape(N_DEV, -1), name)
            parts.append(grad.reshape(N_DEV, piece_rows, PACK_COLS))
            rows += piece_rows
        if rows < self.rows:
            parts.append(jnp.zeros((N_DEV, self.rows - rows, PACK_COLS), BF16))
        return parts[0] if len(parts) == 1 else jnp.concatenate(parts, axis=1)

    def gathered_piece(self, gathered, name):
        shape = SHARDS[name][0]
        rows = gathered[:, self.offsets[name]:self.offsets[name] + _piece_rows(name), :]
        return rows.reshape(N_DEV, -1)[:, :shape[0] * shape[1]]

    def pieces(self, gathered, name):
        return _Pieces(gathered, self.offsets[name], _piece_rows(name))


GATHER_FFN1 = _Pack(("ffn1_w_gate", "ffn1_w_up", "ffn1_w_down"), BF16_ROWS)
GATHER_MIX = _Pack(("w_out", "ple_w_gate", "w_in", "ple_w_proj", "conv_w", "conv_w_mid", "conv_w_low"), BF16_ROWS)
GATHER_FFN2 = _Pack(("ffn2_w_gate", "ffn2_w_up", "ffn2_w_down"), BF16_ROWS)
SCATTER_LATE = _Pack(("ffn2_w_gate", "ffn2_w_up", "ffn2_w_down", "w_out", "ple_w_gate", "ple_w_proj"), BF16_ROWS)
SCATTER_IN = _Pack(("w_in", "conv_w"), BF16_ROWS)
SCATTER_GATE = _Pack(("ffn1_w_gate",), BF16_ROWS)
SCATTER_UP = _Pack(("ffn1_w_up",), BF16_ROWS)
SCATTER_DOWN = _Pack(("ffn1_w_down",), BF16_ROWS)


def _pack_small(vals, behind=()):
    flat = jnp.concatenate([vals[name].reshape(-1).astype(F32) for name in SMALL] + [b.reshape(-1) for b in behind])
    return jnp.pad(flat, (0, SMALL_ROWS * PACK_COLS - flat.shape[0])).reshape(SMALL_ROWS, PACK_COLS)


def _unpack_small(packed, shapes):
    out, off = {}, 0
    flat = packed.reshape(-1)
    for name in SMALL:
        n = 1
        for s in shapes[name]:
            n *= s
        out[name] = flat[off:off + n].reshape(shapes[name])
        off += n
    return out


WEIGHTS = ("ffn1_norm", "ffn1_w_gate", "ffn1_w_up", "ffn1_w_down", "mix_norm", "w_in", "gm_ln_g", "gm_ln_b", "gm_w_s",
           "gm_b_s", "gm_out_norm", "conv_w", "conv_b", "dt_bias", "a_log", "d_skip", "ssm_norm", "w_out", "ffn2_norm",
           "ffn2_w_gate", "ffn2_w_up", "ffn2_w_down", "ple_norm", "ple_w_gate", "ple_b_gate", "ple_w_proj", "final_norm")


def _step(x, p, target, w, m, v):
    local = lambda d: {name: d[name][0] for name in BIG}

    shards = {name: val.astype(BF16) for name, val in local(w).items()}
    conv_high = lax.reduce_precision(w["conv_w"][0], 8, 7)
    conv_mid = lax.reduce_precision(w["conv_w"][0] - conv_high, 8, 7)
    shards["conv_w"] = conv_high.astype(BF16)
    shards["conv_w_mid"] = conv_mid.astype(BF16)
    shards["conv_w_low"] = (w["conv_w"][0] - conv_high - conv_mid).astype(BF16)
    g_ffn1 = _comm_alone([_gather_comm(GATHER_FFN1.pack_local(shards))], "gather_ffn1")[0]

    row = lambda name: w[name].reshape(1, -1)
    gm_w_s = w["gm_w_s"][0]
    gm_b_st = jnp.transpose(w["gm_b_s"][0])
    ffn1 = (row("ffn1_norm"),) + tuple(GATHER_FFN1.pieces(g_ffn1, name) for name in GATHER_FFN1.names)
    gm = (row("gm_ln_g"), row("gm_ln_b"), gm_w_s, gm_b_st, row("gm_out_norm"))

    h1, n1, a1, b1, s1, g_mix = _ffn_fwd(x, *ffn1, "ffn1_fwd", comm=_gather_comm(GATHER_MIX.pack_local(shards)))
    w_in_t = GATHER_MIX.gathered_piece(g_mix, "w_in").reshape(IN_PROJ, D_MODEL)
    w_in_t = jnp.concatenate([w_in_t, jnp.zeros((IN_PROJ_PAD - IN_PROJ, D_MODEL), BF16)], axis=0)
    w_proj_t = GATHER_MIX.gathered_piece(g_mix, "ple_w_proj").reshape(D_MODEL, D_PLE)
    conv_w = sum(GATHER_MIX.gathered_piece(g_mix, name).astype(F32) for name in ("conv_w", "conv_w_mid", "conv_w_low"))
    conv_w = conv_w.reshape(CONV_DIM, SSM_CONV).T
    ssd = (row("dt_bias"), row("a_log"), row("d_skip"), row("ssm_norm"))
    w_out = GATHER_MIX.pieces(g_mix, "w_out")

    proj, n2, x16, xc = _mix_in_fwd(h1, row("mix_norm"), w_in_t, conv_w, row("conv_b"))
    ya = _gm_fwd(proj, *gm)
    yb, s_all, g_ffn2 = _ssd_fwd(proj, xc, *ssd, comm=_gather_comm(GATHER_FFN2.pack_local(shards)))
    ffn2 = (row("ffn2_norm"),) + tuple(GATHER_FFN2.pieces(g_ffn2, name) for name in GATHER_FFN2.names)
    h3, n3, a3, b3, s3, h2 = _ffn_fwd(h1, *ffn2, "ffn2_fwd", mixed=(ya, yb, w_out))

    g, gp = {}, {}
    dh3, loss, gp["ple_w_gate"], d_w_proj, g["ple_norm"], g["ple_b_gate"], g["final_norm"] = _tail(
        h3, p, target, row("ple_norm"), GATHER_MIX.pieces(g_mix, "ple_w_gate"), row("ple_b_gate"), w_proj_t,
        row("final_norm"))
    gp["ple_w_proj"] = d_w_proj.T

    dh2, da3, db3, g["ffn2_norm"] = _ffn_dgrad(h2, dh3, a3, b3, *ffn2, "ffn2_dgrad")
    gp["ffn2_w_gate"] = _wgrad(n3, da3, FF_BN, "ffn2_wgrad_gate", transpose_out=True)
    gp["ffn2_w_up"] = _wgrad(n3, db3, FF_BN, "ffn2_wgrad_up", transpose_out=True)
    gp["ffn2_w_down"] = _wgrad(s3, dh3, DOWN_BN, "ffn2_wgrad_down", scale=0.5, bk=DOWN_BK)

    dya, dyb = _out_proj_dgrad(dh2, w_out)
    gp["w_out"] = jnp.concatenate([_wgrad(ya, dh2, SQUARE_BN, "w_out_wgrad_a"), _wgrad(yb, dh2, SQUARE_BN, "w_out_wgrad_b")], axis=0)

    dp_zxd, d_conv_w, g["conv_b"], g["dt_bias"], g["a_log"], g["d_skip"], g["ssm_norm"], parts_late = _ssd_bwd(
        proj, x16, xc, dyb, s_all, conv_w, *ssd, comm=_exchange_comm(SCATTER_LATE.pack_owner_major(gp)))
    gp["conv_w"] = d_conv_w.T
    dp_uv, g["gm_ln_g"], g["gm_ln_b"], g["gm_w_s"], dbst, g["gm_out_norm"] = _gm_bwd(proj, dya, *gm)
    g["gm_b_s"] = jnp.transpose(dbst)

    parts = {}
    gp["w_in"] = jnp.concatenate([_wgrad(n2, dp_uv, SQUARE_BN, "w_in_wgrad_uv", transpose_out=True),
                                  _wgrad(n2, dp_zxd, ZXD_BN, "w_in_wgrad_zxd", transpose_out=True)], axis=0)[:IN_PROJ]
    dh1, g["mix_norm"], parts[SCATTER_IN] = _mix_in_dgrad(h1, dh2, dp_uv, dp_zxd, row("mix_norm"), w_in_t,
                                                          comm=_exchange_comm(SCATTER_IN.pack_owner_major(gp)))

    dx, da1, db1, g["ffn1_norm"] = _ffn_dgrad(x, dh1, a1, b1, *ffn1, "ffn1_dgrad")
    gp["ffn1_w_gate"], small_parts = _wgrad(n1, da1, FF_BN, "ffn1_wgrad_gate", transpose_out=True,
                                            comm=_gather_comm(_pack_small(g, behind=[loss])))
    gp["ffn1_w_up"], parts[SCATTER_GATE] = _wgrad(n1, db1, FF_BN, "ffn1_wgrad_up", transpose_out=True,
                                                  comm=_exchange_comm(SCATTER_GATE.pack_owner_major(gp)))
    gp["ffn1_w_down"], parts[SCATTER_UP] = _wgrad(s1, dh1, DOWN_BN, "ffn1_wgrad_down", scale=0.5, bk=DOWN_BK,
                                                  comm=_exchange_comm(SCATTER_UP.pack_owner_major(gp)))
    parts[SCATTER_DOWN] = _comm_alone([_exchange_comm(SCATTER_DOWN.pack_owner_major(gp))], "scatter_ffn1_down")[0]
    parts[SCATTER_LATE] = parts_late

    res_big = {}
    for pack, pack_parts in parts.items():
        for name in pack.names:
            shape, transposed = SHARDS[name]
            if name in ("ple_w_proj", "conv_w"):
                nat = pack.gathered_piece(pack_parts, name).reshape((N_DEV,) + shape[::-1])
                res_big[name] = _sum_adamw(jnp.transpose(nat, (0, 2, 1)), w[name][0], m[name][0], v[name][0], shape[0],
                                           "adamw_" + name)
            else:
                flip = (lambda a: jnp.transpose(a, (0, 2, 1))) if transposed else (lambda a: a)
                res = _adamw_shard(pack_parts, pack.offsets[name], flip(w[name]), flip(m[name]), flip(v[name]),
                                   "adamw_" + name, n_tiles=4 if name == "w_in" else 2)
                res_big[name] = [flip(r) for r in res]

    small_shapes = {name: w[name].shape for name in SMALL}
    res_small = _sum_adamw(small_parts, _pack_small(w), _pack_small(m), _pack_small(v), SMALL_ROWS, "adamw_small")
    loss = res_small[0].reshape(-1)[sum(w[name].size for name in SMALL)]
    res_small = [_unpack_small(r, small_shapes) for r in res_small]

    outs = []
    for k in range(4):
        for name in WEIGHTS:
            if name in res_small[k]:
                outs.append(res_small[k][name])
            else:
                outs.append(res_big[name][k].reshape(w[name].shape))
    return loss, dx, outs


def kernel(x, p, ffn1_norm, ffn1_w_gate, ffn1_w_up, ffn1_w_down, mix_norm, w_in, gm_ln_g, gm_ln_b, gm_w_s, gm_b_s, gm_out_norm, conv_w, conv_b, dt_bias, a_log, d_skip, ssm_norm, w_out, ffn2_norm, ffn2_w_gate, ffn2_w_up, ffn2_w_down, ple_norm, ple_w_gate, ple_b_gate, ple_w_proj, final_norm, loss_target, m_ffn1_norm, m_ffn1_w_gate, m_ffn1_w_up, m_ffn1_w_down, m_mix_norm, m_w_in, m_gm_ln_g, m_gm_ln_b, m_gm_w_s, m_gm_b_s, m_gm_out_norm, m_conv_w, m_conv_b, m_dt_bias, m_a_log, m_d_skip, m_ssm_norm, m_w_out, m_ffn2_norm, m_ffn2_w_gate, m_ffn2_w_up, m_ffn2_w_down, m_ple_norm, m_ple_w_gate, m_ple_b_gate, m_ple_w_proj, m_final_norm, v_ffn1_norm, v_ffn1_w_gate, v_ffn1_w_up, v_ffn1_w_down, v_mix_norm, v_w_in, v_gm_ln_g, v_gm_ln_b, v_gm_w_s, v_gm_b_s, v_gm_out_norm, v_conv_w, v_conv_b, v_dt_bias, v_a_log, v_d_skip, v_ssm_norm, v_w_out, v_ffn2_norm, v_ffn2_w_gate, v_ffn2_w_up, v_ffn2_w_down, v_ple_norm, v_ple_w_gate, v_ple_b_gate, v_ple_w_proj, v_final_norm):
    args = locals()
    w = {name: args[name] for name in WEIGHTS}
    m = {name: args["m_" + name] for name in WEIGHTS}
    v = {name: args["v_" + name] for name in WEIGHTS}
    loss, dx, outs = _step(x[0], p[0, 0], loss_target[0], w, m, v)
    return (loss, dx[None], *outs)
```

```python
import functools
from typing import NamedTuple

import jax
import jax.numpy as jnp
from jax import lax
from jax.experimental import pallas as pl
from jax.experimental.pallas import tpu as pltpu

F32 = jnp.float32
BF16 = jnp.bfloat16
MESH = pl.DeviceIdType.MESH
N_DEV = 8

D_MODEL = 1024
D_FF = 2816
D_PLE = 256
GM_WIDTH = 1024
GM_HEADS = 8
GM_HEAD_DIM = 128
CHUNK = 128
SSM_WIDTH = 1024
SSM_HEADS = 16
SSM_HEAD_DIM = 64
SSM_GROUPS = 2
SSM_STATE = 128
SSM_CONV = 4
CONV_DIM = SSM_WIDTH + 2 * SSM_GROUPS * SSM_STATE
IN_PROJ = 2 * GM_WIDTH + SSM_WIDTH + CONV_DIM + SSM_HEADS
LANES = 128
BF16_ROWS = 16
F32_ROWS = 8
IN_PROJ_PAD = IN_PROJ - SSM_HEADS + LANES
UV_W = 2 * GM_WIDTH
ZXD_W = IN_PROJ_PAD - UV_W
HALO = 8
EPS = 1e-6

ADAM_LR = 0.001
ADAM_B1 = 0.9
ADAM_B2 = 0.999
ADAM_EPS = 1e-08
ADAM_WD = 0.01
ADAM_STEP = 10

VMEM_LIMIT = 56 * 1024 * 1024
PACK_COLS = 1024


def _rms(x, g):
    return x * lax.rsqrt(jnp.mean(x * x, axis=-1, keepdims=True) + EPS) * g


def _gelu(x):
    return 0.5 * x * (1.0 + lax.erf(x * (2.0 ** -0.5)))


def _silu(x):
    return x * jax.nn.sigmoid(x)


def _dot(a, b):
    return jnp.dot(a.astype(BF16), b.astype(BF16), preferred_element_type=F32)


def _dot_nt(a, b):
    return lax.dot_general(a.astype(BF16), b.astype(BF16), (((1,), (1,)), ((), ())), preferred_element_type=F32)


def _dot_tn(a, b):
    return lax.dot_general(a.astype(BF16), b.astype(BF16), (((0,), (0,)), ((), ())), preferred_element_type=F32)


def _split3(x):
    hi = x.astype(BF16)
    rest = x - hi.astype(F32)
    mid = rest.astype(BF16)
    return hi, mid, (rest - mid.astype(F32)).astype(BF16)


def _exact_dot(x, mask, dims, x_first=True, n_terms=3):
    terms = [lax.dot_general(*((t, mask) if x_first else (mask, t)), (dims, ((), ())), preferred_element_type=F32)
             for t in _split3(x)[:n_terms]]
    total = terms[0]
    for term in terms[1:]:
        total = total + term
    return total


def _mask_product(fwd_dims, fwd_x_first, bwd_dims, bwd_x_first, bwd_terms=3):
    @jax.custom_vjp
    def product(x, mask):
        return _exact_dot(x, mask, fwd_dims, fwd_x_first)

    def fwd(x, mask):
        return product(x, mask), mask

    def bwd(mask, g):
        return _exact_dot(g, mask, bwd_dims, bwd_x_first, bwd_terms), jnp.zeros_like(mask)

    product.defvjp(fwd, bwd)
    return product


_widen = _mask_product(((1,), (0,)), True, ((1,), (1,)), True, bwd_terms=2)
_cumsum_rows = _mask_product(((1,), (0,)), False, ((0,), (0,)), False)
_cumsum_cols = _mask_product(((0,), (0,)), True, ((1,), (1,)), False)


class _Pieces(NamedTuple):
    gathered: jax.Array
    row_off: int
    rows: int


class _Comm(NamedTuple):
    phases: object
    src: jax.Array
    dst: jax.ShapeDtypeStruct


def _tiled(body, name, n_steps, tiled_in, full_in, big_in, tiled_out, acc_out, scratch=(), reverse=False, comm=None):
    n_t, n_f, n_b, n_to, n_a = len(tiled_in), len(full_in), len(big_in), len(tiled_out), len(acc_out)
    n_c = 1 if comm else 0

    def row(i):
        return n_steps - 1 - i if reverse else i

    in_specs, args = [], []
    for arr, br, bc, cb in tiled_in:
        if callable(cb):
            in_specs.append(pl.BlockSpec((br, bc), cb))
        else:
            in_specs.append(pl.BlockSpec((br, bc), functools.partial(lambda i, cb: (row(i), cb), cb=cb)))
        args.append(arr)
    for arr in full_in:
        in_specs.append(pl.BlockSpec(arr.shape, functools.partial(lambda i, nd: (0,) * nd, nd=arr.ndim)))
        args.append(arr)
    big_shapes, n_copies = [], 0
    for big in big_in:
        in_specs.append(pl.BlockSpec(memory_space=pl.ANY))
        if isinstance(big, _Pieces):
            args.append(big.gathered)
            big_shapes.append(((N_DEV * big.rows, PACK_COLS), big.gathered.dtype))
            n_copies += N_DEV
        else:
            args.append(big)
            big_shapes.append((big.shape, big.dtype))
            n_copies += 1
    if comm:
        in_specs.append(pl.BlockSpec(memory_space=pl.ANY))
        args.append(comm.src)
    out_specs, out_shape = [], []
    for rows, cols, dt, br in tiled_out:
        out_specs.append(pl.BlockSpec((br, cols), lambda i: (row(i), 0)))
        out_shape.append(jax.ShapeDtypeStruct((rows, cols), dt))
    for shp, dt in acc_out:
        out_specs.append(pl.BlockSpec(shp, functools.partial(lambda i, nd: (0,) * nd, nd=len(shp))))
        out_shape.append(jax.ShapeDtypeStruct(shp, dt))
    if comm:
        out_specs.append(pl.BlockSpec(memory_space=pl.ANY))
        out_shape.append(comm.dst)
    scratch_shapes = [pltpu.VMEM(shp, dt) for shp, dt in big_shapes] + list(scratch)
    if n_copies:
        scratch_shapes.append(pltpu.SemaphoreType.DMA((n_copies,)))
    if comm:
        scratch_shapes += [pltpu.SemaphoreType.DMA((N_DEV - 1,)), pltpu.SemaphoreType.DMA((N_DEV - 1,)), pltpu.SemaphoreType.DMA]

    def kern(*refs):
        n_in = n_t + n_f + n_b + n_c
        ins = refs[: n_t + n_f]
        big_hbm = refs[n_t + n_f : n_t + n_f + n_b]
        outs = refs[n_in : n_in + n_to + n_a]
        rest = refs[n_in + n_to + n_a + n_c :]
        big_vmem, scr = rest[:n_b], rest[n_b:]
        if comm:
            scr, comm_sems = scr[:-3], scr[-3:]
            comm_start, comm_mid, comm_finish = comm.phases(refs[n_in - 1], refs[n_in + n_to + n_a], *comm_sems)
        if n_copies:
            scr, copy_sems = scr[:-1], scr[-1]
        step = pl.program_id(0)

        @pl.when(step == 0)
        def _():
            copies = []
            for big, src, dst in zip(big_in, big_hbm, big_vmem):
                if isinstance(big, _Pieces):
                    for j in range(N_DEV):
                        copies.append((src.at[j, pl.ds(big.row_off, big.rows), :], dst.at[pl.ds(j * big.rows, big.rows), :]))
                else:
                    copies.append((src, dst))
            copies = [pltpu.make_async_copy(a, b, copy_sems.at[k]) for k, (a, b) in enumerate(copies)]
            for cp in copies:
                cp.start()
            for cp in copies:
                cp.wait()
            for acc in outs[n_to:]:
                acc[...] = jnp.zeros(acc.shape, acc.dtype)
            if comm:
                comm_start()

        body(row(step), *ins, *big_vmem, *outs, *scr)
        if comm:
            pl.when(step == (n_steps - 1) // 2)(comm_mid)
            pl.when(step == n_steps - 1)(comm_finish)

    res = pl.pallas_call(
        kern,
        out_shape=out_shape,
        grid=(n_steps,),
        in_specs=in_specs,
        out_specs=out_specs,
        scratch_shapes=scratch_shapes,
        name=name,
        compiler_params=pltpu.CompilerParams(dimension_semantics=("arbitrary",), vmem_limit_bytes=VMEM_LIMIT),
    )(*args)
    return res


FWD_CHUNKS = ((0, 1536), (1536, D_FF))
DGRAD_CHUNKS = ((0, 1024), (1024, 2048), (2048, D_FF))
FFN_TM = 256


def _ffn_fwd(h, g, wg_t, wu_t, wd, name, comm=None, mixed=None):
    T = h.shape[0]

    def ffn(x, g_ref, wg_ref, wu_ref, wd_ref, o_ref, n_ref, a_ref, b_ref, s_ref):
        n = _rms(x, g_ref[...]).astype(BF16)
        n_ref[...] = n
        f = jnp.zeros(x.shape, F32)
        for lo, hi in FWD_CHUNKS:
            a = _dot_nt(n, wg_ref[lo:hi, :])
            b = _dot_nt(n, wu_ref[lo:hi, :])
            s = (_silu(a) * b).astype(BF16)
            a_ref[:, lo:hi] = a.astype(BF16)
            b_ref[:, lo:hi] = b.astype(BF16)
            s_ref[:, lo:hi] = s
            f = f + jnp.dot(s, wd_ref[lo:hi, :], preferred_element_type=F32)
        o_ref[...] = x + 0.5 * f

    def body_plain(i, h_ref, *refs):
        ffn(h_ref[...], *refs)

    def body_mixed(i, h_ref, ya_ref, yb_ref, g_ref, wg_ref, wu_ref, wd_ref, wo_ref, o_ref, n_ref, a_ref, b_ref, s_ref, x_ref):
        x = (h_ref[...] + jnp.dot(ya_ref[...], wo_ref[:GM_WIDTH, :], preferred_element_type=F32)
             + jnp.dot(yb_ref[...], wo_ref[GM_WIDTH:, :], preferred_element_type=F32))
        x_ref[...] = x
        ffn(x, g_ref, wg_ref, wu_ref, wd_ref, o_ref, n_ref, a_ref, b_ref, s_ref)

    body = body_mixed if mixed else body_plain
    tiled_in, big_in = [(h, FFN_TM, D_MODEL, 0)], [wg_t, wu_t, wd]
    tiled_out = [(T, D_MODEL, F32, FFN_TM), (T, D_MODEL, BF16, FFN_TM), (T, D_FF, BF16, FFN_TM), (T, D_FF, BF16, FFN_TM),
                 (T, D_FF, BF16, FFN_TM)]
    if mixed:
        tiled_in += [(mixed[0], FFN_TM, GM_WIDTH, 0), (mixed[1], FFN_TM, SSM_WIDTH, 0)]
        big_in.append(mixed[2])
        tiled_out.append((T, D_MODEL, F32, FFN_TM))
    return _tiled(body, name, T // FFN_TM, tiled_in, [g], big_in, tiled_out, [], comm=comm)


def _ffn_dgrad(h, dout, a16, b16, g, wg_t, wu_t, wd, name):
    T = h.shape[0]

    def body(i, h_ref, do_ref, a_ref, b_ref, g_ref, wg_ref, wu_ref, wd_ref, dh_ref, da_ref, db_ref, dg_ref):
        dout = do_ref[...]
        _, rms_vjp = jax.vjp(_rms, h_ref[...], g_ref[...])
        dfo = (0.5 * dout).astype(BF16)
        dn = jnp.zeros(dout.shape, F32)
        for lo, hi in DGRAD_CHUNKS:
            a = a_ref[:, lo:hi].astype(F32)
            b = b_ref[:, lo:hi].astype(F32)
            sg = jax.nn.sigmoid(a)
            ds = _dot_nt(dfo, wd_ref[lo:hi, :])
            db = (ds * (a * sg)).astype(BF16)
            da = (ds * b * (sg * (1.0 + a * (1.0 - sg)))).astype(BF16)
            dn = dn + _dot(da, wg_ref[lo:hi, :]) + _dot(db, wu_ref[lo:hi, :])
            da_ref[:, lo:hi] = da
            db_ref[:, lo:hi] = db
        dx, dg = rms_vjp(dn)
        dh_ref[...] = dout + dx
        dg_ref[...] += dg

    return _tiled(body, name, T // FFN_TM,
                  [(h, FFN_TM, D_MODEL, 0), (dout, FFN_TM, D_MODEL, 0), (a16, FFN_TM, D_FF, 0), (b16, FFN_TM, D_FF, 0)],
                  [g], [wg_t, wu_t, wd],
                  [(T, D_MODEL, F32, FFN_TM), (T, D_FF, BF16, FFN_TM), (T, D_FF, BF16, FFN_TM)], [((1, D_MODEL), F32)])


FF_BN = D_FF // 2
DOWN_BN, DOWN_BK = 512, 1024
SQUARE_BN = 1024
ZXD_BN = ZXD_W // 3


def _wgrad(a, b, bn, name, scale=None, transpose_out=False, bk=2048, comm=None, b_cols=None):
    T, M = a.shape
    b_first, N = b_cols if b_cols else (0, b.shape[1])
    bk = min(bk, T)
    assert M % LANES == 0 and N % bn == 0 and b_first % bn == 0 and T % bk == 0
    n_j, n_k = N // bn, T // bk
    j_first = b_first // bn
    n_c = 1 if comm else 0

    def kern(*refs):
        a_ref, b_ref, o_ref, acc_ref = refs[0], refs[1], refs[2 + n_c], refs[3 + 2 * n_c]
        j, k = pl.program_id(0), pl.program_id(1)
        if comm:
            comm_start, comm_mid, comm_finish = comm.phases(refs[2], refs[4], *refs[6:])
            pl.when((j == 0) & (k == 0))(comm_start)

        @pl.when(k == 0)
        def _():
            acc_ref[...] = jnp.zeros(acc_ref.shape, F32)

        bv = b_ref[...]
        if scale is not None:
            bv = bv * scale
        acc_ref[...] += _dot_tn(a_ref[...], bv)

        @pl.when(k == n_k - 1)
        def _():
            acc = acc_ref[...]
            o_ref[...] = (acc.T if transpose_out else acc).astype(BF16)

        if comm:
            pl.when((j == (n_j - 1) // 2) & (k == n_k - 1))(comm_mid)
            pl.when((j == n_j - 1) & (k == n_k - 1))(comm_finish)

    if transpose_out:
        out_shape, out_spec = (N, M), pl.BlockSpec((bn, M), lambda j, k: (j, 0))
    else:
        out_shape, out_spec = (M, N), pl.BlockSpec((M, bn), lambda j, k: (0, j))
    any_spec = pl.BlockSpec(memory_space=pl.ANY)
    comm_sems = [pltpu.SemaphoreType.DMA((N_DEV - 1,)), pltpu.SemaphoreType.DMA((N_DEV - 1,)), pltpu.SemaphoreType.DMA]
    res = pl.pallas_call(
        kern,
        out_shape=[jax.ShapeDtypeStruct(out_shape, BF16)] + ([comm.dst] if comm else []),
        grid=(n_j, n_k),
        in_specs=[pl.BlockSpec((bk, M), lambda j, k: (k, 0)),
                  pl.BlockSpec((bk, bn), lambda j, k: (k, j_first + j))] + [any_spec] * n_c,
        out_specs=[out_spec] + [any_spec] * n_c,
        scratch_shapes=[pltpu.VMEM((M, bn), F32)] + (comm_sems if comm else []),
        name=name,
        compiler_params=pltpu.CompilerParams(dimension_semantics=("arbitrary", "arbitrary"), vmem_limit_bytes=VMEM_LIMIT),
    )(a, b, *([comm.src] if comm else []))
    return res if comm else res[0]


PROJ_TM = 512
PROJ_DGRAD_TM = 256
UVZ_W = 2 * GM_WIDTH + SSM_WIDTH
PROJ_KEPT = UVZ_W + LANES
Z_BLK = 2 * GM_WIDTH // SSM_WIDTH
DT_BLK = UVZ_W // LANES


def _mix_in_fwd(h, g, w_in_t, conv_w, conv_b):
    T = h.shape[0]

    def body(i, h_ref, g_ref, cw_ref, cb_ref, w_ref, p_ref, n_ref, x_ref, xc_ref, ext_ref):
        @pl.when(i == 0)
        def _():
            ext_ref[0:HALO, :] = jnp.zeros((HALO, CONV_DIM), F32)

        n = _rms(h_ref[...], g_ref[...]).astype(BF16)
        n_ref[...] = n
        proj = _dot_nt(n, w_ref[...])
        p_ref[:, :UVZ_W] = proj[:, :UVZ_W]
        p_ref[:, UVZ_W:] = proj[:, UVZ_W + CONV_DIM:]
        xbc = proj[:, UVZ_W:UVZ_W + CONV_DIM]
        x_ref[...] = xbc.astype(BF16)
        ext_ref[HALO:, :] = xbc
        xc_ref[...] = _conv_taps(ext_ref, cw_ref[...], cb_ref[...], PROJ_TM)
        ext_ref[0:HALO, :] = ext_ref[PROJ_TM:PROJ_TM + HALO, :]

    return _tiled(body, "mix_in_fwd", T // PROJ_TM, [(h, PROJ_TM, D_MODEL, 0)], [g, conv_w, conv_b], [w_in_t],
                  [(T, PROJ_KEPT, F32, PROJ_TM), (T, D_MODEL, BF16, PROJ_TM), (T, CONV_DIM, BF16, PROJ_TM),
                   (T, CONV_DIM, F32, PROJ_TM)], [],
                  scratch=[pltpu.VMEM((HALO + PROJ_TM, CONV_DIM), F32)])


def _mix_in_dgrad(h, dh_in, dp_uv, dp_zxd, g, w_in_t, comm=None):
    T = h.shape[0]

    def body(i, h_ref, dh_ref, duv_ref, dzxd_ref, g_ref, w_ref, o_ref, dg_ref):
        dn = _dot(duv_ref[...], w_ref[:UV_W, :]) + _dot(dzxd_ref[...], w_ref[UV_W:, :])
        _, rms_vjp = jax.vjp(_rms, h_ref[...], g_ref[...])
        dx, dg = rms_vjp(dn)
        o_ref[...] = dh_ref[...] + dx
        dg_ref[...] += dg

    return _tiled(body, "mix_in_dgrad", T // PROJ_DGRAD_TM,
                  [(h, PROJ_DGRAD_TM, D_MODEL, 0), (dh_in, PROJ_DGRAD_TM, D_MODEL, 0), (dp_uv, PROJ_DGRAD_TM, UV_W, 0),
                   (dp_zxd, PROJ_DGRAD_TM, ZXD_W, 0)], [g], [w_in_t],
                  [(T, D_MODEL, F32, PROJ_DGRAD_TM)], [((1, D_MODEL), F32)], comm=comm)


def _out_proj_dgrad(dh, w_out):
    T = dh.shape[0]

    def body(i, dh_ref, w_ref, dya_ref, dyb_ref):
        d = dh_ref[...].astype(BF16)
        dya_ref[...] = _dot_nt(d, w_ref[:GM_WIDTH, :])
        dyb_ref[...] = _dot_nt(d, w_ref[GM_WIDTH:, :])

    return _tiled(body, "out_proj_dgrad", T // PROJ_TM, [(dh, PROJ_TM, D_MODEL, 0)], [], [w_out],
                  [(T, GM_WIDTH, F32, PROJ_TM), (T, SSM_WIDTH, F32, PROJ_TM)], [])


def _gm_chunk(u, v, ln_g, ln_b, b_st, out_g, *w_heads):
    ug = _gelu(u)
    vg = _gelu(v)
    mu = jnp.mean(vg, axis=-1, keepdims=True)
    xc = vg - mu
    vn = xc * lax.rsqrt(jnp.mean(xc * xc, axis=-1, keepdims=True) + EPS) * ln_g + ln_b
    t_idx = lax.broadcasted_iota(jnp.int32, (CHUNK, CHUNK), 0)
    s_idx = lax.broadcasted_iota(jnp.int32, (CHUNK, CHUNK), 1)
    causal = t_idx >= s_idx
    mixed = []
    for hd in range(GM_HEADS):
        wm = jnp.where(causal, w_heads[hd], 0.0)
        cols = slice(hd * GM_HEAD_DIM, (hd + 1) * GM_HEAD_DIM)
        mixed.append(_dot(wm, vn[:, cols]) + b_st[:, hd:hd + 1])
    ya0 = ug * jnp.concatenate(mixed, axis=1)
    return _rms(ya0, out_g)


GM_FWD_CHUNKS = 2


def _gm_fwd(proj, ln_g, ln_b, w_s, b_st, out_g):
    T = proj.shape[0]

    rows = GM_FWD_CHUNKS * CHUNK

    def body(i, u_ref, v_ref, lg_ref, lb_ref, w_ref, bs_ref, og_ref, ya_ref):
        w_heads = [w_ref[hd] for hd in range(GM_HEADS)]
        for c in range(GM_FWD_CHUNKS):
            tok = pl.ds(c * CHUNK, CHUNK)
            ya = _gm_chunk(u_ref[tok, :], v_ref[tok, :], lg_ref[...], lb_ref[...], bs_ref[...], og_ref[...], *w_heads)
            ya_ref[tok, :] = ya.astype(BF16)

    return _tiled(body, "gmlp_fwd", T // rows, [(proj, rows, GM_WIDTH, 0), (proj, rows, GM_WIDTH, 1)],
                  [ln_g, ln_b, w_s, b_st, out_g], [], [(T, GM_WIDTH, BF16, rows)], [])[0]


def _gm_bwd(proj, dya, ln_g, ln_b, w_s, b_st, out_g):
    T = proj.shape[0]

    def body(i, u_ref, v_ref, dy_ref, lg_ref, lb_ref, w_ref, bs_ref, og_ref, duv_ref, dlg_ref, dlb_ref, dw_ref, dbs_ref,
             dog_ref):
        w_heads = [w_ref[hd] for hd in range(GM_HEADS)]
        _, vjp = jax.vjp(_gm_chunk, u_ref[...], v_ref[...], lg_ref[...], lb_ref[...], bs_ref[...], og_ref[...], *w_heads)
        grads = vjp(dy_ref[...])
        duv_ref[:, :GM_WIDTH] = grads[0].astype(BF16)
        duv_ref[:, GM_WIDTH:] = grads[1].astype(BF16)
        dlg_ref[...] += grads[2]
        dlb_ref[...] += grads[3]
        dbs_ref[...] += grads[4]
        dog_ref[...] += grads[5]
        for hd in range(GM_HEADS):
            dw_ref[hd] += grads[6 + hd]

    return _tiled(body, "gmlp_bwd", T // CHUNK,
                  [(proj, CHUNK, GM_WIDTH, 0), (proj, CHUNK, GM_WIDTH, 1), (dya, CHUNK, GM_WIDTH, 0)],
                  [ln_g, ln_b, w_s, b_st, out_g], [], [(T, UV_W, BF16, CHUNK)],
                  [((1, GM_WIDTH), F32), ((1, GM_WIDTH), F32), ((GM_HEADS, CHUNK, CHUNK), F32),
                   ((CHUNK, GM_HEADS), F32), ((1, GM_WIDTH), F32)])


def _ssd_chunk(xc, z, dtr, s_in, dt_bias, a_log, d_skip, norm_g):
    half = SSM_WIDTH // SSM_GROUPS
    l_idx = lax.broadcasted_iota(jnp.int32, (CHUNK, CHUNK), 0)
    s_idx = lax.broadcasted_iota(jnp.int32, (CHUNK, CHUNK), 1)
    causal = l_idx >= s_idx
    head_of_col = lax.broadcasted_iota(jnp.int32, (SSM_HEADS, SSM_WIDTH), 1) // SSM_HEAD_DIM
    expand = (head_of_col == lax.broadcasted_iota(jnp.int32, (SSM_HEADS, SSM_WIDTH), 0)).astype(BF16)

    xcs = _silu(xc)
    xs = xcs[:, :SSM_WIDTH]
    dt = jax.nn.softplus(dtr + dt_bias)
    adt = dt * (-jnp.exp(a_log))
    acs = _cumsum_rows(adt, causal.astype(BF16))
    acs_t = _cumsum_cols(adt, (l_idx <= s_idx).astype(BF16))
    tot = acs[CHUNK - 1:CHUNK, :]
    dt_w = _widen(dt, expand)
    out_decay_w = _widen(jnp.exp(acs), expand)
    state_decay_w = _widen(jnp.exp(tot - acs), expand)
    chunk_decay_w = _widen(jnp.exp(tot), expand)
    d_skip_w = _widen(d_skip, expand)
    xdt = xs * dt_w
    xdt_decayed = xdt * state_decay_w

    y_diag, y_off, states = [], [], []
    for grp in range(SSM_GROUPS):
        b0 = SSM_WIDTH + grp * SSM_STATE
        c0 = SSM_WIDTH + SSM_GROUPS * SSM_STATE + grp * SSM_STATE
        bm = xcs[:, b0:b0 + SSM_STATE].astype(BF16)
        cm = xcs[:, c0:c0 + SSM_STATE].astype(BF16)
        cb = _dot_nt(cm, bm)
        for k in range(grp * SSM_HEADS // SSM_GROUPS, (grp + 1) * SSM_HEADS // SSM_GROUPS):
            decay = jnp.exp(jnp.where(causal, acs[:, k:k + 1] - acs_t[k:k + 1, :], -jnp.inf))
            y_diag.append(_dot(cb * decay, xdt[:, k * SSM_HEAD_DIM:(k + 1) * SSM_HEAD_DIM]))
        cols = slice(grp * half, (grp + 1) * half)
        states.append(_dot_tn(bm, xdt_decayed[:, cols]))
        y_off.append(_dot(cm, s_in[:, cols]))
    y = jnp.concatenate(y_diag, axis=1) + jnp.concatenate(y_off, axis=1) * out_decay_w + xs * d_skip_w
    s_out = s_in * chunk_decay_w + jnp.concatenate(states, axis=1)
    y = y * _silu(z)
    normed = []
    for grp in range(SSM_GROUPS):
        yg = y[:, grp * half:(grp + 1) * half]
        normed.append(yg * lax.rsqrt(jnp.mean(yg * yg, axis=-1, keepdims=True) + EPS))
    return jnp.concatenate(normed, axis=1) * norm_g, s_out


def _sum_row_tiles(x):
    return x.reshape(x.shape[0] // F32_ROWS, F32_ROWS, x.shape[1]).sum(axis=0)


def _conv_taps(ext_ref, w, b, rows):
    y = b
    for k in range(SSM_CONV):
        y = y + w[k:k + 1, :] * ext_ref[pl.ds(HALO - (SSM_CONV - 1) + k, rows), :]
    return y


def _ssd_fwd(proj, xc, dt_bias, a_log, d_skip, norm_g, comm=None):
    T = proj.shape[0]
    n_chunks = T // CHUNK

    def body(i, z_ref, xc_ref, dt_ref, dtb_ref, al_ref, dsk_ref, ng_ref, yb_ref, sin_ref, st_ref):
        @pl.when(i == 0)
        def _():
            st_ref[...] = jnp.zeros(st_ref.shape, F32)

        s_in = st_ref[...]
        yb, s_out = _ssd_chunk(xc_ref[...], z_ref[...], dt_ref[:, 0:SSM_HEADS], s_in, dtb_ref[...], al_ref[...],
                               dsk_ref[...], ng_ref[...])
        yb_ref[...] = yb.astype(BF16)
        sin_ref[...] = s_in
        st_ref[...] = s_out

    return _tiled(body, "ssd_fwd", n_chunks,
                  [(proj, CHUNK, SSM_WIDTH, Z_BLK), (xc, CHUNK, CONV_DIM, 0), (proj, CHUNK, LANES, DT_BLK)],
                  [dt_bias, a_log, d_skip, norm_g], [],
                  [(T, SSM_WIDTH, BF16, CHUNK), (n_chunks * SSM_STATE, SSM_WIDTH, F32, SSM_STATE)], [],
                  scratch=[pltpu.VMEM((SSM_STATE, SSM_WIDTH), F32)], comm=comm)


def _ssd_bwd(proj, x16, xc, dyb, s_all, conv_w, dt_bias, a_log, d_skip, norm_g, comm=None):
    T = proj.shape[0]
    n_chunks = T // CHUNK

    def body(i, z_ref, x_ref, xc_ref, dt_ref, dy_ref, sin_ref, cw_ref, dtb_ref, al_ref, dsk_ref, ng_ref,
             dzxd_ref, dcw_ref, dcb_ref, ddtb_ref, dal_ref, ddsk_ref, dng_ref, dext_ref, dst_ref, cw_acc, cb_acc):
        @pl.when(i == n_chunks - 1)
        def _():
            dext_ref[CHUNK:, :] = jnp.zeros((HALO, CONV_DIM), F32)
            dst_ref[...] = jnp.zeros(dst_ref.shape, F32)
            cw_acc[...] = jnp.zeros(cw_acc.shape, F32)
            cb_acc[...] = jnp.zeros(cb_acc.shape, F32)

        _, vjp = jax.vjp(_ssd_chunk, xc_ref[...], z_ref[...], dt_ref[:, 0:SSM_HEADS], sin_ref[...], dtb_ref[...], al_ref[...],
                         dsk_ref[...], ng_ref[...])
        dxc, dz, ddtr, ds_in, ddtb, dal, ddsk, dng = vjp((dy_ref[...], dst_ref[...]))
        dst_ref[...] = ds_in
        ddtb_ref[...] += ddtb
        dal_ref[...] += dal
        ddsk_ref[...] += ddsk
        dng_ref[...] += dng
        dext_ref[0:CHUNK, :] = dxc
        cw = cw_ref[...]
        x = x_ref[...].astype(F32)
        dx = jnp.zeros((CHUNK, CONV_DIM), F32)
        for k in range(SSM_CONV):
            shifted = dext_ref[pl.ds(SSM_CONV - 1 - k, CHUNK), :]
            dx = dx + cw[k:k + 1, :] * shifted
            cw_acc[k] += _sum_row_tiles(shifted * x)
        cb_acc[...] += _sum_row_tiles(dxc)

        @pl.when(i == 0)
        def _():
            dcw_ref[...] = jnp.sum(cw_acc[...], axis=1)
            dcb_ref[...] = jnp.sum(cb_acc[...], axis=0, keepdims=True)

        dext_ref[CHUNK:, :] = dext_ref[0:HALO, :]
        dzxd_ref[:, 0:SSM_WIDTH] = dz.astype(BF16)
        dzxd_ref[:, SSM_WIDTH:SSM_WIDTH + CONV_DIM] = dx.astype(BF16)
        dzxd_ref[:, SSM_WIDTH + CONV_DIM:] = jnp.concatenate(
            [ddtr, jnp.zeros((CHUNK, LANES - SSM_HEADS), F32)], axis=1).astype(BF16)

    return _tiled(body, "ssd_bwd", n_chunks,
                  [(proj, CHUNK, SSM_WIDTH, Z_BLK), (x16, CHUNK, CONV_DIM, 0), (xc, CHUNK, CONV_DIM, 0),
                   (proj, CHUNK, LANES, DT_BLK), (dyb, CHUNK, SSM_WIDTH, 0), (s_all, SSM_STATE, SSM_WIDTH, 0)],
                  [conv_w, dt_bias, a_log, d_skip, norm_g], [],
                  [(T, ZXD_W, BF16, CHUNK)],
                  [((SSM_CONV, CONV_DIM), F32), ((1, CONV_DIM), F32), ((1, SSM_HEADS), F32), ((1, SSM_HEADS), F32),
                   ((1, SSM_HEADS), F32), ((1, SSM_WIDTH), F32)],
                  scratch=[pltpu.VMEM((CHUNK + HALO, CONV_DIM), F32), pltpu.VMEM((SSM_STATE, SSM_WIDTH), F32),
                           pltpu.VMEM((SSM_CONV, F32_ROWS, CONV_DIM), F32), pltpu.VMEM((F32_ROWS, CONV_DIM), F32)],
                  reverse=True, comm=comm)


TAIL_TM = 512


def _tail(h, p, target, ple_norm, w_gate, b_gate, w_proj_t, final_norm):
    T = h.shape[0]

    def head(x, pre, pp, b_g, f_norm, tgt):
        gate = jax.nn.sigmoid(pre + b_g)
        out = _rms(x + gate * pp, f_norm)
        err = out - tgt
        return 0.5 * jnp.sum(jnp.mean(err * err, axis=-1, keepdims=True), axis=0, keepdims=True)

    def body(i, h_ref, p_ref, t_ref, pn_ref, bg_ref, fn_ref, wg_ref, wp_ref, dh_ref, loss_ref, dwg_ref, dwp_ref, dpn_ref,
             dbg_ref, dfn_ref):
        x = h_ref[...]
        n4f, n_vjp = jax.vjp(_rms, x, pn_ref[...])
        n4 = n4f.astype(BF16)
        pre = jnp.dot(n4, wg_ref[...], preferred_element_type=F32)
        p16 = p_ref[...].astype(BF16)
        pp = _dot_nt(p16, wp_ref[...])
        loss, h_vjp = jax.vjp(functools.partial(head, tgt=t_ref[...]), x, pre, pp, bg_ref[...], fn_ref[...])
        dx, dpre, dpp, dbg, dfn = h_vjp(jnp.ones((1, 1), F32))
        dpre16 = dpre.astype(BF16)
        dn4 = _dot_nt(dpre16, wg_ref[...])
        dx2, dpn = n_vjp(dn4)
        dh_ref[...] = dx + dx2
        loss_ref[...] += loss
        dwg_ref[...] += _dot_tn(n4, dpre16)
        dwp_ref[...] += _dot_tn(p16, dpp)
        dpn_ref[...] += dpn
        dbg_ref[...] += dbg
        dfn_ref[...] += dfn

    return _tiled(body, "tail", T // TAIL_TM,
                  [(h, TAIL_TM, D_MODEL, 0), (p, TAIL_TM, D_PLE, 0), (target, TAIL_TM, D_MODEL, 0)],
                  [ple_norm, b_gate, final_norm], [w_gate, w_proj_t],
                  [(T, D_MODEL, F32, TAIL_TM)],
                  [((1, 1), F32), ((D_MODEL, D_MODEL), F32), ((D_PLE, D_MODEL), F32), ((1, D_MODEL), F32),
                   ((1, D_MODEL), F32), ((1, D_MODEL), F32)])


def _gather_phases(x_ref, out_ref, send_sems, recv_sems, local_sem):
    mx, my, mc = lax.axis_index("x"), lax.axis_index("y"), lax.axis_index("c")
    me, sibling = (mx, my, mc), (mx, my, 1 - mc)
    chips = [(1 - mx, my), (mx, 1 - my), (1 - mx, 1 - my)]

    def rows(px, py, pc):
        return out_ref.at[4 * px + 2 * py + pc]

    def copy(k, block, to, src=None):
        return pltpu.make_async_remote_copy(
            src_ref=rows(*block) if src is None else src, dst_ref=rows(*block),
            send_sem=send_sems.at[k], recv_sem=recv_sems.at[k], device_id=to, device_id_type=MESH)

    mine = pltpu.make_async_copy(x_ref, rows(*me), local_sem)
    first = [copy(0, me, sibling, src=x_ref)] + [copy(1 + j, me, (*chip, mc), src=x_ref) for j, chip in enumerate(chips)]
    passed = [copy(4 + j, (*chip, mc), sibling) for j, chip in enumerate(chips)]

    def start():
        mine.start()
        for cp in first:
            cp.start()

    def mid():
        for j, chip in enumerate(chips):
            copy(1 + j, (*chip, mc), me).wait_recv()
            passed[j].start()

    def finish():
        copy(0, sibling, me).wait_recv()
        for j, chip in enumerate(chips):
            copy(4 + j, (*chip, 1 - mc), me).wait_recv()
        for cp in first + passed:
            cp.wait_send()
        mine.wait()

    return start, mid, finish


def _exchange_phases(x_ref, out_ref, send_sems, recv_sems, local_sem):
    mx, my, mc = lax.axis_index("x"), lax.axis_index("y"), lax.axis_index("c")
    me = 4 * mx + 2 * my + mc
    mine = pltpu.make_async_copy(x_ref.at[me], out_ref.at[me], local_sem)
    copies = []
    for k in range(1, N_DEV):
        px = 1 - mx if k & 4 else mx
        py = 1 - my if k & 2 else my
        pc = 1 - mc if k & 1 else mc
        copies.append(pltpu.make_async_remote_copy(
            src_ref=x_ref.at[4 * px + 2 * py + pc], dst_ref=out_ref.at[me], send_sem=send_sems.at[k - 1],
            recv_sem=recv_sems.at[k - 1], device_id=(px, py, pc), device_id_type=MESH))

    def start():
        mine.start()
        for cp in copies:
            cp.start()

    def finish():
        for cp in copies:
            cp.wait_recv()
        for cp in copies:
            cp.wait_send()
        mine.wait()

    return start, lambda: None, finish


def _gather_comm(x):
    return _Comm(_gather_phases, x, jax.ShapeDtypeStruct((N_DEV,) + x.shape, x.dtype))


def _exchange_comm(x):
    return _Comm(_exchange_phases, x, jax.ShapeDtypeStruct(x.shape, x.dtype))


def _comm_alone(comms, name):
    n = len(comms)

    def body(*refs):
        phases = [comm.phases(refs[k], refs[n + k], *refs[2 * n + 3 * k:2 * n + 3 * k + 3]) for k, comm in enumerate(comms)]
        for step in range(3):
            for phase in phases:
                phase[step]()

    any_spec = pl.BlockSpec(memory_space=pl.ANY)
    return pl.pallas_call(
        body,
        out_shape=[comm.dst for comm in comms],
        in_specs=[any_spec] * n,
        out_specs=[any_spec] * n,
        scratch_shapes=[pltpu.SemaphoreType.DMA((N_DEV - 1,)), pltpu.SemaphoreType.DMA((N_DEV - 1,)), pltpu.SemaphoreType.DMA] * n,
        name=name,
    )(*[comm.src for comm in comms])


def _sum_parts(p_ref):
    g = p_ref[0].astype(F32)
    for j in range(1, N_DEV):
        g = g + p_ref[j].astype(F32)
    return g


def _adamw_store(g, w_ref, m_ref, v_ref, g_ref, d_ref, nm_ref, nv_ref):
    m_new = ADAM_B1 * m_ref[...] + (1.0 - ADAM_B1) * g
    v_new = ADAM_B2 * v_ref[...] + (1.0 - ADAM_B2) * jnp.square(g)
    m_hat = m_new / (1.0 - ADAM_B1 ** ADAM_STEP)
    v_hat = v_new / (1.0 - ADAM_B2 ** ADAM_STEP)
    g_ref[...] = g
    d_ref[...] = -ADAM_LR * (m_hat / (jnp.sqrt(v_hat) + ADAM_EPS) + ADAM_WD * w_ref[...])
    nm_ref[...] = m_new
    nv_ref[...] = v_new


def _adamw_shard(parts, off, w, m, v, name, n_tiles):
    parts = parts if isinstance(parts, (tuple, list)) else (parts,)
    n_p = len(parts)
    _, rows, c = w.shape
    assert c == PACK_COLS == sum(part.shape[2] for part in parts)
    by_rows = rows % BF16_ROWS == 0
    if by_rows:
        tr = rows // n_tiles
        window = (N_DEV, tr, PACK_COLS)
        spec = pl.BlockSpec((None, tr, PACK_COLS), lambda i: (0, i, 0))
    else:
        assert n_p == 1
        padded, tc = -(-rows // BF16_ROWS) * BF16_ROWS, PACK_COLS // n_tiles
        window = (N_DEV, padded, tc)
        spec = pl.BlockSpec((None, rows, tc), lambda i: (0, 0, i))

    def kern(*refs):
        p_hbm, (w_ref, m_ref, v_ref, g_ref, d_ref, nm_ref, nv_ref, buf, sems) = refs[:n_p], refs[n_p:]
        i = pl.program_id(0)
        copies, col = [], 0
        for k, part in enumerate(p_hbm):
            width = part.shape[2]
            if by_rows:
                src = part.at[:, pl.ds(pl.multiple_of(off + i * tr, BF16_ROWS), tr), :]
                dst = buf.at[:, :, pl.ds(col, width)]
            else:
                src = part.at[:, pl.ds(off, padded), pl.ds(pl.multiple_of(i * tc, LANES), tc)]
                dst = buf
            copies.append(pltpu.make_async_copy(src, dst, sems.at[k]))
            col += width
        for cp in copies:
            cp.start()
        for cp in copies:
            cp.wait()
        g = _sum_parts(buf)
        if not by_rows:
            keep = lax.broadcasted_iota(jnp.int32, (rows, padded), 0) == lax.broadcasted_iota(jnp.int32, (rows, padded), 1)
            g = _exact_dot(g, keep.astype(BF16), ((1,), (0,)), x_first=False)
        _adamw_store(g, w_ref, m_ref, v_ref, g_ref, d_ref, nm_ref, nv_ref)

    return pl.pallas_call(
        kern,
        out_shape=[jax.ShapeDtypeStruct(w.shape, F32)] * 4,
        grid=(n_tiles,),
        in_specs=[pl.BlockSpec(memory_space=pl.ANY)] * n_p + [spec, spec, spec],
        out_specs=[spec] * 4,
        scratch_shapes=[pltpu.VMEM(window, parts[0].dtype), pltpu.SemaphoreType.DMA((n_p,))],
        name=name,
        compiler_params=pltpu.CompilerParams(dimension_semantics=("arbitrary",), vmem_limit_bytes=VMEM_LIMIT),
    )(*parts, w, m, v)


def _sum_adamw(parts, w, m, v, tr, name):
    _, R, C = parts.shape

    def kern(p_ref, w_ref, m_ref, v_ref, g_ref, d_ref, nm_ref, nv_ref):
        _adamw_store(_sum_parts(p_ref), w_ref, m_ref, v_ref, g_ref, d_ref, nm_ref, nv_ref)

    row_spec = pl.BlockSpec((tr, C), lambda i: (i, 0))
    return pl.pallas_call(
        kern,
        out_shape=[jax.ShapeDtypeStruct((R, C), F32)] * 4,
        grid=(R // tr,),
        in_specs=[pl.BlockSpec((N_DEV, tr, C), lambda i: (0, i, 0)), row_spec, row_spec, row_spec],
        out_specs=[row_spec] * 4,
        name=name,
        compiler_params=pltpu.CompilerParams(dimension_semantics=("arbitrary",), vmem_limit_bytes=VMEM_LIMIT),
    )(parts, w, m, v)


FF_SHARD = D_FF // N_DEV
CONV_SHARD = (SSM_CONV, CONV_DIM // N_DEV)
SHARDS = {"ffn1_w_gate": ((D_MODEL, FF_SHARD), True), "ffn1_w_up": ((D_MODEL, FF_SHARD), True),
          "ffn1_w_down": ((FF_SHARD, D_MODEL), False),
          "ffn2_w_gate": ((D_MODEL, FF_SHARD), True), "ffn2_w_up": ((D_MODEL, FF_SHARD), True),
          "ffn2_w_down": ((FF_SHARD, D_MODEL), False),
          "w_out": ((2 * D_MODEL // N_DEV, D_MODEL), False), "ple_w_gate": ((D_MODEL // N_DEV, D_MODEL), False),
          "w_in": ((D_MODEL, IN_PROJ // N_DEV), True), "ple_w_proj": ((D_PLE, D_MODEL // N_DEV), True),
          "conv_w": (CONV_SHARD, True),
          "conv_w_mid": (CONV_SHARD, True), "conv_w_low": (CONV_SHARD, True)}
BIG = tuple(name for name in SHARDS if not name.startswith("conv_w_"))
SMALL = ("ffn1_norm", "mix_norm", "gm_ln_g", "gm_ln_b", "gm_w_s", "gm_b_s", "gm_out_norm", "conv_b", "dt_bias", "a_log",
         "d_skip", "ssm_norm", "ffn2_norm", "ple_norm", "ple_b_gate", "final_norm")
SMALL_ROWS = 144


def _piece_rows(name):
    shape = SHARDS[name][0]
    return -(-(shape[0] * shape[1]) // PACK_COLS)


def _pad_cols(flat, name):
    pad = _piece_rows(name) * PACK_COLS - flat.shape[-1]
    return flat if pad == 0 else jnp.pad(flat, [(0, 0)] * (flat.ndim - 1) + [(0, pad)])


class _Pack:
    def __init__(self, names, tile_rows):
        self.names, self.tile_rows, self.offsets, off = names, tile_rows, {}, 0
        for name in names:
            self.offsets[name] = off
            off += _piece_rows(name)
        self.rows = -(-off // tile_rows) * tile_rows

    def pack_local(self, vals):
        parts = []
        for name in self.names:
            val = vals[name]
            parts.append(_pad_cols((val.T if SHARDS[name][1] else val).reshape(-1), name))
        flat = jnp.concatenate(parts)
        return jnp.pad(flat, (0, self.rows * PACK_COLS - flat.shape[0])).reshape(self.rows, PACK_COLS)

    def pack_owner_major(self, grads):
        parts, rows = [], 0
        for name in self.names:
            grad, piece_rows = grads[name].astype(BF16), _piece_rows(name)
            if grad.shape != (N_DEV * piece_rows, PACK_COLS):
                grad = _pad_cols(grad.reshape(N_DEV, -1), name)
            parts.append(grad.reshape(N_DEV, piece_rows, PACK_COLS))
            rows += piece_rows
        if rows < self.rows:
            parts.append(jnp.zeros((N_DEV, self.rows - rows, PACK_COLS), BF16))
        return parts[0] if len(parts) == 1 else jnp.concatenate(parts, axis=1)

    def gathered_piece(self, gathered, name):
        shape = SHARDS[name][0]
        rows = gathered[:, self.offsets[name]:self.offsets[name] + _piece_rows(name), :]
        return rows.reshape(N_DEV, -1)[:, :shape[0] * shape[1]]

    def pieces(self, gathered, name):
        return _Pieces(gathered, self.offsets[name], _piece_rows(name))


GATHER_FFN1 = _Pack(("ffn1_w_gate", "ffn1_w_up", "ffn1_w_down"), BF16_ROWS)
GATHER_MIX = _Pack(("w_out", "ple_w_gate", "w_in", "ple_w_proj", "conv_w", "conv_w_mid", "conv_w_low"), BF16_ROWS)
GATHER_FFN2 = _Pack(("ffn2_w_gate", "ffn2_w_up", "ffn2_w_down"), BF16_ROWS)
SCATTER_LATE = _Pack(("ffn2_w_gate", "ffn2_w_up", "ffn2_w_down", "w_out", "ple_w_gate", "ple_w_proj"), BF16_ROWS)
SCATTER_IN = _Pack(("w_in", "conv_w"), BF16_ROWS)
SCATTER_GATE = _Pack(("ffn1_w_gate",), BF16_ROWS)
SCATTER_UP = _Pack(("ffn1_w_up",), BF16_ROWS)
SCATTER_DOWN = _Pack(("ffn1_w_down",), BF16_ROWS)


def _pack_small(vals, behind=()):
    flat = jnp.concatenate([vals[name].reshape(-1).astype(F32) for name in SMALL] + [b.reshape(-1) for b in behind])
    return jnp.pad(flat, (0, SMALL_ROWS * PACK_COLS - flat.shape[0])).reshape(SMALL_ROWS, PACK_COLS)


def _unpack_small(packed, shapes):
    out, off = {}, 0
    flat = packed.reshape(-1)
    for name in SMALL:
        n = 1
        for s in shapes[name]:
            n *= s
        out[name] = flat[off:off + n].reshape(shapes[name])
        off += n
    return out


WEIGHTS = ("ffn1_norm", "ffn1_w_gate", "ffn1_w_up", "ffn1_w_down", "mix_norm", "w_in", "gm_ln_g", "gm_ln_b", "gm_w_s",
           "gm_b_s", "gm_out_norm", "conv_w", "conv_b", "dt_bias", "a_log", "d_skip", "ssm_norm", "w_out", "ffn2_norm",
           "ffn2_w_gate", "ffn2_w_up", "ffn2_w_down", "ple_norm", "ple_w_gate", "ple_b_gate", "ple_w_proj", "final_norm")


def _step(x, p, target, w, m, v):
    local = lambda d: {name: d[name][0] for name in BIG}

    shards = {name: val.astype(BF16) for name, val in local(w).items()}
    conv_high = lax.reduce_precision(w["conv_w"][0], 8, 7)
    conv_mid = lax.reduce_precision(w["conv_w"][0] - conv_high, 8, 7)
    shards["conv_w"] = conv_high.astype(BF16)
    shards["conv_w_mid"] = conv_mid.astype(BF16)
    shards["conv_w_low"] = (w["conv_w"][0] - conv_high - conv_mid).astype(BF16)
    g_ffn1 = _comm_alone([_gather_comm(GATHER_FFN1.pack_local(shards))], "gather_ffn1")[0]

    row = lambda name: w[name].reshape(1, -1)
    gm_w_s = w["gm_w_s"][0]
    gm_b_st = jnp.transpose(w["gm_b_s"][0])
    ffn1 = (row("ffn1_norm"),) + tuple(GATHER_FFN1.pieces(g_ffn1, name) for name in GATHER_FFN1.names)
    gm = (row("gm_ln_g"), row("gm_ln_b"), gm_w_s, gm_b_st, row("gm_out_norm"))

    h1, n1, a1, b1, s1, g_mix = _ffn_fwd(x, *ffn1, "ffn1_fwd", comm=_gather_comm(GATHER_MIX.pack_local(shards)))
    w_in_t = GATHER_MIX.gathered_piece(g_mix, "w_in").reshape(IN_PROJ, D_MODEL)
    w_in_t = jnp.concatenate([w_in_t, jnp.zeros((IN_PROJ_PAD - IN_PROJ, D_MODEL), BF16)], axis=0)
    w_proj_t = GATHER_MIX.gathered_piece(g_mix, "ple_w_proj").reshape(D_MODEL, D_PLE)
    conv_w = sum(GATHER_MIX.gathered_piece(g_mix, name).astype(F32) for name in ("conv_w", "conv_w_mid", "conv_w_low"))
    conv_w = conv_w.reshape(CONV_DIM, SSM_CONV).T
    ssd = (row("dt_bias"), row("a_log"), row("d_skip"), row("ssm_norm"))
    w_out = GATHER_MIX.pieces(g_mix, "w_out")

    proj, n2, x16, xc = _mix_in_fwd(h1, row("mix_norm"), w_in_t, conv_w, row("conv_b"))
    ya = _gm_fwd(proj, *gm)
    yb, s_all, g_ffn2 = _ssd_fwd(proj, xc, *ssd, comm=_gather_comm(GATHER_FFN2.pack_local(shards)))
    ffn2 = (row("ffn2_norm"),) + tuple(GATHER_FFN2.pieces(g_ffn2, name) for name in GATHER_FFN2.names)
    h3, n3, a3, b3, s3, h2 = _ffn_fwd(h1, *ffn2, "ffn2_fwd", mixed=(ya, yb, w_out))

    g, gp = {}, {}
    dh3, loss, gp["ple_w_gate"], d_w_proj, g["ple_norm"], g["ple_b_gate"], g["final_norm"] = _tail(
        h3, p, target, row("ple_norm"), GATHER_MIX.pieces(g_mix, "ple_w_gate"), row("ple_b_gate"), w_proj_t,
        row("final_norm"))
    gp["ple_w_proj"] = d_w_proj.T

    dh2, da3, db3, g["ffn2_norm"] = _ffn_dgrad(h2, dh3, a3, b3, *ffn2, "ffn2_dgrad")
    gp["ffn2_w_gate"] = _wgrad(n3, da3, FF_BN, "ffn2_wgrad_gate", transpose_out=True)
    gp["ffn2_w_up"] = _wgrad(n3, db3, FF_BN, "ffn2_wgrad_up", transpose_out=True)
    gp["ffn2_w_down"] = _wgrad(s3, dh3, DOWN_BN, "ffn2_wgrad_down", scale=0.5, bk=DOWN_BK)

    dya, dyb = _out_proj_dgrad(dh2, w_out)
    gp["w_out"] = jnp.concatenate([_wgrad(ya, dh2, SQUARE_BN, "w_out_wgrad_a"), _wgrad(yb, dh2, SQUARE_BN, "w_out_wgrad_b")], axis=0)

    dp_zxd, d_conv_w, g["conv_b"], g["dt_bias"], g["a_log"], g["d_skip"], g["ssm_norm"], parts_late = _ssd_bwd(
        proj, x16, xc, dyb, s_all, conv_w, *ssd, comm=_exchange_comm(SCATTER_LATE.pack_owner_major(gp)))
    gp["conv_w"] = d_conv_w.T
    dp_uv, g["gm_ln_g"], g["gm_ln_b"], g["gm_w_s"], dbst, g["gm_out_norm"] = _gm_bwd(proj, dya, *gm)
    g["gm_b_s"] = jnp.transpose(dbst)

    parts = {}
    gp["w_in"] = jnp.concatenate([_wgrad(n2, dp_uv, SQUARE_BN, "w_in_wgrad_uv", transpose_out=True),
                                  _wgrad(n2, dp_zxd, ZXD_BN, "w_in_wgrad_zxd", transpose_out=True)], axis=0)[:IN_PROJ]
    dh1, g["mix_norm"], parts[SCATTER_IN] = _mix_in_dgrad(h1, dh2, dp_uv, dp_zxd, row("mix_norm"), w_in_t,
                                                          comm=_exchange_comm(SCATTER_IN.pack_owner_major(gp)))

    dx, da1, db1, g["ffn1_norm"] = _ffn_dgrad(x, dh1, a1, b1, *ffn1, "ffn1_dgrad")
    gp["ffn1_w_gate"], small_parts = _wgrad(n1, da1, FF_BN, "ffn1_wgrad_gate", transpose_out=True,
                                            comm=_gather_comm(_pack_small(g, behind=[loss])))
    gp["ffn1_w_up"], parts[SCATTER_GATE] = _wgrad(n1, db1, FF_BN, "ffn1_wgrad_up", transpose_out=True,
                                                  comm=_exchange_comm(SCATTER_GATE.pack_owner_major(gp)))
    halves, half_parts, comm = [], [], _exchange_comm(SCATTER_UP.pack_owner_major(gp))
    for k in range(2):
        half, done = _wgrad(s1, dh1, DOWN_BN, "ffn1_wgrad_down_%d" % k, scale=0.5, bk=DOWN_BK, comm=comm,
                            b_cols=(k * DOWN_BN, DOWN_BN))
        halves.append(half)
        half_parts.append(done)
        comm = _exchange_comm(half.reshape(N_DEV, FF_SHARD, DOWN_BN))
    parts[SCATTER_UP] = half_parts[0]
    parts[SCATTER_DOWN] = (half_parts[1], _comm_alone([comm], "scatter_ffn1_down_1")[0])
    parts[SCATTER_LATE] = parts_late

    res_big = {}
    for pack, pack_parts in parts.items():
        for name in pack.names:
            shape, transposed = SHARDS[name]
            if name in ("ple_w_proj", "conv_w"):
                nat = pack.gathered_piece(pack_parts, name).reshape((N_DEV,) + shape[::-1])
                res_big[name] = _sum_adamw(jnp.transpose(nat, (0, 2, 1)), w[name][0], m[name][0], v[name][0], shape[0],
                                           "adamw_" + name)
            else:
                flip = (lambda a: jnp.transpose(a, (0, 2, 1))) if transposed else (lambda a: a)
                res = _adamw_shard(pack_parts, pack.offsets[name], flip(w[name]), flip(m[name]), flip(v[name]),
                                   "adamw_" + name, n_tiles=4 if name == "w_in" else 2)
                res_big[name] = [flip(r) for r in res]

    small_shapes = {name: w[name].shape for name in SMALL}
    res_small = _sum_adamw(small_parts, _pack_small(w), _pack_small(m), _pack_small(v), SMALL_ROWS, "adamw_small")
    loss = res_small[0].reshape(-1)[sum(w[name].size for name in SMALL)]
    res_small = [_unpack_small(r, small_shapes) for r in res_small]

    outs = []
    for k in range(4):
        for name in WEIGHTS:
            if name in res_small[k]:
                outs.append(res_small[k][name])
            else:
                outs.append(res_big[name][k].reshape(w[name].shape))
    return loss, dx, outs


def kernel(x, p, ffn1_norm, ffn1_w_gate, ffn1_w_up, ffn1_w_down, mix_norm, w_in, gm_ln_g, gm_ln_b, gm_w_s, gm_b_s, gm_out_norm, conv_w, conv_b, dt_bias, a_log, d_skip, ssm_norm, w_out, ffn2_norm, ffn2_w_gate, ffn2_w_up, ffn2_w_down, ple_norm, ple_w_gate, ple_b_gate, ple_w_proj, final_norm, loss_target, m_ffn1_norm, m_ffn1_w_gate, m_ffn1_w_up, m_ffn1_w_down, m_mix_norm, m_w_in, m_gm_ln_g, m_gm_ln_b, m_gm_w_s, m_gm_b_s, m_gm_out_norm, m_conv_w, m_conv_b, m_dt_bias, m_a_log, m_d_skip, m_ssm_norm, m_w_out, m_ffn2_norm, m_ffn2_w_gate, m_ffn2_w_up, m_ffn2_w_down, m_ple_norm, m_ple_w_gate, m_ple_b_gate, m_ple_w_proj, m_final_norm, v_ffn1_norm, v_ffn1_w_gate, v_ffn1_w_up, v_ffn1_w_down, v_mix_norm, v_w_in, v_gm_ln_g, v_gm_ln_b, v_gm_w_s, v_gm_b_s, v_gm_out_norm, v_conv_w, v_conv_b, v_dt_bias, v_a_log, v_d_skip, v_ssm_norm, v_w_out, v_ffn2_norm, v_ffn2_w_gate, v_ffn2_w_up, v_ffn2_w_down, v_ple_norm, v_ple_w_gate, v_ple_b_gate, v_ple_w_proj, v_final_norm):
    args = locals()
    w = {name: args[name] for name in WEIGHTS}
    m = {name: args["m_" + name] for name in WEIGHTS}
    v = {name: args["v_" + name] for name in WEIGHTS}
    loss, dx, outs = _step(x[0], p[0, 0], loss_target[0], w, m, v)
    return (loss, dx[None], *outs)
```

```python
import functools
from typing import NamedTuple

import jax
import jax.numpy as jnp
from jax import lax
from jax.experimental import pallas as pl
from jax.experimental.pallas import tpu as pltpu

F32 = jnp.float32
BF16 = jnp.bfloat16
MESH = pl.DeviceIdType.MESH
N_DEV = 8
N_CHIPS = 4

D_MODEL = 1024
D_FF = 2816
D_PLE = 256
GM_WIDTH = 1024
GM_HEADS = 8
GM_HEAD_DIM = 128
CHUNK = 128
SSM_WIDTH = 1024
SSM_HEADS = 16
SSM_HEAD_DIM = 64
SSM_GROUPS = 2
SSM_STATE = 128
SSM_CONV = 4
CONV_DIM = SSM_WIDTH + 2 * SSM_GROUPS * SSM_STATE
IN_PROJ = 2 * GM_WIDTH + SSM_WIDTH + CONV_DIM + SSM_HEADS
LANES = 128
BF16_ROWS = 16
F32_ROWS = 8
IN_PROJ_PAD = IN_PROJ - SSM_HEADS + LANES
UV_W = 2 * GM_WIDTH
ZXD_W = IN_PROJ_PAD - UV_W
HALO = 8
EPS = 1e-6

ADAM_LR = 0.001
ADAM_B1 = 0.9
ADAM_B2 = 0.999
ADAM_EPS = 1e-08
ADAM_WD = 0.01
ADAM_STEP = 10

VMEM_LIMIT = 56 * 1024 * 1024
PACK_COLS = 1024


def _rms(x, g):
    return x * lax.rsqrt(jnp.mean(x * x, axis=-1, keepdims=True) + EPS) * g


def _gelu(x):
    return 0.5 * x * (1.0 + lax.erf(x * (2.0 ** -0.5)))


def _silu(x):
    return x * jax.nn.sigmoid(x)


def _dot(a, b):
    return jnp.dot(a.astype(BF16), b.astype(BF16), preferred_element_type=F32)


def _dot_nt(a, b):
    return lax.dot_general(a.astype(BF16), b.astype(BF16), (((1,), (1,)), ((), ())), preferred_element_type=F32)


def _dot_tn(a, b):
    return lax.dot_general(a.astype(BF16), b.astype(BF16), (((0,), (0,)), ((), ())), preferred_element_type=F32)


def _split3(x):
    hi = x.astype(BF16)
    rest = x - hi.astype(F32)
    mid = rest.astype(BF16)
    return hi, mid, (rest - mid.astype(F32)).astype(BF16)


def _exact_dot(x, mask, dims, x_first=True, n_terms=3):
    terms = [lax.dot_general(*((t, mask) if x_first else (mask, t)), (dims, ((), ())), preferred_element_type=F32)
             for t in _split3(x)[:n_terms]]
    total = terms[0]
    for term in terms[1:]:
        total = total + term
    return total


def _mask_product(fwd_dims, fwd_x_first, bwd_dims, bwd_x_first, bwd_terms=3):
    @jax.custom_vjp
    def product(x, mask):
        return _exact_dot(x, mask, fwd_dims, fwd_x_first)

    def fwd(x, mask):
        return product(x, mask), mask

    def bwd(mask, g):
        return _exact_dot(g, mask, bwd_dims, bwd_x_first, bwd_terms), jnp.zeros_like(mask)

    product.defvjp(fwd, bwd)
    return product


_widen = _mask_product(((1,), (0,)), True, ((1,), (1,)), True, bwd_terms=2)
_cumsum_rows = _mask_product(((1,), (0,)), False, ((0,), (0,)), False)
_cumsum_cols = _mask_product(((0,), (0,)), True, ((1,), (1,)), False)


class _Pieces(NamedTuple):
    gathered: jax.Array
    row_off: int
    rows: int


class _Comm(NamedTuple):
    phases: object
    src: jax.Array
    dst: jax.ShapeDtypeStruct
    scratch: tuple


def _tiled(body, name, n_steps, tiled_in, full_in, big_in, tiled_out, acc_out, scratch=(), reverse=False, comm=None):
    n_t, n_f, n_b, n_to, n_a = len(tiled_in), len(full_in), len(big_in), len(tiled_out), len(acc_out)
    n_c = 1 if comm else 0

    def row(i):
        return n_steps - 1 - i if reverse else i

    in_specs, args = [], []
    for arr, br, bc, cb in tiled_in:
        if callable(cb):
            in_specs.append(pl.BlockSpec((br, bc), cb))
        else:
            in_specs.append(pl.BlockSpec((br, bc), functools.partial(lambda i, cb: (row(i), cb), cb=cb)))
        args.append(arr)
    for arr in full_in:
        in_specs.append(pl.BlockSpec(arr.shape, functools.partial(lambda i, nd: (0,) * nd, nd=arr.ndim)))
        args.append(arr)
    big_shapes, n_copies = [], 0
    for big in big_in:
        in_specs.append(pl.BlockSpec(memory_space=pl.ANY))
        if isinstance(big, _Pieces):
            args.append(big.gathered)
            big_shapes.append(((N_DEV * big.rows, PACK_COLS), big.gathered.dtype))
            n_copies += N_DEV
        else:
            args.append(big)
            big_shapes.append((big.shape, big.dtype))
            n_copies += 1
    if comm:
        in_specs.append(pl.BlockSpec(memory_space=pl.ANY))
        args.append(comm.src)
    out_specs, out_shape = [], []
    for rows, cols, dt, br in tiled_out:
        out_specs.append(pl.BlockSpec((br, cols), lambda i: (row(i), 0)))
        out_shape.append(jax.ShapeDtypeStruct((rows, cols), dt))
    for shp, dt in acc_out:
        out_specs.append(pl.BlockSpec(shp, functools.partial(lambda i, nd: (0,) * nd, nd=len(shp))))
        out_shape.append(jax.ShapeDtypeStruct(shp, dt))
    if comm:
        out_specs.append(pl.BlockSpec(memory_space=pl.ANY))
        out_shape.append(comm.dst)
    scratch_shapes = [pltpu.VMEM(shp, dt) for shp, dt in big_shapes] + list(scratch)
    if n_copies:
        scratch_shapes.append(pltpu.SemaphoreType.DMA((n_copies,)))
    if comm:
        scratch_shapes += list(comm.scratch)

    def kern(*refs):
        n_in = n_t + n_f + n_b + n_c
        ins = refs[: n_t + n_f]
        big_hbm = refs[n_t + n_f : n_t + n_f + n_b]
        outs = refs[n_in : n_in + n_to + n_a]
        rest = refs[n_in + n_to + n_a + n_c :]
        big_vmem, scr = rest[:n_b], rest[n_b:]
        if comm:
            scr, comm_scr = scr[:-len(comm.scratch)], scr[-len(comm.scratch):]
            comm_start, comm_mid, comm_finish = comm.phases(refs[n_in - 1], refs[n_in + n_to + n_a], *comm_scr)
        if n_copies:
            scr, copy_sems = scr[:-1], scr[-1]
        step = pl.program_id(0)

        @pl.when(step == 0)
        def _():
            copies = []
            for big, src, dst in zip(big_in, big_hbm, big_vmem):
                if isinstance(big, _Pieces):
                    for j in range(N_DEV):
                        copies.append((src.at[j, pl.ds(big.row_off, big.rows), :], dst.at[pl.ds(j * big.rows, big.rows), :]))
                else:
                    copies.append((src, dst))
            copies = [pltpu.make_async_copy(a, b, copy_sems.at[k]) for k, (a, b) in enumerate(copies)]
            for cp in copies:
                cp.start()
            for cp in copies:
                cp.wait()
            for acc in outs[n_to:]:
                acc[...] = jnp.zeros(acc.shape, acc.dtype)
            if comm:
                comm_start()

        body(row(step), *ins, *big_vmem, *outs, *scr)
        if comm:
            pl.when(step == (n_steps - 1) // 2)(comm_mid)
            pl.when(step == n_steps - 1)(comm_finish)

    res = pl.pallas_call(
        kern,
        out_shape=out_shape,
        grid=(n_steps,),
        in_specs=in_specs,
        out_specs=out_specs,
        scratch_shapes=scratch_shapes,
        name=name,
        compiler_params=pltpu.CompilerParams(dimension_semantics=("arbitrary",), vmem_limit_bytes=VMEM_LIMIT),
    )(*args)
    return res


FWD_CHUNKS = ((0, 1536), (1536, D_FF))
DGRAD_CHUNKS = ((0, 1024), (1024, 2048), (2048, D_FF))
FFN_TM = 256


def _ffn_fwd(h, g, wg_t, wu_t, wd, name, comm=None, mixed=None):
    T = h.shape[0]

    def ffn(x, g_ref, wg_ref, wu_ref, wd_ref, o_ref, n_ref, a_ref, b_ref, s_ref):
        n = _rms(x, g_ref[...]).astype(BF16)
        n_ref[...] = n
        f = jnp.zeros(x.shape, F32)
        for lo, hi in FWD_CHUNKS:
            a = _dot_nt(n, wg_ref[lo:hi, :])
            b = _dot_nt(n, wu_ref[lo:hi, :])
            s = (_silu(a) * b).astype(BF16)
            a_ref[:, lo:hi] = a.astype(BF16)
            b_ref[:, lo:hi] = b.astype(BF16)
            s_ref[:, lo:hi] = s
            f = f + jnp.dot(s, wd_ref[lo:hi, :], preferred_element_type=F32)
        o_ref[...] = x + 0.5 * f

    def body_plain(i, h_ref, *refs):
        ffn(h_ref[...], *refs)

    def body_mixed(i, h_ref, ya_ref, yb_ref, g_ref, wg_ref, wu_ref, wd_ref, wo_ref, o_ref, n_ref, a_ref, b_ref, s_ref, x_ref):
        x = (h_ref[...] + jnp.dot(ya_ref[...], wo_ref[:GM_WIDTH, :], preferred_element_type=F32)
             + jnp.dot(yb_ref[...], wo_ref[GM_WIDTH:, :], preferred_element_type=F32))
        x_ref[...] = x
        ffn(x, g_ref, wg_ref, wu_ref, wd_ref, o_ref, n_ref, a_ref, b_ref, s_ref)

    body = body_mixed if mixed else body_plain
    tiled_in, big_in = [(h, FFN_TM, D_MODEL, 0)], [wg_t, wu_t, wd]
    tiled_out = [(T, D_MODEL, F32, FFN_TM), (T, D_MODEL, BF16, FFN_TM), (T, D_FF, BF16, FFN_TM), (T, D_FF, BF16, FFN_TM),
                 (T, D_FF, BF16, FFN_TM)]
    if mixed:
        tiled_in += [(mixed[0], FFN_TM, GM_WIDTH, 0), (mixed[1], FFN_TM, SSM_WIDTH, 0)]
        big_in.append(mixed[2])
        tiled_out.append((T, D_MODEL, F32, FFN_TM))
    return _tiled(body, name, T // FFN_TM, tiled_in, [g], big_in, tiled_out, [], comm=comm)


def _ffn_dgrad(h, dout, a16, b16, g, wg_t, wu_t, wd, name):
    T = h.shape[0]

    def body(i, h_ref, do_ref, a_ref, b_ref, g_ref, wg_ref, wu_ref, wd_ref, dh_ref, da_ref, db_ref, dg_ref):
        dout = do_ref[...]
        _, rms_vjp = jax.vjp(_rms, h_ref[...], g_ref[...])
        dfo = (0.5 * dout).astype(BF16)
        dn = jnp.zeros(dout.shape, F32)
        for lo, hi in DGRAD_CHUNKS:
            a = a_ref[:, lo:hi].astype(F32)
            b = b_ref[:, lo:hi].astype(F32)
            sg = jax.nn.sigmoid(a)
            ds = _dot_nt(dfo, wd_ref[lo:hi, :])
            db = (ds * (a * sg)).astype(BF16)
            da = (ds * b * (sg * (1.0 + a * (1.0 - sg)))).astype(BF16)
            dn = dn + _dot(da, wg_ref[lo:hi, :]) + _dot(db, wu_ref[lo:hi, :])
            da_ref[:, lo:hi] = da
            db_ref[:, lo:hi] = db
        dx, dg = rms_vjp(dn)
        dh_ref[...] = dout + dx
        dg_ref[...] += dg

    return _tiled(body, name, T // FFN_TM,
                  [(h, FFN_TM, D_MODEL, 0), (dout, FFN_TM, D_MODEL, 0), (a16, FFN_TM, D_FF, 0), (b16, FFN_TM, D_FF, 0)],
                  [g], [wg_t, wu_t, wd],
                  [(T, D_MODEL, F32, FFN_TM), (T, D_FF, BF16, FFN_TM), (T, D_FF, BF16, FFN_TM)], [((1, D_MODEL), F32)])


FF_BN = D_FF // 2
DOWN_BN, DOWN_BK = 512, 1024
SQUARE_BN = 1024
ZXD_BN = ZXD_W // 3


def _wgrad(a, b, bn, name, scale=None, transpose_out=False, bk=2048, comm=None, b_cols=None):
    T, M = a.shape
    b_first, N = b_cols if b_cols else (0, b.shape[1])
    bk = min(bk, T)
    assert M % LANES == 0 and N % bn == 0 and b_first % bn == 0 and T % bk == 0
    n_j, n_k = N // bn, T // bk
    j_first = b_first // bn
    n_c = 1 if comm else 0

    def kern(*refs):
        a_ref, b_ref, o_ref, acc_ref = refs[0], refs[1], refs[2 + n_c], refs[3 + 2 * n_c]
        j, k = pl.program_id(0), pl.program_id(1)
        if comm:
            comm_start, comm_mid, comm_finish = comm.phases(refs[2], refs[4], *refs[6:])
            pl.when((j == 0) & (k == 0))(comm_start)

        @pl.when(k == 0)
        def _():
            acc_ref[...] = jnp.zeros(acc_ref.shape, F32)

        bv = b_ref[...]
        if scale is not None:
            bv = bv * scale
        acc_ref[...] += _dot_tn(a_ref[...], bv)

        @pl.when(k == n_k - 1)
        def _():
            acc = acc_ref[...]
            o_ref[...] = (acc.T if transpose_out else acc).astype(BF16)

        if comm:
            pl.when((j == (n_j - 1) // 2) & (k == n_k - 1))(comm_mid)
            pl.when((j == n_j - 1) & (k == n_k - 1))(comm_finish)

    if transpose_out:
        out_shape, out_spec = (N, M), pl.BlockSpec((bn, M), lambda j, k: (j, 0))
    else:
        out_shape, out_spec = (M, N), pl.BlockSpec((M, bn), lambda j, k: (0, j))
    any_spec = pl.BlockSpec(memory_space=pl.ANY)
    res = pl.pallas_call(
        kern,
        out_shape=[jax.ShapeDtypeStruct(out_shape, BF16)] + ([comm.dst] if comm else []),
        grid=(n_j, n_k),
        in_specs=[pl.BlockSpec((bk, M), lambda j, k: (k, 0)),
                  pl.BlockSpec((bk, bn), lambda j, k: (k, j_first + j))] + [any_spec] * n_c,
        out_specs=[out_spec] + [any_spec] * n_c,
        scratch_shapes=[pltpu.VMEM((M, bn), F32)] + (list(comm.scratch) if comm else []),
        name=name,
        compiler_params=pltpu.CompilerParams(dimension_semantics=("arbitrary", "arbitrary"), vmem_limit_bytes=VMEM_LIMIT),
    )(a, b, *([comm.src] if comm else []))
    return res if comm else res[0]


PROJ_TM = 512
PROJ_DGRAD_TM = 256
UVZ_W = 2 * GM_WIDTH + SSM_WIDTH
PROJ_KEPT = UVZ_W + LANES
Z_BLK = 2 * GM_WIDTH // SSM_WIDTH
DT_BLK = UVZ_W // LANES


def _mix_in_fwd(h, g, w_in_t, conv_w, conv_b):
    T = h.shape[0]

    def body(i, h_ref, g_ref, cw_ref, cb_ref, w_ref, p_ref, n_ref, x_ref, xc_ref, ext_ref):
        @pl.when(i == 0)
        def _():
            ext_ref[0:HALO, :] = jnp.zeros((HALO, CONV_DIM), F32)

        n = _rms(h_ref[...], g_ref[...]).astype(BF16)
        n_ref[...] = n
        proj = _dot_nt(n, w_ref[...])
        p_ref[:, :UVZ_W] = proj[:, :UVZ_W]
        p_ref[:, UVZ_W:] = proj[:, UVZ_W + CONV_DIM:]
        xbc = proj[:, UVZ_W:UVZ_W + CONV_DIM]
        x_ref[...] = xbc.astype(BF16)
        ext_ref[HALO:, :] = xbc
        xc_ref[...] = _conv_taps(ext_ref, cw_ref[...], cb_ref[...], PROJ_TM)
        ext_ref[0:HALO, :] = ext_ref[PROJ_TM:PROJ_TM + HALO, :]

    return _tiled(body, "mix_in_fwd", T // PROJ_TM, [(h, PROJ_TM, D_MODEL, 0)], [g, conv_w, conv_b], [w_in_t],
                  [(T, PROJ_KEPT, F32, PROJ_TM), (T, D_MODEL, BF16, PROJ_TM), (T, CONV_DIM, BF16, PROJ_TM),
                   (T, CONV_DIM, F32, PROJ_TM)], [],
                  scratch=[pltpu.VMEM((HALO + PROJ_TM, CONV_DIM), F32)])


def _mix_in_dgrad(h, dh_in, dp_uv, dp_zxd, g, w_in_t, comm=None):
    T = h.shape[0]

    def body(i, h_ref, dh_ref, duv_ref, dzxd_ref, g_ref, w_ref, o_ref, dg_ref):
        dn = _dot(duv_ref[...], w_ref[:UV_W, :]) + _dot(dzxd_ref[...], w_ref[UV_W:, :])
        _, rms_vjp = jax.vjp(_rms, h_ref[...], g_ref[...])
        dx, dg = rms_vjp(dn)
        o_ref[...] = dh_ref[...] + dx
        dg_ref[...] += dg

    return _tiled(body, "mix_in_dgrad", T // PROJ_DGRAD_TM,
                  [(h, PROJ_DGRAD_TM, D_MODEL, 0), (dh_in, PROJ_DGRAD_TM, D_MODEL, 0), (dp_uv, PROJ_DGRAD_TM, UV_W, 0),
                   (dp_zxd, PROJ_DGRAD_TM, ZXD_W, 0)], [g], [w_in_t],
                  [(T, D_MODEL, F32, PROJ_DGRAD_TM)], [((1, D_MODEL), F32)], comm=comm)


def _out_proj_dgrad(dh, w_out):
    T = dh.shape[0]

    def body(i, dh_ref, w_ref, dya_ref, dyb_ref):
        d = dh_ref[...].astype(BF16)
        dya_ref[...] = _dot_nt(d, w_ref[:GM_WIDTH, :])
        dyb_ref[...] = _dot_nt(d, w_ref[GM_WIDTH:, :])

    return _tiled(body, "out_proj_dgrad", T // PROJ_TM, [(dh, PROJ_TM, D_MODEL, 0)], [], [w_out],
                  [(T, GM_WIDTH, F32, PROJ_TM), (T, SSM_WIDTH, F32, PROJ_TM)], [])


def _gm_chunk(u, v, ln_g, ln_b, b_st, out_g, *w_heads):
    ug = _gelu(u)
    vg = _gelu(v)
    mu = jnp.mean(vg, axis=-1, keepdims=True)
    xc = vg - mu
    vn = xc * lax.rsqrt(jnp.mean(xc * xc, axis=-1, keepdims=True) + EPS) * ln_g + ln_b
    t_idx = lax.broadcasted_iota(jnp.int32, (CHUNK, CHUNK), 0)
    s_idx = lax.broadcasted_iota(jnp.int32, (CHUNK, CHUNK), 1)
    causal = t_idx >= s_idx
    mixed = []
    for hd in range(GM_HEADS):
        wm = jnp.where(causal, w_heads[hd], 0.0)
        cols = slice(hd * GM_HEAD_DIM, (hd + 1) * GM_HEAD_DIM)
        mixed.append(_dot(wm, vn[:, cols]) + b_st[:, hd:hd + 1])
    ya0 = ug * jnp.concatenate(mixed, axis=1)
    return _rms(ya0, out_g)


GM_FWD_CHUNKS = 2


def _gm_fwd(proj, ln_g, ln_b, w_s, b_st, out_g):
    T = proj.shape[0]

    rows = GM_FWD_CHUNKS * CHUNK

    def body(i, u_ref, v_ref, lg_ref, lb_ref, w_ref, bs_ref, og_ref, ya_ref):
        w_heads = [w_ref[hd] for hd in range(GM_HEADS)]
        for c in range(GM_FWD_CHUNKS):
            tok = pl.ds(c * CHUNK, CHUNK)
            ya = _gm_chunk(u_ref[tok, :], v_ref[tok, :], lg_ref[...], lb_ref[...], bs_ref[...], og_ref[...], *w_heads)
            ya_ref[tok, :] = ya.astype(BF16)

    return _tiled(body, "gmlp_fwd", T // rows, [(proj, rows, GM_WIDTH, 0), (proj, rows, GM_WIDTH, 1)],
                  [ln_g, ln_b, w_s, b_st, out_g], [], [(T, GM_WIDTH, BF16, rows)], [])[0]


def _gm_bwd(proj, dya, ln_g, ln_b, w_s, b_st, out_g):
    T = proj.shape[0]

    def body(i, u_ref, v_ref, dy_ref, lg_ref, lb_ref, w_ref, bs_ref, og_ref, duv_ref, dlg_ref, dlb_ref, dw_ref, dbs_ref,
             dog_ref):
        w_heads = [w_ref[hd] for hd in range(GM_HEADS)]
        _, vjp = jax.vjp(_gm_chunk, u_ref[...], v_ref[...], lg_ref[...], lb_ref[...], bs_ref[...], og_ref[...], *w_heads)
        grads = vjp(dy_ref[...])
        duv_ref[:, :GM_WIDTH] = grads[0].astype(BF16)
        duv_ref[:, GM_WIDTH:] = grads[1].astype(BF16)
        dlg_ref[...] += grads[2]
        dlb_ref[...] += grads[3]
        dbs_ref[...] += grads[4]
        dog_ref[...] += grads[5]
        for hd in range(GM_HEADS):
            dw_ref[hd] += grads[6 + hd]

    return _tiled(body, "gmlp_bwd", T // CHUNK,
                  [(proj, CHUNK, GM_WIDTH, 0), (proj, CHUNK, GM_WIDTH, 1), (dya, CHUNK, GM_WIDTH, 0)],
                  [ln_g, ln_b, w_s, b_st, out_g], [], [(T, UV_W, BF16, CHUNK)],
                  [((1, GM_WIDTH), F32), ((1, GM_WIDTH), F32), ((GM_HEADS, CHUNK, CHUNK), F32),
                   ((CHUNK, GM_HEADS), F32), ((1, GM_WIDTH), F32)])


def _ssd_chunk(xc, z, dtr, s_in, dt_bias, a_log, d_skip, norm_g):
    half = SSM_WIDTH // SSM_GROUPS
    l_idx = lax.broadcasted_iota(jnp.int32, (CHUNK, CHUNK), 0)
    s_idx = lax.broadcasted_iota(jnp.int32, (CHUNK, CHUNK), 1)
    causal = l_idx >= s_idx
    head_of_col = lax.broadcasted_iota(jnp.int32, (SSM_HEADS, SSM_WIDTH), 1) // SSM_HEAD_DIM
    expand = (head_of_col == lax.broadcasted_iota(jnp.int32, (SSM_HEADS, SSM_WIDTH), 0)).astype(BF16)

    xcs = _silu(xc)
    xs = xcs[:, :SSM_WIDTH]
    dt = jax.nn.softplus(dtr + dt_bias)
    adt = dt * (-jnp.exp(a_log))
    acs = _cumsum_rows(adt, causal.astype(BF16))
    acs_t = _cumsum_cols(adt, (l_idx <= s_idx).astype(BF16))
    tot = acs[CHUNK - 1:CHUNK, :]
    dt_w = _widen(dt, expand)
    out_decay_w = _widen(jnp.exp(acs), expand)
    state_decay_w = _widen(jnp.exp(tot - acs), expand)
    chunk_decay_w = _widen(jnp.exp(tot), expand)
    d_skip_w = _widen(d_skip, expand)
    xdt = xs * dt_w
    xdt_decayed = xdt * state_decay_w

    y_diag, y_off, states = [], [], []
    for grp in range(SSM_GROUPS):
        b0 = SSM_WIDTH + grp * SSM_STATE
        c0 = SSM_WIDTH + SSM_GROUPS * SSM_STATE + grp * SSM_STATE
        bm = xcs[:, b0:b0 + SSM_STATE].astype(BF16)
        cm = xcs[:, c0:c0 + SSM_STATE].astype(BF16)
        cb = _dot_nt(cm, bm)
        for k in range(grp * SSM_HEADS // SSM_GROUPS, (grp + 1) * SSM_HEADS // SSM_GROUPS):
            decay = jnp.exp(jnp.where(causal, acs[:, k:k + 1] - acs_t[k:k + 1, :], -jnp.inf))
            y_diag.append(_dot(cb * decay, xdt[:, k * SSM_HEAD_DIM:(k + 1) * SSM_HEAD_DIM]))
        cols = slice(grp * half, (grp + 1) * half)
        states.append(_dot_tn(bm, xdt_decayed[:, cols]))
        y_off.append(_dot(cm, s_in[:, cols]))
    y = jnp.concatenate(y_diag, axis=1) + jnp.concatenate(y_off, axis=1) * out_decay_w + xs * d_skip_w
    s_out = s_in * chunk_decay_w + jnp.concatenate(states, axis=1)
    y = y * _silu(z)
    normed = []
    for grp in range(SSM_GROUPS):
        yg = y[:, grp * half:(grp + 1) * half]
        normed.append(yg * lax.rsqrt(jnp.mean(yg * yg, axis=-1, keepdims=True) + EPS))
    return jnp.concatenate(normed, axis=1) * norm_g, s_out


def _sum_row_tiles(x):
    return x.reshape(x.shape[0] // F32_ROWS, F32_ROWS, x.shape[1]).sum(axis=0)


def _conv_taps(ext_ref, w, b, rows):
    y = b
    for k in range(SSM_CONV):
        y = y + w[k:k + 1, :] * ext_ref[pl.ds(HALO - (SSM_CONV - 1) + k, rows), :]
    return y


def _ssd_fwd(proj, xc, dt_bias, a_log, d_skip, norm_g, comm=None):
    T = proj.shape[0]
    n_chunks = T // CHUNK

    def body(i, z_ref, xc_ref, dt_ref, dtb_ref, al_ref, dsk_ref, ng_ref, yb_ref, sin_ref, st_ref):
        @pl.when(i == 0)
        def _():
            st_ref[...] = jnp.zeros(st_ref.shape, F32)

        s_in = st_ref[...]
        yb, s_out = _ssd_chunk(xc_ref[...], z_ref[...], dt_ref[:, 0:SSM_HEADS], s_in, dtb_ref[...], al_ref[...],
                               dsk_ref[...], ng_ref[...])
        yb_ref[...] = yb.astype(BF16)
        sin_ref[...] = s_in
        st_ref[...] = s_out

    return _tiled(body, "ssd_fwd", n_chunks,
                  [(proj, CHUNK, SSM_WIDTH, Z_BLK), (xc, CHUNK, CONV_DIM, 0), (proj, CHUNK, LANES, DT_BLK)],
                  [dt_bias, a_log, d_skip, norm_g], [],
                  [(T, SSM_WIDTH, BF16, CHUNK), (n_chunks * SSM_STATE, SSM_WIDTH, F32, SSM_STATE)], [],
                  scratch=[pltpu.VMEM((SSM_STATE, SSM_WIDTH), F32)], comm=comm)


def _ssd_bwd(proj, x16, xc, dyb, s_all, conv_w, dt_bias, a_log, d_skip, norm_g, comm=None):
    T = proj.shape[0]
    n_chunks = T // CHUNK

    def body(i, z_ref, x_ref, xc_ref, dt_ref, dy_ref, sin_ref, cw_ref, dtb_ref, al_ref, dsk_ref, ng_ref,
             dzxd_ref, dcw_ref, dcb_ref, ddtb_ref, dal_ref, ddsk_ref, dng_ref, dext_ref, dst_ref, cw_acc, cb_acc):
        @pl.when(i == n_chunks - 1)
        def _():
            dext_ref[CHUNK:, :] = jnp.zeros((HALO, CONV_DIM), F32)
            dst_ref[...] = jnp.zeros(dst_ref.shape, F32)
            cw_acc[...] = jnp.zeros(cw_acc.shape, F32)
            cb_acc[...] = jnp.zeros(cb_acc.shape, F32)

        _, vjp = jax.vjp(_ssd_chunk, xc_ref[...], z_ref[...], dt_ref[:, 0:SSM_HEADS], sin_ref[...], dtb_ref[...], al_ref[...],
                         dsk_ref[...], ng_ref[...])
        dxc, dz, ddtr, ds_in, ddtb, dal, ddsk, dng = vjp((dy_ref[...], dst_ref[...]))
        dst_ref[...] = ds_in
        ddtb_ref[...] += ddtb
        dal_ref[...] += dal
        ddsk_ref[...] += ddsk
        dng_ref[...] += dng
        dext_ref[0:CHUNK, :] = dxc
        cw = cw_ref[...]
        x = x_ref[...].astype(F32)
        dx = jnp.zeros((CHUNK, CONV_DIM), F32)
        for k in range(SSM_CONV):
            shifted = dext_ref[pl.ds(SSM_CONV - 1 - k, CHUNK), :]
            dx = dx + cw[k:k + 1, :] * shifted
            cw_acc[k] += _sum_row_tiles(shifted * x)
        cb_acc[...] += _sum_row_tiles(dxc)

        @pl.when(i == 0)
        def _():
            dcw_ref[...] = jnp.sum(cw_acc[...], axis=1)
            dcb_ref[...] = jnp.sum(cb_acc[...], axis=0, keepdims=True)

        dext_ref[CHUNK:, :] = dext_ref[0:HALO, :]
        dzxd_ref[:, 0:SSM_WIDTH] = dz.astype(BF16)
        dzxd_ref[:, SSM_WIDTH:SSM_WIDTH + CONV_DIM] = dx.astype(BF16)
        dzxd_ref[:, SSM_WIDTH + CONV_DIM:] = jnp.concatenate(
            [ddtr, jnp.zeros((CHUNK, LANES - SSM_HEADS), F32)], axis=1).astype(BF16)

    return _tiled(body, "ssd_bwd", n_chunks,
                  [(proj, CHUNK, SSM_WIDTH, Z_BLK), (x16, CHUNK, CONV_DIM, 0), (xc, CHUNK, CONV_DIM, 0),
                   (proj, CHUNK, LANES, DT_BLK), (dyb, CHUNK, SSM_WIDTH, 0), (s_all, SSM_STATE, SSM_WIDTH, 0)],
                  [conv_w, dt_bias, a_log, d_skip, norm_g], [],
                  [(T, ZXD_W, BF16, CHUNK)],
                  [((SSM_CONV, CONV_DIM), F32), ((1, CONV_DIM), F32), ((1, SSM_HEADS), F32), ((1, SSM_HEADS), F32),
                   ((1, SSM_HEADS), F32), ((1, SSM_WIDTH), F32)],
                  scratch=[pltpu.VMEM((CHUNK + HALO, CONV_DIM), F32), pltpu.VMEM((SSM_STATE, SSM_WIDTH), F32),
                           pltpu.VMEM((SSM_CONV, F32_ROWS, CONV_DIM), F32), pltpu.VMEM((F32_ROWS, CONV_DIM), F32)],
                  reverse=True, comm=comm)


TAIL_TM = 512


def _tail(h, p, target, ple_norm, w_gate, b_gate, w_proj_t, final_norm):
    T = h.shape[0]

    def head(x, pre, pp, b_g, f_norm, tgt):
        gate = jax.nn.sigmoid(pre + b_g)
        out = _rms(x + gate * pp, f_norm)
        err = out - tgt
        return 0.5 * jnp.sum(jnp.mean(err * err, axis=-1, keepdims=True), axis=0, keepdims=True)

    def body(i, h_ref, p_ref, t_ref, pn_ref, bg_ref, fn_ref, wg_ref, wp_ref, dh_ref, loss_ref, dwg_ref, dwp_ref, dpn_ref,
             dbg_ref, dfn_ref):
        x = h_ref[...]
        n4f, n_vjp = jax.vjp(_rms, x, pn_ref[...])
        n4 = n4f.astype(BF16)
        pre = jnp.dot(n4, wg_ref[...], preferred_element_type=F32)
        p16 = p_ref[...].astype(BF16)
        pp = _dot_nt(p16, wp_ref[...])
        loss, h_vjp = jax.vjp(functools.partial(head, tgt=t_ref[...]), x, pre, pp, bg_ref[...], fn_ref[...])
        dx, dpre, dpp, dbg, dfn = h_vjp(jnp.ones((1, 1), F32))
        dpre16 = dpre.astype(BF16)
        dn4 = _dot_nt(dpre16, wg_ref[...])
        dx2, dpn = n_vjp(dn4)
        dh_ref[...] = dx + dx2
        loss_ref[...] += loss
        dwg_ref[...] += _dot_tn(n4, dpre16)
        dwp_ref[...] += _dot_tn(p16, dpp)
        dpn_ref[...] += dpn
        dbg_ref[...] += dbg
        dfn_ref[...] += dfn

    return _tiled(body, "tail", T // TAIL_TM,
                  [(h, TAIL_TM, D_MODEL, 0), (p, TAIL_TM, D_PLE, 0), (target, TAIL_TM, D_MODEL, 0)],
                  [ple_norm, b_gate, final_norm], [w_gate, w_proj_t],
                  [(T, D_MODEL, F32, TAIL_TM)],
                  [((1, 1), F32), ((D_MODEL, D_MODEL), F32), ((D_PLE, D_MODEL), F32), ((1, D_MODEL), F32),
                   ((1, D_MODEL), F32), ((1, D_MODEL), F32)])


def _gather_phases(x_ref, out_ref, send_sems, recv_sems, local_sem):
    mx, my, mc = lax.axis_index("x"), lax.axis_index("y"), lax.axis_index("c")
    me, sibling = (mx, my, mc), (mx, my, 1 - mc)
    chips = [(1 - mx, my), (mx, 1 - my), (1 - mx, 1 - my)]

    def rows(px, py, pc):
        return out_ref.at[4 * px + 2 * py + pc]

    def copy(k, block, to, src=None):
        return pltpu.make_async_remote_copy(
            src_ref=rows(*block) if src is None else src, dst_ref=rows(*block),
            send_sem=send_sems.at[k], recv_sem=recv_sems.at[k], device_id=to, device_id_type=MESH)

    mine = pltpu.make_async_copy(x_ref, rows(*me), local_sem)
    first = [copy(0, me, sibling, src=x_ref)] + [copy(1 + j, me, (*chip, mc), src=x_ref) for j, chip in enumerate(chips)]
    passed = [copy(4 + j, (*chip, mc), sibling) for j, chip in enumerate(chips)]

    def start():
        mine.start()
        for cp in first:
            cp.start()

    def mid():
        for j, chip in enumerate(chips):
            copy(1 + j, (*chip, mc), me).wait_recv()
            passed[j].start()

    def finish():
        copy(0, sibling, me).wait_recv()
        for j, chip in enumerate(chips):
            copy(4 + j, (*chip, 1 - mc), me).wait_recv()
        for cp in first + passed:
            cp.wait_send()
        mine.wait()

    return start, mid, finish


def _exchange_phases(x_ref, out_ref, send_sems, recv_sems, local_sem):
    mx, my, mc = lax.axis_index("x"), lax.axis_index("y"), lax.axis_index("c")
    me = 4 * mx + 2 * my + mc
    mine = pltpu.make_async_copy(x_ref.at[me], out_ref.at[me], local_sem)
    copies = []
    for k in range(1, N_DEV):
        px = 1 - mx if k & 4 else mx
        py = 1 - my if k & 2 else my
        pc = 1 - mc if k & 1 else mc
        copies.append(pltpu.make_async_remote_copy(
            src_ref=x_ref.at[4 * px + 2 * py + pc], dst_ref=out_ref.at[me], send_sem=send_sems.at[k - 1],
            recv_sem=recv_sems.at[k - 1], device_id=(px, py, pc), device_id_type=MESH))

    def start():
        mine.start()
        for cp in copies:
            cp.start()

    def finish():
        for cp in copies:
            cp.wait_recv()
        for cp in copies:
            cp.wait_send()
        mine.wait()

    return start, lambda: None, finish


def _chip_exchange_phases(x_ref, out_ref, mine, recv, sums, load_sems, pair_send, pair_recv, chip_send, chip_recv, out_sem):
    mx, my, mc = lax.axis_index("x"), lax.axis_index("y"), lax.axis_index("c")
    my_chip = 2 * mx + my
    load = [pltpu.make_async_copy(x_ref.at[2 * q + mc], mine.at[q], load_sems.at[q]) for q in range(N_CHIPS)]
    to_sibling = [pltpu.make_async_remote_copy(
        src_ref=x_ref.at[2 * q + 1 - mc], dst_ref=recv.at[q], send_sem=pair_send.at[q], recv_sem=pair_recv.at[q],
        device_id=(mx, my, 1 - mc), device_id_type=MESH) for q in range(N_CHIPS)]
    to_chips = []
    for k in range(1, N_CHIPS):
        px = 1 - mx if k & 2 else mx
        py = 1 - my if k & 1 else my
        to_chips.append(pltpu.make_async_remote_copy(
            src_ref=sums.at[2 * px + py], dst_ref=out_ref.at[my_chip], send_sem=chip_send.at[k - 1],
            recv_sem=chip_recv.at[k - 1], device_id=(px, py, mc), device_id_type=MESH))
    keep = pltpu.make_async_copy(sums.at[my_chip], out_ref.at[my_chip], out_sem)

    def start():
        for cp in load + to_sibling:
            cp.start()

    def mid():
        for cp in load:
            cp.wait()
        for cp in to_sibling:
            cp.wait_recv()
        for q in range(N_CHIPS):
            sums[q] = (mine[q].astype(F32) + recv[q].astype(F32)).astype(sums.dtype)
        for cp in to_chips + [keep]:
            cp.start()

    def finish():
        for cp in to_chips:
            cp.wait_recv()
        for cp in to_chips + to_sibling:
            cp.wait_send()
        keep.wait()

    return start, mid, finish


FLAT_SCRATCH = (pltpu.SemaphoreType.DMA((N_DEV - 1,)), pltpu.SemaphoreType.DMA((N_DEV - 1,)), pltpu.SemaphoreType.DMA)


def _gather_comm(x):
    return _Comm(_gather_phases, x, jax.ShapeDtypeStruct((N_DEV,) + x.shape, x.dtype), FLAT_SCRATCH)


def _exchange_comm(x):
    return _Comm(_exchange_phases, x, jax.ShapeDtypeStruct(x.shape, x.dtype), FLAT_SCRATCH)


def _chip_exchange_comm(x):
    stage = pltpu.VMEM((N_CHIPS,) + x.shape[1:], x.dtype)
    sems = [pltpu.SemaphoreType.DMA((n,)) for n in (N_CHIPS, N_CHIPS, N_CHIPS, N_CHIPS - 1, N_CHIPS - 1)]
    return _Comm(_chip_exchange_phases, x, jax.ShapeDtypeStruct((N_CHIPS,) + x.shape[1:], x.dtype),
                 (stage, stage, stage, *sems, pltpu.SemaphoreType.DMA))


def _comm_alone(comms, name):
    n = len(comms)

    def body(*refs):
        phases, first = [], 2 * n
        for k, comm in enumerate(comms):
            phases.append(comm.phases(refs[k], refs[n + k], *refs[first:first + len(comm.scratch)]))
            first += len(comm.scratch)
        for step in range(3):
            for phase in phases:
                phase[step]()

    any_spec = pl.BlockSpec(memory_space=pl.ANY)
    return pl.pallas_call(
        body,
        out_shape=[comm.dst for comm in comms],
        in_specs=[any_spec] * n,
        out_specs=[any_spec] * n,
        scratch_shapes=[shape for comm in comms for shape in comm.scratch],
        name=name,
        compiler_params=pltpu.CompilerParams(vmem_limit_bytes=VMEM_LIMIT),
    )(*[comm.src for comm in comms])


def _sum_parts(p_ref):
    g = p_ref[0].astype(F32)
    for j in range(1, p_ref.shape[0]):
        g = g + p_ref[j].astype(F32)
    return g


def _adamw_store(g, w_ref, m_ref, v_ref, g_ref, d_ref, nm_ref, nv_ref):
    m_new = ADAM_B1 * m_ref[...] + (1.0 - ADAM_B1) * g
    v_new = ADAM_B2 * v_ref[...] + (1.0 - ADAM_B2) * jnp.square(g)
    m_hat = m_new / (1.0 - ADAM_B1 ** ADAM_STEP)
    v_hat = v_new / (1.0 - ADAM_B2 ** ADAM_STEP)
    g_ref[...] = g
    d_ref[...] = -ADAM_LR * (m_hat / (jnp.sqrt(v_hat) + ADAM_EPS) + ADAM_WD * w_ref[...])
    nm_ref[...] = m_new
    nv_ref[...] = v_new


def _adamw_shard(parts, off, w, m, v, name, n_tiles):
    parts = parts if isinstance(parts, (tuple, list)) else (parts,)
    n_p = len(parts)
    _, rows, c = w.shape
    assert c == PACK_COLS == sum(part.shape[2] for part in parts)
    by_rows = rows % BF16_ROWS == 0
    if by_rows:
        tr = rows // n_tiles
        window = (parts[0].shape[0], tr, PACK_COLS)
        spec = pl.BlockSpec((None, tr, PACK_COLS), lambda i: (0, i, 0))
    else:
        assert n_p == 1
        padded, tc = -(-rows // BF16_ROWS) * BF16_ROWS, PACK_COLS // n_tiles
        window = (parts[0].shape[0], padded, tc)
        spec = pl.BlockSpec((None, rows, tc), lambda i: (0, 0, i))

    def kern(*refs):
        p_hbm, (w_ref, m_ref, v_ref, g_ref, d_ref, nm_ref, nv_ref, buf, sems) = refs[:n_p], refs[n_p:]
        i = pl.program_id(0)
        copies, col = [], 0
        for k, part in enumerate(p_hbm):
            width = part.shape[2]
            if by_rows:
                src = part.at[:, pl.ds(pl.multiple_of(off + i * tr, BF16_ROWS), tr), :]
                dst = buf.at[:, :, pl.ds(col, width)]
            else:
                src = part.at[:, pl.ds(off, padded), pl.ds(pl.multiple_of(i * tc, LANES), tc)]
                dst = buf
            copies.append(pltpu.make_async_copy(src, dst, sems.at[k]))
            col += width
        for cp in copies:
            cp.start()
        for cp in copies:
            cp.wait()
        g = _sum_parts(buf)
        if not by_rows:
            keep = lax.broadcasted_iota(jnp.int32, (rows, padded), 0) == lax.broadcasted_iota(jnp.int32, (rows, padded), 1)
            g = _exact_dot(g, keep.astype(BF16), ((1,), (0,)), x_first=False)
        _adamw_store(g, w_ref, m_ref, v_ref, g_ref, d_ref, nm_ref, nv_ref)

    return pl.pallas_call(
        kern,
        out_shape=[jax.ShapeDtypeStruct(w.shape, F32)] * 4,
        grid=(n_tiles,),
        in_specs=[pl.BlockSpec(memory_space=pl.ANY)] * n_p + [spec, spec, spec],
        out_specs=[spec] * 4,
        scratch_shapes=[pltpu.VMEM(window, parts[0].dtype), pltpu.SemaphoreType.DMA((n_p,))],
        name=name,
        compiler_params=pltpu.CompilerParams(dimension_semantics=("arbitrary",), vmem_limit_bytes=VMEM_LIMIT),
    )(*parts, w, m, v)


def _sum_adamw(parts, w, m, v, tr, name):
    _, R, C = parts.shape

    def kern(p_ref, w_ref, m_ref, v_ref, g_ref, d_ref, nm_ref, nv_ref):
        _adamw_store(_sum_parts(p_ref), w_ref, m_ref, v_ref, g_ref, d_ref, nm_ref, nv_ref)

    row_spec = pl.BlockSpec((tr, C), lambda i: (i, 0))
    return pl.pallas_call(
        kern,
        out_shape=[jax.ShapeDtypeStruct((R, C), F32)] * 4,
        grid=(R // tr,),
        in_specs=[pl.BlockSpec((N_DEV, tr, C), lambda i: (0, i, 0)), row_spec, row_spec, row_spec],
        out_specs=[row_spec] * 4,
        name=name,
        compiler_params=pltpu.CompilerParams(dimension_semantics=("arbitrary",), vmem_limit_bytes=VMEM_LIMIT),
    )(parts, w, m, v)


FF_SHARD = D_FF // N_DEV
CONV_SHARD = (SSM_CONV, CONV_DIM // N_DEV)
SHARDS = {"ffn1_w_gate": ((D_MODEL, FF_SHARD), True), "ffn1_w_up": ((D_MODEL, FF_SHARD), True),
          "ffn1_w_down": ((FF_SHARD, D_MODEL), False),
          "ffn2_w_gate": ((D_MODEL, FF_SHARD), True), "ffn2_w_up": ((D_MODEL, FF_SHARD), True),
          "ffn2_w_down": ((FF_SHARD, D_MODEL), False),
          "w_out": ((2 * D_MODEL // N_DEV, D_MODEL), False), "ple_w_gate": ((D_MODEL // N_DEV, D_MODEL), False),
          "w_in": ((D_MODEL, IN_PROJ // N_DEV), True), "ple_w_proj": ((D_PLE, D_MODEL // N_DEV), True),
          "conv_w": (CONV_SHARD, True),
          "conv_w_mid": (CONV_SHARD, True), "conv_w_low": (CONV_SHARD, True)}
BIG = tuple(name for name in SHARDS if not name.startswith("conv_w_"))
SMALL = ("ffn1_norm", "mix_norm", "gm_ln_g", "gm_ln_b", "gm_w_s", "gm_b_s", "gm_out_norm", "conv_b", "dt_bias", "a_log",
         "d_skip", "ssm_norm", "ffn2_norm", "ple_norm", "ple_b_gate", "final_norm")
SMALL_ROWS = 144


def _piece_rows(name):
    shape = SHARDS[name][0]
    return -(-(shape[0] * shape[1]) // PACK_COLS)


def _pad_cols(flat, name):
    pad = _piece_rows(name) * PACK_COLS - flat.shape[-1]
    return flat if pad == 0 else jnp.pad(flat, [(0, 0)] * (flat.ndim - 1) + [(0, pad)])


class _Pack:
    def __init__(self, names, tile_rows):
        self.names, self.tile_rows, self.offsets, off = names, tile_rows, {}, 0
        for name in names:
            self.offsets[name] = off
            off += _piece_rows(name)
        self.rows = -(-off // tile_rows) * tile_rows

    def pack_local(self, vals):
        parts = []
        for name in self.names:
            val = vals[name]
            parts.append(_pad_cols((val.T if SHARDS[name][1] else val).reshape(-1), name))
        flat = jnp.concatenate(parts)
        return jnp.pad(flat, (0, self.rows * PACK_COLS - flat.shape[0])).reshape(self.rows, PACK_COLS)

    def pack_owner_major(self, grads):
        parts, rows = [], 0
        for name in self.names:
            grad, piece_rows = grads[name].astype(BF16), _piece_rows(name)
            if grad.shape != (N_DEV * piece_rows, PACK_COLS):
                grad = _pad_cols(grad.reshape(N_DEV, -1), name)
            parts.append(grad.reshape(N_DEV, piece_rows, PACK_COLS))
            rows += piece_rows
        if rows < self.rows:
            parts.append(jnp.zeros((N_DEV, self.rows - rows, PACK_COLS), BF16))
        return parts[0] if len(parts) == 1 else jnp.concatenate(parts, axis=1)

    def gathered_piece(self, gathered, name):
        shape = SHARDS[name][0]
        rows = gathered[:, self.offsets[name]:self.offsets[name] + _piece_rows(name), :]
        return rows.reshape(N_DEV, -1)[:, :shape[0] * shape[1]]

    def pieces(self, gathered, name):
        return _Pieces(gathered, self.offsets[name], _piece_rows(name))


GATHER_FFN1 = _Pack(("ffn1_w_gate", "ffn1_w_up", "ffn1_w_down"), BF16_ROWS)
GATHER_MIX = _Pack(("w_out", "ple_w_gate", "w_in", "ple_w_proj", "conv_w", "conv_w_mid", "conv_w_low"), BF16_ROWS)
GATHER_FFN2 = _Pack(("ffn2_w_gate", "ffn2_w_up", "ffn2_w_down"), BF16_ROWS)
SCATTER_LATE = _Pack(("ffn2_w_gate", "ffn2_w_up", "ffn2_w_down", "w_out", "ple_w_gate", "ple_w_proj"), BF16_ROWS)
SCATTER_IN = _Pack(("w_in", "conv_w"), BF16_ROWS)
SCATTER_GATE = _Pack(("ffn1_w_gate",), BF16_ROWS)
SCATTER_UP = _Pack(("ffn1_w_up",), BF16_ROWS)
SCATTER_DOWN = _Pack(("ffn1_w_down",), BF16_ROWS)


def _pack_small(vals, behind=()):
    flat = jnp.concatenate([vals[name].reshape(-1).astype(F32) for name in SMALL] + [b.reshape(-1) for b in behind])
    return jnp.pad(flat, (0, SMALL_ROWS * PACK_COLS - flat.shape[0])).reshape(SMALL_ROWS, PACK_COLS)


def _unpack_small(packed, shapes):
    out, off = {}, 0
    flat = packed.reshape(-1)
    for name in SMALL:
        n = 1
        for s in shapes[name]:
            n *= s
        out[name] = flat[off:off + n].reshape(shapes[name])
        off += n
    return out


WEIGHTS = ("ffn1_norm", "ffn1_w_gate", "ffn1_w_up", "ffn1_w_down", "mix_norm", "w_in", "gm_ln_g", "gm_ln_b", "gm_w_s",
           "gm_b_s", "gm_out_norm", "conv_w", "conv_b", "dt_bias", "a_log", "d_skip", "ssm_norm", "w_out", "ffn2_norm",
           "ffn2_w_gate", "ffn2_w_up", "ffn2_w_down", "ple_norm", "ple_w_gate", "ple_b_gate", "ple_w_proj", "final_norm")


def _step(x, p, target, w, m, v):
    local = lambda d: {name: d[name][0] for name in BIG}

    shards = {name: val.astype(BF16) for name, val in local(w).items()}
    conv_high = lax.reduce_precision(w["conv_w"][0], 8, 7)
    conv_mid = lax.reduce_precision(w["conv_w"][0] - conv_high, 8, 7)
    shards["conv_w"] = conv_high.astype(BF16)
    shards["conv_w_mid"] = conv_mid.astype(BF16)
    shards["conv_w_low"] = (w["conv_w"][0] - conv_high - conv_mid).astype(BF16)
    g_ffn1 = _comm_alone([_gather_comm(GATHER_FFN1.pack_local(shards))], "gather_ffn1")[0]

    row = lambda name: w[name].reshape(1, -1)
    gm_w_s = w["gm_w_s"][0]
    gm_b_st = jnp.transpose(w["gm_b_s"][0])
    ffn1 = (row("ffn1_norm"),) + tuple(GATHER_FFN1.pieces(g_ffn1, name) for name in GATHER_FFN1.names)
    gm = (row("gm_ln_g"), row("gm_ln_b"), gm_w_s, gm_b_st, row("gm_out_norm"))

    h1, n1, a1, b1, s1, g_mix = _ffn_fwd(x, *ffn1, "ffn1_fwd", comm=_gather_comm(GATHER_MIX.pack_local(shards)))
    w_in_t = GATHER_MIX.gathered_piece(g_mix, "w_in").reshape(IN_PROJ, D_MODEL)
    w_in_t = jnp.concatenate([w_in_t, jnp.zeros((IN_PROJ_PAD - IN_PROJ, D_MODEL), BF16)], axis=0)
    w_proj_t = GATHER_MIX.gathered_piece(g_mix, "ple_w_proj").reshape(D_MODEL, D_PLE)
    conv_w = sum(GATHER_MIX.gathered_piece(g_mix, name).astype(F32) for name in ("conv_w", "conv_w_mid", "conv_w_low"))
    conv_w = conv_w.reshape(CONV_DIM, SSM_CONV).T
    ssd = (row("dt_bias"), row("a_log"), row("d_skip"), row("ssm_norm"))
    w_out = GATHER_MIX.pieces(g_mix, "w_out")

    proj, n2, x16, xc = _mix_in_fwd(h1, row("mix_norm"), w_in_t, conv_w, row("conv_b"))
    ya = _gm_fwd(proj, *gm)
    yb, s_all, g_ffn2 = _ssd_fwd(proj, xc, *ssd, comm=_gather_comm(GATHER_FFN2.pack_local(shards)))
    ffn2 = (row("ffn2_norm"),) + tuple(GATHER_FFN2.pieces(g_ffn2, name) for name in GATHER_FFN2.names)
    h3, n3, a3, b3, s3, h2 = _ffn_fwd(h1, *ffn2, "ffn2_fwd", mixed=(ya, yb, w_out))

    g, gp = {}, {}
    dh3, loss, gp["ple_w_gate"], d_w_proj, g["ple_norm"], g["ple_b_gate"], g["final_norm"] = _tail(
        h3, p, target, row("ple_norm"), GATHER_MIX.pieces(g_mix, "ple_w_gate"), row("ple_b_gate"), w_proj_t,
        row("final_norm"))
    gp["ple_w_proj"] = d_w_proj.T

    dh2, da3, db3, g["ffn2_norm"] = _ffn_dgrad(h2, dh3, a3, b3, *ffn2, "ffn2_dgrad")
    gp["ffn2_w_gate"] = _wgrad(n3, da3, FF_BN, "ffn2_wgrad_gate", transpose_out=True)
    gp["ffn2_w_up"] = _wgrad(n3, db3, FF_BN, "ffn2_wgrad_up", transpose_out=True)
    gp["ffn2_w_down"] = _wgrad(s3, dh3, DOWN_BN, "ffn2_wgrad_down", scale=0.5, bk=DOWN_BK)

    dya, dyb = _out_proj_dgrad(dh2, w_out)
    gp["w_out"] = jnp.concatenate([_wgrad(ya, dh2, SQUARE_BN, "w_out_wgrad_a"), _wgrad(yb, dh2, SQUARE_BN, "w_out_wgrad_b")], axis=0)

    dp_zxd, d_conv_w, g["conv_b"], g["dt_bias"], g["a_log"], g["d_skip"], g["ssm_norm"], parts_late = _ssd_bwd(
        proj, x16, xc, dyb, s_all, conv_w, *ssd, comm=_exchange_comm(SCATTER_LATE.pack_owner_major(gp)))
    gp["conv_w"] = d_conv_w.T
    dp_uv, g["gm_ln_g"], g["gm_ln_b"], g["gm_w_s"], dbst, g["gm_out_norm"] = _gm_bwd(proj, dya, *gm)
    g["gm_b_s"] = jnp.transpose(dbst)

    parts = {}
    gp["w_in"] = jnp.concatenate([_wgrad(n2, dp_uv, SQUARE_BN, "w_in_wgrad_uv", transpose_out=True),
                                  _wgrad(n2, dp_zxd, ZXD_BN, "w_in_wgrad_zxd", transpose_out=True)], axis=0)[:IN_PROJ]
    dh1, g["mix_norm"], parts[SCATTER_IN] = _mix_in_dgrad(h1, dh2, dp_uv, dp_zxd, row("mix_norm"), w_in_t,
                                                          comm=_exchange_comm(SCATTER_IN.pack_owner_major(gp)))

    dx, da1, db1, g["ffn1_norm"] = _ffn_dgrad(x, dh1, a1, b1, *ffn1, "ffn1_dgrad")
    gp["ffn1_w_gate"], small_parts = _wgrad(n1, da1, FF_BN, "ffn1_wgrad_gate", transpose_out=True,
                                            comm=_gather_comm(_pack_small(g, behind=[loss])))
    gp["ffn1_w_up"], parts[SCATTER_GATE] = _wgrad(n1, db1, FF_BN, "ffn1_wgrad_up", transpose_out=True,
                                                  comm=_chip_exchange_comm(SCATTER_GATE.pack_owner_major(gp)))
    gp["ffn1_w_down"], parts[SCATTER_UP] = _wgrad(s1, dh1, DOWN_BN, "ffn1_wgrad_down", scale=0.5, bk=DOWN_BK,
                                                  comm=_chip_exchange_comm(SCATTER_UP.pack_owner_major(gp)))
    parts[SCATTER_DOWN] = _comm_alone([_chip_exchange_comm(SCATTER_DOWN.pack_owner_major(gp))], "scatter_ffn1_down")[0]
    parts[SCATTER_LATE] = parts_late

    res_big = {}
    for pack, pack_parts in parts.items():
        for name in pack.names:
            shape, transposed = SHARDS[name]
            if name in ("ple_w_proj", "conv_w"):
                nat = pack.gathered_piece(pack_parts, name).reshape((N_DEV,) + shape[::-1])
                res_big[name] = _sum_adamw(jnp.transpose(nat, (0, 2, 1)), w[name][0], m[name][0], v[name][0], shape[0],
                                           "adamw_" + name)
            else:
                flip = (lambda a: jnp.transpose(a, (0, 2, 1))) if transposed else (lambda a: a)
                res = _adamw_shard(pack_parts, pack.offsets[name], flip(w[name]), flip(m[name]), flip(v[name]),
                                   "adamw_" + name, n_tiles=4 if name == "w_in" else 2)
                res_big[name] = [flip(r) for r in res]

    small_shapes = {name: w[name].shape for name in SMALL}
    res_small = _sum_adamw(small_parts, _pack_small(w), _pack_small(m), _pack_small(v), SMALL_ROWS, "adamw_small")
    loss = res_small[0].reshape(-1)[sum(w[name].size for name in SMALL)]
    res_small = [_unpack_small(r, small_shapes) for r in res_small]

    outs = []
    for k in range(4):
        for name in WEIGHTS:
            if name in res_small[k]:
                outs.append(res_small[k][name])
            else:
                outs.append(res_big[name][k].reshape(w[name].shape))
    return loss, dx, outs


def kernel(x, p, ffn1_norm, ffn1_w_gate, ffn1_w_up, ffn1_w_down, mix_norm, w_in, gm_ln_g, gm_ln_b, gm_w_s, gm_b_s, gm_out_norm, conv_w, conv_b, dt_bias, a_log, d_skip, ssm_norm, w_out, ffn2_norm, ffn2_w_gate, ffn2_w_up, ffn2_w_down, ple_norm, ple_w_gate, ple_b_gate, ple_w_proj, final_norm, loss_target, m_ffn1_norm, m_ffn1_w_gate, m_ffn1_w_up, m_ffn1_w_down, m_mix_norm, m_w_in, m_gm_ln_g, m_gm_ln_b, m_gm_w_s, m_gm_b_s, m_gm_out_norm, m_conv_w, m_conv_b, m_dt_bias, m_a_log, m_d_skip, m_ssm_norm, m_w_out, m_ffn2_norm, m_ffn2_w_gate, m_ffn2_w_up, m_ffn2_w_down, m_ple_norm, m_ple_w_gate, m_ple_b_gate, m_ple_w_proj, m_final_norm, v_ffn1_norm, v_ffn1_w_gate, v_ffn1_w_up, v_ffn1_w_down, v_mix_norm, v_w_in, v_gm_ln_g, v_gm_ln_b, v_gm_w_s, v_gm_b_s, v_gm_out_norm, v_conv_w, v_conv_b, v_dt_bias, v_a_log, v_d_skip, v_ssm_norm, v_w_out, v_ffn2_norm, v_ffn2_w_gate, v_ffn2_w_up, v_ffn2_w_down, v_ple_norm, v_ple_w_gate, v_ple_b_gate, v_ple_w_proj, v_final_norm):
    args = locals()
    w = {name: args[name] for name in WEIGHTS}
    m = {name: args["m_" + name] for name in WEIGHTS}
    v = {name: args["v_" + name] for name in WEIGHTS}
    loss, dx, outs = _step(x[0], p[0, 0], loss_target[0], w, m, v)
    return (loss, dx[None], *outs)
```

```python
import functools
from typing import NamedTuple

import jax
import jax.numpy as jnp
from jax import lax
from jax.experimental import pallas as pl
from jax.experimental.pallas import tpu as pltpu

F32 = jnp.float32
BF16 = jnp.bfloat16
MESH = pl.DeviceIdType.MESH
N_DEV = 8
N_CHIPS = 4

D_MODEL = 1024
D_FF = 2816
D_PLE = 256
GM_WIDTH = 1024
GM_HEADS = 8
GM_HEAD_DIM = 128
CHUNK = 128
SSM_WIDTH = 1024
SSM_HEADS = 16
SSM_HEAD_DIM = 64
SSM_GROUPS = 2
SSM_STATE = 128
SSM_CONV = 4
CONV_DIM = SSM_WIDTH + 2 * SSM_GROUPS * SSM_STATE
IN_PROJ = 2 * GM_WIDTH + SSM_WIDTH + CONV_DIM + SSM_HEADS
LANES = 128
BF16_ROWS = 16
F32_ROWS = 8
IN_PROJ_PAD = IN_PROJ - SSM_HEADS + LANES
UV_W = 2 * GM_WIDTH
ZXD_W = IN_PROJ_PAD - UV_W
HALO = 8
EPS = 1e-6

ADAM_LR = 0.001
ADAM_B1 = 0.9
ADAM_B2 = 0.999
ADAM_EPS = 1e-08
ADAM_WD = 0.01
ADAM_STEP = 10

VMEM_LIMIT = 56 * 1024 * 1024
PACK_COLS = 1024


def _rms(x, g):
    return x * lax.rsqrt(jnp.mean(x * x, axis=-1, keepdims=True) + EPS) * g


def _gelu(x):
    return 0.5 * x * (1.0 + lax.erf(x * (2.0 ** -0.5)))


def _silu(x):
    return x * jax.nn.sigmoid(x)


def _dot(a, b):
    return jnp.dot(a.astype(BF16), b.astype(BF16), preferred_element_type=F32)


def _dot_nt(a, b):
    return lax.dot_general(a.astype(BF16), b.astype(BF16), (((1,), (1,)), ((), ())), preferred_element_type=F32)


def _dot_tn(a, b):
    return lax.dot_general(a.astype(BF16), b.astype(BF16), (((0,), (0,)), ((), ())), preferred_element_type=F32)


def _split3(x):
    hi = x.astype(BF16)
    rest = x - hi.astype(F32)
    mid = rest.astype(BF16)
    return hi, mid, (rest - mid.astype(F32)).astype(BF16)


def _exact_dot(x, mask, dims, x_first=True, n_terms=3):
    terms = [lax.dot_general(*((t, mask) if x_first else (mask, t)), (dims, ((), ())), preferred_element_type=F32)
             for t in _split3(x)[:n_terms]]
    total = terms[0]
    for term in terms[1:]:
        total = total + term
    return total


def _mask_product(fwd_dims, fwd_x_first, bwd_dims, bwd_x_first, bwd_terms=3):
    @jax.custom_vjp
    def product(x, mask):
        return _exact_dot(x, mask, fwd_dims, fwd_x_first)

    def fwd(x, mask):
        return product(x, mask), mask

    def bwd(mask, g):
        return _exact_dot(g, mask, bwd_dims, bwd_x_first, bwd_terms), jnp.zeros_like(mask)

    product.defvjp(fwd, bwd)
    return product


_widen = _mask_product(((1,), (0,)), True, ((1,), (1,)), True, bwd_terms=2)
_cumsum_rows = _mask_product(((1,), (0,)), False, ((0,), (0,)), False)
_cumsum_cols = _mask_product(((0,), (0,)), True, ((1,), (1,)), False)


class _Pieces(NamedTuple):
    gathered: jax.Array
    row_off: int
    rows: int


class _Comm(NamedTuple):
    phases: object
    src: jax.Array
    dst: jax.ShapeDtypeStruct
    scratch: tuple


def _tiled(body, name, n_steps, tiled_in, full_in, big_in, tiled_out, acc_out, scratch=(), reverse=False, comm=None):
    n_t, n_f, n_b, n_to, n_a = len(tiled_in), len(full_in), len(big_in), len(tiled_out), len(acc_out)
    n_c = 1 if comm else 0

    def row(i):
        return n_steps - 1 - i if reverse else i

    in_specs, args = [], []
    for arr, br, bc, cb in tiled_in:
        if callable(cb):
            in_specs.append(pl.BlockSpec((br, bc), cb))
        else:
            in_specs.append(pl.BlockSpec((br, bc), functools.partial(lambda i, cb: (row(i), cb), cb=cb)))
        args.append(arr)
    for arr in full_in:
        in_specs.append(pl.BlockSpec(arr.shape, functools.partial(lambda i, nd: (0,) * nd, nd=arr.ndim)))
        args.append(arr)
    big_shapes, n_copies = [], 0
    for big in big_in:
        in_specs.append(pl.BlockSpec(memory_space=pl.ANY))
        if isinstance(big, _Pieces):
            args.append(big.gathered)
            big_shapes.append(((N_DEV * big.rows, PACK_COLS), big.gathered.dtype))
            n_copies += N_DEV
        else:
            args.append(big)
            big_shapes.append((big.shape, big.dtype))
            n_copies += 1
    if comm:
        in_specs.append(pl.BlockSpec(memory_space=pl.ANY))
        args.append(comm.src)
    out_specs, out_shape = [], []
    for rows, cols, dt, br in tiled_out:
        out_specs.append(pl.BlockSpec((br, cols), lambda i: (row(i), 0)))
        out_shape.append(jax.ShapeDtypeStruct((rows, cols), dt))
    for shp, dt in acc_out:
        out_specs.append(pl.BlockSpec(shp, functools.partial(lambda i, nd: (0,) * nd, nd=len(shp))))
        out_shape.append(jax.ShapeDtypeStruct(shp, dt))
    if comm:
        out_specs.append(pl.BlockSpec(memory_space=pl.ANY))
        out_shape.append(comm.dst)
    scratch_shapes = [pltpu.VMEM(shp, dt) for shp, dt in big_shapes] + list(scratch)
    if n_copies:
        scratch_shapes.append(pltpu.SemaphoreType.DMA((n_copies,)))
    if comm:
        scratch_shapes += list(comm.scratch)

    def kern(*refs):
        n_in = n_t + n_f + n_b + n_c
        ins = refs[: n_t + n_f]
        big_hbm = refs[n_t + n_f : n_t + n_f + n_b]
        outs = refs[n_in : n_in + n_to + n_a]
        rest = refs[n_in + n_to + n_a + n_c :]
        big_vmem, scr = rest[:n_b], rest[n_b:]
        if comm:
            scr, comm_scr = scr[:-len(comm.scratch)], scr[-len(comm.scratch):]
            comm_start, comm_mid, comm_finish = comm.phases(refs[n_in - 1], refs[n_in + n_to + n_a], *comm_scr)
        if n_copies:
            scr, copy_sems = scr[:-1], scr[-1]
        step = pl.program_id(0)

        @pl.when(step == 0)
        def _():
            copies = []
            for big, src, dst in zip(big_in, big_hbm, big_vmem):
                if isinstance(big, _Pieces):
                    for j in range(N_DEV):
                        copies.append((src.at[j, pl.ds(big.row_off, big.rows), :], dst.at[pl.ds(j * big.rows, big.rows), :]))
                else:
                    copies.append((src, dst))
            copies = [pltpu.make_async_copy(a, b, copy_sems.at[k]) for k, (a, b) in enumerate(copies)]
            for cp in copies:
                cp.start()
            for cp in copies:
                cp.wait()
            for acc in outs[n_to:]:
                acc[...] = jnp.zeros(acc.shape, acc.dtype)
            if comm:
                comm_start()

        body(row(step), *ins, *big_vmem, *outs, *scr)
        if comm:
            pl.when(step == (n_steps - 1) // 2)(comm_mid)
            pl.when(step == n_steps - 1)(comm_finish)

    res = pl.pallas_call(
        kern,
        out_shape=out_shape,
        grid=(n_steps,),
        in_specs=in_specs,
        out_specs=out_specs,
        scratch_shapes=scratch_shapes,
        name=name,
        compiler_params=pltpu.CompilerParams(dimension_semantics=("arbitrary",), vmem_limit_bytes=VMEM_LIMIT),
    )(*args)
    return res


FWD_CHUNKS = ((0, 1536), (1536, D_FF))
DGRAD_CHUNKS = ((0, 1024), (1024, 2048), (2048, D_FF))
FFN_TM = 256


def _ffn_fwd(h, g, wg_t, wu_t, wd, name, comm=None, mixed=None):
    T = h.shape[0]

    def ffn(x, g_ref, wg_ref, wu_ref, wd_ref, o_ref, n_ref, a_ref, b_ref, s_ref):
        n = _rms(x, g_ref[...]).astype(BF16)
        n_ref[...] = n
        f = jnp.zeros(x.shape, F32)
        for lo, hi in FWD_CHUNKS:
            a = _dot_nt(n, wg_ref[lo:hi, :])
            b = _dot_nt(n, wu_ref[lo:hi, :])
            s = (_silu(a) * b).astype(BF16)
            a_ref[:, lo:hi] = a.astype(BF16)
            b_ref[:, lo:hi] = b.astype(BF16)
            s_ref[:, lo:hi] = s
            f = f + jnp.dot(s, wd_ref[lo:hi, :], preferred_element_type=F32)
        o_ref[...] = x + 0.5 * f

    def body_plain(i, h_ref, *refs):
        ffn(h_ref[...], *refs)

    def body_mixed(i, h_ref, ya_ref, yb_ref, g_ref, wg_ref, wu_ref, wd_ref, wo_ref, o_ref, n_ref, a_ref, b_ref, s_ref, x_ref):
        x = (h_ref[...] + jnp.dot(ya_ref[...], wo_ref[:GM_WIDTH, :], preferred_element_type=F32)
             + jnp.dot(yb_ref[...], wo_ref[GM_WIDTH:, :], preferred_element_type=F32))
        x_ref[...] = x
        ffn(x, g_ref, wg_ref, wu_ref, wd_ref, o_ref, n_ref, a_ref, b_ref, s_ref)

    body = body_mixed if mixed else body_plain
    tiled_in, big_in = [(h, FFN_TM, D_MODEL, 0)], [wg_t, wu_t, wd]
    tiled_out = [(T, D_MODEL, F32, FFN_TM), (T, D_MODEL, BF16, FFN_TM), (T, D_FF, BF16, FFN_TM), (T, D_FF, BF16, FFN_TM),
                 (T, D_FF, BF16, FFN_TM)]
    if mixed:
        tiled_in += [(mixed[0], FFN_TM, GM_WIDTH, 0), (mixed[1], FFN_TM, SSM_WIDTH, 0)]
        big_in.append(mixed[2])
        tiled_out.append((T, D_MODEL, F32, FFN_TM))
    return _tiled(body, name, T // FFN_TM, tiled_in, [g], big_in, tiled_out, [], comm=comm)


def _ffn_dgrad(h, dout, a16, b16, g, wg_t, wu_t, wd, name):
    T = h.shape[0]

    def body(i, h_ref, do_ref, a_ref, b_ref, g_ref, wg_ref, wu_ref, wd_ref, dh_ref, da_ref, db_ref, dg_ref):
        dout = do_ref[...]
        _, rms_vjp = jax.vjp(_rms, h_ref[...], g_ref[...])
        dfo = (0.5 * dout).astype(BF16)
        dn = jnp.zeros(dout.shape, F32)
        for lo, hi in DGRAD_CHUNKS:
            a = a_ref[:, lo:hi].astype(F32)
            b = b_ref[:, lo:hi].astype(F32)
            sg = jax.nn.sigmoid(a)
            ds = _dot_nt(dfo, wd_ref[lo:hi, :])
            db = (ds * (a * sg)).astype(BF16)
            da = (ds * b * (sg * (1.0 + a * (1.0 - sg)))).astype(BF16)
            dn = dn + _dot(da, wg_ref[lo:hi, :]) + _dot(db, wu_ref[lo:hi, :])
            da_ref[:, lo:hi] = da
            db_ref[:, lo:hi] = db
        dx, dg = rms_vjp(dn)
        dh_ref[...] = dout + dx
        dg_ref[...] += dg

    return _tiled(body, name, T // FFN_TM,
                  [(h, FFN_TM, D_MODEL, 0), (dout, FFN_TM, D_MODEL, 0), (a16, FFN_TM, D_FF, 0), (b16, FFN_TM, D_FF, 0)],
                  [g], [wg_t, wu_t, wd],
                  [(T, D_MODEL, F32, FFN_TM), (T, D_FF, BF16, FFN_TM), (T, D_FF, BF16, FFN_TM)], [((1, D_MODEL), F32)])


FF_BN = D_FF // 2
DOWN_BN, DOWN_BK = 512, 1024
SQUARE_BN = 1024
ZXD_BN = ZXD_W // 3


def _wgrad(a, b, bn, name, scale=None, transpose_out=False, bk=2048, comm=None):
    T, M = a.shape
    N = b.shape[1]
    bk = min(bk, T)
    assert M % LANES == 0 and N % bn == 0 and T % bk == 0
    n_j, n_k = N // bn, T // bk
    n_c = 1 if comm else 0

    def kern(*refs):
        a_ref, b_ref, o_ref, acc_ref = refs[0], refs[1], refs[2 + n_c], refs[3 + 2 * n_c]
        j, k = pl.program_id(0), pl.program_id(1)
        if comm:
            comm_start, comm_mid, comm_finish = comm.phases(refs[2], refs[4], *refs[6:])
            pl.when((j == 0) & (k == 0))(comm_start)

        @pl.when(k == 0)
        def _():
            acc_ref[...] = jnp.zeros(acc_ref.shape, F32)

        bv = b_ref[...]
        if scale is not None:
            bv = bv * scale
        acc_ref[...] += _dot_tn(a_ref[...], bv)

        @pl.when(k == n_k - 1)
        def _():
            acc = acc_ref[...]
            o_ref[...] = (acc.T if transpose_out else acc).astype(BF16)

        if comm:
            pl.when((j == (n_j - 1) // 2) & (k == n_k - 1))(comm_mid)
            pl.when((j == n_j - 1) & (k == n_k - 1))(comm_finish)

    if transpose_out:
        out_shape, out_spec = (N, M), pl.BlockSpec((bn, M), lambda j, k: (j, 0))
    else:
        out_shape, out_spec = (M, N), pl.BlockSpec((M, bn), lambda j, k: (0, j))
    any_spec = pl.BlockSpec(memory_space=pl.ANY)
    res = pl.pallas_call(
        kern,
        out_shape=[jax.ShapeDtypeStruct(out_shape, BF16)] + ([comm.dst] if comm else []),
        grid=(n_j, n_k),
        in_specs=[pl.BlockSpec((bk, M), lambda j, k: (k, 0)), pl.BlockSpec((bk, bn), lambda j, k: (k, j))] + [any_spec] * n_c,
        out_specs=[out_spec] + [any_spec] * n_c,
        scratch_shapes=[pltpu.VMEM((M, bn), F32)] + (list(comm.scratch) if comm else []),
        name=name,
        compiler_params=pltpu.CompilerParams(dimension_semantics=("arbitrary", "arbitrary"), vmem_limit_bytes=VMEM_LIMIT),
    )(a, b, *([comm.src] if comm else []))
    return res if comm else res[0]


PROJ_TM = 512
PROJ_DGRAD_TM = 256
UVZ_W = 2 * GM_WIDTH + SSM_WIDTH
PROJ_KEPT = UVZ_W + LANES
Z_BLK = 2 * GM_WIDTH // SSM_WIDTH
DT_BLK = UVZ_W // LANES


def _mix_in_fwd(h, g, w_in_t, conv_w, conv_b):
    T = h.shape[0]

    def body(i, h_ref, g_ref, cw_ref, cb_ref, w_ref, p_ref, n_ref, x_ref, xc_ref, ext_ref):
        @pl.when(i == 0)
        def _():
            ext_ref[0:HALO, :] = jnp.zeros((HALO, CONV_DIM), F32)

        n = _rms(h_ref[...], g_ref[...]).astype(BF16)
        n_ref[...] = n
        proj = _dot_nt(n, w_ref[...])
        p_ref[:, :UVZ_W] = proj[:, :UVZ_W]
        p_ref[:, UVZ_W:] = proj[:, UVZ_W + CONV_DIM:]
        xbc = proj[:, UVZ_W:UVZ_W + CONV_DIM]
        x_ref[...] = xbc.astype(BF16)
        ext_ref[HALO:, :] = xbc
        xc_ref[...] = _conv_taps(ext_ref, cw_ref[...], cb_ref[...], PROJ_TM)
        ext_ref[0:HALO, :] = ext_ref[PROJ_TM:PROJ_TM + HALO, :]

    return _tiled(body, "mix_in_fwd", T // PROJ_TM, [(h, PROJ_TM, D_MODEL, 0)], [g, conv_w, conv_b], [w_in_t],
                  [(T, PROJ_KEPT, F32, PROJ_TM), (T, D_MODEL, BF16, PROJ_TM), (T, CONV_DIM, BF16, PROJ_TM),
                   (T, CONV_DIM, F32, PROJ_TM)], [],
                  scratch=[pltpu.VMEM((HALO + PROJ_TM, CONV_DIM), F32)])


def _mix_in_dgrad(h, dh_in, dp_uv, dp_zxd, g, w_in_t, comm=None):
    T = h.shape[0]

    def body(i, h_ref, dh_ref, duv_ref, dzxd_ref, g_ref, w_ref, o_ref, dg_ref):
        dn = _dot(duv_ref[...], w_ref[:UV_W, :]) + _dot(dzxd_ref[...], w_ref[UV_W:, :])
        _, rms_vjp = jax.vjp(_rms, h_ref[...], g_ref[...])
        dx, dg = rms_vjp(dn)
        o_ref[...] = dh_ref[...] + dx
        dg_ref[...] += dg

    return _tiled(body, "mix_in_dgrad", T // PROJ_DGRAD_TM,
                  [(h, PROJ_DGRAD_TM, D_MODEL, 0), (dh_in, PROJ_DGRAD_TM, D_MODEL, 0), (dp_uv, PROJ_DGRAD_TM, UV_W, 0),
                   (dp_zxd, PROJ_DGRAD_TM, ZXD_W, 0)], [g], [w_in_t],
                  [(T, D_MODEL, F32, PROJ_DGRAD_TM)], [((1, D_MODEL), F32)], comm=comm)


def _out_proj_dgrad(dh, w_out):
    T = dh.shape[0]

    def body(i, dh_ref, w_ref, dya_ref, dyb_ref):
        d = dh_ref[...].astype(BF16)
        dya_ref[...] = _dot_nt(d, w_ref[:GM_WIDTH, :])
        dyb_ref[...] = _dot_nt(d, w_ref[GM_WIDTH:, :])

    return _tiled(body, "out_proj_dgrad", T // PROJ_TM, [(dh, PROJ_TM, D_MODEL, 0)], [], [w_out],
                  [(T, GM_WIDTH, F32, PROJ_TM), (T, SSM_WIDTH, F32, PROJ_TM)], [])


def _gm_chunk(u, v, ln_g, ln_b, b_st, out_g, *w_heads):
    ug = _gelu(u)
    vg = _gelu(v)
    mu = jnp.mean(vg, axis=-1, keepdims=True)
    xc = vg - mu
    vn = xc * lax.rsqrt(jnp.mean(xc * xc, axis=-1, keepdims=True) + EPS) * ln_g + ln_b
    t_idx = lax.broadcasted_iota(jnp.int32, (CHUNK, CHUNK), 0)
    s_idx = lax.broadcasted_iota(jnp.int32, (CHUNK, CHUNK), 1)
    causal = t_idx >= s_idx
    mixed = []
    for hd in range(GM_HEADS):
        wm = jnp.where(causal, w_heads[hd], 0.0)
        cols = slice(hd * GM_HEAD_DIM, (hd + 1) * GM_HEAD_DIM)
        mixed.append(_dot(wm, vn[:, cols]) + b_st[:, hd:hd + 1])
    ya0 = ug * jnp.concatenate(mixed, axis=1)
    return _rms(ya0, out_g)


GM_FWD_CHUNKS = 2


def _gm_fwd(proj, ln_g, ln_b, w_s, b_st, out_g):
    T = proj.shape[0]

    rows = GM_FWD_CHUNKS * CHUNK

    def body(i, u_ref, v_ref, lg_ref, lb_ref, w_ref, bs_ref, og_ref, ya_ref):
        w_heads = [w_ref[hd] for hd in range(GM_HEADS)]
        for c in range(GM_FWD_CHUNKS):
            tok = pl.ds(c * CHUNK, CHUNK)
            ya = _gm_chunk(u_ref[tok, :], v_ref[tok, :], lg_ref[...], lb_ref[...], bs_ref[...], og_ref[...], *w_heads)
            ya_ref[tok, :] = ya.astype(BF16)

    return _tiled(body, "gmlp_fwd", T // rows, [(proj, rows, GM_WIDTH, 0), (proj, rows, GM_WIDTH, 1)],
                  [ln_g, ln_b, w_s, b_st, out_g], [], [(T, GM_WIDTH, BF16, rows)], [])[0]


def _gm_bwd(proj, dya, ln_g, ln_b, w_s, b_st, out_g):
    T = proj.shape[0]

    def body(i, u_ref, v_ref, dy_ref, lg_ref, lb_ref, w_ref, bs_ref, og_ref, duv_ref, dlg_ref, dlb_ref, dw_ref, dbs_ref,
             dog_ref):
        w_heads = [w_ref[hd] for hd in range(GM_HEADS)]
        _, vjp = jax.vjp(_gm_chunk, u_ref[...], v_ref[...], lg_ref[...], lb_ref[...], bs_ref[...], og_ref[...], *w_heads)
        grads = vjp(dy_ref[...])
        duv_ref[:, :GM_WIDTH] = grads[0].astype(BF16)
        duv_ref[:, GM_WIDTH:] = grads[1].astype(BF16)
        dlg_ref[...] += grads[2]
        dlb_ref[...] += grads[3]
        dbs_ref[...] += grads[4]
        dog_ref[...] += grads[5]
        for hd in range(GM_HEADS):
            dw_ref[hd] += grads[6 + hd]

    return _tiled(body, "gmlp_bwd", T // CHUNK,
                  [(proj, CHUNK, GM_WIDTH, 0), (proj, CHUNK, GM_WIDTH, 1), (dya, CHUNK, GM_WIDTH, 0)],
                  [ln_g, ln_b, w_s, b_st, out_g], [], [(T, UV_W, BF16, CHUNK)],
                  [((1, GM_WIDTH), F32), ((1, GM_WIDTH), F32), ((GM_HEADS, CHUNK, CHUNK), F32),
                   ((CHUNK, GM_HEADS), F32), ((1, GM_WIDTH), F32)])


def _ssd_chunk(xc, z, dtr, s_in, dt_bias, a_log, d_skip, norm_g):
    half = SSM_WIDTH // SSM_GROUPS
    l_idx = lax.broadcasted_iota(jnp.int32, (CHUNK, CHUNK), 0)
    s_idx = lax.broadcasted_iota(jnp.int32, (CHUNK, CHUNK), 1)
    causal = l_idx >= s_idx
    head_of_col = lax.broadcasted_iota(jnp.int32, (SSM_HEADS, SSM_WIDTH), 1) // SSM_HEAD_DIM
    expand = (head_of_col == lax.broadcasted_iota(jnp.int32, (SSM_HEADS, SSM_WIDTH), 0)).astype(BF16)

    xcs = _silu(xc)
    xs = xcs[:, :SSM_WIDTH]
    dt = jax.nn.softplus(dtr + dt_bias)
    adt = dt * (-jnp.exp(a_log))
    acs = _cumsum_rows(adt, causal.astype(BF16))
    acs_t = _cumsum_cols(adt, (l_idx <= s_idx).astype(BF16))
    tot = acs[CHUNK - 1:CHUNK, :]
    dt_w = _widen(dt, expand)
    out_decay_w = _widen(jnp.exp(acs), expand)
    state_decay_w = _widen(jnp.exp(tot - acs), expand)
    chunk_decay_w = _widen(jnp.exp(tot), expand)
    d_skip_w = _widen(d_skip, expand)
    xdt = xs * dt_w
    xdt_decayed = xdt * state_decay_w

    y_diag, y_off, states = [], [], []
    for grp in range(SSM_GROUPS):
        b0 = SSM_WIDTH + grp * SSM_STATE
        c0 = SSM_WIDTH + SSM_GROUPS * SSM_STATE + grp * SSM_STATE
        bm = xcs[:, b0:b0 + SSM_STATE].astype(BF16)
        cm = xcs[:, c0:c0 + SSM_STATE].astype(BF16)
        cb = _dot_nt(cm, bm)
        for k in range(grp * SSM_HEADS // SSM_GROUPS, (grp + 1) * SSM_HEADS // SSM_GROUPS):
            decay = jnp.exp(jnp.where(causal, acs[:, k:k + 1] - acs_t[k:k + 1, :], -jnp.inf))
            y_diag.append(_dot(cb * decay, xdt[:, k * SSM_HEAD_DIM:(k + 1) * SSM_HEAD_DIM]))
        cols = slice(grp * half, (grp + 1) * half)
        states.append(_dot_tn(bm, xdt_decayed[:, cols]))
        y_off.append(_dot(cm, s_in[:, cols]))
    y = jnp.concatenate(y_diag, axis=1) + jnp.concatenate(y_off, axis=1) * out_decay_w + xs * d_skip_w
    s_out = s_in * chunk_decay_w + jnp.concatenate(states, axis=1)
    y = y * _silu(z)
    normed = []
    for grp in range(SSM_GROUPS):
        yg = y[:, grp * half:(grp + 1) * half]
        normed.append(yg * lax.rsqrt(jnp.mean(yg * yg, axis=-1, keepdims=True) + EPS))
    return jnp.concatenate(normed, axis=1) * norm_g, s_out


def _sum_row_tiles(x):
    return x.reshape(x.shape[0] // F32_ROWS, F32_ROWS, x.shape[1]).sum(axis=0)


def _conv_taps(ext_ref, w, b, rows):
    y = b
    for k in range(SSM_CONV):
        y = y + w[k:k + 1, :] * ext_ref[pl.ds(HALO - (SSM_CONV - 1) + k, rows), :]
    return y


def _ssd_fwd(proj, xc, dt_bias, a_log, d_skip, norm_g, comm=None):
    T = proj.shape[0]
    n_chunks = T // CHUNK

    def body(i, z_ref, xc_ref, dt_ref, dtb_ref, al_ref, dsk_ref, ng_ref, yb_ref, sin_ref, st_ref):
        @pl.when(i == 0)
        def _():
            st_ref[...] = jnp.zeros(st_ref.shape, F32)

        s_in = st_ref[...]
        yb, s_out = _ssd_chunk(xc_ref[...], z_ref[...], dt_ref[:, 0:SSM_HEADS], s_in, dtb_ref[...], al_ref[...],
                               dsk_ref[...], ng_ref[...])
        yb_ref[...] = yb.astype(BF16)
        sin_ref[...] = s_in
        st_ref[...] = s_out

    return _tiled(body, "ssd_fwd", n_chunks,
                  [(proj, CHUNK, SSM_WIDTH, Z_BLK), (xc, CHUNK, CONV_DIM, 0), (proj, CHUNK, LANES, DT_BLK)],
                  [dt_bias, a_log, d_skip, norm_g], [],
                  [(T, SSM_WIDTH, BF16, CHUNK), (n_chunks * SSM_STATE, SSM_WIDTH, F32, SSM_STATE)], [],
                  scratch=[pltpu.VMEM((SSM_STATE, SSM_WIDTH), F32)], comm=comm)


def _ssd_bwd(proj, x16, xc, dyb, s_all, conv_w, dt_bias, a_log, d_skip, norm_g, comm=None):
    T = proj.shape[0]
    n_chunks = T // CHUNK

    def body(i, z_ref, x_ref, xc_ref, dt_ref, dy_ref, sin_ref, cw_ref, dtb_ref, al_ref, dsk_ref, ng_ref,
             dzxd_ref, dcw_ref, dcb_ref, ddtb_ref, dal_ref, ddsk_ref, dng_ref, dext_ref, dst_ref, cw_acc, cb_acc):
        @pl.when(i == n_chunks - 1)
        def _():
            dext_ref[CHUNK:, :] = jnp.zeros((HALO, CONV_DIM), F32)
            dst_ref[...] = jnp.zeros(dst_ref.shape, F32)
            cw_acc[...] = jnp.zeros(cw_acc.shape, F32)
            cb_acc[...] = jnp.zeros(cb_acc.shape, F32)

        _, vjp = jax.vjp(_ssd_chunk, xc_ref[...], z_ref[...], dt_ref[:, 0:SSM_HEADS], sin_ref[...], dtb_ref[...], al_ref[...],
                         dsk_ref[...], ng_ref[...])
        dxc, dz, ddtr, ds_in, ddtb, dal, ddsk, dng = vjp((dy_ref[...], dst_ref[...]))
        dst_ref[...] = ds_in
        ddtb_ref[...] += ddtb
        dal_ref[...] += dal
        ddsk_ref[...] += ddsk
        dng_ref[...] += dng
        dext_ref[0:CHUNK, :] = dxc
        cw = cw_ref[...]
        x = x_ref[...].astype(F32)
        dx = jnp.zeros((CHUNK, CONV_DIM), F32)
        for k in range(SSM_CONV):
            shifted = dext_ref[pl.ds(SSM_CONV - 1 - k, CHUNK), :]
            dx = dx + cw[k:k + 1, :] * shifted
            cw_acc[k] += _sum_row_tiles(shifted * x)
        cb_acc[...] += _sum_row_tiles(dxc)

        @pl.when(i == 0)
        def _():
            dcw_ref[...] = jnp.sum(cw_acc[...], axis=1)
            dcb_ref[...] = jnp.sum(cb_acc[...], axis=0, keepdims=True)

        dext_ref[CHUNK:, :] = dext_ref[0:HALO, :]
        dzxd_ref[:, 0:SSM_WIDTH] = dz.astype(BF16)
        dzxd_ref[:, SSM_WIDTH:SSM_WIDTH + CONV_DIM] = dx.astype(BF16)
        dzxd_ref[:, SSM_WIDTH + CONV_DIM:] = jnp.concatenate(
            [ddtr, jnp.zeros((CHUNK, LANES - SSM_HEADS), F32)], axis=1).astype(BF16)

    return _tiled(body, "ssd_bwd", n_chunks,
                  [(proj, CHUNK, SSM_WIDTH, Z_BLK), (x16, CHUNK, CONV_DIM, 0), (xc, CHUNK, CONV_DIM, 0),
                   (proj, CHUNK, LANES, DT_BLK), (dyb, CHUNK, SSM_WIDTH, 0), (s_all, SSM_STATE, SSM_WIDTH, 0)],
                  [conv_w, dt_bias, a_log, d_skip, norm_g], [],
                  [(T, ZXD_W, BF16, CHUNK)],
                  [((SSM_CONV, CONV_DIM), F32), ((1, CONV_DIM), F32), ((1, SSM_HEADS), F32), ((1, SSM_HEADS), F32),
                   ((1, SSM_HEADS), F32), ((1, SSM_WIDTH), F32)],
                  scratch=[pltpu.VMEM((CHUNK + HALO, CONV_DIM), F32), pltpu.VMEM((SSM_STATE, SSM_WIDTH), F32),
                           pltpu.VMEM((SSM_CONV, F32_ROWS, CONV_DIM), F32), pltpu.VMEM((F32_ROWS, CONV_DIM), F32)],
                  reverse=True, comm=comm)


TAIL_TM = 512


def _tail(h, p, target, ple_norm, w_gate, b_gate, w_proj_t, final_norm):
    T = h.shape[0]

    def head(x, pre, pp, b_g, f_norm, tgt):
        gate = jax.nn.sigmoid(pre + b_g)
        out = _rms(x + gate * pp, f_norm)
        err = out - tgt
        return 0.5 * jnp.sum(jnp.mean(err * err, axis=-1, keepdims=True), axis=0, keepdims=True)

    def body(i, h_ref, p_ref, t_ref, pn_ref, bg_ref, fn_ref, wg_ref, wp_ref, dh_ref, loss_ref, dwg_ref, dwp_ref, dpn_ref,
             dbg_ref, dfn_ref):
        x = h_ref[...]
        n4f, n_vjp = jax.vjp(_rms, x, pn_ref[...])
        n4 = n4f.astype(BF16)
        pre = jnp.dot(n4, wg_ref[...], preferred_element_type=F32)
        p16 = p_ref[...].astype(BF16)
        pp = _dot_nt(p16, wp_ref[...])
        loss, h_vjp = jax.vjp(functools.partial(head, tgt=t_ref[...]), x, pre, pp, bg_ref[...], fn_ref[...])
        dx, dpre, dpp, dbg, dfn = h_vjp(jnp.ones((1, 1), F32))
        dpre16 = dpre.astype(BF16)
        dn4 = _dot_nt(dpre16, wg_ref[...])
        dx2, dpn = n_vjp(dn4)
        dh_ref[...] = dx + dx2
        loss_ref[...] += loss
        dwg_ref[...] += _dot_tn(n4, dpre16)
        dwp_ref[...] += _dot_tn(p16, dpp)
        dpn_ref[...] += dpn
        dbg_ref[...] += dbg
        dfn_ref[...] += dfn

    return _tiled(body, "tail", T // TAIL_TM,
                  [(h, TAIL_TM, D_MODEL, 0), (p, TAIL_TM, D_PLE, 0), (target, TAIL_TM, D_MODEL, 0)],
                  [ple_norm, b_gate, final_norm], [w_gate, w_proj_t],
                  [(T, D_MODEL, F32, TAIL_TM)],
                  [((1, 1), F32), ((D_MODEL, D_MODEL), F32), ((D_PLE, D_MODEL), F32), ((1, D_MODEL), F32),
                   ((1, D_MODEL), F32), ((1, D_MODEL), F32)])


def _gather_phases(x_ref, out_ref, send_sems, recv_sems, local_sem):
    mx, my, mc = lax.axis_index("x"), lax.axis_index("y"), lax.axis_index("c")
    me, sibling = (mx, my, mc), (mx, my, 1 - mc)
    chips = [(1 - mx, my), (mx, 1 - my), (1 - mx, 1 - my)]

    def rows(px, py, pc):
        return out_ref.at[4 * px + 2 * py + pc]

    def copy(k, block, to, src=None):
        return pltpu.make_async_remote_copy(
            src_ref=rows(*block) if src is None else src, dst_ref=rows(*block),
            send_sem=send_sems.at[k], recv_sem=recv_sems.at[k], device_id=to, device_id_type=MESH)

    mine = pltpu.make_async_copy(x_ref, rows(*me), local_sem)
    first = [copy(0, me, sibling, src=x_ref)] + [copy(1 + j, me, (*chip, mc), src=x_ref) for j, chip in enumerate(chips)]
    passed = [copy(4 + j, (*chip, mc), sibling) for j, chip in enumerate(chips)]

    def start():
        mine.start()
        for cp in first:
            cp.start()

    def mid():
        for j, chip in enumerate(chips):
            copy(1 + j, (*chip, mc), me).wait_recv()
            passed[j].start()

    def finish():
        copy(0, sibling, me).wait_recv()
        for j, chip in enumerate(chips):
            copy(4 + j, (*chip, 1 - mc), me).wait_recv()
        for cp in first + passed:
            cp.wait_send()
        mine.wait()

    return start, mid, finish


def _exchange_phases(x_ref, out_ref, send_sems, recv_sems, local_sem):
    mx, my, mc = lax.axis_index("x"), lax.axis_index("y"), lax.axis_index("c")
    me = 4 * mx + 2 * my + mc
    mine = pltpu.make_async_copy(x_ref.at[me], out_ref.at[me], local_sem)
    copies = []
    for k in range(1, N_DEV):
        px = 1 - mx if k & 4 else mx
        py = 1 - my if k & 2 else my
        pc = 1 - mc if k & 1 else mc
        copies.append(pltpu.make_async_remote_copy(
            src_ref=x_ref.at[4 * px + 2 * py + pc], dst_ref=out_ref.at[me], send_sem=send_sems.at[k - 1],
            recv_sem=recv_sems.at[k - 1], device_id=(px, py, pc), device_id_type=MESH))

    def start():
        mine.start()
        for cp in copies:
            cp.start()

    def finish():
        for cp in copies:
            cp.wait_recv()
        for cp in copies:
            cp.wait_send()
        mine.wait()

    return start, lambda: None, finish


def _chip_exchange_phases(x_ref, out_ref, mine, recv, sums, load_sems, pair_send, pair_recv, chip_send, chip_recv, out_sem):
    mx, my, mc = lax.axis_index("x"), lax.axis_index("y"), lax.axis_index("c")
    my_chip = 2 * mx + my
    load = [pltpu.make_async_copy(x_ref.at[2 * q + mc], mine.at[q], load_sems.at[q]) for q in range(N_CHIPS)]
    to_sibling = [pltpu.make_async_remote_copy(
        src_ref=x_ref.at[2 * q + 1 - mc], dst_ref=recv.at[q], send_sem=pair_send.at[q], recv_sem=pair_recv.at[q],
        device_id=(mx, my, 1 - mc), device_id_type=MESH) for q in range(N_CHIPS)]
    to_chips = []
    for k in range(1, N_CHIPS):
        px = 1 - mx if k & 2 else mx
        py = 1 - my if k & 1 else my
        to_chips.append(pltpu.make_async_remote_copy(
            src_ref=sums.at[2 * px + py], dst_ref=out_ref.at[my_chip], send_sem=chip_send.at[k - 1],
            recv_sem=chip_recv.at[k - 1], device_id=(px, py, mc), device_id_type=MESH))
    keep = pltpu.make_async_copy(sums.at[my_chip], out_ref.at[my_chip], out_sem)

    def start():
        for cp in load + to_sibling:
            cp.start()

    def mid():
        for cp in load:
            cp.wait()
        for cp in to_sibling:
            cp.wait_recv()
        for q in range(N_CHIPS):
            sums[q] = (mine[q].astype(F32) + recv[q].astype(F32)).astype(sums.dtype)
        for cp in to_chips + [keep]:
            cp.start()

    def finish():
        for cp in to_chips:
            cp.wait_recv()
        for cp in to_chips + to_sibling:
            cp.wait_send()
        keep.wait()

    return start, mid, finish


FLAT_SCRATCH = (pltpu.SemaphoreType.DMA((N_DEV - 1,)), pltpu.SemaphoreType.DMA((N_DEV - 1,)), pltpu.SemaphoreType.DMA)


def _gather_comm(x):
    return _Comm(_gather_phases, x, jax.ShapeDtypeStruct((N_DEV,) + x.shape, x.dtype), FLAT_SCRATCH)


def _exchange_comm(x):
    return _Comm(_exchange_phases, x, jax.ShapeDtypeStruct(x.shape, x.dtype), FLAT_SCRATCH)


def _chip_exchange_comm(x):
    stage = pltpu.VMEM((N_CHIPS,) + x.shape[1:], x.dtype)
    sems = [pltpu.SemaphoreType.DMA((n,)) for n in (N_CHIPS, N_CHIPS, N_CHIPS, N_CHIPS - 1, N_CHIPS - 1)]
    return _Comm(_chip_exchange_phases, x, jax.ShapeDtypeStruct((N_CHIPS,) + x.shape[1:], x.dtype),
                 (stage, stage, stage, *sems, pltpu.SemaphoreType.DMA))


def _comm_alone(comms, name):
    n = len(comms)

    def body(*refs):
        phases, first = [], 2 * n
        for k, comm in enumerate(comms):
            phases.append(comm.phases(refs[k], refs[n + k], *refs[first:first + len(comm.scratch)]))
            first += len(comm.scratch)
        for step in range(3):
            for phase in phases:
                phase[step]()

    any_spec = pl.BlockSpec(memory_space=pl.ANY)
    return pl.pallas_call(
        body,
        out_shape=[comm.dst for comm in comms],
        in_specs=[any_spec] * n,
        out_specs=[any_spec] * n,
        scratch_shapes=[shape for comm in comms for shape in comm.scratch],
        name=name,
        compiler_params=pltpu.CompilerParams(vmem_limit_bytes=VMEM_LIMIT),
    )(*[comm.src for comm in comms])


def _sum_parts(p_ref):
    g = p_ref[0].astype(F32)
    for j in range(1, p_ref.shape[0]):
        g = g + p_ref[j].astype(F32)
    return g


def _adamw_store(g, w_ref, m_ref, v_ref, g_ref, d_ref, nm_ref, nv_ref):
    m_new = ADAM_B1 * m_ref[...] + (1.0 - ADAM_B1) * g
    v_new = ADAM_B2 * v_ref[...] + (1.0 - ADAM_B2) * jnp.square(g)
    m_hat = m_new / (1.0 - ADAM_B1 ** ADAM_STEP)
    v_hat = v_new / (1.0 - ADAM_B2 ** ADAM_STEP)
    g_ref[...] = g
    d_ref[...] = -ADAM_LR * (m_hat / (jnp.sqrt(v_hat) + ADAM_EPS) + ADAM_WD * w_ref[...])
    nm_ref[...] = m_new
    nv_ref[...] = v_new


def _adamw_shard(parts, off, w, m, v, name, n_tiles):
    _, rows, c = w.shape
    assert c == PACK_COLS
    by_rows = rows % BF16_ROWS == 0
    if by_rows:
        tr = rows // n_tiles
        window = (parts.shape[0], tr, PACK_COLS)
        spec = pl.BlockSpec((None, tr, PACK_COLS), lambda i: (0, i, 0))
    else:
        padded, tc = -(-rows // BF16_ROWS) * BF16_ROWS, PACK_COLS // n_tiles
        window = (parts.shape[0], padded, tc)
        spec = pl.BlockSpec((None, rows, tc), lambda i: (0, 0, i))

    def kern(p_hbm, w_ref, m_ref, v_ref, g_ref, d_ref, nm_ref, nv_ref, buf, sem):
        i = pl.program_id(0)
        if by_rows:
            src = p_hbm.at[:, pl.ds(pl.multiple_of(off + i * tr, BF16_ROWS), tr), :]
        else:
            src = p_hbm.at[:, pl.ds(off, padded), pl.ds(pl.multiple_of(i * tc, LANES), tc)]
        cp = pltpu.make_async_copy(src, buf, sem)
        cp.start()
        cp.wait()
        g = _sum_parts(buf)
        if not by_rows:
            keep = lax.broadcasted_iota(jnp.int32, (rows, padded), 0) == lax.broadcasted_iota(jnp.int32, (rows, padded), 1)
            g = _exact_dot(g, keep.astype(BF16), ((1,), (0,)), x_first=False)
        _adamw_store(g, w_ref, m_ref, v_ref, g_ref, d_ref, nm_ref, nv_ref)

    return pl.pallas_call(
        kern,
        out_shape=[jax.ShapeDtypeStruct(w.shape, F32)] * 4,
        grid=(n_tiles,),
        in_specs=[pl.BlockSpec(memory_space=pl.ANY), spec, spec, spec],
        out_specs=[spec] * 4,
        scratch_shapes=[pltpu.VMEM(window, parts.dtype), pltpu.SemaphoreType.DMA],
        name=name,
        compiler_params=pltpu.CompilerParams(dimension_semantics=("arbitrary",), vmem_limit_bytes=VMEM_LIMIT),
    )(parts, w, m, v)


def _sum_adamw(parts, w, m, v, tr, name):
    _, R, C = parts.shape

    def kern(p_ref, w_ref, m_ref, v_ref, g_ref, d_ref, nm_ref, nv_ref):
        _adamw_store(_sum_parts(p_ref), w_ref, m_ref, v_ref, g_ref, d_ref, nm_ref, nv_ref)

    row_spec = pl.BlockSpec((tr, C), lambda i: (i, 0))
    return pl.pallas_call(
        kern,
        out_shape=[jax.ShapeDtypeStruct((R, C), F32)] * 4,
        grid=(R // tr,),
        in_specs=[pl.BlockSpec((N_DEV, tr, C), lambda i: (0, i, 0)), row_spec, row_spec, row_spec],
        out_specs=[row_spec] * 4,
        name=name,
        compiler_params=pltpu.CompilerParams(dimension_semantics=("arbitrary",), vmem_limit_bytes=VMEM_LIMIT),
    )(parts, w, m, v)


FF_SHARD = D_FF // N_DEV
CONV_SHARD = (SSM_CONV, CONV_DIM // N_DEV)
SHARDS = {"ffn1_w_gate": ((D_MODEL, FF_SHARD), True), "ffn1_w_up": ((D_MODEL, FF_SHARD), True),
          "ffn1_w_down": ((FF_SHARD, D_MODEL), False),
          "ffn2_w_gate": ((D_MODEL, FF_SHARD), True), "ffn2_w_up": ((D_MODEL, FF_SHARD), True),
          "ffn2_w_down": ((FF_SHARD, D_MODEL), False),
          "w_out": ((2 * D_MODEL // N_DEV, D_MODEL), False), "ple_w_gate": ((D_MODEL // N_DEV, D_MODEL), False),
          "w_in": ((D_MODEL, IN_PROJ // N_DEV), True), "ple_w_proj": ((D_PLE, D_MODEL // N_DEV), True),
          "conv_w": (CONV_SHARD, True),
          "conv_w_mid": (CONV_SHARD, True), "conv_w_low": (CONV_SHARD, True)}
BIG = tuple(name for name in SHARDS if not name.startswith("conv_w_"))
SMALL = ("ffn1_norm", "mix_norm", "gm_ln_g", "gm_ln_b", "gm_w_s", "gm_b_s", "gm_out_norm", "conv_b", "dt_bias", "a_log",
         "d_skip", "ssm_norm", "ffn2_norm", "ple_norm", "ple_b_gate", "final_norm")
SMALL_ROWS = 144


def _piece_rows(name):
    shape = SHARDS[name][0]
    return -(-(shape[0] * shape[1]) // PACK_COLS)


def _pad_cols(flat, name):
    pad = _piece_rows(name) * PACK_COLS - flat.shape[-1]
    return flat if pad == 0 else jnp.pad(flat, [(0, 0)] * (flat.ndim - 1) + [(0, pad)])


class _Pack:
    def __init__(self, names, tile_rows):
        self.names, self.tile_rows, self.offsets, off = names, tile_rows, {}, 0
        for name in names:
            self.offsets[name] = off
            off += _piece_rows(name)
        self.rows = -(-off // tile_rows) * tile_rows

    def pack_local(self, vals):
        parts = []
        for name in self.names:
            val = vals[name]
            parts.append(_pad_cols((val.T if SHARDS[name][1] else val).reshape(-1), name))
        flat = jnp.concatenate(parts)
        return jnp.pad(flat, (0, self.rows * PACK_COLS - flat.shape[0])).reshape(self.rows, PACK_COLS)

    def pack_owner_major(self, grads):
        parts, rows = [], 0
        for name in self.names:
            grad, piece_rows = grads[name].astype(BF16), _piece_rows(name)
            if grad.shape != (N_DEV * piece_rows, PACK_COLS):
                grad = _pad_cols(grad.reshape(N_DEV, -1), name)
            parts.append(grad.reshape(N_DEV, piece_rows, PACK_COLS))
            rows += piece_rows
        if rows < self.rows:
            parts.append(jnp.zeros((N_DEV, self.rows - rows, PACK_COLS), BF16))
        return parts[0] if len(parts) == 1 else jnp.concatenate(parts, axis=1)

    def gathered_piece(self, gathered, name):
        shape = SHARDS[name][0]
        rows = gathered[:, self.offsets[name]:self.offsets[name] + _piece_rows(name), :]
        return rows.reshape(N_DEV, -1)[:, :shape[0] * shape[1]]

    def pieces(self, gathered, name):
        return _Pieces(gathered, self.offsets[name], _piece_rows(name))


GATHER_FFN1 = _Pack(("ffn1_w_gate", "ffn1_w_up", "ffn1_w_down"), BF16_ROWS)
GATHER_MIX = _Pack(("w_out", "ple_w_gate", "w_in", "ple_w_proj", "conv_w", "conv_w_mid", "conv_w_low"), BF16_ROWS)
GATHER_FFN2 = _Pack(("ffn2_w_gate", "ffn2_w_up", "ffn2_w_down"), BF16_ROWS)
SCATTER_LATE = _Pack(("ffn2_w_gate", "ffn2_w_up", "ffn2_w_down", "w_out", "ple_w_gate", "ple_w_proj"), BF16_ROWS)
SCATTER_IN = _Pack(("w_in", "conv_w"), BF16_ROWS)
SCATTER_GATE = _Pack(("ffn1_w_gate",), BF16_ROWS)
SCATTER_UP = _Pack(("ffn1_w_up",), BF16_ROWS)
SCATTER_DOWN = _Pack(("ffn1_w_down",), BF16_ROWS)


def _pack_small(vals, behind=()):
    flat = jnp.concatenate([vals[name].reshape(-1).astype(F32) for name in SMALL] + [b.reshape(-1) for b in behind])
    return jnp.pad(flat, (0, SMALL_ROWS * PACK_COLS - flat.shape[0])).reshape(SMALL_ROWS, PACK_COLS)


def _unpack_small(packed, shapes):
    out, off = {}, 0
    flat = packed.reshape(-1)
    for name in SMALL:
        n = 1
        for s in shapes[name]:
            n *= s
        out[name] = flat[off:off + n].reshape(shapes[name])
        off += n
    return out


WEIGHTS = ("ffn1_norm", "ffn1_w_gate", "ffn1_w_up", "ffn1_w_down", "mix_norm", "w_in", "gm_ln_g", "gm_ln_b", "gm_w_s",
           "gm_b_s", "gm_out_norm", "conv_w", "conv_b", "dt_bias", "a_log", "d_skip", "ssm_norm", "w_out", "ffn2_norm",
           "ffn2_w_gate", "ffn2_w_up", "ffn2_w_down", "ple_norm", "ple_w_gate", "ple_b_gate", "ple_w_proj", "final_norm")


def _step(x, p, target, w, m, v):
    local = lambda d: {name: d[name][0] for name in BIG}

    shards = {name: val.astype(BF16) for name, val in local(w).items()}
    conv_high = lax.reduce_precision(w["conv_w"][0], 8, 7)
    conv_mid = lax.reduce_precision(w["conv_w"][0] - conv_high, 8, 7)
    shards["conv_w"] = conv_high.astype(BF16)
    shards["conv_w_mid"] = conv_mid.astype(BF16)
    shards["conv_w_low"] = (w["conv_w"][0] - conv_high - conv_mid).astype(BF16)
    g_ffn1 = _comm_alone([_gather_comm(GATHER_FFN1.pack_local(shards))], "gather_ffn1")[0]

    row = lambda name: w[name].reshape(1, -1)
    gm_w_s = w["gm_w_s"][0]
    gm_b_st = jnp.transpose(w["gm_b_s"][0])
    ffn1 = (row("ffn1_norm"),) + tuple(GATHER_FFN1.pieces(g_ffn1, name) for name in GATHER_FFN1.names)
    gm = (row("gm_ln_g"), row("gm_ln_b"), gm_w_s, gm_b_st, row("gm_out_norm"))

    h1, n1, a1, b1, s1, g_mix = _ffn_fwd(x, *ffn1, "ffn1_fwd", comm=_gather_comm(GATHER_MIX.pack_local(shards)))
    w_in_t = GATHER_MIX.gathered_piece(g_mix, "w_in").reshape(IN_PROJ, D_MODEL)
    w_in_t = jnp.concatenate([w_in_t, jnp.zeros((IN_PROJ_PAD - IN_PROJ, D_MODEL), BF16)], axis=0)
    w_proj_t = GATHER_MIX.gathered_piece(g_mix, "ple_w_proj").reshape(D_MODEL, D_PLE)
    conv_w = sum(GATHER_MIX.gathered_piece(g_mix, name).astype(F32) for name in ("conv_w", "conv_w_mid", "conv_w_low"))
    conv_w = conv_w.reshape(CONV_DIM, SSM_CONV).T
    ssd = (row("dt_bias"), row("a_log"), row("d_skip"), row("ssm_norm"))
    w_out = GATHER_MIX.pieces(g_mix, "w_out")

    proj, n2, x16, xc = _mix_in_fwd(h1, row("mix_norm"), w_in_t, conv_w, row("conv_b"))
    ya = _gm_fwd(proj, *gm)
    yb, s_all, g_ffn2 = _ssd_fwd(proj, xc, *ssd, comm=_gather_comm(GATHER_FFN2.pack_local(shards)))
    ffn2 = (row("ffn2_norm"),) + tuple(GATHER_FFN2.pieces(g_ffn2, name) for name in GATHER_FFN2.names)
    h3, n3, a3, b3, s3, h2 = _ffn_fwd(h1, *ffn2, "ffn2_fwd", mixed=(ya, yb, w_out))

    g, gp = {}, {}
    dh3, loss, gp["ple_w_gate"], d_w_proj, g["ple_norm"], g["ple_b_gate"], g["final_norm"] = _tail(
        h3, p, target, row("ple_norm"), GATHER_MIX.pieces(g_mix, "ple_w_gate"), row("ple_b_gate"), w_proj_t,
        row("final_norm"))
    gp["ple_w_proj"] = d_w_proj.T

    dh2, da3, db3, g["ffn2_norm"] = _ffn_dgrad(h2, dh3, a3, b3, *ffn2, "ffn2_dgrad")
    gp["ffn2_w_gate"] = _wgrad(n3, da3, FF_BN, "ffn2_wgrad_gate", transpose_out=True)
    gp["ffn2_w_up"] = _wgrad(n3, db3, FF_BN, "ffn2_wgrad_up", transpose_out=True)
    gp["ffn2_w_down"] = _wgrad(s3, dh3, DOWN_BN, "ffn2_wgrad_down", scale=0.5, bk=DOWN_BK)

    dya, dyb = _out_proj_dgrad(dh2, w_out)
    gp["w_out"] = jnp.concatenate([_wgrad(ya, dh2, SQUARE_BN, "w_out_wgrad_a"), _wgrad(yb, dh2, SQUARE_BN, "w_out_wgrad_b")], axis=0)

    dp_zxd, d_conv_w, g["conv_b"], g["dt_bias"], g["a_log"], g["d_skip"], g["ssm_norm"], parts_late = _ssd_bwd(
        proj, x16, xc, dyb, s_all, conv_w, *ssd, comm=_exchange_comm(SCATTER_LATE.pack_owner_major(gp)))
    gp["conv_w"] = d_conv_w.T
    dp_uv, g["gm_ln_g"], g["gm_ln_b"], g["gm_w_s"], dbst, g["gm_out_norm"] = _gm_bwd(proj, dya, *gm)
    g["gm_b_s"] = jnp.transpose(dbst)

    parts = {}
    gp["w_in"] = jnp.concatenate([_wgrad(n2, dp_uv, SQUARE_BN, "w_in_wgrad_uv", transpose_out=True),
                                  _wgrad(n2, dp_zxd, ZXD_BN, "w_in_wgrad_zxd", transpose_out=True)], axis=0)[:IN_PROJ]
    dh1, g["mix_norm"], parts[SCATTER_IN] = _mix_in_dgrad(h1, dh2, dp_uv, dp_zxd, row("mix_norm"), w_in_t,
                                                          comm=_exchange_comm(SCATTER_IN.pack_owner_major(gp)))

    dx, da1, db1, g["ffn1_norm"] = _ffn_dgrad(x, dh1, a1, b1, *ffn1, "ffn1_dgrad")
    gp["ffn1_w_gate"], small_parts = _wgrad(n1, da1, FF_BN, "ffn1_wgrad_gate", transpose_out=True,
                                            comm=_gather_comm(_pack_small(g, behind=[loss])))
    gp["ffn1_w_up"], parts[SCATTER_GATE] = _wgrad(n1, db1, FF_BN, "ffn1_wgrad_up", transpose_out=True,
                                                  comm=_chip_exchange_comm(SCATTER_GATE.pack_owner_major(gp)))
    gp["ffn1_w_down"], parts[SCATTER_UP] = _wgrad(s1, dh1, DOWN_BN, "ffn1_wgrad_down", scale=0.5, bk=DOWN_BK,
                                                  comm=_chip_exchange_comm(SCATTER_UP.pack_owner_major(gp)))
    parts[SCATTER_DOWN] = _comm_alone([_chip_exchange_comm(SCATTER_DOWN.pack_owner_major(gp))], "scatter_ffn1_down")[0]
    parts[SCATTER_LATE] = parts_late

    res_big = {}
    for pack, pack_parts in parts.items():
        for name in pack.names:
            shape, transposed = SHARDS[name]
            if name in ("ple_w_proj", "conv_w"):
                nat = pack.gathered_piece(pack_parts, name).reshape((N_DEV,) + shape[::-1])
                res_big[name] = _sum_adamw(jnp.transpose(nat, (0, 2, 1)), w[name][0], m[name][0], v[name][0], shape[0],
                                           "adamw_" + name)
            else:
                flip = (lambda a: jnp.transpose(a, (0, 2, 1))) if transposed else (lambda a: a)
                res = _adamw_shard(pack_parts, pack.offsets[name], flip(w[name]), flip(m[name]), flip(v[name]),
                                   "adamw_" + name, n_tiles=4 if name == "w_in" else 2)
                res_big[name] = [flip(r) for r in res]

    small_shapes = {name: w[name].shape for name in SMALL}
    res_small = _sum_adamw(small_parts, _pack_small(w), _pack_small(m), _pack_small(v), SMALL_ROWS, "adamw_small")
    loss = res_small[0].reshape(-1)[sum(w[name].size for name in SMALL)]
    res_small = [_unpack_small(r, small_shapes) for r in res_small]

    outs = []
    for k in range(4):
        for name in WEIGHTS:
            if name in res_small[k]:
                outs.append(res_small[k][name])
            else:
                outs.append(res_big[name][k].reshape(w[name].shape))
    return loss, dx, outs


def kernel(x, p, ffn1_norm, ffn1_w_gate, ffn1_w_up, ffn1_w_down, mix_norm, w_in, gm_ln_g, gm_ln_b, gm_w_s, gm_b_s, gm_out_norm, conv_w, conv_b, dt_bias, a_log, d_skip, ssm_norm, w_out, ffn2_norm, ffn2_w_gate, ffn2_w_up, ffn2_w_down, ple_norm, ple_w_gate, ple_b_gate, ple_w_proj, final_norm, loss_target, m_ffn1_norm, m_ffn1_w_gate, m_ffn1_w_up, m_ffn1_w_down, m_mix_norm, m_w_in, m_gm_ln_g, m_gm_ln_b, m_gm_w_s, m_gm_b_s, m_gm_out_norm, m_conv_w, m_conv_b, m_dt_bias, m_a_log, m_d_skip, m_ssm_norm, m_w_out, m_ffn2_norm, m_ffn2_w_gate, m_ffn2_w_up, m_ffn2_w_down, m_ple_norm, m_ple_w_gate, m_ple_b_gate, m_ple_w_proj, m_final_norm, v_ffn1_norm, v_ffn1_w_gate, v_ffn1_w_up, v_ffn1_w_down, v_mix_norm, v_w_in, v_gm_ln_g, v_gm_ln_b, v_gm_w_s, v_gm_b_s, v_gm_out_norm, v_conv_w, v_conv_b, v_dt_bias, v_a_log, v_d_skip, v_ssm_norm, v_w_out, v_ffn2_norm, v_ffn2_w_gate, v_ffn2_w_up, v_ffn2_w_down, v_ple_norm, v_ple_w_gate, v_ple_b_gate, v_ple_w_proj, v_final_norm):
    args = locals()
    w = {name: args[name] for name in WEIGHTS}
    m = {name: args["m_" + name] for name in WEIGHTS}
    v = {name: args["v_" + name] for name in WEIGHTS}
    loss, dx, outs = _step(x[0], p[0, 0], loss_target[0], w, m, v)
    return (loss, dx[None], *outs)
```

```python
import functools
from typing import NamedTuple

import jax
import jax.numpy as jnp
from jax import lax
from jax.experimental import pallas as pl
from jax.experimental.pallas import tpu as pltpu

F32 = jnp.float32
BF16 = jnp.bfloat16
MESH = pl.DeviceIdType.MESH
N_DEV = 8
N_CHIPS = 4

D_MODEL = 1024
D_FF = 2816
D_PLE = 256
GM_WIDTH = 1024
GM_HEADS = 8
GM_HEAD_DIM = 128
CHUNK = 128
SSM_WIDTH = 1024
SSM_HEADS = 16
SSM_HEAD_DIM = 64
SSM_GROUPS = 2
SSM_STATE = 128
SSM_CONV = 4
CONV_DIM = SSM_WIDTH + 2 * SSM_GROUPS * SSM_STATE
IN_PROJ = 2 * GM_WIDTH + SSM_WIDTH + CONV_DIM + SSM_HEADS
LANES = 128
BF16_ROWS = 16
F32_ROWS = 8
IN_PROJ_PAD = IN_PROJ - SSM_HEADS + LANES
UV_W = 2 * GM_WIDTH
ZXD_W = IN_PROJ_PAD - UV_W
HALO = 8
EPS = 1e-6

ADAM_LR = 0.001
ADAM_B1 = 0.9
ADAM_B2 = 0.999
ADAM_EPS = 1e-08
ADAM_WD = 0.01
ADAM_STEP = 10

VMEM_LIMIT = 56 * 1024 * 1024
PACK_COLS = 1024


def _rms(x, g):
    return x * lax.rsqrt(jnp.mean(x * x, axis=-1, keepdims=True) + EPS) * g


def _gelu(x):
    return 0.5 * x * (1.0 + lax.erf(x * (2.0 ** -0.5)))


def _silu(x):
    return x * jax.nn.sigmoid(x)


def _dot(a, b):
    return jnp.dot(a.astype(BF16), b.astype(BF16), preferred_element_type=F32)


def _dot_nt(a, b):
    return lax.dot_general(a.astype(BF16), b.astype(BF16), (((1,), (1,)), ((), ())), preferred_element_type=F32)


def _dot_tn(a, b):
    return lax.dot_general(a.astype(BF16), b.astype(BF16), (((0,), (0,)), ((), ())), preferred_element_type=F32)


def _split3(x):
    hi = x.astype(BF16)
    rest = x - hi.astype(F32)
    mid = rest.astype(BF16)
    return hi, mid, (rest - mid.astype(F32)).astype(BF16)


def _exact_dot(x, mask, dims, x_first=True, n_terms=3):
    terms = [lax.dot_general(*((t, mask) if x_first else (mask, t)), (dims, ((), ())), preferred_element_type=F32)
             for t in _split3(x)[:n_terms]]
    total = terms[0]
    for term in terms[1:]:
        total = total + term
    return total


def _mask_product(fwd_dims, fwd_x_first, bwd_dims, bwd_x_first, bwd_terms=3):
    @jax.custom_vjp
    def product(x, mask):
        return _exact_dot(x, mask, fwd_dims, fwd_x_first)

    def fwd(x, mask):
        return product(x, mask), mask

    def bwd(mask, g):
        return _exact_dot(g, mask, bwd_dims, bwd_x_first, bwd_terms), jnp.zeros_like(mask)

    product.defvjp(fwd, bwd)
    return product


_widen = _mask_product(((1,), (0,)), True, ((1,), (1,)), True, bwd_terms=2)
_cumsum_rows = _mask_product(((1,), (0,)), False, ((0,), (0,)), False)
_cumsum_cols = _mask_product(((0,), (0,)), True, ((1,), (1,)), False)


class _Pieces(NamedTuple):
    gathered: jax.Array
    row_off: int
    rows: int


class _Comm(NamedTuple):
    phases: object
    src: jax.Array
    dst: jax.ShapeDtypeStruct
    scratch: tuple


def _tiled(body, name, n_steps, tiled_in, full_in, big_in, tiled_out, acc_out, scratch=(), reverse=False, comm=None):
    n_t, n_f, n_b, n_to, n_a = len(tiled_in), len(full_in), len(big_in), len(tiled_out), len(acc_out)
    n_c = 1 if comm else 0

    def row(i):
        return n_steps - 1 - i if reverse else i

    in_specs, args = [], []
    for arr, br, bc, cb in tiled_in:
        if callable(cb):
            in_specs.append(pl.BlockSpec((br, bc), cb))
        else:
            in_specs.append(pl.BlockSpec((br, bc), functools.partial(lambda i, cb: (row(i), cb), cb=cb)))
        args.append(arr)
    for arr in full_in:
        in_specs.append(pl.BlockSpec(arr.shape, functools.partial(lambda i, nd: (0,) * nd, nd=arr.ndim)))
        args.append(arr)
    big_shapes, n_copies = [], 0
    for big in big_in:
        in_specs.append(pl.BlockSpec(memory_space=pl.ANY))
        if isinstance(big, _Pieces):
            args.append(big.gathered)
            big_shapes.append(((N_DEV * big.rows, PACK_COLS), big.gathered.dtype))
            n_copies += N_DEV
        else:
            args.append(big)
            big_shapes.append((big.shape, big.dtype))
            n_copies += 1
    if comm:
        in_specs.append(pl.BlockSpec(memory_space=pl.ANY))
        args.append(comm.src)
    out_specs, out_shape = [], []
    for rows, cols, dt, br in tiled_out:
        out_specs.append(pl.BlockSpec((br, cols), lambda i: (row(i), 0)))
        out_shape.append(jax.ShapeDtypeStruct((rows, cols), dt))
    for shp, dt in acc_out:
        out_specs.append(pl.BlockSpec(shp, functools.partial(lambda i, nd: (0,) * nd, nd=len(shp))))
        out_shape.append(jax.ShapeDtypeStruct(shp, dt))
    if comm:
        out_specs.append(pl.BlockSpec(memory_space=pl.ANY))
        out_shape.append(comm.dst)
    scratch_shapes = [pltpu.VMEM(shp, dt) for shp, dt in big_shapes] + list(scratch)
    if n_copies:
        scratch_shapes.append(pltpu.SemaphoreType.DMA((n_copies,)))
    if comm:
        scratch_shapes += list(comm.scratch)

    def kern(*refs):
        n_in = n_t + n_f + n_b + n_c
        ins = refs[: n_t + n_f]
        big_hbm = refs[n_t + n_f : n_t + n_f + n_b]
        outs = refs[n_in : n_in + n_to + n_a]
        rest = refs[n_in + n_to + n_a + n_c :]
        big_vmem, scr = rest[:n_b], rest[n_b:]
        if comm:
            scr, comm_scr = scr[:-len(comm.scratch)], scr[-len(comm.scratch):]
            comm_start, comm_mid, comm_finish = comm.phases(refs[n_in - 1], refs[n_in + n_to + n_a], *comm_scr)
        if n_copies:
            scr, copy_sems = scr[:-1], scr[-1]
        step = pl.program_id(0)

        @pl.when(step == 0)
        def _():
            copies = []
            for big, src, dst in zip(big_in, big_hbm, big_vmem):
                if isinstance(big, _Pieces):
                    for j in range(N_DEV):
                        copies.append((src.at[j, pl.ds(big.row_off, big.rows), :], dst.at[pl.ds(j * big.rows, big.rows), :]))
                else:
                    copies.append((src, dst))
            copies = [pltpu.make_async_copy(a, b, copy_sems.at[k]) for k, (a, b) in enumerate(copies)]
            for cp in copies:
                cp.start()
            for cp in copies:
                cp.wait()
            for acc in outs[n_to:]:
                acc[...] = jnp.zeros(acc.shape, acc.dtype)
            if comm:
                comm_start()

        body(row(step), *ins, *big_vmem, *outs, *scr)
        if comm:
            pl.when(step == (n_steps - 1) // 2)(comm_mid)
            pl.when(step == n_steps - 1)(comm_finish)

    res = pl.pallas_call(
        kern,
        out_shape=out_shape,
        grid=(n_steps,),
        in_specs=in_specs,
        out_specs=out_specs,
        scratch_shapes=scratch_shapes,
        name=name,
        compiler_params=pltpu.CompilerParams(dimension_semantics=("arbitrary",), vmem_limit_bytes=VMEM_LIMIT),
    )(*args)
    return res


FWD_CHUNKS = ((0, 1536), (1536, D_FF))
DGRAD_CHUNKS = ((0, 1024), (1024, 2048), (2048, D_FF))
FFN_TM = 256


def _ffn_fwd(h, g, wg_t, wu_t, wd, name, comm=None, mixed=None):
    T = h.shape[0]

    def ffn(x, g_ref, wg_ref, wu_ref, wd_ref, o_ref, n_ref, a_ref, b_ref, s_ref):
        n = _rms(x, g_ref[...]).astype(BF16)
        n_ref[...] = n
        f = jnp.zeros(x.shape, F32)
        for lo, hi in FWD_CHUNKS:
            a = _dot_nt(n, wg_ref[lo:hi, :])
            b = _dot_nt(n, wu_ref[lo:hi, :])
            s = (_silu(a) * b).astype(BF16)
            a_ref[:, lo:hi] = a.astype(BF16)
            b_ref[:, lo:hi] = b.astype(BF16)
            s_ref[:, lo:hi] = s
            f = f + jnp.dot(s, wd_ref[lo:hi, :], preferred_element_type=F32)
        o_ref[...] = x + 0.5 * f

    def body_plain(i, h_ref, *refs):
        ffn(h_ref[...], *refs)

    def body_mixed(i, h_ref, ya_ref, yb_ref, g_ref, wg_ref, wu_ref, wd_ref, wo_ref, o_ref, n_ref, a_ref, b_ref, s_ref, x_ref):
        x = (h_ref[...] + jnp.dot(ya_ref[...], wo_ref[:GM_WIDTH, :], preferred_element_type=F32)
             + jnp.dot(yb_ref[...], wo_ref[GM_WIDTH:, :], preferred_element_type=F32))
        x_ref[...] = x
        ffn(x, g_ref, wg_ref, wu_ref, wd_ref, o_ref, n_ref, a_ref, b_ref, s_ref)

    body = body_mixed if mixed else body_plain
    tiled_in, big_in = [(h, FFN_TM, D_MODEL, 0)], [wg_t, wu_t, wd]
    tiled_out = [(T, D_MODEL, F32, FFN_TM), (T, D_MODEL, BF16, FFN_TM), (T, D_FF, BF16, FFN_TM), (T, D_FF, BF16, FFN_TM),
                 (T, D_FF, BF16, FFN_TM)]
    if mixed:
        tiled_in += [(mixed[0], FFN_TM, GM_WIDTH, 0), (mixed[1], FFN_TM, SSM_WIDTH, 0)]
        big_in.append(mixed[2])
        tiled_out.append((T, D_MODEL, F32, FFN_TM))
    return _tiled(body, name, T // FFN_TM, tiled_in, [g], big_in, tiled_out, [], comm=comm)


def _ffn_dgrad(h, dout, a16, b16, g, wg_t, wu_t, wd, name):
    T = h.shape[0]

    def body(i, h_ref, do_ref, a_ref, b_ref, g_ref, wg_ref, wu_ref, wd_ref, dh_ref, da_ref, db_ref, dg_ref):
        dout = do_ref[...]
        _, rms_vjp = jax.vjp(_rms, h_ref[...], g_ref[...])
        dfo = (0.5 * dout).astype(BF16)
        dn = jnp.zeros(dout.shape, F32)
        for lo, hi in DGRAD_CHUNKS:
            a = a_ref[:, lo:hi].astype(F32)
            b = b_ref[:, lo:hi].astype(F32)
            sg = jax.nn.sigmoid(a)
            ds = _dot_nt(dfo, wd_ref[lo:hi, :])
            db = (ds * (a * sg)).astype(BF16)
            da = (ds * b * (sg * (1.0 + a * (1.0 - sg)))).astype(BF16)
            dn = dn + _dot(da, wg_ref[lo:hi, :]) + _dot(db, wu_ref[lo:hi, :])
            da_ref[:, lo:hi] = da
            db_ref[:, lo:hi] = db
        dx, dg = rms_vjp(dn)
        dh_ref[...] = dout + dx
        dg_ref[...] += dg

    return _tiled(body, name, T // FFN_TM,
                  [(h, FFN_TM, D_MODEL, 0), (dout, FFN_TM, D_MODEL, 0), (a16, FFN_TM, D_FF, 0), (b16, FFN_TM, D_FF, 0)],
                  [g], [wg_t, wu_t, wd],
                  [(T, D_MODEL, F32, FFN_TM), (T, D_FF, BF16, FFN_TM), (T, D_FF, BF16, FFN_TM)], [((1, D_MODEL), F32)])


FF_BN = D_FF // 2
DOWN_BN, DOWN_BK = 512, 1024
SQUARE_BN = 1024
ZXD_BN = ZXD_W // 3


def _wgrad(a, b, bn, name, scale=None, transpose_out=False, bk=2048, comm=None):
    T, M = a.shape
    N = b.shape[1]
    bk = min(bk, T)
    assert M % LANES == 0 and N % bn == 0 and T % bk == 0
    n_j, n_k = N // bn, T // bk
    n_c = 1 if comm else 0

    def kern(*refs):
        a_ref, b_ref, o_ref, acc_ref = refs[0], refs[1], refs[2 + n_c], refs[3 + 2 * n_c]
        j, k = pl.program_id(0), pl.program_id(1)
        if comm:
            comm_start, comm_mid, comm_finish = comm.phases(refs[2], refs[4], *refs[6:])
            pl.when((j == 0) & (k == 0))(comm_start)

        @pl.when(k == 0)
        def _():
            acc_ref[...] = jnp.zeros(acc_ref.shape, F32)

        bv = b_ref[...]
        if scale is not None:
            bv = bv * scale
        acc_ref[...] += _dot_tn(a_ref[...], bv)

        @pl.when(k == n_k - 1)
        def _():
            acc = acc_ref[...]
            o_ref[...] = (acc.T if transpose_out else acc).astype(BF16)

        if comm:
            pl.when((j == (n_j - 1) // 2) & (k == n_k - 1))(comm_mid)
            pl.when((j == n_j - 1) & (k == n_k - 1))(comm_finish)

    if transpose_out:
        out_shape, out_spec = (N, M), pl.BlockSpec((bn, M), lambda j, k: (j, 0))
    else:
        out_shape, out_spec = (M, N), pl.BlockSpec((M, bn), lambda j, k: (0, j))
    any_spec = pl.BlockSpec(memory_space=pl.ANY)
    res = pl.pallas_call(
        kern,
        out_shape=[jax.ShapeDtypeStruct(out_shape, BF16)] + ([comm.dst] if comm else []),
        grid=(n_j, n_k),
        in_specs=[pl.BlockSpec((bk, M), lambda j, k: (k, 0)), pl.BlockSpec((bk, bn), lambda j, k: (k, j))] + [any_spec] * n_c,
        out_specs=[out_spec] + [any_spec] * n_c,
        scratch_shapes=[pltpu.VMEM((M, bn), F32)] + (list(comm.scratch) if comm else []),
        name=name,
        compiler_params=pltpu.CompilerParams(dimension_semantics=("arbitrary", "arbitrary"), vmem_limit_bytes=VMEM_LIMIT),
    )(a, b, *([comm.src] if comm else []))
    return res if comm else res[0]


PROJ_TM = 512
PROJ_DGRAD_TM = 256
UVZ_W = 2 * GM_WIDTH + SSM_WIDTH
PROJ_KEPT = UVZ_W + LANES
Z_BLK = 2 * GM_WIDTH // SSM_WIDTH
DT_BLK = UVZ_W // LANES


def _mix_in_fwd(h, g, w_in_t, conv_w, conv_b):
    T = h.shape[0]

    def body(i, h_ref, g_ref, cw_ref, cb_ref, w_ref, p_ref, n_ref, x_ref, xc_ref, ext_ref):
        @pl.when(i == 0)
        def _():
            ext_ref[0:HALO, :] = jnp.zeros((HALO, CONV_DIM), F32)

        n = _rms(h_ref[...], g_ref[...]).astype(BF16)
        n_ref[...] = n
        proj = _dot_nt(n, w_ref[...])
        p_ref[:, :UVZ_W] = proj[:, :UVZ_W]
        p_ref[:, UVZ_W:] = proj[:, UVZ_W + CONV_DIM:]
        xbc = proj[:, UVZ_W:UVZ_W + CONV_DIM]
        x_ref[...] = xbc.astype(BF16)
        ext_ref[HALO:, :] = xbc
        xc_ref[...] = _conv_taps(ext_ref, cw_ref[...], cb_ref[...], PROJ_TM)
        ext_ref[0:HALO, :] = ext_ref[PROJ_TM:PROJ_TM + HALO, :]

    return _tiled(body, "mix_in_fwd", T // PROJ_TM, [(h, PROJ_TM, D_MODEL, 0)], [g, conv_w, conv_b], [w_in_t],
                  [(T, PROJ_KEPT, F32, PROJ_TM), (T, D_MODEL, BF16, PROJ_TM), (T, CONV_DIM, BF16, PROJ_TM),
                   (T, CONV_DIM, F32, PROJ_TM)], [],
                  scratch=[pltpu.VMEM((HALO + PROJ_TM, CONV_DIM), F32)])


def _mix_in_dgrad(h, dh_in, dp_uv, dp_zxd, g, w_in_t, comm=None):
    T = h.shape[0]

    def body(i, h_ref, dh_ref, duv_ref, dzxd_ref, g_ref, w_ref, o_ref, dg_ref):
        dn = _dot(duv_ref[...], w_ref[:UV_W, :]) + _dot(dzxd_ref[...], w_ref[UV_W:, :])
        _, rms_vjp = jax.vjp(_rms, h_ref[...], g_ref[...])
        dx, dg = rms_vjp(dn)
        o_ref[...] = dh_ref[...] + dx
        dg_ref[...] += dg

    return _tiled(body, "mix_in_dgrad", T // PROJ_DGRAD_TM,
                  [(h, PROJ_DGRAD_TM, D_MODEL, 0), (dh_in, PROJ_DGRAD_TM, D_MODEL, 0), (dp_uv, PROJ_DGRAD_TM, UV_W, 0),
                   (dp_zxd, PROJ_DGRAD_TM, ZXD_W, 0)], [g], [w_in_t],
                  [(T, D_MODEL, F32, PROJ_DGRAD_TM)], [((1, D_MODEL), F32)], comm=comm)


def _out_proj_dgrad(dh, w_out):
    T = dh.shape[0]

    def body(i, dh_ref, w_ref, dya_ref, dyb_ref):
        d = dh_ref[...].astype(BF16)
        dya_ref[...] = _dot_nt(d, w_ref[:GM_WIDTH, :])
        dyb_ref[...] = _dot_nt(d, w_ref[GM_WIDTH:, :])

    rows = min(T, 2 * PROJ_TM)
    return _tiled(body, "out_proj_dgrad", T // rows, [(dh, rows, D_MODEL, 0)], [], [w_out],
                  [(T, GM_WIDTH, F32, rows), (T, SSM_WIDTH, F32, rows)], [])


def _gm_chunk(u, v, ln_g, ln_b, b_st, out_g, *w_heads):
    ug = _gelu(u)
    vg = _gelu(v)
    mu = jnp.mean(vg, axis=-1, keepdims=True)
    xc = vg - mu
    vn = xc * lax.rsqrt(jnp.mean(xc * xc, axis=-1, keepdims=True) + EPS) * ln_g + ln_b
    t_idx = lax.broadcasted_iota(jnp.int32, (CHUNK, CHUNK), 0)
    s_idx = lax.broadcasted_iota(jnp.int32, (CHUNK, CHUNK), 1)
    causal = t_idx >= s_idx
    mixed = []
    for hd in range(GM_HEADS):
        wm = jnp.where(causal, w_heads[hd], 0.0)
        cols = slice(hd * GM_HEAD_DIM, (hd + 1) * GM_HEAD_DIM)
        mixed.append(_dot(wm, vn[:, cols]) + b_st[:, hd:hd + 1])
    ya0 = ug * jnp.concatenate(mixed, axis=1)
    return _rms(ya0, out_g)


GM_FWD_CHUNKS = 4


def _gm_fwd(proj, ln_g, ln_b, w_s, b_st, out_g):
    T = proj.shape[0]

    rows = GM_FWD_CHUNKS * CHUNK

    def body(i, u_ref, v_ref, lg_ref, lb_ref, w_ref, bs_ref, og_ref, ya_ref):
        w_heads = [w_ref[hd] for hd in range(GM_HEADS)]
        for c in range(GM_FWD_CHUNKS):
            tok = pl.ds(c * CHUNK, CHUNK)
            ya = _gm_chunk(u_ref[tok, :], v_ref[tok, :], lg_ref[...], lb_ref[...], bs_ref[...], og_ref[...], *w_heads)
            ya_ref[tok, :] = ya.astype(BF16)

    return _tiled(body, "gmlp_fwd", T // rows, [(proj, rows, GM_WIDTH, 0), (proj, rows, GM_WIDTH, 1)],
                  [ln_g, ln_b, w_s, b_st, out_g], [], [(T, GM_WIDTH, BF16, rows)], [])[0]


def _gm_bwd(proj, dya, ln_g, ln_b, w_s, b_st, out_g):
    T = proj.shape[0]

    def body(i, u_ref, v_ref, dy_ref, lg_ref, lb_ref, w_ref, bs_ref, og_ref, duv_ref, dlg_ref, dlb_ref, dw_ref, dbs_ref,
             dog_ref):
        w_heads = [w_ref[hd] for hd in range(GM_HEADS)]
        _, vjp = jax.vjp(_gm_chunk, u_ref[...], v_ref[...], lg_ref[...], lb_ref[...], bs_ref[...], og_ref[...], *w_heads)
        grads = vjp(dy_ref[...])
        duv_ref[:, :GM_WIDTH] = grads[0].astype(BF16)
        duv_ref[:, GM_WIDTH:] = grads[1].astype(BF16)
        dlg_ref[...] += grads[2]
        dlb_ref[...] += grads[3]
        dbs_ref[...] += grads[4]
        dog_ref[...] += grads[5]
        for hd in range(GM_HEADS):
            dw_ref[hd] += grads[6 + hd]

    return _tiled(body, "gmlp_bwd", T // CHUNK,
                  [(proj, CHUNK, GM_WIDTH, 0), (proj, CHUNK, GM_WIDTH, 1), (dya, CHUNK, GM_WIDTH, 0)],
                  [ln_g, ln_b, w_s, b_st, out_g], [], [(T, UV_W, BF16, CHUNK)],
                  [((1, GM_WIDTH), F32), ((1, GM_WIDTH), F32), ((GM_HEADS, CHUNK, CHUNK), F32),
                   ((CHUNK, GM_HEADS), F32), ((1, GM_WIDTH), F32)])


def _ssd_chunk(xc, z, dtr, s_in, dt_bias, a_log, d_skip, norm_g):
    half = SSM_WIDTH // SSM_GROUPS
    l_idx = lax.broadcasted_iota(jnp.int32, (CHUNK, CHUNK), 0)
    s_idx = lax.broadcasted_iota(jnp.int32, (CHUNK, CHUNK), 1)
    causal = l_idx >= s_idx
    head_of_col = lax.broadcasted_iota(jnp.int32, (SSM_HEADS, SSM_WIDTH), 1) // SSM_HEAD_DIM
    expand = (head_of_col == lax.broadcasted_iota(jnp.int32, (SSM_HEADS, SSM_WIDTH), 0)).astype(BF16)

    xcs = _silu(xc)
    xs = xcs[:, :SSM_WIDTH]
    dt = jax.nn.softplus(dtr + dt_bias)
    adt = dt * (-jnp.exp(a_log))
    acs = _cumsum_rows(adt, causal.astype(BF16))
    acs_t = _cumsum_cols(adt, (l_idx <= s_idx).astype(BF16))
    tot = acs[CHUNK - 1:CHUNK, :]
    dt_w = _widen(dt, expand)
    out_decay_w = _widen(jnp.exp(acs), expand)
    state_decay_w = _widen(jnp.exp(tot - acs), expand)
    chunk_decay_w = _widen(jnp.exp(tot), expand)
    d_skip_w = _widen(d_skip, expand)
    xdt = xs * dt_w
    xdt_decayed = xdt * state_decay_w

    y_diag, y_off, states = [], [], []
    for grp in range(SSM_GROUPS):
        b0 = SSM_WIDTH + grp * SSM_STATE
        c0 = SSM_WIDTH + SSM_GROUPS * SSM_STATE + grp * SSM_STATE
        bm = xcs[:, b0:b0 + SSM_STATE].astype(BF16)
        cm = xcs[:, c0:c0 + SSM_STATE].astype(BF16)
        cb = _dot_nt(cm, bm)
        for k in range(grp * SSM_HEADS // SSM_GROUPS, (grp + 1) * SSM_HEADS // SSM_GROUPS):
            decay = jnp.exp(jnp.where(causal, acs[:, k:k + 1] - acs_t[k:k + 1, :], -jnp.inf))
            y_diag.append(_dot(cb * decay, xdt[:, k * SSM_HEAD_DIM:(k + 1) * SSM_HEAD_DIM]))
        cols = slice(grp * half, (grp + 1) * half)
        states.append(_dot_tn(bm, xdt_decayed[:, cols]))
        y_off.append(_dot(cm, s_in[:, cols]))
    y = jnp.concatenate(y_diag, axis=1) + jnp.concatenate(y_off, axis=1) * out_decay_w + xs * d_skip_w
    s_out = s_in * chunk_decay_w + jnp.concatenate(states, axis=1)
    y = y * _silu(z)
    normed = []
    for grp in range(SSM_GROUPS):
        yg = y[:, grp * half:(grp + 1) * half]
        normed.append(yg * lax.rsqrt(jnp.mean(yg * yg, axis=-1, keepdims=True) + EPS))
    return jnp.concatenate(normed, axis=1) * norm_g, s_out


def _sum_row_tiles(x):
    return x.reshape(x.shape[0] // F32_ROWS, F32_ROWS, x.shape[1]).sum(axis=0)


def _conv_taps(ext_ref, w, b, rows):
    y = b
    for k in range(SSM_CONV):
        y = y + w[k:k + 1, :] * ext_ref[pl.ds(HALO - (SSM_CONV - 1) + k, rows), :]
    return y


def _ssd_fwd(proj, xc, dt_bias, a_log, d_skip, norm_g, comm=None):
    T = proj.shape[0]
    n_chunks = T // CHUNK

    def body(i, z_ref, xc_ref, dt_ref, dtb_ref, al_ref, dsk_ref, ng_ref, yb_ref, sin_ref, st_ref):
        @pl.when(i == 0)
        def _():
            st_ref[...] = jnp.zeros(st_ref.shape, F32)

        s_in = st_ref[...]
        yb, s_out = _ssd_chunk(xc_ref[...], z_ref[...], dt_ref[:, 0:SSM_HEADS], s_in, dtb_ref[...], al_ref[...],
                               dsk_ref[...], ng_ref[...])
        yb_ref[...] = yb.astype(BF16)
        sin_ref[...] = s_in
        st_ref[...] = s_out

    return _tiled(body, "ssd_fwd", n_chunks,
                  [(proj, CHUNK, SSM_WIDTH, Z_BLK), (xc, CHUNK, CONV_DIM, 0), (proj, CHUNK, LANES, DT_BLK)],
                  [dt_bias, a_log, d_skip, norm_g], [],
                  [(T, SSM_WIDTH, BF16, CHUNK), (n_chunks * SSM_STATE, SSM_WIDTH, F32, SSM_STATE)], [],
                  scratch=[pltpu.VMEM((SSM_STATE, SSM_WIDTH), F32)], comm=comm)


def _ssd_bwd(proj, x16, xc, dyb, s_all, conv_w, dt_bias, a_log, d_skip, norm_g, comm=None):
    T = proj.shape[0]
    n_chunks = T // CHUNK

    def body(i, z_ref, x_ref, xc_ref, dt_ref, dy_ref, sin_ref, cw_ref, dtb_ref, al_ref, dsk_ref, ng_ref,
             dzxd_ref, dcw_ref, dcb_ref, ddtb_ref, dal_ref, ddsk_ref, dng_ref, dext_ref, dst_ref, cw_acc, cb_acc):
        @pl.when(i == n_chunks - 1)
        def _():
            dext_ref[CHUNK:, :] = jnp.zeros((HALO, CONV_DIM), F32)
            dst_ref[...] = jnp.zeros(dst_ref.shape, F32)
            cw_acc[...] = jnp.zeros(cw_acc.shape, F32)
            cb_acc[...] = jnp.zeros(cb_acc.shape, F32)

        _, vjp = jax.vjp(_ssd_chunk, xc_ref[...], z_ref[...], dt_ref[:, 0:SSM_HEADS], sin_ref[...], dtb_ref[...], al_ref[...],
                         dsk_ref[...], ng_ref[...])
        dxc, dz, ddtr, ds_in, ddtb, dal, ddsk, dng = vjp((dy_ref[...], dst_ref[...]))
        dst_ref[...] = ds_in
        ddtb_ref[...] += ddtb
        dal_ref[...] += dal
        ddsk_ref[...] += ddsk
        dng_ref[...] += dng
        dext_ref[0:CHUNK, :] = dxc
        cw = cw_ref[...]
        x = x_ref[...].astype(F32)
        dx = jnp.zeros((CHUNK, CONV_DIM), F32)
        for k in range(SSM_CONV):
            shifted = dext_ref[pl.ds(SSM_CONV - 1 - k, CHUNK), :]
            dx = dx + cw[k:k + 1, :] * shifted
            cw_acc[k] += _sum_row_tiles(shifted * x)
        cb_acc[...] += _sum_row_tiles(dxc)

        @pl.when(i == 0)
        def _():
            dcw_ref[...] = jnp.sum(cw_acc[...], axis=1)
            dcb_ref[...] = jnp.sum(cb_acc[...], axis=0, keepdims=True)

        dext_ref[CHUNK:, :] = dext_ref[0:HALO, :]
        dzxd_ref[:, 0:SSM_WIDTH] = dz.astype(BF16)
        dzxd_ref[:, SSM_WIDTH:SSM_WIDTH + CONV_DIM] = dx.astype(BF16)
        dzxd_ref[:, SSM_WIDTH + CONV_DIM:] = jnp.concatenate(
            [ddtr, jnp.zeros((CHUNK, LANES - SSM_HEADS), F32)], axis=1).astype(BF16)

    return _tiled(body, "ssd_bwd", n_chunks,
                  [(proj, CHUNK, SSM_WIDTH, Z_BLK), (x16, CHUNK, CONV_DIM, 0), (xc, CHUNK, CONV_DIM, 0),
                   (proj, CHUNK, LANES, DT_BLK), (dyb, CHUNK, SSM_WIDTH, 0), (s_all, SSM_STATE, SSM_WIDTH, 0)],
                  [conv_w, dt_bias, a_log, d_skip, norm_g], [],
                  [(T, ZXD_W, BF16, CHUNK)],
                  [((SSM_CONV, CONV_DIM), F32), ((1, CONV_DIM), F32), ((1, SSM_HEADS), F32), ((1, SSM_HEADS), F32),
                   ((1, SSM_HEADS), F32), ((1, SSM_WIDTH), F32)],
                  scratch=[pltpu.VMEM((CHUNK + HALO, CONV_DIM), F32), pltpu.VMEM((SSM_STATE, SSM_WIDTH), F32),
                           pltpu.VMEM((SSM_CONV, F32_ROWS, CONV_DIM), F32), pltpu.VMEM((F32_ROWS, CONV_DIM), F32)],
                  reverse=True, comm=comm)


TAIL_TM = 512


def _tail(h, p, target, ple_norm, w_gate, b_gate, w_proj_t, final_norm):
    T = h.shape[0]

    def head(x, pre, pp, b_g, f_norm, tgt):
        gate = jax.nn.sigmoid(pre + b_g)
        out = _rms(x + gate * pp, f_norm)
        err = out - tgt
        return 0.5 * jnp.sum(jnp.mean(err * err, axis=-1, keepdims=True), axis=0, keepdims=True)

    def body(i, h_ref, p_ref, t_ref, pn_ref, bg_ref, fn_ref, wg_ref, wp_ref, dh_ref, loss_ref, dwg_ref, dwp_ref, dpn_ref,
             dbg_ref, dfn_ref):
        x = h_ref[...]
        n4f, n_vjp = jax.vjp(_rms, x, pn_ref[...])
        n4 = n4f.astype(BF16)
        pre = jnp.dot(n4, wg_ref[...], preferred_element_type=F32)
        p16 = p_ref[...].astype(BF16)
        pp = _dot_nt(p16, wp_ref[...])
        loss, h_vjp = jax.vjp(functools.partial(head, tgt=t_ref[...]), x, pre, pp, bg_ref[...], fn_ref[...])
        dx, dpre, dpp, dbg, dfn = h_vjp(jnp.ones((1, 1), F32))
        dpre16 = dpre.astype(BF16)
        dn4 = _dot_nt(dpre16, wg_ref[...])
        dx2, dpn = n_vjp(dn4)
        dh_ref[...] = dx + dx2
        loss_ref[...] += loss
        dwg_ref[...] += _dot_tn(n4, dpre16)
        dwp_ref[...] += _dot_tn(p16, dpp)
        dpn_ref[...] += dpn
        dbg_ref[...] += dbg
        dfn_ref[...] += dfn

    return _tiled(body, "tail", T // TAIL_TM,
                  [(h, TAIL_TM, D_MODEL, 0), (p, TAIL_TM, D_PLE, 0), (target, TAIL_TM, D_MODEL, 0)],
                  [ple_norm, b_gate, final_norm], [w_gate, w_proj_t],
                  [(T, D_MODEL, F32, TAIL_TM)],
                  [((1, 1), F32), ((D_MODEL, D_MODEL), F32), ((D_PLE, D_MODEL), F32), ((1, D_MODEL), F32),
                   ((1, D_MODEL), F32), ((1, D_MODEL), F32)])


def _gather_phases(x_ref, out_ref, send_sems, recv_sems, local_sem):
    mx, my, mc = lax.axis_index("x"), lax.axis_index("y"), lax.axis_index("c")
    me, sibling = (mx, my, mc), (mx, my, 1 - mc)
    chips = [(1 - mx, my), (mx, 1 - my), (1 - mx, 1 - my)]

    def rows(px, py, pc):
        return out_ref.at[4 * px + 2 * py + pc]

    def copy(k, block, to, src=None):
        return pltpu.make_async_remote_copy(
            src_ref=rows(*block) if src is None else src, dst_ref=rows(*block),
            send_sem=send_sems.at[k], recv_sem=recv_sems.at[k], device_id=to, device_id_type=MESH)

    mine = pltpu.make_async_copy(x_ref, rows(*me), local_sem)
    first = [copy(0, me, sibling, src=x_ref)] + [copy(1 + j, me, (*chip, mc), src=x_ref) for j, chip in enumerate(chips)]
    passed = [copy(4 + j, (*chip, mc), sibling) for j, chip in enumerate(chips)]

    def start():
        mine.start()
        for cp in first:
            cp.start()

    def mid():
        for j, chip in enumerate(chips):
            copy(1 + j, (*chip, mc), me).wait_recv()
            passed[j].start()

    def finish():
        copy(0, sibling, me).wait_recv()
        for j, chip in enumerate(chips):
            copy(4 + j, (*chip, 1 - mc), me).wait_recv()
        for cp in first + passed:
            cp.wait_send()
        mine.wait()

    return start, mid, finish


def _exchange_phases(x_ref, out_ref, send_sems, recv_sems, local_sem):
    mx, my, mc = lax.axis_index("x"), lax.axis_index("y"), lax.axis_index("c")
    me = 4 * mx + 2 * my + mc
    mine = pltpu.make_async_copy(x_ref.at[me], out_ref.at[me], local_sem)
    copies = []
    for k in range(1, N_DEV):
        px = 1 - mx if k & 4 else mx
        py = 1 - my if k & 2 else my
        pc = 1 - mc if k & 1 else mc
        copies.append(pltpu.make_async_remote_copy(
            src_ref=x_ref.at[4 * px + 2 * py + pc], dst_ref=out_ref.at[me], send_sem=send_sems.at[k - 1],
            recv_sem=recv_sems.at[k - 1], device_id=(px, py, pc), device_id_type=MESH))

    def start():
        mine.start()
        for cp in copies:
            cp.start()

    def finish():
        for cp in copies:
            cp.wait_recv()
        for cp in copies:
            cp.wait_send()
        mine.wait()

    return start, lambda: None, finish


def _chip_exchange_phases(x_ref, out_ref, mine, recv, sums, load_sems, pair_send, pair_recv, chip_send, chip_recv, out_sem):
    mx, my, mc = lax.axis_index("x"), lax.axis_index("y"), lax.axis_index("c")
    my_chip = 2 * mx + my
    load = [pltpu.make_async_copy(x_ref.at[2 * q + mc], mine.at[q], load_sems.at[q]) for q in range(N_CHIPS)]
    to_sibling = [pltpu.make_async_remote_copy(
        src_ref=x_ref.at[2 * q + 1 - mc], dst_ref=recv.at[q], send_sem=pair_send.at[q], recv_sem=pair_recv.at[q],
        device_id=(mx, my, 1 - mc), device_id_type=MESH) for q in range(N_CHIPS)]
    to_chips = []
    for k in range(1, N_CHIPS):
        px = 1 - mx if k & 2 else mx
        py = 1 - my if k & 1 else my
        to_chips.append(pltpu.make_async_remote_copy(
            src_ref=sums.at[2 * px + py], dst_ref=out_ref.at[my_chip], send_sem=chip_send.at[k - 1],
            recv_sem=chip_recv.at[k - 1], device_id=(px, py, mc), device_id_type=MESH))
    keep = pltpu.make_async_copy(sums.at[my_chip], out_ref.at[my_chip], out_sem)

    def start():
        for cp in load + to_sibling:
            cp.start()

    def mid():
        for cp in load:
            cp.wait()
        for cp in to_sibling:
            cp.wait_recv()
        for q in range(N_CHIPS):
            sums[q] = (mine[q].astype(F32) + recv[q].astype(F32)).astype(sums.dtype)
        for cp in to_chips + [keep]:
            cp.start()

    def finish():
        for cp in to_chips:
            cp.wait_recv()
        for cp in to_chips + to_sibling:
            cp.wait_send()
        keep.wait()

    return start, mid, finish


FLAT_SCRATCH = (pltpu.SemaphoreType.DMA((N_DEV - 1,)), pltpu.SemaphoreType.DMA((N_DEV - 1,)), pltpu.SemaphoreType.DMA)


def _gather_comm(x):
    return _Comm(_gather_phases, x, jax.ShapeDtypeStruct((N_DEV,) + x.shape, x.dtype), FLAT_SCRATCH)


def _exchange_comm(x):
    return _Comm(_exchange_phases, x, jax.ShapeDtypeStruct(x.shape, x.dtype), FLAT_SCRATCH)


def _chip_exchange_comm(x):
    stage = pltpu.VMEM((N_CHIPS,) + x.shape[1:], x.dtype)
    sems = [pltpu.SemaphoreType.DMA((n,)) for n in (N_CHIPS, N_CHIPS, N_CHIPS, N_CHIPS - 1, N_CHIPS - 1)]
    return _Comm(_chip_exchange_phases, x, jax.ShapeDtypeStruct((N_CHIPS,) + x.shape[1:], x.dtype),
                 (stage, stage, stage, *sems, pltpu.SemaphoreType.DMA))


def _comm_alone(comms, name):
    n = len(comms)

    def body(*refs):
        phases, first = [], 2 * n
        for k, comm in enumerate(comms):
            phases.append(comm.phases(refs[k], refs[n + k], *refs[first:first + len(comm.scratch)]))
            first += len(comm.scratch)
        for step in range(3):
            for phase in phases:
                phase[step]()

    any_spec = pl.BlockSpec(memory_space=pl.ANY)
    return pl.pallas_call(
        body,
        out_shape=[comm.dst for comm in comms],
        in_specs=[any_spec] * n,
        out_specs=[any_spec] * n,
        scratch_shapes=[shape for comm in comms for shape in comm.scratch],
        name=name,
        compiler_params=pltpu.CompilerParams(vmem_limit_bytes=VMEM_LIMIT),
    )(*[comm.src for comm in comms])


def _sum_parts(p_ref):
    g = p_ref[0].astype(F32)
    for j in range(1, p_ref.shape[0]):
        g = g + p_ref[j].astype(F32)
    return g


def _adamw_store(g, w_ref, m_ref, v_ref, g_ref, d_ref, nm_ref, nv_ref):
    m_new = ADAM_B1 * m_ref[...] + (1.0 - ADAM_B1) * g
    v_new = ADAM_B2 * v_ref[...] + (1.0 - ADAM_B2) * jnp.square(g)
    m_hat = m_new / (1.0 - ADAM_B1 ** ADAM_STEP)
    v_hat = v_new / (1.0 - ADAM_B2 ** ADAM_STEP)
    g_ref[...] = g
    d_ref[...] = -ADAM_LR * (m_hat / (jnp.sqrt(v_hat) + ADAM_EPS) + ADAM_WD * w_ref[...])
    nm_ref[...] = m_new
    nv_ref[...] = v_new


def _adamw_shard(parts, off, w, m, v, name, n_tiles):
    _, rows, c = w.shape
    assert c == PACK_COLS
    by_rows = rows % BF16_ROWS == 0
    if by_rows:
        tr = rows // n_tiles
        window = (parts.shape[0], tr, PACK_COLS)
        spec = pl.BlockSpec((None, tr, PACK_COLS), lambda i: (0, i, 0))
    else:
        padded, tc = -(-rows // BF16_ROWS) * BF16_ROWS, PACK_COLS // n_tiles
        window = (parts.shape[0], padded, tc)
        spec = pl.BlockSpec((None, rows, tc), lambda i: (0, 0, i))

    def kern(p_hbm, w_ref, m_ref, v_ref, g_ref, d_ref, nm_ref, nv_ref, buf, sem):
        i = pl.program_id(0)
        if by_rows:
            src = p_hbm.at[:, pl.ds(pl.multiple_of(off + i * tr, BF16_ROWS), tr), :]
        else:
            src = p_hbm.at[:, pl.ds(off, padded), pl.ds(pl.multiple_of(i * tc, LANES), tc)]
        cp = pltpu.make_async_copy(src, buf, sem)
        cp.start()
        cp.wait()
        g = _sum_parts(buf)
        if not by_rows:
            keep = lax.broadcasted_iota(jnp.int32, (rows, padded), 0) == lax.broadcasted_iota(jnp.int32, (rows, padded), 1)
            g = _exact_dot(g, keep.astype(BF16), ((1,), (0,)), x_first=False)
        _adamw_store(g, w_ref, m_ref, v_ref, g_ref, d_ref, nm_ref, nv_ref)

    return pl.pallas_call(
        kern,
        out_shape=[jax.ShapeDtypeStruct(w.shape, F32)] * 4,
        grid=(n_tiles,),
        in_specs=[pl.BlockSpec(memory_space=pl.ANY), spec, spec, spec],
        out_specs=[spec] * 4,
        scratch_shapes=[pltpu.VMEM(window, parts.dtype), pltpu.SemaphoreType.DMA],
        name=name,
        compiler_params=pltpu.CompilerParams(dimension_semantics=("arbitrary",), vmem_limit_bytes=VMEM_LIMIT),
    )(parts, w, m, v)


def _sum_adamw(parts, w, m, v, tr, name):
    _, R, C = parts.shape

    def kern(p_ref, w_ref, m_ref, v_ref, g_ref, d_ref, nm_ref, nv_ref):
        _adamw_store(_sum_parts(p_ref), w_ref, m_ref, v_ref, g_ref, d_ref, nm_ref, nv_ref)

    row_spec = pl.BlockSpec((tr, C), lambda i: (i, 0))
    return pl.pallas_call(
        kern,
        out_shape=[jax.ShapeDtypeStruct((R, C), F32)] * 4,
        grid=(R // tr,),
        in_specs=[pl.BlockSpec((N_DEV, tr, C), lambda i: (0, i, 0)), row_spec, row_spec, row_spec],
        out_specs=[row_spec] * 4,
        name=name,
        compiler_params=pltpu.CompilerParams(dimension_semantics=("arbitrary",), vmem_limit_bytes=VMEM_LIMIT),
    )(parts, w, m, v)


FF_SHARD = D_FF // N_DEV
CONV_SHARD = (SSM_CONV, CONV_DIM // N_DEV)
SHARDS = {"ffn1_w_gate": ((D_MODEL, FF_SHARD), True), "ffn1_w_up": ((D_MODEL, FF_SHARD), True),
          "ffn1_w_down": ((FF_SHARD, D_MODEL), False),
          "ffn2_w_gate": ((D_MODEL, FF_SHARD), True), "ffn2_w_up": ((D_MODEL, FF_SHARD), True),
          "ffn2_w_down": ((FF_SHARD, D_MODEL), False),
          "w_out": ((2 * D_MODEL // N_DEV, D_MODEL), False), "ple_w_gate": ((D_MODEL // N_DEV, D_MODEL), False),
          "w_in": ((D_MODEL, IN_PROJ // N_DEV), True), "ple_w_proj": ((D_PLE, D_MODEL // N_DEV), True),
          "conv_w": (CONV_SHARD, True),
          "conv_w_mid": (CONV_SHARD, True), "conv_w_low": (CONV_SHARD, True)}
BIG = tuple(name for name in SHARDS if not name.startswith("conv_w_"))
SMALL = ("ffn1_norm", "mix_norm", "gm_ln_g", "gm_ln_b", "gm_w_s", "gm_b_s", "gm_out_norm", "conv_b", "dt_bias", "a_log",
         "d_skip", "ssm_norm", "ffn2_norm", "ple_norm", "ple_b_gate", "final_norm")
SMALL_ROWS = 144


def _piece_rows(name):
    shape = SHARDS[name][0]
    return -(-(shape[0] * shape[1]) // PACK_COLS)


def _pad_cols(flat, name):
    pad = _piece_rows(name) * PACK_COLS - flat.shape[-1]
    return flat if pad == 0 else jnp.pad(flat, [(0, 0)] * (flat.ndim - 1) + [(0, pad)])


class _Pack:
    def __init__(self, names, tile_rows):
        self.names, self.tile_rows, self.offsets, off = names, tile_rows, {}, 0
        for name in names:
            self.offsets[name] = off
            off += _piece_rows(name)
        self.rows = -(-off // tile_rows) * tile_rows

    def pack_local(self, vals):
        parts = []
        for name in self.names:
            val = vals[name]
            parts.append(_pad_cols((val.T if SHARDS[name][1] else val).reshape(-1), name))
        flat = jnp.concatenate(parts)
        return jnp.pad(flat, (0, self.rows * PACK_COLS - flat.shape[0])).reshape(self.rows, PACK_COLS)

    def pack_owner_major(self, grads):
        parts, rows = [], 0
        for name in self.names:
            grad, piece_rows = grads[name].astype(BF16), _piece_rows(name)
            if grad.shape != (N_DEV * piece_rows, PACK_COLS):
                grad = _pad_cols(grad.reshape(N_DEV, -1), name)
            parts.append(grad.reshape(N_DEV, piece_rows, PACK_COLS))
            rows += piece_rows
        if rows < self.rows:
            parts.append(jnp.zeros((N_DEV, self.rows - rows, PACK_COLS), BF16))
        return parts[0] if len(parts) == 1 else jnp.concatenate(parts, axis=1)

    def gathered_piece(self, gathered, name):
        shape = SHARDS[name][0]
        rows = gathered[:, self.offsets[name]:self.offsets[name] + _piece_rows(name), :]
        return rows.reshape(N_DEV, -1)[:, :shape[0] * shape[1]]

    def pieces(self, gathered, name):
        return _Pieces(gathered, self.offsets[name], _piece_rows(name))


GATHER_FFN1 = _Pack(("ffn1_w_gate", "ffn1_w_up", "ffn1_w_down"), BF16_ROWS)
GATHER_MIX = _Pack(("w_out", "ple_w_gate", "w_in", "ple_w_proj", "conv_w", "conv_w_mid", "conv_w_low"), BF16_ROWS)
GATHER_FFN2 = _Pack(("ffn2_w_gate", "ffn2_w_up", "ffn2_w_down"), BF16_ROWS)
SCATTER_LATE = _Pack(("ffn2_w_gate", "ffn2_w_up", "ffn2_w_down", "w_out", "ple_w_gate", "ple_w_proj"), BF16_ROWS)
SCATTER_IN = _Pack(("w_in", "conv_w"), BF16_ROWS)
SCATTER_GATE = _Pack(("ffn1_w_gate",), BF16_ROWS)
SCATTER_UP = _Pack(("ffn1_w_up",), BF16_ROWS)
SCATTER_DOWN = _Pack(("ffn1_w_down",), BF16_ROWS)


def _pack_small(vals, behind=()):
    flat = jnp.concatenate([vals[name].reshape(-1).astype(F32) for name in SMALL] + [b.reshape(-1) for b in behind])
    return jnp.pad(flat, (0, SMALL_ROWS * PACK_COLS - flat.shape[0])).reshape(SMALL_ROWS, PACK_COLS)


def _unpack_small(packed, shapes):
    out, off = {}, 0
    flat = packed.reshape(-1)
    for name in SMALL:
        n = 1
        for s in shapes[name]:
            n *= s
        out[name] = flat[off:off + n].reshape(shapes[name])
        off += n
    return out


WEIGHTS = ("ffn1_norm", "ffn1_w_gate", "ffn1_w_up", "ffn1_w_down", "mix_norm", "w_in", "gm_ln_g", "gm_ln_b", "gm_w_s",
           "gm_b_s", "gm_out_norm", "conv_w", "conv_b", "dt_bias", "a_log", "d_skip", "ssm_norm", "w_out", "ffn2_norm",
           "ffn2_w_gate", "ffn2_w_up", "ffn2_w_down", "ple_norm", "ple_w_gate", "ple_b_gate", "ple_w_proj", "final_norm")


def _step(x, p, target, w, m, v):
    local = lambda d: {name: d[name][0] for name in BIG}

    shards = {name: val.astype(BF16) for name, val in local(w).items()}
    conv_high = lax.reduce_precision(w["conv_w"][0], 8, 7)
    conv_mid = lax.reduce_precision(w["conv_w"][0] - conv_high, 8, 7)
    shards["conv_w"] = conv_high.astype(BF16)
    shards["conv_w_mid"] = conv_mid.astype(BF16)
    shards["conv_w_low"] = (w["conv_w"][0] - conv_high - conv_mid).astype(BF16)
    g_ffn1 = _comm_alone([_gather_comm(GATHER_FFN1.pack_local(shards))], "gather_ffn1")[0]

    row = lambda name: w[name].reshape(1, -1)
    gm_w_s = w["gm_w_s"][0]
    gm_b_st = jnp.transpose(w["gm_b_s"][0])
    ffn1 = (row("ffn1_norm"),) + tuple(GATHER_FFN1.pieces(g_ffn1, name) for name in GATHER_FFN1.names)
    gm = (row("gm_ln_g"), row("gm_ln_b"), gm_w_s, gm_b_st, row("gm_out_norm"))

    h1, n1, a1, b1, s1, g_mix = _ffn_fwd(x, *ffn1, "ffn1_fwd", comm=_gather_comm(GATHER_MIX.pack_local(shards)))
    w_in_t = GATHER_MIX.gathered_piece(g_mix, "w_in").reshape(IN_PROJ, D_MODEL)
    w_in_t = jnp.concatenate([w_in_t, jnp.zeros((IN_PROJ_PAD - IN_PROJ, D_MODEL), BF16)], axis=0)
    w_proj_t = GATHER_MIX.gathered_piece(g_mix, "ple_w_proj").reshape(D_MODEL, D_PLE)
    conv_w = sum(GATHER_MIX.gathered_piece(g_mix, name).astype(F32) for name in ("conv_w", "conv_w_mid", "conv_w_low"))
    conv_w = conv_w.reshape(CONV_DIM, SSM_CONV).T
    ssd = (row("dt_bias"), row("a_log"), row("d_skip"), row("ssm_norm"))
    w_out = GATHER_MIX.pieces(g_mix, "w_out")

    proj, n2, x16, xc = _mix_in_fwd(h1, row("mix_norm"), w_in_t, conv_w, row("conv_b"))
    ya = _gm_fwd(proj, *gm)
    yb, s_all, g_ffn2 = _ssd_fwd(proj, xc, *ssd, comm=_gather_comm(GATHER_FFN2.pack_local(shards)))
    ffn2 = (row("ffn2_norm"),) + tuple(GATHER_FFN2.pieces(g_ffn2, name) for name in GATHER_FFN2.names)
    h3, n3, a3, b3, s3, h2 = _ffn_fwd(h1, *ffn2, "ffn2_fwd", mixed=(ya, yb, w_out))

    g, gp = {}, {}
    dh3, loss, gp["ple_w_gate"], d_w_proj, g["ple_norm"], g["ple_b_gate"], g["final_norm"] = _tail(
        h3, p, target, row("ple_norm"), GATHER_MIX.pieces(g_mix, "ple_w_gate"), row("ple_b_gate"), w_proj_t,
        row("final_norm"))
    gp["ple_w_proj"] = d_w_proj.T

    dh2, da3, db3, g["ffn2_norm"] = _ffn_dgrad(h2, dh3, a3, b3, *ffn2, "ffn2_dgrad")
    gp["ffn2_w_gate"] = _wgrad(n3, da3, FF_BN, "ffn2_wgrad_gate", transpose_out=True)
    gp["ffn2_w_up"] = _wgrad(n3, db3, FF_BN, "ffn2_wgrad_up", transpose_out=True)
    gp["ffn2_w_down"] = _wgrad(s3, dh3, DOWN_BN, "ffn2_wgrad_down", scale=0.5, bk=DOWN_BK)

    dya, dyb = _out_proj_dgrad(dh2, w_out)
    gp["w_out"] = jnp.concatenate([_wgrad(ya, dh2, SQUARE_BN, "w_out_wgrad_a"), _wgrad(yb, dh2, SQUARE_BN, "w_out_wgrad_b")], axis=0)

    dp_zxd, d_conv_w, g["conv_b"], g["dt_bias"], g["a_log"], g["d_skip"], g["ssm_norm"], parts_late = _ssd_bwd(
        proj, x16, xc, dyb, s_all, conv_w, *ssd, comm=_exchange_comm(SCATTER_LATE.pack_owner_major(gp)))
    gp["conv_w"] = d_conv_w.T
    dp_uv, g["gm_ln_g"], g["gm_ln_b"], g["gm_w_s"], dbst, g["gm_out_norm"] = _gm_bwd(proj, dya, *gm)
    g["gm_b_s"] = jnp.transpose(dbst)

    parts = {}
    gp["w_in"] = jnp.concatenate([_wgrad(n2, dp_uv, SQUARE_BN, "w_in_wgrad_uv", transpose_out=True),
                                  _wgrad(n2, dp_zxd, ZXD_BN, "w_in_wgrad_zxd", transpose_out=True)], axis=0)[:IN_PROJ]
    dh1, g["mix_norm"], parts[SCATTER_IN] = _mix_in_dgrad(h1, dh2, dp_uv, dp_zxd, row("mix_norm"), w_in_t,
                                                          comm=_exchange_comm(SCATTER_IN.pack_owner_major(gp)))

    dx, da1, db1, g["ffn1_norm"] = _ffn_dgrad(x, dh1, a1, b1, *ffn1, "ffn1_dgrad")
    gp["ffn1_w_gate"], small_parts = _wgrad(n1, da1, FF_BN, "ffn1_wgrad_gate", transpose_out=True,
                                            comm=_gather_comm(_pack_small(g, behind=[loss])))
    gp["ffn1_w_up"], parts[SCATTER_GATE] = _wgrad(n1, db1, FF_BN, "ffn1_wgrad_up", transpose_out=True,
                                                  comm=_chip_exchange_comm(SCATTER_GATE.pack_owner_major(gp)))
    gp["ffn1_w_down"], parts[SCATTER_UP] = _wgrad(s1, dh1, DOWN_BN, "ffn1_wgrad_down", scale=0.5, bk=DOWN_BK,
                                                  comm=_chip_exchange_comm(SCATTER_UP.pack_owner_major(gp)))
    parts[SCATTER_DOWN] = _comm_alone([_chip_exchange_comm(SCATTER_DOWN.pack_owner_major(gp))], "scatter_ffn1_down")[0]
    parts[SCATTER_LATE] = parts_late

    res_big = {}
    for pack, pack_parts in parts.items():
        for name in pack.names:
            shape, transposed = SHARDS[name]
            if name in ("ple_w_proj", "conv_w"):
                nat = pack.gathered_piece(pack_parts, name).reshape((N_DEV,) + shape[::-1])
                res_big[name] = _sum_adamw(jnp.transpose(nat, (0, 2, 1)), w[name][0], m[name][0], v[name][0], shape[0],
                                           "adamw_" + name)
            else:
                flip = (lambda a: jnp.transpose(a, (0, 2, 1))) if transposed else (lambda a: a)
                res = _adamw_shard(pack_parts, pack.offsets[name], flip(w[name]), flip(m[name]), flip(v[name]),
                                   "adamw_" + name, n_tiles=4 if name == "w_in" else 2)
                res_big[name] = [flip(r) for r in res]

    small_shapes = {name: w[name].shape for name in SMALL}
    res_small = _sum_adamw(small_parts, _pack_small(w), _pack_small(m), _pack_small(v), SMALL_ROWS, "adamw_small")
    loss = res_small[0].reshape(-1)[sum(w[name].size for name in SMALL)]
    res_small = [_unpack_small(r, small_shapes) for r in res_small]

    outs = []
    for k in range(4):
        for name in WEIGHTS:
            if name in res_small[k]:
                outs.append(res_small[k][name])
            else:
                outs.append(res_big[name][k].reshape(w[name].shape))
    return loss, dx, outs


def kernel(x, p, ffn1_norm, ffn1_w_gate, ffn1_w_up, ffn1_w_down, mix_norm, w_in, gm_ln_g, gm_ln_b, gm_w_s, gm_b_s, gm_out_norm, conv_w, conv_b, dt_bias, a_log, d_skip, ssm_norm, w_out, ffn2_norm, ffn2_w_gate, ffn2_w_up, ffn2_w_down, ple_norm, ple_w_gate, ple_b_gate, ple_w_proj, final_norm, loss_target, m_ffn1_norm, m_ffn1_w_gate, m_ffn1_w_up, m_ffn1_w_down, m_mix_norm, m_w_in, m_gm_ln_g, m_gm_ln_b, m_gm_w_s, m_gm_b_s, m_gm_out_norm, m_conv_w, m_conv_b, m_dt_bias, m_a_log, m_d_skip, m_ssm_norm, m_w_out, m_ffn2_norm, m_ffn2_w_gate, m_ffn2_w_up, m_ffn2_w_down, m_ple_norm, m_ple_w_gate, m_ple_b_gate, m_ple_w_proj, m_final_norm, v_ffn1_norm, v_ffn1_w_gate, v_ffn1_w_up, v_ffn1_w_down, v_mix_norm, v_w_in, v_gm_ln_g, v_gm_ln_b, v_gm_w_s, v_gm_b_s, v_gm_out_norm, v_conv_w, v_conv_b, v_dt_bias, v_a_log, v_d_skip, v_ssm_norm, v_w_out, v_ffn2_norm, v_ffn2_w_gate, v_ffn2_w_up, v_ffn2_w_down, v_ple_norm, v_ple_w_gate, v_ple_b_gate, v_ple_w_proj, v_final_norm):
    args = locals()
    w = {name: args[name] for name in WEIGHTS}
    m = {name: args["m_" + name] for name in WEIGHTS}
    v = {name: args["v_" + name] for name in WEIGHTS}
    loss, dx, outs = _step(x[0], p[0, 0], loss_target[0], w, m, v)
    return (loss, dx[None], *outs)
```

```python
import functools
from typing import NamedTuple

import jax
import jax.numpy as jnp
from jax import lax
from jax.experimental import pallas as pl
from jax.experimental.pallas import tpu as pltpu

F32 = jnp.float32
BF16 = jnp.bfloat16
MESH = pl.DeviceIdType.MESH
N_DEV = 8
N_CHIPS = 4

D_MODEL = 1024
D_FF = 2816
D_PLE = 256
GM_WIDTH = 1024
GM_HEADS = 8
GM_HEAD_DIM = 128
CHUNK = 128
SSM_WIDTH = 1024
SSM_HEADS = 16
SSM_HEAD_DIM = 64
SSM_GROUPS = 2
SSM_STATE = 128
SSM_CONV = 4
CONV_DIM = SSM_WIDTH + 2 * SSM_GROUPS * SSM_STATE
IN_PROJ = 2 * GM_WIDTH + SSM_WIDTH + CONV_DIM + SSM_HEADS
LANES = 128
BF16_ROWS = 16
F32_ROWS = 8
IN_PROJ_PAD = IN_PROJ - SSM_HEADS + LANES
UV_W = 2 * GM_WIDTH
ZXD_W = IN_PROJ_PAD - UV_W
HALO = 8
EPS = 1e-6

ADAM_LR = 0.001
ADAM_B1 = 0.9
ADAM_B2 = 0.999
ADAM_EPS = 1e-08
ADAM_WD = 0.01
ADAM_STEP = 10

VMEM_LIMIT = 56 * 1024 * 1024
PACK_COLS = 1024


def _rms(x, g):
    return x * lax.rsqrt(jnp.mean(x * x, axis=-1, keepdims=True) + EPS) * g


def _gelu(x):
    return 0.5 * x * (1.0 + lax.erf(x * (2.0 ** -0.5)))


def _silu(x):
    return x * jax.nn.sigmoid(x)


def _dot(a, b):
    return jnp.dot(a.astype(BF16), b.astype(BF16), preferred_element_type=F32)


def _dot_nt(a, b):
    return lax.dot_general(a.astype(BF16), b.astype(BF16), (((1,), (1,)), ((), ())), preferred_element_type=F32)


def _dot_tn(a, b):
    return lax.dot_general(a.astype(BF16), b.astype(BF16), (((0,), (0,)), ((), ())), preferred_element_type=F32)


def _split3(x):
    hi = x.astype(BF16)
    rest = x - hi.astype(F32)
    mid = rest.astype(BF16)
    return hi, mid, (rest - mid.astype(F32)).astype(BF16)


def _exact_dot(x, mask, dims, x_first=True, n_terms=3):
    terms = [lax.dot_general(*((t, mask) if x_first else (mask, t)), (dims, ((), ())), preferred_element_type=F32)
             for t in _split3(x)[:n_terms]]
    total = terms[0]
    for term in terms[1:]:
        total = total + term
    return total


def _mask_product(fwd_dims, fwd_x_first, bwd_dims, bwd_x_first, bwd_terms=3):
    @jax.custom_vjp
    def product(x, mask):
        return _exact_dot(x, mask, fwd_dims, fwd_x_first)

    def fwd(x, mask):
        return product(x, mask), mask

    def bwd(mask, g):
        return _exact_dot(g, mask, bwd_dims, bwd_x_first, bwd_terms), jnp.zeros_like(mask)

    product.defvjp(fwd, bwd)
    return product


_widen = _mask_product(((1,), (0,)), True, ((1,), (1,)), True, bwd_terms=2)
_cumsum_rows = _mask_product(((1,), (0,)), False, ((0,), (0,)), False)
_cumsum_cols = _mask_product(((0,), (0,)), True, ((1,), (1,)), False)


class _Pieces(NamedTuple):
    gathered: jax.Array
    row_off: int
    rows: int


class _Comm(NamedTuple):
    phases: object
    src: jax.Array
    dst: jax.ShapeDtypeStruct
    scratch: tuple


def _tiled(body, name, n_steps, tiled_in, full_in, big_in, tiled_out, acc_out, scratch=(), reverse=False, comm=None):
    n_t, n_f, n_b, n_to, n_a = len(tiled_in), len(full_in), len(big_in), len(tiled_out), len(acc_out)
    n_c = 1 if comm else 0

    def row(i):
        return n_steps - 1 - i if reverse else i

    in_specs, args = [], []
    for arr, br, bc, cb in tiled_in:
        if callable(cb):
            in_specs.append(pl.BlockSpec((br, bc), cb))
        else:
            in_specs.append(pl.BlockSpec((br, bc), functools.partial(lambda i, cb: (row(i), cb), cb=cb)))
        args.append(arr)
    for arr in full_in:
        in_specs.append(pl.BlockSpec(arr.shape, functools.partial(lambda i, nd: (0,) * nd, nd=arr.ndim)))
        args.append(arr)
    big_shapes, n_copies = [], 0
    for big in big_in:
        in_specs.append(pl.BlockSpec(memory_space=pl.ANY))
        if isinstance(big, _Pieces):
            args.append(big.gathered)
            big_shapes.append(((N_DEV * big.rows, PACK_COLS), big.gathered.dtype))
            n_copies += N_DEV
        else:
            args.append(big)
            big_shapes.append((big.shape, big.dtype))
            n_copies += 1
    if comm:
        in_specs.append(pl.BlockSpec(memory_space=pl.ANY))
        args.append(comm.src)
    out_specs, out_shape = [], []
    for rows, cols, dt, br in tiled_out:
        out_specs.append(pl.BlockSpec((br, cols), lambda i: (row(i), 0)))
        out_shape.append(jax.ShapeDtypeStruct((rows, cols), dt))
    for shp, dt in acc_out:
        out_specs.append(pl.BlockSpec(shp, functools.partial(lambda i, nd: (0,) * nd, nd=len(shp))))
        out_shape.append(jax.ShapeDtypeStruct(shp, dt))
    if comm:
        out_specs.append(pl.BlockSpec(memory_space=pl.ANY))
        out_shape.append(comm.dst)
    scratch_shapes = [pltpu.VMEM(shp, dt) for shp, dt in big_shapes] + list(scratch)
    if n_copies:
        scratch_shapes.append(pltpu.SemaphoreType.DMA((n_copies,)))
    if comm:
        scratch_shapes += list(comm.scratch)

    def kern(*refs):
        n_in = n_t + n_f + n_b + n_c
        ins = refs[: n_t + n_f]
        big_hbm = refs[n_t + n_f : n_t + n_f + n_b]
        outs = refs[n_in : n_in + n_to + n_a]
        rest = refs[n_in + n_to + n_a + n_c :]
        big_vmem, scr = rest[:n_b], rest[n_b:]
        if comm:
            scr, comm_scr = scr[:-len(comm.scratch)], scr[-len(comm.scratch):]
            comm_start, comm_mid, comm_finish = comm.phases(refs[n_in - 1], refs[n_in + n_to + n_a], *comm_scr)
        if n_copies:
            scr, copy_sems = scr[:-1], scr[-1]
        step = pl.program_id(0)

        @pl.when(step == 0)
        def _():
            copies = []
            for big, src, dst in zip(big_in, big_hbm, big_vmem):
                if isinstance(big, _Pieces):
                    for j in range(N_DEV):
                        copies.append((src.at[j, pl.ds(big.row_off, big.rows), :], dst.at[pl.ds(j * big.rows, big.rows), :]))
                else:
                    copies.append((src, dst))
            copies = [pltpu.make_async_copy(a, b, copy_sems.at[k]) for k, (a, b) in enumerate(copies)]
            for cp in copies:
                cp.start()
            for cp in copies:
                cp.wait()
            for acc in outs[n_to:]:
                acc[...] = jnp.zeros(acc.shape, acc.dtype)
            if comm:
                comm_start()

        body(row(step), *ins, *big_vmem, *outs, *scr)
        if comm:
            pl.when(step == (n_steps - 1) // 2)(comm_mid)
            pl.when(step == n_steps - 1)(comm_finish)

    res = pl.pallas_call(
        kern,
        out_shape=out_shape,
        grid=(n_steps,),
        in_specs=in_specs,
        out_specs=out_specs,
        scratch_shapes=scratch_shapes,
        name=name,
        compiler_params=pltpu.CompilerParams(dimension_semantics=("arbitrary",), vmem_limit_bytes=VMEM_LIMIT),
    )(*args)
    return res


FWD_CHUNKS = ((0, 1536), (1536, D_FF))
DGRAD_CHUNKS = ((0, 1024), (1024, 2048), (2048, D_FF))
FFN_TM = 256


def _ffn_fwd(h, g, wg_t, wu_t, wd, name, comm=None, mixed=None):
    T = h.shape[0]

    def ffn(x, g_ref, wg_ref, wu_ref, wd_ref, o_ref, n_ref, a_ref, b_ref, s_ref):
        n = _rms(x, g_ref[...]).astype(BF16)
        n_ref[...] = n
        f = jnp.zeros(x.shape, F32)
        for lo, hi in FWD_CHUNKS:
            a = _dot_nt(n, wg_ref[lo:hi, :])
            b = _dot_nt(n, wu_ref[lo:hi, :])
            s = (_silu(a) * b).astype(BF16)
            a_ref[:, lo:hi] = a.astype(BF16)
            b_ref[:, lo:hi] = b.astype(BF16)
            s_ref[:, lo:hi] = s
            f = f + jnp.dot(s, wd_ref[lo:hi, :], preferred_element_type=F32)
        o_ref[...] = x + 0.5 * f

    def body_plain(i, h_ref, *refs):
        ffn(h_ref[...], *refs)

    def body_mixed(i, h_ref, ya_ref, yb_ref, g_ref, wg_ref, wu_ref, wd_ref, wo_ref, o_ref, n_ref, a_ref, b_ref, s_ref, x_ref):
        x = (h_ref[...] + jnp.dot(ya_ref[...], wo_ref[:GM_WIDTH, :], preferred_element_type=F32)
             + jnp.dot(yb_ref[...], wo_ref[GM_WIDTH:, :], preferred_element_type=F32))
        x_ref[...] = x
        ffn(x, g_ref, wg_ref, wu_ref, wd_ref, o_ref, n_ref, a_ref, b_ref, s_ref)

    body = body_mixed if mixed else body_plain
    tiled_in, big_in = [(h, FFN_TM, D_MODEL, 0)], [wg_t, wu_t, wd]
    tiled_out = [(T, D_MODEL, F32, FFN_TM), (T, D_MODEL, BF16, FFN_TM), (T, D_FF, BF16, FFN_TM), (T, D_FF, BF16, FFN_TM),
                 (T, D_FF, BF16, FFN_TM)]
    if mixed:
        tiled_in += [(mixed[0], FFN_TM, GM_WIDTH, 0), (mixed[1], FFN_TM, SSM_WIDTH, 0)]
        big_in.append(mixed[2])
        tiled_out.append((T, D_MODEL, F32, FFN_TM))
    return _tiled(body, name, T // FFN_TM, tiled_in, [g], big_in, tiled_out, [], comm=comm)


def _ffn_dgrad(h, dout, a16, b16, g, wg_t, wu_t, wd, name):
    T = h.shape[0]

    def body(i, h_ref, do_ref, a_ref, b_ref, g_ref, wg_ref, wu_ref, wd_ref, dh_ref, da_ref, db_ref, dg_ref):
        dout = do_ref[...]
        _, rms_vjp = jax.vjp(_rms, h_ref[...], g_ref[...])
        dfo = (0.5 * dout).astype(BF16)
        dn = jnp.zeros(dout.shape, F32)
        for lo, hi in DGRAD_CHUNKS:
            a = a_ref[:, lo:hi].astype(F32)
            b = b_ref[:, lo:hi].astype(F32)
            sg = jax.nn.sigmoid(a)
            ds = _dot_nt(dfo, wd_ref[lo:hi, :])
            db = (ds * (a * sg)).astype(BF16)
            da = (ds * b * (sg * (1.0 + a * (1.0 - sg)))).astype(BF16)
            dn = dn + _dot(da, wg_ref[lo:hi, :]) + _dot(db, wu_ref[lo:hi, :])
            da_ref[:, lo:hi] = da
            db_ref[:, lo:hi] = db
        dx, dg = rms_vjp(dn)
        dh_ref[...] = dout + dx
        dg_ref[...] += dg

    return _tiled(body, name, T // FFN_TM,
                  [(h, FFN_TM, D_MODEL, 0), (dout, FFN_TM, D_MODEL, 0), (a16, FFN_TM, D_FF, 0), (b16, FFN_TM, D_FF, 0)],
                  [g], [wg_t, wu_t, wd],
                  [(T, D_MODEL, F32, FFN_TM), (T, D_FF, BF16, FFN_TM), (T, D_FF, BF16, FFN_TM)], [((1, D_MODEL), F32)])


FF_BN = D_FF // 2
DOWN_BN, DOWN_BK = 512, 1024
SQUARE_BN = 1024
ZXD_BN = ZXD_W // 3


def _wgrad(a, b, bn, name, scale=None, transpose_out=False, bk=2048, comm=None):
    T, M = a.shape
    N = b.shape[1]
    bk = min(bk, T)
    assert M % LANES == 0 and N % bn == 0 and T % bk == 0
    n_j, n_k = N // bn, T // bk
    n_c = 1 if comm else 0

    def kern(*refs):
        a_ref, b_ref, o_ref, acc_ref = refs[0], refs[1], refs[2 + n_c], refs[3 + 2 * n_c]
        j, k = pl.program_id(0), pl.program_id(1)
        if comm:
            comm_start, comm_mid, comm_finish = comm.phases(refs[2], refs[4], *refs[6:])
            pl.when((j == 0) & (k == 0))(comm_start)

        @pl.when(k == 0)
        def _():
            acc_ref[...] = jnp.zeros(acc_ref.shape, F32)

        bv = b_ref[...]
        if scale is not None:
            bv = bv * scale
        acc_ref[...] += _dot_tn(a_ref[...], bv)

        @pl.when(k == n_k - 1)
        def _():
            acc = acc_ref[...]
            o_ref[...] = (acc.T if transpose_out else acc).astype(BF16)

        if comm:
            pl.when((j == (n_j - 1) // 2) & (k == n_k - 1))(comm_mid)
            pl.when((j == n_j - 1) & (k == n_k - 1))(comm_finish)

    if transpose_out:
        out_shape, out_spec = (N, M), pl.BlockSpec((bn, M), lambda j, k: (j, 0))
    else:
        out_shape, out_spec = (M, N), pl.BlockSpec((M, bn), lambda j, k: (0, j))
    any_spec = pl.BlockSpec(memory_space=pl.ANY)
    res = pl.pallas_call(
        kern,
        out_shape=[jax.ShapeDtypeStruct(out_shape, BF16)] + ([comm.dst] if comm else []),
        grid=(n_j, n_k),
        in_specs=[pl.BlockSpec((bk, M), lambda j, k: (k, 0)), pl.BlockSpec((bk, bn), lambda j, k: (k, j))] + [any_spec] * n_c,
        out_specs=[out_spec] + [any_spec] * n_c,
        scratch_shapes=[pltpu.VMEM((M, bn), F32)] + (list(comm.scratch) if comm else []),
        name=name,
        compiler_params=pltpu.CompilerParams(dimension_semantics=("arbitrary", "arbitrary"), vmem_limit_bytes=VMEM_LIMIT),
    )(a, b, *([comm.src] if comm else []))
    return res if comm else res[0]


PROJ_TM = 512
PROJ_DGRAD_TM = 256
UVZ_W = 2 * GM_WIDTH + SSM_WIDTH
PROJ_KEPT = UVZ_W + LANES
Z_BLK = 2 * GM_WIDTH // SSM_WIDTH
DT_BLK = UVZ_W // LANES


def _mix_in_fwd(h, g, w_in_t, conv_w, conv_b):
    T = h.shape[0]

    def body(i, h_ref, g_ref, cw_ref, cb_ref, w_ref, p_ref, n_ref, x_ref, xc_ref, ext_ref):
        @pl.when(i == 0)
        def _():
            ext_ref[0:HALO, :] = jnp.zeros((HALO, CONV_DIM), F32)

        n = _rms(h_ref[...], g_ref[...]).astype(BF16)
        n_ref[...] = n
        proj = _dot_nt(n, w_ref[...])
        p_ref[:, :UVZ_W] = proj[:, :UVZ_W]
        p_ref[:, UVZ_W:] = proj[:, UVZ_W + CONV_DIM:]
        xbc = proj[:, UVZ_W:UVZ_W + CONV_DIM]
        x_ref[...] = xbc.astype(BF16)
        ext_ref[HALO:, :] = xbc
        xc_ref[...] = _conv_taps(ext_ref, cw_ref[...], cb_ref[...], PROJ_TM)
        ext_ref[0:HALO, :] = ext_ref[PROJ_TM:PROJ_TM + HALO, :]

    return _tiled(body, "mix_in_fwd", T // PROJ_TM, [(h, PROJ_TM, D_MODEL, 0)], [g, conv_w, conv_b], [w_in_t],
                  [(T, PROJ_KEPT, F32, PROJ_TM), (T, D_MODEL, BF16, PROJ_TM), (T, CONV_DIM, BF16, PROJ_TM),
                   (T, CONV_DIM, F32, PROJ_TM)], [],
                  scratch=[pltpu.VMEM((HALO + PROJ_TM, CONV_DIM), F32)])


def _mix_in_dgrad(h, dh_in, dp_uv, dp_zxd, g, w_in_t, comm=None):
    T = h.shape[0]

    def body(i, h_ref, dh_ref, duv_ref, dzxd_ref, g_ref, w_ref, o_ref, dg_ref):
        dn = _dot(duv_ref[...], w_ref[:UV_W, :]) + _dot(dzxd_ref[...], w_ref[UV_W:, :])
        _, rms_vjp = jax.vjp(_rms, h_ref[...], g_ref[...])
        dx, dg = rms_vjp(dn)
        o_ref[...] = dh_ref[...] + dx
        dg_ref[...] += dg

    return _tiled(body, "mix_in_dgrad", T // PROJ_DGRAD_TM,
                  [(h, PROJ_DGRAD_TM, D_MODEL, 0), (dh_in, PROJ_DGRAD_TM, D_MODEL, 0), (dp_uv, PROJ_DGRAD_TM, UV_W, 0),
                   (dp_zxd, PROJ_DGRAD_TM, ZXD_W, 0)], [g], [w_in_t],
                  [(T, D_MODEL, F32, PROJ_DGRAD_TM)], [((1, D_MODEL), F32)], comm=comm)


def _out_proj_dgrad(dh, w_out):
    T = dh.shape[0]

    def body(i, dh_ref, w_ref, dya_ref, dyb_ref):
        d = dh_ref[...].astype(BF16)
        dya_ref[...] = _dot_nt(d, w_ref[:GM_WIDTH, :])
        dyb_ref[...] = _dot_nt(d, w_ref[GM_WIDTH:, :])

    rows = min(T, 2 * PROJ_TM)
    return _tiled(body, "out_proj_dgrad", T // rows, [(dh, rows, D_MODEL, 0)], [], [w_out],
                  [(T, GM_WIDTH, F32, rows), (T, SSM_WIDTH, F32, rows)], [])


def _gm_chunk(u, v, ln_g, ln_b, b_st, out_g, *w_heads):
    ug = _gelu(u)
    vg = _gelu(v)
    mu = jnp.mean(vg, axis=-1, keepdims=True)
    xc = vg - mu
    vn = xc * lax.rsqrt(jnp.mean(xc * xc, axis=-1, keepdims=True) + EPS) * ln_g + ln_b
    t_idx = lax.broadcasted_iota(jnp.int32, (CHUNK, CHUNK), 0)
    s_idx = lax.broadcasted_iota(jnp.int32, (CHUNK, CHUNK), 1)
    causal = t_idx >= s_idx
    mixed = []
    for hd in range(GM_HEADS):
        wm = jnp.where(causal, w_heads[hd], 0.0)
        cols = slice(hd * GM_HEAD_DIM, (hd + 1) * GM_HEAD_DIM)
        mixed.append(_dot(wm, vn[:, cols]) + b_st[:, hd:hd + 1])
    ya0 = ug * jnp.concatenate(mixed, axis=1)
    return _rms(ya0, out_g)


GM_FWD_CHUNKS = 4


def _gm_fwd(proj, ln_g, ln_b, w_s, b_st, out_g):
    T = proj.shape[0]

    rows = GM_FWD_CHUNKS * CHUNK

    def body(i, u_ref, v_ref, lg_ref, lb_ref, w_ref, bs_ref, og_ref, ya_ref):
        w_heads = [w_ref[hd] for hd in range(GM_HEADS)]
        for c in range(GM_FWD_CHUNKS):
            tok = pl.ds(c * CHUNK, CHUNK)
            ya = _gm_chunk(u_ref[tok, :], v_ref[tok, :], lg_ref[...], lb_ref[...], bs_ref[...], og_ref[...], *w_heads)
            ya_ref[tok, :] = ya.astype(BF16)

    return _tiled(body, "gmlp_fwd", T // rows, [(proj, rows, GM_WIDTH, 0), (proj, rows, GM_WIDTH, 1)],
                  [ln_g, ln_b, w_s, b_st, out_g], [], [(T, GM_WIDTH, BF16, rows)], [])[0]


def _gm_bwd(proj, dya, ln_g, ln_b, w_s, b_st, out_g):
    T = proj.shape[0]

    def body(i, u_ref, v_ref, dy_ref, lg_ref, lb_ref, w_ref, bs_ref, og_ref, duv_ref, dlg_ref, dlb_ref, dw_ref, dbs_ref,
             dog_ref):
        w_heads = [w_ref[hd] for hd in range(GM_HEADS)]
        _, vjp = jax.vjp(_gm_chunk, u_ref[...], v_ref[...], lg_ref[...], lb_ref[...], bs_ref[...], og_ref[...], *w_heads)
        grads = vjp(dy_ref[...])
        duv_ref[:, :GM_WIDTH] = grads[0].astype(BF16)
        duv_ref[:, GM_WIDTH:] = grads[1].astype(BF16)
        dlg_ref[...] += grads[2]
        dlb_ref[...] += grads[3]
        dbs_ref[...] += grads[4]
        dog_ref[...] += grads[5]
        for hd in range(GM_HEADS):
            dw_ref[hd] += grads[6 + hd]

    return _tiled(body, "gmlp_bwd", T // CHUNK,
                  [(proj, CHUNK, GM_WIDTH, 0), (proj, CHUNK, GM_WIDTH, 1), (dya, CHUNK, GM_WIDTH, 0)],
                  [ln_g, ln_b, w_s, b_st, out_g], [], [(T, UV_W, BF16, CHUNK)],
                  [((1, GM_WIDTH), F32), ((1, GM_WIDTH), F32), ((GM_HEADS, CHUNK, CHUNK), F32),
                   ((CHUNK, GM_HEADS), F32), ((1, GM_WIDTH), F32)])


def _ssd_chunk(xc, z, dtr, s_in, dt_bias, a_log, d_skip, norm_g):
    half = SSM_WIDTH // SSM_GROUPS
    l_idx = lax.broadcasted_iota(jnp.int32, (CHUNK, CHUNK), 0)
    s_idx = lax.broadcasted_iota(jnp.int32, (CHUNK, CHUNK), 1)
    causal = l_idx >= s_idx
    head_of_col = lax.broadcasted_iota(jnp.int32, (SSM_HEADS, SSM_WIDTH), 1) // SSM_HEAD_DIM
    expand = (head_of_col == lax.broadcasted_iota(jnp.int32, (SSM_HEADS, SSM_WIDTH), 0)).astype(BF16)

    xcs = _silu(xc)
    xs = xcs[:, :SSM_WIDTH]
    dt = jax.nn.softplus(dtr + dt_bias)
    adt = dt * (-jnp.exp(a_log))
    acs = _cumsum_rows(adt, causal.astype(BF16))
    acs_t = _cumsum_cols(adt, (l_idx <= s_idx).astype(BF16))
    tot = acs[CHUNK - 1:CHUNK, :]
    dt_w = _widen(dt, expand)
    out_decay_w = _widen(jnp.exp(acs), expand)
    state_decay_w = _widen(jnp.exp(tot - acs), expand)
    chunk_decay_w = _widen(jnp.exp(tot), expand)
    d_skip_w = _widen(d_skip, expand)
    xdt = xs * dt_w
    xdt_decayed = xdt * state_decay_w

    y_diag, y_off, states = [], [], []
    for grp in range(SSM_GROUPS):
        b0 = SSM_WIDTH + grp * SSM_STATE
        c0 = SSM_WIDTH + SSM_GROUPS * SSM_STATE + grp * SSM_STATE
        bm = xcs[:, b0:b0 + SSM_STATE].astype(BF16)
        cm = xcs[:, c0:c0 + SSM_STATE].astype(BF16)
        cb = _dot_nt(cm, bm)
        for k in range(grp * SSM_HEADS // SSM_GROUPS, (grp + 1) * SSM_HEADS // SSM_GROUPS):
            decay = jnp.exp(jnp.where(causal, acs[:, k:k + 1] - acs_t[k:k + 1, :], -jnp.inf))
            y_diag.append(_dot(cb * decay, xdt[:, k * SSM_HEAD_DIM:(k + 1) * SSM_HEAD_DIM]))
        cols = slice(grp * half, (grp + 1) * half)
        states.append(_dot_tn(bm, xdt_decayed[:, cols]))
        y_off.append(_dot(cm, s_in[:, cols]))
    y = jnp.concatenate(y_diag, axis=1) + jnp.concatenate(y_off, axis=1) * out_decay_w + xs * d_skip_w
    s_out = s_in * chunk_decay_w + jnp.concatenate(states, axis=1)
    y = y * _silu(z)
    normed = []
    for grp in range(SSM_GROUPS):
        yg = y[:, grp * half:(grp + 1) * half]
        normed.append(yg * lax.rsqrt(jnp.mean(yg * yg, axis=-1, keepdims=True) + EPS))
    return jnp.concatenate(normed, axis=1) * norm_g, s_out


def _sum_row_tiles(x):
    return x.reshape(x.shape[0] // F32_ROWS, F32_ROWS, x.shape[1]).sum(axis=0)


def _conv_taps(ext_ref, w, b, rows):
    y = b
    for k in range(SSM_CONV):
        y = y + w[k:k + 1, :] * ext_ref[pl.ds(HALO - (SSM_CONV - 1) + k, rows), :]
    return y


SSD_FWD_CHUNKS = 4


def _ssd_fwd(proj, xc, dt_bias, a_log, d_skip, norm_g, comm=None):
    T = proj.shape[0]
    n_chunks = T // CHUNK
    rows = SSD_FWD_CHUNKS * CHUNK

    def body(i, z_ref, xc_ref, dt_ref, dtb_ref, al_ref, dsk_ref, ng_ref, yb_ref, sin_ref, st_ref):
        @pl.when(i == 0)
        def _():
            st_ref[...] = jnp.zeros(st_ref.shape, F32)

        for c in range(SSD_FWD_CHUNKS):
            tok = pl.ds(c * CHUNK, CHUNK)
            s_in = st_ref[...]
            yb, s_out = _ssd_chunk(xc_ref[tok, :], z_ref[tok, :], dt_ref[tok, 0:SSM_HEADS], s_in, dtb_ref[...], al_ref[...],
                                   dsk_ref[...], ng_ref[...])
            yb_ref[tok, :] = yb.astype(BF16)
            sin_ref[pl.ds(c * SSM_STATE, SSM_STATE), :] = s_in
            st_ref[...] = s_out

    return _tiled(body, "ssd_fwd", T // rows,
                  [(proj, rows, SSM_WIDTH, Z_BLK), (xc, rows, CONV_DIM, 0), (proj, rows, LANES, DT_BLK)],
                  [dt_bias, a_log, d_skip, norm_g], [],
                  [(T, SSM_WIDTH, BF16, rows), (n_chunks * SSM_STATE, SSM_WIDTH, F32, SSD_FWD_CHUNKS * SSM_STATE)], [],
                  scratch=[pltpu.VMEM((SSM_STATE, SSM_WIDTH), F32)], comm=comm)


def _ssd_bwd(proj, x16, xc, dyb, s_all, conv_w, dt_bias, a_log, d_skip, norm_g, comm=None):
    T = proj.shape[0]
    n_chunks = T // CHUNK

    def body(i, z_ref, x_ref, xc_ref, dt_ref, dy_ref, sin_ref, cw_ref, dtb_ref, al_ref, dsk_ref, ng_ref,
             dzxd_ref, dcw_ref, dcb_ref, ddtb_ref, dal_ref, ddsk_ref, dng_ref, dext_ref, dst_ref, cw_acc, cb_acc):
        @pl.when(i == n_chunks - 1)
        def _():
            dext_ref[CHUNK:, :] = jnp.zeros((HALO, CONV_DIM), F32)
            dst_ref[...] = jnp.zeros(dst_ref.shape, F32)
            cw_acc[...] = jnp.zeros(cw_acc.shape, F32)
            cb_acc[...] = jnp.zeros(cb_acc.shape, F32)

        _, vjp = jax.vjp(_ssd_chunk, xc_ref[...], z_ref[...], dt_ref[:, 0:SSM_HEADS], sin_ref[...], dtb_ref[...], al_ref[...],
                         dsk_ref[...], ng_ref[...])
        dxc, dz, ddtr, ds_in, ddtb, dal, ddsk, dng = vjp((dy_ref[...], dst_ref[...]))
        dst_ref[...] = ds_in
        ddtb_ref[...] += ddtb
        dal_ref[...] += dal
        ddsk_ref[...] += ddsk
        dng_ref[...] += dng
        dext_ref[0:CHUNK, :] = dxc
        cw = cw_ref[...]
        x = x_ref[...].astype(F32)
        dx = jnp.zeros((CHUNK, CONV_DIM), F32)
        for k in range(SSM_CONV):
            shifted = dext_ref[pl.ds(SSM_CONV - 1 - k, CHUNK), :]
            dx = dx + cw[k:k + 1, :] * shifted
            cw_acc[k] += _sum_row_tiles(shifted * x)
        cb_acc[...] += _sum_row_tiles(dxc)

        @pl.when(i == 0)
        def _():
            dcw_ref[...] = jnp.sum(cw_acc[...], axis=1)
            dcb_ref[...] = jnp.sum(cb_acc[...], axis=0, keepdims=True)

        dext_ref[CHUNK:, :] = dext_ref[0:HALO, :]
        dzxd_ref[:, 0:SSM_WIDTH] = dz.astype(BF16)
        dzxd_ref[:, SSM_WIDTH:SSM_WIDTH + CONV_DIM] = dx.astype(BF16)
        dzxd_ref[:, SSM_WIDTH + CONV_DIM:] = jnp.concatenate(
            [ddtr, jnp.zeros((CHUNK, LANES - SSM_HEADS), F32)], axis=1).astype(BF16)

    return _tiled(body, "ssd_bwd", n_chunks,
                  [(proj, CHUNK, SSM_WIDTH, Z_BLK), (x16, CHUNK, CONV_DIM, 0), (xc, CHUNK, CONV_DIM, 0),
                   (proj, CHUNK, LANES, DT_BLK), (dyb, CHUNK, SSM_WIDTH, 0), (s_all, SSM_STATE, SSM_WIDTH, 0)],
                  [conv_w, dt_bias, a_log, d_skip, norm_g], [],
                  [(T, ZXD_W, BF16, CHUNK)],
                  [((SSM_CONV, CONV_DIM), F32), ((1, CONV_DIM), F32), ((1, SSM_HEADS), F32), ((1, SSM_HEADS), F32),
                   ((1, SSM_HEADS), F32), ((1, SSM_WIDTH), F32)],
                  scratch=[pltpu.VMEM((CHUNK + HALO, CONV_DIM), F32), pltpu.VMEM((SSM_STATE, SSM_WIDTH), F32),
                           pltpu.VMEM((SSM_CONV, F32_ROWS, CONV_DIM), F32), pltpu.VMEM((F32_ROWS, CONV_DIM), F32)],
                  reverse=True, comm=comm)


TAIL_TM = 512


def _tail(h, p, target, ple_norm, w_gate, b_gate, w_proj_t, final_norm):
    T = h.shape[0]

    def head(x, pre, pp, b_g, f_norm, tgt):
        gate = jax.nn.sigmoid(pre + b_g)
        out = _rms(x + gate * pp, f_norm)
        err = out - tgt
        return 0.5 * jnp.sum(jnp.mean(err * err, axis=-1, keepdims=True), axis=0, keepdims=True)

    def body(i, h_ref, p_ref, t_ref, pn_ref, bg_ref, fn_ref, wg_ref, wp_ref, dh_ref, loss_ref, dwg_ref, dwp_ref, dpn_ref,
             dbg_ref, dfn_ref):
        x = h_ref[...]
        n4f, n_vjp = jax.vjp(_rms, x, pn_ref[...])
        n4 = n4f.astype(BF16)
        pre = jnp.dot(n4, wg_ref[...], preferred_element_type=F32)
        p16 = p_ref[...].astype(BF16)
        pp = _dot_nt(p16, wp_ref[...])
        loss, h_vjp = jax.vjp(functools.partial(head, tgt=t_ref[...]), x, pre, pp, bg_ref[...], fn_ref[...])
        dx, dpre, dpp, dbg, dfn = h_vjp(jnp.ones((1, 1), F32))
        dpre16 = dpre.astype(BF16)
        dn4 = _dot_nt(dpre16, wg_ref[...])
        dx2, dpn = n_vjp(dn4)
        dh_ref[...] = dx + dx2
        loss_ref[...] += loss
        dwg_ref[...] += _dot_tn(n4, dpre16)
        dwp_ref[...] += _dot_tn(p16, dpp)
        dpn_ref[...] += dpn
        dbg_ref[...] += dbg
        dfn_ref[...] += dfn

    return _tiled(body, "tail", T // TAIL_TM,
                  [(h, TAIL_TM, D_MODEL, 0), (p, TAIL_TM, D_PLE, 0), (target, TAIL_TM, D_MODEL, 0)],
                  [ple_norm, b_gate, final_norm], [w_gate, w_proj_t],
                  [(T, D_MODEL, F32, TAIL_TM)],
                  [((1, 1), F32), ((D_MODEL, D_MODEL), F32), ((D_PLE, D_MODEL), F32), ((1, D_MODEL), F32),
                   ((1, D_MODEL), F32), ((1, D_MODEL), F32)])


def _gather_phases(x_ref, out_ref, send_sems, recv_sems, local_sem):
    mx, my, mc = lax.axis_index("x"), lax.axis_index("y"), lax.axis_index("c")
    me, sibling = (mx, my, mc), (mx, my, 1 - mc)
    chips = [(1 - mx, my), (mx, 1 - my), (1 - mx, 1 - my)]

    def rows(px, py, pc):
        return out_ref.at[4 * px + 2 * py + pc]

    def copy(k, block, to, src=None):
        return pltpu.make_async_remote_copy(
            src_ref=rows(*block) if src is None else src, dst_ref=rows(*block),
            send_sem=send_sems.at[k], recv_sem=recv_sems.at[k], device_id=to, device_id_type=MESH)

    mine = pltpu.make_async_copy(x_ref, rows(*me), local_sem)
    first = [copy(0, me, sibling, src=x_ref)] + [copy(1 + j, me, (*chip, mc), src=x_ref) for j, chip in enumerate(chips)]
    passed = [copy(4 + j, (*chip, mc), sibling) for j, chip in enumerate(chips)]

    def start():
        mine.start()
        for cp in first:
            cp.start()

    def mid():
        for j, chip in enumerate(chips):
            copy(1 + j, (*chip, mc), me).wait_recv()
            passed[j].start()

    def finish():
        copy(0, sibling, me).wait_recv()
        for j, chip in enumerate(chips):
            copy(4 + j, (*chip, 1 - mc), me).wait_recv()
        for cp in first + passed:
            cp.wait_send()
        mine.wait()

    return start, mid, finish


def _exchange_phases(x_ref, out_ref, send_sems, recv_sems, local_sem):
    mx, my, mc = lax.axis_index("x"), lax.axis_index("y"), lax.axis_index("c")
    me = 4 * mx + 2 * my + mc
    mine = pltpu.make_async_copy(x_ref.at[me], out_ref.at[me], local_sem)
    copies = []
    for k in range(1, N_DEV):
        px = 1 - mx if k & 4 else mx
        py = 1 - my if k & 2 else my
        pc = 1 - mc if k & 1 else mc
        copies.append(pltpu.make_async_remote_copy(
            src_ref=x_ref.at[4 * px + 2 * py + pc], dst_ref=out_ref.at[me], send_sem=send_sems.at[k - 1],
            recv_sem=recv_sems.at[k - 1], device_id=(px, py, pc), device_id_type=MESH))

    def start():
        mine.start()
        for cp in copies:
            cp.start()

    def finish():
        for cp in copies:
            cp.wait_recv()
        for cp in copies:
            cp.wait_send()
        mine.wait()

    return start, lambda: None, finish


def _chip_exchange_phases(x_ref, out_ref, mine, recv, sums, load_sems, pair_send, pair_recv, chip_send, chip_recv, out_sem):
    mx, my, mc = lax.axis_index("x"), lax.axis_index("y"), lax.axis_index("c")
    my_chip = 2 * mx + my
    load = [pltpu.make_async_copy(x_ref.at[2 * q + mc], mine.at[q], load_sems.at[q]) for q in range(N_CHIPS)]
    to_sibling = [pltpu.make_async_remote_copy(
        src_ref=x_ref.at[2 * q + 1 - mc], dst_ref=recv.at[q], send_sem=pair_send.at[q], recv_sem=pair_recv.at[q],
        device_id=(mx, my, 1 - mc), device_id_type=MESH) for q in range(N_CHIPS)]
    to_chips = []
    for k in range(1, N_CHIPS):
        px = 1 - mx if k & 2 else mx
        py = 1 - my if k & 1 else my
        to_chips.append(pltpu.make_async_remote_copy(
            src_ref=sums.at[2 * px + py], dst_ref=out_ref.at[my_chip], send_sem=chip_send.at[k - 1],
            recv_sem=chip_recv.at[k - 1], device_id=(px, py, mc), device_id_type=MESH))
    keep = pltpu.make_async_copy(sums.at[my_chip], out_ref.at[my_chip], out_sem)

    def start():
        for cp in load + to_sibling:
            cp.start()

    def mid():
        for cp in load:
            cp.wait()
        for cp in to_sibling:
            cp.wait_recv()
        for q in range(N_CHIPS):
            sums[q] = (mine[q].astype(F32) + recv[q].astype(F32)).astype(sums.dtype)
        for cp in to_chips + [keep]:
            cp.start()

    def finish():
        for cp in to_chips:
            cp.wait_recv()
        for cp in to_chips + to_sibling:
            cp.wait_send()
        keep.wait()

    return start, mid, finish


FLAT_SCRATCH = (pltpu.SemaphoreType.DMA((N_DEV - 1,)), pltpu.SemaphoreType.DMA((N_DEV - 1,)), pltpu.SemaphoreType.DMA)


def _gather_comm(x):
    return _Comm(_gather_phases, x, jax.ShapeDtypeStruct((N_DEV,) + x.shape, x.dtype), FLAT_SCRATCH)


def _exchange_comm(x):
    return _Comm(_exchange_phases, x, jax.ShapeDtypeStruct(x.shape, x.dtype), FLAT_SCRATCH)


def _chip_exchange_comm(x):
    stage = pltpu.VMEM((N_CHIPS,) + x.shape[1:], x.dtype)
    sems = [pltpu.SemaphoreType.DMA((n,)) for n in (N_CHIPS, N_CHIPS, N_CHIPS, N_CHIPS - 1, N_CHIPS - 1)]
    return _Comm(_chip_exchange_phases, x, jax.ShapeDtypeStruct((N_CHIPS,) + x.shape[1:], x.dtype),
                 (stage, stage, stage, *sems, pltpu.SemaphoreType.DMA))


def _comm_alone(comms, name):
    n = len(comms)

    def body(*refs):
        phases, first = [], 2 * n
        for k, comm in enumerate(comms):
            phases.append(comm.phases(refs[k], refs[n + k], *refs[first:first + len(comm.scratch)]))
            first += len(comm.scratch)
        for step in range(3):
            for phase in phases:
                phase[step]()

    any_spec = pl.BlockSpec(memory_space=pl.ANY)
    return pl.pallas_call(
        body,
        out_shape=[comm.dst for comm in comms],
        in_specs=[any_spec] * n,
        out_specs=[any_spec] * n,
        scratch_shapes=[shape for comm in comms for shape in comm.scratch],
        name=name,
        compiler_params=pltpu.CompilerParams(vmem_limit_bytes=VMEM_LIMIT),
    )(*[comm.src for comm in comms])


def _sum_parts(p_ref):
    g = p_ref[0].astype(F32)
    for j in range(1, p_ref.shape[0]):
        g = g + p_ref[j].astype(F32)
    return g


def _adamw_store(g, w_ref, m_ref, v_ref, g_ref, d_ref, nm_ref, nv_ref):
    m_new = ADAM_B1 * m_ref[...] + (1.0 - ADAM_B1) * g
    v_new = ADAM_B2 * v_ref[...] + (1.0 - ADAM_B2) * jnp.square(g)
    m_hat = m_new / (1.0 - ADAM_B1 ** ADAM_STEP)
    v_hat = v_new / (1.0 - ADAM_B2 ** ADAM_STEP)
    g_ref[...] = g
    d_ref[...] = -ADAM_LR * (m_hat / (jnp.sqrt(v_hat) + ADAM_EPS) + ADAM_WD * w_ref[...])
    nm_ref[...] = m_new
    nv_ref[...] = v_new


def _adamw_shard(parts, off, w, m, v, name, n_tiles):
    _, rows, c = w.shape
    assert c == PACK_COLS
    by_rows = rows % BF16_ROWS == 0
    if by_rows:
        tr = rows // n_tiles
        window = (parts.shape[0], tr, PACK_COLS)
        spec = pl.BlockSpec((None, tr, PACK_COLS), lambda i: (0, i, 0))
    else:
        padded, tc = -(-rows // BF16_ROWS) * BF16_ROWS, PACK_COLS // n_tiles
        window = (parts.shape[0], padded, tc)
        spec = pl.BlockSpec((None, rows, tc), lambda i: (0, 0, i))

    def kern(p_hbm, w_ref, m_ref, v_ref, g_ref, d_ref, nm_ref, nv_ref, buf, sem):
        i = pl.program_id(0)
        if by_rows:
            src = p_hbm.at[:, pl.ds(pl.multiple_of(off + i * tr, BF16_ROWS), tr), :]
        else:
            src = p_hbm.at[:, pl.ds(off, padded), pl.ds(pl.multiple_of(i * tc, LANES), tc)]
        cp = pltpu.make_async_copy(src, buf, sem)
        cp.start()
        cp.wait()
        g = _sum_parts(buf)
        if not by_rows:
            keep = lax.broadcasted_iota(jnp.int32, (rows, padded), 0) == lax.broadcasted_iota(jnp.int32, (rows, padded), 1)
            g = _exact_dot(g, keep.astype(BF16), ((1,), (0,)), x_first=False)
        _adamw_store(g, w_ref, m_ref, v_ref, g_ref, d_ref, nm_ref, nv_ref)

    return pl.pallas_call(
        kern,
        out_shape=[jax.ShapeDtypeStruct(w.shape, F32)] * 4,
        grid=(n_tiles,),
        in_specs=[pl.BlockSpec(memory_space=pl.ANY), spec, spec, spec],
        out_specs=[spec] * 4,
        scratch_shapes=[pltpu.VMEM(window, parts.dtype), pltpu.SemaphoreType.DMA],
        name=name,
        compiler_params=pltpu.CompilerParams(dimension_semantics=("arbitrary",), vmem_limit_bytes=VMEM_LIMIT),
    )(parts, w, m, v)


def _sum_adamw(parts, w, m, v, tr, name):
    _, R, C = parts.shape

    def kern(p_ref, w_ref, m_ref, v_ref, g_ref, d_ref, nm_ref, nv_ref):
        _adamw_store(_sum_parts(p_ref), w_ref, m_ref, v_ref, g_ref, d_ref, nm_ref, nv_ref)

    row_spec = pl.BlockSpec((tr, C), lambda i: (i, 0))
    return pl.pallas_call(
        kern,
        out_shape=[jax.ShapeDtypeStruct((R, C), F32)] * 4,
        grid=(R // tr,),
        in_specs=[pl.BlockSpec((N_DEV, tr, C), lambda i: (0, i, 0)), row_spec, row_spec, row_spec],
        out_specs=[row_spec] * 4,
        name=name,
        compiler_params=pltpu.CompilerParams(dimension_semantics=("arbitrary",), vmem_limit_bytes=VMEM_LIMIT),
    )(parts, w, m, v)


FF_SHARD = D_FF // N_DEV
CONV_SHARD = (SSM_CONV, CONV_DIM // N_DEV)
SHARDS = {"ffn1_w_gate": ((D_MODEL, FF_SHARD), True), "ffn1_w_up": ((D_MODEL, FF_SHARD), True),
          "ffn1_w_down": ((FF_SHARD, D_MODEL), False),
          "ffn2_w_gate": ((D_MODEL, FF_SHARD), True), "ffn2_w_up": ((D_MODEL, FF_SHARD), True),
          "ffn2_w_down": ((FF_SHARD, D_MODEL), False),
          "w_out": ((2 * D_MODEL // N_DEV, D_MODEL), False), "ple_w_gate": ((D_MODEL // N_DEV, D_MODEL), False),
          "w_in": ((D_MODEL, IN_PROJ // N_DEV), True), "ple_w_proj": ((D_PLE, D_MODEL // N_DEV), True),
          "conv_w": (CONV_SHARD, True),
          "conv_w_mid": (CONV_SHARD, True), "conv_w_low": (CONV_SHARD, True)}
BIG = tuple(name for name in SHARDS if not name.startswith("conv_w_"))
SMALL = ("ffn1_norm", "mix_norm", "gm_ln_g", "gm_ln_b", "gm_w_s", "gm_b_s", "gm_out_norm", "conv_b", "dt_bias", "a_log",
         "d_skip", "ssm_norm", "ffn2_norm", "ple_norm", "ple_b_gate", "final_norm")
SMALL_ROWS = 144


def _piece_rows(name):
    shape = SHARDS[name][0]
    return -(-(shape[0] * shape[1]) // PACK_COLS)


def _pad_cols(flat, name):
    pad = _piece_rows(name) * PACK_COLS - flat.shape[-1]
    return flat if pad == 0 else jnp.pad(flat, [(0, 0)] * (flat.ndim - 1) + [(0, pad)])


class _Pack:
    def __init__(self, names, tile_rows):
        self.names, self.tile_rows, self.offsets, off = names, tile_rows, {}, 0
        for name in names:
            self.offsets[name] = off
            off += _piece_rows(name)
        self.rows = -(-off // tile_rows) * tile_rows

    def pack_local(self, vals):
        parts = []
        for name in self.names:
            val = vals[name]
            parts.append(_pad_cols((val.T if SHARDS[name][1] else val).reshape(-1), name))
        flat = jnp.concatenate(parts)
        return jnp.pad(flat, (0, self.rows * PACK_COLS - flat.shape[0])).reshape(self.rows, PACK_COLS)

    def pack_owner_major(self, grads):
        parts, rows = [], 0
        for name in self.names:
            grad, piece_rows = grads[name].astype(BF16), _piece_rows(name)
            if grad.shape != (N_DEV * piece_rows, PACK_COLS):
                grad = _pad_cols(grad.reshape(N_DEV, -1), name)
            parts.append(grad.reshape(N_DEV, piece_rows, PACK_COLS))
            rows += piece_rows
        if rows < self.rows:
            parts.append(jnp.zeros((N_DEV, self.rows - rows, PACK_COLS), BF16))
        return parts[0] if len(parts) == 1 else jnp.concatenate(parts, axis=1)

    def gathered_piece(self, gathered, name):
        shape = SHARDS[name][0]
        rows = gathered[:, self.offsets[name]:self.offsets[name] + _piece_rows(name), :]
        return rows.reshape(N_DEV, -1)[:, :shape[0] * shape[1]]

    def pieces(self, gathered, name):
        return _Pieces(gathered, self.offsets[name], _piece_rows(name))


GATHER_FFN1 = _Pack(("ffn1_w_gate", "ffn1_w_up", "ffn1_w_down"), BF16_ROWS)
GATHER_MIX = _Pack(("w_out", "ple_w_gate", "w_in", "ple_w_proj", "conv_w", "conv_w_mid", "conv_w_low"), BF16_ROWS)
GATHER_FFN2 = _Pack(("ffn2_w_gate", "ffn2_w_up", "ffn2_w_down"), BF16_ROWS)
SCATTER_LATE = _Pack(("ffn2_w_gate", "ffn2_w_up", "ffn2_w_down", "w_out", "ple_w_gate", "ple_w_proj"), BF16_ROWS)
SCATTER_IN = _Pack(("w_in", "conv_w"), BF16_ROWS)
SCATTER_GATE = _Pack(("ffn1_w_gate",), BF16_ROWS)
SCATTER_UP = _Pack(("ffn1_w_up",), BF16_ROWS)
SCATTER_DOWN = _Pack(("ffn1_w_down",), BF16_ROWS)


def _pack_small(vals, behind=()):
    flat = jnp.concatenate([vals[name].reshape(-1).astype(F32) for name in SMALL] + [b.reshape(-1) for b in behind])
    return jnp.pad(flat, (0, SMALL_ROWS * PACK_COLS - flat.shape[0])).reshape(SMALL_ROWS, PACK_COLS)


def _unpack_small(packed, shapes):
    out, off = {}, 0
    flat = packed.reshape(-1)
    for name in SMALL:
        n = 1
        for s in shapes[name]:
            n *= s
        out[name] = flat[off:off + n].reshape(shapes[name])
        off += n
    return out


WEIGHTS = ("ffn1_norm", "ffn1_w_gate", "ffn1_w_up", "ffn1_w_down", "mix_norm", "w_in", "gm_ln_g", "gm_ln_b", "gm_w_s",
           "gm_b_s", "gm_out_norm", "conv_w", "conv_b", "dt_bias", "a_log", "d_skip", "ssm_norm", "w_out", "ffn2_norm",
           "ffn2_w_gate", "ffn2_w_up", "ffn2_w_down", "ple_norm", "ple_w_gate", "ple_b_gate", "ple_w_proj", "final_norm")


def _step(x, p, target, w, m, v):
    local = lambda d: {name: d[name][0] for name in BIG}

    shards = {name: val.astype(BF16) for name, val in local(w).items()}
    conv_high = lax.reduce_precision(w["conv_w"][0], 8, 7)
    conv_mid = lax.reduce_precision(w["conv_w"][0] - conv_high, 8, 7)
    shards["conv_w"] = conv_high.astype(BF16)
    shards["conv_w_mid"] = conv_mid.astype(BF16)
    shards["conv_w_low"] = (w["conv_w"][0] - conv_high - conv_mid).astype(BF16)
    g_ffn1 = _comm_alone([_gather_comm(GATHER_FFN1.pack_local(shards))], "gather_ffn1")[0]

    row = lambda name: w[name].reshape(1, -1)
    gm_w_s = w["gm_w_s"][0]
    gm_b_st = jnp.transpose(w["gm_b_s"][0])
    ffn1 = (row("ffn1_norm"),) + tuple(GATHER_FFN1.pieces(g_ffn1, name) for name in GATHER_FFN1.names)
    gm = (row("gm_ln_g"), row("gm_ln_b"), gm_w_s, gm_b_st, row("gm_out_norm"))

    h1, n1, a1, b1, s1, g_mix = _ffn_fwd(x, *ffn1, "ffn1_fwd", comm=_gather_comm(GATHER_MIX.pack_local(shards)))
    w_in_t = GATHER_MIX.gathered_piece(g_mix, "w_in").reshape(IN_PROJ, D_MODEL)
    w_in_t = jnp.concatenate([w_in_t, jnp.zeros((IN_PROJ_PAD - IN_PROJ, D_MODEL), BF16)], axis=0)
    w_proj_t = GATHER_MIX.gathered_piece(g_mix, "ple_w_proj").reshape(D_MODEL, D_PLE)
    conv_w = sum(GATHER_MIX.gathered_piece(g_mix, name).astype(F32) for name in ("conv_w", "conv_w_mid", "conv_w_low"))
    conv_w = conv_w.reshape(CONV_DIM, SSM_CONV).T
    ssd = (row("dt_bias"), row("a_log"), row("d_skip"), row("ssm_norm"))
    w_out = GATHER_MIX.pieces(g_mix, "w_out")

    proj, n2, x16, xc = _mix_in_fwd(h1, row("mix_norm"), w_in_t, conv_w, row("conv_b"))
    ya = _gm_fwd(proj, *gm)
    yb, s_all, g_ffn2 = _ssd_fwd(proj, xc, *ssd, comm=_gather_comm(GATHER_FFN2.pack_local(shards)))
    ffn2 = (row("ffn2_norm"),) + tuple(GATHER_FFN2.pieces(g_ffn2, name) for name in GATHER_FFN2.names)
    h3, n3, a3, b3, s3, h2 = _ffn_fwd(h1, *ffn2, "ffn2_fwd", mixed=(ya, yb, w_out))

    g, gp = {}, {}
    dh3, loss, gp["ple_w_gate"], d_w_proj, g["ple_norm"], g["ple_b_gate"], g["final_norm"] = _tail(
        h3, p, target, row("ple_norm"), GATHER_MIX.pieces(g_mix, "ple_w_gate"), row("ple_b_gate"), w_proj_t,
        row("final_norm"))
    gp["ple_w_proj"] = d_w_proj.T

    dh2, da3, db3, g["ffn2_norm"] = _ffn_dgrad(h2, dh3, a3, b3, *ffn2, "ffn2_dgrad")
    gp["ffn2_w_gate"] = _wgrad(n3, da3, FF_BN, "ffn2_wgrad_gate", transpose_out=True)
    gp["ffn2_w_up"] = _wgrad(n3, db3, FF_BN, "ffn2_wgrad_up", transpose_out=True)
    gp["ffn2_w_down"] = _wgrad(s3, dh3, DOWN_BN, "ffn2_wgrad_down", scale=0.5, bk=DOWN_BK)

    dya, dyb = _out_proj_dgrad(dh2, w_out)
    gp["w_out"] = jnp.concatenate([_wgrad(ya, dh2, SQUARE_BN, "w_out_wgrad_a"), _wgrad(yb, dh2, SQUARE_BN, "w_out_wgrad_b")], axis=0)

    dp_zxd, d_conv_w, g["conv_b"], g["dt_bias"], g["a_log"], g["d_skip"], g["ssm_norm"], parts_late = _ssd_bwd(
        proj, x16, xc, dyb, s_all, conv_w, *ssd, comm=_exchange_comm(SCATTER_LATE.pack_owner_major(gp)))
    gp["conv_w"] = d_conv_w.T
    dp_uv, g["gm_ln_g"], g["gm_ln_b"], g["gm_w_s"], dbst, g["gm_out_norm"] = _gm_bwd(proj, dya, *gm)
    g["gm_b_s"] = jnp.transpose(dbst)

    parts = {}
    gp["w_in"] = jnp.concatenate([_wgrad(n2, dp_uv, SQUARE_BN, "w_in_wgrad_uv", transpose_out=True),
                                  _wgrad(n2, dp_zxd, ZXD_BN, "w_in_wgrad_zxd", transpose_out=True)], axis=0)[:IN_PROJ]
    dh1, g["mix_norm"], parts[SCATTER_IN] = _mix_in_dgrad(h1, dh2, dp_uv, dp_zxd, row("mix_norm"), w_in_t,
                                                          comm=_exchange_comm(SCATTER_IN.pack_owner_major(gp)))

    dx, da1, db1, g["ffn1_norm"] = _ffn_dgrad(x, dh1, a1, b1, *ffn1, "ffn1_dgrad")
    gp["ffn1_w_gate"], small_parts = _wgrad(n1, da1, FF_BN, "ffn1_wgrad_gate", transpose_out=True,
                                            comm=_gather_comm(_pack_small(g, behind=[loss])))
    gp["ffn1_w_up"], parts[SCATTER_GATE] = _wgrad(n1, db1, FF_BN, "ffn1_wgrad_up", transpose_out=True,
                                                  comm=_chip_exchange_comm(SCATTER_GATE.pack_owner_major(gp)))
    gp["ffn1_w_down"], parts[SCATTER_UP] = _wgrad(s1, dh1, DOWN_BN, "ffn1_wgrad_down", scale=0.5, bk=DOWN_BK,
                                                  comm=_chip_exchange_comm(SCATTER_UP.pack_owner_major(gp)))
    parts[SCATTER_DOWN] = _comm_alone([_chip_exchange_comm(SCATTER_DOWN.pack_owner_major(gp))], "scatter_ffn1_down")[0]
    parts[SCATTER_LATE] = parts_late

    res_big = {}
    for pack, pack_parts in parts.items():
        for name in pack.names:
            shape, transposed = SHARDS[name]
            if name in ("ple_w_proj", "conv_w"):
                nat = pack.gathered_piece(pack_parts, name).reshape((N_DEV,) + shape[::-1])
                res_big[name] = _sum_adamw(jnp.transpose(nat, (0, 2, 1)), w[name][0], m[name][0], v[name][0], shape[0],
                                           "adamw_" + name)
            else:
                flip = (lambda a: jnp.transpose(a, (0, 2, 1))) if transposed else (lambda a: a)
                res = _adamw_shard(pack_parts, pack.offsets[name], flip(w[name]), flip(m[name]), flip(v[name]),
                                   "adamw_" + name, n_tiles=4 if name == "w_in" else 2)
                res_big[name] = [flip(r) for r in res]

    small_shapes = {name: w[name].shape for name in SMALL}
    res_small = _sum_adamw(small_parts, _pack_small(w), _pack_small(m), _pack_small(v), SMALL_ROWS, "adamw_small")
    loss = res_small[0].reshape(-1)[sum(w[name].size for name in SMALL)]
    res_small = [_unpack_small(r, small_shapes) for r in res_small]

    outs = []
    for k in range(4):
        for name in WEIGHTS:
            if name in res_small[k]:
                outs.append(res_small[k][name])
            else:
                outs.append(res_big[name][k].reshape(w[name].shape))
    return loss, dx, outs


def kernel(x, p, ffn1_norm, ffn1_w_gate, ffn1_w_up, ffn1_w_down, mix_norm, w_in, gm_ln_g, gm_ln_b, gm_w_s, gm_b_s, gm_out_norm, conv_w, conv_b, dt_bias, a_log, d_skip, ssm_norm, w_out, ffn2_norm, ffn2_w_gate, ffn2_w_up, ffn2_w_down, ple_norm, ple_w_gate, ple_b_gate, ple_w_proj, final_norm, loss_target, m_ffn1_norm, m_ffn1_w_gate, m_ffn1_w_up, m_ffn1_w_down, m_mix_norm, m_w_in, m_gm_ln_g, m_gm_ln_b, m_gm_w_s, m_gm_b_s, m_gm_out_norm, m_conv_w, m_conv_b, m_dt_bias, m_a_log, m_d_skip, m_ssm_norm, m_w_out, m_ffn2_norm, m_ffn2_w_gate, m_ffn2_w_up, m_ffn2_w_down, m_ple_norm, m_ple_w_gate, m_ple_b_gate, m_ple_w_proj, m_final_norm, v_ffn1_norm, v_ffn1_w_gate, v_ffn1_w_up, v_ffn1_w_down, v_mix_norm, v_w_in, v_gm_ln_g, v_gm_ln_b, v_gm_w_s, v_gm_b_s, v_gm_out_norm, v_conv_w, v_conv_b, v_dt_bias, v_a_log, v_d_skip, v_ssm_norm, v_w_out, v_ffn2_norm, v_ffn2_w_gate, v_ffn2_w_up, v_ffn2_w_down, v_ple_norm, v_ple_w_gate, v_ple_b_gate, v_ple_w_proj, v_final_norm):
    args = locals()
    w = {name: args[name] for name in WEIGHTS}
    m = {name: args["m_" + name] for name in WEIGHTS}
    v = {name: args["v_" + name] for name in WEIGHTS}
    loss, dx, outs = _step(x[0], p[0, 0], loss_target[0], w, m, v)
    return (loss, dx[None], *outs)
```

```python
import functools
from typing import NamedTuple

import jax
import jax.numpy as jnp
from jax import lax
from jax.experimental import pallas as pl
from jax.experimental.pallas import tpu as pltpu

F32 = jnp.float32
BF16 = jnp.bfloat16
MESH = pl.DeviceIdType.MESH
N_DEV = 8
N_CHIPS = 4

D_MODEL = 1024
D_FF = 2816
D_PLE = 256
GM_WIDTH = 1024
GM_HEADS = 8
GM_HEAD_DIM = 128
CHUNK = 128
SSM_WIDTH = 1024
SSM_HEADS = 16
SSM_HEAD_DIM = 64
SSM_GROUPS = 2
SSM_STATE = 128
SSM_CONV = 4
CONV_DIM = SSM_WIDTH + 2 * SSM_GROUPS * SSM_STATE
IN_PROJ = 2 * GM_WIDTH + SSM_WIDTH + CONV_DIM + SSM_HEADS
LANES = 128
BF16_ROWS = 16
F32_ROWS = 8
IN_PROJ_PAD = IN_PROJ - SSM_HEADS + LANES
UV_W = 2 * GM_WIDTH
ZXD_W = IN_PROJ_PAD - UV_W
HALO = 8
EPS = 1e-6

ADAM_LR = 0.001
ADAM_B1 = 0.9
ADAM_B2 = 0.999
ADAM_EPS = 1e-08
ADAM_WD = 0.01
ADAM_STEP = 10

VMEM_LIMIT = 56 * 1024 * 1024
PACK_COLS = 1024


def _rms(x, g):
    return x * lax.rsqrt(jnp.mean(x * x, axis=-1, keepdims=True) + EPS) * g


def _gelu(x):
    return 0.5 * x * (1.0 + lax.erf(x * (2.0 ** -0.5)))


def _silu(x):
    return x * jax.nn.sigmoid(x)


def _dot(a, b):
    return jnp.dot(a.astype(BF16), b.astype(BF16), preferred_element_type=F32)


def _dot_nt(a, b):
    return lax.dot_general(a.astype(BF16), b.astype(BF16), (((1,), (1,)), ((), ())), preferred_element_type=F32)


def _dot_tn(a, b):
    return lax.dot_general(a.astype(BF16), b.astype(BF16), (((0,), (0,)), ((), ())), preferred_element_type=F32)


def _split3(x):
    hi = x.astype(BF16)
    rest = x - hi.astype(F32)
    mid = rest.astype(BF16)
    return hi, mid, (rest - mid.astype(F32)).astype(BF16)


def _exact_dot(x, mask, dims, x_first=True, n_terms=3):
    terms = [lax.dot_general(*((t, mask) if x_first else (mask, t)), (dims, ((), ())), preferred_element_type=F32)
             for t in _split3(x)[:n_terms]]
    total = terms[0]
    for term in terms[1:]:
        total = total + term
    return total


def _mask_product(fwd_dims, fwd_x_first, bwd_dims, bwd_x_first, bwd_terms=3):
    @jax.custom_vjp
    def product(x, mask):
        return _exact_dot(x, mask, fwd_dims, fwd_x_first)

    def fwd(x, mask):
        return product(x, mask), mask

    def bwd(mask, g):
        return _exact_dot(g, mask, bwd_dims, bwd_x_first, bwd_terms), jnp.zeros_like(mask)

    product.defvjp(fwd, bwd)
    return product


_widen = _mask_product(((1,), (0,)), True, ((1,), (1,)), True, bwd_terms=2)
_cumsum_rows = _mask_product(((1,), (0,)), False, ((0,), (0,)), False)
_cumsum_cols = _mask_product(((0,), (0,)), True, ((1,), (1,)), False)


class _Pieces(NamedTuple):
    gathered: jax.Array
    row_off: int
    rows: int


class _Comm(NamedTuple):
    phases: object
    src: jax.Array
    dst: jax.ShapeDtypeStruct
    scratch: tuple


def _tiled(body, name, n_steps, tiled_in, full_in, big_in, tiled_out, acc_out, scratch=(), reverse=False, comm=None):
    n_t, n_f, n_b, n_to, n_a = len(tiled_in), len(full_in), len(big_in), len(tiled_out), len(acc_out)
    n_c = 1 if comm else 0

    def row(i):
        return n_steps - 1 - i if reverse else i

    in_specs, args = [], []
    for arr, br, bc, cb in tiled_in:
        if callable(cb):
            in_specs.append(pl.BlockSpec((br, bc), cb))
        else:
            in_specs.append(pl.BlockSpec((br, bc), functools.partial(lambda i, cb: (row(i), cb), cb=cb)))
        args.append(arr)
    for arr in full_in:
        in_specs.append(pl.BlockSpec(arr.shape, functools.partial(lambda i, nd: (0,) * nd, nd=arr.ndim)))
        args.append(arr)
    big_shapes, n_copies = [], 0
    for big in big_in:
        in_specs.append(pl.BlockSpec(memory_space=pl.ANY))
        if isinstance(big, _Pieces):
            args.append(big.gathered)
            big_shapes.append(((N_DEV * big.rows, PACK_COLS), big.gathered.dtype))
            n_copies += N_DEV
        else:
            args.append(big)
            big_shapes.append((big.shape, big.dtype))
            n_copies += 1
    if comm:
        in_specs.append(pl.BlockSpec(memory_space=pl.ANY))
        args.append(comm.src)
    out_specs, out_shape = [], []
    for rows, cols, dt, br in tiled_out:
        out_specs.append(pl.BlockSpec((br, cols), lambda i: (row(i), 0)))
        out_shape.append(jax.ShapeDtypeStruct((rows, cols), dt))
    for shp, dt in acc_out:
        out_specs.append(pl.BlockSpec(shp, functools.partial(lambda i, nd: (0,) * nd, nd=len(shp))))
        out_shape.append(jax.ShapeDtypeStruct(shp, dt))
    if comm:
        out_specs.append(pl.BlockSpec(memory_space=pl.ANY))
        out_shape.append(comm.dst)
    scratch_shapes = [pltpu.VMEM(shp, dt) for shp, dt in big_shapes] + list(scratch)
    if n_copies:
        scratch_shapes.append(pltpu.SemaphoreType.DMA((n_copies,)))
    if comm:
        scratch_shapes += list(comm.scratch)

    def kern(*refs):
        n_in = n_t + n_f + n_b + n_c
        ins = refs[: n_t + n_f]
        big_hbm = refs[n_t + n_f : n_t + n_f + n_b]
        outs = refs[n_in : n_in + n_to + n_a]
        rest = refs[n_in + n_to + n_a + n_c :]
        big_vmem, scr = rest[:n_b], rest[n_b:]
        if comm:
            scr, comm_scr = scr[:-len(comm.scratch)], scr[-len(comm.scratch):]
            comm_start, comm_mid, comm_finish = comm.phases(refs[n_in - 1], refs[n_in + n_to + n_a], *comm_scr)
        if n_copies:
            scr, copy_sems = scr[:-1], scr[-1]
        step = pl.program_id(0)

        @pl.when(step == 0)
        def _():
            copies = []
            for big, src, dst in zip(big_in, big_hbm, big_vmem):
                if isinstance(big, _Pieces):
                    for j in range(N_DEV):
                        copies.append((src.at[j, pl.ds(big.row_off, big.rows), :], dst.at[pl.ds(j * big.rows, big.rows), :]))
                else:
                    copies.append((src, dst))
            copies = [pltpu.make_async_copy(a, b, copy_sems.at[k]) for k, (a, b) in enumerate(copies)]
            for cp in copies:
                cp.start()
            for cp in copies:
                cp.wait()
            for acc in outs[n_to:]:
                acc[...] = jnp.zeros(acc.shape, acc.dtype)
            if comm:
                comm_start()

        body(row(step), *ins, *big_vmem, *outs, *scr)
        if comm:
            pl.when(step == (n_steps - 1) // 2)(comm_mid)
            pl.when(step == n_steps - 1)(comm_finish)

    res = pl.pallas_call(
        kern,
        out_shape=out_shape,
        grid=(n_steps,),
        in_specs=in_specs,
        out_specs=out_specs,
        scratch_shapes=scratch_shapes,
        name=name,
        compiler_params=pltpu.CompilerParams(dimension_semantics=("arbitrary",), vmem_limit_bytes=VMEM_LIMIT),
    )(*args)
    return res


FWD_CHUNKS = ((0, 1536), (1536, D_FF))
DGRAD_CHUNKS = ((0, 1024), (1024, 2048), (2048, D_FF))
FFN_TM = 256


def _ffn_fwd(h, g, wg_t, wu_t, wd, name, comm=None, mixed=None):
    T = h.shape[0]

    def ffn(x, g_ref, wg_ref, wu_ref, wd_ref, o_ref, n_ref, a_ref, b_ref, s_ref):
        n = _rms(x, g_ref[...]).astype(BF16)
        n_ref[...] = n
        f = jnp.zeros(x.shape, F32)
        for lo, hi in FWD_CHUNKS:
            a = _dot_nt(n, wg_ref[lo:hi, :])
            b = _dot_nt(n, wu_ref[lo:hi, :])
            s = (_silu(a) * b).astype(BF16)
            a_ref[:, lo:hi] = a.astype(BF16)
            b_ref[:, lo:hi] = b.astype(BF16)
            s_ref[:, lo:hi] = s
            f = f + jnp.dot(s, wd_ref[lo:hi, :], preferred_element_type=F32)
        o_ref[...] = x + 0.5 * f

    def body_plain(i, h_ref, *refs):
        ffn(h_ref[...], *refs)

    def body_mixed(i, h_ref, ya_ref, yb_ref, g_ref, wg_ref, wu_ref, wd_ref, wo_ref, o_ref, n_ref, a_ref, b_ref, s_ref, x_ref):
        x = (h_ref[...] + jnp.dot(ya_ref[...], wo_ref[:GM_WIDTH, :], preferred_element_type=F32)
             + jnp.dot(yb_ref[...], wo_ref[GM_WIDTH:, :], preferred_element_type=F32))
        x_ref[...] = x
        ffn(x, g_ref, wg_ref, wu_ref, wd_ref, o_ref, n_ref, a_ref, b_ref, s_ref)

    body = body_mixed if mixed else body_plain
    tiled_in, big_in = [(h, FFN_TM, D_MODEL, 0)], [wg_t, wu_t, wd]
    tiled_out = [(T, D_MODEL, F32, FFN_TM), (T, D_MODEL, BF16, FFN_TM), (T, D_FF, BF16, FFN_TM), (T, D_FF, BF16, FFN_TM),
                 (T, D_FF, BF16, FFN_TM)]
    if mixed:
        tiled_in += [(mixed[0], FFN_TM, GM_WIDTH, 0), (mixed[1], FFN_TM, SSM_WIDTH, 0)]
        big_in.append(mixed[2])
        tiled_out.append((T, D_MODEL, F32, FFN_TM))
    return _tiled(body, name, T // FFN_TM, tiled_in, [g], big_in, tiled_out, [], comm=comm)


def _ffn_dgrad(h, dout, a16, b16, g, wg_t, wu_t, wd, name):
    T = h.shape[0]

    def body(i, h_ref, do_ref, a_ref, b_ref, g_ref, wg_ref, wu_ref, wd_ref, dh_ref, da_ref, db_ref, dg_ref):
        dout = do_ref[...]
        _, rms_vjp = jax.vjp(_rms, h_ref[...], g_ref[...])
        dfo = (0.5 * dout).astype(BF16)
        dn = jnp.zeros(dout.shape, F32)
        for lo, hi in DGRAD_CHUNKS:
            a = a_ref[:, lo:hi].astype(F32)
            b = b_ref[:, lo:hi].astype(F32)
            sg = jax.nn.sigmoid(a)
            ds = _dot_nt(dfo, wd_ref[lo:hi, :])
            db = (ds * (a * sg)).astype(BF16)
            da = (ds * b * (sg * (1.0 + a * (1.0 - sg)))).astype(BF16)
            dn = dn + _dot(da, wg_ref[lo:hi, :]) + _dot(db, wu_ref[lo:hi, :])
            da_ref[:, lo:hi] = da
            db_ref[:, lo:hi] = db
        dx, dg = rms_vjp(dn)
        dh_ref[...] = dout + dx
        dg_ref[...] += dg

    return _tiled(body, name, T // FFN_TM,
                  [(h, FFN_TM, D_MODEL, 0), (dout, FFN_TM, D_MODEL, 0), (a16, FFN_TM, D_FF, 0), (b16, FFN_TM, D_FF, 0)],
                  [g], [wg_t, wu_t, wd],
                  [(T, D_MODEL, F32, FFN_TM), (T, D_FF, BF16, FFN_TM), (T, D_FF, BF16, FFN_TM)], [((1, D_MODEL), F32)])


FF_BN = D_FF // 2
DOWN_BN, DOWN_BK = 512, 1024
SQUARE_BN = 1024
ZXD_BN = ZXD_W // 3


def _wgrad(a, b, bn, name, scale=None, transpose_out=False, bk=2048, comm=None):
    T, M = a.shape
    N = b.shape[1]
    bk = min(bk, T)
    assert M % LANES == 0 and N % bn == 0 and T % bk == 0
    n_j, n_k = N // bn, T // bk
    n_c = 1 if comm else 0

    def kern(*refs):
        a_ref, b_ref, o_ref, acc_ref = refs[0], refs[1], refs[2 + n_c], refs[3 + 2 * n_c]
        j, k = pl.program_id(0), pl.program_id(1)
        if comm:
            comm_start, comm_mid, comm_finish = comm.phases(refs[2], refs[4], *refs[6:])
            pl.when((j == 0) & (k == 0))(comm_start)

        @pl.when(k == 0)
        def _():
            acc_ref[...] = jnp.zeros(acc_ref.shape, F32)

        bv = b_ref[...]
        if scale is not None:
            bv = bv * scale
        acc_ref[...] += _dot_tn(a_ref[...], bv)

        @pl.when(k == n_k - 1)
        def _():
            acc = acc_ref[...]
            o_ref[...] = (acc.T if transpose_out else acc).astype(BF16)

        if comm:
            pl.when((j == (n_j - 1) // 2) & (k == n_k - 1))(comm_mid)
            pl.when((j == n_j - 1) & (k == n_k - 1))(comm_finish)

    if transpose_out:
        out_shape, out_spec = (N, M), pl.BlockSpec((bn, M), lambda j, k: (j, 0))
    else:
        out_shape, out_spec = (M, N), pl.BlockSpec((M, bn), lambda j, k: (0, j))
    any_spec = pl.BlockSpec(memory_space=pl.ANY)
    res = pl.pallas_call(
        kern,
        out_shape=[jax.ShapeDtypeStruct(out_shape, BF16)] + ([comm.dst] if comm else []),
        grid=(n_j, n_k),
        in_specs=[pl.BlockSpec((bk, M), lambda j, k: (k, 0)), pl.BlockSpec((bk, bn), lambda j, k: (k, j))] + [any_spec] * n_c,
        out_specs=[out_spec] + [any_spec] * n_c,
        scratch_shapes=[pltpu.VMEM((M, bn), F32)] + (list(comm.scratch) if comm else []),
        name=name,
        compiler_params=pltpu.CompilerParams(dimension_semantics=("arbitrary", "arbitrary"), vmem_limit_bytes=VMEM_LIMIT),
    )(a, b, *([comm.src] if comm else []))
    return res if comm else res[0]


PROJ_TM = 512
PROJ_DGRAD_TM = 256
UVZ_W = 2 * GM_WIDTH + SSM_WIDTH
PROJ_KEPT = UVZ_W + LANES
Z_BLK = 2 * GM_WIDTH // SSM_WIDTH
DT_BLK = UVZ_W // LANES


def _mix_in_fwd(h, g, w_in_t, conv_w, conv_b):
    T = h.shape[0]

    def body(i, h_ref, g_ref, cw_ref, cb_ref, w_ref, p_ref, n_ref, x_ref, xc_ref, ext_ref):
        @pl.when(i == 0)
        def _():
            ext_ref[0:HALO, :] = jnp.zeros((HALO, CONV_DIM), F32)

        n = _rms(h_ref[...], g_ref[...]).astype(BF16)
        n_ref[...] = n
        proj = _dot_nt(n, w_ref[...])
        p_ref[:, :UVZ_W] = proj[:, :UVZ_W]
        p_ref[:, UVZ_W:] = proj[:, UVZ_W + CONV_DIM:]
        xbc = proj[:, UVZ_W:UVZ_W + CONV_DIM]
        x_ref[...] = xbc.astype(BF16)
        ext_ref[HALO:, :] = xbc
        xc_ref[...] = _conv_taps(ext_ref, cw_ref[...], cb_ref[...], PROJ_TM)
        ext_ref[0:HALO, :] = ext_ref[PROJ_TM:PROJ_TM + HALO, :]

    return _tiled(body, "mix_in_fwd", T // PROJ_TM, [(h, PROJ_TM, D_MODEL, 0)], [g, conv_w, conv_b], [w_in_t],
                  [(T, PROJ_KEPT, F32, PROJ_TM), (T, D_MODEL, BF16, PROJ_TM), (T, CONV_DIM, BF16, PROJ_TM),
                   (T, CONV_DIM, F32, PROJ_TM)], [],
                  scratch=[pltpu.VMEM((HALO + PROJ_TM, CONV_DIM), F32)])


def _mix_in_dgrad(h, dh_in, dp_uv, dp_zxd, g, w_in_t, comm=None):
    T = h.shape[0]

    def body(i, h_ref, dh_ref, duv_ref, dzxd_ref, g_ref, w_ref, o_ref, dg_ref):
        dn = _dot(duv_ref[...], w_ref[:UV_W, :]) + _dot(dzxd_ref[...], w_ref[UV_W:, :])
        _, rms_vjp = jax.vjp(_rms, h_ref[...], g_ref[...])
        dx, dg = rms_vjp(dn)
        o_ref[...] = dh_ref[...] + dx
        dg_ref[...] += dg

    return _tiled(body, "mix_in_dgrad", T // PROJ_DGRAD_TM,
                  [(h, PROJ_DGRAD_TM, D_MODEL, 0), (dh_in, PROJ_DGRAD_TM, D_MODEL, 0), (dp_uv, PROJ_DGRAD_TM, UV_W, 0),
                   (dp_zxd, PROJ_DGRAD_TM, ZXD_W, 0)], [g], [w_in_t],
                  [(T, D_MODEL, F32, PROJ_DGRAD_TM)], [((1, D_MODEL), F32)], comm=comm)


def _out_proj_dgrad(dh, w_out):
    T = dh.shape[0]

    def body(i, dh_ref, w_ref, dya_ref, dyb_ref):
        d = dh_ref[...].astype(BF16)
        dya_ref[...] = _dot_nt(d, w_ref[:GM_WIDTH, :])
        dyb_ref[...] = _dot_nt(d, w_ref[GM_WIDTH:, :])

    rows = min(T, 2 * PROJ_TM)
    return _tiled(body, "out_proj_dgrad", T // rows, [(dh, rows, D_MODEL, 0)], [], [w_out],
                  [(T, GM_WIDTH, F32, rows), (T, SSM_WIDTH, F32, rows)], [])


def _gm_chunk(u, v, ln_g, ln_b, b_st, out_g, *w_heads):
    ug = _gelu(u)
    vg = _gelu(v)
    mu = jnp.mean(vg, axis=-1, keepdims=True)
    xc = vg - mu
    vn = xc * lax.rsqrt(jnp.mean(xc * xc, axis=-1, keepdims=True) + EPS) * ln_g + ln_b
    t_idx = lax.broadcasted_iota(jnp.int32, (CHUNK, CHUNK), 0)
    s_idx = lax.broadcasted_iota(jnp.int32, (CHUNK, CHUNK), 1)
    causal = t_idx >= s_idx
    mixed = []
    for hd in range(GM_HEADS):
        wm = jnp.where(causal, w_heads[hd], 0.0)
        cols = slice(hd * GM_HEAD_DIM, (hd + 1) * GM_HEAD_DIM)
        mixed.append(_dot(wm, vn[:, cols]) + b_st[:, hd:hd + 1])
    ya0 = ug * jnp.concatenate(mixed, axis=1)
    return _rms(ya0, out_g)


GM_FWD_CHUNKS = 4


def _gm_fwd(proj, ln_g, ln_b, w_s, b_st, out_g):
    T = proj.shape[0]

    rows = GM_FWD_CHUNKS * CHUNK

    def body(i, u_ref, v_ref, lg_ref, lb_ref, w_ref, bs_ref, og_ref, ya_ref):
        w_heads = [w_ref[hd] for hd in range(GM_HEADS)]
        for c in range(GM_FWD_CHUNKS):
            tok = pl.ds(c * CHUNK, CHUNK)
            ya = _gm_chunk(u_ref[tok, :], v_ref[tok, :], lg_ref[...], lb_ref[...], bs_ref[...], og_ref[...], *w_heads)
            ya_ref[tok, :] = ya.astype(BF16)

    return _tiled(body, "gmlp_fwd", T // rows, [(proj, rows, GM_WIDTH, 0), (proj, rows, GM_WIDTH, 1)],
                  [ln_g, ln_b, w_s, b_st, out_g], [], [(T, GM_WIDTH, BF16, rows)], [])[0]


def _gm_bwd(proj, dya, ln_g, ln_b, w_s, b_st, out_g):
    T = proj.shape[0]

    def body(i, u_ref, v_ref, dy_ref, lg_ref, lb_ref, w_ref, bs_ref, og_ref, duv_ref, dlg_ref, dlb_ref, dw_ref, dbs_ref,
             dog_ref):
        w_heads = [w_ref[hd] for hd in range(GM_HEADS)]
        _, vjp = jax.vjp(_gm_chunk, u_ref[...], v_ref[...], lg_ref[...], lb_ref[...], bs_ref[...], og_ref[...], *w_heads)
        grads = vjp(dy_ref[...])
        duv_ref[:, :GM_WIDTH] = grads[0].astype(BF16)
        duv_ref[:, GM_WIDTH:] = grads[1].astype(BF16)
        dlg_ref[...] += grads[2]
        dlb_ref[...] += grads[3]
        dbs_ref[...] += grads[4]
        dog_ref[...] += grads[5]
        for hd in range(GM_HEADS):
            dw_ref[hd] += grads[6 + hd]

    return _tiled(body, "gmlp_bwd", T // CHUNK,
                  [(proj, CHUNK, GM_WIDTH, 0), (proj, CHUNK, GM_WIDTH, 1), (dya, CHUNK, GM_WIDTH, 0)],
                  [ln_g, ln_b, w_s, b_st, out_g], [], [(T, UV_W, BF16, CHUNK)],
                  [((1, GM_WIDTH), F32), ((1, GM_WIDTH), F32), ((GM_HEADS, CHUNK, CHUNK), F32),
                   ((CHUNK, GM_HEADS), F32), ((1, GM_WIDTH), F32)])


def _ssd_chunk(xc, z, dtr, s_in, dt_bias, a_log, d_skip, norm_g):
    half = SSM_WIDTH // SSM_GROUPS
    l_idx = lax.broadcasted_iota(jnp.int32, (CHUNK, CHUNK), 0)
    s_idx = lax.broadcasted_iota(jnp.int32, (CHUNK, CHUNK), 1)
    causal = l_idx >= s_idx
    head_of_col = lax.broadcasted_iota(jnp.int32, (SSM_HEADS, SSM_WIDTH), 1) // SSM_HEAD_DIM
    expand = (head_of_col == lax.broadcasted_iota(jnp.int32, (SSM_HEADS, SSM_WIDTH), 0)).astype(BF16)

    xcs = _silu(xc)
    xs = xcs[:, :SSM_WIDTH]
    dt = jax.nn.softplus(dtr + dt_bias)
    adt = dt * (-jnp.exp(a_log))
    acs = _cumsum_rows(adt, causal.astype(BF16))
    acs_t = _cumsum_cols(adt, (l_idx <= s_idx).astype(BF16))
    tot = acs[CHUNK - 1:CHUNK, :]
    dt_w = _widen(dt, expand)
    out_decay_w = _widen(jnp.exp(acs), expand)
    state_decay_w = _widen(jnp.exp(tot - acs), expand)
    chunk_decay_w = _widen(jnp.exp(tot), expand)
    d_skip_w = _widen(d_skip, expand)
    xdt = xs * dt_w
    xdt_decayed = xdt * state_decay_w

    y_diag, y_off, states = [], [], []
    for grp in range(SSM_GROUPS):
        b0 = SSM_WIDTH + grp * SSM_STATE
        c0 = SSM_WIDTH + SSM_GROUPS * SSM_STATE + grp * SSM_STATE
        bm = xcs[:, b0:b0 + SSM_STATE].astype(BF16)
        cm = xcs[:, c0:c0 + SSM_STATE].astype(BF16)
        cb = _dot_nt(cm, bm)
        for k in range(grp * SSM_HEADS // SSM_GROUPS, (grp + 1) * SSM_HEADS // SSM_GROUPS):
            decay = jnp.exp(jnp.where(causal, acs[:, k:k + 1] - acs_t[k:k + 1, :], -jnp.inf))
            y_diag.append(_dot(cb * decay, xdt[:, k * SSM_HEAD_DIM:(k + 1) * SSM_HEAD_DIM]))
        cols = slice(grp * half, (grp + 1) * half)
        states.append(_dot_tn(bm, xdt_decayed[:, cols]))
        y_off.append(_dot(cm, s_in[:, cols]))
    y = jnp.concatenate(y_diag, axis=1) + jnp.concatenate(y_off, axis=1) * out_decay_w + xs * d_skip_w
    s_out = s_in * chunk_decay_w + jnp.concatenate(states, axis=1)
    y = y * _silu(z)
    normed = []
    for grp in range(SSM_GROUPS):
        yg = y[:, grp * half:(grp + 1) * half]
        normed.append(yg * lax.rsqrt(jnp.mean(yg * yg, axis=-1, keepdims=True) + EPS))
    return jnp.concatenate(normed, axis=1) * norm_g, s_out


def _sum_row_tiles(x):
    return x.reshape(x.shape[0] // F32_ROWS, F32_ROWS, x.shape[1]).sum(axis=0)


def _conv_taps(ext_ref, w, b, rows):
    y = b
    for k in range(SSM_CONV):
        y = y + w[k:k + 1, :] * ext_ref[pl.ds(HALO - (SSM_CONV - 1) + k, rows), :]
    return y


SSD_FWD_CHUNKS = 4


def _ssd_fwd(proj, xc, dt_bias, a_log, d_skip, norm_g, comm=None):
    T = proj.shape[0]
    n_chunks = T // CHUNK
    rows = SSD_FWD_CHUNKS * CHUNK

    def body(i, z_ref, xc_ref, dt_ref, dtb_ref, al_ref, dsk_ref, ng_ref, yb_ref, sin_ref, st_ref):
        @pl.when(i == 0)
        def _():
            st_ref[...] = jnp.zeros(st_ref.shape, F32)

        for c in range(SSD_FWD_CHUNKS):
            tok = pl.ds(c * CHUNK, CHUNK)
            s_in = st_ref[...]
            yb, s_out = _ssd_chunk(xc_ref[tok, :], z_ref[tok, :], dt_ref[tok, 0:SSM_HEADS], s_in, dtb_ref[...], al_ref[...],
                                   dsk_ref[...], ng_ref[...])
            yb_ref[tok, :] = yb.astype(BF16)
            sin_ref[pl.ds(c * SSM_STATE, SSM_STATE), :] = s_in
            st_ref[...] = s_out

    return _tiled(body, "ssd_fwd", T // rows,
                  [(proj, rows, SSM_WIDTH, Z_BLK), (xc, rows, CONV_DIM, 0), (proj, rows, LANES, DT_BLK)],
                  [dt_bias, a_log, d_skip, norm_g], [],
                  [(T, SSM_WIDTH, BF16, rows), (n_chunks * SSM_STATE, SSM_WIDTH, F32, SSD_FWD_CHUNKS * SSM_STATE)], [],
                  scratch=[pltpu.VMEM((SSM_STATE, SSM_WIDTH), F32)], comm=comm)


def _ssd_bwd(proj, x16, xc, dyb, s_all, conv_w, dt_bias, a_log, d_skip, norm_g, comm=None):
    T = proj.shape[0]
    n_chunks = T // CHUNK

    def body(i, z_ref, x_ref, xc_ref, dt_ref, dy_ref, sin_ref, cw_ref, dtb_ref, al_ref, dsk_ref, ng_ref,
             dzxd_ref, dcw_ref, dcb_ref, ddtb_ref, dal_ref, ddsk_ref, dng_ref, dext_ref, dst_ref, cw_acc, cb_acc):
        @pl.when(i == n_chunks - 1)
        def _():
            dext_ref[CHUNK:, :] = jnp.zeros((HALO, CONV_DIM), F32)
            dst_ref[...] = jnp.zeros(dst_ref.shape, F32)
            cw_acc[...] = jnp.zeros(cw_acc.shape, F32)
            cb_acc[...] = jnp.zeros(cb_acc.shape, F32)

        _, vjp = jax.vjp(_ssd_chunk, xc_ref[...], z_ref[...], dt_ref[:, 0:SSM_HEADS], sin_ref[...], dtb_ref[...], al_ref[...],
                         dsk_ref[...], ng_ref[...])
        dxc, dz, ddtr, ds_in, ddtb, dal, ddsk, dng = vjp((dy_ref[...], dst_ref[...]))
        dst_ref[...] = ds_in
        ddtb_ref[...] += ddtb
        dal_ref[...] += dal
        ddsk_ref[...] += ddsk
        dng_ref[...] += dng
        dext_ref[0:CHUNK, :] = dxc
        cw = cw_ref[...]
        x = x_ref[...].astype(F32)
        dx = jnp.zeros((CHUNK, CONV_DIM), F32)
        for k in range(SSM_CONV):
            shifted = dext_ref[pl.ds(SSM_CONV - 1 - k, CHUNK), :]
            dx = dx + cw[k:k + 1, :] * shifted
            cw_acc[k] += _sum_row_tiles(shifted * x)
        cb_acc[...] += _sum_row_tiles(dxc)

        @pl.when(i == 0)
        def _():
            dcw_ref[...] = jnp.sum(cw_acc[...], axis=1)
            dcb_ref[...] = jnp.sum(cb_acc[...], axis=0, keepdims=True)

        dext_ref[CHUNK:, :] = dext_ref[0:HALO, :]
        dzxd_ref[:, 0:SSM_WIDTH] = dz.astype(BF16)
        dzxd_ref[:, SSM_WIDTH:SSM_WIDTH + CONV_DIM] = dx.astype(BF16)
        dzxd_ref[:, SSM_WIDTH + CONV_DIM:] = jnp.concatenate(
            [ddtr, jnp.zeros((CHUNK, LANES - SSM_HEADS), F32)], axis=1).astype(BF16)

    return _tiled(body, "ssd_bwd", n_chunks,
                  [(proj, CHUNK, SSM_WIDTH, Z_BLK), (x16, CHUNK, CONV_DIM, 0), (xc, CHUNK, CONV_DIM, 0),
                   (proj, CHUNK, LANES, DT_BLK), (dyb, CHUNK, SSM_WIDTH, 0), (s_all, SSM_STATE, SSM_WIDTH, 0)],
                  [conv_w, dt_bias, a_log, d_skip, norm_g], [],
                  [(T, ZXD_W, BF16, CHUNK)],
                  [((SSM_CONV, CONV_DIM), F32), ((1, CONV_DIM), F32), ((1, SSM_HEADS), F32), ((1, SSM_HEADS), F32),
                   ((1, SSM_HEADS), F32), ((1, SSM_WIDTH), F32)],
                  scratch=[pltpu.VMEM((CHUNK + HALO, CONV_DIM), F32), pltpu.VMEM((SSM_STATE, SSM_WIDTH), F32),
                           pltpu.VMEM((SSM_CONV, F32_ROWS, CONV_DIM), F32), pltpu.VMEM((F32_ROWS, CONV_DIM), F32)],
                  reverse=True, comm=comm)


TAIL_TM = 512


def _tail(h, p, target, ple_norm, w_gate, b_gate, w_proj_t, final_norm):
    T = h.shape[0]

    def head(x, pre, pp, b_g, f_norm, tgt):
        gate = jax.nn.sigmoid(pre + b_g)
        out = _rms(x + gate * pp, f_norm)
        err = out - tgt
        return 0.5 * jnp.sum(jnp.mean(err * err, axis=-1, keepdims=True), axis=0, keepdims=True)

    def body(i, h_ref, p_ref, t_ref, pn_ref, bg_ref, fn_ref, wg_ref, wp_ref, dh_ref, loss_ref, dwg_ref, dwp_ref, dpn_ref,
             dbg_ref, dfn_ref):
        x = h_ref[...]
        n4f, n_vjp = jax.vjp(_rms, x, pn_ref[...])
        n4 = n4f.astype(BF16)
        pre = jnp.dot(n4, wg_ref[...], preferred_element_type=F32)
        p16 = p_ref[...].astype(BF16)
        pp = _dot_nt(p16, wp_ref[...])
        loss, h_vjp = jax.vjp(functools.partial(head, tgt=t_ref[...]), x, pre, pp, bg_ref[...], fn_ref[...])
        dx, dpre, dpp, dbg, dfn = h_vjp(jnp.ones((1, 1), F32))
        dpre16 = dpre.astype(BF16)
        dn4 = _dot_nt(dpre16, wg_ref[...])
        dx2, dpn = n_vjp(dn4)
        dh_ref[...] = dx + dx2
        loss_ref[...] += loss
        dwg_ref[...] += _dot_tn(n4, dpre16)
        dwp_ref[...] += _dot_tn(p16, dpp)
        dpn_ref[...] += dpn
        dbg_ref[...] += dbg
        dfn_ref[...] += dfn

    return _tiled(body, "tail", T // TAIL_TM,
                  [(h, TAIL_TM, D_MODEL, 0), (p, TAIL_TM, D_PLE, 0), (target, TAIL_TM, D_MODEL, 0)],
                  [ple_norm, b_gate, final_norm], [w_gate, w_proj_t],
                  [(T, D_MODEL, F32, TAIL_TM)],
                  [((1, 1), F32), ((D_MODEL, D_MODEL), F32), ((D_PLE, D_MODEL), F32), ((1, D_MODEL), F32),
                   ((1, D_MODEL), F32), ((1, D_MODEL), F32)])


def _gather_phases(x_ref, out_ref, send_sems, recv_sems, local_sem):
    mx, my, mc = lax.axis_index("x"), lax.axis_index("y"), lax.axis_index("c")
    me, sibling = (mx, my, mc), (mx, my, 1 - mc)
    chips = [(1 - mx, my), (mx, 1 - my), (1 - mx, 1 - my)]

    def rows(px, py, pc):
        return out_ref.at[4 * px + 2 * py + pc]

    def copy(k, block, to, src=None):
        return pltpu.make_async_remote_copy(
            src_ref=rows(*block) if src is None else src, dst_ref=rows(*block),
            send_sem=send_sems.at[k], recv_sem=recv_sems.at[k], device_id=to, device_id_type=MESH)

    mine = pltpu.make_async_copy(x_ref, rows(*me), local_sem)
    first = [copy(0, me, sibling, src=x_ref)] + [copy(1 + j, me, (*chip, mc), src=x_ref) for j, chip in enumerate(chips)]
    passed = [copy(4 + j, (*chip, mc), sibling) for j, chip in enumerate(chips)]

    def start():
        mine.start()
        for cp in first:
            cp.start()

    def mid():
        for j, chip in enumerate(chips):
            copy(1 + j, (*chip, mc), me).wait_recv()
            passed[j].start()

    def finish():
        copy(0, sibling, me).wait_recv()
        for j, chip in enumerate(chips):
            copy(4 + j, (*chip, 1 - mc), me).wait_recv()
        for cp in first + passed:
            cp.wait_send()
        mine.wait()

    return start, mid, finish


def _exchange_phases(x_ref, out_ref, send_sems, recv_sems, local_sem):
    mx, my, mc = lax.axis_index("x"), lax.axis_index("y"), lax.axis_index("c")
    me = 4 * mx + 2 * my + mc
    mine = pltpu.make_async_copy(x_ref.at[me], out_ref.at[me], local_sem)
    copies = []
    for k in range(1, N_DEV):
        px = 1 - mx if k & 4 else mx
        py = 1 - my if k & 2 else my
        pc = 1 - mc if k & 1 else mc
        copies.append(pltpu.make_async_remote_copy(
            src_ref=x_ref.at[4 * px + 2 * py + pc], dst_ref=out_ref.at[me], send_sem=send_sems.at[k - 1],
            recv_sem=recv_sems.at[k - 1], device_id=(px, py, pc), device_id_type=MESH))

    def start():
        mine.start()
        for cp in copies:
            cp.start()

    def finish():
        for cp in copies:
            cp.wait_recv()
        for cp in copies:
            cp.wait_send()
        mine.wait()

    return start, lambda: None, finish


def _chip_exchange_phases(x_ref, out_ref, mine, recv, sums, load_sems, pair_send, pair_recv, chip_send, chip_recv, out_sem):
    mx, my, mc = lax.axis_index("x"), lax.axis_index("y"), lax.axis_index("c")
    my_chip = 2 * mx + my
    load = [pltpu.make_async_copy(x_ref.at[2 * q + mc], mine.at[q], load_sems.at[q]) for q in range(N_CHIPS)]
    to_sibling = [pltpu.make_async_remote_copy(
        src_ref=x_ref.at[2 * q + 1 - mc], dst_ref=recv.at[q], send_sem=pair_send.at[q], recv_sem=pair_recv.at[q],
        device_id=(mx, my, 1 - mc), device_id_type=MESH) for q in range(N_CHIPS)]
    to_chips = []
    for k in range(1, N_CHIPS):
        px = 1 - mx if k & 2 else mx
        py = 1 - my if k & 1 else my
        to_chips.append(pltpu.make_async_remote_copy(
            src_ref=sums.at[2 * px + py], dst_ref=out_ref.at[my_chip], send_sem=chip_send.at[k - 1],
            recv_sem=chip_recv.at[k - 1], device_id=(px, py, mc), device_id_type=MESH))
    keep = pltpu.make_async_copy(sums.at[my_chip], out_ref.at[my_chip], out_sem)

    def start():
        for cp in load + to_sibling:
            cp.start()

    def mid():
        for cp in load:
            cp.wait()
        for cp in to_sibling:
            cp.wait_recv()
        for q in range(N_CHIPS):
            sums[q] = (mine[q].astype(F32) + recv[q].astype(F32)).astype(sums.dtype)
        for cp in to_chips + [keep]:
            cp.start()

    def finish():
        for cp in to_chips:
            cp.wait_recv()
        for cp in to_chips + to_sibling:
            cp.wait_send()
        keep.wait()

    return start, mid, finish


FLAT_SCRATCH = (pltpu.SemaphoreType.DMA((N_DEV - 1,)), pltpu.SemaphoreType.DMA((N_DEV - 1,)), pltpu.SemaphoreType.DMA)


def _gather_comm(x):
    return _Comm(_gather_phases, x, jax.ShapeDtypeStruct((N_DEV,) + x.shape, x.dtype), FLAT_SCRATCH)


def _exchange_comm(x):
    return _Comm(_exchange_phases, x, jax.ShapeDtypeStruct(x.shape, x.dtype), FLAT_SCRATCH)


def _chip_exchange_comm(x):
    stage = pltpu.VMEM((N_CHIPS,) + x.shape[1:], x.dtype)
    sems = [pltpu.SemaphoreType.DMA((n,)) for n in (N_CHIPS, N_CHIPS, N_CHIPS, N_CHIPS - 1, N_CHIPS - 1)]
    return _Comm(_chip_exchange_phases, x, jax.ShapeDtypeStruct((N_CHIPS,) + x.shape[1:], x.dtype),
                 (stage, stage, stage, *sems, pltpu.SemaphoreType.DMA))


def _comm_alone(comms, name):
    n = len(comms)

    def body(*refs):
        phases, first = [], 2 * n
        for k, comm in enumerate(comms):
            phases.append(comm.phases(refs[k], refs[n + k], *refs[first:first + len(comm.scratch)]))
            first += len(comm.scratch)
        for step in range(3):
            for phase in phases:
                phase[step]()

    any_spec = pl.BlockSpec(memory_space=pl.ANY)
    return pl.pallas_call(
        body,
        out_shape=[comm.dst for comm in comms],
        in_specs=[any_spec] * n,
        out_specs=[any_spec] * n,
        scratch_shapes=[shape for comm in comms for shape in comm.scratch],
        name=name,
        compiler_params=pltpu.CompilerParams(vmem_limit_bytes=VMEM_LIMIT),
    )(*[comm.src for comm in comms])


def _sum_parts(p_ref):
    g = p_ref[0].astype(F32)
    for j in range(1, p_ref.shape[0]):
        g = g + p_ref[j].astype(F32)
    return g


def _adamw_store(g, w_ref, m_ref, v_ref, g_ref, d_ref, nm_ref, nv_ref):
    m_new = ADAM_B1 * m_ref[...] + (1.0 - ADAM_B1) * g
    v_new = ADAM_B2 * v_ref[...] + (1.0 - ADAM_B2) * jnp.square(g)
    m_hat = m_new / (1.0 - ADAM_B1 ** ADAM_STEP)
    v_hat = v_new / (1.0 - ADAM_B2 ** ADAM_STEP)
    g_ref[...] = g
    d_ref[...] = -ADAM_LR * (m_hat / (jnp.sqrt(v_hat) + ADAM_EPS) + ADAM_WD * w_ref[...])
    nm_ref[...] = m_new
    nv_ref[...] = v_new


def _adamw_shard(parts, off, w, m, v, name, n_tiles):
    _, rows, c = w.shape
    assert c == PACK_COLS
    by_rows = rows % BF16_ROWS == 0
    if by_rows:
        tr = rows // n_tiles
        window = (parts.shape[0], tr, PACK_COLS)
        spec = pl.BlockSpec((None, tr, PACK_COLS), lambda i: (0, i, 0))
    else:
        padded, tc = -(-rows // BF16_ROWS) * BF16_ROWS, PACK_COLS // n_tiles
        window = (parts.shape[0], padded, tc)
        spec = pl.BlockSpec((None, rows, tc), lambda i: (0, 0, i))
    blocked = by_rows and off % tr == 0

    def update(p_ref, refs):
        g = _sum_parts(p_ref)
        if not by_rows:
            keep = lax.broadcasted_iota(jnp.int32, (rows, padded), 0) == lax.broadcasted_iota(jnp.int32, (rows, padded), 1)
            g = _exact_dot(g, keep.astype(BF16), ((1,), (0,)), x_first=False)
        _adamw_store(g, *refs)

    def kern_blocked(p_ref, *refs):
        update(p_ref, refs)

    def kern_copied(p_hbm, *refs):
        buf, sem = refs[-2:]
        i = pl.program_id(0)
        if by_rows:
            src = p_hbm.at[:, pl.ds(pl.multiple_of(off + i * tr, BF16_ROWS), tr), :]
        else:
            src = p_hbm.at[:, pl.ds(off, padded), pl.ds(pl.multiple_of(i * tc, LANES), tc)]
        cp = pltpu.make_async_copy(src, buf, sem)
        cp.start()
        cp.wait()
        update(buf, refs[:-2])

    if blocked:
        parts_spec, scratch = pl.BlockSpec(window, lambda i: (0, off // tr + i, 0)), []
    else:
        parts_spec, scratch = pl.BlockSpec(memory_space=pl.ANY), [pltpu.VMEM(window, parts.dtype), pltpu.SemaphoreType.DMA]
    return pl.pallas_call(
        kern_blocked if blocked else kern_copied,
        out_shape=[jax.ShapeDtypeStruct(w.shape, F32)] * 4,
        grid=(n_tiles,),
        in_specs=[parts_spec, spec, spec, spec],
        out_specs=[spec] * 4,
        scratch_shapes=scratch,
        name=name,
        compiler_params=pltpu.CompilerParams(dimension_semantics=("arbitrary",), vmem_limit_bytes=VMEM_LIMIT),
    )(parts, w, m, v)


def _sum_adamw(parts, w, m, v, tr, name):
    _, R, C = parts.shape

    def kern(p_ref, w_ref, m_ref, v_ref, g_ref, d_ref, nm_ref, nv_ref):
        _adamw_store(_sum_parts(p_ref), w_ref, m_ref, v_ref, g_ref, d_ref, nm_ref, nv_ref)

    row_spec = pl.BlockSpec((tr, C), lambda i: (i, 0))
    return pl.pallas_call(
        kern,
        out_shape=[jax.ShapeDtypeStruct((R, C), F32)] * 4,
        grid=(R // tr,),
        in_specs=[pl.BlockSpec((N_DEV, tr, C), lambda i: (0, i, 0)), row_spec, row_spec, row_spec],
        out_specs=[row_spec] * 4,
        name=name,
        compiler_params=pltpu.CompilerParams(dimension_semantics=("arbitrary",), vmem_limit_bytes=VMEM_LIMIT),
    )(parts, w, m, v)


FF_SHARD = D_FF // N_DEV
CONV_SHARD = (SSM_CONV, CONV_DIM // N_DEV)
SHARDS = {"ffn1_w_gate": ((D_MODEL, FF_SHARD), True), "ffn1_w_up": ((D_MODEL, FF_SHARD), True),
          "ffn1_w_down": ((FF_SHARD, D_MODEL), False),
          "ffn2_w_gate": ((D_MODEL, FF_SHARD), True), "ffn2_w_up": ((D_MODEL, FF_SHARD), True),
          "ffn2_w_down": ((FF_SHARD, D_MODEL), False),
          "w_out": ((2 * D_MODEL // N_DEV, D_MODEL), False), "ple_w_gate": ((D_MODEL // N_DEV, D_MODEL), False),
          "w_in": ((D_MODEL, IN_PROJ // N_DEV), True), "ple_w_proj": ((D_PLE, D_MODEL // N_DEV), True),
          "conv_w": (CONV_SHARD, True),
          "conv_w_mid": (CONV_SHARD, True), "conv_w_low": (CONV_SHARD, True)}
BIG = tuple(name for name in SHARDS if not name.startswith("conv_w_"))
SMALL = ("ffn1_norm", "mix_norm", "gm_ln_g", "gm_ln_b", "gm_w_s", "gm_b_s", "gm_out_norm", "conv_b", "dt_bias", "a_log",
         "d_skip", "ssm_norm", "ffn2_norm", "ple_norm", "ple_b_gate", "final_norm")
SMALL_ROWS = 144


def _piece_rows(name):
    shape = SHARDS[name][0]
    return -(-(shape[0] * shape[1]) // PACK_COLS)


def _pad_cols(flat, name):
    pad = _piece_rows(name) * PACK_COLS - flat.shape[-1]
    return flat if pad == 0 else jnp.pad(flat, [(0, 0)] * (flat.ndim - 1) + [(0, pad)])


class _Pack:
    def __init__(self, names, tile_rows):
        self.names, self.tile_rows, self.offsets, off = names, tile_rows, {}, 0
        for name in names:
            self.offsets[name] = off
            off += _piece_rows(name)
        self.rows = -(-off // tile_rows) * tile_rows

    def pack_local(self, vals):
        parts = []
        for name in self.names:
            val = vals[name]
            parts.append(_pad_cols((val.T if SHARDS[name][1] else val).reshape(-1), name))
        flat = jnp.concatenate(parts)
        return jnp.pad(flat, (0, self.rows * PACK_COLS - flat.shape[0])).reshape(self.rows, PACK_COLS)

    def pack_owner_major(self, grads):
        parts, rows = [], 0
        for name in self.names:
            grad, piece_rows = grads[name].astype(BF16), _piece_rows(name)
            if grad.shape != (N_DEV * piece_rows, PACK_COLS):
                grad = _pad_cols(grad.reshape(N_DEV, -1), name)
            parts.append(grad.reshape(N_DEV, piece_rows, PACK_COLS))
            rows += piece_rows
        if rows < self.rows:
            parts.append(jnp.zeros((N_DEV, self.rows - rows, PACK_COLS), BF16))
        return parts[0] if len(parts) == 1 else jnp.concatenate(parts, axis=1)

    def gathered_piece(self, gathered, name):
        shape = SHARDS[name][0]
        rows = gathered[:, self.offsets[name]:self.offsets[name] + _piece_rows(name), :]
        return rows.reshape(N_DEV, -1)[:, :shape[0] * shape[1]]

    def pieces(self, gathered, name):
        return _Pieces(gathered, self.offsets[name], _piece_rows(name))


GATHER_FFN1 = _Pack(("ffn1_w_gate", "ffn1_w_up", "ffn1_w_down"), BF16_ROWS)
GATHER_MIX = _Pack(("w_out", "ple_w_gate", "w_in", "ple_w_proj", "conv_w", "conv_w_mid", "conv_w_low"), BF16_ROWS)
GATHER_FFN2 = _Pack(("ffn2_w_gate", "ffn2_w_up", "ffn2_w_down"), BF16_ROWS)
SCATTER_LATE = _Pack(("ffn2_w_gate", "ffn2_w_up", "ffn2_w_down", "w_out", "ple_w_gate", "ple_w_proj"), BF16_ROWS)
SCATTER_IN = _Pack(("w_in", "conv_w"), BF16_ROWS)
SCATTER_GATE = _Pack(("ffn1_w_gate",), BF16_ROWS)
SCATTER_UP = _Pack(("ffn1_w_up",), BF16_ROWS)
SCATTER_DOWN = _Pack(("ffn1_w_down",), BF16_ROWS)


def _pack_small(vals, behind=()):
    flat = jnp.concatenate([vals[name].reshape(-1).astype(F32) for name in SMALL] + [b.reshape(-1) for b in behind])
    return jnp.pad(flat, (0, SMALL_ROWS * PACK_COLS - flat.shape[0])).reshape(SMALL_ROWS, PACK_COLS)


def _unpack_small(packed, shapes):
    out, off = {}, 0
    flat = packed.reshape(-1)
    for name in SMALL:
        n = 1
        for s in shapes[name]:
            n *= s
        out[name] = flat[off:off + n].reshape(shapes[name])
        off += n
    return out


WEIGHTS = ("ffn1_norm", "ffn1_w_gate", "ffn1_w_up", "ffn1_w_down", "mix_norm", "w_in", "gm_ln_g", "gm_ln_b", "gm_w_s",
           "gm_b_s", "gm_out_norm", "conv_w", "conv_b", "dt_bias", "a_log", "d_skip", "ssm_norm", "w_out", "ffn2_norm",
           "ffn2_w_gate", "ffn2_w_up", "ffn2_w_down", "ple_norm", "ple_w_gate", "ple_b_gate", "ple_w_proj", "final_norm")


def _step(x, p, target, w, m, v):
    local = lambda d: {name: d[name][0] for name in BIG}

    shards = {name: val.astype(BF16) for name, val in local(w).items()}
    conv_high = lax.reduce_precision(w["conv_w"][0], 8, 7)
    conv_mid = lax.reduce_precision(w["conv_w"][0] - conv_high, 8, 7)
    shards["conv_w"] = conv_high.astype(BF16)
    shards["conv_w_mid"] = conv_mid.astype(BF16)
    shards["conv_w_low"] = (w["conv_w"][0] - conv_high - conv_mid).astype(BF16)
    g_ffn1 = _comm_alone([_gather_comm(GATHER_FFN1.pack_local(shards))], "gather_ffn1")[0]

    row = lambda name: w[name].reshape(1, -1)
    gm_w_s = w["gm_w_s"][0]
    gm_b_st = jnp.transpose(w["gm_b_s"][0])
    ffn1 = (row("ffn1_norm"),) + tuple(GATHER_FFN1.pieces(g_ffn1, name) for name in GATHER_FFN1.names)
    gm = (row("gm_ln_g"), row("gm_ln_b"), gm_w_s, gm_b_st, row("gm_out_norm"))

    h1, n1, a1, b1, s1, g_mix = _ffn_fwd(x, *ffn1, "ffn1_fwd", comm=_gather_comm(GATHER_MIX.pack_local(shards)))
    w_in_t = GATHER_MIX.gathered_piece(g_mix, "w_in").reshape(IN_PROJ, D_MODEL)
    w_in_t = jnp.concatenate([w_in_t, jnp.zeros((IN_PROJ_PAD - IN_PROJ, D_MODEL), BF16)], axis=0)
    w_proj_t = GATHER_MIX.gathered_piece(g_mix, "ple_w_proj").reshape(D_MODEL, D_PLE)
    conv_w = sum(GATHER_MIX.gathered_piece(g_mix, name).astype(F32) for name in ("conv_w", "conv_w_mid", "conv_w_low"))
    conv_w = conv_w.reshape(CONV_DIM, SSM_CONV).T
    ssd = (row("dt_bias"), row("a_log"), row("d_skip"), row("ssm_norm"))
    w_out = GATHER_MIX.pieces(g_mix, "w_out")

    proj, n2, x16, xc = _mix_in_fwd(h1, row("mix_norm"), w_in_t, conv_w, row("conv_b"))
    ya = _gm_fwd(proj, *gm)
    yb, s_all, g_ffn2 = _ssd_fwd(proj, xc, *ssd, comm=_gather_comm(GATHER_FFN2.pack_local(shards)))
    ffn2 = (row("ffn2_norm"),) + tuple(GATHER_FFN2.pieces(g_ffn2, name) for name in GATHER_FFN2.names)
    h3, n3, a3, b3, s3, h2 = _ffn_fwd(h1, *ffn2, "ffn2_fwd", mixed=(ya, yb, w_out))

    g, gp = {}, {}
    dh3, loss, gp["ple_w_gate"], d_w_proj, g["ple_norm"], g["ple_b_gate"], g["final_norm"] = _tail(
        h3, p, target, row("ple_norm"), GATHER_MIX.pieces(g_mix, "ple_w_gate"), row("ple_b_gate"), w_proj_t,
        row("final_norm"))
    gp["ple_w_proj"] = d_w_proj.T

    dh2, da3, db3, g["ffn2_norm"] = _ffn_dgrad(h2, dh3, a3, b3, *ffn2, "ffn2_dgrad")
    gp["ffn2_w_gate"] = _wgrad(n3, da3, FF_BN, "ffn2_wgrad_gate", transpose_out=True)
    gp["ffn2_w_up"] = _wgrad(n3, db3, FF_BN, "ffn2_wgrad_up", transpose_out=True)
    gp["ffn2_w_down"] = _wgrad(s3, dh3, DOWN_BN, "ffn2_wgrad_down", scale=0.5, bk=DOWN_BK)

    dya, dyb = _out_proj_dgrad(dh2, w_out)
    gp["w_out"] = jnp.concatenate([_wgrad(ya, dh2, SQUARE_BN, "w_out_wgrad_a"), _wgrad(yb, dh2, SQUARE_BN, "w_out_wgrad_b")], axis=0)

    dp_zxd, d_conv_w, g["conv_b"], g["dt_bias"], g["a_log"], g["d_skip"], g["ssm_norm"], parts_late = _ssd_bwd(
        proj, x16, xc, dyb, s_all, conv_w, *ssd, comm=_exchange_comm(SCATTER_LATE.pack_owner_major(gp)))
    gp["conv_w"] = d_conv_w.T
    dp_uv, g["gm_ln_g"], g["gm_ln_b"], g["gm_w_s"], dbst, g["gm_out_norm"] = _gm_bwd(proj, dya, *gm)
    g["gm_b_s"] = jnp.transpose(dbst)

    parts = {}
    gp["w_in"] = jnp.concatenate([_wgrad(n2, dp_uv, SQUARE_BN, "w_in_wgrad_uv", transpose_out=True),
                                  _wgrad(n2, dp_zxd, ZXD_BN, "w_in_wgrad_zxd", transpose_out=True)], axis=0)[:IN_PROJ]
    dh1, g["mix_norm"], parts[SCATTER_IN] = _mix_in_dgrad(h1, dh2, dp_uv, dp_zxd, row("mix_norm"), w_in_t,
                                                          comm=_exchange_comm(SCATTER_IN.pack_owner_major(gp)))

    dx, da1, db1, g["ffn1_norm"] = _ffn_dgrad(x, dh1, a1, b1, *ffn1, "ffn1_dgrad")
    gp["ffn1_w_gate"], small_parts = _wgrad(n1, da1, FF_BN, "ffn1_wgrad_gate", transpose_out=True,
                                            comm=_gather_comm(_pack_small(g, behind=[loss])))
    gp["ffn1_w_up"], parts[SCATTER_GATE] = _wgrad(n1, db1, FF_BN, "ffn1_wgrad_up", transpose_out=True,
                                                  comm=_chip_exchange_comm(SCATTER_GATE.pack_owner_major(gp)))
    gp["ffn1_w_down"], parts[SCATTER_UP] = _wgrad(s1, dh1, DOWN_BN, "ffn1_wgrad_down", scale=0.5, bk=DOWN_BK,
                                                  comm=_chip_exchange_comm(SCATTER_UP.pack_owner_major(gp)))
    parts[SCATTER_DOWN] = _comm_alone([_chip_exchange_comm(SCATTER_DOWN.pack_owner_major(gp))], "scatter_ffn1_down")[0]
    parts[SCATTER_LATE] = parts_late

    res_big = {}
    for pack, pack_parts in parts.items():
        for name in pack.names:
            shape, transposed = SHARDS[name]
            if name in ("ple_w_proj", "conv_w"):
                nat = pack.gathered_piece(pack_parts, name).reshape((N_DEV,) + shape[::-1])
                res_big[name] = _sum_adamw(jnp.transpose(nat, (0, 2, 1)), w[name][0], m[name][0], v[name][0], shape[0],
                                           "adamw_" + name)
            else:
                flip = (lambda a: jnp.transpose(a, (0, 2, 1))) if transposed else (lambda a: a)
                res = _adamw_shard(pack_parts, pack.offsets[name], flip(w[name]), flip(m[name]), flip(v[name]),
                                   "adamw_" + name, n_tiles=4 if name == "w_in" else 2)
                res_big[name] = [flip(r) for r in res]

    small_shapes = {name: w[name].shape for name in SMALL}
    res_small = _sum_adamw(small_parts, _pack_small(w), _pack_small(m), _pack_small(v), SMALL_ROWS, "adamw_small")
    loss = res_small[0].reshape(-1)[sum(w[name].size for name in SMALL)]
    res_small = [_unpack_small(r, small_shapes) for r in res_small]

    outs = []
    for k in range(4):
        for name in WEIGHTS:
            if name in res_small[k]:
                outs.append(res_small[k][name])
            else:
                outs.append(res_big[name][k].reshape(w[name].shape))
    return loss, dx, outs


def kernel(x, p, ffn1_norm, ffn1_w_gate, ffn1_w_up, ffn1_w_down, mix_norm, w_in, gm_ln_g, gm_ln_b, gm_w_s, gm_b_s, gm_out_norm, conv_w, conv_b, dt_bias, a_log, d_skip, ssm_norm, w_out, ffn2_norm, ffn2_w_gate, ffn2_w_up, ffn2_w_down, ple_norm, ple_w_gate, ple_b_gate, ple_w_proj, final_norm, loss_target, m_ffn1_norm, m_ffn1_w_gate, m_ffn1_w_up, m_ffn1_w_down, m_mix_norm, m_w_in, m_gm_ln_g, m_gm_ln_b, m_gm_w_s, m_gm_b_s, m_gm_out_norm, m_conv_w, m_conv_b, m_dt_bias, m_a_log, m_d_skip, m_ssm_norm, m_w_out, m_ffn2_norm, m_ffn2_w_gate, m_ffn2_w_up, m_ffn2_w_down, m_ple_norm, m_ple_w_gate, m_ple_b_gate, m_ple_w_proj, m_final_norm, v_ffn1_norm, v_ffn1_w_gate, v_ffn1_w_up, v_ffn1_w_down, v_mix_norm, v_w_in, v_gm_ln_g, v_gm_ln_b, v_gm_w_s, v_gm_b_s, v_gm_out_norm, v_conv_w, v_conv_b, v_dt_bias, v_a_log, v_d_skip, v_ssm_norm, v_w_out, v_ffn2_norm, v_ffn2_w_gate, v_ffn2_w_up, v_ffn2_w_down, v_ple_norm, v_ple_w_gate, v_ple_b_gate, v_ple_w_proj, v_final_norm):
    args = locals()
    w = {name: args[name] for name in WEIGHTS}
    m = {name: args["m_" + name] for name in WEIGHTS}
    v = {name: args["v_" + name] for name in WEIGHTS}
    loss, dx, outs = _step(x[0], p[0, 0], loss_target[0], w, m, v)
    return (loss, dx[None], *outs)
```

```python
import functools
from typing import NamedTuple

import jax
import jax.numpy as jnp
from jax import lax
from jax.experimental import pallas as pl
from jax.experimental.pallas import tpu as pltpu

F32 = jnp.float32
BF16 = jnp.bfloat16
MESH = pl.DeviceIdType.MESH
N_DEV = 8
N_CHIPS = 4

D_MODEL = 1024
D_FF = 2816
D_PLE = 256
GM_WIDTH = 1024
GM_HEADS = 8
GM_HEAD_DIM = 128
CHUNK = 128
SSM_WIDTH = 1024
SSM_HEADS = 16
SSM_HEAD_DIM = 64
SSM_GROUPS = 2
SSM_STATE = 128
SSM_CONV = 4
CONV_DIM = SSM_WIDTH + 2 * SSM_GROUPS * SSM_STATE
IN_PROJ = 2 * GM_WIDTH + SSM_WIDTH + CONV_DIM + SSM_HEADS
LANES = 128
BF16_ROWS = 16
F32_ROWS = 8
IN_PROJ_PAD = IN_PROJ - SSM_HEADS + LANES
UV_W = 2 * GM_WIDTH
ZXD_W = IN_PROJ_PAD - UV_W
HALO = 8
EPS = 1e-6

ADAM_LR = 0.001
ADAM_B1 = 0.9
ADAM_B2 = 0.999
ADAM_EPS = 1e-08
ADAM_WD = 0.01
ADAM_STEP = 10

VMEM_LIMIT = 56 * 1024 * 1024
PACK_COLS = 1024


def _rms(x, g):
    return x * lax.rsqrt(jnp.mean(x * x, axis=-1, keepdims=True) + EPS) * g


def _gelu(x):
    return 0.5 * x * (1.0 + lax.erf(x * (2.0 ** -0.5)))


def _silu(x):
    return x * jax.nn.sigmoid(x)


def _dot(a, b):
    return jnp.dot(a.astype(BF16), b.astype(BF16), preferred_element_type=F32)


def _dot_nt(a, b):
    return lax.dot_general(a.astype(BF16), b.astype(BF16), (((1,), (1,)), ((), ())), preferred_element_type=F32)


def _dot_tn(a, b):
    return lax.dot_general(a.astype(BF16), b.astype(BF16), (((0,), (0,)), ((), ())), preferred_element_type=F32)


def _split3(x):
    hi = x.astype(BF16)
    rest = x - hi.astype(F32)
    mid = rest.astype(BF16)
    return hi, mid, (rest - mid.astype(F32)).astype(BF16)


def _exact_dot(x, mask, dims, x_first=True, n_terms=3):
    terms = [lax.dot_general(*((t, mask) if x_first else (mask, t)), (dims, ((), ())), preferred_element_type=F32)
             for t in _split3(x)[:n_terms]]
    total = terms[0]
    for term in terms[1:]:
        total = total + term
    return total


def _mask_product(fwd_dims, fwd_x_first, bwd_dims, bwd_x_first, bwd_terms=3):
    @jax.custom_vjp
    def product(x, mask):
        return _exact_dot(x, mask, fwd_dims, fwd_x_first)

    def fwd(x, mask):
        return product(x, mask), mask

    def bwd(mask, g):
        return _exact_dot(g, mask, bwd_dims, bwd_x_first, bwd_terms), jnp.zeros_like(mask)

    product.defvjp(fwd, bwd)
    return product


_widen = _mask_product(((1,), (0,)), True, ((1,), (1,)), True, bwd_terms=2)
_cumsum_rows = _mask_product(((1,), (0,)), False, ((0,), (0,)), False)
_cumsum_cols = _mask_product(((0,), (0,)), True, ((1,), (1,)), False)


class _Pieces(NamedTuple):
    gathered: jax.Array
    row_off: int
    rows: int


class _Comm(NamedTuple):
    phases: object
    src: jax.Array
    dst: jax.ShapeDtypeStruct
    scratch: tuple


def _tiled(body, name, n_steps, tiled_in, full_in, big_in, tiled_out, acc_out, scratch=(), reverse=False, comm=None):
    n_t, n_f, n_b, n_to, n_a = len(tiled_in), len(full_in), len(big_in), len(tiled_out), len(acc_out)
    n_c = 1 if comm else 0

    def row(i):
        return n_steps - 1 - i if reverse else i

    in_specs, args = [], []
    for arr, br, bc, cb in tiled_in:
        if callable(cb):
            in_specs.append(pl.BlockSpec((br, bc), cb))
        else:
            in_specs.append(pl.BlockSpec((br, bc), functools.partial(lambda i, cb: (row(i), cb), cb=cb)))
        args.append(arr)
    for arr in full_in:
        in_specs.append(pl.BlockSpec(arr.shape, functools.partial(lambda i, nd: (0,) * nd, nd=arr.ndim)))
        args.append(arr)
    big_shapes, n_copies = [], 0
    for big in big_in:
        in_specs.append(pl.BlockSpec(memory_space=pl.ANY))
        if isinstance(big, _Pieces):
            args.append(big.gathered)
            big_shapes.append(((N_DEV * big.rows, PACK_COLS), big.gathered.dtype))
            n_copies += N_DEV
        else:
            args.append(big)
            big_shapes.append((big.shape, big.dtype))
            n_copies += 1
    if comm:
        in_specs.append(pl.BlockSpec(memory_space=pl.ANY))
        args.append(comm.src)
    out_specs, out_shape = [], []
    for rows, cols, dt, br in tiled_out:
        out_specs.append(pl.BlockSpec((br, cols), lambda i: (row(i), 0)))
        out_shape.append(jax.ShapeDtypeStruct((rows, cols), dt))
    for shp, dt in acc_out:
        out_specs.append(pl.BlockSpec(shp, functools.partial(lambda i, nd: (0,) * nd, nd=len(shp))))
        out_shape.append(jax.ShapeDtypeStruct(shp, dt))
    if comm:
        out_specs.append(pl.BlockSpec(memory_space=pl.ANY))
        out_shape.append(comm.dst)
    scratch_shapes = [pltpu.VMEM(shp, dt) for shp, dt in big_shapes] + list(scratch)
    if n_copies:
        scratch_shapes.append(pltpu.SemaphoreType.DMA((n_copies,)))
    if comm:
        scratch_shapes += list(comm.scratch)

    def kern(*refs):
        n_in = n_t + n_f + n_b + n_c
        ins = refs[: n_t + n_f]
        big_hbm = refs[n_t + n_f : n_t + n_f + n_b]
        outs = refs[n_in : n_in + n_to + n_a]
        rest = refs[n_in + n_to + n_a + n_c :]
        big_vmem, scr = rest[:n_b], rest[n_b:]
        if comm:
            scr, comm_scr = scr[:-len(comm.scratch)], scr[-len(comm.scratch):]
            comm_start, comm_mid, comm_finish = comm.phases(refs[n_in - 1], refs[n_in + n_to + n_a], *comm_scr)
        if n_copies:
            scr, copy_sems = scr[:-1], scr[-1]
        step = pl.program_id(0)

        @pl.when(step == 0)
        def _():
            copies = []
            for big, src, dst in zip(big_in, big_hbm, big_vmem):
                if isinstance(big, _Pieces):
                    for j in range(N_DEV):
                        copies.append((src.at[j, pl.ds(big.row_off, big.rows), :], dst.at[pl.ds(j * big.rows, big.rows), :]))
                else:
                    copies.append((src, dst))
            copies = [pltpu.make_async_copy(a, b, copy_sems.at[k]) for k, (a, b) in enumerate(copies)]
            for cp in copies:
                cp.start()
            for cp in copies:
                cp.wait()
            for acc in outs[n_to:]:
                acc[...] = jnp.zeros(acc.shape, acc.dtype)
            if comm:
                comm_start()

        body(row(step), *ins, *big_vmem, *outs, *scr)
        if comm:
            pl.when(step == (n_steps - 1) // 2)(comm_mid)
            pl.when(step == n_steps - 1)(comm_finish)

    res = pl.pallas_call(
        kern,
        out_shape=out_shape,
        grid=(n_steps,),
        in_specs=in_specs,
        out_specs=out_specs,
        scratch_shapes=scratch_shapes,
        name=name,
        compiler_params=pltpu.CompilerParams(dimension_semantics=("arbitrary",), vmem_limit_bytes=VMEM_LIMIT),
    )(*args)
    return res


FWD_CHUNKS = ((0, 1536), (1536, D_FF))
DGRAD_CHUNKS = ((0, 1024), (1024, 2048), (2048, D_FF))
FFN_TM = 256


def _ffn_fwd(h, g, wg_t, wu_t, wd, name, comm=None, mixed=None):
    T = h.shape[0]

    def ffn(x, g_ref, wg_ref, wu_ref, wd_ref, o_ref, n_ref, a_ref, b_ref, s_ref):
        n = _rms(x, g_ref[...]).astype(BF16)
        n_ref[...] = n
        f = jnp.zeros(x.shape, F32)
        for lo, hi in FWD_CHUNKS:
            a = _dot_nt(n, wg_ref[lo:hi, :])
            b = _dot_nt(n, wu_ref[lo:hi, :])
            s = (_silu(a) * b).astype(BF16)
            a_ref[:, lo:hi] = a.astype(BF16)
            b_ref[:, lo:hi] = b.astype(BF16)
            s_ref[:, lo:hi] = s
            f = f + jnp.dot(s, wd_ref[lo:hi, :], preferred_element_type=F32)
        o_ref[...] = x + 0.5 * f

    def body_plain(i, h_ref, *refs):
        ffn(h_ref[...], *refs)

    def body_mixed(i, h_ref, ya_ref, yb_ref, g_ref, wg_ref, wu_ref, wd_ref, wo_ref, o_ref, n_ref, a_ref, b_ref, s_ref, x_ref):
        x = (h_ref[...] + jnp.dot(ya_ref[...], wo_ref[:GM_WIDTH, :], preferred_element_type=F32)
             + jnp.dot(yb_ref[...], wo_ref[GM_WIDTH:, :], preferred_element_type=F32))
        x_ref[...] = x
        ffn(x, g_ref, wg_ref, wu_ref, wd_ref, o_ref, n_ref, a_ref, b_ref, s_ref)

    body = body_mixed if mixed else body_plain
    tiled_in, big_in = [(h, FFN_TM, D_MODEL, 0)], [wg_t, wu_t, wd]
    tiled_out = [(T, D_MODEL, F32, FFN_TM), (T, D_MODEL, BF16, FFN_TM), (T, D_FF, BF16, FFN_TM), (T, D_FF, BF16, FFN_TM),
                 (T, D_FF, BF16, FFN_TM)]
    if mixed:
        tiled_in += [(mixed[0], FFN_TM, GM_WIDTH, 0), (mixed[1], FFN_TM, SSM_WIDTH, 0)]
        big_in.append(mixed[2])
        tiled_out.append((T, D_MODEL, F32, FFN_TM))
    return _tiled(body, name, T // FFN_TM, tiled_in, [g], big_in, tiled_out, [], comm=comm)


def _ffn_dgrad(h, dout, a16, b16, g, wg_t, wu_t, wd, name):
    T = h.shape[0]

    def body(i, h_ref, do_ref, a_ref, b_ref, g_ref, wg_ref, wu_ref, wd_ref, dh_ref, da_ref, db_ref, dg_ref):
        dout = do_ref[...]
        _, rms_vjp = jax.vjp(_rms, h_ref[...], g_ref[...])
        dfo = (0.5 * dout).astype(BF16)
        dn = jnp.zeros(dout.shape, F32)
        for lo, hi in DGRAD_CHUNKS:
            a = a_ref[:, lo:hi].astype(F32)
            b = b_ref[:, lo:hi].astype(F32)
            sg = jax.nn.sigmoid(a)
            ds = _dot_nt(dfo, wd_ref[lo:hi, :])
            db = (ds * (a * sg)).astype(BF16)
            da = (ds * b * (sg * (1.0 + a * (1.0 - sg)))).astype(BF16)
            dn = dn + _dot(da, wg_ref[lo:hi, :]) + _dot(db, wu_ref[lo:hi, :])
            da_ref[:, lo:hi] = da
            db_ref[:, lo:hi] = db
        dx, dg = rms_vjp(dn)
        dh_ref[...] = dout + dx
        dg_ref[...] += dg

    return _tiled(body, name, T // FFN_TM,
                  [(h, FFN_TM, D_MODEL, 0), (dout, FFN_TM, D_MODEL, 0), (a16, FFN_TM, D_FF, 0), (b16, FFN_TM, D_FF, 0)],
                  [g], [wg_t, wu_t, wd],
                  [(T, D_MODEL, F32, FFN_TM), (T, D_FF, BF16, FFN_TM), (T, D_FF, BF16, FFN_TM)], [((1, D_MODEL), F32)])


FF_BN = D_FF // 2
DOWN_BN, DOWN_BK = 512, 1024
SQUARE_BN = 1024
ZXD_BN = ZXD_W // 3


def _wgrad(a, b, bn, name, scale=None, transpose_out=False, bk=2048, comm=None):
    T, M = a.shape
    N = b.shape[1]
    bk = min(bk, T)
    assert M % LANES == 0 and N % bn == 0 and T % bk == 0
    n_j, n_k = N // bn, T // bk
    n_c = 1 if comm else 0

    def kern(*refs):
        a_ref, b_ref, o_ref, acc_ref = refs[0], refs[1], refs[2 + n_c], refs[3 + 2 * n_c]
        j, k = pl.program_id(0), pl.program_id(1)
        if comm:
            comm_start, comm_mid, comm_finish = comm.phases(refs[2], refs[4], *refs[6:])
            pl.when((j == 0) & (k == 0))(comm_start)

        @pl.when(k == 0)
        def _():
            acc_ref[...] = jnp.zeros(acc_ref.shape, F32)

        bv = b_ref[...]
        if scale is not None:
            bv = bv * scale
        acc_ref[...] += _dot_tn(a_ref[...], bv)

        @pl.when(k == n_k - 1)
        def _():
            acc = acc_ref[...]
            o_ref[...] = (acc.T if transpose_out else acc).astype(BF16)

        if comm:
            pl.when((j == (n_j - 1) // 2) & (k == n_k - 1))(comm_mid)
            pl.when((j == n_j - 1) & (k == n_k - 1))(comm_finish)

    if transpose_out:
        out_shape, out_spec = (N, M), pl.BlockSpec((bn, M), lambda j, k: (j, 0))
    else:
        out_shape, out_spec = (M, N), pl.BlockSpec((M, bn), lambda j, k: (0, j))
    any_spec = pl.BlockSpec(memory_space=pl.ANY)
    res = pl.pallas_call(
        kern,
        out_shape=[jax.ShapeDtypeStruct(out_shape, BF16)] + ([comm.dst] if comm else []),
        grid=(n_j, n_k),
        in_specs=[pl.BlockSpec((bk, M), lambda j, k: (k, 0)), pl.BlockSpec((bk, bn), lambda j, k: (k, j))] + [any_spec] * n_c,
        out_specs=[out_spec] + [any_spec] * n_c,
        scratch_shapes=[pltpu.VMEM((M, bn), F32)] + (list(comm.scratch) if comm else []),
        name=name,
        compiler_params=pltpu.CompilerParams(dimension_semantics=("arbitrary", "arbitrary"), vmem_limit_bytes=VMEM_LIMIT),
    )(a, b, *([comm.src] if comm else []))
    return res if comm else res[0]


PROJ_TM = 512
PROJ_DGRAD_TM = 256
UVZ_W = 2 * GM_WIDTH + SSM_WIDTH
PROJ_KEPT = UVZ_W + LANES
Z_BLK = 2 * GM_WIDTH // SSM_WIDTH
DT_BLK = UVZ_W // LANES


def _mix_in_fwd(h, g, w_in_t, conv_w, conv_b):
    T = h.shape[0]

    def body(i, h_ref, g_ref, cw_ref, cb_ref, w_ref, p_ref, n_ref, x_ref, xc_ref, ext_ref):
        @pl.when(i == 0)
        def _():
            ext_ref[0:HALO, :] = jnp.zeros((HALO, CONV_DIM), F32)

        n = _rms(h_ref[...], g_ref[...]).astype(BF16)
        n_ref[...] = n
        proj = _dot_nt(n, w_ref[...])
        p_ref[:, :UVZ_W] = proj[:, :UVZ_W]
        p_ref[:, UVZ_W:] = proj[:, UVZ_W + CONV_DIM:]
        xbc = proj[:, UVZ_W:UVZ_W + CONV_DIM]
        x_ref[...] = xbc.astype(BF16)
        ext_ref[HALO:, :] = xbc
        xc_ref[...] = _conv_taps(ext_ref, cw_ref[...], cb_ref[...], PROJ_TM)
        ext_ref[0:HALO, :] = ext_ref[PROJ_TM:PROJ_TM + HALO, :]

    return _tiled(body, "mix_in_fwd", T // PROJ_TM, [(h, PROJ_TM, D_MODEL, 0)], [g, conv_w, conv_b], [w_in_t],
                  [(T, PROJ_KEPT, F32, PROJ_TM), (T, D_MODEL, BF16, PROJ_TM), (T, CONV_DIM, BF16, PROJ_TM),
                   (T, CONV_DIM, F32, PROJ_TM)], [],
                  scratch=[pltpu.VMEM((HALO + PROJ_TM, CONV_DIM), F32)])


def _mix_in_dgrad(h, dh_in, dp_uv, dp_zxd, g, w_in_t, comm=None):
    T = h.shape[0]

    def body(i, h_ref, dh_ref, duv_ref, dzxd_ref, g_ref, w_ref, o_ref, dg_ref):
        dn = _dot(duv_ref[...], w_ref[:UV_W, :]) + _dot(dzxd_ref[...], w_ref[UV_W:, :])
        _, rms_vjp = jax.vjp(_rms, h_ref[...], g_ref[...])
        dx, dg = rms_vjp(dn)
        o_ref[...] = dh_ref[...] + dx
        dg_ref[...] += dg

    return _tiled(body, "mix_in_dgrad", T // PROJ_DGRAD_TM,
                  [(h, PROJ_DGRAD_TM, D_MODEL, 0), (dh_in, PROJ_DGRAD_TM, D_MODEL, 0), (dp_uv, PROJ_DGRAD_TM, UV_W, 0),
                   (dp_zxd, PROJ_DGRAD_TM, ZXD_W, 0)], [g], [w_in_t],
                  [(T, D_MODEL, F32, PROJ_DGRAD_TM)], [((1, D_MODEL), F32)], comm=comm)


def _out_proj_dgrad(dh, w_out):
    T = dh.shape[0]

    def body(i, dh_ref, w_ref, dya_ref, dyb_ref):
        d = dh_ref[...].astype(BF16)
        dya_ref[...] = _dot_nt(d, w_ref[:GM_WIDTH, :])
        dyb_ref[...] = _dot_nt(d, w_ref[GM_WIDTH:, :])

    rows = min(T, 2 * PROJ_TM)
    return _tiled(body, "out_proj_dgrad", T // rows, [(dh, rows, D_MODEL, 0)], [], [w_out],
                  [(T, GM_WIDTH, F32, rows), (T, SSM_WIDTH, F32, rows)], [])


def _gm_chunk(u, v, ln_g, ln_b, b_st, out_g, *w_heads):
    ug = _gelu(u)
    vg = _gelu(v)
    mu = jnp.mean(vg, axis=-1, keepdims=True)
    xc = vg - mu
    vn = xc * lax.rsqrt(jnp.mean(xc * xc, axis=-1, keepdims=True) + EPS) * ln_g + ln_b
    t_idx = lax.broadcasted_iota(jnp.int32, (CHUNK, CHUNK), 0)
    s_idx = lax.broadcasted_iota(jnp.int32, (CHUNK, CHUNK), 1)
    causal = t_idx >= s_idx
    mixed = []
    for hd in range(GM_HEADS):
        wm = jnp.where(causal, w_heads[hd], 0.0)
        cols = slice(hd * GM_HEAD_DIM, (hd + 1) * GM_HEAD_DIM)
        mixed.append(_dot(wm, vn[:, cols]) + b_st[:, hd:hd + 1])
    ya0 = ug * jnp.concatenate(mixed, axis=1)
    return _rms(ya0, out_g)


GM_FWD_CHUNKS = 4


def _gm_fwd(proj, ln_g, ln_b, w_s, b_st, out_g):
    T = proj.shape[0]

    rows = GM_FWD_CHUNKS * CHUNK

    def body(i, u_ref, v_ref, lg_ref, lb_ref, w_ref, bs_ref, og_ref, ya_ref):
        w_heads = [w_ref[hd] for hd in range(GM_HEADS)]
        for c in range(GM_FWD_CHUNKS):
            tok = pl.ds(c * CHUNK, CHUNK)
            ya = _gm_chunk(u_ref[tok, :], v_ref[tok, :], lg_ref[...], lb_ref[...], bs_ref[...], og_ref[...], *w_heads)
            ya_ref[tok, :] = ya.astype(BF16)

    return _tiled(body, "gmlp_fwd", T // rows, [(proj, rows, GM_WIDTH, 0), (proj, rows, GM_WIDTH, 1)],
                  [ln_g, ln_b, w_s, b_st, out_g], [], [(T, GM_WIDTH, BF16, rows)], [])[0]


def _gm_bwd(proj, dya, ln_g, ln_b, w_s, b_st, out_g):
    T = proj.shape[0]

    def body(i, u_ref, v_ref, dy_ref, lg_ref, lb_ref, w_ref, bs_ref, og_ref, duv_ref, dlg_ref, dlb_ref, dw_ref, dbs_ref,
             dog_ref):
        w_heads = [w_ref[hd] for hd in range(GM_HEADS)]
        _, vjp = jax.vjp(_gm_chunk, u_ref[...], v_ref[...], lg_ref[...], lb_ref[...], bs_ref[...], og_ref[...], *w_heads)
        grads = vjp(dy_ref[...])
        duv_ref[:, :GM_WIDTH] = grads[0].astype(BF16)
        duv_ref[:, GM_WIDTH:] = grads[1].astype(BF16)
        dlg_ref[...] += grads[2]
        dlb_ref[...] += grads[3]
        dbs_ref[...] += grads[4]
        dog_ref[...] += grads[5]
        for hd in range(GM_HEADS):
            dw_ref[hd] += grads[6 + hd]

    return _tiled(body, "gmlp_bwd", T // CHUNK,
                  [(proj, CHUNK, GM_WIDTH, 0), (proj, CHUNK, GM_WIDTH, 1), (dya, CHUNK, GM_WIDTH, 0)],
                  [ln_g, ln_b, w_s, b_st, out_g], [], [(T, UV_W, BF16, CHUNK)],
                  [((1, GM_WIDTH), F32), ((1, GM_WIDTH), F32), ((GM_HEADS, CHUNK, CHUNK), F32),
                   ((CHUNK, GM_HEADS), F32), ((1, GM_WIDTH), F32)])


def _ssd_chunk(xc, z, dtr, s_in, dt_bias, a_log, d_skip, norm_g):
    half = SSM_WIDTH // SSM_GROUPS
    l_idx = lax.broadcasted_iota(jnp.int32, (CHUNK, CHUNK), 0)
    s_idx = lax.broadcasted_iota(jnp.int32, (CHUNK, CHUNK), 1)
    causal = l_idx >= s_idx
    head_of_col = lax.broadcasted_iota(jnp.int32, (SSM_HEADS, SSM_WIDTH), 1) // SSM_HEAD_DIM
    expand = (head_of_col == lax.broadcasted_iota(jnp.int32, (SSM_HEADS, SSM_WIDTH), 0)).astype(BF16)

    xcs = _silu(xc)
    xs = xcs[:, :SSM_WIDTH]
    dt = jax.nn.softplus(dtr + dt_bias)
    adt = dt * (-jnp.exp(a_log))
    acs = _cumsum_rows(adt, causal.astype(BF16))
    acs_t = _cumsum_cols(adt, (l_idx <= s_idx).astype(BF16))
    tot = acs[CHUNK - 1:CHUNK, :]
    dt_w = _widen(dt, expand)
    out_decay_w = _widen(jnp.exp(acs), expand)
    state_decay_w = _widen(jnp.exp(tot - acs), expand)
    chunk_decay_w = _widen(jnp.exp(tot), expand)
    d_skip_w = _widen(d_skip, expand)
    xdt = xs * dt_w
    xdt_decayed = xdt * state_decay_w

    y_diag, y_off, states = [], [], []
    for grp in range(SSM_GROUPS):
        b0 = SSM_WIDTH + grp * SSM_STATE
        c0 = SSM_WIDTH + SSM_GROUPS * SSM_STATE + grp * SSM_STATE
        bm = xcs[:, b0:b0 + SSM_STATE].astype(BF16)
        cm = xcs[:, c0:c0 + SSM_STATE].astype(BF16)
        cb = _dot_nt(cm, bm)
        for k in range(grp * SSM_HEADS // SSM_GROUPS, (grp + 1) * SSM_HEADS // SSM_GROUPS):
            decay = jnp.exp(jnp.where(causal, acs[:, k:k + 1] - acs_t[k:k + 1, :], -jnp.inf))
            y_diag.append(_dot(cb * decay, xdt[:, k * SSM_HEAD_DIM:(k + 1) * SSM_HEAD_DIM]))
        cols = slice(grp * half, (grp + 1) * half)
        states.append(_dot_tn(bm, xdt_decayed[:, cols]))
        y_off.append(_dot(cm, s_in[:, cols]))
    y = jnp.concatenate(y_diag, axis=1) + jnp.concatenate(y_off, axis=1) * out_decay_w + xs * d_skip_w
    s_out = s_in * chunk_decay_w + jnp.concatenate(states, axis=1)
    y = y * _silu(z)
    normed = []
    for grp in range(SSM_GROUPS):
        yg = y[:, grp * half:(grp + 1) * half]
        normed.append(yg * lax.rsqrt(jnp.mean(yg * yg, axis=-1, keepdims=True) + EPS))
    return jnp.concatenate(normed, axis=1) * norm_g, s_out


def _sum_row_tiles(x):
    return x.reshape(x.shape[0] // F32_ROWS, F32_ROWS, x.shape[1]).sum(axis=0)


def _conv_taps(ext_ref, w, b, rows):
    y = b
    for k in range(SSM_CONV):
        y = y + w[k:k + 1, :] * ext_ref[pl.ds(HALO - (SSM_CONV - 1) + k, rows), :]
    return y


SSD_FWD_CHUNKS = 4


def _ssd_fwd(proj, xc, dt_bias, a_log, d_skip, norm_g, comm=None):
    T = proj.shape[0]
    n_chunks = T // CHUNK
    rows = SSD_FWD_CHUNKS * CHUNK

    def body(i, z_ref, xc_ref, dt_ref, dtb_ref, al_ref, dsk_ref, ng_ref, yb_ref, sin_ref, st_ref):
        @pl.when(i == 0)
        def _():
            st_ref[...] = jnp.zeros(st_ref.shape, F32)

        for c in range(SSD_FWD_CHUNKS):
            tok = pl.ds(c * CHUNK, CHUNK)
            s_in = st_ref[...]
            yb, s_out = _ssd_chunk(xc_ref[tok, :], z_ref[tok, :], dt_ref[tok, 0:SSM_HEADS], s_in, dtb_ref[...], al_ref[...],
                                   dsk_ref[...], ng_ref[...])
            yb_ref[tok, :] = yb.astype(BF16)
            sin_ref[pl.ds(c * SSM_STATE, SSM_STATE), :] = s_in
            st_ref[...] = s_out

    return _tiled(body, "ssd_fwd", T // rows,
                  [(proj, rows, SSM_WIDTH, Z_BLK), (xc, rows, CONV_DIM, 0), (proj, rows, LANES, DT_BLK)],
                  [dt_bias, a_log, d_skip, norm_g], [],
                  [(T, SSM_WIDTH, BF16, rows), (n_chunks * SSM_STATE, SSM_WIDTH, F32, SSD_FWD_CHUNKS * SSM_STATE)], [],
                  scratch=[pltpu.VMEM((SSM_STATE, SSM_WIDTH), F32)], comm=comm)


def _ssd_bwd(proj, x16, xc, dyb, s_all, conv_w, dt_bias, a_log, d_skip, norm_g, comm=None):
    T = proj.shape[0]
    n_chunks = T // CHUNK

    def body(i, z_ref, x_ref, xc_ref, dt_ref, dy_ref, sin_ref, cw_ref, dtb_ref, al_ref, dsk_ref, ng_ref,
             dzxd_ref, dcw_ref, dcb_ref, ddtb_ref, dal_ref, ddsk_ref, dng_ref, dext_ref, dst_ref, cw_acc, cb_acc):
        @pl.when(i == n_chunks - 1)
        def _():
            dext_ref[CHUNK:, :] = jnp.zeros((HALO, CONV_DIM), F32)
            dst_ref[...] = jnp.zeros(dst_ref.shape, F32)
            cw_acc[...] = jnp.zeros(cw_acc.shape, F32)
            cb_acc[...] = jnp.zeros(cb_acc.shape, F32)

        _, vjp = jax.vjp(_ssd_chunk, xc_ref[...], z_ref[...], dt_ref[:, 0:SSM_HEADS], sin_ref[...], dtb_ref[...], al_ref[...],
                         dsk_ref[...], ng_ref[...])
        dxc, dz, ddtr, ds_in, ddtb, dal, ddsk, dng = vjp((dy_ref[...], dst_ref[...]))
        dst_ref[...] = ds_in
        ddtb_ref[...] += ddtb
        dal_ref[...] += dal
        ddsk_ref[...] += ddsk
        dng_ref[...] += dng
        dext_ref[0:CHUNK, :] = dxc
        cw = cw_ref[...]
        x = x_ref[...].astype(F32)
        dx = jnp.zeros((CHUNK, CONV_DIM), F32)
        for k in range(SSM_CONV):
            shifted = dext_ref[pl.ds(SSM_CONV - 1 - k, CHUNK), :]
            dx = dx + cw[k:k + 1, :] * shifted
            cw_acc[k] += _sum_row_tiles(shifted * x)
        cb_acc[...] += _sum_row_tiles(dxc)

        @pl.when(i == 0)
        def _():
            dcw_ref[...] = jnp.sum(cw_acc[...], axis=1)
            dcb_ref[...] = jnp.sum(cb_acc[...], axis=0, keepdims=True)

        dext_ref[CHUNK:, :] = dext_ref[0:HALO, :]
        dzxd_ref[:, 0:SSM_WIDTH] = dz.astype(BF16)
        dzxd_ref[:, SSM_WIDTH:SSM_WIDTH + CONV_DIM] = dx.astype(BF16)
        dzxd_ref[:, SSM_WIDTH + CONV_DIM:] = jnp.concatenate(
            [ddtr, jnp.zeros((CHUNK, LANES - SSM_HEADS), F32)], axis=1).astype(BF16)

    return _tiled(body, "ssd_bwd", n_chunks,
                  [(proj, CHUNK, SSM_WIDTH, Z_BLK), (x16, CHUNK, CONV_DIM, 0), (xc, CHUNK, CONV_DIM, 0),
                   (proj, CHUNK, LANES, DT_BLK), (dyb, CHUNK, SSM_WIDTH, 0), (s_all, SSM_STATE, SSM_WIDTH, 0)],
                  [conv_w, dt_bias, a_log, d_skip, norm_g], [],
                  [(T, ZXD_W, BF16, CHUNK)],
                  [((SSM_CONV, CONV_DIM), F32), ((1, CONV_DIM), F32), ((1, SSM_HEADS), F32), ((1, SSM_HEADS), F32),
                   ((1, SSM_HEADS), F32), ((1, SSM_WIDTH), F32)],
                  scratch=[pltpu.VMEM((CHUNK + HALO, CONV_DIM), F32), pltpu.VMEM((SSM_STATE, SSM_WIDTH), F32),
                           pltpu.VMEM((SSM_CONV, F32_ROWS, CONV_DIM), F32), pltpu.VMEM((F32_ROWS, CONV_DIM), F32)],
                  reverse=True, comm=comm)


TAIL_TM = 512


def _tail(h, p, target, ple_norm, w_gate, b_gate, w_proj_t, final_norm):
    T = h.shape[0]

    def head(x, pre, pp, b_g, f_norm, tgt):
        gate = jax.nn.sigmoid(pre + b_g)
        out = _rms(x + gate * pp, f_norm)
        err = out - tgt
        return 0.5 * jnp.sum(jnp.mean(err * err, axis=-1, keepdims=True), axis=0, keepdims=True)

    def body(i, h_ref, p_ref, t_ref, pn_ref, bg_ref, fn_ref, wg_ref, wp_ref, dh_ref, loss_ref, dwg_ref, dwp_ref, dpn_ref,
             dbg_ref, dfn_ref):
        x = h_ref[...]
        n4f, n_vjp = jax.vjp(_rms, x, pn_ref[...])
        n4 = n4f.astype(BF16)
        pre = jnp.dot(n4, wg_ref[...], preferred_element_type=F32)
        p16 = p_ref[...].astype(BF16)
        pp = _dot_nt(p16, wp_ref[...])
        loss, h_vjp = jax.vjp(functools.partial(head, tgt=t_ref[...]), x, pre, pp, bg_ref[...], fn_ref[...])
        dx, dpre, dpp, dbg, dfn = h_vjp(jnp.ones((1, 1), F32))
        dpre16 = dpre.astype(BF16)
        dn4 = _dot_nt(dpre16, wg_ref[...])
        dx2, dpn = n_vjp(dn4)
        dh_ref[...] = dx + dx2
        loss_ref[...] += loss
        dwg_ref[...] += _dot_tn(n4, dpre16)
        dwp_ref[...] += _dot_tn(p16, dpp)
        dpn_ref[...] += dpn
        dbg_ref[...] += dbg
        dfn_ref[...] += dfn

    return _tiled(body, "tail", T // TAIL_TM,
                  [(h, TAIL_TM, D_MODEL, 0), (p, TAIL_TM, D_PLE, 0), (target, TAIL_TM, D_MODEL, 0)],
                  [ple_norm, b_gate, final_norm], [w_gate, w_proj_t],
                  [(T, D_MODEL, F32, TAIL_TM)],
                  [((1, 1), F32), ((D_MODEL, D_MODEL), F32), ((D_PLE, D_MODEL), F32), ((1, D_MODEL), F32),
                   ((1, D_MODEL), F32), ((1, D_MODEL), F32)])


def _gather_phases(x_ref, out_ref, send_sems, recv_sems, local_sem):
    mx, my, mc = lax.axis_index("x"), lax.axis_index("y"), lax.axis_index("c")
    me, sibling = (mx, my, mc), (mx, my, 1 - mc)
    chips = [(1 - mx, my), (mx, 1 - my), (1 - mx, 1 - my)]

    def rows(px, py, pc):
        return out_ref.at[4 * px + 2 * py + pc]

    def copy(k, block, to, src=None):
        return pltpu.make_async_remote_copy(
            src_ref=rows(*block) if src is None else src, dst_ref=rows(*block),
            send_sem=send_sems.at[k], recv_sem=recv_sems.at[k], device_id=to, device_id_type=MESH)

    mine = pltpu.make_async_copy(x_ref, rows(*me), local_sem)
    first = [copy(0, me, sibling, src=x_ref)] + [copy(1 + j, me, (*chip, mc), src=x_ref) for j, chip in enumerate(chips)]
    passed = [copy(4 + j, (*chip, mc), sibling) for j, chip in enumerate(chips)]

    def start():
        mine.start()
        for cp in first:
            cp.start()

    def mid():
        for j, chip in enumerate(chips):
            copy(1 + j, (*chip, mc), me).wait_recv()
            passed[j].start()

    def finish():
        copy(0, sibling, me).wait_recv()
        for j, chip in enumerate(chips):
            copy(4 + j, (*chip, 1 - mc), me).wait_recv()
        for cp in first + passed:
            cp.wait_send()
        mine.wait()

    return start, mid, finish


def _exchange_phases(x_ref, out_ref, send_sems, recv_sems, local_sem):
    mx, my, mc = lax.axis_index("x"), lax.axis_index("y"), lax.axis_index("c")
    me = 4 * mx + 2 * my + mc
    mine = pltpu.make_async_copy(x_ref.at[me], out_ref.at[me], local_sem)
    copies = []
    for k in range(1, N_DEV):
        px = 1 - mx if k & 4 else mx
        py = 1 - my if k & 2 else my
        pc = 1 - mc if k & 1 else mc
        copies.append(pltpu.make_async_remote_copy(
            src_ref=x_ref.at[4 * px + 2 * py + pc], dst_ref=out_ref.at[me], send_sem=send_sems.at[k - 1],
            recv_sem=recv_sems.at[k - 1], device_id=(px, py, pc), device_id_type=MESH))

    def start():
        mine.start()
        for cp in copies:
            cp.start()

    def finish():
        for cp in copies:
            cp.wait_recv()
        for cp in copies:
            cp.wait_send()
        mine.wait()

    return start, lambda: None, finish


def _chip_exchange_phases(x_ref, out_ref, mine, recv, sums, load_sems, pair_send, pair_recv, chip_send, chip_recv, out_sem):
    mx, my, mc = lax.axis_index("x"), lax.axis_index("y"), lax.axis_index("c")
    my_chip = 2 * mx + my
    load = [pltpu.make_async_copy(x_ref.at[2 * q + mc], mine.at[q], load_sems.at[q]) for q in range(N_CHIPS)]
    to_sibling = [pltpu.make_async_remote_copy(
        src_ref=x_ref.at[2 * q + 1 - mc], dst_ref=recv.at[q], send_sem=pair_send.at[q], recv_sem=pair_recv.at[q],
        device_id=(mx, my, 1 - mc), device_id_type=MESH) for q in range(N_CHIPS)]
    to_chips = []
    for k in range(1, N_CHIPS):
        px = 1 - mx if k & 2 else mx
        py = 1 - my if k & 1 else my
        to_chips.append(pltpu.make_async_remote_copy(
            src_ref=sums.at[2 * px + py], dst_ref=out_ref.at[my_chip], send_sem=chip_send.at[k - 1],
            recv_sem=chip_recv.at[k - 1], device_id=(px, py, mc), device_id_type=MESH))
    keep = pltpu.make_async_copy(sums.at[my_chip], out_ref.at[my_chip], out_sem)

    def start():
        for cp in load + to_sibling:
            cp.start()

    def mid():
        for cp in load:
            cp.wait()
        for cp in to_sibling:
            cp.wait_recv()
        for q in range(N_CHIPS):
            sums[q] = (mine[q].astype(F32) + recv[q].astype(F32)).astype(sums.dtype)
        for cp in to_chips + [keep]:
            cp.start()

    def finish():
        for cp in to_chips:
            cp.wait_recv()
        for cp in to_chips + to_sibling:
            cp.wait_send()
        keep.wait()

    return start, mid, finish


FLAT_SCRATCH = (pltpu.SemaphoreType.DMA((N_DEV - 1,)), pltpu.SemaphoreType.DMA((N_DEV - 1,)), pltpu.SemaphoreType.DMA)


def _gather_comm(x):
    return _Comm(_gather_phases, x, jax.ShapeDtypeStruct((N_DEV,) + x.shape, x.dtype), FLAT_SCRATCH)


def _exchange_comm(x):
    return _Comm(_exchange_phases, x, jax.ShapeDtypeStruct(x.shape, x.dtype), FLAT_SCRATCH)


def _chip_exchange_comm(x):
    stage = pltpu.VMEM((N_CHIPS,) + x.shape[1:], x.dtype)
    sems = [pltpu.SemaphoreType.DMA((n,)) for n in (N_CHIPS, N_CHIPS, N_CHIPS, N_CHIPS - 1, N_CHIPS - 1)]
    return _Comm(_chip_exchange_phases, x, jax.ShapeDtypeStruct((N_CHIPS,) + x.shape[1:], x.dtype),
                 (stage, stage, stage, *sems, pltpu.SemaphoreType.DMA))


def _comm_alone(comms, name):
    n = len(comms)

    def body(*refs):
        phases, first = [], 2 * n
        for k, comm in enumerate(comms):
            phases.append(comm.phases(refs[k], refs[n + k], *refs[first:first + len(comm.scratch)]))
            first += len(comm.scratch)
        for step in range(3):
            for phase in phases:
                phase[step]()

    any_spec = pl.BlockSpec(memory_space=pl.ANY)
    return pl.pallas_call(
        body,
        out_shape=[comm.dst for comm in comms],
        in_specs=[any_spec] * n,
        out_specs=[any_spec] * n,
        scratch_shapes=[shape for comm in comms for shape in comm.scratch],
        name=name,
        compiler_params=pltpu.CompilerParams(vmem_limit_bytes=VMEM_LIMIT),
    )(*[comm.src for comm in comms])


def _sum_parts(p_ref):
    g = p_ref[0].astype(F32)
    for j in range(1, p_ref.shape[0]):
        g = g + p_ref[j].astype(F32)
    return g


def _adamw_store(g, w_ref, m_ref, v_ref, g_ref, d_ref, nm_ref, nv_ref):
    m_new = ADAM_B1 * m_ref[...] + (1.0 - ADAM_B1) * g
    v_new = ADAM_B2 * v_ref[...] + (1.0 - ADAM_B2) * jnp.square(g)
    m_hat = m_new / (1.0 - ADAM_B1 ** ADAM_STEP)
    v_hat = v_new / (1.0 - ADAM_B2 ** ADAM_STEP)
    g_ref[...] = g
    d_ref[...] = -ADAM_LR * (m_hat / (jnp.sqrt(v_hat) + ADAM_EPS) + ADAM_WD * w_ref[...])
    nm_ref[...] = m_new
    nv_ref[...] = v_new


def _adamw_shard(parts, off, w, m, v, name, n_tiles):
    _, rows, c = w.shape
    assert c == PACK_COLS
    by_rows = rows % BF16_ROWS == 0
    if by_rows:
        tr = rows // n_tiles
        window = (parts.shape[0], tr, PACK_COLS)
        spec = pl.BlockSpec((None, tr, PACK_COLS), lambda i: (0, i, 0))
    else:
        padded, tc = -(-rows // BF16_ROWS) * BF16_ROWS, PACK_COLS // n_tiles
        window = (parts.shape[0], padded, tc)
        spec = pl.BlockSpec((None, rows, tc), lambda i: (0, 0, i))
    blocked = off % (tr if by_rows else padded) == 0

    def update(p_ref, refs):
        g = _sum_parts(p_ref)
        if not by_rows:
            keep = lax.broadcasted_iota(jnp.int32, (rows, padded), 0) == lax.broadcasted_iota(jnp.int32, (rows, padded), 1)
            g = _exact_dot(g, keep.astype(BF16), ((1,), (0,)), x_first=False)
        _adamw_store(g, *refs)

    def kern_blocked(p_ref, *refs):
        update(p_ref, refs)

    def kern_copied(p_hbm, *refs):
        buf, sem = refs[-2:]
        i = pl.program_id(0)
        if by_rows:
            src = p_hbm.at[:, pl.ds(pl.multiple_of(off + i * tr, BF16_ROWS), tr), :]
        else:
            src = p_hbm.at[:, pl.ds(off, padded), pl.ds(pl.multiple_of(i * tc, LANES), tc)]
        cp = pltpu.make_async_copy(src, buf, sem)
        cp.start()
        cp.wait()
        update(buf, refs[:-2])

    if blocked:
        index = (lambda i: (0, off // tr + i, 0)) if by_rows else (lambda i: (0, off // padded, i))
        parts_spec, scratch = pl.BlockSpec(window, index), []
    else:
        parts_spec, scratch = pl.BlockSpec(memory_space=pl.ANY), [pltpu.VMEM(window, parts.dtype), pltpu.SemaphoreType.DMA]
    return pl.pallas_call(
        kern_blocked if blocked else kern_copied,
        out_shape=[jax.ShapeDtypeStruct(w.shape, F32)] * 4,
        grid=(n_tiles,),
        in_specs=[parts_spec, spec, spec, spec],
        out_specs=[spec] * 4,
        scratch_shapes=scratch,
        name=name,
        compiler_params=pltpu.CompilerParams(dimension_semantics=("arbitrary",), vmem_limit_bytes=VMEM_LIMIT),
    )(parts, w, m, v)


def _sum_adamw(parts, w, m, v, tr, name):
    _, R, C = parts.shape

    def kern(p_ref, w_ref, m_ref, v_ref, g_ref, d_ref, nm_ref, nv_ref):
        _adamw_store(_sum_parts(p_ref), w_ref, m_ref, v_ref, g_ref, d_ref, nm_ref, nv_ref)

    row_spec = pl.BlockSpec((tr, C), lambda i: (i, 0))
    return pl.pallas_call(
        kern,
        out_shape=[jax.ShapeDtypeStruct((R, C), F32)] * 4,
        grid=(R // tr,),
        in_specs=[pl.BlockSpec((N_DEV, tr, C), lambda i: (0, i, 0)), row_spec, row_spec, row_spec],
        out_specs=[row_spec] * 4,
        name=name,
        compiler_params=pltpu.CompilerParams(dimension_semantics=("arbitrary",), vmem_limit_bytes=VMEM_LIMIT),
    )(parts, w, m, v)


FF_SHARD = D_FF // N_DEV
CONV_SHARD = (SSM_CONV, CONV_DIM // N_DEV)
SHARDS = {"ffn1_w_gate": ((D_MODEL, FF_SHARD), True), "ffn1_w_up": ((D_MODEL, FF_SHARD), True),
          "ffn1_w_down": ((FF_SHARD, D_MODEL), False),
          "ffn2_w_gate": ((D_MODEL, FF_SHARD), True), "ffn2_w_up": ((D_MODEL, FF_SHARD), True),
          "ffn2_w_down": ((FF_SHARD, D_MODEL), False),
          "w_out": ((2 * D_MODEL // N_DEV, D_MODEL), False), "ple_w_gate": ((D_MODEL // N_DEV, D_MODEL), False),
          "w_in": ((D_MODEL, IN_PROJ // N_DEV), True), "ple_w_proj": ((D_PLE, D_MODEL // N_DEV), True),
          "conv_w": (CONV_SHARD, True),
          "conv_w_mid": (CONV_SHARD, True), "conv_w_low": (CONV_SHARD, True)}
BIG = tuple(name for name in SHARDS if not name.startswith("conv_w_"))
SMALL = ("ffn1_norm", "mix_norm", "gm_ln_g", "gm_ln_b", "gm_w_s", "gm_b_s", "gm_out_norm", "conv_b", "dt_bias", "a_log",
         "d_skip", "ssm_norm", "ffn2_norm", "ple_norm", "ple_b_gate", "final_norm")
SMALL_ROWS = 144


def _piece_rows(name):
    shape = SHARDS[name][0]
    return -(-(shape[0] * shape[1]) // PACK_COLS)


def _pad_cols(flat, name):
    pad = _piece_rows(name) * PACK_COLS - flat.shape[-1]
    return flat if pad == 0 else jnp.pad(flat, [(0, 0)] * (flat.ndim - 1) + [(0, pad)])


class _Pack:
    def __init__(self, names, tile_rows):
        self.names, self.tile_rows, self.offsets, off = names, tile_rows, {}, 0
        for name in names:
            self.offsets[name] = off
            off += _piece_rows(name)
        self.rows = -(-off // tile_rows) * tile_rows

    def pack_local(self, vals):
        parts = []
        for name in self.names:
            val = vals[name]
            parts.append(_pad_cols((val.T if SHARDS[name][1] else val).reshape(-1), name))
        flat = jnp.concatenate(parts)
        return jnp.pad(flat, (0, self.rows * PACK_COLS - flat.shape[0])).reshape(self.rows, PACK_COLS)

    def pack_owner_major(self, grads):
        parts, rows = [], 0
        for name in self.names:
            grad, piece_rows = grads[name].astype(BF16), _piece_rows(name)
            if grad.shape != (N_DEV * piece_rows, PACK_COLS):
                grad = _pad_cols(grad.reshape(N_DEV, -1), name)
            parts.append(grad.reshape(N_DEV, piece_rows, PACK_COLS))
            rows += piece_rows
        if rows < self.rows:
            parts.append(jnp.zeros((N_DEV, self.rows - rows, PACK_COLS), BF16))
        return parts[0] if len(parts) == 1 else jnp.concatenate(parts, axis=1)

    def gathered_piece(self, gathered, name):
        shape = SHARDS[name][0]
        rows = gathered[:, self.offsets[name]:self.offsets[name] + _piece_rows(name), :]
        return rows.reshape(N_DEV, -1)[:, :shape[0] * shape[1]]

    def pieces(self, gathered, name):
        return _Pieces(gathered, self.offsets[name], _piece_rows(name))


GATHER_FFN1 = _Pack(("ffn1_w_gate", "ffn1_w_up", "ffn1_w_down"), BF16_ROWS)
GATHER_MIX = _Pack(("w_out", "ple_w_gate", "w_in", "ple_w_proj", "conv_w", "conv_w_mid", "conv_w_low"), BF16_ROWS)
GATHER_FFN2 = _Pack(("ffn2_w_gate", "ffn2_w_up", "ffn2_w_down"), BF16_ROWS)
SCATTER_LATE = _Pack(("ffn2_w_gate", "ffn2_w_up", "ffn2_w_down", "w_out", "ple_w_gate", "ple_w_proj"), BF16_ROWS)
SCATTER_IN = _Pack(("w_in", "conv_w"), BF16_ROWS)
SCATTER_GATE = _Pack(("ffn1_w_gate",), BF16_ROWS)
SCATTER_UP = _Pack(("ffn1_w_up",), BF16_ROWS)
SCATTER_DOWN = _Pack(("ffn1_w_down",), BF16_ROWS)


def _pack_small(vals, behind=()):
    flat = jnp.concatenate([vals[name].reshape(-1).astype(F32) for name in SMALL] + [b.reshape(-1) for b in behind])
    return jnp.pad(flat, (0, SMALL_ROWS * PACK_COLS - flat.shape[0])).reshape(SMALL_ROWS, PACK_COLS)


def _unpack_small(packed, shapes):
    out, off = {}, 0
    flat = packed.reshape(-1)
    for name in SMALL:
        n = 1
        for s in shapes[name]:
            n *= s
        out[name] = flat[off:off + n].reshape(shapes[name])
        off += n
    return out


WEIGHTS = ("ffn1_norm", "ffn1_w_gate", "ffn1_w_up", "ffn1_w_down", "mix_norm", "w_in", "gm_ln_g", "gm_ln_b", "gm_w_s",
           "gm_b_s", "gm_out_norm", "conv_w", "conv_b", "dt_bias", "a_log", "d_skip", "ssm_norm", "w_out", "ffn2_norm",
           "ffn2_w_gate", "ffn2_w_up", "ffn2_w_down", "ple_norm", "ple_w_gate", "ple_b_gate", "ple_w_proj", "final_norm")


def _step(x, p, target, w, m, v):
    local = lambda d: {name: d[name][0] for name in BIG}

    shards = {name: val.astype(BF16) for name, val in local(w).items()}
    conv_high = lax.reduce_precision(w["conv_w"][0], 8, 7)
    conv_mid = lax.reduce_precision(w["conv_w"][0] - conv_high, 8, 7)
    shards["conv_w"] = conv_high.astype(BF16)
    shards["conv_w_mid"] = conv_mid.astype(BF16)
    shards["conv_w_low"] = (w["conv_w"][0] - conv_high - conv_mid).astype(BF16)
    g_ffn1 = _comm_alone([_gather_comm(GATHER_FFN1.pack_local(shards))], "gather_ffn1")[0]

    row = lambda name: w[name].reshape(1, -1)
    gm_w_s = w["gm_w_s"][0]
    gm_b_st = jnp.transpose(w["gm_b_s"][0])
    ffn1 = (row("ffn1_norm"),) + tuple(GATHER_FFN1.pieces(g_ffn1, name) for name in GATHER_FFN1.names)
    gm = (row("gm_ln_g"), row("gm_ln_b"), gm_w_s, gm_b_st, row("gm_out_norm"))

    h1, n1, a1, b1, s1, g_mix = _ffn_fwd(x, *ffn1, "ffn1_fwd", comm=_gather_comm(GATHER_MIX.pack_local(shards)))
    w_in_t = GATHER_MIX.gathered_piece(g_mix, "w_in").reshape(IN_PROJ, D_MODEL)
    w_in_t = jnp.concatenate([w_in_t, jnp.zeros((IN_PROJ_PAD - IN_PROJ, D_MODEL), BF16)], axis=0)
    w_proj_t = GATHER_MIX.gathered_piece(g_mix, "ple_w_proj").reshape(D_MODEL, D_PLE)
    conv_w = sum(GATHER_MIX.gathered_piece(g_mix, name).astype(F32) for name in ("conv_w", "conv_w_mid", "conv_w_low"))
    conv_w = conv_w.reshape(CONV_DIM, SSM_CONV).T
    ssd = (row("dt_bias"), row("a_log"), row("d_skip"), row("ssm_norm"))
    w_out = GATHER_MIX.pieces(g_mix, "w_out")

    proj, n2, x16, xc = _mix_in_fwd(h1, row("mix_norm"), w_in_t, conv_w, row("conv_b"))
    ya = _gm_fwd(proj, *gm)
    yb, s_all, g_ffn2 = _ssd_fwd(proj, xc, *ssd, comm=_gather_comm(GATHER_FFN2.pack_local(shards)))
    ffn2 = (row("ffn2_norm"),) + tuple(GATHER_FFN2.pieces(g_ffn2, name) for name in GATHER_FFN2.names)
    h3, n3, a3, b3, s3, h2 = _ffn_fwd(h1, *ffn2, "ffn2_fwd", mixed=(ya, yb, w_out))

    g, gp = {}, {}
    dh3, loss, gp["ple_w_gate"], d_w_proj, g["ple_norm"], g["ple_b_gate"], g["final_norm"] = _tail(
        h3, p, target, row("ple_norm"), GATHER_MIX.pieces(g_mix, "ple_w_gate"), row("ple_b_gate"), w_proj_t,
        row("final_norm"))
    gp["ple_w_proj"] = d_w_proj.T

    dh2, da3, db3, g["ffn2_norm"] = _ffn_dgrad(h2, dh3, a3, b3, *ffn2, "ffn2_dgrad")
    gp["ffn2_w_gate"] = _wgrad(n3, da3, FF_BN, "ffn2_wgrad_gate", transpose_out=True)
    gp["ffn2_w_up"] = _wgrad(n3, db3, FF_BN, "ffn2_wgrad_up", transpose_out=True)
    gp["ffn2_w_down"] = _wgrad(s3, dh3, DOWN_BN, "ffn2_wgrad_down", scale=0.5, bk=DOWN_BK)

    dya, dyb = _out_proj_dgrad(dh2, w_out)
    gp["w_out"] = jnp.concatenate([_wgrad(ya, dh2, SQUARE_BN, "w_out_wgrad_a"), _wgrad(yb, dh2, SQUARE_BN, "w_out_wgrad_b")], axis=0)

    dp_zxd, d_conv_w, g["conv_b"], g["dt_bias"], g["a_log"], g["d_skip"], g["ssm_norm"], parts_late = _ssd_bwd(
        proj, x16, xc, dyb, s_all, conv_w, *ssd, comm=_exchange_comm(SCATTER_LATE.pack_owner_major(gp)))
    gp["conv_w"] = d_conv_w.T
    dp_uv, g["gm_ln_g"], g["gm_ln_b"], g["gm_w_s"], dbst, g["gm_out_norm"] = _gm_bwd(proj, dya, *gm)
    g["gm_b_s"] = jnp.transpose(dbst)

    parts = {}
    gp["w_in"] = jnp.concatenate([_wgrad(n2, dp_uv, SQUARE_BN, "w_in_wgrad_uv", transpose_out=True),
                                  _wgrad(n2, dp_zxd, ZXD_BN, "w_in_wgrad_zxd", transpose_out=True)], axis=0)[:IN_PROJ]
    dh1, g["mix_norm"], parts[SCATTER_IN] = _mix_in_dgrad(h1, dh2, dp_uv, dp_zxd, row("mix_norm"), w_in_t,
                                                          comm=_exchange_comm(SCATTER_IN.pack_owner_major(gp)))

    dx, da1, db1, g["ffn1_norm"] = _ffn_dgrad(x, dh1, a1, b1, *ffn1, "ffn1_dgrad")
    gp["ffn1_w_gate"], small_parts = _wgrad(n1, da1, FF_BN, "ffn1_wgrad_gate", transpose_out=True,
                                            comm=_gather_comm(_pack_small(g, behind=[loss])))
    gp["ffn1_w_up"], parts[SCATTER_GATE] = _wgrad(n1, db1, FF_BN, "ffn1_wgrad_up", transpose_out=True,
                                                  comm=_chip_exchange_comm(SCATTER_GATE.pack_owner_major(gp)))
    gp["ffn1_w_down"], parts[SCATTER_UP] = _wgrad(s1, dh1, DOWN_BN, "ffn1_wgrad_down", scale=0.5, bk=DOWN_BK,
                                                  comm=_chip_exchange_comm(SCATTER_UP.pack_owner_major(gp)))
    parts[SCATTER_DOWN] = _comm_alone([_chip_exchange_comm(SCATTER_DOWN.pack_owner_major(gp))], "scatter_ffn1_down")[0]
    parts[SCATTER_LATE] = parts_late

    res_big = {}
    for pack, pack_parts in parts.items():
        for name in pack.names:
            shape, transposed = SHARDS[name]
            if name in ("ple_w_proj", "conv_w"):
                nat = pack.gathered_piece(pack_parts, name).reshape((N_DEV,) + shape[::-1])
                res_big[name] = _sum_adamw(jnp.transpose(nat, (0, 2, 1)), w[name][0], m[name][0], v[name][0], shape[0],
                                           "adamw_" + name)
            else:
                flip = (lambda a: jnp.transpose(a, (0, 2, 1))) if transposed else (lambda a: a)
                res = _adamw_shard(pack_parts, pack.offsets[name], flip(w[name]), flip(m[name]), flip(v[name]),
                                   "adamw_" + name, n_tiles=4 if name == "w_in" else 2)
                res_big[name] = [flip(r) for r in res]

    small_shapes = {name: w[name].shape for name in SMALL}
    res_small = _sum_adamw(small_parts, _pack_small(w), _pack_small(m), _pack_small(v), SMALL_ROWS, "adamw_small")
    loss = res_small[0].reshape(-1)[sum(w[name].size for name in SMALL)]
    res_small = [_unpack_small(r, small_shapes) for r in res_small]

    outs = []
    for k in range(4):
        for name in WEIGHTS:
            if name in res_small[k]:
                outs.append(res_small[k][name])
            else:
                outs.append(res_big[name][k].reshape(w[name].shape))
    return loss, dx, outs


def kernel(x, p, ffn1_norm, ffn1_w_gate, ffn1_w_up, ffn1_w_down, mix_norm, w_in, gm_ln_g, gm_ln_b, gm_w_s, gm_b_s, gm_out_norm, conv_w, conv_b, dt_bias, a_log, d_skip, ssm_norm, w_out, ffn2_norm, ffn2_w_gate, ffn2_w_up, ffn2_w_down, ple_norm, ple_w_gate, ple_b_gate, ple_w_proj, final_norm, loss_target, m_ffn1_norm, m_ffn1_w_gate, m_ffn1_w_up, m_ffn1_w_down, m_mix_norm, m_w_in, m_gm_ln_g, m_gm_ln_b, m_gm_w_s, m_gm_b_s, m_gm_out_norm, m_conv_w, m_conv_b, m_dt_bias, m_a_log, m_d_skip, m_ssm_norm, m_w_out, m_ffn2_norm, m_ffn2_w_gate, m_ffn2_w_up, m_ffn2_w_down, m_ple_norm, m_ple_w_gate, m_ple_b_gate, m_ple_w_proj, m_final_norm, v_ffn1_norm, v_ffn1_w_gate, v_ffn1_w_up, v_ffn1_w_down, v_mix_norm, v_w_in, v_gm_ln_g, v_gm_ln_b, v_gm_w_s, v_gm_b_s, v_gm_out_norm, v_conv_w, v_conv_b, v_dt_bias, v_a_log, v_d_skip, v_ssm_norm, v_w_out, v_ffn2_norm, v_ffn2_w_gate, v_ffn2_w_up, v_ffn2_w_down, v_ple_norm, v_ple_w_gate, v_ple_b_gate, v_ple_w_proj, v_final_norm):
    args = locals()
    w = {name: args[name] for name in WEIGHTS}
    m = {name: args["m_" + name] for name in WEIGHTS}
    v = {name: args["v_" + name] for name in WEIGHTS}
    loss, dx, outs = _step(x[0], p[0, 0], loss_target[0], w, m, v)
    return (loss, dx[None], *outs)
```

```python
import functools
from typing import NamedTuple

import jax
import jax.numpy as jnp
from jax import lax
from jax.experimental import pallas as pl
from jax.experimental.pallas import tpu as pltpu

F32 = jnp.float32
BF16 = jnp.bfloat16
MESH = pl.DeviceIdType.MESH
N_DEV = 8
N_CHIPS = 4

D_MODEL = 1024
D_FF = 2816
D_PLE = 256
GM_WIDTH = 1024
GM_HEADS = 8
GM_HEAD_DIM = 128
CHUNK = 128
SSM_WIDTH = 1024
SSM_HEADS = 16
SSM_HEAD_DIM = 64
SSM_GROUPS = 2
SSM_STATE = 128
SSM_CONV = 4
CONV_DIM = SSM_WIDTH + 2 * SSM_GROUPS * SSM_STATE
IN_PROJ = 2 * GM_WIDTH + SSM_WIDTH + CONV_DIM + SSM_HEADS
LANES = 128
BF16_ROWS = 16
F32_ROWS = 8
IN_PROJ_PAD = IN_PROJ - SSM_HEADS + LANES
UV_W = 2 * GM_WIDTH
ZXD_W = IN_PROJ_PAD - UV_W
HALO = 8
EPS = 1e-6

ADAM_LR = 0.001
ADAM_B1 = 0.9
ADAM_B2 = 0.999
ADAM_EPS = 1e-08
ADAM_WD = 0.01
ADAM_STEP = 10

VMEM_LIMIT = 56 * 1024 * 1024
PACK_COLS = 1024


def _rms(x, g):
    return x * lax.rsqrt(jnp.mean(x * x, axis=-1, keepdims=True) + EPS) * g


def _gelu(x):
    return 0.5 * x * (1.0 + lax.erf(x * (2.0 ** -0.5)))


def _silu(x):
    return x * jax.nn.sigmoid(x)


def _dot(a, b):
    return jnp.dot(a.astype(BF16), b.astype(BF16), preferred_element_type=F32)


def _dot_nt(a, b):
    return lax.dot_general(a.astype(BF16), b.astype(BF16), (((1,), (1,)), ((), ())), preferred_element_type=F32)


def _dot_tn(a, b):
    return lax.dot_general(a.astype(BF16), b.astype(BF16), (((0,), (0,)), ((), ())), preferred_element_type=F32)


def _split3(x):
    hi = x.astype(BF16)
    rest = x - hi.astype(F32)
    mid = rest.astype(BF16)
    return hi, mid, (rest - mid.astype(F32)).astype(BF16)


def _exact_dot(x, mask, dims, x_first=True, n_terms=3):
    terms = [lax.dot_general(*((t, mask) if x_first else (mask, t)), (dims, ((), ())), preferred_element_type=F32)
             for t in _split3(x)[:n_terms]]
    total = terms[0]
    for term in terms[1:]:
        total = total + term
    return total


def _mask_product(fwd_dims, fwd_x_first, bwd_dims, bwd_x_first, bwd_terms=3):
    @jax.custom_vjp
    def product(x, mask):
        return _exact_dot(x, mask, fwd_dims, fwd_x_first)

    def fwd(x, mask):
        return product(x, mask), mask

    def bwd(mask, g):
        return _exact_dot(g, mask, bwd_dims, bwd_x_first, bwd_terms), jnp.zeros_like(mask)

    product.defvjp(fwd, bwd)
    return product


_widen = _mask_product(((1,), (0,)), True, ((1,), (1,)), True, bwd_terms=2)
_cumsum_rows = _mask_product(((1,), (0,)), False, ((0,), (0,)), False)
_cumsum_cols = _mask_product(((0,), (0,)), True, ((1,), (1,)), False)


class _Pieces(NamedTuple):
    gathered: jax.Array
    row_off: int
    rows: int


class _Comm(NamedTuple):
    phases: object
    src: jax.Array
    dst: jax.ShapeDtypeStruct
    scratch: tuple


def _tiled(body, name, n_steps, tiled_in, full_in, big_in, tiled_out, acc_out, scratch=(), reverse=False, comm=None):
    n_t, n_f, n_b, n_to, n_a = len(tiled_in), len(full_in), len(big_in), len(tiled_out), len(acc_out)
    n_c = 1 if comm else 0

    def row(i):
        return n_steps - 1 - i if reverse else i

    in_specs, args = [], []
    for arr, br, bc, cb in tiled_in:
        if callable(cb):
            in_specs.append(pl.BlockSpec((br, bc), cb))
        else:
            in_specs.append(pl.BlockSpec((br, bc), functools.partial(lambda i, cb: (row(i), cb), cb=cb)))
        args.append(arr)
    for arr in full_in:
        in_specs.append(pl.BlockSpec(arr.shape, functools.partial(lambda i, nd: (0,) * nd, nd=arr.ndim)))
        args.append(arr)
    big_shapes, n_copies = [], 0
    for big in big_in:
        in_specs.append(pl.BlockSpec(memory_space=pl.ANY))
        if isinstance(big, _Pieces):
            args.append(big.gathered)
            big_shapes.append(((N_DEV * big.rows, PACK_COLS), big.gathered.dtype))
            n_copies += N_DEV
        else:
            args.append(big)
            big_shapes.append((big.shape, big.dtype))
            n_copies += 1
    if comm:
        in_specs.append(pl.BlockSpec(memory_space=pl.ANY))
        args.append(comm.src)
    out_specs, out_shape = [], []
    for rows, cols, dt, br in tiled_out:
        out_specs.append(pl.BlockSpec((br, cols), lambda i: (row(i), 0)))
        out_shape.append(jax.ShapeDtypeStruct((rows, cols), dt))
    for shp, dt in acc_out:
        out_specs.append(pl.BlockSpec(shp, functools.partial(lambda i, nd: (0,) * nd, nd=len(shp))))
        out_shape.append(jax.ShapeDtypeStruct(shp, dt))
    if comm:
        out_specs.append(pl.BlockSpec(memory_space=pl.ANY))
        out_shape.append(comm.dst)
    scratch_shapes = [pltpu.VMEM(shp, dt) for shp, dt in big_shapes] + list(scratch)
    if n_copies:
        scratch_shapes.append(pltpu.SemaphoreType.DMA((n_copies,)))
    if comm:
        scratch_shapes += list(comm.scratch)

    def kern(*refs):
        n_in = n_t + n_f + n_b + n_c
        ins = refs[: n_t + n_f]
        big_hbm = refs[n_t + n_f : n_t + n_f + n_b]
        outs = refs[n_in : n_in + n_to + n_a]
        rest = refs[n_in + n_to + n_a + n_c :]
        big_vmem, scr = rest[:n_b], rest[n_b:]
        if comm:
            scr, comm_scr = scr[:-len(comm.scratch)], scr[-len(comm.scratch):]
            comm_start, comm_mid, comm_finish = comm.phases(refs[n_in - 1], refs[n_in + n_to + n_a], *comm_scr)
        if n_copies:
            scr, copy_sems = scr[:-1], scr[-1]
        step = pl.program_id(0)

        @pl.when(step == 0)
        def _():
            copies = []
            for big, src, dst in zip(big_in, big_hbm, big_vmem):
                if isinstance(big, _Pieces):
                    for j in range(N_DEV):
                        copies.append((src.at[j, pl.ds(big.row_off, big.rows), :], dst.at[pl.ds(j * big.rows, big.rows), :]))
                else:
                    copies.append((src, dst))
            copies = [pltpu.make_async_copy(a, b, copy_sems.at[k]) for k, (a, b) in enumerate(copies)]
            for cp in copies:
                cp.start()
            for cp in copies:
                cp.wait()
            for acc in outs[n_to:]:
                acc[...] = jnp.zeros(acc.shape, acc.dtype)
            if comm:
                comm_start()

        body(row(step), *ins, *big_vmem, *outs, *scr)
        if comm:
            pl.when(step == (n_steps - 1) // 2)(comm_mid)
            pl.when(step == n_steps - 1)(comm_finish)

    res = pl.pallas_call(
        kern,
        out_shape=out_shape,
        grid=(n_steps,),
        in_specs=in_specs,
        out_specs=out_specs,
        scratch_shapes=scratch_shapes,
        name=name,
        compiler_params=pltpu.CompilerParams(dimension_semantics=("arbitrary",), vmem_limit_bytes=VMEM_LIMIT),
    )(*args)
    return res


FWD_CHUNKS = ((0, 1536), (1536, D_FF))
DGRAD_CHUNKS = ((0, 1024), (1024, 2048), (2048, D_FF))
FFN_TM = 256


def _ffn_fwd(h, g, wg_t, wu_t, wd, name, comm=None, mixed=None):
    T = h.shape[0]

    def ffn(x, g_ref, wg_ref, wu_ref, wd_ref, o_ref, n_ref, a_ref, b_ref, s_ref):
        n = _rms(x, g_ref[...]).astype(BF16)
        n_ref[...] = n
        f = jnp.zeros(x.shape, F32)
        for lo, hi in FWD_CHUNKS:
            a = _dot_nt(n, wg_ref[lo:hi, :])
            b = _dot_nt(n, wu_ref[lo:hi, :])
            s = (_silu(a) * b).astype(BF16)
            a_ref[:, lo:hi] = a.astype(BF16)
            b_ref[:, lo:hi] = b.astype(BF16)
            s_ref[:, lo:hi] = s
            f = f + jnp.dot(s, wd_ref[lo:hi, :], preferred_element_type=F32)
        o_ref[...] = x + 0.5 * f

    def body_plain(i, h_ref, *refs):
        ffn(h_ref[...], *refs)

    def body_mixed(i, h_ref, ya_ref, yb_ref, g_ref, wg_ref, wu_ref, wd_ref, wo_ref, o_ref, n_ref, a_ref, b_ref, s_ref, x_ref):
        x = (h_ref[...] + jnp.dot(ya_ref[...], wo_ref[:GM_WIDTH, :], preferred_element_type=F32)
             + jnp.dot(yb_ref[...], wo_ref[GM_WIDTH:, :], preferred_element_type=F32))
        x_ref[...] = x
        ffn(x, g_ref, wg_ref, wu_ref, wd_ref, o_ref, n_ref, a_ref, b_ref, s_ref)

    body = body_mixed if mixed else body_plain
    tiled_in, big_in = [(h, FFN_TM, D_MODEL, 0)], [wg_t, wu_t, wd]
    tiled_out = [(T, D_MODEL, F32, FFN_TM), (T, D_MODEL, BF16, FFN_TM), (T, D_FF, BF16, FFN_TM), (T, D_FF, BF16, FFN_TM),
                 (T, D_FF, BF16, FFN_TM)]
    if mixed:
        tiled_in += [(mixed[0], FFN_TM, GM_WIDTH, 0), (mixed[1], FFN_TM, SSM_WIDTH, 0)]
        big_in.append(mixed[2])
        tiled_out.append((T, D_MODEL, F32, FFN_TM))
    return _tiled(body, name, T // FFN_TM, tiled_in, [g], big_in, tiled_out, [], comm=comm)


def _ffn_dgrad(h, dout, a16, b16, g, wg_t, wu_t, wd, name):
    T = h.shape[0]

    def body(i, h_ref, do_ref, a_ref, b_ref, g_ref, wg_ref, wu_ref, wd_ref, dh_ref, da_ref, db_ref, dg_ref):
        dout = do_ref[...]
        _, rms_vjp = jax.vjp(_rms, h_ref[...], g_ref[...])
        dfo = (0.5 * dout).astype(BF16)
        dn = jnp.zeros(dout.shape, F32)
        for lo, hi in DGRAD_CHUNKS:
            a = a_ref[:, lo:hi].astype(F32)
            b = b_ref[:, lo:hi].astype(F32)
            sg = jax.nn.sigmoid(a)
            ds = _dot_nt(dfo, wd_ref[lo:hi, :])
            db = (ds * (a * sg)).astype(BF16)
            da = (ds * b * (sg * (1.0 + a * (1.0 - sg)))).astype(BF16)
            dn = dn + _dot(da, wg_ref[lo:hi, :]) + _dot(db, wu_ref[lo:hi, :])
            da_ref[:, lo:hi] = da
            db_ref[:, lo:hi] = db
        dx, dg = rms_vjp(dn)
        dh_ref[...] = dout + dx
        dg_ref[...] += dg

    return _tiled(body, name, T // FFN_TM,
                  [(h, FFN_TM, D_MODEL, 0), (dout, FFN_TM, D_MODEL, 0), (a16, FFN_TM, D_FF, 0), (b16, FFN_TM, D_FF, 0)],
                  [g], [wg_t, wu_t, wd],
                  [(T, D_MODEL, F32, FFN_TM), (T, D_FF, BF16, FFN_TM), (T, D_FF, BF16, FFN_TM)], [((1, D_MODEL), F32)])


FF_BN = D_FF // 2
DOWN_BN, DOWN_BK = 512, 1024
SQUARE_BN = 1024
ZXD_BN = ZXD_W // 3


def _wgrad(a, b, bn, name, scale=None, transpose_out=False, bk=2048, comm=None):
    T, M = a.shape
    N = b.shape[1]
    bk = min(bk, T)
    assert M % LANES == 0 and N % bn == 0 and T % bk == 0
    n_j, n_k = N // bn, T // bk
    n_c = 1 if comm else 0

    def kern(*refs):
        a_ref, b_ref, o_ref, acc_ref = refs[0], refs[1], refs[2 + n_c], refs[3 + 2 * n_c]
        j, k = pl.program_id(0), pl.program_id(1)
        if comm:
            comm_start, comm_mid, comm_finish = comm.phases(refs[2], refs[4], *refs[6:])
            pl.when((j == 0) & (k == 0))(comm_start)

        @pl.when(k == 0)
        def _():
            acc_ref[...] = jnp.zeros(acc_ref.shape, F32)

        bv = b_ref[...]
        if scale is not None:
            bv = bv * scale
        acc_ref[...] += _dot_tn(a_ref[...], bv)

        @pl.when(k == n_k - 1)
        def _():
            acc = acc_ref[...]
            o_ref[...] = (acc.T if transpose_out else acc).astype(BF16)

        if comm:
            pl.when((j == (n_j - 1) // 2) & (k == n_k - 1))(comm_mid)
            pl.when((j == n_j - 1) & (k == n_k - 1))(comm_finish)

    if transpose_out:
        out_shape, out_spec = (N, M), pl.BlockSpec((bn, M), lambda j, k: (j, 0))
    else:
        out_shape, out_spec = (M, N), pl.BlockSpec((M, bn), lambda j, k: (0, j))
    any_spec = pl.BlockSpec(memory_space=pl.ANY)
    res = pl.pallas_call(
        kern,
        out_shape=[jax.ShapeDtypeStruct(out_shape, BF16)] + ([comm.dst] if comm else []),
        grid=(n_j, n_k),
        in_specs=[pl.BlockSpec((bk, M), lambda j, k: (k, 0)), pl.BlockSpec((bk, bn), lambda j, k: (k, j))] + [any_spec] * n_c,
        out_specs=[out_spec] + [any_spec] * n_c,
        scratch_shapes=[pltpu.VMEM((M, bn), F32)] + (list(comm.scratch) if comm else []),
        name=name,
        compiler_params=pltpu.CompilerParams(dimension_semantics=("arbitrary", "arbitrary"), vmem_limit_bytes=VMEM_LIMIT),
    )(a, b, *([comm.src] if comm else []))
    return res if comm else res[0]


PROJ_TM = 512
PROJ_DGRAD_TM = 256
UVZ_W = 2 * GM_WIDTH + SSM_WIDTH
PROJ_KEPT = UVZ_W + LANES
Z_BLK = 2 * GM_WIDTH // SSM_WIDTH
DT_BLK = UVZ_W // LANES


def _mix_in_fwd(h, g, w_in_t, conv_w, conv_b):
    T = h.shape[0]

    def body(i, h_ref, g_ref, cw_ref, cb_ref, w_ref, p_ref, n_ref, x_ref, xc_ref, ext_ref):
        @pl.when(i == 0)
        def _():
            ext_ref[0:HALO, :] = jnp.zeros((HALO, CONV_DIM), F32)

        n = _rms(h_ref[...], g_ref[...]).astype(BF16)
        n_ref[...] = n
        proj = _dot_nt(n, w_ref[...])
        p_ref[:, :UVZ_W] = proj[:, :UVZ_W]
        p_ref[:, UVZ_W:] = jnp.concatenate(
            [proj[:, UVZ_W + CONV_DIM:], jnp.zeros((PROJ_TM, LANES - SSM_HEADS), F32)], axis=1)
        xbc = proj[:, UVZ_W:UVZ_W + CONV_DIM]
        x_ref[...] = xbc.astype(BF16)
        ext_ref[HALO:, :] = xbc
        xc_ref[...] = _conv_taps(ext_ref, cw_ref[...], cb_ref[...], PROJ_TM)
        ext_ref[0:HALO, :] = ext_ref[PROJ_TM:PROJ_TM + HALO, :]

    return _tiled(body, "mix_in_fwd", T // PROJ_TM, [(h, PROJ_TM, D_MODEL, 0)], [g, conv_w, conv_b], [w_in_t],
                  [(T, PROJ_KEPT, F32, PROJ_TM), (T, D_MODEL, BF16, PROJ_TM), (T, CONV_DIM, BF16, PROJ_TM),
                   (T, CONV_DIM, F32, PROJ_TM)], [],
                  scratch=[pltpu.VMEM((HALO + PROJ_TM, CONV_DIM), F32)])


def _mix_in_dgrad(h, dh_in, dp_uv, dp_zxd, g, w_in_t, comm=None):
    T = h.shape[0]

    def body(i, h_ref, dh_ref, duv_ref, dzxd_ref, g_ref, w_ref, o_ref, dg_ref):
        dzxd, zx_w = dzxd_ref[...], ZXD_W - LANES
        dn = (_dot(duv_ref[...], w_ref[:UV_W, :]) + _dot(dzxd[:, :zx_w], w_ref[UV_W:UV_W + zx_w, :])
              + _dot(dzxd[:, zx_w:zx_w + SSM_HEADS], w_ref[UV_W + zx_w:, :]))
        _, rms_vjp = jax.vjp(_rms, h_ref[...], g_ref[...])
        dx, dg = rms_vjp(dn)
        o_ref[...] = dh_ref[...] + dx
        dg_ref[...] += dg

    return _tiled(body, "mix_in_dgrad", T // PROJ_DGRAD_TM,
                  [(h, PROJ_DGRAD_TM, D_MODEL, 0), (dh_in, PROJ_DGRAD_TM, D_MODEL, 0), (dp_uv, PROJ_DGRAD_TM, UV_W, 0),
                   (dp_zxd, PROJ_DGRAD_TM, ZXD_W, 0)], [g], [w_in_t],
                  [(T, D_MODEL, F32, PROJ_DGRAD_TM)], [((1, D_MODEL), F32)], comm=comm)


def _out_proj_dgrad(dh, w_out):
    T = dh.shape[0]

    def body(i, dh_ref, w_ref, dya_ref, dyb_ref):
        d = dh_ref[...].astype(BF16)
        dya_ref[...] = _dot_nt(d, w_ref[:GM_WIDTH, :])
        dyb_ref[...] = _dot_nt(d, w_ref[GM_WIDTH:, :])

    rows = min(T, 2 * PROJ_TM)
    return _tiled(body, "out_proj_dgrad", T // rows, [(dh, rows, D_MODEL, 0)], [], [w_out],
                  [(T, GM_WIDTH, F32, rows), (T, SSM_WIDTH, F32, rows)], [])


def _gm_chunk(u, v, ln_g, ln_b, b_st, out_g, *w_heads):
    ug = _gelu(u)
    vg = _gelu(v)
    mu = jnp.mean(vg, axis=-1, keepdims=True)
    xc = vg - mu
    vn = xc * lax.rsqrt(jnp.mean(xc * xc, axis=-1, keepdims=True) + EPS) * ln_g + ln_b
    t_idx = lax.broadcasted_iota(jnp.int32, (CHUNK, CHUNK), 0)
    s_idx = lax.broadcasted_iota(jnp.int32, (CHUNK, CHUNK), 1)
    causal = t_idx >= s_idx
    mixed = []
    for hd in range(GM_HEADS):
        wm = jnp.where(causal, w_heads[hd], 0.0)
        cols = slice(hd * GM_HEAD_DIM, (hd + 1) * GM_HEAD_DIM)
        mixed.append(_dot(wm, vn[:, cols]) + b_st[:, hd:hd + 1])
    ya0 = ug * jnp.concatenate(mixed, axis=1)
    return _rms(ya0, out_g)


GM_FWD_CHUNKS = 4


def _gm_fwd(proj, ln_g, ln_b, w_s, b_st, out_g):
    T = proj.shape[0]

    rows = GM_FWD_CHUNKS * CHUNK

    def body(i, u_ref, v_ref, lg_ref, lb_ref, w_ref, bs_ref, og_ref, ya_ref):
        w_heads = [w_ref[hd] for hd in range(GM_HEADS)]
        for c in range(GM_FWD_CHUNKS):
            tok = pl.ds(c * CHUNK, CHUNK)
            ya = _gm_chunk(u_ref[tok, :], v_ref[tok, :], lg_ref[...], lb_ref[...], bs_ref[...], og_ref[...], *w_heads)
            ya_ref[tok, :] = ya.astype(BF16)

    return _tiled(body, "gmlp_fwd", T // rows, [(proj, rows, GM_WIDTH, 0), (proj, rows, GM_WIDTH, 1)],
                  [ln_g, ln_b, w_s, b_st, out_g], [], [(T, GM_WIDTH, BF16, rows)], [])[0]


def _gm_bwd(proj, dya, ln_g, ln_b, w_s, b_st, out_g):
    T = proj.shape[0]

    def body(i, u_ref, v_ref, dy_ref, lg_ref, lb_ref, w_ref, bs_ref, og_ref, duv_ref, dlg_ref, dlb_ref, dw_ref, dbs_ref,
             dog_ref):
        w_heads = [w_ref[hd] for hd in range(GM_HEADS)]
        _, vjp = jax.vjp(_gm_chunk, u_ref[...], v_ref[...], lg_ref[...], lb_ref[...], bs_ref[...], og_ref[...], *w_heads)
        grads = vjp(dy_ref[...])
        duv_ref[:, :GM_WIDTH] = grads[0].astype(BF16)
        duv_ref[:, GM_WIDTH:] = grads[1].astype(BF16)
        dlg_ref[...] += grads[2]
        dlb_ref[...] += grads[3]
        dbs_ref[...] += grads[4]
        dog_ref[...] += grads[5]
        for hd in range(GM_HEADS):
            dw_ref[hd] += grads[6 + hd]

    return _tiled(body, "gmlp_bwd", T // CHUNK,
                  [(proj, CHUNK, GM_WIDTH, 0), (proj, CHUNK, GM_WIDTH, 1), (dya, CHUNK, GM_WIDTH, 0)],
                  [ln_g, ln_b, w_s, b_st, out_g], [], [(T, UV_W, BF16, CHUNK)],
                  [((1, GM_WIDTH), F32), ((1, GM_WIDTH), F32), ((GM_HEADS, CHUNK, CHUNK), F32),
                   ((CHUNK, GM_HEADS), F32), ((1, GM_WIDTH), F32)])


def _ssd_chunk(xc, z, dtr, s_in, dt_bias, a_log, d_skip, norm_g):
    half = SSM_WIDTH // SSM_GROUPS
    l_idx = lax.broadcasted_iota(jnp.int32, (CHUNK, CHUNK), 0)
    s_idx = lax.broadcasted_iota(jnp.int32, (CHUNK, CHUNK), 1)
    causal = l_idx >= s_idx
    head_of_col = lax.broadcasted_iota(jnp.int32, (SSM_HEADS, SSM_WIDTH), 1) // SSM_HEAD_DIM
    expand = (head_of_col == lax.broadcasted_iota(jnp.int32, (SSM_HEADS, SSM_WIDTH), 0)).astype(BF16)

    xcs = _silu(xc)
    xs = xcs[:, :SSM_WIDTH]
    dt = jax.nn.softplus(dtr + dt_bias)
    adt = dt * (-jnp.exp(a_log))
    acs = _cumsum_rows(adt, causal.astype(BF16))
    acs_t = _cumsum_cols(adt, (l_idx <= s_idx).astype(BF16))
    tot = acs[CHUNK - 1:CHUNK, :]
    dt_w = _widen(dt, expand)
    out_decay_w = _widen(jnp.exp(acs), expand)
    state_decay_w = _widen(jnp.exp(tot - acs), expand)
    chunk_decay_w = _widen(jnp.exp(tot), expand)
    d_skip_w = _widen(d_skip, expand)
    xdt = xs * dt_w
    xdt_decayed = xdt * state_decay_w

    y_diag, y_off, states = [], [], []
    for grp in range(SSM_GROUPS):
        b0 = SSM_WIDTH + grp * SSM_STATE
        c0 = SSM_WIDTH + SSM_GROUPS * SSM_STATE + grp * SSM_STATE
        bm = xcs[:, b0:b0 + SSM_STATE].astype(BF16)
        cm = xcs[:, c0:c0 + SSM_STATE].astype(BF16)
        cb = _dot_nt(cm, bm)
        for k in range(grp * SSM_HEADS // SSM_GROUPS, (grp + 1) * SSM_HEADS // SSM_GROUPS):
            decay = jnp.exp(jnp.where(causal, acs[:, k:k + 1] - acs_t[k:k + 1, :], -jnp.inf))
            y_diag.append(_dot(cb * decay, xdt[:, k * SSM_HEAD_DIM:(k + 1) * SSM_HEAD_DIM]))
        cols = slice(grp * half, (grp + 1) * half)
        states.append(_dot_tn(bm, xdt_decayed[:, cols]))
        y_off.append(_dot(cm, s_in[:, cols]))
    y = jnp.concatenate(y_diag, axis=1) + jnp.concatenate(y_off, axis=1) * out_decay_w + xs * d_skip_w
    s_out = s_in * chunk_decay_w + jnp.concatenate(states, axis=1)
    y = y * _silu(z)
    normed = []
    for grp in range(SSM_GROUPS):
        yg = y[:, grp * half:(grp + 1) * half]
        normed.append(yg * lax.rsqrt(jnp.mean(yg * yg, axis=-1, keepdims=True) + EPS))
    return jnp.concatenate(normed, axis=1) * norm_g, s_out


def _sum_row_tiles(x):
    return x.reshape(x.shape[0] // F32_ROWS, F32_ROWS, x.shape[1]).sum(axis=0)


def _conv_taps(ext_ref, w, b, rows):
    y = b
    for k in range(SSM_CONV):
        y = y + w[k:k + 1, :] * ext_ref[pl.ds(HALO - (SSM_CONV - 1) + k, rows), :]
    return y


SSD_FWD_CHUNKS = 4


def _ssd_fwd(proj, xc, dt_bias, a_log, d_skip, norm_g, comm=None):
    T = proj.shape[0]
    n_chunks = T // CHUNK
    rows = SSD_FWD_CHUNKS * CHUNK

    def body(i, z_ref, xc_ref, dt_ref, dtb_ref, al_ref, dsk_ref, ng_ref, yb_ref, sin_ref, st_ref):
        @pl.when(i == 0)
        def _():
            st_ref[...] = jnp.zeros(st_ref.shape, F32)

        for c in range(SSD_FWD_CHUNKS):
            tok = pl.ds(c * CHUNK, CHUNK)
            s_in = st_ref[...]
            yb, s_out = _ssd_chunk(xc_ref[tok, :], z_ref[tok, :], dt_ref[tok, 0:SSM_HEADS], s_in, dtb_ref[...], al_ref[...],
                                   dsk_ref[...], ng_ref[...])
            yb_ref[tok, :] = yb.astype(BF16)
            sin_ref[pl.ds(c * SSM_STATE, SSM_STATE), :] = s_in
            st_ref[...] = s_out

    return _tiled(body, "ssd_fwd", T // rows,
                  [(proj, rows, SSM_WIDTH, Z_BLK), (xc, rows, CONV_DIM, 0), (proj, rows, LANES, DT_BLK)],
                  [dt_bias, a_log, d_skip, norm_g], [],
                  [(T, SSM_WIDTH, BF16, rows), (n_chunks * SSM_STATE, SSM_WIDTH, F32, SSD_FWD_CHUNKS * SSM_STATE)], [],
                  scratch=[pltpu.VMEM((SSM_STATE, SSM_WIDTH), F32)], comm=comm)


def _ssd_bwd(proj, x16, xc, dyb, s_all, conv_w, dt_bias, a_log, d_skip, norm_g, comm=None):
    T = proj.shape[0]
    n_chunks = T // CHUNK

    def body(i, z_ref, x_ref, xc_ref, dt_ref, dy_ref, sin_ref, cw_ref, dtb_ref, al_ref, dsk_ref, ng_ref,
             dzxd_ref, dcw_ref, dcb_ref, ddtb_ref, dal_ref, ddsk_ref, dng_ref, dext_ref, dst_ref, cw_acc, cb_acc):
        @pl.when(i == n_chunks - 1)
        def _():
            dext_ref[CHUNK:, :] = jnp.zeros((HALO, CONV_DIM), F32)
            dst_ref[...] = jnp.zeros(dst_ref.shape, F32)
            cw_acc[...] = jnp.zeros(cw_acc.shape, F32)
            cb_acc[...] = jnp.zeros(cb_acc.shape, F32)

        _, vjp = jax.vjp(_ssd_chunk, xc_ref[...], z_ref[...], dt_ref[:, 0:SSM_HEADS], sin_ref[...], dtb_ref[...], al_ref[...],
                         dsk_ref[...], ng_ref[...])
        dxc, dz, ddtr, ds_in, ddtb, dal, ddsk, dng = vjp((dy_ref[...], dst_ref[...]))
        dst_ref[...] = ds_in
        ddtb_ref[...] += ddtb
        dal_ref[...] += dal
        ddsk_ref[...] += ddsk
        dng_ref[...] += dng
        dext_ref[0:CHUNK, :] = dxc
        cw = cw_ref[...]
        x = x_ref[...].astype(F32)
        dx = jnp.zeros((CHUNK, CONV_DIM), F32)
        for k in range(SSM_CONV):
            shifted = dext_ref[pl.ds(SSM_CONV - 1 - k, CHUNK), :]
            dx = dx + cw[k:k + 1, :] * shifted
            cw_acc[k] += _sum_row_tiles(shifted * x)
        cb_acc[...] += _sum_row_tiles(dxc)

        @pl.when(i == 0)
        def _():
            dcw_ref[...] = jnp.sum(cw_acc[...], axis=1)
            dcb_ref[...] = jnp.sum(cb_acc[...], axis=0, keepdims=True)

        dext_ref[CHUNK:, :] = dext_ref[0:HALO, :]
        dzxd_ref[:, 0:SSM_WIDTH] = dz.astype(BF16)
        dzxd_ref[:, SSM_WIDTH:SSM_WIDTH + CONV_DIM] = dx.astype(BF16)
        dzxd_ref[:, SSM_WIDTH + CONV_DIM:] = jnp.concatenate(
            [ddtr, jnp.zeros((CHUNK, LANES - SSM_HEADS), F32)], axis=1).astype(BF16)

    return _tiled(body, "ssd_bwd", n_chunks,
                  [(proj, CHUNK, SSM_WIDTH, Z_BLK), (x16, CHUNK, CONV_DIM, 0), (xc, CHUNK, CONV_DIM, 0),
                   (proj, CHUNK, LANES, DT_BLK), (dyb, CHUNK, SSM_WIDTH, 0), (s_all, SSM_STATE, SSM_WIDTH, 0)],
                  [conv_w, dt_bias, a_log, d_skip, norm_g], [],
                  [(T, ZXD_W, BF16, CHUNK)],
                  [((SSM_CONV, CONV_DIM), F32), ((1, CONV_DIM), F32), ((1, SSM_HEADS), F32), ((1, SSM_HEADS), F32),
                   ((1, SSM_HEADS), F32), ((1, SSM_WIDTH), F32)],
                  scratch=[pltpu.VMEM((CHUNK + HALO, CONV_DIM), F32), pltpu.VMEM((SSM_STATE, SSM_WIDTH), F32),
                           pltpu.VMEM((SSM_CONV, F32_ROWS, CONV_DIM), F32), pltpu.VMEM((F32_ROWS, CONV_DIM), F32)],
                  reverse=True, comm=comm)


TAIL_TM = 512


def _tail(h, p, target, ple_norm, w_gate, b_gate, w_proj_t, final_norm):
    T = h.shape[0]

    def head(x, pre, pp, b_g, f_norm, tgt):
        gate = jax.nn.sigmoid(pre + b_g)
        out = _rms(x + gate * pp, f_norm)
        err = out - tgt
        return 0.5 * jnp.sum(jnp.mean(err * err, axis=-1, keepdims=True), axis=0, keepdims=True)

    def body(i, h_ref, p_ref, t_ref, pn_ref, bg_ref, fn_ref, wg_ref, wp_ref, dh_ref, loss_ref, dwg_ref, dwp_ref, dpn_ref,
             dbg_ref, dfn_ref):
        x = h_ref[...]
        n4f, n_vjp = jax.vjp(_rms, x, pn_ref[...])
        n4 = n4f.astype(BF16)
        pre = jnp.dot(n4, wg_ref[...], preferred_element_type=F32)
        p16 = p_ref[...].astype(BF16)
        pp = _dot_nt(p16, wp_ref[...])
        loss, h_vjp = jax.vjp(functools.partial(head, tgt=t_ref[...]), x, pre, pp, bg_ref[...], fn_ref[...])
        dx, dpre, dpp, dbg, dfn = h_vjp(jnp.ones((1, 1), F32))
        dpre16 = dpre.astype(BF16)
        dn4 = _dot_nt(dpre16, wg_ref[...])
        dx2, dpn = n_vjp(dn4)
        dh_ref[...] = dx + dx2
        loss_ref[...] += loss
        dwg_ref[...] += _dot_tn(n4, dpre16)
        dwp_ref[...] += _dot_tn(p16, dpp)
        dpn_ref[...] += dpn
        dbg_ref[...] += dbg
        dfn_ref[...] += dfn

    return _tiled(body, "tail", T // TAIL_TM,
                  [(h, TAIL_TM, D_MODEL, 0), (p, TAIL_TM, D_PLE, 0), (target, TAIL_TM, D_MODEL, 0)],
                  [ple_norm, b_gate, final_norm], [w_gate, w_proj_t],
                  [(T, D_MODEL, F32, TAIL_TM)],
                  [((1, 1), F32), ((D_MODEL, D_MODEL), F32), ((D_PLE, D_MODEL), F32), ((1, D_MODEL), F32),
                   ((1, D_MODEL), F32), ((1, D_MODEL), F32)])


def _gather_phases(x_ref, out_ref, send_sems, recv_sems, local_sem):
    mx, my, mc = lax.axis_index("x"), lax.axis_index("y"), lax.axis_index("c")
    me, sibling = (mx, my, mc), (mx, my, 1 - mc)
    chips = [(1 - mx, my), (mx, 1 - my), (1 - mx, 1 - my)]

    def rows(px, py, pc):
        return out_ref.at[4 * px + 2 * py + pc]

    def copy(k, block, to, src=None):
        return pltpu.make_async_remote_copy(
            src_ref=rows(*block) if src is None else src, dst_ref=rows(*block),
            send_sem=send_sems.at[k], recv_sem=recv_sems.at[k], device_id=to, device_id_type=MESH)

    mine = pltpu.make_async_copy(x_ref, rows(*me), local_sem)
    first = [copy(0, me, sibling, src=x_ref)] + [copy(1 + j, me, (*chip, mc), src=x_ref) for j, chip in enumerate(chips)]
    passed = [copy(4 + j, (*chip, mc), sibling) for j, chip in enumerate(chips)]

    def start():
        mine.start()
        for cp in first:
            cp.start()

    def mid():
        for j, chip in enumerate(chips):
            copy(1 + j, (*chip, mc), me).wait_recv()
            passed[j].start()

    def finish():
        copy(0, sibling, me).wait_recv()
        for j, chip in enumerate(chips):
            copy(4 + j, (*chip, 1 - mc), me).wait_recv()
        for cp in first + passed:
            cp.wait_send()
        mine.wait()

    return start, mid, finish


def _exchange_phases(x_ref, out_ref, send_sems, recv_sems, local_sem):
    mx, my, mc = lax.axis_index("x"), lax.axis_index("y"), lax.axis_index("c")
    me = 4 * mx + 2 * my + mc
    mine = pltpu.make_async_copy(x_ref.at[me], out_ref.at[me], local_sem)
    copies = []
    for k in range(1, N_DEV):
        px = 1 - mx if k & 4 else mx
        py = 1 - my if k & 2 else my
        pc = 1 - mc if k & 1 else mc
        copies.append(pltpu.make_async_remote_copy(
            src_ref=x_ref.at[4 * px + 2 * py + pc], dst_ref=out_ref.at[me], send_sem=send_sems.at[k - 1],
            recv_sem=recv_sems.at[k - 1], device_id=(px, py, pc), device_id_type=MESH))

    def start():
        mine.start()
        for cp in copies:
            cp.start()

    def finish():
        for cp in copies:
            cp.wait_recv()
        for cp in copies:
            cp.wait_send()
        mine.wait()

    return start, lambda: None, finish


def _chip_exchange_phases(x_ref, out_ref, mine, recv, sums, load_sems, pair_send, pair_recv, chip_send, chip_recv, out_sem):
    mx, my, mc = lax.axis_index("x"), lax.axis_index("y"), lax.axis_index("c")
    my_chip = 2 * mx + my
    load = [pltpu.make_async_copy(x_ref.at[2 * q + mc], mine.at[q], load_sems.at[q]) for q in range(N_CHIPS)]
    to_sibling = [pltpu.make_async_remote_copy(
        src_ref=x_ref.at[2 * q + 1 - mc], dst_ref=recv.at[q], send_sem=pair_send.at[q], recv_sem=pair_recv.at[q],
        device_id=(mx, my, 1 - mc), device_id_type=MESH) for q in range(N_CHIPS)]
    to_chips = []
    for k in range(1, N_CHIPS):
        px = 1 - mx if k & 2 else mx
        py = 1 - my if k & 1 else my
        to_chips.append(pltpu.make_async_remote_copy(
            src_ref=sums.at[2 * px + py], dst_ref=out_ref.at[my_chip], send_sem=chip_send.at[k - 1],
            recv_sem=chip_recv.at[k - 1], device_id=(px, py, mc), device_id_type=MESH))
    keep = pltpu.make_async_copy(sums.at[my_chip], out_ref.at[my_chip], out_sem)

    def start():
        for cp in load + to_sibling:
            cp.start()

    def mid():
        for cp in load:
            cp.wait()
        for cp in to_sibling:
            cp.wait_recv()
        for q in range(N_CHIPS):
            sums[q] = (mine[q].astype(F32) + recv[q].astype(F32)).astype(sums.dtype)
        for cp in to_chips + [keep]:
            cp.start()

    def finish():
        for cp in to_chips:
            cp.wait_recv()
        for cp in to_chips + to_sibling:
            cp.wait_send()
        keep.wait()

    return start, mid, finish


FLAT_SCRATCH = (pltpu.SemaphoreType.DMA((N_DEV - 1,)), pltpu.SemaphoreType.DMA((N_DEV - 1,)), pltpu.SemaphoreType.DMA)


def _gather_comm(x):
    return _Comm(_gather_phases, x, jax.ShapeDtypeStruct((N_DEV,) + x.shape, x.dtype), FLAT_SCRATCH)


def _exchange_comm(x):
    return _Comm(_exchange_phases, x, jax.ShapeDtypeStruct(x.shape, x.dtype), FLAT_SCRATCH)


def _chip_exchange_comm(x):
    stage = pltpu.VMEM((N_CHIPS,) + x.shape[1:], x.dtype)
    sems = [pltpu.SemaphoreType.DMA((n,)) for n in (N_CHIPS, N_CHIPS, N_CHIPS, N_CHIPS - 1, N_CHIPS - 1)]
    return _Comm(_chip_exchange_phases, x, jax.ShapeDtypeStruct((N_CHIPS,) + x.shape[1:], x.dtype),
                 (stage, stage, stage, *sems, pltpu.SemaphoreType.DMA))


def _comm_alone(comms, name):
    n = len(comms)

    def body(*refs):
        phases, first = [], 2 * n
        for k, comm in enumerate(comms):
            phases.append(comm.phases(refs[k], refs[n + k], *refs[first:first + len(comm.scratch)]))
            first += len(comm.scratch)
        for step in range(3):
            for phase in phases:
                phase[step]()

    any_spec = pl.BlockSpec(memory_space=pl.ANY)
    return pl.pallas_call(
        body,
        out_shape=[comm.dst for comm in comms],
        in_specs=[any_spec] * n,
        out_specs=[any_spec] * n,
        scratch_shapes=[shape for comm in comms for shape in comm.scratch],
        name=name,
        compiler_params=pltpu.CompilerParams(vmem_limit_bytes=VMEM_LIMIT),
    )(*[comm.src for comm in comms])


def _sum_parts(p_ref):
    g = p_ref[0].astype(F32)
    for j in range(1, p_ref.shape[0]):
        g = g + p_ref[j].astype(F32)
    return g


def _adamw_store(g, w_ref, m_ref, v_ref, g_ref, d_ref, nm_ref, nv_ref):
    m_new = ADAM_B1 * m_ref[...] + (1.0 - ADAM_B1) * g
    v_new = ADAM_B2 * v_ref[...] + (1.0 - ADAM_B2) * jnp.square(g)
    m_hat = m_new / (1.0 - ADAM_B1 ** ADAM_STEP)
    v_hat = v_new / (1.0 - ADAM_B2 ** ADAM_STEP)
    g_ref[...] = g
    d_ref[...] = -ADAM_LR * (m_hat / (jnp.sqrt(v_hat) + ADAM_EPS) + ADAM_WD * w_ref[...])
    nm_ref[...] = m_new
    nv_ref[...] = v_new


def _adamw_shard(parts, off, w, m, v, name, n_tiles):
    _, rows, c = w.shape
    assert c == PACK_COLS
    by_rows = rows % BF16_ROWS == 0
    if by_rows:
        tr = rows // n_tiles
        window = (parts.shape[0], tr, PACK_COLS)
        spec = pl.BlockSpec((None, tr, PACK_COLS), lambda i: (0, i, 0))
    else:
        padded, tc = -(-rows // BF16_ROWS) * BF16_ROWS, PACK_COLS // n_tiles
        window = (parts.shape[0], padded, tc)
        spec = pl.BlockSpec((None, rows, tc), lambda i: (0, 0, i))
    blocked = off % (tr if by_rows else padded) == 0

    def update(p_ref, refs):
        g = _sum_parts(p_ref)
        if not by_rows:
            keep = lax.broadcasted_iota(jnp.int32, (rows, padded), 0) == lax.broadcasted_iota(jnp.int32, (rows, padded), 1)
            g = _exact_dot(g, keep.astype(BF16), ((1,), (0,)), x_first=False)
        _adamw_store(g, *refs)

    def kern_blocked(p_ref, *refs):
        update(p_ref, refs)

    def kern_copied(p_hbm, *refs):
        buf, sem = refs[-2:]
        i = pl.program_id(0)
        if by_rows:
            src = p_hbm.at[:, pl.ds(pl.multiple_of(off + i * tr, BF16_ROWS), tr), :]
        else:
            src = p_hbm.at[:, pl.ds(off, padded), pl.ds(pl.multiple_of(i * tc, LANES), tc)]
        cp = pltpu.make_async_copy(src, buf, sem)
        cp.start()
        cp.wait()
        update(buf, refs[:-2])

    if blocked:
        index = (lambda i: (0, off // tr + i, 0)) if by_rows else (lambda i: (0, off // padded, i))
        parts_spec, scratch = pl.BlockSpec(window, index), []
    else:
        parts_spec, scratch = pl.BlockSpec(memory_space=pl.ANY), [pltpu.VMEM(window, parts.dtype), pltpu.SemaphoreType.DMA]
    return pl.pallas_call(
        kern_blocked if blocked else kern_copied,
        out_shape=[jax.ShapeDtypeStruct(w.shape, F32)] * 4,
        grid=(n_tiles,),
        in_specs=[parts_spec, spec, spec, spec],
        out_specs=[spec] * 4,
        scratch_shapes=scratch,
        name=name,
        compiler_params=pltpu.CompilerParams(dimension_semantics=("arbitrary",), vmem_limit_bytes=VMEM_LIMIT),
    )(parts, w, m, v)


def _sum_adamw(parts, w, m, v, tr, name):
    _, R, C = parts.shape

    def kern(p_ref, w_ref, m_ref, v_ref, g_ref, d_ref, nm_ref, nv_ref):
        _adamw_store(_sum_parts(p_ref), w_ref, m_ref, v_ref, g_ref, d_ref, nm_ref, nv_ref)

    row_spec = pl.BlockSpec((tr, C), lambda i: (i, 0))
    return pl.pallas_call(
        kern,
        out_shape=[jax.ShapeDtypeStruct((R, C), F32)] * 4,
        grid=(R // tr,),
        in_specs=[pl.BlockSpec((N_DEV, tr, C), lambda i: (0, i, 0)), row_spec, row_spec, row_spec],
        out_specs=[row_spec] * 4,
        name=name,
        compiler_params=pltpu.CompilerParams(dimension_semantics=("arbitrary",), vmem_limit_bytes=VMEM_LIMIT),
    )(parts, w, m, v)


FF_SHARD = D_FF // N_DEV
CONV_SHARD = (SSM_CONV, CONV_DIM // N_DEV)
SHARDS = {"ffn1_w_gate": ((D_MODEL, FF_SHARD), True), "ffn1_w_up": ((D_MODEL, FF_SHARD), True),
          "ffn1_w_down": ((FF_SHARD, D_MODEL), False),
          "ffn2_w_gate": ((D_MODEL, FF_SHARD), True), "ffn2_w_up": ((D_MODEL, FF_SHARD), True),
          "ffn2_w_down": ((FF_SHARD, D_MODEL), False),
          "w_out": ((2 * D_MODEL // N_DEV, D_MODEL), False), "ple_w_gate": ((D_MODEL // N_DEV, D_MODEL), False),
          "w_in": ((D_MODEL, IN_PROJ // N_DEV), True), "ple_w_proj": ((D_PLE, D_MODEL // N_DEV), True),
          "conv_w": (CONV_SHARD, True),
          "conv_w_mid": (CONV_SHARD, True), "conv_w_low": (CONV_SHARD, True)}
BIG = tuple(name for name in SHARDS if not name.startswith("conv_w_"))
SMALL = ("ffn1_norm", "mix_norm", "gm_ln_g", "gm_ln_b", "gm_w_s", "gm_b_s", "gm_out_norm", "conv_b", "dt_bias", "a_log",
         "d_skip", "ssm_norm", "ffn2_norm", "ple_norm", "ple_b_gate", "final_norm")
SMALL_ROWS = 144


def _piece_rows(name):
    shape = SHARDS[name][0]
    return -(-(shape[0] * shape[1]) // PACK_COLS)


def _pad_cols(flat, name):
    pad = _piece_rows(name) * PACK_COLS - flat.shape[-1]
    return flat if pad == 0 else jnp.pad(flat, [(0, 0)] * (flat.ndim - 1) + [(0, pad)])


class _Pack:
    def __init__(self, names, tile_rows):
        self.names, self.tile_rows, self.offsets, off = names, tile_rows, {}, 0
        for name in names:
            self.offsets[name] = off
            off += _piece_rows(name)
        self.rows = -(-off // tile_rows) * tile_rows

    def pack_local(self, vals):
        parts = []
        for name in self.names:
            val = vals[name]
            parts.append(_pad_cols((val.T if SHARDS[name][1] else val).reshape(-1), name))
        flat = jnp.concatenate(parts)
        return jnp.pad(flat, (0, self.rows * PACK_COLS - flat.shape[0])).reshape(self.rows, PACK_COLS)

    def pack_owner_major(self, grads):
        parts, rows = [], 0
        for name in self.names:
            grad, piece_rows = grads[name].astype(BF16), _piece_rows(name)
            if grad.shape != (N_DEV * piece_rows, PACK_COLS):
                grad = _pad_cols(grad.reshape(N_DEV, -1), name)
            parts.append(grad.reshape(N_DEV, piece_rows, PACK_COLS))
            rows += piece_rows
        if rows < self.rows:
            parts.append(jnp.zeros((N_DEV, self.rows - rows, PACK_COLS), BF16))
        return parts[0] if len(parts) == 1 else jnp.concatenate(parts, axis=1)

    def gathered_piece(self, gathered, name):
        shape = SHARDS[name][0]
        rows = gathered[:, self.offsets[name]:self.offsets[name] + _piece_rows(name), :]
        return rows.reshape(N_DEV, -1)[:, :shape[0] * shape[1]]

    def pieces(self, gathered, name):
        return _Pieces(gathered, self.offsets[name], _piece_rows(name))


GATHER_FFN1 = _Pack(("ffn1_w_gate", "ffn1_w_up", "ffn1_w_down"), BF16_ROWS)
GATHER_MIX = _Pack(("w_out", "ple_w_gate", "w_in", "ple_w_proj", "conv_w", "conv_w_mid", "conv_w_low"), BF16_ROWS)
GATHER_FFN2 = _Pack(("ffn2_w_gate", "ffn2_w_up", "ffn2_w_down"), BF16_ROWS)
SCATTER_LATE = _Pack(("ffn2_w_gate", "ffn2_w_up", "ffn2_w_down", "w_out", "ple_w_gate", "ple_w_proj"), BF16_ROWS)
SCATTER_IN = _Pack(("w_in", "conv_w"), BF16_ROWS)
SCATTER_GATE = _Pack(("ffn1_w_gate",), BF16_ROWS)
SCATTER_UP = _Pack(("ffn1_w_up",), BF16_ROWS)
SCATTER_DOWN = _Pack(("ffn1_w_down",), BF16_ROWS)


def _pack_small(vals, behind=()):
    flat = jnp.concatenate([vals[name].reshape(-1).astype(F32) for name in SMALL] + [b.reshape(-1) for b in behind])
    return jnp.pad(flat, (0, SMALL_ROWS * PACK_COLS - flat.shape[0])).reshape(SMALL_ROWS, PACK_COLS)


def _unpack_small(packed, shapes):
    out, off = {}, 0
    flat = packed.reshape(-1)
    for name in SMALL:
        n = 1
        for s in shapes[name]:
            n *= s
        out[name] = flat[off:off + n].reshape(shapes[name])
        off += n
    return out


WEIGHTS = ("ffn1_norm", "ffn1_w_gate", "ffn1_w_up", "ffn1_w_down", "mix_norm", "w_in", "gm_ln_g", "gm_ln_b", "gm_w_s",
           "gm_b_s", "gm_out_norm", "conv_w", "conv_b", "dt_bias", "a_log", "d_skip", "ssm_norm", "w_out", "ffn2_norm",
           "ffn2_w_gate", "ffn2_w_up", "ffn2_w_down", "ple_norm", "ple_w_gate", "ple_b_gate", "ple_w_proj", "final_norm")


def _step(x, p, target, w, m, v):
    local = lambda d: {name: d[name][0] for name in BIG}

    shards = {name: val.astype(BF16) for name, val in local(w).items()}
    conv_high = lax.reduce_precision(w["conv_w"][0], 8, 7)
    conv_mid = lax.reduce_precision(w["conv_w"][0] - conv_high, 8, 7)
    shards["conv_w"] = conv_high.astype(BF16)
    shards["conv_w_mid"] = conv_mid.astype(BF16)
    shards["conv_w_low"] = (w["conv_w"][0] - conv_high - conv_mid).astype(BF16)
    g_ffn1 = _comm_alone([_gather_comm(GATHER_FFN1.pack_local(shards))], "gather_ffn1")[0]

    row = lambda name: w[name].reshape(1, -1)
    gm_w_s = w["gm_w_s"][0]
    gm_b_st = jnp.transpose(w["gm_b_s"][0])
    ffn1 = (row("ffn1_norm"),) + tuple(GATHER_FFN1.pieces(g_ffn1, name) for name in GATHER_FFN1.names)
    gm = (row("gm_ln_g"), row("gm_ln_b"), gm_w_s, gm_b_st, row("gm_out_norm"))

    h1, n1, a1, b1, s1, g_mix = _ffn_fwd(x, *ffn1, "ffn1_fwd", comm=_gather_comm(GATHER_MIX.pack_local(shards)))
    w_in_t = GATHER_MIX.gathered_piece(g_mix, "w_in").reshape(IN_PROJ, D_MODEL)
    w_proj_t = GATHER_MIX.gathered_piece(g_mix, "ple_w_proj").reshape(D_MODEL, D_PLE)
    conv_w = sum(GATHER_MIX.gathered_piece(g_mix, name).astype(F32) for name in ("conv_w", "conv_w_mid", "conv_w_low"))
    conv_w = conv_w.reshape(CONV_DIM, SSM_CONV).T
    ssd = (row("dt_bias"), row("a_log"), row("d_skip"), row("ssm_norm"))
    w_out = GATHER_MIX.pieces(g_mix, "w_out")

    proj, n2, x16, xc = _mix_in_fwd(h1, row("mix_norm"), w_in_t, conv_w, row("conv_b"))
    ya = _gm_fwd(proj, *gm)
    yb, s_all, g_ffn2 = _ssd_fwd(proj, xc, *ssd, comm=_gather_comm(GATHER_FFN2.pack_local(shards)))
    ffn2 = (row("ffn2_norm"),) + tuple(GATHER_FFN2.pieces(g_ffn2, name) for name in GATHER_FFN2.names)
    h3, n3, a3, b3, s3, h2 = _ffn_fwd(h1, *ffn2, "ffn2_fwd", mixed=(ya, yb, w_out))

    g, gp = {}, {}
    dh3, loss, gp["ple_w_gate"], d_w_proj, g["ple_norm"], g["ple_b_gate"], g["final_norm"] = _tail(
        h3, p, target, row("ple_norm"), GATHER_MIX.pieces(g_mix, "ple_w_gate"), row("ple_b_gate"), w_proj_t,
        row("final_norm"))
    gp["ple_w_proj"] = d_w_proj.T

    dh2, da3, db3, g["ffn2_norm"] = _ffn_dgrad(h2, dh3, a3, b3, *ffn2, "ffn2_dgrad")
    gp["ffn2_w_gate"] = _wgrad(n3, da3, FF_BN, "ffn2_wgrad_gate", transpose_out=True)
    gp["ffn2_w_up"] = _wgrad(n3, db3, FF_BN, "ffn2_wgrad_up", transpose_out=True)
    gp["ffn2_w_down"] = _wgrad(s3, dh3, DOWN_BN, "ffn2_wgrad_down", scale=0.5, bk=DOWN_BK)

    dya, dyb = _out_proj_dgrad(dh2, w_out)
    gp["w_out"] = jnp.concatenate([_wgrad(ya, dh2, SQUARE_BN, "w_out_wgrad_a"), _wgrad(yb, dh2, SQUARE_BN, "w_out_wgrad_b")], axis=0)

    dp_zxd, d_conv_w, g["conv_b"], g["dt_bias"], g["a_log"], g["d_skip"], g["ssm_norm"], parts_late = _ssd_bwd(
        proj, x16, xc, dyb, s_all, conv_w, *ssd, comm=_exchange_comm(SCATTER_LATE.pack_owner_major(gp)))
    gp["conv_w"] = d_conv_w.T
    dp_uv, g["gm_ln_g"], g["gm_ln_b"], g["gm_w_s"], dbst, g["gm_out_norm"] = _gm_bwd(proj, dya, *gm)
    g["gm_b_s"] = jnp.transpose(dbst)

    parts = {}
    gp["w_in"] = jnp.concatenate([_wgrad(n2, dp_uv, SQUARE_BN, "w_in_wgrad_uv", transpose_out=True),
                                  _wgrad(n2, dp_zxd, ZXD_BN, "w_in_wgrad_zxd", transpose_out=True)], axis=0)[:IN_PROJ]
    dh1, g["mix_norm"], parts[SCATTER_IN] = _mix_in_dgrad(h1, dh2, dp_uv, dp_zxd, row("mix_norm"), w_in_t,
                                                          comm=_exchange_comm(SCATTER_IN.pack_owner_major(gp)))

    dx, da1, db1, g["ffn1_norm"] = _ffn_dgrad(x, dh1, a1, b1, *ffn1, "ffn1_dgrad")
    gp["ffn1_w_gate"], small_parts = _wgrad(n1, da1, FF_BN, "ffn1_wgrad_gate", transpose_out=True,
                                            comm=_gather_comm(_pack_small(g, behind=[loss])))
    gp["ffn1_w_up"], parts[SCATTER_GATE] = _wgrad(n1, db1, FF_BN, "ffn1_wgrad_up", transpose_out=True,
                                                  comm=_chip_exchange_comm(SCATTER_GATE.pack_owner_major(gp)))
    gp["ffn1_w_down"], parts[SCATTER_UP] = _wgrad(s1, dh1, DOWN_BN, "ffn1_wgrad_down", scale=0.5, bk=DOWN_BK,
                                                  comm=_chip_exchange_comm(SCATTER_UP.pack_owner_major(gp)))
    parts[SCATTER_DOWN] = _comm_alone([_chip_exchange_comm(SCATTER_DOWN.pack_owner_major(gp))], "scatter_ffn1_down")[0]
    parts[SCATTER_LATE] = parts_late

    res_big = {}
    for pack, pack_parts in parts.items():
        for name in pack.names:
            shape, transposed = SHARDS[name]
            if name in ("ple_w_proj", "conv_w"):
                nat = pack.gathered_piece(pack_parts, name).reshape((N_DEV,) + shape[::-1])
                res_big[name] = _sum_adamw(jnp.transpose(nat, (0, 2, 1)), w[name][0], m[name][0], v[name][0], shape[0],
                                           "adamw_" + name)
            else:
                flip = (lambda a: jnp.transpose(a, (0, 2, 1))) if transposed else (lambda a: a)
                res = _adamw_shard(pack_parts, pack.offsets[name], flip(w[name]), flip(m[name]), flip(v[name]),
                                   "adamw_" + name, n_tiles=4 if name == "w_in" else 2)
                res_big[name] = [flip(r) for r in res]

    small_shapes = {name: w[name].shape for name in SMALL}
    res_small = _sum_adamw(small_parts, _pack_small(w), _pack_small(m), _pack_small(v), SMALL_ROWS, "adamw_small")
    loss = res_small[0].reshape(-1)[sum(w[name].size for name in SMALL)]
    res_small = [_unpack_small(r, small_shapes) for r in res_small]

    outs = []
    for k in range(4):
        for name in WEIGHTS:
            if name in res_small[k]:
                outs.append(res_small[k][name])
            else:
                outs.append(res_big[name][k].reshape(w[name].shape))
    return loss, dx, outs


def kernel(x, p, ffn1_norm, ffn1_w_gate, ffn1_w_up, ffn1_w_down, mix_norm, w_in, gm_ln_g, gm_ln_b, gm_w_s, gm_b_s, gm_out_norm, conv_w, conv_b, dt_bias, a_log, d_skip, ssm_norm, w_out, ffn2_norm, ffn2_w_gate, ffn2_w_up, ffn2_w_down, ple_norm, ple_w_gate, ple_b_gate, ple_w_proj, final_norm, loss_target, m_ffn1_norm, m_ffn1_w_gate, m_ffn1_w_up, m_ffn1_w_down, m_mix_norm, m_w_in, m_gm_ln_g, m_gm_ln_b, m_gm_w_s, m_gm_b_s, m_gm_out_norm, m_conv_w, m_conv_b, m_dt_bias, m_a_log, m_d_skip, m_ssm_norm, m_w_out, m_ffn2_norm, m_ffn2_w_gate, m_ffn2_w_up, m_ffn2_w_down, m_ple_norm, m_ple_w_gate, m_ple_b_gate, m_ple_w_proj, m_final_norm, v_ffn1_norm, v_ffn1_w_gate, v_ffn1_w_up, v_ffn1_w_down, v_mix_norm, v_w_in, v_gm_ln_g, v_gm_ln_b, v_gm_w_s, v_gm_b_s, v_gm_out_norm, v_conv_w, v_conv_b, v_dt_bias, v_a_log, v_d_skip, v_ssm_norm, v_w_out, v_ffn2_norm, v_ffn2_w_gate, v_ffn2_w_up, v_ffn2_w_down, v_ple_norm, v_ple_w_gate, v_ple_b_gate, v_ple_w_proj, v_final_norm):
    args = locals()
    w = {name: args[name] for name in WEIGHTS}
    m = {name: args["m_" + name] for name in WEIGHTS}
    v = {name: args["v_" + name] for name in WEIGHTS}
    loss, dx, outs = _step(x[0], p[0, 0], loss_target[0], w, m, v)
    return (loss, dx[None], *outs)
```

```python
import functools
from typing import NamedTuple

import jax
import jax.numpy as jnp
from jax import lax
from jax.experimental import pallas as pl
from jax.experimental.pallas import tpu as pltpu

F32 = jnp.float32
BF16 = jnp.bfloat16
MESH = pl.DeviceIdType.MESH
N_DEV = 8
N_CHIPS = 4

D_MODEL = 1024
D_FF = 2816
D_PLE = 256
GM_WIDTH = 1024
GM_HEADS = 8
GM_HEAD_DIM = 128
CHUNK = 128
SSM_WIDTH = 1024
SSM_HEADS = 16
SSM_HEAD_DIM = 64
SSM_GROUPS = 2
SSM_STATE = 128
SSM_CONV = 4
CONV_DIM = SSM_WIDTH + 2 * SSM_GROUPS * SSM_STATE
IN_PROJ = 2 * GM_WIDTH + SSM_WIDTH + CONV_DIM + SSM_HEADS
LANES = 128
BF16_ROWS = 16
F32_ROWS = 8
IN_PROJ_PAD = IN_PROJ - SSM_HEADS + LANES
UV_W = 2 * GM_WIDTH
ZXD_W = IN_PROJ_PAD - UV_W
HALO = 8
EPS = 1e-6

ADAM_LR = 0.001
ADAM_B1 = 0.9
ADAM_B2 = 0.999
ADAM_EPS = 1e-08
ADAM_WD = 0.01
ADAM_STEP = 10

VMEM_LIMIT = 56 * 1024 * 1024
PACK_COLS = 1024


def _rms(x, g):
    return x * lax.rsqrt(jnp.mean(x * x, axis=-1, keepdims=True) + EPS) * g


def _gelu(x):
    return 0.5 * x * (1.0 + lax.erf(x * (2.0 ** -0.5)))


def _silu(x):
    return x * jax.nn.sigmoid(x)


def _dot(a, b):
    return jnp.dot(a.astype(BF16), b.astype(BF16), preferred_element_type=F32)


def _dot_nt(a, b):
    return lax.dot_general(a.astype(BF16), b.astype(BF16), (((1,), (1,)), ((), ())), preferred_element_type=F32)


def _dot_tn(a, b):
    return lax.dot_general(a.astype(BF16), b.astype(BF16), (((0,), (0,)), ((), ())), preferred_element_type=F32)


def _split3(x):
    hi = x.astype(BF16)
    rest = x - hi.astype(F32)
    mid = rest.astype(BF16)
    return hi, mid, (rest - mid.astype(F32)).astype(BF16)


def _exact_dot(x, mask, dims, x_first=True, n_terms=3):
    terms = [lax.dot_general(*((t, mask) if x_first else (mask, t)), (dims, ((), ())), preferred_element_type=F32)
             for t in _split3(x)[:n_terms]]
    total = terms[0]
    for term in terms[1:]:
        total = total + term
    return total


def _mask_product(fwd_dims, fwd_x_first, bwd_dims, bwd_x_first, bwd_terms=3):
    @jax.custom_vjp
    def product(x, mask):
        return _exact_dot(x, mask, fwd_dims, fwd_x_first)

    def fwd(x, mask):
        return product(x, mask), mask

    def bwd(mask, g):
        return _exact_dot(g, mask, bwd_dims, bwd_x_first, bwd_terms), jnp.zeros_like(mask)

    product.defvjp(fwd, bwd)
    return product


_widen = _mask_product(((1,), (0,)), True, ((1,), (1,)), True, bwd_terms=2)
_cumsum_rows = _mask_product(((1,), (0,)), False, ((0,), (0,)), False)
_cumsum_cols = _mask_product(((0,), (0,)), True, ((1,), (1,)), False)


class _Pieces(NamedTuple):
    gathered: jax.Array
    row_off: int
    rows: int


class _Comm(NamedTuple):
    phases: object
    src: jax.Array
    dst: jax.ShapeDtypeStruct
    scratch: tuple


def _tiled(body, name, n_steps, tiled_in, full_in, big_in, tiled_out, acc_out, scratch=(), reverse=False, comm=None):
    n_t, n_f, n_b, n_to, n_a = len(tiled_in), len(full_in), len(big_in), len(tiled_out), len(acc_out)
    n_c = 1 if comm else 0

    def row(i):
        return n_steps - 1 - i if reverse else i

    in_specs, args = [], []
    for arr, br, bc, cb in tiled_in:
        if callable(cb):
            in_specs.append(pl.BlockSpec((br, bc), cb))
        else:
            in_specs.append(pl.BlockSpec((br, bc), functools.partial(lambda i, cb: (row(i), cb), cb=cb)))
        args.append(arr)
    for arr in full_in:
        in_specs.append(pl.BlockSpec(arr.shape, functools.partial(lambda i, nd: (0,) * nd, nd=arr.ndim)))
        args.append(arr)
    big_shapes, n_copies = [], 0
    for big in big_in:
        in_specs.append(pl.BlockSpec(memory_space=pl.ANY))
        if isinstance(big, _Pieces):
            args.append(big.gathered)
            big_shapes.append(((N_DEV * big.rows, PACK_COLS), big.gathered.dtype))
            n_copies += N_DEV
        else:
            args.append(big)
            big_shapes.append((big.shape, big.dtype))
            n_copies += 1
    if comm:
        in_specs.append(pl.BlockSpec(memory_space=pl.ANY))
        args.append(comm.src)
    out_specs, out_shape = [], []
    for rows, cols, dt, br in tiled_out:
        out_specs.append(pl.BlockSpec((br, cols), lambda i: (row(i), 0)))
        out_shape.append(jax.ShapeDtypeStruct((rows, cols), dt))
    for shp, dt in acc_out:
        out_specs.append(pl.BlockSpec(shp, functools.partial(lambda i, nd: (0,) * nd, nd=len(shp))))
        out_shape.append(jax.ShapeDtypeStruct(shp, dt))
    if comm:
        out_specs.append(pl.BlockSpec(memory_space=pl.ANY))
        out_shape.append(comm.dst)
    scratch_shapes = [pltpu.VMEM(shp, dt) for shp, dt in big_shapes] + list(scratch)
    if n_copies:
        scratch_shapes.append(pltpu.SemaphoreType.DMA((n_copies,)))
    if comm:
        scratch_shapes += list(comm.scratch)

    def kern(*refs):
        n_in = n_t + n_f + n_b + n_c
        ins = refs[: n_t + n_f]
        big_hbm = refs[n_t + n_f : n_t + n_f + n_b]
        outs = refs[n_in : n_in + n_to + n_a]
        rest = refs[n_in + n_to + n_a + n_c :]
        big_vmem, scr = rest[:n_b], rest[n_b:]
        if comm:
            scr, comm_scr = scr[:-len(comm.scratch)], scr[-len(comm.scratch):]
            comm_start, comm_mid, comm_finish = comm.phases(refs[n_in - 1], refs[n_in + n_to + n_a], *comm_scr)
        if n_copies:
            scr, copy_sems = scr[:-1], scr[-1]
        step = pl.program_id(0)

        @pl.when(step == 0)
        def _():
            copies = []
            for big, src, dst in zip(big_in, big_hbm, big_vmem):
                if isinstance(big, _Pieces):
                    for j in range(N_DEV):
                        copies.append((src.at[j, pl.ds(big.row_off, big.rows), :], dst.at[pl.ds(j * big.rows, big.rows), :]))
                else:
                    copies.append((src, dst))
            copies = [pltpu.make_async_copy(a, b, copy_sems.at[k]) for k, (a, b) in enumerate(copies)]
            for cp in copies:
                cp.start()
            for cp in copies:
                cp.wait()
            for acc in outs[n_to:]:
                acc[...] = jnp.zeros(acc.shape, acc.dtype)
            if comm:
                comm_start()

        body(row(step), *ins, *big_vmem, *outs, *scr)
        if comm:
            pl.when(step == (n_steps - 1) // 2)(comm_mid)
            pl.when(step == n_steps - 1)(comm_finish)

    res = pl.pallas_call(
        kern,
        out_shape=out_shape,
        grid=(n_steps,),
        in_specs=in_specs,
        out_specs=out_specs,
        scratch_shapes=scratch_shapes,
        name=name,
        compiler_params=pltpu.CompilerParams(dimension_semantics=("arbitrary",), vmem_limit_bytes=VMEM_LIMIT),
    )(*args)
    return res


FWD_CHUNKS = ((0, 1536), (1536, D_FF))
DGRAD_CHUNKS = ((0, 1024), (1024, 2048), (2048, D_FF))
FFN_TM = 256


def _ffn_fwd(h, g, wg_t, wu_t, wd, name, comm=None, mixed=None):
    T = h.shape[0]

    def ffn(x, g_ref, wg_ref, wu_ref, wd_ref, o_ref, n_ref, a_ref, b_ref, s_ref):
        n = _rms(x, g_ref[...]).astype(BF16)
        n_ref[...] = n
        f = jnp.zeros(x.shape, F32)
        for lo, hi in FWD_CHUNKS:
            a = _dot_nt(n, wg_ref[lo:hi, :])
            b = _dot_nt(n, wu_ref[lo:hi, :])
            s = (_silu(a) * b).astype(BF16)
            a_ref[:, lo:hi] = a.astype(BF16)
            b_ref[:, lo:hi] = b.astype(BF16)
            s_ref[:, lo:hi] = s
            f = f + jnp.dot(s, wd_ref[lo:hi, :], preferred_element_type=F32)
        o_ref[...] = x + 0.5 * f

    def body_plain(i, h_ref, *refs):
        ffn(h_ref[...], *refs)

    def body_mixed(i, h_ref, ya_ref, yb_ref, g_ref, wg_ref, wu_ref, wd_ref, wo_ref, o_ref, n_ref, a_ref, b_ref, s_ref, x_ref):
        x = (h_ref[...] + jnp.dot(ya_ref[...], wo_ref[:GM_WIDTH, :], preferred_element_type=F32)
             + jnp.dot(yb_ref[...], wo_ref[GM_WIDTH:, :], preferred_element_type=F32))
        x_ref[...] = x
        ffn(x, g_ref, wg_ref, wu_ref, wd_ref, o_ref, n_ref, a_ref, b_ref, s_ref)

    body = body_mixed if mixed else body_plain
    tiled_in, big_in = [(h, FFN_TM, D_MODEL, 0)], [wg_t, wu_t, wd]
    tiled_out = [(T, D_MODEL, F32, FFN_TM), (T, D_MODEL, BF16, FFN_TM), (T, D_FF, BF16, FFN_TM), (T, D_FF, BF16, FFN_TM),
                 (T, D_FF, BF16, FFN_TM)]
    if mixed:
        tiled_in += [(mixed[0], FFN_TM, GM_WIDTH, 0), (mixed[1], FFN_TM, SSM_WIDTH, 0)]
        big_in.append(mixed[2])
        tiled_out.append((T, D_MODEL, F32, FFN_TM))
    return _tiled(body, name, T // FFN_TM, tiled_in, [g], big_in, tiled_out, [], comm=comm)


def _ffn_dgrad(h, dout, a16, b16, g, wg_t, wu_t, wd, name):
    T = h.shape[0]

    def body(i, h_ref, do_ref, a_ref, b_ref, g_ref, wg_ref, wu_ref, wd_ref, dh_ref, da_ref, db_ref, dg_ref):
        dout = do_ref[...]
        _, rms_vjp = jax.vjp(_rms, h_ref[...], g_ref[...])
        dfo = (0.5 * dout).astype(BF16)
        dn = jnp.zeros(dout.shape, F32)
        for lo, hi in DGRAD_CHUNKS:
            a = a_ref[:, lo:hi].astype(F32)
            b = b_ref[:, lo:hi].astype(F32)
            sg = jax.nn.sigmoid(a)
            ds = _dot_nt(dfo, wd_ref[lo:hi, :])
            db = (ds * (a * sg)).astype(BF16)
            da = (ds * b * (sg * (1.0 + a * (1.0 - sg)))).astype(BF16)
            dn = dn + _dot(da, wg_ref[lo:hi, :]) + _dot(db, wu_ref[lo:hi, :])
            da_ref[:, lo:hi] = da
            db_ref[:, lo:hi] = db
        dx, dg = rms_vjp(dn)
        dh_ref[...] = dout + dx
        dg_ref[...] += dg

    return _tiled(body, name, T // FFN_TM,
                  [(h, FFN_TM, D_MODEL, 0), (dout, FFN_TM, D_MODEL, 0), (a16, FFN_TM, D_FF, 0), (b16, FFN_TM, D_FF, 0)],
                  [g], [wg_t, wu_t, wd],
                  [(T, D_MODEL, F32, FFN_TM), (T, D_FF, BF16, FFN_TM), (T, D_FF, BF16, FFN_TM)], [((1, D_MODEL), F32)])


FF_BN = D_FF // 2
DOWN_BN, DOWN_BK = 512, 1024
SQUARE_BN = 1024
ZXD_BN = ZXD_W // 3


def _wgrad(a, b, bn, name, scale=None, transpose_out=False, bk=2048, comm=None):
    T, M = a.shape
    N = b.shape[1]
    bk = min(bk, T)
    assert M % LANES == 0 and N % bn == 0 and T % bk == 0
    n_j, n_k = N // bn, T // bk
    n_c = 1 if comm else 0

    def kern(*refs):
        a_ref, b_ref, o_ref, acc_ref = refs[0], refs[1], refs[2 + n_c], refs[3 + 2 * n_c]
        j, k = pl.program_id(0), pl.program_id(1)
        if comm:
            comm_start, comm_mid, comm_finish = comm.phases(refs[2], refs[4], *refs[6:])
            pl.when((j == 0) & (k == 0))(comm_start)

        @pl.when(k == 0)
        def _():
            acc_ref[...] = jnp.zeros(acc_ref.shape, F32)

        bv = b_ref[...]
        if scale is not None:
            bv = bv * scale
        acc_ref[...] += _dot_tn(a_ref[...], bv)

        @pl.when(k == n_k - 1)
        def _():
            acc = acc_ref[...]
            o_ref[...] = (acc.T if transpose_out else acc).astype(BF16)

        if comm:
            pl.when((j == (n_j - 1) // 2) & (k == n_k - 1))(comm_mid)
            pl.when((j == n_j - 1) & (k == n_k - 1))(comm_finish)

    if transpose_out:
        out_shape, out_spec = (N, M), pl.BlockSpec((bn, M), lambda j, k: (j, 0))
    else:
        out_shape, out_spec = (M, N), pl.BlockSpec((M, bn), lambda j, k: (0, j))
    any_spec = pl.BlockSpec(memory_space=pl.ANY)
    res = pl.pallas_call(
        kern,
        out_shape=[jax.ShapeDtypeStruct(out_shape, BF16)] + ([comm.dst] if comm else []),
        grid=(n_j, n_k),
        in_specs=[pl.BlockSpec((bk, M), lambda j, k: (k, 0)), pl.BlockSpec((bk, bn), lambda j, k: (k, j))] + [any_spec] * n_c,
        out_specs=[out_spec] + [any_spec] * n_c,
        scratch_shapes=[pltpu.VMEM((M, bn), F32)] + (list(comm.scratch) if comm else []),
        name=name,
        compiler_params=pltpu.CompilerParams(dimension_semantics=("arbitrary", "arbitrary"), vmem_limit_bytes=VMEM_LIMIT),
    )(a, b, *([comm.src] if comm else []))
    return res if comm else res[0]


PROJ_TM = 512
PROJ_DGRAD_TM = 256
UVZ_W = 2 * GM_WIDTH + SSM_WIDTH
PROJ_KEPT = UVZ_W + LANES
Z_BLK = 2 * GM_WIDTH // SSM_WIDTH
DT_BLK = UVZ_W // LANES


def _mix_in_fwd(h, g, w_in_t, conv_w, conv_b):
    T = h.shape[0]

    def body(i, h_ref, g_ref, cw_ref, cb_ref, w_ref, p_ref, n_ref, x_ref, xc_ref, ext_ref):
        @pl.when(i == 0)
        def _():
            ext_ref[0:HALO, :] = jnp.zeros((HALO, CONV_DIM), F32)

        n = _rms(h_ref[...], g_ref[...]).astype(BF16)
        n_ref[...] = n
        proj = _dot_nt(n, w_ref[...])
        p_ref[:, :UVZ_W] = proj[:, :UVZ_W]
        p_ref[:, UVZ_W:] = jnp.concatenate(
            [proj[:, UVZ_W + CONV_DIM:], jnp.zeros((PROJ_TM, LANES - SSM_HEADS), F32)], axis=1)
        xbc = proj[:, UVZ_W:UVZ_W + CONV_DIM]
        x_ref[...] = xbc.astype(BF16)
        ext_ref[HALO:, :] = xbc
        xc_ref[...] = _conv_taps(ext_ref, cw_ref[...], cb_ref[...], PROJ_TM)
        ext_ref[0:HALO, :] = ext_ref[PROJ_TM:PROJ_TM + HALO, :]

    return _tiled(body, "mix_in_fwd", T // PROJ_TM, [(h, PROJ_TM, D_MODEL, 0)], [g, conv_w, conv_b], [w_in_t],
                  [(T, PROJ_KEPT, F32, PROJ_TM), (T, D_MODEL, BF16, PROJ_TM), (T, CONV_DIM, BF16, PROJ_TM),
                   (T, CONV_DIM, F32, PROJ_TM)], [],
                  scratch=[pltpu.VMEM((HALO + PROJ_TM, CONV_DIM), F32)])


def _mix_in_dgrad(h, dh_in, dp_uv, dp_zxd, g, w_in_t, comm=None):
    T = h.shape[0]

    def body(i, h_ref, dh_ref, duv_ref, dzxd_ref, g_ref, w_ref, o_ref, dg_ref):
        dzxd, zx_w = dzxd_ref[...], ZXD_W - LANES
        dn = (_dot(duv_ref[...], w_ref[:UV_W, :]) + _dot(dzxd[:, :zx_w], w_ref[UV_W:UV_W + zx_w, :])
              + _dot(dzxd[:, zx_w:zx_w + SSM_HEADS], w_ref[UV_W + zx_w:, :]))
        _, rms_vjp = jax.vjp(_rms, h_ref[...], g_ref[...])
        dx, dg = rms_vjp(dn)
        o_ref[...] = dh_ref[...] + dx
        dg_ref[...] += dg

    return _tiled(body, "mix_in_dgrad", T // PROJ_DGRAD_TM,
                  [(h, PROJ_DGRAD_TM, D_MODEL, 0), (dh_in, PROJ_DGRAD_TM, D_MODEL, 0), (dp_uv, PROJ_DGRAD_TM, UV_W, 0),
                   (dp_zxd, PROJ_DGRAD_TM, ZXD_W, 0)], [g], [w_in_t],
                  [(T, D_MODEL, F32, PROJ_DGRAD_TM)], [((1, D_MODEL), F32)], comm=comm)


def _out_proj_dgrad(dh, w_out):
    T = dh.shape[0]

    def body(i, dh_ref, w_ref, dya_ref, dyb_ref):
        d = dh_ref[...].astype(BF16)
        dya_ref[...] = _dot_nt(d, w_ref[:GM_WIDTH, :])
        dyb_ref[...] = _dot_nt(d, w_ref[GM_WIDTH:, :])

    rows = min(T, 2 * PROJ_TM)
    return _tiled(body, "out_proj_dgrad", T // rows, [(dh, rows, D_MODEL, 0)], [], [w_out],
                  [(T, GM_WIDTH, F32, rows), (T, SSM_WIDTH, F32, rows)], [])


def _gm_chunk(u, v, ln_g, ln_b, b_st, out_g, *w_heads):
    ug = _gelu(u)
    vg = _gelu(v)
    mu = jnp.mean(vg, axis=-1, keepdims=True)
    xc = vg - mu
    vn = xc * lax.rsqrt(jnp.mean(xc * xc, axis=-1, keepdims=True) + EPS) * ln_g + ln_b
    t_idx = lax.broadcasted_iota(jnp.int32, (CHUNK, CHUNK), 0)
    s_idx = lax.broadcasted_iota(jnp.int32, (CHUNK, CHUNK), 1)
    causal = t_idx >= s_idx
    mixed = []
    for hd in range(GM_HEADS):
        wm = jnp.where(causal, w_heads[hd], 0.0)
        cols = slice(hd * GM_HEAD_DIM, (hd + 1) * GM_HEAD_DIM)
        mixed.append(_dot(wm, vn[:, cols]) + b_st[:, hd:hd + 1])
    ya0 = ug * jnp.concatenate(mixed, axis=1)
    return _rms(ya0, out_g)


GM_FWD_CHUNKS = 8


def _gm_fwd(proj, ln_g, ln_b, w_s, b_st, out_g):
    T = proj.shape[0]

    rows = GM_FWD_CHUNKS * CHUNK

    def body(i, u_ref, v_ref, lg_ref, lb_ref, w_ref, bs_ref, og_ref, ya_ref):
        w_heads = [w_ref[hd] for hd in range(GM_HEADS)]
        for c in range(GM_FWD_CHUNKS):
            tok = pl.ds(c * CHUNK, CHUNK)
            ya = _gm_chunk(u_ref[tok, :], v_ref[tok, :], lg_ref[...], lb_ref[...], bs_ref[...], og_ref[...], *w_heads)
            ya_ref[tok, :] = ya.astype(BF16)

    return _tiled(body, "gmlp_fwd", T // rows, [(proj, rows, GM_WIDTH, 0), (proj, rows, GM_WIDTH, 1)],
                  [ln_g, ln_b, w_s, b_st, out_g], [], [(T, GM_WIDTH, BF16, rows)], [])[0]


def _gm_bwd(proj, dya, ln_g, ln_b, w_s, b_st, out_g):
    T = proj.shape[0]

    def body(i, u_ref, v_ref, dy_ref, lg_ref, lb_ref, w_ref, bs_ref, og_ref, duv_ref, dlg_ref, dlb_ref, dw_ref, dbs_ref,
             dog_ref):
        w_heads = [w_ref[hd] for hd in range(GM_HEADS)]
        _, vjp = jax.vjp(_gm_chunk, u_ref[...], v_ref[...], lg_ref[...], lb_ref[...], bs_ref[...], og_ref[...], *w_heads)
        grads = vjp(dy_ref[...])
        duv_ref[:, :GM_WIDTH] = grads[0].astype(BF16)
        duv_ref[:, GM_WIDTH:] = grads[1].astype(BF16)
        dlg_ref[...] += grads[2]
        dlb_ref[...] += grads[3]
        dbs_ref[...] += grads[4]
        dog_ref[...] += grads[5]
        for hd in range(GM_HEADS):
            dw_ref[hd] += grads[6 + hd]

    return _tiled(body, "gmlp_bwd", T // CHUNK,
                  [(proj, CHUNK, GM_WIDTH, 0), (proj, CHUNK, GM_WIDTH, 1), (dya, CHUNK, GM_WIDTH, 0)],
                  [ln_g, ln_b, w_s, b_st, out_g], [], [(T, UV_W, BF16, CHUNK)],
                  [((1, GM_WIDTH), F32), ((1, GM_WIDTH), F32), ((GM_HEADS, CHUNK, CHUNK), F32),
                   ((CHUNK, GM_HEADS), F32), ((1, GM_WIDTH), F32)])


def _ssd_chunk(xc, z, dtr, s_in, dt_bias, a_log, d_skip, norm_g):
    half = SSM_WIDTH // SSM_GROUPS
    l_idx = lax.broadcasted_iota(jnp.int32, (CHUNK, CHUNK), 0)
    s_idx = lax.broadcasted_iota(jnp.int32, (CHUNK, CHUNK), 1)
    causal = l_idx >= s_idx
    head_of_col = lax.broadcasted_iota(jnp.int32, (SSM_HEADS, SSM_WIDTH), 1) // SSM_HEAD_DIM
    expand = (head_of_col == lax.broadcasted_iota(jnp.int32, (SSM_HEADS, SSM_WIDTH), 0)).astype(BF16)

    xcs = _silu(xc)
    xs = xcs[:, :SSM_WIDTH]
    dt = jax.nn.softplus(dtr + dt_bias)
    adt = dt * (-jnp.exp(a_log))
    acs = _cumsum_rows(adt, causal.astype(BF16))
    acs_t = _cumsum_cols(adt, (l_idx <= s_idx).astype(BF16))
    tot = acs[CHUNK - 1:CHUNK, :]
    dt_w = _widen(dt, expand)
    out_decay_w = _widen(jnp.exp(acs), expand)
    state_decay_w = _widen(jnp.exp(tot - acs), expand)
    chunk_decay_w = _widen(jnp.exp(tot), expand)
    d_skip_w = _widen(d_skip, expand)
    xdt = xs * dt_w
    xdt_decayed = xdt * state_decay_w

    y_diag, y_off, states = [], [], []
    for grp in range(SSM_GROUPS):
        b0 = SSM_WIDTH + grp * SSM_STATE
        c0 = SSM_WIDTH + SSM_GROUPS * SSM_STATE + grp * SSM_STATE
        bm = xcs[:, b0:b0 + SSM_STATE].astype(BF16)
        cm = xcs[:, c0:c0 + SSM_STATE].astype(BF16)
        cb = _dot_nt(cm, bm)
        for k in range(grp * SSM_HEADS // SSM_GROUPS, (grp + 1) * SSM_HEADS // SSM_GROUPS):
            decay = jnp.exp(jnp.where(causal, acs[:, k:k + 1] - acs_t[k:k + 1, :], -jnp.inf))
            y_diag.append(_dot(cb * decay, xdt[:, k * SSM_HEAD_DIM:(k + 1) * SSM_HEAD_DIM]))
        cols = slice(grp * half, (grp + 1) * half)
        states.append(_dot_tn(bm, xdt_decayed[:, cols]))
        y_off.append(_dot(cm, s_in[:, cols]))
    y = jnp.concatenate(y_diag, axis=1) + jnp.concatenate(y_off, axis=1) * out_decay_w + xs * d_skip_w
    s_out = s_in * chunk_decay_w + jnp.concatenate(states, axis=1)
    y = y * _silu(z)
    normed = []
    for grp in range(SSM_GROUPS):
        yg = y[:, grp * half:(grp + 1) * half]
        normed.append(yg * lax.rsqrt(jnp.mean(yg * yg, axis=-1, keepdims=True) + EPS))
    return jnp.concatenate(normed, axis=1) * norm_g, s_out


def _sum_row_tiles(x):
    return x.reshape(x.shape[0] // F32_ROWS, F32_ROWS, x.shape[1]).sum(axis=0)


def _conv_taps(ext_ref, w, b, rows):
    y = b
    for k in range(SSM_CONV):
        y = y + w[k:k + 1, :] * ext_ref[pl.ds(HALO - (SSM_CONV - 1) + k, rows), :]
    return y


SSD_FWD_CHUNKS = 8


def _ssd_fwd(proj, xc, dt_bias, a_log, d_skip, norm_g, comm=None):
    T = proj.shape[0]
    n_chunks = T // CHUNK
    rows = SSD_FWD_CHUNKS * CHUNK

    def body(i, z_ref, xc_ref, dt_ref, dtb_ref, al_ref, dsk_ref, ng_ref, yb_ref, sin_ref, st_ref):
        @pl.when(i == 0)
        def _():
            st_ref[...] = jnp.zeros(st_ref.shape, F32)

        for c in range(SSD_FWD_CHUNKS):
            tok = pl.ds(c * CHUNK, CHUNK)
            s_in = st_ref[...]
            yb, s_out = _ssd_chunk(xc_ref[tok, :], z_ref[tok, :], dt_ref[tok, 0:SSM_HEADS], s_in, dtb_ref[...], al_ref[...],
                                   dsk_ref[...], ng_ref[...])
            yb_ref[tok, :] = yb.astype(BF16)
            sin_ref[pl.ds(c * SSM_STATE, SSM_STATE), :] = s_in
            st_ref[...] = s_out

    return _tiled(body, "ssd_fwd", T // rows,
                  [(proj, rows, SSM_WIDTH, Z_BLK), (xc, rows, CONV_DIM, 0), (proj, rows, LANES, DT_BLK)],
                  [dt_bias, a_log, d_skip, norm_g], [],
                  [(T, SSM_WIDTH, BF16, rows), (n_chunks * SSM_STATE, SSM_WIDTH, F32, SSD_FWD_CHUNKS * SSM_STATE)], [],
                  scratch=[pltpu.VMEM((SSM_STATE, SSM_WIDTH), F32)], comm=comm)


def _ssd_bwd(proj, x16, xc, dyb, s_all, conv_w, dt_bias, a_log, d_skip, norm_g, comm=None):
    T = proj.shape[0]
    n_chunks = T // CHUNK

    def body(i, z_ref, x_ref, xc_ref, dt_ref, dy_ref, sin_ref, cw_ref, dtb_ref, al_ref, dsk_ref, ng_ref,
             dzxd_ref, dcw_ref, dcb_ref, ddtb_ref, dal_ref, ddsk_ref, dng_ref, dext_ref, dst_ref, cw_acc, cb_acc):
        @pl.when(i == n_chunks - 1)
        def _():
            dext_ref[CHUNK:, :] = jnp.zeros((HALO, CONV_DIM), F32)
            dst_ref[...] = jnp.zeros(dst_ref.shape, F32)
            cw_acc[...] = jnp.zeros(cw_acc.shape, F32)
            cb_acc[...] = jnp.zeros(cb_acc.shape, F32)

        _, vjp = jax.vjp(_ssd_chunk, xc_ref[...], z_ref[...], dt_ref[:, 0:SSM_HEADS], sin_ref[...], dtb_ref[...], al_ref[...],
                         dsk_ref[...], ng_ref[...])
        dxc, dz, ddtr, ds_in, ddtb, dal, ddsk, dng = vjp((dy_ref[...], dst_ref[...]))
        dst_ref[...] = ds_in
        ddtb_ref[...] += ddtb
        dal_ref[...] += dal
        ddsk_ref[...] += ddsk
        dng_ref[...] += dng
        dext_ref[0:CHUNK, :] = dxc
        cw = cw_ref[...]
        x = x_ref[...].astype(F32)
        dx = jnp.zeros((CHUNK, CONV_DIM), F32)
        for k in range(SSM_CONV):
            shifted = dext_ref[pl.ds(SSM_CONV - 1 - k, CHUNK), :]
            dx = dx + cw[k:k + 1, :] * shifted
            cw_acc[k] += _sum_row_tiles(shifted * x)
        cb_acc[...] += _sum_row_tiles(dxc)

        @pl.when(i == 0)
        def _():
            dcw_ref[...] = jnp.sum(cw_acc[...], axis=1)
            dcb_ref[...] = jnp.sum(cb_acc[...], axis=0, keepdims=True)

        dext_ref[CHUNK:, :] = dext_ref[0:HALO, :]
        dzxd_ref[:, 0:SSM_WIDTH] = dz.astype(BF16)
        dzxd_ref[:, SSM_WIDTH:SSM_WIDTH + CONV_DIM] = dx.astype(BF16)
        dzxd_ref[:, SSM_WIDTH + CONV_DIM:] = jnp.concatenate(
            [ddtr, jnp.zeros((CHUNK, LANES - SSM_HEADS), F32)], axis=1).astype(BF16)

    return _tiled(body, "ssd_bwd", n_chunks,
                  [(proj, CHUNK, SSM_WIDTH, Z_BLK), (x16, CHUNK, CONV_DIM, 0), (xc, CHUNK, CONV_DIM, 0),
                   (proj, CHUNK, LANES, DT_BLK), (dyb, CHUNK, SSM_WIDTH, 0), (s_all, SSM_STATE, SSM_WIDTH, 0)],
                  [conv_w, dt_bias, a_log, d_skip, norm_g], [],
                  [(T, ZXD_W, BF16, CHUNK)],
                  [((SSM_CONV, CONV_DIM), F32), ((1, CONV_DIM), F32), ((1, SSM_HEADS), F32), ((1, SSM_HEADS), F32),
                   ((1, SSM_HEADS), F32), ((1, SSM_WIDTH), F32)],
                  scratch=[pltpu.VMEM((CHUNK + HALO, CONV_DIM), F32), pltpu.VMEM((SSM_STATE, SSM_WIDTH), F32),
                           pltpu.VMEM((SSM_CONV, F32_ROWS, CONV_DIM), F32), pltpu.VMEM((F32_ROWS, CONV_DIM), F32)],
                  reverse=True, comm=comm)


TAIL_TM = 512


def _tail(h, p, target, ple_norm, w_gate, b_gate, w_proj_t, final_norm):
    T = h.shape[0]

    def head(x, pre, pp, b_g, f_norm, tgt):
        gate = jax.nn.sigmoid(pre + b_g)
        out = _rms(x + gate * pp, f_norm)
        err = out - tgt
        return 0.5 * jnp.sum(jnp.mean(err * err, axis=-1, keepdims=True), axis=0, keepdims=True)

    def body(i, h_ref, p_ref, t_ref, pn_ref, bg_ref, fn_ref, wg_ref, wp_ref, dh_ref, loss_ref, dwg_ref, dwp_ref, dpn_ref,
             dbg_ref, dfn_ref):
        x = h_ref[...]
        n4f, n_vjp = jax.vjp(_rms, x, pn_ref[...])
        n4 = n4f.astype(BF16)
        pre = jnp.dot(n4, wg_ref[...], preferred_element_type=F32)
        p16 = p_ref[...].astype(BF16)
        pp = _dot_nt(p16, wp_ref[...])
        loss, h_vjp = jax.vjp(functools.partial(head, tgt=t_ref[...]), x, pre, pp, bg_ref[...], fn_ref[...])
        dx, dpre, dpp, dbg, dfn = h_vjp(jnp.ones((1, 1), F32))
        dpre16 = dpre.astype(BF16)
        dn4 = _dot_nt(dpre16, wg_ref[...])
        dx2, dpn = n_vjp(dn4)
        dh_ref[...] = dx + dx2
        loss_ref[...] += loss
        dwg_ref[...] += _dot_tn(n4, dpre16)
        dwp_ref[...] += _dot_tn(p16, dpp)
        dpn_ref[...] += dpn
        dbg_ref[...] += dbg
        dfn_ref[...] += dfn

    return _tiled(body, "tail", T // TAIL_TM,
                  [(h, TAIL_TM, D_MODEL, 0), (p, TAIL_TM, D_PLE, 0), (target, TAIL_TM, D_MODEL, 0)],
                  [ple_norm, b_gate, final_norm], [w_gate, w_proj_t],
                  [(T, D_MODEL, F32, TAIL_TM)],
                  [((1, 1), F32), ((D_MODEL, D_MODEL), F32), ((D_PLE, D_MODEL), F32), ((1, D_MODEL), F32),
                   ((1, D_MODEL), F32), ((1, D_MODEL), F32)])


def _gather_phases(x_ref, out_ref, send_sems, recv_sems, local_sem):
    mx, my, mc = lax.axis_index("x"), lax.axis_index("y"), lax.axis_index("c")
    me, sibling = (mx, my, mc), (mx, my, 1 - mc)
    chips = [(1 - mx, my), (mx, 1 - my), (1 - mx, 1 - my)]

    def rows(px, py, pc):
        return out_ref.at[4 * px + 2 * py + pc]

    def copy(k, block, to, src=None):
        return pltpu.make_async_remote_copy(
            src_ref=rows(*block) if src is None else src, dst_ref=rows(*block),
            send_sem=send_sems.at[k], recv_sem=recv_sems.at[k], device_id=to, device_id_type=MESH)

    mine = pltpu.make_async_copy(x_ref, rows(*me), local_sem)
    first = [copy(0, me, sibling, src=x_ref)] + [copy(1 + j, me, (*chip, mc), src=x_ref) for j, chip in enumerate(chips)]
    passed = [copy(4 + j, (*chip, mc), sibling) for j, chip in enumerate(chips)]

    def start():
        mine.start()
        for cp in first:
            cp.start()

    def mid():
        for j, chip in enumerate(chips):
            copy(1 + j, (*chip, mc), me).wait_recv()
            passed[j].start()

    def finish():
        copy(0, sibling, me).wait_recv()
        for j, chip in enumerate(chips):
            copy(4 + j, (*chip, 1 - mc), me).wait_recv()
        for cp in first + passed:
            cp.wait_send()
        mine.wait()

    return start, mid, finish


def _exchange_phases(x_ref, out_ref, send_sems, recv_sems, local_sem):
    mx, my, mc = lax.axis_index("x"), lax.axis_index("y"), lax.axis_index("c")
    me = 4 * mx + 2 * my + mc
    mine = pltpu.make_async_copy(x_ref.at[me], out_ref.at[me], local_sem)
    copies = []
    for k in range(1, N_DEV):
        px = 1 - mx if k & 4 else mx
        py = 1 - my if k & 2 else my
        pc = 1 - mc if k & 1 else mc
        copies.append(pltpu.make_async_remote_copy(
            src_ref=x_ref.at[4 * px + 2 * py + pc], dst_ref=out_ref.at[me], send_sem=send_sems.at[k - 1],
            recv_sem=recv_sems.at[k - 1], device_id=(px, py, pc), device_id_type=MESH))

    def start():
        mine.start()
        for cp in copies:
            cp.start()

    def finish():
        for cp in copies:
            cp.wait_recv()
        for cp in copies:
            cp.wait_send()
        mine.wait()

    return start, lambda: None, finish


def _chip_exchange_phases(x_ref, out_ref, mine, recv, sums, load_sems, pair_send, pair_recv, chip_send, chip_recv, out_sem):
    mx, my, mc = lax.axis_index("x"), lax.axis_index("y"), lax.axis_index("c")
    my_chip = 2 * mx + my
    load = [pltpu.make_async_copy(x_ref.at[2 * q + mc], mine.at[q], load_sems.at[q]) for q in range(N_CHIPS)]
    to_sibling = [pltpu.make_async_remote_copy(
        src_ref=x_ref.at[2 * q + 1 - mc], dst_ref=recv.at[q], send_sem=pair_send.at[q], recv_sem=pair_recv.at[q],
        device_id=(mx, my, 1 - mc), device_id_type=MESH) for q in range(N_CHIPS)]
    to_chips = []
    for k in range(1, N_CHIPS):
        px = 1 - mx if k & 2 else mx
        py = 1 - my if k & 1 else my
        to_chips.append(pltpu.make_async_remote_copy(
            src_ref=sums.at[2 * px + py], dst_ref=out_ref.at[my_chip], send_sem=chip_send.at[k - 1],
            recv_sem=chip_recv.at[k - 1], device_id=(px, py, mc), device_id_type=MESH))
    keep = pltpu.make_async_copy(sums.at[my_chip], out_ref.at[my_chip], out_sem)

    def start():
        for cp in load + to_sibling:
            cp.start()

    def mid():
        for cp in load:
            cp.wait()
        for cp in to_sibling:
            cp.wait_recv()
        for q in range(N_CHIPS):
            sums[q] = (mine[q].astype(F32) + recv[q].astype(F32)).astype(sums.dtype)
        for cp in to_chips + [keep]:
            cp.start()

    def finish():
        for cp in to_chips:
            cp.wait_recv()
        for cp in to_chips + to_sibling:
            cp.wait_send()
        keep.wait()

    return start, mid, finish


FLAT_SCRATCH = (pltpu.SemaphoreType.DMA((N_DEV - 1,)), pltpu.SemaphoreType.DMA((N_DEV - 1,)), pltpu.SemaphoreType.DMA)


def _gather_comm(x):
    return _Comm(_gather_phases, x, jax.ShapeDtypeStruct((N_DEV,) + x.shape, x.dtype), FLAT_SCRATCH)


def _exchange_comm(x):
    return _Comm(_exchange_phases, x, jax.ShapeDtypeStruct(x.shape, x.dtype), FLAT_SCRATCH)


def _chip_exchange_comm(x):
    stage = pltpu.VMEM((N_CHIPS,) + x.shape[1:], x.dtype)
    sems = [pltpu.SemaphoreType.DMA((n,)) for n in (N_CHIPS, N_CHIPS, N_CHIPS, N_CHIPS - 1, N_CHIPS - 1)]
    return _Comm(_chip_exchange_phases, x, jax.ShapeDtypeStruct((N_CHIPS,) + x.shape[1:], x.dtype),
                 (stage, stage, stage, *sems, pltpu.SemaphoreType.DMA))


def _comm_alone(comms, name):
    n = len(comms)

    def body(*refs):
        phases, first = [], 2 * n
        for k, comm in enumerate(comms):
            phases.append(comm.phases(refs[k], refs[n + k], *refs[first:first + len(comm.scratch)]))
            first += len(comm.scratch)
        for step in range(3):
            for phase in phases:
                phase[step]()

    any_spec = pl.BlockSpec(memory_space=pl.ANY)
    return pl.pallas_call(
        body,
        out_shape=[comm.dst for comm in comms],
        in_specs=[any_spec] * n,
        out_specs=[any_spec] * n,
        scratch_shapes=[shape for comm in comms for shape in comm.scratch],
        name=name,
        compiler_params=pltpu.CompilerParams(vmem_limit_bytes=VMEM_LIMIT),
    )(*[comm.src for comm in comms])


def _sum_parts(p_ref):
    g = p_ref[0].astype(F32)
    for j in range(1, p_ref.shape[0]):
        g = g + p_ref[j].astype(F32)
    return g


def _adamw_store(g, w_ref, m_ref, v_ref, g_ref, d_ref, nm_ref, nv_ref):
    m_new = ADAM_B1 * m_ref[...] + (1.0 - ADAM_B1) * g
    v_new = ADAM_B2 * v_ref[...] + (1.0 - ADAM_B2) * jnp.square(g)
    m_hat = m_new / (1.0 - ADAM_B1 ** ADAM_STEP)
    v_hat = v_new / (1.0 - ADAM_B2 ** ADAM_STEP)
    g_ref[...] = g
    d_ref[...] = -ADAM_LR * (m_hat / (jnp.sqrt(v_hat) + ADAM_EPS) + ADAM_WD * w_ref[...])
    nm_ref[...] = m_new
    nv_ref[...] = v_new


def _adamw_shard(parts, off, w, m, v, name, n_tiles):
    _, rows, c = w.shape
    assert c == PACK_COLS
    by_rows = rows % BF16_ROWS == 0
    if by_rows:
        tr = rows // n_tiles
        window = (parts.shape[0], tr, PACK_COLS)
        spec = pl.BlockSpec((None, tr, PACK_COLS), lambda i: (0, i, 0))
    else:
        padded, tc = -(-rows // BF16_ROWS) * BF16_ROWS, PACK_COLS // n_tiles
        window = (parts.shape[0], padded, tc)
        spec = pl.BlockSpec((None, rows, tc), lambda i: (0, 0, i))
    blocked = off % (tr if by_rows else padded) == 0

    def update(p_ref, refs):
        g = _sum_parts(p_ref)
        if not by_rows:
            keep = lax.broadcasted_iota(jnp.int32, (rows, padded), 0) == lax.broadcasted_iota(jnp.int32, (rows, padded), 1)
            g = _exact_dot(g, keep.astype(BF16), ((1,), (0,)), x_first=False)
        _adamw_store(g, *refs)

    def kern_blocked(p_ref, *refs):
        update(p_ref, refs)

    def kern_copied(p_hbm, *refs):
        buf, sem = refs[-2:]
        i = pl.program_id(0)
        if by_rows:
            src = p_hbm.at[:, pl.ds(pl.multiple_of(off + i * tr, BF16_ROWS), tr), :]
        else:
            src = p_hbm.at[:, pl.ds(off, padded), pl.ds(pl.multiple_of(i * tc, LANES), tc)]
        cp = pltpu.make_async_copy(src, buf, sem)
        cp.start()
        cp.wait()
        update(buf, refs[:-2])

    if blocked:
        index = (lambda i: (0, off // tr + i, 0)) if by_rows else (lambda i: (0, off // padded, i))
        parts_spec, scratch = pl.BlockSpec(window, index), []
    else:
        parts_spec, scratch = pl.BlockSpec(memory_space=pl.ANY), [pltpu.VMEM(window, parts.dtype), pltpu.SemaphoreType.DMA]
    return pl.pallas_call(
        kern_blocked if blocked else kern_copied,
        out_shape=[jax.ShapeDtypeStruct(w.shape, F32)] * 4,
        grid=(n_tiles,),
        in_specs=[parts_spec, spec, spec, spec],
        out_specs=[spec] * 4,
        scratch_shapes=scratch,
        name=name,
        compiler_params=pltpu.CompilerParams(dimension_semantics=("arbitrary",), vmem_limit_bytes=VMEM_LIMIT),
    )(parts, w, m, v)


def _sum_adamw(parts, w, m, v, tr, name):
    _, R, C = parts.shape

    def kern(p_ref, w_ref, m_ref, v_ref, g_ref, d_ref, nm_ref, nv_ref):
        _adamw_store(_sum_parts(p_ref), w_ref, m_ref, v_ref, g_ref, d_ref, nm_ref, nv_ref)

    row_spec = pl.BlockSpec((tr, C), lambda i: (i, 0))
    return pl.pallas_call(
        kern,
        out_shape=[jax.ShapeDtypeStruct((R, C), F32)] * 4,
        grid=(R // tr,),
        in_specs=[pl.BlockSpec((N_DEV, tr, C), lambda i: (0, i, 0)), row_spec, row_spec, row_spec],
        out_specs=[row_spec] * 4,
        name=name,
        compiler_params=pltpu.CompilerParams(dimension_semantics=("arbitrary",), vmem_limit_bytes=VMEM_LIMIT),
    )(parts, w, m, v)


FF_SHARD = D_FF // N_DEV
CONV_SHARD = (SSM_CONV, CONV_DIM // N_DEV)
SHARDS = {"ffn1_w_gate": ((D_MODEL, FF_SHARD), True), "ffn1_w_up": ((D_MODEL, FF_SHARD), True),
          "ffn1_w_down": ((FF_SHARD, D_MODEL), False),
          "ffn2_w_gate": ((D_MODEL, FF_SHARD), True), "ffn2_w_up": ((D_MODEL, FF_SHARD), True),
          "ffn2_w_down": ((FF_SHARD, D_MODEL), False),
          "w_out": ((2 * D_MODEL // N_DEV, D_MODEL), False), "ple_w_gate": ((D_MODEL // N_DEV, D_MODEL), False),
          "w_in": ((D_MODEL, IN_PROJ // N_DEV), True), "ple_w_proj": ((D_PLE, D_MODEL // N_DEV), True),
          "conv_w": (CONV_SHARD, True),
          "conv_w_mid": (CONV_SHARD, True), "conv_w_low": (CONV_SHARD, True)}
BIG = tuple(name for name in SHARDS if not name.startswith("conv_w_"))
SMALL = ("ffn1_norm", "mix_norm", "gm_ln_g", "gm_ln_b", "gm_w_s", "gm_b_s", "gm_out_norm", "conv_b", "dt_bias", "a_log",
         "d_skip", "ssm_norm", "ffn2_norm", "ple_norm", "ple_b_gate", "final_norm")
SMALL_ROWS = 144


def _piece_rows(name):
    shape = SHARDS[name][0]
    return -(-(shape[0] * shape[1]) // PACK_COLS)


def _pad_cols(flat, name):
    pad = _piece_rows(name) * PACK_COLS - flat.shape[-1]
    return flat if pad == 0 else jnp.pad(flat, [(0, 0)] * (flat.ndim - 1) + [(0, pad)])


class _Pack:
    def __init__(self, names, tile_rows):
        self.names, self.tile_rows, self.offsets, off = names, tile_rows, {}, 0
        for name in names:
            self.offsets[name] = off
            off += _piece_rows(name)
        self.rows = -(-off // tile_rows) * tile_rows

    def pack_local(self, vals):
        parts = []
        for name in self.names:
            val = vals[name]
            parts.append(_pad_cols((val.T if SHARDS[name][1] else val).reshape(-1), name))
        flat = jnp.concatenate(parts)
        return jnp.pad(flat, (0, self.rows * PACK_COLS - flat.shape[0])).reshape(self.rows, PACK_COLS)

    def pack_owner_major(self, grads):
        parts, rows = [], 0
        for name in self.names:
            grad, piece_rows = grads[name].astype(BF16), _piece_rows(name)
            if grad.shape != (N_DEV * piece_rows, PACK_COLS):
                grad = _pad_cols(grad.reshape(N_DEV, -1), name)
            parts.append(grad.reshape(N_DEV, piece_rows, PACK_COLS))
            rows += piece_rows
        if rows < self.rows:
            parts.append(jnp.zeros((N_DEV, self.rows - rows, PACK_COLS), BF16))
        return parts[0] if len(parts) == 1 else jnp.concatenate(parts, axis=1)

    def gathered_piece(self, gathered, name):
        shape = SHARDS[name][0]
        rows = gathered[:, self.offsets[name]:self.offsets[name] + _piece_rows(name), :]
        return rows.reshape(N_DEV, -1)[:, :shape[0] * shape[1]]

    def pieces(self, gathered, name):
        return _Pieces(gathered, self.offsets[name], _piece_rows(name))


GATHER_FFN1 = _Pack(("ffn1_w_gate", "ffn1_w_up", "ffn1_w_down"), BF16_ROWS)
GATHER_MIX = _Pack(("w_out", "ple_w_gate", "w_in", "ple_w_proj", "conv_w", "conv_w_mid", "conv_w_low"), BF16_ROWS)
GATHER_FFN2 = _Pack(("ffn2_w_gate", "ffn2_w_up", "ffn2_w_down"), BF16_ROWS)
SCATTER_LATE = _Pack(("ffn2_w_gate", "ffn2_w_up", "ffn2_w_down", "w_out", "ple_w_gate", "ple_w_proj"), BF16_ROWS)
SCATTER_IN = _Pack(("w_in", "conv_w"), BF16_ROWS)
SCATTER_GATE = _Pack(("ffn1_w_gate",), BF16_ROWS)
SCATTER_UP = _Pack(("ffn1_w_up",), BF16_ROWS)
SCATTER_DOWN = _Pack(("ffn1_w_down",), BF16_ROWS)


def _pack_small(vals, behind=()):
    flat = jnp.concatenate([vals[name].reshape(-1).astype(F32) for name in SMALL] + [b.reshape(-1) for b in behind])
    return jnp.pad(flat, (0, SMALL_ROWS * PACK_COLS - flat.shape[0])).reshape(SMALL_ROWS, PACK_COLS)


def _unpack_small(packed, shapes):
    out, off = {}, 0
    flat = packed.reshape(-1)
    for name in SMALL:
        n = 1
        for s in shapes[name]:
            n *= s
        out[name] = flat[off:off + n].reshape(shapes[name])
        off += n
    return out


WEIGHTS = ("ffn1_norm", "ffn1_w_gate", "ffn1_w_up", "ffn1_w_down", "mix_norm", "w_in", "gm_ln_g", "gm_ln_b", "gm_w_s",
           "gm_b_s", "gm_out_norm", "conv_w", "conv_b", "dt_bias", "a_log", "d_skip", "ssm_norm", "w_out", "ffn2_norm",
           "ffn2_w_gate", "ffn2_w_up", "ffn2_w_down", "ple_norm", "ple_w_gate", "ple_b_gate", "ple_w_proj", "final_norm")


def _step(x, p, target, w, m, v):
    local = lambda d: {name: d[name][0] for name in BIG}

    shards = {name: val.astype(BF16) for name, val in local(w).items()}
    conv_high = lax.reduce_precision(w["conv_w"][0], 8, 7)
    conv_mid = lax.reduce_precision(w["conv_w"][0] - conv_high, 8, 7)
    shards["conv_w"] = conv_high.astype(BF16)
    shards["conv_w_mid"] = conv_mid.astype(BF16)
    shards["conv_w_low"] = (w["conv_w"][0] - conv_high - conv_mid).astype(BF16)
    g_ffn1 = _comm_alone([_gather_comm(GATHER_FFN1.pack_local(shards))], "gather_ffn1")[0]

    row = lambda name: w[name].reshape(1, -1)
    gm_w_s = w["gm_w_s"][0]
    gm_b_st = jnp.transpose(w["gm_b_s"][0])
    ffn1 = (row("ffn1_norm"),) + tuple(GATHER_FFN1.pieces(g_ffn1, name) for name in GATHER_FFN1.names)
    gm = (row("gm_ln_g"), row("gm_ln_b"), gm_w_s, gm_b_st, row("gm_out_norm"))

    h1, n1, a1, b1, s1, g_mix = _ffn_fwd(x, *ffn1, "ffn1_fwd", comm=_gather_comm(GATHER_MIX.pack_local(shards)))
    w_in_t = GATHER_MIX.gathered_piece(g_mix, "w_in").reshape(IN_PROJ, D_MODEL)
    w_proj_t = GATHER_MIX.gathered_piece(g_mix, "ple_w_proj").reshape(D_MODEL, D_PLE)
    conv_w = sum(GATHER_MIX.gathered_piece(g_mix, name).astype(F32) for name in ("conv_w", "conv_w_mid", "conv_w_low"))
    conv_w = conv_w.reshape(CONV_DIM, SSM_CONV).T
    ssd = (row("dt_bias"), row("a_log"), row("d_skip"), row("ssm_norm"))
    w_out = GATHER_MIX.pieces(g_mix, "w_out")

    proj, n2, x16, xc = _mix_in_fwd(h1, row("mix_norm"), w_in_t, conv_w, row("conv_b"))
    ya = _gm_fwd(proj, *gm)
    yb, s_all, g_ffn2 = _ssd_fwd(proj, xc, *ssd, comm=_gather_comm(GATHER_FFN2.pack_local(shards)))
    ffn2 = (row("ffn2_norm"),) + tuple(GATHER_FFN2.pieces(g_ffn2, name) for name in GATHER_FFN2.names)
    h3, n3, a3, b3, s3, h2 = _ffn_fwd(h1, *ffn2, "ffn2_fwd", mixed=(ya, yb, w_out))

    g, gp = {}, {}
    dh3, loss, gp["ple_w_gate"], d_w_proj, g["ple_norm"], g["ple_b_gate"], g["final_norm"] = _tail(
        h3, p, target, row("ple_norm"), GATHER_MIX.pieces(g_mix, "ple_w_gate"), row("ple_b_gate"), w_proj_t,
        row("final_norm"))
    gp["ple_w_proj"] = d_w_proj.T

    dh2, da3, db3, g["ffn2_norm"] = _ffn_dgrad(h2, dh3, a3, b3, *ffn2, "ffn2_dgrad")
    gp["ffn2_w_gate"] = _wgrad(n3, da3, FF_BN, "ffn2_wgrad_gate", transpose_out=True)
    gp["ffn2_w_up"] = _wgrad(n3, db3, FF_BN, "ffn2_wgrad_up", transpose_out=True)
    gp["ffn2_w_down"] = _wgrad(s3, dh3, DOWN_BN, "ffn2_wgrad_down", scale=0.5, bk=DOWN_BK)

    dya, dyb = _out_proj_dgrad(dh2, w_out)
    gp["w_out"] = jnp.concatenate([_wgrad(ya, dh2, SQUARE_BN, "w_out_wgrad_a"), _wgrad(yb, dh2, SQUARE_BN, "w_out_wgrad_b")], axis=0)

    dp_zxd, d_conv_w, g["conv_b"], g["dt_bias"], g["a_log"], g["d_skip"], g["ssm_norm"], parts_late = _ssd_bwd(
        proj, x16, xc, dyb, s_all, conv_w, *ssd, comm=_exchange_comm(SCATTER_LATE.pack_owner_major(gp)))
    gp["conv_w"] = d_conv_w.T
    dp_uv, g["gm_ln_g"], g["gm_ln_b"], g["gm_w_s"], dbst, g["gm_out_norm"] = _gm_bwd(proj, dya, *gm)
    g["gm_b_s"] = jnp.transpose(dbst)

    parts = {}
    gp["w_in"] = jnp.concatenate([_wgrad(n2, dp_uv, SQUARE_BN, "w_in_wgrad_uv", transpose_out=True),
                                  _wgrad(n2, dp_zxd, ZXD_BN, "w_in_wgrad_zxd", transpose_out=True)], axis=0)[:IN_PROJ]
    dh1, g["mix_norm"], parts[SCATTER_IN] = _mix_in_dgrad(h1, dh2, dp_uv, dp_zxd, row("mix_norm"), w_in_t,
                                                          comm=_exchange_comm(SCATTER_IN.pack_owner_major(gp)))

    dx, da1, db1, g["ffn1_norm"] = _ffn_dgrad(x, dh1, a1, b1, *ffn1, "ffn1_dgrad")
    gp["ffn1_w_gate"], small_parts = _wgrad(n1, da1, FF_BN, "ffn1_wgrad_gate", transpose_out=True,
                                            comm=_gather_comm(_pack_small(g, behind=[loss])))
    gp["ffn1_w_up"], parts[SCATTER_GATE] = _wgrad(n1, db1, FF_BN, "ffn1_wgrad_up", transpose_out=True,
                                                  comm=_chip_exchange_comm(SCATTER_GATE.pack_owner_major(gp)))
    gp["ffn1_w_down"], parts[SCATTER_UP] = _wgrad(s1, dh1, DOWN_BN, "ffn1_wgrad_down", scale=0.5, bk=DOWN_BK,
                                                  comm=_chip_exchange_comm(SCATTER_UP.pack_owner_major(gp)))
    parts[SCATTER_DOWN] = _comm_alone([_chip_exchange_comm(SCATTER_DOWN.pack_owner_major(gp))], "scatter_ffn1_down")[0]
    parts[SCATTER_LATE] = parts_late

    res_big = {}
    for pack, pack_parts in parts.items():
        for name in pack.names:
            shape, transposed = SHARDS[name]
            if name in ("ple_w_proj", "conv_w"):
                nat = pack.gathered_piece(pack_parts, name).reshape((N_DEV,) + shape[::-1])
                res_big[name] = _sum_adamw(jnp.transpose(nat, (0, 2, 1)), w[name][0], m[name][0], v[name][0], shape[0],
                                           "adamw_" + name)
            else:
                flip = (lambda a: jnp.transpose(a, (0, 2, 1))) if transposed else (lambda a: a)
                res = _adamw_shard(pack_parts, pack.offsets[name], flip(w[name]), flip(m[name]), flip(v[name]),
                                   "adamw_" + name, n_tiles=4 if name == "w_in" else 2)
                res_big[name] = [flip(r) for r in res]

    small_shapes = {name: w[name].shape for name in SMALL}
    res_small = _sum_adamw(small_parts, _pack_small(w), _pack_small(m), _pack_small(v), SMALL_ROWS, "adamw_small")
    loss = res_small[0].reshape(-1)[sum(w[name].size for name in SMALL)]
    res_small = [_unpack_small(r, small_shapes) for r in res_small]

    outs = []
    for k in range(4):
        for name in WEIGHTS:
            if name in res_small[k]:
                outs.append(res_small[k][name])
            else:
                outs.append(res_big[name][k].reshape(w[name].shape))
    return loss, dx, outs


def kernel(x, p, ffn1_norm, ffn1_w_gate, ffn1_w_up, ffn1_w_down, mix_norm, w_in, gm_ln_g, gm_ln_b, gm_w_s, gm_b_s, gm_out_norm, conv_w, conv_b, dt_bias, a_log, d_skip, ssm_norm, w_out, ffn2_norm, ffn2_w_gate, ffn2_w_up, ffn2_w_down, ple_norm, ple_w_gate, ple_b_gate, ple_w_proj, final_norm, loss_target, m_ffn1_norm, m_ffn1_w_gate, m_ffn1_w_up, m_ffn1_w_down, m_mix_norm, m_w_in, m_gm_ln_g, m_gm_ln_b, m_gm_w_s, m_gm_b_s, m_gm_out_norm, m_conv_w, m_conv_b, m_dt_bias, m_a_log, m_d_skip, m_ssm_norm, m_w_out, m_ffn2_norm, m_ffn2_w_gate, m_ffn2_w_up, m_ffn2_w_down, m_ple_norm, m_ple_w_gate, m_ple_b_gate, m_ple_w_proj, m_final_norm, v_ffn1_norm, v_ffn1_w_gate, v_ffn1_w_up, v_ffn1_w_down, v_mix_norm, v_w_in, v_gm_ln_g, v_gm_ln_b, v_gm_w_s, v_gm_b_s, v_gm_out_norm, v_conv_w, v_conv_b, v_dt_bias, v_a_log, v_d_skip, v_ssm_norm, v_w_out, v_ffn2_norm, v_ffn2_w_gate, v_ffn2_w_up, v_ffn2_w_down, v_ple_norm, v_ple_w_gate, v_ple_b_gate, v_ple_w_proj, v_final_norm):
    args = locals()
    w = {name: args[name] for name in WEIGHTS}
    m = {name: args["m_" + name] for name in WEIGHTS}
    v = {name: args["v_" + name] for name in WEIGHTS}
    loss, dx, outs = _step(x[0], p[0, 0], loss_target[0], w, m, v)
    return (loss, dx[None], *outs)
```

```python
import functools
from typing import NamedTuple

import jax
import jax.numpy as jnp
from jax import lax
from jax.experimental import pallas as pl
from jax.experimental.pallas import tpu as pltpu

F32 = jnp.float32
BF16 = jnp.bfloat16
MESH = pl.DeviceIdType.MESH
N_DEV = 8
N_CHIPS = 4

D_MODEL = 1024
D_FF = 2816
D_PLE = 256
GM_WIDTH = 1024
GM_HEADS = 8
GM_HEAD_DIM = 128
CHUNK = 128
SSM_WIDTH = 1024
SSM_HEADS = 16
SSM_HEAD_DIM = 64
SSM_GROUPS = 2
SSM_STATE = 128
SSM_CONV = 4
CONV_DIM = SSM_WIDTH + 2 * SSM_GROUPS * SSM_STATE
IN_PROJ = 2 * GM_WIDTH + SSM_WIDTH + CONV_DIM + SSM_HEADS
LANES = 128
BF16_ROWS = 16
F32_ROWS = 8
IN_PROJ_PAD = IN_PROJ - SSM_HEADS + LANES
UV_W = 2 * GM_WIDTH
ZXD_W = IN_PROJ_PAD - UV_W
HALO = 8
EPS = 1e-6

ADAM_LR = 0.001
ADAM_B1 = 0.9
ADAM_B2 = 0.999
ADAM_EPS = 1e-08
ADAM_WD = 0.01
ADAM_STEP = 10

VMEM_LIMIT = 56 * 1024 * 1024
PACK_COLS = 1024


def _rms(x, g):
    return x * lax.rsqrt(jnp.mean(x * x, axis=-1, keepdims=True) + EPS) * g


def _gelu(x):
    return 0.5 * x * (1.0 + lax.erf(x * (2.0 ** -0.5)))


def _silu(x):
    return x * jax.nn.sigmoid(x)


def _dot(a, b):
    return jnp.dot(a.astype(BF16), b.astype(BF16), preferred_element_type=F32)


def _dot_nt(a, b):
    return lax.dot_general(a.astype(BF16), b.astype(BF16), (((1,), (1,)), ((), ())), preferred_element_type=F32)


def _dot_tn(a, b):
    return lax.dot_general(a.astype(BF16), b.astype(BF16), (((0,), (0,)), ((), ())), preferred_element_type=F32)


def _split3(x):
    hi = x.astype(BF16)
    rest = x - hi.astype(F32)
    mid = rest.astype(BF16)
    return hi, mid, (rest - mid.astype(F32)).astype(BF16)


def _exact_dot(x, mask, dims, x_first=True, n_terms=3):
    terms = [lax.dot_general(*((t, mask) if x_first else (mask, t)), (dims, ((), ())), preferred_element_type=F32)
             for t in _split3(x)[:n_terms]]
    total = terms[0]
    for term in terms[1:]:
        total = total + term
    return total


def _mask_product(fwd_dims, fwd_x_first, bwd_dims, bwd_x_first, bwd_terms=3):
    @jax.custom_vjp
    def product(x, mask):
        return _exact_dot(x, mask, fwd_dims, fwd_x_first)

    def fwd(x, mask):
        return product(x, mask), mask

    def bwd(mask, g):
        return _exact_dot(g, mask, bwd_dims, bwd_x_first, bwd_terms), jnp.zeros_like(mask)

    product.defvjp(fwd, bwd)
    return product


_widen = _mask_product(((1,), (0,)), True, ((1,), (1,)), True, bwd_terms=2)
_cumsum_rows = _mask_product(((1,), (0,)), False, ((0,), (0,)), False)
_cumsum_cols = _mask_product(((0,), (0,)), True, ((1,), (1,)), False)


class _Pieces(NamedTuple):
    gathered: jax.Array
    row_off: int
    rows: int


class _Comm(NamedTuple):
    phases: object
    src: jax.Array
    dst: jax.ShapeDtypeStruct
    scratch: tuple


def _tiled(body, name, n_steps, tiled_in, full_in, big_in, tiled_out, acc_out, scratch=(), reverse=False, comm=None):
    n_t, n_f, n_b, n_to, n_a = len(tiled_in), len(full_in), len(big_in), len(tiled_out), len(acc_out)
    n_c = 1 if comm else 0

    def row(i):
        return n_steps - 1 - i if reverse else i

    in_specs, args = [], []
    for arr, br, bc, cb in tiled_in:
        if callable(cb):
            in_specs.append(pl.BlockSpec((br, bc), cb))
        else:
            in_specs.append(pl.BlockSpec((br, bc), functools.partial(lambda i, cb: (row(i), cb), cb=cb)))
        args.append(arr)
    for arr in full_in:
        in_specs.append(pl.BlockSpec(arr.shape, functools.partial(lambda i, nd: (0,) * nd, nd=arr.ndim)))
        args.append(arr)
    big_shapes, n_copies = [], 0
    for big in big_in:
        in_specs.append(pl.BlockSpec(memory_space=pl.ANY))
        if isinstance(big, _Pieces):
            args.append(big.gathered)
            big_shapes.append(((N_DEV * big.rows, PACK_COLS), big.gathered.dtype))
            n_copies += N_DEV
        else:
            args.append(big)
            big_shapes.append((big.shape, big.dtype))
            n_copies += 1
    if comm:
        in_specs.append(pl.BlockSpec(memory_space=pl.ANY))
        args.append(comm.src)
    out_specs, out_shape = [], []
    for rows, cols, dt, br in tiled_out:
        out_specs.append(pl.BlockSpec((br, cols), lambda i: (row(i), 0)))
        out_shape.append(jax.ShapeDtypeStruct((rows, cols), dt))
    for shp, dt in acc_out:
        out_specs.append(pl.BlockSpec(shp, functools.partial(lambda i, nd: (0,) * nd, nd=len(shp))))
        out_shape.append(jax.ShapeDtypeStruct(shp, dt))
    if comm:
        out_specs.append(pl.BlockSpec(memory_space=pl.ANY))
        out_shape.append(comm.dst)
    scratch_shapes = [pltpu.VMEM(shp, dt) for shp, dt in big_shapes] + list(scratch)
    if n_copies:
        scratch_shapes.append(pltpu.SemaphoreType.DMA((n_copies,)))
    if comm:
        scratch_shapes += list(comm.scratch)

    def kern(*refs):
        n_in = n_t + n_f + n_b + n_c
        ins = refs[: n_t + n_f]
        big_hbm = refs[n_t + n_f : n_t + n_f + n_b]
        outs = refs[n_in : n_in + n_to + n_a]
        rest = refs[n_in + n_to + n_a + n_c :]
        big_vmem, scr = rest[:n_b], rest[n_b:]
        if comm:
            scr, comm_scr = scr[:-len(comm.scratch)], scr[-len(comm.scratch):]
            comm_start, comm_mid, comm_finish = comm.phases(refs[n_in - 1], refs[n_in + n_to + n_a], *comm_scr)
        if n_copies:
            scr, copy_sems = scr[:-1], scr[-1]
        step = pl.program_id(0)

        @pl.when(step == 0)
        def _():
            copies = []
            for big, src, dst in zip(big_in, big_hbm, big_vmem):
                if isinstance(big, _Pieces):
                    for j in range(N_DEV):
                        copies.append((src.at[j, pl.ds(big.row_off, big.rows), :], dst.at[pl.ds(j * big.rows, big.rows), :]))
                else:
                    copies.append((src, dst))
            copies = [pltpu.make_async_copy(a, b, copy_sems.at[k]) for k, (a, b) in enumerate(copies)]
            for cp in copies:
                cp.start()
            for cp in copies:
                cp.wait()
            for acc in outs[n_to:]:
                acc[...] = jnp.zeros(acc.shape, acc.dtype)
            if comm:
                comm_start()

        body(row(step), *ins, *big_vmem, *outs, *scr)
        if comm:
            pl.when(step == (n_steps - 1) // 2)(comm_mid)
            pl.when(step == n_steps - 1)(comm_finish)

    res = pl.pallas_call(
        kern,
        out_shape=out_shape,
        grid=(n_steps,),
        in_specs=in_specs,
        out_specs=out_specs,
        scratch_shapes=scratch_shapes,
        name=name,
        compiler_params=pltpu.CompilerParams(dimension_semantics=("arbitrary",), vmem_limit_bytes=VMEM_LIMIT),
    )(*args)
    return res


FWD_CHUNKS = ((0, 1536), (1536, D_FF))
DGRAD_CHUNKS = ((0, 1024), (1024, 2048), (2048, D_FF))
FFN_TM = 256


def _ffn_fwd(h, g, wg_t, wu_t, wd, name, comm=None, mixed=None):
    T = h.shape[0]

    def ffn(x, g_ref, wg_ref, wu_ref, wd_ref, o_ref, n_ref, a_ref, b_ref, s_ref):
        n = _rms(x, g_ref[...]).astype(BF16)
        n_ref[...] = n
        f = jnp.zeros(x.shape, F32)
        for lo, hi in FWD_CHUNKS:
            a = _dot_nt(n, wg_ref[lo:hi, :])
            b = _dot_nt(n, wu_ref[lo:hi, :])
            s = (_silu(a) * b).astype(BF16)
            a_ref[:, lo:hi] = a.astype(BF16)
            b_ref[:, lo:hi] = b.astype(BF16)
            s_ref[:, lo:hi] = s
            f = f + jnp.dot(s, wd_ref[lo:hi, :], preferred_element_type=F32)
        o_ref[...] = x + 0.5 * f

    def body_plain(i, h_ref, *refs):
        ffn(h_ref[...], *refs)

    def body_mixed(i, h_ref, ya_ref, yb_ref, g_ref, wg_ref, wu_ref, wd_ref, wo_ref, o_ref, n_ref, a_ref, b_ref, s_ref, x_ref):
        x = (h_ref[...] + jnp.dot(ya_ref[...], wo_ref[:GM_WIDTH, :], preferred_element_type=F32)
             + jnp.dot(yb_ref[...], wo_ref[GM_WIDTH:, :], preferred_element_type=F32))
        x_ref[...] = x
        ffn(x, g_ref, wg_ref, wu_ref, wd_ref, o_ref, n_ref, a_ref, b_ref, s_ref)

    body = body_mixed if mixed else body_plain
    tiled_in, big_in = [(h, FFN_TM, D_MODEL, 0)], [wg_t, wu_t, wd]
    tiled_out = [(T, D_MODEL, F32, FFN_TM), (T, D_MODEL, BF16, FFN_TM), (T, D_FF, BF16, FFN_TM), (T, D_FF, BF16, FFN_TM),
                 (T, D_FF, BF16, FFN_TM)]
    if mixed:
        tiled_in += [(mixed[0], FFN_TM, GM_WIDTH, 0), (mixed[1], FFN_TM, SSM_WIDTH, 0)]
        big_in.append(mixed[2])
        tiled_out.append((T, D_MODEL, F32, FFN_TM))
    return _tiled(body, name, T // FFN_TM, tiled_in, [g], big_in, tiled_out, [], comm=comm)


def _ffn_dgrad(h, dout, a16, b16, g, wg_t, wu_t, wd, name):
    T = h.shape[0]

    def body(i, h_ref, do_ref, a_ref, b_ref, g_ref, wg_ref, wu_ref, wd_ref, dh_ref, da_ref, db_ref, dg_ref):
        dout = do_ref[...]
        _, rms_vjp = jax.vjp(_rms, h_ref[...], g_ref[...])
        dfo = (0.5 * dout).astype(BF16)
        dn = jnp.zeros(dout.shape, F32)
        for lo, hi in DGRAD_CHUNKS:
            a = a_ref[:, lo:hi].astype(F32)
            b = b_ref[:, lo:hi].astype(F32)
            sg = jax.nn.sigmoid(a)
            ds = _dot_nt(dfo, wd_ref[lo:hi, :])
            db = (ds * (a * sg)).astype(BF16)
            da = (ds * b * (sg * (1.0 + a * (1.0 - sg)))).astype(BF16)
            dn = dn + _dot(da, wg_ref[lo:hi, :]) + _dot(db, wu_ref[lo:hi, :])
            da_ref[:, lo:hi] = da
            db_ref[:, lo:hi] = db
        dx, dg = rms_vjp(dn)
        dh_ref[...] = dout + dx
        dg_ref[...] += dg

    return _tiled(body, name, T // FFN_TM,
                  [(h, FFN_TM, D_MODEL, 0), (dout, FFN_TM, D_MODEL, 0), (a16, FFN_TM, D_FF, 0), (b16, FFN_TM, D_FF, 0)],
                  [g], [wg_t, wu_t, wd],
                  [(T, D_MODEL, F32, FFN_TM), (T, D_FF, BF16, FFN_TM), (T, D_FF, BF16, FFN_TM)], [((1, D_MODEL), F32)])


FF_BN = D_FF // 2
DOWN_BN, DOWN_BK = 512, 1024
SQUARE_BN = 1024
ZXD_BN = ZXD_W // 3


def _wgrad(a, b, bn, name, scale=None, transpose_out=False, bk=2048, comm=None):
    T, M = a.shape
    N = b.shape[1]
    bk = min(bk, T)
    assert M % LANES == 0 and N % bn == 0 and T % bk == 0
    n_j, n_k = N // bn, T // bk
    n_c = 1 if comm else 0

    def kern(*refs):
        a_ref, b_ref, o_ref, acc_ref = refs[0], refs[1], refs[2 + n_c], refs[3 + 2 * n_c]
        j, k = pl.program_id(0), pl.program_id(1)
        if comm:
            comm_start, comm_mid, comm_finish = comm.phases(refs[2], refs[4], *refs[6:])
            pl.when((j == 0) & (k == 0))(comm_start)

        @pl.when(k == 0)
        def _():
            acc_ref[...] = jnp.zeros(acc_ref.shape, F32)

        bv = b_ref[...]
        if scale is not None:
            bv = bv * scale
        acc_ref[...] += _dot_tn(a_ref[...], bv)

        @pl.when(k == n_k - 1)
        def _():
            acc = acc_ref[...]
            o_ref[...] = (acc.T if transpose_out else acc).astype(BF16)

        if comm:
            pl.when((j == (n_j - 1) // 2) & (k == n_k - 1))(comm_mid)
            pl.when((j == n_j - 1) & (k == n_k - 1))(comm_finish)

    if transpose_out:
        out_shape, out_spec = (N, M), pl.BlockSpec((bn, M), lambda j, k: (j, 0))
    else:
        out_shape, out_spec = (M, N), pl.BlockSpec((M, bn), lambda j, k: (0, j))
    any_spec = pl.BlockSpec(memory_space=pl.ANY)
    res = pl.pallas_call(
        kern,
        out_shape=[jax.ShapeDtypeStruct(out_shape, BF16)] + ([comm.dst] if comm else []),
        grid=(n_j, n_k),
        in_specs=[pl.BlockSpec((bk, M), lambda j, k: (k, 0)), pl.BlockSpec((bk, bn), lambda j, k: (k, j))] + [any_spec] * n_c,
        out_specs=[out_spec] + [any_spec] * n_c,
        scratch_shapes=[pltpu.VMEM((M, bn), F32)] + (list(comm.scratch) if comm else []),
        name=name,
        compiler_params=pltpu.CompilerParams(dimension_semantics=("arbitrary", "arbitrary"), vmem_limit_bytes=VMEM_LIMIT),
    )(a, b, *([comm.src] if comm else []))
    return res if comm else res[0]


PROJ_TM = 512
PROJ_DGRAD_TM = 256
UVZ_W = 2 * GM_WIDTH + SSM_WIDTH
PROJ_KEPT = UVZ_W + LANES
Z_BLK = 2 * GM_WIDTH // SSM_WIDTH
DT_BLK = UVZ_W // LANES


def _mix_in_fwd(h, g, w_in_t, conv_w, conv_b):
    T = h.shape[0]

    def body(i, h_ref, g_ref, cw_ref, cb_ref, w_ref, p_ref, n_ref, x_ref, xc_ref, ext_ref):
        @pl.when(i == 0)
        def _():
            ext_ref[0:HALO, :] = jnp.zeros((HALO, CONV_DIM), F32)

        n = _rms(h_ref[...], g_ref[...]).astype(BF16)
        n_ref[...] = n
        proj = _dot_nt(n, w_ref[...])
        p_ref[:, :UVZ_W] = proj[:, :UVZ_W]
        p_ref[:, UVZ_W:] = jnp.concatenate(
            [proj[:, UVZ_W + CONV_DIM:], jnp.zeros((PROJ_TM, LANES - SSM_HEADS), F32)], axis=1)
        xbc = proj[:, UVZ_W:UVZ_W + CONV_DIM]
        x_ref[...] = xbc.astype(BF16)
        ext_ref[HALO:, :] = xbc
        xc_ref[...] = _conv_taps(ext_ref, cw_ref[...], cb_ref[...], PROJ_TM)
        ext_ref[0:HALO, :] = ext_ref[PROJ_TM:PROJ_TM + HALO, :]

    return _tiled(body, "mix_in_fwd", T // PROJ_TM, [(h, PROJ_TM, D_MODEL, 0)], [g, conv_w, conv_b], [w_in_t],
                  [(T, PROJ_KEPT, F32, PROJ_TM), (T, D_MODEL, BF16, PROJ_TM), (T, CONV_DIM, BF16, PROJ_TM),
                   (T, CONV_DIM, F32, PROJ_TM)], [],
                  scratch=[pltpu.VMEM((HALO + PROJ_TM, CONV_DIM), F32)])


def _mix_in_dgrad(h, dh_in, dp_uv, dp_zxd, g, w_in_t, comm=None):
    T = h.shape[0]

    def body(i, h_ref, dh_ref, duv_ref, dzxd_ref, g_ref, w_ref, o_ref, dg_ref):
        dzxd, zx_w = dzxd_ref[...], ZXD_W - LANES
        dn = (_dot(duv_ref[...], w_ref[:UV_W, :]) + _dot(dzxd[:, :zx_w], w_ref[UV_W:UV_W + zx_w, :])
              + _dot(dzxd[:, zx_w:zx_w + SSM_HEADS], w_ref[UV_W + zx_w:, :]))
        _, rms_vjp = jax.vjp(_rms, h_ref[...], g_ref[...])
        dx, dg = rms_vjp(dn)
        o_ref[...] = dh_ref[...] + dx
        dg_ref[...] += dg

    return _tiled(body, "mix_in_dgrad", T // PROJ_DGRAD_TM,
                  [(h, PROJ_DGRAD_TM, D_MODEL, 0), (dh_in, PROJ_DGRAD_TM, D_MODEL, 0), (dp_uv, PROJ_DGRAD_TM, UV_W, 0),
                   (dp_zxd, PROJ_DGRAD_TM, ZXD_W, 0)], [g], [w_in_t],
                  [(T, D_MODEL, F32, PROJ_DGRAD_TM)], [((1, D_MODEL), F32)], comm=comm)


def _out_proj_dgrad(dh, w_out):
    T = dh.shape[0]

    def body(i, dh_ref, w_ref, dya_ref, dyb_ref):
        d = dh_ref[...].astype(BF16)
        dya_ref[...] = _dot_nt(d, w_ref[:GM_WIDTH, :])
        dyb_ref[...] = _dot_nt(d, w_ref[GM_WIDTH:, :])

    rows = min(T, 2 * PROJ_TM)
    return _tiled(body, "out_proj_dgrad", T // rows, [(dh, rows, D_MODEL, 0)], [], [w_out],
                  [(T, GM_WIDTH, F32, rows), (T, SSM_WIDTH, F32, rows)], [])


def _gm_chunk(u, v, ln_g, ln_b, b_st, out_g, *w_heads):
    ug = _gelu(u)
    vg = _gelu(v)
    mu = jnp.mean(vg, axis=-1, keepdims=True)
    xc = vg - mu
    vn = xc * lax.rsqrt(jnp.mean(xc * xc, axis=-1, keepdims=True) + EPS) * ln_g + ln_b
    t_idx = lax.broadcasted_iota(jnp.int32, (CHUNK, CHUNK), 0)
    s_idx = lax.broadcasted_iota(jnp.int32, (CHUNK, CHUNK), 1)
    causal = t_idx >= s_idx
    mixed = []
    for hd in range(GM_HEADS):
        wm = jnp.where(causal, w_heads[hd], 0.0)
        cols = slice(hd * GM_HEAD_DIM, (hd + 1) * GM_HEAD_DIM)
        mixed.append(_dot(wm, vn[:, cols]) + b_st[:, hd:hd + 1])
    ya0 = ug * jnp.concatenate(mixed, axis=1)
    return _rms(ya0, out_g)


GM_FWD_CHUNKS = 4


def _gm_fwd(proj, ln_g, ln_b, w_s, b_st, out_g):
    T = proj.shape[0]

    rows = GM_FWD_CHUNKS * CHUNK

    def body(i, u_ref, v_ref, lg_ref, lb_ref, w_ref, bs_ref, og_ref, ya_ref):
        w_heads = [w_ref[hd] for hd in range(GM_HEADS)]
        for c in range(GM_FWD_CHUNKS):
            tok = pl.ds(c * CHUNK, CHUNK)
            ya = _gm_chunk(u_ref[tok, :], v_ref[tok, :], lg_ref[...], lb_ref[...], bs_ref[...], og_ref[...], *w_heads)
            ya_ref[tok, :] = ya.astype(BF16)

    return _tiled(body, "gmlp_fwd", T // rows, [(proj, rows, GM_WIDTH, 0), (proj, rows, GM_WIDTH, 1)],
                  [ln_g, ln_b, w_s, b_st, out_g], [], [(T, GM_WIDTH, BF16, rows)], [])[0]


def _gm_bwd(proj, dya, ln_g, ln_b, w_s, b_st, out_g):
    T = proj.shape[0]

    def body(i, u_ref, v_ref, dy_ref, lg_ref, lb_ref, w_ref, bs_ref, og_ref, duv_ref, dlg_ref, dlb_ref, dw_ref, dbs_ref,
             dog_ref):
        w_heads = [w_ref[hd] for hd in range(GM_HEADS)]
        _, vjp = jax.vjp(_gm_chunk, u_ref[...], v_ref[...], lg_ref[...], lb_ref[...], bs_ref[...], og_ref[...], *w_heads)
        grads = vjp(dy_ref[...])
        duv_ref[:, :GM_WIDTH] = grads[0].astype(BF16)
        duv_ref[:, GM_WIDTH:] = grads[1].astype(BF16)
        dlg_ref[...] += grads[2]
        dlb_ref[...] += grads[3]
        dbs_ref[...] += grads[4]
        dog_ref[...] += grads[5]
        for hd in range(GM_HEADS):
            dw_ref[hd] += grads[6 + hd]

    return _tiled(body, "gmlp_bwd", T // CHUNK,
                  [(proj, CHUNK, GM_WIDTH, 0), (proj, CHUNK, GM_WIDTH, 1), (dya, CHUNK, GM_WIDTH, 0)],
                  [ln_g, ln_b, w_s, b_st, out_g], [], [(T, UV_W, BF16, CHUNK)],
                  [((1, GM_WIDTH), F32), ((1, GM_WIDTH), F32), ((GM_HEADS, CHUNK, CHUNK), F32),
                   ((CHUNK, GM_HEADS), F32), ((1, GM_WIDTH), F32)])


def _ssd_chunk(xc, z, dtr, s_in, dt_bias, a_log, d_skip, norm_g):
    half = SSM_WIDTH // SSM_GROUPS
    l_idx = lax.broadcasted_iota(jnp.int32, (CHUNK, CHUNK), 0)
    s_idx = lax.broadcasted_iota(jnp.int32, (CHUNK, CHUNK), 1)
    causal = l_idx >= s_idx
    head_of_col = lax.broadcasted_iota(jnp.int32, (SSM_HEADS, SSM_WIDTH), 1) // SSM_HEAD_DIM
    expand = (head_of_col == lax.broadcasted_iota(jnp.int32, (SSM_HEADS, SSM_WIDTH), 0)).astype(BF16)

    xcs = _silu(xc)
    xs = xcs[:, :SSM_WIDTH]
    dt = jax.nn.softplus(dtr + dt_bias)
    adt = dt * (-jnp.exp(a_log))
    acs = _cumsum_rows(adt, causal.astype(BF16))
    acs_t = _cumsum_cols(adt, (l_idx <= s_idx).astype(BF16))
    tot = acs[CHUNK - 1:CHUNK, :]
    dt_w = _widen(dt, expand)
    out_decay_w = _widen(jnp.exp(acs), expand)
    state_decay_w = _widen(jnp.exp(tot - acs), expand)
    chunk_decay_w = _widen(jnp.exp(tot), expand)
    d_skip_w = _widen(d_skip, expand)
    xdt = xs * dt_w
    xdt_decayed = xdt * state_decay_w

    y_diag, y_off, states = [], [], []
    for grp in range(SSM_GROUPS):
        b0 = SSM_WIDTH + grp * SSM_STATE
        c0 = SSM_WIDTH + SSM_GROUPS * SSM_STATE + grp * SSM_STATE
        bm = xcs[:, b0:b0 + SSM_STATE].astype(BF16)
        cm = xcs[:, c0:c0 + SSM_STATE].astype(BF16)
        cb = _dot_nt(cm, bm)
        for k in range(grp * SSM_HEADS // SSM_GROUPS, (grp + 1) * SSM_HEADS // SSM_GROUPS):
            decay = jnp.exp(jnp.where(causal, acs[:, k:k + 1] - acs_t[k:k + 1, :], -jnp.inf))
            y_diag.append(_dot(cb * decay, xdt[:, k * SSM_HEAD_DIM:(k + 1) * SSM_HEAD_DIM]))
        cols = slice(grp * half, (grp + 1) * half)
        states.append(_dot_tn(bm, xdt_decayed[:, cols]))
        y_off.append(_dot(cm, s_in[:, cols]))
    y = jnp.concatenate(y_diag, axis=1) + jnp.concatenate(y_off, axis=1) * out_decay_w + xs * d_skip_w
    s_out = s_in * chunk_decay_w + jnp.concatenate(states, axis=1)
    y = y * _silu(z)
    normed = []
    for grp in range(SSM_GROUPS):
        yg = y[:, grp * half:(grp + 1) * half]
        normed.append(yg * lax.rsqrt(jnp.mean(yg * yg, axis=-1, keepdims=True) + EPS))
    return jnp.concatenate(normed, axis=1) * norm_g, s_out


def _sum_row_tiles(x):
    return x.reshape(x.shape[0] // F32_ROWS, F32_ROWS, x.shape[1]).sum(axis=0)


def _conv_taps(ext_ref, w, b, rows):
    y = b
    for k in range(SSM_CONV):
        y = y + w[k:k + 1, :] * ext_ref[pl.ds(HALO - (SSM_CONV - 1) + k, rows), :]
    return y


SSD_FWD_CHUNKS = 4


def _ssd_fwd(proj, xc, dt_bias, a_log, d_skip, norm_g, comm=None):
    T = proj.shape[0]
    n_chunks = T // CHUNK
    rows = SSD_FWD_CHUNKS * CHUNK

    def body(i, z_ref, xc_ref, dt_ref, dtb_ref, al_ref, dsk_ref, ng_ref, yb_ref, sin_ref, st_ref):
        @pl.when(i == 0)
        def _():
            st_ref[...] = jnp.zeros(st_ref.shape, F32)

        for c in range(SSD_FWD_CHUNKS):
            tok = pl.ds(c * CHUNK, CHUNK)
            s_in = st_ref[...]
            yb, s_out = _ssd_chunk(xc_ref[tok, :], z_ref[tok, :], dt_ref[tok, 0:SSM_HEADS], s_in, dtb_ref[...], al_ref[...],
                                   dsk_ref[...], ng_ref[...])
            yb_ref[tok, :] = yb.astype(BF16)
            sin_ref[pl.ds(c * SSM_STATE, SSM_STATE), :] = s_in
            st_ref[...] = s_out

    return _tiled(body, "ssd_fwd", T // rows,
                  [(proj, rows, SSM_WIDTH, Z_BLK), (xc, rows, CONV_DIM, 0), (proj, rows, LANES, DT_BLK)],
                  [dt_bias, a_log, d_skip, norm_g], [],
                  [(T, SSM_WIDTH, BF16, rows), (n_chunks * SSM_STATE, SSM_WIDTH, F32, SSD_FWD_CHUNKS * SSM_STATE)], [],
                  scratch=[pltpu.VMEM((SSM_STATE, SSM_WIDTH), F32)], comm=comm)


def _ssd_bwd(proj, x16, xc, dyb, s_all, conv_w, dt_bias, a_log, d_skip, norm_g, comm=None):
    T = proj.shape[0]
    n_chunks = T // CHUNK

    def body(i, z_ref, x_ref, xc_ref, dt_ref, dy_ref, sin_ref, cw_ref, dtb_ref, al_ref, dsk_ref, ng_ref,
             dzxd_ref, dcw_ref, dcb_ref, ddtb_ref, dal_ref, ddsk_ref, dng_ref, dext_ref, dst_ref, cw_acc, cb_acc):
        @pl.when(i == n_chunks - 1)
        def _():
            dext_ref[CHUNK:, :] = jnp.zeros((HALO, CONV_DIM), F32)
            dst_ref[...] = jnp.zeros(dst_ref.shape, F32)
            cw_acc[...] = jnp.zeros(cw_acc.shape, F32)
            cb_acc[...] = jnp.zeros(cb_acc.shape, F32)

        _, vjp = jax.vjp(_ssd_chunk, xc_ref[...], z_ref[...], dt_ref[:, 0:SSM_HEADS], sin_ref[...], dtb_ref[...], al_ref[...],
                         dsk_ref[...], ng_ref[...])
        dxc, dz, ddtr, ds_in, ddtb, dal, ddsk, dng = vjp((dy_ref[...], dst_ref[...]))
        dst_ref[...] = ds_in
        ddtb_ref[...] += ddtb
        dal_ref[...] += dal
        ddsk_ref[...] += ddsk
        dng_ref[...] += dng
        dext_ref[0:CHUNK, :] = dxc
        cw = cw_ref[...]
        x = x_ref[...].astype(F32)
        dx = jnp.zeros((CHUNK, CONV_DIM), F32)
        for k in range(SSM_CONV):
            shifted = dext_ref[pl.ds(SSM_CONV - 1 - k, CHUNK), :]
            dx = dx + cw[k:k + 1, :] * shifted
            cw_acc[k] += _sum_row_tiles(shifted * x)
        cb_acc[...] += _sum_row_tiles(dxc)

        @pl.when(i == 0)
        def _():
            dcw_ref[...] = jnp.sum(cw_acc[...], axis=1)
            dcb_ref[...] = jnp.sum(cb_acc[...], axis=0, keepdims=True)

        dext_ref[CHUNK:, :] = dext_ref[0:HALO, :]
        dzxd_ref[:, 0:SSM_WIDTH] = dz.astype(BF16)
        dzxd_ref[:, SSM_WIDTH:SSM_WIDTH + CONV_DIM] = dx.astype(BF16)
        dzxd_ref[:, SSM_WIDTH + CONV_DIM:] = jnp.concatenate(
            [ddtr, jnp.zeros((CHUNK, LANES - SSM_HEADS), F32)], axis=1).astype(BF16)

    return _tiled(body, "ssd_bwd", n_chunks,
                  [(proj, CHUNK, SSM_WIDTH, Z_BLK), (x16, CHUNK, CONV_DIM, 0), (xc, CHUNK, CONV_DIM, 0),
                   (proj, CHUNK, LANES, DT_BLK), (dyb, CHUNK, SSM_WIDTH, 0), (s_all, SSM_STATE, SSM_WIDTH, 0)],
                  [conv_w, dt_bias, a_log, d_skip, norm_g], [],
                  [(T, ZXD_W, BF16, CHUNK)],
                  [((SSM_CONV, CONV_DIM), F32), ((1, CONV_DIM), F32), ((1, SSM_HEADS), F32), ((1, SSM_HEADS), F32),
                   ((1, SSM_HEADS), F32), ((1, SSM_WIDTH), F32)],
                  scratch=[pltpu.VMEM((CHUNK + HALO, CONV_DIM), F32), pltpu.VMEM((SSM_STATE, SSM_WIDTH), F32),
                           pltpu.VMEM((SSM_CONV, F32_ROWS, CONV_DIM), F32), pltpu.VMEM((F32_ROWS, CONV_DIM), F32)],
                  reverse=True, comm=comm)


TAIL_TM = 512


def _tail(h, p, target, ple_norm, w_gate, b_gate, w_proj_t, final_norm):
    T = h.shape[0]

    def head(x, pre, pp, b_g, f_norm, tgt):
        gate = jax.nn.sigmoid(pre + b_g)
        out = _rms(x + gate * pp, f_norm)
        err = out - tgt
        return 0.5 * jnp.sum(jnp.mean(err * err, axis=-1, keepdims=True), axis=0, keepdims=True)

    def body(i, h_ref, p_ref, t_ref, pn_ref, bg_ref, fn_ref, wg_ref, wp_ref, dh_ref, loss_ref, dwg_ref, dwp_ref, dpn_ref,
             dbg_ref, dfn_ref):
        x = h_ref[...]
        n4f, n_vjp = jax.vjp(_rms, x, pn_ref[...])
        n4 = n4f.astype(BF16)
        pre = jnp.dot(n4, wg_ref[...], preferred_element_type=F32)
        p16 = p_ref[...].astype(BF16)
        pp = _dot_nt(p16, wp_ref[...])
        loss, h_vjp = jax.vjp(functools.partial(head, tgt=t_ref[...]), x, pre, pp, bg_ref[...], fn_ref[...])
        dx, dpre, dpp, dbg, dfn = h_vjp(jnp.ones((1, 1), F32))
        dpre16 = dpre.astype(BF16)
        dn4 = _dot_nt(dpre16, wg_ref[...])
        dx2, dpn = n_vjp(dn4)
        dh_ref[...] = dx + dx2
        loss_ref[...] += loss
        dwg_ref[...] += _dot_tn(n4, dpre16)
        dwp_ref[...] += _dot_tn(p16, dpp)
        dpn_ref[...] += dpn
        dbg_ref[...] += dbg
        dfn_ref[...] += dfn

    return _tiled(body, "tail", T // TAIL_TM,
                  [(h, TAIL_TM, D_MODEL, 0), (p, TAIL_TM, D_PLE, 0), (target, TAIL_TM, D_MODEL, 0)],
                  [ple_norm, b_gate, final_norm], [w_gate, w_proj_t],
                  [(T, D_MODEL, F32, TAIL_TM)],
                  [((1, 1), F32), ((D_MODEL, D_MODEL), F32), ((D_PLE, D_MODEL), F32), ((1, D_MODEL), F32),
                   ((1, D_MODEL), F32), ((1, D_MODEL), F32)])


def _gather_phases(x_ref, out_ref, send_sems, recv_sems, local_sem):
    mx, my, mc = lax.axis_index("x"), lax.axis_index("y"), lax.axis_index("c")
    me, sibling = (mx, my, mc), (mx, my, 1 - mc)
    chips = [(1 - mx, my), (mx, 1 - my), (1 - mx, 1 - my)]

    def rows(px, py, pc):
        return out_ref.at[4 * px + 2 * py + pc]

    def copy(k, block, to, src=None):
        return pltpu.make_async_remote_copy(
            src_ref=rows(*block) if src is None else src, dst_ref=rows(*block),
            send_sem=send_sems.at[k], recv_sem=recv_sems.at[k], device_id=to, device_id_type=MESH)

    mine = pltpu.make_async_copy(x_ref, rows(*me), local_sem)
    first = [copy(0, me, sibling, src=x_ref)] + [copy(1 + j, me, (*chip, mc), src=x_ref) for j, chip in enumerate(chips)]
    passed = [copy(4 + j, (*chip, mc), sibling) for j, chip in enumerate(chips)]

    def start():
        mine.start()
        for cp in first:
            cp.start()

    def mid():
        for j, chip in enumerate(chips):
            copy(1 + j, (*chip, mc), me).wait_recv()
            passed[j].start()

    def finish():
        copy(0, sibling, me).wait_recv()
        for j, chip in enumerate(chips):
            copy(4 + j, (*chip, 1 - mc), me).wait_recv()
        for cp in first + passed:
            cp.wait_send()
        mine.wait()

    return start, mid, finish


def _exchange_phases(x_ref, out_ref, send_sems, recv_sems, local_sem):
    mx, my, mc = lax.axis_index("x"), lax.axis_index("y"), lax.axis_index("c")
    me = 4 * mx + 2 * my + mc
    mine = pltpu.make_async_copy(x_ref.at[me], out_ref.at[me], local_sem)
    copies = []
    for k in range(1, N_DEV):
        px = 1 - mx if k & 4 else mx
        py = 1 - my if k & 2 else my
        pc = 1 - mc if k & 1 else mc
        copies.append(pltpu.make_async_remote_copy(
            src_ref=x_ref.at[4 * px + 2 * py + pc], dst_ref=out_ref.at[me], send_sem=send_sems.at[k - 1],
            recv_sem=recv_sems.at[k - 1], device_id=(px, py, pc), device_id_type=MESH))

    def start():
        mine.start()
        for cp in copies:
            cp.start()

    def finish():
        for cp in copies:
            cp.wait_recv()
        for cp in copies:
            cp.wait_send()
        mine.wait()

    return start, lambda: None, finish


def _chip_exchange_phases(x_ref, out_ref, mine, recv, sums, load_sems, pair_send, pair_recv, chip_send, chip_recv, out_sem):
    mx, my, mc = lax.axis_index("x"), lax.axis_index("y"), lax.axis_index("c")
    my_chip = 2 * mx + my
    load = [pltpu.make_async_copy(x_ref.at[2 * q + mc], mine.at[q], load_sems.at[q]) for q in range(N_CHIPS)]
    to_sibling = [pltpu.make_async_remote_copy(
        src_ref=x_ref.at[2 * q + 1 - mc], dst_ref=recv.at[q], send_sem=pair_send.at[q], recv_sem=pair_recv.at[q],
        device_id=(mx, my, 1 - mc), device_id_type=MESH) for q in range(N_CHIPS)]
    to_chips = []
    for k in range(1, N_CHIPS):
        px = 1 - mx if k & 2 else mx
        py = 1 - my if k & 1 else my
        to_chips.append(pltpu.make_async_remote_copy(
            src_ref=sums.at[2 * px + py], dst_ref=out_ref.at[my_chip], send_sem=chip_send.at[k - 1],
            recv_sem=chip_recv.at[k - 1], device_id=(px, py, mc), device_id_type=MESH))
    keep = pltpu.make_async_copy(sums.at[my_chip], out_ref.at[my_chip], out_sem)

    def start():
        for cp in load + to_sibling:
            cp.start()

    def mid():
        for cp in load:
            cp.wait()
        for cp in to_sibling:
            cp.wait_recv()
        for q in range(N_CHIPS):
            sums[q] = (mine[q].astype(F32) + recv[q].astype(F32)).astype(sums.dtype)
        for cp in to_chips + [keep]:
            cp.start()

    def finish():
        for cp in to_chips:
            cp.wait_recv()
        for cp in to_chips + to_sibling:
            cp.wait_send()
        keep.wait()

    return start, mid, finish


FLAT_SCRATCH = (pltpu.SemaphoreType.DMA((N_DEV - 1,)), pltpu.SemaphoreType.DMA((N_DEV - 1,)), pltpu.SemaphoreType.DMA)


def _gather_comm(x):
    return _Comm(_gather_phases, x, jax.ShapeDtypeStruct((N_DEV,) + x.shape, x.dtype), FLAT_SCRATCH)


def _exchange_comm(x):
    return _Comm(_exchange_phases, x, jax.ShapeDtypeStruct(x.shape, x.dtype), FLAT_SCRATCH)


def _chip_exchange_comm(x):
    stage = pltpu.VMEM((N_CHIPS,) + x.shape[1:], x.dtype)
    sems = [pltpu.SemaphoreType.DMA((n,)) for n in (N_CHIPS, N_CHIPS, N_CHIPS, N_CHIPS - 1, N_CHIPS - 1)]
    return _Comm(_chip_exchange_phases, x, jax.ShapeDtypeStruct((N_CHIPS,) + x.shape[1:], x.dtype),
                 (stage, stage, stage, *sems, pltpu.SemaphoreType.DMA))


def _comm_alone(comms, name):
    n = len(comms)

    def body(*refs):
        phases, first = [], 2 * n
        for k, comm in enumerate(comms):
            phases.append(comm.phases(refs[k], refs[n + k], *refs[first:first + len(comm.scratch)]))
            first += len(comm.scratch)
        for step in range(3):
            for phase in phases:
                phase[step]()

    any_spec = pl.BlockSpec(memory_space=pl.ANY)
    return pl.pallas_call(
        body,
        out_shape=[comm.dst for comm in comms],
        in_specs=[any_spec] * n,
        out_specs=[any_spec] * n,
        scratch_shapes=[shape for comm in comms for shape in comm.scratch],
        name=name,
        compiler_params=pltpu.CompilerParams(vmem_limit_bytes=VMEM_LIMIT),
    )(*[comm.src for comm in comms])


def _sum_parts(p_ref):
    g = p_ref[0].astype(F32)
    for j in range(1, p_ref.shape[0]):
        g = g + p_ref[j].astype(F32)
    return g


def _adamw_store(g, w_ref, m_ref, v_ref, g_ref, d_ref, nm_ref, nv_ref):
    m_new = ADAM_B1 * m_ref[...] + (1.0 - ADAM_B1) * g
    v_new = ADAM_B2 * v_ref[...] + (1.0 - ADAM_B2) * jnp.square(g)
    m_hat = m_new / (1.0 - ADAM_B1 ** ADAM_STEP)
    v_hat = v_new / (1.0 - ADAM_B2 ** ADAM_STEP)
    g_ref[...] = g
    d_ref[...] = -ADAM_LR * (m_hat / (jnp.sqrt(v_hat) + ADAM_EPS) + ADAM_WD * w_ref[...])
    nm_ref[...] = m_new
    nv_ref[...] = v_new


def _adamw_shard(parts, off, w, m, v, name, n_tiles):
    _, rows, c = w.shape
    assert c == PACK_COLS
    by_rows = rows % BF16_ROWS == 0
    if by_rows:
        tr = rows // n_tiles
        window = (parts.shape[0], tr, PACK_COLS)
        spec = pl.BlockSpec((None, tr, PACK_COLS), lambda i: (0, i, 0))
    else:
        padded, tc = -(-rows // BF16_ROWS) * BF16_ROWS, PACK_COLS // n_tiles
        window = (parts.shape[0], padded, tc)
        spec = pl.BlockSpec((None, rows, tc), lambda i: (0, 0, i))
    blocked = off % (tr if by_rows else padded) == 0

    def update(p_ref, refs):
        g = _sum_parts(p_ref)
        if not by_rows:
            keep = lax.broadcasted_iota(jnp.int32, (rows, padded), 0) == lax.broadcasted_iota(jnp.int32, (rows, padded), 1)
            g = _exact_dot(g, keep.astype(BF16), ((1,), (0,)), x_first=False)
        _adamw_store(g, *refs)

    def kern_blocked(p_ref, *refs):
        update(p_ref, refs)

    def kern_copied(p_hbm, *refs):
        buf, sem = refs[-2:]
        i = pl.program_id(0)
        if by_rows:
            src = p_hbm.at[:, pl.ds(pl.multiple_of(off + i * tr, BF16_ROWS), tr), :]
        else:
            src = p_hbm.at[:, pl.ds(off, padded), pl.ds(pl.multiple_of(i * tc, LANES), tc)]
        cp = pltpu.make_async_copy(src, buf, sem)
        cp.start()
        cp.wait()
        update(buf, refs[:-2])

    if blocked:
        index = (lambda i: (0, off // tr + i, 0)) if by_rows else (lambda i: (0, off // padded, i))
        parts_spec, scratch = pl.BlockSpec(window, index), []
    else:
        parts_spec, scratch = pl.BlockSpec(memory_space=pl.ANY), [pltpu.VMEM(window, parts.dtype), pltpu.SemaphoreType.DMA]
    return pl.pallas_call(
        kern_blocked if blocked else kern_copied,
        out_shape=[jax.ShapeDtypeStruct(w.shape, F32)] * 4,
        grid=(n_tiles,),
        in_specs=[parts_spec, spec, spec, spec],
        out_specs=[spec] * 4,
        scratch_shapes=scratch,
        name=name,
        compiler_params=pltpu.CompilerParams(dimension_semantics=("arbitrary",), vmem_limit_bytes=VMEM_LIMIT),
    )(parts, w, m, v)


def _sum_adamw(parts, w, m, v, tr, name):
    _, R, C = parts.shape

    def kern(p_ref, w_ref, m_ref, v_ref, g_ref, d_ref, nm_ref, nv_ref):
        _adamw_store(_sum_parts(p_ref), w_ref, m_ref, v_ref, g_ref, d_ref, nm_ref, nv_ref)

    row_spec = pl.BlockSpec((tr, C), lambda i: (i, 0))
    return pl.pallas_call(
        kern,
        out_shape=[jax.ShapeDtypeStruct((R, C), F32)] * 4,
        grid=(R // tr,),
        in_specs=[pl.BlockSpec((parts.shape[0], tr, C), lambda i: (0, i, 0)), row_spec, row_spec, row_spec],
        out_specs=[row_spec] * 4,
        name=name,
        compiler_params=pltpu.CompilerParams(dimension_semantics=("arbitrary",), vmem_limit_bytes=VMEM_LIMIT),
    )(parts, w, m, v)


FF_SHARD = D_FF // N_DEV
CONV_SHARD = (SSM_CONV, CONV_DIM // N_DEV)
SHARDS = {"ffn1_w_gate": ((D_MODEL, FF_SHARD), True), "ffn1_w_up": ((D_MODEL, FF_SHARD), True),
          "ffn1_w_down": ((FF_SHARD, D_MODEL), False),
          "ffn2_w_gate": ((D_MODEL, FF_SHARD), True), "ffn2_w_up": ((D_MODEL, FF_SHARD), True),
          "ffn2_w_down": ((FF_SHARD, D_MODEL), False),
          "w_out": ((2 * D_MODEL // N_DEV, D_MODEL), False), "ple_w_gate": ((D_MODEL // N_DEV, D_MODEL), False),
          "w_in": ((D_MODEL, IN_PROJ // N_DEV), True), "ple_w_proj": ((D_PLE, D_MODEL // N_DEV), True),
          "conv_w": (CONV_SHARD, True),
          "conv_w_mid": (CONV_SHARD, True), "conv_w_low": (CONV_SHARD, True)}
BIG = tuple(name for name in SHARDS if not name.startswith("conv_w_"))
SMALL = ("ffn1_norm", "mix_norm", "gm_ln_g", "gm_ln_b", "gm_w_s", "gm_b_s", "gm_out_norm", "conv_b", "dt_bias", "a_log",
         "d_skip", "ssm_norm", "ffn2_norm", "ple_norm", "ple_b_gate", "final_norm")
SMALL_ROWS = 144


def _piece_rows(name):
    shape = SHARDS[name][0]
    return -(-(shape[0] * shape[1]) // PACK_COLS)


def _pad_cols(flat, name):
    pad = _piece_rows(name) * PACK_COLS - flat.shape[-1]
    return flat if pad == 0 else jnp.pad(flat, [(0, 0)] * (flat.ndim - 1) + [(0, pad)])


class _Pack:
    def __init__(self, names, tile_rows):
        self.names, self.tile_rows, self.offsets, off = names, tile_rows, {}, 0
        for name in names:
            self.offsets[name] = off
            off += _piece_rows(name)
        self.rows = -(-off // tile_rows) * tile_rows

    def pack_local(self, vals):
        parts = []
        for name in self.names:
            val = vals[name]
            parts.append(_pad_cols((val.T if SHARDS[name][1] else val).reshape(-1), name))
        flat = jnp.concatenate(parts)
        return jnp.pad(flat, (0, self.rows * PACK_COLS - flat.shape[0])).reshape(self.rows, PACK_COLS)

    def pack_owner_major(self, grads):
        parts, rows = [], 0
        for name in self.names:
            grad, piece_rows = grads[name].astype(BF16), _piece_rows(name)
            if grad.shape != (N_DEV * piece_rows, PACK_COLS):
                grad = _pad_cols(grad.reshape(N_DEV, -1), name)
            parts.append(grad.reshape(N_DEV, piece_rows, PACK_COLS))
            rows += piece_rows
        if rows < self.rows:
            parts.append(jnp.zeros((N_DEV, self.rows - rows, PACK_COLS), BF16))
        return parts[0] if len(parts) == 1 else jnp.concatenate(parts, axis=1)

    def gathered_piece(self, gathered, name):
        shape = SHARDS[name][0]
        rows = gathered[:, self.offsets[name]:self.offsets[name] + _piece_rows(name), :]
        return rows.reshape(gathered.shape[0], -1)[:, :shape[0] * shape[1]]

    def pieces(self, gathered, name):
        return _Pieces(gathered, self.offsets[name], _piece_rows(name))


GATHER_FFN1 = _Pack(("ffn1_w_gate", "ffn1_w_up", "ffn1_w_down"), BF16_ROWS)
GATHER_MIX = _Pack(("w_out", "ple_w_gate", "w_in", "ple_w_proj", "conv_w", "conv_w_mid", "conv_w_low"), BF16_ROWS)
GATHER_FFN2 = _Pack(("ffn2_w_gate", "ffn2_w_up", "ffn2_w_down"), BF16_ROWS)
SCATTER_LATE = _Pack(("ffn2_w_gate", "ffn2_w_up", "ffn2_w_down", "w_out", "ple_w_gate", "ple_w_proj"), BF16_ROWS)
SCATTER_IN = _Pack(("w_in", "conv_w"), BF16_ROWS)
SCATTER_GATE = _Pack(("ffn1_w_gate",), BF16_ROWS)
SCATTER_UP = _Pack(("ffn1_w_up",), BF16_ROWS)
SCATTER_DOWN = _Pack(("ffn1_w_down",), BF16_ROWS)


def _pack_small(vals, behind=()):
    flat = jnp.concatenate([vals[name].reshape(-1).astype(F32) for name in SMALL] + [b.reshape(-1) for b in behind])
    return jnp.pad(flat, (0, SMALL_ROWS * PACK_COLS - flat.shape[0])).reshape(SMALL_ROWS, PACK_COLS)


def _unpack_small(packed, shapes):
    out, off = {}, 0
    flat = packed.reshape(-1)
    for name in SMALL:
        n = 1
        for s in shapes[name]:
            n *= s
        out[name] = flat[off:off + n].reshape(shapes[name])
        off += n
    return out


WEIGHTS = ("ffn1_norm", "ffn1_w_gate", "ffn1_w_up", "ffn1_w_down", "mix_norm", "w_in", "gm_ln_g", "gm_ln_b", "gm_w_s",
           "gm_b_s", "gm_out_norm", "conv_w", "conv_b", "dt_bias", "a_log", "d_skip", "ssm_norm", "w_out", "ffn2_norm",
           "ffn2_w_gate", "ffn2_w_up", "ffn2_w_down", "ple_norm", "ple_w_gate", "ple_b_gate", "ple_w_proj", "final_norm")


def _step(x, p, target, w, m, v):
    local = lambda d: {name: d[name][0] for name in BIG}

    shards = {name: val.astype(BF16) for name, val in local(w).items()}
    conv_high = lax.reduce_precision(w["conv_w"][0], 8, 7)
    conv_mid = lax.reduce_precision(w["conv_w"][0] - conv_high, 8, 7)
    shards["conv_w"] = conv_high.astype(BF16)
    shards["conv_w_mid"] = conv_mid.astype(BF16)
    shards["conv_w_low"] = (w["conv_w"][0] - conv_high - conv_mid).astype(BF16)
    g_ffn1 = _comm_alone([_gather_comm(GATHER_FFN1.pack_local(shards))], "gather_ffn1")[0]

    row = lambda name: w[name].reshape(1, -1)
    gm_w_s = w["gm_w_s"][0]
    gm_b_st = jnp.transpose(w["gm_b_s"][0])
    ffn1 = (row("ffn1_norm"),) + tuple(GATHER_FFN1.pieces(g_ffn1, name) for name in GATHER_FFN1.names)
    gm = (row("gm_ln_g"), row("gm_ln_b"), gm_w_s, gm_b_st, row("gm_out_norm"))

    h1, n1, a1, b1, s1, g_mix = _ffn_fwd(x, *ffn1, "ffn1_fwd", comm=_gather_comm(GATHER_MIX.pack_local(shards)))
    w_in_t = GATHER_MIX.gathered_piece(g_mix, "w_in").reshape(IN_PROJ, D_MODEL)
    w_proj_t = GATHER_MIX.gathered_piece(g_mix, "ple_w_proj").reshape(D_MODEL, D_PLE)
    conv_w = sum(GATHER_MIX.gathered_piece(g_mix, name).astype(F32) for name in ("conv_w", "conv_w_mid", "conv_w_low"))
    conv_w = conv_w.reshape(CONV_DIM, SSM_CONV).T
    ssd = (row("dt_bias"), row("a_log"), row("d_skip"), row("ssm_norm"))
    w_out = GATHER_MIX.pieces(g_mix, "w_out")

    proj, n2, x16, xc = _mix_in_fwd(h1, row("mix_norm"), w_in_t, conv_w, row("conv_b"))
    ya = _gm_fwd(proj, *gm)
    yb, s_all, g_ffn2 = _ssd_fwd(proj, xc, *ssd, comm=_gather_comm(GATHER_FFN2.pack_local(shards)))
    ffn2 = (row("ffn2_norm"),) + tuple(GATHER_FFN2.pieces(g_ffn2, name) for name in GATHER_FFN2.names)
    h3, n3, a3, b3, s3, h2 = _ffn_fwd(h1, *ffn2, "ffn2_fwd", mixed=(ya, yb, w_out))

    g, gp = {}, {}
    dh3, loss, gp["ple_w_gate"], d_w_proj, g["ple_norm"], g["ple_b_gate"], g["final_norm"] = _tail(
        h3, p, target, row("ple_norm"), GATHER_MIX.pieces(g_mix, "ple_w_gate"), row("ple_b_gate"), w_proj_t,
        row("final_norm"))
    gp["ple_w_proj"] = d_w_proj.T

    dh2, da3, db3, g["ffn2_norm"] = _ffn_dgrad(h2, dh3, a3, b3, *ffn2, "ffn2_dgrad")
    gp["ffn2_w_gate"] = _wgrad(n3, da3, FF_BN, "ffn2_wgrad_gate", transpose_out=True)
    gp["ffn2_w_up"] = _wgrad(n3, db3, FF_BN, "ffn2_wgrad_up", transpose_out=True)
    gp["ffn2_w_down"] = _wgrad(s3, dh3, DOWN_BN, "ffn2_wgrad_down", scale=0.5, bk=DOWN_BK)

    dya, dyb = _out_proj_dgrad(dh2, w_out)
    gp["w_out"] = jnp.concatenate([_wgrad(ya, dh2, SQUARE_BN, "w_out_wgrad_a"), _wgrad(yb, dh2, SQUARE_BN, "w_out_wgrad_b")], axis=0)

    dp_zxd, d_conv_w, g["conv_b"], g["dt_bias"], g["a_log"], g["d_skip"], g["ssm_norm"], parts_late = _ssd_bwd(
        proj, x16, xc, dyb, s_all, conv_w, *ssd, comm=_exchange_comm(SCATTER_LATE.pack_owner_major(gp)))
    gp["conv_w"] = d_conv_w.T
    dp_uv, g["gm_ln_g"], g["gm_ln_b"], g["gm_w_s"], dbst, g["gm_out_norm"] = _gm_bwd(proj, dya, *gm)
    g["gm_b_s"] = jnp.transpose(dbst)

    parts = {}
    gp["w_in"] = jnp.concatenate([_wgrad(n2, dp_uv, SQUARE_BN, "w_in_wgrad_uv", transpose_out=True),
                                  _wgrad(n2, dp_zxd, ZXD_BN, "w_in_wgrad_zxd", transpose_out=True)], axis=0)[:IN_PROJ]
    dh1, g["mix_norm"], parts[SCATTER_IN] = _mix_in_dgrad(h1, dh2, dp_uv, dp_zxd, row("mix_norm"), w_in_t,
                                                          comm=_chip_exchange_comm(SCATTER_IN.pack_owner_major(gp)))

    dx, da1, db1, g["ffn1_norm"] = _ffn_dgrad(x, dh1, a1, b1, *ffn1, "ffn1_dgrad")
    gp["ffn1_w_gate"], small_parts = _wgrad(n1, da1, FF_BN, "ffn1_wgrad_gate", transpose_out=True,
                                            comm=_gather_comm(_pack_small(g, behind=[loss])))
    gp["ffn1_w_up"], parts[SCATTER_GATE] = _wgrad(n1, db1, FF_BN, "ffn1_wgrad_up", transpose_out=True,
                                                  comm=_chip_exchange_comm(SCATTER_GATE.pack_owner_major(gp)))
    gp["ffn1_w_down"], parts[SCATTER_UP] = _wgrad(s1, dh1, DOWN_BN, "ffn1_wgrad_down", scale=0.5, bk=DOWN_BK,
                                                  comm=_chip_exchange_comm(SCATTER_UP.pack_owner_major(gp)))
    parts[SCATTER_DOWN] = _comm_alone([_chip_exchange_comm(SCATTER_DOWN.pack_owner_major(gp))], "scatter_ffn1_down")[0]
    parts[SCATTER_LATE] = parts_late

    res_big = {}
    for pack, pack_parts in parts.items():
        for name in pack.names:
            shape, transposed = SHARDS[name]
            if name in ("ple_w_proj", "conv_w"):
                nat = pack.gathered_piece(pack_parts, name).reshape((pack_parts.shape[0],) + shape[::-1])
                res_big[name] = _sum_adamw(jnp.transpose(nat, (0, 2, 1)), w[name][0], m[name][0], v[name][0], shape[0],
                                           "adamw_" + name)
            else:
                flip = (lambda a: jnp.transpose(a, (0, 2, 1))) if transposed else (lambda a: a)
                res = _adamw_shard(pack_parts, pack.offsets[name], flip(w[name]), flip(m[name]), flip(v[name]),
                                   "adamw_" + name, n_tiles=4 if name == "w_in" else 2)
                res_big[name] = [flip(r) for r in res]

    small_shapes = {name: w[name].shape for name in SMALL}
    res_small = _sum_adamw(small_parts, _pack_small(w), _pack_small(m), _pack_small(v), SMALL_ROWS, "adamw_small")
    loss = res_small[0].reshape(-1)[sum(w[name].size for name in SMALL)]
    res_small = [_unpack_small(r, small_shapes) for r in res_small]

    outs = []
    for k in range(4):
        for name in WEIGHTS:
            if name in res_small[k]:
                outs.append(res_small[k][name])
            else:
                outs.append(res_big[name][k].reshape(w[name].shape))
    return loss, dx, outs


def kernel(x, p, ffn1_norm, ffn1_w_gate, ffn1_w_up, ffn1_w_down, mix_norm, w_in, gm_ln_g, gm_ln_b, gm_w_s, gm_b_s, gm_out_norm, conv_w, conv_b, dt_bias, a_log, d_skip, ssm_norm, w_out, ffn2_norm, ffn2_w_gate, ffn2_w_up, ffn2_w_down, ple_norm, ple_w_gate, ple_b_gate, ple_w_proj, final_norm, loss_target, m_ffn1_norm, m_ffn1_w_gate, m_ffn1_w_up, m_ffn1_w_down, m_mix_norm, m_w_in, m_gm_ln_g, m_gm_ln_b, m_gm_w_s, m_gm_b_s, m_gm_out_norm, m_conv_w, m_conv_b, m_dt_bias, m_a_log, m_d_skip, m_ssm_norm, m_w_out, m_ffn2_norm, m_ffn2_w_gate, m_ffn2_w_up, m_ffn2_w_down, m_ple_norm, m_ple_w_gate, m_ple_b_gate, m_ple_w_proj, m_final_norm, v_ffn1_norm, v_ffn1_w_gate, v_ffn1_w_up, v_ffn1_w_down, v_mix_norm, v_w_in, v_gm_ln_g, v_gm_ln_b, v_gm_w_s, v_gm_b_s, v_gm_out_norm, v_conv_w, v_conv_b, v_dt_bias, v_a_log, v_d_skip, v_ssm_norm, v_w_out, v_ffn2_norm, v_ffn2_w_gate, v_ffn2_w_up, v_ffn2_w_down, v_ple_norm, v_ple_w_gate, v_ple_b_gate, v_ple_w_proj, v_final_norm):
    args = locals()
    w = {name: args[name] for name in WEIGHTS}
    m = {name: args["m_" + name] for name in WEIGHTS}
    v = {name: args["v_" + name] for name in WEIGHTS}
    loss, dx, outs = _step(x[0], p[0, 0], loss_target[0], w, m, v)
    return (loss, dx[None], *outs)
```

```python
import functools
from typing import NamedTuple

import jax
import jax.numpy as jnp
from jax import lax
from jax.experimental import pallas as pl
from jax.experimental.pallas import tpu as pltpu

F32 = jnp.float32
BF16 = jnp.bfloat16
MESH = pl.DeviceIdType.MESH
N_DEV = 8
N_CHIPS = 4

D_MODEL = 1024
D_FF = 2816
D_PLE = 256
GM_WIDTH = 1024
GM_HEADS = 8
GM_HEAD_DIM = 128
CHUNK = 128
SSM_WIDTH = 1024
SSM_HEADS = 16
SSM_HEAD_DIM = 64
SSM_GROUPS = 2
SSM_STATE = 128
SSM_CONV = 4
CONV_DIM = SSM_WIDTH + 2 * SSM_GROUPS * SSM_STATE
IN_PROJ = 2 * GM_WIDTH + SSM_WIDTH + CONV_DIM + SSM_HEADS
LANES = 128
BF16_ROWS = 16
F32_ROWS = 8
IN_PROJ_PAD = IN_PROJ - SSM_HEADS + LANES
UV_W = 2 * GM_WIDTH
ZXD_W = IN_PROJ_PAD - UV_W
HALO = 8
EPS = 1e-6

ADAM_LR = 0.001
ADAM_B1 = 0.9
ADAM_B2 = 0.999
ADAM_EPS = 1e-08
ADAM_WD = 0.01
ADAM_STEP = 10

VMEM_LIMIT = 56 * 1024 * 1024
PACK_COLS = 1024


def _rms(x, g):
    return x * lax.rsqrt(jnp.mean(x * x, axis=-1, keepdims=True) + EPS) * g


def _gelu(x):
    return 0.5 * x * (1.0 + lax.erf(x * (2.0 ** -0.5)))


def _silu(x):
    return x * jax.nn.sigmoid(x)


def _dot(a, b):
    return jnp.dot(a.astype(BF16), b.astype(BF16), preferred_element_type=F32)


def _dot_nt(a, b):
    return lax.dot_general(a.astype(BF16), b.astype(BF16), (((1,), (1,)), ((), ())), preferred_element_type=F32)


def _dot_tn(a, b):
    return lax.dot_general(a.astype(BF16), b.astype(BF16), (((0,), (0,)), ((), ())), preferred_element_type=F32)


def _split3(x):
    hi = x.astype(BF16)
    rest = x - hi.astype(F32)
    mid = rest.astype(BF16)
    return hi, mid, (rest - mid.astype(F32)).astype(BF16)


def _exact_dot(x, mask, dims, x_first=True, n_terms=3):
    terms = [lax.dot_general(*((t, mask) if x_first else (mask, t)), (dims, ((), ())), preferred_element_type=F32)
             for t in _split3(x)[:n_terms]]
    total = terms[0]
    for term in terms[1:]:
        total = total + term
    return total


def _mask_product(fwd_dims, fwd_x_first, bwd_dims, bwd_x_first, bwd_terms=3):
    @jax.custom_vjp
    def product(x, mask):
        return _exact_dot(x, mask, fwd_dims, fwd_x_first)

    def fwd(x, mask):
        return product(x, mask), mask

    def bwd(mask, g):
        return _exact_dot(g, mask, bwd_dims, bwd_x_first, bwd_terms), jnp.zeros_like(mask)

    product.defvjp(fwd, bwd)
    return product


_widen = _mask_product(((1,), (0,)), True, ((1,), (1,)), True, bwd_terms=2)
_cumsum_rows = _mask_product(((1,), (0,)), False, ((0,), (0,)), False)
_cumsum_cols = _mask_product(((0,), (0,)), True, ((1,), (1,)), False)


class _Pieces(NamedTuple):
    gathered: jax.Array
    row_off: int
    rows: int


class _Comm(NamedTuple):
    phases: object
    src: jax.Array
    dst: jax.ShapeDtypeStruct
    scratch: tuple


def _tiled(body, name, n_steps, tiled_in, full_in, big_in, tiled_out, acc_out, scratch=(), reverse=False, comm=None):
    n_t, n_f, n_b, n_to, n_a = len(tiled_in), len(full_in), len(big_in), len(tiled_out), len(acc_out)
    n_c = 1 if comm else 0

    def row(i):
        return n_steps - 1 - i if reverse else i

    in_specs, args = [], []
    for arr, br, bc, cb in tiled_in:
        if callable(cb):
            in_specs.append(pl.BlockSpec((br, bc), cb))
        else:
            in_specs.append(pl.BlockSpec((br, bc), functools.partial(lambda i, cb: (row(i), cb), cb=cb)))
        args.append(arr)
    for arr in full_in:
        in_specs.append(pl.BlockSpec(arr.shape, functools.partial(lambda i, nd: (0,) * nd, nd=arr.ndim)))
        args.append(arr)
    big_shapes, n_copies = [], 0
    for big in big_in:
        in_specs.append(pl.BlockSpec(memory_space=pl.ANY))
        if isinstance(big, _Pieces):
            args.append(big.gathered)
            big_shapes.append(((N_DEV * big.rows, PACK_COLS), big.gathered.dtype))
            n_copies += N_DEV
        else:
            args.append(big)
            big_shapes.append((big.shape, big.dtype))
            n_copies += 1
    if comm:
        in_specs.append(pl.BlockSpec(memory_space=pl.ANY))
        args.append(comm.src)
    out_specs, out_shape = [], []
    for rows, cols, dt, br in tiled_out:
        out_specs.append(pl.BlockSpec((br, cols), lambda i: (row(i), 0)))
        out_shape.append(jax.ShapeDtypeStruct((rows, cols), dt))
    for shp, dt in acc_out:
        out_specs.append(pl.BlockSpec(shp, functools.partial(lambda i, nd: (0,) * nd, nd=len(shp))))
        out_shape.append(jax.ShapeDtypeStruct(shp, dt))
    if comm:
        out_specs.append(pl.BlockSpec(memory_space=pl.ANY))
        out_shape.append(comm.dst)
    scratch_shapes = [pltpu.VMEM(shp, dt) for shp, dt in big_shapes] + list(scratch)
    if n_copies:
        scratch_shapes.append(pltpu.SemaphoreType.DMA((n_copies,)))
    if comm:
        scratch_shapes += list(comm.scratch)

    def kern(*refs):
        n_in = n_t + n_f + n_b + n_c
        ins = refs[: n_t + n_f]
        big_hbm = refs[n_t + n_f : n_t + n_f + n_b]
        outs = refs[n_in : n_in + n_to + n_a]
        rest = refs[n_in + n_to + n_a + n_c :]
        big_vmem, scr = rest[:n_b], rest[n_b:]
        if comm:
            scr, comm_scr = scr[:-len(comm.scratch)], scr[-len(comm.scratch):]
            comm_start, comm_mid, comm_finish = comm.phases(refs[n_in - 1], refs[n_in + n_to + n_a], *comm_scr)
        if n_copies:
            scr, copy_sems = scr[:-1], scr[-1]
        step = pl.program_id(0)

        @pl.when(step == 0)
        def _():
            copies = []
            for big, src, dst in zip(big_in, big_hbm, big_vmem):
                if isinstance(big, _Pieces):
                    for j in range(N_DEV):
                        copies.append((src.at[j, pl.ds(big.row_off, big.rows), :], dst.at[pl.ds(j * big.rows, big.rows), :]))
                else:
                    copies.append((src, dst))
            copies = [pltpu.make_async_copy(a, b, copy_sems.at[k]) for k, (a, b) in enumerate(copies)]
            for cp in copies:
                cp.start()
            if comm:
                comm_start()
            for acc in outs[n_to:]:
                acc[...] = jnp.zeros(acc.shape, acc.dtype)
            for cp in copies:
                cp.wait()

        body(row(step), *ins, *big_vmem, *outs, *scr)
        if comm:
            pl.when(step == (n_steps - 1) // 2)(comm_mid)
            pl.when(step == n_steps - 1)(comm_finish)

    res = pl.pallas_call(
        kern,
        out_shape=out_shape,
        grid=(n_steps,),
        in_specs=in_specs,
        out_specs=out_specs,
        scratch_shapes=scratch_shapes,
        name=name,
        compiler_params=pltpu.CompilerParams(dimension_semantics=("arbitrary",), vmem_limit_bytes=VMEM_LIMIT),
    )(*args)
    return res


FWD_CHUNKS = ((0, 1536), (1536, D_FF))
DGRAD_CHUNKS = ((0, 1024), (1024, 2048), (2048, D_FF))
FFN_TM = 256


def _ffn_fwd(h, g, wg_t, wu_t, wd, name, comm=None, mixed=None):
    T = h.shape[0]

    def ffn(x, g_ref, wg_ref, wu_ref, wd_ref, o_ref, n_ref, a_ref, b_ref, s_ref):
        n = _rms(x, g_ref[...]).astype(BF16)
        n_ref[...] = n
        f = jnp.zeros(x.shape, F32)
        for lo, hi in FWD_CHUNKS:
            a = _dot_nt(n, wg_ref[lo:hi, :])
            b = _dot_nt(n, wu_ref[lo:hi, :])
            s = (_silu(a) * b).astype(BF16)
            a_ref[:, lo:hi] = a.astype(BF16)
            b_ref[:, lo:hi] = b.astype(BF16)
            s_ref[:, lo:hi] = s
            f = f + jnp.dot(s, wd_ref[lo:hi, :], preferred_element_type=F32)
        o_ref[...] = x + 0.5 * f

    def body_plain(i, h_ref, *refs):
        ffn(h_ref[...], *refs)

    def body_mixed(i, h_ref, ya_ref, yb_ref, g_ref, wg_ref, wu_ref, wd_ref, wo_ref, o_ref, n_ref, a_ref, b_ref, s_ref, x_ref):
        x = (h_ref[...] + jnp.dot(ya_ref[...], wo_ref[:GM_WIDTH, :], preferred_element_type=F32)
             + jnp.dot(yb_ref[...], wo_ref[GM_WIDTH:, :], preferred_element_type=F32))
        x_ref[...] = x
        ffn(x, g_ref, wg_ref, wu_ref, wd_ref, o_ref, n_ref, a_ref, b_ref, s_ref)

    body = body_mixed if mixed else body_plain
    tiled_in, big_in = [(h, FFN_TM, D_MODEL, 0)], [wg_t, wu_t, wd]
    tiled_out = [(T, D_MODEL, F32, FFN_TM), (T, D_MODEL, BF16, FFN_TM), (T, D_FF, BF16, FFN_TM), (T, D_FF, BF16, FFN_TM),
                 (T, D_FF, BF16, FFN_TM)]
    if mixed:
        tiled_in += [(mixed[0], FFN_TM, GM_WIDTH, 0), (mixed[1], FFN_TM, SSM_WIDTH, 0)]
        big_in.append(mixed[2])
        tiled_out.append((T, D_MODEL, F32, FFN_TM))
    return _tiled(body, name, T // FFN_TM, tiled_in, [g], big_in, tiled_out, [], comm=comm)


def _ffn_dgrad(h, dout, a16, b16, g, wg_t, wu_t, wd, name):
    T = h.shape[0]

    def body(i, h_ref, do_ref, a_ref, b_ref, g_ref, wg_ref, wu_ref, wd_ref, dh_ref, da_ref, db_ref, dg_ref):
        dout = do_ref[...]
        _, rms_vjp = jax.vjp(_rms, h_ref[...], g_ref[...])
        dfo = (0.5 * dout).astype(BF16)
        dn = jnp.zeros(dout.shape, F32)
        for lo, hi in DGRAD_CHUNKS:
            a = a_ref[:, lo:hi].astype(F32)
            b = b_ref[:, lo:hi].astype(F32)
            sg = jax.nn.sigmoid(a)
            ds = _dot_nt(dfo, wd_ref[lo:hi, :])
            db = (ds * (a * sg)).astype(BF16)
            da = (ds * b * (sg * (1.0 + a * (1.0 - sg)))).astype(BF16)
            dn = dn + _dot(da, wg_ref[lo:hi, :]) + _dot(db, wu_ref[lo:hi, :])
            da_ref[:, lo:hi] = da
            db_ref[:, lo:hi] = db
        dx, dg = rms_vjp(dn)
        dh_ref[...] = dout + dx
        dg_ref[...] += dg

    return _tiled(body, name, T // FFN_TM,
                  [(h, FFN_TM, D_MODEL, 0), (dout, FFN_TM, D_MODEL, 0), (a16, FFN_TM, D_FF, 0), (b16, FFN_TM, D_FF, 0)],
                  [g], [wg_t, wu_t, wd],
                  [(T, D_MODEL, F32, FFN_TM), (T, D_FF, BF16, FFN_TM), (T, D_FF, BF16, FFN_TM)], [((1, D_MODEL), F32)])


FF_BN = D_FF // 2
DOWN_BN, DOWN_BK = 512, 1024
SQUARE_BN = 1024
ZXD_BN = ZXD_W // 3


def _wgrad(a, b, bn, name, scale=None, transpose_out=False, bk=2048, comm=None):
    T, M = a.shape
    N = b.shape[1]
    bk = min(bk, T)
    assert M % LANES == 0 and N % bn == 0 and T % bk == 0
    n_j, n_k = N // bn, T // bk
    n_c = 1 if comm else 0

    def kern(*refs):
        a_ref, b_ref, o_ref, acc_ref = refs[0], refs[1], refs[2 + n_c], refs[3 + 2 * n_c]
        j, k = pl.program_id(0), pl.program_id(1)
        if comm:
            comm_start, comm_mid, comm_finish = comm.phases(refs[2], refs[4], *refs[6:])
            pl.when((j == 0) & (k == 0))(comm_start)

        @pl.when(k == 0)
        def _():
            acc_ref[...] = jnp.zeros(acc_ref.shape, F32)

        bv = b_ref[...]
        if scale is not None:
            bv = bv * scale
        acc_ref[...] += _dot_tn(a_ref[...], bv)

        @pl.when(k == n_k - 1)
        def _():
            acc = acc_ref[...]
            o_ref[...] = (acc.T if transpose_out else acc).astype(BF16)

        if comm:
            pl.when((j == (n_j - 1) // 2) & (k == n_k - 1))(comm_mid)
            pl.when((j == n_j - 1) & (k == n_k - 1))(comm_finish)

    if transpose_out:
        out_shape, out_spec = (N, M), pl.BlockSpec((bn, M), lambda j, k: (j, 0))
    else:
        out_shape, out_spec = (M, N), pl.BlockSpec((M, bn), lambda j, k: (0, j))
    any_spec = pl.BlockSpec(memory_space=pl.ANY)
    res = pl.pallas_call(
        kern,
        out_shape=[jax.ShapeDtypeStruct(out_shape, BF16)] + ([comm.dst] if comm else []),
        grid=(n_j, n_k),
        in_specs=[pl.BlockSpec((bk, M), lambda j, k: (k, 0)), pl.BlockSpec((bk, bn), lambda j, k: (k, j))] + [any_spec] * n_c,
        out_specs=[out_spec] + [any_spec] * n_c,
        scratch_shapes=[pltpu.VMEM((M, bn), F32)] + (list(comm.scratch) if comm else []),
        name=name,
        compiler_params=pltpu.CompilerParams(dimension_semantics=("arbitrary", "arbitrary"), vmem_limit_bytes=VMEM_LIMIT),
    )(a, b, *([comm.src] if comm else []))
    return res if comm else res[0]


PROJ_TM = 512
PROJ_DGRAD_TM = 256
UVZ_W = 2 * GM_WIDTH + SSM_WIDTH
PROJ_KEPT = UVZ_W + LANES
Z_BLK = 2 * GM_WIDTH // SSM_WIDTH
DT_BLK = UVZ_W // LANES


def _mix_in_fwd(h, g, w_in_t, conv_w, conv_b):
    T = h.shape[0]

    def body(i, h_ref, g_ref, cw_ref, cb_ref, w_ref, p_ref, n_ref, x_ref, xc_ref, ext_ref):
        @pl.when(i == 0)
        def _():
            ext_ref[0:HALO, :] = jnp.zeros((HALO, CONV_DIM), F32)

        n = _rms(h_ref[...], g_ref[...]).astype(BF16)
        n_ref[...] = n
        proj = _dot_nt(n, w_ref[...])
        p_ref[:, :UVZ_W] = proj[:, :UVZ_W]
        p_ref[:, UVZ_W:] = jnp.concatenate(
            [proj[:, UVZ_W + CONV_DIM:], jnp.zeros((PROJ_TM, LANES - SSM_HEADS), F32)], axis=1)
        xbc = proj[:, UVZ_W:UVZ_W + CONV_DIM]
        x_ref[...] = xbc.astype(BF16)
        ext_ref[HALO:, :] = xbc
        xc_ref[...] = _conv_taps(ext_ref, cw_ref[...], cb_ref[...], PROJ_TM)
        ext_ref[0:HALO, :] = ext_ref[PROJ_TM:PROJ_TM + HALO, :]

    return _tiled(body, "mix_in_fwd", T // PROJ_TM, [(h, PROJ_TM, D_MODEL, 0)], [g, conv_w, conv_b], [w_in_t],
                  [(T, PROJ_KEPT, F32, PROJ_TM), (T, D_MODEL, BF16, PROJ_TM), (T, CONV_DIM, BF16, PROJ_TM),
                   (T, CONV_DIM, F32, PROJ_TM)], [],
                  scratch=[pltpu.VMEM((HALO + PROJ_TM, CONV_DIM), F32)])


def _mix_in_dgrad(h, dh_in, dp_uv, dp_zxd, g, w_in_t, comm=None):
    T = h.shape[0]

    def body(i, h_ref, dh_ref, duv_ref, dzxd_ref, g_ref, w_ref, o_ref, dg_ref):
        dzxd, zx_w = dzxd_ref[...], ZXD_W - LANES
        dn = (_dot(duv_ref[...], w_ref[:UV_W, :]) + _dot(dzxd[:, :zx_w], w_ref[UV_W:UV_W + zx_w, :])
              + _dot(dzxd[:, zx_w:zx_w + SSM_HEADS], w_ref[UV_W + zx_w:, :]))
        _, rms_vjp = jax.vjp(_rms, h_ref[...], g_ref[...])
        dx, dg = rms_vjp(dn)
        o_ref[...] = dh_ref[...] + dx
        dg_ref[...] += dg

    return _tiled(body, "mix_in_dgrad", T // PROJ_DGRAD_TM,
                  [(h, PROJ_DGRAD_TM, D_MODEL, 0), (dh_in, PROJ_DGRAD_TM, D_MODEL, 0), (dp_uv, PROJ_DGRAD_TM, UV_W, 0),
                   (dp_zxd, PROJ_DGRAD_TM, ZXD_W, 0)], [g], [w_in_t],
                  [(T, D_MODEL, F32, PROJ_DGRAD_TM)], [((1, D_MODEL), F32)], comm=comm)


def _out_proj_dgrad(dh, w_out):
    T = dh.shape[0]

    def body(i, dh_ref, w_ref, dya_ref, dyb_ref):
        d = dh_ref[...].astype(BF16)
        dya_ref[...] = _dot_nt(d, w_ref[:GM_WIDTH, :])
        dyb_ref[...] = _dot_nt(d, w_ref[GM_WIDTH:, :])

    rows = min(T, 2 * PROJ_TM)
    return _tiled(body, "out_proj_dgrad", T // rows, [(dh, rows, D_MODEL, 0)], [], [w_out],
                  [(T, GM_WIDTH, F32, rows), (T, SSM_WIDTH, F32, rows)], [])


def _gm_chunk(u, v, ln_g, ln_b, b_st, out_g, *w_heads):
    ug = _gelu(u)
    vg = _gelu(v)
    mu = jnp.mean(vg, axis=-1, keepdims=True)
    xc = vg - mu
    vn = xc * lax.rsqrt(jnp.mean(xc * xc, axis=-1, keepdims=True) + EPS) * ln_g + ln_b
    t_idx = lax.broadcasted_iota(jnp.int32, (CHUNK, CHUNK), 0)
    s_idx = lax.broadcasted_iota(jnp.int32, (CHUNK, CHUNK), 1)
    causal = t_idx >= s_idx
    mixed = []
    for hd in range(GM_HEADS):
        wm = jnp.where(causal, w_heads[hd], 0.0)
        cols = slice(hd * GM_HEAD_DIM, (hd + 1) * GM_HEAD_DIM)
        mixed.append(_dot(wm, vn[:, cols]) + b_st[:, hd:hd + 1])
    ya0 = ug * jnp.concatenate(mixed, axis=1)
    return _rms(ya0, out_g)


GM_FWD_CHUNKS = 4


def _gm_fwd(proj, ln_g, ln_b, w_s, b_st, out_g):
    T = proj.shape[0]

    rows = GM_FWD_CHUNKS * CHUNK

    def body(i, u_ref, v_ref, lg_ref, lb_ref, w_ref, bs_ref, og_ref, ya_ref):
        w_heads = [w_ref[hd] for hd in range(GM_HEADS)]
        for c in range(GM_FWD_CHUNKS):
            tok = pl.ds(c * CHUNK, CHUNK)
            ya = _gm_chunk(u_ref[tok, :], v_ref[tok, :], lg_ref[...], lb_ref[...], bs_ref[...], og_ref[...], *w_heads)
            ya_ref[tok, :] = ya.astype(BF16)

    return _tiled(body, "gmlp_fwd", T // rows, [(proj, rows, GM_WIDTH, 0), (proj, rows, GM_WIDTH, 1)],
                  [ln_g, ln_b, w_s, b_st, out_g], [], [(T, GM_WIDTH, BF16, rows)], [])[0]


def _gm_bwd(proj, dya, ln_g, ln_b, w_s, b_st, out_g):
    T = proj.shape[0]

    def body(i, u_ref, v_ref, dy_ref, lg_ref, lb_ref, w_ref, bs_ref, og_ref, duv_ref, dlg_ref, dlb_ref, dw_ref, dbs_ref,
             dog_ref):
        w_heads = [w_ref[hd] for hd in range(GM_HEADS)]
        _, vjp = jax.vjp(_gm_chunk, u_ref[...], v_ref[...], lg_ref[...], lb_ref[...], bs_ref[...], og_ref[...], *w_heads)
        grads = vjp(dy_ref[...])
        duv_ref[:, :GM_WIDTH] = grads[0].astype(BF16)
        duv_ref[:, GM_WIDTH:] = grads[1].astype(BF16)
        dlg_ref[...] += grads[2]
        dlb_ref[...] += grads[3]
        dbs_ref[...] += grads[4]
        dog_ref[...] += grads[5]
        for hd in range(GM_HEADS):
            dw_ref[hd] += grads[6 + hd]

    return _tiled(body, "gmlp_bwd", T // CHUNK,
                  [(proj, CHUNK, GM_WIDTH, 0), (proj, CHUNK, GM_WIDTH, 1), (dya, CHUNK, GM_WIDTH, 0)],
                  [ln_g, ln_b, w_s, b_st, out_g], [], [(T, UV_W, BF16, CHUNK)],
                  [((1, GM_WIDTH), F32), ((1, GM_WIDTH), F32), ((GM_HEADS, CHUNK, CHUNK), F32),
                   ((CHUNK, GM_HEADS), F32), ((1, GM_WIDTH), F32)])


def _ssd_chunk(xc, z, dtr, s_in, dt_bias, a_log, d_skip, norm_g):
    half = SSM_WIDTH // SSM_GROUPS
    l_idx = lax.broadcasted_iota(jnp.int32, (CHUNK, CHUNK), 0)
    s_idx = lax.broadcasted_iota(jnp.int32, (CHUNK, CHUNK), 1)
    causal = l_idx >= s_idx
    head_of_col = lax.broadcasted_iota(jnp.int32, (SSM_HEADS, SSM_WIDTH), 1) // SSM_HEAD_DIM
    expand = (head_of_col == lax.broadcasted_iota(jnp.int32, (SSM_HEADS, SSM_WIDTH), 0)).astype(BF16)

    xcs = _silu(xc)
    xs = xcs[:, :SSM_WIDTH]
    dt = jax.nn.softplus(dtr + dt_bias)
    adt = dt * (-jnp.exp(a_log))
    acs = _cumsum_rows(adt, causal.astype(BF16))
    acs_t = _cumsum_cols(adt, (l_idx <= s_idx).astype(BF16))
    tot = acs[CHUNK - 1:CHUNK, :]
    dt_w = _widen(dt, expand)
    out_decay_w = _widen(jnp.exp(acs), expand)
    state_decay_w = _widen(jnp.exp(tot - acs), expand)
    chunk_decay_w = _widen(jnp.exp(tot), expand)
    d_skip_w = _widen(d_skip, expand)
    xdt = xs * dt_w
    xdt_decayed = xdt * state_decay_w

    y_diag, y_off, states = [], [], []
    for grp in range(SSM_GROUPS):
        b0 = SSM_WIDTH + grp * SSM_STATE
        c0 = SSM_WIDTH + SSM_GROUPS * SSM_STATE + grp * SSM_STATE
        bm = xcs[:, b0:b0 + SSM_STATE].astype(BF16)
        cm = xcs[:, c0:c0 + SSM_STATE].astype(BF16)
        cb = _dot_nt(cm, bm)
        for k in range(grp * SSM_HEADS // SSM_GROUPS, (grp + 1) * SSM_HEADS // SSM_GROUPS):
            decay = jnp.exp(jnp.where(causal, acs[:, k:k + 1] - acs_t[k:k + 1, :], -jnp.inf))
            y_diag.append(_dot(cb * decay, xdt[:, k * SSM_HEAD_DIM:(k + 1) * SSM_HEAD_DIM]))
        cols = slice(grp * half, (grp + 1) * half)
        states.append(_dot_tn(bm, xdt_decayed[:, cols]))
        y_off.append(_dot(cm, s_in[:, cols]))
    y = jnp.concatenate(y_diag, axis=1) + jnp.concatenate(y_off, axis=1) * out_decay_w + xs * d_skip_w
    s_out = s_in * chunk_decay_w + jnp.concatenate(states, axis=1)
    y = y * _silu(z)
    normed = []
    for grp in range(SSM_GROUPS):
        yg = y[:, grp * half:(grp + 1) * half]
        normed.append(yg * lax.rsqrt(jnp.mean(yg * yg, axis=-1, keepdims=True) + EPS))
    return jnp.concatenate(normed, axis=1) * norm_g, s_out


def _sum_row_tiles(x):
    return x.reshape(x.shape[0] // F32_ROWS, F32_ROWS, x.shape[1]).sum(axis=0)


def _conv_taps(ext_ref, w, b, rows):
    y = b
    for k in range(SSM_CONV):
        y = y + w[k:k + 1, :] * ext_ref[pl.ds(HALO - (SSM_CONV - 1) + k, rows), :]
    return y


SSD_FWD_CHUNKS = 4


def _ssd_fwd(proj, xc, dt_bias, a_log, d_skip, norm_g, comm=None):
    T = proj.shape[0]
    n_chunks = T // CHUNK
    rows = SSD_FWD_CHUNKS * CHUNK

    def body(i, z_ref, xc_ref, dt_ref, dtb_ref, al_ref, dsk_ref, ng_ref, yb_ref, sin_ref, st_ref):
        @pl.when(i == 0)
        def _():
            st_ref[...] = jnp.zeros(st_ref.shape, F32)

        for c in range(SSD_FWD_CHUNKS):
            tok = pl.ds(c * CHUNK, CHUNK)
            s_in = st_ref[...]
            yb, s_out = _ssd_chunk(xc_ref[tok, :], z_ref[tok, :], dt_ref[tok, 0:SSM_HEADS], s_in, dtb_ref[...], al_ref[...],
                                   dsk_ref[...], ng_ref[...])
            yb_ref[tok, :] = yb.astype(BF16)
            sin_ref[pl.ds(c * SSM_STATE, SSM_STATE), :] = s_in
            st_ref[...] = s_out

    return _tiled(body, "ssd_fwd", T // rows,
                  [(proj, rows, SSM_WIDTH, Z_BLK), (xc, rows, CONV_DIM, 0), (proj, rows, LANES, DT_BLK)],
                  [dt_bias, a_log, d_skip, norm_g], [],
                  [(T, SSM_WIDTH, BF16, rows), (n_chunks * SSM_STATE, SSM_WIDTH, F32, SSD_FWD_CHUNKS * SSM_STATE)], [],
                  scratch=[pltpu.VMEM((SSM_STATE, SSM_WIDTH), F32)], comm=comm)


def _ssd_bwd(proj, x16, xc, dyb, s_all, conv_w, dt_bias, a_log, d_skip, norm_g, comm=None):
    T = proj.shape[0]
    n_chunks = T // CHUNK

    def body(i, z_ref, x_ref, xc_ref, dt_ref, dy_ref, sin_ref, cw_ref, dtb_ref, al_ref, dsk_ref, ng_ref,
             dzxd_ref, dcw_ref, dcb_ref, ddtb_ref, dal_ref, ddsk_ref, dng_ref, dext_ref, dst_ref, cw_acc, cb_acc):
        @pl.when(i == n_chunks - 1)
        def _():
            dext_ref[CHUNK:, :] = jnp.zeros((HALO, CONV_DIM), F32)
            dst_ref[...] = jnp.zeros(dst_ref.shape, F32)
            cw_acc[...] = jnp.zeros(cw_acc.shape, F32)
            cb_acc[...] = jnp.zeros(cb_acc.shape, F32)

        _, vjp = jax.vjp(_ssd_chunk, xc_ref[...], z_ref[...], dt_ref[:, 0:SSM_HEADS], sin_ref[...], dtb_ref[...], al_ref[...],
                         dsk_ref[...], ng_ref[...])
        dxc, dz, ddtr, ds_in, ddtb, dal, ddsk, dng = vjp((dy_ref[...], dst_ref[...]))
        dst_ref[...] = ds_in
        ddtb_ref[...] += ddtb
        dal_ref[...] += dal
        ddsk_ref[...] += ddsk
        dng_ref[...] += dng
        dext_ref[0:CHUNK, :] = dxc
        cw = cw_ref[...]
        x = x_ref[...].astype(F32)
        dx = jnp.zeros((CHUNK, CONV_DIM), F32)
        for k in range(SSM_CONV):
            shifted = dext_ref[pl.ds(SSM_CONV - 1 - k, CHUNK), :]
            dx = dx + cw[k:k + 1, :] * shifted
            cw_acc[k] += _sum_row_tiles(shifted * x)
        cb_acc[...] += _sum_row_tiles(dxc)

        @pl.when(i == 0)
        def _():
            dcw_ref[...] = jnp.sum(cw_acc[...], axis=1)
            dcb_ref[...] = jnp.sum(cb_acc[...], axis=0, keepdims=True)

        dext_ref[CHUNK:, :] = dext_ref[0:HALO, :]
        dzxd_ref[:, 0:SSM_WIDTH] = dz.astype(BF16)
        dzxd_ref[:, SSM_WIDTH:SSM_WIDTH + CONV_DIM] = dx.astype(BF16)
        dzxd_ref[:, SSM_WIDTH + CONV_DIM:] = jnp.concatenate(
            [ddtr, jnp.zeros((CHUNK, LANES - SSM_HEADS), F32)], axis=1).astype(BF16)

    return _tiled(body, "ssd_bwd", n_chunks,
                  [(proj, CHUNK, SSM_WIDTH, Z_BLK), (x16, CHUNK, CONV_DIM, 0), (xc, CHUNK, CONV_DIM, 0),
                   (proj, CHUNK, LANES, DT_BLK), (dyb, CHUNK, SSM_WIDTH, 0), (s_all, SSM_STATE, SSM_WIDTH, 0)],
                  [conv_w, dt_bias, a_log, d_skip, norm_g], [],
                  [(T, ZXD_W, BF16, CHUNK)],
                  [((SSM_CONV, CONV_DIM), F32), ((1, CONV_DIM), F32), ((1, SSM_HEADS), F32), ((1, SSM_HEADS), F32),
                   ((1, SSM_HEADS), F32), ((1, SSM_WIDTH), F32)],
                  scratch=[pltpu.VMEM((CHUNK + HALO, CONV_DIM), F32), pltpu.VMEM((SSM_STATE, SSM_WIDTH), F32),
                           pltpu.VMEM((SSM_CONV, F32_ROWS, CONV_DIM), F32), pltpu.VMEM((F32_ROWS, CONV_DIM), F32)],
                  reverse=True, comm=comm)


TAIL_TM = 512


def _tail(h, p, target, ple_norm, w_gate, b_gate, w_proj_t, final_norm):
    T = h.shape[0]

    def head(x, pre, pp, b_g, f_norm, tgt):
        gate = jax.nn.sigmoid(pre + b_g)
        out = _rms(x + gate * pp, f_norm)
        err = out - tgt
        return 0.5 * jnp.sum(jnp.mean(err * err, axis=-1, keepdims=True), axis=0, keepdims=True)

    def body(i, h_ref, p_ref, t_ref, pn_ref, bg_ref, fn_ref, wg_ref, wp_ref, dh_ref, loss_ref, dwg_ref, dwp_ref, dpn_ref,
             dbg_ref, dfn_ref):
        x = h_ref[...]
        n4f, n_vjp = jax.vjp(_rms, x, pn_ref[...])
        n4 = n4f.astype(BF16)
        pre = jnp.dot(n4, wg_ref[...], preferred_element_type=F32)
        p16 = p_ref[...].astype(BF16)
        pp = _dot_nt(p16, wp_ref[...])
        loss, h_vjp = jax.vjp(functools.partial(head, tgt=t_ref[...]), x, pre, pp, bg_ref[...], fn_ref[...])
        dx, dpre, dpp, dbg, dfn = h_vjp(jnp.ones((1, 1), F32))
        dpre16 = dpre.astype(BF16)
        dn4 = _dot_nt(dpre16, wg_ref[...])
        dx2, dpn = n_vjp(dn4)
        dh_ref[...] = dx + dx2
        loss_ref[...] += loss
        dwg_ref[...] += _dot_tn(n4, dpre16)
        dwp_ref[...] += _dot_tn(p16, dpp)
        dpn_ref[...] += dpn
        dbg_ref[...] += dbg
        dfn_ref[...] += dfn

    return _tiled(body, "tail", T // TAIL_TM,
                  [(h, TAIL_TM, D_MODEL, 0), (p, TAIL_TM, D_PLE, 0), (target, TAIL_TM, D_MODEL, 0)],
                  [ple_norm, b_gate, final_norm], [w_gate, w_proj_t],
                  [(T, D_MODEL, F32, TAIL_TM)],
                  [((1, 1), F32), ((D_MODEL, D_MODEL), F32), ((D_PLE, D_MODEL), F32), ((1, D_MODEL), F32),
                   ((1, D_MODEL), F32), ((1, D_MODEL), F32)])


def _gather_phases(x_ref, out_ref, send_sems, recv_sems, local_sem):
    mx, my, mc = lax.axis_index("x"), lax.axis_index("y"), lax.axis_index("c")
    me, sibling = (mx, my, mc), (mx, my, 1 - mc)
    chips = [(1 - mx, my), (mx, 1 - my), (1 - mx, 1 - my)]

    def rows(px, py, pc):
        return out_ref.at[4 * px + 2 * py + pc]

    def copy(k, block, to, src=None):
        return pltpu.make_async_remote_copy(
            src_ref=rows(*block) if src is None else src, dst_ref=rows(*block),
            send_sem=send_sems.at[k], recv_sem=recv_sems.at[k], device_id=to, device_id_type=MESH)

    mine = pltpu.make_async_copy(x_ref, rows(*me), local_sem)
    first = [copy(0, me, sibling, src=x_ref)] + [copy(1 + j, me, (*chip, mc), src=x_ref) for j, chip in enumerate(chips)]
    passed = [copy(4 + j, (*chip, mc), sibling) for j, chip in enumerate(chips)]

    def start():
        mine.start()
        for cp in first:
            cp.start()

    def mid():
        for j, chip in enumerate(chips):
            copy(1 + j, (*chip, mc), me).wait_recv()
            passed[j].start()

    def finish():
        copy(0, sibling, me).wait_recv()
        for j, chip in enumerate(chips):
            copy(4 + j, (*chip, 1 - mc), me).wait_recv()
        for cp in first + passed:
            cp.wait_send()
        mine.wait()

    return start, mid, finish


def _exchange_phases(x_ref, out_ref, send_sems, recv_sems, local_sem):
    mx, my, mc = lax.axis_index("x"), lax.axis_index("y"), lax.axis_index("c")
    me = 4 * mx + 2 * my + mc
    mine = pltpu.make_async_copy(x_ref.at[me], out_ref.at[me], local_sem)
    copies = []
    for k in range(1, N_DEV):
        px = 1 - mx if k & 4 else mx
        py = 1 - my if k & 2 else my
        pc = 1 - mc if k & 1 else mc
        copies.append(pltpu.make_async_remote_copy(
            src_ref=x_ref.at[4 * px + 2 * py + pc], dst_ref=out_ref.at[me], send_sem=send_sems.at[k - 1],
            recv_sem=recv_sems.at[k - 1], device_id=(px, py, pc), device_id_type=MESH))

    def start():
        mine.start()
        for cp in copies:
            cp.start()

    def finish():
        for cp in copies:
            cp.wait_recv()
        for cp in copies:
            cp.wait_send()
        mine.wait()

    return start, lambda: None, finish


def _chip_exchange_phases(x_ref, out_ref, mine, recv, sums, load_sems, pair_send, pair_recv, chip_send, chip_recv, out_sem):
    mx, my, mc = lax.axis_index("x"), lax.axis_index("y"), lax.axis_index("c")
    my_chip = 2 * mx + my
    load = [pltpu.make_async_copy(x_ref.at[2 * q + mc], mine.at[q], load_sems.at[q]) for q in range(N_CHIPS)]
    to_sibling = [pltpu.make_async_remote_copy(
        src_ref=x_ref.at[2 * q + 1 - mc], dst_ref=recv.at[q], send_sem=pair_send.at[q], recv_sem=pair_recv.at[q],
        device_id=(mx, my, 1 - mc), device_id_type=MESH) for q in range(N_CHIPS)]
    to_chips = []
    for k in range(1, N_CHIPS):
        px = 1 - mx if k & 2 else mx
        py = 1 - my if k & 1 else my
        to_chips.append(pltpu.make_async_remote_copy(
            src_ref=sums.at[2 * px + py], dst_ref=out_ref.at[my_chip], send_sem=chip_send.at[k - 1],
            recv_sem=chip_recv.at[k - 1], device_id=(px, py, mc), device_id_type=MESH))
    keep = pltpu.make_async_copy(sums.at[my_chip], out_ref.at[my_chip], out_sem)

    def start():
        for cp in load + to_sibling:
            cp.start()

    def mid():
        for cp in load:
            cp.wait()
        for cp in to_sibling:
            cp.wait_recv()
        for q in range(N_CHIPS):
            sums[q] = (mine[q].astype(F32) + recv[q].astype(F32)).astype(sums.dtype)
        for cp in to_chips + [keep]:
            cp.start()

    def finish():
        for cp in to_chips:
            cp.wait_recv()
        for cp in to_chips + to_sibling:
            cp.wait_send()
        keep.wait()

    return start, mid, finish


FLAT_SCRATCH = (pltpu.SemaphoreType.DMA((N_DEV - 1,)), pltpu.SemaphoreType.DMA((N_DEV - 1,)), pltpu.SemaphoreType.DMA)


def _gather_comm(x):
    return _Comm(_gather_phases, x, jax.ShapeDtypeStruct((N_DEV,) + x.shape, x.dtype), FLAT_SCRATCH)


def _exchange_comm(x):
    return _Comm(_exchange_phases, x, jax.ShapeDtypeStruct(x.shape, x.dtype), FLAT_SCRATCH)


def _chip_exchange_comm(x):
    stage = pltpu.VMEM((N_CHIPS,) + x.shape[1:], x.dtype)
    sems = [pltpu.SemaphoreType.DMA((n,)) for n in (N_CHIPS, N_CHIPS, N_CHIPS, N_CHIPS - 1, N_CHIPS - 1)]
    return _Comm(_chip_exchange_phases, x, jax.ShapeDtypeStruct((N_CHIPS,) + x.shape[1:], x.dtype),
                 (stage, stage, stage, *sems, pltpu.SemaphoreType.DMA))


def _comm_alone(comms, name):
    n = len(comms)

    def body(*refs):
        phases, first = [], 2 * n
        for k, comm in enumerate(comms):
            phases.append(comm.phases(refs[k], refs[n + k], *refs[first:first + len(comm.scratch)]))
            first += len(comm.scratch)
        for step in range(3):
            for phase in phases:
                phase[step]()

    any_spec = pl.BlockSpec(memory_space=pl.ANY)
    return pl.pallas_call(
        body,
        out_shape=[comm.dst for comm in comms],
        in_specs=[any_spec] * n,
        out_specs=[any_spec] * n,
        scratch_shapes=[shape for comm in comms for shape in comm.scratch],
        name=name,
        compiler_params=pltpu.CompilerParams(vmem_limit_bytes=VMEM_LIMIT),
    )(*[comm.src for comm in comms])


def _sum_parts(p_ref):
    g = p_ref[0].astype(F32)
    for j in range(1, p_ref.shape[0]):
        g = g + p_ref[j].astype(F32)
    return g


def _adamw_store(g, w_ref, m_ref, v_ref, g_ref, d_ref, nm_ref, nv_ref):
    m_new = ADAM_B1 * m_ref[...] + (1.0 - ADAM_B1) * g
    v_new = ADAM_B2 * v_ref[...] + (1.0 - ADAM_B2) * jnp.square(g)
    m_hat = m_new / (1.0 - ADAM_B1 ** ADAM_STEP)
    v_hat = v_new / (1.0 - ADAM_B2 ** ADAM_STEP)
    g_ref[...] = g
    d_ref[...] = -ADAM_LR * (m_hat / (jnp.sqrt(v_hat) + ADAM_EPS) + ADAM_WD * w_ref[...])
    nm_ref[...] = m_new
    nv_ref[...] = v_new


def _adamw_shard(parts, off, w, m, v, name, n_tiles):
    _, rows, c = w.shape
    assert c == PACK_COLS
    by_rows = rows % BF16_ROWS == 0
    if by_rows:
        tr = rows // n_tiles
        window = (parts.shape[0], tr, PACK_COLS)
        spec = pl.BlockSpec((None, tr, PACK_COLS), lambda i: (0, i, 0))
    else:
        padded, tc = -(-rows // BF16_ROWS) * BF16_ROWS, PACK_COLS // n_tiles
        window = (parts.shape[0], padded, tc)
        spec = pl.BlockSpec((None, rows, tc), lambda i: (0, 0, i))
    blocked = off % (tr if by_rows else padded) == 0

    def update(p_ref, refs):
        g = _sum_parts(p_ref)
        if not by_rows:
            keep = lax.broadcasted_iota(jnp.int32, (rows, padded), 0) == lax.broadcasted_iota(jnp.int32, (rows, padded), 1)
            g = _exact_dot(g, keep.astype(BF16), ((1,), (0,)), x_first=False)
        _adamw_store(g, *refs)

    def kern_blocked(p_ref, *refs):
        update(p_ref, refs)

    def kern_copied(p_hbm, *refs):
        buf, sem = refs[-2:]
        i = pl.program_id(0)
        if by_rows:
            src = p_hbm.at[:, pl.ds(pl.multiple_of(off + i * tr, BF16_ROWS), tr), :]
        else:
            src = p_hbm.at[:, pl.ds(off, padded), pl.ds(pl.multiple_of(i * tc, LANES), tc)]
        cp = pltpu.make_async_copy(src, buf, sem)
        cp.start()
        cp.wait()
        update(buf, refs[:-2])

    if blocked:
        index = (lambda i: (0, off // tr + i, 0)) if by_rows else (lambda i: (0, off // padded, i))
        parts_spec, scratch = pl.BlockSpec(window, index), []
    else:
        parts_spec, scratch = pl.BlockSpec(memory_space=pl.ANY), [pltpu.VMEM(window, parts.dtype), pltpu.SemaphoreType.DMA]
    return pl.pallas_call(
        kern_blocked if blocked else kern_copied,
        out_shape=[jax.ShapeDtypeStruct(w.shape, F32)] * 4,
        grid=(n_tiles,),
        in_specs=[parts_spec, spec, spec, spec],
        out_specs=[spec] * 4,
        scratch_shapes=scratch,
        name=name,
        compiler_params=pltpu.CompilerParams(dimension_semantics=("arbitrary",), vmem_limit_bytes=VMEM_LIMIT),
    )(parts, w, m, v)


def _sum_adamw(parts, w, m, v, tr, name):
    _, R, C = parts.shape

    def kern(p_ref, w_ref, m_ref, v_ref, g_ref, d_ref, nm_ref, nv_ref):
        _adamw_store(_sum_parts(p_ref), w_ref, m_ref, v_ref, g_ref, d_ref, nm_ref, nv_ref)

    row_spec = pl.BlockSpec((tr, C), lambda i: (i, 0))
    return pl.pallas_call(
        kern,
        out_shape=[jax.ShapeDtypeStruct((R, C), F32)] * 4,
        grid=(R // tr,),
        in_specs=[pl.BlockSpec((N_DEV, tr, C), lambda i: (0, i, 0)), row_spec, row_spec, row_spec],
        out_specs=[row_spec] * 4,
        name=name,
        compiler_params=pltpu.CompilerParams(dimension_semantics=("arbitrary",), vmem_limit_bytes=VMEM_LIMIT),
    )(parts, w, m, v)


FF_SHARD = D_FF // N_DEV
CONV_SHARD = (SSM_CONV, CONV_DIM // N_DEV)
SHARDS = {"ffn1_w_gate": ((D_MODEL, FF_SHARD), True), "ffn1_w_up": ((D_MODEL, FF_SHARD), True),
          "ffn1_w_down": ((FF_SHARD, D_MODEL), False),
          "ffn2_w_gate": ((D_MODEL, FF_SHARD), True), "ffn2_w_up": ((D_MODEL, FF_SHARD), True),
          "ffn2_w_down": ((FF_SHARD, D_MODEL), False),
          "w_out": ((2 * D_MODEL // N_DEV, D_MODEL), False), "ple_w_gate": ((D_MODEL // N_DEV, D_MODEL), False),
          "w_in": ((D_MODEL, IN_PROJ // N_DEV), True), "ple_w_proj": ((D_PLE, D_MODEL // N_DEV), True),
          "conv_w": (CONV_SHARD, True),
          "conv_w_mid": (CONV_SHARD, True), "conv_w_low": (CONV_SHARD, True)}
BIG = tuple(name for name in SHARDS if not name.startswith("conv_w_"))
SMALL = ("ffn1_norm", "mix_norm", "gm_ln_g", "gm_ln_b", "gm_w_s", "gm_b_s", "gm_out_norm", "conv_b", "dt_bias", "a_log",
         "d_skip", "ssm_norm", "ffn2_norm", "ple_norm", "ple_b_gate", "final_norm")
SMALL_ROWS = 144


def _piece_rows(name):
    shape = SHARDS[name][0]
    return -(-(shape[0] * shape[1]) // PACK_COLS)


def _pad_cols(flat, name):
    pad = _piece_rows(name) * PACK_COLS - flat.shape[-1]
    return flat if pad == 0 else jnp.pad(flat, [(0, 0)] * (flat.ndim - 1) + [(0, pad)])


class _Pack:
    def __init__(self, names, tile_rows):
        self.names, self.tile_rows, self.offsets, off = names, tile_rows, {}, 0
        for name in names:
            self.offsets[name] = off
            off += _piece_rows(name)
        self.rows = -(-off // tile_rows) * tile_rows

    def pack_local(self, vals):
        parts = []
        for name in self.names:
            val = vals[name]
            parts.append(_pad_cols((val.T if SHARDS[name][1] else val).reshape(-1), name))
        flat = jnp.concatenate(parts)
        return jnp.pad(flat, (0, self.rows * PACK_COLS - flat.shape[0])).reshape(self.rows, PACK_COLS)

    def pack_owner_major(self, grads):
        parts, rows = [], 0
        for name in self.names:
            grad, piece_rows = grads[name].astype(BF16), _piece_rows(name)
            if grad.shape != (N_DEV * piece_rows, PACK_COLS):
                grad = _pad_cols(grad.reshape(N_DEV, -1), name)
            parts.append(grad.reshape(N_DEV, piece_rows, PACK_COLS))
            rows += piece_rows
        if rows < self.rows:
            parts.append(jnp.zeros((N_DEV, self.rows - rows, PACK_COLS), BF16))
        return parts[0] if len(parts) == 1 else jnp.concatenate(parts, axis=1)

    def gathered_piece(self, gathered, name):
        shape = SHARDS[name][0]
        rows = gathered[:, self.offsets[name]:self.offsets[name] + _piece_rows(name), :]
        return rows.reshape(N_DEV, -1)[:, :shape[0] * shape[1]]

    def pieces(self, gathered, name):
        return _Pieces(gathered, self.offsets[name], _piece_rows(name))


GATHER_FFN1 = _Pack(("ffn1_w_gate", "ffn1_w_up", "ffn1_w_down"), BF16_ROWS)
GATHER_MIX = _Pack(("w_out", "ple_w_gate", "w_in", "ple_w_proj", "conv_w", "conv_w_mid", "conv_w_low"), BF16_ROWS)
GATHER_FFN2 = _Pack(("ffn2_w_gate", "ffn2_w_up", "ffn2_w_down"), BF16_ROWS)
SCATTER_LATE = _Pack(("ffn2_w_gate", "ffn2_w_up", "ffn2_w_down", "w_out", "ple_w_gate", "ple_w_proj"), BF16_ROWS)
SCATTER_IN = _Pack(("w_in", "conv_w"), BF16_ROWS)
SCATTER_GATE = _Pack(("ffn1_w_gate",), BF16_ROWS)
SCATTER_UP = _Pack(("ffn1_w_up",), BF16_ROWS)
SCATTER_DOWN = _Pack(("ffn1_w_down",), BF16_ROWS)


def _pack_small(vals, behind=()):
    flat = jnp.concatenate([vals[name].reshape(-1).astype(F32) for name in SMALL] + [b.reshape(-1) for b in behind])
    return jnp.pad(flat, (0, SMALL_ROWS * PACK_COLS - flat.shape[0])).reshape(SMALL_ROWS, PACK_COLS)


def _unpack_small(packed, shapes):
    out, off = {}, 0
    flat = packed.reshape(-1)
    for name in SMALL:
        n = 1
        for s in shapes[name]:
            n *= s
        out[name] = flat[off:off + n].reshape(shapes[name])
        off += n
    return out


WEIGHTS = ("ffn1_norm", "ffn1_w_gate", "ffn1_w_up", "ffn1_w_down", "mix_norm", "w_in", "gm_ln_g", "gm_ln_b", "gm_w_s",
           "gm_b_s", "gm_out_norm", "conv_w", "conv_b", "dt_bias", "a_log", "d_skip", "ssm_norm", "w_out", "ffn2_norm",
           "ffn2_w_gate", "ffn2_w_up", "ffn2_w_down", "ple_norm", "ple_w_gate", "ple_b_gate", "ple_w_proj", "final_norm")


def _step(x, p, target, w, m, v):
    local = lambda d: {name: d[name][0] for name in BIG}

    shards = {name: val.astype(BF16) for name, val in local(w).items()}
    conv_high = lax.reduce_precision(w["conv_w"][0], 8, 7)
    conv_mid = lax.reduce_precision(w["conv_w"][0] - conv_high, 8, 7)
    shards["conv_w"] = conv_high.astype(BF16)
    shards["conv_w_mid"] = conv_mid.astype(BF16)
    shards["conv_w_low"] = (w["conv_w"][0] - conv_high - conv_mid).astype(BF16)
    g_ffn1 = _comm_alone([_gather_comm(GATHER_FFN1.pack_local(shards))], "gather_ffn1")[0]

    row = lambda name: w[name].reshape(1, -1)
    gm_w_s = w["gm_w_s"][0]
    gm_b_st = jnp.transpose(w["gm_b_s"][0])
    ffn1 = (row("ffn1_norm"),) + tuple(GATHER_FFN1.pieces(g_ffn1, name) for name in GATHER_FFN1.names)
    gm = (row("gm_ln_g"), row("gm_ln_b"), gm_w_s, gm_b_st, row("gm_out_norm"))

    h1, n1, a1, b1, s1, g_mix = _ffn_fwd(x, *ffn1, "ffn1_fwd", comm=_gather_comm(GATHER_MIX.pack_local(shards)))
    w_in_t = GATHER_MIX.gathered_piece(g_mix, "w_in").reshape(IN_PROJ, D_MODEL)
    w_proj_t = GATHER_MIX.gathered_piece(g_mix, "ple_w_proj").reshape(D_MODEL, D_PLE)
    conv_w = sum(GATHER_MIX.gathered_piece(g_mix, name).astype(F32) for name in ("conv_w", "conv_w_mid", "conv_w_low"))
    conv_w = conv_w.reshape(CONV_DIM, SSM_CONV).T
    ssd = (row("dt_bias"), row("a_log"), row("d_skip"), row("ssm_norm"))
    w_out = GATHER_MIX.pieces(g_mix, "w_out")

    proj, n2, x16, xc = _mix_in_fwd(h1, row("mix_norm"), w_in_t, conv_w, row("conv_b"))
    ya = _gm_fwd(proj, *gm)
    yb, s_all, g_ffn2 = _ssd_fwd(proj, xc, *ssd, comm=_gather_comm(GATHER_FFN2.pack_local(shards)))
    ffn2 = (row("ffn2_norm"),) + tuple(GATHER_FFN2.pieces(g_ffn2, name) for name in GATHER_FFN2.names)
    h3, n3, a3, b3, s3, h2 = _ffn_fwd(h1, *ffn2, "ffn2_fwd", mixed=(ya, yb, w_out))

    g, gp = {}, {}
    dh3, loss, gp["ple_w_gate"], d_w_proj, g["ple_norm"], g["ple_b_gate"], g["final_norm"] = _tail(
        h3, p, target, row("ple_norm"), GATHER_MIX.pieces(g_mix, "ple_w_gate"), row("ple_b_gate"), w_proj_t,
        row("final_norm"))
    gp["ple_w_proj"] = d_w_proj.T

    dh2, da3, db3, g["ffn2_norm"] = _ffn_dgrad(h2, dh3, a3, b3, *ffn2, "ffn2_dgrad")
    gp["ffn2_w_gate"] = _wgrad(n3, da3, FF_BN, "ffn2_wgrad_gate", transpose_out=True)
    gp["ffn2_w_up"] = _wgrad(n3, db3, FF_BN, "ffn2_wgrad_up", transpose_out=True)
    gp["ffn2_w_down"] = _wgrad(s3, dh3, DOWN_BN, "ffn2_wgrad_down", scale=0.5, bk=DOWN_BK)

    dya, dyb = _out_proj_dgrad(dh2, w_out)
    gp["w_out"] = jnp.concatenate([_wgrad(ya, dh2, SQUARE_BN, "w_out_wgrad_a"), _wgrad(yb, dh2, SQUARE_BN, "w_out_wgrad_b")], axis=0)

    dp_zxd, d_conv_w, g["conv_b"], g["dt_bias"], g["a_log"], g["d_skip"], g["ssm_norm"], parts_late = _ssd_bwd(
        proj, x16, xc, dyb, s_all, conv_w, *ssd, comm=_exchange_comm(SCATTER_LATE.pack_owner_major(gp)))
    gp["conv_w"] = d_conv_w.T
    dp_uv, g["gm_ln_g"], g["gm_ln_b"], g["gm_w_s"], dbst, g["gm_out_norm"] = _gm_bwd(proj, dya, *gm)
    g["gm_b_s"] = jnp.transpose(dbst)

    parts = {}
    gp["w_in"] = jnp.concatenate([_wgrad(n2, dp_uv, SQUARE_BN, "w_in_wgrad_uv", transpose_out=True),
                                  _wgrad(n2, dp_zxd, ZXD_BN, "w_in_wgrad_zxd", transpose_out=True)], axis=0)[:IN_PROJ]
    dh1, g["mix_norm"], parts[SCATTER_IN] = _mix_in_dgrad(h1, dh2, dp_uv, dp_zxd, row("mix_norm"), w_in_t,
                                                          comm=_exchange_comm(SCATTER_IN.pack_owner_major(gp)))

    dx, da1, db1, g["ffn1_norm"] = _ffn_dgrad(x, dh1, a1, b1, *ffn1, "ffn1_dgrad")
    gp["ffn1_w_gate"], small_parts = _wgrad(n1, da1, FF_BN, "ffn1_wgrad_gate", transpose_out=True,
                                            comm=_gather_comm(_pack_small(g, behind=[loss])))
    gp["ffn1_w_up"], parts[SCATTER_GATE] = _wgrad(n1, db1, FF_BN, "ffn1_wgrad_up", transpose_out=True,
                                                  comm=_chip_exchange_comm(SCATTER_GATE.pack_owner_major(gp)))
    gp["ffn1_w_down"], parts[SCATTER_UP] = _wgrad(s1, dh1, DOWN_BN, "ffn1_wgrad_down", scale=0.5, bk=DOWN_BK,
                                                  comm=_chip_exchange_comm(SCATTER_UP.pack_owner_major(gp)))
    parts[SCATTER_DOWN] = _comm_alone([_chip_exchange_comm(SCATTER_DOWN.pack_owner_major(gp))], "scatter_ffn1_down")[0]
    parts[SCATTER_LATE] = parts_late

    res_big = {}
    for pack, pack_parts in parts.items():
        for name in pack.names:
            shape, transposed = SHARDS[name]
            if name in ("ple_w_proj", "conv_w"):
                nat = pack.gathered_piece(pack_parts, name).reshape((N_DEV,) + shape[::-1])
                res_big[name] = _sum_adamw(jnp.transpose(nat, (0, 2, 1)), w[name][0], m[name][0], v[name][0], shape[0],
                                           "adamw_" + name)
            else:
                flip = (lambda a: jnp.transpose(a, (0, 2, 1))) if transposed else (lambda a: a)
                res = _adamw_shard(pack_parts, pack.offsets[name], flip(w[name]), flip(m[name]), flip(v[name]),
                                   "adamw_" + name, n_tiles=4 if name == "w_in" else 2)
                res_big[name] = [flip(r) for r in res]

    small_shapes = {name: w[name].shape for name in SMALL}
    res_small = _sum_adamw(small_parts, _pack_small(w), _pack_small(m), _pack_small(v), SMALL_ROWS, "adamw_small")
    loss = res_small[0].reshape(-1)[sum(w[name].size for name in SMALL)]
    res_small = [_unpack_small(r, small_shapes) for r in res_small]

    outs = []
    for k in range(4):
        for name in WEIGHTS:
            if name in res_small[k]:
                outs.append(res_small[k][name])
            else:
                outs.append(res_big[name][k].reshape(w[name].shape))
    return loss, dx, outs


def kernel(x, p, ffn1_norm, ffn1_w_gate, ffn1_w_up, ffn1_w_down, mix_norm, w_in, gm_ln_g, gm_ln_b, gm_w_s, gm_b_s, gm_out_norm, conv_w, conv_b, dt_bias, a_log, d_skip, ssm_norm, w_out, ffn2_norm, ffn2_w_gate, ffn2_w_up, ffn2_w_down, ple_norm, ple_w_gate, ple_b_gate, ple_w_proj, final_norm, loss_target, m_ffn1_norm, m_ffn1_w_gate, m_ffn1_w_up, m_ffn1_w_down, m_mix_norm, m_w_in, m_gm_ln_g, m_gm_ln_b, m_gm_w_s, m_gm_b_s, m_gm_out_norm, m_conv_w, m_conv_b, m_dt_bias, m_a_log, m_d_skip, m_ssm_norm, m_w_out, m_ffn2_norm, m_ffn2_w_gate, m_ffn2_w_up, m_ffn2_w_down, m_ple_norm, m_ple_w_gate, m_ple_b_gate, m_ple_w_proj, m_final_norm, v_ffn1_norm, v_ffn1_w_gate, v_ffn1_w_up, v_ffn1_w_down, v_mix_norm, v_w_in, v_gm_ln_g, v_gm_ln_b, v_gm_w_s, v_gm_b_s, v_gm_out_norm, v_conv_w, v_conv_b, v_dt_bias, v_a_log, v_d_skip, v_ssm_norm, v_w_out, v_ffn2_norm, v_ffn2_w_gate, v_ffn2_w_up, v_ffn2_w_down, v_ple_norm, v_ple_w_gate, v_ple_b_gate, v_ple_w_proj, v_final_norm):
    args = locals()
    w = {name: args[name] for name in WEIGHTS}
    m = {name: args["m_" + name] for name in WEIGHTS}
    v = {name: args["v_" + name] for name in WEIGHTS}
    loss, dx, outs = _step(x[0], p[0, 0], loss_target[0], w, m, v)
    return (loss, dx[None], *outs)
```

```python
import functools
from typing import NamedTuple

import jax
import jax.numpy as jnp
from jax import lax
from jax.experimental import pallas as pl
from jax.experimental.pallas import tpu as pltpu

F32 = jnp.float32
BF16 = jnp.bfloat16
MESH = pl.DeviceIdType.MESH
N_DEV = 8
N_CHIPS = 4

D_MODEL = 1024
D_FF = 2816
D_PLE = 256
GM_WIDTH = 1024
GM_HEADS = 8
GM_HEAD_DIM = 128
CHUNK = 128
SSM_WIDTH = 1024
SSM_HEADS = 16
SSM_HEAD_DIM = 64
SSM_GROUPS = 2
SSM_STATE = 128
SSM_CONV = 4
CONV_DIM = SSM_WIDTH + 2 * SSM_GROUPS * SSM_STATE
IN_PROJ = 2 * GM_WIDTH + SSM_WIDTH + CONV_DIM + SSM_HEADS
LANES = 128
BF16_ROWS = 16
F32_ROWS = 8
IN_PROJ_PAD = IN_PROJ - SSM_HEADS + LANES
UV_W = 2 * GM_WIDTH
ZXD_W = IN_PROJ_PAD - UV_W
HALO = 8
EPS = 1e-6

ADAM_LR = 0.001
ADAM_B1 = 0.9
ADAM_B2 = 0.999
ADAM_EPS = 1e-08
ADAM_WD = 0.01
ADAM_STEP = 10

VMEM_LIMIT = 56 * 1024 * 1024
PACK_COLS = 1024


def _rms(x, g):
    return x * lax.rsqrt(jnp.mean(x * x, axis=-1, keepdims=True) + EPS) * g


def _gelu(x):
    return 0.5 * x * (1.0 + lax.erf(x * (2.0 ** -0.5)))


def _silu(x):
    return x * jax.nn.sigmoid(x)


def _dot(a, b):
    return jnp.dot(a.astype(BF16), b.astype(BF16), preferred_element_type=F32)


def _dot_nt(a, b):
    return lax.dot_general(a.astype(BF16), b.astype(BF16), (((1,), (1,)), ((), ())), preferred_element_type=F32)


def _dot_tn(a, b):
    return lax.dot_general(a.astype(BF16), b.astype(BF16), (((0,), (0,)), ((), ())), preferred_element_type=F32)


def _split3(x):
    hi = x.astype(BF16)
    rest = x - hi.astype(F32)
    mid = rest.astype(BF16)
    return hi, mid, (rest - mid.astype(F32)).astype(BF16)


def _exact_dot(x, mask, dims, x_first=True, n_terms=3):
    terms = [lax.dot_general(*((t, mask) if x_first else (mask, t)), (dims, ((), ())), preferred_element_type=F32)
             for t in _split3(x)[:n_terms]]
    total = terms[0]
    for term in terms[1:]:
        total = total + term
    return total


def _mask_product(fwd_dims, fwd_x_first, bwd_dims, bwd_x_first, bwd_terms=3):
    @jax.custom_vjp
    def product(x, mask):
        return _exact_dot(x, mask, fwd_dims, fwd_x_first)

    def fwd(x, mask):
        return product(x, mask), mask

    def bwd(mask, g):
        return _exact_dot(g, mask, bwd_dims, bwd_x_first, bwd_terms), jnp.zeros_like(mask)

    product.defvjp(fwd, bwd)
    return product


_widen = _mask_product(((1,), (0,)), True, ((1,), (1,)), True, bwd_terms=2)
_cumsum_rows = _mask_product(((1,), (0,)), False, ((0,), (0,)), False)
_cumsum_cols = _mask_product(((0,), (0,)), True, ((1,), (1,)), False)


class _Pieces(NamedTuple):
    gathered: jax.Array
    row_off: int
    rows: int


class _Comm(NamedTuple):
    phases: object
    src: jax.Array
    dst: jax.ShapeDtypeStruct
    scratch: tuple


class _LazyRef:
    def __init__(self, ref, step, copies):
        self.ref, self.step, self.copies = ref, step, copies

    def __getitem__(self, idx):
        if self.copies:
            copies, self.copies = self.copies, None

            @pl.when(self.step == 0)
            def _():
                for cp in copies:
                    cp.wait()

        return self.ref[idx]


def _tiled(body, name, n_steps, tiled_in, full_in, big_in, tiled_out, acc_out, scratch=(), reverse=False, comm=None,
           lazy=False):
    n_t, n_f, n_b, n_to, n_a = len(tiled_in), len(full_in), len(big_in), len(tiled_out), len(acc_out)
    n_c = 1 if comm else 0

    def row(i):
        return n_steps - 1 - i if reverse else i

    in_specs, args = [], []
    for arr, br, bc, cb in tiled_in:
        if callable(cb):
            in_specs.append(pl.BlockSpec((br, bc), cb))
        else:
            in_specs.append(pl.BlockSpec((br, bc), functools.partial(lambda i, cb: (row(i), cb), cb=cb)))
        args.append(arr)
    for arr in full_in:
        in_specs.append(pl.BlockSpec(arr.shape, functools.partial(lambda i, nd: (0,) * nd, nd=arr.ndim)))
        args.append(arr)
    big_shapes, n_copies = [], 0
    for big in big_in:
        in_specs.append(pl.BlockSpec(memory_space=pl.ANY))
        if isinstance(big, _Pieces):
            args.append(big.gathered)
            big_shapes.append(((N_DEV * big.rows, PACK_COLS), big.gathered.dtype))
            n_copies += N_DEV
        else:
            args.append(big)
            big_shapes.append((big.shape, big.dtype))
            n_copies += 1
    if comm:
        in_specs.append(pl.BlockSpec(memory_space=pl.ANY))
        args.append(comm.src)
    out_specs, out_shape = [], []
    for rows, cols, dt, br in tiled_out:
        out_specs.append(pl.BlockSpec((br, cols), lambda i: (row(i), 0)))
        out_shape.append(jax.ShapeDtypeStruct((rows, cols), dt))
    for shp, dt in acc_out:
        out_specs.append(pl.BlockSpec(shp, functools.partial(lambda i, nd: (0,) * nd, nd=len(shp))))
        out_shape.append(jax.ShapeDtypeStruct(shp, dt))
    if comm:
        out_specs.append(pl.BlockSpec(memory_space=pl.ANY))
        out_shape.append(comm.dst)
    scratch_shapes = [pltpu.VMEM(shp, dt) for shp, dt in big_shapes] + list(scratch)
    if n_copies:
        scratch_shapes.append(pltpu.SemaphoreType.DMA((n_copies,)))
    if comm:
        scratch_shapes += list(comm.scratch)

    def kern(*refs):
        n_in = n_t + n_f + n_b + n_c
        ins = refs[: n_t + n_f]
        big_hbm = refs[n_t + n_f : n_t + n_f + n_b]
        outs = refs[n_in : n_in + n_to + n_a]
        rest = refs[n_in + n_to + n_a + n_c :]
        big_vmem, scr = rest[:n_b], rest[n_b:]
        if comm:
            scr, comm_scr = scr[:-len(comm.scratch)], scr[-len(comm.scratch):]
            comm_start, comm_mid, comm_finish = comm.phases(refs[n_in - 1], refs[n_in + n_to + n_a], *comm_scr)
        if n_copies:
            scr, copy_sems = scr[:-1], scr[-1]
        step = pl.program_id(0)

        groups, k = [], 0
        for big, src, dst in zip(big_in, big_hbm, big_vmem):
            if isinstance(big, _Pieces):
                pairs = [(src.at[j, pl.ds(big.row_off, big.rows), :], dst.at[pl.ds(j * big.rows, big.rows), :])
                         for j in range(N_DEV)]
            else:
                pairs = [(src, dst)]
            groups.append([pltpu.make_async_copy(a, b, copy_sems.at[k + q]) for q, (a, b) in enumerate(pairs)])
            k += len(pairs)

        @pl.when(step == 0)
        def _():
            for grp in groups:
                for cp in grp:
                    cp.start()
            if not lazy:
                for grp in groups:
                    for cp in grp:
                        cp.wait()
            for acc in outs[n_to:]:
                acc[...] = jnp.zeros(acc.shape, acc.dtype)
            if comm:
                comm_start()

        weights = [_LazyRef(ref, step, grp) for ref, grp in zip(big_vmem, groups)] if lazy else big_vmem
        body(row(step), *ins, *weights, *outs, *scr)
        if comm:
            pl.when(step == (n_steps - 1) // 2)(comm_mid)
            pl.when(step == n_steps - 1)(comm_finish)

    res = pl.pallas_call(
        kern,
        out_shape=out_shape,
        grid=(n_steps,),
        in_specs=in_specs,
        out_specs=out_specs,
        scratch_shapes=scratch_shapes,
        name=name,
        compiler_params=pltpu.CompilerParams(dimension_semantics=("arbitrary",), vmem_limit_bytes=VMEM_LIMIT),
    )(*args)
    return res


FWD_CHUNKS = ((0, 1536), (1536, D_FF))
DGRAD_CHUNKS = ((0, 1024), (1024, 2048), (2048, D_FF))
FFN_TM = 256


def _ffn_fwd(h, g, wg_t, wu_t, wd, name, comm=None, mixed=None):
    T = h.shape[0]

    def ffn(x, g_ref, wg_ref, wu_ref, wd_ref, o_ref, n_ref, a_ref, b_ref, s_ref):
        n = _rms(x, g_ref[...]).astype(BF16)
        n_ref[...] = n
        f = jnp.zeros(x.shape, F32)
        for lo, hi in FWD_CHUNKS:
            a = _dot_nt(n, wg_ref[lo:hi, :])
            b = _dot_nt(n, wu_ref[lo:hi, :])
            s = (_silu(a) * b).astype(BF16)
            a_ref[:, lo:hi] = a.astype(BF16)
            b_ref[:, lo:hi] = b.astype(BF16)
            s_ref[:, lo:hi] = s
            f = f + jnp.dot(s, wd_ref[lo:hi, :], preferred_element_type=F32)
        o_ref[...] = x + 0.5 * f

    def body_plain(i, h_ref, *refs):
        ffn(h_ref[...], *refs)

    def body_mixed(i, h_ref, ya_ref, yb_ref, g_ref, wg_ref, wu_ref, wd_ref, wo_ref, o_ref, n_ref, a_ref, b_ref, s_ref, x_ref):
        x = (h_ref[...] + jnp.dot(ya_ref[...], wo_ref[:GM_WIDTH, :], preferred_element_type=F32)
             + jnp.dot(yb_ref[...], wo_ref[GM_WIDTH:, :], preferred_element_type=F32))
        x_ref[...] = x
        ffn(x, g_ref, wg_ref, wu_ref, wd_ref, o_ref, n_ref, a_ref, b_ref, s_ref)

    body = body_mixed if mixed else body_plain
    tiled_in, big_in = [(h, FFN_TM, D_MODEL, 0)], [wg_t, wu_t, wd]
    tiled_out = [(T, D_MODEL, F32, FFN_TM), (T, D_MODEL, BF16, FFN_TM), (T, D_FF, BF16, FFN_TM), (T, D_FF, BF16, FFN_TM),
                 (T, D_FF, BF16, FFN_TM)]
    if mixed:
        tiled_in += [(mixed[0], FFN_TM, GM_WIDTH, 0), (mixed[1], FFN_TM, SSM_WIDTH, 0)]
        big_in.append(mixed[2])
        tiled_out.append((T, D_MODEL, F32, FFN_TM))
    return _tiled(body, name, T // FFN_TM, tiled_in, [g], big_in, tiled_out, [], comm=comm, lazy=True)


def _ffn_dgrad(h, dout, a16, b16, g, wg_t, wu_t, wd, name):
    T = h.shape[0]

    def body(i, h_ref, do_ref, a_ref, b_ref, g_ref, wg_ref, wu_ref, wd_ref, dh_ref, da_ref, db_ref, dg_ref):
        dout = do_ref[...]
        _, rms_vjp = jax.vjp(_rms, h_ref[...], g_ref[...])
        dfo = (0.5 * dout).astype(BF16)
        dn = jnp.zeros(dout.shape, F32)
        for lo, hi in DGRAD_CHUNKS:
            a = a_ref[:, lo:hi].astype(F32)
            b = b_ref[:, lo:hi].astype(F32)
            sg = jax.nn.sigmoid(a)
            ds = _dot_nt(dfo, wd_ref[lo:hi, :])
            db = (ds * (a * sg)).astype(BF16)
            da = (ds * b * (sg * (1.0 + a * (1.0 - sg)))).astype(BF16)
            dn = dn + _dot(da, wg_ref[lo:hi, :]) + _dot(db, wu_ref[lo:hi, :])
            da_ref[:, lo:hi] = da
            db_ref[:, lo:hi] = db
        dx, dg = rms_vjp(dn)
        dh_ref[...] = dout + dx
        dg_ref[...] += dg

    return _tiled(body, name, T // FFN_TM,
                  [(h, FFN_TM, D_MODEL, 0), (dout, FFN_TM, D_MODEL, 0), (a16, FFN_TM, D_FF, 0), (b16, FFN_TM, D_FF, 0)],
                  [g], [wg_t, wu_t, wd],
                  [(T, D_MODEL, F32, FFN_TM), (T, D_FF, BF16, FFN_TM), (T, D_FF, BF16, FFN_TM)], [((1, D_MODEL), F32)],
                  lazy=True)


FF_BN = D_FF // 2
DOWN_BN, DOWN_BK = 512, 1024
SQUARE_BN = 1024
ZXD_BN = ZXD_W // 3


def _wgrad(a, b, bn, name, scale=None, transpose_out=False, bk=2048, comm=None):
    T, M = a.shape
    N = b.shape[1]
    bk = min(bk, T)
    assert M % LANES == 0 and N % bn == 0 and T % bk == 0
    n_j, n_k = N // bn, T // bk
    n_c = 1 if comm else 0

    def kern(*refs):
        a_ref, b_ref, o_ref, acc_ref = refs[0], refs[1], refs[2 + n_c], refs[3 + 2 * n_c]
        j, k = pl.program_id(0), pl.program_id(1)
        if comm:
            comm_start, comm_mid, comm_finish = comm.phases(refs[2], refs[4], *refs[6:])
            pl.when((j == 0) & (k == 0))(comm_start)

        @pl.when(k == 0)
        def _():
            acc_ref[...] = jnp.zeros(acc_ref.shape, F32)

        bv = b_ref[...]
        if scale is not None:
            bv = bv * scale
        acc_ref[...] += _dot_tn(a_ref[...], bv)

        @pl.when(k == n_k - 1)
        def _():
            acc = acc_ref[...]
            o_ref[...] = (acc.T if transpose_out else acc).astype(BF16)

        if comm:
            pl.when((j == (n_j - 1) // 2) & (k == n_k - 1))(comm_mid)
            pl.when((j == n_j - 1) & (k == n_k - 1))(comm_finish)

    if transpose_out:
        out_shape, out_spec = (N, M), pl.BlockSpec((bn, M), lambda j, k: (j, 0))
    else:
        out_shape, out_spec = (M, N), pl.BlockSpec((M, bn), lambda j, k: (0, j))
    any_spec = pl.BlockSpec(memory_space=pl.ANY)
    res = pl.pallas_call(
        kern,
        out_shape=[jax.ShapeDtypeStruct(out_shape, BF16)] + ([comm.dst] if comm else []),
        grid=(n_j, n_k),
        in_specs=[pl.BlockSpec((bk, M), lambda j, k: (k, 0)), pl.BlockSpec((bk, bn), lambda j, k: (k, j))] + [any_spec] * n_c,
        out_specs=[out_spec] + [any_spec] * n_c,
        scratch_shapes=[pltpu.VMEM((M, bn), F32)] + (list(comm.scratch) if comm else []),
        name=name,
        compiler_params=pltpu.CompilerParams(dimension_semantics=("arbitrary", "arbitrary"), vmem_limit_bytes=VMEM_LIMIT),
    )(a, b, *([comm.src] if comm else []))
    return res if comm else res[0]


PROJ_TM = 512
PROJ_DGRAD_TM = 256
UVZ_W = 2 * GM_WIDTH + SSM_WIDTH
PROJ_KEPT = UVZ_W + LANES
Z_BLK = 2 * GM_WIDTH // SSM_WIDTH
DT_BLK = UVZ_W // LANES


def _mix_in_fwd(h, g, w_in_t, conv_w, conv_b):
    T = h.shape[0]

    def body(i, h_ref, g_ref, cw_ref, cb_ref, w_ref, p_ref, n_ref, x_ref, xc_ref, ext_ref):
        @pl.when(i == 0)
        def _():
            ext_ref[0:HALO, :] = jnp.zeros((HALO, CONV_DIM), F32)

        n = _rms(h_ref[...], g_ref[...]).astype(BF16)
        n_ref[...] = n
        proj = _dot_nt(n, w_ref[...])
        p_ref[:, :UVZ_W] = proj[:, :UVZ_W]
        p_ref[:, UVZ_W:] = jnp.concatenate(
            [proj[:, UVZ_W + CONV_DIM:], jnp.zeros((PROJ_TM, LANES - SSM_HEADS), F32)], axis=1)
        xbc = proj[:, UVZ_W:UVZ_W + CONV_DIM]
        x_ref[...] = xbc.astype(BF16)
        ext_ref[HALO:, :] = xbc
        xc_ref[...] = _conv_taps(ext_ref, cw_ref[...], cb_ref[...], PROJ_TM)
        ext_ref[0:HALO, :] = ext_ref[PROJ_TM:PROJ_TM + HALO, :]

    return _tiled(body, "mix_in_fwd", T // PROJ_TM, [(h, PROJ_TM, D_MODEL, 0)], [g, conv_w, conv_b], [w_in_t],
                  [(T, PROJ_KEPT, F32, PROJ_TM), (T, D_MODEL, BF16, PROJ_TM), (T, CONV_DIM, BF16, PROJ_TM),
                   (T, CONV_DIM, F32, PROJ_TM)], [],
                  scratch=[pltpu.VMEM((HALO + PROJ_TM, CONV_DIM), F32)])


def _mix_in_dgrad(h, dh_in, dp_uv, dp_zxd, g, w_in_t, comm=None):
    T = h.shape[0]

    def body(i, h_ref, dh_ref, duv_ref, dzxd_ref, g_ref, w_ref, o_ref, dg_ref):
        dzxd, zx_w = dzxd_ref[...], ZXD_W - LANES
        dn = (_dot(duv_ref[...], w_ref[:UV_W, :]) + _dot(dzxd[:, :zx_w], w_ref[UV_W:UV_W + zx_w, :])
              + _dot(dzxd[:, zx_w:zx_w + SSM_HEADS], w_ref[UV_W + zx_w:, :]))
        _, rms_vjp = jax.vjp(_rms, h_ref[...], g_ref[...])
        dx, dg = rms_vjp(dn)
        o_ref[...] = dh_ref[...] + dx
        dg_ref[...] += dg

    return _tiled(body, "mix_in_dgrad", T // PROJ_DGRAD_TM,
                  [(h, PROJ_DGRAD_TM, D_MODEL, 0), (dh_in, PROJ_DGRAD_TM, D_MODEL, 0), (dp_uv, PROJ_DGRAD_TM, UV_W, 0),
                   (dp_zxd, PROJ_DGRAD_TM, ZXD_W, 0)], [g], [w_in_t],
                  [(T, D_MODEL, F32, PROJ_DGRAD_TM)], [((1, D_MODEL), F32)], comm=comm)


def _out_proj_dgrad(dh, w_out):
    T = dh.shape[0]

    def body(i, dh_ref, w_ref, dya_ref, dyb_ref):
        d = dh_ref[...].astype(BF16)
        dya_ref[...] = _dot_nt(d, w_ref[:GM_WIDTH, :])
        dyb_ref[...] = _dot_nt(d, w_ref[GM_WIDTH:, :])

    rows = min(T, 2 * PROJ_TM)
    return _tiled(body, "out_proj_dgrad", T // rows, [(dh, rows, D_MODEL, 0)], [], [w_out],
                  [(T, GM_WIDTH, F32, rows), (T, SSM_WIDTH, F32, rows)], [])


def _gm_chunk(u, v, ln_g, ln_b, b_st, out_g, *w_heads):
    ug = _gelu(u)
    vg = _gelu(v)
    mu = jnp.mean(vg, axis=-1, keepdims=True)
    xc = vg - mu
    vn = xc * lax.rsqrt(jnp.mean(xc * xc, axis=-1, keepdims=True) + EPS) * ln_g + ln_b
    t_idx = lax.broadcasted_iota(jnp.int32, (CHUNK, CHUNK), 0)
    s_idx = lax.broadcasted_iota(jnp.int32, (CHUNK, CHUNK), 1)
    causal = t_idx >= s_idx
    mixed = []
    for hd in range(GM_HEADS):
        wm = jnp.where(causal, w_heads[hd], 0.0)
        cols = slice(hd * GM_HEAD_DIM, (hd + 1) * GM_HEAD_DIM)
        mixed.append(_dot(wm, vn[:, cols]) + b_st[:, hd:hd + 1])
    ya0 = ug * jnp.concatenate(mixed, axis=1)
    return _rms(ya0, out_g)


GM_FWD_CHUNKS = 4


def _gm_fwd(proj, ln_g, ln_b, w_s, b_st, out_g):
    T = proj.shape[0]

    rows = GM_FWD_CHUNKS * CHUNK

    def body(i, u_ref, v_ref, lg_ref, lb_ref, w_ref, bs_ref, og_ref, ya_ref):
        w_heads = [w_ref[hd] for hd in range(GM_HEADS)]
        for c in range(GM_FWD_CHUNKS):
            tok = pl.ds(c * CHUNK, CHUNK)
            ya = _gm_chunk(u_ref[tok, :], v_ref[tok, :], lg_ref[...], lb_ref[...], bs_ref[...], og_ref[...], *w_heads)
            ya_ref[tok, :] = ya.astype(BF16)

    return _tiled(body, "gmlp_fwd", T // rows, [(proj, rows, GM_WIDTH, 0), (proj, rows, GM_WIDTH, 1)],
                  [ln_g, ln_b, w_s, b_st, out_g], [], [(T, GM_WIDTH, BF16, rows)], [])[0]


def _gm_bwd(proj, dya, ln_g, ln_b, w_s, b_st, out_g):
    T = proj.shape[0]

    def body(i, u_ref, v_ref, dy_ref, lg_ref, lb_ref, w_ref, bs_ref, og_ref, duv_ref, dlg_ref, dlb_ref, dw_ref, dbs_ref,
             dog_ref):
        w_heads = [w_ref[hd] for hd in range(GM_HEADS)]
        _, vjp = jax.vjp(_gm_chunk, u_ref[...], v_ref[...], lg_ref[...], lb_ref[...], bs_ref[...], og_ref[...], *w_heads)
        grads = vjp(dy_ref[...])
        duv_ref[:, :GM_WIDTH] = grads[0].astype(BF16)
        duv_ref[:, GM_WIDTH:] = grads[1].astype(BF16)
        dlg_ref[...] += grads[2]
        dlb_ref[...] += grads[3]
        dbs_ref[...] += grads[4]
        dog_ref[...] += grads[5]
        for hd in range(GM_HEADS):
            dw_ref[hd] += grads[6 + hd]

    return _tiled(body, "gmlp_bwd", T // CHUNK,
                  [(proj, CHUNK, GM_WIDTH, 0), (proj, CHUNK, GM_WIDTH, 1), (dya, CHUNK, GM_WIDTH, 0)],
                  [ln_g, ln_b, w_s, b_st, out_g], [], [(T, UV_W, BF16, CHUNK)],
                  [((1, GM_WIDTH), F32), ((1, GM_WIDTH), F32), ((GM_HEADS, CHUNK, CHUNK), F32),
                   ((CHUNK, GM_HEADS), F32), ((1, GM_WIDTH), F32)])


def _ssd_chunk(xc, z, dtr, s_in, dt_bias, a_log, d_skip, norm_g):
    half = SSM_WIDTH // SSM_GROUPS
    l_idx = lax.broadcasted_iota(jnp.int32, (CHUNK, CHUNK), 0)
    s_idx = lax.broadcasted_iota(jnp.int32, (CHUNK, CHUNK), 1)
    causal = l_idx >= s_idx
    head_of_col = lax.broadcasted_iota(jnp.int32, (SSM_HEADS, SSM_WIDTH), 1) // SSM_HEAD_DIM
    expand = (head_of_col == lax.broadcasted_iota(jnp.int32, (SSM_HEADS, SSM_WIDTH), 0)).astype(BF16)

    xcs = _silu(xc)
    xs = xcs[:, :SSM_WIDTH]
    dt = jax.nn.softplus(dtr + dt_bias)
    adt = dt * (-jnp.exp(a_log))
    acs = _cumsum_rows(adt, causal.astype(BF16))
    acs_t = _cumsum_cols(adt, (l_idx <= s_idx).astype(BF16))
    tot = acs[CHUNK - 1:CHUNK, :]
    dt_w = _widen(dt, expand)
    out_decay_w = _widen(jnp.exp(acs), expand)
    state_decay_w = _widen(jnp.exp(tot - acs), expand)
    chunk_decay_w = _widen(jnp.exp(tot), expand)
    d_skip_w = _widen(d_skip, expand)
    xdt = xs * dt_w
    xdt_decayed = xdt * state_decay_w

    y_diag, y_off, states = [], [], []
    for grp in range(SSM_GROUPS):
        b0 = SSM_WIDTH + grp * SSM_STATE
        c0 = SSM_WIDTH + SSM_GROUPS * SSM_STATE + grp * SSM_STATE
        bm = xcs[:, b0:b0 + SSM_STATE].astype(BF16)
        cm = xcs[:, c0:c0 + SSM_STATE].astype(BF16)
        cb = _dot_nt(cm, bm)
        for k in range(grp * SSM_HEADS // SSM_GROUPS, (grp + 1) * SSM_HEADS // SSM_GROUPS):
            decay = jnp.exp(jnp.where(causal, acs[:, k:k + 1] - acs_t[k:k + 1, :], -jnp.inf))
            y_diag.append(_dot(cb * decay, xdt[:, k * SSM_HEAD_DIM:(k + 1) * SSM_HEAD_DIM]))
        cols = slice(grp * half, (grp + 1) * half)
        states.append(_dot_tn(bm, xdt_decayed[:, cols]))
        y_off.append(_dot(cm, s_in[:, cols]))
    y = jnp.concatenate(y_diag, axis=1) + jnp.concatenate(y_off, axis=1) * out_decay_w + xs * d_skip_w
    s_out = s_in * chunk_decay_w + jnp.concatenate(states, axis=1)
    y = y * _silu(z)
    normed = []
    for grp in range(SSM_GROUPS):
        yg = y[:, grp * half:(grp + 1) * half]
        normed.append(yg * lax.rsqrt(jnp.mean(yg * yg, axis=-1, keepdims=True) + EPS))
    return jnp.concatenate(normed, axis=1) * norm_g, s_out


def _sum_row_tiles(x):
    return x.reshape(x.shape[0] // F32_ROWS, F32_ROWS, x.shape[1]).sum(axis=0)


def _conv_taps(ext_ref, w, b, rows):
    y = b
    for k in range(SSM_CONV):
        y = y + w[k:k + 1, :] * ext_ref[pl.ds(HALO - (SSM_CONV - 1) + k, rows), :]
    return y


SSD_FWD_CHUNKS = 4


def _ssd_fwd(proj, xc, dt_bias, a_log, d_skip, norm_g, comm=None):
    T = proj.shape[0]
    n_chunks = T // CHUNK
    rows = SSD_FWD_CHUNKS * CHUNK

    def body(i, z_ref, xc_ref, dt_ref, dtb_ref, al_ref, dsk_ref, ng_ref, yb_ref, sin_ref, st_ref):
        @pl.when(i == 0)
        def _():
            st_ref[...] = jnp.zeros(st_ref.shape, F32)

        for c in range(SSD_FWD_CHUNKS):
            tok = pl.ds(c * CHUNK, CHUNK)
            s_in = st_ref[...]
            yb, s_out = _ssd_chunk(xc_ref[tok, :], z_ref[tok, :], dt_ref[tok, 0:SSM_HEADS], s_in, dtb_ref[...], al_ref[...],
                                   dsk_ref[...], ng_ref[...])
            yb_ref[tok, :] = yb.astype(BF16)
            sin_ref[pl.ds(c * SSM_STATE, SSM_STATE), :] = s_in
            st_ref[...] = s_out

    return _tiled(body, "ssd_fwd", T // rows,
                  [(proj, rows, SSM_WIDTH, Z_BLK), (xc, rows, CONV_DIM, 0), (proj, rows, LANES, DT_BLK)],
                  [dt_bias, a_log, d_skip, norm_g], [],
                  [(T, SSM_WIDTH, BF16, rows), (n_chunks * SSM_STATE, SSM_WIDTH, F32, SSD_FWD_CHUNKS * SSM_STATE)], [],
                  scratch=[pltpu.VMEM((SSM_STATE, SSM_WIDTH), F32)], comm=comm)


def _ssd_bwd(proj, x16, xc, dyb, s_all, conv_w, dt_bias, a_log, d_skip, norm_g, comm=None):
    T = proj.shape[0]
    n_chunks = T // CHUNK

    def body(i, z_ref, x_ref, xc_ref, dt_ref, dy_ref, sin_ref, cw_ref, dtb_ref, al_ref, dsk_ref, ng_ref,
             dzxd_ref, dcw_ref, dcb_ref, ddtb_ref, dal_ref, ddsk_ref, dng_ref, dext_ref, dst_ref, cw_acc, cb_acc):
        @pl.when(i == n_chunks - 1)
        def _():
            dext_ref[CHUNK:, :] = jnp.zeros((HALO, CONV_DIM), F32)
            dst_ref[...] = jnp.zeros(dst_ref.shape, F32)
            cw_acc[...] = jnp.zeros(cw_acc.shape, F32)
            cb_acc[...] = jnp.zeros(cb_acc.shape, F32)

        _, vjp = jax.vjp(_ssd_chunk, xc_ref[...], z_ref[...], dt_ref[:, 0:SSM_HEADS], sin_ref[...], dtb_ref[...], al_ref[...],
                         dsk_ref[...], ng_ref[...])
        dxc, dz, ddtr, ds_in, ddtb, dal, ddsk, dng = vjp((dy_ref[...], dst_ref[...]))
        dst_ref[...] = ds_in
        ddtb_ref[...] += ddtb
        dal_ref[...] += dal
        ddsk_ref[...] += ddsk
        dng_ref[...] += dng
        dext_ref[0:CHUNK, :] = dxc
        cw = cw_ref[...]
        x = x_ref[...].astype(F32)
        dx = jnp.zeros((CHUNK, CONV_DIM), F32)
        for k in range(SSM_CONV):
            shifted = dext_ref[pl.ds(SSM_CONV - 1 - k, CHUNK), :]
            dx = dx + cw[k:k + 1, :] * shifted
            cw_acc[k] += _sum_row_tiles(shifted * x)
        cb_acc[...] += _sum_row_tiles(dxc)

        @pl.when(i == 0)
        def _():
            dcw_ref[...] = jnp.sum(cw_acc[...], axis=1)
            dcb_ref[...] = jnp.sum(cb_acc[...], axis=0, keepdims=True)

        dext_ref[CHUNK:, :] = dext_ref[0:HALO, :]
        dzxd_ref[:, 0:SSM_WIDTH] = dz.astype(BF16)
        dzxd_ref[:, SSM_WIDTH:SSM_WIDTH + CONV_DIM] = dx.astype(BF16)
        dzxd_ref[:, SSM_WIDTH + CONV_DIM:] = jnp.concatenate(
            [ddtr, jnp.zeros((CHUNK, LANES - SSM_HEADS), F32)], axis=1).astype(BF16)

    return _tiled(body, "ssd_bwd", n_chunks,
                  [(proj, CHUNK, SSM_WIDTH, Z_BLK), (x16, CHUNK, CONV_DIM, 0), (xc, CHUNK, CONV_DIM, 0),
                   (proj, CHUNK, LANES, DT_BLK), (dyb, CHUNK, SSM_WIDTH, 0), (s_all, SSM_STATE, SSM_WIDTH, 0)],
                  [conv_w, dt_bias, a_log, d_skip, norm_g], [],
                  [(T, ZXD_W, BF16, CHUNK)],
                  [((SSM_CONV, CONV_DIM), F32), ((1, CONV_DIM), F32), ((1, SSM_HEADS), F32), ((1, SSM_HEADS), F32),
                   ((1, SSM_HEADS), F32), ((1, SSM_WIDTH), F32)],
                  scratch=[pltpu.VMEM((CHUNK + HALO, CONV_DIM), F32), pltpu.VMEM((SSM_STATE, SSM_WIDTH), F32),
                           pltpu.VMEM((SSM_CONV, F32_ROWS, CONV_DIM), F32), pltpu.VMEM((F32_ROWS, CONV_DIM), F32)],
                  reverse=True, comm=comm)


TAIL_TM = 512


def _tail(h, p, target, ple_norm, w_gate, b_gate, w_proj_t, final_norm):
    T = h.shape[0]

    def head(x, pre, pp, b_g, f_norm, tgt):
        gate = jax.nn.sigmoid(pre + b_g)
        out = _rms(x + gate * pp, f_norm)
        err = out - tgt
        return 0.5 * jnp.sum(jnp.mean(err * err, axis=-1, keepdims=True), axis=0, keepdims=True)

    def body(i, h_ref, p_ref, t_ref, pn_ref, bg_ref, fn_ref, wg_ref, wp_ref, dh_ref, loss_ref, dwg_ref, dwp_ref, dpn_ref,
             dbg_ref, dfn_ref):
        x = h_ref[...]
        n4f, n_vjp = jax.vjp(_rms, x, pn_ref[...])
        n4 = n4f.astype(BF16)
        pre = jnp.dot(n4, wg_ref[...], preferred_element_type=F32)
        p16 = p_ref[...].astype(BF16)
        pp = _dot_nt(p16, wp_ref[...])
        loss, h_vjp = jax.vjp(functools.partial(head, tgt=t_ref[...]), x, pre, pp, bg_ref[...], fn_ref[...])
        dx, dpre, dpp, dbg, dfn = h_vjp(jnp.ones((1, 1), F32))
        dpre16 = dpre.astype(BF16)
        dn4 = _dot_nt(dpre16, wg_ref[...])
        dx2, dpn = n_vjp(dn4)
        dh_ref[...] = dx + dx2
        loss_ref[...] += loss
        dwg_ref[...] += _dot_tn(n4, dpre16)
        dwp_ref[...] += _dot_tn(p16, dpp)
        dpn_ref[...] += dpn
        dbg_ref[...] += dbg
        dfn_ref[...] += dfn

    return _tiled(body, "tail", T // TAIL_TM,
                  [(h, TAIL_TM, D_MODEL, 0), (p, TAIL_TM, D_PLE, 0), (target, TAIL_TM, D_MODEL, 0)],
                  [ple_norm, b_gate, final_norm], [w_gate, w_proj_t],
                  [(T, D_MODEL, F32, TAIL_TM)],
                  [((1, 1), F32), ((D_MODEL, D_MODEL), F32), ((D_PLE, D_MODEL), F32), ((1, D_MODEL), F32),
                   ((1, D_MODEL), F32), ((1, D_MODEL), F32)])


def _gather_phases(x_ref, out_ref, send_sems, recv_sems, local_sem):
    mx, my, mc = lax.axis_index("x"), lax.axis_index("y"), lax.axis_index("c")
    me, sibling = (mx, my, mc), (mx, my, 1 - mc)
    chips = [(1 - mx, my), (mx, 1 - my), (1 - mx, 1 - my)]

    def rows(px, py, pc):
        return out_ref.at[4 * px + 2 * py + pc]

    def copy(k, block, to, src=None):
        return pltpu.make_async_remote_copy(
            src_ref=rows(*block) if src is None else src, dst_ref=rows(*block),
            send_sem=send_sems.at[k], recv_sem=recv_sems.at[k], device_id=to, device_id_type=MESH)

    mine = pltpu.make_async_copy(x_ref, rows(*me), local_sem)
    first = [copy(0, me, sibling, src=x_ref)] + [copy(1 + j, me, (*chip, mc), src=x_ref) for j, chip in enumerate(chips)]
    passed = [copy(4 + j, (*chip, mc), sibling) for j, chip in enumerate(chips)]

    def start():
        mine.start()
        for cp in first:
            cp.start()

    def mid():
        for j, chip in enumerate(chips):
            copy(1 + j, (*chip, mc), me).wait_recv()
            passed[j].start()

    def finish():
        copy(0, sibling, me).wait_recv()
        for j, chip in enumerate(chips):
            copy(4 + j, (*chip, 1 - mc), me).wait_recv()
        for cp in first + passed:
            cp.wait_send()
        mine.wait()

    return start, mid, finish


def _exchange_phases(x_ref, out_ref, send_sems, recv_sems, local_sem):
    mx, my, mc = lax.axis_index("x"), lax.axis_index("y"), lax.axis_index("c")
    me = 4 * mx + 2 * my + mc
    mine = pltpu.make_async_copy(x_ref.at[me], out_ref.at[me], local_sem)
    copies = []
    for k in range(1, N_DEV):
        px = 1 - mx if k & 4 else mx
        py = 1 - my if k & 2 else my
        pc = 1 - mc if k & 1 else mc
        copies.append(pltpu.make_async_remote_copy(
            src_ref=x_ref.at[4 * px + 2 * py + pc], dst_ref=out_ref.at[me], send_sem=send_sems.at[k - 1],
            recv_sem=recv_sems.at[k - 1], device_id=(px, py, pc), device_id_type=MESH))

    def start():
        mine.start()
        for cp in copies:
            cp.start()

    def finish():
        for cp in copies:
            cp.wait_recv()
        for cp in copies:
            cp.wait_send()
        mine.wait()

    return start, lambda: None, finish


def _chip_exchange_phases(x_ref, out_ref, mine, recv, sums, load_sems, pair_send, pair_recv, chip_send, chip_recv, out_sem):
    mx, my, mc = lax.axis_index("x"), lax.axis_index("y"), lax.axis_index("c")
    my_chip = 2 * mx + my
    load = [pltpu.make_async_copy(x_ref.at[2 * q + mc], mine.at[q], load_sems.at[q]) for q in range(N_CHIPS)]
    to_sibling = [pltpu.make_async_remote_copy(
        src_ref=x_ref.at[2 * q + 1 - mc], dst_ref=recv.at[q], send_sem=pair_send.at[q], recv_sem=pair_recv.at[q],
        device_id=(mx, my, 1 - mc), device_id_type=MESH) for q in range(N_CHIPS)]
    to_chips = []
    for k in range(1, N_CHIPS):
        px = 1 - mx if k & 2 else mx
        py = 1 - my if k & 1 else my
        to_chips.append(pltpu.make_async_remote_copy(
            src_ref=sums.at[2 * px + py], dst_ref=out_ref.at[my_chip], send_sem=chip_send.at[k - 1],
            recv_sem=chip_recv.at[k - 1], device_id=(px, py, mc), device_id_type=MESH))
    keep = pltpu.make_async_copy(sums.at[my_chip], out_ref.at[my_chip], out_sem)

    def start():
        for cp in load + to_sibling:
            cp.start()

    def mid():
        for cp in load:
            cp.wait()
        for cp in to_sibling:
            cp.wait_recv()
        for q in range(N_CHIPS):
            sums[q] = (mine[q].astype(F32) + recv[q].astype(F32)).astype(sums.dtype)
        for cp in to_chips + [keep]:
            cp.start()

    def finish():
        for cp in to_chips:
            cp.wait_recv()
        for cp in to_chips + to_sibling:
            cp.wait_send()
        keep.wait()

    return start, mid, finish


FLAT_SCRATCH = (pltpu.SemaphoreType.DMA((N_DEV - 1,)), pltpu.SemaphoreType.DMA((N_DEV - 1,)), pltpu.SemaphoreType.DMA)


def _gather_comm(x):
    return _Comm(_gather_phases, x, jax.ShapeDtypeStruct((N_DEV,) + x.shape, x.dtype), FLAT_SCRATCH)


def _exchange_comm(x):
    return _Comm(_exchange_phases, x, jax.ShapeDtypeStruct(x.shape, x.dtype), FLAT_SCRATCH)


def _chip_exchange_comm(x):
    stage = pltpu.VMEM((N_CHIPS,) + x.shape[1:], x.dtype)
    sems = [pltpu.SemaphoreType.DMA((n,)) for n in (N_CHIPS, N_CHIPS, N_CHIPS, N_CHIPS - 1, N_CHIPS - 1)]
    return _Comm(_chip_exchange_phases, x, jax.ShapeDtypeStruct((N_CHIPS,) + x.shape[1:], x.dtype),
                 (stage, stage, stage, *sems, pltpu.SemaphoreType.DMA))


def _comm_alone(comms, name):
    n = len(comms)

    def body(*refs):
        phases, first = [], 2 * n
        for k, comm in enumerate(comms):
            phases.append(comm.phases(refs[k], refs[n + k], *refs[first:first + len(comm.scratch)]))
            first += len(comm.scratch)
        for step in range(3):
            for phase in phases:
                phase[step]()

    any_spec = pl.BlockSpec(memory_space=pl.ANY)
    return pl.pallas_call(
        body,
        out_shape=[comm.dst for comm in comms],
        in_specs=[any_spec] * n,
        out_specs=[any_spec] * n,
        scratch_shapes=[shape for comm in comms for shape in comm.scratch],
        name=name,
        compiler_params=pltpu.CompilerParams(vmem_limit_bytes=VMEM_LIMIT),
    )(*[comm.src for comm in comms])


def _sum_parts(p_ref):
    g = p_ref[0].astype(F32)
    for j in range(1, p_ref.shape[0]):
        g = g + p_ref[j].astype(F32)
    return g


def _adamw_store(g, w_ref, m_ref, v_ref, g_ref, d_ref, nm_ref, nv_ref):
    m_new = ADAM_B1 * m_ref[...] + (1.0 - ADAM_B1) * g
    v_new = ADAM_B2 * v_ref[...] + (1.0 - ADAM_B2) * jnp.square(g)
    m_hat = m_new / (1.0 - ADAM_B1 ** ADAM_STEP)
    v_hat = v_new / (1.0 - ADAM_B2 ** ADAM_STEP)
    g_ref[...] = g
    d_ref[...] = -ADAM_LR * (m_hat / (jnp.sqrt(v_hat) + ADAM_EPS) + ADAM_WD * w_ref[...])
    nm_ref[...] = m_new
    nv_ref[...] = v_new


def _adamw_shard(parts, off, w, m, v, name, n_tiles):
    _, rows, c = w.shape
    assert c == PACK_COLS
    by_rows = rows % BF16_ROWS == 0
    if by_rows:
        tr = rows // n_tiles
        window = (parts.shape[0], tr, PACK_COLS)
        spec = pl.BlockSpec((None, tr, PACK_COLS), lambda i: (0, i, 0))
    else:
        padded, tc = -(-rows // BF16_ROWS) * BF16_ROWS, PACK_COLS // n_tiles
        window = (parts.shape[0], padded, tc)
        spec = pl.BlockSpec((None, rows, tc), lambda i: (0, 0, i))
    blocked = off % (tr if by_rows else padded) == 0

    def update(p_ref, refs):
        g = _sum_parts(p_ref)
        if not by_rows:
            keep = lax.broadcasted_iota(jnp.int32, (rows, padded), 0) == lax.broadcasted_iota(jnp.int32, (rows, padded), 1)
            g = _exact_dot(g, keep.astype(BF16), ((1,), (0,)), x_first=False)
        _adamw_store(g, *refs)

    def kern_blocked(p_ref, *refs):
        update(p_ref, refs)

    def kern_copied(p_hbm, *refs):
        buf, sem = refs[-2:]
        i = pl.program_id(0)
        if by_rows:
            src = p_hbm.at[:, pl.ds(pl.multiple_of(off + i * tr, BF16_ROWS), tr), :]
        else:
            src = p_hbm.at[:, pl.ds(off, padded), pl.ds(pl.multiple_of(i * tc, LANES), tc)]
        cp = pltpu.make_async_copy(src, buf, sem)
        cp.start()
        cp.wait()
        update(buf, refs[:-2])

    if blocked:
        index = (lambda i: (0, off // tr + i, 0)) if by_rows else (lambda i: (0, off // padded, i))
        parts_spec, scratch = pl.BlockSpec(window, index), []
    else:
        parts_spec, scratch = pl.BlockSpec(memory_space=pl.ANY), [pltpu.VMEM(window, parts.dtype), pltpu.SemaphoreType.DMA]
    return pl.pallas_call(
        kern_blocked if blocked else kern_copied,
        out_shape=[jax.ShapeDtypeStruct(w.shape, F32)] * 4,
        grid=(n_tiles,),
        in_specs=[parts_spec, spec, spec, spec],
        out_specs=[spec] * 4,
        scratch_shapes=scratch,
        name=name,
        compiler_params=pltpu.CompilerParams(dimension_semantics=("arbitrary",), vmem_limit_bytes=VMEM_LIMIT),
    )(parts, w, m, v)


def _sum_adamw(parts, w, m, v, tr, name):
    _, R, C = parts.shape

    def kern(p_ref, w_ref, m_ref, v_ref, g_ref, d_ref, nm_ref, nv_ref):
        _adamw_store(_sum_parts(p_ref), w_ref, m_ref, v_ref, g_ref, d_ref, nm_ref, nv_ref)

    row_spec = pl.BlockSpec((tr, C), lambda i: (i, 0))
    return pl.pallas_call(
        kern,
        out_shape=[jax.ShapeDtypeStruct((R, C), F32)] * 4,
        grid=(R // tr,),
        in_specs=[pl.BlockSpec((N_DEV, tr, C), lambda i: (0, i, 0)), row_spec, row_spec, row_spec],
        out_specs=[row_spec] * 4,
        name=name,
        compiler_params=pltpu.CompilerParams(dimension_semantics=("arbitrary",), vmem_limit_bytes=VMEM_LIMIT),
    )(parts, w, m, v)


FF_SHARD = D_FF // N_DEV
CONV_SHARD = (SSM_CONV, CONV_DIM // N_DEV)
SHARDS = {"ffn1_w_gate": ((D_MODEL, FF_SHARD), True), "ffn1_w_up": ((D_MODEL, FF_SHARD), True),
          "ffn1_w_down": ((FF_SHARD, D_MODEL), False),
          "ffn2_w_gate": ((D_MODEL, FF_SHARD), True), "ffn2_w_up": ((D_MODEL, FF_SHARD), True),
          "ffn2_w_down": ((FF_SHARD, D_MODEL), False),
          "w_out": ((2 * D_MODEL // N_DEV, D_MODEL), False), "ple_w_gate": ((D_MODEL // N_DEV, D_MODEL), False),
          "w_in": ((D_MODEL, IN_PROJ // N_DEV), True), "ple_w_proj": ((D_PLE, D_MODEL // N_DEV), True),
          "conv_w": (CONV_SHARD, True),
          "conv_w_mid": (CONV_SHARD, True), "conv_w_low": (CONV_SHARD, True)}
BIG = tuple(name for name in SHARDS if not name.startswith("conv_w_"))
SMALL = ("ffn1_norm", "mix_norm", "gm_ln_g", "gm_ln_b", "gm_w_s", "gm_b_s", "gm_out_norm", "conv_b", "dt_bias", "a_log",
         "d_skip", "ssm_norm", "ffn2_norm", "ple_norm", "ple_b_gate", "final_norm")
SMALL_ROWS = 144


def _piece_rows(name):
    shape = SHARDS[name][0]
    return -(-(shape[0] * shape[1]) // PACK_COLS)


def _pad_cols(flat, name):
    pad = _piece_rows(name) * PACK_COLS - flat.shape[-1]
    return flat if pad == 0 else jnp.pad(flat, [(0, 0)] * (flat.ndim - 1) + [(0, pad)])


class _Pack:
    def __init__(self, names, tile_rows):
        self.names, self.tile_rows, self.offsets, off = names, tile_rows, {}, 0
        for name in names:
            self.offsets[name] = off
            off += _piece_rows(name)
        self.rows = -(-off // tile_rows) * tile_rows

    def pack_local(self, vals):
        parts = []
        for name in self.names:
            val = vals[name]
            parts.append(_pad_cols((val.T if SHARDS[name][1] else val).reshape(-1), name))
        flat = jnp.concatenate(parts)
        return jnp.pad(flat, (0, self.rows * PACK_COLS - flat.shape[0])).reshape(self.rows, PACK_COLS)

    def pack_owner_major(self, grads):
        parts, rows = [], 0
        for name in self.names:
            grad, piece_rows = grads[name].astype(BF16), _piece_rows(name)
            if grad.shape != (N_DEV * piece_rows, PACK_COLS):
                grad = _pad_cols(grad.reshape(N_DEV, -1), name)
            parts.append(grad.reshape(N_DEV, piece_rows, PACK_COLS))
            rows += piece_rows
        if rows < self.rows:
            parts.append(jnp.zeros((N_DEV, self.rows - rows, PACK_COLS), BF16))
        return parts[0] if len(parts) == 1 else jnp.concatenate(parts, axis=1)

    def gathered_piece(self, gathered, name):
        shape = SHARDS[name][0]
        rows = gathered[:, self.offsets[name]:self.offsets[name] + _piece_rows(name), :]
        return rows.reshape(N_DEV, -1)[:, :shape[0] * shape[1]]

    def pieces(self, gathered, name):
        return _Pieces(gathered, self.offsets[name], _piece_rows(name))


GATHER_FFN1 = _Pack(("ffn1_w_gate", "ffn1_w_up", "ffn1_w_down"), BF16_ROWS)
GATHER_MIX = _Pack(("w_out", "ple_w_gate", "w_in", "ple_w_proj", "conv_w", "conv_w_mid", "conv_w_low"), BF16_ROWS)
GATHER_FFN2 = _Pack(("ffn2_w_gate", "ffn2_w_up", "ffn2_w_down"), BF16_ROWS)
SCATTER_LATE = _Pack(("ffn2_w_gate", "ffn2_w_up", "ffn2_w_down", "w_out", "ple_w_gate", "ple_w_proj"), BF16_ROWS)
SCATTER_IN = _Pack(("w_in", "conv_w"), BF16_ROWS)
SCATTER_GATE = _Pack(("ffn1_w_gate",), BF16_ROWS)
SCATTER_UP = _Pack(("ffn1_w_up",), BF16_ROWS)
SCATTER_DOWN = _Pack(("ffn1_w_down",), BF16_ROWS)


def _pack_small(vals, behind=()):
    flat = jnp.concatenate([vals[name].reshape(-1).astype(F32) for name in SMALL] + [b.reshape(-1) for b in behind])
    return jnp.pad(flat, (0, SMALL_ROWS * PACK_COLS - flat.shape[0])).reshape(SMALL_ROWS, PACK_COLS)


def _unpack_small(packed, shapes):
    out, off = {}, 0
    flat = packed.reshape(-1)
    for name in SMALL:
        n = 1
        for s in shapes[name]:
            n *= s
        out[name] = flat[off:off + n].reshape(shapes[name])
        off += n
    return out


WEIGHTS = ("ffn1_norm", "ffn1_w_gate", "ffn1_w_up", "ffn1_w_down", "mix_norm", "w_in", "gm_ln_g", "gm_ln_b", "gm_w_s",
           "gm_b_s", "gm_out_norm", "conv_w", "conv_b", "dt_bias", "a_log", "d_skip", "ssm_norm", "w_out", "ffn2_norm",
           "ffn2_w_gate", "ffn2_w_up", "ffn2_w_down", "ple_norm", "ple_w_gate", "ple_b_gate", "ple_w_proj", "final_norm")


def _step(x, p, target, w, m, v):
    local = lambda d: {name: d[name][0] for name in BIG}

    shards = {name: val.astype(BF16) for name, val in local(w).items()}
    conv_high = lax.reduce_precision(w["conv_w"][0], 8, 7)
    conv_mid = lax.reduce_precision(w["conv_w"][0] - conv_high, 8, 7)
    shards["conv_w"] = conv_high.astype(BF16)
    shards["conv_w_mid"] = conv_mid.astype(BF16)
    shards["conv_w_low"] = (w["conv_w"][0] - conv_high - conv_mid).astype(BF16)
    g_ffn1 = _comm_alone([_gather_comm(GATHER_FFN1.pack_local(shards))], "gather_ffn1")[0]

    row = lambda name: w[name].reshape(1, -1)
    gm_w_s = w["gm_w_s"][0]
    gm_b_st = jnp.transpose(w["gm_b_s"][0])
    ffn1 = (row("ffn1_norm"),) + tuple(GATHER_FFN1.pieces(g_ffn1, name) for name in GATHER_FFN1.names)
    gm = (row("gm_ln_g"), row("gm_ln_b"), gm_w_s, gm_b_st, row("gm_out_norm"))

    h1, n1, a1, b1, s1, g_mix = _ffn_fwd(x, *ffn1, "ffn1_fwd", comm=_gather_comm(GATHER_MIX.pack_local(shards)))
    w_in_t = GATHER_MIX.gathered_piece(g_mix, "w_in").reshape(IN_PROJ, D_MODEL)
    w_proj_t = GATHER_MIX.gathered_piece(g_mix, "ple_w_proj").reshape(D_MODEL, D_PLE)
    conv_w = sum(GATHER_MIX.gathered_piece(g_mix, name).astype(F32) for name in ("conv_w", "conv_w_mid", "conv_w_low"))
    conv_w = conv_w.reshape(CONV_DIM, SSM_CONV).T
    ssd = (row("dt_bias"), row("a_log"), row("d_skip"), row("ssm_norm"))
    w_out = GATHER_MIX.pieces(g_mix, "w_out")

    proj, n2, x16, xc = _mix_in_fwd(h1, row("mix_norm"), w_in_t, conv_w, row("conv_b"))
    ya = _gm_fwd(proj, *gm)
    yb, s_all, g_ffn2 = _ssd_fwd(proj, xc, *ssd, comm=_gather_comm(GATHER_FFN2.pack_local(shards)))
    ffn2 = (row("ffn2_norm"),) + tuple(GATHER_FFN2.pieces(g_ffn2, name) for name in GATHER_FFN2.names)
    h3, n3, a3, b3, s3, h2 = _ffn_fwd(h1, *ffn2, "ffn2_fwd", mixed=(ya, yb, w_out))

    g, gp = {}, {}
    dh3, loss, gp["ple_w_gate"], d_w_proj, g["ple_norm"], g["ple_b_gate"], g["final_norm"] = _tail(
        h3, p, target, row("ple_norm"), GATHER_MIX.pieces(g_mix, "ple_w_gate"), row("ple_b_gate"), w_proj_t,
        row("final_norm"))
    gp["ple_w_proj"] = d_w_proj.T

    dh2, da3, db3, g["ffn2_norm"] = _ffn_dgrad(h2, dh3, a3, b3, *ffn2, "ffn2_dgrad")
    gp["ffn2_w_gate"] = _wgrad(n3, da3, FF_BN, "ffn2_wgrad_gate", transpose_out=True)
    gp["ffn2_w_up"] = _wgrad(n3, db3, FF_BN, "ffn2_wgrad_up", transpose_out=True)
    gp["ffn2_w_down"] = _wgrad(s3, dh3, DOWN_BN, "ffn2_wgrad_down", scale=0.5, bk=DOWN_BK)

    dya, dyb = _out_proj_dgrad(dh2, w_out)
    gp["w_out"] = jnp.concatenate([_wgrad(ya, dh2, SQUARE_BN, "w_out_wgrad_a"), _wgrad(yb, dh2, SQUARE_BN, "w_out_wgrad_b")], axis=0)

    dp_zxd, d_conv_w, g["conv_b"], g["dt_bias"], g["a_log"], g["d_skip"], g["ssm_norm"], parts_late = _ssd_bwd(
        proj, x16, xc, dyb, s_all, conv_w, *ssd, comm=_exchange_comm(SCATTER_LATE.pack_owner_major(gp)))
    gp["conv_w"] = d_conv_w.T
    dp_uv, g["gm_ln_g"], g["gm_ln_b"], g["gm_w_s"], dbst, g["gm_out_norm"] = _gm_bwd(proj, dya, *gm)
    g["gm_b_s"] = jnp.transpose(dbst)

    parts = {}
    gp["w_in"] = jnp.concatenate([_wgrad(n2, dp_uv, SQUARE_BN, "w_in_wgrad_uv", transpose_out=True),
                                  _wgrad(n2, dp_zxd, ZXD_BN, "w_in_wgrad_zxd", transpose_out=True)], axis=0)[:IN_PROJ]
    dh1, g["mix_norm"], parts[SCATTER_IN] = _mix_in_dgrad(h1, dh2, dp_uv, dp_zxd, row("mix_norm"), w_in_t,
                                                          comm=_exchange_comm(SCATTER_IN.pack_owner_major(gp)))

    dx, da1, db1, g["ffn1_norm"] = _ffn_dgrad(x, dh1, a1, b1, *ffn1, "ffn1_dgrad")
    gp["ffn1_w_gate"], small_parts = _wgrad(n1, da1, FF_BN, "ffn1_wgrad_gate", transpose_out=True,
                                            comm=_gather_comm(_pack_small(g, behind=[loss])))
    gp["ffn1_w_up"], parts[SCATTER_GATE] = _wgrad(n1, db1, FF_BN, "ffn1_wgrad_up", transpose_out=True,
                                                  comm=_chip_exchange_comm(SCATTER_GATE.pack_owner_major(gp)))
    gp["ffn1_w_down"], parts[SCATTER_UP] = _wgrad(s1, dh1, DOWN_BN, "ffn1_wgrad_down", scale=0.5, bk=DOWN_BK,
                                                  comm=_chip_exchange_comm(SCATTER_UP.pack_owner_major(gp)))
    parts[SCATTER_DOWN] = _comm_alone([_chip_exchange_comm(SCATTER_DOWN.pack_owner_major(gp))], "scatter_ffn1_down")[0]
    parts[SCATTER_LATE] = parts_late

    res_big = {}
    for pack, pack_parts in parts.items():
        for name in pack.names:
            shape, transposed = SHARDS[name]
            if name in ("ple_w_proj", "conv_w"):
                nat = pack.gathered_piece(pack_parts, name).reshape((N_DEV,) + shape[::-1])
                res_big[name] = _sum_adamw(jnp.transpose(nat, (0, 2, 1)), w[name][0], m[name][0], v[name][0], shape[0],
                                           "adamw_" + name)
            else:
                flip = (lambda a: jnp.transpose(a, (0, 2, 1))) if transposed else (lambda a: a)
                res = _adamw_shard(pack_parts, pack.offsets[name], flip(w[name]), flip(m[name]), flip(v[name]),
                                   "adamw_" + name, n_tiles=4 if name == "w_in" else 2)
                res_big[name] = [flip(r) for r in res]

    small_shapes = {name: w[name].shape for name in SMALL}
    res_small = _sum_adamw(small_parts, _pack_small(w), _pack_small(m), _pack_small(v), SMALL_ROWS, "adamw_small")
    loss = res_small[0].reshape(-1)[sum(w[name].size for name in SMALL)]
    res_small = [_unpack_small(r, small_shapes) for r in res_small]

    outs = []
    for k in range(4):
        for name in WEIGHTS:
            if name in res_small[k]:
                outs.append(res_small[k][name])
            else:
                outs.append(res_big[name][k].reshape(w[name].shape))
    return loss, dx, outs


def kernel(x, p, ffn1_norm, ffn1_w_gate, ffn1_w_up, ffn1_w_down, mix_norm, w_in, gm_ln_g, gm_ln_b, gm_w_s, gm_b_s, gm_out_norm, conv_w, conv_b, dt_bias, a_log, d_skip, ssm_norm, w_out, ffn2_norm, ffn2_w_gate, ffn2_w_up, ffn2_w_down, ple_norm, ple_w_gate, ple_b_gate, ple_w_proj, final_norm, loss_target, m_ffn1_norm, m_ffn1_w_gate, m_ffn1_w_up, m_ffn1_w_down, m_mix_norm, m_w_in, m_gm_ln_g, m_gm_ln_b, m_gm_w_s, m_gm_b_s, m_gm_out_norm, m_conv_w, m_conv_b, m_dt_bias, m_a_log, m_d_skip, m_ssm_norm, m_w_out, m_ffn2_norm, m_ffn2_w_gate, m_ffn2_w_up, m_ffn2_w_down, m_ple_norm, m_ple_w_gate, m_ple_b_gate, m_ple_w_proj, m_final_norm, v_ffn1_norm, v_ffn1_w_gate, v_ffn1_w_up, v_ffn1_w_down, v_mix_norm, v_w_in, v_gm_ln_g, v_gm_ln_b, v_gm_w_s, v_gm_b_s, v_gm_out_norm, v_conv_w, v_conv_b, v_dt_bias, v_a_log, v_d_skip, v_ssm_norm, v_w_out, v_ffn2_norm, v_ffn2_w_gate, v_ffn2_w_up, v_ffn2_w_down, v_ple_norm, v_ple_w_gate, v_ple_b_gate, v_ple_w_proj, v_final_norm):
    args = locals()
    w = {name: args[name] for name in WEIGHTS}
    m = {name: args["m_" + name] for name in WEIGHTS}
    v = {name: args["v_" + name] for name in WEIGHTS}
    loss, dx, outs = _step(x[0], p[0, 0], loss_target[0], w, m, v)
    return (loss, dx[None], *outs)
```
